```python
import jax, jax.numpy as jnp
from jax import lax
import numpy as np

D_MODEL = 1024
BATCH = 8
SEQ = 4096
DEPTH = 1

MIX_WIDTH = D_MODEL
ATTN_WIDTH = MIX_WIDTH // 2
CONV_WIDTH = MIX_WIDTH - ATTN_WIDTH
HEAD_DIM = 64
N_HEADS = ATTN_WIDTH // HEAD_DIM
DILATED_PATTERNS = ((128, 1), (512, 4), (2048, 16))
BLOCK = 128
ROPE_THETA = 10000.0
CONV_KERNEL = 31
D_FF = 2816
IN_COLS = 3 * ATTN_WIDTH + 2 * CONV_WIDTH
N_MOD = 9
RMS_EPS = 1e-6
LN_EPS = 1e-5

kernel_name = "hybrid_dilated_attn_conformer_conv_macaron_adaln"


def rms_norm(x, g):
    xf = x.astype(jnp.float32)
    y = xf * lax.rsqrt(jnp.mean(xf * xf, axis=-1, keepdims=True) + RMS_EPS)
    return (y * g.astype(jnp.float32)).astype(x.dtype)


def layer_norm(x, g, b):
    xf = x.astype(jnp.float32)
    mu = jnp.mean(xf, axis=-1, keepdims=True)
    var = jnp.mean(jnp.square(xf - mu), axis=-1, keepdims=True)
    y = (xf - mu) * lax.rsqrt(var + LN_EPS)
    return (y * g.astype(jnp.float32) + b.astype(jnp.float32)).astype(x.dtype)


def modulate(x, shift, scale):
    return x * (1.0 + scale) + shift


def swiglu(x, w_gate, w_up, w_down):
    return (jax.nn.silu(x @ w_gate) * (x @ w_up)) @ w_down


def apply_rope(t, cos, sin):
    tf = t.astype(jnp.float32)
    t1, t2 = jnp.split(tf, 2, axis=-1)
    c, s = cos[None, :, None, :], sin[None, :, None, :]
    return jnp.concatenate([t1 * c - t2 * s, t2 * c + t1 * s], axis=-1).astype(t.dtype)


def dilated_branch(q, k, v, window, dilation):
    B, S, H, Dh = q.shape
    L = S // dilation
    steps = window // dilation
    assert steps <= BLOCK
    nb = -(-L // BLOCK)
    Lp = nb * BLOCK

    def to_sub(t):
        t = t.reshape(B, L, dilation, H, Dh).transpose(0, 2, 3, 1, 4)
        t = jnp.pad(t, ((0, 0), (0, 0), (0, 0), (0, Lp - L), (0, 0)))
        return t.reshape(B, dilation, H, nb, BLOCK, Dh)

    def band(tb):
        tp = jnp.pad(tb, ((0, 0), (0, 0), (0, 0), (1, 0), (0, 0), (0, 0)))
        return jnp.concatenate([tp[:, :, :, :-1], tp[:, :, :, 1:]], axis=-2)

    qb = to_sub(q)
    kb = band(to_sub(k))
    vb = band(to_sub(v))
    s = jnp.einsum('brhnqd,brhnkd->brhnqk', qb, kb).astype(jnp.float32) * (Dh ** -0.5)
    qi = jnp.arange(BLOCK)[:, None]
    kj = jnp.arange(2 * BLOCK)[None, :]
    blk = jnp.arange(nb)[:, None, None]
    dist = qi - kj + BLOCK
    key_pos = blk * BLOCK + kj - BLOCK
    valid = (dist >= 0) & (dist <= steps) & (key_pos >= 0)
    s = jnp.where(valid, s, -jnp.inf)
    m = jnp.max(s, axis=-1, keepdims=True)
    p = jnp.exp(s - m)
    den = jnp.sum(p, axis=-1, keepdims=True)
    o = jnp.einsum('brhnqk,brhnkd->brhnqd', p.astype(vb.dtype), vb).astype(jnp.float32) / den
    lse = (m + jnp.log(den))[..., 0]
    o = o.reshape(B, dilation, H, Lp, Dh)[:, :, :, :L].transpose(0, 3, 1, 2, 4).reshape(B, S, H, Dh)
    lse = lse.reshape(B, dilation, H, Lp)[:, :, :, :L].transpose(0, 3, 1, 2).reshape(B, S, H)
    return o, lse


def hybrid_mixer(n, w_in, conv_dw_w, conv_dw_b, conv_ln_g, conv_ln_b,
                 attn_out_g, conv_out_g, w_out):
    B, S, _ = n.shape
    proj = n @ w_in
    q, k, v, glu_a, glu_b = jnp.split(
        proj, [ATTN_WIDTH, 2 * ATTN_WIDTH, 3 * ATTN_WIDTH, 3 * ATTN_WIDTH + CONV_WIDTH], axis=-1)

    pos = jnp.arange(S, dtype=jnp.float32)
    inv_freq = ROPE_THETA ** (-jnp.arange(0, HEAD_DIM, 2, dtype=jnp.float32) / HEAD_DIM)
    ang = pos[:, None] * inv_freq[None, :]
    cos, sin = jnp.cos(ang), jnp.sin(ang)
    q = apply_rope(q.reshape(B, S, N_HEADS, HEAD_DIM), cos, sin)
    k = apply_rope(k.reshape(B, S, N_HEADS, HEAD_DIM), cos, sin)
    v = v.reshape(B, S, N_HEADS, HEAD_DIM)
    outs, lses = [], []
    for window, dilation in DILATED_PATTERNS:
        o, lse = dilated_branch(q, k, v, window, dilation)
        outs.append(o)
        lses.append(lse)
    w_mix = jax.nn.softmax(jnp.stack(lses, axis=0), axis=0)
    attn = jnp.einsum('gbsh,gbshd->bshd', w_mix, jnp.stack(outs, axis=0))
    attn = attn.reshape(B, S, ATTN_WIDTH).astype(n.dtype)

    u = glu_a * jax.nn.sigmoid(glu_b)
    u = lax.conv_general_dilated(
        u, conv_dw_w, window_strides=(1,), padding=[(CONV_KERNEL - 1, 0)],
        dimension_numbers=('NWC', 'WIO', 'NWC'), feature_group_count=CONV_WIDTH) + conv_dw_b
    u = jax.nn.silu(layer_norm(u, conv_ln_g, conv_ln_b))

    y = jnp.concatenate([rms_norm(attn, attn_out_g), rms_norm(u, conv_out_g)], axis=-1)
    return y @ w_out


def _fwd_setup_inputs(seed: int = 0) -> dict:
    key = jax.random.key(seed)
    ks = jax.random.split(key, 24)
    f32 = jnp.float32

    def nrm(k, shape, fan_in, gain=1.0):
        return jax.random.normal(k, shape, f32) * (gain * fan_in ** -0.5)

    def gain(k, shape):
        return 1.0 + 0.02 * jax.random.normal(k, shape, f32)

    def bias(k, shape):
        return 0.02 * jax.random.normal(k, shape, f32)

    L = DEPTH
    return {
        "x": jax.random.normal(ks[0], (BATCH, SEQ, D_MODEL), f32),
        "c": jax.random.normal(ks[1], (BATCH, D_MODEL), f32),
        "w_ada": nrm(ks[2], (L, D_MODEL, N_MOD * D_MODEL), D_MODEL, 0.5),
        "b_ada": bias(ks[3], (L, N_MOD * D_MODEL)),
        "ffn1_norm_g": gain(ks[4], (L, D_MODEL)),
        "ffn1_w_gate": nrm(ks[5], (L, D_MODEL, D_FF), D_MODEL),
        "ffn1_w_up": nrm(ks[6], (L, D_MODEL, D_FF), D_MODEL),
        "ffn1_w_down": nrm(ks[7], (L, D_FF, D_MODEL), D_FF),
        "mix_norm_g": gain(ks[8], (L, D_MODEL)),
        "w_in": nrm(ks[9], (L, D_MODEL, IN_COLS), D_MODEL),
        "conv_dw_w": nrm(ks[10], (L, CONV_KERNEL, 1, CONV_WIDTH), CONV_KERNEL),
        "conv_dw_b": bias(ks[11], (L, CONV_WIDTH)),
        "conv_ln_g": gain(ks[12], (L, CONV_WIDTH)),
        "conv_ln_b": bias(ks[13], (L, CONV_WIDTH)),
        "attn_out_g": gain(ks[14], (L, ATTN_WIDTH)),
        "conv_out_g": gain(ks[15], (L, CONV_WIDTH)),
        "w_out": nrm(ks[16], (L, MIX_WIDTH, D_MODEL), MIX_WIDTH),
        "ffn2_norm_g": gain(ks[17], (L, D_MODEL)),
        "ffn2_w_gate": nrm(ks[18], (L, D_MODEL, D_FF), D_MODEL),
        "ffn2_w_up": nrm(ks[19], (L, D_MODEL, D_FF), D_MODEL),
        "ffn2_w_down": nrm(ks[20], (L, D_FF, D_MODEL), D_FF),
        "final_norm_g": gain(ks[21], (D_MODEL,)),
    }


def _fwd_reference(x, c, w_ada, b_ada, ffn1_norm_g, ffn1_w_gate, ffn1_w_up, ffn1_w_down,
              mix_norm_g, w_in, conv_dw_w, conv_dw_b, conv_ln_g, conv_ln_b,
              attn_out_g, conv_out_g, w_out, ffn2_norm_g, ffn2_w_gate, ffn2_w_up,
              ffn2_w_down, final_norm_g):
    h = x
    for l in range(DEPTH):
        mod = jax.nn.silu(c) @ w_ada[l] + b_ada[l]
        sh1, sc1, g1, sh2, sc2, g2, sh3, sc3, g3 = jnp.split(mod[:, None, :], N_MOD, axis=-1)
        n1 = modulate(rms_norm(h, ffn1_norm_g[l]), sh1, sc1)
        h = h + 0.5 * g1 * swiglu(n1, ffn1_w_gate[l], ffn1_w_up[l], ffn1_w_down[l])
        n2 = modulate(rms_norm(h, mix_norm_g[l]), sh2, sc2)
        h = h + g2 * hybrid_mixer(n2, w_in[l], conv_dw_w[l], conv_dw_b[l], conv_ln_g[l],
                                  conv_ln_b[l], attn_out_g[l], conv_out_g[l], w_out[l])
        n3 = modulate(rms_norm(h, ffn2_norm_g[l]), sh3, sc3)
        h = h + 0.5 * g3 * swiglu(n3, ffn2_w_gate[l], ffn2_w_up[l], ffn2_w_down[l])
    return rms_norm(h, final_norm_g)


import jax as _jax
import jax.numpy as _jnp

TWIN_FORMAT = 'train_step'
FWD_PARAMS = ['x', 'c', 'w_ada', 'b_ada', 'ffn1_norm_g', 'ffn1_w_gate', 'ffn1_w_up', 'ffn1_w_down', 'mix_norm_g', 'w_in', 'conv_dw_w', 'conv_dw_b', 'conv_ln_g', 'conv_ln_b', 'attn_out_g', 'conv_out_g', 'w_out', 'ffn2_norm_g', 'ffn2_w_gate', 'ffn2_w_up', 'ffn2_w_down', 'final_norm_g']
TWIN_WEIGHTS = ['w_ada', 'b_ada', 'ffn1_norm_g', 'ffn1_w_gate', 'ffn1_w_up', 'ffn1_w_down', 'mix_norm_g', 'w_in', 'conv_dw_w', 'conv_dw_b', 'conv_ln_g', 'conv_ln_b', 'attn_out_g', 'conv_out_g', 'w_out', 'ffn2_norm_g', 'ffn2_w_gate', 'ffn2_w_up', 'ffn2_w_down', 'final_norm_g']
TWIN_DIFF_INPUT = 'x'
TWIN_INPUTS = ['x', 'c', 'w_ada', 'b_ada', 'ffn1_norm_g', 'ffn1_w_gate', 'ffn1_w_up', 'ffn1_w_down', 'mix_norm_g', 'w_in', 'conv_dw_w', 'conv_dw_b', 'conv_ln_g', 'conv_ln_b', 'attn_out_g', 'conv_out_g', 'w_out', 'ffn2_norm_g', 'ffn2_w_gate', 'ffn2_w_up', 'ffn2_w_down', 'final_norm_g', 'loss_target', 'm_w_ada', 'm_b_ada', 'm_ffn1_norm_g', 'm_ffn1_w_gate', 'm_ffn1_w_up', 'm_ffn1_w_down', 'm_mix_norm_g', 'm_w_in', 'm_conv_dw_w', 'm_conv_dw_b', 'm_conv_ln_g', 'm_conv_ln_b', 'm_attn_out_g', 'm_conv_out_g', 'm_w_out', 'm_ffn2_norm_g', 'm_ffn2_w_gate', 'm_ffn2_w_up', 'm_ffn2_w_down', 'm_final_norm_g', 'v_w_ada', 'v_b_ada', 'v_ffn1_norm_g', 'v_ffn1_w_gate', 'v_ffn1_w_up', 'v_ffn1_w_down', 'v_mix_norm_g', 'v_w_in', 'v_conv_dw_w', 'v_conv_dw_b', 'v_conv_ln_g', 'v_conv_ln_b', 'v_attn_out_g', 'v_conv_out_g', 'v_w_out', 'v_ffn2_norm_g', 'v_ffn2_w_gate', 'v_ffn2_w_up', 'v_ffn2_w_down', 'v_final_norm_g']
TWIN_OUTPUTS = ['loss', 'grad_x', 'grad_w_ada', 'grad_b_ada', 'grad_ffn1_norm_g', 'grad_ffn1_w_gate', 'grad_ffn1_w_up', 'grad_ffn1_w_down', 'grad_mix_norm_g', 'grad_w_in', 'grad_conv_dw_w', 'grad_conv_dw_b', 'grad_conv_ln_g', 'grad_conv_ln_b', 'grad_attn_out_g', 'grad_conv_out_g', 'grad_w_out', 'grad_ffn2_norm_g', 'grad_ffn2_w_gate', 'grad_ffn2_w_up', 'grad_ffn2_w_down', 'grad_final_norm_g', 'delta_w_ada', 'delta_b_ada', 'delta_ffn1_norm_g', 'delta_ffn1_w_gate', 'delta_ffn1_w_up', 'delta_ffn1_w_down', 'delta_mix_norm_g', 'delta_w_in', 'delta_conv_dw_w', 'delta_conv_dw_b', 'delta_conv_ln_g', 'delta_conv_ln_b', 'delta_attn_out_g', 'delta_conv_out_g', 'delta_w_out', 'delta_ffn2_norm_g', 'delta_ffn2_w_gate', 'delta_ffn2_w_up', 'delta_ffn2_w_down', 'delta_final_norm_g', 'new_m_w_ada', 'new_m_b_ada', 'new_m_ffn1_norm_g', 'new_m_ffn1_w_gate', 'new_m_ffn1_w_up', 'new_m_ffn1_w_down', 'new_m_mix_norm_g', 'new_m_w_in', 'new_m_conv_dw_w', 'new_m_conv_dw_b', 'new_m_conv_ln_g', 'new_m_conv_ln_b', 'new_m_attn_out_g', 'new_m_conv_out_g', 'new_m_w_out', 'new_m_ffn2_norm_g', 'new_m_ffn2_w_gate', 'new_m_ffn2_w_up', 'new_m_ffn2_w_down', 'new_m_final_norm_g', 'new_v_w_ada', 'new_v_b_ada', 'new_v_ffn1_norm_g', 'new_v_ffn1_w_gate', 'new_v_ffn1_w_up', 'new_v_ffn1_w_down', 'new_v_mix_norm_g', 'new_v_w_in', 'new_v_conv_dw_w', 'new_v_conv_dw_b', 'new_v_conv_ln_g', 'new_v_conv_ln_b', 'new_v_attn_out_g', 'new_v_conv_out_g', 'new_v_w_out', 'new_v_ffn2_norm_g', 'new_v_ffn2_w_gate', 'new_v_ffn2_w_up', 'new_v_ffn2_w_down', 'new_v_final_norm_g']
TWIN_LEAF_KINDS = {'loss': 'loss', 'grad_x': 'grad_x', 'grad_w_ada': 'grad_w', 'grad_b_ada': 'grad_w', 'grad_ffn1_norm_g': 'grad_w', 'grad_ffn1_w_gate': 'grad_w', 'grad_ffn1_w_up': 'grad_w', 'grad_ffn1_w_down': 'grad_w', 'grad_mix_norm_g': 'grad_w', 'grad_w_in': 'grad_w', 'grad_conv_dw_w': 'grad_w', 'grad_conv_dw_b': 'grad_w', 'grad_conv_ln_g': 'grad_w', 'grad_conv_ln_b': 'grad_w', 'grad_attn_out_g': 'grad_w', 'grad_conv_out_g': 'grad_w', 'grad_w_out': 'grad_w', 'grad_ffn2_norm_g': 'grad_w', 'grad_ffn2_w_gate': 'grad_w', 'grad_ffn2_w_up': 'grad_w', 'grad_ffn2_w_down': 'grad_w', 'grad_final_norm_g': 'grad_w', 'delta_w_ada': 'delta_w', 'delta_b_ada': 'delta_w', 'delta_ffn1_norm_g': 'delta_w', 'delta_ffn1_w_gate': 'delta_w', 'delta_ffn1_w_up': 'delta_w', 'delta_ffn1_w_down': 'delta_w', 'delta_mix_norm_g': 'delta_w', 'delta_w_in': 'delta_w', 'delta_conv_dw_w': 'delta_w', 'delta_conv_dw_b': 'delta_w', 'delta_conv_ln_g': 'delta_w', 'delta_conv_ln_b': 'delta_w', 'delta_attn_out_g': 'delta_w', 'delta_conv_out_g': 'delta_w', 'delta_w_out': 'delta_w', 'delta_ffn2_norm_g': 'delta_w', 'delta_ffn2_w_gate': 'delta_w', 'delta_ffn2_w_up': 'delta_w', 'delta_ffn2_w_down': 'delta_w', 'delta_final_norm_g': 'delta_w', 'new_m_w_ada': 'new_m', 'new_m_b_ada': 'new_m', 'new_m_ffn1_norm_g': 'new_m', 'new_m_ffn1_w_gate': 'new_m', 'new_m_ffn1_w_up': 'new_m', 'new_m_ffn1_w_down': 'new_m', 'new_m_mix_norm_g': 'new_m', 'new_m_w_in': 'new_m', 'new_m_conv_dw_w': 'new_m', 'new_m_conv_dw_b': 'new_m', 'new_m_conv_ln_g': 'new_m', 'new_m_conv_ln_b': 'new_m', 'new_m_attn_out_g': 'new_m', 'new_m_conv_out_g': 'new_m', 'new_m_w_out': 'new_m', 'new_m_ffn2_norm_g': 'new_m', 'new_m_ffn2_w_gate': 'new_m', 'new_m_ffn2_w_up': 'new_m', 'new_m_ffn2_w_down': 'new_m', 'new_m_final_norm_g': 'new_m', 'new_v_w_ada': 'new_v', 'new_v_b_ada': 'new_v', 'new_v_ffn1_norm_g': 'new_v', 'new_v_ffn1_w_gate': 'new_v', 'new_v_ffn1_w_up': 'new_v', 'new_v_ffn1_w_down': 'new_v', 'new_v_mix_norm_g': 'new_v', 'new_v_w_in': 'new_v', 'new_v_conv_dw_w': 'new_v', 'new_v_conv_dw_b': 'new_v', 'new_v_conv_ln_g': 'new_v', 'new_v_conv_ln_b': 'new_v', 'new_v_attn_out_g': 'new_v', 'new_v_conv_out_g': 'new_v', 'new_v_w_out': 'new_v', 'new_v_ffn2_norm_g': 'new_v', 'new_v_ffn2_w_gate': 'new_v', 'new_v_ffn2_w_up': 'new_v', 'new_v_ffn2_w_down': 'new_v', 'new_v_final_norm_g': 'new_v'}


def _forward(args):
    return _fwd_reference(*[args[k] for k in FWD_PARAMS])


def _output_shape():
    def fwd():
        inp = _fwd_setup_inputs(0)
        return _fwd_reference(*[inp[k] for k in FWD_PARAMS])
    out = _jax.eval_shape(fwd)
    return out.shape, out.dtype

N_MICROBATCH = 1
ADAM_LR = 0.001
ADAM_B1 = 0.9
ADAM_B2 = 0.999
ADAM_EPS = 1e-08
ADAM_WD = 0.01
ADAM_STEP = 10
PER_EXAMPLE_BATCH_AXIS = {'x': 0, 'c': 0, 'loss_target': 0}
SHARED_INPUTS = []
_WEIGHT_DTYPES = {'w_ada': _jnp.float32, 'b_ada': _jnp.float32, 'ffn1_norm_g': _jnp.float32, 'ffn1_w_gate': _jnp.float32, 'ffn1_w_up': _jnp.float32, 'ffn1_w_down': _jnp.float32, 'mix_norm_g': _jnp.float32, 'w_in': _jnp.float32, 'conv_dw_w': _jnp.float32, 'conv_dw_b': _jnp.float32, 'conv_ln_g': _jnp.float32, 'conv_ln_b': _jnp.float32, 'attn_out_g': _jnp.float32, 'conv_out_g': _jnp.float32, 'w_out': _jnp.float32, 'ffn2_norm_g': _jnp.float32, 'ffn2_w_gate': _jnp.float32, 'ffn2_w_up': _jnp.float32, 'ffn2_w_down': _jnp.float32, 'final_norm_g': _jnp.float32}
MOMENT_SCALE = {'w_ada': 5.351281e-02, 'b_ada': 9.145706e-02, 'ffn1_norm_g': 2.626691e-02, 'ffn1_w_gate': 1.169915e-02, 'ffn1_w_up': 1.133257e-02, 'ffn1_w_down': 1.877946e-02, 'mix_norm_g': 5.363096e-02, 'w_in': 3.935964e-02, 'conv_dw_w': 5.336015e-02, 'conv_dw_b': 9.991275e-02, 'conv_ln_g': 6.628828e-02, 'conv_ln_b': 6.494188e-02, 'attn_out_g': 5.606871e-02, 'conv_out_g': 5.490142e-02, 'w_out': 5.686715e-02, 'ffn2_norm_g': 2.489711e-02, 'ffn2_w_gate': 1.128028e-02, 'ffn2_w_up': 1.092928e-02, 'ffn2_w_down': 1.812957e-02, 'final_norm_g': 3.208121e+01}


def _to_microbatches(a, axis):
    t = _jnp.moveaxis(a, axis, 0)
    t = t.reshape((N_MICROBATCH, t.shape[0] // N_MICROBATCH) + t.shape[1:])
    return _jnp.moveaxis(t, 1, axis + 1)


def setup_inputs(seed: int = 0) -> dict:
    inp = _fwd_setup_inputs(seed)
    key = _jax.random.fold_in(_jax.random.key(seed), 7919)
    shape, _ = _output_shape()
    out = dict(inp)
    out["loss_target"] = _jax.random.normal(_jax.random.fold_in(key, 0), shape, _jnp.float32)
    for i, name in enumerate(TWIN_WEIGHTS):
        w = inp[name].astype(_jnp.float32)
        if MOMENT_SCALE is None:
            s = _jnp.sqrt(_jnp.mean(_jnp.square(w)) + 1e-30)
        else:
            s = MOMENT_SCALE[name]
        km, kv = _jax.random.split(_jax.random.fold_in(key, i + 1))
        out[name] = w
        out["m_" + name] = s * _jax.random.normal(km, w.shape, _jnp.float32)
        out["v_" + name] = (s * s) * _jax.random.uniform(kv, w.shape, _jnp.float32, 0.5, 1.5)
    if N_MICROBATCH > 1:
        for name, axis in PER_EXAMPLE_BATCH_AXIS.items():
            out[name] = _to_microbatches(out[name], axis)
    return {'x': out['x'], 'c': out['c'], 'w_ada': out['w_ada'], 'b_ada': out['b_ada'], 'ffn1_norm_g': out['ffn1_norm_g'], 'ffn1_w_gate': out['ffn1_w_gate'], 'ffn1_w_up': out['ffn1_w_up'], 'ffn1_w_down': out['ffn1_w_down'], 'mix_norm_g': out['mix_norm_g'], 'w_in': out['w_in'], 'conv_dw_w': out['conv_dw_w'], 'conv_dw_b': out['conv_dw_b'], 'conv_ln_g': out['conv_ln_g'], 'conv_ln_b': out['conv_ln_b'], 'attn_out_g': out['attn_out_g'], 'conv_out_g': out['conv_out_g'], 'w_out': out['w_out'], 'ffn2_norm_g': out['ffn2_norm_g'], 'ffn2_w_gate': out['ffn2_w_gate'], 'ffn2_w_up': out['ffn2_w_up'], 'ffn2_w_down': out['ffn2_w_down'], 'final_norm_g': out['final_norm_g'], 'loss_target': out['loss_target'], 'm_w_ada': out['m_w_ada'], 'm_b_ada': out['m_b_ada'], 'm_ffn1_norm_g': out['m_ffn1_norm_g'], 'm_ffn1_w_gate': out['m_ffn1_w_gate'], 'm_ffn1_w_up': out['m_ffn1_w_up'], 'm_ffn1_w_down': out['m_ffn1_w_down'], 'm_mix_norm_g': out['m_mix_norm_g'], 'm_w_in': out['m_w_in'], 'm_conv_dw_w': out['m_conv_dw_w'], 'm_conv_dw_b': out['m_conv_dw_b'], 'm_conv_ln_g': out['m_conv_ln_g'], 'm_conv_ln_b': out['m_conv_ln_b'], 'm_attn_out_g': out['m_attn_out_g'], 'm_conv_out_g': out['m_conv_out_g'], 'm_w_out': out['m_w_out'], 'm_ffn2_norm_g': out['m_ffn2_norm_g'], 'm_ffn2_w_gate': out['m_ffn2_w_gate'], 'm_ffn2_w_up': out['m_ffn2_w_up'], 'm_ffn2_w_down': out['m_ffn2_w_down'], 'm_final_norm_g': out['m_final_norm_g'], 'v_w_ada': out['v_w_ada'], 'v_b_ada': out['v_b_ada'], 'v_ffn1_norm_g': out['v_ffn1_norm_g'], 'v_ffn1_w_gate': out['v_ffn1_w_gate'], 'v_ffn1_w_up': out['v_ffn1_w_up'], 'v_ffn1_w_down': out['v_ffn1_w_down'], 'v_mix_norm_g': out['v_mix_norm_g'], 'v_w_in': out['v_w_in'], 'v_conv_dw_w': out['v_conv_dw_w'], 'v_conv_dw_b': out['v_conv_dw_b'], 'v_conv_ln_g': out['v_conv_ln_g'], 'v_conv_ln_b': out['v_conv_ln_b'], 'v_attn_out_g': out['v_attn_out_g'], 'v_conv_out_g': out['v_conv_out_g'], 'v_w_out': out['v_w_out'], 'v_ffn2_norm_g': out['v_ffn2_norm_g'], 'v_ffn2_w_gate': out['v_ffn2_w_gate'], 'v_ffn2_w_up': out['v_ffn2_w_up'], 'v_ffn2_w_down': out['v_ffn2_w_down'], 'v_final_norm_g': out['v_final_norm_g']}


def _loss(weights, diff, rest, loss_target):
    with _jax.named_scope("forward"):
        args = {**rest, TWIN_DIFF_INPUT: diff, **{k: w.astype(_WEIGHT_DTYPES[k]) for k, w in weights.items()}}
        y = _forward(args)
    with _jax.named_scope("loss_head"):
        err = _jnp.square(y.astype(_jnp.float32) - loss_target)
        return 0.5 * _jnp.sum(_jnp.mean(err, axis=-1)) if err.ndim else 0.5 * err


def _adamw(w, g, m, v):
    m = ADAM_B1 * m + (1.0 - ADAM_B1) * g
    v = ADAM_B2 * v + (1.0 - ADAM_B2) * _jnp.square(g)
    m_hat = m / (1.0 - ADAM_B1 ** ADAM_STEP)
    v_hat = v / (1.0 - ADAM_B2 ** ADAM_STEP)
    delta = -ADAM_LR * (m_hat / (_jnp.sqrt(v_hat) + ADAM_EPS) + ADAM_WD * w)
    return delta, m, v


def reference(x, c, w_ada, b_ada, ffn1_norm_g, ffn1_w_gate, ffn1_w_up, ffn1_w_down, mix_norm_g, w_in, conv_dw_w, conv_dw_b, conv_ln_g, conv_ln_b, attn_out_g, conv_out_g, w_out, ffn2_norm_g, ffn2_w_gate, ffn2_w_up, ffn2_w_down, final_norm_g, loss_target, m_w_ada, m_b_ada, m_ffn1_norm_g, m_ffn1_w_gate, m_ffn1_w_up, m_ffn1_w_down, m_mix_norm_g, m_w_in, m_conv_dw_w, m_conv_dw_b, m_conv_ln_g, m_conv_ln_b, m_attn_out_g, m_conv_out_g, m_w_out, m_ffn2_norm_g, m_ffn2_w_gate, m_ffn2_w_up, m_ffn2_w_down, m_final_norm_g, v_w_ada, v_b_ada, v_ffn1_norm_g, v_ffn1_w_gate, v_ffn1_w_up, v_ffn1_w_down, v_mix_norm_g, v_w_in, v_conv_dw_w, v_conv_dw_b, v_conv_ln_g, v_conv_ln_b, v_attn_out_g, v_conv_out_g, v_w_out, v_ffn2_norm_g, v_ffn2_w_gate, v_ffn2_w_up, v_ffn2_w_down, v_final_norm_g):
    given = dict(x=x, c=c, w_ada=w_ada, b_ada=b_ada, ffn1_norm_g=ffn1_norm_g, ffn1_w_gate=ffn1_w_gate, ffn1_w_up=ffn1_w_up, ffn1_w_down=ffn1_w_down, mix_norm_g=mix_norm_g, w_in=w_in, conv_dw_w=conv_dw_w, conv_dw_b=conv_dw_b, conv_ln_g=conv_ln_g, conv_ln_b=conv_ln_b, attn_out_g=attn_out_g, conv_out_g=conv_out_g, w_out=w_out, ffn2_norm_g=ffn2_norm_g, ffn2_w_gate=ffn2_w_gate, ffn2_w_up=ffn2_w_up, ffn2_w_down=ffn2_w_down, final_norm_g=final_norm_g, loss_target=loss_target, m_w_ada=m_w_ada, m_b_ada=m_b_ada, m_ffn1_norm_g=m_ffn1_norm_g, m_ffn1_w_gate=m_ffn1_w_gate, m_ffn1_w_up=m_ffn1_w_up, m_ffn1_w_down=m_ffn1_w_down, m_mix_norm_g=m_mix_norm_g, m_w_in=m_w_in, m_conv_dw_w=m_conv_dw_w, m_conv_dw_b=m_conv_dw_b, m_conv_ln_g=m_conv_ln_g, m_conv_ln_b=m_conv_ln_b, m_attn_out_g=m_attn_out_g, m_conv_out_g=m_conv_out_g, m_w_out=m_w_out, m_ffn2_norm_g=m_ffn2_norm_g, m_ffn2_w_gate=m_ffn2_w_gate, m_ffn2_w_up=m_ffn2_w_up, m_ffn2_w_down=m_ffn2_w_down, m_final_norm_g=m_final_norm_g, v_w_ada=v_w_ada, v_b_ada=v_b_ada, v_ffn1_norm_g=v_ffn1_norm_g, v_ffn1_w_gate=v_ffn1_w_gate, v_ffn1_w_up=v_ffn1_w_up, v_ffn1_w_down=v_ffn1_w_down, v_mix_norm_g=v_mix_norm_g, v_w_in=v_w_in, v_conv_dw_w=v_conv_dw_w, v_conv_dw_b=v_conv_dw_b, v_conv_ln_g=v_conv_ln_g, v_conv_ln_b=v_conv_ln_b, v_attn_out_g=v_attn_out_g, v_conv_out_g=v_conv_out_g, v_w_out=v_w_out, v_ffn2_norm_g=v_ffn2_norm_g, v_ffn2_w_gate=v_ffn2_w_gate, v_ffn2_w_up=v_ffn2_w_up, v_ffn2_w_down=v_ffn2_w_down, v_final_norm_g=v_final_norm_g)
    weights = {n: given[n] for n in TWIN_WEIGHTS}
    shared = {n: given[n] for n in SHARED_INPUTS}
    per_example = {n: given[n] for n in ['x', 'c']}
    grad_fn = _jax.value_and_grad(_loss, argnums=(0, 1))

    def one_microbatch(ex, loss_target):
        ex = dict(ex)
        diff = ex.pop(TWIN_DIFF_INPUT)
        return grad_fn(weights, diff, {**shared, **ex}, loss_target)

    if N_MICROBATCH == 1:
        loss, (grad_w, grad_x) = one_microbatch(per_example, given["loss_target"])
    else:
        def body(carry, xs):
            loss_sum, grad_sum = carry
            l_k, (gw_k, gx_k) = one_microbatch(xs[0], xs[1])
            with _jax.named_scope("update"):
                return (loss_sum + l_k, _jax.tree.map(_jnp.add, grad_sum, gw_k)), gx_k

        init = (_jnp.zeros((), _jnp.float32), _jax.tree.map(_jnp.zeros_like, weights))
        (loss, grad_w), grad_x = _jax.lax.scan(body, init, (per_example, given["loss_target"]))
    with _jax.named_scope("update"):
        delta_w, new_m, new_v = {}, {}, {}
        for n in TWIN_WEIGHTS:
            delta_w[n], new_m[n], new_v[n] = _adamw(weights[n], grad_w[n], given["m_" + n], given["v_" + n])
    return (loss, grad_x, *[grad_w[n] for n in TWIN_WEIGHTS], *[delta_w[n] for n in TWIN_WEIGHTS],
            *[new_m[n] for n in TWIN_WEIGHTS], *[new_v[n] for n in TWIN_WEIGHTS])
```

```python
import functools

import jax
import jax.numpy as jnp
from jax import lax
from jax.experimental import pallas as pl
from jax.experimental.pallas import tpu as pltpu

F32 = jnp.float32
BF16 = jnp.bfloat16
MESH = pl.DeviceIdType.MESH
ANY = pl.BlockSpec(memory_space=pl.ANY)

N_DEV = 8
N_CHIP = 4
HEAD_DIM = 64
HALF_HEAD = HEAD_DIM // 2
LANES = 128
BLOCK = 128
DILATIONS = (1, 4, 16)
ATTN_TILE = BLOCK * max(DILATIONS)
ROPE_THETA = 10000.0
CONV_KERNEL = 31
CONV_HALO = 32
CONV_CHUNK = 512
CONV_SUB = 128
RMS_EPS = 1e-6
LN_EPS = 1e-5
ADAM_LR = 0.001
ADAM_B1 = 0.9
ADAM_B2 = 0.999
ADAM_EPS = 1e-08
ADAM_WD = 0.01
ADAM_STEP = 10
VMEM_LIMIT = 56 * 1024 * 1024
NEG = -1e30


def _params(n_axes):
    return pltpu.CompilerParams(dimension_semantics=("arbitrary",) * n_axes, vmem_limit_bytes=VMEM_LIMIT)


def _tile(n, target, unit):
    best = None
    for t in range(unit, min(n, target) + 1, unit):
        if n % t == 0:
            best = t
    return best if best is not None else n


def _sigmoid(x):
    return 1.0 / (1.0 + jnp.exp(-x))


def _rows(fn, rows_in, vecs_in, rows_out, vecs_out, *, tile, name):
    norm = [r if isinstance(r, tuple) else (r, r.shape[1], 0) for r in rows_in]
    n_rows = norm[0][0].shape[0]
    n_tiles = n_rows // tile
    in_specs, args = [], []
    for arr, width, cb in norm:
        in_specs.append(pl.BlockSpec((tile, width), functools.partial(lambda i, cb: (i, cb), cb=cb)))
        args.append(arr)
    for v in vecs_in:
        in_specs.append(pl.BlockSpec((1, v.shape[1]), lambda i: (0, 0)))
        args.append(v)
    out_shape = [jax.ShapeDtypeStruct((n_rows, w), dt) for w, dt in rows_out]
    out_shape += [jax.ShapeDtypeStruct((1, w), F32) for w in vecs_out]
    out_specs = [pl.BlockSpec((tile, w), lambda i: (i, 0)) for w, _ in rows_out]
    out_specs += [pl.BlockSpec((1, w), lambda i: (0, 0)) for w in vecs_out]
    n_in, n_ro = len(args), len(rows_out)

    def body(*refs):
        vals = [r[...] for r in refs[:n_in]]
        outs = refs[n_in:]
        row_vals, vec_vals = fn(*vals)
        for ref, val in zip(outs[:n_ro], row_vals):
            if isinstance(val, tuple):
                w = val[0].shape[1]
                for j, piece in enumerate(val):
                    ref[:, j * w:(j + 1) * w] = piece.astype(ref.dtype)
            else:
                ref[...] = val.astype(ref.dtype)
        if vecs_out:
            @pl.when(pl.program_id(0) == 0)
            def _():
                for ref in outs[n_ro:]:
                    ref[...] = jnp.zeros_like(ref)
            for ref, val in zip(outs[n_ro:], vec_vals):
                ref[...] += val

    res = pl.pallas_call(body, grid=(n_tiles,), in_specs=in_specs, out_specs=out_specs, out_shape=out_shape,
                         compiler_params=_params(1), name=name)(*args)
    return res


def _colsum(x):
    return jnp.sum(x, axis=0, keepdims=True)


def _rms_stats(h):
    r = lax.rsqrt(jnp.mean(h * h, axis=-1, keepdims=True) + RMS_EPS)
    return r, h * r


def _rms_back(r, xn, dxn):
    return r * (dxn - xn * jnp.mean(dxn * xn, axis=-1, keepdims=True))


def _norm_mod_fwd(h, gain, scale, shift, name):
    def fn(h, gain, scale, shift):
        _, xn = _rms_stats(h)
        return [(xn * gain) * (1.0 + scale) + shift], []
    return _rows(fn, [h], [gain, scale, shift], [(h.shape[1], BF16)], [], tile=512, name=name)[0]


def _norm_mod_bwd(dn, h, dh_in, gain, scale, name):
    def fn(dn, h, dh_in, gain, scale):
        r, xn = _rms_stats(h)
        y = xn * gain
        dy = dn * (1.0 + scale)
        dh = dh_in + _rms_back(r, xn, dy * gain)
        return [dh], [_colsum(dn), _colsum(dn * y), _colsum(dy * xn)]
    d = h.shape[1]
    return _rows(fn, [dn, h, dh_in], [gain, scale], [(d, F32)], [d, d, d], tile=256, name=name)


def _final_loss(h, target, gain, name):
    d = h.shape[1]

    def fn(h, target, gain):
        r, xn = _rms_stats(h)
        err = xn * gain - target
        dout = err * (1.0 / d)
        dh = _rms_back(r, xn, dout * gain)
        return [dh], [_colsum(err * err), _colsum(dout * xn)]
    return _rows(fn, [h, target], [gain], [(d, F32)], [d, d], tile=256, name=name)


def _gate_bwd(dh, f, gate, coef, name):
    def fn(dh, f, gate):
        return [(coef * gate) * dh], [coef * _colsum(f.astype(F32) * dh)]
    d = dh.shape[1]
    return _rows(fn, [dh, f], [gate], [(d, BF16)], [d], tile=512, name=name)


def _partner(x):
    width = x.shape[1]
    lane = lax.broadcasted_iota(jnp.int32, x.shape, 1) % HEAD_DIM
    return jnp.where(lane < HALF_HEAD, pltpu.roll(x, width - HALF_HEAD, 1), pltpu.roll(x, HALF_HEAD, 1))


def _rope_fwd(proj, cos, sin_signed, width, name):
    qscale = HEAD_DIM ** -0.5

    def fn(q, k, cos, sin):
        qr = q * cos + _partner(q) * sin
        kr = k * cos + _partner(k) * sin
        return [qr * qscale, kr], []
    return _rows(fn, [(proj, width, 0), (proj, width, 1), cos, sin_signed], [], [(width, F32), (width, F32)], [],
                 tile=512, name=name)


def _dproj_assemble(dq, dk, dv, dga, dgb, cos, sin_signed, name):
    width = dq.shape[1]
    qscale = HEAD_DIM ** -0.5

    def fn(dq, dk, dv, dga, dgb, cos, sin):
        dq0 = (dq * cos - _partner(dq) * sin) * qscale
        dk0 = dk * cos - _partner(dk) * sin
        return [(dq0, dk0, dv, dga, dgb)], []
    return _rows(fn, [dq, dk, dv, dga, dgb, cos, sin_signed], [], [(5 * width, BF16)], [], tile=256, name=name)[0]


def _mix_post_fwd(attn, u1, attn_g, ln_g, ln_b, conv_g, name):
    def fn(attn, u1, attn_g, ln_g, ln_b, conv_g):
        _, xa = _rms_stats(attn)
        mu = jnp.mean(u1, axis=-1, keepdims=True)
        xc = u1 - mu
        rstd = lax.rsqrt(jnp.mean(xc * xc, axis=-1, keepdims=True) + LN_EPS)
        u2 = (xc * rstd) * ln_g + ln_b
        u3 = u2 * _sigmoid(u2)
        _, x3 = _rms_stats(u3)
        return [(xa * attn_g, x3 * conv_g)], []
    w = attn.shape[1]
    return _rows(fn, [attn, u1], [attn_g, ln_g, ln_b, conv_g], [(2 * w, BF16)], [], tile=512, name=name)[0]


def _mix_post_bwd(dy, attn, u1, attn_g, ln_g, ln_b, conv_g, name):
    w = attn.shape[1]

    def fn(dya, dyc, attn, u1, attn_g, ln_g, ln_b, conv_g):
        ra, xa = _rms_stats(attn)
        dattn = _rms_back(ra, xa, dya * attn_g)
        mu = jnp.mean(u1, axis=-1, keepdims=True)
        xc = u1 - mu
        rstd = lax.rsqrt(jnp.mean(xc * xc, axis=-1, keepdims=True) + LN_EPS)
        xh = xc * rstd
        u2 = xh * ln_g + ln_b
        sig = _sigmoid(u2)
        u3 = u2 * sig
        r3, x3 = _rms_stats(u3)
        du3 = _rms_back(r3, x3, dyc * conv_g)
        du2 = du3 * (sig + u3 * (1.0 - sig))
        dxh = du2 * ln_g
        du1 = rstd * (dxh - jnp.mean(dxh, axis=-1, keepdims=True) - xh * jnp.mean(dxh * xh, axis=-1, keepdims=True))
        return [dattn, du1], [_colsum(dya * xa), _colsum(dyc * x3), _colsum(du2 * xh), _colsum(du2)]
    return _rows(fn, [(dy, w, 0), (dy, w, 1), attn, u1], [attn_g, ln_g, ln_b, conv_g], [(w, F32), (w, F32)],
                 [w, w, w, w], tile=256, name=name)


def _silu_rows(c_all, name):
    def fn(c):
        return [c * _sigmoid(c)], []
    return _rows(fn, [c_all], [], [(c_all.shape[1], BF16)], [], tile=c_all.shape[0], name=name)[0]


def _mm(groups, epi, extras, vecs, outs, *, trans_rhs, tm, tn, name):
    m = groups[0][0][0].shape[0]
    n = groups[0][0][1].shape[0] if trans_rhs else groups[0][0][1].shape[1]
    tm, tn = min(tm, m), min(tn, n)
    in_specs, args = [], []
    for grp in groups:
        for lhs, rhs in grp:
            k = lhs.shape[1]
            in_specs.append(pl.BlockSpec((tm, k), lambda j, i: (i, 0)))
            in_specs.append(pl.BlockSpec((tn, k), lambda j, i: (j, 0)) if trans_rhs
                            else pl.BlockSpec((k, tn), lambda j, i: (0, j)))
            args += [lhs, rhs]
    for e in extras:
        in_specs.append(pl.BlockSpec((tm, tn), lambda j, i: (i, j)))
        args.append(e)
    for v in vecs:
        in_specs.append(pl.BlockSpec((1, tn), lambda j, i: (0, j)))
        args.append(v)
    sizes = [len(g) for g in groups]
    n_mm, n_ex, n_vec = 2 * sum(sizes), len(extras), len(vecs)
    dims = (((1,), (1,)), ((), ())) if trans_rhs else (((1,), (0,)), ((), ()))

    def body(*refs):
        accs, pos = [], 0
        for size in sizes:
            acc = None
            for _ in range(size):
                part = lax.dot_general(refs[pos][...].astype(BF16), refs[pos + 1][...].astype(BF16), dims,
                                       preferred_element_type=F32)
                acc = part if acc is None else acc + part
                pos += 2
            accs.append(acc)
        ex = [r[...] for r in refs[n_mm:n_mm + n_ex]]
        vc = [r[...] for r in refs[n_mm + n_ex:n_mm + n_ex + n_vec]]
        for ref, val in zip(refs[n_mm + n_ex + n_vec:], epi(accs, ex, vc)):
            ref[...] = val.astype(ref.dtype)

    return pl.pallas_call(
        body, grid=(n // tn, m // tm), in_specs=in_specs,
        out_specs=[pl.BlockSpec((tm, tn), lambda j, i: (i, j)) for _ in outs],
        out_shape=[jax.ShapeDtypeStruct((m, n), dt) for dt in outs],
        compiler_params=_params(2), name=name)(*args)


def _mm_tn(lhs, rhs, name):
    t, a = lhs.shape
    b = rhs.shape[1]
    ta = a if a <= 1536 else _tile(a, 1536, LANES)
    tk = _tile(t, 512, 8)

    def body(l_ref, r_ref, o_ref):
        @pl.when(pl.program_id(1) == 0)
        def _():
            o_ref[...] = jnp.zeros_like(o_ref)
        o_ref[...] += lax.dot_general(l_ref[...].astype(BF16), r_ref[...].astype(BF16), (((0,), (0,)), ((), ())),
                                      preferred_element_type=F32)

    return pl.pallas_call(
        body, grid=(a // ta, t // tk),
        in_specs=[pl.BlockSpec((tk, ta), lambda i, k: (k, i)), pl.BlockSpec((tk, b), lambda i, k: (k, 0))],
        out_specs=pl.BlockSpec((ta, b), lambda i, k: (i, 0)), out_shape=jax.ShapeDtypeStruct((a, b), F32),
        compiler_params=_params(2), name=name)(lhs, rhs)


def _ffn_tn(f):
    return _tile(f, 1536, LANES)


def _ffn_up(n, wg_t, wu_t, name):
    def epi(accs, ex, vc):
        a, b = accs
        return [a, b, (a * _sigmoid(a)) * b]
    return _mm([[(n, wg_t)], [(n, wu_t)]], epi, [], [], [BF16, BF16, BF16], trans_rhs=True, tm=256,
               tn=_ffn_tn(wg_t.shape[0]), name=name)


def _residual_mm(lhs, w, res, gate, coef, name):
    def epi(accs, ex, vc):
        return [ex[0] + (coef * vc[0]) * accs[0], accs[0]]
    return _mm([[(lhs, w)]], epi, [res], [gate], [F32, BF16], trans_rhs=False, tm=512, tn=w.shape[1], name=name)


def _ffn_bwd_hidden(df, wd, a, b, name):
    def epi(accs, ex, vc):
        dh = accs[0]
        av, bv = ex[0].astype(F32), ex[1].astype(F32)
        sig = _sigmoid(av)
        silu = av * sig
        return [dh * bv * (sig + silu * (1.0 - sig)), dh * silu]
    return _mm([[(df, wd)]], epi, [a, b], [], [BF16, BF16], trans_rhs=True, tm=256, tn=_ffn_tn(wd.shape[0]),
               name=name)


def _plain_mm(pairs, out_dtype, trans_rhs, tn, name, tm=512):
    def epi(accs, ex, vc):
        return [accs[0]]
    return _mm([pairs], epi, [], [], [out_dtype], trans_rhs=trans_rhs, tm=tm, tn=tn, name=name)[0]


def _head_masks():
    lane = lax.broadcasted_iota(jnp.int32, (1, LANES), 1)
    return [(lane // HEAD_DIM == h).astype(F32) for h in range(LANES // HEAD_DIM)]


def _band_masks():
    qi = lax.broadcasted_iota(jnp.int32, (BLOCK, BLOCK), 0)
    kj = lax.broadcasted_iota(jnp.int32, (BLOCK, BLOCK), 1)
    return kj <= qi, kj >= qi


def _dot_nt(a, b):
    return lax.dot_general(a.astype(BF16), b.astype(BF16), (((1,), (1,)), ((), ())), preferred_element_type=F32)


def _dot_nn(a, b):
    return lax.dot_general(a.astype(BF16), b.astype(BF16), (((1,), (0,)), ((), ())), preferred_element_type=F32)


def _dot_tn(a, b):
    return lax.dot_general(a.astype(BF16), b.astype(BF16), (((0,), (0,)), ((), ())), preferred_element_type=F32)


def _lane_pick(x, h):
    lane = lax.broadcasted_iota(jnp.int32, x.shape, 1)
    return jnp.sum(jnp.where(lane == h * HEAD_DIM, x, 0.0), axis=1, keepdims=True)


def _attn_specs(width, v_block, n_halo_of):
    cur = pl.BlockSpec((ATTN_TILE, LANES), lambda hb, n: (n, hb))
    nbr = pl.BlockSpec((ATTN_TILE, LANES), lambda hb, n: (n_halo_of(n), hb))
    vcur = pl.BlockSpec((ATTN_TILE, LANES), lambda hb, n: (n, v_block + hb))
    vnbr = pl.BlockSpec((ATTN_TILE, LANES), lambda hb, n: (n_halo_of(n), v_block + hb))
    return cur, nbr, vcur, vnbr


def _attn_fwd(q, k, proj, v_block, name):
    s, width = q.shape
    n_tiles = s // ATTN_TILE
    cur, prev, vcur, vprev = _attn_specs(width, v_block, lambda n: jnp.maximum(n - 1, 0))

    def body(q_ref, k_ref, kp_ref, v_ref, vp_ref, o_ref, l_ref, kk, vv, o_s, l_s):
        n = pl.program_id(1)
        kk[0:ATTN_TILE, :] = kp_ref[...]
        kk[ATTN_TILE:, :] = k_ref[...]
        vv[0:ATTN_TILE, :] = vp_ref[...]
        vv[ATTN_TILE:, :] = v_ref[...]
        hmask = _head_masks()
        same_ok, prev_ok = _band_masks()
        for bi, d in enumerate(DILATIONS):
            span = BLOCK * d

            def blk(idx, carry, bi=bi, d=d, span=span):
                g = idx // d
                q0 = g * span + idx % d
                rows = pl.ds(q0, BLOCK, stride=d)
                qb = q_ref[rows, :]
                kc = kk[pl.ds(ATTN_TILE + q0, BLOCK, stride=d), :]
                kp = kk[pl.ds(ATTN_TILE + q0 - span, BLOCK, stride=d), :]
                vc = vv[pl.ds(ATTN_TILE + q0, BLOCK, stride=d), :]
                vp = vv[pl.ds(ATTN_TILE + q0 - span, BLOCK, stride=d), :]
                has_prev = jnp.logical_or(n > 0, g > 0)
                pmask = jnp.logical_and(prev_ok, has_prev)
                o_acc = jnp.zeros((BLOCK, LANES), F32)
                l_acc = jnp.zeros((BLOCK, LANES), F32)
                for hm in hmask:
                    sc = jnp.where(same_ok, _dot_nt(qb, kc * hm), NEG)
                    sp = jnp.where(pmask, _dot_nt(qb, kp * hm), NEG)
                    mx = jnp.maximum(jnp.max(sc, axis=1, keepdims=True), jnp.max(sp, axis=1, keepdims=True))
                    pc = jnp.exp(sc - mx)
                    pp = jnp.exp(sp - mx)
                    den = jnp.sum(pc, axis=1, keepdims=True) + jnp.sum(pp, axis=1, keepdims=True)
                    o_acc = o_acc + (_dot_nn(pc, vc * hm) + _dot_nn(pp, vp * hm)) / den
                    l_acc = l_acc + (mx + jnp.log(den)) * hm
                o_s[bi, rows, :] = o_acc
                l_s[bi, rows, :] = l_acc
                return carry

            lax.fori_loop(0, ATTN_TILE // BLOCK, blk, 0)
        ls = [l_s[bi] for bi in range(len(DILATIONS))]
        top = functools.reduce(jnp.maximum, ls)
        ws = [jnp.exp(l - top) for l in ls]
        den = functools.reduce(lambda a, b: a + b, ws)
        num = functools.reduce(lambda a, b: a + b, [w * o_s[bi] for bi, w in enumerate(ws)])
        o_ref[...] = num / den
        l_ref[...] = top + jnp.log(den)

    return pl.pallas_call(
        body, grid=(width // LANES, n_tiles), in_specs=[cur, cur, prev, vcur, vprev],
        out_specs=[cur, cur], out_shape=[jax.ShapeDtypeStruct((s, width), F32)] * 2,
        scratch_shapes=[pltpu.VMEM((2 * ATTN_TILE, LANES), F32), pltpu.VMEM((2 * ATTN_TILE, LANES), F32),
                        pltpu.VMEM((len(DILATIONS), ATTN_TILE, LANES), F32),
                        pltpu.VMEM((len(DILATIONS), ATTN_TILE, LANES), F32)],
        compiler_params=_params(2), name=name)(q, k, k, proj, proj)


def _attn_bwd_q(q, k, proj, v_block, do, o, lse, name):
    s, width = q.shape
    n_tiles = s // ATTN_TILE
    cur, prev, vcur, vprev = _attn_specs(width, v_block, lambda n: jnp.maximum(n - 1, 0))

    def body(q_ref, k_ref, kp_ref, v_ref, vp_ref, do_ref, o_ref, l_ref, dq_ref, kk, vv):
        n = pl.program_id(1)
        kk[0:ATTN_TILE, :] = kp_ref[...]
        kk[ATTN_TILE:, :] = k_ref[...]
        vv[0:ATTN_TILE, :] = vp_ref[...]
        vv[ATTN_TILE:, :] = v_ref[...]
        dq_ref[...] = jnp.zeros_like(dq_ref)
        hmask = _head_masks()
        same_ok, prev_ok = _band_masks()
        for d in DILATIONS:
            span = BLOCK * d

            def blk(idx, carry, d=d, span=span):
                g = idx // d
                q0 = g * span + idx % d
                rows = pl.ds(q0, BLOCK, stride=d)
                qb = q_ref[rows, :]
                dob = do_ref[rows, :]
                prod = dob * o_ref[rows, :]
                lb = l_ref[rows, :]
                kc = kk[pl.ds(ATTN_TILE + q0, BLOCK, stride=d), :]
                kp = kk[pl.ds(ATTN_TILE + q0 - span, BLOCK, stride=d), :]
                vc = vv[pl.ds(ATTN_TILE + q0, BLOCK, stride=d), :]
                vp = vv[pl.ds(ATTN_TILE + q0 - span, BLOCK, stride=d), :]
                pmask = jnp.logical_and(prev_ok, jnp.logical_or(n > 0, g > 0))
                dq = jnp.zeros((BLOCK, LANES), F32)
                for h, hm in enumerate(hmask):
                    lh = _lane_pick(lb, h)
                    delta = jnp.sum(prod * hm, axis=1, keepdims=True)
                    kcm, kpm = kc * hm, kp * hm
                    pc = jnp.where(same_ok, jnp.exp(_dot_nt(qb, kcm) - lh), 0.0)
                    pp = jnp.where(pmask, jnp.exp(_dot_nt(qb, kpm) - lh), 0.0)
                    dsc = pc * (_dot_nt(dob, vc * hm) - delta)
                    dsp = pp * (_dot_nt(dob, vp * hm) - delta)
                    dq = dq + _dot_nn(dsc, kcm) + _dot_nn(dsp, kpm)
                dq_ref[rows, :] += dq
                return carry

            lax.fori_loop(0, ATTN_TILE // BLOCK, blk, 0)

    return pl.pallas_call(
        body, grid=(width // LANES, n_tiles), in_specs=[cur, cur, prev, vcur, vprev, cur, cur, cur],
        out_specs=cur, out_shape=jax.ShapeDtypeStruct((s, width), F32),
        scratch_shapes=[pltpu.VMEM((2 * ATTN_TILE, LANES), F32), pltpu.VMEM((2 * ATTN_TILE, LANES), F32)],
        compiler_params=_params(2), name=name)(q, k, k, proj, proj, do, o, lse)


def _attn_bwd_kv(q, k, proj, v_block, do, o, lse, name):
    s, width = q.shape
    n_tiles = s // ATTN_TILE
    cur, nxt, vcur, _ = _attn_specs(width, v_block, lambda n: jnp.minimum(n + 1, n_tiles - 1))

    def body(k_ref, v_ref, q_ref, qn_ref, do_ref, don_ref, o_ref, on_ref, l_ref, ln_ref, dk_ref, dv_ref,
             qq, dd, pr, ll):
        n = pl.program_id(1)
        qq[0:ATTN_TILE, :] = q_ref[...]
        qq[ATTN_TILE:, :] = qn_ref[...]
        dd[0:ATTN_TILE, :] = do_ref[...]
        dd[ATTN_TILE:, :] = don_ref[...]
        pr[0:ATTN_TILE, :] = do_ref[...] * o_ref[...]
        pr[ATTN_TILE:, :] = don_ref[...] * on_ref[...]
        ll[0:ATTN_TILE, :] = l_ref[...]
        ll[ATTN_TILE:, :] = ln_ref[...]
        dk_ref[...] = jnp.zeros_like(dk_ref)
        dv_ref[...] = jnp.zeros_like(dv_ref)
        hmask = _head_masks()
        same_ok, prev_ok = _band_masks()
        for d in DILATIONS:
            span = BLOCK * d
            n_groups = ATTN_TILE // span

            def blk(idx, carry, d=d, span=span, n_groups=n_groups):
                g = idx // d
                k0 = g * span + idx % d
                rows = pl.ds(k0, BLOCK, stride=d)
                kb = k_ref[rows, :]
                vb = v_ref[rows, :]
                has_next = jnp.logical_or(n < n_tiles - 1, g < n_groups - 1)
                nmask = jnp.logical_and(prev_ok, has_next)
                dk = jnp.zeros((BLOCK, LANES), F32)
                dv = jnp.zeros((BLOCK, LANES), F32)
                for off, mask in ((0, same_ok), (span, nmask)):
                    qrows = pl.ds(k0 + off, BLOCK, stride=d)
                    qb = qq[qrows, :]
                    dob = dd[qrows, :]
                    prod = pr[qrows, :]
                    lb = ll[qrows, :]
                    for h, hm in enumerate(hmask):
                        lh = _lane_pick(lb, h)
                        delta = jnp.sum(prod * hm, axis=1, keepdims=True)
                        qm, dom = qb * hm, dob * hm
                        p = jnp.where(mask, jnp.exp(_dot_nt(qm, kb) - lh), 0.0)
                        ds = p * (_dot_nt(dom, vb) - delta)
                        dv = dv + _dot_tn(p, dom)
                        dk = dk + _dot_tn(ds, qm)
                dk_ref[rows, :] += dk
                dv_ref[rows, :] += dv
                return carry

            lax.fori_loop(0, ATTN_TILE // BLOCK, blk, 0)

    return pl.pallas_call(
        body, grid=(width // LANES, n_tiles), in_specs=[cur, vcur, cur, nxt, cur, nxt, cur, nxt, cur, nxt],
        out_specs=[cur, cur], out_shape=[jax.ShapeDtypeStruct((s, width), F32)] * 2,
        scratch_shapes=[pltpu.VMEM((2 * ATTN_TILE, LANES), F32)] * 4,
        compiler_params=_params(2), name=name)(k, proj, q, q, do, do, o, o, lse, lse)


def _conv_specs(s, a_block, b_block):
    per = CONV_CHUNK // CONV_HALO
    a_cur = pl.BlockSpec((CONV_CHUNK, LANES), lambda cb, i: (i, a_block + cb))
    b_cur = pl.BlockSpec((CONV_CHUNK, LANES), lambda cb, i: (i, b_block + cb))
    a_halo = pl.BlockSpec((CONV_HALO, LANES), lambda cb, i: (jnp.maximum(i * per - 1, 0), a_block + cb))
    b_halo = pl.BlockSpec((CONV_HALO, LANES), lambda cb, i: (jnp.maximum(i * per - 1, 0), b_block + cb))
    w_spec = pl.BlockSpec((CONV_KERNEL, LANES), lambda cb, i: (0, cb))
    vec = pl.BlockSpec((1, LANES), lambda cb, i: (0, cb))
    out = pl.BlockSpec((CONV_CHUNK, LANES), lambda cb, i: (i, cb))
    return a_cur, b_cur, a_halo, b_halo, w_spec, vec, out


def _fill_glu_window(win, a_ref, b_ref, ah_ref, bh_ref, first):
    halo = ah_ref[...] * _sigmoid(bh_ref[...])
    win[0:CONV_HALO, :] = jnp.where(first, 0.0, halo)
    win[CONV_HALO:, :] = a_ref[...] * _sigmoid(b_ref[...])


def _conv_fwd(proj, a_block, b_block, w, bias, name):
    s = proj.shape[0]
    cw = w.shape[1]
    a_cur, b_cur, a_halo, b_halo, w_spec, vec, out = _conv_specs(s, a_block, b_block)
    lead = CONV_HALO - (CONV_KERNEL - 1)

    def body(a_ref, b_ref, ah_ref, bh_ref, w_ref, bias_ref, o_ref, win):
        _fill_glu_window(win, a_ref, b_ref, ah_ref, bh_ref, pl.program_id(1) == 0)
        for sub in range(CONV_CHUNK // CONV_SUB):
            base = sub * CONV_SUB
            acc = jnp.zeros((CONV_SUB, LANES), F32) + bias_ref[...]
            for j in range(CONV_KERNEL):
                acc = acc + w_ref[j:j + 1, :] * win[base + lead + j:base + lead + j + CONV_SUB, :]
            o_ref[base:base + CONV_SUB, :] = acc

    return pl.pallas_call(
        body, grid=(cw // LANES, s // CONV_CHUNK), in_specs=[a_cur, b_cur, a_halo, b_halo, w_spec, vec],
        out_specs=out, out_shape=jax.ShapeDtypeStruct((s, cw), F32),
        scratch_shapes=[pltpu.VMEM((CONV_CHUNK + CONV_HALO, LANES), F32)],
        compiler_params=_params(2), name=name)(proj, proj, proj, proj, w, bias)


def _conv_bwd(proj, a_block, b_block, w, du1, name):
    s = proj.shape[0]
    cw = w.shape[1]
    a_cur, b_cur, a_halo, b_halo, w_spec, vec, out = _conv_specs(s, a_block, b_block)
    per = CONV_CHUNK // CONV_HALO
    n_chunks = s // CONV_CHUNK
    d_next = pl.BlockSpec((CONV_HALO, LANES), lambda cb, i: (jnp.minimum((i + 1) * per, s // CONV_HALO - 1), cb))
    lead = CONV_HALO - (CONV_KERNEL - 1)

    def body(a_ref, b_ref, ah_ref, bh_ref, w_ref, d_ref, dn_ref, da_ref, db_ref, dw_ref, dbias_ref, win, dwin):
        i = pl.program_id(1)
        _fill_glu_window(win, a_ref, b_ref, ah_ref, bh_ref, i == 0)
        dwin[0:CONV_CHUNK, :] = d_ref[...]
        dwin[CONV_CHUNK:, :] = jnp.where(i == n_chunks - 1, 0.0, dn_ref[...])

        @pl.when(i == 0)
        def _():
            dw_ref[...] = jnp.zeros_like(dw_ref)
            dbias_ref[...] = jnp.zeros_like(dbias_ref)

        dbias_ref[...] += _colsum(d_ref[...])
        for sub in range(CONV_CHUNK // CONV_SUB):
            base = sub * CONV_SUB
            dcur = dwin[base:base + CONV_SUB, :]
            du0 = jnp.zeros((CONV_SUB, LANES), F32)
            for j in range(CONV_KERNEL):
                back = CONV_KERNEL - 1 - j
                du0 = du0 + w_ref[j:j + 1, :] * dwin[base + back:base + back + CONV_SUB, :]
                dw_ref[j:j + 1, :] += _colsum(dcur * win[base + lead + j:base + lead + j + CONV_SUB, :])
            av = a_ref[base:base + CONV_SUB, :]
            sig = _sigmoid(b_ref[base:base + CONV_SUB, :])
            da_ref[base:base + CONV_SUB, :] = du0 * sig
            db_ref[base:base + CONV_SUB, :] = du0 * av * sig * (1.0 - sig)

    return pl.pallas_call(
        body, grid=(cw // LANES, n_chunks), in_specs=[a_cur, b_cur, a_halo, b_halo, w_spec, out, d_next],
        out_specs=[out, out, w_spec, vec],
        out_shape=[jax.ShapeDtypeStruct((s, cw), F32), jax.ShapeDtypeStruct((s, cw), F32),
                   jax.ShapeDtypeStruct((CONV_KERNEL, cw), F32), jax.ShapeDtypeStruct((1, cw), F32)],
        scratch_shapes=[pltpu.VMEM((CONV_CHUNK + CONV_HALO, LANES), F32)] * 2,
        compiler_params=_params(2), name=name)(proj, proj, proj, proj, w, du1, du1)


def _adamw_math(w, g, m, v):
    m = ADAM_B1 * m + (1.0 - ADAM_B1) * g
    v = ADAM_B2 * v + (1.0 - ADAM_B2) * (g * g)
    m_hat = m / (1.0 - ADAM_B1 ** ADAM_STEP)
    v_hat = v / (1.0 - ADAM_B2 ** ADAM_STEP)
    delta = -ADAM_LR * (m_hat / (jnp.sqrt(v_hat) + ADAM_EPS) + ADAM_WD * w)
    return delta, m, v


def _adamw_big(w, g, m, v, name):
    rows, cols = w.shape
    tile = _tile(rows, 256, 8)
    spec = pl.BlockSpec((tile, cols), lambda i: (i, 0))

    def body(w_ref, g_ref, m_ref, v_ref, d_out, m_out, v_out):
        d_out[...], m_out[...], v_out[...] = _adamw_math(w_ref[...], g_ref[...], m_ref[...], v_ref[...])

    return pl.pallas_call(body, grid=(rows // tile,), in_specs=[spec] * 4, out_specs=[spec] * 3,
                          out_shape=[jax.ShapeDtypeStruct(w.shape, F32)] * 3, compiler_params=_params(1),
                          name=name)(w, g, m, v)


def _adamw_small(ws, gs, ms, vs, name):
    n = len(ws)

    def body(*refs):
        ins, outs = refs[:4 * n], refs[4 * n:]
        for t in range(n):
            res = _adamw_math(ins[t][...], ins[n + t][...], ins[2 * n + t][...], ins[3 * n + t][...])
            for j in range(3):
                outs[j * n + t][...] = res[j]

    shapes = [jax.ShapeDtypeStruct(w.shape, F32) for w in ws]
    res = pl.pallas_call(body, out_shape=shapes * 3, compiler_params=pltpu.CompilerParams(vmem_limit_bytes=VMEM_LIMIT),
                         name=name)(*ws, *gs, *ms, *vs)
    return res[:n], res[n:2 * n], res[2 * n:]


def _sum_blocks(x, n_blocks, name):
    r = x.shape[0] // n_blocks

    def body(x_ref, o_ref):
        acc = x_ref[0:r, :]
        for b in range(1, n_blocks):
            acc = acc + x_ref[b * r:(b + 1) * r, :]
        o_ref[...] = acc

    return pl.pallas_call(body, out_shape=jax.ShapeDtypeStruct((r, x.shape[1]), F32),
                          compiler_params=pltpu.CompilerParams(vmem_limit_bytes=VMEM_LIMIT), name=name)(x)


def _coords():
    return lax.axis_index("x"), lax.axis_index("y"), lax.axis_index("c")


def _flip(v, bit):
    return 1 - v if bit else v


def _ag_small(x, name):
    r, c = x.shape

    def body(x_ref, o_ref, send, recv, local_sem):
        mx, my, mc = _coords()

        def rows(px, py, pc):
            return o_ref.at[pl.ds(pl.multiple_of((4 * px + 2 * py + pc) * r, 8), r), :]

        local = pltpu.make_async_copy(x_ref, rows(mx, my, mc), local_sem)
        local.start()
        peers = [(_flip(mx, k >> 2 & 1), _flip(my, k >> 1 & 1), _flip(mc, k & 1)) for k in range(1, N_DEV)]
        sends = [pltpu.make_async_remote_copy(x_ref, rows(mx, my, mc), send.at[k], recv.at[k], device_id=p,
                                              device_id_type=MESH) for k, p in enumerate(peers)]
        for cp in sends:
            cp.start()
        for k, p in enumerate(peers):
            pltpu.make_async_remote_copy(x_ref, rows(*p), send.at[k], recv.at[k], device_id=p,
                                         device_id_type=MESH).wait_recv()
        for cp in sends:
            cp.wait_send()
        local.wait()

    vm = pl.BlockSpec(memory_space=pltpu.VMEM)
    return pl.pallas_call(
        body, in_specs=[vm], out_specs=vm, out_shape=jax.ShapeDtypeStruct((N_DEV * r, c), x.dtype),
        scratch_shapes=[pltpu.SemaphoreType.DMA((N_DEV - 1,)), pltpu.SemaphoreType.DMA((N_DEV - 1,)),
                        pltpu.SemaphoreType.DMA(())],
        name=name)(x)


def _ag_weights(shards, name):
    n_t = len(shards)

    def body(*refs):
        x_refs, o_refs = refs[:n_t], refs[n_t:2 * n_t]
        send, recv, local_sem = refs[2 * n_t:]
        mx, my, mc = _coords()
        me, sibling = (mx, my, mc), (mx, my, 1 - mc)
        chips = [(1 - mx, my), (mx, 1 - my), (1 - mx, 1 - my)]

        def rows(t, px, py, pc):
            r = x_refs[t].shape[0]
            return o_refs[t].at[pl.ds(pl.multiple_of((4 * px + 2 * py + pc) * r, 8), r), :]

        def copy(t, k, block, to, src=None):
            return pltpu.make_async_remote_copy(
                src_ref=rows(t, *block) if src is None else src, dst_ref=rows(t, *block),
                send_sem=send.at[t, k], recv_sem=recv.at[t, k], device_id=to, device_id_type=MESH)

        locals_, started = [], []
        for t in range(n_t):
            lc = pltpu.make_async_copy(x_refs[t], rows(t, *me), local_sem.at[t])
            lc.start()
            locals_.append(lc)
            first = [copy(t, 0, me, sibling, src=x_refs[t])]
            first += [copy(t, 1 + j, me, (*chip, mc), src=x_refs[t]) for j, chip in enumerate(chips)]
            for cp in first:
                cp.start()
            started += first
        for j, chip in enumerate(chips):
            for t in range(n_t):
                copy(t, 1 + j, (*chip, mc), me).wait_recv()
                passed = copy(t, 4 + j, (*chip, mc), sibling)
                passed.start()
                started.append(passed)
        for t in range(n_t):
            copy(t, 0, sibling, me).wait_recv()
            for j, chip in enumerate(chips):
                copy(t, 4 + j, (*chip, 1 - mc), me).wait_recv()
        for cp in started:
            cp.wait_send()
        for lc in locals_:
            lc.wait()

    return pl.pallas_call(
        body, in_specs=[ANY] * n_t, out_specs=[ANY] * n_t,
        out_shape=[jax.ShapeDtypeStruct((N_DEV * x.shape[0], x.shape[1]), x.dtype) for x in shards],
        scratch_shapes=[pltpu.SemaphoreType.DMA((n_t, 7)), pltpu.SemaphoreType.DMA((n_t, 7)),
                        pltpu.SemaphoreType.DMA((n_t,))],
        name=name)(*shards)


def _rs_sibling(grads, name):
    n_t = len(grads)

    def body(*refs):
        g_refs, land = refs[:n_t], refs[n_t:2 * n_t]
        send, recv = refs[2 * n_t:]
        mx, my, mc = _coords()
        copies = [pltpu.make_async_remote_copy(g_refs[t].at[:, 1 - mc], land[t], send.at[t], recv.at[t],
                                               device_id=(mx, my, 1 - mc), device_id_type=MESH) for t in range(n_t)]
        for cp in copies:
            cp.start()
        for cp in copies:
            cp.wait()

    return pl.pallas_call(
        body, in_specs=[ANY] * n_t, out_specs=[ANY] * n_t,
        out_shape=[jax.ShapeDtypeStruct((N_CHIP,) + g.shape[2:], F32) for g in grads],
        scratch_shapes=[pltpu.SemaphoreType.DMA((n_t,)), pltpu.SemaphoreType.DMA((n_t,))],
        name=name)(*grads)


def _chip_partial(g4, land, name):
    _, _, r, c = g4.shape
    tr = _tile(r, 256, 16)

    def body(g_ref, l_ref, o_ref):
        o_ref[...] = (g_ref[...] + l_ref[...]).astype(o_ref.dtype)

    return pl.pallas_call(
        body, grid=(N_CHIP, r // tr),
        in_specs=[pl.BlockSpec((None, None, tr, c), lambda q, i: (q, lax.axis_index("c"), i, 0)),
                  pl.BlockSpec((None, tr, c), lambda q, i: (q, i, 0))],
        out_specs=pl.BlockSpec((None, tr, c), lambda q, i: (q, i, 0)),
        out_shape=jax.ShapeDtypeStruct((N_CHIP, r, c), BF16), compiler_params=_params(2), name=name)(g4, land)


def _rs_chips(parts, name):
    n_t = len(parts)

    def body(*refs):
        p_refs, land = refs[:n_t], refs[n_t:2 * n_t]
        send, recv, local_sem = refs[2 * n_t:]
        mx, my, mc = _coords()
        my_chip = 2 * mx + my
        flips = [(1, 0), (0, 1), (1, 1)]
        started = []
        for t in range(n_t):
            lc = pltpu.make_async_copy(p_refs[t].at[my_chip], land[t].at[my_chip], local_sem.at[t])
            lc.start()
            started.append(lc)
        sends = []
        for t in range(n_t):
            for k, (fx, fy) in enumerate(flips):
                px, py = _flip(mx, fx), _flip(my, fy)
                cp = pltpu.make_async_remote_copy(p_refs[t].at[2 * px + py], land[t].at[my_chip], send.at[t, k],
                                                  recv.at[t, k], device_id=(px, py, mc), device_id_type=MESH)
                cp.start()
                sends.append(cp)
        for t in range(n_t):
            for k, (fx, fy) in enumerate(flips):
                px, py = _flip(mx, fx), _flip(my, fy)
                pltpu.make_async_remote_copy(p_refs[t].at[my_chip], land[t].at[2 * px + py], send.at[t, k],
                                             recv.at[t, k], device_id=(px, py, mc), device_id_type=MESH).wait_recv()
        for cp in sends:
            cp.wait_send()
        for lc in started:
            lc.wait()

    return pl.pallas_call(
        body, in_specs=[ANY] * n_t, out_specs=[ANY] * n_t,
        out_shape=[jax.ShapeDtypeStruct(p.shape, p.dtype) for p in parts],
        scratch_shapes=[pltpu.SemaphoreType.DMA((n_t, 3)), pltpu.SemaphoreType.DMA((n_t, 3)),
                        pltpu.SemaphoreType.DMA((n_t,))],
        name=name)(*parts)


def _sum_chips(land, name):
    _, r, c = land.shape
    tr = _tile(r, 256, 16)

    def body(l_ref, o_ref):
        acc = l_ref[0].astype(F32)
        for q in range(1, N_CHIP):
            acc = acc + l_ref[q].astype(F32)
        o_ref[...] = acc

    return pl.pallas_call(
        body, grid=(r // tr,), in_specs=[pl.BlockSpec((N_CHIP, tr, c), lambda i: (0, i, 0))],
        out_specs=pl.BlockSpec((tr, c), lambda i: (i, 0)), out_shape=jax.ShapeDtypeStruct((r, c), F32),
        compiler_params=_params(1), name=name)(land)


def _rope_tables(s, width):
    pos = jnp.arange(s, dtype=F32)
    inv_freq = ROPE_THETA ** (-jnp.arange(0, HEAD_DIM, 2, dtype=F32) / HEAD_DIM)
    ang = pos[:, None] * inv_freq[None, :]
    cos, sin = jnp.cos(ang), jnp.sin(ang)
    heads = width // HEAD_DIM
    return jnp.tile(jnp.concatenate([cos, cos], axis=1), (1, heads)), jnp.tile(jnp.concatenate([-sin, sin], axis=1), (1, heads))


def _pad_rows(v, rows):
    return jnp.concatenate([v, jnp.zeros((rows - 1, v.shape[1]), v.dtype)], axis=0)


def kernel(x, c, w_ada, b_ada, ffn1_norm_g, ffn1_w_gate, ffn1_w_up, ffn1_w_down, mix_norm_g, w_in, conv_dw_w, conv_dw_b, conv_ln_g, conv_ln_b, attn_out_g, conv_out_g, w_out, ffn2_norm_g, ffn2_w_gate, ffn2_w_up, ffn2_w_down, final_norm_g, loss_target, m_w_ada, m_b_ada, m_ffn1_norm_g, m_ffn1_w_gate, m_ffn1_w_up, m_ffn1_w_down, m_mix_norm_g, m_w_in, m_conv_dw_w, m_conv_dw_b, m_conv_ln_g, m_conv_ln_b, m_attn_out_g, m_conv_out_g, m_w_out, m_ffn2_norm_g, m_ffn2_w_gate, m_ffn2_w_up, m_ffn2_w_down, m_final_norm_g, v_w_ada, v_b_ada, v_ffn1_norm_g, v_ffn1_w_gate, v_ffn1_w_up, v_ffn1_w_down, v_mix_norm_g, v_w_in, v_conv_dw_w, v_conv_dw_b, v_conv_ln_g, v_conv_ln_b, v_attn_out_g, v_conv_out_g, v_w_out, v_ffn2_norm_g, v_ffn2_w_gate, v_ffn2_w_up, v_ffn2_w_down, v_final_norm_g):
    mx, my, mc = _coords()
    me = 4 * mx + 2 * my + mc
    s, d = x.shape[1], x.shape[2]
    aw = d // 2
    x2, target = x[0], loss_target[0]
    n_mod = w_ada.shape[2] * N_DEV // d
    mod_cols = w_ada.shape[2]

    cw_shard = conv_dw_w.shape[3]
    n_taps = CONV_KERNEL * cw_shard
    first_len = -(-(d + n_taps) // LANES) * LANES
    first = jnp.concatenate([c, conv_dw_w[0, :, 0, :].reshape(1, n_taps), jnp.zeros((1, first_len - d - n_taps), F32)], axis=1)
    first_all = _ag_small(_pad_rows(first, 8), "ag_c_taps")[0::8]
    c_all = first_all[:, :d]
    conv_w = first_all[:, d:d + n_taps].reshape(N_DEV, CONV_KERNEL, cw_shard).transpose(1, 0, 2).reshape(CONV_KERNEL, aw)

    silu_c = _silu_rows(c_all, "silu_c")
    mod_part = _plain_mm([(silu_c, w_ada[0])], F32, False, mod_cols, "mod_mm")
    mod_all = _ag_small(mod_part, "ag_mod").reshape(N_DEV, N_DEV, mod_cols)
    mod = lax.dynamic_index_in_dim(mod_all, me, axis=1, keepdims=False).reshape(1, n_mod * d) + b_ada
    sh1, sc1, g1, sh2, sc2, g2, sh3, sc3, g3 = [mod[:, i * d:(i + 1) * d] for i in range(n_mod)]

    shards = [ffn1_w_gate[0].T, ffn1_w_up[0].T, ffn1_w_down[0], w_in[0].T, w_out[0],
              ffn2_w_gate[0].T, ffn2_w_up[0].T, ffn2_w_down[0]]
    wg1, wu1, wd1, win_t, wout, wg2, wu2, wd2 = _ag_weights([w.astype(BF16) for w in shards], "ag_weights")

    n1 = _norm_mod_fwd(x2, ffn1_norm_g, sc1, sh1, "norm1")
    a1, b1, hid1 = _ffn_up(n1, wg1, wu1, "ffn1_up")
    h1, f1 = _residual_mm(hid1, wd1, x2, g1, 0.5, "ffn1_down")
    n2 = _norm_mod_fwd(h1, mix_norm_g, sc2, sh2, "norm2")
    proj = _plain_mm([(n2, win_t)], F32, True, _tile(5 * aw, 1536, LANES), "proj")
    cos, sin_signed = _rope_tables(s, aw)
    q_rot, k_rot = _rope_fwd(proj, cos, sin_signed, aw, "rope")
    lanes_per = aw // LANES
    attn, lse = _attn_fwd(q_rot, k_rot, proj, 2 * lanes_per, "attn_fwd")
    u1 = _conv_fwd(proj, 3 * lanes_per, 4 * lanes_per, conv_w, conv_dw_b, "conv_fwd")
    y = _mix_post_fwd(attn, u1, attn_out_g, conv_ln_g, conv_ln_b, conv_out_g, "mix_post")
    h2, mix = _residual_mm(y, wout, h1, g2, 1.0, "mix_out")
    n3 = _norm_mod_fwd(h2, ffn2_norm_g, sc3, sh3, "norm3")
    a3, b3, hid3 = _ffn_up(n3, wg2, wu2, "ffn2_up")
    h3, f3 = _residual_mm(hid3, wd2, h2, g3, 0.5, "ffn2_down")

    dh3, err2, d_final_g = _final_loss(h3, target, final_norm_g.reshape(1, d), "final_loss")
    loss = lax.psum(0.5 * jnp.sum(err2) / d, ("x", "y", "c"))

    df3, dg3 = _gate_bwd(dh3, f3, g3, 0.5, "gate3_bwd")
    da3, db3 = _ffn_bwd_hidden(df3, wd2, a3, b3, "ffn2_hidden_bwd")
    dn3 = _plain_mm([(da3, wg2), (db3, wu2)], F32, False, d, "ffn2_dn", tm=256)
    dh2, dsh3, dsc3, dgn3 = _norm_mod_bwd(dn3, h2, dh3, ffn2_norm_g, sc3, "norm3_bwd")
    g_wg2, g_wu2, g_wd2 = _mm_tn(da3, n3, "ffn2_dwg"), _mm_tn(db3, n3, "ffn2_dwu"), _mm_tn(hid3, df3, "ffn2_dwd")

    dmix, dg2 = _gate_bwd(dh2, mix, g2, 1.0, "gate2_bwd")
    dy = _plain_mm([(dmix, wout)], F32, True, d, "mix_dy")
    g_wout = _mm_tn(y, dmix, "mix_dwout")
    dattn, du1, d_attn_g, d_conv_g, d_ln_g, d_ln_b = _mix_post_bwd(
        dy, attn, u1, attn_out_g, conv_ln_g, conv_ln_b, conv_out_g, "mix_post_bwd")
    dga, dgb, d_taps, d_conv_b = _conv_bwd(proj, 3 * lanes_per, 4 * lanes_per, conv_w, du1, "conv_bwd")
    dq = _attn_bwd_q(q_rot, k_rot, proj, 2 * lanes_per, dattn, attn, lse, "attn_bwd_q")
    dk, dv = _attn_bwd_kv(q_rot, k_rot, proj, 2 * lanes_per, dattn, attn, lse, "attn_bwd_kv")
    dproj = _dproj_assemble(dq, dk, dv, dga, dgb, cos, sin_signed, "dproj")
    dn2 = _plain_mm([(dproj, win_t)], F32, False, d, "mix_dn")
    g_win = _mm_tn(dproj, n2, "mix_dwin")
    dh1, dsh2, dsc2, dgn2 = _norm_mod_bwd(dn2, h1, dh2, mix_norm_g, sc2, "norm2_bwd")

    df1, dg1 = _gate_bwd(dh1, f1, g1, 0.5, "gate1_bwd")
    da1, db1 = _ffn_bwd_hidden(df1, wd1, a1, b1, "ffn1_hidden_bwd")
    dn1 = _plain_mm([(da1, wg1), (db1, wu1)], F32, False, d, "ffn1_dn", tm=256)
    dx, dsh1, dsc1, dgn1 = _norm_mod_bwd(dn1, x2, dh1, ffn1_norm_g, sc1, "norm1_bwd")
    g_wg1, g_wu1, g_wd1 = _mm_tn(da1, n1, "ffn1_dwg"), _mm_tn(db1, n1, "ffn1_dwu"), _mm_tn(hid1, df1, "ffn1_dwd")

    dmod = jnp.concatenate([dsh1, dsc1, dg1, dsh2, dsc2, dg2, dsh3, dsc3, dg3], axis=1)
    small = [dmod, dgn1, dgn2, dgn3, d_final_g, d_conv_b, d_ln_g, d_ln_b, d_attn_g, d_conv_g,
             d_taps.reshape(1, CONV_KERNEL * aw)]
    sizes = [v.shape[1] for v in small]
    total = sum(sizes)
    padded = -(-total // (8 * LANES)) * (8 * LANES)
    packed = jnp.concatenate(small + [jnp.zeros((1, padded - total), F32)], axis=1).reshape(8, padded // 8)
    gathered = _ag_small(packed, "ag_small_grads")
    summed = _sum_blocks(gathered, N_DEV, "sum_small_grads").reshape(1, padded)
    offs = [sum(sizes[:i]) for i in range(len(sizes))]
    (g_b_ada, g_gn1, g_gn2, g_gn3, g_final, g_conv_b, g_ln_g, g_ln_b, g_attn_g, g_conv_g, g_taps) = [
        summed[:, o:o + n] for o, n in zip(offs, sizes)]
    g_taps_shard = lax.dynamic_slice_in_dim(g_taps.reshape(CONV_KERNEL, aw), me * cw_shard, cw_shard, axis=1)
    dmod_all = gathered.reshape(N_DEV, padded)[:, :n_mod * d]
    dmod_cols = lax.dynamic_slice_in_dim(dmod_all, me * mod_cols, mod_cols, axis=1)
    g_w_ada = _mm_tn(silu_c, dmod_cols, "ada_dw")

    full = [g_wg1, g_wu1, g_wd1, g_win, g_wout, g_wg2, g_wu2, g_wd2]
    g4 = [g.reshape(N_CHIP, 2, g.shape[0] // N_DEV, g.shape[1]) for g in full]
    land = _rs_sibling(g4, "rs_sibling")
    parts = [_chip_partial(a, b, "chip_partial_%d" % t) for t, (a, b) in enumerate(zip(g4, land))]
    land2 = _rs_chips(parts, "rs_chips")
    sums = [_sum_chips(l, "sum_chips_%d" % t) for t, l in enumerate(land2)]
    s_wg1, s_wu1, s_wd1, s_win, s_wout, s_wg2, s_wu2, s_wd2 = sums

    grads = {
        "w_ada": g_w_ada, "b_ada": g_b_ada, "ffn1_norm_g": g_gn1, "ffn1_w_gate": s_wg1.T, "ffn1_w_up": s_wu1.T,
        "ffn1_w_down": s_wd1, "mix_norm_g": g_gn2, "w_in": s_win.T, "conv_dw_w": g_taps_shard, "conv_dw_b": g_conv_b,
        "conv_ln_g": g_ln_g, "conv_ln_b": g_ln_b, "attn_out_g": g_attn_g, "conv_out_g": g_conv_g, "w_out": s_wout,
        "ffn2_norm_g": g_gn3, "ffn2_w_gate": s_wg2.T, "ffn2_w_up": s_wu2.T, "ffn2_w_down": s_wd2,
        "final_norm_g": g_final,
    }
    weights = dict(w_ada=w_ada, b_ada=b_ada, ffn1_norm_g=ffn1_norm_g, ffn1_w_gate=ffn1_w_gate, ffn1_w_up=ffn1_w_up, ffn1_w_down=ffn1_w_down, mix_norm_g=mix_norm_g, w_in=w_in, conv_dw_w=conv_dw_w, conv_dw_b=conv_dw_b, conv_ln_g=conv_ln_g, conv_ln_b=conv_ln_b, attn_out_g=attn_out_g, conv_out_g=conv_out_g, w_out=w_out, ffn2_norm_g=ffn2_norm_g, ffn2_w_gate=ffn2_w_gate, ffn2_w_up=ffn2_w_up, ffn2_w_down=ffn2_w_down, final_norm_g=final_norm_g)
    moms = dict(w_ada=m_w_ada, b_ada=m_b_ada, ffn1_norm_g=m_ffn1_norm_g, ffn1_w_gate=m_ffn1_w_gate, ffn1_w_up=m_ffn1_w_up, ffn1_w_down=m_ffn1_w_down, mix_norm_g=m_mix_norm_g, w_in=m_w_in, conv_dw_w=m_conv_dw_w, conv_dw_b=m_conv_dw_b, conv_ln_g=m_conv_ln_g, conv_ln_b=m_conv_ln_b, attn_out_g=m_attn_out_g, conv_out_g=m_conv_out_g, w_out=m_w_out, ffn2_norm_g=m_ffn2_norm_g, ffn2_w_gate=m_ffn2_w_gate, ffn2_w_up=m_ffn2_w_up, ffn2_w_down=m_ffn2_w_down, final_norm_g=m_final_norm_g)
    vars_ = dict(w_ada=v_w_ada, b_ada=v_b_ada, ffn1_norm_g=v_ffn1_norm_g, ffn1_w_gate=v_ffn1_w_gate, ffn1_w_up=v_ffn1_w_up, ffn1_w_down=v_ffn1_w_down, mix_norm_g=v_mix_norm_g, w_in=v_w_in, conv_dw_w=v_conv_dw_w, conv_dw_b=v_conv_dw_b, conv_ln_g=v_conv_ln_g, conv_ln_b=v_conv_ln_b, attn_out_g=v_attn_out_g, conv_out_g=v_conv_out_g, w_out=v_w_out, ffn2_norm_g=v_ffn2_norm_g, ffn2_w_gate=v_ffn2_w_gate, ffn2_w_up=v_ffn2_w_up, ffn2_w_down=v_ffn2_w_down, final_norm_g=v_final_norm_g)
    names = list(weights)
    big = ["w_ada", "ffn1_w_gate", "ffn1_w_up", "ffn1_w_down", "w_in", "w_out", "ffn2_w_gate", "ffn2_w_up",
           "ffn2_w_down"]
    shape2 = {n: (weights[n].shape[-2] if weights[n].ndim > 1 else 1, weights[n].shape[-1]) for n in names}
    shape2["conv_dw_w"] = (CONV_KERNEL, cw_shard)
    g_out, d_out, m_out, v_out = {}, {}, {}, {}
    for n in big:
        g2d = grads[n].reshape(shape2[n])
        res = _adamw_big(weights[n].reshape(shape2[n]), g2d, moms[n].reshape(shape2[n]), vars_[n].reshape(shape2[n]),
                         "adamw_" + n)
        g_out[n], (d_out[n], m_out[n], v_out[n]) = g2d, res
    rest = [n for n in names if n not in big]
    res = _adamw_small([weights[n].reshape(shape2[n]) for n in rest], [grads[n].reshape(shape2[n]) for n in rest],
                       [moms[n].reshape(shape2[n]) for n in rest], [vars_[n].reshape(shape2[n]) for n in rest],
                       "adamw_small")
    for i, n in enumerate(rest):
        g_out[n], d_out[n], m_out[n], v_out[n] = grads[n], res[0][i], res[1][i], res[2][i]

    def shaped(table):
        return [table[n].reshape(weights[n].shape) for n in names]

    return (loss, dx.reshape(x.shape), *shaped(g_out), *shaped(d_out), *shaped(m_out), *shaped(v_out))
```

```python
import functools

import jax
import jax.numpy as jnp
from jax import lax
from jax.experimental import pallas as pl
from jax.experimental.pallas import tpu as pltpu

F32 = jnp.float32
BF16 = jnp.bfloat16
MESH = pl.DeviceIdType.MESH
ANY = pl.BlockSpec(memory_space=pl.ANY)

N_DEV = 8
N_CHIP = 4
HEAD_DIM = 64
HALF_HEAD = HEAD_DIM // 2
LANES = 128
BLOCK = 128
DILATIONS = (1, 4, 16)
ATTN_TILE = BLOCK * max(DILATIONS)
ROPE_THETA = 10000.0
CONV_KERNEL = 31
CONV_HALO = 32
CONV_CHUNK = 512
CONV_SUB = 128
RMS_EPS = 1e-6
LN_EPS = 1e-5
ADAM_LR = 0.001
ADAM_B1 = 0.9
ADAM_B2 = 0.999
ADAM_EPS = 1e-08
ADAM_WD = 0.01
ADAM_STEP = 10
VMEM_LIMIT = 56 * 1024 * 1024
NEG = -1e30


def _params(n_axes):
    return pltpu.CompilerParams(dimension_semantics=("arbitrary",) * n_axes, vmem_limit_bytes=VMEM_LIMIT)


def _tile(n, target, unit):
    best = None
    for t in range(unit, min(n, target) + 1, unit):
        if n % t == 0:
            best = t
    return best if best is not None else n


def _sigmoid(x):
    return 1.0 / (1.0 + jnp.exp(-x))


def _rows(fn, rows_in, vecs_in, rows_out, vecs_out, *, tile, name):
    norm = [r if isinstance(r, tuple) else (r, r.shape[1], 0) for r in rows_in]
    n_rows = norm[0][0].shape[0]
    n_tiles = n_rows // tile
    in_specs, args = [], []
    for arr, width, cb in norm:
        in_specs.append(pl.BlockSpec((tile, width), functools.partial(lambda i, cb: (i, cb), cb=cb)))
        args.append(arr)
    for v in vecs_in:
        in_specs.append(pl.BlockSpec((1, v.shape[1]), lambda i: (0, 0)))
        args.append(v)
    out_shape = [jax.ShapeDtypeStruct((n_rows, w), dt) for w, dt in rows_out]
    out_shape += [jax.ShapeDtypeStruct((1, w), F32) for w in vecs_out]
    out_specs = [pl.BlockSpec((tile, w), lambda i: (i, 0)) for w, _ in rows_out]
    out_specs += [pl.BlockSpec((1, w), lambda i: (0, 0)) for w in vecs_out]
    n_in, n_ro = len(args), len(rows_out)

    def body(*refs):
        vals = [r[...] for r in refs[:n_in]]
        outs = refs[n_in:]
        row_vals, vec_vals = fn(*vals)
        for ref, val in zip(outs[:n_ro], row_vals):
            if isinstance(val, tuple):
                w = val[0].shape[1]
                for j, piece in enumerate(val):
                    ref[:, j * w:(j + 1) * w] = piece.astype(ref.dtype)
            else:
                ref[...] = val.astype(ref.dtype)
        if vecs_out:
            @pl.when(pl.program_id(0) == 0)
            def _():
                for ref in outs[n_ro:]:
                    ref[...] = jnp.zeros_like(ref)
            for ref, val in zip(outs[n_ro:], vec_vals):
                ref[...] += val

    res = pl.pallas_call(body, grid=(n_tiles,), in_specs=in_specs, out_specs=out_specs, out_shape=out_shape,
                         compiler_params=_params(1), name=name)(*args)
    return res


def _colsum(x):
    return jnp.sum(x, axis=0, keepdims=True)


def _rms_stats(h):
    r = lax.rsqrt(jnp.mean(h * h, axis=-1, keepdims=True) + RMS_EPS)
    return r, h * r


def _rms_back(r, xn, dxn):
    return r * (dxn - xn * jnp.mean(dxn * xn, axis=-1, keepdims=True))


def _norm_mod_fwd(h, gain, scale, shift, name):
    def fn(h, gain, scale, shift):
        _, xn = _rms_stats(h)
        return [(xn * gain) * (1.0 + scale) + shift], []
    return _rows(fn, [h], [gain, scale, shift], [(h.shape[1], BF16)], [], tile=512, name=name)[0]


def _norm_mod_bwd(dn, h, dh_in, gain, scale, name):
    def fn(dn, h, dh_in, gain, scale):
        r, xn = _rms_stats(h)
        y = xn * gain
        dy = dn * (1.0 + scale)
        dh = dh_in + _rms_back(r, xn, dy * gain)
        return [dh], [_colsum(dn), _colsum(dn * y), _colsum(dy * xn)]
    d = h.shape[1]
    return _rows(fn, [dn, h, dh_in], [gain, scale], [(d, F32)], [d, d, d], tile=256, name=name)


def _final_loss(h, target, gain, name):
    d = h.shape[1]

    def fn(h, target, gain):
        r, xn = _rms_stats(h)
        err = xn * gain - target
        dout = err * (1.0 / d)
        dh = _rms_back(r, xn, dout * gain)
        return [dh], [_colsum(err * err), _colsum(dout * xn)]
    return _rows(fn, [h, target], [gain], [(d, F32)], [d, d], tile=256, name=name)


def _gate_bwd(dh, f, gate, coef, name):
    def fn(dh, f, gate):
        return [(coef * gate) * dh], [coef * _colsum(f.astype(F32) * dh)]
    d = dh.shape[1]
    return _rows(fn, [dh, f], [gate], [(d, BF16)], [d], tile=512, name=name)


def _partner(x):
    width = x.shape[1]
    lane = lax.broadcasted_iota(jnp.int32, x.shape, 1) % HEAD_DIM
    return jnp.where(lane < HALF_HEAD, pltpu.roll(x, width - HALF_HEAD, 1), pltpu.roll(x, HALF_HEAD, 1))


def _rope_fwd(proj, cos, sin_signed, width, name):
    qscale = HEAD_DIM ** -0.5

    def fn(q, k, cos, sin):
        qr = q * cos + _partner(q) * sin
        kr = k * cos + _partner(k) * sin
        return [qr * qscale, kr], []
    return _rows(fn, [(proj, width, 0), (proj, width, 1), cos, sin_signed], [], [(width, F32), (width, F32)], [],
                 tile=512, name=name)


def _dproj_assemble(dq, dk, dv, dga, dgb, cos, sin_signed, name):
    width = dq.shape[1]
    qscale = HEAD_DIM ** -0.5

    def fn(dq, dk, dv, dga, dgb, cos, sin):
        dq0 = (dq * cos - _partner(dq) * sin) * qscale
        dk0 = dk * cos - _partner(dk) * sin
        return [(dq0, dk0, dv, dga, dgb)], []
    return _rows(fn, [dq, dk, dv, dga, dgb, cos, sin_signed], [], [(5 * width, BF16)], [], tile=256, name=name)[0]


def _mix_post_fwd(attn, u1, attn_g, ln_g, ln_b, conv_g, name):
    def fn(attn, u1, attn_g, ln_g, ln_b, conv_g):
        _, xa = _rms_stats(attn)
        mu = jnp.mean(u1, axis=-1, keepdims=True)
        xc = u1 - mu
        rstd = lax.rsqrt(jnp.mean(xc * xc, axis=-1, keepdims=True) + LN_EPS)
        u2 = (xc * rstd) * ln_g + ln_b
        u3 = u2 * _sigmoid(u2)
        _, x3 = _rms_stats(u3)
        return [(xa * attn_g, x3 * conv_g)], []
    w = attn.shape[1]
    return _rows(fn, [attn, u1], [attn_g, ln_g, ln_b, conv_g], [(2 * w, BF16)], [], tile=512, name=name)[0]


def _mix_post_bwd(dy, attn, u1, attn_g, ln_g, ln_b, conv_g, name):
    w = attn.shape[1]

    def fn(dya, dyc, attn, u1, attn_g, ln_g, ln_b, conv_g):
        ra, xa = _rms_stats(attn)
        dattn = _rms_back(ra, xa, dya * attn_g)
        mu = jnp.mean(u1, axis=-1, keepdims=True)
        xc = u1 - mu
        rstd = lax.rsqrt(jnp.mean(xc * xc, axis=-1, keepdims=True) + LN_EPS)
        xh = xc * rstd
        u2 = xh * ln_g + ln_b
        sig = _sigmoid(u2)
        u3 = u2 * sig
        r3, x3 = _rms_stats(u3)
        du3 = _rms_back(r3, x3, dyc * conv_g)
        du2 = du3 * (sig + u3 * (1.0 - sig))
        dxh = du2 * ln_g
        du1 = rstd * (dxh - jnp.mean(dxh, axis=-1, keepdims=True) - xh * jnp.mean(dxh * xh, axis=-1, keepdims=True))
        return [dattn, du1], [_colsum(dya * xa), _colsum(dyc * x3), _colsum(du2 * xh), _colsum(du2)]
    return _rows(fn, [(dy, w, 0), (dy, w, 1), attn, u1], [attn_g, ln_g, ln_b, conv_g], [(w, F32), (w, F32)],
                 [w, w, w, w], tile=256, name=name)


def _silu_rows(c_all, name):
    def fn(c):
        return [c * _sigmoid(c)], []
    return _rows(fn, [c_all], [], [(c_all.shape[1], BF16)], [], tile=c_all.shape[0], name=name)[0]


def _mm(groups, epi, extras, vecs, outs, *, trans_rhs, tm, tn, name):
    m = groups[0][0][0].shape[0]
    n = groups[0][0][1].shape[0] if trans_rhs else groups[0][0][1].shape[1]
    tm, tn = min(tm, m), min(tn, n)
    in_specs, args = [], []
    for grp in groups:
        for lhs, rhs in grp:
            k = lhs.shape[1]
            in_specs.append(pl.BlockSpec((tm, k), lambda j, i: (i, 0)))
            in_specs.append(pl.BlockSpec((tn, k), lambda j, i: (j, 0)) if trans_rhs
                            else pl.BlockSpec((k, tn), lambda j, i: (0, j)))
            args += [lhs, rhs]
    for e in extras:
        in_specs.append(pl.BlockSpec((tm, tn), lambda j, i: (i, j)))
        args.append(e)
    for v in vecs:
        in_specs.append(pl.BlockSpec((1, tn), lambda j, i: (0, j)))
        args.append(v)
    sizes = [len(g) for g in groups]
    n_mm, n_ex, n_vec = 2 * sum(sizes), len(extras), len(vecs)
    dims = (((1,), (1,)), ((), ())) if trans_rhs else (((1,), (0,)), ((), ()))

    def body(*refs):
        accs, pos = [], 0
        for size in sizes:
            acc = None
            for _ in range(size):
                part = lax.dot_general(refs[pos][...].astype(BF16), refs[pos + 1][...].astype(BF16), dims,
                                       preferred_element_type=F32)
                acc = part if acc is None else acc + part
                pos += 2
            accs.append(acc)
        ex = [r[...] for r in refs[n_mm:n_mm + n_ex]]
        vc = [r[...] for r in refs[n_mm + n_ex:n_mm + n_ex + n_vec]]
        for ref, val in zip(refs[n_mm + n_ex + n_vec:], epi(accs, ex, vc)):
            ref[...] = val.astype(ref.dtype)

    return pl.pallas_call(
        body, grid=(n // tn, m // tm), in_specs=in_specs,
        out_specs=[pl.BlockSpec((tm, tn), lambda j, i: (i, j)) for _ in outs],
        out_shape=[jax.ShapeDtypeStruct((m, n), dt) for dt in outs],
        compiler_params=_params(2), name=name)(*args)


def _mm_tn(lhs, rhs, name):
    t, a = lhs.shape
    b = rhs.shape[1]
    ta = a if a <= 1536 else _tile(a, 1536, LANES)
    tk = _tile(t, 512, 8)

    def body(l_ref, r_ref, o_ref):
        @pl.when(pl.program_id(1) == 0)
        def _():
            o_ref[...] = jnp.zeros_like(o_ref)
        o_ref[...] += lax.dot_general(l_ref[...].astype(BF16), r_ref[...].astype(BF16), (((0,), (0,)), ((), ())),
                                      preferred_element_type=F32)

    return pl.pallas_call(
        body, grid=(a // ta, t // tk),
        in_specs=[pl.BlockSpec((tk, ta), lambda i, k: (k, i)), pl.BlockSpec((tk, b), lambda i, k: (k, 0))],
        out_specs=pl.BlockSpec((ta, b), lambda i, k: (i, 0)), out_shape=jax.ShapeDtypeStruct((a, b), F32),
        compiler_params=_params(2), name=name)(lhs, rhs)


def _ffn_tn(f):
    return _tile(f, 1536, LANES)


def _ffn_up(n, wg_t, wu_t, name):
    def epi(accs, ex, vc):
        a, b = accs
        return [a, b, (a * _sigmoid(a)) * b]
    return _mm([[(n, wg_t)], [(n, wu_t)]], epi, [], [], [BF16, BF16, BF16], trans_rhs=True, tm=256,
               tn=_ffn_tn(wg_t.shape[0]), name=name)


def _residual_mm(lhs, w, res, gate, coef, name):
    def epi(accs, ex, vc):
        return [ex[0] + (coef * vc[0]) * accs[0], accs[0]]
    return _mm([[(lhs, w)]], epi, [res], [gate], [F32, BF16], trans_rhs=False, tm=512, tn=w.shape[1], name=name)


def _ffn_bwd_hidden(df, wd, a, b, name):
    def epi(accs, ex, vc):
        dh = accs[0]
        av, bv = ex[0].astype(F32), ex[1].astype(F32)
        sig = _sigmoid(av)
        silu = av * sig
        return [dh * bv * (sig + silu * (1.0 - sig)), dh * silu]
    return _mm([[(df, wd)]], epi, [a, b], [], [BF16, BF16], trans_rhs=True, tm=256, tn=_ffn_tn(wd.shape[0]),
               name=name)


def _plain_mm(pairs, out_dtype, trans_rhs, tn, name, tm=512):
    def epi(accs, ex, vc):
        return [accs[0]]
    return _mm([pairs], epi, [], [], [out_dtype], trans_rhs=trans_rhs, tm=tm, tn=tn, name=name)[0]


HEADS_PER_TILE = LANES // HEAD_DIM


def _stack_heads(x):
    lane = lax.broadcasted_iota(jnp.int32, (1, LANES), 1)
    return jnp.concatenate([x * (lane // HEAD_DIM == h).astype(F32) for h in range(HEADS_PER_TILE)], axis=0)


def _unstack_heads(y):
    r = y.shape[0] // HEADS_PER_TILE
    lane = lax.broadcasted_iota(jnp.int32, (r, y.shape[1]), 1)
    out = y[0:r]
    for h in range(1, HEADS_PER_TILE):
        out = jnp.where(lane // HEAD_DIM == h, y[h * r:(h + 1) * r], out)
    return out


def _stacked_lse(lb):
    return jnp.concatenate([_lane_pick(lb, h) for h in range(HEADS_PER_TILE)], axis=0)


def _band_masks(n_row_blocks, n_col_blocks):
    shape = (n_row_blocks * BLOCK, n_col_blocks * BLOCK)
    qi = lax.broadcasted_iota(jnp.int32, shape, 0) % BLOCK
    kj = lax.broadcasted_iota(jnp.int32, shape, 1) % BLOCK
    return kj <= qi, kj >= qi


def _query_masks():
    same_ok, before_ok = _band_masks(HEADS_PER_TILE, 2)
    is_cur = lax.broadcasted_iota(jnp.int32, same_ok.shape, 1) >= BLOCK
    return jnp.logical_and(is_cur, same_ok), jnp.logical_and(jnp.logical_not(is_cur), before_ok)


def _dot_nt(a, b):
    return lax.dot_general(a.astype(BF16), b.astype(BF16), (((1,), (1,)), ((), ())), preferred_element_type=F32)


def _dot_nn(a, b):
    return lax.dot_general(a.astype(BF16), b.astype(BF16), (((1,), (0,)), ((), ())), preferred_element_type=F32)


def _dot_tn(a, b):
    return lax.dot_general(a.astype(BF16), b.astype(BF16), (((0,), (0,)), ((), ())), preferred_element_type=F32)


def _lane_pick(x, h):
    lane = lax.broadcasted_iota(jnp.int32, x.shape, 1)
    return jnp.sum(jnp.where(lane == h * HEAD_DIM, x, 0.0), axis=1, keepdims=True)


def _attn_specs(width, v_block, n_halo_of):
    cur = pl.BlockSpec((ATTN_TILE, LANES), lambda hb, n: (n, hb))
    nbr = pl.BlockSpec((ATTN_TILE, LANES), lambda hb, n: (n_halo_of(n), hb))
    vcur = pl.BlockSpec((ATTN_TILE, LANES), lambda hb, n: (n, v_block + hb))
    vnbr = pl.BlockSpec((ATTN_TILE, LANES), lambda hb, n: (n_halo_of(n), v_block + hb))
    return cur, nbr, vcur, vnbr


def _attn_fwd(q, k, proj, v_block, name):
    s, width = q.shape
    n_tiles = s // ATTN_TILE
    cur, prev, vcur, vprev = _attn_specs(width, v_block, lambda n: jnp.maximum(n - 1, 0))

    def body(q_ref, k_ref, kp_ref, v_ref, vp_ref, o_ref, l_ref, kk, vv, o_s, l_s):
        n = pl.program_id(1)
        kk[0:ATTN_TILE, :] = kp_ref[...]
        kk[ATTN_TILE:, :] = k_ref[...]
        vv[0:ATTN_TILE, :] = vp_ref[...]
        vv[ATTN_TILE:, :] = v_ref[...]
        cur_valid, prev_valid = _query_masks()
        for bi, d in enumerate(DILATIONS):
            span = BLOCK * d

            def blk(idx, carry, bi=bi, d=d, span=span):
                g = idx // d
                q0 = g * span + idx % d
                rows = pl.ds(q0, BLOCK, stride=d)
                q2 = _stack_heads(q_ref[rows, :])
                keys = jnp.concatenate([kk[pl.ds(ATTN_TILE + q0 - span, BLOCK, stride=d), :],
                                        kk[pl.ds(ATTN_TILE + q0, BLOCK, stride=d), :]], axis=0)
                vals = jnp.concatenate([vv[pl.ds(ATTN_TILE + q0 - span, BLOCK, stride=d), :],
                                        vv[pl.ds(ATTN_TILE + q0, BLOCK, stride=d), :]], axis=0)
                has_prev = jnp.logical_or(n > 0, g > 0)
                valid = jnp.logical_or(cur_valid, jnp.logical_and(prev_valid, has_prev))
                sc = jnp.where(valid, _dot_nt(q2, keys), NEG)
                mx = jnp.max(sc, axis=1, keepdims=True)
                p = jnp.exp(sc - mx)
                den = jnp.sum(p, axis=1, keepdims=True)
                o_s[bi, rows, :] = _unstack_heads(_dot_nn(p, vals) / den)
                l_s[bi, rows, :] = _unstack_heads(jnp.broadcast_to(mx + jnp.log(den), (q2.shape[0], LANES)))
                return carry

            lax.fori_loop(0, ATTN_TILE // BLOCK, blk, 0, unroll=2)
        ls = [l_s[bi] for bi in range(len(DILATIONS))]
        top = functools.reduce(jnp.maximum, ls)
        ws = [jnp.exp(l - top) for l in ls]
        den = functools.reduce(lambda a, b: a + b, ws)
        num = functools.reduce(lambda a, b: a + b, [w * o_s[bi] for bi, w in enumerate(ws)])
        o_ref[...] = num / den
        l_ref[...] = top + jnp.log(den)

    return pl.pallas_call(
        body, grid=(width // LANES, n_tiles), in_specs=[cur, cur, prev, vcur, vprev],
        out_specs=[cur, cur], out_shape=[jax.ShapeDtypeStruct((s, width), F32)] * 2,
        scratch_shapes=[pltpu.VMEM((2 * ATTN_TILE, LANES), F32), pltpu.VMEM((2 * ATTN_TILE, LANES), F32),
                        pltpu.VMEM((len(DILATIONS), ATTN_TILE, LANES), F32),
                        pltpu.VMEM((len(DILATIONS), ATTN_TILE, LANES), F32)],
        compiler_params=_params(2), name=name)(q, k, k, proj, proj)


def _attn_bwd_q(q, k, proj, v_block, do, o, lse, name):
    s, width = q.shape
    n_tiles = s // ATTN_TILE
    cur, prev, vcur, vprev = _attn_specs(width, v_block, lambda n: jnp.maximum(n - 1, 0))

    def body(q_ref, k_ref, kp_ref, v_ref, vp_ref, do_ref, o_ref, l_ref, dq_ref, kk, vv):
        n = pl.program_id(1)
        kk[0:ATTN_TILE, :] = kp_ref[...]
        kk[ATTN_TILE:, :] = k_ref[...]
        vv[0:ATTN_TILE, :] = vp_ref[...]
        vv[ATTN_TILE:, :] = v_ref[...]
        dq_ref[...] = jnp.zeros_like(dq_ref)
        cur_valid, prev_valid = _query_masks()
        for d in DILATIONS:
            span = BLOCK * d

            def blk(idx, carry, d=d, span=span):
                g = idx // d
                q0 = g * span + idx % d
                rows = pl.ds(q0, BLOCK, stride=d)
                dob = do_ref[rows, :]
                q2 = _stack_heads(q_ref[rows, :])
                do2 = _stack_heads(dob)
                delta = jnp.sum(_stack_heads(dob * o_ref[rows, :]), axis=1, keepdims=True)
                lse2 = _stacked_lse(l_ref[rows, :])
                keys = jnp.concatenate([kk[pl.ds(ATTN_TILE + q0 - span, BLOCK, stride=d), :],
                                        kk[pl.ds(ATTN_TILE + q0, BLOCK, stride=d), :]], axis=0)
                vals = jnp.concatenate([vv[pl.ds(ATTN_TILE + q0 - span, BLOCK, stride=d), :],
                                        vv[pl.ds(ATTN_TILE + q0, BLOCK, stride=d), :]], axis=0)
                has_prev = jnp.logical_or(n > 0, g > 0)
                valid = jnp.logical_or(cur_valid, jnp.logical_and(prev_valid, has_prev))
                p = jnp.where(valid, jnp.exp(_dot_nt(q2, keys) - lse2), 0.0)
                ds = p * (_dot_nt(do2, vals) - delta)
                dq_ref[rows, :] += _unstack_heads(_dot_nn(ds, keys))
                return carry

            lax.fori_loop(0, ATTN_TILE // BLOCK, blk, 0, unroll=2)

    return pl.pallas_call(
        body, grid=(width // LANES, n_tiles), in_specs=[cur, cur, prev, vcur, vprev, cur, cur, cur],
        out_specs=cur, out_shape=jax.ShapeDtypeStruct((s, width), F32),
        scratch_shapes=[pltpu.VMEM((2 * ATTN_TILE, LANES), F32), pltpu.VMEM((2 * ATTN_TILE, LANES), F32)],
        compiler_params=_params(2), name=name)(q, k, k, proj, proj, do, o, lse)


def _attn_bwd_kv(q, k, proj, v_block, do, o, lse, name):
    s, width = q.shape
    n_tiles = s // ATTN_TILE
    cur, nxt, vcur, _ = _attn_specs(width, v_block, lambda n: jnp.minimum(n + 1, n_tiles - 1))

    def body(k_ref, v_ref, q_ref, qn_ref, do_ref, don_ref, o_ref, on_ref, l_ref, ln_ref, dk_ref, dv_ref,
             qq, dd, pr, ll):
        n = pl.program_id(1)
        qq[0:ATTN_TILE, :] = q_ref[...]
        qq[ATTN_TILE:, :] = qn_ref[...]
        dd[0:ATTN_TILE, :] = do_ref[...]
        dd[ATTN_TILE:, :] = don_ref[...]
        pr[0:ATTN_TILE, :] = do_ref[...] * o_ref[...]
        pr[ATTN_TILE:, :] = don_ref[...] * on_ref[...]
        ll[0:ATTN_TILE, :] = l_ref[...]
        ll[ATTN_TILE:, :] = ln_ref[...]
        dk_ref[...] = jnp.zeros_like(dk_ref)
        dv_ref[...] = jnp.zeros_like(dv_ref)
        same_ok, before_ok = _band_masks(2 * HEADS_PER_TILE, 1)
        is_same = lax.broadcasted_iota(jnp.int32, same_ok.shape, 0) < HEADS_PER_TILE * BLOCK
        same_valid = jnp.logical_and(is_same, same_ok)
        after_valid = jnp.logical_and(jnp.logical_not(is_same), before_ok)
        for d in DILATIONS:
            span = BLOCK * d
            n_groups = ATTN_TILE // span

            def blk(idx, carry, d=d, span=span, n_groups=n_groups):
                g = idx // d
                k0 = g * span + idx % d
                rows = pl.ds(k0, BLOCK, stride=d)
                kb = k_ref[rows, :]
                vb = v_ref[rows, :]
                here, after = pl.ds(k0, BLOCK, stride=d), pl.ds(k0 + span, BLOCK, stride=d)
                q4 = jnp.concatenate([_stack_heads(qq[here, :]), _stack_heads(qq[after, :])], axis=0)
                do4 = jnp.concatenate([_stack_heads(dd[here, :]), _stack_heads(dd[after, :])], axis=0)
                delta = jnp.sum(jnp.concatenate([_stack_heads(pr[here, :]), _stack_heads(pr[after, :])], axis=0),
                                axis=1, keepdims=True)
                lse4 = jnp.concatenate([_stacked_lse(ll[here, :]), _stacked_lse(ll[after, :])], axis=0)
                has_next = jnp.logical_or(n < n_tiles - 1, g < n_groups - 1)
                valid = jnp.logical_or(same_valid, jnp.logical_and(after_valid, has_next))
                p = jnp.where(valid, jnp.exp(_dot_nt(q4, kb) - lse4), 0.0)
                ds = p * (_dot_nt(do4, vb) - delta)
                dv_ref[rows, :] += _dot_tn(p, do4)
                dk_ref[rows, :] += _dot_tn(ds, q4)
                return carry

            lax.fori_loop(0, ATTN_TILE // BLOCK, blk, 0, unroll=2)

    return pl.pallas_call(
        body, grid=(width // LANES, n_tiles), in_specs=[cur, vcur, cur, nxt, cur, nxt, cur, nxt, cur, nxt],
        out_specs=[cur, cur], out_shape=[jax.ShapeDtypeStruct((s, width), F32)] * 2,
        scratch_shapes=[pltpu.VMEM((2 * ATTN_TILE, LANES), F32)] * 4,
        compiler_params=_params(2), name=name)(k, proj, q, q, do, do, o, o, lse, lse)


def _conv_specs(s, a_block, b_block):
    per = CONV_CHUNK // CONV_HALO
    a_cur = pl.BlockSpec((CONV_CHUNK, LANES), lambda cb, i: (i, a_block + cb))
    b_cur = pl.BlockSpec((CONV_CHUNK, LANES), lambda cb, i: (i, b_block + cb))
    a_halo = pl.BlockSpec((CONV_HALO, LANES), lambda cb, i: (jnp.maximum(i * per - 1, 0), a_block + cb))
    b_halo = pl.BlockSpec((CONV_HALO, LANES), lambda cb, i: (jnp.maximum(i * per - 1, 0), b_block + cb))
    w_spec = pl.BlockSpec((CONV_KERNEL, LANES), lambda cb, i: (0, cb))
    vec = pl.BlockSpec((1, LANES), lambda cb, i: (0, cb))
    out = pl.BlockSpec((CONV_CHUNK, LANES), lambda cb, i: (i, cb))
    return a_cur, b_cur, a_halo, b_halo, w_spec, vec, out


def _fill_glu_window(win, a_ref, b_ref, ah_ref, bh_ref, first):
    halo = ah_ref[...] * _sigmoid(bh_ref[...])
    win[0:CONV_HALO, :] = jnp.where(first, 0.0, halo)
    win[CONV_HALO:, :] = a_ref[...] * _sigmoid(b_ref[...])


def _conv_fwd(proj, a_block, b_block, w, bias, name):
    s = proj.shape[0]
    cw = w.shape[1]
    a_cur, b_cur, a_halo, b_halo, w_spec, vec, out = _conv_specs(s, a_block, b_block)
    lead = CONV_HALO - (CONV_KERNEL - 1)

    def body(a_ref, b_ref, ah_ref, bh_ref, w_ref, bias_ref, o_ref, win):
        _fill_glu_window(win, a_ref, b_ref, ah_ref, bh_ref, pl.program_id(1) == 0)
        for sub in range(CONV_CHUNK // CONV_SUB):
            base = sub * CONV_SUB
            acc = jnp.zeros((CONV_SUB, LANES), F32) + bias_ref[...]
            for j in range(CONV_KERNEL):
                acc = acc + w_ref[j:j + 1, :] * win[base + lead + j:base + lead + j + CONV_SUB, :]
            o_ref[base:base + CONV_SUB, :] = acc

    return pl.pallas_call(
        body, grid=(cw // LANES, s // CONV_CHUNK), in_specs=[a_cur, b_cur, a_halo, b_halo, w_spec, vec],
        out_specs=out, out_shape=jax.ShapeDtypeStruct((s, cw), F32),
        scratch_shapes=[pltpu.VMEM((CONV_CHUNK + CONV_HALO, LANES), F32)],
        compiler_params=_params(2), name=name)(proj, proj, proj, proj, w, bias)


def _conv_bwd(proj, a_block, b_block, w, du1, name):
    s = proj.shape[0]
    cw = w.shape[1]
    a_cur, b_cur, a_halo, b_halo, w_spec, vec, out = _conv_specs(s, a_block, b_block)
    per = CONV_CHUNK // CONV_HALO
    n_chunks = s // CONV_CHUNK
    d_next = pl.BlockSpec((CONV_HALO, LANES), lambda cb, i: (jnp.minimum((i + 1) * per, s // CONV_HALO - 1), cb))
    lead = CONV_HALO - (CONV_KERNEL - 1)

    def body(a_ref, b_ref, ah_ref, bh_ref, w_ref, d_ref, dn_ref, da_ref, db_ref, dw_ref, dbias_ref, win, dwin):
        i = pl.program_id(1)
        _fill_glu_window(win, a_ref, b_ref, ah_ref, bh_ref, i == 0)
        dwin[0:CONV_CHUNK, :] = d_ref[...]
        dwin[CONV_CHUNK:, :] = jnp.where(i == n_chunks - 1, 0.0, dn_ref[...])

        @pl.when(i == 0)
        def _():
            dw_ref[...] = jnp.zeros_like(dw_ref)
            dbias_ref[...] = jnp.zeros_like(dbias_ref)

        dbias_ref[...] += _colsum(d_ref[...])
        for sub in range(CONV_CHUNK // CONV_SUB):
            base = sub * CONV_SUB
            dcur = dwin[base:base + CONV_SUB, :]
            du0 = jnp.zeros((CONV_SUB, LANES), F32)
            for j in range(CONV_KERNEL):
                back = CONV_KERNEL - 1 - j
                du0 = du0 + w_ref[j:j + 1, :] * dwin[base + back:base + back + CONV_SUB, :]
                dw_ref[j:j + 1, :] += _colsum(dcur * win[base + lead + j:base + lead + j + CONV_SUB, :])
            av = a_ref[base:base + CONV_SUB, :]
            sig = _sigmoid(b_ref[base:base + CONV_SUB, :])
            da_ref[base:base + CONV_SUB, :] = du0 * sig
            db_ref[base:base + CONV_SUB, :] = du0 * av * sig * (1.0 - sig)

    return pl.pallas_call(
        body, grid=(cw // LANES, n_chunks), in_specs=[a_cur, b_cur, a_halo, b_halo, w_spec, out, d_next],
        out_specs=[out, out, w_spec, vec],
        out_shape=[jax.ShapeDtypeStruct((s, cw), F32), jax.ShapeDtypeStruct((s, cw), F32),
                   jax.ShapeDtypeStruct((CONV_KERNEL, cw), F32), jax.ShapeDtypeStruct((1, cw), F32)],
        scratch_shapes=[pltpu.VMEM((CONV_CHUNK + CONV_HALO, LANES), F32)] * 2,
        compiler_params=_params(2), name=name)(proj, proj, proj, proj, w, du1, du1)


def _adamw_math(w, g, m, v):
    m = ADAM_B1 * m + (1.0 - ADAM_B1) * g
    v = ADAM_B2 * v + (1.0 - ADAM_B2) * (g * g)
    m_hat = m / (1.0 - ADAM_B1 ** ADAM_STEP)
    v_hat = v / (1.0 - ADAM_B2 ** ADAM_STEP)
    delta = -ADAM_LR * (m_hat / (jnp.sqrt(v_hat) + ADAM_EPS) + ADAM_WD * w)
    return delta, m, v


def _adamw_big(w, g, m, v, name):
    rows, cols = w.shape
    tile = _tile(rows, 256, 8)
    spec = pl.BlockSpec((tile, cols), lambda i: (i, 0))

    def body(w_ref, g_ref, m_ref, v_ref, d_out, m_out, v_out):
        d_out[...], m_out[...], v_out[...] = _adamw_math(w_ref[...], g_ref[...], m_ref[...], v_ref[...])

    return pl.pallas_call(body, grid=(rows // tile,), in_specs=[spec] * 4, out_specs=[spec] * 3,
                          out_shape=[jax.ShapeDtypeStruct(w.shape, F32)] * 3, compiler_params=_params(1),
                          name=name)(w, g, m, v)


def _adamw_small(ws, gs, ms, vs, name):
    n = len(ws)

    def body(*refs):
        ins, outs = refs[:4 * n], refs[4 * n:]
        for t in range(n):
            res = _adamw_math(ins[t][...], ins[n + t][...], ins[2 * n + t][...], ins[3 * n + t][...])
            for j in range(3):
                outs[j * n + t][...] = res[j]

    shapes = [jax.ShapeDtypeStruct(w.shape, F32) for w in ws]
    res = pl.pallas_call(body, out_shape=shapes * 3, compiler_params=pltpu.CompilerParams(vmem_limit_bytes=VMEM_LIMIT),
                         name=name)(*ws, *gs, *ms, *vs)
    return res[:n], res[n:2 * n], res[2 * n:]


def _sum_blocks(x, n_blocks, name):
    r = x.shape[0] // n_blocks

    def body(x_ref, o_ref):
        acc = x_ref[0:r, :]
        for b in range(1, n_blocks):
            acc = acc + x_ref[b * r:(b + 1) * r, :]
        o_ref[...] = acc

    return pl.pallas_call(body, out_shape=jax.ShapeDtypeStruct((r, x.shape[1]), F32),
                          compiler_params=pltpu.CompilerParams(vmem_limit_bytes=VMEM_LIMIT), name=name)(x)


def _coords():
    return lax.axis_index("x"), lax.axis_index("y"), lax.axis_index("c")


def _flip(v, bit):
    return 1 - v if bit else v


def _ag_small(x, name):
    r, c = x.shape

    def body(x_ref, o_ref, send, recv, local_sem):
        mx, my, mc = _coords()

        def rows(px, py, pc):
            return o_ref.at[pl.ds(pl.multiple_of((4 * px + 2 * py + pc) * r, 8), r), :]

        local = pltpu.make_async_copy(x_ref, rows(mx, my, mc), local_sem)
        local.start()
        peers = [(_flip(mx, k >> 2 & 1), _flip(my, k >> 1 & 1), _flip(mc, k & 1)) for k in range(1, N_DEV)]
        sends = [pltpu.make_async_remote_copy(x_ref, rows(mx, my, mc), send.at[k], recv.at[k], device_id=p,
                                              device_id_type=MESH) for k, p in enumerate(peers)]
        for cp in sends:
            cp.start()
        for k, p in enumerate(peers):
            pltpu.make_async_remote_copy(x_ref, rows(*p), send.at[k], recv.at[k], device_id=p,
                                         device_id_type=MESH).wait_recv()
        for cp in sends:
            cp.wait_send()
        local.wait()

    vm = pl.BlockSpec(memory_space=pltpu.VMEM)
    return pl.pallas_call(
        body, in_specs=[vm], out_specs=vm, out_shape=jax.ShapeDtypeStruct((N_DEV * r, c), x.dtype),
        scratch_shapes=[pltpu.SemaphoreType.DMA((N_DEV - 1,)), pltpu.SemaphoreType.DMA((N_DEV - 1,)),
                        pltpu.SemaphoreType.DMA(())],
        name=name)(x)


def _ag_weights(shards, name):
    n_t = len(shards)

    def body(*refs):
        x_refs, o_refs = refs[:n_t], refs[n_t:2 * n_t]
        send, recv, local_sem = refs[2 * n_t:]
        mx, my, mc = _coords()
        me, sibling = (mx, my, mc), (mx, my, 1 - mc)
        chips = [(1 - mx, my), (mx, 1 - my), (1 - mx, 1 - my)]

        def rows(t, px, py, pc):
            r = x_refs[t].shape[0]
            return o_refs[t].at[pl.ds(pl.multiple_of((4 * px + 2 * py + pc) * r, 8), r), :]

        def copy(t, k, block, to, src=None):
            return pltpu.make_async_remote_copy(
                src_ref=rows(t, *block) if src is None else src, dst_ref=rows(t, *block),
                send_sem=send.at[t, k], recv_sem=recv.at[t, k], device_id=to, device_id_type=MESH)

        locals_, started = [], []
        for t in range(n_t):
            lc = pltpu.make_async_copy(x_refs[t], rows(t, *me), local_sem.at[t])
            lc.start()
            locals_.append(lc)
            first = [copy(t, 0, me, sibling, src=x_refs[t])]
            first += [copy(t, 1 + j, me, (*chip, mc), src=x_refs[t]) for j, chip in enumerate(chips)]
            for cp in first:
                cp.start()
            started += first
        for j, chip in enumerate(chips):
            for t in range(n_t):
                copy(t, 1 + j, (*chip, mc), me).wait_recv()
                passed = copy(t, 4 + j, (*chip, mc), sibling)
                passed.start()
                started.append(passed)
        for t in range(n_t):
            copy(t, 0, sibling, me).wait_recv()
            for j, chip in enumerate(chips):
                copy(t, 4 + j, (*chip, 1 - mc), me).wait_recv()
        for cp in started:
            cp.wait_send()
        for lc in locals_:
            lc.wait()

    return pl.pallas_call(
        body, in_specs=[ANY] * n_t, out_specs=[ANY] * n_t,
        out_shape=[jax.ShapeDtypeStruct((N_DEV * x.shape[0], x.shape[1]), x.dtype) for x in shards],
        scratch_shapes=[pltpu.SemaphoreType.DMA((n_t, 7)), pltpu.SemaphoreType.DMA((n_t, 7)),
                        pltpu.SemaphoreType.DMA((n_t,))],
        name=name)(*shards)


def _rs_sibling(grads, name):
    n_t = len(grads)

    def body(*refs):
        g_refs, land = refs[:n_t], refs[n_t:2 * n_t]
        send, recv = refs[2 * n_t:]
        mx, my, mc = _coords()
        copies = [pltpu.make_async_remote_copy(g_refs[t].at[:, 1 - mc], land[t], send.at[t], recv.at[t],
                                               device_id=(mx, my, 1 - mc), device_id_type=MESH) for t in range(n_t)]
        for cp in copies:
            cp.start()
        for cp in copies:
            cp.wait()

    return pl.pallas_call(
        body, in_specs=[ANY] * n_t, out_specs=[ANY] * n_t,
        out_shape=[jax.ShapeDtypeStruct((N_CHIP,) + g.shape[2:], F32) for g in grads],
        scratch_shapes=[pltpu.SemaphoreType.DMA((n_t,)), pltpu.SemaphoreType.DMA((n_t,))],
        name=name)(*grads)


def _chip_partial(g4, land, name):
    _, _, r, c = g4.shape
    tr = _tile(r, 256, 16)

    def body(g_ref, l_ref, o_ref):
        o_ref[...] = (g_ref[...] + l_ref[...]).astype(o_ref.dtype)

    return pl.pallas_call(
        body, grid=(N_CHIP, r // tr),
        in_specs=[pl.BlockSpec((None, None, tr, c), lambda q, i: (q, lax.axis_index("c"), i, 0)),
                  pl.BlockSpec((None, tr, c), lambda q, i: (q, i, 0))],
        out_specs=pl.BlockSpec((None, tr, c), lambda q, i: (q, i, 0)),
        out_shape=jax.ShapeDtypeStruct((N_CHIP, r, c), BF16), compiler_params=_params(2), name=name)(g4, land)


def _rs_chips(parts, name):
    n_t = len(parts)

    def body(*refs):
        p_refs, land = refs[:n_t], refs[n_t:2 * n_t]
        send, recv, local_sem = refs[2 * n_t:]
        mx, my, mc = _coords()
        my_chip = 2 * mx + my
        flips = [(1, 0), (0, 1), (1, 1)]
        started = []
        for t in range(n_t):
            lc = pltpu.make_async_copy(p_refs[t].at[my_chip], land[t].at[my_chip], local_sem.at[t])
            lc.start()
            started.append(lc)
        sends = []
        for t in range(n_t):
            for k, (fx, fy) in enumerate(flips):
                px, py = _flip(mx, fx), _flip(my, fy)
                cp = pltpu.make_async_remote_copy(p_refs[t].at[2 * px + py], land[t].at[my_chip], send.at[t, k],
                                                  recv.at[t, k], device_id=(px, py, mc), device_id_type=MESH)
                cp.start()
                sends.append(cp)
        for t in range(n_t):
            for k, (fx, fy) in enumerate(flips):
                px, py = _flip(mx, fx), _flip(my, fy)
                pltpu.make_async_remote_copy(p_refs[t].at[my_chip], land[t].at[2 * px + py], send.at[t, k],
                                             recv.at[t, k], device_id=(px, py, mc), device_id_type=MESH).wait_recv()
        for cp in sends:
            cp.wait_send()
        for lc in started:
            lc.wait()

    return pl.pallas_call(
        body, in_specs=[ANY] * n_t, out_specs=[ANY] * n_t,
        out_shape=[jax.ShapeDtypeStruct(p.shape, p.dtype) for p in parts],
        scratch_shapes=[pltpu.SemaphoreType.DMA((n_t, 3)), pltpu.SemaphoreType.DMA((n_t, 3)),
                        pltpu.SemaphoreType.DMA((n_t,))],
        name=name)(*parts)


def _sum_chips(land, name):
    _, r, c = land.shape
    tr = _tile(r, 256, 16)

    def body(l_ref, o_ref):
        acc = l_ref[0].astype(F32)
        for q in range(1, N_CHIP):
            acc = acc + l_ref[q].astype(F32)
        o_ref[...] = acc

    return pl.pallas_call(
        body, grid=(r // tr,), in_specs=[pl.BlockSpec((N_CHIP, tr, c), lambda i: (0, i, 0))],
        out_specs=pl.BlockSpec((tr, c), lambda i: (i, 0)), out_shape=jax.ShapeDtypeStruct((r, c), F32),
        compiler_params=_params(1), name=name)(land)


def _rope_tables(s, width):
    pos = jnp.arange(s, dtype=F32)
    inv_freq = ROPE_THETA ** (-jnp.arange(0, HEAD_DIM, 2, dtype=F32) / HEAD_DIM)
    ang = pos[:, None] * inv_freq[None, :]
    cos, sin = jnp.cos(ang), jnp.sin(ang)
    heads = width // HEAD_DIM
    return jnp.tile(jnp.concatenate([cos, cos], axis=1), (1, heads)), jnp.tile(jnp.concatenate([-sin, sin], axis=1), (1, heads))


def _pad_rows(v, rows):
    return jnp.concatenate([v, jnp.zeros((rows - 1, v.shape[1]), v.dtype)], axis=0)


def kernel(x, c, w_ada, b_ada, ffn1_norm_g, ffn1_w_gate, ffn1_w_up, ffn1_w_down, mix_norm_g, w_in, conv_dw_w, conv_dw_b, conv_ln_g, conv_ln_b, attn_out_g, conv_out_g, w_out, ffn2_norm_g, ffn2_w_gate, ffn2_w_up, ffn2_w_down, final_norm_g, loss_target, m_w_ada, m_b_ada, m_ffn1_norm_g, m_ffn1_w_gate, m_ffn1_w_up, m_ffn1_w_down, m_mix_norm_g, m_w_in, m_conv_dw_w, m_conv_dw_b, m_conv_ln_g, m_conv_ln_b, m_attn_out_g, m_conv_out_g, m_w_out, m_ffn2_norm_g, m_ffn2_w_gate, m_ffn2_w_up, m_ffn2_w_down, m_final_norm_g, v_w_ada, v_b_ada, v_ffn1_norm_g, v_ffn1_w_gate, v_ffn1_w_up, v_ffn1_w_down, v_mix_norm_g, v_w_in, v_conv_dw_w, v_conv_dw_b, v_conv_ln_g, v_conv_ln_b, v_attn_out_g, v_conv_out_g, v_w_out, v_ffn2_norm_g, v_ffn2_w_gate, v_ffn2_w_up, v_ffn2_w_down, v_final_norm_g):
    mx, my, mc = _coords()
    me = 4 * mx + 2 * my + mc
    s, d = x.shape[1], x.shape[2]
    aw = d // 2
    x2, target = x[0], loss_target[0]
    n_mod = w_ada.shape[2] * N_DEV // d
    mod_cols = w_ada.shape[2]

    cw_shard = conv_dw_w.shape[3]
    n_taps = CONV_KERNEL * cw_shard
    first_len = -(-(d + n_taps) // LANES) * LANES
    first = jnp.concatenate([c, conv_dw_w[0, :, 0, :].reshape(1, n_taps), jnp.zeros((1, first_len - d - n_taps), F32)], axis=1)
    first_all = _ag_small(_pad_rows(first, 8), "ag_c_taps")[0::8]
    c_all = first_all[:, :d]
    conv_w = first_all[:, d:d + n_taps].reshape(N_DEV, CONV_KERNEL, cw_shard).transpose(1, 0, 2).reshape(CONV_KERNEL, aw)

    silu_c = _silu_rows(c_all, "silu_c")
    mod_part = _plain_mm([(silu_c, w_ada[0])], F32, False, mod_cols, "mod_mm")
    mod_all = _ag_small(mod_part, "ag_mod").reshape(N_DEV, N_DEV, mod_cols)
    mod = lax.dynamic_index_in_dim(mod_all, me, axis=1, keepdims=False).reshape(1, n_mod * d) + b_ada
    sh1, sc1, g1, sh2, sc2, g2, sh3, sc3, g3 = [mod[:, i * d:(i + 1) * d] for i in range(n_mod)]

    shards = [ffn1_w_gate[0].T, ffn1_w_up[0].T, ffn1_w_down[0], w_in[0].T, w_out[0],
              ffn2_w_gate[0].T, ffn2_w_up[0].T, ffn2_w_down[0]]
    wg1, wu1, wd1, win_t, wout, wg2, wu2, wd2 = _ag_weights([w.astype(BF16) for w in shards], "ag_weights")

    n1 = _norm_mod_fwd(x2, ffn1_norm_g, sc1, sh1, "norm1")
    a1, b1, hid1 = _ffn_up(n1, wg1, wu1, "ffn1_up")
    h1, f1 = _residual_mm(hid1, wd1, x2, g1, 0.5, "ffn1_down")
    n2 = _norm_mod_fwd(h1, mix_norm_g, sc2, sh2, "norm2")
    proj = _plain_mm([(n2, win_t)], F32, True, _tile(5 * aw, 1536, LANES), "proj")
    cos, sin_signed = _rope_tables(s, aw)
    q_rot, k_rot = _rope_fwd(proj, cos, sin_signed, aw, "rope")
    lanes_per = aw // LANES
    attn, lse = _attn_fwd(q_rot, k_rot, proj, 2 * lanes_per, "attn_fwd")
    u1 = _conv_fwd(proj, 3 * lanes_per, 4 * lanes_per, conv_w, conv_dw_b, "conv_fwd")
    y = _mix_post_fwd(attn, u1, attn_out_g, conv_ln_g, conv_ln_b, conv_out_g, "mix_post")
    h2, mix = _residual_mm(y, wout, h1, g2, 1.0, "mix_out")
    n3 = _norm_mod_fwd(h2, ffn2_norm_g, sc3, sh3, "norm3")
    a3, b3, hid3 = _ffn_up(n3, wg2, wu2, "ffn2_up")
    h3, f3 = _residual_mm(hid3, wd2, h2, g3, 0.5, "ffn2_down")

    dh3, err2, d_final_g = _final_loss(h3, target, final_norm_g.reshape(1, d), "final_loss")
    loss = lax.psum(0.5 * jnp.sum(err2) / d, ("x", "y", "c"))

    df3, dg3 = _gate_bwd(dh3, f3, g3, 0.5, "gate3_bwd")
    da3, db3 = _ffn_bwd_hidden(df3, wd2, a3, b3, "ffn2_hidden_bwd")
    dn3 = _plain_mm([(da3, wg2), (db3, wu2)], F32, False, d, "ffn2_dn", tm=256)
    dh2, dsh3, dsc3, dgn3 = _norm_mod_bwd(dn3, h2, dh3, ffn2_norm_g, sc3, "norm3_bwd")
    g_wg2, g_wu2, g_wd2 = _mm_tn(da3, n3, "ffn2_dwg"), _mm_tn(db3, n3, "ffn2_dwu"), _mm_tn(hid3, df3, "ffn2_dwd")

    dmix, dg2 = _gate_bwd(dh2, mix, g2, 1.0, "gate2_bwd")
    dy = _plain_mm([(dmix, wout)], F32, True, d, "mix_dy")
    g_wout = _mm_tn(y, dmix, "mix_dwout")
    dattn, du1, d_attn_g, d_conv_g, d_ln_g, d_ln_b = _mix_post_bwd(
        dy, attn, u1, attn_out_g, conv_ln_g, conv_ln_b, conv_out_g, "mix_post_bwd")
    dga, dgb, d_taps, d_conv_b = _conv_bwd(proj, 3 * lanes_per, 4 * lanes_per, conv_w, du1, "conv_bwd")
    dq = _attn_bwd_q(q_rot, k_rot, proj, 2 * lanes_per, dattn, attn, lse, "attn_bwd_q")
    dk, dv = _attn_bwd_kv(q_rot, k_rot, proj, 2 * lanes_per, dattn, attn, lse, "attn_bwd_kv")
    dproj = _dproj_assemble(dq, dk, dv, dga, dgb, cos, sin_signed, "dproj")
    dn2 = _plain_mm([(dproj, win_t)], F32, False, d, "mix_dn")
    g_win = _mm_tn(dproj, n2, "mix_dwin")
    dh1, dsh2, dsc2, dgn2 = _norm_mod_bwd(dn2, h1, dh2, mix_norm_g, sc2, "norm2_bwd")

    df1, dg1 = _gate_bwd(dh1, f1, g1, 0.5, "gate1_bwd")
    da1, db1 = _ffn_bwd_hidden(df1, wd1, a1, b1, "ffn1_hidden_bwd")
    dn1 = _plain_mm([(da1, wg1), (db1, wu1)], F32, False, d, "ffn1_dn", tm=256)
    dx, dsh1, dsc1, dgn1 = _norm_mod_bwd(dn1, x2, dh1, ffn1_norm_g, sc1, "norm1_bwd")
    g_wg1, g_wu1, g_wd1 = _mm_tn(da1, n1, "ffn1_dwg"), _mm_tn(db1, n1, "ffn1_dwu"), _mm_tn(hid1, df1, "ffn1_dwd")

    dmod = jnp.concatenate([dsh1, dsc1, dg1, dsh2, dsc2, dg2, dsh3, dsc3, dg3], axis=1)
    small = [dmod, dgn1, dgn2, dgn3, d_final_g, d_conv_b, d_ln_g, d_ln_b, d_attn_g, d_conv_g,
             d_taps.reshape(1, CONV_KERNEL * aw)]
    sizes = [v.shape[1] for v in small]
    total = sum(sizes)
    padded = -(-total // (8 * LANES)) * (8 * LANES)
    packed = jnp.concatenate(small + [jnp.zeros((1, padded - total), F32)], axis=1).reshape(8, padded // 8)
    gathered = _ag_small(packed, "ag_small_grads")
    summed = _sum_blocks(gathered, N_DEV, "sum_small_grads").reshape(1, padded)
    offs = [sum(sizes[:i]) for i in range(len(sizes))]
    (g_b_ada, g_gn1, g_gn2, g_gn3, g_final, g_conv_b, g_ln_g, g_ln_b, g_attn_g, g_conv_g, g_taps) = [
        summed[:, o:o + n] for o, n in zip(offs, sizes)]
    g_taps_shard = lax.dynamic_slice_in_dim(g_taps.reshape(CONV_KERNEL, aw), me * cw_shard, cw_shard, axis=1)
    dmod_all = gathered.reshape(N_DEV, padded)[:, :n_mod * d]
    dmod_cols = lax.dynamic_slice_in_dim(dmod_all, me * mod_cols, mod_cols, axis=1)
    g_w_ada = _mm_tn(silu_c, dmod_cols, "ada_dw")

    full = [g_wg1, g_wu1, g_wd1, g_win, g_wout, g_wg2, g_wu2, g_wd2]
    g4 = [g.reshape(N_CHIP, 2, g.shape[0] // N_DEV, g.shape[1]) for g in full]
    land = _rs_sibling(g4, "rs_sibling")
    parts = [_chip_partial(a, b, "chip_partial_%d" % t) for t, (a, b) in enumerate(zip(g4, land))]
    land2 = _rs_chips(parts, "rs_chips")
    sums = [_sum_chips(l, "sum_chips_%d" % t) for t, l in enumerate(land2)]
    s_wg1, s_wu1, s_wd1, s_win, s_wout, s_wg2, s_wu2, s_wd2 = sums

    grads = {
        "w_ada": g_w_ada, "b_ada": g_b_ada, "ffn1_norm_g": g_gn1, "ffn1_w_gate": s_wg1.T, "ffn1_w_up": s_wu1.T,
        "ffn1_w_down": s_wd1, "mix_norm_g": g_gn2, "w_in": s_win.T, "conv_dw_w": g_taps_shard, "conv_dw_b": g_conv_b,
        "conv_ln_g": g_ln_g, "conv_ln_b": g_ln_b, "attn_out_g": g_attn_g, "conv_out_g": g_conv_g, "w_out": s_wout,
        "ffn2_norm_g": g_gn3, "ffn2_w_gate": s_wg2.T, "ffn2_w_up": s_wu2.T, "ffn2_w_down": s_wd2,
        "final_norm_g": g_final,
    }
    weights = dict(w_ada=w_ada, b_ada=b_ada, ffn1_norm_g=ffn1_norm_g, ffn1_w_gate=ffn1_w_gate, ffn1_w_up=ffn1_w_up, ffn1_w_down=ffn1_w_down, mix_norm_g=mix_norm_g, w_in=w_in, conv_dw_w=conv_dw_w, conv_dw_b=conv_dw_b, conv_ln_g=conv_ln_g, conv_ln_b=conv_ln_b, attn_out_g=attn_out_g, conv_out_g=conv_out_g, w_out=w_out, ffn2_norm_g=ffn2_norm_g, ffn2_w_gate=ffn2_w_gate, ffn2_w_up=ffn2_w_up, ffn2_w_down=ffn2_w_down, final_norm_g=final_norm_g)
    moms = dict(w_ada=m_w_ada, b_ada=m_b_ada, ffn1_norm_g=m_ffn1_norm_g, ffn1_w_gate=m_ffn1_w_gate, ffn1_w_up=m_ffn1_w_up, ffn1_w_down=m_ffn1_w_down, mix_norm_g=m_mix_norm_g, w_in=m_w_in, conv_dw_w=m_conv_dw_w, conv_dw_b=m_conv_dw_b, conv_ln_g=m_conv_ln_g, conv_ln_b=m_conv_ln_b, attn_out_g=m_attn_out_g, conv_out_g=m_conv_out_g, w_out=m_w_out, ffn2_norm_g=m_ffn2_norm_g, ffn2_w_gate=m_ffn2_w_gate, ffn2_w_up=m_ffn2_w_up, ffn2_w_down=m_ffn2_w_down, final_norm_g=m_final_norm_g)
    vars_ = dict(w_ada=v_w_ada, b_ada=v_b_ada, ffn1_norm_g=v_ffn1_norm_g, ffn1_w_gate=v_ffn1_w_gate, ffn1_w_up=v_ffn1_w_up, ffn1_w_down=v_ffn1_w_down, mix_norm_g=v_mix_norm_g, w_in=v_w_in, conv_dw_w=v_conv_dw_w, conv_dw_b=v_conv_dw_b, conv_ln_g=v_conv_ln_g, conv_ln_b=v_conv_ln_b, attn_out_g=v_attn_out_g, conv_out_g=v_conv_out_g, w_out=v_w_out, ffn2_norm_g=v_ffn2_norm_g, ffn2_w_gate=v_ffn2_w_gate, ffn2_w_up=v_ffn2_w_up, ffn2_w_down=v_ffn2_w_down, final_norm_g=v_final_norm_g)
    names = list(weights)
    big = ["w_ada", "ffn1_w_gate", "ffn1_w_up", "ffn1_w_down", "w_in", "w_out", "ffn2_w_gate", "ffn2_w_up",
           "ffn2_w_down"]
    shape2 = {n: (weights[n].shape[-2] if weights[n].ndim > 1 else 1, weights[n].shape[-1]) for n in names}
    shape2["conv_dw_w"] = (CONV_KERNEL, cw_shard)
    g_out, d_out, m_out, v_out = {}, {}, {}, {}
    for n in big:
        g2d = grads[n].reshape(shape2[n])
        res = _adamw_big(weights[n].reshape(shape2[n]), g2d, moms[n].reshape(shape2[n]), vars_[n].reshape(shape2[n]),
                         "adamw_" + n)
        g_out[n], (d_out[n], m_out[n], v_out[n]) = g2d, res
    rest = [n for n in names if n not in big]
    res = _adamw_small([weights[n].reshape(shape2[n]) for n in rest], [grads[n].reshape(shape2[n]) for n in rest],
                       [moms[n].reshape(shape2[n]) for n in rest], [vars_[n].reshape(shape2[n]) for n in rest],
                       "adamw_small")
    for i, n in enumerate(rest):
        g_out[n], d_out[n], m_out[n], v_out[n] = grads[n], res[0][i], res[1][i], res[2][i]

    def shaped(table):
        return [table[n].reshape(weights[n].shape) for n in names]

    return (loss, dx.reshape(x.shape), *shaped(g_out), *shaped(d_out), *shaped(m_out), *shaped(v_out))
```

```python
import functools

import jax
import jax.numpy as jnp
from jax import lax
from jax.experimental import pallas as pl
from jax.experimental.pallas import tpu as pltpu

F32 = jnp.float32
BF16 = jnp.bfloat16
MESH = pl.DeviceIdType.MESH
ANY = pl.BlockSpec(memory_space=pl.ANY)

N_DEV = 8
N_CHIP = 4
HEAD_DIM = 64
HALF_HEAD = HEAD_DIM // 2
LANES = 128
BLOCK = 128
DILATIONS = (1, 4, 16)
ATTN_TILE = BLOCK * max(DILATIONS)
ROPE_THETA = 10000.0
CONV_KERNEL = 31
CONV_HALO = 32
CONV_CHUNK = 512
CONV_SUB = 128
RMS_EPS = 1e-6
LN_EPS = 1e-5
ADAM_LR = 0.001
ADAM_B1 = 0.9
ADAM_B2 = 0.999
ADAM_EPS = 1e-08
ADAM_WD = 0.01
ADAM_STEP = 10
VMEM_LIMIT = 56 * 1024 * 1024
NEG = -1e30


def _params(n_axes):
    return pltpu.CompilerParams(dimension_semantics=("arbitrary",) * n_axes, vmem_limit_bytes=VMEM_LIMIT)


def _tile(n, target, unit):
    best = None
    for t in range(unit, min(n, target) + 1, unit):
        if n % t == 0:
            best = t
    return best if best is not None else n


def _sigmoid(x):
    return 1.0 / (1.0 + jnp.exp(-x))


def _call(body, *, grid, in_specs, out_specs, out_shape, args, name, scratch_shapes=(), comm=None):
    params = _params(len(grid))
    if comm is None:
        return pl.pallas_call(body, grid=grid, in_specs=list(in_specs), out_specs=list(out_specs),
                              out_shape=list(out_shape), scratch_shapes=list(scratch_shapes),
                              compiler_params=params, name=name)(*args)
    n_in, n_out, n_scr = len(args), len(out_shape), len(scratch_shapes)
    c_in, c_out = len(comm.inputs), len(comm.out_shapes)
    steps = 1
    for g in grid:
        steps *= g

    def hosted(*refs):
        pos = 0
        parts = []
        for size in (n_in, c_in, n_out, c_out, n_scr, len(comm.scratch)):
            parts.append(refs[pos:pos + size])
            pos += size
        ins, cin, outs, cout, scr, cscr = parts
        step = 0
        for axis, g in enumerate(grid):
            step = step * g + pl.program_id(axis)

        @pl.when(step == 0)
        def _():
            comm.start(cin, cout, cscr)

        body(*ins, *outs, *scr)
        if comm.mid is not None and steps >= 4:
            @pl.when(step == (3 * steps) // 4)
            def _():
                comm.mid(cin, cout, cscr)

        @pl.when(step == steps - 1)
        def _():
            if comm.mid is not None and steps < 4:
                comm.mid(cin, cout, cscr)
            comm.finish(cin, cout, cscr)

    res = pl.pallas_call(
        hosted, grid=grid, in_specs=list(in_specs) + [ANY] * c_in, out_specs=list(out_specs) + [ANY] * c_out,
        out_shape=list(out_shape) + list(comm.out_shapes), scratch_shapes=list(scratch_shapes) + list(comm.scratch),
        compiler_params=params, name=name)(*args, *comm.inputs)
    return res[:n_out], res[n_out:]


def _rows(fn, rows_in, vecs_in, rows_out, vecs_out, *, tile, name, comm=None):
    norm = [r if isinstance(r, tuple) else (r, r.shape[1], 0) for r in rows_in]
    n_rows = norm[0][0].shape[0]
    n_tiles = n_rows // tile
    in_specs, args = [], []
    for arr, width, cb in norm:
        in_specs.append(pl.BlockSpec((tile, width), functools.partial(lambda i, cb: (i, cb), cb=cb)))
        args.append(arr)
    for v in vecs_in:
        in_specs.append(pl.BlockSpec((1, v.shape[1]), lambda i: (0, 0)))
        args.append(v)
    out_shape = [jax.ShapeDtypeStruct((n_rows, w), dt) for w, dt in rows_out]
    out_shape += [jax.ShapeDtypeStruct((1, w), F32) for w in vecs_out]
    out_specs = [pl.BlockSpec((tile, w), lambda i: (i, 0)) for w, _ in rows_out]
    out_specs += [pl.BlockSpec((1, w), lambda i: (0, 0)) for w in vecs_out]
    n_in, n_ro = len(args), len(rows_out)

    def body(*refs):
        vals = [r[...] for r in refs[:n_in]]
        outs = refs[n_in:]
        row_vals, vec_vals = fn(*vals)
        for ref, val in zip(outs[:n_ro], row_vals):
            if isinstance(val, tuple):
                w = val[0].shape[1]
                for j, piece in enumerate(val):
                    ref[:, j * w:(j + 1) * w] = piece.astype(ref.dtype)
            else:
                ref[...] = val.astype(ref.dtype)
        if vecs_out:
            @pl.when(pl.program_id(0) == 0)
            def _():
                for ref in outs[n_ro:]:
                    ref[...] = jnp.zeros_like(ref)
            for ref, val in zip(outs[n_ro:], vec_vals):
                ref[...] += val

    return _call(body, grid=(n_tiles,), in_specs=in_specs, out_specs=out_specs, out_shape=out_shape, args=args,
                 name=name, comm=comm)


def _colsum(x):
    return jnp.sum(x, axis=0, keepdims=True)


def _rms_stats(h):
    r = lax.rsqrt(jnp.mean(h * h, axis=-1, keepdims=True) + RMS_EPS)
    return r, h * r


def _rms_back(r, xn, dxn):
    return r * (dxn - xn * jnp.mean(dxn * xn, axis=-1, keepdims=True))


def _norm_mod_fwd(h, gain, scale, shift, name):
    def fn(h, gain, scale, shift):
        _, xn = _rms_stats(h)
        return [(xn * gain) * (1.0 + scale) + shift], []
    return _rows(fn, [h], [gain, scale, shift], [(h.shape[1], BF16)], [], tile=512, name=name)[0]


def _norm_mod_bwd(dn, h, dh_in, gain, scale, name, comm=None):
    def fn(dn, h, dh_in, gain, scale):
        r, xn = _rms_stats(h)
        y = xn * gain
        dy = dn * (1.0 + scale)
        dh = dh_in + _rms_back(r, xn, dy * gain)
        return [dh], [_colsum(dn), _colsum(dn * y), _colsum(dy * xn)]
    d = h.shape[1]
    return _rows(fn, [dn, h, dh_in], [gain, scale], [(d, F32)], [d, d, d], tile=256, name=name, comm=comm)


def _final_loss(h, target, gain, name):
    d = h.shape[1]

    def fn(h, target, gain):
        r, xn = _rms_stats(h)
        err = xn * gain - target
        dout = err * (1.0 / d)
        dh = _rms_back(r, xn, dout * gain)
        return [dh], [_colsum(err * err), _colsum(dout * xn)]
    return _rows(fn, [h, target], [gain], [(d, F32)], [d, d], tile=256, name=name)


def _gate_bwd(dh, f, gate, coef, name):
    def fn(dh, f, gate):
        return [(coef * gate) * dh], [coef * _colsum(f.astype(F32) * dh)]
    d = dh.shape[1]
    return _rows(fn, [dh, f], [gate], [(d, BF16)], [d], tile=512, name=name)


def _partner(x):
    width = x.shape[1]
    lane = lax.broadcasted_iota(jnp.int32, x.shape, 1) % HEAD_DIM
    return jnp.where(lane < HALF_HEAD, pltpu.roll(x, width - HALF_HEAD, 1), pltpu.roll(x, HALF_HEAD, 1))


def _rope_fwd(proj, cos, sin_signed, width, name):
    qscale = HEAD_DIM ** -0.5

    def fn(q, k, cos, sin):
        qr = q * cos + _partner(q) * sin
        kr = k * cos + _partner(k) * sin
        return [qr * qscale, kr], []
    return _rows(fn, [(proj, width, 0), (proj, width, 1), cos, sin_signed], [], [(width, F32), (width, F32)], [],
                 tile=512, name=name)


def _dproj_assemble(dq, dk, dv, dga, dgb, cos, sin_signed, name):
    width = dq.shape[1]
    qscale = HEAD_DIM ** -0.5

    def fn(dq, dk, dv, dga, dgb, cos, sin):
        dq0 = (dq * cos - _partner(dq) * sin) * qscale
        dk0 = dk * cos - _partner(dk) * sin
        return [(dq0, dk0, dv, dga, dgb)], []
    return _rows(fn, [dq, dk, dv, dga, dgb, cos, sin_signed], [], [(5 * width, BF16)], [], tile=256, name=name)[0]


def _mix_post_fwd(attn, u1, attn_g, ln_g, ln_b, conv_g, name):
    def fn(attn, u1, attn_g, ln_g, ln_b, conv_g):
        _, xa = _rms_stats(attn)
        mu = jnp.mean(u1, axis=-1, keepdims=True)
        xc = u1 - mu
        rstd = lax.rsqrt(jnp.mean(xc * xc, axis=-1, keepdims=True) + LN_EPS)
        u2 = (xc * rstd) * ln_g + ln_b
        u3 = u2 * _sigmoid(u2)
        _, x3 = _rms_stats(u3)
        return [(xa * attn_g, x3 * conv_g)], []
    w = attn.shape[1]
    return _rows(fn, [attn, u1], [attn_g, ln_g, ln_b, conv_g], [(2 * w, BF16)], [], tile=512, name=name)[0]


def _mix_post_bwd(dy, attn, u1, attn_g, ln_g, ln_b, conv_g, name):
    w = attn.shape[1]

    def fn(dya, dyc, attn, u1, attn_g, ln_g, ln_b, conv_g):
        ra, xa = _rms_stats(attn)
        dattn = _rms_back(ra, xa, dya * attn_g)
        mu = jnp.mean(u1, axis=-1, keepdims=True)
        xc = u1 - mu
        rstd = lax.rsqrt(jnp.mean(xc * xc, axis=-1, keepdims=True) + LN_EPS)
        xh = xc * rstd
        u2 = xh * ln_g + ln_b
        sig = _sigmoid(u2)
        u3 = u2 * sig
        r3, x3 = _rms_stats(u3)
        du3 = _rms_back(r3, x3, dyc * conv_g)
        du2 = du3 * (sig + u3 * (1.0 - sig))
        dxh = du2 * ln_g
        du1 = rstd * (dxh - jnp.mean(dxh, axis=-1, keepdims=True) - xh * jnp.mean(dxh * xh, axis=-1, keepdims=True))
        return [dattn, du1], [_colsum(dya * xa), _colsum(dyc * x3), _colsum(du2 * xh), _colsum(du2)]
    return _rows(fn, [(dy, w, 0), (dy, w, 1), attn, u1], [attn_g, ln_g, ln_b, conv_g], [(w, F32), (w, F32)],
                 [w, w, w, w], tile=256, name=name)


def _silu_rows(c_all, name):
    def fn(c):
        return [c * _sigmoid(c)], []
    return _rows(fn, [c_all], [], [(c_all.shape[1], BF16)], [], tile=c_all.shape[0], name=name)[0]


def _mm(groups, epi, extras, vecs, outs, *, trans_rhs, tm, tn, name, comm=None):
    m = groups[0][0][0].shape[0]
    n = groups[0][0][1].shape[0] if trans_rhs else groups[0][0][1].shape[1]
    tm, tn = min(tm, m), min(tn, n)
    in_specs, args = [], []
    for grp in groups:
        for lhs, rhs in grp:
            k = lhs.shape[1]
            in_specs.append(pl.BlockSpec((tm, k), lambda j, i: (i, 0)))
            in_specs.append(pl.BlockSpec((tn, k), lambda j, i: (j, 0)) if trans_rhs
                            else pl.BlockSpec((k, tn), lambda j, i: (0, j)))
            args += [lhs, rhs]
    for e in extras:
        in_specs.append(pl.BlockSpec((tm, tn), lambda j, i: (i, j)))
        args.append(e)
    for v in vecs:
        in_specs.append(pl.BlockSpec((1, tn), lambda j, i: (0, j)))
        args.append(v)
    sizes = [len(g) for g in groups]
    n_mm, n_ex, n_vec = 2 * sum(sizes), len(extras), len(vecs)
    dims = (((1,), (1,)), ((), ())) if trans_rhs else (((1,), (0,)), ((), ()))

    def body(*refs):
        accs, pos = [], 0
        for size in sizes:
            acc = None
            for _ in range(size):
                part = lax.dot_general(refs[pos][...].astype(BF16), refs[pos + 1][...].astype(BF16), dims,
                                       preferred_element_type=F32)
                acc = part if acc is None else acc + part
                pos += 2
            accs.append(acc)
        ex = [r[...] for r in refs[n_mm:n_mm + n_ex]]
        vc = [r[...] for r in refs[n_mm + n_ex:n_mm + n_ex + n_vec]]
        for ref, val in zip(refs[n_mm + n_ex + n_vec:], epi(accs, ex, vc)):
            ref[...] = val.astype(ref.dtype)

    return _call(body, grid=(n // tn, m // tm), in_specs=in_specs,
                 out_specs=[pl.BlockSpec((tm, tn), lambda j, i: (i, j)) for _ in outs],
                 out_shape=[jax.ShapeDtypeStruct((m, n), dt) for dt in outs], args=args, name=name, comm=comm)


def _mm_tn(lhs, rhs, name, comm=None):
    t, a = lhs.shape
    b = rhs.shape[1]
    ta = a if a <= 1536 else _tile(a, 1536, LANES)
    tk = _tile(t, 512, 8)

    def body(l_ref, r_ref, o_ref):
        @pl.when(pl.program_id(1) == 0)
        def _():
            o_ref[...] = jnp.zeros_like(o_ref)
        o_ref[...] += lax.dot_general(l_ref[...].astype(BF16), r_ref[...].astype(BF16), (((0,), (0,)), ((), ())),
                                      preferred_element_type=F32)

    res = _call(body, grid=(a // ta, t // tk),
                in_specs=[pl.BlockSpec((tk, ta), lambda i, k: (k, i)), pl.BlockSpec((tk, b), lambda i, k: (k, 0))],
                out_specs=[pl.BlockSpec((ta, b), lambda i, k: (i, 0))], out_shape=[jax.ShapeDtypeStruct((a, b), F32)],
                args=(lhs, rhs), name=name, comm=comm)
    return res[0] if comm is None else (res[0][0], res[1])


def _ffn_tn(f):
    return _tile(f, 1536, LANES)


def _ffn_up(n, wg_t, wu_t, name, comm=None):
    def epi(accs, ex, vc):
        a, b = accs
        return [a, b, (a * _sigmoid(a)) * b]
    return _mm([[(n, wg_t)], [(n, wu_t)]], epi, [], [], [BF16, BF16, BF16], trans_rhs=True, tm=256,
               tn=_ffn_tn(wg_t.shape[0]), name=name, comm=comm)


def _residual_mm(lhs, w, res, gate, coef, name):
    def epi(accs, ex, vc):
        return [ex[0] + (coef * vc[0]) * accs[0], accs[0]]
    return _mm([[(lhs, w)]], epi, [res], [gate], [F32, BF16], trans_rhs=False, tm=512, tn=w.shape[1], name=name)


def _ffn_bwd_hidden(df, wd, a, b, name, comm=None):
    def epi(accs, ex, vc):
        dh = accs[0]
        av, bv = ex[0].astype(F32), ex[1].astype(F32)
        sig = _sigmoid(av)
        silu = av * sig
        return [dh * bv * (sig + silu * (1.0 - sig)), dh * silu]
    return _mm([[(df, wd)]], epi, [a, b], [], [BF16, BF16], trans_rhs=True, tm=256, tn=_ffn_tn(wd.shape[0]),
               name=name, comm=comm)


def _plain_mm(pairs, out_dtype, trans_rhs, tn, name, tm=512, comm=None):
    def epi(accs, ex, vc):
        return [accs[0]]
    res = _mm([pairs], epi, [], [], [out_dtype], trans_rhs=trans_rhs, tm=tm, tn=tn, name=name, comm=comm)
    return res[0] if comm is None else (res[0][0], res[1])


HEADS_PER_TILE = LANES // HEAD_DIM


def _stack_heads(x):
    lane = lax.broadcasted_iota(jnp.int32, (1, LANES), 1)
    return jnp.concatenate([x * (lane // HEAD_DIM == h).astype(F32) for h in range(HEADS_PER_TILE)], axis=0)


def _unstack_heads(y):
    r = y.shape[0] // HEADS_PER_TILE
    lane = lax.broadcasted_iota(jnp.int32, (r, y.shape[1]), 1)
    out = y[0:r]
    for h in range(1, HEADS_PER_TILE):
        out = jnp.where(lane // HEAD_DIM == h, y[h * r:(h + 1) * r], out)
    return out


def _stacked_lse(lb):
    return jnp.concatenate([_lane_pick(lb, h) for h in range(HEADS_PER_TILE)], axis=0)


def _band_masks(n_row_blocks, n_col_blocks):
    shape = (n_row_blocks * BLOCK, n_col_blocks * BLOCK)
    qi = lax.broadcasted_iota(jnp.int32, shape, 0) % BLOCK
    kj = lax.broadcasted_iota(jnp.int32, shape, 1) % BLOCK
    return kj <= qi, kj >= qi


def _query_masks():
    same_ok, before_ok = _band_masks(HEADS_PER_TILE, 2)
    is_cur = lax.broadcasted_iota(jnp.int32, same_ok.shape, 1) >= BLOCK
    return jnp.logical_and(is_cur, same_ok), jnp.logical_and(jnp.logical_not(is_cur), before_ok)


def _dot_nt(a, b):
    return lax.dot_general(a.astype(BF16), b.astype(BF16), (((1,), (1,)), ((), ())), preferred_element_type=F32)


def _dot_nn(a, b):
    return lax.dot_general(a.astype(BF16), b.astype(BF16), (((1,), (0,)), ((), ())), preferred_element_type=F32)


def _dot_tn(a, b):
    return lax.dot_general(a.astype(BF16), b.astype(BF16), (((0,), (0,)), ((), ())), preferred_element_type=F32)


def _lane_pick(x, h):
    lane = lax.broadcasted_iota(jnp.int32, x.shape, 1)
    return jnp.sum(jnp.where(lane == h * HEAD_DIM, x, 0.0), axis=1, keepdims=True)


def _attn_specs(width, v_block, n_halo_of):
    cur = pl.BlockSpec((ATTN_TILE, LANES), lambda hb, n: (n, hb))
    nbr = pl.BlockSpec((ATTN_TILE, LANES), lambda hb, n: (n_halo_of(n), hb))
    vcur = pl.BlockSpec((ATTN_TILE, LANES), lambda hb, n: (n, v_block + hb))
    vnbr = pl.BlockSpec((ATTN_TILE, LANES), lambda hb, n: (n_halo_of(n), v_block + hb))
    return cur, nbr, vcur, vnbr


def _attn_fwd(q, k, proj, v_block, name, comm=None):
    s, width = q.shape
    n_tiles = s // ATTN_TILE
    cur, prev, vcur, vprev = _attn_specs(width, v_block, lambda n: jnp.maximum(n - 1, 0))

    def body(q_ref, k_ref, kp_ref, v_ref, vp_ref, o_ref, l_ref, kk, vv, o_s, l_s):
        n = pl.program_id(1)
        kk[0:ATTN_TILE, :] = kp_ref[...]
        kk[ATTN_TILE:, :] = k_ref[...]
        vv[0:ATTN_TILE, :] = vp_ref[...]
        vv[ATTN_TILE:, :] = v_ref[...]
        cur_valid, prev_valid = _query_masks()
        for bi, d in enumerate(DILATIONS):
            span = BLOCK * d

            def blk(idx, carry, bi=bi, d=d, span=span):
                g = idx // d
                q0 = g * span + idx % d
                rows = pl.ds(q0, BLOCK, stride=d)
                q2 = _stack_heads(q_ref[rows, :])
                keys = jnp.concatenate([kk[pl.ds(ATTN_TILE + q0 - span, BLOCK, stride=d), :],
                                        kk[pl.ds(ATTN_TILE + q0, BLOCK, stride=d), :]], axis=0)
                vals = jnp.concatenate([vv[pl.ds(ATTN_TILE + q0 - span, BLOCK, stride=d), :],
                                        vv[pl.ds(ATTN_TILE + q0, BLOCK, stride=d), :]], axis=0)
                has_prev = jnp.logical_or(n > 0, g > 0)
                valid = jnp.logical_or(cur_valid, jnp.logical_and(prev_valid, has_prev))
                sc = jnp.where(valid, _dot_nt(q2, keys), NEG)
                mx = jnp.max(sc, axis=1, keepdims=True)
                p = jnp.exp(sc - mx)
                den = jnp.sum(p, axis=1, keepdims=True)
                o_s[bi, rows, :] = _unstack_heads(_dot_nn(p, vals) / den)
                l_s[bi, rows, :] = _unstack_heads(jnp.broadcast_to(mx + jnp.log(den), (q2.shape[0], LANES)))
                return carry

            lax.fori_loop(0, ATTN_TILE // BLOCK, blk, 0, unroll=2)
        ls = [l_s[bi] for bi in range(len(DILATIONS))]
        top = functools.reduce(jnp.maximum, ls)
        ws = [jnp.exp(l - top) for l in ls]
        den = functools.reduce(lambda a, b: a + b, ws)
        num = functools.reduce(lambda a, b: a + b, [w * o_s[bi] for bi, w in enumerate(ws)])
        o_ref[...] = num / den
        l_ref[...] = top + jnp.log(den)

    return _call(
        body, grid=(width // LANES, n_tiles), in_specs=[cur, cur, prev, vcur, vprev],
        out_specs=[cur, cur], out_shape=[jax.ShapeDtypeStruct((s, width), F32)] * 2,
        scratch_shapes=[pltpu.VMEM((2 * ATTN_TILE, LANES), F32), pltpu.VMEM((2 * ATTN_TILE, LANES), F32),
                        pltpu.VMEM((len(DILATIONS), ATTN_TILE, LANES), F32),
                        pltpu.VMEM((len(DILATIONS), ATTN_TILE, LANES), F32)],
        args=(q, k, k, proj, proj), name=name, comm=comm)


def _attn_bwd_q(q, k, proj, v_block, do, o, lse, name, comm=None):
    s, width = q.shape
    n_tiles = s // ATTN_TILE
    cur, prev, vcur, vprev = _attn_specs(width, v_block, lambda n: jnp.maximum(n - 1, 0))

    def body(q_ref, k_ref, kp_ref, v_ref, vp_ref, do_ref, o_ref, l_ref, dq_ref, kk, vv):
        n = pl.program_id(1)
        kk[0:ATTN_TILE, :] = kp_ref[...]
        kk[ATTN_TILE:, :] = k_ref[...]
        vv[0:ATTN_TILE, :] = vp_ref[...]
        vv[ATTN_TILE:, :] = v_ref[...]
        dq_ref[...] = jnp.zeros_like(dq_ref)
        cur_valid, prev_valid = _query_masks()
        for d in DILATIONS:
            span = BLOCK * d

            def blk(idx, carry, d=d, span=span):
                g = idx // d
                q0 = g * span + idx % d
                rows = pl.ds(q0, BLOCK, stride=d)
                dob = do_ref[rows, :]
                q2 = _stack_heads(q_ref[rows, :])
                do2 = _stack_heads(dob)
                delta = jnp.sum(_stack_heads(dob * o_ref[rows, :]), axis=1, keepdims=True)
                lse2 = _stacked_lse(l_ref[rows, :])
                keys = jnp.concatenate([kk[pl.ds(ATTN_TILE + q0 - span, BLOCK, stride=d), :],
                                        kk[pl.ds(ATTN_TILE + q0, BLOCK, stride=d), :]], axis=0)
                vals = jnp.concatenate([vv[pl.ds(ATTN_TILE + q0 - span, BLOCK, stride=d), :],
                                        vv[pl.ds(ATTN_TILE + q0, BLOCK, stride=d), :]], axis=0)
                has_prev = jnp.logical_or(n > 0, g > 0)
                valid = jnp.logical_or(cur_valid, jnp.logical_and(prev_valid, has_prev))
                p = jnp.where(valid, jnp.exp(_dot_nt(q2, keys) - lse2), 0.0)
                ds = p * (_dot_nt(do2, vals) - delta)
                dq_ref[rows, :] += _unstack_heads(_dot_nn(ds, keys))
                return carry

            lax.fori_loop(0, ATTN_TILE // BLOCK, blk, 0, unroll=2)

    return _call(
        body, grid=(width // LANES, n_tiles), in_specs=[cur, cur, prev, vcur, vprev, cur, cur, cur],
        out_specs=[cur], out_shape=[jax.ShapeDtypeStruct((s, width), F32)],
        scratch_shapes=[pltpu.VMEM((2 * ATTN_TILE, LANES), F32), pltpu.VMEM((2 * ATTN_TILE, LANES), F32)],
        args=(q, k, k, proj, proj, do, o, lse), name=name, comm=comm)


def _attn_bwd_kv(q, k, proj, v_block, do, o, lse, name):
    s, width = q.shape
    n_tiles = s // ATTN_TILE
    cur, nxt, vcur, _ = _attn_specs(width, v_block, lambda n: jnp.minimum(n + 1, n_tiles - 1))

    def body(k_ref, v_ref, q_ref, qn_ref, do_ref, don_ref, o_ref, on_ref, l_ref, ln_ref, dk_ref, dv_ref,
             qq, dd, pr, ll):
        n = pl.program_id(1)
        qq[0:ATTN_TILE, :] = q_ref[...]
        qq[ATTN_TILE:, :] = qn_ref[...]
        dd[0:ATTN_TILE, :] = do_ref[...]
        dd[ATTN_TILE:, :] = don_ref[...]
        pr[0:ATTN_TILE, :] = do_ref[...] * o_ref[...]
        pr[ATTN_TILE:, :] = don_ref[...] * on_ref[...]
        ll[0:ATTN_TILE, :] = l_ref[...]
        ll[ATTN_TILE:, :] = ln_ref[...]
        dk_ref[...] = jnp.zeros_like(dk_ref)
        dv_ref[...] = jnp.zeros_like(dv_ref)
        same_ok, before_ok = _band_masks(2 * HEADS_PER_TILE, 1)
        is_same = lax.broadcasted_iota(jnp.int32, same_ok.shape, 0) < HEADS_PER_TILE * BLOCK
        same_valid = jnp.logical_and(is_same, same_ok)
        after_valid = jnp.logical_and(jnp.logical_not(is_same), before_ok)
        for d in DILATIONS:
            span = BLOCK * d
            n_groups = ATTN_TILE // span

            def blk(idx, carry, d=d, span=span, n_groups=n_groups):
                g = idx // d
                k0 = g * span + idx % d
                rows = pl.ds(k0, BLOCK, stride=d)
                kb = k_ref[rows, :]
                vb = v_ref[rows, :]
                here, after = pl.ds(k0, BLOCK, stride=d), pl.ds(k0 + span, BLOCK, stride=d)
                q4 = jnp.concatenate([_stack_heads(qq[here, :]), _stack_heads(qq[after, :])], axis=0)
                do4 = jnp.concatenate([_stack_heads(dd[here, :]), _stack_heads(dd[after, :])], axis=0)
                delta = jnp.sum(jnp.concatenate([_stack_heads(pr[here, :]), _stack_heads(pr[after, :])], axis=0),
                                axis=1, keepdims=True)
                lse4 = jnp.concatenate([_stacked_lse(ll[here, :]), _stacked_lse(ll[after, :])], axis=0)
                has_next = jnp.logical_or(n < n_tiles - 1, g < n_groups - 1)
                valid = jnp.logical_or(same_valid, jnp.logical_and(after_valid, has_next))
                p = jnp.where(valid, jnp.exp(_dot_nt(q4, kb) - lse4), 0.0)
                ds = p * (_dot_nt(do4, vb) - delta)
                dv_ref[rows, :] += _dot_tn(p, do4)
                dk_ref[rows, :] += _dot_tn(ds, q4)
                return carry

            lax.fori_loop(0, ATTN_TILE // BLOCK, blk, 0, unroll=2)

    return pl.pallas_call(
        body, grid=(width // LANES, n_tiles), in_specs=[cur, vcur, cur, nxt, cur, nxt, cur, nxt, cur, nxt],
        out_specs=[cur, cur], out_shape=[jax.ShapeDtypeStruct((s, width), F32)] * 2,
        scratch_shapes=[pltpu.VMEM((2 * ATTN_TILE, LANES), F32)] * 4,
        compiler_params=_params(2), name=name)(k, proj, q, q, do, do, o, o, lse, lse)


def _conv_specs(s, a_block, b_block):
    per = CONV_CHUNK // CONV_HALO
    a_cur = pl.BlockSpec((CONV_CHUNK, LANES), lambda cb, i: (i, a_block + cb))
    b_cur = pl.BlockSpec((CONV_CHUNK, LANES), lambda cb, i: (i, b_block + cb))
    a_halo = pl.BlockSpec((CONV_HALO, LANES), lambda cb, i: (jnp.maximum(i * per - 1, 0), a_block + cb))
    b_halo = pl.BlockSpec((CONV_HALO, LANES), lambda cb, i: (jnp.maximum(i * per - 1, 0), b_block + cb))
    w_spec = pl.BlockSpec((CONV_KERNEL, LANES), lambda cb, i: (0, cb))
    vec = pl.BlockSpec((1, LANES), lambda cb, i: (0, cb))
    out = pl.BlockSpec((CONV_CHUNK, LANES), lambda cb, i: (i, cb))
    return a_cur, b_cur, a_halo, b_halo, w_spec, vec, out


def _fill_glu_window(win, a_ref, b_ref, ah_ref, bh_ref, first):
    halo = ah_ref[...] * _sigmoid(bh_ref[...])
    win[0:CONV_HALO, :] = jnp.where(first, 0.0, halo)
    win[CONV_HALO:, :] = a_ref[...] * _sigmoid(b_ref[...])


def _conv_fwd(proj, a_block, b_block, w, bias, name):
    s = proj.shape[0]
    cw = w.shape[1]
    a_cur, b_cur, a_halo, b_halo, w_spec, vec, out = _conv_specs(s, a_block, b_block)
    lead = CONV_HALO - (CONV_KERNEL - 1)

    def body(a_ref, b_ref, ah_ref, bh_ref, w_ref, bias_ref, o_ref, win):
        _fill_glu_window(win, a_ref, b_ref, ah_ref, bh_ref, pl.program_id(1) == 0)
        for sub in range(CONV_CHUNK // CONV_SUB):
            base = sub * CONV_SUB
            acc = jnp.zeros((CONV_SUB, LANES), F32) + bias_ref[...]
            for j in range(CONV_KERNEL):
                acc = acc + w_ref[j:j + 1, :] * win[base + lead + j:base + lead + j + CONV_SUB, :]
            o_ref[base:base + CONV_SUB, :] = acc

    return pl.pallas_call(
        body, grid=(cw // LANES, s // CONV_CHUNK), in_specs=[a_cur, b_cur, a_halo, b_halo, w_spec, vec],
        out_specs=out, out_shape=jax.ShapeDtypeStruct((s, cw), F32),
        scratch_shapes=[pltpu.VMEM((CONV_CHUNK + CONV_HALO, LANES), F32)],
        compiler_params=_params(2), name=name)(proj, proj, proj, proj, w, bias)


def _conv_bwd(proj, a_block, b_block, w, du1, name):
    s = proj.shape[0]
    cw = w.shape[1]
    a_cur, b_cur, a_halo, b_halo, w_spec, vec, out = _conv_specs(s, a_block, b_block)
    per = CONV_CHUNK // CONV_HALO
    n_chunks = s // CONV_CHUNK
    d_next = pl.BlockSpec((CONV_HALO, LANES), lambda cb, i: (jnp.minimum((i + 1) * per, s // CONV_HALO - 1), cb))
    lead = CONV_HALO - (CONV_KERNEL - 1)

    def body(a_ref, b_ref, ah_ref, bh_ref, w_ref, d_ref, dn_ref, da_ref, db_ref, dw_ref, dbias_ref, win, dwin):
        i = pl.program_id(1)
        _fill_glu_window(win, a_ref, b_ref, ah_ref, bh_ref, i == 0)
        dwin[0:CONV_CHUNK, :] = d_ref[...]
        dwin[CONV_CHUNK:, :] = jnp.where(i == n_chunks - 1, 0.0, dn_ref[...])

        @pl.when(i == 0)
        def _():
            dw_ref[...] = jnp.zeros_like(dw_ref)
            dbias_ref[...] = jnp.zeros_like(dbias_ref)

        dbias_ref[...] += _colsum(d_ref[...])
        for sub in range(CONV_CHUNK // CONV_SUB):
            base = sub * CONV_SUB
            dcur = dwin[base:base + CONV_SUB, :]
            du0 = jnp.zeros((CONV_SUB, LANES), F32)
            for j in range(CONV_KERNEL):
                back = CONV_KERNEL - 1 - j
                du0 = du0 + w_ref[j:j + 1, :] * dwin[base + back:base + back + CONV_SUB, :]
                dw_ref[j:j + 1, :] += _colsum(dcur * win[base + lead + j:base + lead + j + CONV_SUB, :])
            av = a_ref[base:base + CONV_SUB, :]
            sig = _sigmoid(b_ref[base:base + CONV_SUB, :])
            da_ref[base:base + CONV_SUB, :] = du0 * sig
            db_ref[base:base + CONV_SUB, :] = du0 * av * sig * (1.0 - sig)

    return pl.pallas_call(
        body, grid=(cw // LANES, n_chunks), in_specs=[a_cur, b_cur, a_halo, b_halo, w_spec, out, d_next],
        out_specs=[out, out, w_spec, vec],
        out_shape=[jax.ShapeDtypeStruct((s, cw), F32), jax.ShapeDtypeStruct((s, cw), F32),
                   jax.ShapeDtypeStruct((CONV_KERNEL, cw), F32), jax.ShapeDtypeStruct((1, cw), F32)],
        scratch_shapes=[pltpu.VMEM((CONV_CHUNK + CONV_HALO, LANES), F32)] * 2,
        compiler_params=_params(2), name=name)(proj, proj, proj, proj, w, du1, du1)


def _adamw_math(w, g, m, v):
    m = ADAM_B1 * m + (1.0 - ADAM_B1) * g
    v = ADAM_B2 * v + (1.0 - ADAM_B2) * (g * g)
    m_hat = m / (1.0 - ADAM_B1 ** ADAM_STEP)
    v_hat = v / (1.0 - ADAM_B2 ** ADAM_STEP)
    delta = -ADAM_LR * (m_hat / (jnp.sqrt(v_hat) + ADAM_EPS) + ADAM_WD * w)
    return delta, m, v


def _adamw_big(w, g, m, v, name):
    rows, cols = w.shape
    tile = _tile(rows, 256, 8)
    spec = pl.BlockSpec((tile, cols), lambda i: (i, 0))

    def body(w_ref, g_ref, m_ref, v_ref, d_out, m_out, v_out):
        d_out[...], m_out[...], v_out[...] = _adamw_math(w_ref[...], g_ref[...], m_ref[...], v_ref[...])

    return pl.pallas_call(body, grid=(rows // tile,), in_specs=[spec] * 4, out_specs=[spec] * 3,
                          out_shape=[jax.ShapeDtypeStruct(w.shape, F32)] * 3, compiler_params=_params(1),
                          name=name)(w, g, m, v)


def _adamw_small(ws, gs, ms, vs, name):
    n = len(ws)

    def body(*refs):
        ins, outs = refs[:4 * n], refs[4 * n:]
        for t in range(n):
            res = _adamw_math(ins[t][...], ins[n + t][...], ins[2 * n + t][...], ins[3 * n + t][...])
            for j in range(3):
                outs[j * n + t][...] = res[j]

    shapes = [jax.ShapeDtypeStruct(w.shape, F32) for w in ws]
    res = pl.pallas_call(body, out_shape=shapes * 3, compiler_params=pltpu.CompilerParams(vmem_limit_bytes=VMEM_LIMIT),
                         name=name)(*ws, *gs, *ms, *vs)
    return res[:n], res[n:2 * n], res[2 * n:]


def _sum_blocks(x, n_blocks, name):
    r = x.shape[0] // n_blocks

    def body(x_ref, o_ref):
        acc = x_ref[0:r, :]
        for b in range(1, n_blocks):
            acc = acc + x_ref[b * r:(b + 1) * r, :]
        o_ref[...] = acc

    return pl.pallas_call(body, out_shape=jax.ShapeDtypeStruct((r, x.shape[1]), F32),
                          compiler_params=pltpu.CompilerParams(vmem_limit_bytes=VMEM_LIMIT), name=name)(x)


def _coords():
    return lax.axis_index("x"), lax.axis_index("y"), lax.axis_index("c")


def _flip(v, bit):
    return 1 - v if bit else v


def _ag_small(x, name):
    r, c = x.shape

    def body(x_ref, o_ref, send, recv, local_sem):
        mx, my, mc = _coords()

        def rows(px, py, pc):
            return o_ref.at[pl.ds(pl.multiple_of((4 * px + 2 * py + pc) * r, 8), r), :]

        local = pltpu.make_async_copy(x_ref, rows(mx, my, mc), local_sem)
        local.start()
        peers = [(_flip(mx, k >> 2 & 1), _flip(my, k >> 1 & 1), _flip(mc, k & 1)) for k in range(1, N_DEV)]
        sends = [pltpu.make_async_remote_copy(x_ref, rows(mx, my, mc), send.at[k], recv.at[k], device_id=p,
                                              device_id_type=MESH) for k, p in enumerate(peers)]
        for cp in sends:
            cp.start()
        for k, p in enumerate(peers):
            pltpu.make_async_remote_copy(x_ref, rows(*p), send.at[k], recv.at[k], device_id=p,
                                         device_id_type=MESH).wait_recv()
        for cp in sends:
            cp.wait_send()
        local.wait()

    vm = pl.BlockSpec(memory_space=pltpu.VMEM)
    return pl.pallas_call(
        body, in_specs=[vm], out_specs=vm, out_shape=jax.ShapeDtypeStruct((N_DEV * r, c), x.dtype),
        scratch_shapes=[pltpu.SemaphoreType.DMA((N_DEV - 1,)), pltpu.SemaphoreType.DMA((N_DEV - 1,)),
                        pltpu.SemaphoreType.DMA(())],
        name=name)(x)


class _GatherWeights:
    def __init__(self, shards):
        n_t = len(shards)
        self.inputs = list(shards)
        self.out_shapes = [jax.ShapeDtypeStruct((N_DEV * x.shape[0], x.shape[1]), x.dtype) for x in shards]
        self.scratch = [pltpu.SemaphoreType.DMA((n_t, 7)), pltpu.SemaphoreType.DMA((n_t, 7)),
                        pltpu.SemaphoreType.DMA((n_t,))]

    def _plan(self, x_refs, o_refs, sems):
        send, recv, local_sem = sems
        mx, my, mc = _coords()
        me, sibling = (mx, my, mc), (mx, my, 1 - mc)
        chips = [(1 - mx, my), (mx, 1 - my), (1 - mx, 1 - my)]

        def rows(t, px, py, pc):
            r = x_refs[t].shape[0]
            return o_refs[t].at[pl.ds(pl.multiple_of((4 * px + 2 * py + pc) * r, 8), r), :]

        def copy(t, k, block, to, src=None):
            return pltpu.make_async_remote_copy(
                src_ref=rows(t, *block) if src is None else src, dst_ref=rows(t, *block),
                send_sem=send.at[t, k], recv_sem=recv.at[t, k], device_id=to, device_id_type=MESH)

        def local(t):
            return pltpu.make_async_copy(x_refs[t], rows(t, *me), local_sem.at[t])

        return me, sibling, chips, mc, copy, local

    def start(self, x_refs, o_refs, sems):
        me, sibling, chips, mc, copy, local = self._plan(x_refs, o_refs, sems)
        for t in range(len(x_refs)):
            local(t).start()
            copy(t, 0, me, sibling, src=x_refs[t]).start()
            for j, chip in enumerate(chips):
                copy(t, 1 + j, me, (*chip, mc), src=x_refs[t]).start()

    def mid(self, x_refs, o_refs, sems):
        me, sibling, chips, mc, copy, local = self._plan(x_refs, o_refs, sems)
        for j, chip in enumerate(chips):
            for t in range(len(x_refs)):
                copy(t, 1 + j, (*chip, mc), me).wait_recv()
                copy(t, 4 + j, (*chip, mc), sibling).start()

    def finish(self, x_refs, o_refs, sems):
        me, sibling, chips, mc, copy, local = self._plan(x_refs, o_refs, sems)
        for t in range(len(x_refs)):
            copy(t, 0, sibling, me).wait_recv()
            for j, chip in enumerate(chips):
                copy(t, 4 + j, (*chip, 1 - mc), me).wait_recv()
            copy(t, 0, me, sibling, src=x_refs[t]).wait_send()
            for j, chip in enumerate(chips):
                copy(t, 1 + j, me, (*chip, mc), src=x_refs[t]).wait_send()
                copy(t, 4 + j, (*chip, mc), sibling).wait_send()
            local(t).wait()


class _SiblingExchange:
    mid = None

    def __init__(self, grads):
        n_t = len(grads)
        self.inputs = list(grads)
        self.out_shapes = [jax.ShapeDtypeStruct((N_CHIP,) + g.shape[2:], F32) for g in grads]
        self.scratch = [pltpu.SemaphoreType.DMA((n_t,)), pltpu.SemaphoreType.DMA((n_t,))]

    def _copies(self, g_refs, land, sems):
        send, recv = sems
        mx, my, mc = _coords()
        return [pltpu.make_async_remote_copy(g_refs[t].at[:, 1 - mc], land[t], send.at[t], recv.at[t],
                                             device_id=(mx, my, 1 - mc), device_id_type=MESH)
                for t in range(len(g_refs))]

    def start(self, g_refs, land, sems):
        for cp in self._copies(g_refs, land, sems):
            cp.start()

    def finish(self, g_refs, land, sems):
        for cp in self._copies(g_refs, land, sems):
            cp.wait()


class _Together:
    def __init__(self, *comms):
        self.comms = comms
        self.inputs = [x for c in comms for x in c.inputs]
        self.out_shapes = [x for c in comms for x in c.out_shapes]
        self.scratch = [x for c in comms for x in c.scratch]
        self.mid = self._mid if any(c.mid is not None for c in comms) else None

    def _each(self, phase, cin, cout, sems):
        i = o = s = 0
        for c in self.comms:
            fn = getattr(c, phase)
            ni, no, ns = len(c.inputs), len(c.out_shapes), len(c.scratch)
            if fn is not None:
                fn(cin[i:i + ni], cout[o:o + no], sems[s:s + ns])
            i, o, s = i + ni, o + no, s + ns

    def start(self, cin, cout, sems):
        self._each("start", cin, cout, sems)

    def _mid(self, cin, cout, sems):
        self._each("mid", cin, cout, sems)

    def finish(self, cin, cout, sems):
        self._each("finish", cin, cout, sems)


def _standalone(comm, name):
    def body():
        pass
    return _call(body, grid=(1,), in_specs=[], out_specs=[], out_shape=[], args=(), name=name, comm=comm)[1]


def _chip_partial(g4, land, name):
    _, _, r, c = g4.shape
    tr = _tile(r, 256, 16)

    def body(g_ref, l_ref, o_ref):
        o_ref[...] = (g_ref[...] + l_ref[...]).astype(o_ref.dtype)

    return pl.pallas_call(
        body, grid=(N_CHIP, r // tr),
        in_specs=[pl.BlockSpec((None, None, tr, c), lambda q, i: (q, lax.axis_index("c"), i, 0)),
                  pl.BlockSpec((None, tr, c), lambda q, i: (q, i, 0))],
        out_specs=pl.BlockSpec((None, tr, c), lambda q, i: (q, i, 0)),
        out_shape=jax.ShapeDtypeStruct((N_CHIP, r, c), BF16), compiler_params=_params(2), name=name)(g4, land)


class _ChipExchange:
    mid = None

    def __init__(self, parts):
        n_t = len(parts)
        self.inputs = list(parts)
        self.out_shapes = [jax.ShapeDtypeStruct(p.shape, p.dtype) for p in parts]
        self.scratch = [pltpu.SemaphoreType.DMA((n_t, 3)), pltpu.SemaphoreType.DMA((n_t, 3)),
                        pltpu.SemaphoreType.DMA((n_t,))]

    def _plan(self, p_refs, land, sems):
        send, recv, local_sem = sems
        mx, my, mc = _coords()
        my_chip = 2 * mx + my
        peers = [(_flip(mx, fx), _flip(my, fy)) for fx, fy in ((1, 0), (0, 1), (1, 1))]

        def out(t, k):
            px, py = peers[k]
            return pltpu.make_async_remote_copy(p_refs[t].at[2 * px + py], land[t].at[my_chip], send.at[t, k],
                                                recv.at[t, k], device_id=(px, py, mc), device_id_type=MESH)

        def arrival(t, k):
            px, py = peers[k]
            return pltpu.make_async_remote_copy(p_refs[t].at[my_chip], land[t].at[2 * px + py], send.at[t, k],
                                                recv.at[t, k], device_id=(px, py, mc), device_id_type=MESH)

        def local(t):
            return pltpu.make_async_copy(p_refs[t].at[my_chip], land[t].at[my_chip], local_sem.at[t])

        return out, arrival, local

    def start(self, p_refs, land, sems):
        out, arrival, local = self._plan(p_refs, land, sems)
        for t in range(len(p_refs)):
            local(t).start()
            for k in range(3):
                out(t, k).start()

    def finish(self, p_refs, land, sems):
        out, arrival, local = self._plan(p_refs, land, sems)
        for t in range(len(p_refs)):
            for k in range(3):
                arrival(t, k).wait_recv()
                out(t, k).wait_send()
            local(t).wait()


def _sum_chips(land, name):
    _, r, c = land.shape
    tr = _tile(r, 256, 16)

    def body(l_ref, o_ref):
        acc = l_ref[0].astype(F32)
        for q in range(1, N_CHIP):
            acc = acc + l_ref[q].astype(F32)
        o_ref[...] = acc

    return pl.pallas_call(
        body, grid=(r // tr,), in_specs=[pl.BlockSpec((N_CHIP, tr, c), lambda i: (0, i, 0))],
        out_specs=pl.BlockSpec((tr, c), lambda i: (i, 0)), out_shape=jax.ShapeDtypeStruct((r, c), F32),
        compiler_params=_params(1), name=name)(land)


def _rope_tables(s, width):
    pos = jnp.arange(s, dtype=F32)
    inv_freq = ROPE_THETA ** (-jnp.arange(0, HEAD_DIM, 2, dtype=F32) / HEAD_DIM)
    ang = pos[:, None] * inv_freq[None, :]
    cos, sin = jnp.cos(ang), jnp.sin(ang)
    heads = width // HEAD_DIM
    return jnp.tile(jnp.concatenate([cos, cos], axis=1), (1, heads)), jnp.tile(jnp.concatenate([-sin, sin], axis=1), (1, heads))


def _pad_rows(v, rows):
    return jnp.concatenate([v, jnp.zeros((rows - 1, v.shape[1]), v.dtype)], axis=0)


def kernel(x, c, w_ada, b_ada, ffn1_norm_g, ffn1_w_gate, ffn1_w_up, ffn1_w_down, mix_norm_g, w_in, conv_dw_w, conv_dw_b, conv_ln_g, conv_ln_b, attn_out_g, conv_out_g, w_out, ffn2_norm_g, ffn2_w_gate, ffn2_w_up, ffn2_w_down, final_norm_g, loss_target, m_w_ada, m_b_ada, m_ffn1_norm_g, m_ffn1_w_gate, m_ffn1_w_up, m_ffn1_w_down, m_mix_norm_g, m_w_in, m_conv_dw_w, m_conv_dw_b, m_conv_ln_g, m_conv_ln_b, m_attn_out_g, m_conv_out_g, m_w_out, m_ffn2_norm_g, m_ffn2_w_gate, m_ffn2_w_up, m_ffn2_w_down, m_final_norm_g, v_w_ada, v_b_ada, v_ffn1_norm_g, v_ffn1_w_gate, v_ffn1_w_up, v_ffn1_w_down, v_mix_norm_g, v_w_in, v_conv_dw_w, v_conv_dw_b, v_conv_ln_g, v_conv_ln_b, v_attn_out_g, v_conv_out_g, v_w_out, v_ffn2_norm_g, v_ffn2_w_gate, v_ffn2_w_up, v_ffn2_w_down, v_final_norm_g):
    mx, my, mc = _coords()
    me = 4 * mx + 2 * my + mc
    s, d = x.shape[1], x.shape[2]
    aw = d // 2
    x2, target = x[0], loss_target[0]
    n_mod = w_ada.shape[2] * N_DEV // d
    mod_cols = w_ada.shape[2]

    cw_shard = conv_dw_w.shape[3]
    n_taps = CONV_KERNEL * cw_shard
    first_len = -(-(d + n_taps) // LANES) * LANES
    first = jnp.concatenate([c, conv_dw_w[0, :, 0, :].reshape(1, n_taps), jnp.zeros((1, first_len - d - n_taps), F32)], axis=1)
    first_all = _ag_small(_pad_rows(first, 8), "ag_c_taps")[0::8]
    c_all = first_all[:, :d]
    conv_w = first_all[:, d:d + n_taps].reshape(N_DEV, CONV_KERNEL, cw_shard).transpose(1, 0, 2).reshape(CONV_KERNEL, aw)

    silu_c = _silu_rows(c_all, "silu_c")
    mod_part = _plain_mm([(silu_c, w_ada[0])], F32, False, mod_cols, "mod_mm")
    mod_all = _ag_small(mod_part, "ag_mod").reshape(N_DEV, N_DEV, mod_cols)
    mod = lax.dynamic_index_in_dim(mod_all, me, axis=1, keepdims=False).reshape(1, n_mod * d) + b_ada
    sh1, sc1, g1, sh2, sc2, g2, sh3, sc3, g3 = [mod[:, i * d:(i + 1) * d] for i in range(n_mod)]

    def shard(w, transpose):
        return (w[0].T if transpose else w[0]).astype(BF16)

    def split(g):
        return g.reshape(N_CHIP, 2, g.shape[0] // N_DEV, g.shape[1])

    def partials(g4s, lands, tag):
        return [_chip_partial(a, b, "chip_partial_%s%d" % (tag, t)) for t, (a, b) in enumerate(zip(g4s, lands))]

    wg1, wu1 = _standalone(_GatherWeights([shard(ffn1_w_gate, True), shard(ffn1_w_up, True)]), "ag_ffn1_in")
    gather_mid = _GatherWeights([shard(ffn1_w_down, False), shard(w_in, True), shard(w_out, False)])
    gather_late = _GatherWeights([shard(ffn2_w_gate, True), shard(ffn2_w_up, True), shard(ffn2_w_down, False)])

    n1 = _norm_mod_fwd(x2, ffn1_norm_g, sc1, sh1, "norm1")
    (a1, b1, hid1), (wd1, win_t, wout) = _ffn_up(n1, wg1, wu1, "ffn1_up", comm=gather_mid)
    h1, f1 = _residual_mm(hid1, wd1, x2, g1, 0.5, "ffn1_down")
    n2 = _norm_mod_fwd(h1, mix_norm_g, sc2, sh2, "norm2")
    proj = _plain_mm([(n2, win_t)], F32, True, _tile(5 * aw, 1536, LANES), "proj")
    cos, sin_signed = _rope_tables(s, aw)
    q_rot, k_rot = _rope_fwd(proj, cos, sin_signed, aw, "rope")
    lanes_per = aw // LANES
    (attn, lse), (wg2, wu2, wd2) = _attn_fwd(q_rot, k_rot, proj, 2 * lanes_per, "attn_fwd", comm=gather_late)
    u1 = _conv_fwd(proj, 3 * lanes_per, 4 * lanes_per, conv_w, conv_dw_b, "conv_fwd")
    y = _mix_post_fwd(attn, u1, attn_out_g, conv_ln_g, conv_ln_b, conv_out_g, "mix_post")
    h2, mix = _residual_mm(y, wout, h1, g2, 1.0, "mix_out")
    n3 = _norm_mod_fwd(h2, ffn2_norm_g, sc3, sh3, "norm3")
    a3, b3, hid3 = _ffn_up(n3, wg2, wu2, "ffn2_up")
    h3, f3 = _residual_mm(hid3, wd2, h2, g3, 0.5, "ffn2_down")

    dh3, err2, d_final_g = _final_loss(h3, target, final_norm_g.reshape(1, d), "final_loss")
    loss = lax.psum(0.5 * jnp.sum(err2) / d, ("x", "y", "c"))

    df3, dg3 = _gate_bwd(dh3, f3, g3, 0.5, "gate3_bwd")
    da3, db3 = _ffn_bwd_hidden(df3, wd2, a3, b3, "ffn2_hidden_bwd")
    g4_a = [split(_mm_tn(da3, n3, "ffn2_dwg")), split(_mm_tn(db3, n3, "ffn2_dwu")), split(_mm_tn(hid3, df3, "ffn2_dwd"))]
    dn3, land_a = _plain_mm([(da3, wg2), (db3, wu2)], F32, False, d, "ffn2_dn", tm=256, comm=_SiblingExchange(g4_a))
    parts_a = partials(g4_a, land_a, "a")
    dh2, dsh3, dsc3, dgn3 = _norm_mod_bwd(dn3, h2, dh3, ffn2_norm_g, sc3, "norm3_bwd")

    dmix, dg2 = _gate_bwd(dh2, mix, g2, 1.0, "gate2_bwd")
    dy = _plain_mm([(dmix, wout)], F32, True, d, "mix_dy")
    g_wout = _mm_tn(y, dmix, "mix_dwout")
    dattn, du1, d_attn_g, d_conv_g, d_ln_g, d_ln_b = _mix_post_bwd(
        dy, attn, u1, attn_out_g, conv_ln_g, conv_ln_b, conv_out_g, "mix_post_bwd")
    dga, dgb, d_taps, d_conv_b = _conv_bwd(proj, 3 * lanes_per, 4 * lanes_per, conv_w, du1, "conv_bwd")
    (dq,), sums_a = _attn_bwd_q(q_rot, k_rot, proj, 2 * lanes_per, dattn, attn, lse, "attn_bwd_q",
                                comm=_ChipExchange(parts_a))
    dk, dv = _attn_bwd_kv(q_rot, k_rot, proj, 2 * lanes_per, dattn, attn, lse, "attn_bwd_kv")
    dproj = _dproj_assemble(dq, dk, dv, dga, dgb, cos, sin_signed, "dproj")
    dn2 = _plain_mm([(dproj, win_t)], F32, False, d, "mix_dn")
    g4_b = [split(g_wout), split(_mm_tn(dproj, n2, "mix_dwin"))]
    (dh1, dsh2, dsc2, dgn2), land_b = _norm_mod_bwd(dn2, h1, dh2, mix_norm_g, sc2, "norm2_bwd",
                                                    comm=_SiblingExchange(g4_b))
    parts_b = partials(g4_b, land_b, "b")

    df1, dg1 = _gate_bwd(dh1, f1, g1, 0.5, "gate1_bwd")
    g4_c = [split(_mm_tn(hid1, df1, "ffn1_dwd"))]
    (da1, db1), both = _ffn_bwd_hidden(df1, wd1, a1, b1, "ffn1_hidden_bwd",
                                       comm=_Together(_ChipExchange(parts_b), _SiblingExchange(g4_c)))
    sums_b, land_c = both[:2], both[2:]
    parts_c = partials(g4_c, land_c, "c")
    g_wg1, sums_c = _mm_tn(da1, n1, "ffn1_dwg", comm=_ChipExchange(parts_c))
    g4_d = [split(g_wg1), split(_mm_tn(db1, n1, "ffn1_dwu"))]
    dn1, land_d = _plain_mm([(da1, wg1), (db1, wu1)], F32, False, d, "ffn1_dn", tm=256, comm=_SiblingExchange(g4_d))
    parts_d = partials(g4_d, land_d, "d")
    (dx, dsh1, dsc1, dgn1), sums_d = _norm_mod_bwd(dn1, x2, dh1, ffn1_norm_g, sc1, "norm1_bwd",
                                                   comm=_ChipExchange(parts_d))

    dmod = jnp.concatenate([dsh1, dsc1, dg1, dsh2, dsc2, dg2, dsh3, dsc3, dg3], axis=1)
    small = [dmod, dgn1, dgn2, dgn3, d_final_g, d_conv_b, d_ln_g, d_ln_b, d_attn_g, d_conv_g,
             d_taps.reshape(1, CONV_KERNEL * aw)]
    sizes = [v.shape[1] for v in small]
    total = sum(sizes)
    padded = -(-total // (8 * LANES)) * (8 * LANES)
    packed = jnp.concatenate(small + [jnp.zeros((1, padded - total), F32)], axis=1).reshape(8, padded // 8)
    gathered = _ag_small(packed, "ag_small_grads")
    summed = _sum_blocks(gathered, N_DEV, "sum_small_grads").reshape(1, padded)
    offs = [sum(sizes[:i]) for i in range(len(sizes))]
    (g_b_ada, g_gn1, g_gn2, g_gn3, g_final, g_conv_b, g_ln_g, g_ln_b, g_attn_g, g_conv_g, g_taps) = [
        summed[:, o:o + n] for o, n in zip(offs, sizes)]
    g_taps_shard = lax.dynamic_slice_in_dim(g_taps.reshape(CONV_KERNEL, aw), me * cw_shard, cw_shard, axis=1)
    dmod_all = gathered.reshape(N_DEV, padded)[:, :n_mod * d]
    dmod_cols = lax.dynamic_slice_in_dim(dmod_all, me * mod_cols, mod_cols, axis=1)
    g_w_ada = _mm_tn(silu_c, dmod_cols, "ada_dw")

    arrived = list(sums_a) + list(sums_b) + list(sums_c) + list(sums_d)
    s_wg2, s_wu2, s_wd2, s_wout, s_win, s_wd1, s_wg1, s_wu1 = [
        _sum_chips(l, "sum_chips_%d" % t) for t, l in enumerate(arrived)]

    grads = {
        "w_ada": g_w_ada, "b_ada": g_b_ada, "ffn1_norm_g": g_gn1, "ffn1_w_gate": s_wg1.T, "ffn1_w_up": s_wu1.T,
        "ffn1_w_down": s_wd1, "mix_norm_g": g_gn2, "w_in": s_win.T, "conv_dw_w": g_taps_shard, "conv_dw_b": g_conv_b,
        "conv_ln_g": g_ln_g, "conv_ln_b": g_ln_b, "attn_out_g": g_attn_g, "conv_out_g": g_conv_g, "w_out": s_wout,
        "ffn2_norm_g": g_gn3, "ffn2_w_gate": s_wg2.T, "ffn2_w_up": s_wu2.T, "ffn2_w_down": s_wd2,
        "final_norm_g": g_final,
    }
    weights = dict(w_ada=w_ada, b_ada=b_ada, ffn1_norm_g=ffn1_norm_g, ffn1_w_gate=ffn1_w_gate, ffn1_w_up=ffn1_w_up, ffn1_w_down=ffn1_w_down, mix_norm_g=mix_norm_g, w_in=w_in, conv_dw_w=conv_dw_w, conv_dw_b=conv_dw_b, conv_ln_g=conv_ln_g, conv_ln_b=conv_ln_b, attn_out_g=attn_out_g, conv_out_g=conv_out_g, w_out=w_out, ffn2_norm_g=ffn2_norm_g, ffn2_w_gate=ffn2_w_gate, ffn2_w_up=ffn2_w_up, ffn2_w_down=ffn2_w_down, final_norm_g=final_norm_g)
    moms = dict(w_ada=m_w_ada, b_ada=m_b_ada, ffn1_norm_g=m_ffn1_norm_g, ffn1_w_gate=m_ffn1_w_gate, ffn1_w_up=m_ffn1_w_up, ffn1_w_down=m_ffn1_w_down, mix_norm_g=m_mix_norm_g, w_in=m_w_in, conv_dw_w=m_conv_dw_w, conv_dw_b=m_conv_dw_b, conv_ln_g=m_conv_ln_g, conv_ln_b=m_conv_ln_b, attn_out_g=m_attn_out_g, conv_out_g=m_conv_out_g, w_out=m_w_out, ffn2_norm_g=m_ffn2_norm_g, ffn2_w_gate=m_ffn2_w_gate, ffn2_w_up=m_ffn2_w_up, ffn2_w_down=m_ffn2_w_down, final_norm_g=m_final_norm_g)
    vars_ = dict(w_ada=v_w_ada, b_ada=v_b_ada, ffn1_norm_g=v_ffn1_norm_g, ffn1_w_gate=v_ffn1_w_gate, ffn1_w_up=v_ffn1_w_up, ffn1_w_down=v_ffn1_w_down, mix_norm_g=v_mix_norm_g, w_in=v_w_in, conv_dw_w=v_conv_dw_w, conv_dw_b=v_conv_dw_b, conv_ln_g=v_conv_ln_g, conv_ln_b=v_conv_ln_b, attn_out_g=v_attn_out_g, conv_out_g=v_conv_out_g, w_out=v_w_out, ffn2_norm_g=v_ffn2_norm_g, ffn2_w_gate=v_ffn2_w_gate, ffn2_w_up=v_ffn2_w_up, ffn2_w_down=v_ffn2_w_down, final_norm_g=v_final_norm_g)
    names = list(weights)
    big = ["w_ada", "ffn1_w_gate", "ffn1_w_up", "ffn1_w_down", "w_in", "w_out", "ffn2_w_gate", "ffn2_w_up",
           "ffn2_w_down"]
    shape2 = {n: (weights[n].shape[-2] if weights[n].ndim > 1 else 1, weights[n].shape[-1]) for n in names}
    shape2["conv_dw_w"] = (CONV_KERNEL, cw_shard)
    g_out, d_out, m_out, v_out = {}, {}, {}, {}
    for n in big:
        g2d = grads[n].reshape(shape2[n])
        res = _adamw_big(weights[n].reshape(shape2[n]), g2d, moms[n].reshape(shape2[n]), vars_[n].reshape(shape2[n]),
                         "adamw_" + n)
        g_out[n], (d_out[n], m_out[n], v_out[n]) = g2d, res
    rest = [n for n in names if n not in big]
    res = _adamw_small([weights[n].reshape(shape2[n]) for n in rest], [grads[n].reshape(shape2[n]) for n in rest],
                       [moms[n].reshape(shape2[n]) for n in rest], [vars_[n].reshape(shape2[n]) for n in rest],
                       "adamw_small")
    for i, n in enumerate(rest):
        g_out[n], d_out[n], m_out[n], v_out[n] = grads[n], res[0][i], res[1][i], res[2][i]

    def shaped(table):
        return [table[n].reshape(weights[n].shape) for n in names]

    return (loss, dx.reshape(x.shape), *shaped(g_out), *shaped(d_out), *shaped(m_out), *shaped(v_out))
```

```python
import functools

import jax
import jax.numpy as jnp
from jax import lax
from jax.experimental import pallas as pl
from jax.experimental.pallas import tpu as pltpu

F32 = jnp.float32
BF16 = jnp.bfloat16
MESH = pl.DeviceIdType.MESH
ANY = pl.BlockSpec(memory_space=pl.ANY)

N_DEV = 8
N_CHIP = 4
HEAD_DIM = 64
HALF_HEAD = HEAD_DIM // 2
LANES = 128
BLOCK = 128
DILATIONS = (1, 4, 16)
ATTN_TILE = BLOCK * max(DILATIONS)
ROPE_THETA = 10000.0
CONV_KERNEL = 31
CONV_HALO = 32
CONV_CHUNK = 512
CONV_SUB = 128
RMS_EPS = 1e-6
LN_EPS = 1e-5
ADAM_LR = 0.001
ADAM_B1 = 0.9
ADAM_B2 = 0.999
ADAM_EPS = 1e-08
ADAM_WD = 0.01
ADAM_STEP = 10
VMEM_LIMIT = 56 * 1024 * 1024
NEG = -1e30


def _params(n_axes):
    return pltpu.CompilerParams(dimension_semantics=("arbitrary",) * n_axes, vmem_limit_bytes=VMEM_LIMIT)


def _tile(n, target, unit):
    best = None
    for t in range(unit, min(n, target) + 1, unit):
        if n % t == 0:
            best = t
    return best if best is not None else n


def _sigmoid(x):
    return 0.5 * (jnp.tanh(0.5 * x) + 1.0)


def _call(body, *, grid, in_specs, out_specs, out_shape, args, name, scratch_shapes=(), comm=None):
    params = _params(len(grid))
    if comm is None:
        return pl.pallas_call(body, grid=grid, in_specs=list(in_specs), out_specs=list(out_specs),
                              out_shape=list(out_shape), scratch_shapes=list(scratch_shapes),
                              compiler_params=params, name=name)(*args)
    n_in, n_out, n_scr = len(args), len(out_shape), len(scratch_shapes)
    c_in, c_out = len(comm.inputs), len(comm.out_shapes)
    steps = 1
    for g in grid:
        steps *= g

    def hosted(*refs):
        pos = 0
        parts = []
        for size in (n_in, c_in, n_out, c_out, n_scr, len(comm.scratch)):
            parts.append(refs[pos:pos + size])
            pos += size
        ins, cin, outs, cout, scr, cscr = parts
        step = 0
        for axis, g in enumerate(grid):
            step = step * g + pl.program_id(axis)

        @pl.when(step == 0)
        def _():
            comm.start(cin, cout, cscr)

        body(*ins, *outs, *scr)
        if comm.mid is not None and steps >= 4:
            @pl.when(step == (3 * steps) // 4)
            def _():
                comm.mid(cin, cout, cscr)

        @pl.when(step == steps - 1)
        def _():
            if comm.mid is not None and steps < 4:
                comm.mid(cin, cout, cscr)
            comm.finish(cin, cout, cscr)

    res = pl.pallas_call(
        hosted, grid=grid, in_specs=list(in_specs) + [ANY] * c_in, out_specs=list(out_specs) + [ANY] * c_out,
        out_shape=list(out_shape) + list(comm.out_shapes), scratch_shapes=list(scratch_shapes) + list(comm.scratch),
        compiler_params=params, name=name)(*args, *comm.inputs)
    return res[:n_out], res[n_out:]


def _rows(fn, rows_in, vecs_in, rows_out, vecs_out, *, tile, name, comm=None):
    norm = [r if isinstance(r, tuple) else (r, r.shape[1], 0) for r in rows_in]
    n_rows = norm[0][0].shape[0]
    n_tiles = n_rows // tile
    in_specs, args = [], []
    for arr, width, cb in norm:
        in_specs.append(pl.BlockSpec((tile, width), functools.partial(lambda i, cb: (i, cb), cb=cb)))
        args.append(arr)
    for v in vecs_in:
        in_specs.append(pl.BlockSpec((1, v.shape[1]), lambda i: (0, 0)))
        args.append(v)
    out_shape = [jax.ShapeDtypeStruct((n_rows, w), dt) for w, dt in rows_out]
    out_shape += [jax.ShapeDtypeStruct((1, w), F32) for w in vecs_out]
    out_specs = [pl.BlockSpec((tile, w), lambda i: (i, 0)) for w, _ in rows_out]
    out_specs += [pl.BlockSpec((1, w), lambda i: (0, 0)) for w in vecs_out]
    n_in, n_ro = len(args), len(rows_out)

    def body(*refs):
        vals = [r[...] for r in refs[:n_in]]
        outs = refs[n_in:]
        row_vals, vec_vals = fn(*vals)
        for ref, val in zip(outs[:n_ro], row_vals):
            if isinstance(val, tuple):
                w = val[0].shape[1]
                for j, piece in enumerate(val):
                    ref[:, j * w:(j + 1) * w] = piece.astype(ref.dtype)
            else:
                ref[...] = val.astype(ref.dtype)
        if vecs_out:
            @pl.when(pl.program_id(0) == 0)
            def _():
                for ref in outs[n_ro:]:
                    ref[...] = jnp.zeros_like(ref)
            for ref, val in zip(outs[n_ro:], vec_vals):
                ref[...] += val

    return _call(body, grid=(n_tiles,), in_specs=in_specs, out_specs=out_specs, out_shape=out_shape, args=args,
                 name=name, comm=comm)


def _colsum(x):
    return jnp.sum(x, axis=0, keepdims=True)


def _rms_stats(h):
    r = lax.rsqrt(jnp.mean(h * h, axis=-1, keepdims=True) + RMS_EPS)
    return r, h * r


def _rms_back(r, xn, dxn):
    return r * (dxn - xn * jnp.mean(dxn * xn, axis=-1, keepdims=True))


def _norm_mod_fwd(h, gain, scale, shift, name):
    def fn(h, gain, scale, shift):
        _, xn = _rms_stats(h)
        return [(xn * gain) * (1.0 + scale) + shift], []
    return _rows(fn, [h], [gain, scale, shift], [(h.shape[1], BF16)], [], tile=512, name=name)[0]


def _norm_mod_bwd(dn, h, dh_in, gain, scale, name, comm=None):
    def fn(dn, h, dh_in, gain, scale):
        r, xn = _rms_stats(h)
        y = xn * gain
        dy = dn * (1.0 + scale)
        dh = dh_in + _rms_back(r, xn, dy * gain)
        return [dh], [_colsum(dn), _colsum(dn * y), _colsum(dy * xn)]
    d = h.shape[1]
    return _rows(fn, [dn, h, dh_in], [gain, scale], [(d, F32)], [d, d, d], tile=256, name=name, comm=comm)


def _final_loss(h, target, gain, name):
    d = h.shape[1]

    def fn(h, target, gain):
        r, xn = _rms_stats(h)
        err = xn * gain - target
        dout = err * (1.0 / d)
        dh = _rms_back(r, xn, dout * gain)
        return [dh], [_colsum(err * err), _colsum(dout * xn)]
    return _rows(fn, [h, target], [gain], [(d, F32)], [d, d], tile=256, name=name)


def _gate_bwd(dh, f, gate, coef, name):
    def fn(dh, f, gate):
        return [(coef * gate) * dh], [coef * _colsum(f.astype(F32) * dh)]
    d = dh.shape[1]
    return _rows(fn, [dh, f], [gate], [(d, BF16)], [d], tile=512, name=name)


def _partner(x):
    width = x.shape[1]
    lane = lax.broadcasted_iota(jnp.int32, x.shape, 1) % HEAD_DIM
    return jnp.where(lane < HALF_HEAD, pltpu.roll(x, width - HALF_HEAD, 1), pltpu.roll(x, HALF_HEAD, 1))


def _rope_fwd(proj, cos, sin_signed, width, name):
    qscale = HEAD_DIM ** -0.5

    def fn(q, k, cos, sin):
        qr = q * cos + _partner(q) * sin
        kr = k * cos + _partner(k) * sin
        return [qr * qscale, kr], []
    return _rows(fn, [(proj, width, 0), (proj, width, 1), cos, sin_signed], [], [(width, F32), (width, F32)], [],
                 tile=512, name=name)


def _dproj_assemble(dq, dk, dv, dga, dgb, cos, sin_signed, name):
    width = dq.shape[1]
    qscale = HEAD_DIM ** -0.5

    def fn(dq, dk, dv, dga, dgb, cos, sin):
        dq0 = (dq * cos - _partner(dq) * sin) * qscale
        dk0 = dk * cos - _partner(dk) * sin
        return [(dq0, dk0, dv, dga, dgb)], []
    return _rows(fn, [dq, dk, dv, dga, dgb, cos, sin_signed], [], [(5 * width, BF16)], [], tile=256, name=name)[0]


def _mix_post_fwd(attn, u1, attn_g, ln_g, ln_b, conv_g, name):
    def fn(attn, u1, attn_g, ln_g, ln_b, conv_g):
        _, xa = _rms_stats(attn)
        mu = jnp.mean(u1, axis=-1, keepdims=True)
        xc = u1 - mu
        rstd = lax.rsqrt(jnp.mean(xc * xc, axis=-1, keepdims=True) + LN_EPS)
        u2 = (xc * rstd) * ln_g + ln_b
        u3 = u2 * _sigmoid(u2)
        _, x3 = _rms_stats(u3)
        return [(xa * attn_g, x3 * conv_g)], []
    w = attn.shape[1]
    return _rows(fn, [attn, u1], [attn_g, ln_g, ln_b, conv_g], [(2 * w, BF16)], [], tile=512, name=name)[0]


def _mix_post_bwd(dy, attn, u1, attn_g, ln_g, ln_b, conv_g, name):
    w = attn.shape[1]

    def fn(dya, dyc, attn, u1, attn_g, ln_g, ln_b, conv_g):
        ra, xa = _rms_stats(attn)
        dattn = _rms_back(ra, xa, dya * attn_g)
        mu = jnp.mean(u1, axis=-1, keepdims=True)
        xc = u1 - mu
        rstd = lax.rsqrt(jnp.mean(xc * xc, axis=-1, keepdims=True) + LN_EPS)
        xh = xc * rstd
        u2 = xh * ln_g + ln_b
        sig = _sigmoid(u2)
        u3 = u2 * sig
        r3, x3 = _rms_stats(u3)
        du3 = _rms_back(r3, x3, dyc * conv_g)
        du2 = du3 * (sig + u3 * (1.0 - sig))
        dxh = du2 * ln_g
        du1 = rstd * (dxh - jnp.mean(dxh, axis=-1, keepdims=True) - xh * jnp.mean(dxh * xh, axis=-1, keepdims=True))
        return [dattn, du1], [_colsum(dya * xa), _colsum(dyc * x3), _colsum(du2 * xh), _colsum(du2)]
    return _rows(fn, [(dy, w, 0), (dy, w, 1), attn, u1], [attn_g, ln_g, ln_b, conv_g], [(w, F32), (w, F32)],
                 [w, w, w, w], tile=256, name=name)


def _silu_rows(c_all, name):
    def fn(c):
        return [c * _sigmoid(c)], []
    return _rows(fn, [c_all], [], [(c_all.shape[1], BF16)], [], tile=c_all.shape[0], name=name)[0]


def _mm(groups, epi, extras, vecs, outs, *, trans_rhs, tm, tn, name, comm=None):
    m = groups[0][0][0].shape[0]
    n = groups[0][0][1].shape[0] if trans_rhs else groups[0][0][1].shape[1]
    tm, tn = min(tm, m), min(tn, n)
    in_specs, args = [], []
    for grp in groups:
        for lhs, rhs in grp:
            k = lhs.shape[1]
            in_specs.append(pl.BlockSpec((tm, k), lambda j, i: (i, 0)))
            in_specs.append(pl.BlockSpec((tn, k), lambda j, i: (j, 0)) if trans_rhs
                            else pl.BlockSpec((k, tn), lambda j, i: (0, j)))
            args += [lhs, rhs]
    for e in extras:
        in_specs.append(pl.BlockSpec((tm, tn), lambda j, i: (i, j)))
        args.append(e)
    for v in vecs:
        in_specs.append(pl.BlockSpec((1, tn), lambda j, i: (0, j)))
        args.append(v)
    sizes = [len(g) for g in groups]
    n_mm, n_ex, n_vec = 2 * sum(sizes), len(extras), len(vecs)
    dims = (((1,), (1,)), ((), ())) if trans_rhs else (((1,), (0,)), ((), ()))

    def body(*refs):
        accs, pos = [], 0
        for size in sizes:
            acc = None
            for _ in range(size):
                part = lax.dot_general(refs[pos][...].astype(BF16), refs[pos + 1][...].astype(BF16), dims,
                                       preferred_element_type=F32)
                acc = part if acc is None else acc + part
                pos += 2
            accs.append(acc)
        ex = [r[...] for r in refs[n_mm:n_mm + n_ex]]
        vc = [r[...] for r in refs[n_mm + n_ex:n_mm + n_ex + n_vec]]
        for ref, val in zip(refs[n_mm + n_ex + n_vec:], epi(accs, ex, vc)):
            ref[...] = val.astype(ref.dtype)

    return _call(body, grid=(n // tn, m // tm), in_specs=in_specs,
                 out_specs=[pl.BlockSpec((tm, tn), lambda j, i: (i, j)) for _ in outs],
                 out_shape=[jax.ShapeDtypeStruct((m, n), dt) for dt in outs], args=args, name=name, comm=comm)


def _mm_tn(lhs, rhs, name, comm=None):
    t, a = lhs.shape
    b = rhs.shape[1]
    ta = a if a <= 1536 else _tile(a, 1536, LANES)
    tk = _tile(t, 512, 8)

    def body(l_ref, r_ref, o_ref):
        @pl.when(pl.program_id(1) == 0)
        def _():
            o_ref[...] = jnp.zeros_like(o_ref)
        o_ref[...] += lax.dot_general(l_ref[...].astype(BF16), r_ref[...].astype(BF16), (((0,), (0,)), ((), ())),
                                      preferred_element_type=F32)

    res = _call(body, grid=(a // ta, t // tk),
                in_specs=[pl.BlockSpec((tk, ta), lambda i, k: (k, i)), pl.BlockSpec((tk, b), lambda i, k: (k, 0))],
                out_specs=[pl.BlockSpec((ta, b), lambda i, k: (i, 0))], out_shape=[jax.ShapeDtypeStruct((a, b), F32)],
                args=(lhs, rhs), name=name, comm=comm)
    return res[0] if comm is None else (res[0][0], res[1])


def _ffn_tn(f):
    return _tile(f, 1536, LANES)


def _ffn_up(n, wg_t, wu_t, name, comm=None):
    def epi(accs, ex, vc):
        a, b = accs
        return [a, b, (a * _sigmoid(a)) * b]
    return _mm([[(n, wg_t)], [(n, wu_t)]], epi, [], [], [BF16, BF16, BF16], trans_rhs=True, tm=256,
               tn=_ffn_tn(wg_t.shape[0]), name=name, comm=comm)


def _residual_mm(lhs, w, res, gate, coef, name, comm=None):
    def epi(accs, ex, vc):
        return [ex[0] + (coef * vc[0]) * accs[0], accs[0]]
    return _mm([[(lhs, w)]], epi, [res], [gate], [F32, BF16], trans_rhs=False, tm=512, tn=w.shape[1], name=name,
               comm=comm)


def _ffn_bwd_hidden(df, wd, a, b, name, comm=None):
    def epi(accs, ex, vc):
        dh = accs[0]
        av, bv = ex[0].astype(F32), ex[1].astype(F32)
        sig = _sigmoid(av)
        silu = av * sig
        return [dh * bv * (sig + silu * (1.0 - sig)), dh * silu]
    return _mm([[(df, wd)]], epi, [a, b], [], [BF16, BF16], trans_rhs=True, tm=256, tn=_ffn_tn(wd.shape[0]),
               name=name, comm=comm)


def _plain_mm(pairs, out_dtype, trans_rhs, tn, name, tm=512, comm=None):
    def epi(accs, ex, vc):
        return [accs[0]]
    res = _mm([pairs], epi, [], [], [out_dtype], trans_rhs=trans_rhs, tm=tm, tn=tn, name=name, comm=comm)
    return res[0] if comm is None else (res[0][0], res[1])


HEADS_PER_TILE = LANES // HEAD_DIM


def _stack_heads(x):
    lane = lax.broadcasted_iota(jnp.int32, (1, LANES), 1)
    return jnp.concatenate([x * (lane // HEAD_DIM == h).astype(F32) for h in range(HEADS_PER_TILE)], axis=0)


def _unstack_heads(y):
    r = y.shape[0] // HEADS_PER_TILE
    lane = lax.broadcasted_iota(jnp.int32, (r, y.shape[1]), 1)
    out = y[0:r]
    for h in range(1, HEADS_PER_TILE):
        out = jnp.where(lane // HEAD_DIM == h, y[h * r:(h + 1) * r], out)
    return out


def _stacked_lse(lb):
    return jnp.concatenate([_lane_pick(lb, h) for h in range(HEADS_PER_TILE)], axis=0)


def _band_masks(n_row_blocks, n_col_blocks):
    shape = (n_row_blocks * BLOCK, n_col_blocks * BLOCK)
    qi = lax.broadcasted_iota(jnp.int32, shape, 0) % BLOCK
    kj = lax.broadcasted_iota(jnp.int32, shape, 1) % BLOCK
    return kj <= qi, kj >= qi


def _query_masks():
    same_ok, before_ok = _band_masks(HEADS_PER_TILE, 2)
    is_cur = lax.broadcasted_iota(jnp.int32, same_ok.shape, 1) >= BLOCK
    return jnp.logical_and(is_cur, same_ok), jnp.logical_and(jnp.logical_not(is_cur), before_ok)


def _dot_nt(a, b):
    return lax.dot_general(a.astype(BF16), b.astype(BF16), (((1,), (1,)), ((), ())), preferred_element_type=F32)


def _dot_nn(a, b):
    return lax.dot_general(a.astype(BF16), b.astype(BF16), (((1,), (0,)), ((), ())), preferred_element_type=F32)


def _dot_tn(a, b):
    return lax.dot_general(a.astype(BF16), b.astype(BF16), (((0,), (0,)), ((), ())), preferred_element_type=F32)


def _lane_pick(x, h):
    lane = lax.broadcasted_iota(jnp.int32, x.shape, 1)
    return jnp.sum(jnp.where(lane == h * HEAD_DIM, x, 0.0), axis=1, keepdims=True)


def _attn_specs(width, v_block, n_halo_of):
    cur = pl.BlockSpec((ATTN_TILE, LANES), lambda hb, n: (n, hb))
    nbr = pl.BlockSpec((ATTN_TILE, LANES), lambda hb, n: (n_halo_of(n), hb))
    vcur = pl.BlockSpec((ATTN_TILE, LANES), lambda hb, n: (n, v_block + hb))
    vnbr = pl.BlockSpec((ATTN_TILE, LANES), lambda hb, n: (n_halo_of(n), v_block + hb))
    return cur, nbr, vcur, vnbr


def _attn_fwd(q, k, proj, v_block, name, comm=None):
    s, width = q.shape
    n_tiles = s // ATTN_TILE
    cur, prev, vcur, vprev = _attn_specs(width, v_block, lambda n: jnp.maximum(n - 1, 0))

    def body(q_ref, k_ref, kp_ref, v_ref, vp_ref, o_ref, l_ref, kk, vv, o_s, l_s):
        n = pl.program_id(1)
        kk[0:ATTN_TILE, :] = kp_ref[...]
        kk[ATTN_TILE:, :] = k_ref[...]
        vv[0:ATTN_TILE, :] = vp_ref[...]
        vv[ATTN_TILE:, :] = v_ref[...]
        cur_valid, prev_valid = _query_masks()
        for bi, d in enumerate(DILATIONS):
            span = BLOCK * d

            def blk(idx, carry, bi=bi, d=d, span=span):
                g = idx // d
                q0 = g * span + idx % d
                rows = pl.ds(q0, BLOCK, stride=d)
                q2 = _stack_heads(q_ref[rows, :])
                keys = jnp.concatenate([kk[pl.ds(ATTN_TILE + q0 - span, BLOCK, stride=d), :],
                                        kk[pl.ds(ATTN_TILE + q0, BLOCK, stride=d), :]], axis=0)
                vals = jnp.concatenate([vv[pl.ds(ATTN_TILE + q0 - span, BLOCK, stride=d), :],
                                        vv[pl.ds(ATTN_TILE + q0, BLOCK, stride=d), :]], axis=0)
                has_prev = jnp.logical_or(n > 0, g > 0)
                valid = jnp.logical_or(cur_valid, jnp.logical_and(prev_valid, has_prev))
                sc = jnp.where(valid, _dot_nt(q2, keys), NEG)
                mx = jnp.max(sc, axis=1, keepdims=True)
                p = jnp.exp(sc - mx)
                den = jnp.sum(p, axis=1, keepdims=True)
                o_s[bi, rows, :] = _unstack_heads(_dot_nn(p, vals) / den)
                l_s[bi, rows, :] = _unstack_heads(jnp.broadcast_to(mx + jnp.log(den), (q2.shape[0], LANES)))
                return carry

            lax.fori_loop(0, ATTN_TILE // BLOCK, blk, 0, unroll=2)
        ls = [l_s[bi] for bi in range(len(DILATIONS))]
        top = functools.reduce(jnp.maximum, ls)
        ws = [jnp.exp(l - top) for l in ls]
        den = functools.reduce(lambda a, b: a + b, ws)
        num = functools.reduce(lambda a, b: a + b, [w * o_s[bi] for bi, w in enumerate(ws)])
        o_ref[...] = num / den
        l_ref[...] = top + jnp.log(den)

    return _call(
        body, grid=(width // LANES, n_tiles), in_specs=[cur, cur, prev, vcur, vprev],
        out_specs=[cur, cur], out_shape=[jax.ShapeDtypeStruct((s, width), F32)] * 2,
        scratch_shapes=[pltpu.VMEM((2 * ATTN_TILE, LANES), F32), pltpu.VMEM((2 * ATTN_TILE, LANES), F32),
                        pltpu.VMEM((len(DILATIONS), ATTN_TILE, LANES), F32),
                        pltpu.VMEM((len(DILATIONS), ATTN_TILE, LANES), F32)],
        args=(q, k, k, proj, proj), name=name, comm=comm)


def _attn_bwd_q(q, k, proj, v_block, do, o, lse, name, comm=None):
    s, width = q.shape
    n_tiles = s // ATTN_TILE
    cur, prev, vcur, vprev = _attn_specs(width, v_block, lambda n: jnp.maximum(n - 1, 0))

    def body(q_ref, k_ref, kp_ref, v_ref, vp_ref, do_ref, o_ref, l_ref, dq_ref, kk, vv):
        n = pl.program_id(1)
        kk[0:ATTN_TILE, :] = kp_ref[...]
        kk[ATTN_TILE:, :] = k_ref[...]
        vv[0:ATTN_TILE, :] = vp_ref[...]
        vv[ATTN_TILE:, :] = v_ref[...]
        dq_ref[...] = jnp.zeros_like(dq_ref)
        cur_valid, prev_valid = _query_masks()
        for d in DILATIONS:
            span = BLOCK * d

            def blk(idx, carry, d=d, span=span):
                g = idx // d
                q0 = g * span + idx % d
                rows = pl.ds(q0, BLOCK, stride=d)
                dob = do_ref[rows, :]
                q2 = _stack_heads(q_ref[rows, :])
                do2 = _stack_heads(dob)
                delta = jnp.sum(_stack_heads(dob * o_ref[rows, :]), axis=1, keepdims=True)
                lse2 = _stacked_lse(l_ref[rows, :])
                keys = jnp.concatenate([kk[pl.ds(ATTN_TILE + q0 - span, BLOCK, stride=d), :],
                                        kk[pl.ds(ATTN_TILE + q0, BLOCK, stride=d), :]], axis=0)
                vals = jnp.concatenate([vv[pl.ds(ATTN_TILE + q0 - span, BLOCK, stride=d), :],
                                        vv[pl.ds(ATTN_TILE + q0, BLOCK, stride=d), :]], axis=0)
                has_prev = jnp.logical_or(n > 0, g > 0)
                valid = jnp.logical_or(cur_valid, jnp.logical_and(prev_valid, has_prev))
                p = jnp.where(valid, jnp.exp(_dot_nt(q2, keys) - lse2), 0.0)
                ds = p * (_dot_nt(do2, vals) - delta)
                dq_ref[rows, :] += _unstack_heads(_dot_nn(ds, keys))
                return carry

            lax.fori_loop(0, ATTN_TILE // BLOCK, blk, 0, unroll=2)

    return _call(
        body, grid=(width // LANES, n_tiles), in_specs=[cur, cur, prev, vcur, vprev, cur, cur, cur],
        out_specs=[cur], out_shape=[jax.ShapeDtypeStruct((s, width), F32)],
        scratch_shapes=[pltpu.VMEM((2 * ATTN_TILE, LANES), F32), pltpu.VMEM((2 * ATTN_TILE, LANES), F32)],
        args=(q, k, k, proj, proj, do, o, lse), name=name, comm=comm)


def _attn_bwd_kv(q, k, proj, v_block, do, o, lse, name):
    s, width = q.shape
    n_tiles = s // ATTN_TILE
    cur, nxt, vcur, _ = _attn_specs(width, v_block, lambda n: jnp.minimum(n + 1, n_tiles - 1))

    def body(k_ref, v_ref, q_ref, qn_ref, do_ref, don_ref, o_ref, on_ref, l_ref, ln_ref, dk_ref, dv_ref,
             qq, dd, pr, ll):
        n = pl.program_id(1)
        qq[0:ATTN_TILE, :] = q_ref[...]
        qq[ATTN_TILE:, :] = qn_ref[...]
        dd[0:ATTN_TILE, :] = do_ref[...]
        dd[ATTN_TILE:, :] = don_ref[...]
        pr[0:ATTN_TILE, :] = do_ref[...] * o_ref[...]
        pr[ATTN_TILE:, :] = don_ref[...] * on_ref[...]
        ll[0:ATTN_TILE, :] = l_ref[...]
        ll[ATTN_TILE:, :] = ln_ref[...]
        dk_ref[...] = jnp.zeros_like(dk_ref)
        dv_ref[...] = jnp.zeros_like(dv_ref)
        same_ok, before_ok = _band_masks(2 * HEADS_PER_TILE, 1)
        is_same = lax.broadcasted_iota(jnp.int32, same_ok.shape, 0) < HEADS_PER_TILE * BLOCK
        same_valid = jnp.logical_and(is_same, same_ok)
        after_valid = jnp.logical_and(jnp.logical_not(is_same), before_ok)
        for d in DILATIONS:
            span = BLOCK * d
            n_groups = ATTN_TILE // span

            def blk(idx, carry, d=d, span=span, n_groups=n_groups):
                g = idx // d
                k0 = g * span + idx % d
                rows = pl.ds(k0, BLOCK, stride=d)
                kb = k_ref[rows, :]
                vb = v_ref[rows, :]
                here, after = pl.ds(k0, BLOCK, stride=d), pl.ds(k0 + span, BLOCK, stride=d)
                q4 = jnp.concatenate([_stack_heads(qq[here, :]), _stack_heads(qq[after, :])], axis=0)
                do4 = jnp.concatenate([_stack_heads(dd[here, :]), _stack_heads(dd[after, :])], axis=0)
                delta = jnp.sum(jnp.concatenate([_stack_heads(pr[here, :]), _stack_heads(pr[after, :])], axis=0),
                                axis=1, keepdims=True)
                lse4 = jnp.concatenate([_stacked_lse(ll[here, :]), _stacked_lse(ll[after, :])], axis=0)
                has_next = jnp.logical_or(n < n_tiles - 1, g < n_groups - 1)
                valid = jnp.logical_or(same_valid, jnp.logical_and(after_valid, has_next))
                p = jnp.where(valid, jnp.exp(_dot_nt(q4, kb) - lse4), 0.0)
                ds = p * (_dot_nt(do4, vb) - delta)
                dv_ref[rows, :] += _dot_tn(p, do4)
                dk_ref[rows, :] += _dot_tn(ds, q4)
                return carry

            lax.fori_loop(0, ATTN_TILE // BLOCK, blk, 0, unroll=2)

    return pl.pallas_call(
        body, grid=(width // LANES, n_tiles), in_specs=[cur, vcur, cur, nxt, cur, nxt, cur, nxt, cur, nxt],
        out_specs=[cur, cur], out_shape=[jax.ShapeDtypeStruct((s, width), F32)] * 2,
        scratch_shapes=[pltpu.VMEM((2 * ATTN_TILE, LANES), F32)] * 4,
        compiler_params=_params(2), name=name)(k, proj, q, q, do, do, o, o, lse, lse)


def _conv_specs(s, a_block, b_block):
    per = CONV_CHUNK // CONV_HALO
    a_cur = pl.BlockSpec((CONV_CHUNK, LANES), lambda cb, i: (i, a_block + cb))
    b_cur = pl.BlockSpec((CONV_CHUNK, LANES), lambda cb, i: (i, b_block + cb))
    a_halo = pl.BlockSpec((CONV_HALO, LANES), lambda cb, i: (jnp.maximum(i * per - 1, 0), a_block + cb))
    b_halo = pl.BlockSpec((CONV_HALO, LANES), lambda cb, i: (jnp.maximum(i * per - 1, 0), b_block + cb))
    w_spec = pl.BlockSpec((CONV_KERNEL, LANES), lambda cb, i: (0, cb))
    vec = pl.BlockSpec((1, LANES), lambda cb, i: (0, cb))
    out = pl.BlockSpec((CONV_CHUNK, LANES), lambda cb, i: (i, cb))
    return a_cur, b_cur, a_halo, b_halo, w_spec, vec, out


def _fill_glu_window(win, a_ref, b_ref, ah_ref, bh_ref, first):
    halo = ah_ref[...] * _sigmoid(bh_ref[...])
    win[0:CONV_HALO, :] = jnp.where(first, 0.0, halo)
    win[CONV_HALO:, :] = a_ref[...] * _sigmoid(b_ref[...])


def _conv_fwd(proj, a_block, b_block, w, bias, name):
    s = proj.shape[0]
    cw = w.shape[1]
    a_cur, b_cur, a_halo, b_halo, w_spec, vec, out = _conv_specs(s, a_block, b_block)
    lead = CONV_HALO - (CONV_KERNEL - 1)

    def body(a_ref, b_ref, ah_ref, bh_ref, w_ref, bias_ref, o_ref, win):
        _fill_glu_window(win, a_ref, b_ref, ah_ref, bh_ref, pl.program_id(1) == 0)
        for sub in range(CONV_CHUNK // CONV_SUB):
            base = sub * CONV_SUB
            acc = jnp.zeros((CONV_SUB, LANES), F32) + bias_ref[...]
            for j in range(CONV_KERNEL):
                acc = acc + w_ref[j:j + 1, :] * win[base + lead + j:base + lead + j + CONV_SUB, :]
            o_ref[base:base + CONV_SUB, :] = acc

    return pl.pallas_call(
        body, grid=(cw // LANES, s // CONV_CHUNK), in_specs=[a_cur, b_cur, a_halo, b_halo, w_spec, vec],
        out_specs=out, out_shape=jax.ShapeDtypeStruct((s, cw), F32),
        scratch_shapes=[pltpu.VMEM((CONV_CHUNK + CONV_HALO, LANES), F32)],
        compiler_params=_params(2), name=name)(proj, proj, proj, proj, w, bias)


def _conv_bwd(proj, a_block, b_block, w, du1, name):
    s = proj.shape[0]
    cw = w.shape[1]
    a_cur, b_cur, a_halo, b_halo, w_spec, vec, out = _conv_specs(s, a_block, b_block)
    per = CONV_CHUNK // CONV_HALO
    n_chunks = s // CONV_CHUNK
    d_next = pl.BlockSpec((CONV_HALO, LANES), lambda cb, i: (jnp.minimum((i + 1) * per, s // CONV_HALO - 1), cb))
    lead = CONV_HALO - (CONV_KERNEL - 1)

    def body(a_ref, b_ref, ah_ref, bh_ref, w_ref, d_ref, dn_ref, da_ref, db_ref, dw_ref, dbias_ref, win, dwin):
        i = pl.program_id(1)
        _fill_glu_window(win, a_ref, b_ref, ah_ref, bh_ref, i == 0)
        dwin[0:CONV_CHUNK, :] = d_ref[...]
        dwin[CONV_CHUNK:, :] = jnp.where(i == n_chunks - 1, 0.0, dn_ref[...])

        @pl.when(i == 0)
        def _():
            dw_ref[...] = jnp.zeros_like(dw_ref)
            dbias_ref[...] = jnp.zeros_like(dbias_ref)

        dbias_ref[...] += _colsum(d_ref[...])
        for sub in range(CONV_CHUNK // CONV_SUB):
            base = sub * CONV_SUB
            dcur = dwin[base:base + CONV_SUB, :]
            du0 = jnp.zeros((CONV_SUB, LANES), F32)
            for j in range(CONV_KERNEL):
                back = CONV_KERNEL - 1 - j
                du0 = du0 + w_ref[j:j + 1, :] * dwin[base + back:base + back + CONV_SUB, :]
                dw_ref[j:j + 1, :] += _colsum(dcur * win[base + lead + j:base + lead + j + CONV_SUB, :])
            av = a_ref[base:base + CONV_SUB, :]
            sig = _sigmoid(b_ref[base:base + CONV_SUB, :])
            da_ref[base:base + CONV_SUB, :] = du0 * sig
            db_ref[base:base + CONV_SUB, :] = du0 * av * sig * (1.0 - sig)

    return pl.pallas_call(
        body, grid=(cw // LANES, n_chunks), in_specs=[a_cur, b_cur, a_halo, b_halo, w_spec, out, d_next],
        out_specs=[out, out, w_spec, vec],
        out_shape=[jax.ShapeDtypeStruct((s, cw), F32), jax.ShapeDtypeStruct((s, cw), F32),
                   jax.ShapeDtypeStruct((CONV_KERNEL, cw), F32), jax.ShapeDtypeStruct((1, cw), F32)],
        scratch_shapes=[pltpu.VMEM((CONV_CHUNK + CONV_HALO, LANES), F32)] * 2,
        compiler_params=_params(2), name=name)(proj, proj, proj, proj, w, du1, du1)


def _adamw_math(w, g, m, v):
    m = ADAM_B1 * m + (1.0 - ADAM_B1) * g
    v = ADAM_B2 * v + (1.0 - ADAM_B2) * (g * g)
    m_hat = m / (1.0 - ADAM_B1 ** ADAM_STEP)
    v_hat = v / (1.0 - ADAM_B2 ** ADAM_STEP)
    delta = -ADAM_LR * (m_hat / (jnp.sqrt(v_hat) + ADAM_EPS) + ADAM_WD * w)
    return delta, m, v


def _adamw_big(w, g, m, v, name):
    rows, cols = w.shape
    tile = _tile(rows, 256, 8)
    spec = pl.BlockSpec((tile, cols), lambda i: (i, 0))

    def body(w_ref, g_ref, m_ref, v_ref, d_out, m_out, v_out):
        d_out[...], m_out[...], v_out[...] = _adamw_math(w_ref[...], g_ref[...], m_ref[...], v_ref[...])

    return pl.pallas_call(body, grid=(rows // tile,), in_specs=[spec] * 4, out_specs=[spec] * 3,
                          out_shape=[jax.ShapeDtypeStruct(w.shape, F32)] * 3, compiler_params=_params(1),
                          name=name)(w, g, m, v)


def _adamw_reduced(w, land, m, v, name):
    rows, cols = w.shape
    tile = _tile(rows, 256, 16)
    spec = pl.BlockSpec((tile, cols), lambda i: (i, 0))

    def body(w_ref, l_ref, m_ref, v_ref, g_out, d_out, m_out, v_out):
        g = l_ref[0].astype(F32)
        for q in range(1, N_CHIP):
            g = g + l_ref[q].astype(F32)
        g_out[...] = g
        d_out[...], m_out[...], v_out[...] = _adamw_math(w_ref[...], g, m_ref[...], v_ref[...])

    return pl.pallas_call(body, grid=(rows // tile,),
                          in_specs=[spec, pl.BlockSpec((N_CHIP, tile, cols), lambda i: (0, i, 0)), spec, spec],
                          out_specs=[spec] * 4, out_shape=[jax.ShapeDtypeStruct(w.shape, F32)] * 4,
                          compiler_params=_params(1), name=name)(w, land, m, v)


def _adamw_small(ws, gs, ms, vs, name):
    n = len(ws)

    def body(*refs):
        ins, outs = refs[:4 * n], refs[4 * n:]
        for t in range(n):
            res = _adamw_math(ins[t][...], ins[n + t][...], ins[2 * n + t][...], ins[3 * n + t][...])
            for j in range(3):
                outs[j * n + t][...] = res[j]

    shapes = [jax.ShapeDtypeStruct(w.shape, F32) for w in ws]
    res = pl.pallas_call(body, out_shape=shapes * 3, compiler_params=pltpu.CompilerParams(vmem_limit_bytes=VMEM_LIMIT),
                         name=name)(*ws, *gs, *ms, *vs)
    return res[:n], res[n:2 * n], res[2 * n:]


def _sum_blocks(x, n_blocks, name):
    r = x.shape[0] // n_blocks

    def body(x_ref, o_ref):
        acc = x_ref[0:r, :]
        for b in range(1, n_blocks):
            acc = acc + x_ref[b * r:(b + 1) * r, :]
        o_ref[...] = acc

    return pl.pallas_call(body, out_shape=jax.ShapeDtypeStruct((r, x.shape[1]), F32),
                          compiler_params=pltpu.CompilerParams(vmem_limit_bytes=VMEM_LIMIT), name=name)(x)


def _coords():
    return lax.axis_index("x"), lax.axis_index("y"), lax.axis_index("c")


def _flip(v, bit):
    return 1 - v if bit else v


def _ag_small(x, name):
    r, c = x.shape

    def body(x_ref, o_ref, send, recv, local_sem):
        mx, my, mc = _coords()

        def rows(px, py, pc):
            return o_ref.at[pl.ds(pl.multiple_of((4 * px + 2 * py + pc) * r, 8), r), :]

        local = pltpu.make_async_copy(x_ref, rows(mx, my, mc), local_sem)
        local.start()
        peers = [(_flip(mx, k >> 2 & 1), _flip(my, k >> 1 & 1), _flip(mc, k & 1)) for k in range(1, N_DEV)]
        sends = [pltpu.make_async_remote_copy(x_ref, rows(mx, my, mc), send.at[k], recv.at[k], device_id=p,
                                              device_id_type=MESH) for k, p in enumerate(peers)]
        for cp in sends:
            cp.start()
        for k, p in enumerate(peers):
            pltpu.make_async_remote_copy(x_ref, rows(*p), send.at[k], recv.at[k], device_id=p,
                                         device_id_type=MESH).wait_recv()
        for cp in sends:
            cp.wait_send()
        local.wait()

    vm = pl.BlockSpec(memory_space=pltpu.VMEM)
    return pl.pallas_call(
        body, in_specs=[vm], out_specs=vm, out_shape=jax.ShapeDtypeStruct((N_DEV * r, c), x.dtype),
        scratch_shapes=[pltpu.SemaphoreType.DMA((N_DEV - 1,)), pltpu.SemaphoreType.DMA((N_DEV - 1,)),
                        pltpu.SemaphoreType.DMA(())],
        name=name)(x)


class _GatherWeights:
    def __init__(self, shards):
        n_t = len(shards)
        self.inputs = list(shards)
        self.out_shapes = [jax.ShapeDtypeStruct((N_DEV * x.shape[0], x.shape[1]), x.dtype) for x in shards]
        self.scratch = [pltpu.SemaphoreType.DMA((n_t, 7)), pltpu.SemaphoreType.DMA((n_t, 7)),
                        pltpu.SemaphoreType.DMA((n_t,))]

    def _plan(self, x_refs, o_refs, sems):
        send, recv, local_sem = sems
        mx, my, mc = _coords()
        me, sibling = (mx, my, mc), (mx, my, 1 - mc)
        chips = [(1 - mx, my), (mx, 1 - my), (1 - mx, 1 - my)]

        def rows(t, px, py, pc):
            r = x_refs[t].shape[0]
            return o_refs[t].at[pl.ds(pl.multiple_of((4 * px + 2 * py + pc) * r, 8), r), :]

        def copy(t, k, block, to, src=None):
            return pltpu.make_async_remote_copy(
                src_ref=rows(t, *block) if src is None else src, dst_ref=rows(t, *block),
                send_sem=send.at[t, k], recv_sem=recv.at[t, k], device_id=to, device_id_type=MESH)

        def local(t):
            return pltpu.make_async_copy(x_refs[t], rows(t, *me), local_sem.at[t])

        return me, sibling, chips, mc, copy, local

    def start(self, x_refs, o_refs, sems):
        me, sibling, chips, mc, copy, local = self._plan(x_refs, o_refs, sems)
        for t in range(len(x_refs)):
            local(t).start()
            copy(t, 0, me, sibling, src=x_refs[t]).start()
            for j, chip in enumerate(chips):
                copy(t, 1 + j, me, (*chip, mc), src=x_refs[t]).start()

    def mid(self, x_refs, o_refs, sems):
        me, sibling, chips, mc, copy, local = self._plan(x_refs, o_refs, sems)
        for j, chip in enumerate(chips):
            for t in range(len(x_refs)):
                copy(t, 1 + j, (*chip, mc), me).wait_recv()
                copy(t, 4 + j, (*chip, mc), sibling).start()

    def finish(self, x_refs, o_refs, sems):
        me, sibling, chips, mc, copy, local = self._plan(x_refs, o_refs, sems)
        for t in range(len(x_refs)):
            copy(t, 0, sibling, me).wait_recv()
            for j, chip in enumerate(chips):
                copy(t, 4 + j, (*chip, 1 - mc), me).wait_recv()
            copy(t, 0, me, sibling, src=x_refs[t]).wait_send()
            for j, chip in enumerate(chips):
                copy(t, 1 + j, me, (*chip, mc), src=x_refs[t]).wait_send()
                copy(t, 4 + j, (*chip, mc), sibling).wait_send()
            local(t).wait()


class _SiblingExchange:
    mid = None

    def __init__(self, grads):
        n_t = len(grads)
        self.inputs = list(grads)
        self.out_shapes = [jax.ShapeDtypeStruct((N_CHIP,) + g.shape[2:], F32) for g in grads]
        self.scratch = [pltpu.SemaphoreType.DMA((n_t,)), pltpu.SemaphoreType.DMA((n_t,))]

    def _copies(self, g_refs, land, sems):
        send, recv = sems
        mx, my, mc = _coords()
        return [pltpu.make_async_remote_copy(g_refs[t].at[:, 1 - mc], land[t], send.at[t], recv.at[t],
                                             device_id=(mx, my, 1 - mc), device_id_type=MESH)
                for t in range(len(g_refs))]

    def start(self, g_refs, land, sems):
        for cp in self._copies(g_refs, land, sems):
            cp.start()

    def finish(self, g_refs, land, sems):
        for cp in self._copies(g_refs, land, sems):
            cp.wait()


class _Together:
    def __init__(self, *comms):
        self.comms = comms
        self.inputs = [x for c in comms for x in c.inputs]
        self.out_shapes = [x for c in comms for x in c.out_shapes]
        self.scratch = [x for c in comms for x in c.scratch]
        self.mid = self._mid if any(c.mid is not None for c in comms) else None

    def _each(self, phase, cin, cout, sems):
        i = o = s = 0
        for c in self.comms:
            fn = getattr(c, phase)
            ni, no, ns = len(c.inputs), len(c.out_shapes), len(c.scratch)
            if fn is not None:
                fn(cin[i:i + ni], cout[o:o + no], sems[s:s + ns])
            i, o, s = i + ni, o + no, s + ns

    def start(self, cin, cout, sems):
        self._each("start", cin, cout, sems)

    def _mid(self, cin, cout, sems):
        self._each("mid", cin, cout, sems)

    def finish(self, cin, cout, sems):
        self._each("finish", cin, cout, sems)


def _standalone(comm, name):
    def body():
        pass
    return _call(body, grid=(1,), in_specs=[], out_specs=[], out_shape=[], args=(), name=name, comm=comm)[1]


def _chip_partial(g4, land, name):
    _, _, r, c = g4.shape
    tr = _tile(r, 256, 16)

    def body(g_ref, l_ref, o_ref):
        o_ref[...] = (g_ref[...] + l_ref[...]).astype(o_ref.dtype)

    return pl.pallas_call(
        body, grid=(N_CHIP, r // tr),
        in_specs=[pl.BlockSpec((None, None, tr, c), lambda q, i: (q, lax.axis_index("c"), i, 0)),
                  pl.BlockSpec((None, tr, c), lambda q, i: (q, i, 0))],
        out_specs=pl.BlockSpec((None, tr, c), lambda q, i: (q, i, 0)),
        out_shape=jax.ShapeDtypeStruct((N_CHIP, r, c), BF16), compiler_params=_params(2), name=name)(g4, land)


class _ChipExchange:
    mid = None

    def __init__(self, parts):
        n_t = len(parts)
        self.inputs = list(parts)
        self.out_shapes = [jax.ShapeDtypeStruct(p.shape, p.dtype) for p in parts]
        self.scratch = [pltpu.SemaphoreType.DMA((n_t, 3)), pltpu.SemaphoreType.DMA((n_t, 3)),
                        pltpu.SemaphoreType.DMA((n_t,))]

    def _plan(self, p_refs, land, sems):
        send, recv, local_sem = sems
        mx, my, mc = _coords()
        my_chip = 2 * mx + my
        peers = [(_flip(mx, fx), _flip(my, fy)) for fx, fy in ((1, 0), (0, 1), (1, 1))]

        def out(t, k):
            px, py = peers[k]
            return pltpu.make_async_remote_copy(p_refs[t].at[2 * px + py], land[t].at[my_chip], send.at[t, k],
                                                recv.at[t, k], device_id=(px, py, mc), device_id_type=MESH)

        def arrival(t, k):
            px, py = peers[k]
            return pltpu.make_async_remote_copy(p_refs[t].at[my_chip], land[t].at[2 * px + py], send.at[t, k],
                                                recv.at[t, k], device_id=(px, py, mc), device_id_type=MESH)

        def local(t):
            return pltpu.make_async_copy(p_refs[t].at[my_chip], land[t].at[my_chip], local_sem.at[t])

        return out, arrival, local

    def start(self, p_refs, land, sems):
        out, arrival, local = self._plan(p_refs, land, sems)
        for t in range(len(p_refs)):
            local(t).start()
            for k in range(3):
                out(t, k).start()

    def finish(self, p_refs, land, sems):
        out, arrival, local = self._plan(p_refs, land, sems)
        for t in range(len(p_refs)):
            for k in range(3):
                arrival(t, k).wait_recv()
                out(t, k).wait_send()
            local(t).wait()


def _rope_tables(s, width):
    pos = jnp.arange(s, dtype=F32)
    inv_freq = ROPE_THETA ** (-jnp.arange(0, HEAD_DIM, 2, dtype=F32) / HEAD_DIM)
    ang = pos[:, None] * inv_freq[None, :]
    cos, sin = jnp.cos(ang), jnp.sin(ang)
    heads = width // HEAD_DIM
    return jnp.tile(jnp.concatenate([cos, cos], axis=1), (1, heads)), jnp.tile(jnp.concatenate([-sin, sin], axis=1), (1, heads))


def _pad_rows(v, rows):
    return jnp.concatenate([v, jnp.zeros((rows - 1, v.shape[1]), v.dtype)], axis=0)


def kernel(x, c, w_ada, b_ada, ffn1_norm_g, ffn1_w_gate, ffn1_w_up, ffn1_w_down, mix_norm_g, w_in, conv_dw_w, conv_dw_b, conv_ln_g, conv_ln_b, attn_out_g, conv_out_g, w_out, ffn2_norm_g, ffn2_w_gate, ffn2_w_up, ffn2_w_down, final_norm_g, loss_target, m_w_ada, m_b_ada, m_ffn1_norm_g, m_ffn1_w_gate, m_ffn1_w_up, m_ffn1_w_down, m_mix_norm_g, m_w_in, m_conv_dw_w, m_conv_dw_b, m_conv_ln_g, m_conv_ln_b, m_attn_out_g, m_conv_out_g, m_w_out, m_ffn2_norm_g, m_ffn2_w_gate, m_ffn2_w_up, m_ffn2_w_down, m_final_norm_g, v_w_ada, v_b_ada, v_ffn1_norm_g, v_ffn1_w_gate, v_ffn1_w_up, v_ffn1_w_down, v_mix_norm_g, v_w_in, v_conv_dw_w, v_conv_dw_b, v_conv_ln_g, v_conv_ln_b, v_attn_out_g, v_conv_out_g, v_w_out, v_ffn2_norm_g, v_ffn2_w_gate, v_ffn2_w_up, v_ffn2_w_down, v_final_norm_g):
    mx, my, mc = _coords()
    me = 4 * mx + 2 * my + mc
    s, d = x.shape[1], x.shape[2]
    aw = d // 2
    x2, target = x[0], loss_target[0]
    n_mod = w_ada.shape[2] * N_DEV // d
    mod_cols = w_ada.shape[2]

    cw_shard = conv_dw_w.shape[3]
    n_taps = CONV_KERNEL * cw_shard
    first_len = -(-(d + n_taps) // LANES) * LANES
    first = jnp.concatenate([c, conv_dw_w[0, :, 0, :].reshape(1, n_taps), jnp.zeros((1, first_len - d - n_taps), F32)], axis=1)
    first_all = _ag_small(_pad_rows(first, 8), "ag_c_taps")[0::8]
    c_all = first_all[:, :d]
    conv_w = first_all[:, d:d + n_taps].reshape(N_DEV, CONV_KERNEL, cw_shard).transpose(1, 0, 2).reshape(CONV_KERNEL, aw)

    silu_c = _silu_rows(c_all, "silu_c")
    mod_part = _plain_mm([(silu_c, w_ada[0])], F32, False, mod_cols, "mod_mm")
    mod_all = _ag_small(mod_part, "ag_mod").reshape(N_DEV, N_DEV, mod_cols)
    mod = lax.dynamic_index_in_dim(mod_all, me, axis=1, keepdims=False).reshape(1, n_mod * d) + b_ada
    sh1, sc1, g1, sh2, sc2, g2, sh3, sc3, g3 = [mod[:, i * d:(i + 1) * d] for i in range(n_mod)]

    def shard(w, transpose):
        return (w[0].T if transpose else w[0]).astype(BF16)

    def split(g):
        return g.reshape(N_CHIP, 2, g.shape[0] // N_DEV, g.shape[1])

    def partials(g4s, lands, tag):
        return [_chip_partial(a, b, "chip_partial_%s%d" % (tag, t)) for t, (a, b) in enumerate(zip(g4s, lands))]

    wg1, wu1 = _standalone(_GatherWeights([shard(ffn1_w_gate, True), shard(ffn1_w_up, True)]), "ag_ffn1_in")
    gather_late = _GatherWeights([shard(ffn2_w_gate, True), shard(ffn2_w_up, True), shard(ffn2_w_down, False),
                                  shard(w_out, False)])

    n1 = _norm_mod_fwd(x2, ffn1_norm_g, sc1, sh1, "norm1")
    (a1, b1, hid1), (wd1,) = _ffn_up(n1, wg1, wu1, "ffn1_up", comm=_GatherWeights([shard(ffn1_w_down, False)]))
    (h1, f1), (win_t,) = _residual_mm(hid1, wd1, x2, g1, 0.5, "ffn1_down", comm=_GatherWeights([shard(w_in, True)]))
    n2 = _norm_mod_fwd(h1, mix_norm_g, sc2, sh2, "norm2")
    proj = _plain_mm([(n2, win_t)], F32, True, _tile(5 * aw, 1536, LANES), "proj")
    cos, sin_signed = _rope_tables(s, aw)
    q_rot, k_rot = _rope_fwd(proj, cos, sin_signed, aw, "rope")
    lanes_per = aw // LANES
    (attn, lse), (wg2, wu2, wd2, wout) = _attn_fwd(q_rot, k_rot, proj, 2 * lanes_per, "attn_fwd", comm=gather_late)
    u1 = _conv_fwd(proj, 3 * lanes_per, 4 * lanes_per, conv_w, conv_dw_b, "conv_fwd")
    y = _mix_post_fwd(attn, u1, attn_out_g, conv_ln_g, conv_ln_b, conv_out_g, "mix_post")
    h2, mix = _residual_mm(y, wout, h1, g2, 1.0, "mix_out")
    n3 = _norm_mod_fwd(h2, ffn2_norm_g, sc3, sh3, "norm3")
    a3, b3, hid3 = _ffn_up(n3, wg2, wu2, "ffn2_up")
    h3, f3 = _residual_mm(hid3, wd2, h2, g3, 0.5, "ffn2_down")

    dh3, err2, d_final_g = _final_loss(h3, target, final_norm_g.reshape(1, d), "final_loss")
    loss = lax.psum(0.5 * jnp.sum(err2) / d, ("x", "y", "c"))

    df3, dg3 = _gate_bwd(dh3, f3, g3, 0.5, "gate3_bwd")
    da3, db3 = _ffn_bwd_hidden(df3, wd2, a3, b3, "ffn2_hidden_bwd")
    g4_a = [split(_mm_tn(da3, n3, "ffn2_dwg")), split(_mm_tn(db3, n3, "ffn2_dwu")), split(_mm_tn(hid3, df3, "ffn2_dwd"))]
    dn3, land_a = _plain_mm([(da3, wg2), (db3, wu2)], F32, False, d, "ffn2_dn", tm=256, comm=_SiblingExchange(g4_a))
    parts_a = partials(g4_a, land_a, "a")
    dh2, dsh3, dsc3, dgn3 = _norm_mod_bwd(dn3, h2, dh3, ffn2_norm_g, sc3, "norm3_bwd")

    dmix, dg2 = _gate_bwd(dh2, mix, g2, 1.0, "gate2_bwd")
    dy = _plain_mm([(dmix, wout)], F32, True, d, "mix_dy")
    g_wout = _mm_tn(y, dmix, "mix_dwout")
    dattn, du1, d_attn_g, d_conv_g, d_ln_g, d_ln_b = _mix_post_bwd(
        dy, attn, u1, attn_out_g, conv_ln_g, conv_ln_b, conv_out_g, "mix_post_bwd")
    dga, dgb, d_taps, d_conv_b = _conv_bwd(proj, 3 * lanes_per, 4 * lanes_per, conv_w, du1, "conv_bwd")
    (dq,), sums_a = _attn_bwd_q(q_rot, k_rot, proj, 2 * lanes_per, dattn, attn, lse, "attn_bwd_q",
                                comm=_ChipExchange(parts_a))
    dk, dv = _attn_bwd_kv(q_rot, k_rot, proj, 2 * lanes_per, dattn, attn, lse, "attn_bwd_kv")
    dproj = _dproj_assemble(dq, dk, dv, dga, dgb, cos, sin_signed, "dproj")
    dn2 = _plain_mm([(dproj, win_t)], F32, False, d, "mix_dn")
    g4_b = [split(g_wout), split(_mm_tn(dproj, n2, "mix_dwin"))]
    (dh1, dsh2, dsc2, dgn2), land_b = _norm_mod_bwd(dn2, h1, dh2, mix_norm_g, sc2, "norm2_bwd",
                                                    comm=_SiblingExchange(g4_b))
    parts_b = partials(g4_b, land_b, "b")

    df1, dg1 = _gate_bwd(dh1, f1, g1, 0.5, "gate1_bwd")
    g4_c = [split(_mm_tn(hid1, df1, "ffn1_dwd"))]
    (da1, db1), both = _ffn_bwd_hidden(df1, wd1, a1, b1, "ffn1_hidden_bwd",
                                       comm=_Together(_ChipExchange(parts_b), _SiblingExchange(g4_c)))
    sums_b, land_c = both[:2], both[2:]
    parts_c = partials(g4_c, land_c, "c")
    g_wu1, sums_c = _mm_tn(db1, n1, "ffn1_dwu", comm=_ChipExchange(parts_c))
    g4_d = [split(g_wu1)]
    g_wg1, land_d = _mm_tn(da1, n1, "ffn1_dwg", comm=_SiblingExchange(g4_d))
    parts_d = partials(g4_d, land_d, "d")
    g4_e = [split(g_wg1)]
    dn1, both = _plain_mm([(da1, wg1), (db1, wu1)], F32, False, d, "ffn1_dn", tm=256,
                          comm=_Together(_ChipExchange(parts_d), _SiblingExchange(g4_e)))
    sums_d, land_e = both[:1], both[1:]
    parts_e = partials(g4_e, land_e, "e")
    (dx, dsh1, dsc1, dgn1), sums_e = _norm_mod_bwd(dn1, x2, dh1, ffn1_norm_g, sc1, "norm1_bwd",
                                                   comm=_ChipExchange(parts_e))

    dmod = jnp.concatenate([dsh1, dsc1, dg1, dsh2, dsc2, dg2, dsh3, dsc3, dg3], axis=1)
    small = [dmod, dgn1, dgn2, dgn3, d_final_g, d_conv_b, d_ln_g, d_ln_b, d_attn_g, d_conv_g,
             d_taps.reshape(1, CONV_KERNEL * aw)]
    sizes = [v.shape[1] for v in small]
    total = sum(sizes)
    padded = -(-total // (8 * LANES)) * (8 * LANES)
    packed = jnp.concatenate(small + [jnp.zeros((1, padded - total), F32)], axis=1).reshape(8, padded // 8)
    gathered = _ag_small(packed, "ag_small_grads")
    summed = _sum_blocks(gathered, N_DEV, "sum_small_grads").reshape(1, padded)
    offs = [sum(sizes[:i]) for i in range(len(sizes))]
    (g_b_ada, g_gn1, g_gn2, g_gn3, g_final, g_conv_b, g_ln_g, g_ln_b, g_attn_g, g_conv_g, g_taps) = [
        summed[:, o:o + n] for o, n in zip(offs, sizes)]
    g_taps_shard = lax.dynamic_slice_in_dim(g_taps.reshape(CONV_KERNEL, aw), me * cw_shard, cw_shard, axis=1)
    dmod_all = gathered.reshape(N_DEV, padded)[:, :n_mod * d]
    dmod_cols = lax.dynamic_slice_in_dim(dmod_all, me * mod_cols, mod_cols, axis=1)
    g_w_ada = _mm_tn(silu_c, dmod_cols, "ada_dw")

    arrived = dict(zip(["ffn2_w_gate", "ffn2_w_up", "ffn2_w_down", "w_out", "w_in", "ffn1_w_down", "ffn1_w_up",
                        "ffn1_w_gate"], list(sums_a) + list(sums_b) + list(sums_c) + list(sums_d) + list(sums_e)))
    transposed = ("ffn1_w_gate", "ffn1_w_up", "w_in", "ffn2_w_gate", "ffn2_w_up")
    grads = {
        "w_ada": g_w_ada, "b_ada": g_b_ada, "ffn1_norm_g": g_gn1, "mix_norm_g": g_gn2, "conv_dw_w": g_taps_shard,
        "conv_dw_b": g_conv_b, "conv_ln_g": g_ln_g, "conv_ln_b": g_ln_b, "attn_out_g": g_attn_g,
        "conv_out_g": g_conv_g, "ffn2_norm_g": g_gn3, "final_norm_g": g_final,
    }
    weights = dict(w_ada=w_ada, b_ada=b_ada, ffn1_norm_g=ffn1_norm_g, ffn1_w_gate=ffn1_w_gate, ffn1_w_up=ffn1_w_up, ffn1_w_down=ffn1_w_down, mix_norm_g=mix_norm_g, w_in=w_in, conv_dw_w=conv_dw_w, conv_dw_b=conv_dw_b, conv_ln_g=conv_ln_g, conv_ln_b=conv_ln_b, attn_out_g=attn_out_g, conv_out_g=conv_out_g, w_out=w_out, ffn2_norm_g=ffn2_norm_g, ffn2_w_gate=ffn2_w_gate, ffn2_w_up=ffn2_w_up, ffn2_w_down=ffn2_w_down, final_norm_g=final_norm_g)
    moms = dict(w_ada=m_w_ada, b_ada=m_b_ada, ffn1_norm_g=m_ffn1_norm_g, ffn1_w_gate=m_ffn1_w_gate, ffn1_w_up=m_ffn1_w_up, ffn1_w_down=m_ffn1_w_down, mix_norm_g=m_mix_norm_g, w_in=m_w_in, conv_dw_w=m_conv_dw_w, conv_dw_b=m_conv_dw_b, conv_ln_g=m_conv_ln_g, conv_ln_b=m_conv_ln_b, attn_out_g=m_attn_out_g, conv_out_g=m_conv_out_g, w_out=m_w_out, ffn2_norm_g=m_ffn2_norm_g, ffn2_w_gate=m_ffn2_w_gate, ffn2_w_up=m_ffn2_w_up, ffn2_w_down=m_ffn2_w_down, final_norm_g=m_final_norm_g)
    vars_ = dict(w_ada=v_w_ada, b_ada=v_b_ada, ffn1_norm_g=v_ffn1_norm_g, ffn1_w_gate=v_ffn1_w_gate, ffn1_w_up=v_ffn1_w_up, ffn1_w_down=v_ffn1_w_down, mix_norm_g=v_mix_norm_g, w_in=v_w_in, conv_dw_w=v_conv_dw_w, conv_dw_b=v_conv_dw_b, conv_ln_g=v_conv_ln_g, conv_ln_b=v_conv_ln_b, attn_out_g=v_attn_out_g, conv_out_g=v_conv_out_g, w_out=v_w_out, ffn2_norm_g=v_ffn2_norm_g, ffn2_w_gate=v_ffn2_w_gate, ffn2_w_up=v_ffn2_w_up, ffn2_w_down=v_ffn2_w_down, final_norm_g=v_final_norm_g)
    names = list(weights)
    big = ["w_ada", "ffn1_w_gate", "ffn1_w_up", "ffn1_w_down", "w_in", "w_out", "ffn2_w_gate", "ffn2_w_up",
           "ffn2_w_down"]
    shape2 = {n: (weights[n].shape[-2] if weights[n].ndim > 1 else 1, weights[n].shape[-1]) for n in names}
    shape2["conv_dw_w"] = (CONV_KERNEL, cw_shard)
    g_out, d_out, m_out, v_out = {}, {}, {}, {}
    for n in big:
        if n in arrived:
            def view(t, n=n):
                return t[0].T if n in transposed else t[0]
            res = _adamw_reduced(view(weights[n]), arrived[n], view(moms[n]), view(vars_[n]), "adamw_" + n)
            g_out[n], d_out[n], m_out[n], v_out[n] = [r.T if n in transposed else r for r in res]
        else:
            g2d = grads[n].reshape(shape2[n])
            res = _adamw_big(weights[n].reshape(shape2[n]), g2d, moms[n].reshape(shape2[n]),
                             vars_[n].reshape(shape2[n]), "adamw_" + n)
            g_out[n], (d_out[n], m_out[n], v_out[n]) = g2d, res
    rest = [n for n in names if n not in big]
    res = _adamw_small([weights[n].reshape(shape2[n]) for n in rest], [grads[n].reshape(shape2[n]) for n in rest],
                       [moms[n].reshape(shape2[n]) for n in rest], [vars_[n].reshape(shape2[n]) for n in rest],
                       "adamw_small")
    for i, n in enumerate(rest):
        g_out[n], d_out[n], m_out[n], v_out[n] = grads[n], res[0][i], res[1][i], res[2][i]

    def shaped(table):
        return [table[n].reshape(weights[n].shape) for n in names]

    return (loss, dx.reshape(x.shape), *shaped(g_out), *shaped(d_out), *shaped(m_out), *shaped(v_out))
```

```python
import functools

import jax
import jax.numpy as jnp
from jax import lax
from jax.experimental import pallas as pl
from jax.experimental.pallas import tpu as pltpu

F32 = jnp.float32
BF16 = jnp.bfloat16
MESH = pl.DeviceIdType.MESH
ANY = pl.BlockSpec(memory_space=pl.ANY)

N_DEV = 8
N_CHIP = 4
HEAD_DIM = 64
HALF_HEAD = HEAD_DIM // 2
LANES = 128
BLOCK = 128
DILATIONS = (1, 4, 16)
ATTN_TILE = BLOCK * max(DILATIONS)
ROPE_THETA = 10000.0
CONV_KERNEL = 31
CONV_HALO = 32
CONV_CHUNK = 512
CONV_SUB = 128
RMS_EPS = 1e-6
LN_EPS = 1e-5
ADAM_LR = 0.001
ADAM_B1 = 0.9
ADAM_B2 = 0.999
ADAM_EPS = 1e-08
ADAM_WD = 0.01
ADAM_STEP = 10
VMEM_LIMIT = 56 * 1024 * 1024
NEG = -1e30


def _params(n_axes):
    return pltpu.CompilerParams(dimension_semantics=("arbitrary",) * n_axes, vmem_limit_bytes=VMEM_LIMIT)


def _tile(n, target, unit):
    best = None
    for t in range(unit, min(n, target) + 1, unit):
        if n % t == 0:
            best = t
    return best if best is not None else n


def _sigmoid(x):
    return 0.5 * (jnp.tanh(0.5 * x) + 1.0)


def _call(body, *, grid, in_specs, out_specs, out_shape, args, name, scratch_shapes=(), comm=None):
    params = _params(len(grid))
    if comm is None:
        return pl.pallas_call(body, grid=grid, in_specs=list(in_specs), out_specs=list(out_specs),
                              out_shape=list(out_shape), scratch_shapes=list(scratch_shapes),
                              compiler_params=params, name=name)(*args)
    n_in, n_out, n_scr = len(args), len(out_shape), len(scratch_shapes)
    c_in, c_out = len(comm.inputs), len(comm.out_shapes)
    steps = 1
    for g in grid:
        steps *= g

    def hosted(*refs):
        pos = 0
        parts = []
        for size in (n_in, c_in, n_out, c_out, n_scr, len(comm.scratch)):
            parts.append(refs[pos:pos + size])
            pos += size
        ins, cin, outs, cout, scr, cscr = parts
        step = 0
        for axis, g in enumerate(grid):
            step = step * g + pl.program_id(axis)

        @pl.when(step == 0)
        def _():
            comm.start(cin, cout, cscr)

        body(*ins, *outs, *scr)
        if comm.mid is not None and steps >= 4:
            @pl.when(step == (3 * steps) // 4)
            def _():
                comm.mid(cin, cout, cscr)

        @pl.when(step == steps - 1)
        def _():
            if comm.mid is not None and steps < 4:
                comm.mid(cin, cout, cscr)
            comm.finish(cin, cout, cscr)

    res = pl.pallas_call(
        hosted, grid=grid, in_specs=list(in_specs) + [ANY] * c_in, out_specs=list(out_specs) + [ANY] * c_out,
        out_shape=list(out_shape) + list(comm.out_shapes), scratch_shapes=list(scratch_shapes) + list(comm.scratch),
        compiler_params=params, name=name)(*args, *comm.inputs)
    return res[:n_out], res[n_out:]


def _rows(fn, rows_in, vecs_in, rows_out, vecs_out, *, tile, name, comm=None):
    norm = [r if isinstance(r, tuple) else (r, r.shape[1], 0) for r in rows_in]
    n_rows = norm[0][0].shape[0]
    n_tiles = n_rows // tile
    in_specs, args = [], []
    for arr, width, cb in norm:
        in_specs.append(pl.BlockSpec((tile, width), functools.partial(lambda i, cb: (i, cb), cb=cb)))
        args.append(arr)
    for v in vecs_in:
        in_specs.append(pl.BlockSpec((1, v.shape[1]), lambda i: (0, 0)))
        args.append(v)
    out_shape = [jax.ShapeDtypeStruct((n_rows, w), dt) for w, dt in rows_out]
    out_shape += [jax.ShapeDtypeStruct((1, w), F32) for w in vecs_out]
    out_specs = [pl.BlockSpec((tile, w), lambda i: (i, 0)) for w, _ in rows_out]
    out_specs += [pl.BlockSpec((1, w), lambda i: (0, 0)) for w in vecs_out]
    n_in, n_ro = len(args), len(rows_out)

    def body(*refs):
        vals = [r[...] for r in refs[:n_in]]
        outs = refs[n_in:]
        row_vals, vec_vals = fn(*vals)
        for ref, val in zip(outs[:n_ro], row_vals):
            if isinstance(val, tuple):
                w = val[0].shape[1]
                for j, piece in enumerate(val):
                    ref[:, j * w:(j + 1) * w] = piece.astype(ref.dtype)
            else:
                ref[...] = val.astype(ref.dtype)
        if vecs_out:
            @pl.when(pl.program_id(0) == 0)
            def _():
                for ref in outs[n_ro:]:
                    ref[...] = jnp.zeros_like(ref)
            for ref, val in zip(outs[n_ro:], vec_vals):
                ref[...] += val

    return _call(body, grid=(n_tiles,), in_specs=in_specs, out_specs=out_specs, out_shape=out_shape, args=args,
                 name=name, comm=comm)


def _colsum(x):
    return jnp.sum(x, axis=0, keepdims=True)


def _rms_stats(h):
    r = lax.rsqrt(jnp.mean(h * h, axis=-1, keepdims=True) + RMS_EPS)
    return r, h * r


def _rms_back(r, xn, dxn):
    return r * (dxn - xn * jnp.mean(dxn * xn, axis=-1, keepdims=True))


def _norm_mod_fwd(h, gain, scale, shift, name):
    def fn(h, gain, scale, shift):
        _, xn = _rms_stats(h)
        return [(xn * gain) * (1.0 + scale) + shift], []
    return _rows(fn, [h], [gain, scale, shift], [(h.shape[1], BF16)], [], tile=512, name=name)[0]


def _branch_back(dh, f, gate, coef):
    return (coef * gate) * dh, coef * _colsum(f.astype(F32) * dh)


def _norm_mod_bwd(dn, h, dh_in, gain, scale, name, branch=None, comm=None):
    d = h.shape[1]

    def back(dn, h, dh_in, gain, scale):
        r, xn = _rms_stats(h)
        y = xn * gain
        dy = dn * (1.0 + scale)
        dh = dh_in + _rms_back(r, xn, dy * gain)
        return dh, [_colsum(dn), _colsum(dn * y), _colsum(dy * xn)]

    if branch is None:
        def fn(dn, h, dh_in, gain, scale):
            dh, vecs = back(dn, h, dh_in, gain, scale)
            return [dh], vecs
        return _rows(fn, [dn, h, dh_in], [gain, scale], [(d, F32)], [d, d, d], tile=256, name=name, comm=comm)
    f, gate, coef = branch

    def fn_branch(dn, h, dh_in, f, gain, scale, gate):
        dh, vecs = back(dn, h, dh_in, gain, scale)
        df, dgate = _branch_back(dh, f, gate, coef)
        return [dh, df], vecs + [dgate]
    return _rows(fn_branch, [dn, h, dh_in, f], [gain, scale, gate], [(d, F32), (d, BF16)], [d, d, d, d], tile=256,
                 name=name, comm=comm)


def _final_loss(h, target, gain, f, gate, coef, name):
    d = h.shape[1]

    def fn(h, target, f, gain, gate):
        r, xn = _rms_stats(h)
        err = xn * gain - target
        dout = err * (1.0 / d)
        dh = _rms_back(r, xn, dout * gain)
        df, dgate = _branch_back(dh, f, gate, coef)
        return [dh, df], [_colsum(err * err), _colsum(dout * xn), dgate]
    return _rows(fn, [h, target, f], [gain, gate], [(d, F32), (d, BF16)], [d, d, d], tile=256, name=name)


def _partner(x):
    width = x.shape[1]
    lane = lax.broadcasted_iota(jnp.int32, x.shape, 1) % HEAD_DIM
    return jnp.where(lane < HALF_HEAD, pltpu.roll(x, width - HALF_HEAD, 1), pltpu.roll(x, HALF_HEAD, 1))


def _rope_fwd(proj, cos, sin_signed, width, name):
    qscale = HEAD_DIM ** -0.5

    def fn(q, k, cos, sin):
        qr = q * cos + _partner(q) * sin
        kr = k * cos + _partner(k) * sin
        return [qr * qscale, kr], []
    return _rows(fn, [(proj, width, 0), (proj, width, 1), cos, sin_signed], [], [(width, F32), (width, F32)], [],
                 tile=512, name=name)


def _dproj_assemble(dq, dk, dv, dga, dgb, cos, sin_signed, name):
    width = dq.shape[1]
    qscale = HEAD_DIM ** -0.5

    def fn(dq, dk, dv, dga, dgb, cos, sin):
        dq0 = (dq * cos - _partner(dq) * sin) * qscale
        dk0 = dk * cos - _partner(dk) * sin
        return [(dq0, dk0, dv, dga, dgb)], []
    return _rows(fn, [dq, dk, dv, dga, dgb, cos, sin_signed], [], [(5 * width, BF16)], [], tile=256, name=name)[0]


def _mix_post_fwd(attn, u1, attn_g, ln_g, ln_b, conv_g, name):
    def fn(attn, u1, attn_g, ln_g, ln_b, conv_g):
        _, xa = _rms_stats(attn)
        mu = jnp.mean(u1, axis=-1, keepdims=True)
        xc = u1 - mu
        rstd = lax.rsqrt(jnp.mean(xc * xc, axis=-1, keepdims=True) + LN_EPS)
        u2 = (xc * rstd) * ln_g + ln_b
        u3 = u2 * _sigmoid(u2)
        _, x3 = _rms_stats(u3)
        return [(xa * attn_g, x3 * conv_g)], []
    w = attn.shape[1]
    return _rows(fn, [attn, u1], [attn_g, ln_g, ln_b, conv_g], [(2 * w, BF16)], [], tile=512, name=name)[0]


def _mix_post_bwd(dy, attn, u1, attn_g, ln_g, ln_b, conv_g, name):
    w = attn.shape[1]

    def fn(dya, dyc, attn, u1, attn_g, ln_g, ln_b, conv_g):
        ra, xa = _rms_stats(attn)
        dattn = _rms_back(ra, xa, dya * attn_g)
        mu = jnp.mean(u1, axis=-1, keepdims=True)
        xc = u1 - mu
        rstd = lax.rsqrt(jnp.mean(xc * xc, axis=-1, keepdims=True) + LN_EPS)
        xh = xc * rstd
        u2 = xh * ln_g + ln_b
        sig = _sigmoid(u2)
        u3 = u2 * sig
        r3, x3 = _rms_stats(u3)
        du3 = _rms_back(r3, x3, dyc * conv_g)
        du2 = du3 * (sig + u3 * (1.0 - sig))
        dxh = du2 * ln_g
        du1 = rstd * (dxh - jnp.mean(dxh, axis=-1, keepdims=True) - xh * jnp.mean(dxh * xh, axis=-1, keepdims=True))
        return [dattn, du1], [_colsum(dya * xa), _colsum(dyc * x3), _colsum(du2 * xh), _colsum(du2)]
    return _rows(fn, [(dy, w, 0), (dy, w, 1), attn, u1], [attn_g, ln_g, ln_b, conv_g], [(w, F32), (w, F32)],
                 [w, w, w, w], tile=256, name=name)


def _silu_rows(c_all, name):
    def fn(c):
        return [c * _sigmoid(c)], []
    return _rows(fn, [c_all], [], [(c_all.shape[1], BF16)], [], tile=c_all.shape[0], name=name)[0]


def _mm(groups, epi, extras, vecs, outs, *, trans_rhs, tm, tn, name, comm=None):
    m = groups[0][0][0].shape[0]
    n = groups[0][0][1].shape[0] if trans_rhs else groups[0][0][1].shape[1]
    tm, tn = min(tm, m), min(tn, n)
    in_specs, args = [], []
    for grp in groups:
        for lhs, rhs in grp:
            k = lhs.shape[1]
            in_specs.append(pl.BlockSpec((tm, k), lambda j, i: (i, 0)))
            in_specs.append(pl.BlockSpec((tn, k), lambda j, i: (j, 0)) if trans_rhs
                            else pl.BlockSpec((k, tn), lambda j, i: (0, j)))
            args += [lhs, rhs]
    for e in extras:
        in_specs.append(pl.BlockSpec((tm, tn), lambda j, i: (i, j)))
        args.append(e)
    for v in vecs:
        in_specs.append(pl.BlockSpec((1, tn), lambda j, i: (0, j)))
        args.append(v)
    sizes = [len(g) for g in groups]
    n_mm, n_ex, n_vec = 2 * sum(sizes), len(extras), len(vecs)
    dims = (((1,), (1,)), ((), ())) if trans_rhs else (((1,), (0,)), ((), ()))

    def body(*refs):
        accs, pos = [], 0
        for size in sizes:
            acc = None
            for _ in range(size):
                part = lax.dot_general(refs[pos][...].astype(BF16), refs[pos + 1][...].astype(BF16), dims,
                                       preferred_element_type=F32)
                acc = part if acc is None else acc + part
                pos += 2
            accs.append(acc)
        ex = [r[...] for r in refs[n_mm:n_mm + n_ex]]
        vc = [r[...] for r in refs[n_mm + n_ex:n_mm + n_ex + n_vec]]
        for ref, val in zip(refs[n_mm + n_ex + n_vec:], epi(accs, ex, vc)):
            ref[...] = val.astype(ref.dtype)

    return _call(body, grid=(n // tn, m // tm), in_specs=in_specs,
                 out_specs=[pl.BlockSpec((tm, tn), lambda j, i: (i, j)) for _ in outs],
                 out_shape=[jax.ShapeDtypeStruct((m, n), dt) for dt in outs], args=args, name=name, comm=comm)


def _mm_tn(lhs, rhs, name, comm=None):
    t, a = lhs.shape
    b = rhs.shape[1]
    ta = a if a <= 1536 else _tile(a, 1536, LANES)
    tk = _tile(t, 512, 8)

    def body(l_ref, r_ref, o_ref):
        @pl.when(pl.program_id(1) == 0)
        def _():
            o_ref[...] = jnp.zeros_like(o_ref)
        o_ref[...] += lax.dot_general(l_ref[...].astype(BF16), r_ref[...].astype(BF16), (((0,), (0,)), ((), ())),
                                      preferred_element_type=F32)

    res = _call(body, grid=(a // ta, t // tk),
                in_specs=[pl.BlockSpec((tk, ta), lambda i, k: (k, i)), pl.BlockSpec((tk, b), lambda i, k: (k, 0))],
                out_specs=[pl.BlockSpec((ta, b), lambda i, k: (i, 0))], out_shape=[jax.ShapeDtypeStruct((a, b), F32)],
                args=(lhs, rhs), name=name, comm=comm)
    return res[0] if comm is None else (res[0][0], res[1])


def _ffn_tn(f):
    return _tile(f, 1536, LANES)


def _ffn_up(n, wg_t, wu_t, name, comm=None):
    def epi(accs, ex, vc):
        a, b = accs
        return [a, b, (a * _sigmoid(a)) * b]
    return _mm([[(n, wg_t)], [(n, wu_t)]], epi, [], [], [BF16, BF16, BF16], trans_rhs=True, tm=256,
               tn=_ffn_tn(wg_t.shape[0]), name=name, comm=comm)


def _residual_mm(lhs, w, res, gate, coef, name, norm=None, comm=None):
    def epi(accs, ex, vc):
        h = ex[0] + (coef * vc[0]) * accs[0]
        if norm is None:
            return [h, accs[0]]
        _, xn = _rms_stats(h)
        return [h, accs[0], (xn * vc[1]) * (1.0 + vc[2]) + vc[3]]
    vecs = [gate] + (list(norm) if norm is not None else [])
    outs = [F32, BF16] + ([BF16] if norm is not None else [])
    return _mm([[(lhs, w)]], epi, [res], vecs, outs, trans_rhs=False, tm=512, tn=w.shape[1], name=name, comm=comm)


def _ffn_bwd_hidden(df, wd, a, b, name, comm=None):
    def epi(accs, ex, vc):
        dh = accs[0]
        av, bv = ex[0].astype(F32), ex[1].astype(F32)
        sig = _sigmoid(av)
        silu = av * sig
        return [dh * bv * (sig + silu * (1.0 - sig)), dh * silu]
    return _mm([[(df, wd)]], epi, [a, b], [], [BF16, BF16], trans_rhs=True, tm=256, tn=_ffn_tn(wd.shape[0]),
               name=name, comm=comm)


def _plain_mm(pairs, out_dtype, trans_rhs, tn, name, tm=512, comm=None):
    def epi(accs, ex, vc):
        return [accs[0]]
    res = _mm([pairs], epi, [], [], [out_dtype], trans_rhs=trans_rhs, tm=tm, tn=tn, name=name, comm=comm)
    return res[0] if comm is None else (res[0][0], res[1])


HEADS_PER_TILE = LANES // HEAD_DIM


def _stack_heads(x):
    lane = lax.broadcasted_iota(jnp.int32, (1, LANES), 1)
    return jnp.concatenate([x * (lane // HEAD_DIM == h).astype(F32) for h in range(HEADS_PER_TILE)], axis=0)


def _unstack_heads(y):
    r = y.shape[0] // HEADS_PER_TILE
    lane = lax.broadcasted_iota(jnp.int32, (r, y.shape[1]), 1)
    out = y[0:r]
    for h in range(1, HEADS_PER_TILE):
        out = jnp.where(lane // HEAD_DIM == h, y[h * r:(h + 1) * r], out)
    return out


def _stacked_lse(lb):
    return jnp.concatenate([_lane_pick(lb, h) for h in range(HEADS_PER_TILE)], axis=0)


def _band_masks(n_row_blocks, n_col_blocks):
    shape = (n_row_blocks * BLOCK, n_col_blocks * BLOCK)
    qi = lax.broadcasted_iota(jnp.int32, shape, 0) % BLOCK
    kj = lax.broadcasted_iota(jnp.int32, shape, 1) % BLOCK
    return kj <= qi, kj >= qi


def _query_masks():
    same_ok, before_ok = _band_masks(HEADS_PER_TILE, 2)
    is_cur = lax.broadcasted_iota(jnp.int32, same_ok.shape, 1) >= BLOCK
    return jnp.logical_and(is_cur, same_ok), jnp.logical_and(jnp.logical_not(is_cur), before_ok)


def _dot_nt(a, b):
    return lax.dot_general(a.astype(BF16), b.astype(BF16), (((1,), (1,)), ((), ())), preferred_element_type=F32)


def _dot_nn(a, b):
    return lax.dot_general(a.astype(BF16), b.astype(BF16), (((1,), (0,)), ((), ())), preferred_element_type=F32)


def _dot_tn(a, b):
    return lax.dot_general(a.astype(BF16), b.astype(BF16), (((0,), (0,)), ((), ())), preferred_element_type=F32)


def _lane_pick(x, h):
    lane = lax.broadcasted_iota(jnp.int32, x.shape, 1)
    return jnp.sum(jnp.where(lane == h * HEAD_DIM, x, 0.0), axis=1, keepdims=True)


def _attn_specs(width, v_block, n_halo_of):
    cur = pl.BlockSpec((ATTN_TILE, LANES), lambda hb, n: (n, hb))
    nbr = pl.BlockSpec((ATTN_TILE, LANES), lambda hb, n: (n_halo_of(n), hb))
    vcur = pl.BlockSpec((ATTN_TILE, LANES), lambda hb, n: (n, v_block + hb))
    vnbr = pl.BlockSpec((ATTN_TILE, LANES), lambda hb, n: (n_halo_of(n), v_block + hb))
    return cur, nbr, vcur, vnbr


def _attn_fwd(q, k, proj, v_block, name, comm=None):
    s, width = q.shape
    n_tiles = s // ATTN_TILE
    cur, prev, vcur, vprev = _attn_specs(width, v_block, lambda n: jnp.maximum(n - 1, 0))

    def body(q_ref, k_ref, kp_ref, v_ref, vp_ref, o_ref, l_ref, kk, vv, o_s, l_s):
        n = pl.program_id(1)
        kk[0:ATTN_TILE, :] = kp_ref[...]
        kk[ATTN_TILE:, :] = k_ref[...]
        vv[0:ATTN_TILE, :] = vp_ref[...]
        vv[ATTN_TILE:, :] = v_ref[...]
        cur_valid, prev_valid = _query_masks()
        for bi, d in enumerate(DILATIONS):
            span = BLOCK * d

            def blk(idx, carry, bi=bi, d=d, span=span):
                g = idx // d
                q0 = g * span + idx % d
                rows = pl.ds(q0, BLOCK, stride=d)
                q2 = _stack_heads(q_ref[rows, :])
                keys = jnp.concatenate([kk[pl.ds(ATTN_TILE + q0 - span, BLOCK, stride=d), :],
                                        kk[pl.ds(ATTN_TILE + q0, BLOCK, stride=d), :]], axis=0)
                vals = jnp.concatenate([vv[pl.ds(ATTN_TILE + q0 - span, BLOCK, stride=d), :],
                                        vv[pl.ds(ATTN_TILE + q0, BLOCK, stride=d), :]], axis=0)
                has_prev = jnp.logical_or(n > 0, g > 0)
                valid = jnp.logical_or(cur_valid, jnp.logical_and(prev_valid, has_prev))
                sc = jnp.where(valid, _dot_nt(q2, keys), NEG)
                mx = jnp.max(sc, axis=1, keepdims=True)
                p = jnp.exp(sc - mx)
                den = jnp.sum(p, axis=1, keepdims=True)
                o_s[bi, rows, :] = _unstack_heads(_dot_nn(p, vals) / den)
                l_s[bi, rows, :] = _unstack_heads(jnp.broadcast_to(mx + jnp.log(den), (q2.shape[0], LANES)))
                return carry

            lax.fori_loop(0, ATTN_TILE // BLOCK, blk, 0, unroll=8)
        ls = [l_s[bi] for bi in range(len(DILATIONS))]
        top = functools.reduce(jnp.maximum, ls)
        ws = [jnp.exp(l - top) for l in ls]
        den = functools.reduce(lambda a, b: a + b, ws)
        num = functools.reduce(lambda a, b: a + b, [w * o_s[bi] for bi, w in enumerate(ws)])
        o_ref[...] = num / den
        l_ref[...] = top + jnp.log(den)

    return _call(
        body, grid=(width // LANES, n_tiles), in_specs=[cur, cur, prev, vcur, vprev],
        out_specs=[cur, cur], out_shape=[jax.ShapeDtypeStruct((s, width), F32)] * 2,
        scratch_shapes=[pltpu.VMEM((2 * ATTN_TILE, LANES), F32), pltpu.VMEM((2 * ATTN_TILE, LANES), F32),
                        pltpu.VMEM((len(DILATIONS), ATTN_TILE, LANES), F32),
                        pltpu.VMEM((len(DILATIONS), ATTN_TILE, LANES), F32)],
        args=(q, k, k, proj, proj), name=name, comm=comm)


def _attn_bwd_q(q, k, proj, v_block, do, o, lse, name, comm=None):
    s, width = q.shape
    n_tiles = s // ATTN_TILE
    cur, prev, vcur, vprev = _attn_specs(width, v_block, lambda n: jnp.maximum(n - 1, 0))

    def body(q_ref, k_ref, kp_ref, v_ref, vp_ref, do_ref, o_ref, l_ref, dq_ref, kk, vv):
        n = pl.program_id(1)
        kk[0:ATTN_TILE, :] = kp_ref[...]
        kk[ATTN_TILE:, :] = k_ref[...]
        vv[0:ATTN_TILE, :] = vp_ref[...]
        vv[ATTN_TILE:, :] = v_ref[...]
        dq_ref[...] = jnp.zeros_like(dq_ref)
        cur_valid, prev_valid = _query_masks()
        for d in DILATIONS:
            span = BLOCK * d

            def blk(idx, carry, d=d, span=span):
                g = idx // d
                q0 = g * span + idx % d
                rows = pl.ds(q0, BLOCK, stride=d)
                dob = do_ref[rows, :]
                q2 = _stack_heads(q_ref[rows, :])
                do2 = _stack_heads(dob)
                delta = jnp.sum(_stack_heads(dob * o_ref[rows, :]), axis=1, keepdims=True)
                lse2 = _stacked_lse(l_ref[rows, :])
                keys = jnp.concatenate([kk[pl.ds(ATTN_TILE + q0 - span, BLOCK, stride=d), :],
                                        kk[pl.ds(ATTN_TILE + q0, BLOCK, stride=d), :]], axis=0)
                vals = jnp.concatenate([vv[pl.ds(ATTN_TILE + q0 - span, BLOCK, stride=d), :],
                                        vv[pl.ds(ATTN_TILE + q0, BLOCK, stride=d), :]], axis=0)
                has_prev = jnp.logical_or(n > 0, g > 0)
                valid = jnp.logical_or(cur_valid, jnp.logical_and(prev_valid, has_prev))
                p = jnp.where(valid, jnp.exp(_dot_nt(q2, keys) - lse2), 0.0)
                ds = p * (_dot_nt(do2, vals) - delta)
                dq_ref[rows, :] += _unstack_heads(_dot_nn(ds, keys))
                return carry

            lax.fori_loop(0, ATTN_TILE // BLOCK, blk, 0, unroll=8)

    return _call(
        body, grid=(width // LANES, n_tiles), in_specs=[cur, cur, prev, vcur, vprev, cur, cur, cur],
        out_specs=[cur], out_shape=[jax.ShapeDtypeStruct((s, width), F32)],
        scratch_shapes=[pltpu.VMEM((2 * ATTN_TILE, LANES), F32), pltpu.VMEM((2 * ATTN_TILE, LANES), F32)],
        args=(q, k, k, proj, proj, do, o, lse), name=name, comm=comm)


def _attn_bwd_kv(q, k, proj, v_block, do, o, lse, name):
    s, width = q.shape
    n_tiles = s // ATTN_TILE
    cur, nxt, vcur, _ = _attn_specs(width, v_block, lambda n: jnp.minimum(n + 1, n_tiles - 1))

    def body(k_ref, v_ref, q_ref, qn_ref, do_ref, don_ref, o_ref, on_ref, l_ref, ln_ref, dk_ref, dv_ref,
             qq, dd, pr, ll):
        n = pl.program_id(1)
        qq[0:ATTN_TILE, :] = q_ref[...]
        qq[ATTN_TILE:, :] = qn_ref[...]
        dd[0:ATTN_TILE, :] = do_ref[...]
        dd[ATTN_TILE:, :] = don_ref[...]
        pr[0:ATTN_TILE, :] = do_ref[...] * o_ref[...]
        pr[ATTN_TILE:, :] = don_ref[...] * on_ref[...]
        ll[0:ATTN_TILE, :] = l_ref[...]
        ll[ATTN_TILE:, :] = ln_ref[...]
        dk_ref[...] = jnp.zeros_like(dk_ref)
        dv_ref[...] = jnp.zeros_like(dv_ref)
        same_ok, before_ok = _band_masks(2 * HEADS_PER_TILE, 1)
        is_same = lax.broadcasted_iota(jnp.int32, same_ok.shape, 0) < HEADS_PER_TILE * BLOCK
        same_valid = jnp.logical_and(is_same, same_ok)
        after_valid = jnp.logical_and(jnp.logical_not(is_same), before_ok)
        for d in DILATIONS:
            span = BLOCK * d
            n_groups = ATTN_TILE // span

            def blk(idx, carry, d=d, span=span, n_groups=n_groups):
                g = idx // d
                k0 = g * span + idx % d
                rows = pl.ds(k0, BLOCK, stride=d)
                kb = k_ref[rows, :]
                vb = v_ref[rows, :]
                here, after = pl.ds(k0, BLOCK, stride=d), pl.ds(k0 + span, BLOCK, stride=d)
                q4 = jnp.concatenate([_stack_heads(qq[here, :]), _stack_heads(qq[after, :])], axis=0)
                do4 = jnp.concatenate([_stack_heads(dd[here, :]), _stack_heads(dd[after, :])], axis=0)
                delta = jnp.sum(jnp.concatenate([_stack_heads(pr[here, :]), _stack_heads(pr[after, :])], axis=0),
                                axis=1, keepdims=True)
                lse4 = jnp.concatenate([_stacked_lse(ll[here, :]), _stacked_lse(ll[after, :])], axis=0)
                has_next = jnp.logical_or(n < n_tiles - 1, g < n_groups - 1)
                valid = jnp.logical_or(same_valid, jnp.logical_and(after_valid, has_next))
                p = jnp.where(valid, jnp.exp(_dot_nt(q4, kb) - lse4), 0.0)
                ds = p * (_dot_nt(do4, vb) - delta)
                dv_ref[rows, :] += _dot_tn(p, do4)
                dk_ref[rows, :] += _dot_tn(ds, q4)
                return carry

            lax.fori_loop(0, ATTN_TILE // BLOCK, blk, 0, unroll=4)

    return pl.pallas_call(
        body, grid=(width // LANES, n_tiles), in_specs=[cur, vcur, cur, nxt, cur, nxt, cur, nxt, cur, nxt],
        out_specs=[cur, cur], out_shape=[jax.ShapeDtypeStruct((s, width), F32)] * 2,
        scratch_shapes=[pltpu.VMEM((2 * ATTN_TILE, LANES), F32)] * 4,
        compiler_params=_params(2), name=name)(k, proj, q, q, do, do, o, o, lse, lse)


def _conv_specs(s, a_block, b_block):
    per = CONV_CHUNK // CONV_HALO
    a_cur = pl.BlockSpec((CONV_CHUNK, LANES), lambda cb, i: (i, a_block + cb))
    b_cur = pl.BlockSpec((CONV_CHUNK, LANES), lambda cb, i: (i, b_block + cb))
    a_halo = pl.BlockSpec((CONV_HALO, LANES), lambda cb, i: (jnp.maximum(i * per - 1, 0), a_block + cb))
    b_halo = pl.BlockSpec((CONV_HALO, LANES), lambda cb, i: (jnp.maximum(i * per - 1, 0), b_block + cb))
    w_spec = pl.BlockSpec((CONV_KERNEL, LANES), lambda cb, i: (0, cb))
    vec = pl.BlockSpec((1, LANES), lambda cb, i: (0, cb))
    out = pl.BlockSpec((CONV_CHUNK, LANES), lambda cb, i: (i, cb))
    return a_cur, b_cur, a_halo, b_halo, w_spec, vec, out


def _fill_glu_window(win, a_ref, b_ref, ah_ref, bh_ref, first):
    halo = ah_ref[...] * _sigmoid(bh_ref[...])
    win[0:CONV_HALO, :] = jnp.where(first, 0.0, halo)
    win[CONV_HALO:, :] = a_ref[...] * _sigmoid(b_ref[...])


def _conv_fwd(proj, a_block, b_block, w, bias, name):
    s = proj.shape[0]
    cw = w.shape[1]
    a_cur, b_cur, a_halo, b_halo, w_spec, vec, out = _conv_specs(s, a_block, b_block)
    lead = CONV_HALO - (CONV_KERNEL - 1)

    def body(a_ref, b_ref, ah_ref, bh_ref, w_ref, bias_ref, o_ref, win):
        _fill_glu_window(win, a_ref, b_ref, ah_ref, bh_ref, pl.program_id(1) == 0)
        for sub in range(CONV_CHUNK // CONV_SUB):
            base = sub * CONV_SUB
            acc = jnp.zeros((CONV_SUB, LANES), F32) + bias_ref[...]
            for j in range(CONV_KERNEL):
                acc = acc + w_ref[j:j + 1, :] * win[base + lead + j:base + lead + j + CONV_SUB, :]
            o_ref[base:base + CONV_SUB, :] = acc

    return pl.pallas_call(
        body, grid=(cw // LANES, s // CONV_CHUNK), in_specs=[a_cur, b_cur, a_halo, b_halo, w_spec, vec],
        out_specs=out, out_shape=jax.ShapeDtypeStruct((s, cw), F32),
        scratch_shapes=[pltpu.VMEM((CONV_CHUNK + CONV_HALO, LANES), F32)],
        compiler_params=_params(2), name=name)(proj, proj, proj, proj, w, bias)


def _conv_bwd(proj, a_block, b_block, w, du1, name):
    s = proj.shape[0]
    cw = w.shape[1]
    a_cur, b_cur, a_halo, b_halo, w_spec, vec, out = _conv_specs(s, a_block, b_block)
    per = CONV_CHUNK // CONV_HALO
    n_chunks = s // CONV_CHUNK
    d_next = pl.BlockSpec((CONV_HALO, LANES), lambda cb, i: (jnp.minimum((i + 1) * per, s // CONV_HALO - 1), cb))
    lead = CONV_HALO - (CONV_KERNEL - 1)

    def body(a_ref, b_ref, ah_ref, bh_ref, w_ref, d_ref, dn_ref, da_ref, db_ref, dw_ref, dbias_ref, win, dwin):
        i = pl.program_id(1)
        _fill_glu_window(win, a_ref, b_ref, ah_ref, bh_ref, i == 0)
        dwin[0:CONV_CHUNK, :] = d_ref[...]
        dwin[CONV_CHUNK:, :] = jnp.where(i == n_chunks - 1, 0.0, dn_ref[...])

        @pl.when(i == 0)
        def _():
            dw_ref[...] = jnp.zeros_like(dw_ref)
            dbias_ref[...] = jnp.zeros_like(dbias_ref)

        dbias_ref[...] += _colsum(d_ref[...])
        for sub in range(CONV_CHUNK // CONV_SUB):
            base = sub * CONV_SUB
            dcur = dwin[base:base + CONV_SUB, :]
            du0 = jnp.zeros((CONV_SUB, LANES), F32)
            for j in range(CONV_KERNEL):
                back = CONV_KERNEL - 1 - j
                du0 = du0 + w_ref[j:j + 1, :] * dwin[base + back:base + back + CONV_SUB, :]
                dw_ref[j:j + 1, :] += _colsum(dcur * win[base + lead + j:base + lead + j + CONV_SUB, :])
            av = a_ref[base:base + CONV_SUB, :]
            sig = _sigmoid(b_ref[base:base + CONV_SUB, :])
            da_ref[base:base + CONV_SUB, :] = du0 * sig
            db_ref[base:base + CONV_SUB, :] = du0 * av * sig * (1.0 - sig)

    return pl.pallas_call(
        body, grid=(cw // LANES, n_chunks), in_specs=[a_cur, b_cur, a_halo, b_halo, w_spec, out, d_next],
        out_specs=[out, out, w_spec, vec],
        out_shape=[jax.ShapeDtypeStruct((s, cw), F32), jax.ShapeDtypeStruct((s, cw), F32),
                   jax.ShapeDtypeStruct((CONV_KERNEL, cw), F32), jax.ShapeDtypeStruct((1, cw), F32)],
        scratch_shapes=[pltpu.VMEM((CONV_CHUNK + CONV_HALO, LANES), F32)] * 2,
        compiler_params=_params(2), name=name)(proj, proj, proj, proj, w, du1, du1)


def _adamw_math(w, g, m, v):
    m = ADAM_B1 * m + (1.0 - ADAM_B1) * g
    v = ADAM_B2 * v + (1.0 - ADAM_B2) * (g * g)
    m_hat = m / (1.0 - ADAM_B1 ** ADAM_STEP)
    v_hat = v / (1.0 - ADAM_B2 ** ADAM_STEP)
    delta = -ADAM_LR * (m_hat / (jnp.sqrt(v_hat) + ADAM_EPS) + ADAM_WD * w)
    return delta, m, v


def _adamw_big(w, g, m, v, name):
    rows, cols = w.shape
    tile = _tile(rows, 256, 8)
    spec = pl.BlockSpec((tile, cols), lambda i: (i, 0))

    def body(w_ref, g_ref, m_ref, v_ref, d_out, m_out, v_out):
        d_out[...], m_out[...], v_out[...] = _adamw_math(w_ref[...], g_ref[...], m_ref[...], v_ref[...])

    return pl.pallas_call(body, grid=(rows // tile,), in_specs=[spec] * 4, out_specs=[spec] * 3,
                          out_shape=[jax.ShapeDtypeStruct(w.shape, F32)] * 3, compiler_params=_params(1),
                          name=name)(w, g, m, v)


def _adamw_reduced(w, land, m, v, name):
    rows, cols = w.shape
    tile = _tile(rows, 256, 16)
    spec = pl.BlockSpec((tile, cols), lambda i: (i, 0))

    def body(w_ref, l_ref, m_ref, v_ref, g_out, d_out, m_out, v_out):
        g = l_ref[0].astype(F32)
        for q in range(1, N_CHIP):
            g = g + l_ref[q].astype(F32)
        g_out[...] = g
        d_out[...], m_out[...], v_out[...] = _adamw_math(w_ref[...], g, m_ref[...], v_ref[...])

    return pl.pallas_call(body, grid=(rows // tile,),
                          in_specs=[spec, pl.BlockSpec((N_CHIP, tile, cols), lambda i: (0, i, 0)), spec, spec],
                          out_specs=[spec] * 4, out_shape=[jax.ShapeDtypeStruct(w.shape, F32)] * 4,
                          compiler_params=_params(1), name=name)(w, land, m, v)


def _adamw_small(ws, gs, ms, vs, name):
    n = len(ws)

    def body(*refs):
        ins, outs = refs[:4 * n], refs[4 * n:]
        for t in range(n):
            res = _adamw_math(ins[t][...], ins[n + t][...], ins[2 * n + t][...], ins[3 * n + t][...])
            for j in range(3):
                outs[j * n + t][...] = res[j]

    shapes = [jax.ShapeDtypeStruct(w.shape, F32) for w in ws]
    res = pl.pallas_call(body, out_shape=shapes * 3, compiler_params=pltpu.CompilerParams(vmem_limit_bytes=VMEM_LIMIT),
                         name=name)(*ws, *gs, *ms, *vs)
    return res[:n], res[n:2 * n], res[2 * n:]


def _sum_blocks(x, n_blocks, name):
    r = x.shape[0] // n_blocks

    def body(x_ref, o_ref):
        acc = x_ref[0:r, :]
        for b in range(1, n_blocks):
            acc = acc + x_ref[b * r:(b + 1) * r, :]
        o_ref[...] = acc

    return pl.pallas_call(body, out_shape=jax.ShapeDtypeStruct((r, x.shape[1]), F32),
                          compiler_params=pltpu.CompilerParams(vmem_limit_bytes=VMEM_LIMIT), name=name)(x)


def _coords():
    return lax.axis_index("x"), lax.axis_index("y"), lax.axis_index("c")


def _flip(v, bit):
    return 1 - v if bit else v


def _ag_small(x, name):
    r, c = x.shape

    def body(x_ref, o_ref, send, recv, local_sem):
        mx, my, mc = _coords()

        def rows(px, py, pc):
            return o_ref.at[pl.ds(pl.multiple_of((4 * px + 2 * py + pc) * r, 8), r), :]

        local = pltpu.make_async_copy(x_ref, rows(mx, my, mc), local_sem)
        local.start()
        peers = [(_flip(mx, k >> 2 & 1), _flip(my, k >> 1 & 1), _flip(mc, k & 1)) for k in range(1, N_DEV)]
        sends = [pltpu.make_async_remote_copy(x_ref, rows(mx, my, mc), send.at[k], recv.at[k], device_id=p,
                                              device_id_type=MESH) for k, p in enumerate(peers)]
        for cp in sends:
            cp.start()
        for k, p in enumerate(peers):
            pltpu.make_async_remote_copy(x_ref, rows(*p), send.at[k], recv.at[k], device_id=p,
                                         device_id_type=MESH).wait_recv()
        for cp in sends:
            cp.wait_send()
        local.wait()

    vm = pl.BlockSpec(memory_space=pltpu.VMEM)
    return pl.pallas_call(
        body, in_specs=[vm], out_specs=vm, out_shape=jax.ShapeDtypeStruct((N_DEV * r, c), x.dtype),
        scratch_shapes=[pltpu.SemaphoreType.DMA((N_DEV - 1,)), pltpu.SemaphoreType.DMA((N_DEV - 1,)),
                        pltpu.SemaphoreType.DMA(())],
        name=name)(x)


class _GatherWeights:
    def __init__(self, shards):
        n_t = len(shards)
        self.inputs = list(shards)
        self.out_shapes = [jax.ShapeDtypeStruct((N_DEV * x.shape[0], x.shape[1]), x.dtype) for x in shards]
        self.scratch = [pltpu.SemaphoreType.DMA((n_t, 7)), pltpu.SemaphoreType.DMA((n_t, 7)),
                        pltpu.SemaphoreType.DMA((n_t,))]

    def _plan(self, x_refs, o_refs, sems):
        send, recv, local_sem = sems
        mx, my, mc = _coords()
        me, sibling = (mx, my, mc), (mx, my, 1 - mc)
        chips = [(1 - mx, my), (mx, 1 - my), (1 - mx, 1 - my)]

        def rows(t, px, py, pc):
            r = x_refs[t].shape[0]
            return o_refs[t].at[pl.ds(pl.multiple_of((4 * px + 2 * py + pc) * r, 8), r), :]

        def copy(t, k, block, to, src=None):
            return pltpu.make_async_remote_copy(
                src_ref=rows(t, *block) if src is None else src, dst_ref=rows(t, *block),
                send_sem=send.at[t, k], recv_sem=recv.at[t, k], device_id=to, device_id_type=MESH)

        def local(t):
            return pltpu.make_async_copy(x_refs[t], rows(t, *me), local_sem.at[t])

        return me, sibling, chips, mc, copy, local

    def start(self, x_refs, o_refs, sems):
        me, sibling, chips, mc, copy, local = self._plan(x_refs, o_refs, sems)
        for t in range(len(x_refs)):
            local(t).start()
            copy(t, 0, me, sibling, src=x_refs[t]).start()
            for j, chip in enumerate(chips):
                copy(t, 1 + j, me, (*chip, mc), src=x_refs[t]).start()

    def mid(self, x_refs, o_refs, sems):
        me, sibling, chips, mc, copy, local = self._plan(x_refs, o_refs, sems)
        for j, chip in enumerate(chips):
            for t in range(len(x_refs)):
                copy(t, 1 + j, (*chip, mc), me).wait_recv()
                copy(t, 4 + j, (*chip, mc), sibling).start()

    def finish(self, x_refs, o_refs, sems):
        me, sibling, chips, mc, copy, local = self._plan(x_refs, o_refs, sems)
        for t in range(len(x_refs)):
            copy(t, 0, sibling, me).wait_recv()
            for j, chip in enumerate(chips):
                copy(t, 4 + j, (*chip, 1 - mc), me).wait_recv()
            copy(t, 0, me, sibling, src=x_refs[t]).wait_send()
            for j, chip in enumerate(chips):
                copy(t, 1 + j, me, (*chip, mc), src=x_refs[t]).wait_send()
                copy(t, 4 + j, (*chip, mc), sibling).wait_send()
            local(t).wait()


class _SiblingExchange:
    mid = None

    def __init__(self, grads):
        n_t = len(grads)
        self.inputs = list(grads)
        self.out_shapes = [jax.ShapeDtypeStruct((N_CHIP,) + g.shape[2:], F32) for g in grads]
        self.scratch = [pltpu.SemaphoreType.DMA((n_t,)), pltpu.SemaphoreType.DMA((n_t,))]

    def _copies(self, g_refs, land, sems):
        send, recv = sems
        mx, my, mc = _coords()
        return [pltpu.make_async_remote_copy(g_refs[t].at[:, 1 - mc], land[t], send.at[t], recv.at[t],
                                             device_id=(mx, my, 1 - mc), device_id_type=MESH)
                for t in range(len(g_refs))]

    def start(self, g_refs, land, sems):
        for cp in self._copies(g_refs, land, sems):
            cp.start()

    def finish(self, g_refs, land, sems):
        for cp in self._copies(g_refs, land, sems):
            cp.wait()


class _Together:
    def __init__(self, *comms):
        self.comms = comms
        self.inputs = [x for c in comms for x in c.inputs]
        self.out_shapes = [x for c in comms for x in c.out_shapes]
        self.scratch = [x for c in comms for x in c.scratch]
        self.mid = self._mid if any(c.mid is not None for c in comms) else None

    def _each(self, phase, cin, cout, sems):
        i = o = s = 0
        for c in self.comms:
            fn = getattr(c, phase)
            ni, no, ns = len(c.inputs), len(c.out_shapes), len(c.scratch)
            if fn is not None:
                fn(cin[i:i + ni], cout[o:o + no], sems[s:s + ns])
            i, o, s = i + ni, o + no, s + ns

    def start(self, cin, cout, sems):
        self._each("start", cin, cout, sems)

    def _mid(self, cin, cout, sems):
        self._each("mid", cin, cout, sems)

    def finish(self, cin, cout, sems):
        self._each("finish", cin, cout, sems)


def _standalone(comm, name):
    def body():
        pass
    return _call(body, grid=(1,), in_specs=[], out_specs=[], out_shape=[], args=(), name=name, comm=comm)[1]


def _chip_partial(g4, land, name):
    _, _, r, c = g4.shape
    tr = _tile(r, 256, 16)

    def body(g_ref, l_ref, o_ref):
        o_ref[...] = (g_ref[...] + l_ref[...]).astype(o_ref.dtype)

    return pl.pallas_call(
        body, grid=(N_CHIP, r // tr),
        in_specs=[pl.BlockSpec((None, None, tr, c), lambda q, i: (q, lax.axis_index("c"), i, 0)),
                  pl.BlockSpec((None, tr, c), lambda q, i: (q, i, 0))],
        out_specs=pl.BlockSpec((None, tr, c), lambda q, i: (q, i, 0)),
        out_shape=jax.ShapeDtypeStruct((N_CHIP, r, c), BF16), compiler_params=_params(2), name=name)(g4, land)


class _ChipExchange:
    mid = None

    def __init__(self, parts):
        n_t = len(parts)
        self.inputs = list(parts)
        self.out_shapes = [jax.ShapeDtypeStruct(p.shape, p.dtype) for p in parts]
        self.scratch = [pltpu.SemaphoreType.DMA((n_t, 3)), pltpu.SemaphoreType.DMA((n_t, 3)),
                        pltpu.SemaphoreType.DMA((n_t,))]

    def _plan(self, p_refs, land, sems):
        send, recv, local_sem = sems
        mx, my, mc = _coords()
        my_chip = 2 * mx + my
        peers = [(_flip(mx, fx), _flip(my, fy)) for fx, fy in ((1, 0), (0, 1), (1, 1))]

        def out(t, k):
            px, py = peers[k]
            return pltpu.make_async_remote_copy(p_refs[t].at[2 * px + py], land[t].at[my_chip], send.at[t, k],
                                                recv.at[t, k], device_id=(px, py, mc), device_id_type=MESH)

        def arrival(t, k):
            px, py = peers[k]
            return pltpu.make_async_remote_copy(p_refs[t].at[my_chip], land[t].at[2 * px + py], send.at[t, k],
                                                recv.at[t, k], device_id=(px, py, mc), device_id_type=MESH)

        def local(t):
            return pltpu.make_async_copy(p_refs[t].at[my_chip], land[t].at[my_chip], local_sem.at[t])

        return out, arrival, local

    def start(self, p_refs, land, sems):
        out, arrival, local = self._plan(p_refs, land, sems)
        for t in range(len(p_refs)):
            local(t).start()
            for k in range(3):
                out(t, k).start()

    def finish(self, p_refs, land, sems):
        out, arrival, local = self._plan(p_refs, land, sems)
        for t in range(len(p_refs)):
            for k in range(3):
                arrival(t, k).wait_recv()
                out(t, k).wait_send()
            local(t).wait()


def _rope_tables(s, width):
    pos = jnp.arange(s, dtype=F32)
    inv_freq = ROPE_THETA ** (-jnp.arange(0, HEAD_DIM, 2, dtype=F32) / HEAD_DIM)
    ang = pos[:, None] * inv_freq[None, :]
    cos, sin = jnp.cos(ang), jnp.sin(ang)
    heads = width // HEAD_DIM
    return jnp.tile(jnp.concatenate([cos, cos], axis=1), (1, heads)), jnp.tile(jnp.concatenate([-sin, sin], axis=1), (1, heads))


def _pad_rows(v, rows):
    return jnp.concatenate([v, jnp.zeros((rows - 1, v.shape[1]), v.dtype)], axis=0)


def kernel(x, c, w_ada, b_ada, ffn1_norm_g, ffn1_w_gate, ffn1_w_up, ffn1_w_down, mix_norm_g, w_in, conv_dw_w, conv_dw_b, conv_ln_g, conv_ln_b, attn_out_g, conv_out_g, w_out, ffn2_norm_g, ffn2_w_gate, ffn2_w_up, ffn2_w_down, final_norm_g, loss_target, m_w_ada, m_b_ada, m_ffn1_norm_g, m_ffn1_w_gate, m_ffn1_w_up, m_ffn1_w_down, m_mix_norm_g, m_w_in, m_conv_dw_w, m_conv_dw_b, m_conv_ln_g, m_conv_ln_b, m_attn_out_g, m_conv_out_g, m_w_out, m_ffn2_norm_g, m_ffn2_w_gate, m_ffn2_w_up, m_ffn2_w_down, m_final_norm_g, v_w_ada, v_b_ada, v_ffn1_norm_g, v_ffn1_w_gate, v_ffn1_w_up, v_ffn1_w_down, v_mix_norm_g, v_w_in, v_conv_dw_w, v_conv_dw_b, v_conv_ln_g, v_conv_ln_b, v_attn_out_g, v_conv_out_g, v_w_out, v_ffn2_norm_g, v_ffn2_w_gate, v_ffn2_w_up, v_ffn2_w_down, v_final_norm_g):
    mx, my, mc = _coords()
    me = 4 * mx + 2 * my + mc
    s, d = x.shape[1], x.shape[2]
    aw = d // 2
    x2, target = x[0], loss_target[0]
    n_mod = w_ada.shape[2] * N_DEV // d
    mod_cols = w_ada.shape[2]

    cw_shard = conv_dw_w.shape[3]
    n_taps = CONV_KERNEL * cw_shard
    first_len = -(-(d + n_taps) // LANES) * LANES
    first = jnp.concatenate([c, conv_dw_w[0, :, 0, :].reshape(1, n_taps), jnp.zeros((1, first_len - d - n_taps), F32)], axis=1)
    first_all = _ag_small(_pad_rows(first, 8), "ag_c_taps")[0::8]
    c_all = first_all[:, :d]
    conv_w = first_all[:, d:d + n_taps].reshape(N_DEV, CONV_KERNEL, cw_shard).transpose(1, 0, 2).reshape(CONV_KERNEL, aw)

    silu_c = _silu_rows(c_all, "silu_c")
    mod_part = _plain_mm([(silu_c, w_ada[0])], F32, False, mod_cols, "mod_mm")
    mod_all = _ag_small(mod_part, "ag_mod").reshape(N_DEV, N_DEV, mod_cols)
    mod = lax.dynamic_index_in_dim(mod_all, me, axis=1, keepdims=False).reshape(1, n_mod * d) + b_ada
    sh1, sc1, g1, sh2, sc2, g2, sh3, sc3, g3 = [mod[:, i * d:(i + 1) * d] for i in range(n_mod)]

    def shard(w, transpose):
        return (w[0].T if transpose else w[0]).astype(BF16)

    def split(g):
        return g.reshape(N_CHIP, 2, g.shape[0] // N_DEV, g.shape[1])

    def partials(g4s, lands, tag):
        return [_chip_partial(a, b, "chip_partial_%s%d" % (tag, t)) for t, (a, b) in enumerate(zip(g4s, lands))]

    wg1, wu1 = _standalone(_GatherWeights([shard(ffn1_w_gate, True), shard(ffn1_w_up, True)]), "ag_ffn1_in")
    gather_late = _GatherWeights([shard(ffn2_w_gate, True), shard(ffn2_w_up, True), shard(ffn2_w_down, False),
                                  shard(w_out, False)])

    n1 = _norm_mod_fwd(x2, ffn1_norm_g, sc1, sh1, "norm1")
    (a1, b1, hid1), (wd1,) = _ffn_up(n1, wg1, wu1, "ffn1_up", comm=_GatherWeights([shard(ffn1_w_down, False)]))
    (h1, f1, n2), (win_t,) = _residual_mm(hid1, wd1, x2, g1, 0.5, "ffn1_down", norm=(mix_norm_g, sc2, sh2),
                                          comm=_GatherWeights([shard(w_in, True)]))
    proj = _plain_mm([(n2, win_t)], F32, True, _tile(5 * aw, 1536, LANES), "proj")
    cos, sin_signed = _rope_tables(s, aw)
    q_rot, k_rot = _rope_fwd(proj, cos, sin_signed, aw, "rope")
    lanes_per = aw // LANES
    (attn, lse), (wg2, wu2, wd2, wout) = _attn_fwd(q_rot, k_rot, proj, 2 * lanes_per, "attn_fwd", comm=gather_late)
    u1 = _conv_fwd(proj, 3 * lanes_per, 4 * lanes_per, conv_w, conv_dw_b, "conv_fwd")
    y = _mix_post_fwd(attn, u1, attn_out_g, conv_ln_g, conv_ln_b, conv_out_g, "mix_post")
    h2, mix, n3 = _residual_mm(y, wout, h1, g2, 1.0, "mix_out", norm=(ffn2_norm_g, sc3, sh3))
    a3, b3, hid3 = _ffn_up(n3, wg2, wu2, "ffn2_up")
    h3, f3 = _residual_mm(hid3, wd2, h2, g3, 0.5, "ffn2_down")

    dh3, df3, err2, d_final_g, dg3 = _final_loss(h3, target, final_norm_g.reshape(1, d), f3, g3, 0.5, "final_loss")
    loss = lax.psum(0.5 * jnp.sum(err2) / d, ("x", "y", "c"))

    da3, db3 = _ffn_bwd_hidden(df3, wd2, a3, b3, "ffn2_hidden_bwd")
    g4_a = [split(_mm_tn(da3, n3, "ffn2_dwg")), split(_mm_tn(db3, n3, "ffn2_dwu")), split(_mm_tn(hid3, df3, "ffn2_dwd"))]
    dn3, land_a = _plain_mm([(da3, wg2), (db3, wu2)], F32, False, d, "ffn2_dn", tm=256, comm=_SiblingExchange(g4_a))
    parts_a = partials(g4_a, land_a, "a")
    dh2, dmix, dsh3, dsc3, dgn3, dg2 = _norm_mod_bwd(dn3, h2, dh3, ffn2_norm_g, sc3, "norm3_bwd",
                                                     branch=(mix, g2, 1.0))
    dy = _plain_mm([(dmix, wout)], F32, True, d, "mix_dy")
    g_wout = _mm_tn(y, dmix, "mix_dwout")
    dattn, du1, d_attn_g, d_conv_g, d_ln_g, d_ln_b = _mix_post_bwd(
        dy, attn, u1, attn_out_g, conv_ln_g, conv_ln_b, conv_out_g, "mix_post_bwd")
    dga, dgb, d_taps, d_conv_b = _conv_bwd(proj, 3 * lanes_per, 4 * lanes_per, conv_w, du1, "conv_bwd")
    (dq,), sums_a = _attn_bwd_q(q_rot, k_rot, proj, 2 * lanes_per, dattn, attn, lse, "attn_bwd_q",
                                comm=_ChipExchange(parts_a))
    dk, dv = _attn_bwd_kv(q_rot, k_rot, proj, 2 * lanes_per, dattn, attn, lse, "attn_bwd_kv")
    dproj = _dproj_assemble(dq, dk, dv, dga, dgb, cos, sin_signed, "dproj")
    dn2 = _plain_mm([(dproj, win_t)], F32, False, d, "mix_dn")
    g4_b = [split(g_wout), split(_mm_tn(dproj, n2, "mix_dwin"))]
    (dh1, df1, dsh2, dsc2, dgn2, dg1), land_b = _norm_mod_bwd(dn2, h1, dh2, mix_norm_g, sc2, "norm2_bwd",
                                                              branch=(f1, g1, 0.5), comm=_SiblingExchange(g4_b))
    parts_b = partials(g4_b, land_b, "b")
    g4_c = [split(_mm_tn(hid1, df1, "ffn1_dwd"))]
    (da1, db1), both = _ffn_bwd_hidden(df1, wd1, a1, b1, "ffn1_hidden_bwd",
                                       comm=_Together(_ChipExchange(parts_b), _SiblingExchange(g4_c)))
    sums_b, land_c = both[:2], both[2:]
    parts_c = partials(g4_c, land_c, "c")
    g_wu1, sums_c = _mm_tn(db1, n1, "ffn1_dwu", comm=_ChipExchange(parts_c))
    g4_d = [split(g_wu1)]
    g_wg1, land_d = _mm_tn(da1, n1, "ffn1_dwg", comm=_SiblingExchange(g4_d))
    parts_d = partials(g4_d, land_d, "d")
    g4_e = [split(g_wg1)]
    dn1, both = _plain_mm([(da1, wg1), (db1, wu1)], F32, False, d, "ffn1_dn", tm=256,
                          comm=_Together(_ChipExchange(parts_d), _SiblingExchange(g4_e)))
    sums_d, land_e = both[:1], both[1:]
    parts_e = partials(g4_e, land_e, "e")
    (dx, dsh1, dsc1, dgn1), sums_e = _norm_mod_bwd(dn1, x2, dh1, ffn1_norm_g, sc1, "norm1_bwd",
                                                   comm=_ChipExchange(parts_e))

    dmod = jnp.concatenate([dsh1, dsc1, dg1, dsh2, dsc2, dg2, dsh3, dsc3, dg3], axis=1)
    small = [dmod, dgn1, dgn2, dgn3, d_final_g, d_conv_b, d_ln_g, d_ln_b, d_attn_g, d_conv_g,
             d_taps.reshape(1, CONV_KERNEL * aw)]
    sizes = [v.shape[1] for v in small]
    total = sum(sizes)
    padded = -(-total // (8 * LANES)) * (8 * LANES)
    packed = jnp.concatenate(small + [jnp.zeros((1, padded - total), F32)], axis=1).reshape(8, padded // 8)
    gathered = _ag_small(packed, "ag_small_grads")
    summed = _sum_blocks(gathered, N_DEV, "sum_small_grads").reshape(1, padded)
    offs = [sum(sizes[:i]) for i in range(len(sizes))]
    (g_b_ada, g_gn1, g_gn2, g_gn3, g_final, g_conv_b, g_ln_g, g_ln_b, g_attn_g, g_conv_g, g_taps) = [
        summed[:, o:o + n] for o, n in zip(offs, sizes)]
    g_taps_shard = lax.dynamic_slice_in_dim(g_taps.reshape(CONV_KERNEL, aw), me * cw_shard, cw_shard, axis=1)
    dmod_all = gathered.reshape(N_DEV, padded)[:, :n_mod * d]
    dmod_cols = lax.dynamic_slice_in_dim(dmod_all, me * mod_cols, mod_cols, axis=1)
    g_w_ada = _mm_tn(silu_c, dmod_cols, "ada_dw")

    arrived = dict(zip(["ffn2_w_gate", "ffn2_w_up", "ffn2_w_down", "w_out", "w_in", "ffn1_w_down", "ffn1_w_up",
                        "ffn1_w_gate"], list(sums_a) + list(sums_b) + list(sums_c) + list(sums_d) + list(sums_e)))
    transposed = ("ffn1_w_gate", "ffn1_w_up", "w_in", "ffn2_w_gate", "ffn2_w_up")
    grads = {
        "w_ada": g_w_ada, "b_ada": g_b_ada, "ffn1_norm_g": g_gn1, "mix_norm_g": g_gn2, "conv_dw_w": g_taps_shard,
        "conv_dw_b": g_conv_b, "conv_ln_g": g_ln_g, "conv_ln_b": g_ln_b, "attn_out_g": g_attn_g,
        "conv_out_g": g_conv_g, "ffn2_norm_g": g_gn3, "final_norm_g": g_final,
    }
    weights = dict(w_ada=w_ada, b_ada=b_ada, ffn1_norm_g=ffn1_norm_g, ffn1_w_gate=ffn1_w_gate, ffn1_w_up=ffn1_w_up, ffn1_w_down=ffn1_w_down, mix_norm_g=mix_norm_g, w_in=w_in, conv_dw_w=conv_dw_w, conv_dw_b=conv_dw_b, conv_ln_g=conv_ln_g, conv_ln_b=conv_ln_b, attn_out_g=attn_out_g, conv_out_g=conv_out_g, w_out=w_out, ffn2_norm_g=ffn2_norm_g, ffn2_w_gate=ffn2_w_gate, ffn2_w_up=ffn2_w_up, ffn2_w_down=ffn2_w_down, final_norm_g=final_norm_g)
    moms = dict(w_ada=m_w_ada, b_ada=m_b_ada, ffn1_norm_g=m_ffn1_norm_g, ffn1_w_gate=m_ffn1_w_gate, ffn1_w_up=m_ffn1_w_up, ffn1_w_down=m_ffn1_w_down, mix_norm_g=m_mix_norm_g, w_in=m_w_in, conv_dw_w=m_conv_dw_w, conv_dw_b=m_conv_dw_b, conv_ln_g=m_conv_ln_g, conv_ln_b=m_conv_ln_b, attn_out_g=m_attn_out_g, conv_out_g=m_conv_out_g, w_out=m_w_out, ffn2_norm_g=m_ffn2_norm_g, ffn2_w_gate=m_ffn2_w_gate, ffn2_w_up=m_ffn2_w_up, ffn2_w_down=m_ffn2_w_down, final_norm_g=m_final_norm_g)
    vars_ = dict(w_ada=v_w_ada, b_ada=v_b_ada, ffn1_norm_g=v_ffn1_norm_g, ffn1_w_gate=v_ffn1_w_gate, ffn1_w_up=v_ffn1_w_up, ffn1_w_down=v_ffn1_w_down, mix_norm_g=v_mix_norm_g, w_in=v_w_in, conv_dw_w=v_conv_dw_w, conv_dw_b=v_conv_dw_b, conv_ln_g=v_conv_ln_g, conv_ln_b=v_conv_ln_b, attn_out_g=v_attn_out_g, conv_out_g=v_conv_out_g, w_out=v_w_out, ffn2_norm_g=v_ffn2_norm_g, ffn2_w_gate=v_ffn2_w_gate, ffn2_w_up=v_ffn2_w_up, ffn2_w_down=v_ffn2_w_down, final_norm_g=v_final_norm_g)
    names = list(weights)
    big = ["w_ada", "ffn1_w_gate", "ffn1_w_up", "ffn1_w_down", "w_in", "w_out", "ffn2_w_gate", "ffn2_w_up",
           "ffn2_w_down"]
    shape2 = {n: (weights[n].shape[-2] if weights[n].ndim > 1 else 1, weights[n].shape[-1]) for n in names}
    shape2["conv_dw_w"] = (CONV_KERNEL, cw_shard)
    g_out, d_out, m_out, v_out = {}, {}, {}, {}
    for n in big:
        if n in arrived:
            def view(t, n=n):
                return t[0].T if n in transposed else t[0]
            res = _adamw_reduced(view(weights[n]), arrived[n], view(moms[n]), view(vars_[n]), "adamw_" + n)
            g_out[n], d_out[n], m_out[n], v_out[n] = [r.T if n in transposed else r for r in res]
        else:
            g2d = grads[n].reshape(shape2[n])
            res = _adamw_big(weights[n].reshape(shape2[n]), g2d, moms[n].reshape(shape2[n]),
                             vars_[n].reshape(shape2[n]), "adamw_" + n)
            g_out[n], (d_out[n], m_out[n], v_out[n]) = g2d, res
    rest = [n for n in names if n not in big]
    res = _adamw_small([weights[n].reshape(shape2[n]) for n in rest], [grads[n].reshape(shape2[n]) for n in rest],
                       [moms[n].reshape(shape2[n]) for n in rest], [vars_[n].reshape(shape2[n]) for n in rest],
                       "adamw_small")
    for i, n in enumerate(rest):
        g_out[n], d_out[n], m_out[n], v_out[n] = grads[n], res[0][i], res[1][i], res[2][i]

    def shaped(table):
        return [table[n].reshape(weights[n].shape) for n in names]

    return (loss, dx.reshape(x.shape), *shaped(g_out), *shaped(d_out), *shaped(m_out), *shaped(v_out))
```

```python
import functools

import jax
import jax.numpy as jnp
from jax import lax
from jax.experimental import pallas as pl
from jax.experimental.pallas import tpu as pltpu

F32 = jnp.float32
BF16 = jnp.bfloat16
MESH = pl.DeviceIdType.MESH
ANY = pl.BlockSpec(memory_space=pl.ANY)

N_DEV = 8
N_CHIP = 4
HEAD_DIM = 64
HALF_HEAD = HEAD_DIM // 2
LANES = 128
BLOCK = 128
DILATIONS = (1, 4, 16)
MERGE_CHUNK = 512
ROPE_THETA = 10000.0
CONV_KERNEL = 31
CONV_HALO = 32
CONV_CHUNK = 512
CONV_SUB = 128
RMS_EPS = 1e-6
LN_EPS = 1e-5
ADAM_LR = 0.001
ADAM_B1 = 0.9
ADAM_B2 = 0.999
ADAM_EPS = 1e-08
ADAM_WD = 0.01
ADAM_STEP = 10
VMEM_LIMIT = 56 * 1024 * 1024
NEG = -1e30


def _params(n_axes):
    return pltpu.CompilerParams(dimension_semantics=("arbitrary",) * n_axes, vmem_limit_bytes=VMEM_LIMIT)


def _tile(n, target, unit):
    best = None
    for t in range(unit, min(n, target) + 1, unit):
        if n % t == 0:
            best = t
    return best if best is not None else n


def _sigmoid(x):
    return 0.5 * (jnp.tanh(0.5 * x) + 1.0)


def _call(body, *, grid, in_specs, out_specs, out_shape, args, name, scratch_shapes=(), comm=None):
    params = _params(len(grid))
    if comm is None:
        return pl.pallas_call(body, grid=grid, in_specs=list(in_specs), out_specs=list(out_specs),
                              out_shape=list(out_shape), scratch_shapes=list(scratch_shapes),
                              compiler_params=params, name=name)(*args)
    n_in, n_out, n_scr = len(args), len(out_shape), len(scratch_shapes)
    c_in, c_out = len(comm.inputs), len(comm.out_shapes)
    steps = 1
    for g in grid:
        steps *= g

    def hosted(*refs):
        pos = 0
        parts = []
        for size in (n_in, c_in, n_out, c_out, n_scr, len(comm.scratch)):
            parts.append(refs[pos:pos + size])
            pos += size
        ins, cin, outs, cout, scr, cscr = parts
        step = 0
        for axis, g in enumerate(grid):
            step = step * g + pl.program_id(axis)

        @pl.when(step == 0)
        def _():
            comm.start(cin, cout, cscr)

        body(*ins, *outs, *scr)
        if comm.mid is not None and steps >= 4:
            @pl.when(step == (3 * steps) // 4)
            def _():
                comm.mid(cin, cout, cscr)

        @pl.when(step == steps - 1)
        def _():
            if comm.mid is not None and steps < 4:
                comm.mid(cin, cout, cscr)
            comm.finish(cin, cout, cscr)

    res = pl.pallas_call(
        hosted, grid=grid, in_specs=list(in_specs) + [ANY] * c_in, out_specs=list(out_specs) + [ANY] * c_out,
        out_shape=list(out_shape) + list(comm.out_shapes), scratch_shapes=list(scratch_shapes) + list(comm.scratch),
        compiler_params=params, name=name)(*args, *comm.inputs)
    return res[:n_out], res[n_out:]


def _rows(fn, rows_in, vecs_in, rows_out, vecs_out, *, tile, name, comm=None):
    norm = [r if isinstance(r, tuple) else (r, r.shape[1], 0) for r in rows_in]
    n_rows = norm[0][0].shape[0]
    n_tiles = n_rows // tile
    in_specs, args = [], []
    for arr, width, cb in norm:
        in_specs.append(pl.BlockSpec((tile, width), functools.partial(lambda i, cb: (i, cb), cb=cb)))
        args.append(arr)
    for v in vecs_in:
        in_specs.append(pl.BlockSpec((1, v.shape[1]), lambda i: (0, 0)))
        args.append(v)
    out_shape = [jax.ShapeDtypeStruct((n_rows, w), dt) for w, dt in rows_out]
    out_shape += [jax.ShapeDtypeStruct((1, w), F32) for w in vecs_out]
    out_specs = [pl.BlockSpec((tile, w), lambda i: (i, 0)) for w, _ in rows_out]
    out_specs += [pl.BlockSpec((1, w), lambda i: (0, 0)) for w in vecs_out]
    n_in, n_ro = len(args), len(rows_out)

    def body(*refs):
        vals = [r[...] for r in refs[:n_in]]
        outs = refs[n_in:]
        row_vals, vec_vals = fn(*vals)
        for ref, val in zip(outs[:n_ro], row_vals):
            if isinstance(val, tuple):
                w = val[0].shape[1]
                for j, piece in enumerate(val):
                    ref[:, j * w:(j + 1) * w] = piece.astype(ref.dtype)
            else:
                ref[...] = val.astype(ref.dtype)
        if vecs_out:
            @pl.when(pl.program_id(0) == 0)
            def _():
                for ref in outs[n_ro:]:
                    ref[...] = jnp.zeros_like(ref)
            for ref, val in zip(outs[n_ro:], vec_vals):
                ref[...] += val

    return _call(body, grid=(n_tiles,), in_specs=in_specs, out_specs=out_specs, out_shape=out_shape, args=args,
                 name=name, comm=comm)


def _colsum(x):
    return jnp.sum(x, axis=0, keepdims=True)


def _rms_stats(h):
    r = lax.rsqrt(jnp.mean(h * h, axis=-1, keepdims=True) + RMS_EPS)
    return r, h * r


def _rms_back(r, xn, dxn):
    return r * (dxn - xn * jnp.mean(dxn * xn, axis=-1, keepdims=True))


def _norm_mod_fwd(h, gain, scale, shift, name):
    def fn(h, gain, scale, shift):
        _, xn = _rms_stats(h)
        return [(xn * gain) * (1.0 + scale) + shift], []
    return _rows(fn, [h], [gain, scale, shift], [(h.shape[1], BF16)], [], tile=512, name=name)[0]


def _branch_back(dh, f, gate, coef):
    return (coef * gate) * dh, coef * _colsum(f.astype(F32) * dh)


def _norm_mod_bwd(dn, h, dh_in, gain, scale, name, branch=None, comm=None):
    d = h.shape[1]

    def back(dn, h, dh_in, gain, scale):
        r, xn = _rms_stats(h)
        y = xn * gain
        dy = dn * (1.0 + scale)
        dh = dh_in + _rms_back(r, xn, dy * gain)
        return dh, [_colsum(dn), _colsum(dn * y), _colsum(dy * xn)]

    if branch is None:
        def fn(dn, h, dh_in, gain, scale):
            dh, vecs = back(dn, h, dh_in, gain, scale)
            return [dh], vecs
        return _rows(fn, [dn, h, dh_in], [gain, scale], [(d, F32)], [d, d, d], tile=256, name=name, comm=comm)
    f, gate, coef = branch

    def fn_branch(dn, h, dh_in, f, gain, scale, gate):
        dh, vecs = back(dn, h, dh_in, gain, scale)
        df, dgate = _branch_back(dh, f, gate, coef)
        return [dh, df], vecs + [dgate]
    return _rows(fn_branch, [dn, h, dh_in, f], [gain, scale, gate], [(d, F32), (d, BF16)], [d, d, d, d], tile=256,
                 name=name, comm=comm)


def _final_loss(h, target, gain, f, gate, coef, name):
    d = h.shape[1]

    def fn(h, target, f, gain, gate):
        r, xn = _rms_stats(h)
        err = xn * gain - target
        dout = err * (1.0 / d)
        dh = _rms_back(r, xn, dout * gain)
        df, dgate = _branch_back(dh, f, gate, coef)
        return [dh, df], [_colsum(err * err), _colsum(dout * xn), dgate]
    return _rows(fn, [h, target, f], [gain, gate], [(d, F32), (d, BF16)], [d, d, d], tile=256, name=name)


def _partner(x):
    width = x.shape[1]
    lane = lax.broadcasted_iota(jnp.int32, x.shape, 1) % HEAD_DIM
    return jnp.where(lane < HALF_HEAD, pltpu.roll(x, width - HALF_HEAD, 1), pltpu.roll(x, HALF_HEAD, 1))


def _proj_rope(n, w_t, cos, sin_signed, width, name):
    s, kdim = n.shape
    n_cols = w_t.shape[0]
    tm = _tile(s, 512, 8)
    qscale = HEAD_DIM ** -0.5

    def body(n_ref, w_ref, cos_ref, sin_ref, o_ref):
        j = pl.program_id(0)
        acc = lax.dot_general(n_ref[...].astype(BF16), w_ref[...].astype(BF16), (((1,), (1,)), ((), ())),
                              preferred_element_type=F32)

        @pl.when(j >= 2)
        def _():
            o_ref[...] = acc

        @pl.when(j < 2)
        def _():
            rot = acc * cos_ref[...] + _partner(acc) * sin_ref[...]
            o_ref[...] = jnp.where(j == 0, qscale, 1.0) * rot

    table = pl.BlockSpec((tm, width), lambda j, i: (jnp.where(j < 2, i, 0), 0))
    return pl.pallas_call(
        body, grid=(n_cols // width, s // tm),
        in_specs=[pl.BlockSpec((tm, kdim), lambda j, i: (i, 0)), pl.BlockSpec((width, kdim), lambda j, i: (j, 0)),
                  table, table],
        out_specs=pl.BlockSpec((tm, width), lambda j, i: (i, j)), out_shape=jax.ShapeDtypeStruct((s, n_cols), F32),
        compiler_params=_params(2), name=name)(n, w_t, cos, sin_signed)


def _dproj_assemble(dq, dk, dv, dga, dgb, cos, sin_signed, name):
    width = dq.shape[1]
    qscale = HEAD_DIM ** -0.5

    def fn(dq, dk, dv, dga, dgb, cos, sin):
        dq0 = (dq * cos - _partner(dq) * sin) * qscale
        dk0 = dk * cos - _partner(dk) * sin
        return [(dq0, dk0, dv, dga, dgb)], []
    return _rows(fn, [dq, dk, dv, dga, dgb, cos, sin_signed], [], [(5 * width, BF16)], [], tile=256, name=name)[0]


def _mix_post_fwd(attn, u1, attn_g, ln_g, ln_b, conv_g, name):
    def fn(attn, u1, attn_g, ln_g, ln_b, conv_g):
        _, xa = _rms_stats(attn)
        mu = jnp.mean(u1, axis=-1, keepdims=True)
        xc = u1 - mu
        rstd = lax.rsqrt(jnp.mean(xc * xc, axis=-1, keepdims=True) + LN_EPS)
        u2 = (xc * rstd) * ln_g + ln_b
        u3 = u2 * _sigmoid(u2)
        _, x3 = _rms_stats(u3)
        return [(xa * attn_g, x3 * conv_g)], []
    w = attn.shape[1]
    return _rows(fn, [attn, u1], [attn_g, ln_g, ln_b, conv_g], [(2 * w, BF16)], [], tile=512, name=name)[0]


def _mix_post_bwd(dy, attn, u1, attn_g, ln_g, ln_b, conv_g, name):
    w = attn.shape[1]

    def fn(dya, dyc, attn, u1, attn_g, ln_g, ln_b, conv_g):
        ra, xa = _rms_stats(attn)
        dattn = _rms_back(ra, xa, dya * attn_g)
        mu = jnp.mean(u1, axis=-1, keepdims=True)
        xc = u1 - mu
        rstd = lax.rsqrt(jnp.mean(xc * xc, axis=-1, keepdims=True) + LN_EPS)
        xh = xc * rstd
        u2 = xh * ln_g + ln_b
        sig = _sigmoid(u2)
        u3 = u2 * sig
        r3, x3 = _rms_stats(u3)
        du3 = _rms_back(r3, x3, dyc * conv_g)
        du2 = du3 * (sig + u3 * (1.0 - sig))
        dxh = du2 * ln_g
        du1 = rstd * (dxh - jnp.mean(dxh, axis=-1, keepdims=True) - xh * jnp.mean(dxh * xh, axis=-1, keepdims=True))
        return [dattn, du1], [_colsum(dya * xa), _colsum(dyc * x3), _colsum(du2 * xh), _colsum(du2)]
    return _rows(fn, [(dy, w, 0), (dy, w, 1), attn, u1], [attn_g, ln_g, ln_b, conv_g], [(w, F32), (w, F32)],
                 [w, w, w, w], tile=256, name=name)


def _silu_rows(c_all, name):
    def fn(c):
        return [c * _sigmoid(c)], []
    return _rows(fn, [c_all], [], [(c_all.shape[1], BF16)], [], tile=c_all.shape[0], name=name)[0]


def _mm(groups, epi, extras, vecs, outs, *, trans_rhs, tm, tn, name, comm=None):
    m = groups[0][0][0].shape[0]
    n = groups[0][0][1].shape[0] if trans_rhs else groups[0][0][1].shape[1]
    tm, tn = min(tm, m), min(tn, n)
    in_specs, args = [], []
    for grp in groups:
        for lhs, rhs in grp:
            k = lhs.shape[1]
            in_specs.append(pl.BlockSpec((tm, k), lambda j, i: (i, 0)))
            in_specs.append(pl.BlockSpec((tn, k), lambda j, i: (j, 0)) if trans_rhs
                            else pl.BlockSpec((k, tn), lambda j, i: (0, j)))
            args += [lhs, rhs]
    for e in extras:
        in_specs.append(pl.BlockSpec((tm, tn), lambda j, i: (i, j)))
        args.append(e)
    for v in vecs:
        in_specs.append(pl.BlockSpec((1, tn), lambda j, i: (0, j)))
        args.append(v)
    sizes = [len(g) for g in groups]
    n_mm, n_ex, n_vec = 2 * sum(sizes), len(extras), len(vecs)
    dims = (((1,), (1,)), ((), ())) if trans_rhs else (((1,), (0,)), ((), ()))

    def body(*refs):
        accs, pos = [], 0
        for size in sizes:
            acc = None
            for _ in range(size):
                part = lax.dot_general(refs[pos][...].astype(BF16), refs[pos + 1][...].astype(BF16), dims,
                                       preferred_element_type=F32)
                acc = part if acc is None else acc + part
                pos += 2
            accs.append(acc)
        ex = [r[...] for r in refs[n_mm:n_mm + n_ex]]
        vc = [r[...] for r in refs[n_mm + n_ex:n_mm + n_ex + n_vec]]
        for ref, val in zip(refs[n_mm + n_ex + n_vec:], epi(accs, ex, vc)):
            ref[...] = val.astype(ref.dtype)

    return _call(body, grid=(n // tn, m // tm), in_specs=in_specs,
                 out_specs=[pl.BlockSpec((tm, tn), lambda j, i: (i, j)) for _ in outs],
                 out_shape=[jax.ShapeDtypeStruct((m, n), dt) for dt in outs], args=args, name=name, comm=comm)


def _mm_tn(lhs, rhs, name, comm=None):
    t, a = lhs.shape
    b = rhs.shape[1]
    ta = a if a <= 1536 else _tile(a, 1536, LANES)
    tk = _tile(t, 512, 8)

    def body(l_ref, r_ref, o_ref):
        @pl.when(pl.program_id(1) == 0)
        def _():
            o_ref[...] = jnp.zeros_like(o_ref)
        o_ref[...] += lax.dot_general(l_ref[...].astype(BF16), r_ref[...].astype(BF16), (((0,), (0,)), ((), ())),
                                      preferred_element_type=F32)

    res = _call(body, grid=(a // ta, t // tk),
                in_specs=[pl.BlockSpec((tk, ta), lambda i, k: (k, i)), pl.BlockSpec((tk, b), lambda i, k: (k, 0))],
                out_specs=[pl.BlockSpec((ta, b), lambda i, k: (i, 0))], out_shape=[jax.ShapeDtypeStruct((a, b), F32)],
                args=(lhs, rhs), name=name, comm=comm)
    return res[0] if comm is None else (res[0][0], res[1])


def _ffn_tn(f):
    return _tile(f, 1536, LANES)


def _ffn_up(n, wg_t, wu_t, name, comm=None):
    def epi(accs, ex, vc):
        a, b = accs
        return [a, b, (a * _sigmoid(a)) * b]
    return _mm([[(n, wg_t)], [(n, wu_t)]], epi, [], [], [BF16, BF16, BF16], trans_rhs=True, tm=256,
               tn=_ffn_tn(wg_t.shape[0]), name=name, comm=comm)


def _residual_mm(lhs, w, res, gate, coef, name, norm=None, comm=None):
    def epi(accs, ex, vc):
        h = ex[0] + (coef * vc[0]) * accs[0]
        if norm is None:
            return [h, accs[0]]
        _, xn = _rms_stats(h)
        return [h, accs[0], (xn * vc[1]) * (1.0 + vc[2]) + vc[3]]
    vecs = [gate] + (list(norm) if norm is not None else [])
    outs = [F32, BF16] + ([BF16] if norm is not None else [])
    return _mm([[(lhs, w)]], epi, [res], vecs, outs, trans_rhs=False, tm=512, tn=w.shape[1], name=name, comm=comm)


def _ffn_bwd_hidden(df, wd, a, b, name, comm=None):
    def epi(accs, ex, vc):
        dh = accs[0]
        av, bv = ex[0].astype(F32), ex[1].astype(F32)
        sig = _sigmoid(av)
        silu = av * sig
        return [dh * bv * (sig + silu * (1.0 - sig)), dh * silu]
    return _mm([[(df, wd)]], epi, [a, b], [], [BF16, BF16], trans_rhs=True, tm=256, tn=_ffn_tn(wd.shape[0]),
               name=name, comm=comm)


def _plain_mm(pairs, out_dtype, trans_rhs, tn, name, tm=512, comm=None):
    def epi(accs, ex, vc):
        return [accs[0]]
    res = _mm([pairs], epi, [], [], [out_dtype], trans_rhs=trans_rhs, tm=tm, tn=tn, name=name, comm=comm)
    return res[0] if comm is None else (res[0][0], res[1])


HEADS_PER_TILE = LANES // HEAD_DIM


def _stack_heads(x):
    lane = lax.broadcasted_iota(jnp.int32, (1, LANES), 1)
    return jnp.concatenate([x * (lane // HEAD_DIM == h).astype(F32) for h in range(HEADS_PER_TILE)], axis=0)


def _unstack_heads(y):
    r = y.shape[0] // HEADS_PER_TILE
    lane = lax.broadcasted_iota(jnp.int32, (r, y.shape[1]), 1)
    out = y[0:r]
    for h in range(1, HEADS_PER_TILE):
        out = jnp.where(lane // HEAD_DIM == h, y[h * r:(h + 1) * r], out)
    return out


def _stacked_lse(lb):
    return jnp.concatenate([_lane_pick(lb, h) for h in range(HEADS_PER_TILE)], axis=0)


def _band_masks(n_row_blocks, n_col_blocks):
    shape = (n_row_blocks * BLOCK, n_col_blocks * BLOCK)
    qi = lax.broadcasted_iota(jnp.int32, shape, 0) % BLOCK
    kj = lax.broadcasted_iota(jnp.int32, shape, 1) % BLOCK
    return kj <= qi, kj >= qi


def _query_masks():
    same_ok, before_ok = _band_masks(HEADS_PER_TILE, 2)
    is_cur = lax.broadcasted_iota(jnp.int32, same_ok.shape, 1) >= BLOCK
    return jnp.logical_and(is_cur, same_ok), jnp.logical_and(jnp.logical_not(is_cur), before_ok)


def _dot_nt(a, b):
    return lax.dot_general(a.astype(BF16), b.astype(BF16), (((1,), (1,)), ((), ())), preferred_element_type=F32)


def _dot_nn(a, b):
    return lax.dot_general(a.astype(BF16), b.astype(BF16), (((1,), (0,)), ((), ())), preferred_element_type=F32)


def _dot_tn(a, b):
    return lax.dot_general(a.astype(BF16), b.astype(BF16), (((0,), (0,)), ((), ())), preferred_element_type=F32)


def _lane_pick(x, h):
    lane = lax.broadcasted_iota(jnp.int32, x.shape, 1)
    return jnp.sum(jnp.where(lane == h * HEAD_DIM, x, 0.0), axis=1, keepdims=True)


def _block_rows(idx, d):
    span = BLOCK * d
    g = idx // d
    q0 = g * span + idx % d
    has_prev = g > 0
    p0 = jnp.where(has_prev, q0 - span, q0)
    return pl.ds(q0, BLOCK, stride=d), pl.ds(p0, BLOCK, stride=d), has_prev


def _qkv_specs(s, tiles):
    q, k, v = [pl.BlockSpec((s, LANES), functools.partial(lambda hb, off: (0, off + hb), off=i * tiles))
               for i in range(3)]
    return q, k, v, pl.BlockSpec((s, LANES), lambda hb: (0, hb))


def _attn_seq_fwd(proj, width, name, comm=None):
    s = proj.shape[0]
    q_spec, k_spec, v_spec, cur = _qkv_specs(s, width // LANES)

    def body(q_ref, k_ref, v_ref, o_ref, l_ref, o_s, l_s):
        cur_valid, prev_valid = _query_masks()
        for bi, d in enumerate(DILATIONS):
            def blk(idx, carry, bi=bi, d=d):
                rows, prev, has_prev = _block_rows(idx, d)
                q2 = _stack_heads(q_ref[rows, :])
                keys = jnp.concatenate([k_ref[prev, :], k_ref[rows, :]], axis=0)
                vals = jnp.concatenate([v_ref[prev, :], v_ref[rows, :]], axis=0)
                valid = jnp.logical_or(cur_valid, jnp.logical_and(prev_valid, has_prev))
                sc = jnp.where(valid, _dot_nt(q2, keys), NEG)
                mx = jnp.max(sc, axis=1, keepdims=True)
                p = jnp.exp(sc - mx)
                den = jnp.sum(p, axis=1, keepdims=True)
                o_s[bi, rows, :] = _unstack_heads(_dot_nn(p, vals) / den)
                l_s[bi, rows, :] = _unstack_heads(jnp.broadcast_to(mx + jnp.log(den), (q2.shape[0], LANES)))
                return carry

            lax.fori_loop(0, s // BLOCK, blk, 0, unroll=8)
        for c in range(s // MERGE_CHUNK):
            rows = slice(c * MERGE_CHUNK, (c + 1) * MERGE_CHUNK)
            ls = [l_s[bi, rows, :] for bi in range(len(DILATIONS))]
            top = functools.reduce(jnp.maximum, ls)
            ws = [jnp.exp(l - top) for l in ls]
            den = functools.reduce(lambda a, b: a + b, ws)
            num = functools.reduce(lambda a, b: a + b, [w * o_s[bi, rows, :] for bi, w in enumerate(ws)])
            o_ref[rows, :] = num / den
            l_ref[rows, :] = top + jnp.log(den)

    return _call(
        body, grid=(width // LANES,), in_specs=[q_spec, k_spec, v_spec], out_specs=[cur, cur],
        out_shape=[jax.ShapeDtypeStruct((s, width), F32)] * 2,
        scratch_shapes=[pltpu.VMEM((len(DILATIONS), s, LANES), F32)] * 2,
        args=(proj, proj, proj), name=name, comm=comm)


def _attn_seq_bwd(proj, do, o, lse, name, comm=None):
    s, width = do.shape
    q_spec, k_spec, v_spec, cur = _qkv_specs(s, width // LANES)

    def body(q_ref, k_ref, v_ref, do_ref, o_ref, l_ref, dq_ref, dk_ref, dv_ref):
        dq_ref[...] = jnp.zeros_like(dq_ref)
        dk_ref[...] = jnp.zeros_like(dk_ref)
        dv_ref[...] = jnp.zeros_like(dv_ref)
        cur_valid, prev_valid = _query_masks()
        for d in DILATIONS:
            def blk(idx, carry, d=d):
                rows, prev, has_prev = _block_rows(idx, d)
                dob = do_ref[rows, :]
                q2 = _stack_heads(q_ref[rows, :])
                do2 = _stack_heads(dob)
                delta = jnp.sum(_stack_heads(dob * o_ref[rows, :]), axis=1, keepdims=True)
                lse2 = _stacked_lse(l_ref[rows, :])
                keys = jnp.concatenate([k_ref[prev, :], k_ref[rows, :]], axis=0)
                vals = jnp.concatenate([v_ref[prev, :], v_ref[rows, :]], axis=0)
                valid = jnp.logical_or(cur_valid, jnp.logical_and(prev_valid, has_prev))
                p = jnp.where(valid, jnp.exp(_dot_nt(q2, keys) - lse2), 0.0)
                ds = p * (_dot_nt(do2, vals) - delta)
                dq_ref[rows, :] += _unstack_heads(_dot_nn(ds, keys))
                dkk = _dot_tn(ds, q2)
                dvv = _dot_tn(p, do2)
                dk_ref[prev, :] += dkk[0:BLOCK]
                dk_ref[rows, :] += dkk[BLOCK:]
                dv_ref[prev, :] += dvv[0:BLOCK]
                dv_ref[rows, :] += dvv[BLOCK:]
                return carry

            lax.fori_loop(0, s // BLOCK, blk, 0, unroll=4)

    return _call(
        body, grid=(width // LANES,), in_specs=[q_spec, k_spec, v_spec, cur, cur, cur], out_specs=[cur, cur, cur],
        out_shape=[jax.ShapeDtypeStruct((s, width), F32)] * 3,
        args=(proj, proj, proj, do, o, lse), name=name, comm=comm)


def _conv_specs(s, a_block, b_block):
    per = CONV_CHUNK // CONV_HALO
    a_cur = pl.BlockSpec((CONV_CHUNK, LANES), lambda cb, i: (i, a_block + cb))
    b_cur = pl.BlockSpec((CONV_CHUNK, LANES), lambda cb, i: (i, b_block + cb))
    a_halo = pl.BlockSpec((CONV_HALO, LANES), lambda cb, i: (jnp.maximum(i * per - 1, 0), a_block + cb))
    b_halo = pl.BlockSpec((CONV_HALO, LANES), lambda cb, i: (jnp.maximum(i * per - 1, 0), b_block + cb))
    w_spec = pl.BlockSpec((CONV_KERNEL, LANES), lambda cb, i: (0, cb))
    vec = pl.BlockSpec((1, LANES), lambda cb, i: (0, cb))
    out = pl.BlockSpec((CONV_CHUNK, LANES), lambda cb, i: (i, cb))
    return a_cur, b_cur, a_halo, b_halo, w_spec, vec, out


def _fill_glu_window(win, a_ref, b_ref, ah_ref, bh_ref, first):
    halo = ah_ref[...] * _sigmoid(bh_ref[...])
    win[0:CONV_HALO, :] = jnp.where(first, 0.0, halo)
    win[CONV_HALO:, :] = a_ref[...] * _sigmoid(b_ref[...])


def _conv_fwd(proj, a_block, b_block, w, bias, name):
    s = proj.shape[0]
    cw = w.shape[1]
    a_cur, b_cur, a_halo, b_halo, w_spec, vec, out = _conv_specs(s, a_block, b_block)
    lead = CONV_HALO - (CONV_KERNEL - 1)

    def body(a_ref, b_ref, ah_ref, bh_ref, w_ref, bias_ref, o_ref, win):
        _fill_glu_window(win, a_ref, b_ref, ah_ref, bh_ref, pl.program_id(1) == 0)
        for sub in range(CONV_CHUNK // CONV_SUB):
            base = sub * CONV_SUB
            acc = jnp.zeros((CONV_SUB, LANES), F32) + bias_ref[...]
            for j in range(CONV_KERNEL):
                acc = acc + w_ref[j:j + 1, :] * win[base + lead + j:base + lead + j + CONV_SUB, :]
            o_ref[base:base + CONV_SUB, :] = acc

    return pl.pallas_call(
        body, grid=(cw // LANES, s // CONV_CHUNK), in_specs=[a_cur, b_cur, a_halo, b_halo, w_spec, vec],
        out_specs=out, out_shape=jax.ShapeDtypeStruct((s, cw), F32),
        scratch_shapes=[pltpu.VMEM((CONV_CHUNK + CONV_HALO, LANES), F32)],
        compiler_params=_params(2), name=name)(proj, proj, proj, proj, w, bias)


def _conv_bwd(proj, a_block, b_block, w, du1, name):
    s = proj.shape[0]
    cw = w.shape[1]
    a_cur, b_cur, a_halo, b_halo, w_spec, vec, out = _conv_specs(s, a_block, b_block)
    per = CONV_CHUNK // CONV_HALO
    n_chunks = s // CONV_CHUNK
    d_next = pl.BlockSpec((CONV_HALO, LANES), lambda cb, i: (jnp.minimum((i + 1) * per, s // CONV_HALO - 1), cb))
    lead = CONV_HALO - (CONV_KERNEL - 1)

    def body(a_ref, b_ref, ah_ref, bh_ref, w_ref, d_ref, dn_ref, da_ref, db_ref, dw_ref, dbias_ref, win, dwin):
        i = pl.program_id(1)
        _fill_glu_window(win, a_ref, b_ref, ah_ref, bh_ref, i == 0)
        dwin[0:CONV_CHUNK, :] = d_ref[...]
        dwin[CONV_CHUNK:, :] = jnp.where(i == n_chunks - 1, 0.0, dn_ref[...])

        @pl.when(i == 0)
        def _():
            dw_ref[...] = jnp.zeros_like(dw_ref)
            dbias_ref[...] = jnp.zeros_like(dbias_ref)

        dbias_ref[...] += _colsum(d_ref[...])
        for sub in range(CONV_CHUNK // CONV_SUB):
            base = sub * CONV_SUB
            dcur = dwin[base:base + CONV_SUB, :]
            du0 = jnp.zeros((CONV_SUB, LANES), F32)
            for j in range(CONV_KERNEL):
                back = CONV_KERNEL - 1 - j
                du0 = du0 + w_ref[j:j + 1, :] * dwin[base + back:base + back + CONV_SUB, :]
                dw_ref[j:j + 1, :] += _colsum(dcur * win[base + lead + j:base + lead + j + CONV_SUB, :])
            av = a_ref[base:base + CONV_SUB, :]
            sig = _sigmoid(b_ref[base:base + CONV_SUB, :])
            da_ref[base:base + CONV_SUB, :] = du0 * sig
            db_ref[base:base + CONV_SUB, :] = du0 * av * sig * (1.0 - sig)

    return pl.pallas_call(
        body, grid=(cw // LANES, n_chunks), in_specs=[a_cur, b_cur, a_halo, b_halo, w_spec, out, d_next],
        out_specs=[out, out, w_spec, vec],
        out_shape=[jax.ShapeDtypeStruct((s, cw), F32), jax.ShapeDtypeStruct((s, cw), F32),
                   jax.ShapeDtypeStruct((CONV_KERNEL, cw), F32), jax.ShapeDtypeStruct((1, cw), F32)],
        scratch_shapes=[pltpu.VMEM((CONV_CHUNK + CONV_HALO, LANES), F32)] * 2,
        compiler_params=_params(2), name=name)(proj, proj, proj, proj, w, du1, du1)


def _adamw_math(w, g, m, v):
    m = ADAM_B1 * m + (1.0 - ADAM_B1) * g
    v = ADAM_B2 * v + (1.0 - ADAM_B2) * (g * g)
    m_hat = m / (1.0 - ADAM_B1 ** ADAM_STEP)
    v_hat = v / (1.0 - ADAM_B2 ** ADAM_STEP)
    delta = -ADAM_LR * (m_hat / (jnp.sqrt(v_hat) + ADAM_EPS) + ADAM_WD * w)
    return delta, m, v


def _adamw_big(w, g, m, v, name):
    rows, cols = w.shape
    tile = _tile(rows, 256, 8)
    spec = pl.BlockSpec((tile, cols), lambda i: (i, 0))

    def body(w_ref, g_ref, m_ref, v_ref, d_out, m_out, v_out):
        d_out[...], m_out[...], v_out[...] = _adamw_math(w_ref[...], g_ref[...], m_ref[...], v_ref[...])

    return pl.pallas_call(body, grid=(rows // tile,), in_specs=[spec] * 4, out_specs=[spec] * 3,
                          out_shape=[jax.ShapeDtypeStruct(w.shape, F32)] * 3, compiler_params=_params(1),
                          name=name)(w, g, m, v)


def _adamw_reduced(w, land, m, v, name):
    rows, cols = w.shape
    tile = _tile(rows, 256, 16)
    spec = pl.BlockSpec((tile, cols), lambda i: (i, 0))

    def body(w_ref, l_ref, m_ref, v_ref, g_out, d_out, m_out, v_out):
        g = l_ref[0].astype(F32)
        for q in range(1, N_CHIP):
            g = g + l_ref[q].astype(F32)
        g_out[...] = g
        d_out[...], m_out[...], v_out[...] = _adamw_math(w_ref[...], g, m_ref[...], v_ref[...])

    return pl.pallas_call(body, grid=(rows // tile,),
                          in_specs=[spec, pl.BlockSpec((N_CHIP, tile, cols), lambda i: (0, i, 0)), spec, spec],
                          out_specs=[spec] * 4, out_shape=[jax.ShapeDtypeStruct(w.shape, F32)] * 4,
                          compiler_params=_params(1), name=name)(w, land, m, v)


def _adamw_small(ws, gs, ms, vs, name):
    n = len(ws)

    def body(*refs):
        ins, outs = refs[:4 * n], refs[4 * n:]
        for t in range(n):
            res = _adamw_math(ins[t][...], ins[n + t][...], ins[2 * n + t][...], ins[3 * n + t][...])
            for j in range(3):
                outs[j * n + t][...] = res[j]

    shapes = [jax.ShapeDtypeStruct(w.shape, F32) for w in ws]
    res = pl.pallas_call(body, out_shape=shapes * 3, compiler_params=pltpu.CompilerParams(vmem_limit_bytes=VMEM_LIMIT),
                         name=name)(*ws, *gs, *ms, *vs)
    return res[:n], res[n:2 * n], res[2 * n:]


def _sum_blocks(x, n_blocks, name):
    r = x.shape[0] // n_blocks

    def body(x_ref, o_ref):
        acc = x_ref[0:r, :]
        for b in range(1, n_blocks):
            acc = acc + x_ref[b * r:(b + 1) * r, :]
        o_ref[...] = acc

    return pl.pallas_call(body, out_shape=jax.ShapeDtypeStruct((r, x.shape[1]), F32),
                          compiler_params=pltpu.CompilerParams(vmem_limit_bytes=VMEM_LIMIT), name=name)(x)


def _coords():
    return lax.axis_index("x"), lax.axis_index("y"), lax.axis_index("c")


def _flip(v, bit):
    return 1 - v if bit else v


def _ag_small(x, name):
    r, c = x.shape

    def body(x_ref, o_ref, send, recv, local_sem):
        mx, my, mc = _coords()

        def rows(px, py, pc):
            return o_ref.at[pl.ds(pl.multiple_of((4 * px + 2 * py + pc) * r, 8), r), :]

        local = pltpu.make_async_copy(x_ref, rows(mx, my, mc), local_sem)
        local.start()
        peers = [(_flip(mx, k >> 2 & 1), _flip(my, k >> 1 & 1), _flip(mc, k & 1)) for k in range(1, N_DEV)]
        sends = [pltpu.make_async_remote_copy(x_ref, rows(mx, my, mc), send.at[k], recv.at[k], device_id=p,
                                              device_id_type=MESH) for k, p in enumerate(peers)]
        for cp in sends:
            cp.start()
        for k, p in enumerate(peers):
            pltpu.make_async_remote_copy(x_ref, rows(*p), send.at[k], recv.at[k], device_id=p,
                                         device_id_type=MESH).wait_recv()
        for cp in sends:
            cp.wait_send()
        local.wait()

    vm = pl.BlockSpec(memory_space=pltpu.VMEM)
    return pl.pallas_call(
        body, in_specs=[vm], out_specs=vm, out_shape=jax.ShapeDtypeStruct((N_DEV * r, c), x.dtype),
        scratch_shapes=[pltpu.SemaphoreType.DMA((N_DEV - 1,)), pltpu.SemaphoreType.DMA((N_DEV - 1,)),
                        pltpu.SemaphoreType.DMA(())],
        name=name)(x)


class _GatherSmall:
    mid = None

    def __init__(self, x):
        self.inputs = [x]
        self.out_shapes = [jax.ShapeDtypeStruct((N_DEV * x.shape[0], x.shape[1]), x.dtype)]
        self.scratch = [pltpu.SemaphoreType.DMA((N_DEV - 1,)), pltpu.SemaphoreType.DMA((N_DEV - 1,)),
                        pltpu.SemaphoreType.DMA(())]

    def _plan(self, x_refs, o_refs, sems):
        send, recv, local_sem = sems
        x_ref, o_ref = x_refs[0], o_refs[0]
        r = x_ref.shape[0]
        mx, my, mc = _coords()

        def rows(px, py, pc):
            return o_ref.at[pl.ds(pl.multiple_of((4 * px + 2 * py + pc) * r, 8), r), :]

        peers = [(_flip(mx, k >> 2 & 1), _flip(my, k >> 1 & 1), _flip(mc, k & 1)) for k in range(1, N_DEV)]
        out = [pltpu.make_async_remote_copy(x_ref, rows(mx, my, mc), send.at[k], recv.at[k], device_id=p,
                                            device_id_type=MESH) for k, p in enumerate(peers)]
        arrivals = [pltpu.make_async_remote_copy(x_ref, rows(*p), send.at[k], recv.at[k], device_id=p,
                                                 device_id_type=MESH) for k, p in enumerate(peers)]
        return out, arrivals, pltpu.make_async_copy(x_ref, rows(mx, my, mc), local_sem)

    def start(self, x_refs, o_refs, sems):
        out, _, local = self._plan(x_refs, o_refs, sems)
        local.start()
        for cp in out:
            cp.start()

    def finish(self, x_refs, o_refs, sems):
        out, arrivals, local = self._plan(x_refs, o_refs, sems)
        for cp in arrivals:
            cp.wait_recv()
        for cp in out:
            cp.wait_send()
        local.wait()


class _GatherWeights:
    def __init__(self, shards):
        n_t = len(shards)
        self.inputs = list(shards)
        self.out_shapes = [jax.ShapeDtypeStruct((N_DEV * x.shape[0], x.shape[1]), x.dtype) for x in shards]
        self.scratch = [pltpu.SemaphoreType.DMA((n_t, 7)), pltpu.SemaphoreType.DMA((n_t, 7)),
                        pltpu.SemaphoreType.DMA((n_t,))]

    def _plan(self, x_refs, o_refs, sems):
        send, recv, local_sem = sems
        mx, my, mc = _coords()
        me, sibling = (mx, my, mc), (mx, my, 1 - mc)
        chips = [(1 - mx, my), (mx, 1 - my), (1 - mx, 1 - my)]

        def rows(t, px, py, pc):
            r = x_refs[t].shape[0]
            return o_refs[t].at[pl.ds(pl.multiple_of((4 * px + 2 * py + pc) * r, 8), r), :]

        def copy(t, k, block, to, src=None):
            return pltpu.make_async_remote_copy(
                src_ref=rows(t, *block) if src is None else src, dst_ref=rows(t, *block),
                send_sem=send.at[t, k], recv_sem=recv.at[t, k], device_id=to, device_id_type=MESH)

        def local(t):
            return pltpu.make_async_copy(x_refs[t], rows(t, *me), local_sem.at[t])

        return me, sibling, chips, mc, copy, local

    def start(self, x_refs, o_refs, sems):
        me, sibling, chips, mc, copy, local = self._plan(x_refs, o_refs, sems)
        for t in range(len(x_refs)):
            local(t).start()
            copy(t, 0, me, sibling, src=x_refs[t]).start()
            for j, chip in enumerate(chips):
                copy(t, 1 + j, me, (*chip, mc), src=x_refs[t]).start()

    def mid(self, x_refs, o_refs, sems):
        me, sibling, chips, mc, copy, local = self._plan(x_refs, o_refs, sems)
        for j, chip in enumerate(chips):
            for t in range(len(x_refs)):
                copy(t, 1 + j, (*chip, mc), me).wait_recv()
                copy(t, 4 + j, (*chip, mc), sibling).start()

    def finish(self, x_refs, o_refs, sems):
        me, sibling, chips, mc, copy, local = self._plan(x_refs, o_refs, sems)
        for t in range(len(x_refs)):
            copy(t, 0, sibling, me).wait_recv()
            for j, chip in enumerate(chips):
                copy(t, 4 + j, (*chip, 1 - mc), me).wait_recv()
            copy(t, 0, me, sibling, src=x_refs[t]).wait_send()
            for j, chip in enumerate(chips):
                copy(t, 1 + j, me, (*chip, mc), src=x_refs[t]).wait_send()
                copy(t, 4 + j, (*chip, mc), sibling).wait_send()
            local(t).wait()


class _SiblingExchange:
    mid = None

    def __init__(self, grads):
        n_t = len(grads)
        self.inputs = list(grads)
        self.out_shapes = [jax.ShapeDtypeStruct((N_CHIP,) + g.shape[2:], F32) for g in grads]
        self.scratch = [pltpu.SemaphoreType.DMA((n_t,)), pltpu.SemaphoreType.DMA((n_t,))]

    def _copies(self, g_refs, land, sems):
        send, recv = sems
        mx, my, mc = _coords()
        return [pltpu.make_async_remote_copy(g_refs[t].at[:, 1 - mc], land[t], send.at[t], recv.at[t],
                                             device_id=(mx, my, 1 - mc), device_id_type=MESH)
                for t in range(len(g_refs))]

    def start(self, g_refs, land, sems):
        for cp in self._copies(g_refs, land, sems):
            cp.start()

    def finish(self, g_refs, land, sems):
        for cp in self._copies(g_refs, land, sems):
            cp.wait()


class _Together:
    def __init__(self, *comms):
        self.comms = comms
        self.inputs = [x for c in comms for x in c.inputs]
        self.out_shapes = [x for c in comms for x in c.out_shapes]
        self.scratch = [x for c in comms for x in c.scratch]
        self.mid = self._mid if any(c.mid is not None for c in comms) else None

    def _each(self, phase, cin, cout, sems):
        i = o = s = 0
        for c in self.comms:
            fn = getattr(c, phase)
            ni, no, ns = len(c.inputs), len(c.out_shapes), len(c.scratch)
            if fn is not None:
                fn(cin[i:i + ni], cout[o:o + no], sems[s:s + ns])
            i, o, s = i + ni, o + no, s + ns

    def start(self, cin, cout, sems):
        self._each("start", cin, cout, sems)

    def _mid(self, cin, cout, sems):
        self._each("mid", cin, cout, sems)

    def finish(self, cin, cout, sems):
        self._each("finish", cin, cout, sems)


def _standalone(comm, name):
    def body():
        pass
    return _call(body, grid=(1,), in_specs=[], out_specs=[], out_shape=[], args=(), name=name, comm=comm)[1]


def _chip_partial(g4, land, name):
    _, _, r, c = g4.shape
    tr = _tile(r, 256, 16)

    def body(g_ref, l_ref, o_ref):
        o_ref[...] = (g_ref[...] + l_ref[...]).astype(o_ref.dtype)

    return pl.pallas_call(
        body, grid=(N_CHIP, r // tr),
        in_specs=[pl.BlockSpec((None, None, tr, c), lambda q, i: (q, lax.axis_index("c"), i, 0)),
                  pl.BlockSpec((None, tr, c), lambda q, i: (q, i, 0))],
        out_specs=pl.BlockSpec((None, tr, c), lambda q, i: (q, i, 0)),
        out_shape=jax.ShapeDtypeStruct((N_CHIP, r, c), BF16), compiler_params=_params(2), name=name)(g4, land)


class _ChipExchange:
    mid = None

    def __init__(self, parts):
        n_t = len(parts)
        self.inputs = list(parts)
        self.out_shapes = [jax.ShapeDtypeStruct(p.shape, p.dtype) for p in parts]
        self.scratch = [pltpu.SemaphoreType.DMA((n_t, 3)), pltpu.SemaphoreType.DMA((n_t, 3)),
                        pltpu.SemaphoreType.DMA((n_t,))]

    def _plan(self, p_refs, land, sems):
        send, recv, local_sem = sems
        mx, my, mc = _coords()
        my_chip = 2 * mx + my
        peers = [(_flip(mx, fx), _flip(my, fy)) for fx, fy in ((1, 0), (0, 1), (1, 1))]

        def out(t, k):
            px, py = peers[k]
            return pltpu.make_async_remote_copy(p_refs[t].at[2 * px + py], land[t].at[my_chip], send.at[t, k],
                                                recv.at[t, k], device_id=(px, py, mc), device_id_type=MESH)

        def arrival(t, k):
            px, py = peers[k]
            return pltpu.make_async_remote_copy(p_refs[t].at[my_chip], land[t].at[2 * px + py], send.at[t, k],
                                                recv.at[t, k], device_id=(px, py, mc), device_id_type=MESH)

        def local(t):
            return pltpu.make_async_copy(p_refs[t].at[my_chip], land[t].at[my_chip], local_sem.at[t])

        return out, arrival, local

    def start(self, p_refs, land, sems):
        out, arrival, local = self._plan(p_refs, land, sems)
        for t in range(len(p_refs)):
            local(t).start()
            for k in range(3):
                out(t, k).start()

    def finish(self, p_refs, land, sems):
        out, arrival, local = self._plan(p_refs, land, sems)
        for t in range(len(p_refs)):
            for k in range(3):
                arrival(t, k).wait_recv()
                out(t, k).wait_send()
            local(t).wait()


def _rope_tables(s, width):
    pos = jnp.arange(s, dtype=F32)
    inv_freq = ROPE_THETA ** (-jnp.arange(0, HEAD_DIM, 2, dtype=F32) / HEAD_DIM)
    ang = pos[:, None] * inv_freq[None, :]
    cos, sin = jnp.cos(ang), jnp.sin(ang)
    heads = width // HEAD_DIM
    return jnp.tile(jnp.concatenate([cos, cos], axis=1), (1, heads)), jnp.tile(jnp.concatenate([-sin, sin], axis=1), (1, heads))


def _pad_rows(v, rows):
    return jnp.concatenate([v, jnp.zeros((rows - 1, v.shape[1]), v.dtype)], axis=0)


def kernel(x, c, w_ada, b_ada, ffn1_norm_g, ffn1_w_gate, ffn1_w_up, ffn1_w_down, mix_norm_g, w_in, conv_dw_w, conv_dw_b, conv_ln_g, conv_ln_b, attn_out_g, conv_out_g, w_out, ffn2_norm_g, ffn2_w_gate, ffn2_w_up, ffn2_w_down, final_norm_g, loss_target, m_w_ada, m_b_ada, m_ffn1_norm_g, m_ffn1_w_gate, m_ffn1_w_up, m_ffn1_w_down, m_mix_norm_g, m_w_in, m_conv_dw_w, m_conv_dw_b, m_conv_ln_g, m_conv_ln_b, m_attn_out_g, m_conv_out_g, m_w_out, m_ffn2_norm_g, m_ffn2_w_gate, m_ffn2_w_up, m_ffn2_w_down, m_final_norm_g, v_w_ada, v_b_ada, v_ffn1_norm_g, v_ffn1_w_gate, v_ffn1_w_up, v_ffn1_w_down, v_mix_norm_g, v_w_in, v_conv_dw_w, v_conv_dw_b, v_conv_ln_g, v_conv_ln_b, v_attn_out_g, v_conv_out_g, v_w_out, v_ffn2_norm_g, v_ffn2_w_gate, v_ffn2_w_up, v_ffn2_w_down, v_final_norm_g):
    mx, my, mc = _coords()
    me = 4 * mx + 2 * my + mc
    s, d = x.shape[1], x.shape[2]
    aw = d // 2
    x2, target = x[0], loss_target[0]
    n_mod = w_ada.shape[2] * N_DEV // d
    mod_cols = w_ada.shape[2]

    def shard(w, transpose):
        return (w[0].T if transpose else w[0]).astype(BF16)

    cw_shard = conv_dw_w.shape[3]
    n_taps = CONV_KERNEL * cw_shard
    first_len = -(-(d + n_taps) // LANES) * LANES
    first = jnp.concatenate([c, conv_dw_w[0, :, 0, :].reshape(1, n_taps), jnp.zeros((1, first_len - d - n_taps), F32)], axis=1)
    first_all, wg1, wu1 = _standalone(
        _Together(_GatherSmall(_pad_rows(first, 8)), _GatherWeights([shard(ffn1_w_gate, True), shard(ffn1_w_up, True)])),
        "ag_first")
    first_all = first_all[0::8]
    c_all = first_all[:, :d]
    conv_w = first_all[:, d:d + n_taps].reshape(N_DEV, CONV_KERNEL, cw_shard).transpose(1, 0, 2).reshape(CONV_KERNEL, aw)

    silu_c = _silu_rows(c_all, "silu_c")
    mod_part = _plain_mm([(silu_c, w_ada[0])], F32, False, mod_cols, "mod_mm")
    mod_all = _ag_small(mod_part, "ag_mod").reshape(N_DEV, N_DEV, mod_cols)
    mod = lax.dynamic_index_in_dim(mod_all, me, axis=1, keepdims=False).reshape(1, n_mod * d) + b_ada
    sh1, sc1, g1, sh2, sc2, g2, sh3, sc3, g3 = [mod[:, i * d:(i + 1) * d] for i in range(n_mod)]

    def split(g):
        return g.reshape(N_CHIP, 2, g.shape[0] // N_DEV, g.shape[1])

    def partials(g4s, lands, tag):
        return [_chip_partial(a, b, "chip_partial_%s%d" % (tag, t)) for t, (a, b) in enumerate(zip(g4s, lands))]

    gather_late = _GatherWeights([shard(ffn2_w_gate, True), shard(ffn2_w_up, True), shard(ffn2_w_down, False),
                                  shard(w_out, False)])

    n1 = _norm_mod_fwd(x2, ffn1_norm_g, sc1, sh1, "norm1")
    (a1, b1, hid1), (wd1,) = _ffn_up(n1, wg1, wu1, "ffn1_up", comm=_GatherWeights([shard(ffn1_w_down, False)]))
    (h1, f1, n2), (win_t,) = _residual_mm(hid1, wd1, x2, g1, 0.5, "ffn1_down", norm=(mix_norm_g, sc2, sh2),
                                          comm=_GatherWeights([shard(w_in, True)]))
    cos, sin_signed = _rope_tables(s, aw)
    proj = _proj_rope(n2, win_t, cos, sin_signed, aw, "proj")
    lanes_per = aw // LANES
    (attn, lse), (wg2, wu2, wd2, wout) = _attn_seq_fwd(proj, aw, "attn_fwd", comm=gather_late)
    u1 = _conv_fwd(proj, 3 * lanes_per, 4 * lanes_per, conv_w, conv_dw_b, "conv_fwd")
    y = _mix_post_fwd(attn, u1, attn_out_g, conv_ln_g, conv_ln_b, conv_out_g, "mix_post")
    h2, mix, n3 = _residual_mm(y, wout, h1, g2, 1.0, "mix_out", norm=(ffn2_norm_g, sc3, sh3))
    a3, b3, hid3 = _ffn_up(n3, wg2, wu2, "ffn2_up")
    h3, f3 = _residual_mm(hid3, wd2, h2, g3, 0.5, "ffn2_down")

    dh3, df3, err2, d_final_g, dg3 = _final_loss(h3, target, final_norm_g.reshape(1, d), f3, g3, 0.5, "final_loss")
    loss = lax.psum(0.5 * jnp.sum(err2) / d, ("x", "y", "c"))

    da3, db3 = _ffn_bwd_hidden(df3, wd2, a3, b3, "ffn2_hidden_bwd")
    g4_a = [split(_mm_tn(da3, n3, "ffn2_dwg")), split(_mm_tn(db3, n3, "ffn2_dwu")), split(_mm_tn(hid3, df3, "ffn2_dwd"))]
    dn3, land_a = _plain_mm([(da3, wg2), (db3, wu2)], F32, False, d, "ffn2_dn", tm=256, comm=_SiblingExchange(g4_a))
    parts_a = partials(g4_a, land_a, "a")
    dh2, dmix, dsh3, dsc3, dgn3, dg2 = _norm_mod_bwd(dn3, h2, dh3, ffn2_norm_g, sc3, "norm3_bwd",
                                                     branch=(mix, g2, 1.0))
    dy = _plain_mm([(dmix, wout)], F32, True, d, "mix_dy")
    g_wout = _mm_tn(y, dmix, "mix_dwout")
    dattn, du1, d_attn_g, d_conv_g, d_ln_g, d_ln_b = _mix_post_bwd(
        dy, attn, u1, attn_out_g, conv_ln_g, conv_ln_b, conv_out_g, "mix_post_bwd")
    dga, dgb, d_taps, d_conv_b = _conv_bwd(proj, 3 * lanes_per, 4 * lanes_per, conv_w, du1, "conv_bwd")
    (dq, dk, dv), sums_a = _attn_seq_bwd(proj, dattn, attn, lse, "attn_bwd", comm=_ChipExchange(parts_a))
    dproj = _dproj_assemble(dq, dk, dv, dga, dgb, cos, sin_signed, "dproj")
    dn2 = _plain_mm([(dproj, win_t)], F32, False, d, "mix_dn")
    g4_b = [split(g_wout), split(_mm_tn(dproj, n2, "mix_dwin"))]
    (dh1, df1, dsh2, dsc2, dgn2, dg1), land_b = _norm_mod_bwd(dn2, h1, dh2, mix_norm_g, sc2, "norm2_bwd",
                                                              branch=(f1, g1, 0.5), comm=_SiblingExchange(g4_b))
    parts_b = partials(g4_b, land_b, "b")
    g4_c = [split(_mm_tn(hid1, df1, "ffn1_dwd"))]
    (da1, db1), both = _ffn_bwd_hidden(df1, wd1, a1, b1, "ffn1_hidden_bwd",
                                       comm=_Together(_ChipExchange(parts_b), _SiblingExchange(g4_c)))
    sums_b, land_c = both[:2], both[2:]
    parts_c = partials(g4_c, land_c, "c")
    g_wu1, sums_c = _mm_tn(db1, n1, "ffn1_dwu", comm=_ChipExchange(parts_c))
    g4_d = [split(g_wu1)]
    g_wg1, land_d = _mm_tn(da1, n1, "ffn1_dwg", comm=_SiblingExchange(g4_d))
    parts_d = partials(g4_d, land_d, "d")
    g4_e = [split(g_wg1)]
    dn1, both = _plain_mm([(da1, wg1), (db1, wu1)], F32, False, d, "ffn1_dn", tm=256,
                          comm=_Together(_ChipExchange(parts_d), _SiblingExchange(g4_e)))
    sums_d, land_e = both[:1], both[1:]
    parts_e = partials(g4_e, land_e, "e")
    (dx, dsh1, dsc1, dgn1), sums_e = _norm_mod_bwd(dn1, x2, dh1, ffn1_norm_g, sc1, "norm1_bwd",
                                                   comm=_ChipExchange(parts_e))

    dmod = jnp.concatenate([dsh1, dsc1, dg1, dsh2, dsc2, dg2, dsh3, dsc3, dg3], axis=1)
    small = [dmod, dgn1, dgn2, dgn3, d_final_g, d_conv_b, d_ln_g, d_ln_b, d_attn_g, d_conv_g,
             d_taps.reshape(1, CONV_KERNEL * aw)]
    sizes = [v.shape[1] for v in small]
    total = sum(sizes)
    padded = -(-total // (8 * LANES)) * (8 * LANES)
    packed = jnp.concatenate(small + [jnp.zeros((1, padded - total), F32)], axis=1).reshape(8, padded // 8)
    gathered = _ag_small(packed, "ag_small_grads")
    summed = _sum_blocks(gathered, N_DEV, "sum_small_grads").reshape(1, padded)
    offs = [sum(sizes[:i]) for i in range(len(sizes))]
    (g_b_ada, g_gn1, g_gn2, g_gn3, g_final, g_conv_b, g_ln_g, g_ln_b, g_attn_g, g_conv_g, g_taps) = [
        summed[:, o:o + n] for o, n in zip(offs, sizes)]
    g_taps_shard = lax.dynamic_slice_in_dim(g_taps.reshape(CONV_KERNEL, aw), me * cw_shard, cw_shard, axis=1)
    dmod_all = gathered.reshape(N_DEV, padded)[:, :n_mod * d]
    dmod_cols = lax.dynamic_slice_in_dim(dmod_all, me * mod_cols, mod_cols, axis=1)
    g_w_ada = _mm_tn(silu_c, dmod_cols, "ada_dw")

    arrived = dict(zip(["ffn2_w_gate", "ffn2_w_up", "ffn2_w_down", "w_out", "w_in", "ffn1_w_down", "ffn1_w_up",
                        "ffn1_w_gate"], list(sums_a) + list(sums_b) + list(sums_c) + list(sums_d) + list(sums_e)))
    transposed = ("ffn1_w_gate", "ffn1_w_up", "w_in", "ffn2_w_gate", "ffn2_w_up")
    grads = {
        "w_ada": g_w_ada, "b_ada": g_b_ada, "ffn1_norm_g": g_gn1, "mix_norm_g": g_gn2, "conv_dw_w": g_taps_shard,
        "conv_dw_b": g_conv_b, "conv_ln_g": g_ln_g, "conv_ln_b": g_ln_b, "attn_out_g": g_attn_g,
        "conv_out_g": g_conv_g, "ffn2_norm_g": g_gn3, "final_norm_g": g_final,
    }
    weights = dict(w_ada=w_ada, b_ada=b_ada, ffn1_norm_g=ffn1_norm_g, ffn1_w_gate=ffn1_w_gate, ffn1_w_up=ffn1_w_up, ffn1_w_down=ffn1_w_down, mix_norm_g=mix_norm_g, w_in=w_in, conv_dw_w=conv_dw_w, conv_dw_b=conv_dw_b, conv_ln_g=conv_ln_g, conv_ln_b=conv_ln_b, attn_out_g=attn_out_g, conv_out_g=conv_out_g, w_out=w_out, ffn2_norm_g=ffn2_norm_g, ffn2_w_gate=ffn2_w_gate, ffn2_w_up=ffn2_w_up, ffn2_w_down=ffn2_w_down, final_norm_g=final_norm_g)
    moms = dict(w_ada=m_w_ada, b_ada=m_b_ada, ffn1_norm_g=m_ffn1_norm_g, ffn1_w_gate=m_ffn1_w_gate, ffn1_w_up=m_ffn1_w_up, ffn1_w_down=m_ffn1_w_down, mix_norm_g=m_mix_norm_g, w_in=m_w_in, conv_dw_w=m_conv_dw_w, conv_dw_b=m_conv_dw_b, conv_ln_g=m_conv_ln_g, conv_ln_b=m_conv_ln_b, attn_out_g=m_attn_out_g, conv_out_g=m_conv_out_g, w_out=m_w_out, ffn2_norm_g=m_ffn2_norm_g, ffn2_w_gate=m_ffn2_w_gate, ffn2_w_up=m_ffn2_w_up, ffn2_w_down=m_ffn2_w_down, final_norm_g=m_final_norm_g)
    vars_ = dict(w_ada=v_w_ada, b_ada=v_b_ada, ffn1_norm_g=v_ffn1_norm_g, ffn1_w_gate=v_ffn1_w_gate, ffn1_w_up=v_ffn1_w_up, ffn1_w_down=v_ffn1_w_down, mix_norm_g=v_mix_norm_g, w_in=v_w_in, conv_dw_w=v_conv_dw_w, conv_dw_b=v_conv_dw_b, conv_ln_g=v_conv_ln_g, conv_ln_b=v_conv_ln_b, attn_out_g=v_attn_out_g, conv_out_g=v_conv_out_g, w_out=v_w_out, ffn2_norm_g=v_ffn2_norm_g, ffn2_w_gate=v_ffn2_w_gate, ffn2_w_up=v_ffn2_w_up, ffn2_w_down=v_ffn2_w_down, final_norm_g=v_final_norm_g)
    names = list(weights)
    big = ["w_ada", "ffn1_w_gate", "ffn1_w_up", "ffn1_w_down", "w_in", "w_out", "ffn2_w_gate", "ffn2_w_up",
           "ffn2_w_down"]
    shape2 = {n: (weights[n].shape[-2] if weights[n].ndim > 1 else 1, weights[n].shape[-1]) for n in names}
    shape2["conv_dw_w"] = (CONV_KERNEL, cw_shard)
    g_out, d_out, m_out, v_out = {}, {}, {}, {}
    for n in big:
        if n in arrived:
            def view(t, n=n):
                return t[0].T if n in transposed else t[0]
            res = _adamw_reduced(view(weights[n]), arrived[n], view(moms[n]), view(vars_[n]), "adamw_" + n)
            g_out[n], d_out[n], m_out[n], v_out[n] = [r.T if n in transposed else r for r in res]
        else:
            g2d = grads[n].reshape(shape2[n])
            res = _adamw_big(weights[n].reshape(shape2[n]), g2d, moms[n].reshape(shape2[n]),
                             vars_[n].reshape(shape2[n]), "adamw_" + n)
            g_out[n], (d_out[n], m_out[n], v_out[n]) = g2d, res
    rest = [n for n in names if n not in big]
    res = _adamw_small([weights[n].reshape(shape2[n]) for n in rest], [grads[n].reshape(shape2[n]) for n in rest],
                       [moms[n].reshape(shape2[n]) for n in rest], [vars_[n].reshape(shape2[n]) for n in rest],
                       "adamw_small")
    for i, n in enumerate(rest):
        g_out[n], d_out[n], m_out[n], v_out[n] = grads[n], res[0][i], res[1][i], res[2][i]

    def shaped(table):
        return [table[n].reshape(weights[n].shape) for n in names]

    return (loss, dx.reshape(x.shape), *shaped(g_out), *shaped(d_out), *shaped(m_out), *shaped(v_out))
```

```python
import functools

import jax
import jax.numpy as jnp
from jax import lax
from jax.experimental import pallas as pl
from jax.experimental.pallas import tpu as pltpu

F32 = jnp.float32
BF16 = jnp.bfloat16
MESH = pl.DeviceIdType.MESH
ANY = pl.BlockSpec(memory_space=pl.ANY)

N_DEV = 8
N_CHIP = 4
HEAD_DIM = 64
HALF_HEAD = HEAD_DIM // 2
LANES = 128
BLOCK = 128
DILATIONS = (1, 4, 16)
MERGE_CHUNK = 512
ROPE_THETA = 10000.0
CONV_KERNEL = 31
CONV_HALO = 32
CONV_CHUNK = 512
CONV_SUB = 128
RMS_EPS = 1e-6
LN_EPS = 1e-5
ADAM_LR = 0.001
ADAM_B1 = 0.9
ADAM_B2 = 0.999
ADAM_EPS = 1e-08
ADAM_WD = 0.01
ADAM_STEP = 10
VMEM_LIMIT = 56 * 1024 * 1024
NEG = -1e30


def _params(n_axes):
    return pltpu.CompilerParams(dimension_semantics=("arbitrary",) * n_axes, vmem_limit_bytes=VMEM_LIMIT)


def _tile(n, target, unit):
    best = None
    for t in range(unit, min(n, target) + 1, unit):
        if n % t == 0:
            best = t
    return best if best is not None else n


def _sigmoid(x):
    return 0.5 * (jnp.tanh(0.5 * x) + 1.0)


def _call(body, *, grid, in_specs, out_specs, out_shape, args, name, scratch_shapes=(), comm=None):
    params = _params(len(grid))
    if comm is None:
        return pl.pallas_call(body, grid=grid, in_specs=list(in_specs), out_specs=list(out_specs),
                              out_shape=list(out_shape), scratch_shapes=list(scratch_shapes),
                              compiler_params=params, name=name)(*args)
    n_in, n_out, n_scr = len(args), len(out_shape), len(scratch_shapes)
    c_in, c_out = len(comm.inputs), len(comm.out_shapes)
    steps = 1
    for g in grid:
        steps *= g

    def hosted(*refs):
        pos = 0
        parts = []
        for size in (n_in, c_in, n_out, c_out, n_scr, len(comm.scratch)):
            parts.append(refs[pos:pos + size])
            pos += size
        ins, cin, outs, cout, scr, cscr = parts
        step = 0
        for axis, g in enumerate(grid):
            step = step * g + pl.program_id(axis)

        @pl.when(step == 0)
        def _():
            comm.start(cin, cout, cscr)

        body(*ins, *outs, *scr)
        if comm.mid is not None and steps >= 4:
            @pl.when(step == (3 * steps) // 4)
            def _():
                comm.mid(cin, cout, cscr)

        @pl.when(step == steps - 1)
        def _():
            if comm.mid is not None and steps < 4:
                comm.mid(cin, cout, cscr)
            comm.finish(cin, cout, cscr)

    res = pl.pallas_call(
        hosted, grid=grid, in_specs=list(in_specs) + [ANY] * c_in, out_specs=list(out_specs) + [ANY] * c_out,
        out_shape=list(out_shape) + list(comm.out_shapes), scratch_shapes=list(scratch_shapes) + list(comm.scratch),
        compiler_params=params, name=name)(*args, *comm.inputs)
    return res[:n_out], res[n_out:]


def _rows(fn, rows_in, vecs_in, rows_out, vecs_out, *, tile, name, comm=None):
    norm = [r if isinstance(r, tuple) else (r, r.shape[1], 0) for r in rows_in]
    n_rows = norm[0][0].shape[0]
    n_tiles = n_rows // tile
    in_specs, args = [], []
    for arr, width, cb in norm:
        in_specs.append(pl.BlockSpec((tile, width), functools.partial(lambda i, cb: (i, cb), cb=cb)))
        args.append(arr)
    for v in vecs_in:
        in_specs.append(pl.BlockSpec((1, v.shape[1]), lambda i: (0, 0)))
        args.append(v)
    out_shape = [jax.ShapeDtypeStruct((n_rows, w), dt) for w, dt in rows_out]
    out_shape += [jax.ShapeDtypeStruct((1, w), F32) for w in vecs_out]
    out_specs = [pl.BlockSpec((tile, w), lambda i: (i, 0)) for w, _ in rows_out]
    out_specs += [pl.BlockSpec((1, w), lambda i: (0, 0)) for w in vecs_out]
    n_in, n_ro = len(args), len(rows_out)

    def body(*refs):
        vals = [r[...] for r in refs[:n_in]]
        outs = refs[n_in:]
        row_vals, vec_vals = fn(*vals)
        for ref, val in zip(outs[:n_ro], row_vals):
            if isinstance(val, tuple):
                w = val[0].shape[1]
                for j, piece in enumerate(val):
                    ref[:, j * w:(j + 1) * w] = piece.astype(ref.dtype)
            else:
                ref[...] = val.astype(ref.dtype)
        if vecs_out:
            @pl.when(pl.program_id(0) == 0)
            def _():
                for ref in outs[n_ro:]:
                    ref[...] = jnp.zeros_like(ref)
            for ref, val in zip(outs[n_ro:], vec_vals):
                ref[...] += val

    return _call(body, grid=(n_tiles,), in_specs=in_specs, out_specs=out_specs, out_shape=out_shape, args=args,
                 name=name, comm=comm)


def _colsum(x):
    return jnp.sum(x, axis=0, keepdims=True)


def _rms_stats(h):
    r = lax.rsqrt(jnp.mean(h * h, axis=-1, keepdims=True) + RMS_EPS)
    return r, h * r


def _rms_back(r, xn, dxn):
    return r * (dxn - xn * jnp.mean(dxn * xn, axis=-1, keepdims=True))


def _norm_mod_fwd(h, gain, scale, shift, name):
    def fn(h, gain, scale, shift):
        _, xn = _rms_stats(h)
        return [(xn * gain) * (1.0 + scale) + shift], []
    return _rows(fn, [h], [gain, scale, shift], [(h.shape[1], BF16)], [], tile=512, name=name)[0]


def _branch_back(dh, f, gate, coef):
    return (coef * gate) * dh, coef * _colsum(f.astype(F32) * dh)


def _norm_mod_bwd(dn, h, dh_in, gain, scale, name, branch=None, comm=None):
    d = h.shape[1]

    def back(dn, h, dh_in, gain, scale):
        r, xn = _rms_stats(h)
        y = xn * gain
        dy = dn * (1.0 + scale)
        dh = dh_in + _rms_back(r, xn, dy * gain)
        return dh, [_colsum(dn), _colsum(dn * y), _colsum(dy * xn)]

    if branch is None:
        def fn(dn, h, dh_in, gain, scale):
            dh, vecs = back(dn, h, dh_in, gain, scale)
            return [dh], vecs
        return _rows(fn, [dn, h, dh_in], [gain, scale], [(d, F32)], [d, d, d], tile=256, name=name, comm=comm)
    f, gate, coef = branch

    def fn_branch(dn, h, dh_in, f, gain, scale, gate):
        dh, vecs = back(dn, h, dh_in, gain, scale)
        df, dgate = _branch_back(dh, f, gate, coef)
        return [dh, df], vecs + [dgate]
    return _rows(fn_branch, [dn, h, dh_in, f], [gain, scale, gate], [(d, F32), (d, BF16)], [d, d, d, d], tile=256,
                 name=name, comm=comm)


def _last_mm_loss(lhs, w, res, gate, coef, target, gain, name):
    d = w.shape[1]

    def epi(accs, ex, vc):
        f = accs[0]
        h = ex[0] + (coef * vc[0]) * f
        r, xn = _rms_stats(h)
        err = xn * vc[1] - ex[1]
        dout = err * (1.0 / d)
        dh = _rms_back(r, xn, dout * vc[1])
        df, dgate = _branch_back(dh, f, vc[0], coef)
        return [dh, df, _colsum(err * err), _colsum(dout * xn), dgate]
    return _mm([[(lhs, w)]], epi, [res, target], [gate, gain], [F32, BF16], trans_rhs=False, tm=256, tn=d,
               name=name, n_sums=3)


def _partner(x):
    width = x.shape[1]
    lane = lax.broadcasted_iota(jnp.int32, x.shape, 1) % HEAD_DIM
    return jnp.where(lane < HALF_HEAD, pltpu.roll(x, width - HALF_HEAD, 1), pltpu.roll(x, HALF_HEAD, 1))


def _proj_rope(n, w_t, cos, sin_signed, width, name):
    s, kdim = n.shape
    n_cols = w_t.shape[0]
    tm = _tile(s, 1024, 8)
    qscale = HEAD_DIM ** -0.5

    def body(n_ref, w_ref, cos_ref, sin_ref, o_ref):
        j = pl.program_id(0)
        acc = lax.dot_general(n_ref[...].astype(BF16), w_ref[...].astype(BF16), (((1,), (1,)), ((), ())),
                              preferred_element_type=F32)

        @pl.when(j >= 2)
        def _():
            o_ref[...] = acc

        @pl.when(j < 2)
        def _():
            rot = acc * cos_ref[...] + _partner(acc) * sin_ref[...]
            o_ref[...] = jnp.where(j == 0, qscale, 1.0) * rot

    table = pl.BlockSpec((tm, width), lambda j, i: (jnp.where(j < 2, i, 0), 0))
    return pl.pallas_call(
        body, grid=(n_cols // width, s // tm),
        in_specs=[pl.BlockSpec((tm, kdim), lambda j, i: (i, 0)), pl.BlockSpec((width, kdim), lambda j, i: (j, 0)),
                  table, table],
        out_specs=pl.BlockSpec((tm, width), lambda j, i: (i, j)), out_shape=jax.ShapeDtypeStruct((s, n_cols), F32),
        compiler_params=_params(2), name=name)(n, w_t, cos, sin_signed)


def _dproj_assemble(dq, dk, dv, dga, dgb, cos, sin_signed, name):
    width = dq.shape[1]
    qscale = HEAD_DIM ** -0.5

    def fn(dq, dk, dv, dga, dgb, cos, sin):
        dq0 = (dq * cos - _partner(dq) * sin) * qscale
        dk0 = dk * cos - _partner(dk) * sin
        return [(dq0, dk0, dv, dga, dgb)], []
    return _rows(fn, [dq, dk, dv, dga, dgb, cos, sin_signed], [], [(5 * width, BF16)], [], tile=256, name=name)[0]


def _mix_post_fwd(attn, u1, attn_g, ln_g, ln_b, conv_g, name):
    def fn(attn, u1, attn_g, ln_g, ln_b, conv_g):
        _, xa = _rms_stats(attn)
        mu = jnp.mean(u1, axis=-1, keepdims=True)
        xc = u1 - mu
        rstd = lax.rsqrt(jnp.mean(xc * xc, axis=-1, keepdims=True) + LN_EPS)
        u2 = (xc * rstd) * ln_g + ln_b
        u3 = u2 * _sigmoid(u2)
        _, x3 = _rms_stats(u3)
        return [(xa * attn_g, x3 * conv_g)], []
    w = attn.shape[1]
    return _rows(fn, [attn, u1], [attn_g, ln_g, ln_b, conv_g], [(2 * w, BF16)], [], tile=512, name=name)[0]


def _mix_post_bwd(dy, attn, u1, attn_g, ln_g, ln_b, conv_g, name):
    w = attn.shape[1]

    def fn(dya, dyc, attn, u1, attn_g, ln_g, ln_b, conv_g):
        ra, xa = _rms_stats(attn)
        dattn = _rms_back(ra, xa, dya * attn_g)
        mu = jnp.mean(u1, axis=-1, keepdims=True)
        xc = u1 - mu
        rstd = lax.rsqrt(jnp.mean(xc * xc, axis=-1, keepdims=True) + LN_EPS)
        xh = xc * rstd
        u2 = xh * ln_g + ln_b
        sig = _sigmoid(u2)
        u3 = u2 * sig
        r3, x3 = _rms_stats(u3)
        du3 = _rms_back(r3, x3, dyc * conv_g)
        du2 = du3 * (sig + u3 * (1.0 - sig))
        dxh = du2 * ln_g
        du1 = rstd * (dxh - jnp.mean(dxh, axis=-1, keepdims=True) - xh * jnp.mean(dxh * xh, axis=-1, keepdims=True))
        return [dattn, du1], [_colsum(dya * xa), _colsum(dyc * x3), _colsum(du2 * xh), _colsum(du2)]
    return _rows(fn, [(dy, w, 0), (dy, w, 1), attn, u1], [attn_g, ln_g, ln_b, conv_g], [(w, F32), (w, F32)],
                 [w, w, w, w], tile=256, name=name)


def _silu_rows(c_all, name):
    def fn(c):
        return [c * _sigmoid(c)], []
    return _rows(fn, [c_all], [], [(c_all.shape[1], BF16)], [], tile=c_all.shape[0], name=name)[0]


def _mm(groups, epi, extras, vecs, outs, *, trans_rhs, tm, tn, name, n_sums=0, comm=None):
    m = groups[0][0][0].shape[0]
    n = groups[0][0][1].shape[0] if trans_rhs else groups[0][0][1].shape[1]
    tm, tn = min(tm, m), min(tn, n)
    in_specs, args = [], []
    for grp in groups:
        for lhs, rhs in grp:
            k = lhs.shape[1]
            in_specs.append(pl.BlockSpec((tm, k), lambda j, i: (i, 0)))
            in_specs.append(pl.BlockSpec((tn, k), lambda j, i: (j, 0)) if trans_rhs
                            else pl.BlockSpec((k, tn), lambda j, i: (0, j)))
            args += [lhs, rhs]
    for e in extras:
        in_specs.append(pl.BlockSpec((tm, tn), lambda j, i: (i, j)))
        args.append(e)
    for v in vecs:
        in_specs.append(pl.BlockSpec((1, tn), lambda j, i: (0, j)))
        args.append(v)
    sizes = [len(g) for g in groups]
    n_mm, n_ex, n_vec = 2 * sum(sizes), len(extras), len(vecs)
    dims = (((1,), (1,)), ((), ())) if trans_rhs else (((1,), (0,)), ((), ()))

    def body(*refs):
        accs, pos = [], 0
        for size in sizes:
            acc = None
            for _ in range(size):
                part = lax.dot_general(refs[pos][...].astype(BF16), refs[pos + 1][...].astype(BF16), dims,
                                       preferred_element_type=F32)
                acc = part if acc is None else acc + part
                pos += 2
            accs.append(acc)
        ex = [r[...] for r in refs[n_mm:n_mm + n_ex]]
        vc = [r[...] for r in refs[n_mm + n_ex:n_mm + n_ex + n_vec]]
        out_refs = refs[n_mm + n_ex + n_vec:]
        vals = epi(accs, ex, vc)
        for ref, val in zip(out_refs[:len(outs)], vals):
            ref[...] = val.astype(ref.dtype)
        if n_sums:
            @pl.when(pl.program_id(1) == 0)
            def _():
                for ref in out_refs[len(outs):]:
                    ref[...] = jnp.zeros_like(ref)
            for ref, val in zip(out_refs[len(outs):], vals[len(outs):]):
                ref[...] += val

    return _call(body, grid=(n // tn, m // tm), in_specs=in_specs,
                 out_specs=[pl.BlockSpec((tm, tn), lambda j, i: (i, j)) for _ in outs]
                 + [pl.BlockSpec((1, tn), lambda j, i: (0, j))] * n_sums,
                 out_shape=[jax.ShapeDtypeStruct((m, n), dt) for dt in outs]
                 + [jax.ShapeDtypeStruct((1, n), F32)] * n_sums, args=args, name=name, comm=comm)


def _mm_tn(lhs, rhs, name, comm=None):
    t, a = lhs.shape
    b = rhs.shape[1]
    ta = a if a <= 1536 else _tile(a, 1536, LANES)
    tk = _tile(t, 512, 8)

    def body(l_ref, r_ref, o_ref):
        @pl.when(pl.program_id(1) == 0)
        def _():
            o_ref[...] = jnp.zeros_like(o_ref)
        o_ref[...] += lax.dot_general(l_ref[...].astype(BF16), r_ref[...].astype(BF16), (((0,), (0,)), ((), ())),
                                      preferred_element_type=F32)

    res = _call(body, grid=(a // ta, t // tk),
                in_specs=[pl.BlockSpec((tk, ta), lambda i, k: (k, i)), pl.BlockSpec((tk, b), lambda i, k: (k, 0))],
                out_specs=[pl.BlockSpec((ta, b), lambda i, k: (i, 0))], out_shape=[jax.ShapeDtypeStruct((a, b), F32)],
                args=(lhs, rhs), name=name, comm=comm)
    return res[0] if comm is None else (res[0][0], res[1])


def _ffn_tn(f):
    return _tile(f, 1536, LANES)


def _ffn_up(n, wg_t, wu_t, name, comm=None):
    def epi(accs, ex, vc):
        a, b = accs
        return [a, b, (a * _sigmoid(a)) * b]
    return _mm([[(n, wg_t)], [(n, wu_t)]], epi, [], [], [BF16, BF16, BF16], trans_rhs=True, tm=512,
               tn=_ffn_tn(wg_t.shape[0]), name=name, comm=comm)


def _residual_mm(lhs, w, res, gate, coef, name, norm=None, comm=None):
    def epi(accs, ex, vc):
        h = ex[0] + (coef * vc[0]) * accs[0]
        if norm is None:
            return [h, accs[0]]
        _, xn = _rms_stats(h)
        return [h, accs[0], (xn * vc[1]) * (1.0 + vc[2]) + vc[3]]
    vecs = [gate] + (list(norm) if norm is not None else [])
    outs = [F32, BF16] + ([BF16] if norm is not None else [])
    return _mm([[(lhs, w)]], epi, [res], vecs, outs, trans_rhs=False, tm=512, tn=w.shape[1], name=name, comm=comm)


def _ffn_bwd_hidden(df, wd, a, b, name, comm=None):
    def epi(accs, ex, vc):
        dh = accs[0]
        av, bv = ex[0].astype(F32), ex[1].astype(F32)
        sig = _sigmoid(av)
        silu = av * sig
        return [dh * bv * (sig + silu * (1.0 - sig)), dh * silu]
    return _mm([[(df, wd)]], epi, [a, b], [], [BF16, BF16], trans_rhs=True, tm=512, tn=_ffn_tn(wd.shape[0]),
               name=name, comm=comm)


def _plain_mm(pairs, out_dtype, trans_rhs, tn, name, tm=512, comm=None):
    def epi(accs, ex, vc):
        return [accs[0]]
    res = _mm([pairs], epi, [], [], [out_dtype], trans_rhs=trans_rhs, tm=tm, tn=tn, name=name, comm=comm)
    return res[0] if comm is None else (res[0][0], res[1])


HEADS_PER_TILE = LANES // HEAD_DIM


def _stack_heads(x):
    lane = lax.broadcasted_iota(jnp.int32, (1, LANES), 1)
    return jnp.concatenate([x * (lane // HEAD_DIM == h).astype(F32) for h in range(HEADS_PER_TILE)], axis=0)


def _unstack_heads(y):
    r = y.shape[0] // HEADS_PER_TILE
    lane = lax.broadcasted_iota(jnp.int32, (r, y.shape[1]), 1)
    out = y[0:r]
    for h in range(1, HEADS_PER_TILE):
        out = jnp.where(lane // HEAD_DIM == h, y[h * r:(h + 1) * r], out)
    return out


def _stacked_lse(lb):
    return jnp.concatenate([_lane_pick(lb, h) for h in range(HEADS_PER_TILE)], axis=0)


def _band_masks(n_row_blocks, n_col_blocks):
    shape = (n_row_blocks * BLOCK, n_col_blocks * BLOCK)
    qi = lax.broadcasted_iota(jnp.int32, shape, 0) % BLOCK
    kj = lax.broadcasted_iota(jnp.int32, shape, 1) % BLOCK
    return kj <= qi, kj >= qi


def _query_masks():
    same_ok, before_ok = _band_masks(HEADS_PER_TILE, 2)
    is_cur = lax.broadcasted_iota(jnp.int32, same_ok.shape, 1) >= BLOCK
    return jnp.logical_and(is_cur, same_ok), jnp.logical_and(jnp.logical_not(is_cur), before_ok)


def _dot_nt(a, b):
    return lax.dot_general(a.astype(BF16), b.astype(BF16), (((1,), (1,)), ((), ())), preferred_element_type=F32)


def _dot_nn(a, b):
    return lax.dot_general(a.astype(BF16), b.astype(BF16), (((1,), (0,)), ((), ())), preferred_element_type=F32)


def _dot_tn(a, b):
    return lax.dot_general(a.astype(BF16), b.astype(BF16), (((0,), (0,)), ((), ())), preferred_element_type=F32)


def _lane_pick(x, h):
    lane = lax.broadcasted_iota(jnp.int32, x.shape, 1)
    return jnp.sum(jnp.where(lane == h * HEAD_DIM, x, 0.0), axis=1, keepdims=True)


def _block_rows(idx, d):
    span = BLOCK * d
    g = idx // d
    q0 = g * span + idx % d
    has_prev = g > 0
    p0 = jnp.where(has_prev, q0 - span, q0)
    return pl.ds(q0, BLOCK, stride=d), pl.ds(p0, BLOCK, stride=d), has_prev


def _qkv_specs(s, tiles):
    q, k, v = [pl.BlockSpec((s, LANES), functools.partial(lambda hb, off: (0, off + hb), off=i * tiles))
               for i in range(3)]
    return q, k, v, pl.BlockSpec((s, LANES), lambda hb: (0, hb))


def _attn_seq_fwd(proj, width, name, comm=None):
    s = proj.shape[0]
    q_spec, k_spec, v_spec, cur = _qkv_specs(s, width // LANES)

    def body(q_ref, k_ref, v_ref, o_ref, l_ref, o_s, l_s):
        cur_valid, prev_valid = _query_masks()
        for bi, d in enumerate(DILATIONS):
            def blk(idx, carry, bi=bi, d=d):
                rows, prev, has_prev = _block_rows(idx, d)
                q2 = _stack_heads(q_ref[rows, :])
                keys = jnp.concatenate([k_ref[prev, :], k_ref[rows, :]], axis=0)
                vals = jnp.concatenate([v_ref[prev, :], v_ref[rows, :]], axis=0)
                valid = jnp.logical_or(cur_valid, jnp.logical_and(prev_valid, has_prev))
                sc = jnp.where(valid, _dot_nt(q2, keys), NEG)
                mx = jnp.max(sc, axis=1, keepdims=True)
                p = jnp.exp(sc - mx)
                den = jnp.sum(p, axis=1, keepdims=True)
                o_s[bi, rows, :] = _unstack_heads(_dot_nn(p, vals) / den)
                l_s[bi, rows, :] = _unstack_heads(jnp.broadcast_to(mx + jnp.log(den), (q2.shape[0], LANES)))
                return carry

            lax.fori_loop(0, s // BLOCK, blk, 0, unroll=8)
        for c in range(s // MERGE_CHUNK):
            rows = slice(c * MERGE_CHUNK, (c + 1) * MERGE_CHUNK)
            ls = [l_s[bi, rows, :] for bi in range(len(DILATIONS))]
            top = functools.reduce(jnp.maximum, ls)
            ws = [jnp.exp(l - top) for l in ls]
            den = functools.reduce(lambda a, b: a + b, ws)
            num = functools.reduce(lambda a, b: a + b, [w * o_s[bi, rows, :] for bi, w in enumerate(ws)])
            o_ref[rows, :] = num / den
            l_ref[rows, :] = top + jnp.log(den)

    return _call(
        body, grid=(width // LANES,), in_specs=[q_spec, k_spec, v_spec], out_specs=[cur, cur],
        out_shape=[jax.ShapeDtypeStruct((s, width), F32)] * 2,
        scratch_shapes=[pltpu.VMEM((len(DILATIONS), s, LANES), F32)] * 2,
        args=(proj, proj, proj), name=name, comm=comm)


def _attn_seq_bwd(proj, do, o, lse, name, comm=None):
    s, width = do.shape
    q_spec, k_spec, v_spec, cur = _qkv_specs(s, width // LANES)

    def body(q_ref, k_ref, v_ref, do_ref, o_ref, l_ref, dq_ref, dk_ref, dv_ref):
        dq_ref[...] = jnp.zeros_like(dq_ref)
        dk_ref[...] = jnp.zeros_like(dk_ref)
        dv_ref[...] = jnp.zeros_like(dv_ref)
        cur_valid, prev_valid = _query_masks()
        for d in DILATIONS:
            def blk(idx, carry, d=d):
                rows, prev, has_prev = _block_rows(idx, d)
                dob = do_ref[rows, :]
                q2 = _stack_heads(q_ref[rows, :])
                do2 = _stack_heads(dob)
                delta = jnp.sum(_stack_heads(dob * o_ref[rows, :]), axis=1, keepdims=True)
                lse2 = _stacked_lse(l_ref[rows, :])
                keys = jnp.concatenate([k_ref[prev, :], k_ref[rows, :]], axis=0)
                vals = jnp.concatenate([v_ref[prev, :], v_ref[rows, :]], axis=0)
                valid = jnp.logical_or(cur_valid, jnp.logical_and(prev_valid, has_prev))
                p = jnp.where(valid, jnp.exp(_dot_nt(q2, keys) - lse2), 0.0)
                ds = p * (_dot_nt(do2, vals) - delta)
                dq_ref[rows, :] += _unstack_heads(_dot_nn(ds, keys))
                dkk = _dot_tn(ds, q2)
                dvv = _dot_tn(p, do2)
                dk_ref[prev, :] += dkk[0:BLOCK]
                dk_ref[rows, :] += dkk[BLOCK:]
                dv_ref[prev, :] += dvv[0:BLOCK]
                dv_ref[rows, :] += dvv[BLOCK:]
                return carry

            lax.fori_loop(0, s // BLOCK, blk, 0, unroll=4)

    return _call(
        body, grid=(width // LANES,), in_specs=[q_spec, k_spec, v_spec, cur, cur, cur], out_specs=[cur, cur, cur],
        out_shape=[jax.ShapeDtypeStruct((s, width), F32)] * 3,
        args=(proj, proj, proj, do, o, lse), name=name, comm=comm)


def _conv_specs(s, a_block, b_block):
    per = CONV_CHUNK // CONV_HALO
    a_cur = pl.BlockSpec((CONV_CHUNK, LANES), lambda cb, i: (i, a_block + cb))
    b_cur = pl.BlockSpec((CONV_CHUNK, LANES), lambda cb, i: (i, b_block + cb))
    a_halo = pl.BlockSpec((CONV_HALO, LANES), lambda cb, i: (jnp.maximum(i * per - 1, 0), a_block + cb))
    b_halo = pl.BlockSpec((CONV_HALO, LANES), lambda cb, i: (jnp.maximum(i * per - 1, 0), b_block + cb))
    w_spec = pl.BlockSpec((CONV_KERNEL, LANES), lambda cb, i: (0, cb))
    vec = pl.BlockSpec((1, LANES), lambda cb, i: (0, cb))
    out = pl.BlockSpec((CONV_CHUNK, LANES), lambda cb, i: (i, cb))
    return a_cur, b_cur, a_halo, b_halo, w_spec, vec, out


def _fill_glu_window(win, a_ref, b_ref, ah_ref, bh_ref, first):
    halo = ah_ref[...] * _sigmoid(bh_ref[...])
    win[0:CONV_HALO, :] = jnp.where(first, 0.0, halo)
    win[CONV_HALO:, :] = a_ref[...] * _sigmoid(b_ref[...])


def _conv_fwd(proj, a_block, b_block, w, bias, name):
    s = proj.shape[0]
    cw = w.shape[1]
    a_cur, b_cur, a_halo, b_halo, w_spec, vec, out = _conv_specs(s, a_block, b_block)
    lead = CONV_HALO - (CONV_KERNEL - 1)

    def body(a_ref, b_ref, ah_ref, bh_ref, w_ref, bias_ref, o_ref, win):
        _fill_glu_window(win, a_ref, b_ref, ah_ref, bh_ref, pl.program_id(1) == 0)
        for sub in range(CONV_CHUNK // CONV_SUB):
            base = sub * CONV_SUB
            acc = jnp.zeros((CONV_SUB, LANES), F32) + bias_ref[...]
            for j in range(CONV_KERNEL):
                acc = acc + w_ref[j:j + 1, :] * win[base + lead + j:base + lead + j + CONV_SUB, :]
            o_ref[base:base + CONV_SUB, :] = acc

    return pl.pallas_call(
        body, grid=(cw // LANES, s // CONV_CHUNK), in_specs=[a_cur, b_cur, a_halo, b_halo, w_spec, vec],
        out_specs=out, out_shape=jax.ShapeDtypeStruct((s, cw), F32),
        scratch_shapes=[pltpu.VMEM((CONV_CHUNK + CONV_HALO, LANES), F32)],
        compiler_params=_params(2), name=name)(proj, proj, proj, proj, w, bias)


def _conv_bwd(proj, a_block, b_block, w, du1, name):
    s = proj.shape[0]
    cw = w.shape[1]
    a_cur, b_cur, a_halo, b_halo, w_spec, vec, out = _conv_specs(s, a_block, b_block)
    per = CONV_CHUNK // CONV_HALO
    n_chunks = s // CONV_CHUNK
    d_next = pl.BlockSpec((CONV_HALO, LANES), lambda cb, i: (jnp.minimum((i + 1) * per, s // CONV_HALO - 1), cb))
    lead = CONV_HALO - (CONV_KERNEL - 1)

    def body(a_ref, b_ref, ah_ref, bh_ref, w_ref, d_ref, dn_ref, da_ref, db_ref, dw_ref, dbias_ref, win, dwin):
        i = pl.program_id(1)
        _fill_glu_window(win, a_ref, b_ref, ah_ref, bh_ref, i == 0)
        dwin[0:CONV_CHUNK, :] = d_ref[...]
        dwin[CONV_CHUNK:, :] = jnp.where(i == n_chunks - 1, 0.0, dn_ref[...])

        @pl.when(i == 0)
        def _():
            dw_ref[...] = jnp.zeros_like(dw_ref)
            dbias_ref[...] = jnp.zeros_like(dbias_ref)

        dbias_ref[...] += _colsum(d_ref[...])
        for sub in range(CONV_CHUNK // CONV_SUB):
            base = sub * CONV_SUB
            dcur = dwin[base:base + CONV_SUB, :]
            du0 = jnp.zeros((CONV_SUB, LANES), F32)
            for j in range(CONV_KERNEL):
                back = CONV_KERNEL - 1 - j
                du0 = du0 + w_ref[j:j + 1, :] * dwin[base + back:base + back + CONV_SUB, :]
                dw_ref[j:j + 1, :] += _colsum(dcur * win[base + lead + j:base + lead + j + CONV_SUB, :])
            av = a_ref[base:base + CONV_SUB, :]
            sig = _sigmoid(b_ref[base:base + CONV_SUB, :])
            da_ref[base:base + CONV_SUB, :] = du0 * sig
            db_ref[base:base + CONV_SUB, :] = du0 * av * sig * (1.0 - sig)

    return pl.pallas_call(
        body, grid=(cw // LANES, n_chunks), in_specs=[a_cur, b_cur, a_halo, b_halo, w_spec, out, d_next],
        out_specs=[out, out, w_spec, vec],
        out_shape=[jax.ShapeDtypeStruct((s, cw), F32), jax.ShapeDtypeStruct((s, cw), F32),
                   jax.ShapeDtypeStruct((CONV_KERNEL, cw), F32), jax.ShapeDtypeStruct((1, cw), F32)],
        scratch_shapes=[pltpu.VMEM((CONV_CHUNK + CONV_HALO, LANES), F32)] * 2,
        compiler_params=_params(2), name=name)(proj, proj, proj, proj, w, du1, du1)


def _adamw_math(w, g, m, v):
    m = ADAM_B1 * m + (1.0 - ADAM_B1) * g
    v = ADAM_B2 * v + (1.0 - ADAM_B2) * (g * g)
    m_hat = m / (1.0 - ADAM_B1 ** ADAM_STEP)
    v_hat = v / (1.0 - ADAM_B2 ** ADAM_STEP)
    delta = -ADAM_LR * (m_hat / (jnp.sqrt(v_hat) + ADAM_EPS) + ADAM_WD * w)
    return delta, m, v


def _adamw_big(w, g, m, v, name):
    rows, cols = w.shape
    tile = _tile(rows, 256, 8)
    spec = pl.BlockSpec((tile, cols), lambda i: (i, 0))

    def body(w_ref, g_ref, m_ref, v_ref, d_out, m_out, v_out):
        d_out[...], m_out[...], v_out[...] = _adamw_math(w_ref[...], g_ref[...], m_ref[...], v_ref[...])

    return pl.pallas_call(body, grid=(rows // tile,), in_specs=[spec] * 4, out_specs=[spec] * 3,
                          out_shape=[jax.ShapeDtypeStruct(w.shape, F32)] * 3, compiler_params=_params(1),
                          name=name)(w, g, m, v)


def _adamw_reduced(w, land, m, v, name):
    rows, cols = w.shape
    tile = _tile(rows, 256, 16)
    spec = pl.BlockSpec((tile, cols), lambda i: (i, 0))

    def body(w_ref, l_ref, m_ref, v_ref, g_out, d_out, m_out, v_out):
        g = l_ref[0].astype(F32)
        for q in range(1, N_CHIP):
            g = g + l_ref[q].astype(F32)
        g_out[...] = g
        d_out[...], m_out[...], v_out[...] = _adamw_math(w_ref[...], g, m_ref[...], v_ref[...])

    return pl.pallas_call(body, grid=(rows // tile,),
                          in_specs=[spec, pl.BlockSpec((N_CHIP, tile, cols), lambda i: (0, i, 0)), spec, spec],
                          out_specs=[spec] * 4, out_shape=[jax.ShapeDtypeStruct(w.shape, F32)] * 4,
                          compiler_params=_params(1), name=name)(w, land, m, v)


def _adamw_small(ws, gs, ms, vs, name):
    n = len(ws)

    def body(*refs):
        ins, outs = refs[:4 * n], refs[4 * n:]
        for t in range(n):
            res = _adamw_math(ins[t][...], ins[n + t][...], ins[2 * n + t][...], ins[3 * n + t][...])
            for j in range(3):
                outs[j * n + t][...] = res[j]

    shapes = [jax.ShapeDtypeStruct(w.shape, F32) for w in ws]
    res = pl.pallas_call(body, out_shape=shapes * 3, compiler_params=pltpu.CompilerParams(vmem_limit_bytes=VMEM_LIMIT),
                         name=name)(*ws, *gs, *ms, *vs)
    return res[:n], res[n:2 * n], res[2 * n:]


def _sum_blocks(x, n_blocks, name):
    r = x.shape[0] // n_blocks

    def body(x_ref, o_ref):
        acc = x_ref[0:r, :]
        for b in range(1, n_blocks):
            acc = acc + x_ref[b * r:(b + 1) * r, :]
        o_ref[...] = acc

    return pl.pallas_call(body, out_shape=jax.ShapeDtypeStruct((r, x.shape[1]), F32),
                          compiler_params=pltpu.CompilerParams(vmem_limit_bytes=VMEM_LIMIT), name=name)(x)


def _coords():
    return lax.axis_index("x"), lax.axis_index("y"), lax.axis_index("c")


def _flip(v, bit):
    return 1 - v if bit else v


def _ag_small(x, name):
    r, c = x.shape

    def body(x_ref, o_ref, send, recv, local_sem):
        mx, my, mc = _coords()

        def rows(px, py, pc):
            return o_ref.at[pl.ds(pl.multiple_of((4 * px + 2 * py + pc) * r, 8), r), :]

        local = pltpu.make_async_copy(x_ref, rows(mx, my, mc), local_sem)
        local.start()
        peers = [(_flip(mx, k >> 2 & 1), _flip(my, k >> 1 & 1), _flip(mc, k & 1)) for k in range(1, N_DEV)]
        sends = [pltpu.make_async_remote_copy(x_ref, rows(mx, my, mc), send.at[k], recv.at[k], device_id=p,
                                              device_id_type=MESH) for k, p in enumerate(peers)]
        for cp in sends:
            cp.start()
        for k, p in enumerate(peers):
            pltpu.make_async_remote_copy(x_ref, rows(*p), send.at[k], recv.at[k], device_id=p,
                                         device_id_type=MESH).wait_recv()
        for cp in sends:
            cp.wait_send()
        local.wait()

    vm = pl.BlockSpec(memory_space=pltpu.VMEM)
    return pl.pallas_call(
        body, in_specs=[vm], out_specs=vm, out_shape=jax.ShapeDtypeStruct((N_DEV * r, c), x.dtype),
        scratch_shapes=[pltpu.SemaphoreType.DMA((N_DEV - 1,)), pltpu.SemaphoreType.DMA((N_DEV - 1,)),
                        pltpu.SemaphoreType.DMA(())],
        name=name)(x)


class _GatherSmall:
    mid = None

    def __init__(self, x):
        self.inputs = [x]
        self.out_shapes = [jax.ShapeDtypeStruct((N_DEV * x.shape[0], x.shape[1]), x.dtype)]
        self.scratch = [pltpu.SemaphoreType.DMA((N_DEV - 1,)), pltpu.SemaphoreType.DMA((N_DEV - 1,)),
                        pltpu.SemaphoreType.DMA(())]

    def _plan(self, x_refs, o_refs, sems):
        send, recv, local_sem = sems
        x_ref, o_ref = x_refs[0], o_refs[0]
        r = x_ref.shape[0]
        mx, my, mc = _coords()

        def rows(px, py, pc):
            return o_ref.at[pl.ds(pl.multiple_of((4 * px + 2 * py + pc) * r, 8), r), :]

        peers = [(_flip(mx, k >> 2 & 1), _flip(my, k >> 1 & 1), _flip(mc, k & 1)) for k in range(1, N_DEV)]
        out = [pltpu.make_async_remote_copy(x_ref, rows(mx, my, mc), send.at[k], recv.at[k], device_id=p,
                                            device_id_type=MESH) for k, p in enumerate(peers)]
        arrivals = [pltpu.make_async_remote_copy(x_ref, rows(*p), send.at[k], recv.at[k], device_id=p,
                                                 device_id_type=MESH) for k, p in enumerate(peers)]
        return out, arrivals, pltpu.make_async_copy(x_ref, rows(mx, my, mc), local_sem)

    def start(self, x_refs, o_refs, sems):
        out, _, local = self._plan(x_refs, o_refs, sems)
        local.start()
        for cp in out:
            cp.start()

    def finish(self, x_refs, o_refs, sems):
        out, arrivals, local = self._plan(x_refs, o_refs, sems)
        for cp in arrivals:
            cp.wait_recv()
        for cp in out:
            cp.wait_send()
        local.wait()


class _GatherWeights:
    def __init__(self, shards):
        n_t = len(shards)
        self.inputs = list(shards)
        self.out_shapes = [jax.ShapeDtypeStruct((N_DEV * x.shape[0], x.shape[1]), x.dtype) for x in shards]
        self.scratch = [pltpu.SemaphoreType.DMA((n_t, 7)), pltpu.SemaphoreType.DMA((n_t, 7)),
                        pltpu.SemaphoreType.DMA((n_t,))]

    def _plan(self, x_refs, o_refs, sems):
        send, recv, local_sem = sems
        mx, my, mc = _coords()
        me, sibling = (mx, my, mc), (mx, my, 1 - mc)
        chips = [(1 - mx, my), (mx, 1 - my), (1 - mx, 1 - my)]

        def rows(t, px, py, pc):
            r = x_refs[t].shape[0]
            return o_refs[t].at[pl.ds(pl.multiple_of((4 * px + 2 * py + pc) * r, 8), r), :]

        def copy(t, k, block, to, src=None):
            return pltpu.make_async_remote_copy(
                src_ref=rows(t, *block) if src is None else src, dst_ref=rows(t, *block),
                send_sem=send.at[t, k], recv_sem=recv.at[t, k], device_id=to, device_id_type=MESH)

        def local(t):
            return pltpu.make_async_copy(x_refs[t], rows(t, *me), local_sem.at[t])

        return me, sibling, chips, mc, copy, local

    def start(self, x_refs, o_refs, sems):
        me, sibling, chips, mc, copy, local = self._plan(x_refs, o_refs, sems)
        for t in range(len(x_refs)):
            local(t).start()
            copy(t, 0, me, sibling, src=x_refs[t]).start()
            for j, chip in enumerate(chips):
                copy(t, 1 + j, me, (*chip, mc), src=x_refs[t]).start()

    def mid(self, x_refs, o_refs, sems):
        me, sibling, chips, mc, copy, local = self._plan(x_refs, o_refs, sems)
        for j, chip in enumerate(chips):
            for t in range(len(x_refs)):
                copy(t, 1 + j, (*chip, mc), me).wait_recv()
                copy(t, 4 + j, (*chip, mc), sibling).start()

    def finish(self, x_refs, o_refs, sems):
        me, sibling, chips, mc, copy, local = self._plan(x_refs, o_refs, sems)
        for t in range(len(x_refs)):
            copy(t, 0, sibling, me).wait_recv()
            for j, chip in enumerate(chips):
                copy(t, 4 + j, (*chip, 1 - mc), me).wait_recv()
            copy(t, 0, me, sibling, src=x_refs[t]).wait_send()
            for j, chip in enumerate(chips):
                copy(t, 1 + j, me, (*chip, mc), src=x_refs[t]).wait_send()
                copy(t, 4 + j, (*chip, mc), sibling).wait_send()
            local(t).wait()


class _SiblingExchange:
    mid = None

    def __init__(self, grads):
        n_t = len(grads)
        self.inputs = list(grads)
        self.out_shapes = [jax.ShapeDtypeStruct((N_CHIP,) + g.shape[2:], F32) for g in grads]
        self.scratch = [pltpu.SemaphoreType.DMA((n_t,)), pltpu.SemaphoreType.DMA((n_t,))]

    def _copies(self, g_refs, land, sems):
        send, recv = sems
        mx, my, mc = _coords()
        return [pltpu.make_async_remote_copy(g_refs[t].at[:, 1 - mc], land[t], send.at[t], recv.at[t],
                                             device_id=(mx, my, 1 - mc), device_id_type=MESH)
                for t in range(len(g_refs))]

    def start(self, g_refs, land, sems):
        for cp in self._copies(g_refs, land, sems):
            cp.start()

    def finish(self, g_refs, land, sems):
        for cp in self._copies(g_refs, land, sems):
            cp.wait()


class _Together:
    def __init__(self, *comms):
        self.comms = comms
        self.inputs = [x for c in comms for x in c.inputs]
        self.out_shapes = [x for c in comms for x in c.out_shapes]
        self.scratch = [x for c in comms for x in c.scratch]
        self.mid = self._mid if any(c.mid is not None for c in comms) else None

    def _each(self, phase, cin, cout, sems):
        i = o = s = 0
        for c in self.comms:
            fn = getattr(c, phase)
            ni, no, ns = len(c.inputs), len(c.out_shapes), len(c.scratch)
            if fn is not None:
                fn(cin[i:i + ni], cout[o:o + no], sems[s:s + ns])
            i, o, s = i + ni, o + no, s + ns

    def start(self, cin, cout, sems):
        self._each("start", cin, cout, sems)

    def _mid(self, cin, cout, sems):
        self._each("mid", cin, cout, sems)

    def finish(self, cin, cout, sems):
        self._each("finish", cin, cout, sems)


def _standalone(comm, name):
    def body():
        pass
    return _call(body, grid=(1,), in_specs=[], out_specs=[], out_shape=[], args=(), name=name, comm=comm)[1]


def _chip_partials(g4s, lands, name):
    n_t = len(g4s)
    in_specs, out_specs, out_shape = [], [], []
    for g4 in g4s:
        _, _, r, c = g4.shape
        in_specs.append(pl.BlockSpec((None, None, r, c), lambda q: (q, lax.axis_index("c"), 0, 0)))
        out_specs.append(pl.BlockSpec((None, r, c), lambda q: (q, 0, 0)))
        out_shape.append(jax.ShapeDtypeStruct((N_CHIP, r, c), BF16))
    in_specs += [pl.BlockSpec((None,) + g4.shape[2:], lambda q: (q, 0, 0)) for g4 in g4s]

    def body(*refs):
        for t in range(n_t):
            refs[2 * n_t + t][...] = (refs[t][...] + refs[n_t + t][...]).astype(BF16)

    return pl.pallas_call(body, grid=(N_CHIP,), in_specs=in_specs, out_specs=out_specs, out_shape=out_shape,
                          compiler_params=_params(1), name=name)(*g4s, *lands)


class _ChipExchange:
    mid = None

    def __init__(self, parts):
        n_t = len(parts)
        self.inputs = list(parts)
        self.out_shapes = [jax.ShapeDtypeStruct(p.shape, p.dtype) for p in parts]
        self.scratch = [pltpu.SemaphoreType.DMA((n_t, 3)), pltpu.SemaphoreType.DMA((n_t, 3)),
                        pltpu.SemaphoreType.DMA((n_t,))]

    def _plan(self, p_refs, land, sems):
        send, recv, local_sem = sems
        mx, my, mc = _coords()
        my_chip = 2 * mx + my
        peers = [(_flip(mx, fx), _flip(my, fy)) for fx, fy in ((1, 0), (0, 1), (1, 1))]

        def out(t, k):
            px, py = peers[k]
            return pltpu.make_async_remote_copy(p_refs[t].at[2 * px + py], land[t].at[my_chip], send.at[t, k],
                                                recv.at[t, k], device_id=(px, py, mc), device_id_type=MESH)

        def arrival(t, k):
            px, py = peers[k]
            return pltpu.make_async_remote_copy(p_refs[t].at[my_chip], land[t].at[2 * px + py], send.at[t, k],
                                                recv.at[t, k], device_id=(px, py, mc), device_id_type=MESH)

        def local(t):
            return pltpu.make_async_copy(p_refs[t].at[my_chip], land[t].at[my_chip], local_sem.at[t])

        return out, arrival, local

    def start(self, p_refs, land, sems):
        out, arrival, local = self._plan(p_refs, land, sems)
        for t in range(len(p_refs)):
            local(t).start()
            for k in range(3):
                out(t, k).start()

    def finish(self, p_refs, land, sems):
        out, arrival, local = self._plan(p_refs, land, sems)
        for t in range(len(p_refs)):
            for k in range(3):
                arrival(t, k).wait_recv()
                out(t, k).wait_send()
            local(t).wait()


def _rope_tables(s, width):
    heads = width // HEAD_DIM
    inv_freq = ROPE_THETA ** (-jnp.arange(0, HEAD_DIM, 2, dtype=F32) / HEAD_DIM)
    inv_full = jnp.tile(inv_freq, 2 * heads)
    sign = jnp.tile(jnp.concatenate([-jnp.ones((HALF_HEAD,), F32), jnp.ones((HALF_HEAD,), F32)]), heads)
    ang = jnp.arange(s, dtype=F32)[:, None] * inv_full[None, :]
    return jnp.cos(ang), jnp.sin(ang) * sign[None, :]


def _pad_rows(v, rows):
    return jnp.concatenate([v, jnp.zeros((rows - 1, v.shape[1]), v.dtype)], axis=0)


def kernel(x, c, w_ada, b_ada, ffn1_norm_g, ffn1_w_gate, ffn1_w_up, ffn1_w_down, mix_norm_g, w_in, conv_dw_w, conv_dw_b, conv_ln_g, conv_ln_b, attn_out_g, conv_out_g, w_out, ffn2_norm_g, ffn2_w_gate, ffn2_w_up, ffn2_w_down, final_norm_g, loss_target, m_w_ada, m_b_ada, m_ffn1_norm_g, m_ffn1_w_gate, m_ffn1_w_up, m_ffn1_w_down, m_mix_norm_g, m_w_in, m_conv_dw_w, m_conv_dw_b, m_conv_ln_g, m_conv_ln_b, m_attn_out_g, m_conv_out_g, m_w_out, m_ffn2_norm_g, m_ffn2_w_gate, m_ffn2_w_up, m_ffn2_w_down, m_final_norm_g, v_w_ada, v_b_ada, v_ffn1_norm_g, v_ffn1_w_gate, v_ffn1_w_up, v_ffn1_w_down, v_mix_norm_g, v_w_in, v_conv_dw_w, v_conv_dw_b, v_conv_ln_g, v_conv_ln_b, v_attn_out_g, v_conv_out_g, v_w_out, v_ffn2_norm_g, v_ffn2_w_gate, v_ffn2_w_up, v_ffn2_w_down, v_final_norm_g):
    mx, my, mc = _coords()
    me = 4 * mx + 2 * my + mc
    s, d = x.shape[1], x.shape[2]
    aw = d // 2
    x2, target = x[0], loss_target[0]
    n_mod = w_ada.shape[2] * N_DEV // d
    mod_cols = w_ada.shape[2]

    def shard(w, transpose):
        return (w[0].T if transpose else w[0]).astype(BF16)

    cw_shard = conv_dw_w.shape[3]
    n_taps = CONV_KERNEL * cw_shard
    first_len = -(-(d + n_taps) // LANES) * LANES
    first = jnp.concatenate([c, conv_dw_w[0, :, 0, :].reshape(1, n_taps), jnp.zeros((1, first_len - d - n_taps), F32)], axis=1)
    first_all, wg1, wu1 = _standalone(
        _Together(_GatherSmall(_pad_rows(first, 8)), _GatherWeights([shard(ffn1_w_gate, True), shard(ffn1_w_up, True)])),
        "ag_first")
    first_all = first_all[0::8]
    c_all = first_all[:, :d]
    conv_w = first_all[:, d:d + n_taps].reshape(N_DEV, CONV_KERNEL, cw_shard).transpose(1, 0, 2).reshape(CONV_KERNEL, aw)

    silu_c = _silu_rows(c_all, "silu_c")
    mod_part = _plain_mm([(silu_c, w_ada[0])], F32, False, mod_cols, "mod_mm")
    mod_all = _ag_small(mod_part, "ag_mod").reshape(N_DEV, N_DEV, mod_cols)
    mod = lax.dynamic_index_in_dim(mod_all, me, axis=1, keepdims=False).reshape(1, n_mod * d) + b_ada
    sh1, sc1, g1, sh2, sc2, g2, sh3, sc3, g3 = [mod[:, i * d:(i + 1) * d] for i in range(n_mod)]

    def split(g):
        return g.reshape(N_CHIP, 2, g.shape[0] // N_DEV, g.shape[1])

    def partials(g4s, lands, tag):
        return _chip_partials(g4s, lands, "chip_partials_" + tag)

    gather_late = _GatherWeights([shard(ffn2_w_gate, True), shard(ffn2_w_up, True), shard(ffn2_w_down, False),
                                  shard(w_out, False)])

    n1 = _norm_mod_fwd(x2, ffn1_norm_g, sc1, sh1, "norm1")
    (a1, b1, hid1), (wd1,) = _ffn_up(n1, wg1, wu1, "ffn1_up", comm=_GatherWeights([shard(ffn1_w_down, False)]))
    (h1, f1, n2), (win_t,) = _residual_mm(hid1, wd1, x2, g1, 0.5, "ffn1_down", norm=(mix_norm_g, sc2, sh2),
                                          comm=_GatherWeights([shard(w_in, True)]))
    cos, sin_signed = _rope_tables(s, aw)
    proj = _proj_rope(n2, win_t, cos, sin_signed, aw, "proj")
    lanes_per = aw // LANES
    (attn, lse), (wg2, wu2, wd2, wout) = _attn_seq_fwd(proj, aw, "attn_fwd", comm=gather_late)
    u1 = _conv_fwd(proj, 3 * lanes_per, 4 * lanes_per, conv_w, conv_dw_b, "conv_fwd")
    y = _mix_post_fwd(attn, u1, attn_out_g, conv_ln_g, conv_ln_b, conv_out_g, "mix_post")
    h2, mix, n3 = _residual_mm(y, wout, h1, g2, 1.0, "mix_out", norm=(ffn2_norm_g, sc3, sh3))
    a3, b3, hid3 = _ffn_up(n3, wg2, wu2, "ffn2_up")

    dh3, df3, err2, d_final_g, dg3 = _last_mm_loss(hid3, wd2, h2, g3, 0.5, target, final_norm_g.reshape(1, d),
                                                   "ffn2_down_loss")
    loss = lax.psum(0.5 * jnp.sum(err2) / d, ("x", "y", "c"))

    da3, db3 = _ffn_bwd_hidden(df3, wd2, a3, b3, "ffn2_hidden_bwd")
    g4_a = [split(_mm_tn(da3, n3, "ffn2_dwg")), split(_mm_tn(db3, n3, "ffn2_dwu")), split(_mm_tn(hid3, df3, "ffn2_dwd"))]
    dn3, land_a = _plain_mm([(da3, wg2), (db3, wu2)], F32, False, d, "ffn2_dn", tm=512, comm=_SiblingExchange(g4_a))
    parts_a = partials(g4_a, land_a, "a")
    dh2, dmix, dsh3, dsc3, dgn3, dg2 = _norm_mod_bwd(dn3, h2, dh3, ffn2_norm_g, sc3, "norm3_bwd",
                                                     branch=(mix, g2, 1.0))
    dy = _plain_mm([(dmix, wout)], F32, True, d, "mix_dy")
    g_wout = _mm_tn(y, dmix, "mix_dwout")
    dattn, du1, d_attn_g, d_conv_g, d_ln_g, d_ln_b = _mix_post_bwd(
        dy, attn, u1, attn_out_g, conv_ln_g, conv_ln_b, conv_out_g, "mix_post_bwd")
    dga, dgb, d_taps, d_conv_b = _conv_bwd(proj, 3 * lanes_per, 4 * lanes_per, conv_w, du1, "conv_bwd")
    (dq, dk, dv), sums_a = _attn_seq_bwd(proj, dattn, attn, lse, "attn_bwd", comm=_ChipExchange(parts_a))
    dproj = _dproj_assemble(dq, dk, dv, dga, dgb, cos, sin_signed, "dproj")
    dn2 = _plain_mm([(dproj, win_t)], F32, False, d, "mix_dn")
    g4_b = [split(g_wout), split(_mm_tn(dproj, n2, "mix_dwin"))]
    (dh1, df1, dsh2, dsc2, dgn2, dg1), land_b = _norm_mod_bwd(dn2, h1, dh2, mix_norm_g, sc2, "norm2_bwd",
                                                              branch=(f1, g1, 0.5), comm=_SiblingExchange(g4_b))
    parts_b = partials(g4_b, land_b, "b")
    g4_c = [split(_mm_tn(hid1, df1, "ffn1_dwd"))]
    (da1, db1), both = _ffn_bwd_hidden(df1, wd1, a1, b1, "ffn1_hidden_bwd",
                                       comm=_Together(_ChipExchange(parts_b), _SiblingExchange(g4_c)))
    sums_b, land_c = both[:2], both[2:]
    parts_c = partials(g4_c, land_c, "c")
    g_wu1, sums_c = _mm_tn(db1, n1, "ffn1_dwu", comm=_ChipExchange(parts_c))
    g4_d = [split(g_wu1)]
    g_wg1, land_d = _mm_tn(da1, n1, "ffn1_dwg", comm=_SiblingExchange(g4_d))
    parts_d = partials(g4_d, land_d, "d")
    g4_e = [split(g_wg1)]
    dn1, both = _plain_mm([(da1, wg1), (db1, wu1)], F32, False, d, "ffn1_dn", tm=512,
                          comm=_Together(_ChipExchange(parts_d), _SiblingExchange(g4_e)))
    sums_d, land_e = both[:1], both[1:]
    parts_e = partials(g4_e, land_e, "e")
    (dx, dsh1, dsc1, dgn1), sums_e = _norm_mod_bwd(dn1, x2, dh1, ffn1_norm_g, sc1, "norm1_bwd",
                                                   comm=_ChipExchange(parts_e))

    dmod = jnp.concatenate([dsh1, dsc1, dg1, dsh2, dsc2, dg2, dsh3, dsc3, dg3], axis=1)
    small = [dmod, dgn1, dgn2, dgn3, d_final_g, d_conv_b, d_ln_g, d_ln_b, d_attn_g, d_conv_g,
             d_taps.reshape(1, CONV_KERNEL * aw)]
    sizes = [v.shape[1] for v in small]
    total = sum(sizes)
    padded = -(-total // (8 * LANES)) * (8 * LANES)
    packed = jnp.concatenate(small + [jnp.zeros((1, padded - total), F32)], axis=1).reshape(8, padded // 8)
    gathered = _ag_small(packed, "ag_small_grads")
    summed = _sum_blocks(gathered, N_DEV, "sum_small_grads").reshape(1, padded)
    offs = [sum(sizes[:i]) for i in range(len(sizes))]
    (g_b_ada, g_gn1, g_gn2, g_gn3, g_final, g_conv_b, g_ln_g, g_ln_b, g_attn_g, g_conv_g, g_taps) = [
        summed[:, o:o + n] for o, n in zip(offs, sizes)]
    g_taps_shard = lax.dynamic_slice_in_dim(g_taps.reshape(CONV_KERNEL, aw), me * cw_shard, cw_shard, axis=1)
    dmod_all = gathered.reshape(N_DEV, padded)[:, :n_mod * d]
    dmod_cols = lax.dynamic_slice_in_dim(dmod_all, me * mod_cols, mod_cols, axis=1)
    g_w_ada = _mm_tn(silu_c, dmod_cols, "ada_dw")

    arrived = dict(zip(["ffn2_w_gate", "ffn2_w_up", "ffn2_w_down", "w_out", "w_in", "ffn1_w_down", "ffn1_w_up",
                        "ffn1_w_gate"], list(sums_a) + list(sums_b) + list(sums_c) + list(sums_d) + list(sums_e)))
    transposed = ("ffn1_w_gate", "ffn1_w_up", "w_in", "ffn2_w_gate", "ffn2_w_up")
    grads = {
        "w_ada": g_w_ada, "b_ada": g_b_ada, "ffn1_norm_g": g_gn1, "mix_norm_g": g_gn2, "conv_dw_w": g_taps_shard,
        "conv_dw_b": g_conv_b, "conv_ln_g": g_ln_g, "conv_ln_b": g_ln_b, "attn_out_g": g_attn_g,
        "conv_out_g": g_conv_g, "ffn2_norm_g": g_gn3, "final_norm_g": g_final,
    }
    weights = dict(w_ada=w_ada, b_ada=b_ada, ffn1_norm_g=ffn1_norm_g, ffn1_w_gate=ffn1_w_gate, ffn1_w_up=ffn1_w_up, ffn1_w_down=ffn1_w_down, mix_norm_g=mix_norm_g, w_in=w_in, conv_dw_w=conv_dw_w, conv_dw_b=conv_dw_b, conv_ln_g=conv_ln_g, conv_ln_b=conv_ln_b, attn_out_g=attn_out_g, conv_out_g=conv_out_g, w_out=w_out, ffn2_norm_g=ffn2_norm_g, ffn2_w_gate=ffn2_w_gate, ffn2_w_up=ffn2_w_up, ffn2_w_down=ffn2_w_down, final_norm_g=final_norm_g)
    moms = dict(w_ada=m_w_ada, b_ada=m_b_ada, ffn1_norm_g=m_ffn1_norm_g, ffn1_w_gate=m_ffn1_w_gate, ffn1_w_up=m_ffn1_w_up, ffn1_w_down=m_ffn1_w_down, mix_norm_g=m_mix_norm_g, w_in=m_w_in, conv_dw_w=m_conv_dw_w, conv_dw_b=m_conv_dw_b, conv_ln_g=m_conv_ln_g, conv_ln_b=m_conv_ln_b, attn_out_g=m_attn_out_g, conv_out_g=m_conv_out_g, w_out=m_w_out, ffn2_norm_g=m_ffn2_norm_g, ffn2_w_gate=m_ffn2_w_gate, ffn2_w_up=m_ffn2_w_up, ffn2_w_down=m_ffn2_w_down, final_norm_g=m_final_norm_g)
    vars_ = dict(w_ada=v_w_ada, b_ada=v_b_ada, ffn1_norm_g=v_ffn1_norm_g, ffn1_w_gate=v_ffn1_w_gate, ffn1_w_up=v_ffn1_w_up, ffn1_w_down=v_ffn1_w_down, mix_norm_g=v_mix_norm_g, w_in=v_w_in, conv_dw_w=v_conv_dw_w, conv_dw_b=v_conv_dw_b, conv_ln_g=v_conv_ln_g, conv_ln_b=v_conv_ln_b, attn_out_g=v_attn_out_g, conv_out_g=v_conv_out_g, w_out=v_w_out, ffn2_norm_g=v_ffn2_norm_g, ffn2_w_gate=v_ffn2_w_gate, ffn2_w_up=v_ffn2_w_up, ffn2_w_down=v_ffn2_w_down, final_norm_g=v_final_norm_g)
    names = list(weights)
    big = ["w_ada", "ffn1_w_gate", "ffn1_w_up", "ffn1_w_down", "w_in", "w_out", "ffn2_w_gate", "ffn2_w_up",
           "ffn2_w_down"]
    shape2 = {n: (weights[n].shape[-2] if weights[n].ndim > 1 else 1, weights[n].shape[-1]) for n in names}
    shape2["conv_dw_w"] = (CONV_KERNEL, cw_shard)
    g_out, d_out, m_out, v_out = {}, {}, {}, {}
    for n in big:
        if n in arrived:
            def view(t, n=n):
                return t[0].T if n in transposed else t[0]
            res = _adamw_reduced(view(weights[n]), arrived[n], view(moms[n]), view(vars_[n]), "adamw_" + n)
            g_out[n], d_out[n], m_out[n], v_out[n] = [r.T if n in transposed else r for r in res]
        else:
            g2d = grads[n].reshape(shape2[n])
            res = _adamw_big(weights[n].reshape(shape2[n]), g2d, moms[n].reshape(shape2[n]),
                             vars_[n].reshape(shape2[n]), "adamw_" + n)
            g_out[n], (d_out[n], m_out[n], v_out[n]) = g2d, res
    rest = [n for n in names if n not in big]
    res = _adamw_small([weights[n].reshape(shape2[n]) for n in rest], [grads[n].reshape(shape2[n]) for n in rest],
                       [moms[n].reshape(shape2[n]) for n in rest], [vars_[n].reshape(shape2[n]) for n in rest],
                       "adamw_small")
    for i, n in enumerate(rest):
        g_out[n], d_out[n], m_out[n], v_out[n] = grads[n], res[0][i], res[1][i], res[2][i]

    def shaped(table):
        return [table[n].reshape(weights[n].shape) for n in names]

    return (loss, dx.reshape(x.shape), *shaped(g_out), *shaped(d_out), *shaped(m_out), *shaped(v_out))
```

```python
import functools

import jax
import jax.numpy as jnp
from jax import lax
from jax.experimental import pallas as pl
from jax.experimental.pallas import tpu as pltpu

F32 = jnp.float32
BF16 = jnp.bfloat16
MESH = pl.DeviceIdType.MESH
ANY = pl.BlockSpec(memory_space=pl.ANY)

N_DEV = 8
N_CHIP = 4
HEAD_DIM = 64
HALF_HEAD = HEAD_DIM // 2
LANES = 128
BLOCK = 128
DILATIONS = (1, 4, 16)
MERGE_CHUNK = 512
ROPE_THETA = 10000.0
CONV_KERNEL = 31
CONV_HALO = 32
CONV_CHUNK = 512
CONV_SUB = 128
RMS_EPS = 1e-6
LN_EPS = 1e-5
ADAM_LR = 0.001
ADAM_B1 = 0.9
ADAM_B2 = 0.999
ADAM_EPS = 1e-08
ADAM_WD = 0.01
ADAM_STEP = 10
VMEM_LIMIT = 56 * 1024 * 1024
NEG = -1e30


def _params(n_axes):
    return pltpu.CompilerParams(dimension_semantics=("arbitrary",) * n_axes, vmem_limit_bytes=VMEM_LIMIT)


def _tile(n, target, unit):
    best = None
    for t in range(unit, min(n, target) + 1, unit):
        if n % t == 0:
            best = t
    return best if best is not None else n


def _sigmoid(x):
    return 0.5 * (jnp.tanh(0.5 * x) + 1.0)


def _call(body, *, grid, in_specs, out_specs, out_shape, args, name, scratch_shapes=(), comm=None):
    params = _params(len(grid))
    if comm is None:
        return pl.pallas_call(body, grid=grid, in_specs=list(in_specs), out_specs=list(out_specs),
                              out_shape=list(out_shape), scratch_shapes=list(scratch_shapes),
                              compiler_params=params, name=name)(*args)
    n_in, n_out, n_scr = len(args), len(out_shape), len(scratch_shapes)
    c_in, c_out = len(comm.inputs), len(comm.out_shapes)
    steps = 1
    for g in grid:
        steps *= g

    def hosted(*refs):
        pos = 0
        parts = []
        for size in (n_in, c_in, n_out, c_out, n_scr, len(comm.scratch)):
            parts.append(refs[pos:pos + size])
            pos += size
        ins, cin, outs, cout, scr, cscr = parts
        step = 0
        for axis, g in enumerate(grid):
            step = step * g + pl.program_id(axis)

        @pl.when(step == 0)
        def _():
            comm.start(cin, cout, cscr)

        body(*ins, *outs, *scr)
        if comm.mid is not None and steps >= 4:
            @pl.when(step == (3 * steps) // 4)
            def _():
                comm.mid(cin, cout, cscr)

        @pl.when(step == steps - 1)
        def _():
            if comm.mid is not None and steps < 4:
                comm.mid(cin, cout, cscr)
            comm.finish(cin, cout, cscr)

    res = pl.pallas_call(
        hosted, grid=grid, in_specs=list(in_specs) + [ANY] * c_in, out_specs=list(out_specs) + [ANY] * c_out,
        out_shape=list(out_shape) + list(comm.out_shapes), scratch_shapes=list(scratch_shapes) + list(comm.scratch),
        compiler_params=params, name=name)(*args, *comm.inputs)
    return res[:n_out], res[n_out:]


def _rows(fn, rows_in, vecs_in, rows_out, vecs_out, *, tile, name, comm=None):
    norm = [r if isinstance(r, tuple) else (r, r.shape[1], 0) for r in rows_in]
    n_rows = norm[0][0].shape[0]
    n_tiles = n_rows // tile
    in_specs, args = [], []
    for arr, width, cb in norm:
        in_specs.append(pl.BlockSpec((tile, width), functools.partial(lambda i, cb: (i, cb), cb=cb)))
        args.append(arr)
    for v in vecs_in:
        in_specs.append(pl.BlockSpec((1, v.shape[1]), lambda i: (0, 0)))
        args.append(v)
    out_shape = [jax.ShapeDtypeStruct((n_rows, w), dt) for w, dt in rows_out]
    out_shape += [jax.ShapeDtypeStruct((1, w), F32) for w in vecs_out]
    out_specs = [pl.BlockSpec((tile, w), lambda i: (i, 0)) for w, _ in rows_out]
    out_specs += [pl.BlockSpec((1, w), lambda i: (0, 0)) for w in vecs_out]
    n_in, n_ro = len(args), len(rows_out)

    def body(*refs):
        vals = [r[...] for r in refs[:n_in]]
        outs = refs[n_in:]
        row_vals, vec_vals = fn(*vals)
        for ref, val in zip(outs[:n_ro], row_vals):
            if isinstance(val, tuple):
                w = val[0].shape[1]
                for j, piece in enumerate(val):
                    ref[:, j * w:(j + 1) * w] = piece.astype(ref.dtype)
            else:
                ref[...] = val.astype(ref.dtype)
        if vecs_out:
            @pl.when(pl.program_id(0) == 0)
            def _():
                for ref in outs[n_ro:]:
                    ref[...] = jnp.zeros_like(ref)
            for ref, val in zip(outs[n_ro:], vec_vals):
                ref[...] += val

    return _call(body, grid=(n_tiles,), in_specs=in_specs, out_specs=out_specs, out_shape=out_shape, args=args,
                 name=name, comm=comm)


def _colsum(x):
    return jnp.sum(x, axis=0, keepdims=True)


def _rms_stats(h):
    r = lax.rsqrt(jnp.mean(h * h, axis=-1, keepdims=True) + RMS_EPS)
    return r, h * r


def _rms_back(r, xn, dxn):
    return r * (dxn - xn * jnp.mean(dxn * xn, axis=-1, keepdims=True))


def _norm_mod_fwd(h, gain, scale, shift, name):
    def fn(h, gain, scale, shift):
        _, xn = _rms_stats(h)
        return [(xn * gain) * (1.0 + scale) + shift], []
    return _rows(fn, [h], [gain, scale, shift], [(h.shape[1], BF16)], [], tile=512, name=name)[0]


def _branch_back(dh, f, gate, coef):
    return (coef * gate) * dh, coef * _colsum(f.astype(F32) * dh)


def _norm_mod_bwd(dn, h, dh_in, gain, scale, name, branch=None, comm=None):
    d = h.shape[1]

    def back(dn, h, dh_in, gain, scale):
        r, xn = _rms_stats(h)
        y = xn * gain
        dy = dn * (1.0 + scale)
        dh = dh_in + _rms_back(r, xn, dy * gain)
        return dh, [_colsum(dn), _colsum(dn * y), _colsum(dy * xn)]

    if branch is None:
        def fn(dn, h, dh_in, gain, scale):
            dh, vecs = back(dn, h, dh_in, gain, scale)
            return [dh], vecs
        return _rows(fn, [dn, h, dh_in], [gain, scale], [(d, F32)], [d, d, d], tile=256, name=name, comm=comm)
    f, gate, coef = branch

    def fn_branch(dn, h, dh_in, f, gain, scale, gate):
        dh, vecs = back(dn, h, dh_in, gain, scale)
        df, dgate = _branch_back(dh, f, gate, coef)
        return [dh, df], vecs + [dgate]
    return _rows(fn_branch, [dn, h, dh_in, f], [gain, scale, gate], [(d, F32), (d, BF16)], [d, d, d, d], tile=256,
                 name=name, comm=comm)


def _last_mm_loss(lhs, w, res, gate, coef, target, gain, name):
    d = w.shape[1]

    def epi(accs, ex, vc):
        f = accs[0]
        h = ex[0] + (coef * vc[0]) * f
        r, xn = _rms_stats(h)
        err = xn * vc[1] - ex[1]
        dout = err * (1.0 / d)
        dh = _rms_back(r, xn, dout * vc[1])
        df, dgate = _branch_back(dh, f, vc[0], coef)
        return [dh, df, _colsum(err * err), _colsum(dout * xn), dgate]
    return _mm([[(lhs, w)]], epi, [res, target], [gate, gain], [F32, BF16], trans_rhs=False, tm=256, tn=d,
               name=name, n_sums=3)


def _partner(x):
    width = x.shape[1]
    lane = lax.broadcasted_iota(jnp.int32, x.shape, 1) % HEAD_DIM
    return jnp.where(lane < HALF_HEAD, pltpu.roll(x, width - HALF_HEAD, 1), pltpu.roll(x, HALF_HEAD, 1))


def _proj_rope(n, w_t, cos, sin_signed, width, name):
    s, kdim = n.shape
    n_cols = w_t.shape[0]
    tm = _tile(s, 1024, 8)
    qscale = HEAD_DIM ** -0.5

    def body(n_ref, w_ref, cos_ref, sin_ref, o_ref):
        j = pl.program_id(0)
        acc = lax.dot_general(n_ref[...].astype(BF16), w_ref[...].astype(BF16), (((1,), (1,)), ((), ())),
                              preferred_element_type=F32)

        @pl.when(j >= 2)
        def _():
            o_ref[...] = acc

        @pl.when(j < 2)
        def _():
            cos = jnp.tile(cos_ref[...], (1, width // LANES))
            sin = jnp.tile(sin_ref[...], (1, width // LANES))
            o_ref[...] = jnp.where(j == 0, qscale, 1.0) * (acc * cos + _partner(acc) * sin)

    table = pl.BlockSpec((tm, LANES), lambda j, i: (jnp.where(j < 2, i, 0), 0))
    return pl.pallas_call(
        body, grid=(n_cols // width, s // tm),
        in_specs=[pl.BlockSpec((tm, kdim), lambda j, i: (i, 0)), pl.BlockSpec((width, kdim), lambda j, i: (j, 0)),
                  table, table],
        out_specs=pl.BlockSpec((tm, width), lambda j, i: (i, j)), out_shape=jax.ShapeDtypeStruct((s, n_cols), F32),
        compiler_params=_params(2), name=name)(n, w_t, cos, sin_signed)


def _mix_post_fwd(attn, u1, attn_g, ln_g, ln_b, conv_g, name):
    def fn(attn, u1, attn_g, ln_g, ln_b, conv_g):
        _, xa = _rms_stats(attn)
        mu = jnp.mean(u1, axis=-1, keepdims=True)
        xc = u1 - mu
        rstd = lax.rsqrt(jnp.mean(xc * xc, axis=-1, keepdims=True) + LN_EPS)
        u2 = (xc * rstd) * ln_g + ln_b
        u3 = u2 * _sigmoid(u2)
        _, x3 = _rms_stats(u3)
        return [(xa * attn_g, x3 * conv_g)], []
    w = attn.shape[1]
    return _rows(fn, [attn, u1], [attn_g, ln_g, ln_b, conv_g], [(2 * w, BF16)], [], tile=512, name=name)[0]


def _mix_post_bwd(dy, attn, u1, attn_g, ln_g, ln_b, conv_g, name):
    w = attn.shape[1]

    def fn(dya, dyc, attn, u1, attn_g, ln_g, ln_b, conv_g):
        ra, xa = _rms_stats(attn)
        dattn = _rms_back(ra, xa, dya * attn_g)
        mu = jnp.mean(u1, axis=-1, keepdims=True)
        xc = u1 - mu
        rstd = lax.rsqrt(jnp.mean(xc * xc, axis=-1, keepdims=True) + LN_EPS)
        xh = xc * rstd
        u2 = xh * ln_g + ln_b
        sig = _sigmoid(u2)
        u3 = u2 * sig
        r3, x3 = _rms_stats(u3)
        du3 = _rms_back(r3, x3, dyc * conv_g)
        du2 = du3 * (sig + u3 * (1.0 - sig))
        dxh = du2 * ln_g
        du1 = rstd * (dxh - jnp.mean(dxh, axis=-1, keepdims=True) - xh * jnp.mean(dxh * xh, axis=-1, keepdims=True))
        return [dattn, du1], [_colsum(dya * xa), _colsum(dyc * x3), _colsum(du2 * xh), _colsum(du2)]
    return _rows(fn, [(dy, w, 0), (dy, w, 1), attn, u1], [attn_g, ln_g, ln_b, conv_g], [(w, F32), (w, F32)],
                 [w, w, w, w], tile=256, name=name)


def _silu_rows(c_all, name):
    def fn(c):
        return [c * _sigmoid(c)], []
    return _rows(fn, [c_all], [], [(c_all.shape[1], BF16)], [], tile=c_all.shape[0], name=name)[0]


def _mm(groups, epi, extras, vecs, outs, *, trans_rhs, tm, tn, name, n_sums=0, comm=None):
    m = groups[0][0][0].shape[0]
    n = groups[0][0][1].shape[0] if trans_rhs else groups[0][0][1].shape[1]
    tm, tn = min(tm, m), min(tn, n)
    in_specs, args = [], []
    for grp in groups:
        for lhs, rhs in grp:
            k = lhs.shape[1]
            in_specs.append(pl.BlockSpec((tm, k), lambda j, i: (i, 0)))
            in_specs.append(pl.BlockSpec((tn, k), lambda j, i: (j, 0)) if trans_rhs
                            else pl.BlockSpec((k, tn), lambda j, i: (0, j)))
            args += [lhs, rhs]
    for e in extras:
        in_specs.append(pl.BlockSpec((tm, tn), lambda j, i: (i, j)))
        args.append(e)
    for v in vecs:
        in_specs.append(pl.BlockSpec((1, tn), lambda j, i: (0, j)))
        args.append(v)
    sizes = [len(g) for g in groups]
    n_mm, n_ex, n_vec = 2 * sum(sizes), len(extras), len(vecs)
    dims = (((1,), (1,)), ((), ())) if trans_rhs else (((1,), (0,)), ((), ()))

    def body(*refs):
        accs, pos = [], 0
        for size in sizes:
            acc = None
            for _ in range(size):
                part = lax.dot_general(refs[pos][...].astype(BF16), refs[pos + 1][...].astype(BF16), dims,
                                       preferred_element_type=F32)
                acc = part if acc is None else acc + part
                pos += 2
            accs.append(acc)
        ex = [r[...] for r in refs[n_mm:n_mm + n_ex]]
        vc = [r[...] for r in refs[n_mm + n_ex:n_mm + n_ex + n_vec]]
        out_refs = refs[n_mm + n_ex + n_vec:]
        vals = epi(accs, ex, vc)
        for ref, val in zip(out_refs[:len(outs)], vals):
            ref[...] = val.astype(ref.dtype)
        if n_sums:
            @pl.when(pl.program_id(1) == 0)
            def _():
                for ref in out_refs[len(outs):]:
                    ref[...] = jnp.zeros_like(ref)
            for ref, val in zip(out_refs[len(outs):], vals[len(outs):]):
                ref[...] += val

    return _call(body, grid=(n // tn, m // tm), in_specs=in_specs,
                 out_specs=[pl.BlockSpec((tm, tn), lambda j, i: (i, j)) for _ in outs]
                 + [pl.BlockSpec((1, tn), lambda j, i: (0, j))] * n_sums,
                 out_shape=[jax.ShapeDtypeStruct((m, n), dt) for dt in outs]
                 + [jax.ShapeDtypeStruct((1, n), F32)] * n_sums, args=args, name=name, comm=comm)


def _mm_tn(lhs, rhs, name, comm=None):
    t, a = lhs.shape
    b = rhs.shape[1]
    ta = a if a <= 1536 else _tile(a, 1536, LANES)
    tk = _tile(t, 512, 8)

    def body(l_ref, r_ref, o_ref):
        @pl.when(pl.program_id(1) == 0)
        def _():
            o_ref[...] = jnp.zeros_like(o_ref)
        o_ref[...] += lax.dot_general(l_ref[...].astype(BF16), r_ref[...].astype(BF16), (((0,), (0,)), ((), ())),
                                      preferred_element_type=F32)

    res = _call(body, grid=(a // ta, t // tk),
                in_specs=[pl.BlockSpec((tk, ta), lambda i, k: (k, i)), pl.BlockSpec((tk, b), lambda i, k: (k, 0))],
                out_specs=[pl.BlockSpec((ta, b), lambda i, k: (i, 0))], out_shape=[jax.ShapeDtypeStruct((a, b), F32)],
                args=(lhs, rhs), name=name, comm=comm)
    return res[0] if comm is None else (res[0][0], res[1])


def _ffn_tn(f):
    return _tile(f, 1536, LANES)


def _ffn_up(n, wg_t, wu_t, name, comm=None):
    def epi(accs, ex, vc):
        a, b = accs
        return [a, b, (a * _sigmoid(a)) * b]
    return _mm([[(n, wg_t)], [(n, wu_t)]], epi, [], [], [BF16, BF16, BF16], trans_rhs=True, tm=512,
               tn=_ffn_tn(wg_t.shape[0]), name=name, comm=comm)


def _ffn_gate(n, wg_t, name, comm=None):
    def epi(accs, ex, vc):
        return [accs[0]]
    return _mm([[(n, wg_t)]], epi, [], [], [BF16], trans_rhs=True, tm=512, tn=_ffn_tn(wg_t.shape[0]), name=name,
               comm=comm)


def _ffn_up_given_gate(n, wu_t, a, name, comm=None):
    def epi(accs, ex, vc):
        av = ex[0].astype(F32)
        return [accs[0], (av * _sigmoid(av)) * accs[0]]
    return _mm([[(n, wu_t)]], epi, [a], [], [BF16, BF16], trans_rhs=True, tm=512, tn=_ffn_tn(wu_t.shape[0]),
               name=name, comm=comm)


def _residual_mm(lhs, w, res, gate, coef, name, norm=None, comm=None):
    def epi(accs, ex, vc):
        h = ex[0] + (coef * vc[0]) * accs[0]
        if norm is None:
            return [h, accs[0]]
        _, xn = _rms_stats(h)
        return [h, accs[0], (xn * vc[1]) * (1.0 + vc[2]) + vc[3]]
    vecs = [gate] + (list(norm) if norm is not None else [])
    outs = [F32, BF16] + ([BF16] if norm is not None else [])
    return _mm([[(lhs, w)]], epi, [res], vecs, outs, trans_rhs=False, tm=512, tn=w.shape[1], name=name, comm=comm)


def _ffn_bwd_hidden(df, wd, a, b, name, comm=None):
    def epi(accs, ex, vc):
        dh = accs[0]
        av, bv = ex[0].astype(F32), ex[1].astype(F32)
        sig = _sigmoid(av)
        silu = av * sig
        return [dh * bv * (sig + silu * (1.0 - sig)), dh * silu]
    return _mm([[(df, wd)]], epi, [a, b], [], [BF16, BF16], trans_rhs=True, tm=512, tn=_ffn_tn(wd.shape[0]),
               name=name, comm=comm)


def _plain_mm(pairs, out_dtype, trans_rhs, tn, name, tm=512, comm=None):
    def epi(accs, ex, vc):
        return [accs[0]]
    res = _mm([pairs], epi, [], [], [out_dtype], trans_rhs=trans_rhs, tm=tm, tn=tn, name=name, comm=comm)
    return res[0] if comm is None else (res[0][0], res[1])


HEADS_PER_TILE = LANES // HEAD_DIM


def _stack_heads(x):
    lane = lax.broadcasted_iota(jnp.int32, (1, LANES), 1)
    return jnp.concatenate([x * (lane // HEAD_DIM == h).astype(F32) for h in range(HEADS_PER_TILE)], axis=0)


def _unstack_heads(y):
    r = y.shape[0] // HEADS_PER_TILE
    lane = lax.broadcasted_iota(jnp.int32, (r, y.shape[1]), 1)
    out = y[0:r]
    for h in range(1, HEADS_PER_TILE):
        out = jnp.where(lane // HEAD_DIM == h, y[h * r:(h + 1) * r], out)
    return out


def _stacked_lse(lb):
    return jnp.concatenate([_lane_pick(lb, h) for h in range(HEADS_PER_TILE)], axis=0)


def _band_masks(n_row_blocks, n_col_blocks):
    shape = (n_row_blocks * BLOCK, n_col_blocks * BLOCK)
    qi = lax.broadcasted_iota(jnp.int32, shape, 0) % BLOCK
    kj = lax.broadcasted_iota(jnp.int32, shape, 1) % BLOCK
    return kj <= qi, kj >= qi


def _query_masks():
    same_ok, before_ok = _band_masks(HEADS_PER_TILE, 2)
    is_cur = lax.broadcasted_iota(jnp.int32, same_ok.shape, 1) >= BLOCK
    return jnp.logical_and(is_cur, same_ok), jnp.logical_and(jnp.logical_not(is_cur), before_ok)


def _dot_nt(a, b):
    return lax.dot_general(a.astype(BF16), b.astype(BF16), (((1,), (1,)), ((), ())), preferred_element_type=F32)


def _dot_nn(a, b):
    return lax.dot_general(a.astype(BF16), b.astype(BF16), (((1,), (0,)), ((), ())), preferred_element_type=F32)


def _dot_tn(a, b):
    return lax.dot_general(a.astype(BF16), b.astype(BF16), (((0,), (0,)), ((), ())), preferred_element_type=F32)


def _lane_pick(x, h):
    lane = lax.broadcasted_iota(jnp.int32, x.shape, 1)
    return jnp.sum(jnp.where(lane == h * HEAD_DIM, x, 0.0), axis=1, keepdims=True)


def _block_rows(idx, d):
    span = BLOCK * d
    g = idx // d
    q0 = g * span + idx % d
    has_prev = g > 0
    p0 = jnp.where(has_prev, q0 - span, q0)
    return pl.ds(q0, BLOCK, stride=d), pl.ds(p0, BLOCK, stride=d), has_prev


def _qkv_specs(s, tiles):
    q, k, v = [pl.BlockSpec((s, LANES), functools.partial(lambda hb, off: (0, off + hb), off=i * tiles))
               for i in range(3)]
    return q, k, v, pl.BlockSpec((s, LANES), lambda hb: (0, hb))


def _attn_seq_fwd(proj, width, name, comm=None):
    s = proj.shape[0]
    q_spec, k_spec, v_spec, cur = _qkv_specs(s, width // LANES)

    def body(q_ref, k_ref, v_ref, o_ref, l_ref, o_s, l_s):
        cur_valid, prev_valid = _query_masks()
        for bi, d in enumerate(DILATIONS):
            def blk(idx, carry, bi=bi, d=d):
                rows, prev, has_prev = _block_rows(idx, d)
                q2 = _stack_heads(q_ref[rows, :])
                keys = jnp.concatenate([k_ref[prev, :], k_ref[rows, :]], axis=0)
                vals = jnp.concatenate([v_ref[prev, :], v_ref[rows, :]], axis=0)
                valid = jnp.logical_or(cur_valid, jnp.logical_and(prev_valid, has_prev))
                sc = jnp.where(valid, _dot_nt(q2, keys), NEG)
                mx = jnp.max(sc, axis=1, keepdims=True)
                p = jnp.exp(sc - mx)
                den = jnp.sum(p, axis=1, keepdims=True)
                o_s[bi, rows, :] = _unstack_heads(_dot_nn(p, vals) / den)
                l_s[bi, rows, :] = _unstack_heads(jnp.broadcast_to(mx + jnp.log(den), (q2.shape[0], LANES)))
                return carry

            lax.fori_loop(0, s // BLOCK, blk, 0, unroll=8)
        for c in range(s // MERGE_CHUNK):
            rows = slice(c * MERGE_CHUNK, (c + 1) * MERGE_CHUNK)
            ls = [l_s[bi, rows, :] for bi in range(len(DILATIONS))]
            top = functools.reduce(jnp.maximum, ls)
            ws = [jnp.exp(l - top) for l in ls]
            den = functools.reduce(lambda a, b: a + b, ws)
            num = functools.reduce(lambda a, b: a + b, [w * o_s[bi, rows, :] for bi, w in enumerate(ws)])
            o_ref[rows, :] = num / den
            l_ref[rows, :] = top + jnp.log(den)

    return _call(
        body, grid=(width // LANES,), in_specs=[q_spec, k_spec, v_spec], out_specs=[cur, cur],
        out_shape=[jax.ShapeDtypeStruct((s, width), F32)] * 2,
        scratch_shapes=[pltpu.VMEM((len(DILATIONS), s, LANES), F32)] * 2,
        args=(proj, proj, proj), name=name, comm=comm)


def _attn_seq_bwd(proj, do, o, lse, cos, sin_signed, name, comm=None):
    s, width = do.shape
    q_spec, k_spec, v_spec, cur = _qkv_specs(s, width // LANES)
    table = pl.BlockSpec((s, LANES), lambda hb: (0, 0))
    qscale = HEAD_DIM ** -0.5

    def body(q_ref, k_ref, v_ref, do_ref, o_ref, l_ref, cos_ref, sin_ref, dq_out, dk_out, dv_out,
             dq_ref, dk_ref, dv_ref):
        dq_ref[...] = jnp.zeros_like(dq_ref)
        dk_ref[...] = jnp.zeros_like(dk_ref)
        dv_ref[...] = jnp.zeros_like(dv_ref)
        cur_valid, prev_valid = _query_masks()
        for d in DILATIONS:
            def blk(idx, carry, d=d):
                rows, prev, has_prev = _block_rows(idx, d)
                dob = do_ref[rows, :]
                q2 = _stack_heads(q_ref[rows, :])
                do2 = _stack_heads(dob)
                delta = jnp.sum(_stack_heads(dob * o_ref[rows, :]), axis=1, keepdims=True)
                lse2 = _stacked_lse(l_ref[rows, :])
                keys = jnp.concatenate([k_ref[prev, :], k_ref[rows, :]], axis=0)
                vals = jnp.concatenate([v_ref[prev, :], v_ref[rows, :]], axis=0)
                valid = jnp.logical_or(cur_valid, jnp.logical_and(prev_valid, has_prev))
                p = jnp.where(valid, jnp.exp(_dot_nt(q2, keys) - lse2), 0.0)
                ds = p * (_dot_nt(do2, vals) - delta)
                dq_ref[rows, :] += _unstack_heads(_dot_nn(ds, keys))
                dkk = _dot_tn(ds, q2)
                dvv = _dot_tn(p, do2)
                dk_ref[prev, :] += dkk[0:BLOCK]
                dk_ref[rows, :] += dkk[BLOCK:]
                dv_ref[prev, :] += dvv[0:BLOCK]
                dv_ref[rows, :] += dvv[BLOCK:]
                return carry

            lax.fori_loop(0, s // BLOCK, blk, 0, unroll=4)
        for c in range(s // MERGE_CHUNK):
            rows = slice(c * MERGE_CHUNK, (c + 1) * MERGE_CHUNK)
            cos, sin = cos_ref[rows, :], sin_ref[rows, :]
            dq, dk = dq_ref[rows, :], dk_ref[rows, :]
            dq_out[rows, :] = ((dq * cos - _partner(dq) * sin) * qscale).astype(BF16)
            dk_out[rows, :] = (dk * cos - _partner(dk) * sin).astype(BF16)
            dv_out[rows, :] = dv_ref[rows, :].astype(BF16)

    return _call(
        body, grid=(width // LANES,), in_specs=[q_spec, k_spec, v_spec, cur, cur, cur, table, table],
        out_specs=[cur, cur, cur], out_shape=[jax.ShapeDtypeStruct((s, width), BF16)] * 3,
        scratch_shapes=[pltpu.VMEM((s, LANES), F32)] * 3,
        args=(proj, proj, proj, do, o, lse, cos, sin_signed), name=name, comm=comm)


def _conv_specs(s, a_block, b_block):
    per = CONV_CHUNK // CONV_HALO
    a_cur = pl.BlockSpec((CONV_CHUNK, LANES), lambda cb, i: (i, a_block + cb))
    b_cur = pl.BlockSpec((CONV_CHUNK, LANES), lambda cb, i: (i, b_block + cb))
    a_halo = pl.BlockSpec((CONV_HALO, LANES), lambda cb, i: (jnp.maximum(i * per - 1, 0), a_block + cb))
    b_halo = pl.BlockSpec((CONV_HALO, LANES), lambda cb, i: (jnp.maximum(i * per - 1, 0), b_block + cb))
    w_spec = pl.BlockSpec((CONV_KERNEL, LANES), lambda cb, i: (0, cb))
    vec = pl.BlockSpec((1, LANES), lambda cb, i: (0, cb))
    out = pl.BlockSpec((CONV_CHUNK, LANES), lambda cb, i: (i, cb))
    return a_cur, b_cur, a_halo, b_halo, w_spec, vec, out


def _fill_glu_window(win, a_ref, b_ref, ah_ref, bh_ref, first):
    halo = ah_ref[...] * _sigmoid(bh_ref[...])
    win[0:CONV_HALO, :] = jnp.where(first, 0.0, halo)
    win[CONV_HALO:, :] = a_ref[...] * _sigmoid(b_ref[...])


def _conv_fwd(proj, a_block, b_block, w, bias, name):
    s = proj.shape[0]
    cw = w.shape[1]
    a_cur, b_cur, a_halo, b_halo, w_spec, vec, out = _conv_specs(s, a_block, b_block)
    lead = CONV_HALO - (CONV_KERNEL - 1)

    def body(a_ref, b_ref, ah_ref, bh_ref, w_ref, bias_ref, o_ref, win):
        _fill_glu_window(win, a_ref, b_ref, ah_ref, bh_ref, pl.program_id(1) == 0)
        for sub in range(CONV_CHUNK // CONV_SUB):
            base = sub * CONV_SUB
            acc = jnp.zeros((CONV_SUB, LANES), F32) + bias_ref[...]
            for j in range(CONV_KERNEL):
                acc = acc + w_ref[j:j + 1, :] * win[base + lead + j:base + lead + j + CONV_SUB, :]
            o_ref[base:base + CONV_SUB, :] = acc

    return pl.pallas_call(
        body, grid=(cw // LANES, s // CONV_CHUNK), in_specs=[a_cur, b_cur, a_halo, b_halo, w_spec, vec],
        out_specs=out, out_shape=jax.ShapeDtypeStruct((s, cw), F32),
        scratch_shapes=[pltpu.VMEM((CONV_CHUNK + CONV_HALO, LANES), F32)],
        compiler_params=_params(2), name=name)(proj, proj, proj, proj, w, bias)


def _conv_bwd(proj, a_block, b_block, w, du1, name):
    s = proj.shape[0]
    cw = w.shape[1]
    a_cur, b_cur, a_halo, b_halo, w_spec, vec, out = _conv_specs(s, a_block, b_block)
    per = CONV_CHUNK // CONV_HALO
    n_chunks = s // CONV_CHUNK
    d_next = pl.BlockSpec((CONV_HALO, LANES), lambda cb, i: (jnp.minimum((i + 1) * per, s // CONV_HALO - 1), cb))
    lead = CONV_HALO - (CONV_KERNEL - 1)

    def body(a_ref, b_ref, ah_ref, bh_ref, w_ref, d_ref, dn_ref, da_ref, db_ref, dw_ref, dbias_ref, win, dwin):
        i = pl.program_id(1)
        _fill_glu_window(win, a_ref, b_ref, ah_ref, bh_ref, i == 0)
        dwin[0:CONV_CHUNK, :] = d_ref[...]
        dwin[CONV_CHUNK:, :] = jnp.where(i == n_chunks - 1, 0.0, dn_ref[...])

        @pl.when(i == 0)
        def _():
            dw_ref[...] = jnp.zeros_like(dw_ref)
            dbias_ref[...] = jnp.zeros_like(dbias_ref)

        dbias_ref[...] += _colsum(d_ref[...])
        for sub in range(CONV_CHUNK // CONV_SUB):
            base = sub * CONV_SUB
            dcur = dwin[base:base + CONV_SUB, :]
            du0 = jnp.zeros((CONV_SUB, LANES), F32)
            for j in range(CONV_KERNEL):
                back = CONV_KERNEL - 1 - j
                du0 = du0 + w_ref[j:j + 1, :] * dwin[base + back:base + back + CONV_SUB, :]
                dw_ref[j:j + 1, :] += _colsum(dcur * win[base + lead + j:base + lead + j + CONV_SUB, :])
            av = a_ref[base:base + CONV_SUB, :]
            sig = _sigmoid(b_ref[base:base + CONV_SUB, :])
            da_ref[base:base + CONV_SUB, :] = (du0 * sig).astype(BF16)
            db_ref[base:base + CONV_SUB, :] = (du0 * av * sig * (1.0 - sig)).astype(BF16)

    return pl.pallas_call(
        body, grid=(cw // LANES, n_chunks), in_specs=[a_cur, b_cur, a_halo, b_halo, w_spec, out, d_next],
        out_specs=[out, out, w_spec, vec],
        out_shape=[jax.ShapeDtypeStruct((s, cw), BF16), jax.ShapeDtypeStruct((s, cw), BF16),
                   jax.ShapeDtypeStruct((CONV_KERNEL, cw), F32), jax.ShapeDtypeStruct((1, cw), F32)],
        scratch_shapes=[pltpu.VMEM((CONV_CHUNK + CONV_HALO, LANES), F32)] * 2,
        compiler_params=_params(2), name=name)(proj, proj, proj, proj, w, du1, du1)


def _adamw_math(w, g, m, v):
    m = ADAM_B1 * m + (1.0 - ADAM_B1) * g
    v = ADAM_B2 * v + (1.0 - ADAM_B2) * (g * g)
    m_hat = m / (1.0 - ADAM_B1 ** ADAM_STEP)
    v_hat = v / (1.0 - ADAM_B2 ** ADAM_STEP)
    delta = -ADAM_LR * (m_hat / (jnp.sqrt(v_hat) + ADAM_EPS) + ADAM_WD * w)
    return delta, m, v


def _adamw_big(w, g, m, v, name):
    rows, cols = w.shape
    tile = _tile(rows, 256, 8)
    spec = pl.BlockSpec((tile, cols), lambda i: (i, 0))

    def body(w_ref, g_ref, m_ref, v_ref, d_out, m_out, v_out):
        d_out[...], m_out[...], v_out[...] = _adamw_math(w_ref[...], g_ref[...], m_ref[...], v_ref[...])

    return pl.pallas_call(body, grid=(rows // tile,), in_specs=[spec] * 4, out_specs=[spec] * 3,
                          out_shape=[jax.ShapeDtypeStruct(w.shape, F32)] * 3, compiler_params=_params(1),
                          name=name)(w, g, m, v)


def _adamw_reduced(w, land, m, v, name):
    rows, cols = w.shape
    tile = _tile(rows, 256, 16)
    spec = pl.BlockSpec((tile, cols), lambda i: (i, 0))

    def body(w_ref, l_ref, m_ref, v_ref, g_out, d_out, m_out, v_out):
        g = l_ref[0].astype(F32)
        for q in range(1, N_CHIP):
            g = g + l_ref[q].astype(F32)
        g_out[...] = g
        d_out[...], m_out[...], v_out[...] = _adamw_math(w_ref[...], g, m_ref[...], v_ref[...])

    return pl.pallas_call(body, grid=(rows // tile,),
                          in_specs=[spec, pl.BlockSpec((N_CHIP, tile, cols), lambda i: (0, i, 0)), spec, spec],
                          out_specs=[spec] * 4, out_shape=[jax.ShapeDtypeStruct(w.shape, F32)] * 4,
                          compiler_params=_params(1), name=name)(w, land, m, v)


def _adamw_small(ws, gs, ms, vs, name):
    n = len(ws)

    def body(*refs):
        ins, outs = refs[:4 * n], refs[4 * n:]
        for t in range(n):
            res = _adamw_math(ins[t][...], ins[n + t][...], ins[2 * n + t][...], ins[3 * n + t][...])
            for j in range(3):
                outs[j * n + t][...] = res[j]

    shapes = [jax.ShapeDtypeStruct(w.shape, F32) for w in ws]
    res = pl.pallas_call(body, out_shape=shapes * 3, compiler_params=pltpu.CompilerParams(vmem_limit_bytes=VMEM_LIMIT),
                         name=name)(*ws, *gs, *ms, *vs)
    return res[:n], res[n:2 * n], res[2 * n:]


def _sum_blocks(x, n_blocks, name):
    r = x.shape[0] // n_blocks

    def body(x_ref, o_ref):
        acc = x_ref[0:r, :]
        for b in range(1, n_blocks):
            acc = acc + x_ref[b * r:(b + 1) * r, :]
        o_ref[...] = acc

    return pl.pallas_call(body, out_shape=jax.ShapeDtypeStruct((r, x.shape[1]), F32),
                          compiler_params=pltpu.CompilerParams(vmem_limit_bytes=VMEM_LIMIT), name=name)(x)


def _coords():
    return lax.axis_index("x"), lax.axis_index("y"), lax.axis_index("c")


def _flip(v, bit):
    return 1 - v if bit else v


def _ag_small(x, name):
    r, c = x.shape

    def body(x_ref, o_ref, send, recv, local_sem):
        mx, my, mc = _coords()

        def rows(px, py, pc):
            return o_ref.at[pl.ds(pl.multiple_of((4 * px + 2 * py + pc) * r, 8), r), :]

        local = pltpu.make_async_copy(x_ref, rows(mx, my, mc), local_sem)
        local.start()
        peers = [(_flip(mx, k >> 2 & 1), _flip(my, k >> 1 & 1), _flip(mc, k & 1)) for k in range(1, N_DEV)]
        sends = [pltpu.make_async_remote_copy(x_ref, rows(mx, my, mc), send.at[k], recv.at[k], device_id=p,
                                              device_id_type=MESH) for k, p in enumerate(peers)]
        for cp in sends:
            cp.start()
        for k, p in enumerate(peers):
            pltpu.make_async_remote_copy(x_ref, rows(*p), send.at[k], recv.at[k], device_id=p,
                                         device_id_type=MESH).wait_recv()
        for cp in sends:
            cp.wait_send()
        local.wait()

    vm = pl.BlockSpec(memory_space=pltpu.VMEM)
    return pl.pallas_call(
        body, in_specs=[vm], out_specs=vm, out_shape=jax.ShapeDtypeStruct((N_DEV * r, c), x.dtype),
        scratch_shapes=[pltpu.SemaphoreType.DMA((N_DEV - 1,)), pltpu.SemaphoreType.DMA((N_DEV - 1,)),
                        pltpu.SemaphoreType.DMA(())],
        name=name)(x)


class _GatherSmall:
    mid = None

    def __init__(self, x):
        self.inputs = [x]
        self.out_shapes = [jax.ShapeDtypeStruct((N_DEV * x.shape[0], x.shape[1]), x.dtype)]
        self.scratch = [pltpu.SemaphoreType.DMA((N_DEV - 1,)), pltpu.SemaphoreType.DMA((N_DEV - 1,)),
                        pltpu.SemaphoreType.DMA(())]

    def _plan(self, x_refs, o_refs, sems):
        send, recv, local_sem = sems
        x_ref, o_ref = x_refs[0], o_refs[0]
        r = x_ref.shape[0]
        mx, my, mc = _coords()

        def rows(px, py, pc):
            return o_ref.at[pl.ds(pl.multiple_of((4 * px + 2 * py + pc) * r, 8), r), :]

        peers = [(_flip(mx, k >> 2 & 1), _flip(my, k >> 1 & 1), _flip(mc, k & 1)) for k in range(1, N_DEV)]
        out = [pltpu.make_async_remote_copy(x_ref, rows(mx, my, mc), send.at[k], recv.at[k], device_id=p,
                                            device_id_type=MESH) for k, p in enumerate(peers)]
        arrivals = [pltpu.make_async_remote_copy(x_ref, rows(*p), send.at[k], recv.at[k], device_id=p,
                                                 device_id_type=MESH) for k, p in enumerate(peers)]
        return out, arrivals, pltpu.make_async_copy(x_ref, rows(mx, my, mc), local_sem)

    def start(self, x_refs, o_refs, sems):
        out, _, local = self._plan(x_refs, o_refs, sems)
        local.start()
        for cp in out:
            cp.start()

    def finish(self, x_refs, o_refs, sems):
        out, arrivals, local = self._plan(x_refs, o_refs, sems)
        for cp in arrivals:
            cp.wait_recv()
        for cp in out:
            cp.wait_send()
        local.wait()


class _GatherWeights:
    def __init__(self, shards):
        n_t = len(shards)
        self.inputs = list(shards)
        self.out_shapes = [jax.ShapeDtypeStruct((N_DEV * x.shape[0], x.shape[1]), x.dtype) for x in shards]
        self.scratch = [pltpu.SemaphoreType.DMA((n_t, 7)), pltpu.SemaphoreType.DMA((n_t, 7)),
                        pltpu.SemaphoreType.DMA((n_t,))]

    def _plan(self, x_refs, o_refs, sems):
        send, recv, local_sem = sems
        mx, my, mc = _coords()
        me, sibling = (mx, my, mc), (mx, my, 1 - mc)
        chips = [(1 - mx, my), (mx, 1 - my), (1 - mx, 1 - my)]

        def rows(t, px, py, pc):
            r = x_refs[t].shape[0]
            return o_refs[t].at[pl.ds(pl.multiple_of((4 * px + 2 * py + pc) * r, 8), r), :]

        def copy(t, k, block, to, src=None):
            return pltpu.make_async_remote_copy(
                src_ref=rows(t, *block) if src is None else src, dst_ref=rows(t, *block),
                send_sem=send.at[t, k], recv_sem=recv.at[t, k], device_id=to, device_id_type=MESH)

        def local(t):
            return pltpu.make_async_copy(x_refs[t], rows(t, *me), local_sem.at[t])

        return me, sibling, chips, mc, copy, local

    def start(self, x_refs, o_refs, sems):
        me, sibling, chips, mc, copy, local = self._plan(x_refs, o_refs, sems)
        for t in range(len(x_refs)):
            local(t).start()
            copy(t, 0, me, sibling, src=x_refs[t]).start()
            for j, chip in enumerate(chips):
                copy(t, 1 + j, me, (*chip, mc), src=x_refs[t]).start()

    def mid(self, x_refs, o_refs, sems):
        me, sibling, chips, mc, copy, local = self._plan(x_refs, o_refs, sems)
        for j, chip in enumerate(chips):
            for t in range(len(x_refs)):
                copy(t, 1 + j, (*chip, mc), me).wait_recv()
                copy(t, 4 + j, (*chip, mc), sibling).start()

    def finish(self, x_refs, o_refs, sems):
        me, sibling, chips, mc, copy, local = self._plan(x_refs, o_refs, sems)
        for t in range(len(x_refs)):
            copy(t, 0, sibling, me).wait_recv()
            for j, chip in enumerate(chips):
                copy(t, 4 + j, (*chip, 1 - mc), me).wait_recv()
            copy(t, 0, me, sibling, src=x_refs[t]).wait_send()
            for j, chip in enumerate(chips):
                copy(t, 1 + j, me, (*chip, mc), src=x_refs[t]).wait_send()
                copy(t, 4 + j, (*chip, mc), sibling).wait_send()
            local(t).wait()


class _SiblingExchange:
    mid = None

    def __init__(self, grads):
        n_t = len(grads)
        self.inputs = list(grads)
        self.out_shapes = [jax.ShapeDtypeStruct((N_CHIP,) + g.shape[2:], F32) for g in grads]
        self.scratch = [pltpu.SemaphoreType.DMA((n_t,)), pltpu.SemaphoreType.DMA((n_t,))]

    def _copies(self, g_refs, land, sems):
        send, recv = sems
        mx, my, mc = _coords()
        return [pltpu.make_async_remote_copy(g_refs[t].at[:, 1 - mc], land[t], send.at[t], recv.at[t],
                                             device_id=(mx, my, 1 - mc), device_id_type=MESH)
                for t in range(len(g_refs))]

    def start(self, g_refs, land, sems):
        for cp in self._copies(g_refs, land, sems):
            cp.start()

    def finish(self, g_refs, land, sems):
        for cp in self._copies(g_refs, land, sems):
            cp.wait()


class _Together:
    def __init__(self, *comms):
        self.comms = comms
        self.inputs = [x for c in comms for x in c.inputs]
        self.out_shapes = [x for c in comms for x in c.out_shapes]
        self.scratch = [x for c in comms for x in c.scratch]
        self.mid = self._mid if any(c.mid is not None for c in comms) else None

    def _each(self, phase, cin, cout, sems):
        i = o = s = 0
        for c in self.comms:
            fn = getattr(c, phase)
            ni, no, ns = len(c.inputs), len(c.out_shapes), len(c.scratch)
            if fn is not None:
                fn(cin[i:i + ni], cout[o:o + no], sems[s:s + ns])
            i, o, s = i + ni, o + no, s + ns

    def start(self, cin, cout, sems):
        self._each("start", cin, cout, sems)

    def _mid(self, cin, cout, sems):
        self._each("mid", cin, cout, sems)

    def finish(self, cin, cout, sems):
        self._each("finish", cin, cout, sems)


def _standalone(comm, name):
    def body():
        pass
    return _call(body, grid=(1,), in_specs=[], out_specs=[], out_shape=[], args=(), name=name, comm=comm)[1]


def _chip_partials(g4s, lands, name):
    n_t = len(g4s)
    in_specs, out_specs, out_shape = [], [], []
    for g4 in g4s:
        _, _, r, c = g4.shape
        in_specs.append(pl.BlockSpec((None, None, r, c), lambda q: (q, lax.axis_index("c"), 0, 0)))
        out_specs.append(pl.BlockSpec((None, r, c), lambda q: (q, 0, 0)))
        out_shape.append(jax.ShapeDtypeStruct((N_CHIP, r, c), BF16))
    in_specs += [pl.BlockSpec((None,) + g4.shape[2:], lambda q: (q, 0, 0)) for g4 in g4s]

    def body(*refs):
        for t in range(n_t):
            refs[2 * n_t + t][...] = (refs[t][...] + refs[n_t + t][...]).astype(BF16)

    return pl.pallas_call(body, grid=(N_CHIP,), in_specs=in_specs, out_specs=out_specs, out_shape=out_shape,
                          compiler_params=_params(1), name=name)(*g4s, *lands)


class _ChipExchange:
    mid = None

    def __init__(self, parts):
        n_t = len(parts)
        self.inputs = list(parts)
        self.out_shapes = [jax.ShapeDtypeStruct(p.shape, p.dtype) for p in parts]
        self.scratch = [pltpu.SemaphoreType.DMA((n_t, 3)), pltpu.SemaphoreType.DMA((n_t, 3)),
                        pltpu.SemaphoreType.DMA((n_t,))]

    def _plan(self, p_refs, land, sems):
        send, recv, local_sem = sems
        mx, my, mc = _coords()
        my_chip = 2 * mx + my
        peers = [(_flip(mx, fx), _flip(my, fy)) for fx, fy in ((1, 0), (0, 1), (1, 1))]

        def out(t, k):
            px, py = peers[k]
            return pltpu.make_async_remote_copy(p_refs[t].at[2 * px + py], land[t].at[my_chip], send.at[t, k],
                                                recv.at[t, k], device_id=(px, py, mc), device_id_type=MESH)

        def arrival(t, k):
            px, py = peers[k]
            return pltpu.make_async_remote_copy(p_refs[t].at[my_chip], land[t].at[2 * px + py], send.at[t, k],
                                                recv.at[t, k], device_id=(px, py, mc), device_id_type=MESH)

        def local(t):
            return pltpu.make_async_copy(p_refs[t].at[my_chip], land[t].at[my_chip], local_sem.at[t])

        return out, arrival, local

    def start(self, p_refs, land, sems):
        out, arrival, local = self._plan(p_refs, land, sems)
        for t in range(len(p_refs)):
            local(t).start()
            for k in range(3):
                out(t, k).start()

    def finish(self, p_refs, land, sems):
        out, arrival, local = self._plan(p_refs, land, sems)
        for t in range(len(p_refs)):
            for k in range(3):
                arrival(t, k).wait_recv()
                out(t, k).wait_send()
            local(t).wait()


def _rope_tables(s, width):
    heads = width // HEAD_DIM
    inv_freq = ROPE_THETA ** (-jnp.arange(0, HEAD_DIM, 2, dtype=F32) / HEAD_DIM)
    inv_full = jnp.tile(inv_freq, 2 * heads)
    sign = jnp.tile(jnp.concatenate([-jnp.ones((HALF_HEAD,), F32), jnp.ones((HALF_HEAD,), F32)]), heads)
    ang = jnp.arange(s, dtype=F32)[:, None] * inv_full[None, :]
    return jnp.cos(ang), jnp.sin(ang) * sign[None, :]


def _pad_rows(v, rows):
    return jnp.concatenate([v, jnp.zeros((rows - 1, v.shape[1]), v.dtype)], axis=0)


def kernel(x, c, w_ada, b_ada, ffn1_norm_g, ffn1_w_gate, ffn1_w_up, ffn1_w_down, mix_norm_g, w_in, conv_dw_w, conv_dw_b, conv_ln_g, conv_ln_b, attn_out_g, conv_out_g, w_out, ffn2_norm_g, ffn2_w_gate, ffn2_w_up, ffn2_w_down, final_norm_g, loss_target, m_w_ada, m_b_ada, m_ffn1_norm_g, m_ffn1_w_gate, m_ffn1_w_up, m_ffn1_w_down, m_mix_norm_g, m_w_in, m_conv_dw_w, m_conv_dw_b, m_conv_ln_g, m_conv_ln_b, m_attn_out_g, m_conv_out_g, m_w_out, m_ffn2_norm_g, m_ffn2_w_gate, m_ffn2_w_up, m_ffn2_w_down, m_final_norm_g, v_w_ada, v_b_ada, v_ffn1_norm_g, v_ffn1_w_gate, v_ffn1_w_up, v_ffn1_w_down, v_mix_norm_g, v_w_in, v_conv_dw_w, v_conv_dw_b, v_conv_ln_g, v_conv_ln_b, v_attn_out_g, v_conv_out_g, v_w_out, v_ffn2_norm_g, v_ffn2_w_gate, v_ffn2_w_up, v_ffn2_w_down, v_final_norm_g):
    mx, my, mc = _coords()
    me = 4 * mx + 2 * my + mc
    s, d = x.shape[1], x.shape[2]
    aw = d // 2
    x2, target = x[0], loss_target[0]
    n_mod = w_ada.shape[2] * N_DEV // d
    mod_cols = w_ada.shape[2]

    def shard(w, transpose):
        return (w[0].T if transpose else w[0]).astype(BF16)

    cw_shard = conv_dw_w.shape[3]
    n_taps = CONV_KERNEL * cw_shard
    first_len = -(-(d + n_taps) // LANES) * LANES
    first = jnp.concatenate([c, conv_dw_w[0, :, 0, :].reshape(1, n_taps), jnp.zeros((1, first_len - d - n_taps), F32)], axis=1)
    first_all, wg1 = _standalone(
        _Together(_GatherSmall(_pad_rows(first, 8)), _GatherWeights([shard(ffn1_w_gate, True)])), "ag_first")
    first_all = first_all[0::8]
    c_all = first_all[:, :d]
    conv_w = first_all[:, d:d + n_taps].reshape(N_DEV, CONV_KERNEL, cw_shard).transpose(1, 0, 2).reshape(CONV_KERNEL, aw)

    silu_c = _silu_rows(c_all, "silu_c")
    mod_part = _plain_mm([(silu_c, w_ada[0])], F32, False, mod_cols, "mod_mm")
    mod_all = _ag_small(mod_part, "ag_mod").reshape(N_DEV, N_DEV, mod_cols)
    mod = lax.dynamic_index_in_dim(mod_all, me, axis=1, keepdims=False).reshape(1, n_mod * d) + b_ada
    sh1, sc1, g1, sh2, sc2, g2, sh3, sc3, g3 = [mod[:, i * d:(i + 1) * d] for i in range(n_mod)]

    def split(g):
        return g.reshape(N_CHIP, 2, g.shape[0] // N_DEV, g.shape[1])

    def partials(g4s, lands, tag):
        return _chip_partials(g4s, lands, "chip_partials_" + tag)

    gather_late = _GatherWeights([shard(ffn2_w_gate, True), shard(ffn2_w_up, True), shard(ffn2_w_down, False),
                                  shard(w_out, False)])

    n1 = _norm_mod_fwd(x2, ffn1_norm_g, sc1, sh1, "norm1")
    (a1,), (wu1,) = _ffn_gate(n1, wg1, "ffn1_gate", comm=_GatherWeights([shard(ffn1_w_up, True)]))
    (b1, hid1), (wd1,) = _ffn_up_given_gate(n1, wu1, a1, "ffn1_up", comm=_GatherWeights([shard(ffn1_w_down, False)]))
    (h1, f1, n2), (win_t,) = _residual_mm(hid1, wd1, x2, g1, 0.5, "ffn1_down", norm=(mix_norm_g, sc2, sh2),
                                          comm=_GatherWeights([shard(w_in, True)]))
    cos, sin_signed = _rope_tables(s, LANES)
    proj = _proj_rope(n2, win_t, cos, sin_signed, aw, "proj")
    lanes_per = aw // LANES
    (attn, lse), (wg2, wu2, wd2, wout) = _attn_seq_fwd(proj, aw, "attn_fwd", comm=gather_late)
    u1 = _conv_fwd(proj, 3 * lanes_per, 4 * lanes_per, conv_w, conv_dw_b, "conv_fwd")
    y = _mix_post_fwd(attn, u1, attn_out_g, conv_ln_g, conv_ln_b, conv_out_g, "mix_post")
    h2, mix, n3 = _residual_mm(y, wout, h1, g2, 1.0, "mix_out", norm=(ffn2_norm_g, sc3, sh3))
    a3, b3, hid3 = _ffn_up(n3, wg2, wu2, "ffn2_up")

    dh3, df3, err2, d_final_g, dg3 = _last_mm_loss(hid3, wd2, h2, g3, 0.5, target, final_norm_g.reshape(1, d),
                                                   "ffn2_down_loss")
    loss = lax.psum(0.5 * jnp.sum(err2) / d, ("x", "y", "c"))

    da3, db3 = _ffn_bwd_hidden(df3, wd2, a3, b3, "ffn2_hidden_bwd")
    g4_a = [split(_mm_tn(da3, n3, "ffn2_dwg")), split(_mm_tn(db3, n3, "ffn2_dwu")), split(_mm_tn(hid3, df3, "ffn2_dwd"))]
    dn3, land_a = _plain_mm([(da3, wg2), (db3, wu2)], F32, False, d, "ffn2_dn", tm=512, comm=_SiblingExchange(g4_a))
    parts_a = partials(g4_a, land_a, "a")
    dh2, dmix, dsh3, dsc3, dgn3, dg2 = _norm_mod_bwd(dn3, h2, dh3, ffn2_norm_g, sc3, "norm3_bwd",
                                                     branch=(mix, g2, 1.0))
    dy = _plain_mm([(dmix, wout)], F32, True, d, "mix_dy")
    g_wout = _mm_tn(y, dmix, "mix_dwout")
    dattn, du1, d_attn_g, d_conv_g, d_ln_g, d_ln_b = _mix_post_bwd(
        dy, attn, u1, attn_out_g, conv_ln_g, conv_ln_b, conv_out_g, "mix_post_bwd")
    dga, dgb, d_taps, d_conv_b = _conv_bwd(proj, 3 * lanes_per, 4 * lanes_per, conv_w, du1, "conv_bwd")
    (dq, dk, dv), sums_a = _attn_seq_bwd(proj, dattn, attn, lse, cos, sin_signed, "attn_bwd",
                                         comm=_ChipExchange(parts_a))
    dproj = jnp.concatenate([dq, dk, dv, dga, dgb], axis=1)
    dn2 = _plain_mm([(dproj, win_t)], F32, False, d, "mix_dn")
    g4_b = [split(g_wout), split(_mm_tn(dproj, n2, "mix_dwin"))]
    (dh1, df1, dsh2, dsc2, dgn2, dg1), land_b = _norm_mod_bwd(dn2, h1, dh2, mix_norm_g, sc2, "norm2_bwd",
                                                              branch=(f1, g1, 0.5), comm=_SiblingExchange(g4_b))
    parts_b = partials(g4_b, land_b, "b")
    g4_c = [split(_mm_tn(hid1, df1, "ffn1_dwd"))]
    (da1, db1), both = _ffn_bwd_hidden(df1, wd1, a1, b1, "ffn1_hidden_bwd",
                                       comm=_Together(_ChipExchange(parts_b), _SiblingExchange(g4_c)))
    sums_b, land_c = both[:2], both[2:]
    parts_c = partials(g4_c, land_c, "c")
    g_wu1, sums_c = _mm_tn(db1, n1, "ffn1_dwu", comm=_ChipExchange(parts_c))
    g4_d = [split(g_wu1)]
    g_wg1, land_d = _mm_tn(da1, n1, "ffn1_dwg", comm=_SiblingExchange(g4_d))
    parts_d = partials(g4_d, land_d, "d")
    g4_e = [split(g_wg1)]
    dn1, both = _plain_mm([(da1, wg1), (db1, wu1)], F32, False, d, "ffn1_dn", tm=512,
                          comm=_Together(_ChipExchange(parts_d), _SiblingExchange(g4_e)))
    sums_d, land_e = both[:1], both[1:]
    parts_e = partials(g4_e, land_e, "e")
    (dx, dsh1, dsc1, dgn1), sums_e = _norm_mod_bwd(dn1, x2, dh1, ffn1_norm_g, sc1, "norm1_bwd",
                                                   comm=_ChipExchange(parts_e))

    dmod = jnp.concatenate([dsh1, dsc1, dg1, dsh2, dsc2, dg2, dsh3, dsc3, dg3], axis=1)
    small = [dmod, dgn1, dgn2, dgn3, d_final_g, d_conv_b, d_ln_g, d_ln_b, d_attn_g, d_conv_g,
             d_taps.reshape(1, CONV_KERNEL * aw)]
    sizes = [v.shape[1] for v in small]
    total = sum(sizes)
    padded = -(-total // (8 * LANES)) * (8 * LANES)
    packed = jnp.concatenate(small + [jnp.zeros((1, padded - total), F32)], axis=1).reshape(8, padded // 8)
    gathered = _ag_small(packed, "ag_small_grads")
    summed = _sum_blocks(gathered, N_DEV, "sum_small_grads").reshape(1, padded)
    offs = [sum(sizes[:i]) for i in range(len(sizes))]
    (g_b_ada, g_gn1, g_gn2, g_gn3, g_final, g_conv_b, g_ln_g, g_ln_b, g_attn_g, g_conv_g, g_taps) = [
        summed[:, o:o + n] for o, n in zip(offs, sizes)]
    g_taps_shard = lax.dynamic_slice_in_dim(g_taps.reshape(CONV_KERNEL, aw), me * cw_shard, cw_shard, axis=1)
    dmod_all = gathered.reshape(N_DEV, padded)[:, :n_mod * d]
    dmod_cols = lax.dynamic_slice_in_dim(dmod_all, me * mod_cols, mod_cols, axis=1)
    g_w_ada = _mm_tn(silu_c, dmod_cols, "ada_dw")

    arrived = dict(zip(["ffn2_w_gate", "ffn2_w_up", "ffn2_w_down", "w_out", "w_in", "ffn1_w_down", "ffn1_w_up",
                        "ffn1_w_gate"], list(sums_a) + list(sums_b) + list(sums_c) + list(sums_d) + list(sums_e)))
    transposed = ("ffn1_w_gate", "ffn1_w_up", "w_in", "ffn2_w_gate", "ffn2_w_up")
    grads = {
        "w_ada": g_w_ada, "b_ada": g_b_ada, "ffn1_norm_g": g_gn1, "mix_norm_g": g_gn2, "conv_dw_w": g_taps_shard,
        "conv_dw_b": g_conv_b, "conv_ln_g": g_ln_g, "conv_ln_b": g_ln_b, "attn_out_g": g_attn_g,
        "conv_out_g": g_conv_g, "ffn2_norm_g": g_gn3, "final_norm_g": g_final,
    }
    weights = dict(w_ada=w_ada, b_ada=b_ada, ffn1_norm_g=ffn1_norm_g, ffn1_w_gate=ffn1_w_gate, ffn1_w_up=ffn1_w_up, ffn1_w_down=ffn1_w_down, mix_norm_g=mix_norm_g, w_in=w_in, conv_dw_w=conv_dw_w, conv_dw_b=conv_dw_b, conv_ln_g=conv_ln_g, conv_ln_b=conv_ln_b, attn_out_g=attn_out_g, conv_out_g=conv_out_g, w_out=w_out, ffn2_norm_g=ffn2_norm_g, ffn2_w_gate=ffn2_w_gate, ffn2_w_up=ffn2_w_up, ffn2_w_down=ffn2_w_down, final_norm_g=final_norm_g)
    moms = dict(w_ada=m_w_ada, b_ada=m_b_ada, ffn1_norm_g=m_ffn1_norm_g, ffn1_w_gate=m_ffn1_w_gate, ffn1_w_up=m_ffn1_w_up, ffn1_w_down=m_ffn1_w_down, mix_norm_g=m_mix_norm_g, w_in=m_w_in, conv_dw_w=m_conv_dw_w, conv_dw_b=m_conv_dw_b, conv_ln_g=m_conv_ln_g, conv_ln_b=m_conv_ln_b, attn_out_g=m_attn_out_g, conv_out_g=m_conv_out_g, w_out=m_w_out, ffn2_norm_g=m_ffn2_norm_g, ffn2_w_gate=m_ffn2_w_gate, ffn2_w_up=m_ffn2_w_up, ffn2_w_down=m_ffn2_w_down, final_norm_g=m_final_norm_g)
    vars_ = dict(w_ada=v_w_ada, b_ada=v_b_ada, ffn1_norm_g=v_ffn1_norm_g, ffn1_w_gate=v_ffn1_w_gate, ffn1_w_up=v_ffn1_w_up, ffn1_w_down=v_ffn1_w_down, mix_norm_g=v_mix_norm_g, w_in=v_w_in, conv_dw_w=v_conv_dw_w, conv_dw_b=v_conv_dw_b, conv_ln_g=v_conv_ln_g, conv_ln_b=v_conv_ln_b, attn_out_g=v_attn_out_g, conv_out_g=v_conv_out_g, w_out=v_w_out, ffn2_norm_g=v_ffn2_norm_g, ffn2_w_gate=v_ffn2_w_gate, ffn2_w_up=v_ffn2_w_up, ffn2_w_down=v_ffn2_w_down, final_norm_g=v_final_norm_g)
    names = list(weights)
    big = ["w_ada", "ffn1_w_gate", "ffn1_w_up", "ffn1_w_down", "w_in", "w_out", "ffn2_w_gate", "ffn2_w_up",
           "ffn2_w_down"]
    shape2 = {n: (weights[n].shape[-2] if weights[n].ndim > 1 else 1, weights[n].shape[-1]) for n in names}
    shape2["conv_dw_w"] = (CONV_KERNEL, cw_shard)
    g_out, d_out, m_out, v_out = {}, {}, {}, {}
    for n in big:
        if n in arrived:
            def view(t, n=n):
                return t[0].T if n in transposed else t[0]
            res = _adamw_reduced(view(weights[n]), arrived[n], view(moms[n]), view(vars_[n]), "adamw_" + n)
            g_out[n], d_out[n], m_out[n], v_out[n] = [r.T if n in transposed else r for r in res]
        else:
            g2d = grads[n].reshape(shape2[n])
            res = _adamw_big(weights[n].reshape(shape2[n]), g2d, moms[n].reshape(shape2[n]),
                             vars_[n].reshape(shape2[n]), "adamw_" + n)
            g_out[n], (d_out[n], m_out[n], v_out[n]) = g2d, res
    rest = [n for n in names if n not in big]
    res = _adamw_small([weights[n].reshape(shape2[n]) for n in rest], [grads[n].reshape(shape2[n]) for n in rest],
                       [moms[n].reshape(shape2[n]) for n in rest], [vars_[n].reshape(shape2[n]) for n in rest],
                       "adamw_small")
    for i, n in enumerate(rest):
        g_out[n], d_out[n], m_out[n], v_out[n] = grads[n], res[0][i], res[1][i], res[2][i]

    def shaped(table):
        return [table[n].reshape(weights[n].shape) for n in names]

    return (loss, dx.reshape(x.shape), *shaped(g_out), *shaped(d_out), *shaped(m_out), *shaped(v_out))
```

```python
import functools

import jax
import jax.numpy as jnp
from jax import lax
from jax.experimental import pallas as pl
from jax.experimental.pallas import tpu as pltpu

F32 = jnp.float32
BF16 = jnp.bfloat16
MESH = pl.DeviceIdType.MESH
ANY = pl.BlockSpec(memory_space=pl.ANY)

N_DEV = 8
N_CHIP = 4
HEAD_DIM = 64
HALF_HEAD = HEAD_DIM // 2
LANES = 128
BLOCK = 128
DILATIONS = (1, 4, 16)
MERGE_CHUNK = 512
ROPE_THETA = 10000.0
CONV_KERNEL = 31
CONV_HALO = 32
CONV_CHUNK = 512
CONV_SUB = 128
RMS_EPS = 1e-6
LN_EPS = 1e-5
ADAM_LR = 0.001
ADAM_B1 = 0.9
ADAM_B2 = 0.999
ADAM_EPS = 1e-08
ADAM_WD = 0.01
ADAM_STEP = 10
VMEM_LIMIT = 56 * 1024 * 1024
NEG = -1e30


def _params(n_axes):
    return pltpu.CompilerParams(dimension_semantics=("arbitrary",) * n_axes, vmem_limit_bytes=VMEM_LIMIT)


def _tile(n, target, unit):
    best = None
    for t in range(unit, min(n, target) + 1, unit):
        if n % t == 0:
            best = t
    return best if best is not None else n


def _sigmoid(x):
    return 0.5 * (jnp.tanh(0.5 * x) + 1.0)


def _call(body, *, grid, in_specs, out_specs, out_shape, args, name, scratch_shapes=(), comm=None):
    params = _params(len(grid))
    if comm is None:
        return pl.pallas_call(body, grid=grid, in_specs=list(in_specs), out_specs=list(out_specs),
                              out_shape=list(out_shape), scratch_shapes=list(scratch_shapes),
                              compiler_params=params, name=name)(*args)
    n_in, n_out, n_scr = len(args), len(out_shape), len(scratch_shapes)
    c_in, c_out = len(comm.inputs), len(comm.out_shapes)
    steps = 1
    for g in grid:
        steps *= g

    def hosted(*refs):
        pos = 0
        parts = []
        for size in (n_in, c_in, n_out, c_out, n_scr, len(comm.scratch)):
            parts.append(refs[pos:pos + size])
            pos += size
        ins, cin, outs, cout, scr, cscr = parts
        step = 0
        for axis, g in enumerate(grid):
            step = step * g + pl.program_id(axis)

        @pl.when(step == 0)
        def _():
            comm.start(cin, cout, cscr)

        body(*ins, *outs, *scr)
        if comm.mid is not None and steps >= 4:
            @pl.when(step == (3 * steps) // 4)
            def _():
                comm.mid(cin, cout, cscr)

        @pl.when(step == steps - 1)
        def _():
            if comm.mid is not None and steps < 4:
                comm.mid(cin, cout, cscr)
            comm.finish(cin, cout, cscr)

    res = pl.pallas_call(
        hosted, grid=grid, in_specs=list(in_specs) + [ANY] * c_in, out_specs=list(out_specs) + [ANY] * c_out,
        out_shape=list(out_shape) + list(comm.out_shapes), scratch_shapes=list(scratch_shapes) + list(comm.scratch),
        compiler_params=params, name=name)(*args, *comm.inputs)
    return res[:n_out], res[n_out:]


def _rows(fn, rows_in, vecs_in, rows_out, vecs_out, *, tile, name, comm=None):
    norm = [r if isinstance(r, tuple) else (r, r.shape[1], 0) for r in rows_in]
    n_rows = norm[0][0].shape[0]
    n_tiles = n_rows // tile
    in_specs, args = [], []
    for arr, width, cb in norm:
        in_specs.append(pl.BlockSpec((tile, width), functools.partial(lambda i, cb: (i, cb), cb=cb)))
        args.append(arr)
    for v in vecs_in:
        in_specs.append(pl.BlockSpec((1, v.shape[1]), lambda i: (0, 0)))
        args.append(v)
    out_shape = [jax.ShapeDtypeStruct((n_rows, w), dt) for w, dt in rows_out]
    out_shape += [jax.ShapeDtypeStruct((1, w), F32) for w in vecs_out]
    out_specs = [pl.BlockSpec((tile, w), lambda i: (i, 0)) for w, _ in rows_out]
    out_specs += [pl.BlockSpec((1, w), lambda i: (0, 0)) for w in vecs_out]
    n_in, n_ro = len(args), len(rows_out)

    def body(*refs):
        vals = [r[...] for r in refs[:n_in]]
        outs = refs[n_in:]
        row_vals, vec_vals = fn(*vals)
        for ref, val in zip(outs[:n_ro], row_vals):
            if isinstance(val, tuple):
                w = val[0].shape[1]
                for j, piece in enumerate(val):
                    ref[:, j * w:(j + 1) * w] = piece.astype(ref.dtype)
            else:
                ref[...] = val.astype(ref.dtype)
        if vecs_out:
            @pl.when(pl.program_id(0) == 0)
            def _():
                for ref in outs[n_ro:]:
                    ref[...] = jnp.zeros_like(ref)
            for ref, val in zip(outs[n_ro:], vec_vals):
                ref[...] += val

    return _call(body, grid=(n_tiles,), in_specs=in_specs, out_specs=out_specs, out_shape=out_shape, args=args,
                 name=name, comm=comm)


def _colsum(x):
    return jnp.sum(x, axis=0, keepdims=True)


def _rms_stats(h):
    r = lax.rsqrt(jnp.mean(h * h, axis=-1, keepdims=True) + RMS_EPS)
    return r, h * r


def _rms_back(r, xn, dxn):
    return r * (dxn - xn * jnp.mean(dxn * xn, axis=-1, keepdims=True))


def _norm_mod_fwd(h, gain, scale, shift, name):
    def fn(h, gain, scale, shift):
        _, xn = _rms_stats(h)
        return [(xn * gain) * (1.0 + scale) + shift], []
    return _rows(fn, [h], [gain, scale, shift], [(h.shape[1], BF16)], [], tile=512, name=name)[0]


def _branch_back(dh, f, gate, coef):
    return (coef * gate) * dh, coef * _colsum(f.astype(F32) * dh)


def _norm_mod_bwd(dn, h, dh_in, gain, scale, name, branch=None, comm=None):
    d = h.shape[1]

    def back(dn, h, dh_in, gain, scale):
        dn = dn.astype(F32)
        r, xn = _rms_stats(h)
        y = xn * gain
        dy = dn * (1.0 + scale)
        dh = dh_in + _rms_back(r, xn, dy * gain)
        return dh, [_colsum(dn), _colsum(dn * y), _colsum(dy * xn)]

    if branch is None:
        def fn(dn, h, dh_in, gain, scale):
            dh, vecs = back(dn, h, dh_in, gain, scale)
            return [dh], vecs
        return _rows(fn, [dn, h, dh_in], [gain, scale], [(d, F32)], [d, d, d], tile=256, name=name, comm=comm)
    f, gate, coef = branch

    def fn_branch(dn, h, dh_in, f, gain, scale, gate):
        dh, vecs = back(dn, h, dh_in, gain, scale)
        df, dgate = _branch_back(dh, f, gate, coef)
        return [dh, df], vecs + [dgate]
    return _rows(fn_branch, [dn, h, dh_in, f], [gain, scale, gate], [(d, F32), (d, BF16)], [d, d, d, d], tile=256,
                 name=name, comm=comm)


def _last_mm_loss(lhs, w, res, gate, coef, target, gain, name):
    d = w.shape[1]

    def epi(accs, ex, vc):
        f = accs[0]
        h = ex[0] + (coef * vc[0]) * f
        r, xn = _rms_stats(h)
        err = xn * vc[1] - ex[1]
        dout = err * (1.0 / d)
        dh = _rms_back(r, xn, dout * vc[1])
        df, dgate = _branch_back(dh, f, vc[0], coef)
        return [dh, df, _colsum(err * err), _colsum(dout * xn), dgate]
    return _mm([[(lhs, w)]], epi, [res, target], [gate, gain], [F32, BF16], trans_rhs=False, tm=256, tn=d,
               name=name, n_sums=3)


def _partner(x):
    width = x.shape[1]
    lane = lax.broadcasted_iota(jnp.int32, x.shape, 1) % HEAD_DIM
    return jnp.where(lane < HALF_HEAD, pltpu.roll(x, width - HALF_HEAD, 1), pltpu.roll(x, HALF_HEAD, 1))


def _proj_rope(n, w_t, cos, sin_signed, width, name):
    s, kdim = n.shape
    n_cols = w_t.shape[0]
    tm = _tile(s, 1024, 8)
    qscale = HEAD_DIM ** -0.5

    def body(n_ref, w_ref, cos_ref, sin_ref, o_ref):
        j = pl.program_id(0)
        acc = lax.dot_general(n_ref[...].astype(BF16), w_ref[...].astype(BF16), (((1,), (1,)), ((), ())),
                              preferred_element_type=F32)

        @pl.when(j >= 2)
        def _():
            o_ref[...] = acc

        @pl.when(j < 2)
        def _():
            cos = jnp.tile(cos_ref[...], (1, width // LANES))
            sin = jnp.tile(sin_ref[...], (1, width // LANES))
            o_ref[...] = jnp.where(j == 0, qscale, 1.0) * (acc * cos + _partner(acc) * sin)

    table = pl.BlockSpec((tm, LANES), lambda j, i: (jnp.where(j < 2, i, 0), 0))
    return pl.pallas_call(
        body, grid=(n_cols // width, s // tm),
        in_specs=[pl.BlockSpec((tm, kdim), lambda j, i: (i, 0)), pl.BlockSpec((width, kdim), lambda j, i: (j, 0)),
                  table, table],
        out_specs=pl.BlockSpec((tm, width), lambda j, i: (i, j)), out_shape=jax.ShapeDtypeStruct((s, n_cols), F32),
        compiler_params=_params(2), name=name)(n, w_t, cos, sin_signed)


def _mix_post_fwd(attn, u1, attn_g, ln_g, ln_b, conv_g, name):
    def fn(attn, u1, attn_g, ln_g, ln_b, conv_g):
        _, xa = _rms_stats(attn)
        mu = jnp.mean(u1, axis=-1, keepdims=True)
        xc = u1 - mu
        rstd = lax.rsqrt(jnp.mean(xc * xc, axis=-1, keepdims=True) + LN_EPS)
        u2 = (xc * rstd) * ln_g + ln_b
        u3 = u2 * _sigmoid(u2)
        _, x3 = _rms_stats(u3)
        return [(xa * attn_g, x3 * conv_g)], []
    w = attn.shape[1]
    return _rows(fn, [attn, u1], [attn_g, ln_g, ln_b, conv_g], [(2 * w, BF16)], [], tile=512, name=name)[0]


def _mix_post_bwd(dy, attn, u1, attn_g, ln_g, ln_b, conv_g, name):
    w = attn.shape[1]

    def fn(dya, dyc, attn, u1, attn_g, ln_g, ln_b, conv_g):
        dya, dyc = dya.astype(F32), dyc.astype(F32)
        ra, xa = _rms_stats(attn)
        dattn = _rms_back(ra, xa, dya * attn_g)
        mu = jnp.mean(u1, axis=-1, keepdims=True)
        xc = u1 - mu
        rstd = lax.rsqrt(jnp.mean(xc * xc, axis=-1, keepdims=True) + LN_EPS)
        xh = xc * rstd
        u2 = xh * ln_g + ln_b
        sig = _sigmoid(u2)
        u3 = u2 * sig
        r3, x3 = _rms_stats(u3)
        du3 = _rms_back(r3, x3, dyc * conv_g)
        du2 = du3 * (sig + u3 * (1.0 - sig))
        dxh = du2 * ln_g
        du1 = rstd * (dxh - jnp.mean(dxh, axis=-1, keepdims=True) - xh * jnp.mean(dxh * xh, axis=-1, keepdims=True))
        return [dattn, du1], [_colsum(dya * xa), _colsum(dyc * x3), _colsum(du2 * xh), _colsum(du2)]
    return _rows(fn, [(dy, w, 0), (dy, w, 1), attn, u1], [attn_g, ln_g, ln_b, conv_g], [(w, F32), (w, F32)],
                 [w, w, w, w], tile=256, name=name)


def _silu_rows(c_all, name):
    def fn(c):
        return [c * _sigmoid(c)], []
    return _rows(fn, [c_all], [], [(c_all.shape[1], BF16)], [], tile=c_all.shape[0], name=name)[0]


def _mm(groups, epi, extras, vecs, outs, *, trans_rhs, tm, tn, name, n_sums=0, comm=None):
    m = groups[0][0][0].shape[0]
    n = groups[0][0][1].shape[0] if trans_rhs else groups[0][0][1].shape[1]
    tm, tn = min(tm, m), min(tn, n)
    in_specs, args = [], []
    for grp in groups:
        for lhs, rhs in grp:
            k = lhs.shape[1]
            in_specs.append(pl.BlockSpec((tm, k), lambda j, i: (i, 0)))
            in_specs.append(pl.BlockSpec((tn, k), lambda j, i: (j, 0)) if trans_rhs
                            else pl.BlockSpec((k, tn), lambda j, i: (0, j)))
            args += [lhs, rhs]
    for e in extras:
        in_specs.append(pl.BlockSpec((tm, tn), lambda j, i: (i, j)))
        args.append(e)
    for v in vecs:
        in_specs.append(pl.BlockSpec((1, tn), lambda j, i: (0, j)))
        args.append(v)
    sizes = [len(g) for g in groups]
    n_mm, n_ex, n_vec = 2 * sum(sizes), len(extras), len(vecs)
    dims = (((1,), (1,)), ((), ())) if trans_rhs else (((1,), (0,)), ((), ()))

    def body(*refs):
        accs, pos = [], 0
        for size in sizes:
            acc = None
            for _ in range(size):
                part = lax.dot_general(refs[pos][...].astype(BF16), refs[pos + 1][...].astype(BF16), dims,
                                       preferred_element_type=F32)
                acc = part if acc is None else acc + part
                pos += 2
            accs.append(acc)
        ex = [r[...] for r in refs[n_mm:n_mm + n_ex]]
        vc = [r[...] for r in refs[n_mm + n_ex:n_mm + n_ex + n_vec]]
        out_refs = refs[n_mm + n_ex + n_vec:]
        vals = epi(accs, ex, vc)
        for ref, val in zip(out_refs[:len(outs)], vals):
            ref[...] = val.astype(ref.dtype)
        if n_sums:
            @pl.when(pl.program_id(1) == 0)
            def _():
                for ref in out_refs[len(outs):]:
                    ref[...] = jnp.zeros_like(ref)
            for ref, val in zip(out_refs[len(outs):], vals[len(outs):]):
                ref[...] += val

    return _call(body, grid=(n // tn, m // tm), in_specs=in_specs,
                 out_specs=[pl.BlockSpec((tm, tn), lambda j, i: (i, j)) for _ in outs]
                 + [pl.BlockSpec((1, tn), lambda j, i: (0, j))] * n_sums,
                 out_shape=[jax.ShapeDtypeStruct((m, n), dt) for dt in outs]
                 + [jax.ShapeDtypeStruct((1, n), F32)] * n_sums, args=args, name=name, comm=comm)


def _mm_tn(lhs, rhs, name, comm=None):
    t, a = lhs.shape
    b = rhs.shape[1]
    ta = a if a <= 1536 else _tile(a, 1536, LANES)
    tk = _tile(t, 2048, 8)

    def body(l_ref, r_ref, o_ref):
        @pl.when(pl.program_id(1) == 0)
        def _():
            o_ref[...] = jnp.zeros_like(o_ref)
        o_ref[...] += lax.dot_general(l_ref[...].astype(BF16), r_ref[...].astype(BF16), (((0,), (0,)), ((), ())),
                                      preferred_element_type=F32)

    res = _call(body, grid=(a // ta, t // tk),
                in_specs=[pl.BlockSpec((tk, ta), lambda i, k: (k, i)), pl.BlockSpec((tk, b), lambda i, k: (k, 0))],
                out_specs=[pl.BlockSpec((ta, b), lambda i, k: (i, 0))], out_shape=[jax.ShapeDtypeStruct((a, b), F32)],
                args=(lhs, rhs), name=name, comm=comm)
    return res[0] if comm is None else (res[0][0], res[1])


def _ffn_tn(f):
    return _tile(f, 1536, LANES)


def _ffn_up(n, wg_t, wu_t, name, comm=None):
    def epi(accs, ex, vc):
        a, b = accs
        return [a, b, (a * _sigmoid(a)) * b]
    return _mm([[(n, wg_t)], [(n, wu_t)]], epi, [], [], [BF16, BF16, BF16], trans_rhs=True, tm=512,
               tn=_ffn_tn(wg_t.shape[0]), name=name, comm=comm)


def _ffn_gate(n, wg_t, name, comm=None):
    def epi(accs, ex, vc):
        return [accs[0]]
    return _mm([[(n, wg_t)]], epi, [], [], [BF16], trans_rhs=True, tm=512, tn=_ffn_tn(wg_t.shape[0]), name=name,
               comm=comm)


def _ffn_up_given_gate(n, wu_t, a, name, comm=None):
    def epi(accs, ex, vc):
        av = ex[0].astype(F32)
        return [accs[0], (av * _sigmoid(av)) * accs[0]]
    return _mm([[(n, wu_t)]], epi, [a], [], [BF16, BF16], trans_rhs=True, tm=512, tn=_ffn_tn(wu_t.shape[0]),
               name=name, comm=comm)


def _residual_mm(lhs, w, res, gate, coef, name, norm=None, comm=None):
    def epi(accs, ex, vc):
        h = ex[0] + (coef * vc[0]) * accs[0]
        if norm is None:
            return [h, accs[0]]
        _, xn = _rms_stats(h)
        return [h, accs[0], (xn * vc[1]) * (1.0 + vc[2]) + vc[3]]
    vecs = [gate] + (list(norm) if norm is not None else [])
    outs = [F32, BF16] + ([BF16] if norm is not None else [])
    return _mm([[(lhs, w)]], epi, [res], vecs, outs, trans_rhs=False, tm=512, tn=w.shape[1], name=name, comm=comm)


def _ffn_bwd_hidden(df, wd, a, b, name, comm=None):
    def epi(accs, ex, vc):
        dh = accs[0]
        av, bv = ex[0].astype(F32), ex[1].astype(F32)
        sig = _sigmoid(av)
        silu = av * sig
        return [dh * bv * (sig + silu * (1.0 - sig)), dh * silu]
    return _mm([[(df, wd)]], epi, [a, b], [], [BF16, BF16], trans_rhs=True, tm=512, tn=_ffn_tn(wd.shape[0]),
               name=name, comm=comm)


def _plain_mm(pairs, out_dtype, trans_rhs, tn, name, tm=512, comm=None):
    def epi(accs, ex, vc):
        return [accs[0]]
    res = _mm([pairs], epi, [], [], [out_dtype], trans_rhs=trans_rhs, tm=tm, tn=tn, name=name, comm=comm)
    return res[0] if comm is None else (res[0][0], res[1])


HEADS_PER_TILE = LANES // HEAD_DIM


def _stack_heads(x):
    lane = lax.broadcasted_iota(jnp.int32, (1, LANES), 1)
    return jnp.concatenate([x * (lane // HEAD_DIM == h).astype(F32) for h in range(HEADS_PER_TILE)], axis=0)


def _unstack_heads(y):
    r = y.shape[0] // HEADS_PER_TILE
    lane = lax.broadcasted_iota(jnp.int32, (r, y.shape[1]), 1)
    out = y[0:r]
    for h in range(1, HEADS_PER_TILE):
        out = jnp.where(lane // HEAD_DIM == h, y[h * r:(h + 1) * r], out)
    return out


def _stacked_lse(lb):
    return jnp.concatenate([_lane_pick(lb, h) for h in range(HEADS_PER_TILE)], axis=0)


def _band_masks(n_row_blocks, n_col_blocks):
    shape = (n_row_blocks * BLOCK, n_col_blocks * BLOCK)
    qi = lax.broadcasted_iota(jnp.int32, shape, 0) % BLOCK
    kj = lax.broadcasted_iota(jnp.int32, shape, 1) % BLOCK
    return kj <= qi, kj >= qi


def _query_masks():
    same_ok, before_ok = _band_masks(HEADS_PER_TILE, 2)
    is_cur = lax.broadcasted_iota(jnp.int32, same_ok.shape, 1) >= BLOCK
    return jnp.logical_and(is_cur, same_ok), jnp.logical_and(jnp.logical_not(is_cur), before_ok)


def _dot_nt(a, b):
    return lax.dot_general(a.astype(BF16), b.astype(BF16), (((1,), (1,)), ((), ())), preferred_element_type=F32)


def _dot_nn(a, b):
    return lax.dot_general(a.astype(BF16), b.astype(BF16), (((1,), (0,)), ((), ())), preferred_element_type=F32)


def _dot_tn(a, b):
    return lax.dot_general(a.astype(BF16), b.astype(BF16), (((0,), (0,)), ((), ())), preferred_element_type=F32)


def _lane_pick(x, h):
    lane = lax.broadcasted_iota(jnp.int32, x.shape, 1)
    return jnp.sum(jnp.where(lane == h * HEAD_DIM, x, 0.0), axis=1, keepdims=True)


def _block_rows(idx, d):
    span = BLOCK * d
    g = idx // d
    q0 = g * span + idx % d
    has_prev = g > 0
    p0 = jnp.where(has_prev, q0 - span, q0)
    return pl.ds(q0, BLOCK, stride=d), pl.ds(p0, BLOCK, stride=d), has_prev


def _qkv_specs(s, tiles):
    q, k, v = [pl.BlockSpec((s, LANES), functools.partial(lambda hb, off: (0, off + hb), off=i * tiles))
               for i in range(3)]
    return q, k, v, pl.BlockSpec((s, LANES), lambda hb: (0, hb))


def _attn_seq_fwd(proj, width, name, comm=None):
    s = proj.shape[0]
    q_spec, k_spec, v_spec, cur = _qkv_specs(s, width // LANES)

    def body(q_ref, k_ref, v_ref, o_ref, l_ref, o_s, l_s):
        cur_valid, prev_valid = _query_masks()
        for bi, d in enumerate(DILATIONS):
            def blk(idx, carry, bi=bi, d=d):
                rows, prev, has_prev = _block_rows(idx, d)
                q2 = _stack_heads(q_ref[rows, :])
                keys = jnp.concatenate([k_ref[prev, :], k_ref[rows, :]], axis=0)
                vals = jnp.concatenate([v_ref[prev, :], v_ref[rows, :]], axis=0)
                valid = jnp.logical_or(cur_valid, jnp.logical_and(prev_valid, has_prev))
                sc = jnp.where(valid, _dot_nt(q2, keys), NEG)
                mx = jnp.max(sc, axis=1, keepdims=True)
                p = jnp.exp(sc - mx)
                den = jnp.sum(p, axis=1, keepdims=True)
                o_s[bi, rows, :] = _unstack_heads(_dot_nn(p, vals) / den)
                l_s[bi, rows, :] = _unstack_heads(jnp.broadcast_to(mx + jnp.log(den), (q2.shape[0], LANES)))
                return carry

            lax.fori_loop(0, s // BLOCK, blk, 0, unroll=8)
        for c in range(s // MERGE_CHUNK):
            rows = slice(c * MERGE_CHUNK, (c + 1) * MERGE_CHUNK)
            ls = [l_s[bi, rows, :] for bi in range(len(DILATIONS))]
            top = functools.reduce(jnp.maximum, ls)
            ws = [jnp.exp(l - top) for l in ls]
            den = functools.reduce(lambda a, b: a + b, ws)
            num = functools.reduce(lambda a, b: a + b, [w * o_s[bi, rows, :] for bi, w in enumerate(ws)])
            o_ref[rows, :] = num / den
            l_ref[rows, :] = top + jnp.log(den)

    return _call(
        body, grid=(width // LANES,), in_specs=[q_spec, k_spec, v_spec], out_specs=[cur, cur],
        out_shape=[jax.ShapeDtypeStruct((s, width), F32)] * 2,
        scratch_shapes=[pltpu.VMEM((len(DILATIONS), s, LANES), F32)] * 2,
        args=(proj, proj, proj), name=name, comm=comm)


def _attn_seq_bwd(proj, do, o, lse, cos, sin_signed, name, comm=None):
    s, width = do.shape
    q_spec, k_spec, v_spec, cur = _qkv_specs(s, width // LANES)
    table = pl.BlockSpec((s, LANES), lambda hb: (0, 0))
    qscale = HEAD_DIM ** -0.5

    def body(q_ref, k_ref, v_ref, do_ref, o_ref, l_ref, cos_ref, sin_ref, dq_out, dk_out, dv_out,
             dq_ref, dk_ref, dv_ref):
        dq_ref[...] = jnp.zeros_like(dq_ref)
        dk_ref[...] = jnp.zeros_like(dk_ref)
        dv_ref[...] = jnp.zeros_like(dv_ref)
        cur_valid, prev_valid = _query_masks()
        for d in DILATIONS:
            def blk(idx, carry, d=d):
                rows, prev, has_prev = _block_rows(idx, d)
                dob = do_ref[rows, :]
                q2 = _stack_heads(q_ref[rows, :])
                do2 = _stack_heads(dob)
                delta = jnp.sum(_stack_heads(dob * o_ref[rows, :]), axis=1, keepdims=True)
                lse2 = _stacked_lse(l_ref[rows, :])
                keys = jnp.concatenate([k_ref[prev, :], k_ref[rows, :]], axis=0)
                vals = jnp.concatenate([v_ref[prev, :], v_ref[rows, :]], axis=0)
                valid = jnp.logical_or(cur_valid, jnp.logical_and(prev_valid, has_prev))
                p = jnp.where(valid, jnp.exp(_dot_nt(q2, keys) - lse2), 0.0)
                ds = p * (_dot_nt(do2, vals) - delta)
                dq_ref[rows, :] += _unstack_heads(_dot_nn(ds, keys))
                dkk = _dot_tn(ds, q2)
                dvv = _dot_tn(p, do2)
                dk_ref[prev, :] += dkk[0:BLOCK]
                dk_ref[rows, :] += dkk[BLOCK:]
                dv_ref[prev, :] += dvv[0:BLOCK]
                dv_ref[rows, :] += dvv[BLOCK:]
                return carry

            lax.fori_loop(0, s // BLOCK, blk, 0, unroll=4)
        for c in range(s // MERGE_CHUNK):
            rows = slice(c * MERGE_CHUNK, (c + 1) * MERGE_CHUNK)
            cos, sin = cos_ref[rows, :], sin_ref[rows, :]
            dq, dk = dq_ref[rows, :], dk_ref[rows, :]
            dq_out[rows, :] = ((dq * cos - _partner(dq) * sin) * qscale).astype(BF16)
            dk_out[rows, :] = (dk * cos - _partner(dk) * sin).astype(BF16)
            dv_out[rows, :] = dv_ref[rows, :].astype(BF16)

    return _call(
        body, grid=(width // LANES,), in_specs=[q_spec, k_spec, v_spec, cur, cur, cur, table, table],
        out_specs=[cur, cur, cur], out_shape=[jax.ShapeDtypeStruct((s, width), BF16)] * 3,
        scratch_shapes=[pltpu.VMEM((s, LANES), F32)] * 3,
        args=(proj, proj, proj, do, o, lse, cos, sin_signed), name=name, comm=comm)


def _conv_specs(s, a_block, b_block):
    per = CONV_CHUNK // CONV_HALO
    a_cur = pl.BlockSpec((CONV_CHUNK, LANES), lambda cb, i: (i, a_block + cb))
    b_cur = pl.BlockSpec((CONV_CHUNK, LANES), lambda cb, i: (i, b_block + cb))
    a_halo = pl.BlockSpec((CONV_HALO, LANES), lambda cb, i: (jnp.maximum(i * per - 1, 0), a_block + cb))
    b_halo = pl.BlockSpec((CONV_HALO, LANES), lambda cb, i: (jnp.maximum(i * per - 1, 0), b_block + cb))
    w_spec = pl.BlockSpec((CONV_KERNEL, LANES), lambda cb, i: (0, cb))
    vec = pl.BlockSpec((1, LANES), lambda cb, i: (0, cb))
    out = pl.BlockSpec((CONV_CHUNK, LANES), lambda cb, i: (i, cb))
    return a_cur, b_cur, a_halo, b_halo, w_spec, vec, out


def _fill_glu_window(win, a_ref, b_ref, ah_ref, bh_ref, first):
    halo = ah_ref[...] * _sigmoid(bh_ref[...])
    win[0:CONV_HALO, :] = jnp.where(first, 0.0, halo)
    win[CONV_HALO:, :] = a_ref[...] * _sigmoid(b_ref[...])


def _conv_fwd(proj, a_block, b_block, w, bias, name):
    s = proj.shape[0]
    cw = w.shape[1]
    a_cur, b_cur, a_halo, b_halo, w_spec, vec, out = _conv_specs(s, a_block, b_block)
    lead = CONV_HALO - (CONV_KERNEL - 1)

    def body(a_ref, b_ref, ah_ref, bh_ref, w_ref, bias_ref, o_ref, win):
        _fill_glu_window(win, a_ref, b_ref, ah_ref, bh_ref, pl.program_id(1) == 0)
        for sub in range(CONV_CHUNK // CONV_SUB):
            base = sub * CONV_SUB
            acc = jnp.zeros((CONV_SUB, LANES), F32) + bias_ref[...]
            for j in range(CONV_KERNEL):
                acc = acc + w_ref[j:j + 1, :] * win[base + lead + j:base + lead + j + CONV_SUB, :]
            o_ref[base:base + CONV_SUB, :] = acc

    return pl.pallas_call(
        body, grid=(cw // LANES, s // CONV_CHUNK), in_specs=[a_cur, b_cur, a_halo, b_halo, w_spec, vec],
        out_specs=out, out_shape=jax.ShapeDtypeStruct((s, cw), F32),
        scratch_shapes=[pltpu.VMEM((CONV_CHUNK + CONV_HALO, LANES), F32)],
        compiler_params=_params(2), name=name)(proj, proj, proj, proj, w, bias)


def _conv_bwd(proj, a_block, b_block, w, du1, name):
    s = proj.shape[0]
    cw = w.shape[1]
    a_cur, b_cur, a_halo, b_halo, w_spec, vec, out = _conv_specs(s, a_block, b_block)
    per = CONV_CHUNK // CONV_HALO
    n_chunks = s // CONV_CHUNK
    d_next = pl.BlockSpec((CONV_HALO, LANES), lambda cb, i: (jnp.minimum((i + 1) * per, s // CONV_HALO - 1), cb))
    lead = CONV_HALO - (CONV_KERNEL - 1)

    def body(a_ref, b_ref, ah_ref, bh_ref, w_ref, d_ref, dn_ref, da_ref, db_ref, dw_ref, dbias_ref, win, dwin):
        i = pl.program_id(1)
        _fill_glu_window(win, a_ref, b_ref, ah_ref, bh_ref, i == 0)
        dwin[0:CONV_CHUNK, :] = d_ref[...]
        dwin[CONV_CHUNK:, :] = jnp.where(i == n_chunks - 1, 0.0, dn_ref[...])

        @pl.when(i == 0)
        def _():
            dw_ref[...] = jnp.zeros_like(dw_ref)
            dbias_ref[...] = jnp.zeros_like(dbias_ref)

        dbias_ref[...] += _colsum(d_ref[...])
        for sub in range(CONV_CHUNK // CONV_SUB):
            base = sub * CONV_SUB
            dcur = dwin[base:base + CONV_SUB, :]
            du0 = jnp.zeros((CONV_SUB, LANES), F32)
            for j in range(CONV_KERNEL):
                back = CONV_KERNEL - 1 - j
                du0 = du0 + w_ref[j:j + 1, :] * dwin[base + back:base + back + CONV_SUB, :]
                dw_ref[j:j + 1, :] += _colsum(dcur * win[base + lead + j:base + lead + j + CONV_SUB, :])
            av = a_ref[base:base + CONV_SUB, :]
            sig = _sigmoid(b_ref[base:base + CONV_SUB, :])
            da_ref[base:base + CONV_SUB, :] = (du0 * sig).astype(BF16)
            db_ref[base:base + CONV_SUB, :] = (du0 * av * sig * (1.0 - sig)).astype(BF16)

    return pl.pallas_call(
        body, grid=(cw // LANES, n_chunks), in_specs=[a_cur, b_cur, a_halo, b_halo, w_spec, out, d_next],
        out_specs=[out, out, w_spec, vec],
        out_shape=[jax.ShapeDtypeStruct((s, cw), BF16), jax.ShapeDtypeStruct((s, cw), BF16),
                   jax.ShapeDtypeStruct((CONV_KERNEL, cw), F32), jax.ShapeDtypeStruct((1, cw), F32)],
        scratch_shapes=[pltpu.VMEM((CONV_CHUNK + CONV_HALO, LANES), F32)] * 2,
        compiler_params=_params(2), name=name)(proj, proj, proj, proj, w, du1, du1)


def _adamw_math(w, g, m, v):
    m = ADAM_B1 * m + (1.0 - ADAM_B1) * g
    v = ADAM_B2 * v + (1.0 - ADAM_B2) * (g * g)
    m_hat = m / (1.0 - ADAM_B1 ** ADAM_STEP)
    v_hat = v / (1.0 - ADAM_B2 ** ADAM_STEP)
    delta = -ADAM_LR * (m_hat / (jnp.sqrt(v_hat) + ADAM_EPS) + ADAM_WD * w)
    return delta, m, v


def _adamw_big(w, g, m, v, name):
    rows, cols = w.shape
    tile = _tile(rows, 256, 8)
    spec = pl.BlockSpec((tile, cols), lambda i: (i, 0))

    def body(w_ref, g_ref, m_ref, v_ref, d_out, m_out, v_out):
        d_out[...], m_out[...], v_out[...] = _adamw_math(w_ref[...], g_ref[...], m_ref[...], v_ref[...])

    return pl.pallas_call(body, grid=(rows // tile,), in_specs=[spec] * 4, out_specs=[spec] * 3,
                          out_shape=[jax.ShapeDtypeStruct(w.shape, F32)] * 3, compiler_params=_params(1),
                          name=name)(w, g, m, v)


def _adamw_reduced(w, land, m, v, name):
    rows, cols = w.shape
    tile = _tile(rows, 256, 16)
    spec = pl.BlockSpec((tile, cols), lambda i: (i, 0))

    def body(w_ref, l_ref, m_ref, v_ref, g_out, d_out, m_out, v_out):
        g = l_ref[0].astype(F32)
        for q in range(1, N_CHIP):
            g = g + l_ref[q].astype(F32)
        g_out[...] = g
        d_out[...], m_out[...], v_out[...] = _adamw_math(w_ref[...], g, m_ref[...], v_ref[...])

    return pl.pallas_call(body, grid=(rows // tile,),
                          in_specs=[spec, pl.BlockSpec((N_CHIP, tile, cols), lambda i: (0, i, 0)), spec, spec],
                          out_specs=[spec] * 4, out_shape=[jax.ShapeDtypeStruct(w.shape, F32)] * 4,
                          compiler_params=_params(1), name=name)(w, land, m, v)


def _adamw_small(ws, gs, ms, vs, name):
    n = len(ws)

    def body(*refs):
        ins, outs = refs[:4 * n], refs[4 * n:]
        for t in range(n):
            res = _adamw_math(ins[t][...], ins[n + t][...], ins[2 * n + t][...], ins[3 * n + t][...])
            for j in range(3):
                outs[j * n + t][...] = res[j]

    shapes = [jax.ShapeDtypeStruct(w.shape, F32) for w in ws]
    res = pl.pallas_call(body, out_shape=shapes * 3, compiler_params=pltpu.CompilerParams(vmem_limit_bytes=VMEM_LIMIT),
                         name=name)(*ws, *gs, *ms, *vs)
    return res[:n], res[n:2 * n], res[2 * n:]


def _sum_blocks(x, n_blocks, name):
    r = x.shape[0] // n_blocks

    def body(x_ref, o_ref):
        acc = x_ref[0:r, :]
        for b in range(1, n_blocks):
            acc = acc + x_ref[b * r:(b + 1) * r, :]
        o_ref[...] = acc

    return pl.pallas_call(body, out_shape=jax.ShapeDtypeStruct((r, x.shape[1]), F32),
                          compiler_params=pltpu.CompilerParams(vmem_limit_bytes=VMEM_LIMIT), name=name)(x)


def _coords():
    return lax.axis_index("x"), lax.axis_index("y"), lax.axis_index("c")


def _flip(v, bit):
    return 1 - v if bit else v


def _ag_small(x, name):
    r, c = x.shape

    def body(x_ref, o_ref, send, recv, local_sem):
        mx, my, mc = _coords()

        def rows(px, py, pc):
            return o_ref.at[pl.ds(pl.multiple_of((4 * px + 2 * py + pc) * r, 8), r), :]

        local = pltpu.make_async_copy(x_ref, rows(mx, my, mc), local_sem)
        local.start()
        peers = [(_flip(mx, k >> 2 & 1), _flip(my, k >> 1 & 1), _flip(mc, k & 1)) for k in range(1, N_DEV)]
        sends = [pltpu.make_async_remote_copy(x_ref, rows(mx, my, mc), send.at[k], recv.at[k], device_id=p,
                                              device_id_type=MESH) for k, p in enumerate(peers)]
        for cp in sends:
            cp.start()
        for k, p in enumerate(peers):
            pltpu.make_async_remote_copy(x_ref, rows(*p), send.at[k], recv.at[k], device_id=p,
                                         device_id_type=MESH).wait_recv()
        for cp in sends:
            cp.wait_send()
        local.wait()

    vm = pl.BlockSpec(memory_space=pltpu.VMEM)
    return pl.pallas_call(
        body, in_specs=[vm], out_specs=vm, out_shape=jax.ShapeDtypeStruct((N_DEV * r, c), x.dtype),
        scratch_shapes=[pltpu.SemaphoreType.DMA((N_DEV - 1,)), pltpu.SemaphoreType.DMA((N_DEV - 1,)),
                        pltpu.SemaphoreType.DMA(())],
        name=name)(x)


class _GatherSmall:
    mid = None

    def __init__(self, x):
        self.inputs = [x]
        self.out_shapes = [jax.ShapeDtypeStruct((N_DEV * x.shape[0], x.shape[1]), x.dtype)]
        self.scratch = [pltpu.SemaphoreType.DMA((N_DEV - 1,)), pltpu.SemaphoreType.DMA((N_DEV - 1,)),
                        pltpu.SemaphoreType.DMA(())]

    def _plan(self, x_refs, o_refs, sems):
        send, recv, local_sem = sems
        x_ref, o_ref = x_refs[0], o_refs[0]
        r = x_ref.shape[0]
        mx, my, mc = _coords()

        def rows(px, py, pc):
            return o_ref.at[pl.ds(pl.multiple_of((4 * px + 2 * py + pc) * r, 8), r), :]

        peers = [(_flip(mx, k >> 2 & 1), _flip(my, k >> 1 & 1), _flip(mc, k & 1)) for k in range(1, N_DEV)]
        out = [pltpu.make_async_remote_copy(x_ref, rows(mx, my, mc), send.at[k], recv.at[k], device_id=p,
                                            device_id_type=MESH) for k, p in enumerate(peers)]
        arrivals = [pltpu.make_async_remote_copy(x_ref, rows(*p), send.at[k], recv.at[k], device_id=p,
                                                 device_id_type=MESH) for k, p in enumerate(peers)]
        return out, arrivals, pltpu.make_async_copy(x_ref, rows(mx, my, mc), local_sem)

    def start(self, x_refs, o_refs, sems):
        out, _, local = self._plan(x_refs, o_refs, sems)
        local.start()
        for cp in out:
            cp.start()

    def finish(self, x_refs, o_refs, sems):
        out, arrivals, local = self._plan(x_refs, o_refs, sems)
        for cp in arrivals:
            cp.wait_recv()
        for cp in out:
            cp.wait_send()
        local.wait()


class _GatherWeights:
    def __init__(self, shards):
        n_t = len(shards)
        self.inputs = list(shards)
        self.out_shapes = [jax.ShapeDtypeStruct((N_DEV * x.shape[0], x.shape[1]), x.dtype) for x in shards]
        self.scratch = [pltpu.SemaphoreType.DMA((n_t, 7)), pltpu.SemaphoreType.DMA((n_t, 7)),
                        pltpu.SemaphoreType.DMA((n_t,))]

    def _plan(self, x_refs, o_refs, sems):
        send, recv, local_sem = sems
        mx, my, mc = _coords()
        me, sibling = (mx, my, mc), (mx, my, 1 - mc)
        chips = [(1 - mx, my), (mx, 1 - my), (1 - mx, 1 - my)]

        def rows(t, px, py, pc):
            r = x_refs[t].shape[0]
            return o_refs[t].at[pl.ds(pl.multiple_of((4 * px + 2 * py + pc) * r, 8), r), :]

        def copy(t, k, block, to, src=None):
            return pltpu.make_async_remote_copy(
                src_ref=rows(t, *block) if src is None else src, dst_ref=rows(t, *block),
                send_sem=send.at[t, k], recv_sem=recv.at[t, k], device_id=to, device_id_type=MESH)

        def local(t):
            return pltpu.make_async_copy(x_refs[t], rows(t, *me), local_sem.at[t])

        return me, sibling, chips, mc, copy, local

    def start(self, x_refs, o_refs, sems):
        me, sibling, chips, mc, copy, local = self._plan(x_refs, o_refs, sems)
        for t in range(len(x_refs)):
            local(t).start()
            copy(t, 0, me, sibling, src=x_refs[t]).start()
            for j, chip in enumerate(chips):
                copy(t, 1 + j, me, (*chip, mc), src=x_refs[t]).start()

    def mid(self, x_refs, o_refs, sems):
        me, sibling, chips, mc, copy, local = self._plan(x_refs, o_refs, sems)
        for j, chip in enumerate(chips):
            for t in range(len(x_refs)):
                copy(t, 1 + j, (*chip, mc), me).wait_recv()
                copy(t, 4 + j, (*chip, mc), sibling).start()

    def finish(self, x_refs, o_refs, sems):
        me, sibling, chips, mc, copy, local = self._plan(x_refs, o_refs, sems)
        for t in range(len(x_refs)):
            copy(t, 0, sibling, me).wait_recv()
            for j, chip in enumerate(chips):
                copy(t, 4 + j, (*chip, 1 - mc), me).wait_recv()
            copy(t, 0, me, sibling, src=x_refs[t]).wait_send()
            for j, chip in enumerate(chips):
                copy(t, 1 + j, me, (*chip, mc), src=x_refs[t]).wait_send()
                copy(t, 4 + j, (*chip, mc), sibling).wait_send()
            local(t).wait()


class _SiblingExchange:
    mid = None

    def __init__(self, grads):
        n_t = len(grads)
        self.inputs = list(grads)
        self.out_shapes = [jax.ShapeDtypeStruct((N_CHIP,) + g.shape[2:], F32) for g in grads]
        self.scratch = [pltpu.SemaphoreType.DMA((n_t,)), pltpu.SemaphoreType.DMA((n_t,))]

    def _copies(self, g_refs, land, sems):
        send, recv = sems
        mx, my, mc = _coords()
        return [pltpu.make_async_remote_copy(g_refs[t].at[:, 1 - mc], land[t], send.at[t], recv.at[t],
                                             device_id=(mx, my, 1 - mc), device_id_type=MESH)
                for t in range(len(g_refs))]

    def start(self, g_refs, land, sems):
        for cp in self._copies(g_refs, land, sems):
            cp.start()

    def finish(self, g_refs, land, sems):
        for cp in self._copies(g_refs, land, sems):
            cp.wait()


class _Together:
    def __init__(self, *comms):
        self.comms = comms
        self.inputs = [x for c in comms for x in c.inputs]
        self.out_shapes = [x for c in comms for x in c.out_shapes]
        self.scratch = [x for c in comms for x in c.scratch]
        self.mid = self._mid if any(c.mid is not None for c in comms) else None

    def _each(self, phase, cin, cout, sems):
        i = o = s = 0
        for c in self.comms:
            fn = getattr(c, phase)
            ni, no, ns = len(c.inputs), len(c.out_shapes), len(c.scratch)
            if fn is not None:
                fn(cin[i:i + ni], cout[o:o + no], sems[s:s + ns])
            i, o, s = i + ni, o + no, s + ns

    def start(self, cin, cout, sems):
        self._each("start", cin, cout, sems)

    def _mid(self, cin, cout, sems):
        self._each("mid", cin, cout, sems)

    def finish(self, cin, cout, sems):
        self._each("finish", cin, cout, sems)


def _standalone(comm, name):
    def body():
        pass
    return _call(body, grid=(1,), in_specs=[], out_specs=[], out_shape=[], args=(), name=name, comm=comm)[1]


def _chip_partials(g4s, lands, name):
    n_t = len(g4s)
    in_specs, out_specs, out_shape = [], [], []
    for g4 in g4s:
        _, _, r, c = g4.shape
        in_specs.append(pl.BlockSpec((None, None, r, c), lambda q: (q, lax.axis_index("c"), 0, 0)))
        out_specs.append(pl.BlockSpec((None, r, c), lambda q: (q, 0, 0)))
        out_shape.append(jax.ShapeDtypeStruct((N_CHIP, r, c), BF16))
    in_specs += [pl.BlockSpec((None,) + g4.shape[2:], lambda q: (q, 0, 0)) for g4 in g4s]

    def body(*refs):
        for t in range(n_t):
            refs[2 * n_t + t][...] = (refs[t][...] + refs[n_t + t][...]).astype(BF16)

    return pl.pallas_call(body, grid=(N_CHIP,), in_specs=in_specs, out_specs=out_specs, out_shape=out_shape,
                          compiler_params=_params(1), name=name)(*g4s, *lands)


class _ChipExchange:
    mid = None

    def __init__(self, parts):
        n_t = len(parts)
        self.inputs = list(parts)
        self.out_shapes = [jax.ShapeDtypeStruct(p.shape, p.dtype) for p in parts]
        self.scratch = [pltpu.SemaphoreType.DMA((n_t, 3)), pltpu.SemaphoreType.DMA((n_t, 3)),
                        pltpu.SemaphoreType.DMA((n_t,))]

    def _plan(self, p_refs, land, sems):
        send, recv, local_sem = sems
        mx, my, mc = _coords()
        my_chip = 2 * mx + my
        peers = [(_flip(mx, fx), _flip(my, fy)) for fx, fy in ((1, 0), (0, 1), (1, 1))]

        def out(t, k):
            px, py = peers[k]
            return pltpu.make_async_remote_copy(p_refs[t].at[2 * px + py], land[t].at[my_chip], send.at[t, k],
                                                recv.at[t, k], device_id=(px, py, mc), device_id_type=MESH)

        def arrival(t, k):
            px, py = peers[k]
            return pltpu.make_async_remote_copy(p_refs[t].at[my_chip], land[t].at[2 * px + py], send.at[t, k],
                                                recv.at[t, k], device_id=(px, py, mc), device_id_type=MESH)

        def local(t):
            return pltpu.make_async_copy(p_refs[t].at[my_chip], land[t].at[my_chip], local_sem.at[t])

        return out, arrival, local

    def start(self, p_refs, land, sems):
        out, arrival, local = self._plan(p_refs, land, sems)
        for t in range(len(p_refs)):
            local(t).start()
            for k in range(3):
                out(t, k).start()

    def finish(self, p_refs, land, sems):
        out, arrival, local = self._plan(p_refs, land, sems)
        for t in range(len(p_refs)):
            for k in range(3):
                arrival(t, k).wait_recv()
                out(t, k).wait_send()
            local(t).wait()


def _rope_tables(s, width):
    heads = width // HEAD_DIM
    inv_freq = ROPE_THETA ** (-jnp.arange(0, HEAD_DIM, 2, dtype=F32) / HEAD_DIM)
    inv_full = jnp.tile(inv_freq, 2 * heads)
    sign = jnp.tile(jnp.concatenate([-jnp.ones((HALF_HEAD,), F32), jnp.ones((HALF_HEAD,), F32)]), heads)
    ang = jnp.arange(s, dtype=F32)[:, None] * inv_full[None, :]
    return jnp.cos(ang), jnp.sin(ang) * sign[None, :]


def _pad_rows(v, rows):
    return jnp.concatenate([v, jnp.zeros((rows - 1, v.shape[1]), v.dtype)], axis=0)


def kernel(x, c, w_ada, b_ada, ffn1_norm_g, ffn1_w_gate, ffn1_w_up, ffn1_w_down, mix_norm_g, w_in, conv_dw_w, conv_dw_b, conv_ln_g, conv_ln_b, attn_out_g, conv_out_g, w_out, ffn2_norm_g, ffn2_w_gate, ffn2_w_up, ffn2_w_down, final_norm_g, loss_target, m_w_ada, m_b_ada, m_ffn1_norm_g, m_ffn1_w_gate, m_ffn1_w_up, m_ffn1_w_down, m_mix_norm_g, m_w_in, m_conv_dw_w, m_conv_dw_b, m_conv_ln_g, m_conv_ln_b, m_attn_out_g, m_conv_out_g, m_w_out, m_ffn2_norm_g, m_ffn2_w_gate, m_ffn2_w_up, m_ffn2_w_down, m_final_norm_g, v_w_ada, v_b_ada, v_ffn1_norm_g, v_ffn1_w_gate, v_ffn1_w_up, v_ffn1_w_down, v_mix_norm_g, v_w_in, v_conv_dw_w, v_conv_dw_b, v_conv_ln_g, v_conv_ln_b, v_attn_out_g, v_conv_out_g, v_w_out, v_ffn2_norm_g, v_ffn2_w_gate, v_ffn2_w_up, v_ffn2_w_down, v_final_norm_g):
    mx, my, mc = _coords()
    me = 4 * mx + 2 * my + mc
    s, d = x.shape[1], x.shape[2]
    aw = d // 2
    x2, target = x[0], loss_target[0]
    n_mod = w_ada.shape[2] * N_DEV // d
    mod_cols = w_ada.shape[2]

    def shard(w, transpose):
        return (w[0].T if transpose else w[0]).astype(BF16)

    cw_shard = conv_dw_w.shape[3]
    n_taps = CONV_KERNEL * cw_shard
    first_len = -(-(d + n_taps) // LANES) * LANES
    first = jnp.concatenate([c, conv_dw_w[0, :, 0, :].reshape(1, n_taps), jnp.zeros((1, first_len - d - n_taps), F32)], axis=1)
    first_all, wg1 = _standalone(
        _Together(_GatherSmall(_pad_rows(first, 8)), _GatherWeights([shard(ffn1_w_gate, True)])), "ag_first")
    first_all = first_all[0::8]
    c_all = first_all[:, :d]
    conv_w = first_all[:, d:d + n_taps].reshape(N_DEV, CONV_KERNEL, cw_shard).transpose(1, 0, 2).reshape(CONV_KERNEL, aw)

    silu_c = _silu_rows(c_all, "silu_c")
    mod_part = _plain_mm([(silu_c, w_ada[0])], F32, False, mod_cols, "mod_mm")
    mod_all = _ag_small(mod_part, "ag_mod").reshape(N_DEV, N_DEV, mod_cols)
    mod = lax.dynamic_index_in_dim(mod_all, me, axis=1, keepdims=False).reshape(1, n_mod * d) + b_ada
    sh1, sc1, g1, sh2, sc2, g2, sh3, sc3, g3 = [mod[:, i * d:(i + 1) * d] for i in range(n_mod)]

    def split(g):
        return g.reshape(N_CHIP, 2, g.shape[0] // N_DEV, g.shape[1])

    def partials(g4s, lands, tag):
        return _chip_partials(g4s, lands, "chip_partials_" + tag)

    gather_late = _GatherWeights([shard(ffn2_w_gate, True), shard(ffn2_w_up, True), shard(ffn2_w_down, False),
                                  shard(w_out, False)])

    n1 = _norm_mod_fwd(x2, ffn1_norm_g, sc1, sh1, "norm1")
    (a1,), (wu1,) = _ffn_gate(n1, wg1, "ffn1_gate", comm=_GatherWeights([shard(ffn1_w_up, True)]))
    (b1, hid1), (wd1,) = _ffn_up_given_gate(n1, wu1, a1, "ffn1_up", comm=_GatherWeights([shard(ffn1_w_down, False)]))
    (h1, f1, n2), (win_t,) = _residual_mm(hid1, wd1, x2, g1, 0.5, "ffn1_down", norm=(mix_norm_g, sc2, sh2),
                                          comm=_GatherWeights([shard(w_in, True)]))
    cos, sin_signed = _rope_tables(s, LANES)
    proj = _proj_rope(n2, win_t, cos, sin_signed, aw, "proj")
    lanes_per = aw // LANES
    (attn, lse), (wg2, wu2, wd2, wout) = _attn_seq_fwd(proj, aw, "attn_fwd", comm=gather_late)
    u1 = _conv_fwd(proj, 3 * lanes_per, 4 * lanes_per, conv_w, conv_dw_b, "conv_fwd")
    y = _mix_post_fwd(attn, u1, attn_out_g, conv_ln_g, conv_ln_b, conv_out_g, "mix_post")
    h2, mix, n3 = _residual_mm(y, wout, h1, g2, 1.0, "mix_out", norm=(ffn2_norm_g, sc3, sh3))
    a3, b3, hid3 = _ffn_up(n3, wg2, wu2, "ffn2_up")

    dh3, df3, err2, d_final_g, dg3 = _last_mm_loss(hid3, wd2, h2, g3, 0.5, target, final_norm_g.reshape(1, d),
                                                   "ffn2_down_loss")
    loss = lax.psum(0.5 * jnp.sum(err2) / d, ("x", "y", "c"))

    da3, db3 = _ffn_bwd_hidden(df3, wd2, a3, b3, "ffn2_hidden_bwd")
    g4_a = [split(_mm_tn(da3, n3, "ffn2_dwg")), split(_mm_tn(db3, n3, "ffn2_dwu")), split(_mm_tn(hid3, df3, "ffn2_dwd"))]
    dn3, land_a = _plain_mm([(da3, wg2), (db3, wu2)], BF16, False, d, "ffn2_dn", comm=_SiblingExchange(g4_a))
    parts_a = partials(g4_a, land_a, "a")
    dh2, dmix, dsh3, dsc3, dgn3, dg2 = _norm_mod_bwd(dn3, h2, dh3, ffn2_norm_g, sc3, "norm3_bwd",
                                                     branch=(mix, g2, 1.0))
    dy = _plain_mm([(dmix, wout)], BF16, True, d, "mix_dy")
    g_wout = _mm_tn(y, dmix, "mix_dwout")
    dattn, du1, d_attn_g, d_conv_g, d_ln_g, d_ln_b = _mix_post_bwd(
        dy, attn, u1, attn_out_g, conv_ln_g, conv_ln_b, conv_out_g, "mix_post_bwd")
    dga, dgb, d_taps, d_conv_b = _conv_bwd(proj, 3 * lanes_per, 4 * lanes_per, conv_w, du1, "conv_bwd")
    (dq, dk, dv), sums_a = _attn_seq_bwd(proj, dattn, attn, lse, cos, sin_signed, "attn_bwd",
                                         comm=_ChipExchange(parts_a))
    dproj = jnp.concatenate([dq, dk, dv, dga, dgb], axis=1)
    dn2 = _plain_mm([(dproj, win_t)], BF16, False, d, "mix_dn")
    g4_b = [split(g_wout), split(_mm_tn(dproj, n2, "mix_dwin"))]
    (dh1, df1, dsh2, dsc2, dgn2, dg1), land_b = _norm_mod_bwd(dn2, h1, dh2, mix_norm_g, sc2, "norm2_bwd",
                                                              branch=(f1, g1, 0.5), comm=_SiblingExchange(g4_b))
    parts_b = partials(g4_b, land_b, "b")
    g4_c = [split(_mm_tn(hid1, df1, "ffn1_dwd"))]
    (da1, db1), both = _ffn_bwd_hidden(df1, wd1, a1, b1, "ffn1_hidden_bwd",
                                       comm=_Together(_ChipExchange(parts_b), _SiblingExchange(g4_c)))
    sums_b, land_c = both[:2], both[2:]
    parts_c = partials(g4_c, land_c, "c")
    g_wu1, sums_c = _mm_tn(db1, n1, "ffn1_dwu", comm=_ChipExchange(parts_c))
    g4_d = [split(g_wu1)]
    g_wg1, land_d = _mm_tn(da1, n1, "ffn1_dwg", comm=_SiblingExchange(g4_d))
    parts_d = partials(g4_d, land_d, "d")
    g4_e = [split(g_wg1)]
    dn1, both = _plain_mm([(da1, wg1), (db1, wu1)], BF16, False, d, "ffn1_dn",
                          comm=_Together(_ChipExchange(parts_d), _SiblingExchange(g4_e)))
    sums_d, land_e = both[:1], both[1:]
    parts_e = partials(g4_e, land_e, "e")
    (dx, dsh1, dsc1, dgn1), sums_e = _norm_mod_bwd(dn1, x2, dh1, ffn1_norm_g, sc1, "norm1_bwd",
                                                   comm=_ChipExchange(parts_e))

    dmod = jnp.concatenate([dsh1, dsc1, dg1, dsh2, dsc2, dg2, dsh3, dsc3, dg3], axis=1)
    small = [dmod, dgn1, dgn2, dgn3, d_final_g, d_conv_b, d_ln_g, d_ln_b, d_attn_g, d_conv_g,
             d_taps.reshape(1, CONV_KERNEL * aw)]
    sizes = [v.shape[1] for v in small]
    total = sum(sizes)
    padded = -(-total // (8 * LANES)) * (8 * LANES)
    packed = jnp.concatenate(small + [jnp.zeros((1, padded - total), F32)], axis=1).reshape(8, padded // 8)
    gathered = _ag_small(packed, "ag_small_grads")
    summed = _sum_blocks(gathered, N_DEV, "sum_small_grads").reshape(1, padded)
    offs = [sum(sizes[:i]) for i in range(len(sizes))]
    (g_b_ada, g_gn1, g_gn2, g_gn3, g_final, g_conv_b, g_ln_g, g_ln_b, g_attn_g, g_conv_g, g_taps) = [
        summed[:, o:o + n] for o, n in zip(offs, sizes)]
    g_taps_shard = lax.dynamic_slice_in_dim(g_taps.reshape(CONV_KERNEL, aw), me * cw_shard, cw_shard, axis=1)
    dmod_all = gathered.reshape(N_DEV, padded)[:, :n_mod * d]
    dmod_cols = lax.dynamic_slice_in_dim(dmod_all, me * mod_cols, mod_cols, axis=1)
    g_w_ada = _mm_tn(silu_c, dmod_cols, "ada_dw")

    arrived = dict(zip(["ffn2_w_gate", "ffn2_w_up", "ffn2_w_down", "w_out", "w_in", "ffn1_w_down", "ffn1_w_up",
                        "ffn1_w_gate"], list(sums_a) + list(sums_b) + list(sums_c) + list(sums_d) + list(sums_e)))
    transposed = ("ffn1_w_gate", "ffn1_w_up", "w_in", "ffn2_w_gate", "ffn2_w_up")
    grads = {
        "w_ada": g_w_ada, "b_ada": g_b_ada, "ffn1_norm_g": g_gn1, "mix_norm_g": g_gn2, "conv_dw_w": g_taps_shard,
        "conv_dw_b": g_conv_b, "conv_ln_g": g_ln_g, "conv_ln_b": g_ln_b, "attn_out_g": g_attn_g,
        "conv_out_g": g_conv_g, "ffn2_norm_g": g_gn3, "final_norm_g": g_final,
    }
    weights = dict(w_ada=w_ada, b_ada=b_ada, ffn1_norm_g=ffn1_norm_g, ffn1_w_gate=ffn1_w_gate, ffn1_w_up=ffn1_w_up, ffn1_w_down=ffn1_w_down, mix_norm_g=mix_norm_g, w_in=w_in, conv_dw_w=conv_dw_w, conv_dw_b=conv_dw_b, conv_ln_g=conv_ln_g, conv_ln_b=conv_ln_b, attn_out_g=attn_out_g, conv_out_g=conv_out_g, w_out=w_out, ffn2_norm_g=ffn2_norm_g, ffn2_w_gate=ffn2_w_gate, ffn2_w_up=ffn2_w_up, ffn2_w_down=ffn2_w_down, final_norm_g=final_norm_g)
    moms = dict(w_ada=m_w_ada, b_ada=m_b_ada, ffn1_norm_g=m_ffn1_norm_g, ffn1_w_gate=m_ffn1_w_gate, ffn1_w_up=m_ffn1_w_up, ffn1_w_down=m_ffn1_w_down, mix_norm_g=m_mix_norm_g, w_in=m_w_in, conv_dw_w=m_conv_dw_w, conv_dw_b=m_conv_dw_b, conv_ln_g=m_conv_ln_g, conv_ln_b=m_conv_ln_b, attn_out_g=m_attn_out_g, conv_out_g=m_conv_out_g, w_out=m_w_out, ffn2_norm_g=m_ffn2_norm_g, ffn2_w_gate=m_ffn2_w_gate, ffn2_w_up=m_ffn2_w_up, ffn2_w_down=m_ffn2_w_down, final_norm_g=m_final_norm_g)
    vars_ = dict(w_ada=v_w_ada, b_ada=v_b_ada, ffn1_norm_g=v_ffn1_norm_g, ffn1_w_gate=v_ffn1_w_gate, ffn1_w_up=v_ffn1_w_up, ffn1_w_down=v_ffn1_w_down, mix_norm_g=v_mix_norm_g, w_in=v_w_in, conv_dw_w=v_conv_dw_w, conv_dw_b=v_conv_dw_b, conv_ln_g=v_conv_ln_g, conv_ln_b=v_conv_ln_b, attn_out_g=v_attn_out_g, conv_out_g=v_conv_out_g, w_out=v_w_out, ffn2_norm_g=v_ffn2_norm_g, ffn2_w_gate=v_ffn2_w_gate, ffn2_w_up=v_ffn2_w_up, ffn2_w_down=v_ffn2_w_down, final_norm_g=v_final_norm_g)
    names = list(weights)
    big = ["w_ada", "ffn1_w_gate", "ffn1_w_up", "ffn1_w_down", "w_in", "w_out", "ffn2_w_gate", "ffn2_w_up",
           "ffn2_w_down"]
    shape2 = {n: (weights[n].shape[-2] if weights[n].ndim > 1 else 1, weights[n].shape[-1]) for n in names}
    shape2["conv_dw_w"] = (CONV_KERNEL, cw_shard)
    g_out, d_out, m_out, v_out = {}, {}, {}, {}
    for n in big:
        if n in arrived:
            def view(t, n=n):
                return t[0].T if n in transposed else t[0]
            res = _adamw_reduced(view(weights[n]), arrived[n], view(moms[n]), view(vars_[n]), "adamw_" + n)
            g_out[n], d_out[n], m_out[n], v_out[n] = [r.T if n in transposed else r for r in res]
        else:
            g2d = grads[n].reshape(shape2[n])
            res = _adamw_big(weights[n].reshape(shape2[n]), g2d, moms[n].reshape(shape2[n]),
                             vars_[n].reshape(shape2[n]), "adamw_" + n)
            g_out[n], (d_out[n], m_out[n], v_out[n]) = g2d, res
    rest = [n for n in names if n not in big]
    res = _adamw_small([weights[n].reshape(shape2[n]) for n in rest], [grads[n].reshape(shape2[n]) for n in rest],
                       [moms[n].reshape(shape2[n]) for n in rest], [vars_[n].reshape(shape2[n]) for n in rest],
                       "adamw_small")
    for i, n in enumerate(rest):
        g_out[n], d_out[n], m_out[n], v_out[n] = grads[n], res[0][i], res[1][i], res[2][i]

    def shaped(table):
        return [table[n].reshape(weights[n].shape) for n in names]

    return (loss, dx.reshape(x.shape), *shaped(g_out), *shaped(d_out), *shaped(m_out), *shaped(v_out))
```

```python
import functools

import jax
import jax.numpy as jnp
from jax import lax
from jax.experimental import pallas as pl
from jax.experimental.pallas import tpu as pltpu

F32 = jnp.float32
BF16 = jnp.bfloat16
MESH = pl.DeviceIdType.MESH
ANY = pl.BlockSpec(memory_space=pl.ANY)

N_DEV = 8
N_CHIP = 4
HEAD_DIM = 64
HALF_HEAD = HEAD_DIM // 2
LANES = 128
BLOCK = 128
DILATIONS = (1, 4, 16)
MERGE_CHUNK = 512
ROPE_THETA = 10000.0
CONV_KERNEL = 31
CONV_HALO = 32
CONV_CHUNK = 512
CONV_SUB = 128
RMS_EPS = 1e-6
LN_EPS = 1e-5
ADAM_LR = 0.001
ADAM_B1 = 0.9
ADAM_B2 = 0.999
ADAM_EPS = 1e-08
ADAM_WD = 0.01
ADAM_STEP = 10
VMEM_LIMIT = 56 * 1024 * 1024
NEG = -1e30


def _params(n_axes):
    return pltpu.CompilerParams(dimension_semantics=("arbitrary",) * n_axes, vmem_limit_bytes=VMEM_LIMIT)


def _tile(n, target, unit):
    best = None
    for t in range(unit, min(n, target) + 1, unit):
        if n % t == 0:
            best = t
    return best if best is not None else n


def _sigmoid(x):
    return 0.5 * (jnp.tanh(0.5 * x) + 1.0)


def _call(body, *, grid, in_specs, out_specs, out_shape, args, name, scratch_shapes=(), comm=None):
    params = _params(len(grid))
    if comm is None:
        return pl.pallas_call(body, grid=grid, in_specs=list(in_specs), out_specs=list(out_specs),
                              out_shape=list(out_shape), scratch_shapes=list(scratch_shapes),
                              compiler_params=params, name=name)(*args)
    n_in, n_out, n_scr = len(args), len(out_shape), len(scratch_shapes)
    c_in, c_out = len(comm.inputs), len(comm.out_shapes)
    steps = 1
    for g in grid:
        steps *= g

    def hosted(*refs):
        pos = 0
        parts = []
        for size in (n_in, c_in, n_out, c_out, n_scr, len(comm.scratch)):
            parts.append(refs[pos:pos + size])
            pos += size
        ins, cin, outs, cout, scr, cscr = parts
        step = 0
        for axis, g in enumerate(grid):
            step = step * g + pl.program_id(axis)

        @pl.when(step == 0)
        def _():
            comm.start(cin, cout, cscr)

        body(*ins, *outs, *scr)
        if comm.mid is not None and steps >= 4:
            @pl.when(step == (3 * steps) // 4)
            def _():
                comm.mid(cin, cout, cscr)

        @pl.when(step == steps - 1)
        def _():
            if comm.mid is not None and steps < 4:
                comm.mid(cin, cout, cscr)
            comm.finish(cin, cout, cscr)

    res = pl.pallas_call(
        hosted, grid=grid, in_specs=list(in_specs) + [ANY] * c_in, out_specs=list(out_specs) + [ANY] * c_out,
        out_shape=list(out_shape) + list(comm.out_shapes), scratch_shapes=list(scratch_shapes) + list(comm.scratch),
        compiler_params=params, name=name)(*args, *comm.inputs)
    return res[:n_out], res[n_out:]


def _rows(fn, rows_in, vecs_in, rows_out, vecs_out, *, tile, name, comm=None):
    norm = [r if isinstance(r, tuple) else (r, r.shape[1], 0) for r in rows_in]
    n_rows = norm[0][0].shape[0]
    n_tiles = n_rows // tile
    in_specs, args = [], []
    for arr, width, cb in norm:
        in_specs.append(pl.BlockSpec((tile, width), functools.partial(lambda i, cb: (i, cb), cb=cb)))
        args.append(arr)
    for v in vecs_in:
        in_specs.append(pl.BlockSpec((1, v.shape[1]), lambda i: (0, 0)))
        args.append(v)
    out_shape = [jax.ShapeDtypeStruct((n_rows, w), dt) for w, dt in rows_out]
    out_shape += [jax.ShapeDtypeStruct((1, w), F32) for w in vecs_out]
    out_specs = [pl.BlockSpec((tile, w), lambda i: (i, 0)) for w, _ in rows_out]
    out_specs += [pl.BlockSpec((1, w), lambda i: (0, 0)) for w in vecs_out]
    n_in, n_ro = len(args), len(rows_out)

    def body(*refs):
        vals = [r[...] for r in refs[:n_in]]
        outs = refs[n_in:]
        row_vals, vec_vals = fn(*vals)
        for ref, val in zip(outs[:n_ro], row_vals):
            if isinstance(val, tuple):
                w = val[0].shape[1]
                for j, piece in enumerate(val):
                    ref[:, j * w:(j + 1) * w] = piece.astype(ref.dtype)
            else:
                ref[...] = val.astype(ref.dtype)
        if vecs_out:
            @pl.when(pl.program_id(0) == 0)
            def _():
                for ref in outs[n_ro:]:
                    ref[...] = jnp.zeros_like(ref)
            for ref, val in zip(outs[n_ro:], vec_vals):
                ref[...] += val

    return _call(body, grid=(n_tiles,), in_specs=in_specs, out_specs=out_specs, out_shape=out_shape, args=args,
                 name=name, comm=comm)


def _colsum(x):
    return jnp.sum(x, axis=0, keepdims=True)


def _rms_stats(h):
    r = lax.rsqrt(jnp.mean(h * h, axis=-1, keepdims=True) + RMS_EPS)
    return r, h * r


def _rms_back(r, xn, dxn):
    return r * (dxn - xn * jnp.mean(dxn * xn, axis=-1, keepdims=True))


def _norm_mod_fwd(h, gain, scale, shift, name):
    def fn(h, gain, scale, shift):
        _, xn = _rms_stats(h)
        return [(xn * gain) * (1.0 + scale) + shift], []
    return _rows(fn, [h], [gain, scale, shift], [(h.shape[1], BF16)], [], tile=512, name=name)[0]


def _branch_back(dh, f, gate, coef):
    return (coef * gate) * dh, coef * _colsum(f.astype(F32) * dh)


def _norm_mod_back(dn, h, dh_in, gain, scale):
    dn = dn.astype(F32)
    r, xn = _rms_stats(h)
    y = xn * gain
    dy = dn * (1.0 + scale)
    dh = dh_in + _rms_back(r, xn, dy * gain)
    return dh, [_colsum(dn), _colsum(dn * y), _colsum(dy * xn)]


def _norm_mod_bwd(dn, h, dh_in, gain, scale, name, comm=None):
    d = h.shape[1]

    def fn(dn, h, dh_in, gain, scale):
        dh, vecs = _norm_mod_back(dn, h, dh_in, gain, scale)
        return [dh], vecs
    return _rows(fn, [dn, h, dh_in], [gain, scale], [(d, F32)], [d, d, d], tile=256, name=name, comm=comm)


def _mm_norm_mod_bwd(pairs, h, dh_in, gain, scale, branch, name, tm, comm=None):
    f, gate, coef = branch

    def epi(accs, ex, vc):
        dh, vecs = _norm_mod_back(accs[0], ex[0], ex[1], vc[0], vc[1])
        df, dgate = _branch_back(dh, ex[2], vc[2], coef)
        return [dh, df] + vecs + [dgate]
    return _mm([pairs], epi, [h, dh_in, f], [gain, scale, gate], [F32, BF16], trans_rhs=False, tm=tm,
               tn=h.shape[1], name=name, n_sums=4, comm=comm)


def _last_mm_loss(lhs, w, res, gate, coef, target, gain, name):
    d = w.shape[1]

    def epi(accs, ex, vc):
        f = accs[0]
        h = ex[0] + (coef * vc[0]) * f
        r, xn = _rms_stats(h)
        err = xn * vc[1] - ex[1]
        dout = err * (1.0 / d)
        dh = _rms_back(r, xn, dout * vc[1])
        df, dgate = _branch_back(dh, f, vc[0], coef)
        return [dh, df, _colsum(err * err), _colsum(dout * xn), dgate]
    return _mm([[(lhs, w)]], epi, [res, target], [gate, gain], [F32, BF16], trans_rhs=False, tm=256, tn=d,
               name=name, n_sums=3)


def _partner(x):
    if x.shape[1] > LANES:
        return jnp.concatenate([_partner(x[:, c:c + LANES]) for c in range(0, x.shape[1], LANES)], axis=1)
    lane = lax.broadcasted_iota(jnp.int32, x.shape, 1) % HEAD_DIM
    return jnp.where(lane < HALF_HEAD, pltpu.roll(x, LANES - HALF_HEAD, 1), pltpu.roll(x, HALF_HEAD, 1))


def _proj_rope(n, w_t, cos, sin_signed, width, name):
    s, kdim = n.shape
    n_cols = w_t.shape[0]
    tm = _tile(s, 1024, 8)
    qscale = HEAD_DIM ** -0.5

    chunk = _tile(tm, 256, 8)

    def body(n_ref, w_ref, cos_ref, sin_ref, o_ref):
        j = pl.program_id(0)

        def products(rows):
            return lax.dot_general(n_ref[rows, :].astype(BF16), w_ref[...].astype(BF16), (((1,), (1,)), ((), ())),
                                   preferred_element_type=F32)

        @pl.when(j >= 2)
        def _():
            for c in range(tm // chunk):
                rows = slice(c * chunk, (c + 1) * chunk)
                o_ref[rows, :] = products(rows)

        @pl.when(j < 2)
        def _():
            scale = jnp.where(j == 0, qscale, 1.0)
            for c in range(tm // chunk):
                rows = slice(c * chunk, (c + 1) * chunk)
                acc = products(rows)
                cos = jnp.tile(cos_ref[rows, :], (1, width // LANES))
                sin = jnp.tile(sin_ref[rows, :], (1, width // LANES))
                o_ref[rows, :] = scale * (acc * cos + _partner(acc) * sin)

    table = pl.BlockSpec((tm, LANES), lambda j, i: (jnp.where(j < 2, i, 0), 0))
    return pl.pallas_call(
        body, grid=(n_cols // width, s // tm),
        in_specs=[pl.BlockSpec((tm, kdim), lambda j, i: (i, 0)), pl.BlockSpec((width, kdim), lambda j, i: (j, 0)),
                  table, table],
        out_specs=pl.BlockSpec((tm, width), lambda j, i: (i, j)), out_shape=jax.ShapeDtypeStruct((s, n_cols), F32),
        compiler_params=_params(2), name=name)(n, w_t, cos, sin_signed)


def _mix_post_fwd(attn, u1, attn_g, ln_g, ln_b, conv_g, name):
    def fn(attn, u1, attn_g, ln_g, ln_b, conv_g):
        _, xa = _rms_stats(attn)
        mu = jnp.mean(u1, axis=-1, keepdims=True)
        xc = u1 - mu
        rstd = lax.rsqrt(jnp.mean(xc * xc, axis=-1, keepdims=True) + LN_EPS)
        u2 = (xc * rstd) * ln_g + ln_b
        u3 = u2 * _sigmoid(u2)
        _, x3 = _rms_stats(u3)
        return [(xa * attn_g, x3 * conv_g)], []
    w = attn.shape[1]
    return _rows(fn, [attn, u1], [attn_g, ln_g, ln_b, conv_g], [(2 * w, BF16)], [], tile=512, name=name)[0]


def _mix_post_bwd(dy, attn, u1, attn_g, ln_g, ln_b, conv_g, name):
    w = attn.shape[1]

    def fn(dya, dyc, attn, u1, attn_g, ln_g, ln_b, conv_g):
        dya, dyc = dya.astype(F32), dyc.astype(F32)
        ra, xa = _rms_stats(attn)
        dattn = _rms_back(ra, xa, dya * attn_g)
        mu = jnp.mean(u1, axis=-1, keepdims=True)
        xc = u1 - mu
        rstd = lax.rsqrt(jnp.mean(xc * xc, axis=-1, keepdims=True) + LN_EPS)
        xh = xc * rstd
        u2 = xh * ln_g + ln_b
        sig = _sigmoid(u2)
        u3 = u2 * sig
        r3, x3 = _rms_stats(u3)
        du3 = _rms_back(r3, x3, dyc * conv_g)
        du2 = du3 * (sig + u3 * (1.0 - sig))
        dxh = du2 * ln_g
        du1 = rstd * (dxh - jnp.mean(dxh, axis=-1, keepdims=True) - xh * jnp.mean(dxh * xh, axis=-1, keepdims=True))
        return [dattn, du1], [_colsum(dya * xa), _colsum(dyc * x3), _colsum(du2 * xh), _colsum(du2)]
    return _rows(fn, [(dy, w, 0), (dy, w, 1), attn, u1], [attn_g, ln_g, ln_b, conv_g], [(w, F32), (w, F32)],
                 [w, w, w, w], tile=256, name=name)


def _silu_rows(c_all, name):
    def fn(c):
        return [c * _sigmoid(c)], []
    return _rows(fn, [c_all], [], [(c_all.shape[1], BF16)], [], tile=c_all.shape[0], name=name)[0]


def _mm(groups, epi, extras, vecs, outs, *, trans_rhs, tm, tn, name, n_sums=0, comm=None):
    m = groups[0][0][0].shape[0]
    n = groups[0][0][1].shape[0] if trans_rhs else groups[0][0][1].shape[1]
    tm, tn = min(tm, m), min(tn, n)
    in_specs, args = [], []
    for grp in groups:
        for lhs, rhs in grp:
            k = lhs.shape[1]
            in_specs.append(pl.BlockSpec((tm, k), lambda j, i: (i, 0)))
            in_specs.append(pl.BlockSpec((tn, k), lambda j, i: (j, 0)) if trans_rhs
                            else pl.BlockSpec((k, tn), lambda j, i: (0, j)))
            args += [lhs, rhs]
    for e in extras:
        in_specs.append(pl.BlockSpec((tm, tn), lambda j, i: (i, j)))
        args.append(e)
    for v in vecs:
        in_specs.append(pl.BlockSpec((1, tn), lambda j, i: (0, j)))
        args.append(v)
    sizes = [len(g) for g in groups]
    n_mm, n_ex, n_vec = 2 * sum(sizes), len(extras), len(vecs)
    dims = (((1,), (1,)), ((), ())) if trans_rhs else (((1,), (0,)), ((), ()))

    def body(*refs):
        accs, pos = [], 0
        for size in sizes:
            acc = None
            for _ in range(size):
                part = lax.dot_general(refs[pos][...].astype(BF16), refs[pos + 1][...].astype(BF16), dims,
                                       preferred_element_type=F32)
                acc = part if acc is None else acc + part
                pos += 2
            accs.append(acc)
        ex = [r[...] for r in refs[n_mm:n_mm + n_ex]]
        vc = [r[...] for r in refs[n_mm + n_ex:n_mm + n_ex + n_vec]]
        out_refs = refs[n_mm + n_ex + n_vec:]
        vals = epi(accs, ex, vc)
        for ref, val in zip(out_refs[:len(outs)], vals):
            ref[...] = val.astype(ref.dtype)
        if n_sums:
            @pl.when(pl.program_id(1) == 0)
            def _():
                for ref in out_refs[len(outs):]:
                    ref[...] = jnp.zeros_like(ref)
            for ref, val in zip(out_refs[len(outs):], vals[len(outs):]):
                ref[...] += val

    return _call(body, grid=(n // tn, m // tm), in_specs=in_specs,
                 out_specs=[pl.BlockSpec((tm, tn), lambda j, i: (i, j)) for _ in outs]
                 + [pl.BlockSpec((1, tn), lambda j, i: (0, j))] * n_sums,
                 out_shape=[jax.ShapeDtypeStruct((m, n), dt) for dt in outs]
                 + [jax.ShapeDtypeStruct((1, n), F32)] * n_sums, args=args, name=name, comm=comm)


def _mm_tn(lhs, rhs, name, comm=None):
    t, a = lhs.shape
    b = rhs.shape[1]
    ta = a if a <= 1536 else _tile(a, 1536, LANES)
    tk = _tile(t, 2048, 8)

    def body(l_ref, r_ref, o_ref):
        @pl.when(pl.program_id(1) == 0)
        def _():
            o_ref[...] = jnp.zeros_like(o_ref)
        o_ref[...] += lax.dot_general(l_ref[...].astype(BF16), r_ref[...].astype(BF16), (((0,), (0,)), ((), ())),
                                      preferred_element_type=F32)

    res = _call(body, grid=(a // ta, t // tk),
                in_specs=[pl.BlockSpec((tk, ta), lambda i, k: (k, i)), pl.BlockSpec((tk, b), lambda i, k: (k, 0))],
                out_specs=[pl.BlockSpec((ta, b), lambda i, k: (i, 0))], out_shape=[jax.ShapeDtypeStruct((a, b), F32)],
                args=(lhs, rhs), name=name, comm=comm)
    return res[0] if comm is None else (res[0][0], res[1])


def _ffn_tn(f):
    return _tile(f, 1536, LANES)


def _ffn_up(n, wg_t, wu_t, name, comm=None):
    def epi(accs, ex, vc):
        a, b = accs
        return [a, b, (a * _sigmoid(a)) * b]
    return _mm([[(n, wg_t)], [(n, wu_t)]], epi, [], [], [BF16, BF16, BF16], trans_rhs=True, tm=512,
               tn=_ffn_tn(wg_t.shape[0]), name=name, comm=comm)


def _ffn_gate(n, wg_t, name, comm=None):
    def epi(accs, ex, vc):
        return [accs[0]]
    return _mm([[(n, wg_t)]], epi, [], [], [BF16], trans_rhs=True, tm=512, tn=_ffn_tn(wg_t.shape[0]), name=name,
               comm=comm)


def _ffn_up_given_gate(n, wu_t, a, name, comm=None):
    def epi(accs, ex, vc):
        av = ex[0].astype(F32)
        return [accs[0], (av * _sigmoid(av)) * accs[0]]
    return _mm([[(n, wu_t)]], epi, [a], [], [BF16, BF16], trans_rhs=True, tm=512, tn=_ffn_tn(wu_t.shape[0]),
               name=name, comm=comm)


def _residual_mm(lhs, w, res, gate, coef, name, norm=None, comm=None):
    def epi(accs, ex, vc):
        h = ex[0] + (coef * vc[0]) * accs[0]
        if norm is None:
            return [h, accs[0]]
        _, xn = _rms_stats(h)
        return [h, accs[0], (xn * vc[1]) * (1.0 + vc[2]) + vc[3]]
    vecs = [gate] + (list(norm) if norm is not None else [])
    outs = [F32, BF16] + ([BF16] if norm is not None else [])
    return _mm([[(lhs, w)]], epi, [res], vecs, outs, trans_rhs=False, tm=512, tn=w.shape[1], name=name, comm=comm)


def _ffn_bwd_hidden(df, wd, a, b, name, comm=None):
    def epi(accs, ex, vc):
        dh = accs[0]
        av, bv = ex[0].astype(F32), ex[1].astype(F32)
        sig = _sigmoid(av)
        silu = av * sig
        return [dh * bv * (sig + silu * (1.0 - sig)), dh * silu]
    return _mm([[(df, wd)]], epi, [a, b], [], [BF16, BF16], trans_rhs=True, tm=512, tn=_ffn_tn(wd.shape[0]),
               name=name, comm=comm)


def _plain_mm(pairs, out_dtype, trans_rhs, tn, name, tm=512, comm=None):
    def epi(accs, ex, vc):
        return [accs[0]]
    res = _mm([pairs], epi, [], [], [out_dtype], trans_rhs=trans_rhs, tm=tm, tn=tn, name=name, comm=comm)
    return res[0] if comm is None else (res[0][0], res[1])


HEADS_PER_TILE = LANES // HEAD_DIM


def _stack_heads(x):
    lane = lax.broadcasted_iota(jnp.int32, (1, LANES), 1)
    return jnp.concatenate([x * (lane // HEAD_DIM == h).astype(F32) for h in range(HEADS_PER_TILE)], axis=0)


def _unstack_heads(y):
    r = y.shape[0] // HEADS_PER_TILE
    lane = lax.broadcasted_iota(jnp.int32, (r, y.shape[1]), 1)
    out = y[0:r]
    for h in range(1, HEADS_PER_TILE):
        out = jnp.where(lane // HEAD_DIM == h, y[h * r:(h + 1) * r], out)
    return out


def _stacked_lse(lb):
    return jnp.concatenate([_lane_pick(lb, h) for h in range(HEADS_PER_TILE)], axis=0)


def _band_masks(n_row_blocks, n_col_blocks):
    shape = (n_row_blocks * BLOCK, n_col_blocks * BLOCK)
    qi = lax.broadcasted_iota(jnp.int32, shape, 0) % BLOCK
    kj = lax.broadcasted_iota(jnp.int32, shape, 1) % BLOCK
    return kj <= qi, kj >= qi


def _query_masks():
    same_ok, before_ok = _band_masks(HEADS_PER_TILE, 2)
    is_cur = lax.broadcasted_iota(jnp.int32, same_ok.shape, 1) >= BLOCK
    return jnp.logical_and(is_cur, same_ok), jnp.logical_and(jnp.logical_not(is_cur), before_ok)


def _dot_nt(a, b):
    return lax.dot_general(a.astype(BF16), b.astype(BF16), (((1,), (1,)), ((), ())), preferred_element_type=F32)


def _dot_nn(a, b):
    return lax.dot_general(a.astype(BF16), b.astype(BF16), (((1,), (0,)), ((), ())), preferred_element_type=F32)


def _dot_tn(a, b):
    return lax.dot_general(a.astype(BF16), b.astype(BF16), (((0,), (0,)), ((), ())), preferred_element_type=F32)


def _lane_pick(x, h):
    lane = lax.broadcasted_iota(jnp.int32, x.shape, 1)
    return jnp.sum(jnp.where(lane == h * HEAD_DIM, x, 0.0), axis=1, keepdims=True)


def _block_rows(idx, d):
    span = BLOCK * d
    g = idx // d
    q0 = g * span + idx % d
    has_prev = g > 0
    p0 = jnp.where(has_prev, q0 - span, q0)
    return pl.ds(q0, BLOCK, stride=d), pl.ds(p0, BLOCK, stride=d), has_prev


def _qkv_specs(s, tiles):
    q, k, v = [pl.BlockSpec((s, LANES), functools.partial(lambda hb, off: (0, off + hb), off=i * tiles))
               for i in range(3)]
    return q, k, v, pl.BlockSpec((s, LANES), lambda hb: (0, hb))


def _attn_seq_fwd(proj, width, name, comm=None):
    s = proj.shape[0]
    q_spec, k_spec, v_spec, cur = _qkv_specs(s, width // LANES)

    def body(q_ref, k_ref, v_ref, o_ref, l_ref, o_s, l_s):
        cur_valid, prev_valid = _query_masks()
        for bi, d in enumerate(DILATIONS):
            def blk(idx, carry, bi=bi, d=d):
                rows, prev, has_prev = _block_rows(idx, d)
                q2 = _stack_heads(q_ref[rows, :])
                keys = jnp.concatenate([k_ref[prev, :], k_ref[rows, :]], axis=0)
                vals = jnp.concatenate([v_ref[prev, :], v_ref[rows, :]], axis=0)
                valid = jnp.logical_or(cur_valid, jnp.logical_and(prev_valid, has_prev))
                sc = jnp.where(valid, _dot_nt(q2, keys), NEG)
                mx = jnp.max(sc, axis=1, keepdims=True)
                p = jnp.exp(sc - mx)
                den = jnp.sum(p, axis=1, keepdims=True)
                o_s[bi, rows, :] = _unstack_heads(_dot_nn(p, vals) / den)
                l_s[bi, rows, :] = _unstack_heads(jnp.broadcast_to(mx + jnp.log(den), (q2.shape[0], LANES)))
                return carry

            lax.fori_loop(0, s // BLOCK, blk, 0, unroll=8)
        for c in range(s // MERGE_CHUNK):
            rows = slice(c * MERGE_CHUNK, (c + 1) * MERGE_CHUNK)
            ls = [l_s[bi, rows, :] for bi in range(len(DILATIONS))]
            top = functools.reduce(jnp.maximum, ls)
            ws = [jnp.exp(l - top) for l in ls]
            den = functools.reduce(lambda a, b: a + b, ws)
            num = functools.reduce(lambda a, b: a + b, [w * o_s[bi, rows, :] for bi, w in enumerate(ws)])
            o_ref[rows, :] = num / den
            l_ref[rows, :] = top + jnp.log(den)

    return _call(
        body, grid=(width // LANES,), in_specs=[q_spec, k_spec, v_spec], out_specs=[cur, cur],
        out_shape=[jax.ShapeDtypeStruct((s, width), F32)] * 2,
        scratch_shapes=[pltpu.VMEM((len(DILATIONS), s, LANES), F32)] * 2,
        args=(proj, proj, proj), name=name, comm=comm)


def _attn_seq_bwd(proj, do, o, lse, cos, sin_signed, name, comm=None):
    s, width = do.shape
    q_spec, k_spec, v_spec, cur = _qkv_specs(s, width // LANES)
    table = pl.BlockSpec((s, LANES), lambda hb: (0, 0))
    qscale = HEAD_DIM ** -0.5

    def body(q_ref, k_ref, v_ref, do_ref, o_ref, l_ref, cos_ref, sin_ref, dq_out, dk_out, dv_out,
             dq_ref, dk_ref, dv_ref):
        dq_ref[...] = jnp.zeros_like(dq_ref)
        dk_ref[...] = jnp.zeros_like(dk_ref)
        dv_ref[...] = jnp.zeros_like(dv_ref)
        cur_valid, prev_valid = _query_masks()
        for d in DILATIONS:
            def blk(idx, carry, d=d):
                rows, prev, has_prev = _block_rows(idx, d)
                dob = do_ref[rows, :]
                q2 = _stack_heads(q_ref[rows, :])
                do2 = _stack_heads(dob)
                delta = jnp.sum(_stack_heads(dob * o_ref[rows, :]), axis=1, keepdims=True)
                lse2 = _stacked_lse(l_ref[rows, :])
                keys = jnp.concatenate([k_ref[prev, :], k_ref[rows, :]], axis=0)
                vals = jnp.concatenate([v_ref[prev, :], v_ref[rows, :]], axis=0)
                valid = jnp.logical_or(cur_valid, jnp.logical_and(prev_valid, has_prev))
                p = jnp.where(valid, jnp.exp(_dot_nt(q2, keys) - lse2), 0.0)
                ds = p * (_dot_nt(do2, vals) - delta)
                dq_ref[rows, :] += _unstack_heads(_dot_nn(ds, keys))
                dkk = _dot_tn(ds, q2)
                dvv = _dot_tn(p, do2)
                dk_ref[prev, :] += dkk[0:BLOCK]
                dk_ref[rows, :] += dkk[BLOCK:]
                dv_ref[prev, :] += dvv[0:BLOCK]
                dv_ref[rows, :] += dvv[BLOCK:]
                return carry

            lax.fori_loop(0, s // BLOCK, blk, 0, unroll=4)
        for c in range(s // MERGE_CHUNK):
            rows = slice(c * MERGE_CHUNK, (c + 1) * MERGE_CHUNK)
            cos, sin = cos_ref[rows, :], sin_ref[rows, :]
            dq, dk = dq_ref[rows, :], dk_ref[rows, :]
            dq_out[rows, :] = ((dq * cos - _partner(dq) * sin) * qscale).astype(BF16)
            dk_out[rows, :] = (dk * cos - _partner(dk) * sin).astype(BF16)
            dv_out[rows, :] = dv_ref[rows, :].astype(BF16)

    return _call(
        body, grid=(width // LANES,), in_specs=[q_spec, k_spec, v_spec, cur, cur, cur, table, table],
        out_specs=[cur, cur, cur], out_shape=[jax.ShapeDtypeStruct((s, width), BF16)] * 3,
        scratch_shapes=[pltpu.VMEM((s, LANES), F32)] * 3,
        args=(proj, proj, proj, do, o, lse, cos, sin_signed), name=name, comm=comm)


def _conv_specs(s, a_block, b_block):
    per = CONV_CHUNK // CONV_HALO
    a_cur = pl.BlockSpec((CONV_CHUNK, LANES), lambda cb, i: (i, a_block + cb))
    b_cur = pl.BlockSpec((CONV_CHUNK, LANES), lambda cb, i: (i, b_block + cb))
    a_halo = pl.BlockSpec((CONV_HALO, LANES), lambda cb, i: (jnp.maximum(i * per - 1, 0), a_block + cb))
    b_halo = pl.BlockSpec((CONV_HALO, LANES), lambda cb, i: (jnp.maximum(i * per - 1, 0), b_block + cb))
    w_spec = pl.BlockSpec((CONV_KERNEL, LANES), lambda cb, i: (0, cb))
    vec = pl.BlockSpec((1, LANES), lambda cb, i: (0, cb))
    out = pl.BlockSpec((CONV_CHUNK, LANES), lambda cb, i: (i, cb))
    return a_cur, b_cur, a_halo, b_halo, w_spec, vec, out


def _fill_glu_window(win, a_ref, b_ref, ah_ref, bh_ref, first):
    halo = ah_ref[...] * _sigmoid(bh_ref[...])
    win[0:CONV_HALO, :] = jnp.where(first, 0.0, halo)
    win[CONV_HALO:, :] = a_ref[...] * _sigmoid(b_ref[...])


def _conv_fwd(proj, a_block, b_block, w, bias, name):
    s = proj.shape[0]
    cw = w.shape[1]
    a_cur, b_cur, a_halo, b_halo, w_spec, vec, out = _conv_specs(s, a_block, b_block)
    lead = CONV_HALO - (CONV_KERNEL - 1)

    def body(a_ref, b_ref, ah_ref, bh_ref, w_ref, bias_ref, o_ref, win):
        _fill_glu_window(win, a_ref, b_ref, ah_ref, bh_ref, pl.program_id(1) == 0)
        for sub in range(CONV_CHUNK // CONV_SUB):
            base = sub * CONV_SUB
            acc = jnp.zeros((CONV_SUB, LANES), F32) + bias_ref[...]
            for j in range(CONV_KERNEL):
                acc = acc + w_ref[j:j + 1, :] * win[base + lead + j:base + lead + j + CONV_SUB, :]
            o_ref[base:base + CONV_SUB, :] = acc

    return pl.pallas_call(
        body, grid=(cw // LANES, s // CONV_CHUNK), in_specs=[a_cur, b_cur, a_halo, b_halo, w_spec, vec],
        out_specs=out, out_shape=jax.ShapeDtypeStruct((s, cw), F32),
        scratch_shapes=[pltpu.VMEM((CONV_CHUNK + CONV_HALO, LANES), F32)],
        compiler_params=_params(2), name=name)(proj, proj, proj, proj, w, bias)


def _conv_bwd(proj, a_block, b_block, w, du1, name):
    s = proj.shape[0]
    cw = w.shape[1]
    a_cur, b_cur, a_halo, b_halo, w_spec, vec, out = _conv_specs(s, a_block, b_block)
    per = CONV_CHUNK // CONV_HALO
    n_chunks = s // CONV_CHUNK
    d_next = pl.BlockSpec((CONV_HALO, LANES), lambda cb, i: (jnp.minimum((i + 1) * per, s // CONV_HALO - 1), cb))
    lead = CONV_HALO - (CONV_KERNEL - 1)

    def body(a_ref, b_ref, ah_ref, bh_ref, w_ref, d_ref, dn_ref, da_ref, db_ref, dw_ref, dbias_ref, win, dwin):
        i = pl.program_id(1)
        _fill_glu_window(win, a_ref, b_ref, ah_ref, bh_ref, i == 0)
        dwin[0:CONV_CHUNK, :] = d_ref[...]
        dwin[CONV_CHUNK:, :] = jnp.where(i == n_chunks - 1, 0.0, dn_ref[...])

        @pl.when(i == 0)
        def _():
            dw_ref[...] = jnp.zeros_like(dw_ref)
            dbias_ref[...] = jnp.zeros_like(dbias_ref)

        dbias_ref[...] += _colsum(d_ref[...])
        for sub in range(CONV_CHUNK // CONV_SUB):
            base = sub * CONV_SUB
            dcur = dwin[base:base + CONV_SUB, :]
            du0 = jnp.zeros((CONV_SUB, LANES), F32)
            for j in range(CONV_KERNEL):
                back = CONV_KERNEL - 1 - j
                du0 = du0 + w_ref[j:j + 1, :] * dwin[base + back:base + back + CONV_SUB, :]
                dw_ref[j:j + 1, :] += _colsum(dcur * win[base + lead + j:base + lead + j + CONV_SUB, :])
            av = a_ref[base:base + CONV_SUB, :]
            sig = _sigmoid(b_ref[base:base + CONV_SUB, :])
            da_ref[base:base + CONV_SUB, :] = (du0 * sig).astype(BF16)
            db_ref[base:base + CONV_SUB, :] = (du0 * av * sig * (1.0 - sig)).astype(BF16)

    return pl.pallas_call(
        body, grid=(cw // LANES, n_chunks), in_specs=[a_cur, b_cur, a_halo, b_halo, w_spec, out, d_next],
        out_specs=[out, out, w_spec, vec],
        out_shape=[jax.ShapeDtypeStruct((s, cw), BF16), jax.ShapeDtypeStruct((s, cw), BF16),
                   jax.ShapeDtypeStruct((CONV_KERNEL, cw), F32), jax.ShapeDtypeStruct((1, cw), F32)],
        scratch_shapes=[pltpu.VMEM((CONV_CHUNK + CONV_HALO, LANES), F32)] * 2,
        compiler_params=_params(2), name=name)(proj, proj, proj, proj, w, du1, du1)


def _adamw_math(w, g, m, v):
    m = ADAM_B1 * m + (1.0 - ADAM_B1) * g
    v = ADAM_B2 * v + (1.0 - ADAM_B2) * (g * g)
    m_hat = m / (1.0 - ADAM_B1 ** ADAM_STEP)
    v_hat = v / (1.0 - ADAM_B2 ** ADAM_STEP)
    delta = -ADAM_LR * (m_hat / (jnp.sqrt(v_hat) + ADAM_EPS) + ADAM_WD * w)
    return delta, m, v


def _adamw_big(w, g, m, v, name):
    rows, cols = w.shape
    tile = _tile(rows, 256, 8)
    spec = pl.BlockSpec((tile, cols), lambda i: (i, 0))

    def body(w_ref, g_ref, m_ref, v_ref, d_out, m_out, v_out):
        d_out[...], m_out[...], v_out[...] = _adamw_math(w_ref[...], g_ref[...], m_ref[...], v_ref[...])

    return pl.pallas_call(body, grid=(rows // tile,), in_specs=[spec] * 4, out_specs=[spec] * 3,
                          out_shape=[jax.ShapeDtypeStruct(w.shape, F32)] * 3, compiler_params=_params(1),
                          name=name)(w, g, m, v)


def _adamw_reduced(w, land, m, v, name):
    rows, cols = w.shape
    tile = _tile(rows, 256, 16)
    spec = pl.BlockSpec((tile, cols), lambda i: (i, 0))

    def body(w_ref, l_ref, m_ref, v_ref, g_out, d_out, m_out, v_out):
        g = l_ref[0].astype(F32)
        for q in range(1, N_CHIP):
            g = g + l_ref[q].astype(F32)
        g_out[...] = g
        d_out[...], m_out[...], v_out[...] = _adamw_math(w_ref[...], g, m_ref[...], v_ref[...])

    return pl.pallas_call(body, grid=(rows // tile,),
                          in_specs=[spec, pl.BlockSpec((N_CHIP, tile, cols), lambda i: (0, i, 0)), spec, spec],
                          out_specs=[spec] * 4, out_shape=[jax.ShapeDtypeStruct(w.shape, F32)] * 4,
                          compiler_params=_params(1), name=name)(w, land, m, v)


def _adamw_small(ws, gs, ms, vs, name):
    n = len(ws)

    def body(*refs):
        ins, outs = refs[:4 * n], refs[4 * n:]
        for t in range(n):
            res = _adamw_math(ins[t][...], ins[n + t][...], ins[2 * n + t][...], ins[3 * n + t][...])
            for j in range(3):
                outs[j * n + t][...] = res[j]

    shapes = [jax.ShapeDtypeStruct(w.shape, F32) for w in ws]
    res = pl.pallas_call(body, out_shape=shapes * 3, compiler_params=pltpu.CompilerParams(vmem_limit_bytes=VMEM_LIMIT),
                         name=name)(*ws, *gs, *ms, *vs)
    return res[:n], res[n:2 * n], res[2 * n:]


def _sum_blocks(x, n_blocks, name):
    r = x.shape[0] // n_blocks

    def body(x_ref, o_ref):
        acc = x_ref[0:r, :]
        for b in range(1, n_blocks):
            acc = acc + x_ref[b * r:(b + 1) * r, :]
        o_ref[...] = acc

    return pl.pallas_call(body, out_shape=jax.ShapeDtypeStruct((r, x.shape[1]), F32),
                          compiler_params=pltpu.CompilerParams(vmem_limit_bytes=VMEM_LIMIT), name=name)(x)


def _coords():
    return lax.axis_index("x"), lax.axis_index("y"), lax.axis_index("c")


def _flip(v, bit):
    return 1 - v if bit else v


def _ag_small(x, name):
    r, c = x.shape

    def body(x_ref, o_ref, send, recv, local_sem):
        mx, my, mc = _coords()

        def rows(px, py, pc):
            return o_ref.at[pl.ds(pl.multiple_of((4 * px + 2 * py + pc) * r, 8), r), :]

        local = pltpu.make_async_copy(x_ref, rows(mx, my, mc), local_sem)
        local.start()
        peers = [(_flip(mx, k >> 2 & 1), _flip(my, k >> 1 & 1), _flip(mc, k & 1)) for k in range(1, N_DEV)]
        sends = [pltpu.make_async_remote_copy(x_ref, rows(mx, my, mc), send.at[k], recv.at[k], device_id=p,
                                              device_id_type=MESH) for k, p in enumerate(peers)]
        for cp in sends:
            cp.start()
        for k, p in enumerate(peers):
            pltpu.make_async_remote_copy(x_ref, rows(*p), send.at[k], recv.at[k], device_id=p,
                                         device_id_type=MESH).wait_recv()
        for cp in sends:
            cp.wait_send()
        local.wait()

    vm = pl.BlockSpec(memory_space=pltpu.VMEM)
    return pl.pallas_call(
        body, in_specs=[vm], out_specs=vm, out_shape=jax.ShapeDtypeStruct((N_DEV * r, c), x.dtype),
        scratch_shapes=[pltpu.SemaphoreType.DMA((N_DEV - 1,)), pltpu.SemaphoreType.DMA((N_DEV - 1,)),
                        pltpu.SemaphoreType.DMA(())],
        name=name)(x)


class _GatherSmall:
    mid = None

    def __init__(self, x):
        self.inputs = [x]
        self.out_shapes = [jax.ShapeDtypeStruct((N_DEV * x.shape[0], x.shape[1]), x.dtype)]
        self.scratch = [pltpu.SemaphoreType.DMA((N_DEV - 1,)), pltpu.SemaphoreType.DMA((N_DEV - 1,)),
                        pltpu.SemaphoreType.DMA(())]

    def _plan(self, x_refs, o_refs, sems):
        send, recv, local_sem = sems
        x_ref, o_ref = x_refs[0], o_refs[0]
        r = x_ref.shape[0]
        mx, my, mc = _coords()

        def rows(px, py, pc):
            return o_ref.at[pl.ds(pl.multiple_of((4 * px + 2 * py + pc) * r, 8), r), :]

        peers = [(_flip(mx, k >> 2 & 1), _flip(my, k >> 1 & 1), _flip(mc, k & 1)) for k in range(1, N_DEV)]
        out = [pltpu.make_async_remote_copy(x_ref, rows(mx, my, mc), send.at[k], recv.at[k], device_id=p,
                                            device_id_type=MESH) for k, p in enumerate(peers)]
        arrivals = [pltpu.make_async_remote_copy(x_ref, rows(*p), send.at[k], recv.at[k], device_id=p,
                                                 device_id_type=MESH) for k, p in enumerate(peers)]
        return out, arrivals, pltpu.make_async_copy(x_ref, rows(mx, my, mc), local_sem)

    def start(self, x_refs, o_refs, sems):
        out, _, local = self._plan(x_refs, o_refs, sems)
        local.start()
        for cp in out:
            cp.start()

    def finish(self, x_refs, o_refs, sems):
        out, arrivals, local = self._plan(x_refs, o_refs, sems)
        for cp in arrivals:
            cp.wait_recv()
        for cp in out:
            cp.wait_send()
        local.wait()


class _GatherWeights:
    def __init__(self, shards):
        n_t = len(shards)
        self.inputs = list(shards)
        self.out_shapes = [jax.ShapeDtypeStruct((N_DEV * x.shape[0], x.shape[1]), x.dtype) for x in shards]
        self.scratch = [pltpu.SemaphoreType.DMA((n_t, 7)), pltpu.SemaphoreType.DMA((n_t, 7)),
                        pltpu.SemaphoreType.DMA((n_t,))]

    def _plan(self, x_refs, o_refs, sems):
        send, recv, local_sem = sems
        mx, my, mc = _coords()
        me, sibling = (mx, my, mc), (mx, my, 1 - mc)
        chips = [(1 - mx, my), (mx, 1 - my), (1 - mx, 1 - my)]

        def rows(t, px, py, pc):
            r = x_refs[t].shape[0]
            return o_refs[t].at[pl.ds(pl.multiple_of((4 * px + 2 * py + pc) * r, 8), r), :]

        def copy(t, k, block, to, src=None):
            return pltpu.make_async_remote_copy(
                src_ref=rows(t, *block) if src is None else src, dst_ref=rows(t, *block),
                send_sem=send.at[t, k], recv_sem=recv.at[t, k], device_id=to, device_id_type=MESH)

        def local(t):
            return pltpu.make_async_copy(x_refs[t], rows(t, *me), local_sem.at[t])

        return me, sibling, chips, mc, copy, local

    def start(self, x_refs, o_refs, sems):
        me, sibling, chips, mc, copy, local = self._plan(x_refs, o_refs, sems)
        for t in range(len(x_refs)):
            local(t).start()
            copy(t, 0, me, sibling, src=x_refs[t]).start()
            for j, chip in enumerate(chips):
                copy(t, 1 + j, me, (*chip, mc), src=x_refs[t]).start()

    def mid(self, x_refs, o_refs, sems):
        me, sibling, chips, mc, copy, local = self._plan(x_refs, o_refs, sems)
        for j, chip in enumerate(chips):
            for t in range(len(x_refs)):
                copy(t, 1 + j, (*chip, mc), me).wait_recv()
                copy(t, 4 + j, (*chip, mc), sibling).start()

    def finish(self, x_refs, o_refs, sems):
        me, sibling, chips, mc, copy, local = self._plan(x_refs, o_refs, sems)
        for t in range(len(x_refs)):
            copy(t, 0, sibling, me).wait_recv()
            for j, chip in enumerate(chips):
                copy(t, 4 + j, (*chip, 1 - mc), me).wait_recv()
            copy(t, 0, me, sibling, src=x_refs[t]).wait_send()
            for j, chip in enumerate(chips):
                copy(t, 1 + j, me, (*chip, mc), src=x_refs[t]).wait_send()
                copy(t, 4 + j, (*chip, mc), sibling).wait_send()
            local(t).wait()


class _SiblingExchange:
    mid = None

    def __init__(self, grads):
        n_t = len(grads)
        self.inputs = list(grads)
        self.out_shapes = [jax.ShapeDtypeStruct((N_CHIP,) + g.shape[2:], F32) for g in grads]
        self.scratch = [pltpu.SemaphoreType.DMA((n_t,)), pltpu.SemaphoreType.DMA((n_t,))]

    def _copies(self, g_refs, land, sems):
        send, recv = sems
        mx, my, mc = _coords()
        return [pltpu.make_async_remote_copy(g_refs[t].at[:, 1 - mc], land[t], send.at[t], recv.at[t],
                                             device_id=(mx, my, 1 - mc), device_id_type=MESH)
                for t in range(len(g_refs))]

    def start(self, g_refs, land, sems):
        for cp in self._copies(g_refs, land, sems):
            cp.start()

    def finish(self, g_refs, land, sems):
        for cp in self._copies(g_refs, land, sems):
            cp.wait()


class _Together:
    def __init__(self, *comms):
        self.comms = comms
        self.inputs = [x for c in comms for x in c.inputs]
        self.out_shapes = [x for c in comms for x in c.out_shapes]
        self.scratch = [x for c in comms for x in c.scratch]
        self.mid = self._mid if any(c.mid is not None for c in comms) else None

    def _each(self, phase, cin, cout, sems):
        i = o = s = 0
        for c in self.comms:
            fn = getattr(c, phase)
            ni, no, ns = len(c.inputs), len(c.out_shapes), len(c.scratch)
            if fn is not None:
                fn(cin[i:i + ni], cout[o:o + no], sems[s:s + ns])
            i, o, s = i + ni, o + no, s + ns

    def start(self, cin, cout, sems):
        self._each("start", cin, cout, sems)

    def _mid(self, cin, cout, sems):
        self._each("mid", cin, cout, sems)

    def finish(self, cin, cout, sems):
        self._each("finish", cin, cout, sems)


def _standalone(comm, name):
    def body():
        pass
    return _call(body, grid=(1,), in_specs=[], out_specs=[], out_shape=[], args=(), name=name, comm=comm)[1]


def _chip_partials(g4s, lands, name):
    n_t = len(g4s)
    in_specs, out_specs, out_shape = [], [], []
    for g4 in g4s:
        _, _, r, c = g4.shape
        in_specs.append(pl.BlockSpec((None, None, r, c), lambda q: (q, lax.axis_index("c"), 0, 0)))
        out_specs.append(pl.BlockSpec((None, r, c), lambda q: (q, 0, 0)))
        out_shape.append(jax.ShapeDtypeStruct((N_CHIP, r, c), BF16))
    in_specs += [pl.BlockSpec((None,) + g4.shape[2:], lambda q: (q, 0, 0)) for g4 in g4s]

    def body(*refs):
        for t in range(n_t):
            refs[2 * n_t + t][...] = (refs[t][...] + refs[n_t + t][...]).astype(BF16)

    return pl.pallas_call(body, grid=(N_CHIP,), in_specs=in_specs, out_specs=out_specs, out_shape=out_shape,
                          compiler_params=_params(1), name=name)(*g4s, *lands)


class _ChipExchange:
    mid = None

    def __init__(self, parts):
        n_t = len(parts)
        self.inputs = list(parts)
        self.out_shapes = [jax.ShapeDtypeStruct(p.shape, p.dtype) for p in parts]
        self.scratch = [pltpu.SemaphoreType.DMA((n_t, 3)), pltpu.SemaphoreType.DMA((n_t, 3)),
                        pltpu.SemaphoreType.DMA((n_t,))]

    def _plan(self, p_refs, land, sems):
        send, recv, local_sem = sems
        mx, my, mc = _coords()
        my_chip = 2 * mx + my
        peers = [(_flip(mx, fx), _flip(my, fy)) for fx, fy in ((1, 0), (0, 1), (1, 1))]

        def out(t, k):
            px, py = peers[k]
            return pltpu.make_async_remote_copy(p_refs[t].at[2 * px + py], land[t].at[my_chip], send.at[t, k],
                                                recv.at[t, k], device_id=(px, py, mc), device_id_type=MESH)

        def arrival(t, k):
            px, py = peers[k]
            return pltpu.make_async_remote_copy(p_refs[t].at[my_chip], land[t].at[2 * px + py], send.at[t, k],
                                                recv.at[t, k], device_id=(px, py, mc), device_id_type=MESH)

        def local(t):
            return pltpu.make_async_copy(p_refs[t].at[my_chip], land[t].at[my_chip], local_sem.at[t])

        return out, arrival, local

    def start(self, p_refs, land, sems):
        out, arrival, local = self._plan(p_refs, land, sems)
        for t in range(len(p_refs)):
            local(t).start()
            for k in range(3):
                out(t, k).start()

    def finish(self, p_refs, land, sems):
        out, arrival, local = self._plan(p_refs, land, sems)
        for t in range(len(p_refs)):
            for k in range(3):
                arrival(t, k).wait_recv()
                out(t, k).wait_send()
            local(t).wait()


def _rope_tables(s, width):
    heads = width // HEAD_DIM
    inv_freq = ROPE_THETA ** (-jnp.arange(0, HEAD_DIM, 2, dtype=F32) / HEAD_DIM)
    inv_full = jnp.tile(inv_freq, 2 * heads)
    sign = jnp.tile(jnp.concatenate([-jnp.ones((HALF_HEAD,), F32), jnp.ones((HALF_HEAD,), F32)]), heads)
    ang = jnp.arange(s, dtype=F32)[:, None] * inv_full[None, :]
    return jnp.cos(ang), jnp.sin(ang) * sign[None, :]


def _pad_rows(v, rows):
    return jnp.concatenate([v, jnp.zeros((rows - 1, v.shape[1]), v.dtype)], axis=0)


def kernel(x, c, w_ada, b_ada, ffn1_norm_g, ffn1_w_gate, ffn1_w_up, ffn1_w_down, mix_norm_g, w_in, conv_dw_w, conv_dw_b, conv_ln_g, conv_ln_b, attn_out_g, conv_out_g, w_out, ffn2_norm_g, ffn2_w_gate, ffn2_w_up, ffn2_w_down, final_norm_g, loss_target, m_w_ada, m_b_ada, m_ffn1_norm_g, m_ffn1_w_gate, m_ffn1_w_up, m_ffn1_w_down, m_mix_norm_g, m_w_in, m_conv_dw_w, m_conv_dw_b, m_conv_ln_g, m_conv_ln_b, m_attn_out_g, m_conv_out_g, m_w_out, m_ffn2_norm_g, m_ffn2_w_gate, m_ffn2_w_up, m_ffn2_w_down, m_final_norm_g, v_w_ada, v_b_ada, v_ffn1_norm_g, v_ffn1_w_gate, v_ffn1_w_up, v_ffn1_w_down, v_mix_norm_g, v_w_in, v_conv_dw_w, v_conv_dw_b, v_conv_ln_g, v_conv_ln_b, v_attn_out_g, v_conv_out_g, v_w_out, v_ffn2_norm_g, v_ffn2_w_gate, v_ffn2_w_up, v_ffn2_w_down, v_final_norm_g):
    mx, my, mc = _coords()
    me = 4 * mx + 2 * my + mc
    s, d = x.shape[1], x.shape[2]
    aw = d // 2
    x2, target = x[0], loss_target[0]
    n_mod = w_ada.shape[2] * N_DEV // d
    mod_cols = w_ada.shape[2]

    def shard(w, transpose):
        return (w[0].T if transpose else w[0]).astype(BF16)

    cw_shard = conv_dw_w.shape[3]
    n_taps = CONV_KERNEL * cw_shard
    first_len = -(-(d + n_taps) // LANES) * LANES
    first = jnp.concatenate([c, conv_dw_w[0, :, 0, :].reshape(1, n_taps), jnp.zeros((1, first_len - d - n_taps), F32)], axis=1)
    first_all, wg1 = _standalone(
        _Together(_GatherSmall(_pad_rows(first, 8)), _GatherWeights([shard(ffn1_w_gate, True)])), "ag_first")
    first_all = first_all[0::8]
    c_all = first_all[:, :d]
    conv_w = first_all[:, d:d + n_taps].reshape(N_DEV, CONV_KERNEL, cw_shard).transpose(1, 0, 2).reshape(CONV_KERNEL, aw)

    silu_c = _silu_rows(c_all, "silu_c")
    mod_part = _plain_mm([(silu_c, w_ada[0])], F32, False, mod_cols, "mod_mm")
    mod_all = _ag_small(mod_part, "ag_mod").reshape(N_DEV, N_DEV, mod_cols)
    mod = lax.dynamic_index_in_dim(mod_all, me, axis=1, keepdims=False).reshape(1, n_mod * d) + b_ada
    sh1, sc1, g1, sh2, sc2, g2, sh3, sc3, g3 = [mod[:, i * d:(i + 1) * d] for i in range(n_mod)]

    def split(g):
        return g.reshape(N_CHIP, 2, g.shape[0] // N_DEV, g.shape[1])

    def partials(g4s, lands, tag):
        return _chip_partials(g4s, lands, "chip_partials_" + tag)

    gather_late = _GatherWeights([shard(ffn2_w_gate, True), shard(ffn2_w_up, True), shard(ffn2_w_down, False),
                                  shard(w_out, False)])

    n1 = _norm_mod_fwd(x2, ffn1_norm_g, sc1, sh1, "norm1")
    (a1,), (wu1,) = _ffn_gate(n1, wg1, "ffn1_gate", comm=_GatherWeights([shard(ffn1_w_up, True)]))
    (b1, hid1), (wd1,) = _ffn_up_given_gate(n1, wu1, a1, "ffn1_up", comm=_GatherWeights([shard(ffn1_w_down, False)]))
    (h1, f1, n2), (win_t,) = _residual_mm(hid1, wd1, x2, g1, 0.5, "ffn1_down", norm=(mix_norm_g, sc2, sh2),
                                          comm=_GatherWeights([shard(w_in, True)]))
    cos, sin_signed = _rope_tables(s, LANES)
    proj = _proj_rope(n2, win_t, cos, sin_signed, aw, "proj")
    lanes_per = aw // LANES
    (attn, lse), (wg2, wu2, wd2, wout) = _attn_seq_fwd(proj, aw, "attn_fwd", comm=gather_late)
    u1 = _conv_fwd(proj, 3 * lanes_per, 4 * lanes_per, conv_w, conv_dw_b, "conv_fwd")
    y = _mix_post_fwd(attn, u1, attn_out_g, conv_ln_g, conv_ln_b, conv_out_g, "mix_post")
    h2, mix, n3 = _residual_mm(y, wout, h1, g2, 1.0, "mix_out", norm=(ffn2_norm_g, sc3, sh3))
    a3, b3, hid3 = _ffn_up(n3, wg2, wu2, "ffn2_up")

    dh3, df3, err2, d_final_g, dg3 = _last_mm_loss(hid3, wd2, h2, g3, 0.5, target, final_norm_g.reshape(1, d),
                                                   "ffn2_down_loss")
    loss_part = jnp.zeros((1, LANES), F32).at[0, 0].set(0.5 * jnp.sum(err2) / d)

    da3, db3 = _ffn_bwd_hidden(df3, wd2, a3, b3, "ffn2_hidden_bwd")
    g4_a = [split(_mm_tn(da3, n3, "ffn2_dwg")), split(_mm_tn(db3, n3, "ffn2_dwu")), split(_mm_tn(hid3, df3, "ffn2_dwd"))]
    (dh2, dmix, dsh3, dsc3, dgn3, dg2), land_a = _mm_norm_mod_bwd(
        [(da3, wg2), (db3, wu2)], h2, dh3, ffn2_norm_g, sc3, (mix, g2, 1.0), "ffn2_dn_norm3_bwd", tm=256,
        comm=_SiblingExchange(g4_a))
    parts_a = partials(g4_a, land_a, "a")
    dy = _plain_mm([(dmix, wout)], BF16, True, d, "mix_dy")
    g_wout = _mm_tn(y, dmix, "mix_dwout")
    dattn, du1, d_attn_g, d_conv_g, d_ln_g, d_ln_b = _mix_post_bwd(
        dy, attn, u1, attn_out_g, conv_ln_g, conv_ln_b, conv_out_g, "mix_post_bwd")
    dga, dgb, d_taps, d_conv_b = _conv_bwd(proj, 3 * lanes_per, 4 * lanes_per, conv_w, du1, "conv_bwd")
    (dq, dk, dv), sums_a = _attn_seq_bwd(proj, dattn, attn, lse, cos, sin_signed, "attn_bwd",
                                         comm=_ChipExchange(parts_a))
    dproj = jnp.concatenate([dq, dk, dv, dga, dgb], axis=1)
    g4_b = [split(g_wout), split(_mm_tn(dproj, n2, "mix_dwin"))]
    (dh1, df1, dsh2, dsc2, dgn2, dg1), land_b = _mm_norm_mod_bwd(
        [(dproj, win_t)], h1, dh2, mix_norm_g, sc2, (f1, g1, 0.5), "mix_dn_norm2_bwd", tm=512,
        comm=_SiblingExchange(g4_b))
    parts_b = partials(g4_b, land_b, "b")
    g4_c = [split(_mm_tn(hid1, df1, "ffn1_dwd"))]
    (da1, db1), both = _ffn_bwd_hidden(df1, wd1, a1, b1, "ffn1_hidden_bwd",
                                       comm=_Together(_ChipExchange(parts_b), _SiblingExchange(g4_c)))
    sums_b, land_c = both[:2], both[2:]
    parts_c = partials(g4_c, land_c, "c")
    g_wu1, sums_c = _mm_tn(db1, n1, "ffn1_dwu", comm=_ChipExchange(parts_c))
    g4_d = [split(g_wu1)]
    g_wg1, land_d = _mm_tn(da1, n1, "ffn1_dwg", comm=_SiblingExchange(g4_d))
    parts_d = partials(g4_d, land_d, "d")
    g4_e = [split(g_wg1)]
    dn1, both = _plain_mm([(da1, wg1), (db1, wu1)], BF16, False, d, "ffn1_dn",
                          comm=_Together(_ChipExchange(parts_d), _SiblingExchange(g4_e)))
    sums_d, land_e = both[:1], both[1:]
    parts_e = partials(g4_e, land_e, "e")
    (dx, dsh1, dsc1, dgn1), sums_e = _norm_mod_bwd(dn1, x2, dh1, ffn1_norm_g, sc1, "norm1_bwd",
                                                   comm=_ChipExchange(parts_e))

    dmod = jnp.concatenate([dsh1, dsc1, dg1, dsh2, dsc2, dg2, dsh3, dsc3, dg3], axis=1)
    small = [dmod, dgn1, dgn2, dgn3, d_final_g, d_conv_b, d_ln_g, d_ln_b, d_attn_g, d_conv_g,
             d_taps.reshape(1, CONV_KERNEL * aw), loss_part]
    sizes = [v.shape[1] for v in small]
    total = sum(sizes)
    padded = -(-total // (8 * LANES)) * (8 * LANES)
    packed = jnp.concatenate(small + [jnp.zeros((1, padded - total), F32)], axis=1).reshape(8, padded // 8)
    gathered = _ag_small(packed, "ag_small_grads")
    summed = _sum_blocks(gathered, N_DEV, "sum_small_grads").reshape(1, padded)
    offs = [sum(sizes[:i]) for i in range(len(sizes))]
    (g_b_ada, g_gn1, g_gn2, g_gn3, g_final, g_conv_b, g_ln_g, g_ln_b, g_attn_g, g_conv_g, g_taps, loss_row) = [
        summed[:, o:o + n] for o, n in zip(offs, sizes)]
    loss = loss_row[0, 0]
    g_taps_shard = lax.dynamic_slice_in_dim(g_taps.reshape(CONV_KERNEL, aw), me * cw_shard, cw_shard, axis=1)
    dmod_all = gathered.reshape(N_DEV, padded)[:, :n_mod * d]
    dmod_cols = lax.dynamic_slice_in_dim(dmod_all, me * mod_cols, mod_cols, axis=1)
    g_w_ada = _mm_tn(silu_c, dmod_cols, "ada_dw")

    arrived = dict(zip(["ffn2_w_gate", "ffn2_w_up", "ffn2_w_down", "w_out", "w_in", "ffn1_w_down", "ffn1_w_up",
                        "ffn1_w_gate"], list(sums_a) + list(sums_b) + list(sums_c) + list(sums_d) + list(sums_e)))
    transposed = ("ffn1_w_gate", "ffn1_w_up", "w_in", "ffn2_w_gate", "ffn2_w_up")
    grads = {
        "w_ada": g_w_ada, "b_ada": g_b_ada, "ffn1_norm_g": g_gn1, "mix_norm_g": g_gn2, "conv_dw_w": g_taps_shard,
        "conv_dw_b": g_conv_b, "conv_ln_g": g_ln_g, "conv_ln_b": g_ln_b, "attn_out_g": g_attn_g,
        "conv_out_g": g_conv_g, "ffn2_norm_g": g_gn3, "final_norm_g": g_final,
    }
    weights = dict(w_ada=w_ada, b_ada=b_ada, ffn1_norm_g=ffn1_norm_g, ffn1_w_gate=ffn1_w_gate, ffn1_w_up=ffn1_w_up, ffn1_w_down=ffn1_w_down, mix_norm_g=mix_norm_g, w_in=w_in, conv_dw_w=conv_dw_w, conv_dw_b=conv_dw_b, conv_ln_g=conv_ln_g, conv_ln_b=conv_ln_b, attn_out_g=attn_out_g, conv_out_g=conv_out_g, w_out=w_out, ffn2_norm_g=ffn2_norm_g, ffn2_w_gate=ffn2_w_gate, ffn2_w_up=ffn2_w_up, ffn2_w_down=ffn2_w_down, final_norm_g=final_norm_g)
    moms = dict(w_ada=m_w_ada, b_ada=m_b_ada, ffn1_norm_g=m_ffn1_norm_g, ffn1_w_gate=m_ffn1_w_gate, ffn1_w_up=m_ffn1_w_up, ffn1_w_down=m_ffn1_w_down, mix_norm_g=m_mix_norm_g, w_in=m_w_in, conv_dw_w=m_conv_dw_w, conv_dw_b=m_conv_dw_b, conv_ln_g=m_conv_ln_g, conv_ln_b=m_conv_ln_b, attn_out_g=m_attn_out_g, conv_out_g=m_conv_out_g, w_out=m_w_out, ffn2_norm_g=m_ffn2_norm_g, ffn2_w_gate=m_ffn2_w_gate, ffn2_w_up=m_ffn2_w_up, ffn2_w_down=m_ffn2_w_down, final_norm_g=m_final_norm_g)
    vars_ = dict(w_ada=v_w_ada, b_ada=v_b_ada, ffn1_norm_g=v_ffn1_norm_g, ffn1_w_gate=v_ffn1_w_gate, ffn1_w_up=v_ffn1_w_up, ffn1_w_down=v_ffn1_w_down, mix_norm_g=v_mix_norm_g, w_in=v_w_in, conv_dw_w=v_conv_dw_w, conv_dw_b=v_conv_dw_b, conv_ln_g=v_conv_ln_g, conv_ln_b=v_conv_ln_b, attn_out_g=v_attn_out_g, conv_out_g=v_conv_out_g, w_out=v_w_out, ffn2_norm_g=v_ffn2_norm_g, ffn2_w_gate=v_ffn2_w_gate, ffn2_w_up=v_ffn2_w_up, ffn2_w_down=v_ffn2_w_down, final_norm_g=v_final_norm_g)
    names = list(weights)
    big = ["w_ada", "ffn1_w_gate", "ffn1_w_up", "ffn1_w_down", "w_in", "w_out", "ffn2_w_gate", "ffn2_w_up",
           "ffn2_w_down"]
    shape2 = {n: (weights[n].shape[-2] if weights[n].ndim > 1 else 1, weights[n].shape[-1]) for n in names}
    shape2["conv_dw_w"] = (CONV_KERNEL, cw_shard)
    g_out, d_out, m_out, v_out = {}, {}, {}, {}
    for n in big:
        if n in arrived:
            def view(t, n=n):
                return t[0].T if n in transposed else t[0]
            res = _adamw_reduced(view(weights[n]), arrived[n], view(moms[n]), view(vars_[n]), "adamw_" + n)
            g_out[n], d_out[n], m_out[n], v_out[n] = [r.T if n in transposed else r for r in res]
        else:
            g2d = grads[n].reshape(shape2[n])
            res = _adamw_big(weights[n].reshape(shape2[n]), g2d, moms[n].reshape(shape2[n]),
                             vars_[n].reshape(shape2[n]), "adamw_" + n)
            g_out[n], (d_out[n], m_out[n], v_out[n]) = g2d, res
    rest = [n for n in names if n not in big]
    res = _adamw_small([weights[n].reshape(shape2[n]) for n in rest], [grads[n].reshape(shape2[n]) for n in rest],
                       [moms[n].reshape(shape2[n]) for n in rest], [vars_[n].reshape(shape2[n]) for n in rest],
                       "adamw_small")
    for i, n in enumerate(rest):
        g_out[n], d_out[n], m_out[n], v_out[n] = grads[n], res[0][i], res[1][i], res[2][i]

    def shaped(table):
        return [table[n].reshape(weights[n].shape) for n in names]

    return (loss, dx.reshape(x.shape), *shaped(g_out), *shaped(d_out), *shaped(m_out), *shaped(v_out))
```

```python
import functools

import jax
import jax.numpy as jnp
from jax import lax
from jax.experimental import pallas as pl
from jax.experimental.pallas import tpu as pltpu

F32 = jnp.float32
BF16 = jnp.bfloat16
MESH = pl.DeviceIdType.MESH
ANY = pl.BlockSpec(memory_space=pl.ANY)

N_DEV = 8
N_CHIP = 4
HEAD_DIM = 64
HALF_HEAD = HEAD_DIM // 2
LANES = 128
BLOCK = 128
DILATIONS = (1, 4, 16)
MERGE_CHUNK = 512
ROPE_THETA = 10000.0
CONV_KERNEL = 31
CONV_HALO = 32
CONV_CHUNK = 512
CONV_SUB = 128
RMS_EPS = 1e-6
LN_EPS = 1e-5
ADAM_LR = 0.001
ADAM_B1 = 0.9
ADAM_B2 = 0.999
ADAM_EPS = 1e-08
ADAM_WD = 0.01
ADAM_STEP = 10
VMEM_LIMIT = 56 * 1024 * 1024
NEG = -1e30


def _params(n_axes):
    return pltpu.CompilerParams(dimension_semantics=("arbitrary",) * n_axes, vmem_limit_bytes=VMEM_LIMIT)


def _tile(n, target, unit):
    best = None
    for t in range(unit, min(n, target) + 1, unit):
        if n % t == 0:
            best = t
    return best if best is not None else n


def _sigmoid(x):
    return 0.5 * (jnp.tanh(0.5 * x) + 1.0)


def _call(body, *, grid, in_specs, out_specs, out_shape, args, name, scratch_shapes=(), comm=None):
    params = _params(len(grid))
    if comm is None:
        return pl.pallas_call(body, grid=grid, in_specs=list(in_specs), out_specs=list(out_specs),
                              out_shape=list(out_shape), scratch_shapes=list(scratch_shapes),
                              compiler_params=params, name=name)(*args)
    n_in, n_out, n_scr = len(args), len(out_shape), len(scratch_shapes)
    c_in, c_out = len(comm.inputs), len(comm.out_shapes)
    steps = 1
    for g in grid:
        steps *= g

    def hosted(*refs):
        pos = 0
        parts = []
        for size in (n_in, c_in, n_out, c_out, n_scr, len(comm.scratch)):
            parts.append(refs[pos:pos + size])
            pos += size
        ins, cin, outs, cout, scr, cscr = parts
        step = 0
        for axis, g in enumerate(grid):
            step = step * g + pl.program_id(axis)

        @pl.when(step == 0)
        def _():
            comm.start(cin, cout, cscr)

        body(*ins, *outs, *scr)
        if comm.mid is not None and steps >= 4:
            @pl.when(step == (3 * steps) // 4)
            def _():
                comm.mid(cin, cout, cscr)

        @pl.when(step == steps - 1)
        def _():
            if comm.mid is not None and steps < 4:
                comm.mid(cin, cout, cscr)
            comm.finish(cin, cout, cscr)

    res = pl.pallas_call(
        hosted, grid=grid, in_specs=list(in_specs) + [ANY] * c_in, out_specs=list(out_specs) + [ANY] * c_out,
        out_shape=list(out_shape) + list(comm.out_shapes), scratch_shapes=list(scratch_shapes) + list(comm.scratch),
        compiler_params=params, name=name)(*args, *comm.inputs)
    return res[:n_out], res[n_out:]


def _rows(fn, rows_in, vecs_in, rows_out, vecs_out, *, tile, name, comm=None):
    norm = [r if isinstance(r, tuple) else (r, r.shape[1], 0) for r in rows_in]
    n_rows = norm[0][0].shape[0]
    n_tiles = n_rows // tile
    in_specs, args = [], []
    for arr, width, cb in norm:
        in_specs.append(pl.BlockSpec((tile, width), functools.partial(lambda i, cb: (i, cb), cb=cb)))
        args.append(arr)
    for v in vecs_in:
        in_specs.append(pl.BlockSpec((1, v.shape[1]), lambda i: (0, 0)))
        args.append(v)
    out_shape = [jax.ShapeDtypeStruct((n_rows, w), dt) for w, dt in rows_out]
    out_shape += [jax.ShapeDtypeStruct((1, w), F32) for w in vecs_out]
    out_specs = [pl.BlockSpec((tile, w), lambda i: (i, 0)) for w, _ in rows_out]
    out_specs += [pl.BlockSpec((1, w), lambda i: (0, 0)) for w in vecs_out]
    n_in, n_ro = len(args), len(rows_out)

    def body(*refs):
        vals = [r[...] for r in refs[:n_in]]
        outs = refs[n_in:]
        row_vals, vec_vals = fn(*vals)
        for ref, val in zip(outs[:n_ro], row_vals):
            if isinstance(val, tuple):
                w = val[0].shape[1]
                for j, piece in enumerate(val):
                    ref[:, j * w:(j + 1) * w] = piece.astype(ref.dtype)
            else:
                ref[...] = val.astype(ref.dtype)
        if vecs_out:
            @pl.when(pl.program_id(0) == 0)
            def _():
                for ref in outs[n_ro:]:
                    ref[...] = jnp.zeros_like(ref)
            for ref, val in zip(outs[n_ro:], vec_vals):
                ref[...] += val

    return _call(body, grid=(n_tiles,), in_specs=in_specs, out_specs=out_specs, out_shape=out_shape, args=args,
                 name=name, comm=comm)


def _colsum(x):
    return jnp.sum(x, axis=0, keepdims=True)


def _rms_stats(h):
    r = lax.rsqrt(jnp.mean(h * h, axis=-1, keepdims=True) + RMS_EPS)
    return r, h * r


def _rms_back(r, xn, dxn):
    return r * (dxn - xn * jnp.mean(dxn * xn, axis=-1, keepdims=True))


def _branch_back(dh, f, gate, coef):
    return (coef * gate) * dh, coef * _colsum(f.astype(F32) * dh)


def _norm_mod_back(dn, h, dh_in, gain, scale):
    dn = dn.astype(F32)
    r, xn = _rms_stats(h)
    y = xn * gain
    dy = dn * (1.0 + scale)
    dh = dh_in + _rms_back(r, xn, dy * gain)
    return dh, [_colsum(dn), _colsum(dn * y), _colsum(dy * xn)]


def _norm_mod_bwd(dn, h, dh_in, gain, scale, name, comm=None):
    d = h.shape[1]

    def fn(dn, h, dh_in, gain, scale):
        dh, vecs = _norm_mod_back(dn, h, dh_in, gain, scale)
        return [dh], vecs
    return _rows(fn, [dn, h, dh_in], [gain, scale], [(d, F32)], [d, d, d], tile=256, name=name, comm=comm)


def _mm_norm_mod_bwd(pairs, h, dh_in, gain, scale, branch, name, tm, comm=None):
    f, gate, coef = branch

    def epi(accs, ex, vc):
        dh, vecs = _norm_mod_back(accs[0], ex[0], ex[1], vc[0], vc[1])
        df, dgate = _branch_back(dh, ex[2], vc[2], coef)
        return [dh, df] + vecs + [dgate]
    return _mm([pairs], epi, [h, dh_in, f], [gain, scale, gate], [F32, BF16], trans_rhs=False, tm=tm,
               tn=h.shape[1], name=name, n_sums=4, comm=comm)


def _last_mm_loss(lhs, w, res, gate, coef, target, gain, name):
    d = w.shape[1]

    def epi(accs, ex, vc):
        f = accs[0]
        h = ex[0] + (coef * vc[0]) * f
        r, xn = _rms_stats(h)
        err = xn * vc[1] - ex[1]
        dout = err * (1.0 / d)
        dh = _rms_back(r, xn, dout * vc[1])
        df, dgate = _branch_back(dh, f, vc[0], coef)
        return [dh, df, _colsum(err * err), _colsum(dout * xn), dgate]
    return _mm([[(lhs, w)]], epi, [res, target], [gate, gain], [F32, BF16], trans_rhs=False, tm=256, tn=d,
               name=name, n_sums=3)


def _partner(x):
    if x.shape[1] > LANES:
        return jnp.concatenate([_partner(x[:, c:c + LANES]) for c in range(0, x.shape[1], LANES)], axis=1)
    lane = lax.broadcasted_iota(jnp.int32, x.shape, 1) % HEAD_DIM
    return jnp.where(lane < HALF_HEAD, pltpu.roll(x, LANES - HALF_HEAD, 1), pltpu.roll(x, HALF_HEAD, 1))


def _proj_rope(n, w_t, cos, sin_signed, width, name):
    s, kdim = n.shape
    n_cols = w_t.shape[0]
    tm = _tile(s, 1024, 8)
    qscale = HEAD_DIM ** -0.5

    chunk = _tile(tm, 256, 8)

    def body(n_ref, w_ref, cos_ref, sin_ref, o_ref):
        j = pl.program_id(0)

        def products(rows):
            return lax.dot_general(n_ref[rows, :].astype(BF16), w_ref[...].astype(BF16), (((1,), (1,)), ((), ())),
                                   preferred_element_type=F32)

        @pl.when(j >= 2)
        def _():
            for c in range(tm // chunk):
                rows = slice(c * chunk, (c + 1) * chunk)
                o_ref[rows, :] = products(rows)

        @pl.when(j < 2)
        def _():
            scale = jnp.where(j == 0, qscale, 1.0)
            for c in range(tm // chunk):
                rows = slice(c * chunk, (c + 1) * chunk)
                acc = products(rows)
                cos = jnp.tile(cos_ref[rows, :], (1, width // LANES))
                sin = jnp.tile(sin_ref[rows, :], (1, width // LANES))
                o_ref[rows, :] = scale * (acc * cos + _partner(acc) * sin)

    table = pl.BlockSpec((tm, LANES), lambda j, i: (jnp.where(j < 2, i, 0), 0))
    return pl.pallas_call(
        body, grid=(n_cols // width, s // tm),
        in_specs=[pl.BlockSpec((tm, kdim), lambda j, i: (i, 0)), pl.BlockSpec((width, kdim), lambda j, i: (j, 0)),
                  table, table],
        out_specs=pl.BlockSpec((tm, width), lambda j, i: (i, j)), out_shape=jax.ShapeDtypeStruct((s, n_cols), F32),
        compiler_params=_params(2), name=name)(n, w_t, cos, sin_signed)


def _mix_post(attn, u1, attn_g, ln_g, ln_b, conv_g):
    _, xa = _rms_stats(attn)
    mu = jnp.mean(u1, axis=-1, keepdims=True)
    xc = u1 - mu
    rstd = lax.rsqrt(jnp.mean(xc * xc, axis=-1, keepdims=True) + LN_EPS)
    u2 = (xc * rstd) * ln_g + ln_b
    u3 = u2 * _sigmoid(u2)
    _, x3 = _rms_stats(u3)
    return jnp.concatenate([xa * attn_g, x3 * conv_g], axis=1)


def _mix_post_back(dy, attn, u1, attn_g, ln_g, ln_b, conv_g):
    w = attn.shape[1]
    dya, dyc = dy[:, :w], dy[:, w:]
    ra, xa = _rms_stats(attn)
    dattn = _rms_back(ra, xa, dya * attn_g)
    mu = jnp.mean(u1, axis=-1, keepdims=True)
    xc = u1 - mu
    rstd = lax.rsqrt(jnp.mean(xc * xc, axis=-1, keepdims=True) + LN_EPS)
    xh = xc * rstd
    u2 = xh * ln_g + ln_b
    sig = _sigmoid(u2)
    u3 = u2 * sig
    r3, x3 = _rms_stats(u3)
    du3 = _rms_back(r3, x3, dyc * conv_g)
    du2 = du3 * (sig + u3 * (1.0 - sig))
    dxh = du2 * ln_g
    du1 = rstd * (dxh - jnp.mean(dxh, axis=-1, keepdims=True) - xh * jnp.mean(dxh * xh, axis=-1, keepdims=True))
    return dattn, du1, [_colsum(dya * xa), _colsum(dyc * x3), _colsum(du2 * xh), _colsum(du2)]


def _silu_rows(c_all, name):
    def fn(c):
        return [c * _sigmoid(c)], []
    return _rows(fn, [c_all], [], [(c_all.shape[1], BF16)], [], tile=c_all.shape[0], name=name)[0]


def _mm(groups, epi, extras, vecs, outs, *, trans_rhs, tm, tn, name, n_sums=0, pre=None, pre_inputs=(),
        comm=None):
    m = (pre_inputs[0] if pre is not None else groups[0][0][0]).shape[0]
    n = groups[0][0][1].shape[0] if trans_rhs else groups[0][0][1].shape[1]
    tm, tn = min(tm, m), min(tn, n)
    in_specs, args, uses_pre = [], [], []
    for grp in groups:
        for lhs, rhs in grp:
            k = rhs.shape[1] if trans_rhs else rhs.shape[0]
            uses_pre.append(lhs is None)
            if lhs is not None:
                in_specs.append(pl.BlockSpec((tm, k), lambda j, i: (i, 0)))
                args.append(lhs)
            in_specs.append(pl.BlockSpec((tn, k), lambda j, i: (j, 0)) if trans_rhs
                            else pl.BlockSpec((k, tn), lambda j, i: (0, j)))
            args.append(rhs)
    n_mm = len(args)
    for p in pre_inputs:
        in_specs.append(pl.BlockSpec((tm, p.shape[1]), lambda j, i: (i, 0)))
        args.append(p)
    for e in extras:
        in_specs.append(pl.BlockSpec((tm, tn), lambda j, i: (i, j)) if e.shape[1] == n
                        else pl.BlockSpec((tm, e.shape[1]), lambda j, i: (i, 0)))
        args.append(e)
    for v in vecs:
        in_specs.append(pl.BlockSpec((1, tn), lambda j, i: (0, j)) if v.shape[1] == n
                        else pl.BlockSpec((1, v.shape[1]), lambda j, i: (0, 0)))
        args.append(v)
    sizes = [len(g) for g in groups]
    n_pre, n_ex, n_vec = len(pre_inputs), len(extras), len(vecs)
    dims = (((1,), (1,)), ((), ())) if trans_rhs else (((1,), (0,)), ((), ()))
    out_specs, out_shape = [], []
    if pre is not None:
        k_pre = args[n_mm - 1].shape[1] if trans_rhs else args[n_mm - 1].shape[0]
        out_specs.append(pl.BlockSpec((tm, k_pre), lambda j, i: (i, 0)))
        out_shape.append(jax.ShapeDtypeStruct((m, k_pre), BF16))
    for o in outs:
        dt, width = o if isinstance(o, tuple) else (o, n)
        out_specs.append(pl.BlockSpec((tm, tn), lambda j, i: (i, j)) if width == n
                         else pl.BlockSpec((tm, width), lambda j, i: (i, 0)))
        out_shape.append(jax.ShapeDtypeStruct((m, width), dt))
    n_tiles_out = len(out_specs)
    out_specs += [pl.BlockSpec((1, tn), lambda j, i: (0, j))] * n_sums
    out_shape += [jax.ShapeDtypeStruct((1, n), F32)] * n_sums

    def body(*refs):
        ins = refs[:n_mm + n_pre + n_ex + n_vec]
        out_refs = refs[n_mm + n_pre + n_ex + n_vec:]
        vc = [r[...] for r in ins[n_mm + n_pre + n_ex:]]
        vals = []
        made = None
        if pre is not None:
            made = pre([r[...] for r in ins[n_mm:n_mm + n_pre]], vc).astype(BF16)
            vals.append(made)
        accs, pos, pair = [], 0, 0
        for size in sizes:
            acc = None
            for _ in range(size):
                if uses_pre[pair]:
                    lhs_tile = made
                else:
                    lhs_tile = ins[pos][...].astype(BF16)
                    pos += 1
                part = lax.dot_general(lhs_tile, ins[pos][...].astype(BF16), dims, preferred_element_type=F32)
                acc = part if acc is None else acc + part
                pos += 1
                pair += 1
            accs.append(acc)
        ex = [r[...] for r in ins[n_mm + n_pre:n_mm + n_pre + n_ex]]
        vals += epi(accs, ex, vc)
        for ref, val in zip(out_refs[:n_tiles_out], vals):
            ref[...] = val.astype(ref.dtype)
        if n_sums:
            @pl.when(pl.program_id(1) == 0)
            def _():
                for ref in out_refs[n_tiles_out:]:
                    ref[...] = jnp.zeros_like(ref)
            for ref, val in zip(out_refs[n_tiles_out:], vals[n_tiles_out:]):
                ref[...] += val

    return _call(body, grid=(n // tn, m // tm), in_specs=in_specs, out_specs=out_specs, out_shape=out_shape,
                 args=args, name=name, comm=comm)


def _mm_tn(lhs, rhs, name, comm=None):
    t, a = lhs.shape
    b = rhs.shape[1]
    ta = a if a <= 1536 else _tile(a, 1536, LANES)
    tk = _tile(t, 2048, 8)

    def body(l_ref, r_ref, o_ref):
        @pl.when(pl.program_id(1) == 0)
        def _():
            o_ref[...] = jnp.zeros_like(o_ref)
        o_ref[...] += lax.dot_general(l_ref[...].astype(BF16), r_ref[...].astype(BF16), (((0,), (0,)), ((), ())),
                                      preferred_element_type=F32)

    res = _call(body, grid=(a // ta, t // tk),
                in_specs=[pl.BlockSpec((tk, ta), lambda i, k: (k, i)), pl.BlockSpec((tk, b), lambda i, k: (k, 0))],
                out_specs=[pl.BlockSpec((ta, b), lambda i, k: (i, 0))], out_shape=[jax.ShapeDtypeStruct((a, b), F32)],
                args=(lhs, rhs), name=name, comm=comm)
    return res[0] if comm is None else (res[0][0], res[1])


def _ffn_tn(f):
    return _tile(f, 1536, LANES)


def _ffn_up(n, wg_t, wu_t, name, comm=None):
    def epi(accs, ex, vc):
        a, b = accs
        return [a, b, (a * _sigmoid(a)) * b]
    return _mm([[(n, wg_t)], [(n, wu_t)]], epi, [], [], [BF16, BF16, BF16], trans_rhs=True, tm=512,
               tn=_ffn_tn(wg_t.shape[0]), name=name, comm=comm)


def _norm_gate(h, gain, scale, shift, wg_t, name, comm=None):
    def pre(tiles, vc):
        _, xn = _rms_stats(tiles[0])
        return (xn * vc[0]) * (1.0 + vc[1]) + vc[2]

    def epi(accs, ex, vc):
        return [accs[0]]
    return _mm([[(None, wg_t)]], epi, [], [gain, scale, shift], [BF16], trans_rhs=True, tm=512,
               tn=wg_t.shape[0], name=name, pre=pre, pre_inputs=[h], comm=comm)


def _mix_out(attn, u1, post, w, res, gate, norm, name):
    def pre(tiles, vc):
        return _mix_post(tiles[0], tiles[1], *vc[4:8])

    def epi(accs, ex, vc):
        h = ex[0] + vc[0] * accs[0]
        _, xn = _rms_stats(h)
        return [h, accs[0], (xn * vc[1]) * (1.0 + vc[2]) + vc[3]]
    return _mm([[(None, w)]], epi, [res], [gate] + list(norm) + list(post), [F32, BF16, BF16], trans_rhs=False,
               tm=512, tn=w.shape[1], name=name, pre=pre, pre_inputs=[attn, u1])


def _mix_dy_post_bwd(dmix, w, attn, u1, post, name):
    width = attn.shape[1]

    def epi(accs, ex, vc):
        dattn, du1, sums = _mix_post_back(accs[0], ex[0], ex[1], *vc)
        return [dattn, du1, jnp.concatenate(sums[0:2], axis=1), jnp.concatenate(sums[2:4], axis=1)]
    return _mm([[(dmix, w)]], epi, [attn, u1], list(post), [(F32, width), (F32, width)], trans_rhs=True, tm=256,
               tn=w.shape[0], name=name, n_sums=2)


def _ffn_up_given_gate(n, wu_t, a, name, comm=None):
    def epi(accs, ex, vc):
        av = ex[0].astype(F32)
        return [accs[0], (av * _sigmoid(av)) * accs[0]]
    return _mm([[(n, wu_t)]], epi, [a], [], [BF16, BF16], trans_rhs=True, tm=512, tn=_ffn_tn(wu_t.shape[0]),
               name=name, comm=comm)


def _residual_mm(lhs, w, res, gate, coef, name, norm=None, comm=None):
    def epi(accs, ex, vc):
        h = ex[0] + (coef * vc[0]) * accs[0]
        if norm is None:
            return [h, accs[0]]
        _, xn = _rms_stats(h)
        return [h, accs[0], (xn * vc[1]) * (1.0 + vc[2]) + vc[3]]
    vecs = [gate] + (list(norm) if norm is not None else [])
    outs = [F32, BF16] + ([BF16] if norm is not None else [])
    return _mm([[(lhs, w)]], epi, [res], vecs, outs, trans_rhs=False, tm=512, tn=w.shape[1], name=name, comm=comm)


def _ffn_bwd_hidden(df, wd, a, b, name, comm=None):
    def epi(accs, ex, vc):
        dh = accs[0]
        av, bv = ex[0].astype(F32), ex[1].astype(F32)
        sig = _sigmoid(av)
        silu = av * sig
        return [dh * bv * (sig + silu * (1.0 - sig)), dh * silu]
    return _mm([[(df, wd)]], epi, [a, b], [], [BF16, BF16], trans_rhs=True, tm=512, tn=_ffn_tn(wd.shape[0]),
               name=name, comm=comm)


def _plain_mm(pairs, out_dtype, trans_rhs, tn, name, tm=512, comm=None):
    def epi(accs, ex, vc):
        return [accs[0]]
    res = _mm([pairs], epi, [], [], [out_dtype], trans_rhs=trans_rhs, tm=tm, tn=tn, name=name, comm=comm)
    return res[0] if comm is None else (res[0][0], res[1])


HEADS_PER_TILE = LANES // HEAD_DIM


def _stack_heads(x):
    lane = lax.broadcasted_iota(jnp.int32, (1, LANES), 1)
    return jnp.concatenate([x * (lane // HEAD_DIM == h).astype(F32) for h in range(HEADS_PER_TILE)], axis=0)


def _unstack_heads(y):
    r = y.shape[0] // HEADS_PER_TILE
    lane = lax.broadcasted_iota(jnp.int32, (r, y.shape[1]), 1)
    out = y[0:r]
    for h in range(1, HEADS_PER_TILE):
        out = jnp.where(lane // HEAD_DIM == h, y[h * r:(h + 1) * r], out)
    return out


def _stacked_lse(lb):
    return jnp.concatenate([_lane_pick(lb, h) for h in range(HEADS_PER_TILE)], axis=0)


def _band_masks(n_row_blocks, n_col_blocks):
    shape = (n_row_blocks * BLOCK, n_col_blocks * BLOCK)
    qi = lax.broadcasted_iota(jnp.int32, shape, 0) % BLOCK
    kj = lax.broadcasted_iota(jnp.int32, shape, 1) % BLOCK
    return kj <= qi, kj >= qi


def _query_masks():
    same_ok, before_ok = _band_masks(HEADS_PER_TILE, 2)
    is_cur = lax.broadcasted_iota(jnp.int32, same_ok.shape, 1) >= BLOCK
    return jnp.logical_and(is_cur, same_ok), jnp.logical_and(jnp.logical_not(is_cur), before_ok)


def _dot_nt(a, b):
    return lax.dot_general(a.astype(BF16), b.astype(BF16), (((1,), (1,)), ((), ())), preferred_element_type=F32)


def _dot_nn(a, b):
    return lax.dot_general(a.astype(BF16), b.astype(BF16), (((1,), (0,)), ((), ())), preferred_element_type=F32)


def _dot_tn(a, b):
    return lax.dot_general(a.astype(BF16), b.astype(BF16), (((0,), (0,)), ((), ())), preferred_element_type=F32)


def _lane_pick(x, h):
    lane = lax.broadcasted_iota(jnp.int32, x.shape, 1)
    return jnp.sum(jnp.where(lane == h * HEAD_DIM, x, 0.0), axis=1, keepdims=True)


def _block_rows(idx, d):
    span = BLOCK * d
    g = idx // d
    q0 = g * span + idx % d
    has_prev = g > 0
    p0 = jnp.where(has_prev, q0 - span, q0)
    return pl.ds(q0, BLOCK, stride=d), pl.ds(p0, BLOCK, stride=d), has_prev


def _qkv_specs(s, tiles):
    q, k, v = [pl.BlockSpec((s, LANES), functools.partial(lambda hb, off: (0, off + hb), off=i * tiles))
               for i in range(3)]
    return q, k, v, pl.BlockSpec((s, LANES), lambda hb: (0, hb))


def _attn_seq_fwd(proj, width, name, comm=None):
    s = proj.shape[0]
    q_spec, k_spec, v_spec, cur = _qkv_specs(s, width // LANES)

    def body(q_ref, k_ref, v_ref, o_ref, l_ref, o_s, l_s):
        cur_valid, prev_valid = _query_masks()
        for bi, d in enumerate(DILATIONS):
            def blk(idx, carry, bi=bi, d=d):
                rows, prev, has_prev = _block_rows(idx, d)
                q2 = _stack_heads(q_ref[rows, :])
                keys = jnp.concatenate([k_ref[prev, :], k_ref[rows, :]], axis=0)
                vals = jnp.concatenate([v_ref[prev, :], v_ref[rows, :]], axis=0)
                valid = jnp.logical_or(cur_valid, jnp.logical_and(prev_valid, has_prev))
                sc = jnp.where(valid, _dot_nt(q2, keys), NEG)
                mx = jnp.max(sc, axis=1, keepdims=True)
                p = jnp.exp(sc - mx)
                den = jnp.sum(p, axis=1, keepdims=True)
                o_s[bi, rows, :] = _unstack_heads(_dot_nn(p, vals) / den)
                l_s[bi, rows, :] = _unstack_heads(jnp.broadcast_to(mx + jnp.log(den), (q2.shape[0], LANES)))
                return carry

            lax.fori_loop(0, s // BLOCK, blk, 0, unroll=8)
        for c in range(s // MERGE_CHUNK):
            rows = slice(c * MERGE_CHUNK, (c + 1) * MERGE_CHUNK)
            ls = [l_s[bi, rows, :] for bi in range(len(DILATIONS))]
            top = functools.reduce(jnp.maximum, ls)
            ws = [jnp.exp(l - top) for l in ls]
            den = functools.reduce(lambda a, b: a + b, ws)
            num = functools.reduce(lambda a, b: a + b, [w * o_s[bi, rows, :] for bi, w in enumerate(ws)])
            o_ref[rows, :] = num / den
            l_ref[rows, :] = top + jnp.log(den)

    return _call(
        body, grid=(width // LANES,), in_specs=[q_spec, k_spec, v_spec], out_specs=[cur, cur],
        out_shape=[jax.ShapeDtypeStruct((s, width), F32)] * 2,
        scratch_shapes=[pltpu.VMEM((len(DILATIONS), s, LANES), F32)] * 2,
        args=(proj, proj, proj), name=name, comm=comm)


def _attn_seq_bwd(proj, do, o, lse, cos, sin_signed, name, comm=None):
    s, width = do.shape
    q_spec, k_spec, v_spec, cur = _qkv_specs(s, width // LANES)
    table = pl.BlockSpec((s, LANES), lambda hb: (0, 0))
    qscale = HEAD_DIM ** -0.5

    def body(q_ref, k_ref, v_ref, do_ref, o_ref, l_ref, cos_ref, sin_ref, dq_out, dk_out, dv_out,
             dq_ref, dk_ref, dv_ref):
        dq_ref[...] = jnp.zeros_like(dq_ref)
        dk_ref[...] = jnp.zeros_like(dk_ref)
        dv_ref[...] = jnp.zeros_like(dv_ref)
        cur_valid, prev_valid = _query_masks()
        for d in DILATIONS:
            def blk(idx, carry, d=d):
                rows, prev, has_prev = _block_rows(idx, d)
                dob = do_ref[rows, :]
                q2 = _stack_heads(q_ref[rows, :])
                do2 = _stack_heads(dob)
                delta = jnp.sum(_stack_heads(dob * o_ref[rows, :]), axis=1, keepdims=True)
                lse2 = _stacked_lse(l_ref[rows, :])
                keys = jnp.concatenate([k_ref[prev, :], k_ref[rows, :]], axis=0)
                vals = jnp.concatenate([v_ref[prev, :], v_ref[rows, :]], axis=0)
                valid = jnp.logical_or(cur_valid, jnp.logical_and(prev_valid, has_prev))
                p = jnp.where(valid, jnp.exp(_dot_nt(q2, keys) - lse2), 0.0)
                ds = p * (_dot_nt(do2, vals) - delta)
                dq_ref[rows, :] += _unstack_heads(_dot_nn(ds, keys))
                dkk = _dot_tn(ds, q2)
                dvv = _dot_tn(p, do2)
                dk_ref[prev, :] += dkk[0:BLOCK]
                dk_ref[rows, :] += dkk[BLOCK:]
                dv_ref[prev, :] += dvv[0:BLOCK]
                dv_ref[rows, :] += dvv[BLOCK:]
                return carry

            lax.fori_loop(0, s // BLOCK, blk, 0, unroll=4)
        for c in range(s // MERGE_CHUNK):
            rows = slice(c * MERGE_CHUNK, (c + 1) * MERGE_CHUNK)
            cos, sin = cos_ref[rows, :], sin_ref[rows, :]
            dq, dk = dq_ref[rows, :], dk_ref[rows, :]
            dq_out[rows, :] = ((dq * cos - _partner(dq) * sin) * qscale).astype(BF16)
            dk_out[rows, :] = (dk * cos - _partner(dk) * sin).astype(BF16)
            dv_out[rows, :] = dv_ref[rows, :].astype(BF16)

    return _call(
        body, grid=(width // LANES,), in_specs=[q_spec, k_spec, v_spec, cur, cur, cur, table, table],
        out_specs=[cur, cur, cur], out_shape=[jax.ShapeDtypeStruct((s, width), BF16)] * 3,
        scratch_shapes=[pltpu.VMEM((s, LANES), F32)] * 3,
        args=(proj, proj, proj, do, o, lse, cos, sin_signed), name=name, comm=comm)


def _conv_specs(s, a_block, b_block):
    per = CONV_CHUNK // CONV_HALO
    a_cur = pl.BlockSpec((CONV_CHUNK, LANES), lambda cb, i: (i, a_block + cb))
    b_cur = pl.BlockSpec((CONV_CHUNK, LANES), lambda cb, i: (i, b_block + cb))
    a_halo = pl.BlockSpec((CONV_HALO, LANES), lambda cb, i: (jnp.maximum(i * per - 1, 0), a_block + cb))
    b_halo = pl.BlockSpec((CONV_HALO, LANES), lambda cb, i: (jnp.maximum(i * per - 1, 0), b_block + cb))
    w_spec = pl.BlockSpec((CONV_KERNEL, LANES), lambda cb, i: (0, cb))
    vec = pl.BlockSpec((1, LANES), lambda cb, i: (0, cb))
    out = pl.BlockSpec((CONV_CHUNK, LANES), lambda cb, i: (i, cb))
    return a_cur, b_cur, a_halo, b_halo, w_spec, vec, out


def _fill_glu_window(win, a_ref, b_ref, ah_ref, bh_ref, first):
    halo = ah_ref[...] * _sigmoid(bh_ref[...])
    win[0:CONV_HALO, :] = jnp.where(first, 0.0, halo)
    win[CONV_HALO:, :] = a_ref[...] * _sigmoid(b_ref[...])


def _conv_fwd(proj, a_block, b_block, w, bias, name):
    s = proj.shape[0]
    cw = w.shape[1]
    a_cur, b_cur, a_halo, b_halo, w_spec, vec, out = _conv_specs(s, a_block, b_block)
    lead = CONV_HALO - (CONV_KERNEL - 1)

    def body(a_ref, b_ref, ah_ref, bh_ref, w_ref, bias_ref, o_ref, win):
        _fill_glu_window(win, a_ref, b_ref, ah_ref, bh_ref, pl.program_id(1) == 0)
        for sub in range(CONV_CHUNK // CONV_SUB):
            base = sub * CONV_SUB
            acc = jnp.zeros((CONV_SUB, LANES), F32) + bias_ref[...]
            for j in range(CONV_KERNEL):
                acc = acc + w_ref[j:j + 1, :] * win[base + lead + j:base + lead + j + CONV_SUB, :]
            o_ref[base:base + CONV_SUB, :] = acc

    return pl.pallas_call(
        body, grid=(cw // LANES, s // CONV_CHUNK), in_specs=[a_cur, b_cur, a_halo, b_halo, w_spec, vec],
        out_specs=out, out_shape=jax.ShapeDtypeStruct((s, cw), F32),
        scratch_shapes=[pltpu.VMEM((CONV_CHUNK + CONV_HALO, LANES), F32)],
        compiler_params=_params(2), name=name)(proj, proj, proj, proj, w, bias)


def _conv_bwd(proj, a_block, b_block, w, du1, name):
    s = proj.shape[0]
    cw = w.shape[1]
    a_cur, b_cur, a_halo, b_halo, w_spec, vec, out = _conv_specs(s, a_block, b_block)
    per = CONV_CHUNK // CONV_HALO
    n_chunks = s // CONV_CHUNK
    d_next = pl.BlockSpec((CONV_HALO, LANES), lambda cb, i: (jnp.minimum((i + 1) * per, s // CONV_HALO - 1), cb))
    lead = CONV_HALO - (CONV_KERNEL - 1)

    def body(a_ref, b_ref, ah_ref, bh_ref, w_ref, d_ref, dn_ref, da_ref, db_ref, dw_ref, dbias_ref, win, dwin):
        i = pl.program_id(1)
        _fill_glu_window(win, a_ref, b_ref, ah_ref, bh_ref, i == 0)
        dwin[0:CONV_CHUNK, :] = d_ref[...]
        dwin[CONV_CHUNK:, :] = jnp.where(i == n_chunks - 1, 0.0, dn_ref[...])

        @pl.when(i == 0)
        def _():
            dw_ref[...] = jnp.zeros_like(dw_ref)
            dbias_ref[...] = jnp.zeros_like(dbias_ref)

        dbias_ref[...] += _colsum(d_ref[...])
        for sub in range(CONV_CHUNK // CONV_SUB):
            base = sub * CONV_SUB
            dcur = dwin[base:base + CONV_SUB, :]
            du0 = jnp.zeros((CONV_SUB, LANES), F32)
            for j in range(CONV_KERNEL):
                back = CONV_KERNEL - 1 - j
                du0 = du0 + w_ref[j:j + 1, :] * dwin[base + back:base + back + CONV_SUB, :]
                dw_ref[j:j + 1, :] += _colsum(dcur * win[base + lead + j:base + lead + j + CONV_SUB, :])
            av = a_ref[base:base + CONV_SUB, :]
            sig = _sigmoid(b_ref[base:base + CONV_SUB, :])
            da_ref[base:base + CONV_SUB, :] = (du0 * sig).astype(BF16)
            db_ref[base:base + CONV_SUB, :] = (du0 * av * sig * (1.0 - sig)).astype(BF16)

    return pl.pallas_call(
        body, grid=(cw // LANES, n_chunks), in_specs=[a_cur, b_cur, a_halo, b_halo, w_spec, out, d_next],
        out_specs=[out, out, w_spec, vec],
        out_shape=[jax.ShapeDtypeStruct((s, cw), BF16), jax.ShapeDtypeStruct((s, cw), BF16),
                   jax.ShapeDtypeStruct((CONV_KERNEL, cw), F32), jax.ShapeDtypeStruct((1, cw), F32)],
        scratch_shapes=[pltpu.VMEM((CONV_CHUNK + CONV_HALO, LANES), F32)] * 2,
        compiler_params=_params(2), name=name)(proj, proj, proj, proj, w, du1, du1)


def _adamw_math(w, g, m, v):
    m = ADAM_B1 * m + (1.0 - ADAM_B1) * g
    v = ADAM_B2 * v + (1.0 - ADAM_B2) * (g * g)
    m_hat = m / (1.0 - ADAM_B1 ** ADAM_STEP)
    v_hat = v / (1.0 - ADAM_B2 ** ADAM_STEP)
    delta = -ADAM_LR * (m_hat / (jnp.sqrt(v_hat) + ADAM_EPS) + ADAM_WD * w)
    return delta, m, v


def _adamw_big(w, g, m, v, name):
    rows, cols = w.shape
    tile = _tile(rows, 256, 8)
    spec = pl.BlockSpec((tile, cols), lambda i: (i, 0))

    def body(w_ref, g_ref, m_ref, v_ref, d_out, m_out, v_out):
        d_out[...], m_out[...], v_out[...] = _adamw_math(w_ref[...], g_ref[...], m_ref[...], v_ref[...])

    return pl.pallas_call(body, grid=(rows // tile,), in_specs=[spec] * 4, out_specs=[spec] * 3,
                          out_shape=[jax.ShapeDtypeStruct(w.shape, F32)] * 3, compiler_params=_params(1),
                          name=name)(w, g, m, v)


def _adamw_reduced(w, land, m, v, name):
    rows, cols = w.shape
    tile = _tile(rows, 256, 16)
    spec = pl.BlockSpec((tile, cols), lambda i: (i, 0))

    def body(w_ref, l_ref, m_ref, v_ref, g_out, d_out, m_out, v_out):
        g = l_ref[0].astype(F32)
        for q in range(1, N_CHIP):
            g = g + l_ref[q].astype(F32)
        g_out[...] = g
        d_out[...], m_out[...], v_out[...] = _adamw_math(w_ref[...], g, m_ref[...], v_ref[...])

    return pl.pallas_call(body, grid=(rows // tile,),
                          in_specs=[spec, pl.BlockSpec((N_CHIP, tile, cols), lambda i: (0, i, 0)), spec, spec],
                          out_specs=[spec] * 4, out_shape=[jax.ShapeDtypeStruct(w.shape, F32)] * 4,
                          compiler_params=_params(1), name=name)(w, land, m, v)


def _adamw_small(ws, gs, ms, vs, name):
    n = len(ws)

    def body(*refs):
        ins, outs = refs[:4 * n], refs[4 * n:]
        for t in range(n):
            res = _adamw_math(ins[t][...], ins[n + t][...], ins[2 * n + t][...], ins[3 * n + t][...])
            for j in range(3):
                outs[j * n + t][...] = res[j]

    shapes = [jax.ShapeDtypeStruct(w.shape, F32) for w in ws]
    res = pl.pallas_call(body, out_shape=shapes * 3, compiler_params=pltpu.CompilerParams(vmem_limit_bytes=VMEM_LIMIT),
                         name=name)(*ws, *gs, *ms, *vs)
    return res[:n], res[n:2 * n], res[2 * n:]


def _sum_blocks(x, n_blocks, name):
    r = x.shape[0] // n_blocks

    def body(x_ref, o_ref):
        acc = x_ref[0:r, :]
        for b in range(1, n_blocks):
            acc = acc + x_ref[b * r:(b + 1) * r, :]
        o_ref[...] = acc

    return pl.pallas_call(body, out_shape=jax.ShapeDtypeStruct((r, x.shape[1]), F32),
                          compiler_params=pltpu.CompilerParams(vmem_limit_bytes=VMEM_LIMIT), name=name)(x)


def _coords():
    return lax.axis_index("x"), lax.axis_index("y"), lax.axis_index("c")


def _flip(v, bit):
    return 1 - v if bit else v


def _ag_small(x, name):
    r, c = x.shape

    def body(x_ref, o_ref, send, recv, local_sem):
        mx, my, mc = _coords()

        def rows(px, py, pc):
            return o_ref.at[pl.ds(pl.multiple_of((4 * px + 2 * py + pc) * r, 8), r), :]

        local = pltpu.make_async_copy(x_ref, rows(mx, my, mc), local_sem)
        local.start()
        peers = [(_flip(mx, k >> 2 & 1), _flip(my, k >> 1 & 1), _flip(mc, k & 1)) for k in range(1, N_DEV)]
        sends = [pltpu.make_async_remote_copy(x_ref, rows(mx, my, mc), send.at[k], recv.at[k], device_id=p,
                                              device_id_type=MESH) for k, p in enumerate(peers)]
        for cp in sends:
            cp.start()
        for k, p in enumerate(peers):
            pltpu.make_async_remote_copy(x_ref, rows(*p), send.at[k], recv.at[k], device_id=p,
                                         device_id_type=MESH).wait_recv()
        for cp in sends:
            cp.wait_send()
        local.wait()

    vm = pl.BlockSpec(memory_space=pltpu.VMEM)
    return pl.pallas_call(
        body, in_specs=[vm], out_specs=vm, out_shape=jax.ShapeDtypeStruct((N_DEV * r, c), x.dtype),
        scratch_shapes=[pltpu.SemaphoreType.DMA((N_DEV - 1,)), pltpu.SemaphoreType.DMA((N_DEV - 1,)),
                        pltpu.SemaphoreType.DMA(())],
        name=name)(x)


class _GatherSmall:
    mid = None

    def __init__(self, x):
        self.inputs = [x]
        self.out_shapes = [jax.ShapeDtypeStruct((N_DEV * x.shape[0], x.shape[1]), x.dtype)]
        self.scratch = [pltpu.SemaphoreType.DMA((N_DEV - 1,)), pltpu.SemaphoreType.DMA((N_DEV - 1,)),
                        pltpu.SemaphoreType.DMA(())]

    def _plan(self, x_refs, o_refs, sems):
        send, recv, local_sem = sems
        x_ref, o_ref = x_refs[0], o_refs[0]
        r = x_ref.shape[0]
        mx, my, mc = _coords()

        def rows(px, py, pc):
            return o_ref.at[pl.ds(pl.multiple_of((4 * px + 2 * py + pc) * r, 8), r), :]

        peers = [(_flip(mx, k >> 2 & 1), _flip(my, k >> 1 & 1), _flip(mc, k & 1)) for k in range(1, N_DEV)]
        out = [pltpu.make_async_remote_copy(x_ref, rows(mx, my, mc), send.at[k], recv.at[k], device_id=p,
                                            device_id_type=MESH) for k, p in enumerate(peers)]
        arrivals = [pltpu.make_async_remote_copy(x_ref, rows(*p), send.at[k], recv.at[k], device_id=p,
                                                 device_id_type=MESH) for k, p in enumerate(peers)]
        return out, arrivals, pltpu.make_async_copy(x_ref, rows(mx, my, mc), local_sem)

    def start(self, x_refs, o_refs, sems):
        out, _, local = self._plan(x_refs, o_refs, sems)
        local.start()
        for cp in out:
            cp.start()

    def finish(self, x_refs, o_refs, sems):
        out, arrivals, local = self._plan(x_refs, o_refs, sems)
        for cp in arrivals:
            cp.wait_recv()
        for cp in out:
            cp.wait_send()
        local.wait()


class _GatherWeights:
    def __init__(self, shards):
        n_t = len(shards)
        self.inputs = list(shards)
        self.out_shapes = [jax.ShapeDtypeStruct((N_DEV * x.shape[0], x.shape[1]), x.dtype) for x in shards]
        self.scratch = [pltpu.SemaphoreType.DMA((n_t, 7)), pltpu.SemaphoreType.DMA((n_t, 7)),
                        pltpu.SemaphoreType.DMA((n_t,))]

    def _plan(self, x_refs, o_refs, sems):
        send, recv, local_sem = sems
        mx, my, mc = _coords()
        me, sibling = (mx, my, mc), (mx, my, 1 - mc)
        chips = [(1 - mx, my), (mx, 1 - my), (1 - mx, 1 - my)]

        def rows(t, px, py, pc):
            r = x_refs[t].shape[0]
            return o_refs[t].at[pl.ds(pl.multiple_of((4 * px + 2 * py + pc) * r, 8), r), :]

        def copy(t, k, block, to, src=None):
            return pltpu.make_async_remote_copy(
                src_ref=rows(t, *block) if src is None else src, dst_ref=rows(t, *block),
                send_sem=send.at[t, k], recv_sem=recv.at[t, k], device_id=to, device_id_type=MESH)

        def local(t):
            return pltpu.make_async_copy(x_refs[t], rows(t, *me), local_sem.at[t])

        return me, sibling, chips, mc, copy, local

    def start(self, x_refs, o_refs, sems):
        me, sibling, chips, mc, copy, local = self._plan(x_refs, o_refs, sems)
        for t in range(len(x_refs)):
            local(t).start()
            copy(t, 0, me, sibling, src=x_refs[t]).start()
            for j, chip in enumerate(chips):
                copy(t, 1 + j, me, (*chip, mc), src=x_refs[t]).start()

    def mid(self, x_refs, o_refs, sems):
        me, sibling, chips, mc, copy, local = self._plan(x_refs, o_refs, sems)
        for j, chip in enumerate(chips):
            for t in range(len(x_refs)):
                copy(t, 1 + j, (*chip, mc), me).wait_recv()
                copy(t, 4 + j, (*chip, mc), sibling).start()

    def finish(self, x_refs, o_refs, sems):
        me, sibling, chips, mc, copy, local = self._plan(x_refs, o_refs, sems)
        for t in range(len(x_refs)):
            copy(t, 0, sibling, me).wait_recv()
            for j, chip in enumerate(chips):
                copy(t, 4 + j, (*chip, 1 - mc), me).wait_recv()
            copy(t, 0, me, sibling, src=x_refs[t]).wait_send()
            for j, chip in enumerate(chips):
                copy(t, 1 + j, me, (*chip, mc), src=x_refs[t]).wait_send()
                copy(t, 4 + j, (*chip, mc), sibling).wait_send()
            local(t).wait()


class _SiblingExchange:
    mid = None

    def __init__(self, grads):
        n_t = len(grads)
        self.inputs = list(grads)
        self.out_shapes = [jax.ShapeDtypeStruct((N_CHIP,) + g.shape[2:], F32) for g in grads]
        self.scratch = [pltpu.SemaphoreType.DMA((n_t,)), pltpu.SemaphoreType.DMA((n_t,))]

    def _copies(self, g_refs, land, sems):
        send, recv = sems
        mx, my, mc = _coords()
        return [pltpu.make_async_remote_copy(g_refs[t].at[:, 1 - mc], land[t], send.at[t], recv.at[t],
                                             device_id=(mx, my, 1 - mc), device_id_type=MESH)
                for t in range(len(g_refs))]

    def start(self, g_refs, land, sems):
        for cp in self._copies(g_refs, land, sems):
            cp.start()

    def finish(self, g_refs, land, sems):
        for cp in self._copies(g_refs, land, sems):
            cp.wait()


class _Together:
    def __init__(self, *comms):
        self.comms = comms
        self.inputs = [x for c in comms for x in c.inputs]
        self.out_shapes = [x for c in comms for x in c.out_shapes]
        self.scratch = [x for c in comms for x in c.scratch]
        self.mid = self._mid if any(c.mid is not None for c in comms) else None

    def _each(self, phase, cin, cout, sems):
        i = o = s = 0
        for c in self.comms:
            fn = getattr(c, phase)
            ni, no, ns = len(c.inputs), len(c.out_shapes), len(c.scratch)
            if fn is not None:
                fn(cin[i:i + ni], cout[o:o + no], sems[s:s + ns])
            i, o, s = i + ni, o + no, s + ns

    def start(self, cin, cout, sems):
        self._each("start", cin, cout, sems)

    def _mid(self, cin, cout, sems):
        self._each("mid", cin, cout, sems)

    def finish(self, cin, cout, sems):
        self._each("finish", cin, cout, sems)


def _standalone(comm, name):
    def body():
        pass
    return _call(body, grid=(1,), in_specs=[], out_specs=[], out_shape=[], args=(), name=name, comm=comm)[1]


def _chip_partials(g4s, lands, name):
    n_t = len(g4s)
    in_specs, out_specs, out_shape = [], [], []
    for g4 in g4s:
        _, _, r, c = g4.shape
        in_specs.append(pl.BlockSpec((None, None, r, c), lambda q: (q, lax.axis_index("c"), 0, 0)))
        out_specs.append(pl.BlockSpec((None, r, c), lambda q: (q, 0, 0)))
        out_shape.append(jax.ShapeDtypeStruct((N_CHIP, r, c), BF16))
    in_specs += [pl.BlockSpec((None,) + g4.shape[2:], lambda q: (q, 0, 0)) for g4 in g4s]

    def body(*refs):
        for t in range(n_t):
            refs[2 * n_t + t][...] = (refs[t][...] + refs[n_t + t][...]).astype(BF16)

    return pl.pallas_call(body, grid=(N_CHIP,), in_specs=in_specs, out_specs=out_specs, out_shape=out_shape,
                          compiler_params=_params(1), name=name)(*g4s, *lands)


class _ChipExchange:
    mid = None

    def __init__(self, parts):
        n_t = len(parts)
        self.inputs = list(parts)
        self.out_shapes = [jax.ShapeDtypeStruct(p.shape, p.dtype) for p in parts]
        self.scratch = [pltpu.SemaphoreType.DMA((n_t, 3)), pltpu.SemaphoreType.DMA((n_t, 3)),
                        pltpu.SemaphoreType.DMA((n_t,))]

    def _plan(self, p_refs, land, sems):
        send, recv, local_sem = sems
        mx, my, mc = _coords()
        my_chip = 2 * mx + my
        peers = [(_flip(mx, fx), _flip(my, fy)) for fx, fy in ((1, 0), (0, 1), (1, 1))]

        def out(t, k):
            px, py = peers[k]
            return pltpu.make_async_remote_copy(p_refs[t].at[2 * px + py], land[t].at[my_chip], send.at[t, k],
                                                recv.at[t, k], device_id=(px, py, mc), device_id_type=MESH)

        def arrival(t, k):
            px, py = peers[k]
            return pltpu.make_async_remote_copy(p_refs[t].at[my_chip], land[t].at[2 * px + py], send.at[t, k],
                                                recv.at[t, k], device_id=(px, py, mc), device_id_type=MESH)

        def local(t):
            return pltpu.make_async_copy(p_refs[t].at[my_chip], land[t].at[my_chip], local_sem.at[t])

        return out, arrival, local

    def start(self, p_refs, land, sems):
        out, arrival, local = self._plan(p_refs, land, sems)
        for t in range(len(p_refs)):
            local(t).start()
            for k in range(3):
                out(t, k).start()

    def finish(self, p_refs, land, sems):
        out, arrival, local = self._plan(p_refs, land, sems)
        for t in range(len(p_refs)):
            for k in range(3):
                arrival(t, k).wait_recv()
                out(t, k).wait_send()
            local(t).wait()


def _rope_tables(s, width):
    heads = width // HEAD_DIM
    inv_freq = ROPE_THETA ** (-jnp.arange(0, HEAD_DIM, 2, dtype=F32) / HEAD_DIM)
    inv_full = jnp.tile(inv_freq, 2 * heads)
    sign = jnp.tile(jnp.concatenate([-jnp.ones((HALF_HEAD,), F32), jnp.ones((HALF_HEAD,), F32)]), heads)
    ang = jnp.arange(s, dtype=F32)[:, None] * inv_full[None, :]
    return jnp.cos(ang), jnp.sin(ang) * sign[None, :]


def _pad_rows(v, rows):
    return jnp.concatenate([v, jnp.zeros((rows - 1, v.shape[1]), v.dtype)], axis=0)


def kernel(x, c, w_ada, b_ada, ffn1_norm_g, ffn1_w_gate, ffn1_w_up, ffn1_w_down, mix_norm_g, w_in, conv_dw_w, conv_dw_b, conv_ln_g, conv_ln_b, attn_out_g, conv_out_g, w_out, ffn2_norm_g, ffn2_w_gate, ffn2_w_up, ffn2_w_down, final_norm_g, loss_target, m_w_ada, m_b_ada, m_ffn1_norm_g, m_ffn1_w_gate, m_ffn1_w_up, m_ffn1_w_down, m_mix_norm_g, m_w_in, m_conv_dw_w, m_conv_dw_b, m_conv_ln_g, m_conv_ln_b, m_attn_out_g, m_conv_out_g, m_w_out, m_ffn2_norm_g, m_ffn2_w_gate, m_ffn2_w_up, m_ffn2_w_down, m_final_norm_g, v_w_ada, v_b_ada, v_ffn1_norm_g, v_ffn1_w_gate, v_ffn1_w_up, v_ffn1_w_down, v_mix_norm_g, v_w_in, v_conv_dw_w, v_conv_dw_b, v_conv_ln_g, v_conv_ln_b, v_attn_out_g, v_conv_out_g, v_w_out, v_ffn2_norm_g, v_ffn2_w_gate, v_ffn2_w_up, v_ffn2_w_down, v_final_norm_g):
    mx, my, mc = _coords()
    me = 4 * mx + 2 * my + mc
    s, d = x.shape[1], x.shape[2]
    aw = d // 2
    x2, target = x[0], loss_target[0]
    n_mod = w_ada.shape[2] * N_DEV // d
    mod_cols = w_ada.shape[2]

    def shard(w, transpose):
        return (w[0].T if transpose else w[0]).astype(BF16)

    cw_shard = conv_dw_w.shape[3]
    n_taps = CONV_KERNEL * cw_shard
    first_len = -(-(d + n_taps) // LANES) * LANES
    first = jnp.concatenate([c, conv_dw_w[0, :, 0, :].reshape(1, n_taps), jnp.zeros((1, first_len - d - n_taps), F32)], axis=1)
    first_all, wg1 = _standalone(
        _Together(_GatherSmall(_pad_rows(first, 8)), _GatherWeights([shard(ffn1_w_gate, True)])), "ag_first")
    first_all = first_all[0::8]
    c_all = first_all[:, :d]
    conv_w = first_all[:, d:d + n_taps].reshape(N_DEV, CONV_KERNEL, cw_shard).transpose(1, 0, 2).reshape(CONV_KERNEL, aw)

    silu_c = _silu_rows(c_all, "silu_c")
    mod_part = _plain_mm([(silu_c, w_ada[0])], F32, False, mod_cols, "mod_mm")
    mod_all = _ag_small(mod_part, "ag_mod").reshape(N_DEV, N_DEV, mod_cols)
    mod = lax.dynamic_index_in_dim(mod_all, me, axis=1, keepdims=False).reshape(1, n_mod * d) + b_ada
    sh1, sc1, g1, sh2, sc2, g2, sh3, sc3, g3 = [mod[:, i * d:(i + 1) * d] for i in range(n_mod)]

    def split(g):
        return g.reshape(N_CHIP, 2, g.shape[0] // N_DEV, g.shape[1])

    def partials(g4s, lands, tag):
        return _chip_partials(g4s, lands, "chip_partials_" + tag)

    gather_late = _GatherWeights([shard(ffn2_w_gate, True), shard(ffn2_w_up, True), shard(ffn2_w_down, False),
                                  shard(w_out, False)])

    (n1, a1), (wu1,) = _norm_gate(x2, ffn1_norm_g, sc1, sh1, wg1, "ffn1_gate",
                                  comm=_GatherWeights([shard(ffn1_w_up, True)]))
    (b1, hid1), (wd1,) = _ffn_up_given_gate(n1, wu1, a1, "ffn1_up", comm=_GatherWeights([shard(ffn1_w_down, False)]))
    (h1, f1, n2), (win_t,) = _residual_mm(hid1, wd1, x2, g1, 0.5, "ffn1_down", norm=(mix_norm_g, sc2, sh2),
                                          comm=_GatherWeights([shard(w_in, True)]))
    cos, sin_signed = _rope_tables(s, LANES)
    proj = _proj_rope(n2, win_t, cos, sin_signed, aw, "proj")
    lanes_per = aw // LANES
    (attn, lse), (wg2, wu2, wd2, wout) = _attn_seq_fwd(proj, aw, "attn_fwd", comm=gather_late)
    u1 = _conv_fwd(proj, 3 * lanes_per, 4 * lanes_per, conv_w, conv_dw_b, "conv_fwd")
    post = (attn_out_g, conv_ln_g, conv_ln_b, conv_out_g)
    y, h2, mix, n3 = _mix_out(attn, u1, post, wout, h1, g2, (ffn2_norm_g, sc3, sh3), "mix_out")
    a3, b3, hid3 = _ffn_up(n3, wg2, wu2, "ffn2_up")

    dh3, df3, err2, d_final_g, dg3 = _last_mm_loss(hid3, wd2, h2, g3, 0.5, target, final_norm_g.reshape(1, d),
                                                   "ffn2_down_loss")
    loss_part = jnp.zeros((1, LANES), F32).at[0, 0].set(0.5 * jnp.sum(err2) / d)

    da3, db3 = _ffn_bwd_hidden(df3, wd2, a3, b3, "ffn2_hidden_bwd")
    g4_a = [split(_mm_tn(da3, n3, "ffn2_dwg")), split(_mm_tn(db3, n3, "ffn2_dwu")), split(_mm_tn(hid3, df3, "ffn2_dwd"))]
    (dh2, dmix, dsh3, dsc3, dgn3, dg2), land_a = _mm_norm_mod_bwd(
        [(da3, wg2), (db3, wu2)], h2, dh3, ffn2_norm_g, sc3, (mix, g2, 1.0), "ffn2_dn_norm3_bwd", tm=256,
        comm=_SiblingExchange(g4_a))
    parts_a = partials(g4_a, land_a, "a")
    g_wout = _mm_tn(y, dmix, "mix_dwout")
    dattn, du1, d_gains, d_ln = _mix_dy_post_bwd(dmix, wout, attn, u1, post, "mix_dy_post_bwd")
    d_attn_g, d_conv_g, d_ln_g, d_ln_b = d_gains[:, :aw], d_gains[:, aw:], d_ln[:, :aw], d_ln[:, aw:]
    dga, dgb, d_taps, d_conv_b = _conv_bwd(proj, 3 * lanes_per, 4 * lanes_per, conv_w, du1, "conv_bwd")
    (dq, dk, dv), sums_a = _attn_seq_bwd(proj, dattn, attn, lse, cos, sin_signed, "attn_bwd",
                                         comm=_ChipExchange(parts_a))
    dproj = jnp.concatenate([dq, dk, dv, dga, dgb], axis=1)
    g4_b = [split(g_wout), split(_mm_tn(dproj, n2, "mix_dwin"))]
    (dh1, df1, dsh2, dsc2, dgn2, dg1), land_b = _mm_norm_mod_bwd(
        [(dproj, win_t)], h1, dh2, mix_norm_g, sc2, (f1, g1, 0.5), "mix_dn_norm2_bwd", tm=512,
        comm=_SiblingExchange(g4_b))
    parts_b = partials(g4_b, land_b, "b")
    g4_c = [split(_mm_tn(hid1, df1, "ffn1_dwd"))]
    (da1, db1), both = _ffn_bwd_hidden(df1, wd1, a1, b1, "ffn1_hidden_bwd",
                                       comm=_Together(_ChipExchange(parts_b), _SiblingExchange(g4_c)))
    sums_b, land_c = both[:2], both[2:]
    parts_c = partials(g4_c, land_c, "c")
    g_wu1, sums_c = _mm_tn(db1, n1, "ffn1_dwu", comm=_ChipExchange(parts_c))
    g4_d = [split(g_wu1)]
    g_wg1, land_d = _mm_tn(da1, n1, "ffn1_dwg", comm=_SiblingExchange(g4_d))
    parts_d = partials(g4_d, land_d, "d")
    g4_e = [split(g_wg1)]
    dn1, both = _plain_mm([(da1, wg1), (db1, wu1)], BF16, False, d, "ffn1_dn",
                          comm=_Together(_ChipExchange(parts_d), _SiblingExchange(g4_e)))
    sums_d, land_e = both[:1], both[1:]
    parts_e = partials(g4_e, land_e, "e")
    (dx, dsh1, dsc1, dgn1), sums_e = _norm_mod_bwd(dn1, x2, dh1, ffn1_norm_g, sc1, "norm1_bwd",
                                                   comm=_ChipExchange(parts_e))

    dmod = jnp.concatenate([dsh1, dsc1, dg1, dsh2, dsc2, dg2, dsh3, dsc3, dg3], axis=1)
    small = [dmod, dgn1, dgn2, dgn3, d_final_g, d_conv_b, d_ln_g, d_ln_b, d_attn_g, d_conv_g,
             d_taps.reshape(1, CONV_KERNEL * aw), loss_part]
    sizes = [v.shape[1] for v in small]
    total = sum(sizes)
    padded = -(-total // (8 * LANES)) * (8 * LANES)
    packed = jnp.concatenate(small + [jnp.zeros((1, padded - total), F32)], axis=1).reshape(8, padded // 8)
    gathered = _ag_small(packed, "ag_small_grads")
    summed = _sum_blocks(gathered, N_DEV, "sum_small_grads").reshape(1, padded)
    offs = [sum(sizes[:i]) for i in range(len(sizes))]
    (g_b_ada, g_gn1, g_gn2, g_gn3, g_final, g_conv_b, g_ln_g, g_ln_b, g_attn_g, g_conv_g, g_taps, loss_row) = [
        summed[:, o:o + n] for o, n in zip(offs, sizes)]
    loss = loss_row[0, 0]
    g_taps_shard = lax.dynamic_slice_in_dim(g_taps.reshape(CONV_KERNEL, aw), me * cw_shard, cw_shard, axis=1)
    dmod_all = gathered.reshape(N_DEV, padded)[:, :n_mod * d]
    dmod_cols = lax.dynamic_slice_in_dim(dmod_all, me * mod_cols, mod_cols, axis=1)
    g_w_ada = _mm_tn(silu_c, dmod_cols, "ada_dw")

    arrived = dict(zip(["ffn2_w_gate", "ffn2_w_up", "ffn2_w_down", "w_out", "w_in", "ffn1_w_down", "ffn1_w_up",
                        "ffn1_w_gate"], list(sums_a) + list(sums_b) + list(sums_c) + list(sums_d) + list(sums_e)))
    transposed = ("ffn1_w_gate", "ffn1_w_up", "w_in", "ffn2_w_gate", "ffn2_w_up")
    grads = {
        "w_ada": g_w_ada, "b_ada": g_b_ada, "ffn1_norm_g": g_gn1, "mix_norm_g": g_gn2, "conv_dw_w": g_taps_shard,
        "conv_dw_b": g_conv_b, "conv_ln_g": g_ln_g, "conv_ln_b": g_ln_b, "attn_out_g": g_attn_g,
        "conv_out_g": g_conv_g, "ffn2_norm_g": g_gn3, "final_norm_g": g_final,
    }
    weights = dict(w_ada=w_ada, b_ada=b_ada, ffn1_norm_g=ffn1_norm_g, ffn1_w_gate=ffn1_w_gate, ffn1_w_up=ffn1_w_up, ffn1_w_down=ffn1_w_down, mix_norm_g=mix_norm_g, w_in=w_in, conv_dw_w=conv_dw_w, conv_dw_b=conv_dw_b, conv_ln_g=conv_ln_g, conv_ln_b=conv_ln_b, attn_out_g=attn_out_g, conv_out_g=conv_out_g, w_out=w_out, ffn2_norm_g=ffn2_norm_g, ffn2_w_gate=ffn2_w_gate, ffn2_w_up=ffn2_w_up, ffn2_w_down=ffn2_w_down, final_norm_g=final_norm_g)
    moms = dict(w_ada=m_w_ada, b_ada=m_b_ada, ffn1_norm_g=m_ffn1_norm_g, ffn1_w_gate=m_ffn1_w_gate, ffn1_w_up=m_ffn1_w_up, ffn1_w_down=m_ffn1_w_down, mix_norm_g=m_mix_norm_g, w_in=m_w_in, conv_dw_w=m_conv_dw_w, conv_dw_b=m_conv_dw_b, conv_ln_g=m_conv_ln_g, conv_ln_b=m_conv_ln_b, attn_out_g=m_attn_out_g, conv_out_g=m_conv_out_g, w_out=m_w_out, ffn2_norm_g=m_ffn2_norm_g, ffn2_w_gate=m_ffn2_w_gate, ffn2_w_up=m_ffn2_w_up, ffn2_w_down=m_ffn2_w_down, final_norm_g=m_final_norm_g)
    vars_ = dict(w_ada=v_w_ada, b_ada=v_b_ada, ffn1_norm_g=v_ffn1_norm_g, ffn1_w_gate=v_ffn1_w_gate, ffn1_w_up=v_ffn1_w_up, ffn1_w_down=v_ffn1_w_down, mix_norm_g=v_mix_norm_g, w_in=v_w_in, conv_dw_w=v_conv_dw_w, conv_dw_b=v_conv_dw_b, conv_ln_g=v_conv_ln_g, conv_ln_b=v_conv_ln_b, attn_out_g=v_attn_out_g, conv_out_g=v_conv_out_g, w_out=v_w_out, ffn2_norm_g=v_ffn2_norm_g, ffn2_w_gate=v_ffn2_w_gate, ffn2_w_up=v_ffn2_w_up, ffn2_w_down=v_ffn2_w_down, final_norm_g=v_final_norm_g)
    names = list(weights)
    big = ["w_ada", "ffn1_w_gate", "ffn1_w_up", "ffn1_w_down", "w_in", "w_out", "ffn2_w_gate", "ffn2_w_up",
           "ffn2_w_down"]
    shape2 = {n: (weights[n].shape[-2] if weights[n].ndim > 1 else 1, weights[n].shape[-1]) for n in names}
    shape2["conv_dw_w"] = (CONV_KERNEL, cw_shard)
    g_out, d_out, m_out, v_out = {}, {}, {}, {}
    for n in big:
        if n in arrived:
            def view(t, n=n):
                return t[0].T if n in transposed else t[0]
            res = _adamw_reduced(view(weights[n]), arrived[n], view(moms[n]), view(vars_[n]), "adamw_" + n)
            g_out[n], d_out[n], m_out[n], v_out[n] = [r.T if n in transposed else r for r in res]
        else:
            g2d = grads[n].reshape(shape2[n])
            res = _adamw_big(weights[n].reshape(shape2[n]), g2d, moms[n].reshape(shape2[n]),
                             vars_[n].reshape(shape2[n]), "adamw_" + n)
            g_out[n], (d_out[n], m_out[n], v_out[n]) = g2d, res
    rest = [n for n in names if n not in big]
    res = _adamw_small([weights[n].reshape(shape2[n]) for n in rest], [grads[n].reshape(shape2[n]) for n in rest],
                       [moms[n].reshape(shape2[n]) for n in rest], [vars_[n].reshape(shape2[n]) for n in rest],
                       "adamw_small")
    for i, n in enumerate(rest):
        g_out[n], d_out[n], m_out[n], v_out[n] = grads[n], res[0][i], res[1][i], res[2][i]

    def shaped(table):
        return [table[n].reshape(weights[n].shape) for n in names]

    return (loss, dx.reshape(x.shape), *shaped(g_out), *shaped(d_out), *shaped(m_out), *shaped(v_out))
```

```python
import functools

import jax
import jax.numpy as jnp
from jax import lax
from jax.experimental import pallas as pl
from jax.experimental.pallas import tpu as pltpu

F32 = jnp.float32
BF16 = jnp.bfloat16
MESH = pl.DeviceIdType.MESH
ANY = pl.BlockSpec(memory_space=pl.ANY)

N_DEV = 8
N_CHIP = 4
HEAD_DIM = 64
HALF_HEAD = HEAD_DIM // 2
LANES = 128
BLOCK = 128
DILATIONS = (1, 4, 16)
MERGE_CHUNK = 512
ROPE_THETA = 10000.0
CONV_KERNEL = 31
CONV_HALO = 32
CONV_CHUNK = 512
CONV_SUB = 128
RMS_EPS = 1e-6
LN_EPS = 1e-5
ADAM_LR = 0.001
ADAM_B1 = 0.9
ADAM_B2 = 0.999
ADAM_EPS = 1e-08
ADAM_WD = 0.01
ADAM_STEP = 10
VMEM_LIMIT = 56 * 1024 * 1024
NEG = -1e30


def _params(n_axes):
    return pltpu.CompilerParams(dimension_semantics=("arbitrary",) * n_axes, vmem_limit_bytes=VMEM_LIMIT)


def _tile(n, target, unit):
    best = None
    for t in range(unit, min(n, target) + 1, unit):
        if n % t == 0:
            best = t
    return best if best is not None else n


def _sigmoid(x):
    return 0.5 * (jnp.tanh(0.5 * x) + 1.0)


def _call(body, *, grid, in_specs, out_specs, out_shape, args, name, scratch_shapes=(), comm=None):
    params = _params(len(grid))
    if comm is None:
        return pl.pallas_call(body, grid=grid, in_specs=list(in_specs), out_specs=list(out_specs),
                              out_shape=list(out_shape), scratch_shapes=list(scratch_shapes),
                              compiler_params=params, name=name)(*args)
    n_in, n_out, n_scr = len(args), len(out_shape), len(scratch_shapes)
    c_in, c_out = len(comm.inputs), len(comm.out_shapes)
    steps = 1
    for g in grid:
        steps *= g

    def hosted(*refs):
        pos = 0
        parts = []
        for size in (n_in, c_in, n_out, c_out, n_scr, len(comm.scratch)):
            parts.append(refs[pos:pos + size])
            pos += size
        ins, cin, outs, cout, scr, cscr = parts
        step = 0
        for axis, g in enumerate(grid):
            step = step * g + pl.program_id(axis)

        @pl.when(step == 0)
        def _():
            comm.start(cin, cout, cscr)

        body(*ins, *outs, *scr)
        if comm.mid is not None and steps >= 4:
            @pl.when(step == (3 * steps) // 4)
            def _():
                comm.mid(cin, cout, cscr)

        @pl.when(step == steps - 1)
        def _():
            if comm.mid is not None and steps < 4:
                comm.mid(cin, cout, cscr)
            comm.finish(cin, cout, cscr)

    res = pl.pallas_call(
        hosted, grid=grid, in_specs=list(in_specs) + [ANY] * c_in, out_specs=list(out_specs) + [ANY] * c_out,
        out_shape=list(out_shape) + list(comm.out_shapes), scratch_shapes=list(scratch_shapes) + list(comm.scratch),
        compiler_params=params, name=name)(*args, *comm.inputs)
    return res[:n_out], res[n_out:]


def _rows(fn, rows_in, vecs_in, rows_out, vecs_out, *, tile, name, comm=None):
    norm = [r if isinstance(r, tuple) else (r, r.shape[1], 0) for r in rows_in]
    n_rows = norm[0][0].shape[0]
    n_tiles = n_rows // tile
    in_specs, args = [], []
    for arr, width, cb in norm:
        in_specs.append(pl.BlockSpec((tile, width), functools.partial(lambda i, cb: (i, cb), cb=cb)))
        args.append(arr)
    for v in vecs_in:
        in_specs.append(pl.BlockSpec((1, v.shape[1]), lambda i: (0, 0)))
        args.append(v)
    out_shape = [jax.ShapeDtypeStruct((n_rows, w), dt) for w, dt in rows_out]
    out_shape += [jax.ShapeDtypeStruct((1, w), F32) for w in vecs_out]
    out_specs = [pl.BlockSpec((tile, w), lambda i: (i, 0)) for w, _ in rows_out]
    out_specs += [pl.BlockSpec((1, w), lambda i: (0, 0)) for w in vecs_out]
    n_in, n_ro = len(args), len(rows_out)

    def body(*refs):
        vals = [r[...] for r in refs[:n_in]]
        outs = refs[n_in:]
        row_vals, vec_vals = fn(*vals)
        for ref, val in zip(outs[:n_ro], row_vals):
            if isinstance(val, tuple):
                w = val[0].shape[1]
                for j, piece in enumerate(val):
                    ref[:, j * w:(j + 1) * w] = piece.astype(ref.dtype)
            else:
                ref[...] = val.astype(ref.dtype)
        if vecs_out:
            @pl.when(pl.program_id(0) == 0)
            def _():
                for ref in outs[n_ro:]:
                    ref[...] = jnp.zeros_like(ref)
            for ref, val in zip(outs[n_ro:], vec_vals):
                ref[...] += val

    return _call(body, grid=(n_tiles,), in_specs=in_specs, out_specs=out_specs, out_shape=out_shape, args=args,
                 name=name, comm=comm)


def _colsum(x):
    return jnp.sum(x, axis=0, keepdims=True)


def _rms_stats(h):
    r = lax.rsqrt(jnp.mean(h * h, axis=-1, keepdims=True) + RMS_EPS)
    return r, h * r


def _rms_back(r, xn, dxn):
    return r * (dxn - xn * jnp.mean(dxn * xn, axis=-1, keepdims=True))


def _branch_back(dh, f, gate, coef):
    return (coef * gate) * dh, coef * _colsum(f.astype(F32) * dh)


def _norm_mod_back(dn, h, dh_in, gain, scale):
    dn = dn.astype(F32)
    r, xn = _rms_stats(h)
    y = xn * gain
    dy = dn * (1.0 + scale)
    dh = dh_in + _rms_back(r, xn, dy * gain)
    return dh, [_colsum(dn), _colsum(dn * y), _colsum(dy * xn)]


def _norm_mod_bwd(dn, h, dh_in, gain, scale, name, comm=None):
    d = h.shape[1]

    def fn(dn, h, dh_in, gain, scale):
        dh, vecs = _norm_mod_back(dn, h, dh_in, gain, scale)
        return [dh], vecs
    return _rows(fn, [dn, h, dh_in], [gain, scale], [(d, F32)], [d, d, d], tile=256, name=name, comm=comm)


def _mm_norm_mod_bwd(pairs, h, dh_in, gain, scale, branch, name, tm, comm=None):
    f, gate, coef = branch

    def epi(accs, ex, vc):
        dh, vecs = _norm_mod_back(accs[0], ex[0], ex[1], vc[0], vc[1])
        df, dgate = _branch_back(dh, ex[2], vc[2], coef)
        return [dh, df] + vecs + [dgate]
    return _mm([pairs], epi, [h, dh_in, f], [gain, scale, gate], [F32, BF16], trans_rhs=False, tm=tm,
               tn=h.shape[1], name=name, n_sums=4, comm=comm)


def _last_mm_loss(lhs, w, res, gate, coef, target, gain, name):
    d = w.shape[1]

    def epi(accs, ex, vc):
        f = accs[0]
        h = ex[0] + (coef * vc[0]) * f
        r, xn = _rms_stats(h)
        err = xn * vc[1] - ex[1]
        dout = err * (1.0 / d)
        dh = _rms_back(r, xn, dout * vc[1])
        df, dgate = _branch_back(dh, f, vc[0], coef)
        return [dh, df, _colsum(err * err), _colsum(dout * xn), dgate]
    return _mm([[(lhs, w)]], epi, [res, target], [gate, gain], [F32, BF16], trans_rhs=False, tm=256, tn=d,
               name=name, n_sums=3)


def _partner(x):
    if x.shape[1] > LANES:
        return jnp.concatenate([_partner(x[:, c:c + LANES]) for c in range(0, x.shape[1], LANES)], axis=1)
    lane = lax.broadcasted_iota(jnp.int32, x.shape, 1) % HEAD_DIM
    return jnp.where(lane < HALF_HEAD, pltpu.roll(x, LANES - HALF_HEAD, 1), pltpu.roll(x, HALF_HEAD, 1))


def _proj_rope(n, w_t, cos, sin_signed, width, name):
    s, kdim = n.shape
    n_cols = w_t.shape[0]
    tm = _tile(s, 1024, 8)
    qscale = HEAD_DIM ** -0.5

    chunk = _tile(tm, 256, 8)

    def body(n_ref, w_ref, cos_ref, sin_ref, o_ref):
        j = pl.program_id(0)

        def products(rows):
            return lax.dot_general(n_ref[rows, :].astype(BF16), w_ref[...].astype(BF16), (((1,), (1,)), ((), ())),
                                   preferred_element_type=F32)

        @pl.when(j >= 2)
        def _():
            for c in range(tm // chunk):
                rows = slice(c * chunk, (c + 1) * chunk)
                o_ref[rows, :] = products(rows)

        @pl.when(j < 2)
        def _():
            scale = jnp.where(j == 0, qscale, 1.0)
            for c in range(tm // chunk):
                rows = slice(c * chunk, (c + 1) * chunk)
                acc = products(rows)
                cos = jnp.tile(cos_ref[rows, :], (1, width // LANES))
                sin = jnp.tile(sin_ref[rows, :], (1, width // LANES))
                o_ref[rows, :] = scale * (acc * cos + _partner(acc) * sin)

    table = pl.BlockSpec((tm, LANES), lambda j, i: (jnp.where(j < 2, i, 0), 0))
    return pl.pallas_call(
        body, grid=(n_cols // width, s // tm),
        in_specs=[pl.BlockSpec((tm, kdim), lambda j, i: (i, 0)), pl.BlockSpec((width, kdim), lambda j, i: (j, 0)),
                  table, table],
        out_specs=pl.BlockSpec((tm, width), lambda j, i: (i, j)), out_shape=jax.ShapeDtypeStruct((s, n_cols), F32),
        compiler_params=_params(2), name=name)(n, w_t, cos, sin_signed)


def _mix_post(attn, u1, attn_g, ln_g, ln_b, conv_g):
    _, xa = _rms_stats(attn)
    mu = jnp.mean(u1, axis=-1, keepdims=True)
    xc = u1 - mu
    rstd = lax.rsqrt(jnp.mean(xc * xc, axis=-1, keepdims=True) + LN_EPS)
    u2 = (xc * rstd) * ln_g + ln_b
    u3 = u2 * _sigmoid(u2)
    _, x3 = _rms_stats(u3)
    return jnp.concatenate([xa * attn_g, x3 * conv_g], axis=1)


def _mix_post_back(dy, attn, u1, attn_g, ln_g, ln_b, conv_g):
    w = attn.shape[1]
    dya, dyc = dy[:, :w], dy[:, w:]
    ra, xa = _rms_stats(attn)
    dattn = _rms_back(ra, xa, dya * attn_g)
    mu = jnp.mean(u1, axis=-1, keepdims=True)
    xc = u1 - mu
    rstd = lax.rsqrt(jnp.mean(xc * xc, axis=-1, keepdims=True) + LN_EPS)
    xh = xc * rstd
    u2 = xh * ln_g + ln_b
    sig = _sigmoid(u2)
    u3 = u2 * sig
    r3, x3 = _rms_stats(u3)
    du3 = _rms_back(r3, x3, dyc * conv_g)
    du2 = du3 * (sig + u3 * (1.0 - sig))
    dxh = du2 * ln_g
    du1 = rstd * (dxh - jnp.mean(dxh, axis=-1, keepdims=True) - xh * jnp.mean(dxh * xh, axis=-1, keepdims=True))
    return dattn, du1, [_colsum(dya * xa), _colsum(dyc * x3), _colsum(du2 * xh), _colsum(du2)]


def _silu_rows(c_all, name):
    def fn(c):
        return [c * _sigmoid(c)], []
    return _rows(fn, [c_all], [], [(c_all.shape[1], BF16)], [], tile=c_all.shape[0], name=name)[0]


def _mm(groups, epi, extras, vecs, outs, *, trans_rhs, tm, tn, name, n_sums=0, pre=None, pre_inputs=(),
        comm=None):
    m = (pre_inputs[0] if pre is not None else groups[0][0][0]).shape[0]
    n = groups[0][0][1].shape[0] if trans_rhs else groups[0][0][1].shape[1]
    tm, tn = min(tm, m), min(tn, n)
    in_specs, args, uses_pre = [], [], []
    for grp in groups:
        for lhs, rhs in grp:
            k = rhs.shape[1] if trans_rhs else rhs.shape[0]
            uses_pre.append(lhs is None)
            if lhs is not None:
                in_specs.append(pl.BlockSpec((tm, k), lambda j, i: (i, 0)))
                args.append(lhs)
            in_specs.append(pl.BlockSpec((tn, k), lambda j, i: (j, 0)) if trans_rhs
                            else pl.BlockSpec((k, tn), lambda j, i: (0, j)))
            args.append(rhs)
    n_mm = len(args)
    for p in pre_inputs:
        in_specs.append(pl.BlockSpec((tm, p.shape[1]), lambda j, i: (i, 0)))
        args.append(p)
    for e in extras:
        in_specs.append(pl.BlockSpec((tm, tn), lambda j, i: (i, j)) if e.shape[1] == n
                        else pl.BlockSpec((tm, e.shape[1]), lambda j, i: (i, 0)))
        args.append(e)
    for v in vecs:
        in_specs.append(pl.BlockSpec((1, tn), lambda j, i: (0, j)) if v.shape[1] == n
                        else pl.BlockSpec((1, v.shape[1]), lambda j, i: (0, 0)))
        args.append(v)
    sizes = [len(g) for g in groups]
    n_pre, n_ex, n_vec = len(pre_inputs), len(extras), len(vecs)
    dims = (((1,), (1,)), ((), ())) if trans_rhs else (((1,), (0,)), ((), ()))
    out_specs, out_shape = [], []
    if pre is not None:
        k_pre = args[n_mm - 1].shape[1] if trans_rhs else args[n_mm - 1].shape[0]
        out_specs.append(pl.BlockSpec((tm, k_pre), lambda j, i: (i, 0)))
        out_shape.append(jax.ShapeDtypeStruct((m, k_pre), BF16))
    for o in outs:
        dt, width = o if isinstance(o, tuple) else (o, n)
        out_specs.append(pl.BlockSpec((tm, tn), lambda j, i: (i, j)) if width == n
                         else pl.BlockSpec((tm, width), lambda j, i: (i, 0)))
        out_shape.append(jax.ShapeDtypeStruct((m, width), dt))
    n_tiles_out = len(out_specs)
    out_specs += [pl.BlockSpec((1, tn), lambda j, i: (0, j))] * n_sums
    out_shape += [jax.ShapeDtypeStruct((1, n), F32)] * n_sums

    def body(*refs):
        ins = refs[:n_mm + n_pre + n_ex + n_vec]
        out_refs = refs[n_mm + n_pre + n_ex + n_vec:]
        vc = [r[...] for r in ins[n_mm + n_pre + n_ex:]]
        vals = []
        made = None
        if pre is not None:
            made = pre([r[...] for r in ins[n_mm:n_mm + n_pre]], vc).astype(BF16)
            vals.append(made)
        accs, pos, pair = [], 0, 0
        for size in sizes:
            acc = None
            for _ in range(size):
                if uses_pre[pair]:
                    lhs_tile = made
                else:
                    lhs_tile = ins[pos][...].astype(BF16)
                    pos += 1
                part = lax.dot_general(lhs_tile, ins[pos][...].astype(BF16), dims, preferred_element_type=F32)
                acc = part if acc is None else acc + part
                pos += 1
                pair += 1
            accs.append(acc)
        ex = [r[...] for r in ins[n_mm + n_pre:n_mm + n_pre + n_ex]]
        vals += epi(accs, ex, vc)
        for ref, val in zip(out_refs[:n_tiles_out], vals):
            ref[...] = val.astype(ref.dtype)
        if n_sums:
            @pl.when(pl.program_id(1) == 0)
            def _():
                for ref in out_refs[n_tiles_out:]:
                    ref[...] = jnp.zeros_like(ref)
            for ref, val in zip(out_refs[n_tiles_out:], vals[n_tiles_out:]):
                ref[...] += val

    return _call(body, grid=(n // tn, m // tm), in_specs=in_specs, out_specs=out_specs, out_shape=out_shape,
                 args=args, name=name, comm=comm)


def _mm_tn(lhs, rhs, name, comm=None):
    t, a = lhs.shape
    b = rhs.shape[1]
    ta = a if a <= 1536 else _tile(a, 1536, LANES)
    tk = _tile(t, 2048, 8)

    def body(l_ref, r_ref, o_ref):
        @pl.when(pl.program_id(1) == 0)
        def _():
            o_ref[...] = jnp.zeros_like(o_ref)
        o_ref[...] += lax.dot_general(l_ref[...].astype(BF16), r_ref[...].astype(BF16), (((0,), (0,)), ((), ())),
                                      preferred_element_type=F32)

    res = _call(body, grid=(a // ta, t // tk),
                in_specs=[pl.BlockSpec((tk, ta), lambda i, k: (k, i)), pl.BlockSpec((tk, b), lambda i, k: (k, 0))],
                out_specs=[pl.BlockSpec((ta, b), lambda i, k: (i, 0))], out_shape=[jax.ShapeDtypeStruct((a, b), F32)],
                args=(lhs, rhs), name=name, comm=comm)
    return res[0] if comm is None else (res[0][0], res[1])


def _ffn_tn(f):
    return _tile(f, 1536, LANES)


def _ffn_up(n, wg_t, wu_t, name, comm=None):
    def epi(accs, ex, vc):
        a, b = accs
        return [a, b, (a * _sigmoid(a)) * b]
    return _mm([[(n, wg_t)], [(n, wu_t)]], epi, [], [], [BF16, BF16, BF16], trans_rhs=True, tm=512,
               tn=_ffn_tn(wg_t.shape[0]), name=name, comm=comm)


def _norm_gate(h, gain, scale, shift, wg_t, name, comm=None):
    def pre(tiles, vc):
        _, xn = _rms_stats(tiles[0])
        return (xn * vc[0]) * (1.0 + vc[1]) + vc[2]

    def epi(accs, ex, vc):
        return [accs[0]]
    return _mm([[(None, wg_t)]], epi, [], [gain, scale, shift], [BF16], trans_rhs=True, tm=512,
               tn=wg_t.shape[0], name=name, pre=pre, pre_inputs=[h], comm=comm)


def _mix_out(attn, u1, post, w, res, gate, norm, name):
    def pre(tiles, vc):
        return _mix_post(tiles[0], tiles[1], *vc[4:8])

    def epi(accs, ex, vc):
        h = ex[0] + vc[0] * accs[0]
        _, xn = _rms_stats(h)
        return [h, accs[0], (xn * vc[1]) * (1.0 + vc[2]) + vc[3]]
    return _mm([[(None, w)]], epi, [res], [gate] + list(norm) + list(post), [F32, BF16, BF16], trans_rhs=False,
               tm=512, tn=w.shape[1], name=name, pre=pre, pre_inputs=[attn, u1])


def _mix_dy_post_bwd(dmix, w, attn, u1, post, name):
    width = attn.shape[1]

    def epi(accs, ex, vc):
        dattn, du1, sums = _mix_post_back(accs[0], ex[0], ex[1], *vc)
        return [dattn, du1, jnp.concatenate(sums[0:2], axis=1), jnp.concatenate(sums[2:4], axis=1)]
    return _mm([[(dmix, w)]], epi, [attn, u1], list(post), [(F32, width), (F32, width)], trans_rhs=True, tm=256,
               tn=w.shape[0], name=name, n_sums=2)


def _ffn_up_given_gate(n, wu_t, a, name, comm=None):
    def epi(accs, ex, vc):
        av = ex[0].astype(F32)
        return [accs[0], (av * _sigmoid(av)) * accs[0]]
    return _mm([[(n, wu_t)]], epi, [a], [], [BF16, BF16], trans_rhs=True, tm=512, tn=_ffn_tn(wu_t.shape[0]),
               name=name, comm=comm)


def _residual_mm(lhs, w, res, gate, coef, name, norm=None, comm=None):
    def epi(accs, ex, vc):
        h = ex[0] + (coef * vc[0]) * accs[0]
        if norm is None:
            return [h, accs[0]]
        _, xn = _rms_stats(h)
        return [h, accs[0], (xn * vc[1]) * (1.0 + vc[2]) + vc[3]]
    vecs = [gate] + (list(norm) if norm is not None else [])
    outs = [F32, BF16] + ([BF16] if norm is not None else [])
    return _mm([[(lhs, w)]], epi, [res], vecs, outs, trans_rhs=False, tm=512, tn=w.shape[1], name=name, comm=comm)


def _ffn_bwd_hidden(df, wd, a, b, name, comm=None):
    def epi(accs, ex, vc):
        dh = accs[0]
        av, bv = ex[0].astype(F32), ex[1].astype(F32)
        sig = _sigmoid(av)
        silu = av * sig
        return [dh * bv * (sig + silu * (1.0 - sig)), dh * silu]
    return _mm([[(df, wd)]], epi, [a, b], [], [BF16, BF16], trans_rhs=True, tm=512, tn=_ffn_tn(wd.shape[0]),
               name=name, comm=comm)


def _plain_mm(pairs, out_dtype, trans_rhs, tn, name, tm=512, comm=None):
    def epi(accs, ex, vc):
        return [accs[0]]
    res = _mm([pairs], epi, [], [], [out_dtype], trans_rhs=trans_rhs, tm=tm, tn=tn, name=name, comm=comm)
    return res[0] if comm is None else (res[0][0], res[1])


HEADS_PER_TILE = LANES // HEAD_DIM


def _stack_heads(x):
    lane = lax.broadcasted_iota(jnp.int32, (1, LANES), 1)
    return jnp.concatenate([x * (lane // HEAD_DIM == h).astype(F32) for h in range(HEADS_PER_TILE)], axis=0)


def _unstack_heads(y):
    r = y.shape[0] // HEADS_PER_TILE
    lane = lax.broadcasted_iota(jnp.int32, (r, y.shape[1]), 1)
    out = y[0:r]
    for h in range(1, HEADS_PER_TILE):
        out = jnp.where(lane // HEAD_DIM == h, y[h * r:(h + 1) * r], out)
    return out


def _stacked_lse(lb):
    return jnp.concatenate([_lane_pick(lb, h) for h in range(HEADS_PER_TILE)], axis=0)


def _band_masks(n_row_blocks, n_col_blocks):
    shape = (n_row_blocks * BLOCK, n_col_blocks * BLOCK)
    qi = lax.broadcasted_iota(jnp.int32, shape, 0) % BLOCK
    kj = lax.broadcasted_iota(jnp.int32, shape, 1) % BLOCK
    return kj <= qi, kj >= qi


def _query_masks():
    first_valid, _ = _band_masks(HEADS_PER_TILE, 1)
    same_ok, before_ok = _band_masks(HEADS_PER_TILE, 2)
    is_cur = lax.broadcasted_iota(jnp.int32, same_ok.shape, 1) >= BLOCK
    return first_valid, jnp.logical_and(is_cur, same_ok), jnp.logical_and(jnp.logical_not(is_cur), before_ok)


def _dot_nt(a, b):
    return lax.dot_general(a.astype(BF16), b.astype(BF16), (((1,), (1,)), ((), ())), preferred_element_type=F32)


def _dot_nn(a, b):
    return lax.dot_general(a.astype(BF16), b.astype(BF16), (((1,), (0,)), ((), ())), preferred_element_type=F32)


def _dot_tn(a, b):
    return lax.dot_general(a.astype(BF16), b.astype(BF16), (((0,), (0,)), ((), ())), preferred_element_type=F32)


def _lane_pick(x, h):
    lane = lax.broadcasted_iota(jnp.int32, x.shape, 1)
    return jnp.sum(jnp.where(lane == h * HEAD_DIM, x, 0.0), axis=1, keepdims=True)


def _block_rows(idx, d):
    span = BLOCK * d
    q0 = (idx // d) * span + idx % d
    return pl.ds(q0, BLOCK, stride=d), pl.ds(q0 - span, BLOCK, stride=d)


def _branch_loops(n_blocks, d, visit, unroll, masks):
    first_valid, cur_part, prev_part = masks
    if d % unroll == 0 and (n_blocks - d) % unroll == 0:
        full_valid = jnp.logical_or(cur_part, prev_part)

        def first(idx, carry):
            rows = pl.ds(idx, BLOCK, stride=d)
            visit(rows, [rows], first_valid)
            return carry

        def rest(idx, carry):
            rows, prev = _block_rows(idx, d)
            visit(rows, [prev, rows], full_valid)
            return carry

        lax.fori_loop(0, d, first, 0, unroll=unroll)
        lax.fori_loop(d, n_blocks, rest, 0, unroll=unroll)
        return

    def every(idx, carry):
        span = BLOCK * d
        q0 = (idx // d) * span + idx % d
        has_prev = idx >= d
        rows = pl.ds(q0, BLOCK, stride=d)
        prev = pl.ds(jnp.where(has_prev, q0 - span, q0), BLOCK, stride=d)
        visit(rows, [prev, rows], jnp.logical_or(cur_part, jnp.logical_and(prev_part, has_prev)))
        return carry

    lax.fori_loop(0, n_blocks, every, 0, unroll=unroll)


def _qkv_specs(s, tiles):
    q, k, v = [pl.BlockSpec((s, LANES), functools.partial(lambda hb, off: (0, off + hb), off=i * tiles))
               for i in range(3)]
    return q, k, v, pl.BlockSpec((s, LANES), lambda hb: (0, hb))


def _attn_seq_fwd(proj, width, name, comm=None):
    s = proj.shape[0]
    q_spec, k_spec, v_spec, cur = _qkv_specs(s, width // LANES)

    def body(q_ref, k_ref, v_ref, o_ref, l_ref, o_s, l_s):
        masks = _query_masks()
        for bi, d in enumerate(DILATIONS):
            def visit(rows, key_rows, valid, bi=bi):
                q2 = _stack_heads(q_ref[rows, :])
                keys = jnp.concatenate([k_ref[r, :] for r in key_rows], axis=0)
                vals = jnp.concatenate([v_ref[r, :] for r in key_rows], axis=0)
                sc = jnp.where(valid, _dot_nt(q2, keys), NEG)
                mx = jnp.max(sc, axis=1, keepdims=True)
                p = jnp.exp(sc - mx)
                den = jnp.sum(p, axis=1, keepdims=True)
                o_s[bi, rows, :] = _unstack_heads(_dot_nn(p, vals) / den)
                l_s[bi, rows, :] = _unstack_heads(jnp.broadcast_to(mx + jnp.log(den), (q2.shape[0], LANES)))

            _branch_loops(s // BLOCK, d, visit, 8, masks)
        for c in range(s // MERGE_CHUNK):
            rows = slice(c * MERGE_CHUNK, (c + 1) * MERGE_CHUNK)
            ls = [l_s[bi, rows, :] for bi in range(len(DILATIONS))]
            top = functools.reduce(jnp.maximum, ls)
            ws = [jnp.exp(l - top) for l in ls]
            den = functools.reduce(lambda a, b: a + b, ws)
            num = functools.reduce(lambda a, b: a + b, [w * o_s[bi, rows, :] for bi, w in enumerate(ws)])
            o_ref[rows, :] = num / den
            l_ref[rows, :] = top + jnp.log(den)

    return _call(
        body, grid=(width // LANES,), in_specs=[q_spec, k_spec, v_spec], out_specs=[cur, cur],
        out_shape=[jax.ShapeDtypeStruct((s, width), F32)] * 2,
        scratch_shapes=[pltpu.VMEM((len(DILATIONS), s, LANES), F32)] * 2,
        args=(proj, proj, proj), name=name, comm=comm)


def _attn_seq_bwd(proj, do, o, lse, cos, sin_signed, name, comm=None):
    s, width = do.shape
    q_spec, k_spec, v_spec, cur = _qkv_specs(s, width // LANES)
    table = pl.BlockSpec((s, LANES), lambda hb: (0, 0))
    qscale = HEAD_DIM ** -0.5

    def body(q_ref, k_ref, v_ref, do_ref, o_ref, l_ref, cos_ref, sin_ref, dq_out, dk_out, dv_out,
             dq_ref, dk_ref, dv_ref):
        dq_ref[...] = jnp.zeros_like(dq_ref)
        dk_ref[...] = jnp.zeros_like(dk_ref)
        dv_ref[...] = jnp.zeros_like(dv_ref)
        masks = _query_masks()
        for d in DILATIONS:
            def visit(rows, key_rows, valid):
                dob = do_ref[rows, :]
                q2 = _stack_heads(q_ref[rows, :])
                do2 = _stack_heads(dob)
                delta = jnp.sum(_stack_heads(dob * o_ref[rows, :]), axis=1, keepdims=True)
                lse2 = _stacked_lse(l_ref[rows, :])
                keys = jnp.concatenate([k_ref[r, :] for r in key_rows], axis=0)
                vals = jnp.concatenate([v_ref[r, :] for r in key_rows], axis=0)
                p = jnp.where(valid, jnp.exp(_dot_nt(q2, keys) - lse2), 0.0)
                ds = p * (_dot_nt(do2, vals) - delta)
                dq_ref[rows, :] += _unstack_heads(_dot_nn(ds, keys))
                dkk = _dot_tn(ds, q2)
                dvv = _dot_tn(p, do2)
                for i, r in enumerate(key_rows):
                    dk_ref[r, :] += dkk[i * BLOCK:(i + 1) * BLOCK]
                    dv_ref[r, :] += dvv[i * BLOCK:(i + 1) * BLOCK]

            _branch_loops(s // BLOCK, d, visit, 8, masks)
        for c in range(s // MERGE_CHUNK):
            rows = slice(c * MERGE_CHUNK, (c + 1) * MERGE_CHUNK)
            cos, sin = cos_ref[rows, :], sin_ref[rows, :]
            dq, dk = dq_ref[rows, :], dk_ref[rows, :]
            dq_out[rows, :] = ((dq * cos - _partner(dq) * sin) * qscale).astype(BF16)
            dk_out[rows, :] = (dk * cos - _partner(dk) * sin).astype(BF16)
            dv_out[rows, :] = dv_ref[rows, :].astype(BF16)

    return _call(
        body, grid=(width // LANES,), in_specs=[q_spec, k_spec, v_spec, cur, cur, cur, table, table],
        out_specs=[cur, cur, cur], out_shape=[jax.ShapeDtypeStruct((s, width), BF16)] * 3,
        scratch_shapes=[pltpu.VMEM((s, LANES), F32)] * 3,
        args=(proj, proj, proj, do, o, lse, cos, sin_signed), name=name, comm=comm)


def _conv_specs(s, a_block, b_block):
    per = CONV_CHUNK // CONV_HALO
    a_cur = pl.BlockSpec((CONV_CHUNK, LANES), lambda cb, i: (i, a_block + cb))
    b_cur = pl.BlockSpec((CONV_CHUNK, LANES), lambda cb, i: (i, b_block + cb))
    a_halo = pl.BlockSpec((CONV_HALO, LANES), lambda cb, i: (jnp.maximum(i * per - 1, 0), a_block + cb))
    b_halo = pl.BlockSpec((CONV_HALO, LANES), lambda cb, i: (jnp.maximum(i * per - 1, 0), b_block + cb))
    w_spec = pl.BlockSpec((CONV_KERNEL, LANES), lambda cb, i: (0, cb))
    vec = pl.BlockSpec((1, LANES), lambda cb, i: (0, cb))
    out = pl.BlockSpec((CONV_CHUNK, LANES), lambda cb, i: (i, cb))
    return a_cur, b_cur, a_halo, b_halo, w_spec, vec, out


def _fill_glu_window(win, a_ref, b_ref, ah_ref, bh_ref, first):
    halo = ah_ref[...] * _sigmoid(bh_ref[...])
    win[0:CONV_HALO, :] = jnp.where(first, 0.0, halo)
    win[CONV_HALO:, :] = a_ref[...] * _sigmoid(b_ref[...])


def _conv_fwd(proj, a_block, b_block, w, bias, name):
    s = proj.shape[0]
    cw = w.shape[1]
    a_cur, b_cur, a_halo, b_halo, w_spec, vec, out = _conv_specs(s, a_block, b_block)
    lead = CONV_HALO - (CONV_KERNEL - 1)

    def body(a_ref, b_ref, ah_ref, bh_ref, w_ref, bias_ref, o_ref, win):
        _fill_glu_window(win, a_ref, b_ref, ah_ref, bh_ref, pl.program_id(1) == 0)
        for sub in range(CONV_CHUNK // CONV_SUB):
            base = sub * CONV_SUB
            acc = jnp.zeros((CONV_SUB, LANES), F32) + bias_ref[...]
            for j in range(CONV_KERNEL):
                acc = acc + w_ref[j:j + 1, :] * win[base + lead + j:base + lead + j + CONV_SUB, :]
            o_ref[base:base + CONV_SUB, :] = acc

    return pl.pallas_call(
        body, grid=(cw // LANES, s // CONV_CHUNK), in_specs=[a_cur, b_cur, a_halo, b_halo, w_spec, vec],
        out_specs=out, out_shape=jax.ShapeDtypeStruct((s, cw), F32),
        scratch_shapes=[pltpu.VMEM((CONV_CHUNK + CONV_HALO, LANES), F32)],
        compiler_params=_params(2), name=name)(proj, proj, proj, proj, w, bias)


def _conv_bwd(proj, a_block, b_block, w, du1, name):
    s = proj.shape[0]
    cw = w.shape[1]
    a_cur, b_cur, a_halo, b_halo, w_spec, vec, out = _conv_specs(s, a_block, b_block)
    per = CONV_CHUNK // CONV_HALO
    n_chunks = s // CONV_CHUNK
    d_next = pl.BlockSpec((CONV_HALO, LANES), lambda cb, i: (jnp.minimum((i + 1) * per, s // CONV_HALO - 1), cb))
    lead = CONV_HALO - (CONV_KERNEL - 1)

    def body(a_ref, b_ref, ah_ref, bh_ref, w_ref, d_ref, dn_ref, da_ref, db_ref, dw_ref, dbias_ref, win, dwin):
        i = pl.program_id(1)
        _fill_glu_window(win, a_ref, b_ref, ah_ref, bh_ref, i == 0)
        dwin[0:CONV_CHUNK, :] = d_ref[...]
        dwin[CONV_CHUNK:, :] = jnp.where(i == n_chunks - 1, 0.0, dn_ref[...])

        @pl.when(i == 0)
        def _():
            dw_ref[...] = jnp.zeros_like(dw_ref)
            dbias_ref[...] = jnp.zeros_like(dbias_ref)

        dbias_ref[...] += _colsum(d_ref[...])
        for sub in range(CONV_CHUNK // CONV_SUB):
            base = sub * CONV_SUB
            dcur = dwin[base:base + CONV_SUB, :]
            du0 = jnp.zeros((CONV_SUB, LANES), F32)
            for j in range(CONV_KERNEL):
                back = CONV_KERNEL - 1 - j
                du0 = du0 + w_ref[j:j + 1, :] * dwin[base + back:base + back + CONV_SUB, :]
                dw_ref[j:j + 1, :] += _colsum(dcur * win[base + lead + j:base + lead + j + CONV_SUB, :])
            av = a_ref[base:base + CONV_SUB, :]
            sig = _sigmoid(b_ref[base:base + CONV_SUB, :])
            da_ref[base:base + CONV_SUB, :] = (du0 * sig).astype(BF16)
            db_ref[base:base + CONV_SUB, :] = (du0 * av * sig * (1.0 - sig)).astype(BF16)

    return pl.pallas_call(
        body, grid=(cw // LANES, n_chunks), in_specs=[a_cur, b_cur, a_halo, b_halo, w_spec, out, d_next],
        out_specs=[out, out, w_spec, vec],
        out_shape=[jax.ShapeDtypeStruct((s, cw), BF16), jax.ShapeDtypeStruct((s, cw), BF16),
                   jax.ShapeDtypeStruct((CONV_KERNEL, cw), F32), jax.ShapeDtypeStruct((1, cw), F32)],
        scratch_shapes=[pltpu.VMEM((CONV_CHUNK + CONV_HALO, LANES), F32)] * 2,
        compiler_params=_params(2), name=name)(proj, proj, proj, proj, w, du1, du1)


def _adamw_math(w, g, m, v):
    m = ADAM_B1 * m + (1.0 - ADAM_B1) * g
    v = ADAM_B2 * v + (1.0 - ADAM_B2) * (g * g)
    m_hat = m / (1.0 - ADAM_B1 ** ADAM_STEP)
    v_hat = v / (1.0 - ADAM_B2 ** ADAM_STEP)
    delta = -ADAM_LR * (m_hat / (jnp.sqrt(v_hat) + ADAM_EPS) + ADAM_WD * w)
    return delta, m, v


def _adamw_big(w, g, m, v, name):
    rows, cols = w.shape
    tile = _tile(rows, 256, 8)
    spec = pl.BlockSpec((tile, cols), lambda i: (i, 0))

    def body(w_ref, g_ref, m_ref, v_ref, d_out, m_out, v_out):
        d_out[...], m_out[...], v_out[...] = _adamw_math(w_ref[...], g_ref[...], m_ref[...], v_ref[...])

    return pl.pallas_call(body, grid=(rows // tile,), in_specs=[spec] * 4, out_specs=[spec] * 3,
                          out_shape=[jax.ShapeDtypeStruct(w.shape, F32)] * 3, compiler_params=_params(1),
                          name=name)(w, g, m, v)


def _adamw_reduced(w, land, m, v, name):
    rows, cols = w.shape
    tile = _tile(rows, 256, 16)
    spec = pl.BlockSpec((tile, cols), lambda i: (i, 0))

    def body(w_ref, l_ref, m_ref, v_ref, g_out, d_out, m_out, v_out):
        g = l_ref[0].astype(F32)
        for q in range(1, N_CHIP):
            g = g + l_ref[q].astype(F32)
        g_out[...] = g
        d_out[...], m_out[...], v_out[...] = _adamw_math(w_ref[...], g, m_ref[...], v_ref[...])

    return pl.pallas_call(body, grid=(rows // tile,),
                          in_specs=[spec, pl.BlockSpec((N_CHIP, tile, cols), lambda i: (0, i, 0)), spec, spec],
                          out_specs=[spec] * 4, out_shape=[jax.ShapeDtypeStruct(w.shape, F32)] * 4,
                          compiler_params=_params(1), name=name)(w, land, m, v)


def _adamw_small(ws, gs, ms, vs, name):
    n = len(ws)

    def body(*refs):
        ins, outs = refs[:4 * n], refs[4 * n:]
        for t in range(n):
            res = _adamw_math(ins[t][...], ins[n + t][...], ins[2 * n + t][...], ins[3 * n + t][...])
            for j in range(3):
                outs[j * n + t][...] = res[j]

    shapes = [jax.ShapeDtypeStruct(w.shape, F32) for w in ws]
    res = pl.pallas_call(body, out_shape=shapes * 3, compiler_params=pltpu.CompilerParams(vmem_limit_bytes=VMEM_LIMIT),
                         name=name)(*ws, *gs, *ms, *vs)
    return res[:n], res[n:2 * n], res[2 * n:]


def _sum_blocks(x, n_blocks, name):
    r = x.shape[0] // n_blocks

    def body(x_ref, o_ref):
        acc = x_ref[0:r, :]
        for b in range(1, n_blocks):
            acc = acc + x_ref[b * r:(b + 1) * r, :]
        o_ref[...] = acc

    return pl.pallas_call(body, out_shape=jax.ShapeDtypeStruct((r, x.shape[1]), F32),
                          compiler_params=pltpu.CompilerParams(vmem_limit_bytes=VMEM_LIMIT), name=name)(x)


def _coords():
    return lax.axis_index("x"), lax.axis_index("y"), lax.axis_index("c")


def _flip(v, bit):
    return 1 - v if bit else v


def _ag_small(x, name):
    r, c = x.shape

    def body(x_ref, o_ref, send, recv, local_sem):
        mx, my, mc = _coords()

        def rows(px, py, pc):
            return o_ref.at[pl.ds(pl.multiple_of((4 * px + 2 * py + pc) * r, 8), r), :]

        local = pltpu.make_async_copy(x_ref, rows(mx, my, mc), local_sem)
        local.start()
        peers = [(_flip(mx, k >> 2 & 1), _flip(my, k >> 1 & 1), _flip(mc, k & 1)) for k in range(1, N_DEV)]
        sends = [pltpu.make_async_remote_copy(x_ref, rows(mx, my, mc), send.at[k], recv.at[k], device_id=p,
                                              device_id_type=MESH) for k, p in enumerate(peers)]
        for cp in sends:
            cp.start()
        for k, p in enumerate(peers):
            pltpu.make_async_remote_copy(x_ref, rows(*p), send.at[k], recv.at[k], device_id=p,
                                         device_id_type=MESH).wait_recv()
        for cp in sends:
            cp.wait_send()
        local.wait()

    vm = pl.BlockSpec(memory_space=pltpu.VMEM)
    return pl.pallas_call(
        body, in_specs=[vm], out_specs=vm, out_shape=jax.ShapeDtypeStruct((N_DEV * r, c), x.dtype),
        scratch_shapes=[pltpu.SemaphoreType.DMA((N_DEV - 1,)), pltpu.SemaphoreType.DMA((N_DEV - 1,)),
                        pltpu.SemaphoreType.DMA(())],
        name=name)(x)


class _GatherSmall:
    mid = None

    def __init__(self, x):
        self.inputs = [x]
        self.out_shapes = [jax.ShapeDtypeStruct((N_DEV * x.shape[0], x.shape[1]), x.dtype)]
        self.scratch = [pltpu.SemaphoreType.DMA((N_DEV - 1,)), pltpu.SemaphoreType.DMA((N_DEV - 1,)),
                        pltpu.SemaphoreType.DMA(())]

    def _plan(self, x_refs, o_refs, sems):
        send, recv, local_sem = sems
        x_ref, o_ref = x_refs[0], o_refs[0]
        r = x_ref.shape[0]
        mx, my, mc = _coords()

        def rows(px, py, pc):
            return o_ref.at[pl.ds(pl.multiple_of((4 * px + 2 * py + pc) * r, 8), r), :]

        peers = [(_flip(mx, k >> 2 & 1), _flip(my, k >> 1 & 1), _flip(mc, k & 1)) for k in range(1, N_DEV)]
        out = [pltpu.make_async_remote_copy(x_ref, rows(mx, my, mc), send.at[k], recv.at[k], device_id=p,
                                            device_id_type=MESH) for k, p in enumerate(peers)]
        arrivals = [pltpu.make_async_remote_copy(x_ref, rows(*p), send.at[k], recv.at[k], device_id=p,
                                                 device_id_type=MESH) for k, p in enumerate(peers)]
        return out, arrivals, pltpu.make_async_copy(x_ref, rows(mx, my, mc), local_sem)

    def start(self, x_refs, o_refs, sems):
        out, _, local = self._plan(x_refs, o_refs, sems)
        local.start()
        for cp in out:
            cp.start()

    def finish(self, x_refs, o_refs, sems):
        out, arrivals, local = self._plan(x_refs, o_refs, sems)
        for cp in arrivals:
            cp.wait_recv()
        for cp in out:
            cp.wait_send()
        local.wait()


class _GatherWeights:
    def __init__(self, shards):
        n_t = len(shards)
        self.inputs = list(shards)
        self.out_shapes = [jax.ShapeDtypeStruct((N_DEV * x.shape[0], x.shape[1]), x.dtype) for x in shards]
        self.scratch = [pltpu.SemaphoreType.DMA((n_t, 7)), pltpu.SemaphoreType.DMA((n_t, 7)),
                        pltpu.SemaphoreType.DMA((n_t,))]

    def _plan(self, x_refs, o_refs, sems):
        send, recv, local_sem = sems
        mx, my, mc = _coords()
        me, sibling = (mx, my, mc), (mx, my, 1 - mc)
        chips = [(1 - mx, my), (mx, 1 - my), (1 - mx, 1 - my)]

        def rows(t, px, py, pc):
            r = x_refs[t].shape[0]
            return o_refs[t].at[pl.ds(pl.multiple_of((4 * px + 2 * py + pc) * r, 8), r), :]

        def copy(t, k, block, to, src=None):
            return pltpu.make_async_remote_copy(
                src_ref=rows(t, *block) if src is None else src, dst_ref=rows(t, *block),
                send_sem=send.at[t, k], recv_sem=recv.at[t, k], device_id=to, device_id_type=MESH)

        def local(t):
            return pltpu.make_async_copy(x_refs[t], rows(t, *me), local_sem.at[t])

        return me, sibling, chips, mc, copy, local

    def start(self, x_refs, o_refs, sems):
        me, sibling, chips, mc, copy, local = self._plan(x_refs, o_refs, sems)
        for t in range(len(x_refs)):
            local(t).start()
            copy(t, 0, me, sibling, src=x_refs[t]).start()
            for j, chip in enumerate(chips):
                copy(t, 1 + j, me, (*chip, mc), src=x_refs[t]).start()

    def mid(self, x_refs, o_refs, sems):
        me, sibling, chips, mc, copy, local = self._plan(x_refs, o_refs, sems)
        for j, chip in enumerate(chips):
            for t in range(len(x_refs)):
                copy(t, 1 + j, (*chip, mc), me).wait_recv()
                copy(t, 4 + j, (*chip, mc), sibling).start()

    def finish(self, x_refs, o_refs, sems):
        me, sibling, chips, mc, copy, local = self._plan(x_refs, o_refs, sems)
        for t in range(len(x_refs)):
            copy(t, 0, sibling, me).wait_recv()
            for j, chip in enumerate(chips):
                copy(t, 4 + j, (*chip, 1 - mc), me).wait_recv()
            copy(t, 0, me, sibling, src=x_refs[t]).wait_send()
            for j, chip in enumerate(chips):
                copy(t, 1 + j, me, (*chip, mc), src=x_refs[t]).wait_send()
                copy(t, 4 + j, (*chip, mc), sibling).wait_send()
            local(t).wait()


class _SiblingExchange:
    mid = None

    def __init__(self, grads):
        n_t = len(grads)
        self.inputs = list(grads)
        self.out_shapes = [jax.ShapeDtypeStruct((N_CHIP,) + g.shape[2:], F32) for g in grads]
        self.scratch = [pltpu.SemaphoreType.DMA((n_t,)), pltpu.SemaphoreType.DMA((n_t,))]

    def _copies(self, g_refs, land, sems):
        send, recv = sems
        mx, my, mc = _coords()
        return [pltpu.make_async_remote_copy(g_refs[t].at[:, 1 - mc], land[t], send.at[t], recv.at[t],
                                             device_id=(mx, my, 1 - mc), device_id_type=MESH)
                for t in range(len(g_refs))]

    def start(self, g_refs, land, sems):
        for cp in self._copies(g_refs, land, sems):
            cp.start()

    def finish(self, g_refs, land, sems):
        for cp in self._copies(g_refs, land, sems):
            cp.wait()


class _Together:
    def __init__(self, *comms):
        self.comms = comms
        self.inputs = [x for c in comms for x in c.inputs]
        self.out_shapes = [x for c in comms for x in c.out_shapes]
        self.scratch = [x for c in comms for x in c.scratch]
        self.mid = self._mid if any(c.mid is not None for c in comms) else None

    def _each(self, phase, cin, cout, sems):
        i = o = s = 0
        for c in self.comms:
            fn = getattr(c, phase)
            ni, no, ns = len(c.inputs), len(c.out_shapes), len(c.scratch)
            if fn is not None:
                fn(cin[i:i + ni], cout[o:o + no], sems[s:s + ns])
            i, o, s = i + ni, o + no, s + ns

    def start(self, cin, cout, sems):
        self._each("start", cin, cout, sems)

    def _mid(self, cin, cout, sems):
        self._each("mid", cin, cout, sems)

    def finish(self, cin, cout, sems):
        self._each("finish", cin, cout, sems)


def _standalone(comm, name):
    def body():
        pass
    return _call(body, grid=(1,), in_specs=[], out_specs=[], out_shape=[], args=(), name=name, comm=comm)[1]


def _chip_partials(g4s, lands, name):
    n_t = len(g4s)
    in_specs, out_specs, out_shape = [], [], []
    for g4 in g4s:
        _, _, r, c = g4.shape
        in_specs.append(pl.BlockSpec((None, None, r, c), lambda q: (q, lax.axis_index("c"), 0, 0)))
        out_specs.append(pl.BlockSpec((None, r, c), lambda q: (q, 0, 0)))
        out_shape.append(jax.ShapeDtypeStruct((N_CHIP, r, c), BF16))
    in_specs += [pl.BlockSpec((None,) + g4.shape[2:], lambda q: (q, 0, 0)) for g4 in g4s]

    def body(*refs):
        for t in range(n_t):
            refs[2 * n_t + t][...] = (refs[t][...] + refs[n_t + t][...]).astype(BF16)

    return pl.pallas_call(body, grid=(N_CHIP,), in_specs=in_specs, out_specs=out_specs, out_shape=out_shape,
                          compiler_params=_params(1), name=name)(*g4s, *lands)


class _ChipExchange:
    mid = None

    def __init__(self, parts):
        n_t = len(parts)
        self.inputs = list(parts)
        self.out_shapes = [jax.ShapeDtypeStruct(p.shape, p.dtype) for p in parts]
        self.scratch = [pltpu.SemaphoreType.DMA((n_t, 3)), pltpu.SemaphoreType.DMA((n_t, 3)),
                        pltpu.SemaphoreType.DMA((n_t,))]

    def _plan(self, p_refs, land, sems):
        send, recv, local_sem = sems
        mx, my, mc = _coords()
        my_chip = 2 * mx + my
        peers = [(_flip(mx, fx), _flip(my, fy)) for fx, fy in ((1, 0), (0, 1), (1, 1))]

        def out(t, k):
            px, py = peers[k]
            return pltpu.make_async_remote_copy(p_refs[t].at[2 * px + py], land[t].at[my_chip], send.at[t, k],
                                                recv.at[t, k], device_id=(px, py, mc), device_id_type=MESH)

        def arrival(t, k):
            px, py = peers[k]
            return pltpu.make_async_remote_copy(p_refs[t].at[my_chip], land[t].at[2 * px + py], send.at[t, k],
                                                recv.at[t, k], device_id=(px, py, mc), device_id_type=MESH)

        def local(t):
            return pltpu.make_async_copy(p_refs[t].at[my_chip], land[t].at[my_chip], local_sem.at[t])

        return out, arrival, local

    def start(self, p_refs, land, sems):
        out, arrival, local = self._plan(p_refs, land, sems)
        for t in range(len(p_refs)):
            local(t).start()
            for k in range(3):
                out(t, k).start()

    def finish(self, p_refs, land, sems):
        out, arrival, local = self._plan(p_refs, land, sems)
        for t in range(len(p_refs)):
            for k in range(3):
                arrival(t, k).wait_recv()
                out(t, k).wait_send()
            local(t).wait()


def _rope_tables(s, width):
    heads = width // HEAD_DIM
    inv_freq = ROPE_THETA ** (-jnp.arange(0, HEAD_DIM, 2, dtype=F32) / HEAD_DIM)
    inv_full = jnp.tile(inv_freq, 2 * heads)
    sign = jnp.tile(jnp.concatenate([-jnp.ones((HALF_HEAD,), F32), jnp.ones((HALF_HEAD,), F32)]), heads)
    ang = jnp.arange(s, dtype=F32)[:, None] * inv_full[None, :]
    return jnp.cos(ang), jnp.sin(ang) * sign[None, :]


def _pad_rows(v, rows):
    return jnp.concatenate([v, jnp.zeros((rows - 1, v.shape[1]), v.dtype)], axis=0)


def kernel(x, c, w_ada, b_ada, ffn1_norm_g, ffn1_w_gate, ffn1_w_up, ffn1_w_down, mix_norm_g, w_in, conv_dw_w, conv_dw_b, conv_ln_g, conv_ln_b, attn_out_g, conv_out_g, w_out, ffn2_norm_g, ffn2_w_gate, ffn2_w_up, ffn2_w_down, final_norm_g, loss_target, m_w_ada, m_b_ada, m_ffn1_norm_g, m_ffn1_w_gate, m_ffn1_w_up, m_ffn1_w_down, m_mix_norm_g, m_w_in, m_conv_dw_w, m_conv_dw_b, m_conv_ln_g, m_conv_ln_b, m_attn_out_g, m_conv_out_g, m_w_out, m_ffn2_norm_g, m_ffn2_w_gate, m_ffn2_w_up, m_ffn2_w_down, m_final_norm_g, v_w_ada, v_b_ada, v_ffn1_norm_g, v_ffn1_w_gate, v_ffn1_w_up, v_ffn1_w_down, v_mix_norm_g, v_w_in, v_conv_dw_w, v_conv_dw_b, v_conv_ln_g, v_conv_ln_b, v_attn_out_g, v_conv_out_g, v_w_out, v_ffn2_norm_g, v_ffn2_w_gate, v_ffn2_w_up, v_ffn2_w_down, v_final_norm_g):
    mx, my, mc = _coords()
    me = 4 * mx + 2 * my + mc
    s, d = x.shape[1], x.shape[2]
    aw = d // 2
    x2, target = x[0], loss_target[0]
    n_mod = w_ada.shape[2] * N_DEV // d
    mod_cols = w_ada.shape[2]

    def shard(w, transpose):
        return (w[0].T if transpose else w[0]).astype(BF16)

    cw_shard = conv_dw_w.shape[3]
    n_taps = CONV_KERNEL * cw_shard
    first_len = -(-(d + n_taps) // LANES) * LANES
    first = jnp.concatenate([c, conv_dw_w[0, :, 0, :].reshape(1, n_taps), jnp.zeros((1, first_len - d - n_taps), F32)], axis=1)
    first_all, wg1 = _standalone(
        _Together(_GatherSmall(_pad_rows(first, 8)), _GatherWeights([shard(ffn1_w_gate, True)])), "ag_first")
    first_all = first_all[0::8]
    c_all = first_all[:, :d]
    conv_w = first_all[:, d:d + n_taps].reshape(N_DEV, CONV_KERNEL, cw_shard).transpose(1, 0, 2).reshape(CONV_KERNEL, aw)

    silu_c = _silu_rows(c_all, "silu_c")
    mod_part = _plain_mm([(silu_c, w_ada[0])], F32, False, mod_cols, "mod_mm")
    mod_all = _ag_small(mod_part, "ag_mod").reshape(N_DEV, N_DEV, mod_cols)
    mod = lax.dynamic_index_in_dim(mod_all, me, axis=1, keepdims=False).reshape(1, n_mod * d) + b_ada
    sh1, sc1, g1, sh2, sc2, g2, sh3, sc3, g3 = [mod[:, i * d:(i + 1) * d] for i in range(n_mod)]

    def split(g):
        return g.reshape(N_CHIP, 2, g.shape[0] // N_DEV, g.shape[1])

    def partials(g4s, lands, tag):
        return _chip_partials(g4s, lands, "chip_partials_" + tag)

    gather_late = _GatherWeights([shard(ffn2_w_gate, True), shard(ffn2_w_up, True), shard(ffn2_w_down, False),
                                  shard(w_out, False)])

    (n1, a1), (wu1,) = _norm_gate(x2, ffn1_norm_g, sc1, sh1, wg1, "ffn1_gate",
                                  comm=_GatherWeights([shard(ffn1_w_up, True)]))
    (b1, hid1), (wd1,) = _ffn_up_given_gate(n1, wu1, a1, "ffn1_up", comm=_GatherWeights([shard(ffn1_w_down, False)]))
    (h1, f1, n2), (win_t,) = _residual_mm(hid1, wd1, x2, g1, 0.5, "ffn1_down", norm=(mix_norm_g, sc2, sh2),
                                          comm=_GatherWeights([shard(w_in, True)]))
    cos, sin_signed = _rope_tables(s, LANES)
    proj = _proj_rope(n2, win_t, cos, sin_signed, aw, "proj")
    lanes_per = aw // LANES
    (attn, lse), (wg2, wu2, wd2, wout) = _attn_seq_fwd(proj, aw, "attn_fwd", comm=gather_late)
    u1 = _conv_fwd(proj, 3 * lanes_per, 4 * lanes_per, conv_w, conv_dw_b, "conv_fwd")
    post = (attn_out_g, conv_ln_g, conv_ln_b, conv_out_g)
    y, h2, mix, n3 = _mix_out(attn, u1, post, wout, h1, g2, (ffn2_norm_g, sc3, sh3), "mix_out")
    a3, b3, hid3 = _ffn_up(n3, wg2, wu2, "ffn2_up")

    dh3, df3, err2, d_final_g, dg3 = _last_mm_loss(hid3, wd2, h2, g3, 0.5, target, final_norm_g.reshape(1, d),
                                                   "ffn2_down_loss")
    loss_part = jnp.zeros((1, LANES), F32).at[0, 0].set(0.5 * jnp.sum(err2) / d)

    da3, db3 = _ffn_bwd_hidden(df3, wd2, a3, b3, "ffn2_hidden_bwd")
    g4_a = [split(_mm_tn(da3, n3, "ffn2_dwg")), split(_mm_tn(db3, n3, "ffn2_dwu")), split(_mm_tn(hid3, df3, "ffn2_dwd"))]
    (dh2, dmix, dsh3, dsc3, dgn3, dg2), land_a = _mm_norm_mod_bwd(
        [(da3, wg2), (db3, wu2)], h2, dh3, ffn2_norm_g, sc3, (mix, g2, 1.0), "ffn2_dn_norm3_bwd", tm=256,
        comm=_SiblingExchange(g4_a))
    parts_a = partials(g4_a, land_a, "a")
    g_wout = _mm_tn(y, dmix, "mix_dwout")
    dattn, du1, d_gains, d_ln = _mix_dy_post_bwd(dmix, wout, attn, u1, post, "mix_dy_post_bwd")
    d_attn_g, d_conv_g, d_ln_g, d_ln_b = d_gains[:, :aw], d_gains[:, aw:], d_ln[:, :aw], d_ln[:, aw:]
    dga, dgb, d_taps, d_conv_b = _conv_bwd(proj, 3 * lanes_per, 4 * lanes_per, conv_w, du1, "conv_bwd")
    (dq, dk, dv), sums_a = _attn_seq_bwd(proj, dattn, attn, lse, cos, sin_signed, "attn_bwd",
                                         comm=_ChipExchange(parts_a))
    dproj = jnp.concatenate([dq, dk, dv, dga, dgb], axis=1)
    g4_b = [split(g_wout), split(_mm_tn(dproj, n2, "mix_dwin"))]
    (dh1, df1, dsh2, dsc2, dgn2, dg1), land_b = _mm_norm_mod_bwd(
        [(dproj, win_t)], h1, dh2, mix_norm_g, sc2, (f1, g1, 0.5), "mix_dn_norm2_bwd", tm=512,
        comm=_SiblingExchange(g4_b))
    parts_b = partials(g4_b, land_b, "b")
    g4_c = [split(_mm_tn(hid1, df1, "ffn1_dwd"))]
    (da1, db1), both = _ffn_bwd_hidden(df1, wd1, a1, b1, "ffn1_hidden_bwd",
                                       comm=_Together(_ChipExchange(parts_b), _SiblingExchange(g4_c)))
    sums_b, land_c = both[:2], both[2:]
    parts_c = partials(g4_c, land_c, "c")
    g_wu1, sums_c = _mm_tn(db1, n1, "ffn1_dwu", comm=_ChipExchange(parts_c))
    g4_d = [split(g_wu1)]
    g_wg1, land_d = _mm_tn(da1, n1, "ffn1_dwg", comm=_SiblingExchange(g4_d))
    parts_d = partials(g4_d, land_d, "d")
    g4_e = [split(g_wg1)]
    dn1, both = _plain_mm([(da1, wg1), (db1, wu1)], BF16, False, d, "ffn1_dn",
                          comm=_Together(_ChipExchange(parts_d), _SiblingExchange(g4_e)))
    sums_d, land_e = both[:1], both[1:]
    parts_e = partials(g4_e, land_e, "e")
    (dx, dsh1, dsc1, dgn1), sums_e = _norm_mod_bwd(dn1, x2, dh1, ffn1_norm_g, sc1, "norm1_bwd",
                                                   comm=_ChipExchange(parts_e))

    dmod = jnp.concatenate([dsh1, dsc1, dg1, dsh2, dsc2, dg2, dsh3, dsc3, dg3], axis=1)
    small = [dmod, dgn1, dgn2, dgn3, d_final_g, d_conv_b, d_ln_g, d_ln_b, d_attn_g, d_conv_g,
             d_taps.reshape(1, CONV_KERNEL * aw), loss_part]
    sizes = [v.shape[1] for v in small]
    total = sum(sizes)
    padded = -(-total // (8 * LANES)) * (8 * LANES)
    packed = jnp.concatenate(small + [jnp.zeros((1, padded - total), F32)], axis=1).reshape(8, padded // 8)
    gathered = _ag_small(packed, "ag_small_grads")
    summed = _sum_blocks(gathered, N_DEV, "sum_small_grads").reshape(1, padded)
    offs = [sum(sizes[:i]) for i in range(len(sizes))]
    (g_b_ada, g_gn1, g_gn2, g_gn3, g_final, g_conv_b, g_ln_g, g_ln_b, g_attn_g, g_conv_g, g_taps, loss_row) = [
        summed[:, o:o + n] for o, n in zip(offs, sizes)]
    loss = loss_row[0, 0]
    g_taps_shard = lax.dynamic_slice_in_dim(g_taps.reshape(CONV_KERNEL, aw), me * cw_shard, cw_shard, axis=1)
    dmod_all = gathered.reshape(N_DEV, padded)[:, :n_mod * d]
    dmod_cols = lax.dynamic_slice_in_dim(dmod_all, me * mod_cols, mod_cols, axis=1)
    g_w_ada = _mm_tn(silu_c, dmod_cols, "ada_dw")

    arrived = dict(zip(["ffn2_w_gate", "ffn2_w_up", "ffn2_w_down", "w_out", "w_in", "ffn1_w_down", "ffn1_w_up",
                        "ffn1_w_gate"], list(sums_a) + list(sums_b) + list(sums_c) + list(sums_d) + list(sums_e)))
    transposed = ("ffn1_w_gate", "ffn1_w_up", "w_in", "ffn2_w_gate", "ffn2_w_up")
    grads = {
        "w_ada": g_w_ada, "b_ada": g_b_ada, "ffn1_norm_g": g_gn1, "mix_norm_g": g_gn2, "conv_dw_w": g_taps_shard,
        "conv_dw_b": g_conv_b, "conv_ln_g": g_ln_g, "conv_ln_b": g_ln_b, "attn_out_g": g_attn_g,
        "conv_out_g": g_conv_g, "ffn2_norm_g": g_gn3, "final_norm_g": g_final,
    }
    weights = dict(w_ada=w_ada, b_ada=b_ada, ffn1_norm_g=ffn1_norm_g, ffn1_w_gate=ffn1_w_gate, ffn1_w_up=ffn1_w_up, ffn1_w_down=ffn1_w_down, mix_norm_g=mix_norm_g, w_in=w_in, conv_dw_w=conv_dw_w, conv_dw_b=conv_dw_b, conv_ln_g=conv_ln_g, conv_ln_b=conv_ln_b, attn_out_g=attn_out_g, conv_out_g=conv_out_g, w_out=w_out, ffn2_norm_g=ffn2_norm_g, ffn2_w_gate=ffn2_w_gate, ffn2_w_up=ffn2_w_up, ffn2_w_down=ffn2_w_down, final_norm_g=final_norm_g)
    moms = dict(w_ada=m_w_ada, b_ada=m_b_ada, ffn1_norm_g=m_ffn1_norm_g, ffn1_w_gate=m_ffn1_w_gate, ffn1_w_up=m_ffn1_w_up, ffn1_w_down=m_ffn1_w_down, mix_norm_g=m_mix_norm_g, w_in=m_w_in, conv_dw_w=m_conv_dw_w, conv_dw_b=m_conv_dw_b, conv_ln_g=m_conv_ln_g, conv_ln_b=m_conv_ln_b, attn_out_g=m_attn_out_g, conv_out_g=m_conv_out_g, w_out=m_w_out, ffn2_norm_g=m_ffn2_norm_g, ffn2_w_gate=m_ffn2_w_gate, ffn2_w_up=m_ffn2_w_up, ffn2_w_down=m_ffn2_w_down, final_norm_g=m_final_norm_g)
    vars_ = dict(w_ada=v_w_ada, b_ada=v_b_ada, ffn1_norm_g=v_ffn1_norm_g, ffn1_w_gate=v_ffn1_w_gate, ffn1_w_up=v_ffn1_w_up, ffn1_w_down=v_ffn1_w_down, mix_norm_g=v_mix_norm_g, w_in=v_w_in, conv_dw_w=v_conv_dw_w, conv_dw_b=v_conv_dw_b, conv_ln_g=v_conv_ln_g, conv_ln_b=v_conv_ln_b, attn_out_g=v_attn_out_g, conv_out_g=v_conv_out_g, w_out=v_w_out, ffn2_norm_g=v_ffn2_norm_g, ffn2_w_gate=v_ffn2_w_gate, ffn2_w_up=v_ffn2_w_up, ffn2_w_down=v_ffn2_w_down, final_norm_g=v_final_norm_g)
    names = list(weights)
    big = ["w_ada", "ffn1_w_gate", "ffn1_w_up", "ffn1_w_down", "w_in", "w_out", "ffn2_w_gate", "ffn2_w_up",
           "ffn2_w_down"]
    shape2 = {n: (weights[n].shape[-2] if weights[n].ndim > 1 else 1, weights[n].shape[-1]) for n in names}
    shape2["conv_dw_w"] = (CONV_KERNEL, cw_shard)
    g_out, d_out, m_out, v_out = {}, {}, {}, {}
    for n in big:
        if n in arrived:
            def view(t, n=n):
                return t[0].T if n in transposed else t[0]
            res = _adamw_reduced(view(weights[n]), arrived[n], view(moms[n]), view(vars_[n]), "adamw_" + n)
            g_out[n], d_out[n], m_out[n], v_out[n] = [r.T if n in transposed else r for r in res]
        else:
            g2d = grads[n].reshape(shape2[n])
            res = _adamw_big(weights[n].reshape(shape2[n]), g2d, moms[n].reshape(shape2[n]),
                             vars_[n].reshape(shape2[n]), "adamw_" + n)
            g_out[n], (d_out[n], m_out[n], v_out[n]) = g2d, res
    rest = [n for n in names if n not in big]
    res = _adamw_small([weights[n].reshape(shape2[n]) for n in rest], [grads[n].reshape(shape2[n]) for n in rest],
                       [moms[n].reshape(shape2[n]) for n in rest], [vars_[n].reshape(shape2[n]) for n in rest],
                       "adamw_small")
    for i, n in enumerate(rest):
        g_out[n], d_out[n], m_out[n], v_out[n] = grads[n], res[0][i], res[1][i], res[2][i]

    def shaped(table):
        return [table[n].reshape(weights[n].shape) for n in names]

    return (loss, dx.reshape(x.shape), *shaped(g_out), *shaped(d_out), *shaped(m_out), *shaped(v_out))
```

```python
import functools

import jax
import jax.numpy as jnp
from jax import lax
from jax.experimental import pallas as pl
from jax.experimental.pallas import tpu as pltpu

F32 = jnp.float32
BF16 = jnp.bfloat16
MESH = pl.DeviceIdType.MESH
ANY = pl.BlockSpec(memory_space=pl.ANY)

N_DEV = 8
N_CHIP = 4
HEAD_DIM = 64
HALF_HEAD = HEAD_DIM // 2
LANES = 128
BLOCK = 128
DILATIONS = (1, 4, 16)
MERGE_CHUNK = 512
ROPE_THETA = 10000.0
CONV_KERNEL = 31
CONV_HALO = 32
CONV_CHUNK = 512
CONV_SUB = 128
RMS_EPS = 1e-6
LN_EPS = 1e-5
ADAM_LR = 0.001
ADAM_B1 = 0.9
ADAM_B2 = 0.999
ADAM_EPS = 1e-08
ADAM_WD = 0.01
ADAM_STEP = 10
VMEM_LIMIT = 56 * 1024 * 1024
NEG = -1e30


def _params(n_axes):
    return pltpu.CompilerParams(dimension_semantics=("arbitrary",) * n_axes, vmem_limit_bytes=VMEM_LIMIT)


def _tile(n, target, unit):
    best = None
    for t in range(unit, min(n, target) + 1, unit):
        if n % t == 0:
            best = t
    return best if best is not None else n


def _sigmoid(x):
    return 0.5 * (jnp.tanh(0.5 * x) + 1.0)


def _call(body, *, grid, in_specs, out_specs, out_shape, args, name, scratch_shapes=(), comm=None):
    params = _params(len(grid))
    if comm is None:
        return pl.pallas_call(body, grid=grid, in_specs=list(in_specs), out_specs=list(out_specs),
                              out_shape=list(out_shape), scratch_shapes=list(scratch_shapes),
                              compiler_params=params, name=name)(*args)
    n_in, n_out, n_scr = len(args), len(out_shape), len(scratch_shapes)
    c_in, c_out = len(comm.inputs), len(comm.out_shapes)
    steps = 1
    for g in grid:
        steps *= g

    def hosted(*refs):
        pos = 0
        parts = []
        for size in (n_in, c_in, n_out, c_out, n_scr, len(comm.scratch)):
            parts.append(refs[pos:pos + size])
            pos += size
        ins, cin, outs, cout, scr, cscr = parts
        step = 0
        for axis, g in enumerate(grid):
            step = step * g + pl.program_id(axis)

        @pl.when(step == 0)
        def _():
            comm.start(cin, cout, cscr)

        body(*ins, *outs, *scr)
        if comm.mid is not None and steps >= 4:
            @pl.when(step == (3 * steps) // 4)
            def _():
                comm.mid(cin, cout, cscr)

        @pl.when(step == steps - 1)
        def _():
            if comm.mid is not None and steps < 4:
                comm.mid(cin, cout, cscr)
            comm.finish(cin, cout, cscr)

    res = pl.pallas_call(
        hosted, grid=grid, in_specs=list(in_specs) + [ANY] * c_in, out_specs=list(out_specs) + [ANY] * c_out,
        out_shape=list(out_shape) + list(comm.out_shapes), scratch_shapes=list(scratch_shapes) + list(comm.scratch),
        compiler_params=params, name=name)(*args, *comm.inputs)
    return res[:n_out], res[n_out:]


def _rows(fn, rows_in, vecs_in, rows_out, vecs_out, *, tile, name, comm=None):
    norm = [r if isinstance(r, tuple) else (r, r.shape[1], 0) for r in rows_in]
    n_rows = norm[0][0].shape[0]
    n_tiles = n_rows // tile
    in_specs, args = [], []
    for arr, width, cb in norm:
        in_specs.append(pl.BlockSpec((tile, width), functools.partial(lambda i, cb: (i, cb), cb=cb)))
        args.append(arr)
    for v in vecs_in:
        in_specs.append(pl.BlockSpec((1, v.shape[1]), lambda i: (0, 0)))
        args.append(v)
    out_shape = [jax.ShapeDtypeStruct((n_rows, w), dt) for w, dt in rows_out]
    out_shape += [jax.ShapeDtypeStruct((1, w), F32) for w in vecs_out]
    out_specs = [pl.BlockSpec((tile, w), lambda i: (i, 0)) for w, _ in rows_out]
    out_specs += [pl.BlockSpec((1, w), lambda i: (0, 0)) for w in vecs_out]
    n_in, n_ro = len(args), len(rows_out)

    def body(*refs):
        vals = [r[...] for r in refs[:n_in]]
        outs = refs[n_in:]
        row_vals, vec_vals = fn(*vals)
        for ref, val in zip(outs[:n_ro], row_vals):
            if isinstance(val, tuple):
                w = val[0].shape[1]
                for j, piece in enumerate(val):
                    ref[:, j * w:(j + 1) * w] = piece.astype(ref.dtype)
            else:
                ref[...] = val.astype(ref.dtype)
        if vecs_out:
            @pl.when(pl.program_id(0) == 0)
            def _():
                for ref in outs[n_ro:]:
                    ref[...] = jnp.zeros_like(ref)
            for ref, val in zip(outs[n_ro:], vec_vals):
                ref[...] += val

    return _call(body, grid=(n_tiles,), in_specs=in_specs, out_specs=out_specs, out_shape=out_shape, args=args,
                 name=name, comm=comm)


def _colsum(x):
    return jnp.sum(x, axis=0, keepdims=True)


def _rms_stats(h):
    r = lax.rsqrt(jnp.mean(h * h, axis=-1, keepdims=True) + RMS_EPS)
    return r, h * r


def _rms_back(r, xn, dxn):
    return r * (dxn - xn * jnp.mean(dxn * xn, axis=-1, keepdims=True))


def _branch_back(dh, f, gate, coef):
    return (coef * gate) * dh, coef * _colsum(f.astype(F32) * dh)


def _norm_mod_back(dn, h, dh_in, gain, scale):
    dn = dn.astype(F32)
    r, xn = _rms_stats(h)
    y = xn * gain
    dy = dn * (1.0 + scale)
    dh = dh_in + _rms_back(r, xn, dy * gain)
    return dh, [_colsum(dn), _colsum(dn * y), _colsum(dy * xn)]


def _norm_mod_bwd(dn, h, dh_in, gain, scale, name, comm=None):
    d = h.shape[1]

    def fn(dn, h, dh_in, gain, scale):
        dh, vecs = _norm_mod_back(dn, h, dh_in, gain, scale)
        return [dh], vecs
    return _rows(fn, [dn, h, dh_in], [gain, scale], [(d, F32)], [d, d, d], tile=256, name=name, comm=comm)


def _mm_norm_mod_bwd(pairs, h, dh_in, gain, scale, branch, name, tm, comm=None):
    f, gate, coef = branch

    def epi(accs, ex, vc):
        dh, vecs = _norm_mod_back(accs[0], ex[0], ex[1], vc[0], vc[1])
        df, dgate = _branch_back(dh, ex[2], vc[2], coef)
        return [dh, df] + vecs + [dgate]
    return _mm([pairs], epi, [h, dh_in, f], [gain, scale, gate], [F32, BF16], trans_rhs=False, tm=tm,
               tn=h.shape[1], name=name, n_sums=4, comm=comm)


def _last_mm_loss(lhs, w, res, gate, coef, target, gain, name):
    d = w.shape[1]

    def epi(accs, ex, vc):
        f = accs[0]
        h = ex[0] + (coef * vc[0]) * f
        r, xn = _rms_stats(h)
        err = xn * vc[1] - ex[1]
        dout = err * (1.0 / d)
        dh = _rms_back(r, xn, dout * vc[1])
        df, dgate = _branch_back(dh, f, vc[0], coef)
        return [dh, df, _colsum(err * err), _colsum(dout * xn), dgate]
    return _mm([[(lhs, w)]], epi, [res, target], [gate, gain], [F32, BF16], trans_rhs=False, tm=256, tn=d,
               name=name, n_sums=3)


def _partner(x):
    if x.shape[1] > LANES:
        return jnp.concatenate([_partner(x[:, c:c + LANES]) for c in range(0, x.shape[1], LANES)], axis=1)
    lane = lax.broadcasted_iota(jnp.int32, x.shape, 1) % HEAD_DIM
    return jnp.where(lane < HALF_HEAD, pltpu.roll(x, LANES - HALF_HEAD, 1), pltpu.roll(x, HALF_HEAD, 1))


def _proj_rope(n, w_t, cos, sin_signed, width, name, comm=None):
    s, kdim = n.shape
    n_cols = w_t.shape[0]
    tm = _tile(s, 1024, 8)
    qscale = HEAD_DIM ** -0.5

    chunk = _tile(tm, 256, 8)

    def body(n_ref, w_ref, cos_ref, sin_ref, o_ref):
        j = pl.program_id(0)

        def products(rows):
            return lax.dot_general(n_ref[rows, :].astype(BF16), w_ref[...].astype(BF16), (((1,), (1,)), ((), ())),
                                   preferred_element_type=F32)

        @pl.when(j >= 2)
        def _():
            for c in range(tm // chunk):
                rows = slice(c * chunk, (c + 1) * chunk)
                o_ref[rows, :] = products(rows)

        @pl.when(j < 2)
        def _():
            scale = jnp.where(j == 0, qscale, 1.0)
            for c in range(tm // chunk):
                rows = slice(c * chunk, (c + 1) * chunk)
                acc = products(rows)
                cos = jnp.tile(cos_ref[rows, :], (1, width // LANES))
                sin = jnp.tile(sin_ref[rows, :], (1, width // LANES))
                o_ref[rows, :] = scale * (acc * cos + _partner(acc) * sin)

    table = pl.BlockSpec((tm, LANES), lambda j, i: (jnp.where(j < 2, i, 0), 0))
    return _call(
        body, grid=(n_cols // width, s // tm),
        in_specs=[pl.BlockSpec((tm, kdim), lambda j, i: (i, 0)), pl.BlockSpec((width, kdim), lambda j, i: (j, 0)),
                  table, table],
        out_specs=[pl.BlockSpec((tm, width), lambda j, i: (i, j))],
        out_shape=[jax.ShapeDtypeStruct((s, n_cols), F32)], args=(n, w_t, cos, sin_signed), name=name, comm=comm)


def _mix_post(attn, u1, attn_g, ln_g, ln_b, conv_g):
    _, xa = _rms_stats(attn)
    mu = jnp.mean(u1, axis=-1, keepdims=True)
    xc = u1 - mu
    rstd = lax.rsqrt(jnp.mean(xc * xc, axis=-1, keepdims=True) + LN_EPS)
    u2 = (xc * rstd) * ln_g + ln_b
    u3 = u2 * _sigmoid(u2)
    _, x3 = _rms_stats(u3)
    return jnp.concatenate([xa * attn_g, x3 * conv_g], axis=1)


def _mix_post_back(dy, attn, u1, attn_g, ln_g, ln_b, conv_g):
    w = attn.shape[1]
    dya, dyc = dy[:, :w], dy[:, w:]
    ra, xa = _rms_stats(attn)
    dattn = _rms_back(ra, xa, dya * attn_g)
    mu = jnp.mean(u1, axis=-1, keepdims=True)
    xc = u1 - mu
    rstd = lax.rsqrt(jnp.mean(xc * xc, axis=-1, keepdims=True) + LN_EPS)
    xh = xc * rstd
    u2 = xh * ln_g + ln_b
    sig = _sigmoid(u2)
    u3 = u2 * sig
    r3, x3 = _rms_stats(u3)
    du3 = _rms_back(r3, x3, dyc * conv_g)
    du2 = du3 * (sig + u3 * (1.0 - sig))
    dxh = du2 * ln_g
    du1 = rstd * (dxh - jnp.mean(dxh, axis=-1, keepdims=True) - xh * jnp.mean(dxh * xh, axis=-1, keepdims=True))
    return dattn, du1, [_colsum(dya * xa), _colsum(dyc * x3), _colsum(du2 * xh), _colsum(du2)]


def _silu_rows(c_all, name):
    def fn(c):
        return [c * _sigmoid(c)], []
    return _rows(fn, [c_all], [], [(c_all.shape[1], BF16)], [], tile=c_all.shape[0], name=name)[0]


def _mm(groups, epi, extras, vecs, outs, *, trans_rhs, tm, tn, name, n_sums=0, pre=None, pre_inputs=(),
        comm=None):
    m = (pre_inputs[0] if pre is not None else groups[0][0][0]).shape[0]
    n = groups[0][0][1].shape[0] if trans_rhs else groups[0][0][1].shape[1]
    tm, tn = min(tm, m), min(tn, n)
    in_specs, args, uses_pre = [], [], []
    for grp in groups:
        for lhs, rhs in grp:
            k = rhs.shape[1] if trans_rhs else rhs.shape[0]
            uses_pre.append(lhs is None)
            if lhs is not None:
                in_specs.append(pl.BlockSpec((tm, k), lambda j, i: (i, 0)))
                args.append(lhs)
            in_specs.append(pl.BlockSpec((tn, k), lambda j, i: (j, 0)) if trans_rhs
                            else pl.BlockSpec((k, tn), lambda j, i: (0, j)))
            args.append(rhs)
    n_mm = len(args)
    for p in pre_inputs:
        in_specs.append(pl.BlockSpec((tm, p.shape[1]), lambda j, i: (i, 0)))
        args.append(p)
    for e in extras:
        in_specs.append(pl.BlockSpec((tm, tn), lambda j, i: (i, j)) if e.shape[1] == n
                        else pl.BlockSpec((tm, e.shape[1]), lambda j, i: (i, 0)))
        args.append(e)
    for v in vecs:
        in_specs.append(pl.BlockSpec((1, tn), lambda j, i: (0, j)) if v.shape[1] == n
                        else pl.BlockSpec((1, v.shape[1]), lambda j, i: (0, 0)))
        args.append(v)
    sizes = [len(g) for g in groups]
    n_pre, n_ex, n_vec = len(pre_inputs), len(extras), len(vecs)
    dims = (((1,), (1,)), ((), ())) if trans_rhs else (((1,), (0,)), ((), ()))
    out_specs, out_shape = [], []
    if pre is not None:
        k_pre = args[n_mm - 1].shape[1] if trans_rhs else args[n_mm - 1].shape[0]
        out_specs.append(pl.BlockSpec((tm, k_pre), lambda j, i: (i, 0)))
        out_shape.append(jax.ShapeDtypeStruct((m, k_pre), BF16))
    for o in outs:
        dt, width = o if isinstance(o, tuple) else (o, n)
        out_specs.append(pl.BlockSpec((tm, tn), lambda j, i: (i, j)) if width == n
                         else pl.BlockSpec((tm, width), lambda j, i: (i, 0)))
        out_shape.append(jax.ShapeDtypeStruct((m, width), dt))
    n_tiles_out = len(out_specs)
    out_specs += [pl.BlockSpec((1, tn), lambda j, i: (0, j))] * n_sums
    out_shape += [jax.ShapeDtypeStruct((1, n), F32)] * n_sums

    def body(*refs):
        ins = refs[:n_mm + n_pre + n_ex + n_vec]
        out_refs = refs[n_mm + n_pre + n_ex + n_vec:]
        vc = [r[...] for r in ins[n_mm + n_pre + n_ex:]]
        vals = []
        made = None
        if pre is not None:
            made = pre([r[...] for r in ins[n_mm:n_mm + n_pre]], vc).astype(BF16)
            vals.append(made)
        accs, pos, pair = [], 0, 0
        for size in sizes:
            acc = None
            for _ in range(size):
                if uses_pre[pair]:
                    lhs_tile = made
                else:
                    lhs_tile = ins[pos][...].astype(BF16)
                    pos += 1
                part = lax.dot_general(lhs_tile, ins[pos][...].astype(BF16), dims, preferred_element_type=F32)
                acc = part if acc is None else acc + part
                pos += 1
                pair += 1
            accs.append(acc)
        ex = [r[...] for r in ins[n_mm + n_pre:n_mm + n_pre + n_ex]]
        vals += epi(accs, ex, vc)
        for ref, val in zip(out_refs[:n_tiles_out], vals):
            ref[...] = val.astype(ref.dtype)
        if n_sums:
            @pl.when(pl.program_id(1) == 0)
            def _():
                for ref in out_refs[n_tiles_out:]:
                    ref[...] = jnp.zeros_like(ref)
            for ref, val in zip(out_refs[n_tiles_out:], vals[n_tiles_out:]):
                ref[...] += val

    return _call(body, grid=(n // tn, m // tm), in_specs=in_specs, out_specs=out_specs, out_shape=out_shape,
                 args=args, name=name, comm=comm)


def _mm_tn(lhs, rhs, name, comm=None):
    t, a = lhs.shape
    b = rhs.shape[1]
    ta = a if a <= 1536 else _tile(a, 1536, LANES)
    tk = _tile(t, 2048, 8)

    def body(l_ref, r_ref, o_ref):
        @pl.when(pl.program_id(1) == 0)
        def _():
            o_ref[...] = jnp.zeros_like(o_ref)
        o_ref[...] += lax.dot_general(l_ref[...].astype(BF16), r_ref[...].astype(BF16), (((0,), (0,)), ((), ())),
                                      preferred_element_type=F32)

    res = _call(body, grid=(a // ta, t // tk),
                in_specs=[pl.BlockSpec((tk, ta), lambda i, k: (k, i)), pl.BlockSpec((tk, b), lambda i, k: (k, 0))],
                out_specs=[pl.BlockSpec((ta, b), lambda i, k: (i, 0))], out_shape=[jax.ShapeDtypeStruct((a, b), F32)],
                args=(lhs, rhs), name=name, comm=comm)
    return res[0] if comm is None else (res[0][0], res[1])


def _ffn_tn(f):
    return _tile(f, 1536, LANES)


def _ffn_up(n, wg_t, wu_t, name, comm=None):
    def epi(accs, ex, vc):
        a, b = accs
        return [a, b, (a * _sigmoid(a)) * b]
    return _mm([[(n, wg_t)], [(n, wu_t)]], epi, [], [], [BF16, BF16, BF16], trans_rhs=True, tm=512,
               tn=_ffn_tn(wg_t.shape[0]), name=name, comm=comm)


def _norm_gate(h, gain, scale, shift, wg_t, name, comm=None):
    def pre(tiles, vc):
        _, xn = _rms_stats(tiles[0])
        return (xn * vc[0]) * (1.0 + vc[1]) + vc[2]

    def epi(accs, ex, vc):
        return [accs[0]]
    return _mm([[(None, wg_t)]], epi, [], [gain, scale, shift], [BF16], trans_rhs=True, tm=512,
               tn=wg_t.shape[0], name=name, pre=pre, pre_inputs=[h], comm=comm)


def _mix_out(attn, u1, post, w, res, gate, norm, name):
    def pre(tiles, vc):
        return _mix_post(tiles[0], tiles[1], *vc[4:8])

    def epi(accs, ex, vc):
        h = ex[0] + vc[0] * accs[0]
        _, xn = _rms_stats(h)
        return [h, accs[0], (xn * vc[1]) * (1.0 + vc[2]) + vc[3]]
    return _mm([[(None, w)]], epi, [res], [gate] + list(norm) + list(post), [F32, BF16, BF16], trans_rhs=False,
               tm=512, tn=w.shape[1], name=name, pre=pre, pre_inputs=[attn, u1])


def _mix_dy_post_bwd(dmix, w, attn, u1, post, name):
    width = attn.shape[1]

    def epi(accs, ex, vc):
        dattn, du1, sums = _mix_post_back(accs[0], ex[0], ex[1], *vc)
        return [dattn, du1, jnp.concatenate(sums[0:2], axis=1), jnp.concatenate(sums[2:4], axis=1)]
    return _mm([[(dmix, w)]], epi, [attn, u1], list(post), [(F32, width), (F32, width)], trans_rhs=True, tm=256,
               tn=w.shape[0], name=name, n_sums=2)


def _ffn_up_given_gate(n, wu_t, a, name, comm=None):
    def epi(accs, ex, vc):
        av = ex[0].astype(F32)
        return [accs[0], (av * _sigmoid(av)) * accs[0]]
    return _mm([[(n, wu_t)]], epi, [a], [], [BF16, BF16], trans_rhs=True, tm=512, tn=_ffn_tn(wu_t.shape[0]),
               name=name, comm=comm)


def _residual_mm(lhs, w, res, gate, coef, name, norm=None, comm=None):
    def epi(accs, ex, vc):
        h = ex[0] + (coef * vc[0]) * accs[0]
        if norm is None:
            return [h, accs[0]]
        _, xn = _rms_stats(h)
        return [h, accs[0], (xn * vc[1]) * (1.0 + vc[2]) + vc[3]]
    vecs = [gate] + (list(norm) if norm is not None else [])
    outs = [F32, BF16] + ([BF16] if norm is not None else [])
    return _mm([[(lhs, w)]], epi, [res], vecs, outs, trans_rhs=False, tm=512, tn=w.shape[1], name=name, comm=comm)


def _ffn_bwd_hidden(df, wd, a, b, name, comm=None):
    def epi(accs, ex, vc):
        dh = accs[0]
        av, bv = ex[0].astype(F32), ex[1].astype(F32)
        sig = _sigmoid(av)
        silu = av * sig
        return [dh * bv * (sig + silu * (1.0 - sig)), dh * silu]
    return _mm([[(df, wd)]], epi, [a, b], [], [BF16, BF16], trans_rhs=True, tm=512, tn=_ffn_tn(wd.shape[0]),
               name=name, comm=comm)


def _plain_mm(pairs, out_dtype, trans_rhs, tn, name, tm=512, comm=None):
    def epi(accs, ex, vc):
        return [accs[0]]
    res = _mm([pairs], epi, [], [], [out_dtype], trans_rhs=trans_rhs, tm=tm, tn=tn, name=name, comm=comm)
    return res[0] if comm is None else (res[0][0], res[1])


HEADS_PER_TILE = LANES // HEAD_DIM


def _stack_heads(x):
    lane = lax.broadcasted_iota(jnp.int32, (1, LANES), 1)
    return jnp.concatenate([x * (lane // HEAD_DIM == h).astype(F32) for h in range(HEADS_PER_TILE)], axis=0)


def _unstack_heads(y):
    r = y.shape[0] // HEADS_PER_TILE
    lane = lax.broadcasted_iota(jnp.int32, (r, y.shape[1]), 1)
    out = y[0:r]
    for h in range(1, HEADS_PER_TILE):
        out = jnp.where(lane // HEAD_DIM == h, y[h * r:(h + 1) * r], out)
    return out


def _stacked_lse(lb):
    return jnp.concatenate([_lane_pick(lb, h) for h in range(HEADS_PER_TILE)], axis=0)


def _band_masks(n_row_blocks, n_col_blocks):
    shape = (n_row_blocks * BLOCK, n_col_blocks * BLOCK)
    qi = lax.broadcasted_iota(jnp.int32, shape, 0) % BLOCK
    kj = lax.broadcasted_iota(jnp.int32, shape, 1) % BLOCK
    return kj <= qi, kj >= qi


def _query_masks():
    first_valid, _ = _band_masks(HEADS_PER_TILE, 1)
    same_ok, before_ok = _band_masks(HEADS_PER_TILE, 2)
    is_cur = lax.broadcasted_iota(jnp.int32, same_ok.shape, 1) >= BLOCK
    return first_valid, jnp.logical_and(is_cur, same_ok), jnp.logical_and(jnp.logical_not(is_cur), before_ok)


def _dot_nt(a, b):
    return lax.dot_general(a.astype(BF16), b.astype(BF16), (((1,), (1,)), ((), ())), preferred_element_type=F32)


def _dot_nn(a, b):
    return lax.dot_general(a.astype(BF16), b.astype(BF16), (((1,), (0,)), ((), ())), preferred_element_type=F32)


def _dot_tn(a, b):
    return lax.dot_general(a.astype(BF16), b.astype(BF16), (((0,), (0,)), ((), ())), preferred_element_type=F32)


def _lane_pick(x, h):
    lane = lax.broadcasted_iota(jnp.int32, x.shape, 1)
    return jnp.sum(jnp.where(lane == h * HEAD_DIM, x, 0.0), axis=1, keepdims=True)


def _block_rows(idx, d):
    span = BLOCK * d
    q0 = (idx // d) * span + idx % d
    return pl.ds(q0, BLOCK, stride=d), pl.ds(q0 - span, BLOCK, stride=d)


def _branch_loops(n_blocks, d, visit, unroll, masks):
    first_valid, cur_part, prev_part = masks
    if d % unroll == 0 and (n_blocks - d) % unroll == 0:
        full_valid = jnp.logical_or(cur_part, prev_part)

        def first(idx, carry):
            rows = pl.ds(idx, BLOCK, stride=d)
            visit(rows, [rows], first_valid)
            return carry

        def rest(idx, carry):
            rows, prev = _block_rows(idx, d)
            visit(rows, [prev, rows], full_valid)
            return carry

        lax.fori_loop(0, d, first, 0, unroll=unroll)
        lax.fori_loop(d, n_blocks, rest, 0, unroll=unroll)
        return

    def every(idx, carry):
        span = BLOCK * d
        q0 = (idx // d) * span + idx % d
        has_prev = idx >= d
        rows = pl.ds(q0, BLOCK, stride=d)
        prev = pl.ds(jnp.where(has_prev, q0 - span, q0), BLOCK, stride=d)
        visit(rows, [prev, rows], jnp.logical_or(cur_part, jnp.logical_and(prev_part, has_prev)))
        return carry

    lax.fori_loop(0, n_blocks, every, 0, unroll=unroll)


def _qkv_specs(s, tiles):
    q, k, v = [pl.BlockSpec((s, LANES), functools.partial(lambda hb, off: (0, off + hb), off=i * tiles))
               for i in range(3)]
    return q, k, v, pl.BlockSpec((s, LANES), lambda hb: (0, hb))


def _attn_seq_fwd(proj, width, name, comm=None):
    s = proj.shape[0]
    q_spec, k_spec, v_spec, cur = _qkv_specs(s, width // LANES)

    def body(q_ref, k_ref, v_ref, o_ref, l_ref, o_s, l_s):
        masks = _query_masks()
        for bi, d in enumerate(DILATIONS):
            def visit(rows, key_rows, valid, bi=bi):
                q2 = _stack_heads(q_ref[rows, :])
                keys = jnp.concatenate([k_ref[r, :] for r in key_rows], axis=0)
                vals = jnp.concatenate([v_ref[r, :] for r in key_rows], axis=0)
                sc = jnp.where(valid, _dot_nt(q2, keys), NEG)
                mx = jnp.max(sc, axis=1, keepdims=True)
                p = jnp.exp(sc - mx)
                den = jnp.sum(p, axis=1, keepdims=True)
                o_s[bi, rows, :] = _unstack_heads(_dot_nn(p, vals) / den)
                l_s[bi, rows, :] = _unstack_heads(jnp.broadcast_to(mx + jnp.log(den), (q2.shape[0], LANES)))

            _branch_loops(s // BLOCK, d, visit, 8, masks)
        for c in range(s // MERGE_CHUNK):
            rows = slice(c * MERGE_CHUNK, (c + 1) * MERGE_CHUNK)
            ls = [l_s[bi, rows, :] for bi in range(len(DILATIONS))]
            top = functools.reduce(jnp.maximum, ls)
            ws = [jnp.exp(l - top) for l in ls]
            den = functools.reduce(lambda a, b: a + b, ws)
            num = functools.reduce(lambda a, b: a + b, [w * o_s[bi, rows, :] for bi, w in enumerate(ws)])
            o_ref[rows, :] = num / den
            l_ref[rows, :] = top + jnp.log(den)

    return _call(
        body, grid=(width // LANES,), in_specs=[q_spec, k_spec, v_spec], out_specs=[cur, cur],
        out_shape=[jax.ShapeDtypeStruct((s, width), F32)] * 2,
        scratch_shapes=[pltpu.VMEM((len(DILATIONS), s, LANES), F32)] * 2,
        args=(proj, proj, proj), name=name, comm=comm)


def _attn_seq_bwd(proj, do, o, lse, cos, sin_signed, name, comm=None):
    s, width = do.shape
    q_spec, k_spec, v_spec, cur = _qkv_specs(s, width // LANES)
    table = pl.BlockSpec((s, LANES), lambda hb: (0, 0))
    qscale = HEAD_DIM ** -0.5

    def body(q_ref, k_ref, v_ref, do_ref, o_ref, l_ref, cos_ref, sin_ref, dq_out, dk_out, dv_out,
             dq_ref, dk_ref, dv_ref):
        dq_ref[...] = jnp.zeros_like(dq_ref)
        dk_ref[...] = jnp.zeros_like(dk_ref)
        dv_ref[...] = jnp.zeros_like(dv_ref)
        masks = _query_masks()
        for d in DILATIONS:
            def visit(rows, key_rows, valid):
                dob = do_ref[rows, :]
                q2 = _stack_heads(q_ref[rows, :])
                do2 = _stack_heads(dob)
                delta = jnp.sum(_stack_heads(dob * o_ref[rows, :]), axis=1, keepdims=True)
                lse2 = _stacked_lse(l_ref[rows, :])
                keys = jnp.concatenate([k_ref[r, :] for r in key_rows], axis=0)
                vals = jnp.concatenate([v_ref[r, :] for r in key_rows], axis=0)
                p = jnp.where(valid, jnp.exp(_dot_nt(q2, keys) - lse2), 0.0)
                ds = p * (_dot_nt(do2, vals) - delta)
                dq_ref[rows, :] += _unstack_heads(_dot_nn(ds, keys))
                dkk = _dot_tn(ds, q2)
                dvv = _dot_tn(p, do2)
                for i, r in enumerate(key_rows):
                    dk_ref[r, :] += dkk[i * BLOCK:(i + 1) * BLOCK]
                    dv_ref[r, :] += dvv[i * BLOCK:(i + 1) * BLOCK]

            _branch_loops(s // BLOCK, d, visit, 8, masks)
        for c in range(s // MERGE_CHUNK):
            rows = slice(c * MERGE_CHUNK, (c + 1) * MERGE_CHUNK)
            cos, sin = cos_ref[rows, :], sin_ref[rows, :]
            dq, dk = dq_ref[rows, :], dk_ref[rows, :]
            dq_out[rows, :] = ((dq * cos - _partner(dq) * sin) * qscale).astype(BF16)
            dk_out[rows, :] = (dk * cos - _partner(dk) * sin).astype(BF16)
            dv_out[rows, :] = dv_ref[rows, :].astype(BF16)

    return _call(
        body, grid=(width // LANES,), in_specs=[q_spec, k_spec, v_spec, cur, cur, cur, table, table],
        out_specs=[cur, cur, cur], out_shape=[jax.ShapeDtypeStruct((s, width), BF16)] * 3,
        scratch_shapes=[pltpu.VMEM((s, LANES), F32)] * 3,
        args=(proj, proj, proj, do, o, lse, cos, sin_signed), name=name, comm=comm)


def _conv_specs(s, a_block, b_block):
    per = CONV_CHUNK // CONV_HALO
    a_cur = pl.BlockSpec((CONV_CHUNK, LANES), lambda cb, i: (i, a_block + cb))
    b_cur = pl.BlockSpec((CONV_CHUNK, LANES), lambda cb, i: (i, b_block + cb))
    a_halo = pl.BlockSpec((CONV_HALO, LANES), lambda cb, i: (jnp.maximum(i * per - 1, 0), a_block + cb))
    b_halo = pl.BlockSpec((CONV_HALO, LANES), lambda cb, i: (jnp.maximum(i * per - 1, 0), b_block + cb))
    w_spec = pl.BlockSpec((CONV_KERNEL, LANES), lambda cb, i: (0, cb))
    vec = pl.BlockSpec((1, LANES), lambda cb, i: (0, cb))
    out = pl.BlockSpec((CONV_CHUNK, LANES), lambda cb, i: (i, cb))
    return a_cur, b_cur, a_halo, b_halo, w_spec, vec, out


def _fill_glu_window(win, a_ref, b_ref, ah_ref, bh_ref, first):
    halo = ah_ref[...] * _sigmoid(bh_ref[...])
    win[0:CONV_HALO, :] = jnp.where(first, 0.0, halo)
    win[CONV_HALO:, :] = a_ref[...] * _sigmoid(b_ref[...])


def _conv_fwd(proj, a_block, b_block, w, bias, name, comm=None):
    s = proj.shape[0]
    cw = w.shape[1]
    a_cur, b_cur, a_halo, b_halo, w_spec, vec, out = _conv_specs(s, a_block, b_block)
    lead = CONV_HALO - (CONV_KERNEL - 1)

    def body(a_ref, b_ref, ah_ref, bh_ref, w_ref, bias_ref, o_ref, win):
        _fill_glu_window(win, a_ref, b_ref, ah_ref, bh_ref, pl.program_id(1) == 0)
        for sub in range(CONV_CHUNK // CONV_SUB):
            base = sub * CONV_SUB
            acc = jnp.zeros((CONV_SUB, LANES), F32) + bias_ref[...]
            for j in range(CONV_KERNEL):
                acc = acc + w_ref[j:j + 1, :] * win[base + lead + j:base + lead + j + CONV_SUB, :]
            o_ref[base:base + CONV_SUB, :] = acc

    return _call(
        body, grid=(cw // LANES, s // CONV_CHUNK), in_specs=[a_cur, b_cur, a_halo, b_halo, w_spec, vec],
        out_specs=[out], out_shape=[jax.ShapeDtypeStruct((s, cw), F32)],
        scratch_shapes=[pltpu.VMEM((CONV_CHUNK + CONV_HALO, LANES), F32)],
        args=(proj, proj, proj, proj, w, bias), name=name, comm=comm)


def _conv_bwd(proj, a_block, b_block, w, du1, name):
    s = proj.shape[0]
    cw = w.shape[1]
    a_cur, b_cur, a_halo, b_halo, w_spec, vec, out = _conv_specs(s, a_block, b_block)
    per = CONV_CHUNK // CONV_HALO
    n_chunks = s // CONV_CHUNK
    d_next = pl.BlockSpec((CONV_HALO, LANES), lambda cb, i: (jnp.minimum((i + 1) * per, s // CONV_HALO - 1), cb))
    lead = CONV_HALO - (CONV_KERNEL - 1)

    def body(a_ref, b_ref, ah_ref, bh_ref, w_ref, d_ref, dn_ref, da_ref, db_ref, dw_ref, dbias_ref, win, dwin):
        i = pl.program_id(1)
        _fill_glu_window(win, a_ref, b_ref, ah_ref, bh_ref, i == 0)
        dwin[0:CONV_CHUNK, :] = d_ref[...]
        dwin[CONV_CHUNK:, :] = jnp.where(i == n_chunks - 1, 0.0, dn_ref[...])

        @pl.when(i == 0)
        def _():
            dw_ref[...] = jnp.zeros_like(dw_ref)
            dbias_ref[...] = jnp.zeros_like(dbias_ref)

        dbias_ref[...] += _colsum(d_ref[...])
        for sub in range(CONV_CHUNK // CONV_SUB):
            base = sub * CONV_SUB
            dcur = dwin[base:base + CONV_SUB, :]
            du0 = jnp.zeros((CONV_SUB, LANES), F32)
            for j in range(CONV_KERNEL):
                back = CONV_KERNEL - 1 - j
                du0 = du0 + w_ref[j:j + 1, :] * dwin[base + back:base + back + CONV_SUB, :]
                dw_ref[j:j + 1, :] += _colsum(dcur * win[base + lead + j:base + lead + j + CONV_SUB, :])
            av = a_ref[base:base + CONV_SUB, :]
            sig = _sigmoid(b_ref[base:base + CONV_SUB, :])
            da_ref[base:base + CONV_SUB, :] = (du0 * sig).astype(BF16)
            db_ref[base:base + CONV_SUB, :] = (du0 * av * sig * (1.0 - sig)).astype(BF16)

    return pl.pallas_call(
        body, grid=(cw // LANES, n_chunks), in_specs=[a_cur, b_cur, a_halo, b_halo, w_spec, out, d_next],
        out_specs=[out, out, w_spec, vec],
        out_shape=[jax.ShapeDtypeStruct((s, cw), BF16), jax.ShapeDtypeStruct((s, cw), BF16),
                   jax.ShapeDtypeStruct((CONV_KERNEL, cw), F32), jax.ShapeDtypeStruct((1, cw), F32)],
        scratch_shapes=[pltpu.VMEM((CONV_CHUNK + CONV_HALO, LANES), F32)] * 2,
        compiler_params=_params(2), name=name)(proj, proj, proj, proj, w, du1, du1)


def _adamw_math(w, g, m, v):
    m = ADAM_B1 * m + (1.0 - ADAM_B1) * g
    v = ADAM_B2 * v + (1.0 - ADAM_B2) * (g * g)
    m_hat = m / (1.0 - ADAM_B1 ** ADAM_STEP)
    v_hat = v / (1.0 - ADAM_B2 ** ADAM_STEP)
    delta = -ADAM_LR * (m_hat / (jnp.sqrt(v_hat) + ADAM_EPS) + ADAM_WD * w)
    return delta, m, v


def _adamw_big(w, g, m, v, name):
    rows, cols = w.shape
    tile = _tile(rows, 256, 8)
    spec = pl.BlockSpec((tile, cols), lambda i: (i, 0))

    def body(w_ref, g_ref, m_ref, v_ref, d_out, m_out, v_out):
        d_out[...], m_out[...], v_out[...] = _adamw_math(w_ref[...], g_ref[...], m_ref[...], v_ref[...])

    return pl.pallas_call(body, grid=(rows // tile,), in_specs=[spec] * 4, out_specs=[spec] * 3,
                          out_shape=[jax.ShapeDtypeStruct(w.shape, F32)] * 3, compiler_params=_params(1),
                          name=name)(w, g, m, v)


def _adamw_reduced(w, land, m, v, name):
    rows, cols = w.shape
    tile = _tile(rows, 256, 16)
    spec = pl.BlockSpec((tile, cols), lambda i: (i, 0))

    def body(w_ref, l_ref, m_ref, v_ref, g_out, d_out, m_out, v_out):
        g = l_ref[0].astype(F32)
        for q in range(1, N_CHIP):
            g = g + l_ref[q].astype(F32)
        g_out[...] = g
        d_out[...], m_out[...], v_out[...] = _adamw_math(w_ref[...], g, m_ref[...], v_ref[...])

    return pl.pallas_call(body, grid=(rows // tile,),
                          in_specs=[spec, pl.BlockSpec((N_CHIP, tile, cols), lambda i: (0, i, 0)), spec, spec],
                          out_specs=[spec] * 4, out_shape=[jax.ShapeDtypeStruct(w.shape, F32)] * 4,
                          compiler_params=_params(1), name=name)(w, land, m, v)


def _adamw_small(ws, gs, ms, vs, name):
    n = len(ws)

    def body(*refs):
        ins, outs = refs[:4 * n], refs[4 * n:]
        for t in range(n):
            res = _adamw_math(ins[t][...], ins[n + t][...], ins[2 * n + t][...], ins[3 * n + t][...])
            for j in range(3):
                outs[j * n + t][...] = res[j]

    shapes = [jax.ShapeDtypeStruct(w.shape, F32) for w in ws]
    res = pl.pallas_call(body, out_shape=shapes * 3, compiler_params=pltpu.CompilerParams(vmem_limit_bytes=VMEM_LIMIT),
                         name=name)(*ws, *gs, *ms, *vs)
    return res[:n], res[n:2 * n], res[2 * n:]


def _sum_blocks(x, n_blocks, name):
    r = x.shape[0] // n_blocks

    def body(x_ref, o_ref):
        acc = x_ref[0:r, :]
        for b in range(1, n_blocks):
            acc = acc + x_ref[b * r:(b + 1) * r, :]
        o_ref[...] = acc

    return pl.pallas_call(body, out_shape=jax.ShapeDtypeStruct((r, x.shape[1]), F32),
                          compiler_params=pltpu.CompilerParams(vmem_limit_bytes=VMEM_LIMIT), name=name)(x)


def _coords():
    return lax.axis_index("x"), lax.axis_index("y"), lax.axis_index("c")


def _flip(v, bit):
    return 1 - v if bit else v


def _ag_small(x, name):
    r, c = x.shape

    def body(x_ref, o_ref, send, recv, local_sem):
        mx, my, mc = _coords()

        def rows(px, py, pc):
            return o_ref.at[pl.ds(pl.multiple_of((4 * px + 2 * py + pc) * r, 8), r), :]

        local = pltpu.make_async_copy(x_ref, rows(mx, my, mc), local_sem)
        local.start()
        peers = [(_flip(mx, k >> 2 & 1), _flip(my, k >> 1 & 1), _flip(mc, k & 1)) for k in range(1, N_DEV)]
        sends = [pltpu.make_async_remote_copy(x_ref, rows(mx, my, mc), send.at[k], recv.at[k], device_id=p,
                                              device_id_type=MESH) for k, p in enumerate(peers)]
        for cp in sends:
            cp.start()
        for k, p in enumerate(peers):
            pltpu.make_async_remote_copy(x_ref, rows(*p), send.at[k], recv.at[k], device_id=p,
                                         device_id_type=MESH).wait_recv()
        for cp in sends:
            cp.wait_send()
        local.wait()

    vm = pl.BlockSpec(memory_space=pltpu.VMEM)
    return pl.pallas_call(
        body, in_specs=[vm], out_specs=vm, out_shape=jax.ShapeDtypeStruct((N_DEV * r, c), x.dtype),
        scratch_shapes=[pltpu.SemaphoreType.DMA((N_DEV - 1,)), pltpu.SemaphoreType.DMA((N_DEV - 1,)),
                        pltpu.SemaphoreType.DMA(())],
        name=name)(x)


class _GatherSmall:
    mid = None

    def __init__(self, x):
        self.inputs = [x]
        self.out_shapes = [jax.ShapeDtypeStruct((N_DEV * x.shape[0], x.shape[1]), x.dtype)]
        self.scratch = [pltpu.SemaphoreType.DMA((N_DEV - 1,)), pltpu.SemaphoreType.DMA((N_DEV - 1,)),
                        pltpu.SemaphoreType.DMA(())]

    def _plan(self, x_refs, o_refs, sems):
        send, recv, local_sem = sems
        x_ref, o_ref = x_refs[0], o_refs[0]
        r = x_ref.shape[0]
        mx, my, mc = _coords()

        def rows(px, py, pc):
            return o_ref.at[pl.ds(pl.multiple_of((4 * px + 2 * py + pc) * r, 8), r), :]

        peers = [(_flip(mx, k >> 2 & 1), _flip(my, k >> 1 & 1), _flip(mc, k & 1)) for k in range(1, N_DEV)]
        out = [pltpu.make_async_remote_copy(x_ref, rows(mx, my, mc), send.at[k], recv.at[k], device_id=p,
                                            device_id_type=MESH) for k, p in enumerate(peers)]
        arrivals = [pltpu.make_async_remote_copy(x_ref, rows(*p), send.at[k], recv.at[k], device_id=p,
                                                 device_id_type=MESH) for k, p in enumerate(peers)]
        return out, arrivals, pltpu.make_async_copy(x_ref, rows(mx, my, mc), local_sem)

    def start(self, x_refs, o_refs, sems):
        out, _, local = self._plan(x_refs, o_refs, sems)
        local.start()
        for cp in out:
            cp.start()

    def finish(self, x_refs, o_refs, sems):
        out, arrivals, local = self._plan(x_refs, o_refs, sems)
        for cp in arrivals:
            cp.wait_recv()
        for cp in out:
            cp.wait_send()
        local.wait()


class _GatherWeights:
    def __init__(self, shards):
        n_t = len(shards)
        self.inputs = list(shards)
        self.out_shapes = [jax.ShapeDtypeStruct((N_DEV * x.shape[0], x.shape[1]), x.dtype) for x in shards]
        self.scratch = [pltpu.SemaphoreType.DMA((n_t, 7)), pltpu.SemaphoreType.DMA((n_t, 7)),
                        pltpu.SemaphoreType.DMA((n_t,))]

    def _plan(self, x_refs, o_refs, sems):
        send, recv, local_sem = sems
        mx, my, mc = _coords()
        me, sibling = (mx, my, mc), (mx, my, 1 - mc)
        chips = [(1 - mx, my), (mx, 1 - my), (1 - mx, 1 - my)]

        def rows(t, px, py, pc):
            r = x_refs[t].shape[0]
            return o_refs[t].at[pl.ds(pl.multiple_of((4 * px + 2 * py + pc) * r, 8), r), :]

        def copy(t, k, block, to, src=None):
            return pltpu.make_async_remote_copy(
                src_ref=rows(t, *block) if src is None else src, dst_ref=rows(t, *block),
                send_sem=send.at[t, k], recv_sem=recv.at[t, k], device_id=to, device_id_type=MESH)

        def local(t):
            return pltpu.make_async_copy(x_refs[t], rows(t, *me), local_sem.at[t])

        return me, sibling, chips, mc, copy, local

    def start(self, x_refs, o_refs, sems):
        me, sibling, chips, mc, copy, local = self._plan(x_refs, o_refs, sems)
        for t in range(len(x_refs)):
            local(t).start()
            copy(t, 0, me, sibling, src=x_refs[t]).start()
            for j, chip in enumerate(chips):
                copy(t, 1 + j, me, (*chip, mc), src=x_refs[t]).start()

    def mid(self, x_refs, o_refs, sems):
        me, sibling, chips, mc, copy, local = self._plan(x_refs, o_refs, sems)
        for j, chip in enumerate(chips):
            for t in range(len(x_refs)):
                copy(t, 1 + j, (*chip, mc), me).wait_recv()
                copy(t, 4 + j, (*chip, mc), sibling).start()

    def finish(self, x_refs, o_refs, sems):
        me, sibling, chips, mc, copy, local = self._plan(x_refs, o_refs, sems)
        for t in range(len(x_refs)):
            copy(t, 0, sibling, me).wait_recv()
            for j, chip in enumerate(chips):
                copy(t, 4 + j, (*chip, 1 - mc), me).wait_recv()
            copy(t, 0, me, sibling, src=x_refs[t]).wait_send()
            for j, chip in enumerate(chips):
                copy(t, 1 + j, me, (*chip, mc), src=x_refs[t]).wait_send()
                copy(t, 4 + j, (*chip, mc), sibling).wait_send()
            local(t).wait()


class _SiblingExchange:
    mid = None

    def __init__(self, grads):
        n_t = len(grads)
        self.inputs = list(grads)
        self.out_shapes = [jax.ShapeDtypeStruct((N_CHIP,) + g.shape[2:], F32) for g in grads]
        self.scratch = [pltpu.SemaphoreType.DMA((n_t,)), pltpu.SemaphoreType.DMA((n_t,))]

    def _copies(self, g_refs, land, sems):
        send, recv = sems
        mx, my, mc = _coords()
        return [pltpu.make_async_remote_copy(g_refs[t].at[:, 1 - mc], land[t], send.at[t], recv.at[t],
                                             device_id=(mx, my, 1 - mc), device_id_type=MESH)
                for t in range(len(g_refs))]

    def start(self, g_refs, land, sems):
        for cp in self._copies(g_refs, land, sems):
            cp.start()

    def finish(self, g_refs, land, sems):
        for cp in self._copies(g_refs, land, sems):
            cp.wait()


class _Together:
    def __init__(self, *comms):
        self.comms = comms
        self.inputs = [x for c in comms for x in c.inputs]
        self.out_shapes = [x for c in comms for x in c.out_shapes]
        self.scratch = [x for c in comms for x in c.scratch]
        self.mid = self._mid if any(c.mid is not None for c in comms) else None

    def _each(self, phase, cin, cout, sems):
        i = o = s = 0
        for c in self.comms:
            fn = getattr(c, phase)
            ni, no, ns = len(c.inputs), len(c.out_shapes), len(c.scratch)
            if fn is not None:
                fn(cin[i:i + ni], cout[o:o + no], sems[s:s + ns])
            i, o, s = i + ni, o + no, s + ns

    def start(self, cin, cout, sems):
        self._each("start", cin, cout, sems)

    def _mid(self, cin, cout, sems):
        self._each("mid", cin, cout, sems)

    def finish(self, cin, cout, sems):
        self._each("finish", cin, cout, sems)


def _standalone(comm, name):
    def body():
        pass
    return _call(body, grid=(1,), in_specs=[], out_specs=[], out_shape=[], args=(), name=name, comm=comm)[1]


def _chip_partials(g4s, lands, name):
    n_t = len(g4s)
    in_specs, out_specs, out_shape = [], [], []
    for g4 in g4s:
        _, _, r, c = g4.shape
        in_specs.append(pl.BlockSpec((None, None, r, c), lambda q: (q, lax.axis_index("c"), 0, 0)))
        out_specs.append(pl.BlockSpec((None, r, c), lambda q: (q, 0, 0)))
        out_shape.append(jax.ShapeDtypeStruct((N_CHIP, r, c), BF16))
    in_specs += [pl.BlockSpec((None,) + g4.shape[2:], lambda q: (q, 0, 0)) for g4 in g4s]

    def body(*refs):
        for t in range(n_t):
            refs[2 * n_t + t][...] = (refs[t][...] + refs[n_t + t][...]).astype(BF16)

    return pl.pallas_call(body, grid=(N_CHIP,), in_specs=in_specs, out_specs=out_specs, out_shape=out_shape,
                          compiler_params=_params(1), name=name)(*g4s, *lands)


class _ChipExchange:
    mid = None

    def __init__(self, parts):
        n_t = len(parts)
        self.inputs = list(parts)
        self.out_shapes = [jax.ShapeDtypeStruct(p.shape, p.dtype) for p in parts]
        self.scratch = [pltpu.SemaphoreType.DMA((n_t, 3)), pltpu.SemaphoreType.DMA((n_t, 3)),
                        pltpu.SemaphoreType.DMA((n_t,))]

    def _plan(self, p_refs, land, sems):
        send, recv, local_sem = sems
        mx, my, mc = _coords()
        my_chip = 2 * mx + my
        peers = [(_flip(mx, fx), _flip(my, fy)) for fx, fy in ((1, 0), (0, 1), (1, 1))]

        def out(t, k):
            px, py = peers[k]
            return pltpu.make_async_remote_copy(p_refs[t].at[2 * px + py], land[t].at[my_chip], send.at[t, k],
                                                recv.at[t, k], device_id=(px, py, mc), device_id_type=MESH)

        def arrival(t, k):
            px, py = peers[k]
            return pltpu.make_async_remote_copy(p_refs[t].at[my_chip], land[t].at[2 * px + py], send.at[t, k],
                                                recv.at[t, k], device_id=(px, py, mc), device_id_type=MESH)

        def local(t):
            return pltpu.make_async_copy(p_refs[t].at[my_chip], land[t].at[my_chip], local_sem.at[t])

        return out, arrival, local

    def start(self, p_refs, land, sems):
        out, arrival, local = self._plan(p_refs, land, sems)
        for t in range(len(p_refs)):
            local(t).start()
            for k in range(3):
                out(t, k).start()

    def finish(self, p_refs, land, sems):
        out, arrival, local = self._plan(p_refs, land, sems)
        for t in range(len(p_refs)):
            for k in range(3):
                arrival(t, k).wait_recv()
                out(t, k).wait_send()
            local(t).wait()


def _rope_tables(s, width):
    heads = width // HEAD_DIM
    inv_freq = ROPE_THETA ** (-jnp.arange(0, HEAD_DIM, 2, dtype=F32) / HEAD_DIM)
    inv_full = jnp.tile(inv_freq, 2 * heads)
    sign = jnp.tile(jnp.concatenate([-jnp.ones((HALF_HEAD,), F32), jnp.ones((HALF_HEAD,), F32)]), heads)
    ang = jnp.arange(s, dtype=F32)[:, None] * inv_full[None, :]
    return jnp.cos(ang), jnp.sin(ang) * sign[None, :]


def _pad_rows(v, rows):
    return jnp.concatenate([v, jnp.zeros((rows - 1, v.shape[1]), v.dtype)], axis=0)


def kernel(x, c, w_ada, b_ada, ffn1_norm_g, ffn1_w_gate, ffn1_w_up, ffn1_w_down, mix_norm_g, w_in, conv_dw_w, conv_dw_b, conv_ln_g, conv_ln_b, attn_out_g, conv_out_g, w_out, ffn2_norm_g, ffn2_w_gate, ffn2_w_up, ffn2_w_down, final_norm_g, loss_target, m_w_ada, m_b_ada, m_ffn1_norm_g, m_ffn1_w_gate, m_ffn1_w_up, m_ffn1_w_down, m_mix_norm_g, m_w_in, m_conv_dw_w, m_conv_dw_b, m_conv_ln_g, m_conv_ln_b, m_attn_out_g, m_conv_out_g, m_w_out, m_ffn2_norm_g, m_ffn2_w_gate, m_ffn2_w_up, m_ffn2_w_down, m_final_norm_g, v_w_ada, v_b_ada, v_ffn1_norm_g, v_ffn1_w_gate, v_ffn1_w_up, v_ffn1_w_down, v_mix_norm_g, v_w_in, v_conv_dw_w, v_conv_dw_b, v_conv_ln_g, v_conv_ln_b, v_attn_out_g, v_conv_out_g, v_w_out, v_ffn2_norm_g, v_ffn2_w_gate, v_ffn2_w_up, v_ffn2_w_down, v_final_norm_g):
    mx, my, mc = _coords()
    me = 4 * mx + 2 * my + mc
    s, d = x.shape[1], x.shape[2]
    aw = d // 2
    x2, target = x[0], loss_target[0]
    n_mod = w_ada.shape[2] * N_DEV // d
    mod_cols = w_ada.shape[2]

    def shard(w, transpose):
        return (w[0].T if transpose else w[0]).astype(BF16)

    cw_shard = conv_dw_w.shape[3]
    n_taps = CONV_KERNEL * cw_shard
    first_len = -(-(d + n_taps) // LANES) * LANES
    first = jnp.concatenate([c, conv_dw_w[0, :, 0, :].reshape(1, n_taps), jnp.zeros((1, first_len - d - n_taps), F32)], axis=1)
    first_all, wg1 = _standalone(
        _Together(_GatherSmall(_pad_rows(first, 8)), _GatherWeights([shard(ffn1_w_gate, True)])), "ag_first")
    first_all = first_all[0::8]
    c_all = first_all[:, :d]
    conv_w = first_all[:, d:d + n_taps].reshape(N_DEV, CONV_KERNEL, cw_shard).transpose(1, 0, 2).reshape(CONV_KERNEL, aw)

    silu_c = _silu_rows(c_all, "silu_c")
    mod_part = _plain_mm([(silu_c, w_ada[0])], F32, False, mod_cols, "mod_mm")
    mod_all = _ag_small(mod_part, "ag_mod").reshape(N_DEV, N_DEV, mod_cols)
    mod = lax.dynamic_index_in_dim(mod_all, me, axis=1, keepdims=False).reshape(1, n_mod * d) + b_ada
    sh1, sc1, g1, sh2, sc2, g2, sh3, sc3, g3 = [mod[:, i * d:(i + 1) * d] for i in range(n_mod)]

    def split(g):
        return g.reshape(N_CHIP, 2, g.shape[0] // N_DEV, g.shape[1])

    def partials(g4s, lands, tag):
        return _chip_partials(g4s, lands, "chip_partials_" + tag)

    (n1, a1), (wu1,) = _norm_gate(x2, ffn1_norm_g, sc1, sh1, wg1, "ffn1_gate",
                                  comm=_GatherWeights([shard(ffn1_w_up, True)]))
    (b1, hid1), (wd1,) = _ffn_up_given_gate(n1, wu1, a1, "ffn1_up", comm=_GatherWeights([shard(ffn1_w_down, False)]))
    (h1, f1, n2), (win_t,) = _residual_mm(hid1, wd1, x2, g1, 0.5, "ffn1_down", norm=(mix_norm_g, sc2, sh2),
                                          comm=_GatherWeights([shard(w_in, True)]))
    cos, sin_signed = _rope_tables(s, LANES)
    (proj,), (wd2,) = _proj_rope(n2, win_t, cos, sin_signed, aw, "proj",
                                 comm=_GatherWeights([shard(ffn2_w_down, False)]))
    lanes_per = aw // LANES
    (attn, lse), (wg2, wu2) = _attn_seq_fwd(
        proj, aw, "attn_fwd", comm=_GatherWeights([shard(ffn2_w_gate, True), shard(ffn2_w_up, True)]))
    (u1,), (wout,) = _conv_fwd(proj, 3 * lanes_per, 4 * lanes_per, conv_w, conv_dw_b, "conv_fwd",
                               comm=_GatherWeights([shard(w_out, False)]))
    post = (attn_out_g, conv_ln_g, conv_ln_b, conv_out_g)
    y, h2, mix, n3 = _mix_out(attn, u1, post, wout, h1, g2, (ffn2_norm_g, sc3, sh3), "mix_out")
    a3, b3, hid3 = _ffn_up(n3, wg2, wu2, "ffn2_up")

    dh3, df3, err2, d_final_g, dg3 = _last_mm_loss(hid3, wd2, h2, g3, 0.5, target, final_norm_g.reshape(1, d),
                                                   "ffn2_down_loss")
    loss_part = jnp.zeros((1, LANES), F32).at[0, 0].set(0.5 * jnp.sum(err2) / d)

    da3, db3 = _ffn_bwd_hidden(df3, wd2, a3, b3, "ffn2_hidden_bwd")
    g4_a = [split(_mm_tn(da3, n3, "ffn2_dwg")), split(_mm_tn(db3, n3, "ffn2_dwu")), split(_mm_tn(hid3, df3, "ffn2_dwd"))]
    (dh2, dmix, dsh3, dsc3, dgn3, dg2), land_a = _mm_norm_mod_bwd(
        [(da3, wg2), (db3, wu2)], h2, dh3, ffn2_norm_g, sc3, (mix, g2, 1.0), "ffn2_dn_norm3_bwd", tm=256,
        comm=_SiblingExchange(g4_a))
    parts_a = partials(g4_a, land_a, "a")
    g_wout = _mm_tn(y, dmix, "mix_dwout")
    dattn, du1, d_gains, d_ln = _mix_dy_post_bwd(dmix, wout, attn, u1, post, "mix_dy_post_bwd")
    d_attn_g, d_conv_g, d_ln_g, d_ln_b = d_gains[:, :aw], d_gains[:, aw:], d_ln[:, :aw], d_ln[:, aw:]
    dga, dgb, d_taps, d_conv_b = _conv_bwd(proj, 3 * lanes_per, 4 * lanes_per, conv_w, du1, "conv_bwd")
    (dq, dk, dv), sums_a = _attn_seq_bwd(proj, dattn, attn, lse, cos, sin_signed, "attn_bwd",
                                         comm=_ChipExchange(parts_a))
    dproj = jnp.concatenate([dq, dk, dv, dga, dgb], axis=1)
    g4_b = [split(g_wout), split(_mm_tn(dproj, n2, "mix_dwin"))]
    (dh1, df1, dsh2, dsc2, dgn2, dg1), land_b = _mm_norm_mod_bwd(
        [(dproj, win_t)], h1, dh2, mix_norm_g, sc2, (f1, g1, 0.5), "mix_dn_norm2_bwd", tm=512,
        comm=_SiblingExchange(g4_b))
    parts_b = partials(g4_b, land_b, "b")
    g4_c = [split(_mm_tn(hid1, df1, "ffn1_dwd"))]
    (da1, db1), both = _ffn_bwd_hidden(df1, wd1, a1, b1, "ffn1_hidden_bwd",
                                       comm=_Together(_ChipExchange(parts_b), _SiblingExchange(g4_c)))
    sums_b, land_c = both[:2], both[2:]
    parts_c = partials(g4_c, land_c, "c")
    g_wu1, sums_c = _mm_tn(db1, n1, "ffn1_dwu", comm=_ChipExchange(parts_c))
    g4_d = [split(g_wu1)]
    g_wg1, land_d = _mm_tn(da1, n1, "ffn1_dwg", comm=_SiblingExchange(g4_d))
    parts_d = partials(g4_d, land_d, "d")
    g4_e = [split(g_wg1)]
    dn1, both = _plain_mm([(da1, wg1), (db1, wu1)], BF16, False, d, "ffn1_dn",
                          comm=_Together(_ChipExchange(parts_d), _SiblingExchange(g4_e)))
    sums_d, land_e = both[:1], both[1:]
    parts_e = partials(g4_e, land_e, "e")
    (dx, dsh1, dsc1, dgn1), sums_e = _norm_mod_bwd(dn1, x2, dh1, ffn1_norm_g, sc1, "norm1_bwd",
                                                   comm=_ChipExchange(parts_e))

    dmod = jnp.concatenate([dsh1, dsc1, dg1, dsh2, dsc2, dg2, dsh3, dsc3, dg3], axis=1)
    small = [dmod, dgn1, dgn2, dgn3, d_final_g, d_conv_b, d_ln_g, d_ln_b, d_attn_g, d_conv_g,
             d_taps.reshape(1, CONV_KERNEL * aw), loss_part]
    sizes = [v.shape[1] for v in small]
    total = sum(sizes)
    padded = -(-total // (8 * LANES)) * (8 * LANES)
    packed = jnp.concatenate(small + [jnp.zeros((1, padded - total), F32)], axis=1).reshape(8, padded // 8)
    gathered = _ag_small(packed, "ag_small_grads")
    summed = _sum_blocks(gathered, N_DEV, "sum_small_grads").reshape(1, padded)
    offs = [sum(sizes[:i]) for i in range(len(sizes))]
    (g_b_ada, g_gn1, g_gn2, g_gn3, g_final, g_conv_b, g_ln_g, g_ln_b, g_attn_g, g_conv_g, g_taps, loss_row) = [
        summed[:, o:o + n] for o, n in zip(offs, sizes)]
    loss = loss_row[0, 0]
    g_taps_shard = lax.dynamic_slice_in_dim(g_taps.reshape(CONV_KERNEL, aw), me * cw_shard, cw_shard, axis=1)
    dmod_all = gathered.reshape(N_DEV, padded)[:, :n_mod * d]
    dmod_cols = lax.dynamic_slice_in_dim(dmod_all, me * mod_cols, mod_cols, axis=1)
    g_w_ada = _mm_tn(silu_c, dmod_cols, "ada_dw")

    arrived = dict(zip(["ffn2_w_gate", "ffn2_w_up", "ffn2_w_down", "w_out", "w_in", "ffn1_w_down", "ffn1_w_up",
                        "ffn1_w_gate"], list(sums_a) + list(sums_b) + list(sums_c) + list(sums_d) + list(sums_e)))
    transposed = ("ffn1_w_gate", "ffn1_w_up", "w_in", "ffn2_w_gate", "ffn2_w_up")
    grads = {
        "w_ada": g_w_ada, "b_ada": g_b_ada, "ffn1_norm_g": g_gn1, "mix_norm_g": g_gn2, "conv_dw_w": g_taps_shard,
        "conv_dw_b": g_conv_b, "conv_ln_g": g_ln_g, "conv_ln_b": g_ln_b, "attn_out_g": g_attn_g,
        "conv_out_g": g_conv_g, "ffn2_norm_g": g_gn3, "final_norm_g": g_final,
    }
    weights = dict(w_ada=w_ada, b_ada=b_ada, ffn1_norm_g=ffn1_norm_g, ffn1_w_gate=ffn1_w_gate, ffn1_w_up=ffn1_w_up, ffn1_w_down=ffn1_w_down, mix_norm_g=mix_norm_g, w_in=w_in, conv_dw_w=conv_dw_w, conv_dw_b=conv_dw_b, conv_ln_g=conv_ln_g, conv_ln_b=conv_ln_b, attn_out_g=attn_out_g, conv_out_g=conv_out_g, w_out=w_out, ffn2_norm_g=ffn2_norm_g, ffn2_w_gate=ffn2_w_gate, ffn2_w_up=ffn2_w_up, ffn2_w_down=ffn2_w_down, final_norm_g=final_norm_g)
    moms = dict(w_ada=m_w_ada, b_ada=m_b_ada, ffn1_norm_g=m_ffn1_norm_g, ffn1_w_gate=m_ffn1_w_gate, ffn1_w_up=m_ffn1_w_up, ffn1_w_down=m_ffn1_w_down, mix_norm_g=m_mix_norm_g, w_in=m_w_in, conv_dw_w=m_conv_dw_w, conv_dw_b=m_conv_dw_b, conv_ln_g=m_conv_ln_g, conv_ln_b=m_conv_ln_b, attn_out_g=m_attn_out_g, conv_out_g=m_conv_out_g, w_out=m_w_out, ffn2_norm_g=m_ffn2_norm_g, ffn2_w_gate=m_ffn2_w_gate, ffn2_w_up=m_ffn2_w_up, ffn2_w_down=m_ffn2_w_down, final_norm_g=m_final_norm_g)
    vars_ = dict(w_ada=v_w_ada, b_ada=v_b_ada, ffn1_norm_g=v_ffn1_norm_g, ffn1_w_gate=v_ffn1_w_gate, ffn1_w_up=v_ffn1_w_up, ffn1_w_down=v_ffn1_w_down, mix_norm_g=v_mix_norm_g, w_in=v_w_in, conv_dw_w=v_conv_dw_w, conv_dw_b=v_conv_dw_b, conv_ln_g=v_conv_ln_g, conv_ln_b=v_conv_ln_b, attn_out_g=v_attn_out_g, conv_out_g=v_conv_out_g, w_out=v_w_out, ffn2_norm_g=v_ffn2_norm_g, ffn2_w_gate=v_ffn2_w_gate, ffn2_w_up=v_ffn2_w_up, ffn2_w_down=v_ffn2_w_down, final_norm_g=v_final_norm_g)
    names = list(weights)
    big = ["w_ada", "ffn1_w_gate", "ffn1_w_up", "ffn1_w_down", "w_in", "w_out", "ffn2_w_gate", "ffn2_w_up",
           "ffn2_w_down"]
    shape2 = {n: (weights[n].shape[-2] if weights[n].ndim > 1 else 1, weights[n].shape[-1]) for n in names}
    shape2["conv_dw_w"] = (CONV_KERNEL, cw_shard)
    g_out, d_out, m_out, v_out = {}, {}, {}, {}
    for n in big:
        if n in arrived:
            def view(t, n=n):
                return t[0].T if n in transposed else t[0]
            res = _adamw_reduced(view(weights[n]), arrived[n], view(moms[n]), view(vars_[n]), "adamw_" + n)
            g_out[n], d_out[n], m_out[n], v_out[n] = [r.T if n in transposed else r for r in res]
        else:
            g2d = grads[n].reshape(shape2[n])
            res = _adamw_big(weights[n].reshape(shape2[n]), g2d, moms[n].reshape(shape2[n]),
                             vars_[n].reshape(shape2[n]), "adamw_" + n)
            g_out[n], (d_out[n], m_out[n], v_out[n]) = g2d, res
    rest = [n for n in names if n not in big]
    res = _adamw_small([weights[n].reshape(shape2[n]) for n in rest], [grads[n].reshape(shape2[n]) for n in rest],
                       [moms[n].reshape(shape2[n]) for n in rest], [vars_[n].reshape(shape2[n]) for n in rest],
                       "adamw_small")
    for i, n in enumerate(rest):
        g_out[n], d_out[n], m_out[n], v_out[n] = grads[n], res[0][i], res[1][i], res[2][i]

    def shaped(table):
        return [table[n].reshape(weights[n].shape) for n in names]

    return (loss, dx.reshape(x.shape), *shaped(g_out), *shaped(d_out), *shaped(m_out), *shaped(v_out))
```

```python
import functools

import jax
import jax.numpy as jnp
from jax import lax
from jax.experimental import pallas as pl
from jax.experimental.pallas import tpu as pltpu

F32 = jnp.float32
BF16 = jnp.bfloat16
MESH = pl.DeviceIdType.MESH
ANY = pl.BlockSpec(memory_space=pl.ANY)

N_DEV = 8
N_CHIP = 4
HEAD_DIM = 64
HALF_HEAD = HEAD_DIM // 2
LANES = 128
BLOCK = 128
DILATIONS = (1, 4, 16)
MERGE_CHUNK = 512
ROPE_THETA = 10000.0
CONV_KERNEL = 31
CONV_HALO = 32
CONV_CHUNK = 512
CONV_SUB = 128
RMS_EPS = 1e-6
LN_EPS = 1e-5
ADAM_LR = 0.001
ADAM_B1 = 0.9
ADAM_B2 = 0.999
ADAM_EPS = 1e-08
ADAM_WD = 0.01
ADAM_STEP = 10
VMEM_LIMIT = 56 * 1024 * 1024
NEG = -1e30


def _params(n_axes):
    return pltpu.CompilerParams(dimension_semantics=("arbitrary",) * n_axes, vmem_limit_bytes=VMEM_LIMIT)


def _tile(n, target, unit):
    best = None
    for t in range(unit, min(n, target) + 1, unit):
        if n % t == 0:
            best = t
    return best if best is not None else n


def _sigmoid(x):
    return 0.5 * (jnp.tanh(0.5 * x) + 1.0)


def _call(body, *, grid, in_specs, out_specs, out_shape, args, name, scratch_shapes=(), comm=None):
    params = _params(len(grid))
    if comm is None:
        return pl.pallas_call(body, grid=grid, in_specs=list(in_specs), out_specs=list(out_specs),
                              out_shape=list(out_shape), scratch_shapes=list(scratch_shapes),
                              compiler_params=params, name=name)(*args)
    n_in, n_out, n_scr = len(args), len(out_shape), len(scratch_shapes)
    c_in, c_out = len(comm.inputs), len(comm.out_shapes)
    steps = 1
    for g in grid:
        steps *= g

    def hosted(*refs):
        pos = 0
        parts = []
        for size in (n_in, c_in, n_out, c_out, n_scr, len(comm.scratch)):
            parts.append(refs[pos:pos + size])
            pos += size
        ins, cin, outs, cout, scr, cscr = parts
        step = 0
        for axis, g in enumerate(grid):
            step = step * g + pl.program_id(axis)

        @pl.when(step == 0)
        def _():
            comm.start(cin, cout, cscr)

        body(*ins, *outs, *scr)
        if comm.mid is not None and steps >= 4:
            @pl.when(step == (3 * steps) // 4)
            def _():
                comm.mid(cin, cout, cscr)

        @pl.when(step == steps - 1)
        def _():
            if comm.mid is not None and steps < 4:
                comm.mid(cin, cout, cscr)
            comm.finish(cin, cout, cscr)

    res = pl.pallas_call(
        hosted, grid=grid, in_specs=list(in_specs) + [ANY] * c_in, out_specs=list(out_specs) + [ANY] * c_out,
        out_shape=list(out_shape) + list(comm.out_shapes), scratch_shapes=list(scratch_shapes) + list(comm.scratch),
        compiler_params=params, name=name)(*args, *comm.inputs)
    return res[:n_out], res[n_out:]


def _rows(fn, rows_in, vecs_in, rows_out, vecs_out, *, tile, name, comm=None):
    norm = [r if isinstance(r, tuple) else (r, r.shape[1], 0) for r in rows_in]
    n_rows = norm[0][0].shape[0]
    n_tiles = n_rows // tile
    in_specs, args = [], []
    for arr, width, cb in norm:
        in_specs.append(pl.BlockSpec((tile, width), functools.partial(lambda i, cb: (i, cb), cb=cb)))
        args.append(arr)
    for v in vecs_in:
        in_specs.append(pl.BlockSpec((1, v.shape[1]), lambda i: (0, 0)))
        args.append(v)
    out_shape = [jax.ShapeDtypeStruct((n_rows, w), dt) for w, dt in rows_out]
    out_shape += [jax.ShapeDtypeStruct((1, w), F32) for w in vecs_out]
    out_specs = [pl.BlockSpec((tile, w), lambda i: (i, 0)) for w, _ in rows_out]
    out_specs += [pl.BlockSpec((1, w), lambda i: (0, 0)) for w in vecs_out]
    n_in, n_ro = len(args), len(rows_out)

    def body(*refs):
        vals = [r[...] for r in refs[:n_in]]
        outs = refs[n_in:]
        row_vals, vec_vals = fn(*vals)
        for ref, val in zip(outs[:n_ro], row_vals):
            if isinstance(val, tuple):
                w = val[0].shape[1]
                for j, piece in enumerate(val):
                    ref[:, j * w:(j + 1) * w] = piece.astype(ref.dtype)
            else:
                ref[...] = val.astype(ref.dtype)
        if vecs_out:
            @pl.when(pl.program_id(0) == 0)
            def _():
                for ref in outs[n_ro:]:
                    ref[...] = jnp.zeros_like(ref)
            for ref, val in zip(outs[n_ro:], vec_vals):
                ref[...] += val

    return _call(body, grid=(n_tiles,), in_specs=in_specs, out_specs=out_specs, out_shape=out_shape, args=args,
                 name=name, comm=comm)


def _colsum(x):
    return jnp.sum(x, axis=0, keepdims=True)


def _rms_stats(h):
    r = lax.rsqrt(jnp.mean(h * h, axis=-1, keepdims=True) + RMS_EPS)
    return r, h * r


def _rms_back(r, xn, dxn):
    return r * (dxn - xn * jnp.mean(dxn * xn, axis=-1, keepdims=True))


def _branch_back(dh, f, gate, coef):
    return (coef * gate) * dh, coef * _colsum(f.astype(F32) * dh)


def _norm_mod_back(dn, h, dh_in, gain, scale):
    dn = dn.astype(F32)
    r, xn = _rms_stats(h)
    y = xn * gain
    dy = dn * (1.0 + scale)
    dh = dh_in + _rms_back(r, xn, dy * gain)
    return dh, [_colsum(dn), _colsum(dn * y), _colsum(dy * xn)]


def _norm_mod_bwd(dn, h, dh_in, gain, scale, name, comm=None):
    d = h.shape[1]

    def fn(dn, h, dh_in, gain, scale):
        dh, vecs = _norm_mod_back(dn, h, dh_in, gain, scale)
        return [dh], vecs
    return _rows(fn, [dn, h, dh_in], [gain, scale], [(d, F32)], [d, d, d], tile=256, name=name, comm=comm)


def _mm_norm_mod_bwd(pairs, h, dh_in, gain, scale, branch, name, tm, comm=None):
    f, gate, coef = branch

    def epi(accs, ex, vc):
        dh, vecs = _norm_mod_back(accs[0], ex[0], ex[1], vc[0], vc[1])
        df, dgate = _branch_back(dh, ex[2], vc[2], coef)
        return [dh, df] + vecs + [dgate]
    return _mm([pairs], epi, [h, dh_in, f], [gain, scale, gate], [F32, BF16], trans_rhs=False, tm=tm,
               tn=h.shape[1], name=name, n_sums=4, comm=comm)


def _last_mm_loss(lhs, w, res, gate, coef, target, gain, name):
    d = w.shape[1]

    def epi(accs, ex, vc):
        f = accs[0]
        h = ex[0] + (coef * vc[0]) * f
        r, xn = _rms_stats(h)
        err = xn * vc[1] - ex[1]
        dout = err * (1.0 / d)
        dh = _rms_back(r, xn, dout * vc[1])
        df, dgate = _branch_back(dh, f, vc[0], coef)
        return [dh, df, _colsum(err * err), _colsum(dout * xn), dgate]
    return _mm([[(lhs, w)]], epi, [res, target], [gate, gain], [F32, BF16], trans_rhs=False, tm=256, tn=d,
               name=name, n_sums=3)


def _partner(x):
    if x.shape[1] > LANES:
        return jnp.concatenate([_partner(x[:, c:c + LANES]) for c in range(0, x.shape[1], LANES)], axis=1)
    lane = lax.broadcasted_iota(jnp.int32, x.shape, 1) % HEAD_DIM
    return jnp.where(lane < HALF_HEAD, pltpu.roll(x, LANES - HALF_HEAD, 1), pltpu.roll(x, HALF_HEAD, 1))


def _proj_rope(n, w_t, cos, sin_signed, width, name, comm=None):
    s, kdim = n.shape
    n_cols = w_t.shape[0]
    tm = _tile(s, 1024, 8)
    qscale = HEAD_DIM ** -0.5

    chunk = _tile(tm, 256, 8)

    def body(n_ref, w_ref, cos_ref, sin_ref, o_ref):
        j = pl.program_id(0)

        def products(rows):
            return lax.dot_general(n_ref[rows, :].astype(BF16), w_ref[...].astype(BF16), (((1,), (1,)), ((), ())),
                                   preferred_element_type=F32)

        @pl.when(j >= 2)
        def _():
            for c in range(tm // chunk):
                rows = slice(c * chunk, (c + 1) * chunk)
                o_ref[rows, :] = products(rows)

        @pl.when(j < 2)
        def _():
            scale = jnp.where(j == 0, qscale, 1.0)
            for c in range(tm // chunk):
                rows = slice(c * chunk, (c + 1) * chunk)
                acc = products(rows)
                cos = jnp.tile(cos_ref[rows, :], (1, width // LANES))
                sin = jnp.tile(sin_ref[rows, :], (1, width // LANES))
                o_ref[rows, :] = scale * (acc * cos + _partner(acc) * sin)

    table = pl.BlockSpec((tm, LANES), lambda j, i: (jnp.where(j < 2, i, 0), 0))
    return _call(
        body, grid=(n_cols // width, s // tm),
        in_specs=[pl.BlockSpec((tm, kdim), lambda j, i: (i, 0)), pl.BlockSpec((width, kdim), lambda j, i: (j, 0)),
                  table, table],
        out_specs=[pl.BlockSpec((tm, width), lambda j, i: (i, j))],
        out_shape=[jax.ShapeDtypeStruct((s, n_cols), F32)], args=(n, w_t, cos, sin_signed), name=name, comm=comm)


def _mix_post(attn, u1, attn_g, ln_g, ln_b, conv_g):
    _, xa = _rms_stats(attn)
    mu = jnp.mean(u1, axis=-1, keepdims=True)
    xc = u1 - mu
    rstd = lax.rsqrt(jnp.mean(xc * xc, axis=-1, keepdims=True) + LN_EPS)
    u2 = (xc * rstd) * ln_g + ln_b
    u3 = u2 * _sigmoid(u2)
    _, x3 = _rms_stats(u3)
    return jnp.concatenate([xa * attn_g, x3 * conv_g], axis=1)


def _mix_post_back(dy, attn, u1, attn_g, ln_g, ln_b, conv_g):
    w = attn.shape[1]
    dya, dyc = dy[:, :w], dy[:, w:]
    ra, xa = _rms_stats(attn)
    dattn = _rms_back(ra, xa, dya * attn_g)
    mu = jnp.mean(u1, axis=-1, keepdims=True)
    xc = u1 - mu
    rstd = lax.rsqrt(jnp.mean(xc * xc, axis=-1, keepdims=True) + LN_EPS)
    xh = xc * rstd
    u2 = xh * ln_g + ln_b
    sig = _sigmoid(u2)
    u3 = u2 * sig
    r3, x3 = _rms_stats(u3)
    du3 = _rms_back(r3, x3, dyc * conv_g)
    du2 = du3 * (sig + u3 * (1.0 - sig))
    dxh = du2 * ln_g
    du1 = rstd * (dxh - jnp.mean(dxh, axis=-1, keepdims=True) - xh * jnp.mean(dxh * xh, axis=-1, keepdims=True))
    return dattn, du1, [_colsum(dya * xa), _colsum(dyc * x3), _colsum(du2 * xh), _colsum(du2)]


def _silu_rows(c_all, name):
    def fn(c):
        return [c * _sigmoid(c)], []
    return _rows(fn, [c_all], [], [(c_all.shape[1], BF16)], [], tile=c_all.shape[0], name=name)[0]


def _mm(groups, epi, extras, vecs, outs, *, trans_rhs, tm, tn, name, n_sums=0, pre=None, pre_inputs=(),
        comm=None):
    m = (pre_inputs[0] if pre is not None else groups[0][0][0]).shape[0]
    n = groups[0][0][1].shape[0] if trans_rhs else groups[0][0][1].shape[1]
    tm, tn = min(tm, m), min(tn, n)
    in_specs, args, uses_pre = [], [], []
    for grp in groups:
        for lhs, rhs in grp:
            k = rhs.shape[1] if trans_rhs else rhs.shape[0]
            uses_pre.append(lhs is None)
            if lhs is not None:
                in_specs.append(pl.BlockSpec((tm, k), lambda j, i: (i, 0)))
                args.append(lhs)
            in_specs.append(pl.BlockSpec((tn, k), lambda j, i: (j, 0)) if trans_rhs
                            else pl.BlockSpec((k, tn), lambda j, i: (0, j)))
            args.append(rhs)
    n_mm = len(args)
    for p in pre_inputs:
        in_specs.append(pl.BlockSpec((tm, p.shape[1]), lambda j, i: (i, 0)))
        args.append(p)
    for e in extras:
        in_specs.append(pl.BlockSpec((tm, tn), lambda j, i: (i, j)) if e.shape[1] == n
                        else pl.BlockSpec((tm, e.shape[1]), lambda j, i: (i, 0)))
        args.append(e)
    for v in vecs:
        in_specs.append(pl.BlockSpec((1, tn), lambda j, i: (0, j)) if v.shape[1] == n
                        else pl.BlockSpec((1, v.shape[1]), lambda j, i: (0, 0)))
        args.append(v)
    sizes = [len(g) for g in groups]
    n_pre, n_ex, n_vec = len(pre_inputs), len(extras), len(vecs)
    dims = (((1,), (1,)), ((), ())) if trans_rhs else (((1,), (0,)), ((), ()))
    out_specs, out_shape = [], []
    if pre is not None:
        k_pre = args[n_mm - 1].shape[1] if trans_rhs else args[n_mm - 1].shape[0]
        out_specs.append(pl.BlockSpec((tm, k_pre), lambda j, i: (i, 0)))
        out_shape.append(jax.ShapeDtypeStruct((m, k_pre), BF16))
    for o in outs:
        dt, width = o if isinstance(o, tuple) else (o, n)
        out_specs.append(pl.BlockSpec((tm, tn), lambda j, i: (i, j)) if width == n
                         else pl.BlockSpec((tm, width), lambda j, i: (i, 0)))
        out_shape.append(jax.ShapeDtypeStruct((m, width), dt))
    n_tiles_out = len(out_specs)
    out_specs += [pl.BlockSpec((1, tn), lambda j, i: (0, j))] * n_sums
    out_shape += [jax.ShapeDtypeStruct((1, n), F32)] * n_sums

    def body(*refs):
        ins = refs[:n_mm + n_pre + n_ex + n_vec]
        out_refs = refs[n_mm + n_pre + n_ex + n_vec:]
        vc = [r[...] for r in ins[n_mm + n_pre + n_ex:]]
        vals = []
        made = None
        if pre is not None:
            made = pre([r[...] for r in ins[n_mm:n_mm + n_pre]], vc).astype(BF16)
            vals.append(made)
        accs, pos, pair = [], 0, 0
        for size in sizes:
            acc = None
            for _ in range(size):
                if uses_pre[pair]:
                    lhs_tile = made
                else:
                    lhs_tile = ins[pos][...].astype(BF16)
                    pos += 1
                part = lax.dot_general(lhs_tile, ins[pos][...].astype(BF16), dims, preferred_element_type=F32)
                acc = part if acc is None else acc + part
                pos += 1
                pair += 1
            accs.append(acc)
        ex = [r[...] for r in ins[n_mm + n_pre:n_mm + n_pre + n_ex]]
        vals += epi(accs, ex, vc)
        for ref, val in zip(out_refs[:n_tiles_out], vals):
            ref[...] = val.astype(ref.dtype)
        if n_sums:
            @pl.when(pl.program_id(1) == 0)
            def _():
                for ref in out_refs[n_tiles_out:]:
                    ref[...] = jnp.zeros_like(ref)
            for ref, val in zip(out_refs[n_tiles_out:], vals[n_tiles_out:]):
                ref[...] += val

    return _call(body, grid=(n // tn, m // tm), in_specs=in_specs, out_specs=out_specs, out_shape=out_shape,
                 args=args, name=name, comm=comm)


def _mm_tn(lhs, rhs, name, comm=None):
    t, a = lhs.shape
    b = rhs.shape[1]
    ta = a if a <= 1536 else _tile(a, 1536, LANES)
    tk = _tile(t, 2048, 8)

    def body(l_ref, r_ref, o_ref):
        @pl.when(pl.program_id(1) == 0)
        def _():
            o_ref[...] = jnp.zeros_like(o_ref)
        o_ref[...] += lax.dot_general(l_ref[...].astype(BF16), r_ref[...].astype(BF16), (((0,), (0,)), ((), ())),
                                      preferred_element_type=F32)

    res = _call(body, grid=(a // ta, t // tk),
                in_specs=[pl.BlockSpec((tk, ta), lambda i, k: (k, i)), pl.BlockSpec((tk, b), lambda i, k: (k, 0))],
                out_specs=[pl.BlockSpec((ta, b), lambda i, k: (i, 0))], out_shape=[jax.ShapeDtypeStruct((a, b), F32)],
                args=(lhs, rhs), name=name, comm=comm)
    return res[0] if comm is None else (res[0][0], res[1])


def _ffn_tn(f):
    return _tile(f, 1536, LANES)


def _swiglu_parts(a, b):
    sig = _sigmoid(a)
    silu = a * sig
    return [silu, b * (sig + silu * (1.0 - sig)), silu * b]


def _ffn_up(n, wg_t, wu_t, name, comm=None):
    def epi(accs, ex, vc):
        return _swiglu_parts(accs[0], accs[1])
    return _mm([[(n, wg_t)], [(n, wu_t)]], epi, [], [], [BF16, BF16, BF16], trans_rhs=True, tm=512,
               tn=_ffn_tn(wg_t.shape[0]), name=name, comm=comm)


def _norm_gate(h, gain, scale, shift, wg_t, name, comm=None):
    def pre(tiles, vc):
        _, xn = _rms_stats(tiles[0])
        return (xn * vc[0]) * (1.0 + vc[1]) + vc[2]

    def epi(accs, ex, vc):
        return [accs[0]]
    return _mm([[(None, wg_t)]], epi, [], [gain, scale, shift], [BF16], trans_rhs=True, tm=512,
               tn=wg_t.shape[0], name=name, pre=pre, pre_inputs=[h], comm=comm)


def _mix_out(attn, u1, post, w, res, gate, norm, name):
    def pre(tiles, vc):
        return _mix_post(tiles[0], tiles[1], *vc[4:8])

    def epi(accs, ex, vc):
        h = ex[0] + vc[0] * accs[0]
        _, xn = _rms_stats(h)
        return [h, accs[0], (xn * vc[1]) * (1.0 + vc[2]) + vc[3]]
    return _mm([[(None, w)]], epi, [res], [gate] + list(norm) + list(post), [F32, BF16, BF16], trans_rhs=False,
               tm=512, tn=w.shape[1], name=name, pre=pre, pre_inputs=[attn, u1])


def _mix_dy_post_bwd(dmix, w, attn, u1, post, name):
    width = attn.shape[1]

    def epi(accs, ex, vc):
        dattn, du1, sums = _mix_post_back(accs[0], ex[0], ex[1], *vc)
        return [dattn, du1, jnp.concatenate(sums[0:2], axis=1), jnp.concatenate(sums[2:4], axis=1)]
    return _mm([[(dmix, w)]], epi, [attn, u1], list(post), [(F32, width), (F32, width)], trans_rhs=True, tm=256,
               tn=w.shape[0], name=name, n_sums=2)


def _ffn_up_given_gate(n, wu_t, a, name, comm=None):
    def epi(accs, ex, vc):
        return _swiglu_parts(ex[0].astype(F32), accs[0])
    return _mm([[(n, wu_t)]], epi, [a], [], [BF16, BF16, BF16], trans_rhs=True, tm=512,
               tn=_ffn_tn(wu_t.shape[0]), name=name, comm=comm)


def _residual_mm(lhs, w, res, gate, coef, name, norm=None, comm=None):
    def epi(accs, ex, vc):
        h = ex[0] + (coef * vc[0]) * accs[0]
        if norm is None:
            return [h, accs[0]]
        _, xn = _rms_stats(h)
        return [h, accs[0], (xn * vc[1]) * (1.0 + vc[2]) + vc[3]]
    vecs = [gate] + (list(norm) if norm is not None else [])
    outs = [F32, BF16] + ([BF16] if norm is not None else [])
    return _mm([[(lhs, w)]], epi, [res], vecs, outs, trans_rhs=False, tm=512, tn=w.shape[1], name=name, comm=comm)


def _ffn_bwd_hidden(df, wd, dhid_db, dhid_da, name, comm=None):
    def epi(accs, ex, vc):
        return [accs[0] * ex[1].astype(F32), accs[0] * ex[0].astype(F32)]
    return _mm([[(df, wd)]], epi, [dhid_db, dhid_da], [], [BF16, BF16], trans_rhs=True, tm=512,
               tn=_ffn_tn(wd.shape[0]), name=name, comm=comm)


def _plain_mm(pairs, out_dtype, trans_rhs, tn, name, tm=512, comm=None):
    def epi(accs, ex, vc):
        return [accs[0]]
    res = _mm([pairs], epi, [], [], [out_dtype], trans_rhs=trans_rhs, tm=tm, tn=tn, name=name, comm=comm)
    return res[0] if comm is None else (res[0][0], res[1])


HEADS_PER_TILE = LANES // HEAD_DIM


def _stack_heads(x):
    lane = lax.broadcasted_iota(jnp.int32, (1, LANES), 1)
    return jnp.concatenate([x * (lane // HEAD_DIM == h).astype(F32) for h in range(HEADS_PER_TILE)], axis=0)


def _unstack_heads(y):
    r = y.shape[0] // HEADS_PER_TILE
    lane = lax.broadcasted_iota(jnp.int32, (r, y.shape[1]), 1)
    out = y[0:r]
    for h in range(1, HEADS_PER_TILE):
        out = jnp.where(lane // HEAD_DIM == h, y[h * r:(h + 1) * r], out)
    return out


def _stacked_lse(lb):
    return jnp.concatenate([_lane_pick(lb, h) for h in range(HEADS_PER_TILE)], axis=0)


def _band_masks(n_row_blocks, n_col_blocks):
    shape = (n_row_blocks * BLOCK, n_col_blocks * BLOCK)
    qi = lax.broadcasted_iota(jnp.int32, shape, 0) % BLOCK
    kj = lax.broadcasted_iota(jnp.int32, shape, 1) % BLOCK
    return kj <= qi, kj >= qi


def _query_masks():
    first_valid, _ = _band_masks(HEADS_PER_TILE, 1)
    same_ok, before_ok = _band_masks(HEADS_PER_TILE, 2)
    is_cur = lax.broadcasted_iota(jnp.int32, same_ok.shape, 1) >= BLOCK
    return first_valid, jnp.logical_and(is_cur, same_ok), jnp.logical_and(jnp.logical_not(is_cur), before_ok)


def _dot_nt(a, b):
    return lax.dot_general(a.astype(BF16), b.astype(BF16), (((1,), (1,)), ((), ())), preferred_element_type=F32)


def _dot_nn(a, b):
    return lax.dot_general(a.astype(BF16), b.astype(BF16), (((1,), (0,)), ((), ())), preferred_element_type=F32)


def _dot_tn(a, b):
    return lax.dot_general(a.astype(BF16), b.astype(BF16), (((0,), (0,)), ((), ())), preferred_element_type=F32)


def _lane_pick(x, h):
    lane = lax.broadcasted_iota(jnp.int32, x.shape, 1)
    return jnp.sum(jnp.where(lane == h * HEAD_DIM, x, 0.0), axis=1, keepdims=True)


def _block_rows(idx, d):
    span = BLOCK * d
    q0 = (idx // d) * span + idx % d
    return pl.ds(q0, BLOCK, stride=d), pl.ds(q0 - span, BLOCK, stride=d)


def _branch_loops(n_blocks, d, visit, unroll, masks):
    first_valid, cur_part, prev_part = masks
    if d % unroll == 0 and (n_blocks - d) % unroll == 0:
        full_valid = jnp.logical_or(cur_part, prev_part)

        def first(idx, carry):
            rows = pl.ds(idx, BLOCK, stride=d)
            visit(rows, [rows], first_valid)
            return carry

        def rest(idx, carry):
            rows, prev = _block_rows(idx, d)
            visit(rows, [prev, rows], full_valid)
            return carry

        lax.fori_loop(0, d, first, 0, unroll=unroll)
        lax.fori_loop(d, n_blocks, rest, 0, unroll=unroll)
        return

    def every(idx, carry):
        span = BLOCK * d
        q0 = (idx // d) * span + idx % d
        has_prev = idx >= d
        rows = pl.ds(q0, BLOCK, stride=d)
        prev = pl.ds(jnp.where(has_prev, q0 - span, q0), BLOCK, stride=d)
        visit(rows, [prev, rows], jnp.logical_or(cur_part, jnp.logical_and(prev_part, has_prev)))
        return carry

    lax.fori_loop(0, n_blocks, every, 0, unroll=unroll)


def _qkv_specs(s, tiles):
    q, k, v = [pl.BlockSpec((s, LANES), functools.partial(lambda hb, off: (0, off + hb), off=i * tiles))
               for i in range(3)]
    return q, k, v, pl.BlockSpec((s, LANES), lambda hb: (0, hb))


def _attn_seq_fwd(proj, width, name, comm=None):
    s = proj.shape[0]
    q_spec, k_spec, v_spec, cur = _qkv_specs(s, width // LANES)

    def body(q_ref, k_ref, v_ref, o_ref, l_ref, o_s, l_s):
        masks = _query_masks()
        for bi, d in enumerate(DILATIONS):
            def visit(rows, key_rows, valid, bi=bi):
                q2 = _stack_heads(q_ref[rows, :])
                keys = jnp.concatenate([k_ref[r, :] for r in key_rows], axis=0)
                vals = jnp.concatenate([v_ref[r, :] for r in key_rows], axis=0)
                sc = jnp.where(valid, _dot_nt(q2, keys), NEG)
                mx = jnp.max(sc, axis=1, keepdims=True)
                p = jnp.exp(sc - mx)
                den = jnp.sum(p, axis=1, keepdims=True)
                o_s[bi, rows, :] = _unstack_heads(_dot_nn(p, vals) / den)
                l_s[bi, rows, :] = _unstack_heads(jnp.broadcast_to(mx + jnp.log(den), (q2.shape[0], LANES)))

            _branch_loops(s // BLOCK, d, visit, 8, masks)
        for c in range(s // MERGE_CHUNK):
            rows = slice(c * MERGE_CHUNK, (c + 1) * MERGE_CHUNK)
            ls = [l_s[bi, rows, :] for bi in range(len(DILATIONS))]
            top = functools.reduce(jnp.maximum, ls)
            ws = [jnp.exp(l - top) for l in ls]
            den = functools.reduce(lambda a, b: a + b, ws)
            num = functools.reduce(lambda a, b: a + b, [w * o_s[bi, rows, :] for bi, w in enumerate(ws)])
            o_ref[rows, :] = num / den
            l_ref[rows, :] = top + jnp.log(den)

    return _call(
        body, grid=(width // LANES,), in_specs=[q_spec, k_spec, v_spec], out_specs=[cur, cur],
        out_shape=[jax.ShapeDtypeStruct((s, width), F32)] * 2,
        scratch_shapes=[pltpu.VMEM((len(DILATIONS), s, LANES), F32)] * 2,
        args=(proj, proj, proj), name=name, comm=comm)


def _attn_seq_bwd(proj, do, o, lse, cos, sin_signed, name, comm=None):
    s, width = do.shape
    q_spec, k_spec, v_spec, cur = _qkv_specs(s, width // LANES)
    table = pl.BlockSpec((s, LANES), lambda hb: (0, 0))
    qscale = HEAD_DIM ** -0.5

    def body(q_ref, k_ref, v_ref, do_ref, o_ref, l_ref, cos_ref, sin_ref, dq_out, dk_out, dv_out,
             dq_ref, dk_ref, dv_ref):
        dq_ref[...] = jnp.zeros_like(dq_ref)
        dk_ref[...] = jnp.zeros_like(dk_ref)
        dv_ref[...] = jnp.zeros_like(dv_ref)
        masks = _query_masks()
        for d in DILATIONS:
            def visit(rows, key_rows, valid):
                dob = do_ref[rows, :]
                q2 = _stack_heads(q_ref[rows, :])
                do2 = _stack_heads(dob)
                delta = jnp.sum(_stack_heads(dob * o_ref[rows, :]), axis=1, keepdims=True)
                lse2 = _stacked_lse(l_ref[rows, :])
                keys = jnp.concatenate([k_ref[r, :] for r in key_rows], axis=0)
                vals = jnp.concatenate([v_ref[r, :] for r in key_rows], axis=0)
                p = jnp.where(valid, jnp.exp(_dot_nt(q2, keys) - lse2), 0.0)
                ds = p * (_dot_nt(do2, vals) - delta)
                dq_ref[rows, :] += _unstack_heads(_dot_nn(ds, keys))
                dkk = _dot_tn(ds, q2)
                dvv = _dot_tn(p, do2)
                for i, r in enumerate(key_rows):
                    dk_ref[r, :] += dkk[i * BLOCK:(i + 1) * BLOCK]
                    dv_ref[r, :] += dvv[i * BLOCK:(i + 1) * BLOCK]

            _branch_loops(s // BLOCK, d, visit, 8, masks)
        for c in range(s // MERGE_CHUNK):
            rows = slice(c * MERGE_CHUNK, (c + 1) * MERGE_CHUNK)
            cos, sin = cos_ref[rows, :], sin_ref[rows, :]
            dq, dk = dq_ref[rows, :], dk_ref[rows, :]
            dq_out[rows, :] = ((dq * cos - _partner(dq) * sin) * qscale).astype(BF16)
            dk_out[rows, :] = (dk * cos - _partner(dk) * sin).astype(BF16)
            dv_out[rows, :] = dv_ref[rows, :].astype(BF16)

    return _call(
        body, grid=(width // LANES,), in_specs=[q_spec, k_spec, v_spec, cur, cur, cur, table, table],
        out_specs=[cur, cur, cur], out_shape=[jax.ShapeDtypeStruct((s, width), BF16)] * 3,
        scratch_shapes=[pltpu.VMEM((s, LANES), F32)] * 3,
        args=(proj, proj, proj, do, o, lse, cos, sin_signed), name=name, comm=comm)


def _conv_specs(s, a_block, b_block):
    per = CONV_CHUNK // CONV_HALO
    a_cur = pl.BlockSpec((CONV_CHUNK, LANES), lambda cb, i: (i, a_block + cb))
    b_cur = pl.BlockSpec((CONV_CHUNK, LANES), lambda cb, i: (i, b_block + cb))
    a_halo = pl.BlockSpec((CONV_HALO, LANES), lambda cb, i: (jnp.maximum(i * per - 1, 0), a_block + cb))
    b_halo = pl.BlockSpec((CONV_HALO, LANES), lambda cb, i: (jnp.maximum(i * per - 1, 0), b_block + cb))
    w_spec = pl.BlockSpec((CONV_KERNEL, LANES), lambda cb, i: (0, cb))
    vec = pl.BlockSpec((1, LANES), lambda cb, i: (0, cb))
    out = pl.BlockSpec((CONV_CHUNK, LANES), lambda cb, i: (i, cb))
    return a_cur, b_cur, a_halo, b_halo, w_spec, vec, out


def _fill_glu_window(win, a_ref, b_ref, ah_ref, bh_ref, first):
    halo = ah_ref[...] * _sigmoid(bh_ref[...])
    win[0:CONV_HALO, :] = jnp.where(first, 0.0, halo)
    win[CONV_HALO:, :] = a_ref[...] * _sigmoid(b_ref[...])


def _conv_fwd(proj, a_block, b_block, w, bias, name, comm=None):
    s = proj.shape[0]
    cw = w.shape[1]
    a_cur, b_cur, a_halo, b_halo, w_spec, vec, out = _conv_specs(s, a_block, b_block)
    lead = CONV_HALO - (CONV_KERNEL - 1)

    def body(a_ref, b_ref, ah_ref, bh_ref, w_ref, bias_ref, o_ref, win):
        _fill_glu_window(win, a_ref, b_ref, ah_ref, bh_ref, pl.program_id(1) == 0)
        for sub in range(CONV_CHUNK // CONV_SUB):
            base = sub * CONV_SUB
            acc = jnp.zeros((CONV_SUB, LANES), F32) + bias_ref[...]
            for j in range(CONV_KERNEL):
                acc = acc + w_ref[j:j + 1, :] * win[base + lead + j:base + lead + j + CONV_SUB, :]
            o_ref[base:base + CONV_SUB, :] = acc

    return _call(
        body, grid=(cw // LANES, s // CONV_CHUNK), in_specs=[a_cur, b_cur, a_halo, b_halo, w_spec, vec],
        out_specs=[out], out_shape=[jax.ShapeDtypeStruct((s, cw), F32)],
        scratch_shapes=[pltpu.VMEM((CONV_CHUNK + CONV_HALO, LANES), F32)],
        args=(proj, proj, proj, proj, w, bias), name=name, comm=comm)


def _conv_bwd(proj, a_block, b_block, w, du1, name):
    s = proj.shape[0]
    cw = w.shape[1]
    a_cur, b_cur, a_halo, b_halo, w_spec, vec, out = _conv_specs(s, a_block, b_block)
    per = CONV_CHUNK // CONV_HALO
    n_chunks = s // CONV_CHUNK
    d_next = pl.BlockSpec((CONV_HALO, LANES), lambda cb, i: (jnp.minimum((i + 1) * per, s // CONV_HALO - 1), cb))
    lead = CONV_HALO - (CONV_KERNEL - 1)

    def body(a_ref, b_ref, ah_ref, bh_ref, w_ref, d_ref, dn_ref, da_ref, db_ref, dw_ref, dbias_ref, win, dwin):
        i = pl.program_id(1)
        _fill_glu_window(win, a_ref, b_ref, ah_ref, bh_ref, i == 0)
        dwin[0:CONV_CHUNK, :] = d_ref[...]
        dwin[CONV_CHUNK:, :] = jnp.where(i == n_chunks - 1, 0.0, dn_ref[...])

        @pl.when(i == 0)
        def _():
            dw_ref[...] = jnp.zeros_like(dw_ref)
            dbias_ref[...] = jnp.zeros_like(dbias_ref)

        dbias_ref[...] += _colsum(d_ref[...])
        for sub in range(CONV_CHUNK // CONV_SUB):
            base = sub * CONV_SUB
            dcur = dwin[base:base + CONV_SUB, :]
            du0 = jnp.zeros((CONV_SUB, LANES), F32)
            for j in range(CONV_KERNEL):
                back = CONV_KERNEL - 1 - j
                du0 = du0 + w_ref[j:j + 1, :] * dwin[base + back:base + back + CONV_SUB, :]
                dw_ref[j:j + 1, :] += _colsum(dcur * win[base + lead + j:base + lead + j + CONV_SUB, :])
            av = a_ref[base:base + CONV_SUB, :]
            sig = _sigmoid(b_ref[base:base + CONV_SUB, :])
            da_ref[base:base + CONV_SUB, :] = (du0 * sig).astype(BF16)
            db_ref[base:base + CONV_SUB, :] = (du0 * av * sig * (1.0 - sig)).astype(BF16)

    return pl.pallas_call(
        body, grid=(cw // LANES, n_chunks), in_specs=[a_cur, b_cur, a_halo, b_halo, w_spec, out, d_next],
        out_specs=[out, out, w_spec, vec],
        out_shape=[jax.ShapeDtypeStruct((s, cw), BF16), jax.ShapeDtypeStruct((s, cw), BF16),
                   jax.ShapeDtypeStruct((CONV_KERNEL, cw), F32), jax.ShapeDtypeStruct((1, cw), F32)],
        scratch_shapes=[pltpu.VMEM((CONV_CHUNK + CONV_HALO, LANES), F32)] * 2,
        compiler_params=_params(2), name=name)(proj, proj, proj, proj, w, du1, du1)


def _adamw_math(w, g, m, v):
    m = ADAM_B1 * m + (1.0 - ADAM_B1) * g
    v = ADAM_B2 * v + (1.0 - ADAM_B2) * (g * g)
    m_hat = m / (1.0 - ADAM_B1 ** ADAM_STEP)
    v_hat = v / (1.0 - ADAM_B2 ** ADAM_STEP)
    delta = -ADAM_LR * (m_hat / (jnp.sqrt(v_hat) + ADAM_EPS) + ADAM_WD * w)
    return delta, m, v


def _adamw_big(w, g, m, v, name):
    rows, cols = w.shape
    tile = _tile(rows, 256, 8)
    spec = pl.BlockSpec((tile, cols), lambda i: (i, 0))

    def body(w_ref, g_ref, m_ref, v_ref, d_out, m_out, v_out):
        d_out[...], m_out[...], v_out[...] = _adamw_math(w_ref[...], g_ref[...], m_ref[...], v_ref[...])

    return pl.pallas_call(body, grid=(rows // tile,), in_specs=[spec] * 4, out_specs=[spec] * 3,
                          out_shape=[jax.ShapeDtypeStruct(w.shape, F32)] * 3, compiler_params=_params(1),
                          name=name)(w, g, m, v)


def _adamw_reduced(w, land, m, v, name):
    rows, cols = w.shape
    tile = _tile(rows, 256, 16)
    spec = pl.BlockSpec((tile, cols), lambda i: (i, 0))

    def body(w_ref, l_ref, m_ref, v_ref, g_out, d_out, m_out, v_out):
        g = l_ref[0].astype(F32)
        for q in range(1, N_CHIP):
            g = g + l_ref[q].astype(F32)
        g_out[...] = g
        d_out[...], m_out[...], v_out[...] = _adamw_math(w_ref[...], g, m_ref[...], v_ref[...])

    return pl.pallas_call(body, grid=(rows // tile,),
                          in_specs=[spec, pl.BlockSpec((N_CHIP, tile, cols), lambda i: (0, i, 0)), spec, spec],
                          out_specs=[spec] * 4, out_shape=[jax.ShapeDtypeStruct(w.shape, F32)] * 4,
                          compiler_params=_params(1), name=name)(w, land, m, v)


def _adamw_small(ws, gs, ms, vs, name):
    n = len(ws)

    def body(*refs):
        ins, outs = refs[:4 * n], refs[4 * n:]
        for t in range(n):
            res = _adamw_math(ins[t][...], ins[n + t][...], ins[2 * n + t][...], ins[3 * n + t][...])
            for j in range(3):
                outs[j * n + t][...] = res[j]

    shapes = [jax.ShapeDtypeStruct(w.shape, F32) for w in ws]
    res = pl.pallas_call(body, out_shape=shapes * 3, compiler_params=pltpu.CompilerParams(vmem_limit_bytes=VMEM_LIMIT),
                         name=name)(*ws, *gs, *ms, *vs)
    return res[:n], res[n:2 * n], res[2 * n:]


def _sum_blocks(x, n_blocks, name):
    r = x.shape[0] // n_blocks

    def body(x_ref, o_ref):
        acc = x_ref[0:r, :]
        for b in range(1, n_blocks):
            acc = acc + x_ref[b * r:(b + 1) * r, :]
        o_ref[...] = acc

    return pl.pallas_call(body, out_shape=jax.ShapeDtypeStruct((r, x.shape[1]), F32),
                          compiler_params=pltpu.CompilerParams(vmem_limit_bytes=VMEM_LIMIT), name=name)(x)


def _coords():
    return lax.axis_index("x"), lax.axis_index("y"), lax.axis_index("c")


def _flip(v, bit):
    return 1 - v if bit else v


def _ag_small(x, name):
    r, c = x.shape

    def body(x_ref, o_ref, send, recv, local_sem):
        mx, my, mc = _coords()

        def rows(px, py, pc):
            return o_ref.at[pl.ds(pl.multiple_of((4 * px + 2 * py + pc) * r, 8), r), :]

        local = pltpu.make_async_copy(x_ref, rows(mx, my, mc), local_sem)
        local.start()
        peers = [(_flip(mx, k >> 2 & 1), _flip(my, k >> 1 & 1), _flip(mc, k & 1)) for k in range(1, N_DEV)]
        sends = [pltpu.make_async_remote_copy(x_ref, rows(mx, my, mc), send.at[k], recv.at[k], device_id=p,
                                              device_id_type=MESH) for k, p in enumerate(peers)]
        for cp in sends:
            cp.start()
        for k, p in enumerate(peers):
            pltpu.make_async_remote_copy(x_ref, rows(*p), send.at[k], recv.at[k], device_id=p,
                                         device_id_type=MESH).wait_recv()
        for cp in sends:
            cp.wait_send()
        local.wait()

    vm = pl.BlockSpec(memory_space=pltpu.VMEM)
    return pl.pallas_call(
        body, in_specs=[vm], out_specs=vm, out_shape=jax.ShapeDtypeStruct((N_DEV * r, c), x.dtype),
        scratch_shapes=[pltpu.SemaphoreType.DMA((N_DEV - 1,)), pltpu.SemaphoreType.DMA((N_DEV - 1,)),
                        pltpu.SemaphoreType.DMA(())],
        name=name)(x)


class _GatherSmall:
    mid = None

    def __init__(self, x):
        self.inputs = [x]
        self.out_shapes = [jax.ShapeDtypeStruct((N_DEV * x.shape[0], x.shape[1]), x.dtype)]
        self.scratch = [pltpu.SemaphoreType.DMA((N_DEV - 1,)), pltpu.SemaphoreType.DMA((N_DEV - 1,)),
                        pltpu.SemaphoreType.DMA(())]

    def _plan(self, x_refs, o_refs, sems):
        send, recv, local_sem = sems
        x_ref, o_ref = x_refs[0], o_refs[0]
        r = x_ref.shape[0]
        mx, my, mc = _coords()

        def rows(px, py, pc):
            return o_ref.at[pl.ds(pl.multiple_of((4 * px + 2 * py + pc) * r, 8), r), :]

        peers = [(_flip(mx, k >> 2 & 1), _flip(my, k >> 1 & 1), _flip(mc, k & 1)) for k in range(1, N_DEV)]
        out = [pltpu.make_async_remote_copy(x_ref, rows(mx, my, mc), send.at[k], recv.at[k], device_id=p,
                                            device_id_type=MESH) for k, p in enumerate(peers)]
        arrivals = [pltpu.make_async_remote_copy(x_ref, rows(*p), send.at[k], recv.at[k], device_id=p,
                                                 device_id_type=MESH) for k, p in enumerate(peers)]
        return out, arrivals, pltpu.make_async_copy(x_ref, rows(mx, my, mc), local_sem)

    def start(self, x_refs, o_refs, sems):
        out, _, local = self._plan(x_refs, o_refs, sems)
        local.start()
        for cp in out:
            cp.start()

    def finish(self, x_refs, o_refs, sems):
        out, arrivals, local = self._plan(x_refs, o_refs, sems)
        for cp in arrivals:
            cp.wait_recv()
        for cp in out:
            cp.wait_send()
        local.wait()


class _GatherWeights:
    def __init__(self, shards):
        n_t = len(shards)
        self.inputs = list(shards)
        self.out_shapes = [jax.ShapeDtypeStruct((N_DEV * x.shape[0], x.shape[1]), x.dtype) for x in shards]
        self.scratch = [pltpu.SemaphoreType.DMA((n_t, 7)), pltpu.SemaphoreType.DMA((n_t, 7)),
                        pltpu.SemaphoreType.DMA((n_t,))]

    def _plan(self, x_refs, o_refs, sems):
        send, recv, local_sem = sems
        mx, my, mc = _coords()
        me, sibling = (mx, my, mc), (mx, my, 1 - mc)
        chips = [(1 - mx, my), (mx, 1 - my), (1 - mx, 1 - my)]

        def rows(t, px, py, pc):
            r = x_refs[t].shape[0]
            return o_refs[t].at[pl.ds(pl.multiple_of((4 * px + 2 * py + pc) * r, 8), r), :]

        def copy(t, k, block, to, src=None):
            return pltpu.make_async_remote_copy(
                src_ref=rows(t, *block) if src is None else src, dst_ref=rows(t, *block),
                send_sem=send.at[t, k], recv_sem=recv.at[t, k], device_id=to, device_id_type=MESH)

        def local(t):
            return pltpu.make_async_copy(x_refs[t], rows(t, *me), local_sem.at[t])

        return me, sibling, chips, mc, copy, local

    def start(self, x_refs, o_refs, sems):
        me, sibling, chips, mc, copy, local = self._plan(x_refs, o_refs, sems)
        for t in range(len(x_refs)):
            local(t).start()
            copy(t, 0, me, sibling, src=x_refs[t]).start()
            for j, chip in enumerate(chips):
                copy(t, 1 + j, me, (*chip, mc), src=x_refs[t]).start()

    def mid(self, x_refs, o_refs, sems):
        me, sibling, chips, mc, copy, local = self._plan(x_refs, o_refs, sems)
        for j, chip in enumerate(chips):
            for t in range(len(x_refs)):
                copy(t, 1 + j, (*chip, mc), me).wait_recv()
                copy(t, 4 + j, (*chip, mc), sibling).start()

    def finish(self, x_refs, o_refs, sems):
        me, sibling, chips, mc, copy, local = self._plan(x_refs, o_refs, sems)
        for t in range(len(x_refs)):
            copy(t, 0, sibling, me).wait_recv()
            for j, chip in enumerate(chips):
                copy(t, 4 + j, (*chip, 1 - mc), me).wait_recv()
            copy(t, 0, me, sibling, src=x_refs[t]).wait_send()
            for j, chip in enumerate(chips):
                copy(t, 1 + j, me, (*chip, mc), src=x_refs[t]).wait_send()
                copy(t, 4 + j, (*chip, mc), sibling).wait_send()
            local(t).wait()


class _SiblingExchange:
    mid = None

    def __init__(self, grads):
        n_t = len(grads)
        self.inputs = list(grads)
        self.out_shapes = [jax.ShapeDtypeStruct((N_CHIP,) + g.shape[2:], F32) for g in grads]
        self.scratch = [pltpu.SemaphoreType.DMA((n_t,)), pltpu.SemaphoreType.DMA((n_t,))]

    def _copies(self, g_refs, land, sems):
        send, recv = sems
        mx, my, mc = _coords()
        return [pltpu.make_async_remote_copy(g_refs[t].at[:, 1 - mc], land[t], send.at[t], recv.at[t],
                                             device_id=(mx, my, 1 - mc), device_id_type=MESH)
                for t in range(len(g_refs))]

    def start(self, g_refs, land, sems):
        for cp in self._copies(g_refs, land, sems):
            cp.start()

    def finish(self, g_refs, land, sems):
        for cp in self._copies(g_refs, land, sems):
            cp.wait()


class _Together:
    def __init__(self, *comms):
        self.comms = comms
        self.inputs = [x for c in comms for x in c.inputs]
        self.out_shapes = [x for c in comms for x in c.out_shapes]
        self.scratch = [x for c in comms for x in c.scratch]
        self.mid = self._mid if any(c.mid is not None for c in comms) else None

    def _each(self, phase, cin, cout, sems):
        i = o = s = 0
        for c in self.comms:
            fn = getattr(c, phase)
            ni, no, ns = len(c.inputs), len(c.out_shapes), len(c.scratch)
            if fn is not None:
                fn(cin[i:i + ni], cout[o:o + no], sems[s:s + ns])
            i, o, s = i + ni, o + no, s + ns

    def start(self, cin, cout, sems):
        self._each("start", cin, cout, sems)

    def _mid(self, cin, cout, sems):
        self._each("mid", cin, cout, sems)

    def finish(self, cin, cout, sems):
        self._each("finish", cin, cout, sems)


def _standalone(comm, name):
    def body():
        pass
    return _call(body, grid=(1,), in_specs=[], out_specs=[], out_shape=[], args=(), name=name, comm=comm)[1]


def _chip_partials(g4s, lands, name):
    n_t = len(g4s)
    in_specs, out_specs, out_shape = [], [], []
    for g4 in g4s:
        _, _, r, c = g4.shape
        in_specs.append(pl.BlockSpec((None, None, r, c), lambda q: (q, lax.axis_index("c"), 0, 0)))
        out_specs.append(pl.BlockSpec((None, r, c), lambda q: (q, 0, 0)))
        out_shape.append(jax.ShapeDtypeStruct((N_CHIP, r, c), BF16))
    in_specs += [pl.BlockSpec((None,) + g4.shape[2:], lambda q: (q, 0, 0)) for g4 in g4s]

    def body(*refs):
        for t in range(n_t):
            refs[2 * n_t + t][...] = (refs[t][...] + refs[n_t + t][...]).astype(BF16)

    return pl.pallas_call(body, grid=(N_CHIP,), in_specs=in_specs, out_specs=out_specs, out_shape=out_shape,
                          compiler_params=_params(1), name=name)(*g4s, *lands)


class _ChipExchange:
    mid = None

    def __init__(self, parts):
        n_t = len(parts)
        self.inputs = list(parts)
        self.out_shapes = [jax.ShapeDtypeStruct(p.shape, p.dtype) for p in parts]
        self.scratch = [pltpu.SemaphoreType.DMA((n_t, 3)), pltpu.SemaphoreType.DMA((n_t, 3)),
                        pltpu.SemaphoreType.DMA((n_t,))]

    def _plan(self, p_refs, land, sems):
        send, recv, local_sem = sems
        mx, my, mc = _coords()
        my_chip = 2 * mx + my
        peers = [(_flip(mx, fx), _flip(my, fy)) for fx, fy in ((1, 0), (0, 1), (1, 1))]

        def out(t, k):
            px, py = peers[k]
            return pltpu.make_async_remote_copy(p_refs[t].at[2 * px + py], land[t].at[my_chip], send.at[t, k],
                                                recv.at[t, k], device_id=(px, py, mc), device_id_type=MESH)

        def arrival(t, k):
            px, py = peers[k]
            return pltpu.make_async_remote_copy(p_refs[t].at[my_chip], land[t].at[2 * px + py], send.at[t, k],
                                                recv.at[t, k], device_id=(px, py, mc), device_id_type=MESH)

        def local(t):
            return pltpu.make_async_copy(p_refs[t].at[my_chip], land[t].at[my_chip], local_sem.at[t])

        return out, arrival, local

    def start(self, p_refs, land, sems):
        out, arrival, local = self._plan(p_refs, land, sems)
        for t in range(len(p_refs)):
            local(t).start()
            for k in range(3):
                out(t, k).start()

    def finish(self, p_refs, land, sems):
        out, arrival, local = self._plan(p_refs, land, sems)
        for t in range(len(p_refs)):
            for k in range(3):
                arrival(t, k).wait_recv()
                out(t, k).wait_send()
            local(t).wait()


def _rope_tables(s, width):
    heads = width // HEAD_DIM
    inv_freq = ROPE_THETA ** (-jnp.arange(0, HEAD_DIM, 2, dtype=F32) / HEAD_DIM)
    inv_full = jnp.tile(inv_freq, 2 * heads)
    sign = jnp.tile(jnp.concatenate([-jnp.ones((HALF_HEAD,), F32), jnp.ones((HALF_HEAD,), F32)]), heads)
    ang = jnp.arange(s, dtype=F32)[:, None] * inv_full[None, :]
    return jnp.cos(ang), jnp.sin(ang) * sign[None, :]


def _pad_rows(v, rows):
    return jnp.concatenate([v, jnp.zeros((rows - 1, v.shape[1]), v.dtype)], axis=0)


def kernel(x, c, w_ada, b_ada, ffn1_norm_g, ffn1_w_gate, ffn1_w_up, ffn1_w_down, mix_norm_g, w_in, conv_dw_w, conv_dw_b, conv_ln_g, conv_ln_b, attn_out_g, conv_out_g, w_out, ffn2_norm_g, ffn2_w_gate, ffn2_w_up, ffn2_w_down, final_norm_g, loss_target, m_w_ada, m_b_ada, m_ffn1_norm_g, m_ffn1_w_gate, m_ffn1_w_up, m_ffn1_w_down, m_mix_norm_g, m_w_in, m_conv_dw_w, m_conv_dw_b, m_conv_ln_g, m_conv_ln_b, m_attn_out_g, m_conv_out_g, m_w_out, m_ffn2_norm_g, m_ffn2_w_gate, m_ffn2_w_up, m_ffn2_w_down, m_final_norm_g, v_w_ada, v_b_ada, v_ffn1_norm_g, v_ffn1_w_gate, v_ffn1_w_up, v_ffn1_w_down, v_mix_norm_g, v_w_in, v_conv_dw_w, v_conv_dw_b, v_conv_ln_g, v_conv_ln_b, v_attn_out_g, v_conv_out_g, v_w_out, v_ffn2_norm_g, v_ffn2_w_gate, v_ffn2_w_up, v_ffn2_w_down, v_final_norm_g):
    mx, my, mc = _coords()
    me = 4 * mx + 2 * my + mc
    s, d = x.shape[1], x.shape[2]
    aw = d // 2
    x2, target = x[0], loss_target[0]
    n_mod = w_ada.shape[2] * N_DEV // d
    mod_cols = w_ada.shape[2]

    def shard(w, transpose):
        return (w[0].T if transpose else w[0]).astype(BF16)

    cw_shard = conv_dw_w.shape[3]
    n_taps = CONV_KERNEL * cw_shard
    first_len = -(-(d + n_taps) // LANES) * LANES
    first = jnp.concatenate([c, conv_dw_w[0, :, 0, :].reshape(1, n_taps), jnp.zeros((1, first_len - d - n_taps), F32)], axis=1)
    first_all, wg1 = _standalone(
        _Together(_GatherSmall(_pad_rows(first, 8)), _GatherWeights([shard(ffn1_w_gate, True)])), "ag_first")
    first_all = first_all[0::8]
    c_all = first_all[:, :d]
    conv_w = first_all[:, d:d + n_taps].reshape(N_DEV, CONV_KERNEL, cw_shard).transpose(1, 0, 2).reshape(CONV_KERNEL, aw)

    silu_c = _silu_rows(c_all, "silu_c")
    mod_part = _plain_mm([(silu_c, w_ada[0])], F32, False, mod_cols, "mod_mm")
    mod_all = _ag_small(mod_part, "ag_mod").reshape(N_DEV, N_DEV, mod_cols)
    mod = lax.dynamic_index_in_dim(mod_all, me, axis=1, keepdims=False).reshape(1, n_mod * d) + b_ada
    sh1, sc1, g1, sh2, sc2, g2, sh3, sc3, g3 = [mod[:, i * d:(i + 1) * d] for i in range(n_mod)]

    def split(g):
        return g.reshape(N_CHIP, 2, g.shape[0] // N_DEV, g.shape[1])

    def partials(g4s, lands, tag):
        return _chip_partials(g4s, lands, "chip_partials_" + tag)

    (n1, a1), (wu1,) = _norm_gate(x2, ffn1_norm_g, sc1, sh1, wg1, "ffn1_gate",
                                  comm=_GatherWeights([shard(ffn1_w_up, True)]))
    (silu1, gs1, hid1), (wd1,) = _ffn_up_given_gate(n1, wu1, a1, "ffn1_up",
                                                    comm=_GatherWeights([shard(ffn1_w_down, False)]))
    (h1, f1, n2), (win_t,) = _residual_mm(hid1, wd1, x2, g1, 0.5, "ffn1_down", norm=(mix_norm_g, sc2, sh2),
                                          comm=_GatherWeights([shard(w_in, True)]))
    cos, sin_signed = _rope_tables(s, LANES)
    (proj,), (wd2,) = _proj_rope(n2, win_t, cos, sin_signed, aw, "proj",
                                 comm=_GatherWeights([shard(ffn2_w_down, False)]))
    lanes_per = aw // LANES
    (attn, lse), (wg2, wu2) = _attn_seq_fwd(
        proj, aw, "attn_fwd", comm=_GatherWeights([shard(ffn2_w_gate, True), shard(ffn2_w_up, True)]))
    (u1,), (wout,) = _conv_fwd(proj, 3 * lanes_per, 4 * lanes_per, conv_w, conv_dw_b, "conv_fwd",
                               comm=_GatherWeights([shard(w_out, False)]))
    post = (attn_out_g, conv_ln_g, conv_ln_b, conv_out_g)
    y, h2, mix, n3 = _mix_out(attn, u1, post, wout, h1, g2, (ffn2_norm_g, sc3, sh3), "mix_out")
    silu3, gs3, hid3 = _ffn_up(n3, wg2, wu2, "ffn2_up")

    dh3, df3, err2, d_final_g, dg3 = _last_mm_loss(hid3, wd2, h2, g3, 0.5, target, final_norm_g.reshape(1, d),
                                                   "ffn2_down_loss")
    loss_part = jnp.zeros((1, LANES), F32).at[0, 0].set(0.5 * jnp.sum(err2) / d)

    da3, db3 = _ffn_bwd_hidden(df3, wd2, silu3, gs3, "ffn2_hidden_bwd")
    g4_a = [split(_mm_tn(da3, n3, "ffn2_dwg")), split(_mm_tn(db3, n3, "ffn2_dwu")), split(_mm_tn(hid3, df3, "ffn2_dwd"))]
    (dh2, dmix, dsh3, dsc3, dgn3, dg2), land_a = _mm_norm_mod_bwd(
        [(da3, wg2), (db3, wu2)], h2, dh3, ffn2_norm_g, sc3, (mix, g2, 1.0), "ffn2_dn_norm3_bwd", tm=256,
        comm=_SiblingExchange(g4_a))
    parts_a = partials(g4_a, land_a, "a")
    g_wout = _mm_tn(y, dmix, "mix_dwout")
    dattn, du1, d_gains, d_ln = _mix_dy_post_bwd(dmix, wout, attn, u1, post, "mix_dy_post_bwd")
    d_attn_g, d_conv_g, d_ln_g, d_ln_b = d_gains[:, :aw], d_gains[:, aw:], d_ln[:, :aw], d_ln[:, aw:]
    dga, dgb, d_taps, d_conv_b = _conv_bwd(proj, 3 * lanes_per, 4 * lanes_per, conv_w, du1, "conv_bwd")
    (dq, dk, dv), sums_a = _attn_seq_bwd(proj, dattn, attn, lse, cos, sin_signed, "attn_bwd",
                                         comm=_ChipExchange(parts_a))
    dproj = jnp.concatenate([dq, dk, dv, dga, dgb], axis=1)
    g4_b = [split(g_wout), split(_mm_tn(dproj, n2, "mix_dwin"))]
    (dh1, df1, dsh2, dsc2, dgn2, dg1), land_b = _mm_norm_mod_bwd(
        [(dproj, win_t)], h1, dh2, mix_norm_g, sc2, (f1, g1, 0.5), "mix_dn_norm2_bwd", tm=512,
        comm=_SiblingExchange(g4_b))
    parts_b = partials(g4_b, land_b, "b")
    g4_c = [split(_mm_tn(hid1, df1, "ffn1_dwd"))]
    (da1, db1), both = _ffn_bwd_hidden(df1, wd1, silu1, gs1, "ffn1_hidden_bwd",
                                       comm=_Together(_ChipExchange(parts_b), _SiblingExchange(g4_c)))
    sums_b, land_c = both[:2], both[2:]
    parts_c = partials(g4_c, land_c, "c")
    g_wu1, sums_c = _mm_tn(db1, n1, "ffn1_dwu", comm=_ChipExchange(parts_c))
    g4_d = [split(g_wu1)]
    g_wg1, land_d = _mm_tn(da1, n1, "ffn1_dwg", comm=_SiblingExchange(g4_d))
    parts_d = partials(g4_d, land_d, "d")
    g4_e = [split(g_wg1)]
    dn1, both = _plain_mm([(da1, wg1), (db1, wu1)], BF16, False, d, "ffn1_dn",
                          comm=_Together(_ChipExchange(parts_d), _SiblingExchange(g4_e)))
    sums_d, land_e = both[:1], both[1:]
    parts_e = partials(g4_e, land_e, "e")
    (dx, dsh1, dsc1, dgn1), sums_e = _norm_mod_bwd(dn1, x2, dh1, ffn1_norm_g, sc1, "norm1_bwd",
                                                   comm=_ChipExchange(parts_e))

    dmod = jnp.concatenate([dsh1, dsc1, dg1, dsh2, dsc2, dg2, dsh3, dsc3, dg3], axis=1)
    small = [dmod, dgn1, dgn2, dgn3, d_final_g, d_conv_b, d_ln_g, d_ln_b, d_attn_g, d_conv_g,
             d_taps.reshape(1, CONV_KERNEL * aw), loss_part]
    sizes = [v.shape[1] for v in small]
    total = sum(sizes)
    padded = -(-total // (8 * LANES)) * (8 * LANES)
    packed = jnp.concatenate(small + [jnp.zeros((1, padded - total), F32)], axis=1).reshape(8, padded // 8)
    gathered = _ag_small(packed, "ag_small_grads")
    summed = _sum_blocks(gathered, N_DEV, "sum_small_grads").reshape(1, padded)
    offs = [sum(sizes[:i]) for i in range(len(sizes))]
    (g_b_ada, g_gn1, g_gn2, g_gn3, g_final, g_conv_b, g_ln_g, g_ln_b, g_attn_g, g_conv_g, g_taps, loss_row) = [
        summed[:, o:o + n] for o, n in zip(offs, sizes)]
    loss = loss_row[0, 0]
    g_taps_shard = lax.dynamic_slice_in_dim(g_taps.reshape(CONV_KERNEL, aw), me * cw_shard, cw_shard, axis=1)
    dmod_all = gathered.reshape(N_DEV, padded)[:, :n_mod * d]
    dmod_cols = lax.dynamic_slice_in_dim(dmod_all, me * mod_cols, mod_cols, axis=1)
    g_w_ada = _mm_tn(silu_c, dmod_cols, "ada_dw")

    arrived = dict(zip(["ffn2_w_gate", "ffn2_w_up", "ffn2_w_down", "w_out", "w_in", "ffn1_w_down", "ffn1_w_up",
                        "ffn1_w_gate"], list(sums_a) + list(sums_b) + list(sums_c) + list(sums_d) + list(sums_e)))
    transposed = ("ffn1_w_gate", "ffn1_w_up", "w_in", "ffn2_w_gate", "ffn2_w_up")
    grads = {
        "w_ada": g_w_ada, "b_ada": g_b_ada, "ffn1_norm_g": g_gn1, "mix_norm_g": g_gn2, "conv_dw_w": g_taps_shard,
        "conv_dw_b": g_conv_b, "conv_ln_g": g_ln_g, "conv_ln_b": g_ln_b, "attn_out_g": g_attn_g,
        "conv_out_g": g_conv_g, "ffn2_norm_g": g_gn3, "final_norm_g": g_final,
    }
    weights = dict(w_ada=w_ada, b_ada=b_ada, ffn1_norm_g=ffn1_norm_g, ffn1_w_gate=ffn1_w_gate, ffn1_w_up=ffn1_w_up, ffn1_w_down=ffn1_w_down, mix_norm_g=mix_norm_g, w_in=w_in, conv_dw_w=conv_dw_w, conv_dw_b=conv_dw_b, conv_ln_g=conv_ln_g, conv_ln_b=conv_ln_b, attn_out_g=attn_out_g, conv_out_g=conv_out_g, w_out=w_out, ffn2_norm_g=ffn2_norm_g, ffn2_w_gate=ffn2_w_gate, ffn2_w_up=ffn2_w_up, ffn2_w_down=ffn2_w_down, final_norm_g=final_norm_g)
    moms = dict(w_ada=m_w_ada, b_ada=m_b_ada, ffn1_norm_g=m_ffn1_norm_g, ffn1_w_gate=m_ffn1_w_gate, ffn1_w_up=m_ffn1_w_up, ffn1_w_down=m_ffn1_w_down, mix_norm_g=m_mix_norm_g, w_in=m_w_in, conv_dw_w=m_conv_dw_w, conv_dw_b=m_conv_dw_b, conv_ln_g=m_conv_ln_g, conv_ln_b=m_conv_ln_b, attn_out_g=m_attn_out_g, conv_out_g=m_conv_out_g, w_out=m_w_out, ffn2_norm_g=m_ffn2_norm_g, ffn2_w_gate=m_ffn2_w_gate, ffn2_w_up=m_ffn2_w_up, ffn2_w_down=m_ffn2_w_down, final_norm_g=m_final_norm_g)
    vars_ = dict(w_ada=v_w_ada, b_ada=v_b_ada, ffn1_norm_g=v_ffn1_norm_g, ffn1_w_gate=v_ffn1_w_gate, ffn1_w_up=v_ffn1_w_up, ffn1_w_down=v_ffn1_w_down, mix_norm_g=v_mix_norm_g, w_in=v_w_in, conv_dw_w=v_conv_dw_w, conv_dw_b=v_conv_dw_b, conv_ln_g=v_conv_ln_g, conv_ln_b=v_conv_ln_b, attn_out_g=v_attn_out_g, conv_out_g=v_conv_out_g, w_out=v_w_out, ffn2_norm_g=v_ffn2_norm_g, ffn2_w_gate=v_ffn2_w_gate, ffn2_w_up=v_ffn2_w_up, ffn2_w_down=v_ffn2_w_down, final_norm_g=v_final_norm_g)
    names = list(weights)
    big = ["w_ada", "ffn1_w_gate", "ffn1_w_up", "ffn1_w_down", "w_in", "w_out", "ffn2_w_gate", "ffn2_w_up",
           "ffn2_w_down"]
    shape2 = {n: (weights[n].shape[-2] if weights[n].ndim > 1 else 1, weights[n].shape[-1]) for n in names}
    shape2["conv_dw_w"] = (CONV_KERNEL, cw_shard)
    g_out, d_out, m_out, v_out = {}, {}, {}, {}
    for n in big:
        if n in arrived:
            def view(t, n=n):
                return t[0].T if n in transposed else t[0]
            res = _adamw_reduced(view(weights[n]), arrived[n], view(moms[n]), view(vars_[n]), "adamw_" + n)
            g_out[n], d_out[n], m_out[n], v_out[n] = [r.T if n in transposed else r for r in res]
        else:
            g2d = grads[n].reshape(shape2[n])
            res = _adamw_big(weights[n].reshape(shape2[n]), g2d, moms[n].reshape(shape2[n]),
                             vars_[n].reshape(shape2[n]), "adamw_" + n)
            g_out[n], (d_out[n], m_out[n], v_out[n]) = g2d, res
    rest = [n for n in names if n not in big]
    res = _adamw_small([weights[n].reshape(shape2[n]) for n in rest], [grads[n].reshape(shape2[n]) for n in rest],
                       [moms[n].reshape(shape2[n]) for n in rest], [vars_[n].reshape(shape2[n]) for n in rest],
                       "adamw_small")
    for i, n in enumerate(rest):
        g_out[n], d_out[n], m_out[n], v_out[n] = grads[n], res[0][i], res[1][i], res[2][i]

    def shaped(table):
        return [table[n].reshape(weights[n].shape) for n in names]

    return (loss, dx.reshape(x.shape), *shaped(g_out), *shaped(d_out), *shaped(m_out), *shaped(v_out))
```

```python
import functools

import jax
import jax.numpy as jnp
from jax import lax
from jax.experimental import pallas as pl
from jax.experimental.pallas import tpu as pltpu

F32 = jnp.float32
BF16 = jnp.bfloat16
MESH = pl.DeviceIdType.MESH
ANY = pl.BlockSpec(memory_space=pl.ANY)

N_DEV = 8
N_CHIP = 4
HEAD_DIM = 64
HALF_HEAD = HEAD_DIM // 2
LANES = 128
BLOCK = 128
DILATIONS = (1, 4, 16)
MERGE_CHUNK = 512
ROPE_THETA = 10000.0
CONV_KERNEL = 31
CONV_HALO = 32
CONV_CHUNK = 512
CONV_SUB = 128
RMS_EPS = 1e-6
LN_EPS = 1e-5
ADAM_LR = 0.001
ADAM_B1 = 0.9
ADAM_B2 = 0.999
ADAM_EPS = 1e-08
ADAM_WD = 0.01
ADAM_STEP = 10
VMEM_LIMIT = 56 * 1024 * 1024
NEG = -1e30


def _params(n_axes):
    return pltpu.CompilerParams(dimension_semantics=("arbitrary",) * n_axes, vmem_limit_bytes=VMEM_LIMIT)


def _tile(n, target, unit):
    best = None
    for t in range(unit, min(n, target) + 1, unit):
        if n % t == 0:
            best = t
    return best if best is not None else n


def _sigmoid(x):
    return 0.5 * (jnp.tanh(0.5 * x) + 1.0)


def _call(body, *, grid, in_specs, out_specs, out_shape, args, name, scratch_shapes=(), comm=None):
    params = _params(len(grid))
    if comm is None:
        return pl.pallas_call(body, grid=grid, in_specs=list(in_specs), out_specs=list(out_specs),
                              out_shape=list(out_shape), scratch_shapes=list(scratch_shapes),
                              compiler_params=params, name=name)(*args)
    n_in, n_out, n_scr = len(args), len(out_shape), len(scratch_shapes)
    c_in, c_out = len(comm.inputs), len(comm.out_shapes)
    steps = 1
    for g in grid:
        steps *= g

    def hosted(*refs):
        pos = 0
        parts = []
        for size in (n_in, c_in, n_out, c_out, n_scr, len(comm.scratch)):
            parts.append(refs[pos:pos + size])
            pos += size
        ins, cin, outs, cout, scr, cscr = parts
        step = 0
        for axis, g in enumerate(grid):
            step = step * g + pl.program_id(axis)

        @pl.when(step == 0)
        def _():
            comm.start(cin, cout, cscr)

        body(*ins, *outs, *scr)
        if comm.mid is not None and steps >= 4:
            @pl.when(step == steps // 2)
            def _():
                comm.mid(cin, cout, cscr)

        @pl.when(step == steps - 1)
        def _():
            if comm.mid is not None and steps < 4:
                comm.mid(cin, cout, cscr)
            comm.finish(cin, cout, cscr)

    res = pl.pallas_call(
        hosted, grid=grid, in_specs=list(in_specs) + [ANY] * c_in, out_specs=list(out_specs) + [ANY] * c_out,
        out_shape=list(out_shape) + list(comm.out_shapes), scratch_shapes=list(scratch_shapes) + list(comm.scratch),
        compiler_params=params, name=name)(*args, *comm.inputs)
    return res[:n_out], res[n_out:]


def _rows(fn, rows_in, vecs_in, rows_out, vecs_out, *, tile, name, comm=None):
    norm = [r if isinstance(r, tuple) else (r, r.shape[1], 0) for r in rows_in]
    n_rows = norm[0][0].shape[0]
    n_tiles = n_rows // tile
    in_specs, args = [], []
    for arr, width, cb in norm:
        in_specs.append(pl.BlockSpec((tile, width), functools.partial(lambda i, cb: (i, cb), cb=cb)))
        args.append(arr)
    for v in vecs_in:
        in_specs.append(pl.BlockSpec((1, v.shape[1]), lambda i: (0, 0)))
        args.append(v)
    out_shape = [jax.ShapeDtypeStruct((n_rows, w), dt) for w, dt in rows_out]
    out_shape += [jax.ShapeDtypeStruct((1, w), F32) for w in vecs_out]
    out_specs = [pl.BlockSpec((tile, w), lambda i: (i, 0)) for w, _ in rows_out]
    out_specs += [pl.BlockSpec((1, w), lambda i: (0, 0)) for w in vecs_out]
    n_in, n_ro = len(args), len(rows_out)

    def body(*refs):
        vals = [r[...] for r in refs[:n_in]]
        outs = refs[n_in:]
        row_vals, vec_vals = fn(*vals)
        for ref, val in zip(outs[:n_ro], row_vals):
            if isinstance(val, tuple):
                w = val[0].shape[1]
                for j, piece in enumerate(val):
                    ref[:, j * w:(j + 1) * w] = piece.astype(ref.dtype)
            else:
                ref[...] = val.astype(ref.dtype)
        if vecs_out:
            @pl.when(pl.program_id(0) == 0)
            def _():
                for ref in outs[n_ro:]:
                    ref[...] = jnp.zeros_like(ref)
            for ref, val in zip(outs[n_ro:], vec_vals):
                ref[...] += val

    return _call(body, grid=(n_tiles,), in_specs=in_specs, out_specs=out_specs, out_shape=out_shape, args=args,
                 name=name, comm=comm)


def _colsum(x):
    return jnp.sum(x, axis=0, keepdims=True)


def _rms_stats(h):
    r = lax.rsqrt(jnp.mean(h * h, axis=-1, keepdims=True) + RMS_EPS)
    return r, h * r


def _rms_back(r, xn, dxn):
    return r * (dxn - xn * jnp.mean(dxn * xn, axis=-1, keepdims=True))


def _branch_back(dh, f, gate, coef):
    return (coef * gate) * dh, coef * _colsum(f.astype(F32) * dh)


def _norm_mod_back(dn, h, dh_in, gain, scale):
    dn = dn.astype(F32)
    r, xn = _rms_stats(h)
    y = xn * gain
    dy = dn * (1.0 + scale)
    dh = dh_in + _rms_back(r, xn, dy * gain)
    return dh, [_colsum(dn), _colsum(dn * y), _colsum(dy * xn)]


def _norm_mod_bwd(dn, h, dh_in, gain, scale, name, comm=None):
    d = h.shape[1]

    def fn(dn, h, dh_in, gain, scale):
        dh, vecs = _norm_mod_back(dn, h, dh_in, gain, scale)
        return [dh], vecs
    return _rows(fn, [dn, h, dh_in], [gain, scale], [(d, F32)], [d, d, d], tile=256, name=name, comm=comm)


def _mm_norm_mod_bwd(pairs, h, dh_in, gain, scale, branch, name, tm, comm=None):
    f, gate, coef = branch

    def epi(accs, ex, vc):
        dh, vecs = _norm_mod_back(accs[0], ex[0], ex[1], vc[0], vc[1])
        df, dgate = _branch_back(dh, ex[2], vc[2], coef)
        return [dh, df] + vecs + [dgate]
    return _mm([pairs], epi, [h, dh_in, f], [gain, scale, gate], [F32, BF16], trans_rhs=False, tm=tm,
               tn=h.shape[1], name=name, n_sums=4, comm=comm)


def _last_mm_loss(lhs, w, res, gate, coef, target, gain, name):
    d = w.shape[1]

    def epi(accs, ex, vc):
        f = accs[0]
        h = ex[0] + (coef * vc[0]) * f
        r, xn = _rms_stats(h)
        err = xn * vc[1] - ex[1]
        dout = err * (1.0 / d)
        dh = _rms_back(r, xn, dout * vc[1])
        df, dgate = _branch_back(dh, f, vc[0], coef)
        return [dh, df, _colsum(err * err), _colsum(dout * xn), dgate]
    return _mm([[(lhs, w)]], epi, [res, target], [gate, gain], [F32, BF16], trans_rhs=False, tm=256, tn=d,
               name=name, n_sums=3)


def _partner(x):
    if x.shape[1] > LANES:
        return jnp.concatenate([_partner(x[:, c:c + LANES]) for c in range(0, x.shape[1], LANES)], axis=1)
    lane = lax.broadcasted_iota(jnp.int32, x.shape, 1) % HEAD_DIM
    return jnp.where(lane < HALF_HEAD, pltpu.roll(x, LANES - HALF_HEAD, 1), pltpu.roll(x, HALF_HEAD, 1))


def _proj_rope(n, w_t, cos, sin_signed, width, name, comm=None):
    s, kdim = n.shape
    n_cols = w_t.shape[0]
    tm = _tile(s, 1024, 8)
    qscale = HEAD_DIM ** -0.5

    chunk = _tile(tm, 256, 8)

    def body(n_ref, w_ref, cos_ref, sin_ref, o_ref):
        j = pl.program_id(0)

        def products(rows):
            return lax.dot_general(n_ref[rows, :].astype(BF16), w_ref[...].astype(BF16), (((1,), (1,)), ((), ())),
                                   preferred_element_type=F32)

        @pl.when(j >= 2)
        def _():
            for c in range(tm // chunk):
                rows = slice(c * chunk, (c + 1) * chunk)
                o_ref[rows, :] = products(rows)

        @pl.when(j < 2)
        def _():
            scale = jnp.where(j == 0, qscale, 1.0)
            for c in range(tm // chunk):
                rows = slice(c * chunk, (c + 1) * chunk)
                acc = products(rows)
                cos = jnp.tile(cos_ref[rows, :], (1, width // LANES))
                sin = jnp.tile(sin_ref[rows, :], (1, width // LANES))
                o_ref[rows, :] = scale * (acc * cos + _partner(acc) * sin)

    table = pl.BlockSpec((tm, LANES), lambda j, i: (jnp.where(j < 2, i, 0), 0))
    return _call(
        body, grid=(n_cols // width, s // tm),
        in_specs=[pl.BlockSpec((tm, kdim), lambda j, i: (i, 0)), pl.BlockSpec((width, kdim), lambda j, i: (j, 0)),
                  table, table],
        out_specs=[pl.BlockSpec((tm, width), lambda j, i: (i, j))],
        out_shape=[jax.ShapeDtypeStruct((s, n_cols), F32)], args=(n, w_t, cos, sin_signed), name=name, comm=comm)


def _mix_post(attn, u1, attn_g, ln_g, ln_b, conv_g):
    _, xa = _rms_stats(attn)
    mu = jnp.mean(u1, axis=-1, keepdims=True)
    xc = u1 - mu
    rstd = lax.rsqrt(jnp.mean(xc * xc, axis=-1, keepdims=True) + LN_EPS)
    u2 = (xc * rstd) * ln_g + ln_b
    u3 = u2 * _sigmoid(u2)
    _, x3 = _rms_stats(u3)
    return jnp.concatenate([xa * attn_g, x3 * conv_g], axis=1)


def _mix_post_back(dy, attn, u1, attn_g, ln_g, ln_b, conv_g):
    w = attn.shape[1]
    dya, dyc = dy[:, :w], dy[:, w:]
    ra, xa = _rms_stats(attn)
    dattn = _rms_back(ra, xa, dya * attn_g)
    mu = jnp.mean(u1, axis=-1, keepdims=True)
    xc = u1 - mu
    rstd = lax.rsqrt(jnp.mean(xc * xc, axis=-1, keepdims=True) + LN_EPS)
    xh = xc * rstd
    u2 = xh * ln_g + ln_b
    sig = _sigmoid(u2)
    u3 = u2 * sig
    r3, x3 = _rms_stats(u3)
    du3 = _rms_back(r3, x3, dyc * conv_g)
    du2 = du3 * (sig + u3 * (1.0 - sig))
    dxh = du2 * ln_g
    du1 = rstd * (dxh - jnp.mean(dxh, axis=-1, keepdims=True) - xh * jnp.mean(dxh * xh, axis=-1, keepdims=True))
    return dattn, du1, [_colsum(dya * xa), _colsum(dyc * x3), _colsum(du2 * xh), _colsum(du2)]


def _silu_rows(c_all, name):
    def fn(c):
        return [c * _sigmoid(c)], []
    return _rows(fn, [c_all], [], [(c_all.shape[1], BF16)], [], tile=c_all.shape[0], name=name)[0]


def _mm(groups, epi, extras, vecs, outs, *, trans_rhs, tm, tn, name, n_sums=0, pre=None, pre_inputs=(),
        comm=None):
    m = (pre_inputs[0] if pre is not None else groups[0][0][0]).shape[0]
    n = groups[0][0][1].shape[0] if trans_rhs else groups[0][0][1].shape[1]
    tm, tn = min(tm, m), min(tn, n)
    in_specs, args, uses_pre = [], [], []
    for grp in groups:
        for lhs, rhs in grp:
            k = rhs.shape[1] if trans_rhs else rhs.shape[0]
            uses_pre.append(lhs is None)
            if lhs is not None:
                in_specs.append(pl.BlockSpec((tm, k), lambda j, i: (i, 0)))
                args.append(lhs)
            in_specs.append(pl.BlockSpec((tn, k), lambda j, i: (j, 0)) if trans_rhs
                            else pl.BlockSpec((k, tn), lambda j, i: (0, j)))
            args.append(rhs)
    n_mm = len(args)
    for p in pre_inputs:
        in_specs.append(pl.BlockSpec((tm, p.shape[1]), lambda j, i: (i, 0)))
        args.append(p)
    for e in extras:
        in_specs.append(pl.BlockSpec((tm, tn), lambda j, i: (i, j)) if e.shape[1] == n
                        else pl.BlockSpec((tm, e.shape[1]), lambda j, i: (i, 0)))
        args.append(e)
    for v in vecs:
        in_specs.append(pl.BlockSpec((1, tn), lambda j, i: (0, j)) if v.shape[1] == n
                        else pl.BlockSpec((1, v.shape[1]), lambda j, i: (0, 0)))
        args.append(v)
    sizes = [len(g) for g in groups]
    n_pre, n_ex, n_vec = len(pre_inputs), len(extras), len(vecs)
    dims = (((1,), (1,)), ((), ())) if trans_rhs else (((1,), (0,)), ((), ()))
    out_specs, out_shape = [], []
    if pre is not None:
        k_pre = args[n_mm - 1].shape[1] if trans_rhs else args[n_mm - 1].shape[0]
        out_specs.append(pl.BlockSpec((tm, k_pre), lambda j, i: (i, 0)))
        out_shape.append(jax.ShapeDtypeStruct((m, k_pre), BF16))
    for o in outs:
        dt, width = o if isinstance(o, tuple) else (o, n)
        out_specs.append(pl.BlockSpec((tm, tn), lambda j, i: (i, j)) if width == n
                         else pl.BlockSpec((tm, width), lambda j, i: (i, 0)))
        out_shape.append(jax.ShapeDtypeStruct((m, width), dt))
    n_tiles_out = len(out_specs)
    out_specs += [pl.BlockSpec((1, tn), lambda j, i: (0, j))] * n_sums
    out_shape += [jax.ShapeDtypeStruct((1, n), F32)] * n_sums

    def body(*refs):
        ins = refs[:n_mm + n_pre + n_ex + n_vec]
        out_refs = refs[n_mm + n_pre + n_ex + n_vec:]
        vc = [r[...] for r in ins[n_mm + n_pre + n_ex:]]
        vals = []
        made = None
        if pre is not None:
            made = pre([r[...] for r in ins[n_mm:n_mm + n_pre]], vc).astype(BF16)
            vals.append(made)
        accs, pos, pair = [], 0, 0
        for size in sizes:
            acc = None
            for _ in range(size):
                if uses_pre[pair]:
                    lhs_tile = made
                else:
                    lhs_tile = ins[pos][...].astype(BF16)
                    pos += 1
                part = lax.dot_general(lhs_tile, ins[pos][...].astype(BF16), dims, preferred_element_type=F32)
                acc = part if acc is None else acc + part
                pos += 1
                pair += 1
            accs.append(acc)
        ex = [r[...] for r in ins[n_mm + n_pre:n_mm + n_pre + n_ex]]
        vals += epi(accs, ex, vc)
        for ref, val in zip(out_refs[:n_tiles_out], vals):
            ref[...] = val.astype(ref.dtype)
        if n_sums:
            @pl.when(pl.program_id(1) == 0)
            def _():
                for ref in out_refs[n_tiles_out:]:
                    ref[...] = jnp.zeros_like(ref)
            for ref, val in zip(out_refs[n_tiles_out:], vals[n_tiles_out:]):
                ref[...] += val

    return _call(body, grid=(n // tn, m // tm), in_specs=in_specs, out_specs=out_specs, out_shape=out_shape,
                 args=args, name=name, comm=comm)


def _mm_tn(lhs, rhs, name, comm=None):
    t, a = lhs.shape
    b = rhs.shape[1]
    ta = a if a <= 1536 else _tile(a, 1536, LANES)
    tk = _tile(t, 2048, 8)

    def body(l_ref, r_ref, o_ref):
        @pl.when(pl.program_id(1) == 0)
        def _():
            o_ref[...] = jnp.zeros_like(o_ref)
        o_ref[...] += lax.dot_general(l_ref[...].astype(BF16), r_ref[...].astype(BF16), (((0,), (0,)), ((), ())),
                                      preferred_element_type=F32)

    res = _call(body, grid=(a // ta, t // tk),
                in_specs=[pl.BlockSpec((tk, ta), lambda i, k: (k, i)), pl.BlockSpec((tk, b), lambda i, k: (k, 0))],
                out_specs=[pl.BlockSpec((ta, b), lambda i, k: (i, 0))], out_shape=[jax.ShapeDtypeStruct((a, b), F32)],
                args=(lhs, rhs), name=name, comm=comm)
    return res[0] if comm is None else (res[0][0], res[1])


def _ffn_tn(f):
    return _tile(f, 1536, LANES)


def _swiglu_parts(a, b):
    sig = _sigmoid(a)
    silu = a * sig
    return [silu, b * (sig + silu * (1.0 - sig)), silu * b]


def _ffn_up(n, wg_t, wu_t, name, comm=None):
    def epi(accs, ex, vc):
        return _swiglu_parts(accs[0], accs[1])
    return _mm([[(n, wg_t)], [(n, wu_t)]], epi, [], [], [BF16, BF16, BF16], trans_rhs=True, tm=512,
               tn=_ffn_tn(wg_t.shape[0]), name=name, comm=comm)


def _norm_gate(h, gain, scale, shift, wg_t, name, comm=None):
    def pre(tiles, vc):
        _, xn = _rms_stats(tiles[0])
        return (xn * vc[0]) * (1.0 + vc[1]) + vc[2]

    def epi(accs, ex, vc):
        return [accs[0]]
    return _mm([[(None, wg_t)]], epi, [], [gain, scale, shift], [BF16], trans_rhs=True, tm=512,
               tn=wg_t.shape[0], name=name, pre=pre, pre_inputs=[h], comm=comm)


def _mix_out(attn, u1, post, w, res, gate, norm, name):
    def pre(tiles, vc):
        return _mix_post(tiles[0], tiles[1], *vc[4:8])

    def epi(accs, ex, vc):
        h = ex[0] + vc[0] * accs[0]
        _, xn = _rms_stats(h)
        return [h, accs[0], (xn * vc[1]) * (1.0 + vc[2]) + vc[3]]
    return _mm([[(None, w)]], epi, [res], [gate] + list(norm) + list(post), [F32, BF16, BF16], trans_rhs=False,
               tm=512, tn=w.shape[1], name=name, pre=pre, pre_inputs=[attn, u1])


def _mix_dy_post_bwd(dmix, w, attn, u1, post, name):
    width = attn.shape[1]

    def epi(accs, ex, vc):
        dattn, du1, sums = _mix_post_back(accs[0], ex[0], ex[1], *vc)
        return [dattn, du1, jnp.concatenate(sums[0:2], axis=1), jnp.concatenate(sums[2:4], axis=1)]
    return _mm([[(dmix, w)]], epi, [attn, u1], list(post), [(F32, width), (F32, width)], trans_rhs=True, tm=256,
               tn=w.shape[0], name=name, n_sums=2)


def _ffn_up_given_gate(n, wu_t, a, name, comm=None):
    def epi(accs, ex, vc):
        return _swiglu_parts(ex[0].astype(F32), accs[0])
    return _mm([[(n, wu_t)]], epi, [a], [], [BF16, BF16, BF16], trans_rhs=True, tm=512,
               tn=_ffn_tn(wu_t.shape[0]), name=name, comm=comm)


def _residual_mm(lhs, w, res, gate, coef, name, norm=None, comm=None):
    def epi(accs, ex, vc):
        h = ex[0] + (coef * vc[0]) * accs[0]
        if norm is None:
            return [h, accs[0]]
        _, xn = _rms_stats(h)
        return [h, accs[0], (xn * vc[1]) * (1.0 + vc[2]) + vc[3]]
    vecs = [gate] + (list(norm) if norm is not None else [])
    outs = [F32, BF16] + ([BF16] if norm is not None else [])
    return _mm([[(lhs, w)]], epi, [res], vecs, outs, trans_rhs=False, tm=512, tn=w.shape[1], name=name, comm=comm)


def _ffn_bwd_hidden(df, wd, dhid_db, dhid_da, name, comm=None):
    def epi(accs, ex, vc):
        return [accs[0] * ex[1].astype(F32), accs[0] * ex[0].astype(F32)]
    return _mm([[(df, wd)]], epi, [dhid_db, dhid_da], [], [BF16, BF16], trans_rhs=True, tm=512,
               tn=_ffn_tn(wd.shape[0]), name=name, comm=comm)


def _plain_mm(pairs, out_dtype, trans_rhs, tn, name, tm=512, comm=None):
    def epi(accs, ex, vc):
        return [accs[0]]
    res = _mm([pairs], epi, [], [], [out_dtype], trans_rhs=trans_rhs, tm=tm, tn=tn, name=name, comm=comm)
    return res[0] if comm is None else (res[0][0], res[1])


HEADS_PER_TILE = LANES // HEAD_DIM


def _stack_heads(x):
    lane = lax.broadcasted_iota(jnp.int32, (1, LANES), 1)
    return jnp.concatenate([x * (lane // HEAD_DIM == h).astype(F32) for h in range(HEADS_PER_TILE)], axis=0)


def _unstack_heads(y):
    r = y.shape[0] // HEADS_PER_TILE
    lane = lax.broadcasted_iota(jnp.int32, (r, y.shape[1]), 1)
    out = y[0:r]
    for h in range(1, HEADS_PER_TILE):
        out = jnp.where(lane // HEAD_DIM == h, y[h * r:(h + 1) * r], out)
    return out


def _stacked_lse(lb):
    return jnp.concatenate([_lane_pick(lb, h) for h in range(HEADS_PER_TILE)], axis=0)


def _band_masks(n_row_blocks, n_col_blocks):
    shape = (n_row_blocks * BLOCK, n_col_blocks * BLOCK)
    qi = lax.broadcasted_iota(jnp.int32, shape, 0) % BLOCK
    kj = lax.broadcasted_iota(jnp.int32, shape, 1) % BLOCK
    return kj <= qi, kj >= qi


def _query_masks():
    first_valid, _ = _band_masks(HEADS_PER_TILE, 1)
    same_ok, before_ok = _band_masks(HEADS_PER_TILE, 2)
    is_cur = lax.broadcasted_iota(jnp.int32, same_ok.shape, 1) >= BLOCK
    return first_valid, jnp.logical_and(is_cur, same_ok), jnp.logical_and(jnp.logical_not(is_cur), before_ok)


def _dot_nt(a, b):
    return lax.dot_general(a.astype(BF16), b.astype(BF16), (((1,), (1,)), ((), ())), preferred_element_type=F32)


def _dot_nn(a, b):
    return lax.dot_general(a.astype(BF16), b.astype(BF16), (((1,), (0,)), ((), ())), preferred_element_type=F32)


def _dot_tn(a, b):
    return lax.dot_general(a.astype(BF16), b.astype(BF16), (((0,), (0,)), ((), ())), preferred_element_type=F32)


def _lane_pick(x, h):
    lane = lax.broadcasted_iota(jnp.int32, x.shape, 1)
    return jnp.sum(jnp.where(lane == h * HEAD_DIM, x, 0.0), axis=1, keepdims=True)


def _block_rows(idx, d):
    span = BLOCK * d
    q0 = (idx // d) * span + idx % d
    return pl.ds(q0, BLOCK, stride=d), pl.ds(q0 - span, BLOCK, stride=d)


def _branch_loops(n_blocks, d, visit, unroll, masks):
    first_valid, cur_part, prev_part = masks
    if d % unroll == 0 and (n_blocks - d) % unroll == 0:
        full_valid = jnp.logical_or(cur_part, prev_part)

        def first(idx, carry):
            rows = pl.ds(idx, BLOCK, stride=d)
            visit(rows, [rows], first_valid)
            return carry

        def rest(idx, carry):
            rows, prev = _block_rows(idx, d)
            visit(rows, [prev, rows], full_valid)
            return carry

        lax.fori_loop(0, d, first, 0, unroll=unroll)
        lax.fori_loop(d, n_blocks, rest, 0, unroll=unroll)
        return

    def every(idx, carry):
        span = BLOCK * d
        q0 = (idx // d) * span + idx % d
        has_prev = idx >= d
        rows = pl.ds(q0, BLOCK, stride=d)
        prev = pl.ds(jnp.where(has_prev, q0 - span, q0), BLOCK, stride=d)
        visit(rows, [prev, rows], jnp.logical_or(cur_part, jnp.logical_and(prev_part, has_prev)))
        return carry

    lax.fori_loop(0, n_blocks, every, 0, unroll=unroll)


def _qkv_specs(s, tiles):
    q, k, v = [pl.BlockSpec((s, LANES), functools.partial(lambda hb, off: (0, off + hb), off=i * tiles))
               for i in range(3)]
    return q, k, v, pl.BlockSpec((s, LANES), lambda hb: (0, hb))


def _attn_seq_fwd(proj, width, name, comm=None):
    s = proj.shape[0]
    q_spec, k_spec, v_spec, cur = _qkv_specs(s, width // LANES)

    def body(q_ref, k_ref, v_ref, o_ref, l_ref, o_s, l_s):
        masks = _query_masks()
        for bi, d in enumerate(DILATIONS):
            def visit(rows, key_rows, valid, bi=bi):
                q2 = _stack_heads(q_ref[rows, :])
                keys = jnp.concatenate([k_ref[r, :] for r in key_rows], axis=0)
                vals = jnp.concatenate([v_ref[r, :] for r in key_rows], axis=0)
                sc = jnp.where(valid, _dot_nt(q2, keys), NEG)
                mx = jnp.max(sc, axis=1, keepdims=True)
                p = jnp.exp(sc - mx)
                den = jnp.sum(p, axis=1, keepdims=True)
                o_s[bi, rows, :] = _unstack_heads(_dot_nn(p, vals) / den)
                l_s[bi, rows, :] = _unstack_heads(jnp.broadcast_to(mx + jnp.log(den), (q2.shape[0], LANES)))

            _branch_loops(s // BLOCK, d, visit, 8, masks)
        for c in range(s // MERGE_CHUNK):
            rows = slice(c * MERGE_CHUNK, (c + 1) * MERGE_CHUNK)
            ls = [l_s[bi, rows, :] for bi in range(len(DILATIONS))]
            top = functools.reduce(jnp.maximum, ls)
            ws = [jnp.exp(l - top) for l in ls]
            den = functools.reduce(lambda a, b: a + b, ws)
            num = functools.reduce(lambda a, b: a + b, [w * o_s[bi, rows, :] for bi, w in enumerate(ws)])
            o_ref[rows, :] = num / den
            l_ref[rows, :] = top + jnp.log(den)

    return _call(
        body, grid=(width // LANES,), in_specs=[q_spec, k_spec, v_spec], out_specs=[cur, cur],
        out_shape=[jax.ShapeDtypeStruct((s, width), F32)] * 2,
        scratch_shapes=[pltpu.VMEM((len(DILATIONS), s, LANES), F32)] * 2,
        args=(proj, proj, proj), name=name, comm=comm)


def _attn_seq_bwd(proj, do, o, lse, cos, sin_signed, name, comm=None):
    s, width = do.shape
    q_spec, k_spec, v_spec, cur = _qkv_specs(s, width // LANES)
    table = pl.BlockSpec((s, LANES), lambda hb: (0, 0))
    qscale = HEAD_DIM ** -0.5

    def body(q_ref, k_ref, v_ref, do_ref, o_ref, l_ref, cos_ref, sin_ref, dq_out, dk_out, dv_out,
             dq_ref, dk_ref, dv_ref):
        dq_ref[...] = jnp.zeros_like(dq_ref)
        dk_ref[...] = jnp.zeros_like(dk_ref)
        dv_ref[...] = jnp.zeros_like(dv_ref)
        masks = _query_masks()
        for d in DILATIONS:
            def visit(rows, key_rows, valid):
                dob = do_ref[rows, :]
                q2 = _stack_heads(q_ref[rows, :])
                do2 = _stack_heads(dob)
                delta = jnp.sum(_stack_heads(dob * o_ref[rows, :]), axis=1, keepdims=True)
                lse2 = _stacked_lse(l_ref[rows, :])
                keys = jnp.concatenate([k_ref[r, :] for r in key_rows], axis=0)
                vals = jnp.concatenate([v_ref[r, :] for r in key_rows], axis=0)
                p = jnp.where(valid, jnp.exp(_dot_nt(q2, keys) - lse2), 0.0)
                ds = p * (_dot_nt(do2, vals) - delta)
                dq_ref[rows, :] += _unstack_heads(_dot_nn(ds, keys))
                dkk = _dot_tn(ds, q2)
                dvv = _dot_tn(p, do2)
                for i, r in enumerate(key_rows):
                    dk_ref[r, :] += dkk[i * BLOCK:(i + 1) * BLOCK]
                    dv_ref[r, :] += dvv[i * BLOCK:(i + 1) * BLOCK]

            _branch_loops(s // BLOCK, d, visit, 8, masks)
        for c in range(s // MERGE_CHUNK):
            rows = slice(c * MERGE_CHUNK, (c + 1) * MERGE_CHUNK)
            cos, sin = cos_ref[rows, :], sin_ref[rows, :]
            dq, dk = dq_ref[rows, :], dk_ref[rows, :]
            dq_out[rows, :] = ((dq * cos - _partner(dq) * sin) * qscale).astype(BF16)
            dk_out[rows, :] = (dk * cos - _partner(dk) * sin).astype(BF16)
            dv_out[rows, :] = dv_ref[rows, :].astype(BF16)

    return _call(
        body, grid=(width // LANES,), in_specs=[q_spec, k_spec, v_spec, cur, cur, cur, table, table],
        out_specs=[cur, cur, cur], out_shape=[jax.ShapeDtypeStruct((s, width), BF16)] * 3,
        scratch_shapes=[pltpu.VMEM((s, LANES), F32)] * 3,
        args=(proj, proj, proj, do, o, lse, cos, sin_signed), name=name, comm=comm)


def _conv_specs(s, a_block, b_block):
    per = CONV_CHUNK // CONV_HALO
    a_cur = pl.BlockSpec((CONV_CHUNK, LANES), lambda cb, i: (i, a_block + cb))
    b_cur = pl.BlockSpec((CONV_CHUNK, LANES), lambda cb, i: (i, b_block + cb))
    a_halo = pl.BlockSpec((CONV_HALO, LANES), lambda cb, i: (jnp.maximum(i * per - 1, 0), a_block + cb))
    b_halo = pl.BlockSpec((CONV_HALO, LANES), lambda cb, i: (jnp.maximum(i * per - 1, 0), b_block + cb))
    w_spec = pl.BlockSpec((CONV_KERNEL, LANES), lambda cb, i: (0, cb))
    vec = pl.BlockSpec((1, LANES), lambda cb, i: (0, cb))
    out = pl.BlockSpec((CONV_CHUNK, LANES), lambda cb, i: (i, cb))
    return a_cur, b_cur, a_halo, b_halo, w_spec, vec, out


def _fill_glu_window(win, a_ref, b_ref, ah_ref, bh_ref, first):
    halo = ah_ref[...] * _sigmoid(bh_ref[...])
    win[0:CONV_HALO, :] = jnp.where(first, 0.0, halo)
    win[CONV_HALO:, :] = a_ref[...] * _sigmoid(b_ref[...])


def _conv_fwd(proj, a_block, b_block, w, bias, name, comm=None):
    s = proj.shape[0]
    cw = w.shape[1]
    a_cur, b_cur, a_halo, b_halo, w_spec, vec, out = _conv_specs(s, a_block, b_block)
    lead = CONV_HALO - (CONV_KERNEL - 1)

    def body(a_ref, b_ref, ah_ref, bh_ref, w_ref, bias_ref, o_ref, win):
        _fill_glu_window(win, a_ref, b_ref, ah_ref, bh_ref, pl.program_id(1) == 0)
        for sub in range(CONV_CHUNK // CONV_SUB):
            base = sub * CONV_SUB
            acc = jnp.zeros((CONV_SUB, LANES), F32) + bias_ref[...]
            for j in range(CONV_KERNEL):
                acc = acc + w_ref[j:j + 1, :] * win[base + lead + j:base + lead + j + CONV_SUB, :]
            o_ref[base:base + CONV_SUB, :] = acc

    return _call(
        body, grid=(cw // LANES, s // CONV_CHUNK), in_specs=[a_cur, b_cur, a_halo, b_halo, w_spec, vec],
        out_specs=[out], out_shape=[jax.ShapeDtypeStruct((s, cw), F32)],
        scratch_shapes=[pltpu.VMEM((CONV_CHUNK + CONV_HALO, LANES), F32)],
        args=(proj, proj, proj, proj, w, bias), name=name, comm=comm)


def _conv_bwd(proj, a_block, b_block, w, du1, name):
    s = proj.shape[0]
    cw = w.shape[1]
    a_cur, b_cur, a_halo, b_halo, w_spec, vec, out = _conv_specs(s, a_block, b_block)
    per = CONV_CHUNK // CONV_HALO
    n_chunks = s // CONV_CHUNK
    d_next = pl.BlockSpec((CONV_HALO, LANES), lambda cb, i: (jnp.minimum((i + 1) * per, s // CONV_HALO - 1), cb))
    lead = CONV_HALO - (CONV_KERNEL - 1)

    def body(a_ref, b_ref, ah_ref, bh_ref, w_ref, d_ref, dn_ref, da_ref, db_ref, dw_ref, dbias_ref, win, dwin):
        i = pl.program_id(1)
        _fill_glu_window(win, a_ref, b_ref, ah_ref, bh_ref, i == 0)
        dwin[0:CONV_CHUNK, :] = d_ref[...]
        dwin[CONV_CHUNK:, :] = jnp.where(i == n_chunks - 1, 0.0, dn_ref[...])

        @pl.when(i == 0)
        def _():
            dw_ref[...] = jnp.zeros_like(dw_ref)
            dbias_ref[...] = jnp.zeros_like(dbias_ref)

        dbias_ref[...] += _colsum(d_ref[...])
        for sub in range(CONV_CHUNK // CONV_SUB):
            base = sub * CONV_SUB
            dcur = dwin[base:base + CONV_SUB, :]
            du0 = jnp.zeros((CONV_SUB, LANES), F32)
            for j in range(CONV_KERNEL):
                back = CONV_KERNEL - 1 - j
                du0 = du0 + w_ref[j:j + 1, :] * dwin[base + back:base + back + CONV_SUB, :]
                dw_ref[j:j + 1, :] += _colsum(dcur * win[base + lead + j:base + lead + j + CONV_SUB, :])
            av = a_ref[base:base + CONV_SUB, :]
            sig = _sigmoid(b_ref[base:base + CONV_SUB, :])
            da_ref[base:base + CONV_SUB, :] = (du0 * sig).astype(BF16)
            db_ref[base:base + CONV_SUB, :] = (du0 * av * sig * (1.0 - sig)).astype(BF16)

    return pl.pallas_call(
        body, grid=(cw // LANES, n_chunks), in_specs=[a_cur, b_cur, a_halo, b_halo, w_spec, out, d_next],
        out_specs=[out, out, w_spec, vec],
        out_shape=[jax.ShapeDtypeStruct((s, cw), BF16), jax.ShapeDtypeStruct((s, cw), BF16),
                   jax.ShapeDtypeStruct((CONV_KERNEL, cw), F32), jax.ShapeDtypeStruct((1, cw), F32)],
        scratch_shapes=[pltpu.VMEM((CONV_CHUNK + CONV_HALO, LANES), F32)] * 2,
        compiler_params=_params(2), name=name)(proj, proj, proj, proj, w, du1, du1)


def _adamw_math(w, g, m, v):
    m = ADAM_B1 * m + (1.0 - ADAM_B1) * g
    v = ADAM_B2 * v + (1.0 - ADAM_B2) * (g * g)
    m_hat = m / (1.0 - ADAM_B1 ** ADAM_STEP)
    v_hat = v / (1.0 - ADAM_B2 ** ADAM_STEP)
    delta = -ADAM_LR * (m_hat / (jnp.sqrt(v_hat) + ADAM_EPS) + ADAM_WD * w)
    return delta, m, v


def _adamw_big(w, g, m, v, name):
    rows, cols = w.shape
    tile = _tile(rows, 256, 8)
    spec = pl.BlockSpec((tile, cols), lambda i: (i, 0))

    def body(w_ref, g_ref, m_ref, v_ref, d_out, m_out, v_out):
        d_out[...], m_out[...], v_out[...] = _adamw_math(w_ref[...], g_ref[...], m_ref[...], v_ref[...])

    return pl.pallas_call(body, grid=(rows // tile,), in_specs=[spec] * 4, out_specs=[spec] * 3,
                          out_shape=[jax.ShapeDtypeStruct(w.shape, F32)] * 3, compiler_params=_params(1),
                          name=name)(w, g, m, v)


def _adamw_reduced(w, land, m, v, name):
    rows, cols = w.shape
    tile = _tile(rows, 256, 16)
    spec = pl.BlockSpec((tile, cols), lambda i: (i, 0))

    def body(w_ref, l_ref, m_ref, v_ref, g_out, d_out, m_out, v_out):
        g = l_ref[0].astype(F32)
        for q in range(1, N_CHIP):
            g = g + l_ref[q].astype(F32)
        g_out[...] = g
        d_out[...], m_out[...], v_out[...] = _adamw_math(w_ref[...], g, m_ref[...], v_ref[...])

    return pl.pallas_call(body, grid=(rows // tile,),
                          in_specs=[spec, pl.BlockSpec((N_CHIP, tile, cols), lambda i: (0, i, 0)), spec, spec],
                          out_specs=[spec] * 4, out_shape=[jax.ShapeDtypeStruct(w.shape, F32)] * 4,
                          compiler_params=_params(1), name=name)(w, land, m, v)


def _adamw_small(ws, gs, ms, vs, name):
    n = len(ws)

    def body(*refs):
        ins, outs = refs[:4 * n], refs[4 * n:]
        for t in range(n):
            res = _adamw_math(ins[t][...], ins[n + t][...], ins[2 * n + t][...], ins[3 * n + t][...])
            for j in range(3):
                outs[j * n + t][...] = res[j]

    shapes = [jax.ShapeDtypeStruct(w.shape, F32) for w in ws]
    res = pl.pallas_call(body, out_shape=shapes * 3, compiler_params=pltpu.CompilerParams(vmem_limit_bytes=VMEM_LIMIT),
                         name=name)(*ws, *gs, *ms, *vs)
    return res[:n], res[n:2 * n], res[2 * n:]


def _sum_blocks(x, n_blocks, name):
    r = x.shape[0] // n_blocks

    def body(x_ref, o_ref):
        acc = x_ref[0:r, :]
        for b in range(1, n_blocks):
            acc = acc + x_ref[b * r:(b + 1) * r, :]
        o_ref[...] = acc

    return pl.pallas_call(body, out_shape=jax.ShapeDtypeStruct((r, x.shape[1]), F32),
                          compiler_params=pltpu.CompilerParams(vmem_limit_bytes=VMEM_LIMIT), name=name)(x)


def _coords():
    return lax.axis_index("x"), lax.axis_index("y"), lax.axis_index("c")


def _flip(v, bit):
    return 1 - v if bit else v


def _ag_small(x, name):
    r, c = x.shape

    def body(x_ref, o_ref, send, recv, local_sem):
        mx, my, mc = _coords()

        def rows(px, py, pc):
            return o_ref.at[pl.ds(pl.multiple_of((4 * px + 2 * py + pc) * r, 8), r), :]

        local = pltpu.make_async_copy(x_ref, rows(mx, my, mc), local_sem)
        local.start()
        peers = [(_flip(mx, k >> 2 & 1), _flip(my, k >> 1 & 1), _flip(mc, k & 1)) for k in range(1, N_DEV)]
        sends = [pltpu.make_async_remote_copy(x_ref, rows(mx, my, mc), send.at[k], recv.at[k], device_id=p,
                                              device_id_type=MESH) for k, p in enumerate(peers)]
        for cp in sends:
            cp.start()
        for k, p in enumerate(peers):
            pltpu.make_async_remote_copy(x_ref, rows(*p), send.at[k], recv.at[k], device_id=p,
                                         device_id_type=MESH).wait_recv()
        for cp in sends:
            cp.wait_send()
        local.wait()

    vm = pl.BlockSpec(memory_space=pltpu.VMEM)
    return pl.pallas_call(
        body, in_specs=[vm], out_specs=vm, out_shape=jax.ShapeDtypeStruct((N_DEV * r, c), x.dtype),
        scratch_shapes=[pltpu.SemaphoreType.DMA((N_DEV - 1,)), pltpu.SemaphoreType.DMA((N_DEV - 1,)),
                        pltpu.SemaphoreType.DMA(())],
        name=name)(x)


class _GatherSmall:
    mid = None

    def __init__(self, x):
        self.inputs = [x]
        self.out_shapes = [jax.ShapeDtypeStruct((N_DEV * x.shape[0], x.shape[1]), x.dtype)]
        self.scratch = [pltpu.SemaphoreType.DMA((N_DEV - 1,)), pltpu.SemaphoreType.DMA((N_DEV - 1,)),
                        pltpu.SemaphoreType.DMA(())]

    def _plan(self, x_refs, o_refs, sems):
        send, recv, local_sem = sems
        x_ref, o_ref = x_refs[0], o_refs[0]
        r = x_ref.shape[0]
        mx, my, mc = _coords()

        def rows(px, py, pc):
            return o_ref.at[pl.ds(pl.multiple_of((4 * px + 2 * py + pc) * r, 8), r), :]

        peers = [(_flip(mx, k >> 2 & 1), _flip(my, k >> 1 & 1), _flip(mc, k & 1)) for k in range(1, N_DEV)]
        out = [pltpu.make_async_remote_copy(x_ref, rows(mx, my, mc), send.at[k], recv.at[k], device_id=p,
                                            device_id_type=MESH) for k, p in enumerate(peers)]
        arrivals = [pltpu.make_async_remote_copy(x_ref, rows(*p), send.at[k], recv.at[k], device_id=p,
                                                 device_id_type=MESH) for k, p in enumerate(peers)]
        return out, arrivals, pltpu.make_async_copy(x_ref, rows(mx, my, mc), local_sem)

    def start(self, x_refs, o_refs, sems):
        out, _, local = self._plan(x_refs, o_refs, sems)
        local.start()
        for cp in out:
            cp.start()

    def finish(self, x_refs, o_refs, sems):
        out, arrivals, local = self._plan(x_refs, o_refs, sems)
        for cp in arrivals:
            cp.wait_recv()
        for cp in out:
            cp.wait_send()
        local.wait()


class _GatherWeights:
    def __init__(self, shards):
        n_t = len(shards)
        self.inputs = list(shards)
        self.out_shapes = [jax.ShapeDtypeStruct((N_DEV * x.shape[0], x.shape[1]), x.dtype) for x in shards]
        self.scratch = [pltpu.SemaphoreType.DMA((n_t, 8)), pltpu.SemaphoreType.DMA((n_t, 8)),
                        pltpu.SemaphoreType.DMA((n_t,))]

    def _plan(self, x_refs, o_refs, sems):
        send, recv, local_sem = sems
        mx, my, mc = _coords()
        me, sibling = (mx, my, mc), (mx, my, 1 - mc)
        xn, yn, diag = (1 - mx, my), (mx, 1 - my), (1 - mx, 1 - my)

        def rows(t, chip, core, half=None):
            r = x_refs[t].shape[0]
            base = (4 * chip[0] + 2 * chip[1] + core) * r
            if half is None:
                return o_refs[t].at[pl.ds(pl.multiple_of(base, 8), r), :]
            return o_refs[t].at[pl.ds(pl.multiple_of(base + half * (r // 2), 8), r // 2), :]

        def copy(t, k, block, to, src=None):
            return pltpu.make_async_remote_copy(
                src_ref=block if src is None else src, dst_ref=block,
                send_sem=send.at[t, k], recv_sem=recv.at[t, k], device_id=to, device_id_type=MESH)

        def local(t):
            return pltpu.make_async_copy(x_refs[t], rows(t, (mx, my), mc), local_sem.at[t])

        return (mx, my), mc, me, sibling, xn, yn, diag, rows, copy, local

    def start(self, x_refs, o_refs, sems):
        chip, mc, me, sibling, xn, yn, diag, rows, copy, local = self._plan(x_refs, o_refs, sems)
        for t in range(len(x_refs)):
            mine = rows(t, chip, mc)
            local(t).start()
            copy(t, 0, mine, sibling, src=x_refs[t]).start()
            copy(t, 1, mine, (*xn, mc), src=x_refs[t]).start()
            copy(t, 2, mine, (*yn, mc), src=x_refs[t]).start()

    def mid(self, x_refs, o_refs, sems):
        chip, mc, me, sibling, xn, yn, diag, rows, copy, local = self._plan(x_refs, o_refs, sems)
        for t in range(len(x_refs)):
            copy(t, 1, rows(t, xn, mc), me).wait_recv()
            copy(t, 3, rows(t, xn, mc, 0), (*yn, mc)).start()
            copy(t, 5, rows(t, xn, mc), sibling).start()
        for t in range(len(x_refs)):
            copy(t, 2, rows(t, yn, mc), me).wait_recv()
            copy(t, 4, rows(t, yn, mc, 1), (*xn, mc)).start()
            copy(t, 6, rows(t, yn, mc), sibling).start()

    def finish(self, x_refs, o_refs, sems):
        chip, mc, me, sibling, xn, yn, diag, rows, copy, local = self._plan(x_refs, o_refs, sems)
        for t in range(len(x_refs)):
            copy(t, 3, rows(t, diag, mc, 0), me).wait_recv()
            copy(t, 4, rows(t, diag, mc, 1), me).wait_recv()
            copy(t, 7, rows(t, diag, mc), sibling).start()
        for t in range(len(x_refs)):
            copy(t, 0, rows(t, chip, 1 - mc), me).wait_recv()
            copy(t, 5, rows(t, xn, 1 - mc), me).wait_recv()
            copy(t, 6, rows(t, yn, 1 - mc), me).wait_recv()
            copy(t, 7, rows(t, diag, 1 - mc), me).wait_recv()
            mine = rows(t, chip, mc)
            copy(t, 0, mine, sibling, src=x_refs[t]).wait_send()
            copy(t, 1, mine, (*xn, mc), src=x_refs[t]).wait_send()
            copy(t, 2, mine, (*yn, mc), src=x_refs[t]).wait_send()
            copy(t, 3, rows(t, xn, mc, 0), (*yn, mc)).wait_send()
            copy(t, 4, rows(t, yn, mc, 1), (*xn, mc)).wait_send()
            copy(t, 5, rows(t, xn, mc), sibling).wait_send()
            copy(t, 6, rows(t, yn, mc), sibling).wait_send()
            copy(t, 7, rows(t, diag, mc), sibling).wait_send()
            local(t).wait()


class _SiblingExchange:
    mid = None

    def __init__(self, grads):
        n_t = len(grads)
        self.inputs = list(grads)
        self.out_shapes = [jax.ShapeDtypeStruct((N_CHIP,) + g.shape[2:], F32) for g in grads]
        self.scratch = [pltpu.SemaphoreType.DMA((n_t,)), pltpu.SemaphoreType.DMA((n_t,))]

    def _copies(self, g_refs, land, sems):
        send, recv = sems
        mx, my, mc = _coords()
        return [pltpu.make_async_remote_copy(g_refs[t].at[:, 1 - mc], land[t], send.at[t], recv.at[t],
                                             device_id=(mx, my, 1 - mc), device_id_type=MESH)
                for t in range(len(g_refs))]

    def start(self, g_refs, land, sems):
        for cp in self._copies(g_refs, land, sems):
            cp.start()

    def finish(self, g_refs, land, sems):
        for cp in self._copies(g_refs, land, sems):
            cp.wait()


class _Together:
    def __init__(self, *comms):
        self.comms = comms
        self.inputs = [x for c in comms for x in c.inputs]
        self.out_shapes = [x for c in comms for x in c.out_shapes]
        self.scratch = [x for c in comms for x in c.scratch]
        self.mid = self._mid if any(c.mid is not None for c in comms) else None

    def _each(self, phase, cin, cout, sems):
        i = o = s = 0
        for c in self.comms:
            fn = getattr(c, phase)
            ni, no, ns = len(c.inputs), len(c.out_shapes), len(c.scratch)
            if fn is not None:
                fn(cin[i:i + ni], cout[o:o + no], sems[s:s + ns])
            i, o, s = i + ni, o + no, s + ns

    def start(self, cin, cout, sems):
        self._each("start", cin, cout, sems)

    def _mid(self, cin, cout, sems):
        self._each("mid", cin, cout, sems)

    def finish(self, cin, cout, sems):
        self._each("finish", cin, cout, sems)


def _standalone(comm, name):
    def body():
        pass
    return _call(body, grid=(1,), in_specs=[], out_specs=[], out_shape=[], args=(), name=name, comm=comm)[1]


def _chip_partials(g4s, lands, name):
    n_t = len(g4s)
    in_specs, out_specs, out_shape = [], [], []
    for g4 in g4s:
        _, _, r, c = g4.shape
        in_specs.append(pl.BlockSpec((None, None, r, c), lambda q: (q, lax.axis_index("c"), 0, 0)))
        out_specs.append(pl.BlockSpec((None, r, c), lambda q: (q, 0, 0)))
        out_shape.append(jax.ShapeDtypeStruct((N_CHIP, r, c), BF16))
    in_specs += [pl.BlockSpec((None,) + g4.shape[2:], lambda q: (q, 0, 0)) for g4 in g4s]

    def body(*refs):
        for t in range(n_t):
            refs[2 * n_t + t][...] = (refs[t][...] + refs[n_t + t][...]).astype(BF16)

    return pl.pallas_call(body, grid=(N_CHIP,), in_specs=in_specs, out_specs=out_specs, out_shape=out_shape,
                          compiler_params=_params(1), name=name)(*g4s, *lands)


class _ChipExchange:
    mid = None

    def __init__(self, parts):
        n_t = len(parts)
        self.inputs = list(parts)
        self.out_shapes = [jax.ShapeDtypeStruct(p.shape, p.dtype) for p in parts]
        self.scratch = [pltpu.SemaphoreType.DMA((n_t, 3)), pltpu.SemaphoreType.DMA((n_t, 3)),
                        pltpu.SemaphoreType.DMA((n_t,))]

    def _plan(self, p_refs, land, sems):
        send, recv, local_sem = sems
        mx, my, mc = _coords()
        my_chip = 2 * mx + my
        peers = [(_flip(mx, fx), _flip(my, fy)) for fx, fy in ((1, 0), (0, 1), (1, 1))]

        def out(t, k):
            px, py = peers[k]
            return pltpu.make_async_remote_copy(p_refs[t].at[2 * px + py], land[t].at[my_chip], send.at[t, k],
                                                recv.at[t, k], device_id=(px, py, mc), device_id_type=MESH)

        def arrival(t, k):
            px, py = peers[k]
            return pltpu.make_async_remote_copy(p_refs[t].at[my_chip], land[t].at[2 * px + py], send.at[t, k],
                                                recv.at[t, k], device_id=(px, py, mc), device_id_type=MESH)

        def local(t):
            return pltpu.make_async_copy(p_refs[t].at[my_chip], land[t].at[my_chip], local_sem.at[t])

        return out, arrival, local

    def start(self, p_refs, land, sems):
        out, arrival, local = self._plan(p_refs, land, sems)
        for t in range(len(p_refs)):
            local(t).start()
            for k in range(3):
                out(t, k).start()

    def finish(self, p_refs, land, sems):
        out, arrival, local = self._plan(p_refs, land, sems)
        for t in range(len(p_refs)):
            for k in range(3):
                arrival(t, k).wait_recv()
                out(t, k).wait_send()
            local(t).wait()


def _rope_tables(s, width):
    heads = width // HEAD_DIM
    inv_freq = ROPE_THETA ** (-jnp.arange(0, HEAD_DIM, 2, dtype=F32) / HEAD_DIM)
    inv_full = jnp.tile(inv_freq, 2 * heads)
    sign = jnp.tile(jnp.concatenate([-jnp.ones((HALF_HEAD,), F32), jnp.ones((HALF_HEAD,), F32)]), heads)
    ang = jnp.arange(s, dtype=F32)[:, None] * inv_full[None, :]
    return jnp.cos(ang), jnp.sin(ang) * sign[None, :]


def _pad_rows(v, rows):
    return jnp.concatenate([v, jnp.zeros((rows - 1, v.shape[1]), v.dtype)], axis=0)


def kernel(x, c, w_ada, b_ada, ffn1_norm_g, ffn1_w_gate, ffn1_w_up, ffn1_w_down, mix_norm_g, w_in, conv_dw_w, conv_dw_b, conv_ln_g, conv_ln_b, attn_out_g, conv_out_g, w_out, ffn2_norm_g, ffn2_w_gate, ffn2_w_up, ffn2_w_down, final_norm_g, loss_target, m_w_ada, m_b_ada, m_ffn1_norm_g, m_ffn1_w_gate, m_ffn1_w_up, m_ffn1_w_down, m_mix_norm_g, m_w_in, m_conv_dw_w, m_conv_dw_b, m_conv_ln_g, m_conv_ln_b, m_attn_out_g, m_conv_out_g, m_w_out, m_ffn2_norm_g, m_ffn2_w_gate, m_ffn2_w_up, m_ffn2_w_down, m_final_norm_g, v_w_ada, v_b_ada, v_ffn1_norm_g, v_ffn1_w_gate, v_ffn1_w_up, v_ffn1_w_down, v_mix_norm_g, v_w_in, v_conv_dw_w, v_conv_dw_b, v_conv_ln_g, v_conv_ln_b, v_attn_out_g, v_conv_out_g, v_w_out, v_ffn2_norm_g, v_ffn2_w_gate, v_ffn2_w_up, v_ffn2_w_down, v_final_norm_g):
    mx, my, mc = _coords()
    me = 4 * mx + 2 * my + mc
    s, d = x.shape[1], x.shape[2]
    aw = d // 2
    x2, target = x[0], loss_target[0]
    n_mod = w_ada.shape[2] * N_DEV // d
    mod_cols = w_ada.shape[2]

    def shard(w, transpose):
        return (w[0].T if transpose else w[0]).astype(BF16)

    cw_shard = conv_dw_w.shape[3]
    n_taps = CONV_KERNEL * cw_shard
    first_len = -(-(d + n_taps) // LANES) * LANES
    first = jnp.concatenate([c, conv_dw_w[0, :, 0, :].reshape(1, n_taps), jnp.zeros((1, first_len - d - n_taps), F32)], axis=1)
    first_all, wg1 = _standalone(
        _Together(_GatherSmall(_pad_rows(first, 8)), _GatherWeights([shard(ffn1_w_gate, True)])), "ag_first")
    first_all = first_all[0::8]
    c_all = first_all[:, :d]
    conv_w = first_all[:, d:d + n_taps].reshape(N_DEV, CONV_KERNEL, cw_shard).transpose(1, 0, 2).reshape(CONV_KERNEL, aw)

    silu_c = _silu_rows(c_all, "silu_c")
    mod_part = _plain_mm([(silu_c, w_ada[0])], F32, False, mod_cols, "mod_mm")
    mod_all = _ag_small(mod_part, "ag_mod").reshape(N_DEV, N_DEV, mod_cols)
    mod = lax.dynamic_index_in_dim(mod_all, me, axis=1, keepdims=False).reshape(1, n_mod * d) + b_ada
    sh1, sc1, g1, sh2, sc2, g2, sh3, sc3, g3 = [mod[:, i * d:(i + 1) * d] for i in range(n_mod)]

    def split(g):
        return g.reshape(N_CHIP, 2, g.shape[0] // N_DEV, g.shape[1])

    def partials(g4s, lands, tag):
        return _chip_partials(g4s, lands, "chip_partials_" + tag)

    (n1, a1), (wu1,) = _norm_gate(x2, ffn1_norm_g, sc1, sh1, wg1, "ffn1_gate",
                                  comm=_GatherWeights([shard(ffn1_w_up, True)]))
    (silu1, gs1, hid1), (wd1,) = _ffn_up_given_gate(n1, wu1, a1, "ffn1_up",
                                                    comm=_GatherWeights([shard(ffn1_w_down, False)]))
    (h1, f1, n2), (win_t,) = _residual_mm(hid1, wd1, x2, g1, 0.5, "ffn1_down", norm=(mix_norm_g, sc2, sh2),
                                          comm=_GatherWeights([shard(w_in, True)]))
    cos, sin_signed = _rope_tables(s, LANES)
    (proj,), (wd2,) = _proj_rope(n2, win_t, cos, sin_signed, aw, "proj",
                                 comm=_GatherWeights([shard(ffn2_w_down, False)]))
    lanes_per = aw // LANES
    (attn, lse), (wg2, wu2) = _attn_seq_fwd(
        proj, aw, "attn_fwd", comm=_GatherWeights([shard(ffn2_w_gate, True), shard(ffn2_w_up, True)]))
    (u1,), (wout,) = _conv_fwd(proj, 3 * lanes_per, 4 * lanes_per, conv_w, conv_dw_b, "conv_fwd",
                               comm=_GatherWeights([shard(w_out, False)]))
    post = (attn_out_g, conv_ln_g, conv_ln_b, conv_out_g)
    y, h2, mix, n3 = _mix_out(attn, u1, post, wout, h1, g2, (ffn2_norm_g, sc3, sh3), "mix_out")
    silu3, gs3, hid3 = _ffn_up(n3, wg2, wu2, "ffn2_up")

    dh3, df3, err2, d_final_g, dg3 = _last_mm_loss(hid3, wd2, h2, g3, 0.5, target, final_norm_g.reshape(1, d),
                                                   "ffn2_down_loss")
    loss_part = jnp.zeros((1, LANES), F32).at[0, 0].set(0.5 * jnp.sum(err2) / d)

    da3, db3 = _ffn_bwd_hidden(df3, wd2, silu3, gs3, "ffn2_hidden_bwd")
    g4_a = [split(_mm_tn(da3, n3, "ffn2_dwg")), split(_mm_tn(db3, n3, "ffn2_dwu")), split(_mm_tn(hid3, df3, "ffn2_dwd"))]
    (dh2, dmix, dsh3, dsc3, dgn3, dg2), land_a = _mm_norm_mod_bwd(
        [(da3, wg2), (db3, wu2)], h2, dh3, ffn2_norm_g, sc3, (mix, g2, 1.0), "ffn2_dn_norm3_bwd", tm=256,
        comm=_SiblingExchange(g4_a))
    parts_a = partials(g4_a, land_a, "a")
    g_wout = _mm_tn(y, dmix, "mix_dwout")
    dattn, du1, d_gains, d_ln = _mix_dy_post_bwd(dmix, wout, attn, u1, post, "mix_dy_post_bwd")
    d_attn_g, d_conv_g, d_ln_g, d_ln_b = d_gains[:, :aw], d_gains[:, aw:], d_ln[:, :aw], d_ln[:, aw:]
    dga, dgb, d_taps, d_conv_b = _conv_bwd(proj, 3 * lanes_per, 4 * lanes_per, conv_w, du1, "conv_bwd")
    (dq, dk, dv), sums_a = _attn_seq_bwd(proj, dattn, attn, lse, cos, sin_signed, "attn_bwd",
                                         comm=_ChipExchange(parts_a))
    dproj = jnp.concatenate([dq, dk, dv, dga, dgb], axis=1)
    g4_b = [split(g_wout), split(_mm_tn(dproj, n2, "mix_dwin"))]
    (dh1, df1, dsh2, dsc2, dgn2, dg1), land_b = _mm_norm_mod_bwd(
        [(dproj, win_t)], h1, dh2, mix_norm_g, sc2, (f1, g1, 0.5), "mix_dn_norm2_bwd", tm=512,
        comm=_SiblingExchange(g4_b))
    parts_b = partials(g4_b, land_b, "b")
    g4_c = [split(_mm_tn(hid1, df1, "ffn1_dwd"))]
    (da1, db1), both = _ffn_bwd_hidden(df1, wd1, silu1, gs1, "ffn1_hidden_bwd",
                                       comm=_Together(_ChipExchange(parts_b), _SiblingExchange(g4_c)))
    sums_b, land_c = both[:2], both[2:]
    parts_c = partials(g4_c, land_c, "c")
    g_wu1, sums_c = _mm_tn(db1, n1, "ffn1_dwu", comm=_ChipExchange(parts_c))
    g4_d = [split(g_wu1)]
    g_wg1, land_d = _mm_tn(da1, n1, "ffn1_dwg", comm=_SiblingExchange(g4_d))
    parts_d = partials(g4_d, land_d, "d")
    g4_e = [split(g_wg1)]
    dn1, both = _plain_mm([(da1, wg1), (db1, wu1)], BF16, False, d, "ffn1_dn",
                          comm=_Together(_ChipExchange(parts_d), _SiblingExchange(g4_e)))
    sums_d, land_e = both[:1], both[1:]
    parts_e = partials(g4_e, land_e, "e")
    (dx, dsh1, dsc1, dgn1), sums_e = _norm_mod_bwd(dn1, x2, dh1, ffn1_norm_g, sc1, "norm1_bwd",
                                                   comm=_ChipExchange(parts_e))

    dmod = jnp.concatenate([dsh1, dsc1, dg1, dsh2, dsc2, dg2, dsh3, dsc3, dg3], axis=1)
    small = [dmod, dgn1, dgn2, dgn3, d_final_g, d_conv_b, d_ln_g, d_ln_b, d_attn_g, d_conv_g,
             d_taps.reshape(1, CONV_KERNEL * aw), loss_part]
    sizes = [v.shape[1] for v in small]
    total = sum(sizes)
    padded = -(-total // (8 * LANES)) * (8 * LANES)
    packed = jnp.concatenate(small + [jnp.zeros((1, padded - total), F32)], axis=1).reshape(8, padded // 8)
    gathered = _ag_small(packed, "ag_small_grads")
    summed = _sum_blocks(gathered, N_DEV, "sum_small_grads").reshape(1, padded)
    offs = [sum(sizes[:i]) for i in range(len(sizes))]
    (g_b_ada, g_gn1, g_gn2, g_gn3, g_final, g_conv_b, g_ln_g, g_ln_b, g_attn_g, g_conv_g, g_taps, loss_row) = [
        summed[:, o:o + n] for o, n in zip(offs, sizes)]
    loss = loss_row[0, 0]
    g_taps_shard = lax.dynamic_slice_in_dim(g_taps.reshape(CONV_KERNEL, aw), me * cw_shard, cw_shard, axis=1)
    dmod_all = gathered.reshape(N_DEV, padded)[:, :n_mod * d]
    dmod_cols = lax.dynamic_slice_in_dim(dmod_all, me * mod_cols, mod_cols, axis=1)
    g_w_ada = _mm_tn(silu_c, dmod_cols, "ada_dw")

    arrived = dict(zip(["ffn2_w_gate", "ffn2_w_up", "ffn2_w_down", "w_out", "w_in", "ffn1_w_down", "ffn1_w_up",
                        "ffn1_w_gate"], list(sums_a) + list(sums_b) + list(sums_c) + list(sums_d) + list(sums_e)))
    transposed = ("ffn1_w_gate", "ffn1_w_up", "w_in", "ffn2_w_gate", "ffn2_w_up")
    grads = {
        "w_ada": g_w_ada, "b_ada": g_b_ada, "ffn1_norm_g": g_gn1, "mix_norm_g": g_gn2, "conv_dw_w": g_taps_shard,
        "conv_dw_b": g_conv_b, "conv_ln_g": g_ln_g, "conv_ln_b": g_ln_b, "attn_out_g": g_attn_g,
        "conv_out_g": g_conv_g, "ffn2_norm_g": g_gn3, "final_norm_g": g_final,
    }
    weights = dict(w_ada=w_ada, b_ada=b_ada, ffn1_norm_g=ffn1_norm_g, ffn1_w_gate=ffn1_w_gate, ffn1_w_up=ffn1_w_up, ffn1_w_down=ffn1_w_down, mix_norm_g=mix_norm_g, w_in=w_in, conv_dw_w=conv_dw_w, conv_dw_b=conv_dw_b, conv_ln_g=conv_ln_g, conv_ln_b=conv_ln_b, attn_out_g=attn_out_g, conv_out_g=conv_out_g, w_out=w_out, ffn2_norm_g=ffn2_norm_g, ffn2_w_gate=ffn2_w_gate, ffn2_w_up=ffn2_w_up, ffn2_w_down=ffn2_w_down, final_norm_g=final_norm_g)
    moms = dict(w_ada=m_w_ada, b_ada=m_b_ada, ffn1_norm_g=m_ffn1_norm_g, ffn1_w_gate=m_ffn1_w_gate, ffn1_w_up=m_ffn1_w_up, ffn1_w_down=m_ffn1_w_down, mix_norm_g=m_mix_norm_g, w_in=m_w_in, conv_dw_w=m_conv_dw_w, conv_dw_b=m_conv_dw_b, conv_ln_g=m_conv_ln_g, conv_ln_b=m_conv_ln_b, attn_out_g=m_attn_out_g, conv_out_g=m_conv_out_g, w_out=m_w_out, ffn2_norm_g=m_ffn2_norm_g, ffn2_w_gate=m_ffn2_w_gate, ffn2_w_up=m_ffn2_w_up, ffn2_w_down=m_ffn2_w_down, final_norm_g=m_final_norm_g)
    vars_ = dict(w_ada=v_w_ada, b_ada=v_b_ada, ffn1_norm_g=v_ffn1_norm_g, ffn1_w_gate=v_ffn1_w_gate, ffn1_w_up=v_ffn1_w_up, ffn1_w_down=v_ffn1_w_down, mix_norm_g=v_mix_norm_g, w_in=v_w_in, conv_dw_w=v_conv_dw_w, conv_dw_b=v_conv_dw_b, conv_ln_g=v_conv_ln_g, conv_ln_b=v_conv_ln_b, attn_out_g=v_attn_out_g, conv_out_g=v_conv_out_g, w_out=v_w_out, ffn2_norm_g=v_ffn2_norm_g, ffn2_w_gate=v_ffn2_w_gate, ffn2_w_up=v_ffn2_w_up, ffn2_w_down=v_ffn2_w_down, final_norm_g=v_final_norm_g)
    names = list(weights)
    big = ["w_ada", "ffn1_w_gate", "ffn1_w_up", "ffn1_w_down", "w_in", "w_out", "ffn2_w_gate", "ffn2_w_up",
           "ffn2_w_down"]
    shape2 = {n: (weights[n].shape[-2] if weights[n].ndim > 1 else 1, weights[n].shape[-1]) for n in names}
    shape2["conv_dw_w"] = (CONV_KERNEL, cw_shard)
    g_out, d_out, m_out, v_out = {}, {}, {}, {}
    for n in big:
        if n in arrived:
            def view(t, n=n):
                return t[0].T if n in transposed else t[0]
            res = _adamw_reduced(view(weights[n]), arrived[n], view(moms[n]), view(vars_[n]), "adamw_" + n)
            g_out[n], d_out[n], m_out[n], v_out[n] = [r.T if n in transposed else r for r in res]
        else:
            g2d = grads[n].reshape(shape2[n])
            res = _adamw_big(weights[n].reshape(shape2[n]), g2d, moms[n].reshape(shape2[n]),
                             vars_[n].reshape(shape2[n]), "adamw_" + n)
            g_out[n], (d_out[n], m_out[n], v_out[n]) = g2d, res
    rest = [n for n in names if n not in big]
    res = _adamw_small([weights[n].reshape(shape2[n]) for n in rest], [grads[n].reshape(shape2[n]) for n in rest],
                       [moms[n].reshape(shape2[n]) for n in rest], [vars_[n].reshape(shape2[n]) for n in rest],
                       "adamw_small")
    for i, n in enumerate(rest):
        g_out[n], d_out[n], m_out[n], v_out[n] = grads[n], res[0][i], res[1][i], res[2][i]

    def shaped(table):
        return [table[n].reshape(weights[n].shape) for n in names]

    return (loss, dx.reshape(x.shape), *shaped(g_out), *shaped(d_out), *shaped(m_out), *shaped(v_out))
```

```python
import functools

import jax
import jax.numpy as jnp
from jax import lax
from jax.experimental import pallas as pl
from jax.experimental.pallas import tpu as pltpu

F32 = jnp.float32
BF16 = jnp.bfloat16
MESH = pl.DeviceIdType.MESH
ANY = pl.BlockSpec(memory_space=pl.ANY)

N_DEV = 8
N_CHIP = 4
HEAD_DIM = 64
HALF_HEAD = HEAD_DIM // 2
LANES = 128
BLOCK = 128
DILATIONS = (1, 4, 16)
MERGE_CHUNK = 512
ROPE_THETA = 10000.0
CONV_KERNEL = 31
CONV_HALO = 32
CONV_CHUNK = 512
CONV_SUB = 128
RMS_EPS = 1e-6
LN_EPS = 1e-5
ADAM_LR = 0.001
ADAM_B1 = 0.9
ADAM_B2 = 0.999
ADAM_EPS = 1e-08
ADAM_WD = 0.01
ADAM_STEP = 10
VMEM_LIMIT = 56 * 1024 * 1024
NEG = -1e30


def _params(n_axes):
    return pltpu.CompilerParams(dimension_semantics=("arbitrary",) * n_axes, vmem_limit_bytes=VMEM_LIMIT)


def _tile(n, target, unit):
    best = None
    for t in range(unit, min(n, target) + 1, unit):
        if n % t == 0:
            best = t
    return best if best is not None else n


def _sigmoid(x):
    return 0.5 * (jnp.tanh(0.5 * x) + 1.0)


def _call(body, *, grid, in_specs, out_specs, out_shape, args, name, scratch_shapes=(), comm=None):
    params = _params(len(grid))
    if comm is None:
        return pl.pallas_call(body, grid=grid, in_specs=list(in_specs), out_specs=list(out_specs),
                              out_shape=list(out_shape), scratch_shapes=list(scratch_shapes),
                              compiler_params=params, name=name)(*args)
    n_in, n_out, n_scr = len(args), len(out_shape), len(scratch_shapes)
    c_in, c_out = len(comm.inputs), len(comm.out_shapes)
    steps = 1
    for g in grid:
        steps *= g

    def hosted(*refs):
        pos = 0
        parts = []
        for size in (n_in, c_in, n_out, c_out, n_scr, len(comm.scratch)):
            parts.append(refs[pos:pos + size])
            pos += size
        ins, cin, outs, cout, scr, cscr = parts
        step = 0
        for axis, g in enumerate(grid):
            step = step * g + pl.program_id(axis)

        @pl.when(step == 0)
        def _():
            comm.start(cin, cout, cscr)

        body(*ins, *outs, *scr)
        if comm.mid is not None and steps >= 4:
            @pl.when(step == steps // 2)
            def _():
                comm.mid(cin, cout, cscr)

        @pl.when(step == steps - 1)
        def _():
            if comm.mid is not None and steps < 4:
                comm.mid(cin, cout, cscr)
            comm.finish(cin, cout, cscr)

    res = pl.pallas_call(
        hosted, grid=grid, in_specs=list(in_specs) + [ANY] * c_in, out_specs=list(out_specs) + [ANY] * c_out,
        out_shape=list(out_shape) + list(comm.out_shapes), scratch_shapes=list(scratch_shapes) + list(comm.scratch),
        compiler_params=params, name=name)(*args, *comm.inputs)
    return res[:n_out], res[n_out:]


def _rows(fn, rows_in, vecs_in, rows_out, vecs_out, *, tile, name, comm=None):
    norm = [r if isinstance(r, tuple) else (r, r.shape[1], 0) for r in rows_in]
    n_rows = norm[0][0].shape[0]
    n_tiles = n_rows // tile
    in_specs, args = [], []
    for arr, width, cb in norm:
        in_specs.append(pl.BlockSpec((tile, width), functools.partial(lambda i, cb: (i, cb), cb=cb)))
        args.append(arr)
    for v in vecs_in:
        in_specs.append(pl.BlockSpec((1, v.shape[1]), lambda i: (0, 0)))
        args.append(v)
    out_shape = [jax.ShapeDtypeStruct((n_rows, w), dt) for w, dt in rows_out]
    out_shape += [jax.ShapeDtypeStruct((1, w), F32) for w in vecs_out]
    out_specs = [pl.BlockSpec((tile, w), lambda i: (i, 0)) for w, _ in rows_out]
    out_specs += [pl.BlockSpec((1, w), lambda i: (0, 0)) for w in vecs_out]
    n_in, n_ro = len(args), len(rows_out)

    def body(*refs):
        vals = [r[...] for r in refs[:n_in]]
        outs = refs[n_in:]
        row_vals, vec_vals = fn(*vals)
        for ref, val in zip(outs[:n_ro], row_vals):
            if isinstance(val, tuple):
                w = val[0].shape[1]
                for j, piece in enumerate(val):
                    ref[:, j * w:(j + 1) * w] = piece.astype(ref.dtype)
            else:
                ref[...] = val.astype(ref.dtype)
        if vecs_out:
            @pl.when(pl.program_id(0) == 0)
            def _():
                for ref in outs[n_ro:]:
                    ref[...] = jnp.zeros_like(ref)
            for ref, val in zip(outs[n_ro:], vec_vals):
                ref[...] += val

    return _call(body, grid=(n_tiles,), in_specs=in_specs, out_specs=out_specs, out_shape=out_shape, args=args,
                 name=name, comm=comm)


def _colsum(x):
    return jnp.sum(x, axis=0, keepdims=True)


def _rms_stats(h):
    r = lax.rsqrt(jnp.mean(h * h, axis=-1, keepdims=True) + RMS_EPS)
    return r, h * r


def _rms_back(r, xn, dxn):
    return r * (dxn - xn * jnp.mean(dxn * xn, axis=-1, keepdims=True))


def _branch_back(dh, f, gate, coef):
    return (coef * gate) * dh, coef * _colsum(f.astype(F32) * dh)


def _norm_mod_back(dn, h, dh_in, gain, scale):
    dn = dn.astype(F32)
    r, xn = _rms_stats(h)
    y = xn * gain
    dy = dn * (1.0 + scale)
    dh = dh_in + _rms_back(r, xn, dy * gain)
    return dh, [_colsum(dn), _colsum(dn * y), _colsum(dy * xn)]


def _norm_mod_bwd(dn, h, dh_in, gain, scale, name, comm=None):
    d = h.shape[1]

    def fn(dn, h, dh_in, gain, scale):
        dh, vecs = _norm_mod_back(dn, h, dh_in, gain, scale)
        return [dh], vecs
    return _rows(fn, [dn, h, dh_in], [gain, scale], [(d, F32)], [d, d, d], tile=256, name=name, comm=comm)


def _mm_norm_mod_bwd(pairs, h, dh_in, gain, scale, branch, name, tm, comm=None):
    f, gate, coef = branch

    def epi(accs, ex, vc):
        dh, vecs = _norm_mod_back(accs[0], ex[0], ex[1], vc[0], vc[1])
        df, dgate = _branch_back(dh, ex[2], vc[2], coef)
        return [dh, df] + vecs + [dgate]
    return _mm([pairs], epi, [h, dh_in, f], [gain, scale, gate], [F32, BF16], trans_rhs=False, tm=tm,
               tn=h.shape[1], name=name, n_sums=4, comm=comm)


def _last_mm_loss(lhs, w, res, gate, coef, target, gain, name):
    d = w.shape[1]

    def epi(accs, ex, vc):
        f = accs[0]
        h = ex[0] + (coef * vc[0]) * f
        r, xn = _rms_stats(h)
        err = xn * vc[1] - ex[1]
        dout = err * (1.0 / d)
        dh = _rms_back(r, xn, dout * vc[1])
        df, dgate = _branch_back(dh, f, vc[0], coef)
        return [dh, df, _colsum(err * err), _colsum(dout * xn), dgate]
    return _mm([[(lhs, w)]], epi, [res, target], [gate, gain], [F32, BF16], trans_rhs=False, tm=256, tn=d,
               name=name, n_sums=3)


def _partner(x):
    if x.shape[1] > LANES:
        return jnp.concatenate([_partner(x[:, c:c + LANES]) for c in range(0, x.shape[1], LANES)], axis=1)
    lane = lax.broadcasted_iota(jnp.int32, x.shape, 1) % HEAD_DIM
    return jnp.where(lane < HALF_HEAD, pltpu.roll(x, LANES - HALF_HEAD, 1), pltpu.roll(x, HALF_HEAD, 1))


def _proj_rope(n, w_t, cos, sin_signed, width, name, comm=None):
    s, kdim = n.shape
    n_cols = w_t.shape[0]
    tm = _tile(s, 1024, 8)
    qscale = HEAD_DIM ** -0.5

    chunk = _tile(tm, 256, 8)

    def body(n_ref, w_ref, cos_ref, sin_ref, o_ref):
        j = pl.program_id(0)

        def products(rows):
            return lax.dot_general(n_ref[rows, :].astype(BF16), w_ref[...].astype(BF16), (((1,), (1,)), ((), ())),
                                   preferred_element_type=F32)

        @pl.when(j >= 2)
        def _():
            for c in range(tm // chunk):
                rows = slice(c * chunk, (c + 1) * chunk)
                o_ref[rows, :] = products(rows)

        @pl.when(j < 2)
        def _():
            scale = jnp.where(j == 0, qscale, 1.0)
            for c in range(tm // chunk):
                rows = slice(c * chunk, (c + 1) * chunk)
                acc = products(rows)
                cos = jnp.tile(cos_ref[rows, :], (1, width // LANES))
                sin = jnp.tile(sin_ref[rows, :], (1, width // LANES))
                o_ref[rows, :] = scale * (acc * cos + _partner(acc) * sin)

    table = pl.BlockSpec((tm, LANES), lambda j, i: (jnp.where(j < 2, i, 0), 0))
    return _call(
        body, grid=(n_cols // width, s // tm),
        in_specs=[pl.BlockSpec((tm, kdim), lambda j, i: (i, 0)), pl.BlockSpec((width, kdim), lambda j, i: (j, 0)),
                  table, table],
        out_specs=[pl.BlockSpec((tm, width), lambda j, i: (i, j))],
        out_shape=[jax.ShapeDtypeStruct((s, n_cols), F32)], args=(n, w_t, cos, sin_signed), name=name, comm=comm)


def _mix_post(attn, u1, attn_g, ln_g, ln_b, conv_g):
    _, xa = _rms_stats(attn)
    mu = jnp.mean(u1, axis=-1, keepdims=True)
    xc = u1 - mu
    rstd = lax.rsqrt(jnp.mean(xc * xc, axis=-1, keepdims=True) + LN_EPS)
    u2 = (xc * rstd) * ln_g + ln_b
    u3 = u2 * _sigmoid(u2)
    _, x3 = _rms_stats(u3)
    return jnp.concatenate([xa * attn_g, x3 * conv_g], axis=1)


def _mix_post_back(dy, attn, u1, attn_g, ln_g, ln_b, conv_g):
    w = attn.shape[1]
    dya, dyc = dy[:, :w], dy[:, w:]
    ra, xa = _rms_stats(attn)
    dattn = _rms_back(ra, xa, dya * attn_g)
    mu = jnp.mean(u1, axis=-1, keepdims=True)
    xc = u1 - mu
    rstd = lax.rsqrt(jnp.mean(xc * xc, axis=-1, keepdims=True) + LN_EPS)
    xh = xc * rstd
    u2 = xh * ln_g + ln_b
    sig = _sigmoid(u2)
    u3 = u2 * sig
    r3, x3 = _rms_stats(u3)
    du3 = _rms_back(r3, x3, dyc * conv_g)
    du2 = du3 * (sig + u3 * (1.0 - sig))
    dxh = du2 * ln_g
    du1 = rstd * (dxh - jnp.mean(dxh, axis=-1, keepdims=True) - xh * jnp.mean(dxh * xh, axis=-1, keepdims=True))
    return dattn, du1, [_colsum(dya * xa), _colsum(dyc * x3), _colsum(du2 * xh), _colsum(du2)]


def _silu_rows(c_all, name):
    def fn(c):
        return [c * _sigmoid(c)], []
    return _rows(fn, [c_all], [], [(c_all.shape[1], BF16)], [], tile=c_all.shape[0], name=name)[0]


def _mm(groups, epi, extras, vecs, outs, *, trans_rhs, tm, tn, name, n_sums=0, pre=None, pre_inputs=(),
        comm=None):
    m = (pre_inputs[0] if pre is not None else groups[0][0][0]).shape[0]
    n = groups[0][0][1].shape[0] if trans_rhs else groups[0][0][1].shape[1]
    tm, tn = min(tm, m), min(tn, n)
    in_specs, args, uses_pre = [], [], []
    for grp in groups:
        for lhs, rhs in grp:
            k = rhs.shape[1] if trans_rhs else rhs.shape[0]
            uses_pre.append(lhs is None)
            if lhs is not None:
                in_specs.append(pl.BlockSpec((tm, k), lambda j, i: (i, 0)))
                args.append(lhs)
            in_specs.append(pl.BlockSpec((tn, k), lambda j, i: (j, 0)) if trans_rhs
                            else pl.BlockSpec((k, tn), lambda j, i: (0, j)))
            args.append(rhs)
    n_mm = len(args)
    for p in pre_inputs:
        in_specs.append(pl.BlockSpec((tm, p.shape[1]), lambda j, i: (i, 0)))
        args.append(p)
    for e in extras:
        in_specs.append(pl.BlockSpec((tm, tn), lambda j, i: (i, j)) if e.shape[1] == n
                        else pl.BlockSpec((tm, e.shape[1]), lambda j, i: (i, 0)))
        args.append(e)
    for v in vecs:
        in_specs.append(pl.BlockSpec((1, tn), lambda j, i: (0, j)) if v.shape[1] == n
                        else pl.BlockSpec((1, v.shape[1]), lambda j, i: (0, 0)))
        args.append(v)
    sizes = [len(g) for g in groups]
    n_pre, n_ex, n_vec = len(pre_inputs), len(extras), len(vecs)
    dims = (((1,), (1,)), ((), ())) if trans_rhs else (((1,), (0,)), ((), ()))
    out_specs, out_shape = [], []
    if pre is not None:
        k_pre = args[n_mm - 1].shape[1] if trans_rhs else args[n_mm - 1].shape[0]
        out_specs.append(pl.BlockSpec((tm, k_pre), lambda j, i: (i, 0)))
        out_shape.append(jax.ShapeDtypeStruct((m, k_pre), BF16))
    for o in outs:
        dt, width = o if isinstance(o, tuple) else (o, n)
        out_specs.append(pl.BlockSpec((tm, tn), lambda j, i: (i, j)) if width == n
                         else pl.BlockSpec((tm, width), lambda j, i: (i, 0)))
        out_shape.append(jax.ShapeDtypeStruct((m, width), dt))
    n_tiles_out = len(out_specs)
    out_specs += [pl.BlockSpec((1, tn), lambda j, i: (0, j))] * n_sums
    out_shape += [jax.ShapeDtypeStruct((1, n), F32)] * n_sums

    def body(*refs):
        ins = refs[:n_mm + n_pre + n_ex + n_vec]
        out_refs = refs[n_mm + n_pre + n_ex + n_vec:]
        vc = [r[...] for r in ins[n_mm + n_pre + n_ex:]]
        vals = []
        made = None
        if pre is not None:
            made = pre([r[...] for r in ins[n_mm:n_mm + n_pre]], vc).astype(BF16)
            vals.append(made)
        accs, pos, pair = [], 0, 0
        for size in sizes:
            acc = None
            for _ in range(size):
                if uses_pre[pair]:
                    lhs_tile = made
                else:
                    lhs_tile = ins[pos][...].astype(BF16)
                    pos += 1
                part = lax.dot_general(lhs_tile, ins[pos][...].astype(BF16), dims, preferred_element_type=F32)
                acc = part if acc is None else acc + part
                pos += 1
                pair += 1
            accs.append(acc)
        ex = [r[...] for r in ins[n_mm + n_pre:n_mm + n_pre + n_ex]]
        vals += epi(accs, ex, vc)
        for ref, val in zip(out_refs[:n_tiles_out], vals):
            ref[...] = val.astype(ref.dtype)
        if n_sums:
            @pl.when(pl.program_id(1) == 0)
            def _():
                for ref in out_refs[n_tiles_out:]:
                    ref[...] = jnp.zeros_like(ref)
            for ref, val in zip(out_refs[n_tiles_out:], vals[n_tiles_out:]):
                ref[...] += val

    return _call(body, grid=(n // tn, m // tm), in_specs=in_specs, out_specs=out_specs, out_shape=out_shape,
                 args=args, name=name, comm=comm)


def _mm_tn(lhs, rhs, name, comm=None):
    t, a = lhs.shape
    b = rhs.shape[1]
    ta = a if a <= 1536 else _tile(a, 1536, LANES)
    tk = _tile(t, 2048, 8)

    def body(l_ref, r_ref, o_ref):
        @pl.when(pl.program_id(1) == 0)
        def _():
            o_ref[...] = jnp.zeros_like(o_ref)
        o_ref[...] += lax.dot_general(l_ref[...].astype(BF16), r_ref[...].astype(BF16), (((0,), (0,)), ((), ())),
                                      preferred_element_type=F32)

    res = _call(body, grid=(a // ta, t // tk),
                in_specs=[pl.BlockSpec((tk, ta), lambda i, k: (k, i)), pl.BlockSpec((tk, b), lambda i, k: (k, 0))],
                out_specs=[pl.BlockSpec((ta, b), lambda i, k: (i, 0))], out_shape=[jax.ShapeDtypeStruct((a, b), F32)],
                args=(lhs, rhs), name=name, comm=comm)
    return res[0] if comm is None else (res[0][0], res[1])


def _ffn_tn(f):
    return _tile(f, 1536, LANES)


def _swiglu_parts(a, b):
    sig = _sigmoid(a)
    silu = a * sig
    return [silu, b * (sig + silu * (1.0 - sig)), silu * b]


def _ffn_up(n, wg_t, wu_t, name, comm=None):
    def epi(accs, ex, vc):
        return _swiglu_parts(accs[0], accs[1])
    return _mm([[(n, wg_t)], [(n, wu_t)]], epi, [], [], [BF16, BF16, BF16], trans_rhs=True, tm=512,
               tn=_ffn_tn(wg_t.shape[0]), name=name, comm=comm)


def _norm_gate(h, gain, scale, shift, wg_t, name, comm=None):
    def pre(tiles, vc):
        _, xn = _rms_stats(tiles[0])
        return (xn * vc[0]) * (1.0 + vc[1]) + vc[2]

    def epi(accs, ex, vc):
        sig = _sigmoid(accs[0])
        silu = accs[0] * sig
        return [silu, sig + silu * (1.0 - sig)]
    return _mm([[(None, wg_t)]], epi, [], [gain, scale, shift], [BF16, BF16], trans_rhs=True, tm=256,
               tn=wg_t.shape[0], name=name, pre=pre, pre_inputs=[h], comm=comm)


def _mix_out(attn, u1, post, w, res, gate, norm, name):
    def pre(tiles, vc):
        return _mix_post(tiles[0], tiles[1], *vc[4:8])

    def epi(accs, ex, vc):
        h = ex[0] + vc[0] * accs[0]
        _, xn = _rms_stats(h)
        return [h, accs[0], (xn * vc[1]) * (1.0 + vc[2]) + vc[3]]
    return _mm([[(None, w)]], epi, [res], [gate] + list(norm) + list(post), [F32, BF16, BF16], trans_rhs=False,
               tm=512, tn=w.shape[1], name=name, pre=pre, pre_inputs=[attn, u1])


def _mix_dy_post_bwd(dmix, w, attn, u1, post, name):
    width = attn.shape[1]

    def epi(accs, ex, vc):
        dattn, du1, sums = _mix_post_back(accs[0], ex[0], ex[1], *vc)
        return [dattn, du1, jnp.concatenate(sums[0:2], axis=1), jnp.concatenate(sums[2:4], axis=1)]
    return _mm([[(dmix, w)]], epi, [attn, u1], list(post), [(F32, width), (F32, width)], trans_rhs=True, tm=256,
               tn=w.shape[0], name=name, n_sums=2)


def _ffn_up_given_gate(n, wu_t, silu, dsilu, name, comm=None):
    def epi(accs, ex, vc):
        return [accs[0] * ex[1].astype(F32), ex[0].astype(F32) * accs[0]]
    return _mm([[(n, wu_t)]], epi, [silu, dsilu], [], [BF16, BF16], trans_rhs=True, tm=512,
               tn=_ffn_tn(wu_t.shape[0]), name=name, comm=comm)


def _residual_mm(lhs, w, res, gate, coef, name, norm=None, comm=None):
    def epi(accs, ex, vc):
        h = ex[0] + (coef * vc[0]) * accs[0]
        if norm is None:
            return [h, accs[0]]
        _, xn = _rms_stats(h)
        return [h, accs[0], (xn * vc[1]) * (1.0 + vc[2]) + vc[3]]
    vecs = [gate] + (list(norm) if norm is not None else [])
    outs = [F32, BF16] + ([BF16] if norm is not None else [])
    return _mm([[(lhs, w)]], epi, [res], vecs, outs, trans_rhs=False, tm=512, tn=w.shape[1], name=name, comm=comm)


def _ffn_bwd_hidden(df, wd, dhid_db, dhid_da, name, comm=None):
    def epi(accs, ex, vc):
        return [accs[0] * ex[1].astype(F32), accs[0] * ex[0].astype(F32)]
    return _mm([[(df, wd)]], epi, [dhid_db, dhid_da], [], [BF16, BF16], trans_rhs=True, tm=512,
               tn=_ffn_tn(wd.shape[0]), name=name, comm=comm)


def _plain_mm(pairs, out_dtype, trans_rhs, tn, name, tm=512, comm=None):
    def epi(accs, ex, vc):
        return [accs[0]]
    res = _mm([pairs], epi, [], [], [out_dtype], trans_rhs=trans_rhs, tm=tm, tn=tn, name=name, comm=comm)
    return res[0] if comm is None else (res[0][0], res[1])


HEADS_PER_TILE = LANES // HEAD_DIM


def _stack_heads(x):
    lane = lax.broadcasted_iota(jnp.int32, (1, LANES), 1)
    return jnp.concatenate([x * (lane // HEAD_DIM == h).astype(F32) for h in range(HEADS_PER_TILE)], axis=0)


def _unstack_heads(y):
    r = y.shape[0] // HEADS_PER_TILE
    lane = lax.broadcasted_iota(jnp.int32, (r, y.shape[1]), 1)
    out = y[0:r]
    for h in range(1, HEADS_PER_TILE):
        out = jnp.where(lane // HEAD_DIM == h, y[h * r:(h + 1) * r], out)
    return out


def _stacked_lse(lb):
    return jnp.concatenate([_lane_pick(lb, h) for h in range(HEADS_PER_TILE)], axis=0)


def _band_masks(n_row_blocks, n_col_blocks):
    shape = (n_row_blocks * BLOCK, n_col_blocks * BLOCK)
    qi = lax.broadcasted_iota(jnp.int32, shape, 0) % BLOCK
    kj = lax.broadcasted_iota(jnp.int32, shape, 1) % BLOCK
    return kj <= qi, kj >= qi


def _query_masks():
    first_valid, _ = _band_masks(HEADS_PER_TILE, 1)
    same_ok, before_ok = _band_masks(HEADS_PER_TILE, 2)
    is_cur = lax.broadcasted_iota(jnp.int32, same_ok.shape, 1) >= BLOCK
    return first_valid, jnp.logical_and(is_cur, same_ok), jnp.logical_and(jnp.logical_not(is_cur), before_ok)


def _dot_nt(a, b):
    return lax.dot_general(a.astype(BF16), b.astype(BF16), (((1,), (1,)), ((), ())), preferred_element_type=F32)


def _dot_nn(a, b):
    return lax.dot_general(a.astype(BF16), b.astype(BF16), (((1,), (0,)), ((), ())), preferred_element_type=F32)


def _dot_tn(a, b):
    return lax.dot_general(a.astype(BF16), b.astype(BF16), (((0,), (0,)), ((), ())), preferred_element_type=F32)


def _lane_pick(x, h):
    lane = lax.broadcasted_iota(jnp.int32, x.shape, 1)
    return jnp.sum(jnp.where(lane == h * HEAD_DIM, x, 0.0), axis=1, keepdims=True)


def _block_rows(idx, d):
    span = BLOCK * d
    q0 = (idx // d) * span + idx % d
    return pl.ds(q0, BLOCK, stride=d), pl.ds(q0 - span, BLOCK, stride=d)


def _branch_loops(n_blocks, d, visit, unroll, masks):
    first_valid, cur_part, prev_part = masks
    if d % unroll == 0 and (n_blocks - d) % unroll == 0:
        full_valid = jnp.logical_or(cur_part, prev_part)

        def first(idx, carry):
            rows = pl.ds(idx, BLOCK, stride=d)
            visit(rows, [rows], first_valid)
            return carry

        def rest(idx, carry):
            rows, prev = _block_rows(idx, d)
            visit(rows, [prev, rows], full_valid)
            return carry

        lax.fori_loop(0, d, first, 0, unroll=unroll)
        lax.fori_loop(d, n_blocks, rest, 0, unroll=unroll)
        return

    def every(idx, carry):
        span = BLOCK * d
        q0 = (idx // d) * span + idx % d
        has_prev = idx >= d
        rows = pl.ds(q0, BLOCK, stride=d)
        prev = pl.ds(jnp.where(has_prev, q0 - span, q0), BLOCK, stride=d)
        visit(rows, [prev, rows], jnp.logical_or(cur_part, jnp.logical_and(prev_part, has_prev)))
        return carry

    lax.fori_loop(0, n_blocks, every, 0, unroll=unroll)


def _qkv_specs(s, tiles):
    q, k, v = [pl.BlockSpec((s, LANES), functools.partial(lambda hb, off: (0, off + hb), off=i * tiles))
               for i in range(3)]
    return q, k, v, pl.BlockSpec((s, LANES), lambda hb: (0, hb))


def _attn_seq_fwd(proj, width, name, comm=None):
    s = proj.shape[0]
    q_spec, k_spec, v_spec, cur = _qkv_specs(s, width // LANES)

    def body(q_ref, k_ref, v_ref, o_ref, l_ref, o_s, l_s):
        masks = _query_masks()
        for bi, d in enumerate(DILATIONS):
            def visit(rows, key_rows, valid, bi=bi):
                q2 = _stack_heads(q_ref[rows, :])
                keys = jnp.concatenate([k_ref[r, :] for r in key_rows], axis=0)
                vals = jnp.concatenate([v_ref[r, :] for r in key_rows], axis=0)
                sc = jnp.where(valid, _dot_nt(q2, keys), NEG)
                mx = jnp.max(sc, axis=1, keepdims=True)
                p = jnp.exp(sc - mx)
                den = jnp.sum(p, axis=1, keepdims=True)
                o_s[bi, rows, :] = _unstack_heads(_dot_nn(p, vals) / den)
                l_s[bi, rows, :] = _unstack_heads(jnp.broadcast_to(mx + jnp.log(den), (q2.shape[0], LANES)))

            _branch_loops(s // BLOCK, d, visit, 8, masks)
        for c in range(s // MERGE_CHUNK):
            rows = slice(c * MERGE_CHUNK, (c + 1) * MERGE_CHUNK)
            ls = [l_s[bi, rows, :] for bi in range(len(DILATIONS))]
            top = functools.reduce(jnp.maximum, ls)
            ws = [jnp.exp(l - top) for l in ls]
            den = functools.reduce(lambda a, b: a + b, ws)
            num = functools.reduce(lambda a, b: a + b, [w * o_s[bi, rows, :] for bi, w in enumerate(ws)])
            o_ref[rows, :] = num / den
            l_ref[rows, :] = top + jnp.log(den)

    return _call(
        body, grid=(width // LANES,), in_specs=[q_spec, k_spec, v_spec], out_specs=[cur, cur],
        out_shape=[jax.ShapeDtypeStruct((s, width), F32)] * 2,
        scratch_shapes=[pltpu.VMEM((len(DILATIONS), s, LANES), F32)] * 2,
        args=(proj, proj, proj), name=name, comm=comm)


def _attn_seq_bwd(proj, do, o, lse, cos, sin_signed, name, comm=None):
    s, width = do.shape
    q_spec, k_spec, v_spec, cur = _qkv_specs(s, width // LANES)
    table = pl.BlockSpec((s, LANES), lambda hb: (0, 0))
    qscale = HEAD_DIM ** -0.5

    def body(q_ref, k_ref, v_ref, do_ref, o_ref, l_ref, cos_ref, sin_ref, dq_out, dk_out, dv_out,
             dq_ref, dk_ref, dv_ref):
        dq_ref[...] = jnp.zeros_like(dq_ref)
        dk_ref[...] = jnp.zeros_like(dk_ref)
        dv_ref[...] = jnp.zeros_like(dv_ref)
        masks = _query_masks()
        for d in DILATIONS:
            def visit(rows, key_rows, valid):
                dob = do_ref[rows, :]
                q2 = _stack_heads(q_ref[rows, :])
                do2 = _stack_heads(dob)
                delta = jnp.sum(_stack_heads(dob * o_ref[rows, :]), axis=1, keepdims=True)
                lse2 = _stacked_lse(l_ref[rows, :])
                keys = jnp.concatenate([k_ref[r, :] for r in key_rows], axis=0)
                vals = jnp.concatenate([v_ref[r, :] for r in key_rows], axis=0)
                p = jnp.where(valid, jnp.exp(_dot_nt(q2, keys) - lse2), 0.0)
                ds = p * (_dot_nt(do2, vals) - delta)
                dq_ref[rows, :] += _unstack_heads(_dot_nn(ds, keys))
                dkk = _dot_tn(ds, q2)
                dvv = _dot_tn(p, do2)
                for i, r in enumerate(key_rows):
                    dk_ref[r, :] += dkk[i * BLOCK:(i + 1) * BLOCK]
                    dv_ref[r, :] += dvv[i * BLOCK:(i + 1) * BLOCK]

            _branch_loops(s // BLOCK, d, visit, 8, masks)
        for c in range(s // MERGE_CHUNK):
            rows = slice(c * MERGE_CHUNK, (c + 1) * MERGE_CHUNK)
            cos, sin = cos_ref[rows, :], sin_ref[rows, :]
            dq, dk = dq_ref[rows, :], dk_ref[rows, :]
            dq_out[rows, :] = ((dq * cos - _partner(dq) * sin) * qscale).astype(BF16)
            dk_out[rows, :] = (dk * cos - _partner(dk) * sin).astype(BF16)
            dv_out[rows, :] = dv_ref[rows, :].astype(BF16)

    return _call(
        body, grid=(width // LANES,), in_specs=[q_spec, k_spec, v_spec, cur, cur, cur, table, table],
        out_specs=[cur, cur, cur], out_shape=[jax.ShapeDtypeStruct((s, width), BF16)] * 3,
        scratch_shapes=[pltpu.VMEM((s, LANES), F32)] * 3,
        args=(proj, proj, proj, do, o, lse, cos, sin_signed), name=name, comm=comm)


def _conv_specs(s, a_block, b_block):
    per = CONV_CHUNK // CONV_HALO
    a_cur = pl.BlockSpec((CONV_CHUNK, LANES), lambda cb, i: (i, a_block + cb))
    b_cur = pl.BlockSpec((CONV_CHUNK, LANES), lambda cb, i: (i, b_block + cb))
    a_halo = pl.BlockSpec((CONV_HALO, LANES), lambda cb, i: (jnp.maximum(i * per - 1, 0), a_block + cb))
    b_halo = pl.BlockSpec((CONV_HALO, LANES), lambda cb, i: (jnp.maximum(i * per - 1, 0), b_block + cb))
    w_spec = pl.BlockSpec((CONV_KERNEL, LANES), lambda cb, i: (0, cb))
    vec = pl.BlockSpec((1, LANES), lambda cb, i: (0, cb))
    out = pl.BlockSpec((CONV_CHUNK, LANES), lambda cb, i: (i, cb))
    return a_cur, b_cur, a_halo, b_halo, w_spec, vec, out


def _fill_glu_window(win, a_ref, b_ref, ah_ref, bh_ref, first):
    halo = ah_ref[...] * _sigmoid(bh_ref[...])
    win[0:CONV_HALO, :] = jnp.where(first, 0.0, halo)
    win[CONV_HALO:, :] = a_ref[...] * _sigmoid(b_ref[...])


def _conv_fwd(proj, a_block, b_block, w, bias, name, comm=None):
    s = proj.shape[0]
    cw = w.shape[1]
    a_cur, b_cur, a_halo, b_halo, w_spec, vec, out = _conv_specs(s, a_block, b_block)
    lead = CONV_HALO - (CONV_KERNEL - 1)

    def body(a_ref, b_ref, ah_ref, bh_ref, w_ref, bias_ref, o_ref, win):
        _fill_glu_window(win, a_ref, b_ref, ah_ref, bh_ref, pl.program_id(1) == 0)
        for sub in range(CONV_CHUNK // CONV_SUB):
            base = sub * CONV_SUB
            acc = jnp.zeros((CONV_SUB, LANES), F32) + bias_ref[...]
            for j in range(CONV_KERNEL):
                acc = acc + w_ref[j:j + 1, :] * win[base + lead + j:base + lead + j + CONV_SUB, :]
            o_ref[base:base + CONV_SUB, :] = acc

    return _call(
        body, grid=(cw // LANES, s // CONV_CHUNK), in_specs=[a_cur, b_cur, a_halo, b_halo, w_spec, vec],
        out_specs=[out], out_shape=[jax.ShapeDtypeStruct((s, cw), F32)],
        scratch_shapes=[pltpu.VMEM((CONV_CHUNK + CONV_HALO, LANES), F32)],
        args=(proj, proj, proj, proj, w, bias), name=name, comm=comm)


def _conv_bwd(proj, a_block, b_block, w, du1, name):
    s = proj.shape[0]
    cw = w.shape[1]
    a_cur, b_cur, a_halo, b_halo, w_spec, vec, out = _conv_specs(s, a_block, b_block)
    per = CONV_CHUNK // CONV_HALO
    n_chunks = s // CONV_CHUNK
    d_next = pl.BlockSpec((CONV_HALO, LANES), lambda cb, i: (jnp.minimum((i + 1) * per, s // CONV_HALO - 1), cb))
    lead = CONV_HALO - (CONV_KERNEL - 1)

    def body(a_ref, b_ref, ah_ref, bh_ref, w_ref, d_ref, dn_ref, da_ref, db_ref, dw_ref, dbias_ref, win, dwin):
        i = pl.program_id(1)
        _fill_glu_window(win, a_ref, b_ref, ah_ref, bh_ref, i == 0)
        dwin[0:CONV_CHUNK, :] = d_ref[...]
        dwin[CONV_CHUNK:, :] = jnp.where(i == n_chunks - 1, 0.0, dn_ref[...])

        @pl.when(i == 0)
        def _():
            dw_ref[...] = jnp.zeros_like(dw_ref)
            dbias_ref[...] = jnp.zeros_like(dbias_ref)

        dbias_ref[...] += _colsum(d_ref[...])
        for sub in range(CONV_CHUNK // CONV_SUB):
            base = sub * CONV_SUB
            dcur = dwin[base:base + CONV_SUB, :]
            du0 = jnp.zeros((CONV_SUB, LANES), F32)
            for j in range(CONV_KERNEL):
                back = CONV_KERNEL - 1 - j
                du0 = du0 + w_ref[j:j + 1, :] * dwin[base + back:base + back + CONV_SUB, :]
                dw_ref[j:j + 1, :] += _colsum(dcur * win[base + lead + j:base + lead + j + CONV_SUB, :])
            av = a_ref[base:base + CONV_SUB, :]
            sig = _sigmoid(b_ref[base:base + CONV_SUB, :])
            da_ref[base:base + CONV_SUB, :] = (du0 * sig).astype(BF16)
            db_ref[base:base + CONV_SUB, :] = (du0 * av * sig * (1.0 - sig)).astype(BF16)

    return pl.pallas_call(
        body, grid=(cw // LANES, n_chunks), in_specs=[a_cur, b_cur, a_halo, b_halo, w_spec, out, d_next],
        out_specs=[out, out, w_spec, vec],
        out_shape=[jax.ShapeDtypeStruct((s, cw), BF16), jax.ShapeDtypeStruct((s, cw), BF16),
                   jax.ShapeDtypeStruct((CONV_KERNEL, cw), F32), jax.ShapeDtypeStruct((1, cw), F32)],
        scratch_shapes=[pltpu.VMEM((CONV_CHUNK + CONV_HALO, LANES), F32)] * 2,
        compiler_params=_params(2), name=name)(proj, proj, proj, proj, w, du1, du1)


def _adamw_math(w, g, m, v):
    m = ADAM_B1 * m + (1.0 - ADAM_B1) * g
    v = ADAM_B2 * v + (1.0 - ADAM_B2) * (g * g)
    m_hat = m / (1.0 - ADAM_B1 ** ADAM_STEP)
    v_hat = v / (1.0 - ADAM_B2 ** ADAM_STEP)
    delta = -ADAM_LR * (m_hat / (jnp.sqrt(v_hat) + ADAM_EPS) + ADAM_WD * w)
    return delta, m, v


def _adamw_big(w, g, m, v, name):
    rows, cols = w.shape
    tile = _tile(rows, 256, 8)
    spec = pl.BlockSpec((tile, cols), lambda i: (i, 0))

    def body(w_ref, g_ref, m_ref, v_ref, d_out, m_out, v_out):
        d_out[...], m_out[...], v_out[...] = _adamw_math(w_ref[...], g_ref[...], m_ref[...], v_ref[...])

    return pl.pallas_call(body, grid=(rows // tile,), in_specs=[spec] * 4, out_specs=[spec] * 3,
                          out_shape=[jax.ShapeDtypeStruct(w.shape, F32)] * 3, compiler_params=_params(1),
                          name=name)(w, g, m, v)


def _adamw_reduced(w, land, m, v, name):
    rows, cols = w.shape
    tile = _tile(rows, 256, 16)
    spec = pl.BlockSpec((tile, cols), lambda i: (i, 0))

    def body(w_ref, l_ref, m_ref, v_ref, g_out, d_out, m_out, v_out):
        g = l_ref[0].astype(F32)
        for q in range(1, N_CHIP):
            g = g + l_ref[q].astype(F32)
        g_out[...] = g
        d_out[...], m_out[...], v_out[...] = _adamw_math(w_ref[...], g, m_ref[...], v_ref[...])

    return pl.pallas_call(body, grid=(rows // tile,),
                          in_specs=[spec, pl.BlockSpec((N_CHIP, tile, cols), lambda i: (0, i, 0)), spec, spec],
                          out_specs=[spec] * 4, out_shape=[jax.ShapeDtypeStruct(w.shape, F32)] * 4,
                          compiler_params=_params(1), name=name)(w, land, m, v)


def _adamw_small(ws, gs, ms, vs, name):
    n = len(ws)

    def body(*refs):
        ins, outs = refs[:4 * n], refs[4 * n:]
        for t in range(n):
            res = _adamw_math(ins[t][...], ins[n + t][...], ins[2 * n + t][...], ins[3 * n + t][...])
            for j in range(3):
                outs[j * n + t][...] = res[j]

    shapes = [jax.ShapeDtypeStruct(w.shape, F32) for w in ws]
    res = pl.pallas_call(body, out_shape=shapes * 3, compiler_params=pltpu.CompilerParams(vmem_limit_bytes=VMEM_LIMIT),
                         name=name)(*ws, *gs, *ms, *vs)
    return res[:n], res[n:2 * n], res[2 * n:]


def _sum_blocks(x, n_blocks, name):
    r = x.shape[0] // n_blocks

    def body(x_ref, o_ref):
        acc = x_ref[0:r, :]
        for b in range(1, n_blocks):
            acc = acc + x_ref[b * r:(b + 1) * r, :]
        o_ref[...] = acc

    return pl.pallas_call(body, out_shape=jax.ShapeDtypeStruct((r, x.shape[1]), F32),
                          compiler_params=pltpu.CompilerParams(vmem_limit_bytes=VMEM_LIMIT), name=name)(x)


def _coords():
    return lax.axis_index("x"), lax.axis_index("y"), lax.axis_index("c")


def _flip(v, bit):
    return 1 - v if bit else v


def _ag_small(x, name):
    r, c = x.shape

    def body(x_ref, o_ref, send, recv, local_sem):
        mx, my, mc = _coords()

        def rows(px, py, pc):
            return o_ref.at[pl.ds(pl.multiple_of((4 * px + 2 * py + pc) * r, 8), r), :]

        local = pltpu.make_async_copy(x_ref, rows(mx, my, mc), local_sem)
        local.start()
        peers = [(_flip(mx, k >> 2 & 1), _flip(my, k >> 1 & 1), _flip(mc, k & 1)) for k in range(1, N_DEV)]
        sends = [pltpu.make_async_remote_copy(x_ref, rows(mx, my, mc), send.at[k], recv.at[k], device_id=p,
                                              device_id_type=MESH) for k, p in enumerate(peers)]
        for cp in sends:
            cp.start()
        for k, p in enumerate(peers):
            pltpu.make_async_remote_copy(x_ref, rows(*p), send.at[k], recv.at[k], device_id=p,
                                         device_id_type=MESH).wait_recv()
        for cp in sends:
            cp.wait_send()
        local.wait()

    vm = pl.BlockSpec(memory_space=pltpu.VMEM)
    return pl.pallas_call(
        body, in_specs=[vm], out_specs=vm, out_shape=jax.ShapeDtypeStruct((N_DEV * r, c), x.dtype),
        scratch_shapes=[pltpu.SemaphoreType.DMA((N_DEV - 1,)), pltpu.SemaphoreType.DMA((N_DEV - 1,)),
                        pltpu.SemaphoreType.DMA(())],
        name=name)(x)


class _GatherSmall:
    mid = None

    def __init__(self, x):
        self.inputs = [x]
        self.out_shapes = [jax.ShapeDtypeStruct((N_DEV * x.shape[0], x.shape[1]), x.dtype)]
        self.scratch = [pltpu.SemaphoreType.DMA((N_DEV - 1,)), pltpu.SemaphoreType.DMA((N_DEV - 1,)),
                        pltpu.SemaphoreType.DMA(())]

    def _plan(self, x_refs, o_refs, sems):
        send, recv, local_sem = sems
        x_ref, o_ref = x_refs[0], o_refs[0]
        r = x_ref.shape[0]
        mx, my, mc = _coords()

        def rows(px, py, pc):
            return o_ref.at[pl.ds(pl.multiple_of((4 * px + 2 * py + pc) * r, 8), r), :]

        peers = [(_flip(mx, k >> 2 & 1), _flip(my, k >> 1 & 1), _flip(mc, k & 1)) for k in range(1, N_DEV)]
        out = [pltpu.make_async_remote_copy(x_ref, rows(mx, my, mc), send.at[k], recv.at[k], device_id=p,
                                            device_id_type=MESH) for k, p in enumerate(peers)]
        arrivals = [pltpu.make_async_remote_copy(x_ref, rows(*p), send.at[k], recv.at[k], device_id=p,
                                                 device_id_type=MESH) for k, p in enumerate(peers)]
        return out, arrivals, pltpu.make_async_copy(x_ref, rows(mx, my, mc), local_sem)

    def start(self, x_refs, o_refs, sems):
        out, _, local = self._plan(x_refs, o_refs, sems)
        local.start()
        for cp in out:
            cp.start()

    def finish(self, x_refs, o_refs, sems):
        out, arrivals, local = self._plan(x_refs, o_refs, sems)
        for cp in arrivals:
            cp.wait_recv()
        for cp in out:
            cp.wait_send()
        local.wait()


class _GatherWeights:
    def __init__(self, shards):
        n_t = len(shards)
        self.inputs = list(shards)
        self.out_shapes = [jax.ShapeDtypeStruct((N_DEV * x.shape[0], x.shape[1]), x.dtype) for x in shards]
        self.scratch = [pltpu.SemaphoreType.DMA((n_t, 8)), pltpu.SemaphoreType.DMA((n_t, 8)),
                        pltpu.SemaphoreType.DMA((n_t,))]

    def _plan(self, x_refs, o_refs, sems):
        send, recv, local_sem = sems
        mx, my, mc = _coords()
        me, sibling = (mx, my, mc), (mx, my, 1 - mc)
        xn, yn, diag = (1 - mx, my), (mx, 1 - my), (1 - mx, 1 - my)

        def rows(t, chip, core, half=None):
            r = x_refs[t].shape[0]
            base = (4 * chip[0] + 2 * chip[1] + core) * r
            if half is None:
                return o_refs[t].at[pl.ds(pl.multiple_of(base, 8), r), :]
            return o_refs[t].at[pl.ds(pl.multiple_of(base + half * (r // 2), 8), r // 2), :]

        def copy(t, k, block, to, src=None):
            return pltpu.make_async_remote_copy(
                src_ref=block if src is None else src, dst_ref=block,
                send_sem=send.at[t, k], recv_sem=recv.at[t, k], device_id=to, device_id_type=MESH)

        def local(t):
            return pltpu.make_async_copy(x_refs[t], rows(t, (mx, my), mc), local_sem.at[t])

        return (mx, my), mc, me, sibling, xn, yn, diag, rows, copy, local

    def start(self, x_refs, o_refs, sems):
        chip, mc, me, sibling, xn, yn, diag, rows, copy, local = self._plan(x_refs, o_refs, sems)
        for t in range(len(x_refs)):
            mine = rows(t, chip, mc)
            local(t).start()
            copy(t, 0, mine, sibling, src=x_refs[t]).start()
            copy(t, 1, mine, (*xn, mc), src=x_refs[t]).start()
            copy(t, 2, mine, (*yn, mc), src=x_refs[t]).start()

    def mid(self, x_refs, o_refs, sems):
        chip, mc, me, sibling, xn, yn, diag, rows, copy, local = self._plan(x_refs, o_refs, sems)
        for t in range(len(x_refs)):
            copy(t, 1, rows(t, xn, mc), me).wait_recv()
            copy(t, 3, rows(t, xn, mc, 0), (*yn, mc)).start()
            copy(t, 5, rows(t, xn, mc), sibling).start()
        for t in range(len(x_refs)):
            copy(t, 2, rows(t, yn, mc), me).wait_recv()
            copy(t, 4, rows(t, yn, mc, 1), (*xn, mc)).start()
            copy(t, 6, rows(t, yn, mc), sibling).start()

    def finish(self, x_refs, o_refs, sems):
        chip, mc, me, sibling, xn, yn, diag, rows, copy, local = self._plan(x_refs, o_refs, sems)
        for t in range(len(x_refs)):
            copy(t, 3, rows(t, diag, mc, 0), me).wait_recv()
            copy(t, 4, rows(t, diag, mc, 1), me).wait_recv()
            copy(t, 7, rows(t, diag, mc), sibling).start()
        for t in range(len(x_refs)):
            copy(t, 0, rows(t, chip, 1 - mc), me).wait_recv()
            copy(t, 5, rows(t, xn, 1 - mc), me).wait_recv()
            copy(t, 6, rows(t, yn, 1 - mc), me).wait_recv()
            copy(t, 7, rows(t, diag, 1 - mc), me).wait_recv()
            mine = rows(t, chip, mc)
            copy(t, 0, mine, sibling, src=x_refs[t]).wait_send()
            copy(t, 1, mine, (*xn, mc), src=x_refs[t]).wait_send()
            copy(t, 2, mine, (*yn, mc), src=x_refs[t]).wait_send()
            copy(t, 3, rows(t, xn, mc, 0), (*yn, mc)).wait_send()
            copy(t, 4, rows(t, yn, mc, 1), (*xn, mc)).wait_send()
            copy(t, 5, rows(t, xn, mc), sibling).wait_send()
            copy(t, 6, rows(t, yn, mc), sibling).wait_send()
            copy(t, 7, rows(t, diag, mc), sibling).wait_send()
            local(t).wait()


class _SiblingExchange:
    mid = None

    def __init__(self, grads):
        n_t = len(grads)
        self.inputs = list(grads)
        self.out_shapes = [jax.ShapeDtypeStruct((N_CHIP,) + g.shape[2:], F32) for g in grads]
        self.scratch = [pltpu.SemaphoreType.DMA((n_t,)), pltpu.SemaphoreType.DMA((n_t,))]

    def _copies(self, g_refs, land, sems):
        send, recv = sems
        mx, my, mc = _coords()
        return [pltpu.make_async_remote_copy(g_refs[t].at[:, 1 - mc], land[t], send.at[t], recv.at[t],
                                             device_id=(mx, my, 1 - mc), device_id_type=MESH)
                for t in range(len(g_refs))]

    def start(self, g_refs, land, sems):
        for cp in self._copies(g_refs, land, sems):
            cp.start()

    def finish(self, g_refs, land, sems):
        for cp in self._copies(g_refs, land, sems):
            cp.wait()


class _Together:
    def __init__(self, *comms):
        self.comms = comms
        self.inputs = [x for c in comms for x in c.inputs]
        self.out_shapes = [x for c in comms for x in c.out_shapes]
        self.scratch = [x for c in comms for x in c.scratch]
        self.mid = self._mid if any(c.mid is not None for c in comms) else None

    def _each(self, phase, cin, cout, sems):
        i = o = s = 0
        for c in self.comms:
            fn = getattr(c, phase)
            ni, no, ns = len(c.inputs), len(c.out_shapes), len(c.scratch)
            if fn is not None:
                fn(cin[i:i + ni], cout[o:o + no], sems[s:s + ns])
            i, o, s = i + ni, o + no, s + ns

    def start(self, cin, cout, sems):
        self._each("start", cin, cout, sems)

    def _mid(self, cin, cout, sems):
        self._each("mid", cin, cout, sems)

    def finish(self, cin, cout, sems):
        self._each("finish", cin, cout, sems)


def _standalone(comm, name):
    def body():
        pass
    return _call(body, grid=(1,), in_specs=[], out_specs=[], out_shape=[], args=(), name=name, comm=comm)[1]


def _chip_partials(g4s, lands, name):
    n_t = len(g4s)
    in_specs, out_specs, out_shape = [], [], []
    for g4 in g4s:
        _, _, r, c = g4.shape
        in_specs.append(pl.BlockSpec((None, None, r, c), lambda q: (q, lax.axis_index("c"), 0, 0)))
        out_specs.append(pl.BlockSpec((None, r, c), lambda q: (q, 0, 0)))
        out_shape.append(jax.ShapeDtypeStruct((N_CHIP, r, c), BF16))
    in_specs += [pl.BlockSpec((None,) + g4.shape[2:], lambda q: (q, 0, 0)) for g4 in g4s]

    def body(*refs):
        for t in range(n_t):
            refs[2 * n_t + t][...] = (refs[t][...] + refs[n_t + t][...]).astype(BF16)

    return pl.pallas_call(body, grid=(N_CHIP,), in_specs=in_specs, out_specs=out_specs, out_shape=out_shape,
                          compiler_params=_params(1), name=name)(*g4s, *lands)


class _ChipExchange:
    mid = None

    def __init__(self, parts):
        n_t = len(parts)
        self.inputs = list(parts)
        self.out_shapes = [jax.ShapeDtypeStruct(p.shape, p.dtype) for p in parts]
        self.scratch = [pltpu.SemaphoreType.DMA((n_t, 3)), pltpu.SemaphoreType.DMA((n_t, 3)),
                        pltpu.SemaphoreType.DMA((n_t,))]

    def _plan(self, p_refs, land, sems):
        send, recv, local_sem = sems
        mx, my, mc = _coords()
        my_chip = 2 * mx + my
        peers = [(_flip(mx, fx), _flip(my, fy)) for fx, fy in ((1, 0), (0, 1), (1, 1))]

        def out(t, k):
            px, py = peers[k]
            return pltpu.make_async_remote_copy(p_refs[t].at[2 * px + py], land[t].at[my_chip], send.at[t, k],
                                                recv.at[t, k], device_id=(px, py, mc), device_id_type=MESH)

        def arrival(t, k):
            px, py = peers[k]
            return pltpu.make_async_remote_copy(p_refs[t].at[my_chip], land[t].at[2 * px + py], send.at[t, k],
                                                recv.at[t, k], device_id=(px, py, mc), device_id_type=MESH)

        def local(t):
            return pltpu.make_async_copy(p_refs[t].at[my_chip], land[t].at[my_chip], local_sem.at[t])

        return out, arrival, local

    def start(self, p_refs, land, sems):
        out, arrival, local = self._plan(p_refs, land, sems)
        for t in range(len(p_refs)):
            local(t).start()
            for k in range(3):
                out(t, k).start()

    def finish(self, p_refs, land, sems):
        out, arrival, local = self._plan(p_refs, land, sems)
        for t in range(len(p_refs)):
            for k in range(3):
                arrival(t, k).wait_recv()
                out(t, k).wait_send()
            local(t).wait()


def _rope_tables(s, width):
    heads = width // HEAD_DIM
    inv_freq = ROPE_THETA ** (-jnp.arange(0, HEAD_DIM, 2, dtype=F32) / HEAD_DIM)
    inv_full = jnp.tile(inv_freq, 2 * heads)
    sign = jnp.tile(jnp.concatenate([-jnp.ones((HALF_HEAD,), F32), jnp.ones((HALF_HEAD,), F32)]), heads)
    ang = jnp.arange(s, dtype=F32)[:, None] * inv_full[None, :]
    return jnp.cos(ang), jnp.sin(ang) * sign[None, :]


def _pad_rows(v, rows):
    return jnp.concatenate([v, jnp.zeros((rows - 1, v.shape[1]), v.dtype)], axis=0)


def kernel(x, c, w_ada, b_ada, ffn1_norm_g, ffn1_w_gate, ffn1_w_up, ffn1_w_down, mix_norm_g, w_in, conv_dw_w, conv_dw_b, conv_ln_g, conv_ln_b, attn_out_g, conv_out_g, w_out, ffn2_norm_g, ffn2_w_gate, ffn2_w_up, ffn2_w_down, final_norm_g, loss_target, m_w_ada, m_b_ada, m_ffn1_norm_g, m_ffn1_w_gate, m_ffn1_w_up, m_ffn1_w_down, m_mix_norm_g, m_w_in, m_conv_dw_w, m_conv_dw_b, m_conv_ln_g, m_conv_ln_b, m_attn_out_g, m_conv_out_g, m_w_out, m_ffn2_norm_g, m_ffn2_w_gate, m_ffn2_w_up, m_ffn2_w_down, m_final_norm_g, v_w_ada, v_b_ada, v_ffn1_norm_g, v_ffn1_w_gate, v_ffn1_w_up, v_ffn1_w_down, v_mix_norm_g, v_w_in, v_conv_dw_w, v_conv_dw_b, v_conv_ln_g, v_conv_ln_b, v_attn_out_g, v_conv_out_g, v_w_out, v_ffn2_norm_g, v_ffn2_w_gate, v_ffn2_w_up, v_ffn2_w_down, v_final_norm_g):
    mx, my, mc = _coords()
    me = 4 * mx + 2 * my + mc
    s, d = x.shape[1], x.shape[2]
    aw = d // 2
    x2, target = x[0], loss_target[0]
    n_mod = w_ada.shape[2] * N_DEV // d
    mod_cols = w_ada.shape[2]

    def shard(w, transpose):
        return (w[0].T if transpose else w[0]).astype(BF16)

    cw_shard = conv_dw_w.shape[3]
    n_taps = CONV_KERNEL * cw_shard
    first_len = -(-(d + n_taps) // LANES) * LANES
    first = jnp.concatenate([c, conv_dw_w[0, :, 0, :].reshape(1, n_taps), jnp.zeros((1, first_len - d - n_taps), F32)], axis=1)
    first_all, wg1 = _standalone(
        _Together(_GatherSmall(_pad_rows(first, 8)), _GatherWeights([shard(ffn1_w_gate, True)])), "ag_first")
    first_all = first_all[0::8]
    c_all = first_all[:, :d]
    conv_w = first_all[:, d:d + n_taps].reshape(N_DEV, CONV_KERNEL, cw_shard).transpose(1, 0, 2).reshape(CONV_KERNEL, aw)

    silu_c = _silu_rows(c_all, "silu_c")
    mod_part = _plain_mm([(silu_c, w_ada[0])], F32, False, mod_cols, "mod_mm")
    mod_all = _ag_small(mod_part, "ag_mod").reshape(N_DEV, N_DEV, mod_cols)
    mod = lax.dynamic_index_in_dim(mod_all, me, axis=1, keepdims=False).reshape(1, n_mod * d) + b_ada
    sh1, sc1, g1, sh2, sc2, g2, sh3, sc3, g3 = [mod[:, i * d:(i + 1) * d] for i in range(n_mod)]

    def split(g):
        return g.reshape(N_CHIP, 2, g.shape[0] // N_DEV, g.shape[1])

    def partials(g4s, lands, tag):
        return _chip_partials(g4s, lands, "chip_partials_" + tag)

    (n1, silu1, dsilu1), (wu1,) = _norm_gate(x2, ffn1_norm_g, sc1, sh1, wg1, "ffn1_gate",
                                             comm=_GatherWeights([shard(ffn1_w_up, True)]))
    (gs1, hid1), (wd1,) = _ffn_up_given_gate(n1, wu1, silu1, dsilu1, "ffn1_up",
                                             comm=_GatherWeights([shard(ffn1_w_down, False)]))
    (h1, f1, n2), (win_t,) = _residual_mm(hid1, wd1, x2, g1, 0.5, "ffn1_down", norm=(mix_norm_g, sc2, sh2),
                                          comm=_GatherWeights([shard(w_in, True)]))
    cos, sin_signed = _rope_tables(s, LANES)
    (proj,), (wd2,) = _proj_rope(n2, win_t, cos, sin_signed, aw, "proj",
                                 comm=_GatherWeights([shard(ffn2_w_down, False)]))
    lanes_per = aw // LANES
    (attn, lse), (wg2, wu2) = _attn_seq_fwd(
        proj, aw, "attn_fwd", comm=_GatherWeights([shard(ffn2_w_gate, True), shard(ffn2_w_up, True)]))
    (u1,), (wout,) = _conv_fwd(proj, 3 * lanes_per, 4 * lanes_per, conv_w, conv_dw_b, "conv_fwd",
                               comm=_GatherWeights([shard(w_out, False)]))
    post = (attn_out_g, conv_ln_g, conv_ln_b, conv_out_g)
    y, h2, mix, n3 = _mix_out(attn, u1, post, wout, h1, g2, (ffn2_norm_g, sc3, sh3), "mix_out")
    silu3, gs3, hid3 = _ffn_up(n3, wg2, wu2, "ffn2_up")

    dh3, df3, err2, d_final_g, dg3 = _last_mm_loss(hid3, wd2, h2, g3, 0.5, target, final_norm_g.reshape(1, d),
                                                   "ffn2_down_loss")
    loss_part = jnp.zeros((1, LANES), F32).at[0, 0].set(0.5 * jnp.sum(err2) / d)

    da3, db3 = _ffn_bwd_hidden(df3, wd2, silu3, gs3, "ffn2_hidden_bwd")
    g4_a = [split(_mm_tn(da3, n3, "ffn2_dwg")), split(_mm_tn(db3, n3, "ffn2_dwu")), split(_mm_tn(hid3, df3, "ffn2_dwd"))]
    (dh2, dmix, dsh3, dsc3, dgn3, dg2), land_a = _mm_norm_mod_bwd(
        [(da3, wg2), (db3, wu2)], h2, dh3, ffn2_norm_g, sc3, (mix, g2, 1.0), "ffn2_dn_norm3_bwd", tm=256,
        comm=_SiblingExchange(g4_a))
    parts_a = partials(g4_a, land_a, "a")
    g_wout = _mm_tn(y, dmix, "mix_dwout")
    dattn, du1, d_gains, d_ln = _mix_dy_post_bwd(dmix, wout, attn, u1, post, "mix_dy_post_bwd")
    d_attn_g, d_conv_g, d_ln_g, d_ln_b = d_gains[:, :aw], d_gains[:, aw:], d_ln[:, :aw], d_ln[:, aw:]
    dga, dgb, d_taps, d_conv_b = _conv_bwd(proj, 3 * lanes_per, 4 * lanes_per, conv_w, du1, "conv_bwd")
    (dq, dk, dv), sums_a = _attn_seq_bwd(proj, dattn, attn, lse, cos, sin_signed, "attn_bwd",
                                         comm=_ChipExchange(parts_a))
    dproj = jnp.concatenate([dq, dk, dv, dga, dgb], axis=1)
    g4_b = [split(g_wout), split(_mm_tn(dproj, n2, "mix_dwin"))]
    (dh1, df1, dsh2, dsc2, dgn2, dg1), land_b = _mm_norm_mod_bwd(
        [(dproj, win_t)], h1, dh2, mix_norm_g, sc2, (f1, g1, 0.5), "mix_dn_norm2_bwd", tm=512,
        comm=_SiblingExchange(g4_b))
    parts_b = partials(g4_b, land_b, "b")
    g4_c = [split(_mm_tn(hid1, df1, "ffn1_dwd"))]
    (da1, db1), both = _ffn_bwd_hidden(df1, wd1, silu1, gs1, "ffn1_hidden_bwd",
                                       comm=_Together(_ChipExchange(parts_b), _SiblingExchange(g4_c)))
    sums_b, land_c = both[:2], both[2:]
    parts_c = partials(g4_c, land_c, "c")
    g_wu1, sums_c = _mm_tn(db1, n1, "ffn1_dwu", comm=_ChipExchange(parts_c))
    g4_d = [split(g_wu1)]
    g_wg1, land_d = _mm_tn(da1, n1, "ffn1_dwg", comm=_SiblingExchange(g4_d))
    parts_d = partials(g4_d, land_d, "d")
    g4_e = [split(g_wg1)]
    dn1, both = _plain_mm([(da1, wg1), (db1, wu1)], BF16, False, d, "ffn1_dn",
                          comm=_Together(_ChipExchange(parts_d), _SiblingExchange(g4_e)))
    sums_d, land_e = both[:1], both[1:]
    parts_e = partials(g4_e, land_e, "e")
    (dx, dsh1, dsc1, dgn1), sums_e = _norm_mod_bwd(dn1, x2, dh1, ffn1_norm_g, sc1, "norm1_bwd",
                                                   comm=_ChipExchange(parts_e))

    dmod = jnp.concatenate([dsh1, dsc1, dg1, dsh2, dsc2, dg2, dsh3, dsc3, dg3], axis=1)
    small = [dmod, dgn1, dgn2, dgn3, d_final_g, d_conv_b, d_ln_g, d_ln_b, d_attn_g, d_conv_g,
             d_taps.reshape(1, CONV_KERNEL * aw), loss_part]
    sizes = [v.shape[1] for v in small]
    total = sum(sizes)
    padded = -(-total // (8 * LANES)) * (8 * LANES)
    packed = jnp.concatenate(small + [jnp.zeros((1, padded - total), F32)], axis=1).reshape(8, padded // 8)
    gathered = _ag_small(packed, "ag_small_grads")
    summed = _sum_blocks(gathered, N_DEV, "sum_small_grads").reshape(1, padded)
    offs = [sum(sizes[:i]) for i in range(len(sizes))]
    (g_b_ada, g_gn1, g_gn2, g_gn3, g_final, g_conv_b, g_ln_g, g_ln_b, g_attn_g, g_conv_g, g_taps, loss_row) = [
        summed[:, o:o + n] for o, n in zip(offs, sizes)]
    loss = loss_row[0, 0]
    g_taps_shard = lax.dynamic_slice_in_dim(g_taps.reshape(CONV_KERNEL, aw), me * cw_shard, cw_shard, axis=1)
    dmod_all = gathered.reshape(N_DEV, padded)[:, :n_mod * d]
    dmod_cols = lax.dynamic_slice_in_dim(dmod_all, me * mod_cols, mod_cols, axis=1)
    g_w_ada = _mm_tn(silu_c, dmod_cols, "ada_dw")

    arrived = dict(zip(["ffn2_w_gate", "ffn2_w_up", "ffn2_w_down", "w_out", "w_in", "ffn1_w_down", "ffn1_w_up",
                        "ffn1_w_gate"], list(sums_a) + list(sums_b) + list(sums_c) + list(sums_d) + list(sums_e)))
    transposed = ("ffn1_w_gate", "ffn1_w_up", "w_in", "ffn2_w_gate", "ffn2_w_up")
    grads = {
        "w_ada": g_w_ada, "b_ada": g_b_ada, "ffn1_norm_g": g_gn1, "mix_norm_g": g_gn2, "conv_dw_w": g_taps_shard,
        "conv_dw_b": g_conv_b, "conv_ln_g": g_ln_g, "conv_ln_b": g_ln_b, "attn_out_g": g_attn_g,
        "conv_out_g": g_conv_g, "ffn2_norm_g": g_gn3, "final_norm_g": g_final,
    }
    weights = dict(w_ada=w_ada, b_ada=b_ada, ffn1_norm_g=ffn1_norm_g, ffn1_w_gate=ffn1_w_gate, ffn1_w_up=ffn1_w_up, ffn1_w_down=ffn1_w_down, mix_norm_g=mix_norm_g, w_in=w_in, conv_dw_w=conv_dw_w, conv_dw_b=conv_dw_b, conv_ln_g=conv_ln_g, conv_ln_b=conv_ln_b, attn_out_g=attn_out_g, conv_out_g=conv_out_g, w_out=w_out, ffn2_norm_g=ffn2_norm_g, ffn2_w_gate=ffn2_w_gate, ffn2_w_up=ffn2_w_up, ffn2_w_down=ffn2_w_down, final_norm_g=final_norm_g)
    moms = dict(w_ada=m_w_ada, b_ada=m_b_ada, ffn1_norm_g=m_ffn1_norm_g, ffn1_w_gate=m_ffn1_w_gate, ffn1_w_up=m_ffn1_w_up, ffn1_w_down=m_ffn1_w_down, mix_norm_g=m_mix_norm_g, w_in=m_w_in, conv_dw_w=m_conv_dw_w, conv_dw_b=m_conv_dw_b, conv_ln_g=m_conv_ln_g, conv_ln_b=m_conv_ln_b, attn_out_g=m_attn_out_g, conv_out_g=m_conv_out_g, w_out=m_w_out, ffn2_norm_g=m_ffn2_norm_g, ffn2_w_gate=m_ffn2_w_gate, ffn2_w_up=m_ffn2_w_up, ffn2_w_down=m_ffn2_w_down, final_norm_g=m_final_norm_g)
    vars_ = dict(w_ada=v_w_ada, b_ada=v_b_ada, ffn1_norm_g=v_ffn1_norm_g, ffn1_w_gate=v_ffn1_w_gate, ffn1_w_up=v_ffn1_w_up, ffn1_w_down=v_ffn1_w_down, mix_norm_g=v_mix_norm_g, w_in=v_w_in, conv_dw_w=v_conv_dw_w, conv_dw_b=v_conv_dw_b, conv_ln_g=v_conv_ln_g, conv_ln_b=v_conv_ln_b, attn_out_g=v_attn_out_g, conv_out_g=v_conv_out_g, w_out=v_w_out, ffn2_norm_g=v_ffn2_norm_g, ffn2_w_gate=v_ffn2_w_gate, ffn2_w_up=v_ffn2_w_up, ffn2_w_down=v_ffn2_w_down, final_norm_g=v_final_norm_g)
    names = list(weights)
    big = ["w_ada", "ffn1_w_gate", "ffn1_w_up", "ffn1_w_down", "w_in", "w_out", "ffn2_w_gate", "ffn2_w_up",
           "ffn2_w_down"]
    shape2 = {n: (weights[n].shape[-2] if weights[n].ndim > 1 else 1, weights[n].shape[-1]) for n in names}
    shape2["conv_dw_w"] = (CONV_KERNEL, cw_shard)
    g_out, d_out, m_out, v_out = {}, {}, {}, {}
    for n in big:
        if n in arrived:
            def view(t, n=n):
                return t[0].T if n in transposed else t[0]
            res = _adamw_reduced(view(weights[n]), arrived[n], view(moms[n]), view(vars_[n]), "adamw_" + n)
            g_out[n], d_out[n], m_out[n], v_out[n] = [r.T if n in transposed else r for r in res]
        else:
            g2d = grads[n].reshape(shape2[n])
            res = _adamw_big(weights[n].reshape(shape2[n]), g2d, moms[n].reshape(shape2[n]),
                             vars_[n].reshape(shape2[n]), "adamw_" + n)
            g_out[n], (d_out[n], m_out[n], v_out[n]) = g2d, res
    rest = [n for n in names if n not in big]
    res = _adamw_small([weights[n].reshape(shape2[n]) for n in rest], [grads[n].reshape(shape2[n]) for n in rest],
                       [moms[n].reshape(shape2[n]) for n in rest], [vars_[n].reshape(shape2[n]) for n in rest],
                       "adamw_small")
    for i, n in enumerate(rest):
        g_out[n], d_out[n], m_out[n], v_out[n] = grads[n], res[0][i], res[1][i], res[2][i]

    def shaped(table):
        return [table[n].reshape(weights[n].shape) for n in names]

    return (loss, dx.reshape(x.shape), *shaped(g_out), *shaped(d_out), *shaped(m_out), *shaped(v_out))
```

```python
import functools

import jax
import jax.numpy as jnp
from jax import lax
from jax.experimental import pallas as pl
from jax.experimental.pallas import tpu as pltpu

F32 = jnp.float32
BF16 = jnp.bfloat16
MESH = pl.DeviceIdType.MESH
ANY = pl.BlockSpec(memory_space=pl.ANY)

N_DEV = 8
N_CHIP = 4
HEAD_DIM = 64
HALF_HEAD = HEAD_DIM // 2
LANES = 128
BLOCK = 128
DILATIONS = (1, 4, 16)
MERGE_CHUNK = 512
ROPE_THETA = 10000.0
CONV_KERNEL = 31
CONV_HALO = 32
CONV_CHUNK = 512
CONV_SUB = 128
RMS_EPS = 1e-6
LN_EPS = 1e-5
ADAM_LR = 0.001
ADAM_B1 = 0.9
ADAM_B2 = 0.999
ADAM_EPS = 1e-08
ADAM_WD = 0.01
ADAM_STEP = 10
VMEM_LIMIT = 56 * 1024 * 1024
NEG = -1e30


def _params(n_axes):
    return pltpu.CompilerParams(dimension_semantics=("arbitrary",) * n_axes, vmem_limit_bytes=VMEM_LIMIT)


def _tile(n, target, unit):
    best = None
    for t in range(unit, min(n, target) + 1, unit):
        if n % t == 0:
            best = t
    return best if best is not None else n


def _sigmoid(x):
    return 0.5 * (jnp.tanh(0.5 * x) + 1.0)


def _call(body, *, grid, in_specs, out_specs, out_shape, args, name, scratch_shapes=(), comm=None):
    params = _params(len(grid))
    if comm is None:
        return pl.pallas_call(body, grid=grid, in_specs=list(in_specs), out_specs=list(out_specs),
                              out_shape=list(out_shape), scratch_shapes=list(scratch_shapes),
                              compiler_params=params, name=name)(*args)
    n_in, n_out, n_scr = len(args), len(out_shape), len(scratch_shapes)
    c_in, c_out = len(comm.inputs), len(comm.out_shapes)
    steps = 1
    for g in grid:
        steps *= g

    def hosted(*refs):
        pos = 0
        parts = []
        for size in (n_in, c_in, n_out, c_out, n_scr, len(comm.scratch)):
            parts.append(refs[pos:pos + size])
            pos += size
        ins, cin, outs, cout, scr, cscr = parts
        step = 0
        for axis, g in enumerate(grid):
            step = step * g + pl.program_id(axis)

        @pl.when(step == 0)
        def _():
            comm.start(cin, cout, cscr)

        body(*ins, *outs, *scr)
        if comm.mid is not None and steps >= 4:
            @pl.when(step == steps // 2)
            def _():
                comm.mid(cin, cout, cscr)

        @pl.when(step == steps - 1)
        def _():
            if comm.mid is not None and steps < 4:
                comm.mid(cin, cout, cscr)
            comm.finish(cin, cout, cscr)

    res = pl.pallas_call(
        hosted, grid=grid, in_specs=list(in_specs) + [ANY] * c_in, out_specs=list(out_specs) + [ANY] * c_out,
        out_shape=list(out_shape) + list(comm.out_shapes), scratch_shapes=list(scratch_shapes) + list(comm.scratch),
        compiler_params=params, name=name)(*args, *comm.inputs)
    return res[:n_out], res[n_out:]


def _rows(fn, rows_in, vecs_in, rows_out, vecs_out, *, tile, name, comm=None):
    norm = [r if isinstance(r, tuple) else (r, r.shape[1], 0) for r in rows_in]
    n_rows = norm[0][0].shape[0]
    n_tiles = n_rows // tile
    in_specs, args = [], []
    for arr, width, cb in norm:
        in_specs.append(pl.BlockSpec((tile, width), functools.partial(lambda i, cb: (i, cb), cb=cb)))
        args.append(arr)
    for v in vecs_in:
        in_specs.append(pl.BlockSpec((1, v.shape[1]), lambda i: (0, 0)))
        args.append(v)
    out_shape = [jax.ShapeDtypeStruct((n_rows, w), dt) for w, dt in rows_out]
    out_shape += [jax.ShapeDtypeStruct((1, w), F32) for w in vecs_out]
    out_specs = [pl.BlockSpec((tile, w), lambda i: (i, 0)) for w, _ in rows_out]
    out_specs += [pl.BlockSpec((1, w), lambda i: (0, 0)) for w in vecs_out]
    n_in, n_ro = len(args), len(rows_out)

    def body(*refs):
        vals = [r[...] for r in refs[:n_in]]
        outs = refs[n_in:]
        row_vals, vec_vals = fn(*vals)
        for ref, val in zip(outs[:n_ro], row_vals):
            if isinstance(val, tuple):
                w = val[0].shape[1]
                for j, piece in enumerate(val):
                    ref[:, j * w:(j + 1) * w] = piece.astype(ref.dtype)
            else:
                ref[...] = val.astype(ref.dtype)
        if vecs_out:
            @pl.when(pl.program_id(0) == 0)
            def _():
                for ref in outs[n_ro:]:
                    ref[...] = jnp.zeros_like(ref)
            for ref, val in zip(outs[n_ro:], vec_vals):
                ref[...] += val

    return _call(body, grid=(n_tiles,), in_specs=in_specs, out_specs=out_specs, out_shape=out_shape, args=args,
                 name=name, comm=comm)


def _colsum(x):
    return jnp.sum(x, axis=0, keepdims=True)


def _rms_stats(h):
    r = lax.rsqrt(jnp.mean(h * h, axis=-1, keepdims=True) + RMS_EPS)
    return r, h * r


def _rms_back(r, xn, dxn):
    return r * (dxn - xn * jnp.mean(dxn * xn, axis=-1, keepdims=True))


def _branch_back(dh, f, gate, coef):
    return (coef * gate) * dh, coef * _colsum(f.astype(F32) * dh)


def _norm_mod_back(dn, h, dh_in, gain, scale):
    dn = dn.astype(F32)
    r, xn = _rms_stats(h)
    y = xn * gain
    dy = dn * (1.0 + scale)
    dh = dh_in + _rms_back(r, xn, dy * gain)
    return dh, [_colsum(dn), _colsum(dn * y), _colsum(dy * xn)]


def _norm_mod_bwd(dn, h, dh_in, gain, scale, name, comm=None):
    d = h.shape[1]

    def fn(dn, h, dh_in, gain, scale):
        dh, vecs = _norm_mod_back(dn, h, dh_in, gain, scale)
        return [dh], vecs
    return _rows(fn, [dn, h, dh_in], [gain, scale], [(d, F32)], [d, d, d], tile=256, name=name, comm=comm)


def _mm_norm_mod_bwd(pairs, h, dh_in, gain, scale, branch, name, tm, comm=None):
    f, gate, coef = branch

    def epi(accs, ex, vc):
        dh, vecs = _norm_mod_back(accs[0], ex[0], ex[1], vc[0], vc[1])
        df, dgate = _branch_back(dh, ex[2], vc[2], coef)
        return [dh, df] + vecs + [dgate]
    return _mm([pairs], epi, [h, dh_in, f], [gain, scale, gate], [F32, BF16], trans_rhs=False, tm=tm,
               tn=h.shape[1], name=name, n_sums=4, comm=comm)


def _last_mm_loss(lhs, w, res, gate, coef, target, gain, name):
    d = w.shape[1]

    def epi(accs, ex, vc):
        f = accs[0]
        h = ex[0] + (coef * vc[0]) * f
        r, xn = _rms_stats(h)
        err = xn * vc[1] - ex[1]
        dout = err * (1.0 / d)
        dh = _rms_back(r, xn, dout * vc[1])
        df, dgate = _branch_back(dh, f, vc[0], coef)
        return [dh, df, _colsum(err * err), _colsum(dout * xn), dgate]
    return _mm([[(lhs, w)]], epi, [res, target], [gate, gain], [F32, BF16], trans_rhs=False, tm=256, tn=d,
               name=name, n_sums=3)


def _partner(x):
    if x.shape[1] > LANES:
        return jnp.concatenate([_partner(x[:, c:c + LANES]) for c in range(0, x.shape[1], LANES)], axis=1)
    lane = lax.broadcasted_iota(jnp.int32, x.shape, 1) % HEAD_DIM
    return jnp.where(lane < HALF_HEAD, pltpu.roll(x, LANES - HALF_HEAD, 1), pltpu.roll(x, HALF_HEAD, 1))


def _proj_rope(n, w_t, cos, sin_signed, width, name, comm=None):
    s, kdim = n.shape
    n_cols = w_t.shape[0]
    tm = _tile(s, 1024, 8)
    qscale = HEAD_DIM ** -0.5

    chunk = _tile(tm, 256, 8)

    def body(n_ref, w_ref, cos_ref, sin_ref, o_ref):
        j = pl.program_id(0)

        def products(rows):
            return lax.dot_general(n_ref[rows, :].astype(BF16), w_ref[...].astype(BF16), (((1,), (1,)), ((), ())),
                                   preferred_element_type=F32)

        @pl.when(j >= 2)
        def _():
            for c in range(tm // chunk):
                rows = slice(c * chunk, (c + 1) * chunk)
                o_ref[rows, :] = products(rows)

        @pl.when(j < 2)
        def _():
            scale = jnp.where(j == 0, qscale, 1.0)
            for c in range(tm // chunk):
                rows = slice(c * chunk, (c + 1) * chunk)
                acc = products(rows)
                cos = jnp.tile(cos_ref[rows, :], (1, width // LANES))
                sin = jnp.tile(sin_ref[rows, :], (1, width // LANES))
                o_ref[rows, :] = scale * (acc * cos + _partner(acc) * sin)

    table = pl.BlockSpec((tm, LANES), lambda j, i: (jnp.where(j < 2, i, 0), 0))
    return _call(
        body, grid=(n_cols // width, s // tm),
        in_specs=[pl.BlockSpec((tm, kdim), lambda j, i: (i, 0)), pl.BlockSpec((width, kdim), lambda j, i: (j, 0)),
                  table, table],
        out_specs=[pl.BlockSpec((tm, width), lambda j, i: (i, j))],
        out_shape=[jax.ShapeDtypeStruct((s, n_cols), F32)], args=(n, w_t, cos, sin_signed), name=name, comm=comm)


def _mix_post(attn, u1, attn_g, ln_g, ln_b, conv_g):
    _, xa = _rms_stats(attn)
    mu = jnp.mean(u1, axis=-1, keepdims=True)
    xc = u1 - mu
    rstd = lax.rsqrt(jnp.mean(xc * xc, axis=-1, keepdims=True) + LN_EPS)
    u2 = (xc * rstd) * ln_g + ln_b
    u3 = u2 * _sigmoid(u2)
    _, x3 = _rms_stats(u3)
    return jnp.concatenate([xa * attn_g, x3 * conv_g], axis=1)


def _mix_post_back(dy, attn, u1, attn_g, ln_g, ln_b, conv_g):
    w = attn.shape[1]
    dya, dyc = dy[:, :w], dy[:, w:]
    ra, xa = _rms_stats(attn)
    dattn = _rms_back(ra, xa, dya * attn_g)
    mu = jnp.mean(u1, axis=-1, keepdims=True)
    xc = u1 - mu
    rstd = lax.rsqrt(jnp.mean(xc * xc, axis=-1, keepdims=True) + LN_EPS)
    xh = xc * rstd
    u2 = xh * ln_g + ln_b
    sig = _sigmoid(u2)
    u3 = u2 * sig
    r3, x3 = _rms_stats(u3)
    du3 = _rms_back(r3, x3, dyc * conv_g)
    du2 = du3 * (sig + u3 * (1.0 - sig))
    dxh = du2 * ln_g
    du1 = rstd * (dxh - jnp.mean(dxh, axis=-1, keepdims=True) - xh * jnp.mean(dxh * xh, axis=-1, keepdims=True))
    return dattn, du1, [_colsum(dya * xa), _colsum(dyc * x3), _colsum(du2 * xh), _colsum(du2)]


def _silu_rows(c_all, name):
    def fn(c):
        return [c * _sigmoid(c)], []
    return _rows(fn, [c_all], [], [(c_all.shape[1], BF16)], [], tile=c_all.shape[0], name=name)[0]


def _mm(groups, epi, extras, vecs, outs, *, trans_rhs, tm, tn, name, n_sums=0, pre=None, pre_inputs=(),
        comm=None):
    m = (pre_inputs[0] if pre is not None else groups[0][0][0]).shape[0]
    n = groups[0][0][1].shape[0] if trans_rhs else groups[0][0][1].shape[1]
    tm, tn = min(tm, m), min(tn, n)
    in_specs, args, uses_pre = [], [], []
    for grp in groups:
        for lhs, rhs in grp:
            k = rhs.shape[1] if trans_rhs else rhs.shape[0]
            uses_pre.append(lhs is None)
            if lhs is not None:
                in_specs.append(pl.BlockSpec((tm, k), lambda j, i: (i, 0)))
                args.append(lhs)
            in_specs.append(pl.BlockSpec((tn, k), lambda j, i: (j, 0)) if trans_rhs
                            else pl.BlockSpec((k, tn), lambda j, i: (0, j)))
            args.append(rhs)
    n_mm = len(args)
    for p in pre_inputs:
        in_specs.append(pl.BlockSpec((tm, p.shape[1]), lambda j, i: (i, 0)))
        args.append(p)
    for e in extras:
        in_specs.append(pl.BlockSpec((tm, tn), lambda j, i: (i, j)) if e.shape[1] == n
                        else pl.BlockSpec((tm, e.shape[1]), lambda j, i: (i, 0)))
        args.append(e)
    for v in vecs:
        in_specs.append(pl.BlockSpec((1, tn), lambda j, i: (0, j)) if v.shape[1] == n
                        else pl.BlockSpec((1, v.shape[1]), lambda j, i: (0, 0)))
        args.append(v)
    sizes = [len(g) for g in groups]
    n_pre, n_ex, n_vec = len(pre_inputs), len(extras), len(vecs)
    dims = (((1,), (1,)), ((), ())) if trans_rhs else (((1,), (0,)), ((), ()))
    out_specs, out_shape = [], []
    if pre is not None:
        k_pre = args[n_mm - 1].shape[1] if trans_rhs else args[n_mm - 1].shape[0]
        out_specs.append(pl.BlockSpec((tm, k_pre), lambda j, i: (i, 0)))
        out_shape.append(jax.ShapeDtypeStruct((m, k_pre), BF16))
    for o in outs:
        dt, width = o if isinstance(o, tuple) else (o, n)
        out_specs.append(pl.BlockSpec((tm, tn), lambda j, i: (i, j)) if width == n
                         else pl.BlockSpec((tm, width), lambda j, i: (i, 0)))
        out_shape.append(jax.ShapeDtypeStruct((m, width), dt))
    n_tiles_out = len(out_specs)
    out_specs += [pl.BlockSpec((1, tn), lambda j, i: (0, j))] * n_sums
    out_shape += [jax.ShapeDtypeStruct((1, n), F32)] * n_sums

    def body(*refs):
        ins = refs[:n_mm + n_pre + n_ex + n_vec]
        out_refs = refs[n_mm + n_pre + n_ex + n_vec:]
        vc = [r[...] for r in ins[n_mm + n_pre + n_ex:]]
        vals = []
        made = None
        if pre is not None:
            made = pre([r[...] for r in ins[n_mm:n_mm + n_pre]], vc).astype(BF16)
            vals.append(made)
        accs, pos, pair = [], 0, 0
        for size in sizes:
            acc = None
            for _ in range(size):
                if uses_pre[pair]:
                    lhs_tile = made
                else:
                    lhs_tile = ins[pos][...].astype(BF16)
                    pos += 1
                part = lax.dot_general(lhs_tile, ins[pos][...].astype(BF16), dims, preferred_element_type=F32)
                acc = part if acc is None else acc + part
                pos += 1
                pair += 1
            accs.append(acc)
        ex = [r[...] for r in ins[n_mm + n_pre:n_mm + n_pre + n_ex]]
        vals += epi(accs, ex, vc)
        for ref, val in zip(out_refs[:n_tiles_out], vals):
            ref[...] = val.astype(ref.dtype)
        if n_sums:
            @pl.when(pl.program_id(1) == 0)
            def _():
                for ref in out_refs[n_tiles_out:]:
                    ref[...] = jnp.zeros_like(ref)
            for ref, val in zip(out_refs[n_tiles_out:], vals[n_tiles_out:]):
                ref[...] += val

    return _call(body, grid=(n // tn, m // tm), in_specs=in_specs, out_specs=out_specs, out_shape=out_shape,
                 args=args, name=name, comm=comm)


def _mm_tn(lhs, rhs, name, comm=None):
    t, a = lhs.shape
    b = rhs.shape[1]
    ta = a if a <= 1536 else _tile(a, 1536, LANES)
    tk = _tile(t, 2048, 8)

    def body(l_ref, r_ref, o_ref):
        @pl.when(pl.program_id(1) == 0)
        def _():
            o_ref[...] = jnp.zeros_like(o_ref)
        o_ref[...] += lax.dot_general(l_ref[...].astype(BF16), r_ref[...].astype(BF16), (((0,), (0,)), ((), ())),
                                      preferred_element_type=F32)

    res = _call(body, grid=(a // ta, t // tk),
                in_specs=[pl.BlockSpec((tk, ta), lambda i, k: (k, i)), pl.BlockSpec((tk, b), lambda i, k: (k, 0))],
                out_specs=[pl.BlockSpec((ta, b), lambda i, k: (i, 0))], out_shape=[jax.ShapeDtypeStruct((a, b), F32)],
                args=(lhs, rhs), name=name, comm=comm)
    return res[0] if comm is None else (res[0][0], res[1])


def _ffn_tn(f):
    return _tile(f, 1536, LANES)


def _swiglu_parts(a, b):
    sig = _sigmoid(a)
    silu = a * sig
    return [silu, b * (sig + silu * (1.0 - sig)), silu * b]


def _ffn_up(n, wg_t, wu_t, name, comm=None):
    def epi(accs, ex, vc):
        return _swiglu_parts(accs[0], accs[1])
    return _mm([[(n, wg_t)], [(n, wu_t)]], epi, [], [], [BF16, BF16, BF16], trans_rhs=True, tm=512,
               tn=_ffn_tn(wg_t.shape[0]), name=name, comm=comm)


def _norm_gate(h, gain, scale, shift, wg_t, name, comm=None):
    def pre(tiles, vc):
        _, xn = _rms_stats(tiles[0])
        return (xn * vc[0]) * (1.0 + vc[1]) + vc[2]

    def epi(accs, ex, vc):
        return [accs[0]]
    return _mm([[(None, wg_t)]], epi, [], [gain, scale, shift], [BF16], trans_rhs=True, tm=512,
               tn=wg_t.shape[0], name=name, pre=pre, pre_inputs=[h], comm=comm)


def _mix_out(attn, u1, post, w, res, gate, norm, name):
    def pre(tiles, vc):
        return _mix_post(tiles[0], tiles[1], *vc[4:8])

    def epi(accs, ex, vc):
        h = ex[0] + vc[0] * accs[0]
        _, xn = _rms_stats(h)
        return [h, accs[0], (xn * vc[1]) * (1.0 + vc[2]) + vc[3]]
    return _mm([[(None, w)]], epi, [res], [gate] + list(norm) + list(post), [F32, BF16, BF16], trans_rhs=False,
               tm=512, tn=w.shape[1], name=name, pre=pre, pre_inputs=[attn, u1])


def _mix_dy_post_bwd(dmix, w, attn, u1, post, name):
    width = attn.shape[1]

    def epi(accs, ex, vc):
        dattn, du1, sums = _mix_post_back(accs[0], ex[0], ex[1], *vc)
        return [dattn, du1, jnp.concatenate(sums[0:2], axis=1), jnp.concatenate(sums[2:4], axis=1)]
    return _mm([[(dmix, w)]], epi, [attn, u1], list(post), [(F32, width), (F32, width)], trans_rhs=True, tm=256,
               tn=w.shape[0], name=name, n_sums=2)


def _ffn_up_given_gate(n, wu_t, a, name, comm=None):
    def epi(accs, ex, vc):
        return _swiglu_parts(ex[0].astype(F32), accs[0])
    return _mm([[(n, wu_t)]], epi, [a], [], [BF16, BF16, BF16], trans_rhs=True, tm=512,
               tn=_ffn_tn(wu_t.shape[0]), name=name, comm=comm)


def _residual_mm(lhs, w, res, gate, coef, name, norm=None, comm=None):
    def epi(accs, ex, vc):
        h = ex[0] + (coef * vc[0]) * accs[0]
        if norm is None:
            return [h, accs[0]]
        _, xn = _rms_stats(h)
        return [h, accs[0], (xn * vc[1]) * (1.0 + vc[2]) + vc[3]]
    vecs = [gate] + (list(norm) if norm is not None else [])
    outs = [F32, BF16] + ([BF16] if norm is not None else [])
    return _mm([[(lhs, w)]], epi, [res], vecs, outs, trans_rhs=False, tm=512, tn=w.shape[1], name=name, comm=comm)


def _ffn_bwd_hidden(df, wd, dhid_db, dhid_da, name, comm=None):
    def epi(accs, ex, vc):
        return [accs[0] * ex[1].astype(F32), accs[0] * ex[0].astype(F32)]
    return _mm([[(df, wd)]], epi, [dhid_db, dhid_da], [], [BF16, BF16], trans_rhs=True, tm=512,
               tn=_ffn_tn(wd.shape[0]), name=name, comm=comm)


def _plain_mm(pairs, out_dtype, trans_rhs, tn, name, tm=512, comm=None):
    def epi(accs, ex, vc):
        return [accs[0]]
    res = _mm([pairs], epi, [], [], [out_dtype], trans_rhs=trans_rhs, tm=tm, tn=tn, name=name, comm=comm)
    return res[0] if comm is None else (res[0][0], res[1])


HEADS_PER_TILE = LANES // HEAD_DIM


def _stack_heads(x):
    lane = lax.broadcasted_iota(jnp.int32, (1, LANES), 1)
    return jnp.concatenate([x * (lane // HEAD_DIM == h).astype(F32) for h in range(HEADS_PER_TILE)], axis=0)


def _unstack_heads(y):
    r = y.shape[0] // HEADS_PER_TILE
    lane = lax.broadcasted_iota(jnp.int32, (r, y.shape[1]), 1)
    out = y[0:r]
    for h in range(1, HEADS_PER_TILE):
        out = jnp.where(lane // HEAD_DIM == h, y[h * r:(h + 1) * r], out)
    return out


def _stacked_lse(lb):
    return jnp.concatenate([_lane_pick(lb, h) for h in range(HEADS_PER_TILE)], axis=0)


def _band_masks(n_row_blocks, n_col_blocks):
    shape = (n_row_blocks * BLOCK, n_col_blocks * BLOCK)
    qi = lax.broadcasted_iota(jnp.int32, shape, 0) % BLOCK
    kj = lax.broadcasted_iota(jnp.int32, shape, 1) % BLOCK
    return kj <= qi, kj >= qi


def _query_masks():
    first_valid, _ = _band_masks(HEADS_PER_TILE, 1)
    same_ok, before_ok = _band_masks(HEADS_PER_TILE, 2)
    is_cur = lax.broadcasted_iota(jnp.int32, same_ok.shape, 1) >= BLOCK
    return first_valid, jnp.logical_and(is_cur, same_ok), jnp.logical_and(jnp.logical_not(is_cur), before_ok)


def _dot_nt(a, b):
    return lax.dot_general(a.astype(BF16), b.astype(BF16), (((1,), (1,)), ((), ())), preferred_element_type=F32)


def _dot_nn(a, b):
    return lax.dot_general(a.astype(BF16), b.astype(BF16), (((1,), (0,)), ((), ())), preferred_element_type=F32)


def _dot_tn(a, b):
    return lax.dot_general(a.astype(BF16), b.astype(BF16), (((0,), (0,)), ((), ())), preferred_element_type=F32)


def _lane_pick(x, h):
    lane = lax.broadcasted_iota(jnp.int32, x.shape, 1)
    return jnp.sum(jnp.where(lane == h * HEAD_DIM, x, 0.0), axis=1, keepdims=True)


def _block_rows(idx, d):
    span = BLOCK * d
    q0 = (idx // d) * span + idx % d
    return pl.ds(q0, BLOCK, stride=d), pl.ds(q0 - span, BLOCK, stride=d)


def _branch_loops(n_blocks, d, visit, unroll, masks):
    first_valid, cur_part, prev_part = masks
    if d % unroll == 0 and (n_blocks - d) % unroll == 0:
        full_valid = jnp.logical_or(cur_part, prev_part)

        def first(idx, carry):
            rows = pl.ds(idx, BLOCK, stride=d)
            visit(rows, [rows], first_valid)
            return carry

        def rest(idx, carry):
            rows, prev = _block_rows(idx, d)
            visit(rows, [prev, rows], full_valid)
            return carry

        lax.fori_loop(0, d, first, 0, unroll=unroll)
        lax.fori_loop(d, n_blocks, rest, 0, unroll=unroll)
        return

    def every(idx, carry):
        span = BLOCK * d
        q0 = (idx // d) * span + idx % d
        has_prev = idx >= d
        rows = pl.ds(q0, BLOCK, stride=d)
        prev = pl.ds(jnp.where(has_prev, q0 - span, q0), BLOCK, stride=d)
        visit(rows, [prev, rows], jnp.logical_or(cur_part, jnp.logical_and(prev_part, has_prev)))
        return carry

    lax.fori_loop(0, n_blocks, every, 0, unroll=unroll)


def _qkv_specs(s, tiles):
    q, k, v = [pl.BlockSpec((s, LANES), functools.partial(lambda hb, off: (0, off + hb), off=i * tiles))
               for i in range(3)]
    return q, k, v, pl.BlockSpec((s, LANES), lambda hb: (0, hb))


def _attn_seq_fwd(proj, width, name, comm=None):
    s = proj.shape[0]
    q_spec, k_spec, v_spec, cur = _qkv_specs(s, width // LANES)

    def body(q_ref, k_ref, v_ref, o_ref, l_ref, o_s, l_s):
        masks = _query_masks()
        for bi, d in enumerate(DILATIONS):
            def visit(rows, key_rows, valid, bi=bi):
                q2 = _stack_heads(q_ref[rows, :])
                keys = jnp.concatenate([k_ref[r, :] for r in key_rows], axis=0)
                vals = jnp.concatenate([v_ref[r, :] for r in key_rows], axis=0)
                sc = jnp.where(valid, _dot_nt(q2, keys), NEG)
                mx = jnp.max(sc, axis=1, keepdims=True)
                p = jnp.exp(sc - mx)
                den = jnp.sum(p, axis=1, keepdims=True)
                o_s[bi, rows, :] = _unstack_heads(_dot_nn(p, vals) / den)
                l_s[bi, rows, :] = _unstack_heads(jnp.broadcast_to(mx + jnp.log(den), (q2.shape[0], LANES)))

            _branch_loops(s // BLOCK, d, visit, 8, masks)
        for c in range(s // MERGE_CHUNK):
            rows = slice(c * MERGE_CHUNK, (c + 1) * MERGE_CHUNK)
            ls = [l_s[bi, rows, :] for bi in range(len(DILATIONS))]
            top = functools.reduce(jnp.maximum, ls)
            ws = [jnp.exp(l - top) for l in ls]
            den = functools.reduce(lambda a, b: a + b, ws)
            num = functools.reduce(lambda a, b: a + b, [w * o_s[bi, rows, :] for bi, w in enumerate(ws)])
            o_ref[rows, :] = num / den
            l_ref[rows, :] = top + jnp.log(den)

    return _call(
        body, grid=(width // LANES,), in_specs=[q_spec, k_spec, v_spec], out_specs=[cur, cur],
        out_shape=[jax.ShapeDtypeStruct((s, width), F32)] * 2,
        scratch_shapes=[pltpu.VMEM((len(DILATIONS), s, LANES), F32)] * 2,
        args=(proj, proj, proj), name=name, comm=comm)


def _attn_seq_bwd(proj, do, o, lse, cos, sin_signed, name, comm=None):
    s, width = do.shape
    q_spec, k_spec, v_spec, cur = _qkv_specs(s, width // LANES)
    table = pl.BlockSpec((s, LANES), lambda hb: (0, 0))
    qscale = HEAD_DIM ** -0.5

    def body(q_ref, k_ref, v_ref, do_ref, o_ref, l_ref, cos_ref, sin_ref, dq_out, dk_out, dv_out,
             dq_ref, dk_ref, dv_ref):
        dq_ref[...] = jnp.zeros_like(dq_ref)
        dk_ref[...] = jnp.zeros_like(dk_ref)
        dv_ref[...] = jnp.zeros_like(dv_ref)
        masks = _query_masks()
        for d in DILATIONS:
            def visit(rows, key_rows, valid):
                dob = do_ref[rows, :]
                q2 = _stack_heads(q_ref[rows, :])
                do2 = _stack_heads(dob)
                delta = jnp.sum(_stack_heads(dob * o_ref[rows, :]), axis=1, keepdims=True)
                lse2 = _stacked_lse(l_ref[rows, :])
                keys = jnp.concatenate([k_ref[r, :] for r in key_rows], axis=0)
                vals = jnp.concatenate([v_ref[r, :] for r in key_rows], axis=0)
                p = jnp.where(valid, jnp.exp(_dot_nt(q2, keys) - lse2), 0.0)
                ds = p * (_dot_nt(do2, vals) - delta)
                dq_ref[rows, :] += _unstack_heads(_dot_nn(ds, keys))
                dkk = _dot_tn(ds, q2)
                dvv = _dot_tn(p, do2)
                for i, r in enumerate(key_rows):
                    dk_ref[r, :] += dkk[i * BLOCK:(i + 1) * BLOCK]
                    dv_ref[r, :] += dvv[i * BLOCK:(i + 1) * BLOCK]

            _branch_loops(s // BLOCK, d, visit, 8, masks)
        for c in range(s // MERGE_CHUNK):
            rows = slice(c * MERGE_CHUNK, (c + 1) * MERGE_CHUNK)
            cos, sin = cos_ref[rows, :], sin_ref[rows, :]
            dq, dk = dq_ref[rows, :], dk_ref[rows, :]
            dq_out[rows, :] = ((dq * cos - _partner(dq) * sin) * qscale).astype(BF16)
            dk_out[rows, :] = (dk * cos - _partner(dk) * sin).astype(BF16)
            dv_out[rows, :] = dv_ref[rows, :].astype(BF16)

    return _call(
        body, grid=(width // LANES,), in_specs=[q_spec, k_spec, v_spec, cur, cur, cur, table, table],
        out_specs=[cur, cur, cur], out_shape=[jax.ShapeDtypeStruct((s, width), BF16)] * 3,
        scratch_shapes=[pltpu.VMEM((s, LANES), F32)] * 3,
        args=(proj, proj, proj, do, o, lse, cos, sin_signed), name=name, comm=comm)


def _conv_specs(s, a_block, b_block):
    per = CONV_CHUNK // CONV_HALO
    a_cur = pl.BlockSpec((CONV_CHUNK, LANES), lambda cb, i: (i, a_block + cb))
    b_cur = pl.BlockSpec((CONV_CHUNK, LANES), lambda cb, i: (i, b_block + cb))
    a_halo = pl.BlockSpec((CONV_HALO, LANES), lambda cb, i: (jnp.maximum(i * per - 1, 0), a_block + cb))
    b_halo = pl.BlockSpec((CONV_HALO, LANES), lambda cb, i: (jnp.maximum(i * per - 1, 0), b_block + cb))
    w_spec = pl.BlockSpec((CONV_KERNEL, LANES), lambda cb, i: (0, cb))
    vec = pl.BlockSpec((1, LANES), lambda cb, i: (0, cb))
    out = pl.BlockSpec((CONV_CHUNK, LANES), lambda cb, i: (i, cb))
    return a_cur, b_cur, a_halo, b_halo, w_spec, vec, out


def _fill_glu_window(win, a_ref, b_ref, ah_ref, bh_ref, first):
    halo = ah_ref[...] * _sigmoid(bh_ref[...])
    win[0:CONV_HALO, :] = jnp.where(first, 0.0, halo)
    win[CONV_HALO:, :] = a_ref[...] * _sigmoid(b_ref[...])


def _conv_fwd(proj, a_block, b_block, w, bias, name, comm=None):
    s = proj.shape[0]
    cw = w.shape[1]
    a_cur, b_cur, a_halo, b_halo, w_spec, vec, out = _conv_specs(s, a_block, b_block)
    lead = CONV_HALO - (CONV_KERNEL - 1)

    def body(a_ref, b_ref, ah_ref, bh_ref, w_ref, bias_ref, o_ref, win):
        _fill_glu_window(win, a_ref, b_ref, ah_ref, bh_ref, pl.program_id(1) == 0)
        for sub in range(CONV_CHUNK // CONV_SUB):
            base = sub * CONV_SUB
            acc = jnp.zeros((CONV_SUB, LANES), F32) + bias_ref[...]
            for j in range(CONV_KERNEL):
                acc = acc + w_ref[j:j + 1, :] * win[base + lead + j:base + lead + j + CONV_SUB, :]
            o_ref[base:base + CONV_SUB, :] = acc

    return _call(
        body, grid=(cw // LANES, s // CONV_CHUNK), in_specs=[a_cur, b_cur, a_halo, b_halo, w_spec, vec],
        out_specs=[out], out_shape=[jax.ShapeDtypeStruct((s, cw), F32)],
        scratch_shapes=[pltpu.VMEM((CONV_CHUNK + CONV_HALO, LANES), F32)],
        args=(proj, proj, proj, proj, w, bias), name=name, comm=comm)


def _conv_bwd(proj, a_block, b_block, w, du1, name):
    s = proj.shape[0]
    cw = w.shape[1]
    a_cur, b_cur, a_halo, b_halo, w_spec, vec, out = _conv_specs(s, a_block, b_block)
    per = CONV_CHUNK // CONV_HALO
    n_chunks = s // CONV_CHUNK
    d_next = pl.BlockSpec((CONV_HALO, LANES), lambda cb, i: (jnp.minimum((i + 1) * per, s // CONV_HALO - 1), cb))
    lead = CONV_HALO - (CONV_KERNEL - 1)

    def body(a_ref, b_ref, ah_ref, bh_ref, w_ref, d_ref, dn_ref, da_ref, db_ref, dw_ref, dbias_ref, win, dwin):
        i = pl.program_id(1)
        _fill_glu_window(win, a_ref, b_ref, ah_ref, bh_ref, i == 0)
        dwin[0:CONV_CHUNK, :] = d_ref[...]
        dwin[CONV_CHUNK:, :] = jnp.where(i == n_chunks - 1, 0.0, dn_ref[...])

        @pl.when(i == 0)
        def _():
            dw_ref[...] = jnp.zeros_like(dw_ref)
            dbias_ref[...] = jnp.zeros_like(dbias_ref)

        dbias_ref[...] += _colsum(d_ref[...])
        for sub in range(CONV_CHUNK // CONV_SUB):
            base = sub * CONV_SUB
            dcur = dwin[base:base + CONV_SUB, :]
            du0 = jnp.zeros((CONV_SUB, LANES), F32)
            for j in range(CONV_KERNEL):
                back = CONV_KERNEL - 1 - j
                du0 = du0 + w_ref[j:j + 1, :] * dwin[base + back:base + back + CONV_SUB, :]
                dw_ref[j:j + 1, :] += _colsum(dcur * win[base + lead + j:base + lead + j + CONV_SUB, :])
            av = a_ref[base:base + CONV_SUB, :]
            sig = _sigmoid(b_ref[base:base + CONV_SUB, :])
            da_ref[base:base + CONV_SUB, :] = (du0 * sig).astype(BF16)
            db_ref[base:base + CONV_SUB, :] = (du0 * av * sig * (1.0 - sig)).astype(BF16)

    return pl.pallas_call(
        body, grid=(cw // LANES, n_chunks), in_specs=[a_cur, b_cur, a_halo, b_halo, w_spec, out, d_next],
        out_specs=[out, out, w_spec, vec],
        out_shape=[jax.ShapeDtypeStruct((s, cw), BF16), jax.ShapeDtypeStruct((s, cw), BF16),
                   jax.ShapeDtypeStruct((CONV_KERNEL, cw), F32), jax.ShapeDtypeStruct((1, cw), F32)],
        scratch_shapes=[pltpu.VMEM((CONV_CHUNK + CONV_HALO, LANES), F32)] * 2,
        compiler_params=_params(2), name=name)(proj, proj, proj, proj, w, du1, du1)


def _adamw_math(w, g, m, v):
    m = ADAM_B1 * m + (1.0 - ADAM_B1) * g
    v = ADAM_B2 * v + (1.0 - ADAM_B2) * (g * g)
    m_hat = m / (1.0 - ADAM_B1 ** ADAM_STEP)
    v_hat = v / (1.0 - ADAM_B2 ** ADAM_STEP)
    delta = -ADAM_LR * (m_hat / (jnp.sqrt(v_hat) + ADAM_EPS) + ADAM_WD * w)
    return delta, m, v


def _adamw_big(w, g, m, v, name):
    rows, cols = w.shape
    tile = _tile(rows, 256, 8)
    spec = pl.BlockSpec((tile, cols), lambda i: (i, 0))

    def body(w_ref, g_ref, m_ref, v_ref, d_out, m_out, v_out):
        d_out[...], m_out[...], v_out[...] = _adamw_math(w_ref[...], g_ref[...], m_ref[...], v_ref[...])

    return pl.pallas_call(body, grid=(rows // tile,), in_specs=[spec] * 4, out_specs=[spec] * 3,
                          out_shape=[jax.ShapeDtypeStruct(w.shape, F32)] * 3, compiler_params=_params(1),
                          name=name)(w, g, m, v)


def _adamw_reduced(w, land, m, v, name):
    rows, cols = w.shape
    tile = _tile(rows, 256, 16)
    spec = pl.BlockSpec((tile, cols), lambda i: (i, 0))

    def body(w_ref, l_ref, m_ref, v_ref, g_out, d_out, m_out, v_out):
        g = l_ref[0].astype(F32)
        for q in range(1, N_CHIP):
            g = g + l_ref[q].astype(F32)
        g_out[...] = g
        d_out[...], m_out[...], v_out[...] = _adamw_math(w_ref[...], g, m_ref[...], v_ref[...])

    return pl.pallas_call(body, grid=(rows // tile,),
                          in_specs=[spec, pl.BlockSpec((N_CHIP, tile, cols), lambda i: (0, i, 0)), spec, spec],
                          out_specs=[spec] * 4, out_shape=[jax.ShapeDtypeStruct(w.shape, F32)] * 4,
                          compiler_params=_params(1), name=name)(w, land, m, v)


def _adamw_small(ws, gs, ms, vs, name):
    n = len(ws)

    def body(*refs):
        ins, outs = refs[:4 * n], refs[4 * n:]
        for t in range(n):
            res = _adamw_math(ins[t][...], ins[n + t][...], ins[2 * n + t][...], ins[3 * n + t][...])
            for j in range(3):
                outs[j * n + t][...] = res[j]

    shapes = [jax.ShapeDtypeStruct(w.shape, F32) for w in ws]
    res = pl.pallas_call(body, out_shape=shapes * 3, compiler_params=pltpu.CompilerParams(vmem_limit_bytes=VMEM_LIMIT),
                         name=name)(*ws, *gs, *ms, *vs)
    return res[:n], res[n:2 * n], res[2 * n:]


def _sum_blocks(x, n_blocks, name):
    r = x.shape[0] // n_blocks

    def body(x_ref, o_ref):
        acc = x_ref[0:r, :]
        for b in range(1, n_blocks):
            acc = acc + x_ref[b * r:(b + 1) * r, :]
        o_ref[...] = acc

    return pl.pallas_call(body, out_shape=jax.ShapeDtypeStruct((r, x.shape[1]), F32),
                          compiler_params=pltpu.CompilerParams(vmem_limit_bytes=VMEM_LIMIT), name=name)(x)


def _coords():
    return lax.axis_index("x"), lax.axis_index("y"), lax.axis_index("c")


def _flip(v, bit):
    return 1 - v if bit else v


def _ag_small(x, name):
    r, c = x.shape

    def body(x_ref, o_ref, send, recv, local_sem):
        mx, my, mc = _coords()

        def rows(px, py, pc):
            return o_ref.at[pl.ds(pl.multiple_of((4 * px + 2 * py + pc) * r, 8), r), :]

        local = pltpu.make_async_copy(x_ref, rows(mx, my, mc), local_sem)
        local.start()
        peers = [(_flip(mx, k >> 2 & 1), _flip(my, k >> 1 & 1), _flip(mc, k & 1)) for k in range(1, N_DEV)]
        sends = [pltpu.make_async_remote_copy(x_ref, rows(mx, my, mc), send.at[k], recv.at[k], device_id=p,
                                              device_id_type=MESH) for k, p in enumerate(peers)]
        for cp in sends:
            cp.start()
        for k, p in enumerate(peers):
            pltpu.make_async_remote_copy(x_ref, rows(*p), send.at[k], recv.at[k], device_id=p,
                                         device_id_type=MESH).wait_recv()
        for cp in sends:
            cp.wait_send()
        local.wait()

    vm = pl.BlockSpec(memory_space=pltpu.VMEM)
    return pl.pallas_call(
        body, in_specs=[vm], out_specs=vm, out_shape=jax.ShapeDtypeStruct((N_DEV * r, c), x.dtype),
        scratch_shapes=[pltpu.SemaphoreType.DMA((N_DEV - 1,)), pltpu.SemaphoreType.DMA((N_DEV - 1,)),
                        pltpu.SemaphoreType.DMA(())],
        name=name)(x)


class _GatherSmall:
    mid = None

    def __init__(self, x):
        self.inputs = [x]
        self.out_shapes = [jax.ShapeDtypeStruct((N_DEV * x.shape[0], x.shape[1]), x.dtype)]
        self.scratch = [pltpu.SemaphoreType.DMA((N_DEV - 1,)), pltpu.SemaphoreType.DMA((N_DEV - 1,)),
                        pltpu.SemaphoreType.DMA(())]

    def _plan(self, x_refs, o_refs, sems):
        send, recv, local_sem = sems
        x_ref, o_ref = x_refs[0], o_refs[0]
        r = x_ref.shape[0]
        mx, my, mc = _coords()

        def rows(px, py, pc):
            return o_ref.at[pl.ds(pl.multiple_of((4 * px + 2 * py + pc) * r, 8), r), :]

        peers = [(_flip(mx, k >> 2 & 1), _flip(my, k >> 1 & 1), _flip(mc, k & 1)) for k in range(1, N_DEV)]
        out = [pltpu.make_async_remote_copy(x_ref, rows(mx, my, mc), send.at[k], recv.at[k], device_id=p,
                                            device_id_type=MESH) for k, p in enumerate(peers)]
        arrivals = [pltpu.make_async_remote_copy(x_ref, rows(*p), send.at[k], recv.at[k], device_id=p,
                                                 device_id_type=MESH) for k, p in enumerate(peers)]
        return out, arrivals, pltpu.make_async_copy(x_ref, rows(mx, my, mc), local_sem)

    def start(self, x_refs, o_refs, sems):
        out, _, local = self._plan(x_refs, o_refs, sems)
        local.start()
        for cp in out:
            cp.start()

    def finish(self, x_refs, o_refs, sems):
        out, arrivals, local = self._plan(x_refs, o_refs, sems)
        for cp in arrivals:
            cp.wait_recv()
        for cp in out:
            cp.wait_send()
        local.wait()


class _GatherWeights:
    def __init__(self, shards):
        n_t = len(shards)
        self.inputs = list(shards)
        self.out_shapes = [jax.ShapeDtypeStruct((N_DEV * x.shape[0], x.shape[1]), x.dtype) for x in shards]
        self.scratch = [pltpu.SemaphoreType.DMA((n_t, 8)), pltpu.SemaphoreType.DMA((n_t, 8)),
                        pltpu.SemaphoreType.DMA((n_t,))]

    def _plan(self, x_refs, o_refs, sems):
        send, recv, local_sem = sems
        mx, my, mc = _coords()
        me, sibling = (mx, my, mc), (mx, my, 1 - mc)
        xn, yn, diag = (1 - mx, my), (mx, 1 - my), (1 - mx, 1 - my)

        def rows(t, chip, core, half=None):
            r = x_refs[t].shape[0]
            base = (4 * chip[0] + 2 * chip[1] + core) * r
            if half is None:
                return o_refs[t].at[pl.ds(pl.multiple_of(base, 8), r), :]
            return o_refs[t].at[pl.ds(pl.multiple_of(base + half * (r // 2), 8), r // 2), :]

        def copy(t, k, block, to, src=None):
            return pltpu.make_async_remote_copy(
                src_ref=block if src is None else src, dst_ref=block,
                send_sem=send.at[t, k], recv_sem=recv.at[t, k], device_id=to, device_id_type=MESH)

        def local(t):
            return pltpu.make_async_copy(x_refs[t], rows(t, (mx, my), mc), local_sem.at[t])

        return (mx, my), mc, me, sibling, xn, yn, diag, rows, copy, local

    def start(self, x_refs, o_refs, sems):
        chip, mc, me, sibling, xn, yn, diag, rows, copy, local = self._plan(x_refs, o_refs, sems)
        for t in range(len(x_refs)):
            mine = rows(t, chip, mc)
            local(t).start()
            copy(t, 0, mine, sibling, src=x_refs[t]).start()
            copy(t, 1, mine, (*xn, mc), src=x_refs[t]).start()
            copy(t, 2, mine, (*yn, mc), src=x_refs[t]).start()

    def mid(self, x_refs, o_refs, sems):
        chip, mc, me, sibling, xn, yn, diag, rows, copy, local = self._plan(x_refs, o_refs, sems)
        for t in range(len(x_refs)):
            copy(t, 1, rows(t, xn, mc), me).wait_recv()
            copy(t, 3, rows(t, xn, mc, 0), (*yn, mc)).start()
            copy(t, 5, rows(t, xn, mc), sibling).start()
        for t in range(len(x_refs)):
            copy(t, 2, rows(t, yn, mc), me).wait_recv()
            copy(t, 4, rows(t, yn, mc, 1), (*xn, mc)).start()
            copy(t, 6, rows(t, yn, mc), sibling).start()

    def finish(self, x_refs, o_refs, sems):
        chip, mc, me, sibling, xn, yn, diag, rows, copy, local = self._plan(x_refs, o_refs, sems)
        for t in range(len(x_refs)):
            copy(t, 3, rows(t, diag, mc, 0), me).wait_recv()
            copy(t, 4, rows(t, diag, mc, 1), me).wait_recv()
            copy(t, 7, rows(t, diag, mc), sibling).start()
        for t in range(len(x_refs)):
            copy(t, 0, rows(t, chip, 1 - mc), me).wait_recv()
            copy(t, 5, rows(t, xn, 1 - mc), me).wait_recv()
            copy(t, 6, rows(t, yn, 1 - mc), me).wait_recv()
            copy(t, 7, rows(t, diag, 1 - mc), me).wait_recv()
            mine = rows(t, chip, mc)
            copy(t, 0, mine, sibling, src=x_refs[t]).wait_send()
            copy(t, 1, mine, (*xn, mc), src=x_refs[t]).wait_send()
            copy(t, 2, mine, (*yn, mc), src=x_refs[t]).wait_send()
            copy(t, 3, rows(t, xn, mc, 0), (*yn, mc)).wait_send()
            copy(t, 4, rows(t, yn, mc, 1), (*xn, mc)).wait_send()
            copy(t, 5, rows(t, xn, mc), sibling).wait_send()
            copy(t, 6, rows(t, yn, mc), sibling).wait_send()
            copy(t, 7, rows(t, diag, mc), sibling).wait_send()
            local(t).wait()


class _SiblingExchange:
    mid = None

    def __init__(self, grads):
        n_t = len(grads)
        self.inputs = list(grads)
        self.out_shapes = [jax.ShapeDtypeStruct((N_CHIP,) + g.shape[2:], F32) for g in grads]
        self.scratch = [pltpu.SemaphoreType.DMA((n_t,)), pltpu.SemaphoreType.DMA((n_t,))]

    def _copies(self, g_refs, land, sems):
        send, recv = sems
        mx, my, mc = _coords()
        return [pltpu.make_async_remote_copy(g_refs[t].at[:, 1 - mc], land[t], send.at[t], recv.at[t],
                                             device_id=(mx, my, 1 - mc), device_id_type=MESH)
                for t in range(len(g_refs))]

    def start(self, g_refs, land, sems):
        for cp in self._copies(g_refs, land, sems):
            cp.start()

    def finish(self, g_refs, land, sems):
        for cp in self._copies(g_refs, land, sems):
            cp.wait()


class _Together:
    def __init__(self, *comms):
        self.comms = comms
        self.inputs = [x for c in comms for x in c.inputs]
        self.out_shapes = [x for c in comms for x in c.out_shapes]
        self.scratch = [x for c in comms for x in c.scratch]
        self.mid = self._mid if any(c.mid is not None for c in comms) else None

    def _each(self, phase, cin, cout, sems):
        i = o = s = 0
        for c in self.comms:
            fn = getattr(c, phase)
            ni, no, ns = len(c.inputs), len(c.out_shapes), len(c.scratch)
            if fn is not None:
                fn(cin[i:i + ni], cout[o:o + no], sems[s:s + ns])
            i, o, s = i + ni, o + no, s + ns

    def start(self, cin, cout, sems):
        self._each("start", cin, cout, sems)

    def _mid(self, cin, cout, sems):
        self._each("mid", cin, cout, sems)

    def finish(self, cin, cout, sems):
        self._each("finish", cin, cout, sems)


def _standalone(comm, name):
    def body():
        pass
    return _call(body, grid=(1,), in_specs=[], out_specs=[], out_shape=[], args=(), name=name, comm=comm)[1]


def _chip_partials(g4s, lands, name):
    n_t = len(g4s)
    in_specs, out_specs, out_shape = [], [], []
    for g4 in g4s:
        _, _, r, c = g4.shape
        in_specs.append(pl.BlockSpec((None, None, r, c), lambda q: (q, lax.axis_index("c"), 0, 0)))
        out_specs.append(pl.BlockSpec((None, r, c), lambda q: (q, 0, 0)))
        out_shape.append(jax.ShapeDtypeStruct((N_CHIP, r, c), BF16))
    in_specs += [pl.BlockSpec((None,) + g4.shape[2:], lambda q: (q, 0, 0)) for g4 in g4s]

    def body(*refs):
        for t in range(n_t):
            refs[2 * n_t + t][...] = (refs[t][...] + refs[n_t + t][...]).astype(BF16)

    return pl.pallas_call(body, grid=(N_CHIP,), in_specs=in_specs, out_specs=out_specs, out_shape=out_shape,
                          compiler_params=_params(1), name=name)(*g4s, *lands)


class _ChipExchange:
    mid = None

    def __init__(self, parts):
        n_t = len(parts)
        self.inputs = list(parts)
        self.out_shapes = [jax.ShapeDtypeStruct(p.shape, p.dtype) for p in parts]
        self.scratch = [pltpu.SemaphoreType.DMA((n_t, 3)), pltpu.SemaphoreType.DMA((n_t, 3)),
                        pltpu.SemaphoreType.DMA((n_t,))]

    def _plan(self, p_refs, land, sems):
        send, recv, local_sem = sems
        mx, my, mc = _coords()
        my_chip = 2 * mx + my
        peers = [(_flip(mx, fx), _flip(my, fy)) for fx, fy in ((1, 0), (0, 1), (1, 1))]

        def out(t, k):
            px, py = peers[k]
            return pltpu.make_async_remote_copy(p_refs[t].at[2 * px + py], land[t].at[my_chip], send.at[t, k],
                                                recv.at[t, k], device_id=(px, py, mc), device_id_type=MESH)

        def arrival(t, k):
            px, py = peers[k]
            return pltpu.make_async_remote_copy(p_refs[t].at[my_chip], land[t].at[2 * px + py], send.at[t, k],
                                                recv.at[t, k], device_id=(px, py, mc), device_id_type=MESH)

        def local(t):
            return pltpu.make_async_copy(p_refs[t].at[my_chip], land[t].at[my_chip], local_sem.at[t])

        return out, arrival, local

    def start(self, p_refs, land, sems):
        out, arrival, local = self._plan(p_refs, land, sems)
        for t in range(len(p_refs)):
            local(t).start()
            for k in range(3):
                out(t, k).start()

    def finish(self, p_refs, land, sems):
        out, arrival, local = self._plan(p_refs, land, sems)
        for t in range(len(p_refs)):
            for k in range(3):
                arrival(t, k).wait_recv()
                out(t, k).wait_send()
            local(t).wait()


def _rope_tables(s, width):
    heads = width // HEAD_DIM
    inv_freq = ROPE_THETA ** (-jnp.arange(0, HEAD_DIM, 2, dtype=F32) / HEAD_DIM)
    inv_full = jnp.tile(inv_freq, 2 * heads)
    sign = jnp.tile(jnp.concatenate([-jnp.ones((HALF_HEAD,), F32), jnp.ones((HALF_HEAD,), F32)]), heads)
    ang = jnp.arange(s, dtype=F32)[:, None] * inv_full[None, :]
    return jnp.cos(ang), jnp.sin(ang) * sign[None, :]


def _pad_rows(v, rows):
    return jnp.concatenate([v, jnp.zeros((rows - 1, v.shape[1]), v.dtype)], axis=0)


def kernel(x, c, w_ada, b_ada, ffn1_norm_g, ffn1_w_gate, ffn1_w_up, ffn1_w_down, mix_norm_g, w_in, conv_dw_w, conv_dw_b, conv_ln_g, conv_ln_b, attn_out_g, conv_out_g, w_out, ffn2_norm_g, ffn2_w_gate, ffn2_w_up, ffn2_w_down, final_norm_g, loss_target, m_w_ada, m_b_ada, m_ffn1_norm_g, m_ffn1_w_gate, m_ffn1_w_up, m_ffn1_w_down, m_mix_norm_g, m_w_in, m_conv_dw_w, m_conv_dw_b, m_conv_ln_g, m_conv_ln_b, m_attn_out_g, m_conv_out_g, m_w_out, m_ffn2_norm_g, m_ffn2_w_gate, m_ffn2_w_up, m_ffn2_w_down, m_final_norm_g, v_w_ada, v_b_ada, v_ffn1_norm_g, v_ffn1_w_gate, v_ffn1_w_up, v_ffn1_w_down, v_mix_norm_g, v_w_in, v_conv_dw_w, v_conv_dw_b, v_conv_ln_g, v_conv_ln_b, v_attn_out_g, v_conv_out_g, v_w_out, v_ffn2_norm_g, v_ffn2_w_gate, v_ffn2_w_up, v_ffn2_w_down, v_final_norm_g):
    mx, my, mc = _coords()
    me = 4 * mx + 2 * my + mc
    s, d = x.shape[1], x.shape[2]
    aw = d // 2
    x2, target = x[0], loss_target[0]
    n_mod = w_ada.shape[2] * N_DEV // d
    mod_cols = w_ada.shape[2]

    def shard(w, transpose):
        return (w[0].T if transpose else w[0]).astype(BF16)

    cw_shard = conv_dw_w.shape[3]
    n_taps = CONV_KERNEL * cw_shard
    first_len = -(-(d + n_taps) // LANES) * LANES
    first = jnp.concatenate([c, conv_dw_w[0, :, 0, :].reshape(1, n_taps), jnp.zeros((1, first_len - d - n_taps), F32)], axis=1)
    first_all, wg1 = _standalone(
        _Together(_GatherSmall(_pad_rows(first, 8)), _GatherWeights([shard(ffn1_w_gate, True)])), "ag_first")
    first_all = first_all[0::8]
    c_all = first_all[:, :d]
    conv_w = first_all[:, d:d + n_taps].reshape(N_DEV, CONV_KERNEL, cw_shard).transpose(1, 0, 2).reshape(CONV_KERNEL, aw)

    silu_c = _silu_rows(c_all, "silu_c")
    mod_part = _plain_mm([(silu_c, w_ada[0])], F32, False, mod_cols, "mod_mm")
    mod_all = _ag_small(mod_part, "ag_mod").reshape(N_DEV, N_DEV, mod_cols)
    mod = lax.dynamic_index_in_dim(mod_all, me, axis=1, keepdims=False).reshape(1, n_mod * d) + b_ada
    sh1, sc1, g1, sh2, sc2, g2, sh3, sc3, g3 = [mod[:, i * d:(i + 1) * d] for i in range(n_mod)]

    def split(g):
        return g.reshape(N_CHIP, 2, g.shape[0] // N_DEV, g.shape[1])

    def partials(g4s, lands, tag):
        return _chip_partials(g4s, lands, "chip_partials_" + tag)

    (n1, a1), (wu1,) = _norm_gate(x2, ffn1_norm_g, sc1, sh1, wg1, "ffn1_gate",
                                  comm=_GatherWeights([shard(ffn1_w_up, True)]))
    (silu1, gs1, hid1), (wd1,) = _ffn_up_given_gate(n1, wu1, a1, "ffn1_up",
                                                    comm=_GatherWeights([shard(ffn1_w_down, False)]))
    (h1, f1, n2), (win_t,) = _residual_mm(hid1, wd1, x2, g1, 0.5, "ffn1_down", norm=(mix_norm_g, sc2, sh2),
                                          comm=_GatherWeights([shard(w_in, True)]))
    cos, sin_signed = _rope_tables(s, LANES)
    proj, = _proj_rope(n2, win_t, cos, sin_signed, aw, "proj")
    lanes_per = aw // LANES
    (attn, lse), (wg2, wu2, wd2) = _attn_seq_fwd(
        proj, aw, "attn_fwd",
        comm=_GatherWeights([shard(ffn2_w_gate, True), shard(ffn2_w_up, True), shard(ffn2_w_down, False)]))
    (u1,), (wout,) = _conv_fwd(proj, 3 * lanes_per, 4 * lanes_per, conv_w, conv_dw_b, "conv_fwd",
                               comm=_GatherWeights([shard(w_out, False)]))
    post = (attn_out_g, conv_ln_g, conv_ln_b, conv_out_g)
    y, h2, mix, n3 = _mix_out(attn, u1, post, wout, h1, g2, (ffn2_norm_g, sc3, sh3), "mix_out")
    silu3, gs3, hid3 = _ffn_up(n3, wg2, wu2, "ffn2_up")

    dh3, df3, err2, d_final_g, dg3 = _last_mm_loss(hid3, wd2, h2, g3, 0.5, target, final_norm_g.reshape(1, d),
                                                   "ffn2_down_loss")
    loss_part = jnp.zeros((1, LANES), F32).at[0, 0].set(0.5 * jnp.sum(err2) / d)

    da3, db3 = _ffn_bwd_hidden(df3, wd2, silu3, gs3, "ffn2_hidden_bwd")
    g4_a = [split(_mm_tn(da3, n3, "ffn2_dwg")), split(_mm_tn(db3, n3, "ffn2_dwu")), split(_mm_tn(hid3, df3, "ffn2_dwd"))]
    (dh2, dmix, dsh3, dsc3, dgn3, dg2), land_a = _mm_norm_mod_bwd(
        [(da3, wg2), (db3, wu2)], h2, dh3, ffn2_norm_g, sc3, (mix, g2, 1.0), "ffn2_dn_norm3_bwd", tm=256,
        comm=_SiblingExchange(g4_a))
    parts_a = partials(g4_a, land_a, "a")
    g_wout = _mm_tn(y, dmix, "mix_dwout")
    dattn, du1, d_gains, d_ln = _mix_dy_post_bwd(dmix, wout, attn, u1, post, "mix_dy_post_bwd")
    d_attn_g, d_conv_g, d_ln_g, d_ln_b = d_gains[:, :aw], d_gains[:, aw:], d_ln[:, :aw], d_ln[:, aw:]
    dga, dgb, d_taps, d_conv_b = _conv_bwd(proj, 3 * lanes_per, 4 * lanes_per, conv_w, du1, "conv_bwd")
    (dq, dk, dv), sums_a = _attn_seq_bwd(proj, dattn, attn, lse, cos, sin_signed, "attn_bwd",
                                         comm=_ChipExchange(parts_a))
    dproj = jnp.concatenate([dq, dk, dv, dga, dgb], axis=1)
    g4_b = [split(g_wout), split(_mm_tn(dproj, n2, "mix_dwin"))]
    (dh1, df1, dsh2, dsc2, dgn2, dg1), land_b = _mm_norm_mod_bwd(
        [(dproj, win_t)], h1, dh2, mix_norm_g, sc2, (f1, g1, 0.5), "mix_dn_norm2_bwd", tm=512,
        comm=_SiblingExchange(g4_b))
    parts_b = partials(g4_b, land_b, "b")
    g4_c = [split(_mm_tn(hid1, df1, "ffn1_dwd"))]
    (da1, db1), both = _ffn_bwd_hidden(df1, wd1, silu1, gs1, "ffn1_hidden_bwd",
                                       comm=_Together(_ChipExchange(parts_b), _SiblingExchange(g4_c)))
    sums_b, land_c = both[:2], both[2:]
    parts_c = partials(g4_c, land_c, "c")
    g_wu1, sums_c = _mm_tn(db1, n1, "ffn1_dwu", comm=_ChipExchange(parts_c))
    g4_d = [split(g_wu1)]
    g_wg1, land_d = _mm_tn(da1, n1, "ffn1_dwg", comm=_SiblingExchange(g4_d))
    parts_d = partials(g4_d, land_d, "d")
    g4_e = [split(g_wg1)]
    dn1, both = _plain_mm([(da1, wg1), (db1, wu1)], BF16, False, d, "ffn1_dn",
                          comm=_Together(_ChipExchange(parts_d), _SiblingExchange(g4_e)))
    sums_d, land_e = both[:1], both[1:]
    parts_e = partials(g4_e, land_e, "e")
    (dx, dsh1, dsc1, dgn1), sums_e = _norm_mod_bwd(dn1, x2, dh1, ffn1_norm_g, sc1, "norm1_bwd",
                                                   comm=_ChipExchange(parts_e))

    dmod = jnp.concatenate([dsh1, dsc1, dg1, dsh2, dsc2, dg2, dsh3, dsc3, dg3], axis=1)
    small = [dmod, dgn1, dgn2, dgn3, d_final_g, d_conv_b, d_ln_g, d_ln_b, d_attn_g, d_conv_g,
             d_taps.reshape(1, CONV_KERNEL * aw), loss_part]
    sizes = [v.shape[1] for v in small]
    total = sum(sizes)
    padded = -(-total // (8 * LANES)) * (8 * LANES)
    packed = jnp.concatenate(small + [jnp.zeros((1, padded - total), F32)], axis=1).reshape(8, padded // 8)
    gathered = _ag_small(packed, "ag_small_grads")
    summed = _sum_blocks(gathered, N_DEV, "sum_small_grads").reshape(1, padded)
    offs = [sum(sizes[:i]) for i in range(len(sizes))]
    (g_b_ada, g_gn1, g_gn2, g_gn3, g_final, g_conv_b, g_ln_g, g_ln_b, g_attn_g, g_conv_g, g_taps, loss_row) = [
        summed[:, o:o + n] for o, n in zip(offs, sizes)]
    loss = loss_row[0, 0]
    g_taps_shard = lax.dynamic_slice_in_dim(g_taps.reshape(CONV_KERNEL, aw), me * cw_shard, cw_shard, axis=1)
    dmod_all = gathered.reshape(N_DEV, padded)[:, :n_mod * d]
    dmod_cols = lax.dynamic_slice_in_dim(dmod_all, me * mod_cols, mod_cols, axis=1)
    g_w_ada = _mm_tn(silu_c, dmod_cols, "ada_dw")

    arrived = dict(zip(["ffn2_w_gate", "ffn2_w_up", "ffn2_w_down", "w_out", "w_in", "ffn1_w_down", "ffn1_w_up",
                        "ffn1_w_gate"], list(sums_a) + list(sums_b) + list(sums_c) + list(sums_d) + list(sums_e)))
    transposed = ("ffn1_w_gate", "ffn1_w_up", "w_in", "ffn2_w_gate", "ffn2_w_up")
    grads = {
        "w_ada": g_w_ada, "b_ada": g_b_ada, "ffn1_norm_g": g_gn1, "mix_norm_g": g_gn2, "conv_dw_w": g_taps_shard,
        "conv_dw_b": g_conv_b, "conv_ln_g": g_ln_g, "conv_ln_b": g_ln_b, "attn_out_g": g_attn_g,
        "conv_out_g": g_conv_g, "ffn2_norm_g": g_gn3, "final_norm_g": g_final,
    }
    weights = dict(w_ada=w_ada, b_ada=b_ada, ffn1_norm_g=ffn1_norm_g, ffn1_w_gate=ffn1_w_gate, ffn1_w_up=ffn1_w_up, ffn1_w_down=ffn1_w_down, mix_norm_g=mix_norm_g, w_in=w_in, conv_dw_w=conv_dw_w, conv_dw_b=conv_dw_b, conv_ln_g=conv_ln_g, conv_ln_b=conv_ln_b, attn_out_g=attn_out_g, conv_out_g=conv_out_g, w_out=w_out, ffn2_norm_g=ffn2_norm_g, ffn2_w_gate=ffn2_w_gate, ffn2_w_up=ffn2_w_up, ffn2_w_down=ffn2_w_down, final_norm_g=final_norm_g)
    moms = dict(w_ada=m_w_ada, b_ada=m_b_ada, ffn1_norm_g=m_ffn1_norm_g, ffn1_w_gate=m_ffn1_w_gate, ffn1_w_up=m_ffn1_w_up, ffn1_w_down=m_ffn1_w_down, mix_norm_g=m_mix_norm_g, w_in=m_w_in, conv_dw_w=m_conv_dw_w, conv_dw_b=m_conv_dw_b, conv_ln_g=m_conv_ln_g, conv_ln_b=m_conv_ln_b, attn_out_g=m_attn_out_g, conv_out_g=m_conv_out_g, w_out=m_w_out, ffn2_norm_g=m_ffn2_norm_g, ffn2_w_gate=m_ffn2_w_gate, ffn2_w_up=m_ffn2_w_up, ffn2_w_down=m_ffn2_w_down, final_norm_g=m_final_norm_g)
    vars_ = dict(w_ada=v_w_ada, b_ada=v_b_ada, ffn1_norm_g=v_ffn1_norm_g, ffn1_w_gate=v_ffn1_w_gate, ffn1_w_up=v_ffn1_w_up, ffn1_w_down=v_ffn1_w_down, mix_norm_g=v_mix_norm_g, w_in=v_w_in, conv_dw_w=v_conv_dw_w, conv_dw_b=v_conv_dw_b, conv_ln_g=v_conv_ln_g, conv_ln_b=v_conv_ln_b, attn_out_g=v_attn_out_g, conv_out_g=v_conv_out_g, w_out=v_w_out, ffn2_norm_g=v_ffn2_norm_g, ffn2_w_gate=v_ffn2_w_gate, ffn2_w_up=v_ffn2_w_up, ffn2_w_down=v_ffn2_w_down, final_norm_g=v_final_norm_g)
    names = list(weights)
    big = ["w_ada", "ffn1_w_gate", "ffn1_w_up", "ffn1_w_down", "w_in", "w_out", "ffn2_w_gate", "ffn2_w_up",
           "ffn2_w_down"]
    shape2 = {n: (weights[n].shape[-2] if weights[n].ndim > 1 else 1, weights[n].shape[-1]) for n in names}
    shape2["conv_dw_w"] = (CONV_KERNEL, cw_shard)
    g_out, d_out, m_out, v_out = {}, {}, {}, {}
    for n in big:
        if n in arrived:
            def view(t, n=n):
                return t[0].T if n in transposed else t[0]
            res = _adamw_reduced(view(weights[n]), arrived[n], view(moms[n]), view(vars_[n]), "adamw_" + n)
            g_out[n], d_out[n], m_out[n], v_out[n] = [r.T if n in transposed else r for r in res]
        else:
            g2d = grads[n].reshape(shape2[n])
            res = _adamw_big(weights[n].reshape(shape2[n]), g2d, moms[n].reshape(shape2[n]),
                             vars_[n].reshape(shape2[n]), "adamw_" + n)
            g_out[n], (d_out[n], m_out[n], v_out[n]) = g2d, res
    rest = [n for n in names if n not in big]
    res = _adamw_small([weights[n].reshape(shape2[n]) for n in rest], [grads[n].reshape(shape2[n]) for n in rest],
                       [moms[n].reshape(shape2[n]) for n in rest], [vars_[n].reshape(shape2[n]) for n in rest],
                       "adamw_small")
    for i, n in enumerate(rest):
        g_out[n], d_out[n], m_out[n], v_out[n] = grads[n], res[0][i], res[1][i], res[2][i]

    def shaped(table):
        return [table[n].reshape(weights[n].shape) for n in names]

    return (loss, dx.reshape(x.shape), *shaped(g_out), *shaped(d_out), *shaped(m_out), *shaped(v_out))
```

```python
import functools

import jax
import jax.numpy as jnp
from jax import lax
from jax.experimental import pallas as pl
from jax.experimental.pallas import tpu as pltpu

F32 = jnp.float32
BF16 = jnp.bfloat16
MESH = pl.DeviceIdType.MESH
ANY = pl.BlockSpec(memory_space=pl.ANY)

N_DEV = 8
N_CHIP = 4
HEAD_DIM = 64
HALF_HEAD = HEAD_DIM // 2
LANES = 128
BLOCK = 128
DILATIONS = (1, 4, 16)
MERGE_CHUNK = 512
ROPE_THETA = 10000.0
CONV_KERNEL = 31
CONV_HALO = 32
CONV_CHUNK = 512
CONV_SUB = 128
RMS_EPS = 1e-6
LN_EPS = 1e-5
ADAM_LR = 0.001
ADAM_B1 = 0.9
ADAM_B2 = 0.999
ADAM_EPS = 1e-08
ADAM_WD = 0.01
ADAM_STEP = 10
VMEM_LIMIT = 56 * 1024 * 1024
NEG = -1e30


def _params(n_axes):
    return pltpu.CompilerParams(dimension_semantics=("arbitrary",) * n_axes, vmem_limit_bytes=VMEM_LIMIT)


def _tile(n, target, unit):
    best = None
    for t in range(unit, min(n, target) + 1, unit):
        if n % t == 0:
            best = t
    return best if best is not None else n


def _sigmoid(x):
    return 0.5 * (jnp.tanh(0.5 * x) + 1.0)


def _call(body, *, grid, in_specs, out_specs, out_shape, args, name, scratch_shapes=(), comm=None):
    params = _params(len(grid))
    if comm is None:
        return pl.pallas_call(body, grid=grid, in_specs=list(in_specs), out_specs=list(out_specs),
                              out_shape=list(out_shape), scratch_shapes=list(scratch_shapes),
                              compiler_params=params, name=name)(*args)
    n_in, n_out, n_scr = len(args), len(out_shape), len(scratch_shapes)
    c_in, c_out = len(comm.inputs), len(comm.out_shapes)
    steps = 1
    for g in grid:
        steps *= g

    def hosted(*refs):
        pos = 0
        parts = []
        for size in (n_in, c_in, n_out, c_out, n_scr, len(comm.scratch)):
            parts.append(refs[pos:pos + size])
            pos += size
        ins, cin, outs, cout, scr, cscr = parts
        step = 0
        for axis, g in enumerate(grid):
            step = step * g + pl.program_id(axis)

        @pl.when(step == 0)
        def _():
            comm.start(cin, cout, cscr)

        body(*ins, *outs, *scr)
        if comm.mid is not None and steps >= 4:
            @pl.when(step == steps // 2)
            def _():
                comm.mid(cin, cout, cscr)

        @pl.when(step == steps - 1)
        def _():
            if comm.mid is not None and steps < 4:
                comm.mid(cin, cout, cscr)
            comm.finish(cin, cout, cscr)

    res = pl.pallas_call(
        hosted, grid=grid, in_specs=list(in_specs) + [ANY] * c_in, out_specs=list(out_specs) + [ANY] * c_out,
        out_shape=list(out_shape) + list(comm.out_shapes), scratch_shapes=list(scratch_shapes) + list(comm.scratch),
        compiler_params=params, name=name)(*args, *comm.inputs)
    return res[:n_out], res[n_out:]


def _rows(fn, rows_in, vecs_in, rows_out, vecs_out, *, tile, name, comm=None):
    norm = [r if isinstance(r, tuple) else (r, r.shape[1], 0) for r in rows_in]
    n_rows = norm[0][0].shape[0]
    n_tiles = n_rows // tile
    in_specs, args = [], []
    for arr, width, cb in norm:
        in_specs.append(pl.BlockSpec((tile, width), functools.partial(lambda i, cb: (i, cb), cb=cb)))
        args.append(arr)
    for v in vecs_in:
        in_specs.append(pl.BlockSpec((1, v.shape[1]), lambda i: (0, 0)))
        args.append(v)
    out_shape = [jax.ShapeDtypeStruct((n_rows, w), dt) for w, dt in rows_out]
    out_shape += [jax.ShapeDtypeStruct((1, w), F32) for w in vecs_out]
    out_specs = [pl.BlockSpec((tile, w), lambda i: (i, 0)) for w, _ in rows_out]
    out_specs += [pl.BlockSpec((1, w), lambda i: (0, 0)) for w in vecs_out]
    n_in, n_ro = len(args), len(rows_out)

    def body(*refs):
        vals = [r[...] for r in refs[:n_in]]
        outs = refs[n_in:]
        row_vals, vec_vals = fn(*vals)
        for ref, val in zip(outs[:n_ro], row_vals):
            if isinstance(val, tuple):
                w = val[0].shape[1]
                for j, piece in enumerate(val):
                    ref[:, j * w:(j + 1) * w] = piece.astype(ref.dtype)
            else:
                ref[...] = val.astype(ref.dtype)
        if vecs_out:
            @pl.when(pl.program_id(0) == 0)
            def _():
                for ref in outs[n_ro:]:
                    ref[...] = jnp.zeros_like(ref)
            for ref, val in zip(outs[n_ro:], vec_vals):
                ref[...] += val

    return _call(body, grid=(n_tiles,), in_specs=in_specs, out_specs=out_specs, out_shape=out_shape, args=args,
                 name=name, comm=comm)


def _colsum(x):
    return jnp.sum(x, axis=0, keepdims=True)


def _rms_stats(h):
    r = lax.rsqrt(jnp.mean(h * h, axis=-1, keepdims=True) + RMS_EPS)
    return r, h * r


def _rms_back(r, xn, dxn):
    return r * (dxn - xn * jnp.mean(dxn * xn, axis=-1, keepdims=True))


def _branch_back(dh, f, gate, coef):
    return (coef * gate) * dh, coef * _colsum(f.astype(F32) * dh)


def _norm_mod_back(dn, h, dh_in, gain, scale):
    dn = dn.astype(F32)
    r, xn = _rms_stats(h)
    y = xn * gain
    dy = dn * (1.0 + scale)
    dh = dh_in + _rms_back(r, xn, dy * gain)
    return dh, [_colsum(dn), _colsum(dn * y), _colsum(dy * xn)]


def _norm_mod_bwd(dn, h, dh_in, gain, scale, name, comm=None):
    d = h.shape[1]

    def fn(dn, h, dh_in, gain, scale):
        dh, vecs = _norm_mod_back(dn, h, dh_in, gain, scale)
        return [dh], vecs
    return _rows(fn, [dn, h, dh_in], [gain, scale], [(d, F32)], [d, d, d], tile=256, name=name, comm=comm)


def _mm_norm_mod_bwd(pairs, h, dh_in, gain, scale, branch, name, tm, comm=None):
    f, gate, coef = branch

    def epi(accs, ex, vc):
        dh, vecs = _norm_mod_back(accs[0], ex[0], ex[1], vc[0], vc[1])
        df, dgate = _branch_back(dh, ex[2], vc[2], coef)
        return [dh, df] + vecs + [dgate]
    return _mm([pairs], epi, [h, dh_in, f], [gain, scale, gate], [F32, BF16], trans_rhs=False, tm=tm,
               tn=h.shape[1], name=name, n_sums=4, chunk=tm // 2, comm=comm)


def _last_mm_loss(lhs, w, res, gate, coef, target, gain, name):
    d = w.shape[1]

    def epi(accs, ex, vc):
        f = accs[0]
        h = ex[0] + (coef * vc[0]) * f
        r, xn = _rms_stats(h)
        err = xn * vc[1] - ex[1]
        dout = err * (1.0 / d)
        dh = _rms_back(r, xn, dout * vc[1])
        df, dgate = _branch_back(dh, f, vc[0], coef)
        return [dh, df, _colsum(err * err), _colsum(dout * xn), dgate]
    return _mm([[(lhs, w)]], epi, [res, target], [gate, gain], [F32, BF16], trans_rhs=False, tm=256, tn=d,
               name=name, n_sums=3, chunk=128)


def _partner(x):
    if x.shape[1] > LANES:
        return jnp.concatenate([_partner(x[:, c:c + LANES]) for c in range(0, x.shape[1], LANES)], axis=1)
    lane = lax.broadcasted_iota(jnp.int32, x.shape, 1) % HEAD_DIM
    return jnp.where(lane < HALF_HEAD, pltpu.roll(x, LANES - HALF_HEAD, 1), pltpu.roll(x, HALF_HEAD, 1))


def _proj_rope(n, w_t, cos, sin_signed, width, name, comm=None):
    s, kdim = n.shape
    n_cols = w_t.shape[0]
    tm = _tile(s, 1024, 8)
    qscale = HEAD_DIM ** -0.5

    chunk = _tile(tm, 256, 8)

    def body(n_ref, w_ref, cos_ref, sin_ref, o_ref):
        j = pl.program_id(0)

        def products(rows):
            return lax.dot_general(n_ref[rows, :].astype(BF16), w_ref[...].astype(BF16), (((1,), (1,)), ((), ())),
                                   preferred_element_type=F32)

        @pl.when(j >= 2)
        def _():
            for c in range(tm // chunk):
                rows = slice(c * chunk, (c + 1) * chunk)
                o_ref[rows, :] = products(rows)

        @pl.when(j < 2)
        def _():
            scale = jnp.where(j == 0, qscale, 1.0)
            for c in range(tm // chunk):
                rows = slice(c * chunk, (c + 1) * chunk)
                acc = products(rows)
                cos = jnp.tile(cos_ref[rows, :], (1, width // LANES))
                sin = jnp.tile(sin_ref[rows, :], (1, width // LANES))
                o_ref[rows, :] = scale * (acc * cos + _partner(acc) * sin)

    table = pl.BlockSpec((tm, LANES), lambda j, i: (jnp.where(j < 2, i, 0), 0))
    return _call(
        body, grid=(n_cols // width, s // tm),
        in_specs=[pl.BlockSpec((tm, kdim), lambda j, i: (i, 0)), pl.BlockSpec((width, kdim), lambda j, i: (j, 0)),
                  table, table],
        out_specs=[pl.BlockSpec((tm, width), lambda j, i: (i, j))],
        out_shape=[jax.ShapeDtypeStruct((s, n_cols), F32)], args=(n, w_t, cos, sin_signed), name=name, comm=comm)


def _mix_post(attn, u1, attn_g, ln_g, ln_b, conv_g):
    _, xa = _rms_stats(attn)
    mu = jnp.mean(u1, axis=-1, keepdims=True)
    xc = u1 - mu
    rstd = lax.rsqrt(jnp.mean(xc * xc, axis=-1, keepdims=True) + LN_EPS)
    u2 = (xc * rstd) * ln_g + ln_b
    u3 = u2 * _sigmoid(u2)
    _, x3 = _rms_stats(u3)
    return jnp.concatenate([xa * attn_g, x3 * conv_g], axis=1)


def _mix_post_back(dy, attn, u1, attn_g, ln_g, ln_b, conv_g):
    w = attn.shape[1]
    dya, dyc = dy[:, :w], dy[:, w:]
    ra, xa = _rms_stats(attn)
    dattn = _rms_back(ra, xa, dya * attn_g)
    mu = jnp.mean(u1, axis=-1, keepdims=True)
    xc = u1 - mu
    rstd = lax.rsqrt(jnp.mean(xc * xc, axis=-1, keepdims=True) + LN_EPS)
    xh = xc * rstd
    u2 = xh * ln_g + ln_b
    sig = _sigmoid(u2)
    u3 = u2 * sig
    r3, x3 = _rms_stats(u3)
    du3 = _rms_back(r3, x3, dyc * conv_g)
    du2 = du3 * (sig + u3 * (1.0 - sig))
    dxh = du2 * ln_g
    du1 = rstd * (dxh - jnp.mean(dxh, axis=-1, keepdims=True) - xh * jnp.mean(dxh * xh, axis=-1, keepdims=True))
    return dattn, du1, [_colsum(dya * xa), _colsum(dyc * x3), _colsum(du2 * xh), _colsum(du2)]


def _silu_rows(c_all, name):
    def fn(c):
        return [c * _sigmoid(c)], []
    return _rows(fn, [c_all], [], [(c_all.shape[1], BF16)], [], tile=c_all.shape[0], name=name)[0]


def _mm(groups, epi, extras, vecs, outs, *, trans_rhs, tm, tn, name, n_sums=0, pre=None, pre_inputs=(),
        chunk=None, comm=None):
    m = (pre_inputs[0] if pre is not None else groups[0][0][0]).shape[0]
    n = groups[0][0][1].shape[0] if trans_rhs else groups[0][0][1].shape[1]
    tm, tn = min(tm, m), min(tn, n)
    in_specs, args, uses_pre = [], [], []
    for grp in groups:
        for lhs, rhs in grp:
            k = rhs.shape[1] if trans_rhs else rhs.shape[0]
            uses_pre.append(lhs is None)
            if lhs is not None:
                in_specs.append(pl.BlockSpec((tm, k), lambda j, i: (i, 0)))
                args.append(lhs)
            in_specs.append(pl.BlockSpec((tn, k), lambda j, i: (j, 0)) if trans_rhs
                            else pl.BlockSpec((k, tn), lambda j, i: (0, j)))
            args.append(rhs)
    n_mm = len(args)
    for p in pre_inputs:
        in_specs.append(pl.BlockSpec((tm, p.shape[1]), lambda j, i: (i, 0)))
        args.append(p)
    for e in extras:
        in_specs.append(pl.BlockSpec((tm, tn), lambda j, i: (i, j)) if e.shape[1] == n
                        else pl.BlockSpec((tm, e.shape[1]), lambda j, i: (i, 0)))
        args.append(e)
    for v in vecs:
        in_specs.append(pl.BlockSpec((1, tn), lambda j, i: (0, j)) if v.shape[1] == n
                        else pl.BlockSpec((1, v.shape[1]), lambda j, i: (0, 0)))
        args.append(v)
    sizes = [len(g) for g in groups]
    n_pre, n_ex, n_vec = len(pre_inputs), len(extras), len(vecs)
    dims = (((1,), (1,)), ((), ())) if trans_rhs else (((1,), (0,)), ((), ()))
    out_specs, out_shape = [], []
    if pre is not None:
        k_pre = args[n_mm - 1].shape[1] if trans_rhs else args[n_mm - 1].shape[0]
        out_specs.append(pl.BlockSpec((tm, k_pre), lambda j, i: (i, 0)))
        out_shape.append(jax.ShapeDtypeStruct((m, k_pre), BF16))
    for o in outs:
        dt, width = o if isinstance(o, tuple) else (o, n)
        out_specs.append(pl.BlockSpec((tm, tn), lambda j, i: (i, j)) if width == n
                         else pl.BlockSpec((tm, width), lambda j, i: (i, 0)))
        out_shape.append(jax.ShapeDtypeStruct((m, width), dt))
    n_tiles_out = len(out_specs)
    out_specs += [pl.BlockSpec((1, tn), lambda j, i: (0, j))] * n_sums
    out_shape += [jax.ShapeDtypeStruct((1, n), F32)] * n_sums

    rows_per = min(chunk or tm, tm)

    def body(*refs):
        ins = refs[:n_mm + n_pre + n_ex + n_vec]
        out_refs = refs[n_mm + n_pre + n_ex + n_vec:]
        vc = [r[...] for r in ins[n_mm + n_pre + n_ex:]]

        def strip(rows):
            vals = []
            made = None
            if pre is not None:
                made = pre([r[rows, :] for r in ins[n_mm:n_mm + n_pre]], vc).astype(BF16)
                vals.append(made)
            accs, pos, pair = [], 0, 0
            for size in sizes:
                acc = None
                for _ in range(size):
                    if uses_pre[pair]:
                        lhs_tile = made
                    else:
                        lhs_tile = ins[pos][rows, :].astype(BF16)
                        pos += 1
                    part = lax.dot_general(lhs_tile, ins[pos][...].astype(BF16), dims, preferred_element_type=F32)
                    acc = part if acc is None else acc + part
                    pos += 1
                    pair += 1
                accs.append(acc)
            ex = [r[rows, :] for r in ins[n_mm + n_pre:n_mm + n_pre + n_ex]]
            return vals + epi(accs, ex, vc)

        sums = None
        for c in range(tm // rows_per):
            rows = slice(c * rows_per, (c + 1) * rows_per)
            vals = strip(rows)
            for ref, val in zip(out_refs[:n_tiles_out], vals):
                ref[rows, :] = val.astype(ref.dtype)
            tail = vals[n_tiles_out:]
            sums = tail if sums is None else [a + b for a, b in zip(sums, tail)]
        if n_sums:
            @pl.when(pl.program_id(1) == 0)
            def _():
                for ref in out_refs[n_tiles_out:]:
                    ref[...] = jnp.zeros_like(ref)
            for ref, val in zip(out_refs[n_tiles_out:], sums):
                ref[...] += val

    return _call(body, grid=(n // tn, m // tm), in_specs=in_specs, out_specs=out_specs, out_shape=out_shape,
                 args=args, name=name, comm=comm)


def _mm_tn(lhs, rhs, name, comm=None):
    t, a = lhs.shape
    b = rhs.shape[1]
    ta = a if a <= 1536 else _tile(a, 1536, LANES)
    tk = _tile(t, 2048, 8)

    def body(l_ref, r_ref, o_ref):
        @pl.when(pl.program_id(1) == 0)
        def _():
            o_ref[...] = jnp.zeros_like(o_ref)
        o_ref[...] += lax.dot_general(l_ref[...].astype(BF16), r_ref[...].astype(BF16), (((0,), (0,)), ((), ())),
                                      preferred_element_type=F32)

    res = _call(body, grid=(a // ta, t // tk),
                in_specs=[pl.BlockSpec((tk, ta), lambda i, k: (k, i)), pl.BlockSpec((tk, b), lambda i, k: (k, 0))],
                out_specs=[pl.BlockSpec((ta, b), lambda i, k: (i, 0))], out_shape=[jax.ShapeDtypeStruct((a, b), F32)],
                args=(lhs, rhs), name=name, comm=comm)
    return res[0] if comm is None else (res[0][0], res[1])


def _ffn_tn(f):
    return _tile(f, 1536, LANES)


def _swiglu_parts(a, b):
    sig = _sigmoid(a)
    silu = a * sig
    return [silu, b * (sig + silu * (1.0 - sig)), silu * b]


def _ffn_up(n, wg_t, wu_t, name, comm=None):
    def epi(accs, ex, vc):
        return _swiglu_parts(accs[0], accs[1])
    return _mm([[(n, wg_t)], [(n, wu_t)]], epi, [], [], [BF16, BF16, BF16], trans_rhs=True, tm=512,
               tn=_ffn_tn(wg_t.shape[0]), name=name, chunk=256, comm=comm)


def _norm_gate(h, gain, scale, shift, wg_t, name, comm=None):
    def pre(tiles, vc):
        _, xn = _rms_stats(tiles[0])
        return (xn * vc[0]) * (1.0 + vc[1]) + vc[2]

    def epi(accs, ex, vc):
        return [accs[0]]
    return _mm([[(None, wg_t)]], epi, [], [gain, scale, shift], [BF16], trans_rhs=True, tm=512,
               tn=wg_t.shape[0], name=name, pre=pre, pre_inputs=[h], comm=comm)


def _mix_out(attn, u1, post, w, res, gate, norm, name):
    def pre(tiles, vc):
        return _mix_post(tiles[0], tiles[1], *vc[4:8])

    def epi(accs, ex, vc):
        h = ex[0] + vc[0] * accs[0]
        _, xn = _rms_stats(h)
        return [h, accs[0], (xn * vc[1]) * (1.0 + vc[2]) + vc[3]]
    return _mm([[(None, w)]], epi, [res], [gate] + list(norm) + list(post), [F32, BF16, BF16], trans_rhs=False,
               tm=512, tn=w.shape[1], name=name, pre=pre, pre_inputs=[attn, u1], chunk=256)


def _mix_dy_post_bwd(dmix, w, attn, u1, post, name):
    width = attn.shape[1]

    def epi(accs, ex, vc):
        dattn, du1, sums = _mix_post_back(accs[0], ex[0], ex[1], *vc)
        return [dattn, du1, jnp.concatenate(sums[0:2], axis=1), jnp.concatenate(sums[2:4], axis=1)]
    return _mm([[(dmix, w)]], epi, [attn, u1], list(post), [(F32, width), (F32, width)], trans_rhs=True, tm=256,
               tn=w.shape[0], name=name, n_sums=2)


def _ffn_up_given_gate(n, wu_t, a, name, comm=None):
    def epi(accs, ex, vc):
        return _swiglu_parts(ex[0].astype(F32), accs[0])
    return _mm([[(n, wu_t)]], epi, [a], [], [BF16, BF16, BF16], trans_rhs=True, tm=512,
               tn=_ffn_tn(wu_t.shape[0]), name=name, chunk=256, comm=comm)


def _residual_mm(lhs, w, res, gate, coef, name, norm=None, comm=None):
    def epi(accs, ex, vc):
        h = ex[0] + (coef * vc[0]) * accs[0]
        if norm is None:
            return [h, accs[0]]
        _, xn = _rms_stats(h)
        return [h, accs[0], (xn * vc[1]) * (1.0 + vc[2]) + vc[3]]
    vecs = [gate] + (list(norm) if norm is not None else [])
    outs = [F32, BF16] + ([BF16] if norm is not None else [])
    return _mm([[(lhs, w)]], epi, [res], vecs, outs, trans_rhs=False, tm=512, tn=w.shape[1], name=name, chunk=256,
               comm=comm)


def _ffn_bwd_hidden(df, wd, dhid_db, dhid_da, name, comm=None):
    def epi(accs, ex, vc):
        return [accs[0] * ex[1].astype(F32), accs[0] * ex[0].astype(F32)]
    return _mm([[(df, wd)]], epi, [dhid_db, dhid_da], [], [BF16, BF16], trans_rhs=True, tm=512,
               tn=_ffn_tn(wd.shape[0]), name=name, comm=comm)


def _plain_mm(pairs, out_dtype, trans_rhs, tn, name, tm=512, comm=None):
    def epi(accs, ex, vc):
        return [accs[0]]
    res = _mm([pairs], epi, [], [], [out_dtype], trans_rhs=trans_rhs, tm=tm, tn=tn, name=name, comm=comm)
    return res[0] if comm is None else (res[0][0], res[1])


HEADS_PER_TILE = LANES // HEAD_DIM


def _stack_heads(x):
    lane = lax.broadcasted_iota(jnp.int32, (1, LANES), 1)
    return jnp.concatenate([x * (lane // HEAD_DIM == h).astype(F32) for h in range(HEADS_PER_TILE)], axis=0)


def _unstack_heads(y):
    r = y.shape[0] // HEADS_PER_TILE
    lane = lax.broadcasted_iota(jnp.int32, (r, y.shape[1]), 1)
    out = y[0:r]
    for h in range(1, HEADS_PER_TILE):
        out = jnp.where(lane // HEAD_DIM == h, y[h * r:(h + 1) * r], out)
    return out


def _stacked_lse(lb):
    return jnp.concatenate([_lane_pick(lb, h) for h in range(HEADS_PER_TILE)], axis=0)


def _band_masks(n_row_blocks, n_col_blocks):
    shape = (n_row_blocks * BLOCK, n_col_blocks * BLOCK)
    qi = lax.broadcasted_iota(jnp.int32, shape, 0) % BLOCK
    kj = lax.broadcasted_iota(jnp.int32, shape, 1) % BLOCK
    return kj <= qi, kj >= qi


def _query_masks():
    first_valid, _ = _band_masks(HEADS_PER_TILE, 1)
    same_ok, before_ok = _band_masks(HEADS_PER_TILE, 2)
    is_cur = lax.broadcasted_iota(jnp.int32, same_ok.shape, 1) >= BLOCK
    return first_valid, jnp.logical_and(is_cur, same_ok), jnp.logical_and(jnp.logical_not(is_cur), before_ok)


def _dot_nt(a, b):
    return lax.dot_general(a.astype(BF16), b.astype(BF16), (((1,), (1,)), ((), ())), preferred_element_type=F32)


def _dot_nn(a, b):
    return lax.dot_general(a.astype(BF16), b.astype(BF16), (((1,), (0,)), ((), ())), preferred_element_type=F32)


def _dot_tn(a, b):
    return lax.dot_general(a.astype(BF16), b.astype(BF16), (((0,), (0,)), ((), ())), preferred_element_type=F32)


def _lane_pick(x, h):
    lane = lax.broadcasted_iota(jnp.int32, x.shape, 1)
    return jnp.sum(jnp.where(lane == h * HEAD_DIM, x, 0.0), axis=1, keepdims=True)


def _block_rows(idx, d):
    span = BLOCK * d
    q0 = (idx // d) * span + idx % d
    return pl.ds(q0, BLOCK, stride=d), pl.ds(q0 - span, BLOCK, stride=d)


def _branch_loops(n_blocks, d, visit, unroll, masks):
    first_valid, cur_part, prev_part = masks
    if d % unroll == 0 and (n_blocks - d) % unroll == 0:
        full_valid = jnp.logical_or(cur_part, prev_part)

        def first(idx, carry):
            rows = pl.ds(idx, BLOCK, stride=d)
            visit(rows, [rows], first_valid)
            return carry

        def rest(idx, carry):
            rows, prev = _block_rows(idx, d)
            visit(rows, [prev, rows], full_valid)
            return carry

        lax.fori_loop(0, d, first, 0, unroll=unroll)
        lax.fori_loop(d, n_blocks, rest, 0, unroll=unroll)
        return

    def every(idx, carry):
        span = BLOCK * d
        q0 = (idx // d) * span + idx % d
        has_prev = idx >= d
        rows = pl.ds(q0, BLOCK, stride=d)
        prev = pl.ds(jnp.where(has_prev, q0 - span, q0), BLOCK, stride=d)
        visit(rows, [prev, rows], jnp.logical_or(cur_part, jnp.logical_and(prev_part, has_prev)))
        return carry

    lax.fori_loop(0, n_blocks, every, 0, unroll=unroll)


def _qkv_specs(s, tiles):
    q, k, v = [pl.BlockSpec((s, LANES), functools.partial(lambda hb, off: (0, off + hb), off=i * tiles))
               for i in range(3)]
    return q, k, v, pl.BlockSpec((s, LANES), lambda hb: (0, hb))


def _attn_seq_fwd(proj, width, name, comm=None):
    s = proj.shape[0]
    q_spec, k_spec, v_spec, cur = _qkv_specs(s, width // LANES)

    def body(q_ref, k_ref, v_ref, o_ref, l_ref, o_s, l_s):
        masks = _query_masks()
        for bi, d in enumerate(DILATIONS):
            def visit(rows, key_rows, valid, bi=bi):
                q2 = _stack_heads(q_ref[rows, :])
                keys = jnp.concatenate([k_ref[r, :] for r in key_rows], axis=0)
                vals = jnp.concatenate([v_ref[r, :] for r in key_rows], axis=0)
                sc = jnp.where(valid, _dot_nt(q2, keys), NEG)
                mx = jnp.max(sc, axis=1, keepdims=True)
                p = jnp.exp(sc - mx)
                den = jnp.sum(p, axis=1, keepdims=True)
                o_s[bi, rows, :] = _unstack_heads(_dot_nn(p, vals) / den)
                l_s[bi, rows, :] = _unstack_heads(jnp.broadcast_to(mx + jnp.log(den), (q2.shape[0], LANES)))

            _branch_loops(s // BLOCK, d, visit, 8, masks)
        for c in range(s // MERGE_CHUNK):
            rows = slice(c * MERGE_CHUNK, (c + 1) * MERGE_CHUNK)
            ls = [l_s[bi, rows, :] for bi in range(len(DILATIONS))]
            top = functools.reduce(jnp.maximum, ls)
            ws = [jnp.exp(l - top) for l in ls]
            den = functools.reduce(lambda a, b: a + b, ws)
            num = functools.reduce(lambda a, b: a + b, [w * o_s[bi, rows, :] for bi, w in enumerate(ws)])
            o_ref[rows, :] = num / den
            l_ref[rows, :] = top + jnp.log(den)

    return _call(
        body, grid=(width // LANES,), in_specs=[q_spec, k_spec, v_spec], out_specs=[cur, cur],
        out_shape=[jax.ShapeDtypeStruct((s, width), F32)] * 2,
        scratch_shapes=[pltpu.VMEM((len(DILATIONS), s, LANES), F32)] * 2,
        args=(proj, proj, proj), name=name, comm=comm)


def _attn_seq_bwd(proj, do, o, lse, cos, sin_signed, name, comm=None):
    s, width = do.shape
    q_spec, k_spec, v_spec, cur = _qkv_specs(s, width // LANES)
    table = pl.BlockSpec((s, LANES), lambda hb: (0, 0))
    qscale = HEAD_DIM ** -0.5

    def body(q_ref, k_ref, v_ref, do_ref, o_ref, l_ref, cos_ref, sin_ref, dq_out, dk_out, dv_out,
             dq_ref, dk_ref, dv_ref):
        dq_ref[...] = jnp.zeros_like(dq_ref)
        dk_ref[...] = jnp.zeros_like(dk_ref)
        dv_ref[...] = jnp.zeros_like(dv_ref)
        masks = _query_masks()
        for d in DILATIONS:
            def visit(rows, key_rows, valid):
                dob = do_ref[rows, :]
                q2 = _stack_heads(q_ref[rows, :])
                do2 = _stack_heads(dob)
                delta = jnp.sum(_stack_heads(dob * o_ref[rows, :]), axis=1, keepdims=True)
                lse2 = _stacked_lse(l_ref[rows, :])
                keys = jnp.concatenate([k_ref[r, :] for r in key_rows], axis=0)
                vals = jnp.concatenate([v_ref[r, :] for r in key_rows], axis=0)
                p = jnp.where(valid, jnp.exp(_dot_nt(q2, keys) - lse2), 0.0)
                ds = p * (_dot_nt(do2, vals) - delta)
                dq_ref[rows, :] += _unstack_heads(_dot_nn(ds, keys))
                dkk = _dot_tn(ds, q2)
                dvv = _dot_tn(p, do2)
                for i, r in enumerate(key_rows):
                    dk_ref[r, :] += dkk[i * BLOCK:(i + 1) * BLOCK]
                    dv_ref[r, :] += dvv[i * BLOCK:(i + 1) * BLOCK]

            _branch_loops(s // BLOCK, d, visit, 8, masks)
        for c in range(s // MERGE_CHUNK):
            rows = slice(c * MERGE_CHUNK, (c + 1) * MERGE_CHUNK)
            cos, sin = cos_ref[rows, :], sin_ref[rows, :]
            dq, dk = dq_ref[rows, :], dk_ref[rows, :]
            dq_out[rows, :] = ((dq * cos - _partner(dq) * sin) * qscale).astype(BF16)
            dk_out[rows, :] = (dk * cos - _partner(dk) * sin).astype(BF16)
            dv_out[rows, :] = dv_ref[rows, :].astype(BF16)

    return _call(
        body, grid=(width // LANES,), in_specs=[q_spec, k_spec, v_spec, cur, cur, cur, table, table],
        out_specs=[cur, cur, cur], out_shape=[jax.ShapeDtypeStruct((s, width), BF16)] * 3,
        scratch_shapes=[pltpu.VMEM((s, LANES), F32)] * 3,
        args=(proj, proj, proj, do, o, lse, cos, sin_signed), name=name, comm=comm)


def _conv_specs(s, a_block, b_block):
    per = CONV_CHUNK // CONV_HALO
    a_cur = pl.BlockSpec((CONV_CHUNK, LANES), lambda cb, i: (i, a_block + cb))
    b_cur = pl.BlockSpec((CONV_CHUNK, LANES), lambda cb, i: (i, b_block + cb))
    a_halo = pl.BlockSpec((CONV_HALO, LANES), lambda cb, i: (jnp.maximum(i * per - 1, 0), a_block + cb))
    b_halo = pl.BlockSpec((CONV_HALO, LANES), lambda cb, i: (jnp.maximum(i * per - 1, 0), b_block + cb))
    w_spec = pl.BlockSpec((CONV_KERNEL, LANES), lambda cb, i: (0, cb))
    vec = pl.BlockSpec((1, LANES), lambda cb, i: (0, cb))
    out = pl.BlockSpec((CONV_CHUNK, LANES), lambda cb, i: (i, cb))
    return a_cur, b_cur, a_halo, b_halo, w_spec, vec, out


def _fill_glu_window(win, a_ref, b_ref, ah_ref, bh_ref, first):
    halo = ah_ref[...] * _sigmoid(bh_ref[...])
    win[0:CONV_HALO, :] = jnp.where(first, 0.0, halo)
    win[CONV_HALO:, :] = a_ref[...] * _sigmoid(b_ref[...])


def _conv_fwd(proj, a_block, b_block, w, bias, name, comm=None):
    s = proj.shape[0]
    cw = w.shape[1]
    a_cur, b_cur, a_halo, b_halo, w_spec, vec, out = _conv_specs(s, a_block, b_block)
    lead = CONV_HALO - (CONV_KERNEL - 1)

    def body(a_ref, b_ref, ah_ref, bh_ref, w_ref, bias_ref, o_ref, win):
        _fill_glu_window(win, a_ref, b_ref, ah_ref, bh_ref, pl.program_id(1) == 0)
        for sub in range(CONV_CHUNK // CONV_SUB):
            base = sub * CONV_SUB
            acc = jnp.zeros((CONV_SUB, LANES), F32) + bias_ref[...]
            for j in range(CONV_KERNEL):
                acc = acc + w_ref[j:j + 1, :] * win[base + lead + j:base + lead + j + CONV_SUB, :]
            o_ref[base:base + CONV_SUB, :] = acc

    return _call(
        body, grid=(cw // LANES, s // CONV_CHUNK), in_specs=[a_cur, b_cur, a_halo, b_halo, w_spec, vec],
        out_specs=[out], out_shape=[jax.ShapeDtypeStruct((s, cw), F32)],
        scratch_shapes=[pltpu.VMEM((CONV_CHUNK + CONV_HALO, LANES), F32)],
        args=(proj, proj, proj, proj, w, bias), name=name, comm=comm)


def _conv_bwd(proj, a_block, b_block, w, du1, name):
    s = proj.shape[0]
    cw = w.shape[1]
    a_cur, b_cur, a_halo, b_halo, w_spec, vec, out = _conv_specs(s, a_block, b_block)
    per = CONV_CHUNK // CONV_HALO
    n_chunks = s // CONV_CHUNK
    d_next = pl.BlockSpec((CONV_HALO, LANES), lambda cb, i: (jnp.minimum((i + 1) * per, s // CONV_HALO - 1), cb))
    lead = CONV_HALO - (CONV_KERNEL - 1)

    def body(a_ref, b_ref, ah_ref, bh_ref, w_ref, d_ref, dn_ref, da_ref, db_ref, dw_ref, dbias_ref, win, dwin):
        i = pl.program_id(1)
        _fill_glu_window(win, a_ref, b_ref, ah_ref, bh_ref, i == 0)
        dwin[0:CONV_CHUNK, :] = d_ref[...]
        dwin[CONV_CHUNK:, :] = jnp.where(i == n_chunks - 1, 0.0, dn_ref[...])

        @pl.when(i == 0)
        def _():
            dw_ref[...] = jnp.zeros_like(dw_ref)
            dbias_ref[...] = jnp.zeros_like(dbias_ref)

        dbias_ref[...] += _colsum(d_ref[...])
        for sub in range(CONV_CHUNK // CONV_SUB):
            base = sub * CONV_SUB
            dcur = dwin[base:base + CONV_SUB, :]
            du0 = jnp.zeros((CONV_SUB, LANES), F32)
            for j in range(CONV_KERNEL):
                back = CONV_KERNEL - 1 - j
                du0 = du0 + w_ref[j:j + 1, :] * dwin[base + back:base + back + CONV_SUB, :]
                dw_ref[j:j + 1, :] += _colsum(dcur * win[base + lead + j:base + lead + j + CONV_SUB, :])
            av = a_ref[base:base + CONV_SUB, :]
            sig = _sigmoid(b_ref[base:base + CONV_SUB, :])
            da_ref[base:base + CONV_SUB, :] = (du0 * sig).astype(BF16)
            db_ref[base:base + CONV_SUB, :] = (du0 * av * sig * (1.0 - sig)).astype(BF16)

    return pl.pallas_call(
        body, grid=(cw // LANES, n_chunks), in_specs=[a_cur, b_cur, a_halo, b_halo, w_spec, out, d_next],
        out_specs=[out, out, w_spec, vec],
        out_shape=[jax.ShapeDtypeStruct((s, cw), BF16), jax.ShapeDtypeStruct((s, cw), BF16),
                   jax.ShapeDtypeStruct((CONV_KERNEL, cw), F32), jax.ShapeDtypeStruct((1, cw), F32)],
        scratch_shapes=[pltpu.VMEM((CONV_CHUNK + CONV_HALO, LANES), F32)] * 2,
        compiler_params=_params(2), name=name)(proj, proj, proj, proj, w, du1, du1)


def _adamw_math(w, g, m, v):
    m = ADAM_B1 * m + (1.0 - ADAM_B1) * g
    v = ADAM_B2 * v + (1.0 - ADAM_B2) * (g * g)
    m_hat = m / (1.0 - ADAM_B1 ** ADAM_STEP)
    v_hat = v / (1.0 - ADAM_B2 ** ADAM_STEP)
    delta = -ADAM_LR * (m_hat / (jnp.sqrt(v_hat) + ADAM_EPS) + ADAM_WD * w)
    return delta, m, v


def _adamw_big(w, g, m, v, name):
    rows, cols = w.shape
    tile = _tile(rows, 256, 8)
    spec = pl.BlockSpec((tile, cols), lambda i: (i, 0))

    def body(w_ref, g_ref, m_ref, v_ref, d_out, m_out, v_out):
        d_out[...], m_out[...], v_out[...] = _adamw_math(w_ref[...], g_ref[...], m_ref[...], v_ref[...])

    return pl.pallas_call(body, grid=(rows // tile,), in_specs=[spec] * 4, out_specs=[spec] * 3,
                          out_shape=[jax.ShapeDtypeStruct(w.shape, F32)] * 3, compiler_params=_params(1),
                          name=name)(w, g, m, v)


def _adamw_reduced(w, land, m, v, name):
    rows, cols = w.shape
    tile = _tile(rows, 256, 16)
    spec = pl.BlockSpec((tile, cols), lambda i: (i, 0))

    def body(w_ref, l_ref, m_ref, v_ref, g_out, d_out, m_out, v_out):
        g = l_ref[0].astype(F32)
        for q in range(1, N_CHIP):
            g = g + l_ref[q].astype(F32)
        g_out[...] = g
        d_out[...], m_out[...], v_out[...] = _adamw_math(w_ref[...], g, m_ref[...], v_ref[...])

    return pl.pallas_call(body, grid=(rows // tile,),
                          in_specs=[spec, pl.BlockSpec((N_CHIP, tile, cols), lambda i: (0, i, 0)), spec, spec],
                          out_specs=[spec] * 4, out_shape=[jax.ShapeDtypeStruct(w.shape, F32)] * 4,
                          compiler_params=_params(1), name=name)(w, land, m, v)


def _adamw_small(ws, gs, ms, vs, name):
    n = len(ws)

    def body(*refs):
        ins, outs = refs[:4 * n], refs[4 * n:]
        for t in range(n):
            res = _adamw_math(ins[t][...], ins[n + t][...], ins[2 * n + t][...], ins[3 * n + t][...])
            for j in range(3):
                outs[j * n + t][...] = res[j]

    shapes = [jax.ShapeDtypeStruct(w.shape, F32) for w in ws]
    res = pl.pallas_call(body, out_shape=shapes * 3, compiler_params=pltpu.CompilerParams(vmem_limit_bytes=VMEM_LIMIT),
                         name=name)(*ws, *gs, *ms, *vs)
    return res[:n], res[n:2 * n], res[2 * n:]


def _sum_blocks(x, n_blocks, name):
    r = x.shape[0] // n_blocks

    def body(x_ref, o_ref):
        acc = x_ref[0:r, :]
        for b in range(1, n_blocks):
            acc = acc + x_ref[b * r:(b + 1) * r, :]
        o_ref[...] = acc

    return pl.pallas_call(body, out_shape=jax.ShapeDtypeStruct((r, x.shape[1]), F32),
                          compiler_params=pltpu.CompilerParams(vmem_limit_bytes=VMEM_LIMIT), name=name)(x)


def _coords():
    return lax.axis_index("x"), lax.axis_index("y"), lax.axis_index("c")


def _flip(v, bit):
    return 1 - v if bit else v


def _ag_small(x, name):
    r, c = x.shape

    def body(x_ref, o_ref, send, recv, local_sem):
        mx, my, mc = _coords()

        def rows(px, py, pc):
            return o_ref.at[pl.ds(pl.multiple_of((4 * px + 2 * py + pc) * r, 8), r), :]

        local = pltpu.make_async_copy(x_ref, rows(mx, my, mc), local_sem)
        local.start()
        peers = [(_flip(mx, k >> 2 & 1), _flip(my, k >> 1 & 1), _flip(mc, k & 1)) for k in range(1, N_DEV)]
        sends = [pltpu.make_async_remote_copy(x_ref, rows(mx, my, mc), send.at[k], recv.at[k], device_id=p,
                                              device_id_type=MESH) for k, p in enumerate(peers)]
        for cp in sends:
            cp.start()
        for k, p in enumerate(peers):
            pltpu.make_async_remote_copy(x_ref, rows(*p), send.at[k], recv.at[k], device_id=p,
                                         device_id_type=MESH).wait_recv()
        for cp in sends:
            cp.wait_send()
        local.wait()

    vm = pl.BlockSpec(memory_space=pltpu.VMEM)
    return pl.pallas_call(
        body, in_specs=[vm], out_specs=vm, out_shape=jax.ShapeDtypeStruct((N_DEV * r, c), x.dtype),
        scratch_shapes=[pltpu.SemaphoreType.DMA((N_DEV - 1,)), pltpu.SemaphoreType.DMA((N_DEV - 1,)),
                        pltpu.SemaphoreType.DMA(())],
        name=name)(x)


class _GatherSmall:
    mid = None

    def __init__(self, x):
        self.inputs = [x]
        self.out_shapes = [jax.ShapeDtypeStruct((N_DEV * x.shape[0], x.shape[1]), x.dtype)]
        self.scratch = [pltpu.SemaphoreType.DMA((N_DEV - 1,)), pltpu.SemaphoreType.DMA((N_DEV - 1,)),
                        pltpu.SemaphoreType.DMA(())]

    def _plan(self, x_refs, o_refs, sems):
        send, recv, local_sem = sems
        x_ref, o_ref = x_refs[0], o_refs[0]
        r = x_ref.shape[0]
        mx, my, mc = _coords()

        def rows(px, py, pc):
            return o_ref.at[pl.ds(pl.multiple_of((4 * px + 2 * py + pc) * r, 8), r), :]

        peers = [(_flip(mx, k >> 2 & 1), _flip(my, k >> 1 & 1), _flip(mc, k & 1)) for k in range(1, N_DEV)]
        out = [pltpu.make_async_remote_copy(x_ref, rows(mx, my, mc), send.at[k], recv.at[k], device_id=p,
                                            device_id_type=MESH) for k, p in enumerate(peers)]
        arrivals = [pltpu.make_async_remote_copy(x_ref, rows(*p), send.at[k], recv.at[k], device_id=p,
                                                 device_id_type=MESH) for k, p in enumerate(peers)]
        return out, arrivals, pltpu.make_async_copy(x_ref, rows(mx, my, mc), local_sem)

    def start(self, x_refs, o_refs, sems):
        out, _, local = self._plan(x_refs, o_refs, sems)
        local.start()
        for cp in out:
            cp.start()

    def finish(self, x_refs, o_refs, sems):
        out, arrivals, local = self._plan(x_refs, o_refs, sems)
        for cp in arrivals:
            cp.wait_recv()
        for cp in out:
            cp.wait_send()
        local.wait()


class _GatherWeights:
    def __init__(self, shards):
        n_t = len(shards)
        self.inputs = list(shards)
        self.out_shapes = [jax.ShapeDtypeStruct((N_DEV * x.shape[0], x.shape[1]), x.dtype) for x in shards]
        self.scratch = [pltpu.SemaphoreType.DMA((n_t, 8)), pltpu.SemaphoreType.DMA((n_t, 8)),
                        pltpu.SemaphoreType.DMA((n_t,))]

    def _plan(self, x_refs, o_refs, sems):
        send, recv, local_sem = sems
        mx, my, mc = _coords()
        me, sibling = (mx, my, mc), (mx, my, 1 - mc)
        xn, yn, diag = (1 - mx, my), (mx, 1 - my), (1 - mx, 1 - my)

        def rows(t, chip, core, half=None):
            r = x_refs[t].shape[0]
            base = (4 * chip[0] + 2 * chip[1] + core) * r
            if half is None:
                return o_refs[t].at[pl.ds(pl.multiple_of(base, 8), r), :]
            return o_refs[t].at[pl.ds(pl.multiple_of(base + half * (r // 2), 8), r // 2), :]

        def copy(t, k, block, to, src=None):
            return pltpu.make_async_remote_copy(
                src_ref=block if src is None else src, dst_ref=block,
                send_sem=send.at[t, k], recv_sem=recv.at[t, k], device_id=to, device_id_type=MESH)

        def local(t):
            return pltpu.make_async_copy(x_refs[t], rows(t, (mx, my), mc), local_sem.at[t])

        return (mx, my), mc, me, sibling, xn, yn, diag, rows, copy, local

    def start(self, x_refs, o_refs, sems):
        chip, mc, me, sibling, xn, yn, diag, rows, copy, local = self._plan(x_refs, o_refs, sems)
        for t in range(len(x_refs)):
            mine = rows(t, chip, mc)
            local(t).start()
            copy(t, 0, mine, sibling, src=x_refs[t]).start()
            copy(t, 1, mine, (*xn, mc), src=x_refs[t]).start()
            copy(t, 2, mine, (*yn, mc), src=x_refs[t]).start()

    def mid(self, x_refs, o_refs, sems):
        chip, mc, me, sibling, xn, yn, diag, rows, copy, local = self._plan(x_refs, o_refs, sems)
        for t in range(len(x_refs)):
            copy(t, 1, rows(t, xn, mc), me).wait_recv()
            copy(t, 3, rows(t, xn, mc, 0), (*yn, mc)).start()
            copy(t, 5, rows(t, xn, mc), sibling).start()
        for t in range(len(x_refs)):
            copy(t, 2, rows(t, yn, mc), me).wait_recv()
            copy(t, 4, rows(t, yn, mc, 1), (*xn, mc)).start()
            copy(t, 6, rows(t, yn, mc), sibling).start()

    def finish(self, x_refs, o_refs, sems):
        chip, mc, me, sibling, xn, yn, diag, rows, copy, local = self._plan(x_refs, o_refs, sems)
        for t in range(len(x_refs)):
            copy(t, 3, rows(t, diag, mc, 0), me).wait_recv()
            copy(t, 4, rows(t, diag, mc, 1), me).wait_recv()
            copy(t, 7, rows(t, diag, mc), sibling).start()
        for t in range(len(x_refs)):
            copy(t, 0, rows(t, chip, 1 - mc), me).wait_recv()
            copy(t, 5, rows(t, xn, 1 - mc), me).wait_recv()
            copy(t, 6, rows(t, yn, 1 - mc), me).wait_recv()
            copy(t, 7, rows(t, diag, 1 - mc), me).wait_recv()
            mine = rows(t, chip, mc)
            copy(t, 0, mine, sibling, src=x_refs[t]).wait_send()
            copy(t, 1, mine, (*xn, mc), src=x_refs[t]).wait_send()
            copy(t, 2, mine, (*yn, mc), src=x_refs[t]).wait_send()
            copy(t, 3, rows(t, xn, mc, 0), (*yn, mc)).wait_send()
            copy(t, 4, rows(t, yn, mc, 1), (*xn, mc)).wait_send()
            copy(t, 5, rows(t, xn, mc), sibling).wait_send()
            copy(t, 6, rows(t, yn, mc), sibling).wait_send()
            copy(t, 7, rows(t, diag, mc), sibling).wait_send()
            local(t).wait()


class _SiblingExchange:
    mid = None

    def __init__(self, grads):
        n_t = len(grads)
        self.inputs = list(grads)
        self.out_shapes = [jax.ShapeDtypeStruct((N_CHIP,) + g.shape[2:], F32) for g in grads]
        self.scratch = [pltpu.SemaphoreType.DMA((n_t,)), pltpu.SemaphoreType.DMA((n_t,))]

    def _copies(self, g_refs, land, sems):
        send, recv = sems
        mx, my, mc = _coords()
        return [pltpu.make_async_remote_copy(g_refs[t].at[:, 1 - mc], land[t], send.at[t], recv.at[t],
                                             device_id=(mx, my, 1 - mc), device_id_type=MESH)
                for t in range(len(g_refs))]

    def start(self, g_refs, land, sems):
        for cp in self._copies(g_refs, land, sems):
            cp.start()

    def finish(self, g_refs, land, sems):
        for cp in self._copies(g_refs, land, sems):
            cp.wait()


class _Together:
    def __init__(self, *comms):
        self.comms = comms
        self.inputs = [x for c in comms for x in c.inputs]
        self.out_shapes = [x for c in comms for x in c.out_shapes]
        self.scratch = [x for c in comms for x in c.scratch]
        self.mid = self._mid if any(c.mid is not None for c in comms) else None

    def _each(self, phase, cin, cout, sems):
        i = o = s = 0
        for c in self.comms:
            fn = getattr(c, phase)
            ni, no, ns = len(c.inputs), len(c.out_shapes), len(c.scratch)
            if fn is not None:
                fn(cin[i:i + ni], cout[o:o + no], sems[s:s + ns])
            i, o, s = i + ni, o + no, s + ns

    def start(self, cin, cout, sems):
        self._each("start", cin, cout, sems)

    def _mid(self, cin, cout, sems):
        self._each("mid", cin, cout, sems)

    def finish(self, cin, cout, sems):
        self._each("finish", cin, cout, sems)


def _standalone(comm, name):
    def body():
        pass
    return _call(body, grid=(1,), in_specs=[], out_specs=[], out_shape=[], args=(), name=name, comm=comm)[1]


def _chip_partials(g4s, lands, name):
    n_t = len(g4s)
    in_specs, out_specs, out_shape = [], [], []
    for g4 in g4s:
        _, _, r, c = g4.shape
        in_specs.append(pl.BlockSpec((None, None, r, c), lambda q: (q, lax.axis_index("c"), 0, 0)))
        out_specs.append(pl.BlockSpec((None, r, c), lambda q: (q, 0, 0)))
        out_shape.append(jax.ShapeDtypeStruct((N_CHIP, r, c), BF16))
    in_specs += [pl.BlockSpec((None,) + g4.shape[2:], lambda q: (q, 0, 0)) for g4 in g4s]

    def body(*refs):
        for t in range(n_t):
            refs[2 * n_t + t][...] = (refs[t][...] + refs[n_t + t][...]).astype(BF16)

    return pl.pallas_call(body, grid=(N_CHIP,), in_specs=in_specs, out_specs=out_specs, out_shape=out_shape,
                          compiler_params=_params(1), name=name)(*g4s, *lands)


class _ChipExchange:
    mid = None

    def __init__(self, parts):
        n_t = len(parts)
        self.inputs = list(parts)
        self.out_shapes = [jax.ShapeDtypeStruct(p.shape, p.dtype) for p in parts]
        self.scratch = [pltpu.SemaphoreType.DMA((n_t, 3)), pltpu.SemaphoreType.DMA((n_t, 3)),
                        pltpu.SemaphoreType.DMA((n_t,))]

    def _plan(self, p_refs, land, sems):
        send, recv, local_sem = sems
        mx, my, mc = _coords()
        my_chip = 2 * mx + my
        peers = [(_flip(mx, fx), _flip(my, fy)) for fx, fy in ((1, 0), (0, 1), (1, 1))]

        def out(t, k):
            px, py = peers[k]
            return pltpu.make_async_remote_copy(p_refs[t].at[2 * px + py], land[t].at[my_chip], send.at[t, k],
                                                recv.at[t, k], device_id=(px, py, mc), device_id_type=MESH)

        def arrival(t, k):
            px, py = peers[k]
            return pltpu.make_async_remote_copy(p_refs[t].at[my_chip], land[t].at[2 * px + py], send.at[t, k],
                                                recv.at[t, k], device_id=(px, py, mc), device_id_type=MESH)

        def local(t):
            return pltpu.make_async_copy(p_refs[t].at[my_chip], land[t].at[my_chip], local_sem.at[t])

        return out, arrival, local

    def start(self, p_refs, land, sems):
        out, arrival, local = self._plan(p_refs, land, sems)
        for t in range(len(p_refs)):
            local(t).start()
            for k in range(3):
                out(t, k).start()

    def finish(self, p_refs, land, sems):
        out, arrival, local = self._plan(p_refs, land, sems)
        for t in range(len(p_refs)):
            for k in range(3):
                arrival(t, k).wait_recv()
                out(t, k).wait_send()
            local(t).wait()


def _rope_tables(s, width):
    heads = width // HEAD_DIM
    inv_freq = ROPE_THETA ** (-jnp.arange(0, HEAD_DIM, 2, dtype=F32) / HEAD_DIM)
    inv_full = jnp.tile(inv_freq, 2 * heads)
    sign = jnp.tile(jnp.concatenate([-jnp.ones((HALF_HEAD,), F32), jnp.ones((HALF_HEAD,), F32)]), heads)
    ang = jnp.arange(s, dtype=F32)[:, None] * inv_full[None, :]
    return jnp.cos(ang), jnp.sin(ang) * sign[None, :]


def _pad_rows(v, rows):
    return jnp.concatenate([v, jnp.zeros((rows - 1, v.shape[1]), v.dtype)], axis=0)


def kernel(x, c, w_ada, b_ada, ffn1_norm_g, ffn1_w_gate, ffn1_w_up, ffn1_w_down, mix_norm_g, w_in, conv_dw_w, conv_dw_b, conv_ln_g, conv_ln_b, attn_out_g, conv_out_g, w_out, ffn2_norm_g, ffn2_w_gate, ffn2_w_up, ffn2_w_down, final_norm_g, loss_target, m_w_ada, m_b_ada, m_ffn1_norm_g, m_ffn1_w_gate, m_ffn1_w_up, m_ffn1_w_down, m_mix_norm_g, m_w_in, m_conv_dw_w, m_conv_dw_b, m_conv_ln_g, m_conv_ln_b, m_attn_out_g, m_conv_out_g, m_w_out, m_ffn2_norm_g, m_ffn2_w_gate, m_ffn2_w_up, m_ffn2_w_down, m_final_norm_g, v_w_ada, v_b_ada, v_ffn1_norm_g, v_ffn1_w_gate, v_ffn1_w_up, v_ffn1_w_down, v_mix_norm_g, v_w_in, v_conv_dw_w, v_conv_dw_b, v_conv_ln_g, v_conv_ln_b, v_attn_out_g, v_conv_out_g, v_w_out, v_ffn2_norm_g, v_ffn2_w_gate, v_ffn2_w_up, v_ffn2_w_down, v_final_norm_g):
    mx, my, mc = _coords()
    me = 4 * mx + 2 * my + mc
    s, d = x.shape[1], x.shape[2]
    aw = d // 2
    x2, target = x[0], loss_target[0]
    n_mod = w_ada.shape[2] * N_DEV // d
    mod_cols = w_ada.shape[2]

    def shard(w, transpose):
        return (w[0].T if transpose else w[0]).astype(BF16)

    cw_shard = conv_dw_w.shape[3]
    n_taps = CONV_KERNEL * cw_shard
    first_len = -(-(d + n_taps) // LANES) * LANES
    first = jnp.concatenate([c, conv_dw_w[0, :, 0, :].reshape(1, n_taps), jnp.zeros((1, first_len - d - n_taps), F32)], axis=1)
    first_all, wg1 = _standalone(
        _Together(_GatherSmall(_pad_rows(first, 8)), _GatherWeights([shard(ffn1_w_gate, True)])), "ag_first")
    first_all = first_all[0::8]
    c_all = first_all[:, :d]
    conv_w = first_all[:, d:d + n_taps].reshape(N_DEV, CONV_KERNEL, cw_shard).transpose(1, 0, 2).reshape(CONV_KERNEL, aw)

    silu_c = _silu_rows(c_all, "silu_c")
    mod_part = _plain_mm([(silu_c, w_ada[0])], F32, False, mod_cols, "mod_mm")
    mod_all = _ag_small(mod_part, "ag_mod").reshape(N_DEV, N_DEV, mod_cols)
    mod = lax.dynamic_index_in_dim(mod_all, me, axis=1, keepdims=False).reshape(1, n_mod * d) + b_ada
    sh1, sc1, g1, sh2, sc2, g2, sh3, sc3, g3 = [mod[:, i * d:(i + 1) * d] for i in range(n_mod)]

    def split(g):
        return g.reshape(N_CHIP, 2, g.shape[0] // N_DEV, g.shape[1])

    def partials(g4s, lands, tag):
        return _chip_partials(g4s, lands, "chip_partials_" + tag)

    (n1, a1), (wu1,) = _norm_gate(x2, ffn1_norm_g, sc1, sh1, wg1, "ffn1_gate",
                                  comm=_GatherWeights([shard(ffn1_w_up, True)]))
    (silu1, gs1, hid1), (wd1,) = _ffn_up_given_gate(n1, wu1, a1, "ffn1_up",
                                                    comm=_GatherWeights([shard(ffn1_w_down, False)]))
    (h1, f1, n2), (win_t,) = _residual_mm(hid1, wd1, x2, g1, 0.5, "ffn1_down", norm=(mix_norm_g, sc2, sh2),
                                          comm=_GatherWeights([shard(w_in, True)]))
    cos, sin_signed = _rope_tables(s, LANES)
    proj, = _proj_rope(n2, win_t, cos, sin_signed, aw, "proj")
    lanes_per = aw // LANES
    (attn, lse), (wg2, wu2, wd2) = _attn_seq_fwd(
        proj, aw, "attn_fwd",
        comm=_GatherWeights([shard(ffn2_w_gate, True), shard(ffn2_w_up, True), shard(ffn2_w_down, False)]))
    (u1,), (wout,) = _conv_fwd(proj, 3 * lanes_per, 4 * lanes_per, conv_w, conv_dw_b, "conv_fwd",
                               comm=_GatherWeights([shard(w_out, False)]))
    post = (attn_out_g, conv_ln_g, conv_ln_b, conv_out_g)
    y, h2, mix, n3 = _mix_out(attn, u1, post, wout, h1, g2, (ffn2_norm_g, sc3, sh3), "mix_out")
    silu3, gs3, hid3 = _ffn_up(n3, wg2, wu2, "ffn2_up")

    dh3, df3, err2, d_final_g, dg3 = _last_mm_loss(hid3, wd2, h2, g3, 0.5, target, final_norm_g.reshape(1, d),
                                                   "ffn2_down_loss")
    loss_part = jnp.zeros((1, LANES), F32).at[0, 0].set(0.5 * jnp.sum(err2) / d)

    da3, db3 = _ffn_bwd_hidden(df3, wd2, silu3, gs3, "ffn2_hidden_bwd")
    g4_a = [split(_mm_tn(da3, n3, "ffn2_dwg")), split(_mm_tn(db3, n3, "ffn2_dwu")), split(_mm_tn(hid3, df3, "ffn2_dwd"))]
    (dh2, dmix, dsh3, dsc3, dgn3, dg2), land_a = _mm_norm_mod_bwd(
        [(da3, wg2), (db3, wu2)], h2, dh3, ffn2_norm_g, sc3, (mix, g2, 1.0), "ffn2_dn_norm3_bwd", tm=256,
        comm=_SiblingExchange(g4_a))
    parts_a = partials(g4_a, land_a, "a")
    g_wout = _mm_tn(y, dmix, "mix_dwout")
    dattn, du1, d_gains, d_ln = _mix_dy_post_bwd(dmix, wout, attn, u1, post, "mix_dy_post_bwd")
    d_attn_g, d_conv_g, d_ln_g, d_ln_b = d_gains[:, :aw], d_gains[:, aw:], d_ln[:, :aw], d_ln[:, aw:]
    dga, dgb, d_taps, d_conv_b = _conv_bwd(proj, 3 * lanes_per, 4 * lanes_per, conv_w, du1, "conv_bwd")
    (dq, dk, dv), sums_a = _attn_seq_bwd(proj, dattn, attn, lse, cos, sin_signed, "attn_bwd",
                                         comm=_ChipExchange(parts_a))
    dproj = jnp.concatenate([dq, dk, dv, dga, dgb], axis=1)
    g4_b = [split(g_wout), split(_mm_tn(dproj, n2, "mix_dwin"))]
    (dh1, df1, dsh2, dsc2, dgn2, dg1), land_b = _mm_norm_mod_bwd(
        [(dproj, win_t)], h1, dh2, mix_norm_g, sc2, (f1, g1, 0.5), "mix_dn_norm2_bwd", tm=512,
        comm=_SiblingExchange(g4_b))
    parts_b = partials(g4_b, land_b, "b")
    g4_c = [split(_mm_tn(hid1, df1, "ffn1_dwd"))]
    (da1, db1), both = _ffn_bwd_hidden(df1, wd1, silu1, gs1, "ffn1_hidden_bwd",
                                       comm=_Together(_ChipExchange(parts_b), _SiblingExchange(g4_c)))
    sums_b, land_c = both[:2], both[2:]
    parts_c = partials(g4_c, land_c, "c")
    g_wu1, sums_c = _mm_tn(db1, n1, "ffn1_dwu", comm=_ChipExchange(parts_c))
    g4_d = [split(g_wu1)]
    g_wg1, land_d = _mm_tn(da1, n1, "ffn1_dwg", comm=_SiblingExchange(g4_d))
    parts_d = partials(g4_d, land_d, "d")
    g4_e = [split(g_wg1)]
    dn1, both = _plain_mm([(da1, wg1), (db1, wu1)], BF16, False, d, "ffn1_dn",
                          comm=_Together(_ChipExchange(parts_d), _SiblingExchange(g4_e)))
    sums_d, land_e = both[:1], both[1:]
    parts_e = partials(g4_e, land_e, "e")
    (dx, dsh1, dsc1, dgn1), sums_e = _norm_mod_bwd(dn1, x2, dh1, ffn1_norm_g, sc1, "norm1_bwd",
                                                   comm=_ChipExchange(parts_e))

    dmod = jnp.concatenate([dsh1, dsc1, dg1, dsh2, dsc2, dg2, dsh3, dsc3, dg3], axis=1)
    small = [dmod, dgn1, dgn2, dgn3, d_final_g, d_conv_b, d_ln_g, d_ln_b, d_attn_g, d_conv_g,
             d_taps.reshape(1, CONV_KERNEL * aw), loss_part]
    sizes = [v.shape[1] for v in small]
    total = sum(sizes)
    padded = -(-total // (8 * LANES)) * (8 * LANES)
    packed = jnp.concatenate(small + [jnp.zeros((1, padded - total), F32)], axis=1).reshape(8, padded // 8)
    gathered = _ag_small(packed, "ag_small_grads")
    summed = _sum_blocks(gathered, N_DEV, "sum_small_grads").reshape(1, padded)
    offs = [sum(sizes[:i]) for i in range(len(sizes))]
    (g_b_ada, g_gn1, g_gn2, g_gn3, g_final, g_conv_b, g_ln_g, g_ln_b, g_attn_g, g_conv_g, g_taps, loss_row) = [
        summed[:, o:o + n] for o, n in zip(offs, sizes)]
    loss = loss_row[0, 0]
    g_taps_shard = lax.dynamic_slice_in_dim(g_taps.reshape(CONV_KERNEL, aw), me * cw_shard, cw_shard, axis=1)
    dmod_all = gathered.reshape(N_DEV, padded)[:, :n_mod * d]
    dmod_cols = lax.dynamic_slice_in_dim(dmod_all, me * mod_cols, mod_cols, axis=1)
    g_w_ada = _mm_tn(silu_c, dmod_cols, "ada_dw")

    arrived = dict(zip(["ffn2_w_gate", "ffn2_w_up", "ffn2_w_down", "w_out", "w_in", "ffn1_w_down", "ffn1_w_up",
                        "ffn1_w_gate"], list(sums_a) + list(sums_b) + list(sums_c) + list(sums_d) + list(sums_e)))
    transposed = ("ffn1_w_gate", "ffn1_w_up", "w_in", "ffn2_w_gate", "ffn2_w_up")
    grads = {
        "w_ada": g_w_ada, "b_ada": g_b_ada, "ffn1_norm_g": g_gn1, "mix_norm_g": g_gn2, "conv_dw_w": g_taps_shard,
        "conv_dw_b": g_conv_b, "conv_ln_g": g_ln_g, "conv_ln_b": g_ln_b, "attn_out_g": g_attn_g,
        "conv_out_g": g_conv_g, "ffn2_norm_g": g_gn3, "final_norm_g": g_final,
    }
    weights = dict(w_ada=w_ada, b_ada=b_ada, ffn1_norm_g=ffn1_norm_g, ffn1_w_gate=ffn1_w_gate, ffn1_w_up=ffn1_w_up, ffn1_w_down=ffn1_w_down, mix_norm_g=mix_norm_g, w_in=w_in, conv_dw_w=conv_dw_w, conv_dw_b=conv_dw_b, conv_ln_g=conv_ln_g, conv_ln_b=conv_ln_b, attn_out_g=attn_out_g, conv_out_g=conv_out_g, w_out=w_out, ffn2_norm_g=ffn2_norm_g, ffn2_w_gate=ffn2_w_gate, ffn2_w_up=ffn2_w_up, ffn2_w_down=ffn2_w_down, final_norm_g=final_norm_g)
    moms = dict(w_ada=m_w_ada, b_ada=m_b_ada, ffn1_norm_g=m_ffn1_norm_g, ffn1_w_gate=m_ffn1_w_gate, ffn1_w_up=m_ffn1_w_up, ffn1_w_down=m_ffn1_w_down, mix_norm_g=m_mix_norm_g, w_in=m_w_in, conv_dw_w=m_conv_dw_w, conv_dw_b=m_conv_dw_b, conv_ln_g=m_conv_ln_g, conv_ln_b=m_conv_ln_b, attn_out_g=m_attn_out_g, conv_out_g=m_conv_out_g, w_out=m_w_out, ffn2_norm_g=m_ffn2_norm_g, ffn2_w_gate=m_ffn2_w_gate, ffn2_w_up=m_ffn2_w_up, ffn2_w_down=m_ffn2_w_down, final_norm_g=m_final_norm_g)
    vars_ = dict(w_ada=v_w_ada, b_ada=v_b_ada, ffn1_norm_g=v_ffn1_norm_g, ffn1_w_gate=v_ffn1_w_gate, ffn1_w_up=v_ffn1_w_up, ffn1_w_down=v_ffn1_w_down, mix_norm_g=v_mix_norm_g, w_in=v_w_in, conv_dw_w=v_conv_dw_w, conv_dw_b=v_conv_dw_b, conv_ln_g=v_conv_ln_g, conv_ln_b=v_conv_ln_b, attn_out_g=v_attn_out_g, conv_out_g=v_conv_out_g, w_out=v_w_out, ffn2_norm_g=v_ffn2_norm_g, ffn2_w_gate=v_ffn2_w_gate, ffn2_w_up=v_ffn2_w_up, ffn2_w_down=v_ffn2_w_down, final_norm_g=v_final_norm_g)
    names = list(weights)
    big = ["w_ada", "ffn1_w_gate", "ffn1_w_up", "ffn1_w_down", "w_in", "w_out", "ffn2_w_gate", "ffn2_w_up",
           "ffn2_w_down"]
    shape2 = {n: (weights[n].shape[-2] if weights[n].ndim > 1 else 1, weights[n].shape[-1]) for n in names}
    shape2["conv_dw_w"] = (CONV_KERNEL, cw_shard)
    g_out, d_out, m_out, v_out = {}, {}, {}, {}
    for n in big:
        if n in arrived:
            def view(t, n=n):
                return t[0].T if n in transposed else t[0]
            res = _adamw_reduced(view(weights[n]), arrived[n], view(moms[n]), view(vars_[n]), "adamw_" + n)
            g_out[n], d_out[n], m_out[n], v_out[n] = [r.T if n in transposed else r for r in res]
        else:
            g2d = grads[n].reshape(shape2[n])
            res = _adamw_big(weights[n].reshape(shape2[n]), g2d, moms[n].reshape(shape2[n]),
                             vars_[n].reshape(shape2[n]), "adamw_" + n)
            g_out[n], (d_out[n], m_out[n], v_out[n]) = g2d, res
    rest = [n for n in names if n not in big]
    res = _adamw_small([weights[n].reshape(shape2[n]) for n in rest], [grads[n].reshape(shape2[n]) for n in rest],
                       [moms[n].reshape(shape2[n]) for n in rest], [vars_[n].reshape(shape2[n]) for n in rest],
                       "adamw_small")
    for i, n in enumerate(rest):
        g_out[n], d_out[n], m_out[n], v_out[n] = grads[n], res[0][i], res[1][i], res[2][i]

    def shaped(table):
        return [table[n].reshape(weights[n].shape) for n in names]

    return (loss, dx.reshape(x.shape), *shaped(g_out), *shaped(d_out), *shaped(m_out), *shaped(v_out))
```

```python
import functools

import jax
import jax.numpy as jnp
from jax import lax
from jax.experimental import pallas as pl
from jax.experimental.pallas import tpu as pltpu

F32 = jnp.float32
BF16 = jnp.bfloat16
MESH = pl.DeviceIdType.MESH
ANY = pl.BlockSpec(memory_space=pl.ANY)

N_DEV = 8
N_CHIP = 4
HEAD_DIM = 64
HALF_HEAD = HEAD_DIM // 2
LANES = 128
BLOCK = 128
DILATIONS = (1, 4, 16)
MERGE_CHUNK = 512
ROPE_THETA = 10000.0
CONV_KERNEL = 31
CONV_HALO = 32
CONV_CHUNK = 512
CONV_SUB = 128
RMS_EPS = 1e-6
LN_EPS = 1e-5
ADAM_LR = 0.001
ADAM_B1 = 0.9
ADAM_B2 = 0.999
ADAM_EPS = 1e-08
ADAM_WD = 0.01
ADAM_STEP = 10
VMEM_LIMIT = 56 * 1024 * 1024
NEG = -1e30


def _params(n_axes):
    return pltpu.CompilerParams(dimension_semantics=("arbitrary",) * n_axes, vmem_limit_bytes=VMEM_LIMIT)


def _tile(n, target, unit):
    best = None
    for t in range(unit, min(n, target) + 1, unit):
        if n % t == 0:
            best = t
    return best if best is not None else n


def _sigmoid(x):
    return 0.5 * (jnp.tanh(0.5 * x) + 1.0)


def _call(body, *, grid, in_specs, out_specs, out_shape, args, name, scratch_shapes=(), comm=None):
    params = _params(len(grid))
    if comm is None:
        return pl.pallas_call(body, grid=grid, in_specs=list(in_specs), out_specs=list(out_specs),
                              out_shape=list(out_shape), scratch_shapes=list(scratch_shapes),
                              compiler_params=params, name=name)(*args)
    n_in, n_out, n_scr = len(args), len(out_shape), len(scratch_shapes)
    c_in, c_out = len(comm.inputs), len(comm.out_shapes)
    steps = 1
    for g in grid:
        steps *= g

    def hosted(*refs):
        pos = 0
        parts = []
        for size in (n_in, c_in, n_out, c_out, n_scr, len(comm.scratch)):
            parts.append(refs[pos:pos + size])
            pos += size
        ins, cin, outs, cout, scr, cscr = parts
        step = 0
        for axis, g in enumerate(grid):
            step = step * g + pl.program_id(axis)

        @pl.when(step == 0)
        def _():
            comm.start(cin, cout, cscr)

        body(*ins, *outs, *scr)
        if comm.mid is not None and steps >= 4:
            @pl.when(step == steps // 2)
            def _():
                comm.mid(cin, cout, cscr)

        @pl.when(step == steps - 1)
        def _():
            if comm.mid is not None and steps < 4:
                comm.mid(cin, cout, cscr)
            comm.finish(cin, cout, cscr)

    res = pl.pallas_call(
        hosted, grid=grid, in_specs=list(in_specs) + [ANY] * c_in, out_specs=list(out_specs) + [ANY] * c_out,
        out_shape=list(out_shape) + list(comm.out_shapes), scratch_shapes=list(scratch_shapes) + list(comm.scratch),
        compiler_params=params, name=name)(*args, *comm.inputs)
    return res[:n_out], res[n_out:]


def _rows(fn, rows_in, vecs_in, rows_out, vecs_out, *, tile, name, comm=None):
    norm = [r if isinstance(r, tuple) else (r, r.shape[1], 0) for r in rows_in]
    n_rows = norm[0][0].shape[0]
    n_tiles = n_rows // tile
    in_specs, args = [], []
    for arr, width, cb in norm:
        in_specs.append(pl.BlockSpec((tile, width), functools.partial(lambda i, cb: (i, cb), cb=cb)))
        args.append(arr)
    for v in vecs_in:
        in_specs.append(pl.BlockSpec((1, v.shape[1]), lambda i: (0, 0)))
        args.append(v)
    out_shape = [jax.ShapeDtypeStruct((n_rows, w), dt) for w, dt in rows_out]
    out_shape += [jax.ShapeDtypeStruct((1, w), F32) for w in vecs_out]
    out_specs = [pl.BlockSpec((tile, w), lambda i: (i, 0)) for w, _ in rows_out]
    out_specs += [pl.BlockSpec((1, w), lambda i: (0, 0)) for w in vecs_out]
    n_in, n_ro = len(args), len(rows_out)

    def body(*refs):
        vals = [r[...] for r in refs[:n_in]]
        outs = refs[n_in:]
        row_vals, vec_vals = fn(*vals)
        for ref, val in zip(outs[:n_ro], row_vals):
            if isinstance(val, tuple):
                w = val[0].shape[1]
                for j, piece in enumerate(val):
                    ref[:, j * w:(j + 1) * w] = piece.astype(ref.dtype)
            else:
                ref[...] = val.astype(ref.dtype)
        if vecs_out:
            @pl.when(pl.program_id(0) == 0)
            def _():
                for ref in outs[n_ro:]:
                    ref[...] = jnp.zeros_like(ref)
            for ref, val in zip(outs[n_ro:], vec_vals):
                ref[...] += val

    return _call(body, grid=(n_tiles,), in_specs=in_specs, out_specs=out_specs, out_shape=out_shape, args=args,
                 name=name, comm=comm)


def _colsum(x):
    return jnp.sum(x, axis=0, keepdims=True)


def _rms_stats(h):
    r = lax.rsqrt(jnp.mean(h * h, axis=-1, keepdims=True) + RMS_EPS)
    return r, h * r


def _rms_back(r, xn, dxn):
    return r * (dxn - xn * jnp.mean(dxn * xn, axis=-1, keepdims=True))


def _branch_back(dh, f, gate, coef):
    return (coef * gate) * dh, coef * _colsum(f.astype(F32) * dh)


def _norm_mod_back(dn, h, dh_in, gain, scale):
    dn = dn.astype(F32)
    r, xn = _rms_stats(h)
    y = xn * gain
    dy = dn * (1.0 + scale)
    dh = dh_in + _rms_back(r, xn, dy * gain)
    return dh, [_colsum(dn), _colsum(dn * y), _colsum(dy * xn)]


def _norm_mod_bwd(dn, h, dh_in, gain, scale, name, comm=None):
    d = h.shape[1]

    def fn(dn, h, dh_in, gain, scale):
        dh, vecs = _norm_mod_back(dn, h, dh_in, gain, scale)
        return [dh], vecs
    return _rows(fn, [dn, h, dh_in], [gain, scale], [(d, F32)], [d, d, d], tile=256, name=name, comm=comm)


def _mm_norm_mod_bwd(pairs, h, dh_in, gain, scale, branch, name, tm, comm=None):
    f, gate, coef = branch

    def epi(accs, ex, vc):
        dh, vecs = _norm_mod_back(accs[0], ex[0], ex[1], vc[0], vc[1])
        df, dgate = _branch_back(dh, ex[2], vc[2], coef)
        return [dh, df] + vecs + [dgate]
    return _mm([pairs], epi, [h, dh_in, f], [gain, scale, gate], [F32, BF16], trans_rhs=False, tm=tm,
               tn=h.shape[1], name=name, n_sums=4, comm=comm)


def _last_mm_loss(lhs, w, res, gate, coef, target, gain, name):
    d = w.shape[1]

    def epi(accs, ex, vc):
        f = accs[0]
        h = ex[0] + (coef * vc[0]) * f
        r, xn = _rms_stats(h)
        err = xn * vc[1] - ex[1]
        dout = err * (1.0 / d)
        dh = _rms_back(r, xn, dout * vc[1])
        df, dgate = _branch_back(dh, f, vc[0], coef)
        return [dh, df, _colsum(err * err), _colsum(dout * xn), dgate]
    return _mm([[(lhs, w)]], epi, [res, target], [gate, gain], [F32, BF16], trans_rhs=False, tm=256, tn=d,
               name=name, n_sums=3)


def _partner(x):
    if x.shape[1] > LANES:
        return jnp.concatenate([_partner(x[:, c:c + LANES]) for c in range(0, x.shape[1], LANES)], axis=1)
    lane = lax.broadcasted_iota(jnp.int32, x.shape, 1) % HEAD_DIM
    return jnp.where(lane < HALF_HEAD, pltpu.roll(x, LANES - HALF_HEAD, 1), pltpu.roll(x, HALF_HEAD, 1))


def _proj_rope(n, w_t, cos, sin_signed, width, name, comm=None):
    s, kdim = n.shape
    n_cols = w_t.shape[0]
    tm = _tile(s, 1024, 8)
    qscale = HEAD_DIM ** -0.5

    chunk = _tile(tm, 256, 8)

    def body(n_ref, w_ref, cos_ref, sin_ref, o_ref):
        j = pl.program_id(0)

        def products(rows):
            return lax.dot_general(n_ref[rows, :].astype(BF16), w_ref[...].astype(BF16), (((1,), (1,)), ((), ())),
                                   preferred_element_type=F32)

        @pl.when(j >= 2)
        def _():
            for c in range(tm // chunk):
                rows = slice(c * chunk, (c + 1) * chunk)
                o_ref[rows, :] = products(rows)

        @pl.when(j < 2)
        def _():
            scale = jnp.where(j == 0, qscale, 1.0)
            for c in range(tm // chunk):
                rows = slice(c * chunk, (c + 1) * chunk)
                acc = products(rows)
                cos = jnp.tile(cos_ref[rows, :], (1, width // LANES))
                sin = jnp.tile(sin_ref[rows, :], (1, width // LANES))
                o_ref[rows, :] = scale * (acc * cos + _partner(acc) * sin)

    table = pl.BlockSpec((tm, LANES), lambda j, i: (jnp.where(j < 2, i, 0), 0))
    return _call(
        body, grid=(n_cols // width, s // tm),
        in_specs=[pl.BlockSpec((tm, kdim), lambda j, i: (i, 0)), pl.BlockSpec((width, kdim), lambda j, i: (j, 0)),
                  table, table],
        out_specs=[pl.BlockSpec((tm, width), lambda j, i: (i, j))],
        out_shape=[jax.ShapeDtypeStruct((s, n_cols), F32)], args=(n, w_t, cos, sin_signed), name=name, comm=comm)


def _mix_post(attn, u1, attn_g, ln_g, ln_b, conv_g):
    _, xa = _rms_stats(attn)
    mu = jnp.mean(u1, axis=-1, keepdims=True)
    xc = u1 - mu
    rstd = lax.rsqrt(jnp.mean(xc * xc, axis=-1, keepdims=True) + LN_EPS)
    u2 = (xc * rstd) * ln_g + ln_b
    u3 = u2 * _sigmoid(u2)
    _, x3 = _rms_stats(u3)
    return jnp.concatenate([xa * attn_g, x3 * conv_g], axis=1)


def _mix_post_back(dy, attn, u1, attn_g, ln_g, ln_b, conv_g):
    w = attn.shape[1]
    dya, dyc = dy[:, :w], dy[:, w:]
    ra, xa = _rms_stats(attn)
    dattn = _rms_back(ra, xa, dya * attn_g)
    mu = jnp.mean(u1, axis=-1, keepdims=True)
    xc = u1 - mu
    rstd = lax.rsqrt(jnp.mean(xc * xc, axis=-1, keepdims=True) + LN_EPS)
    xh = xc * rstd
    u2 = xh * ln_g + ln_b
    sig = _sigmoid(u2)
    u3 = u2 * sig
    r3, x3 = _rms_stats(u3)
    du3 = _rms_back(r3, x3, dyc * conv_g)
    du2 = du3 * (sig + u3 * (1.0 - sig))
    dxh = du2 * ln_g
    du1 = rstd * (dxh - jnp.mean(dxh, axis=-1, keepdims=True) - xh * jnp.mean(dxh * xh, axis=-1, keepdims=True))
    return dattn, du1, [_colsum(dya * xa), _colsum(dyc * x3), _colsum(du2 * xh), _colsum(du2)]


def _silu_rows(c_all, name):
    def fn(c):
        return [c * _sigmoid(c)], []
    return _rows(fn, [c_all], [], [(c_all.shape[1], BF16)], [], tile=c_all.shape[0], name=name)[0]


def _mm(groups, epi, extras, vecs, outs, *, trans_rhs, tm, tn, name, n_sums=0, pre=None, pre_inputs=(),
        comm=None):
    m = (pre_inputs[0] if pre is not None else groups[0][0][0]).shape[0]
    n = groups[0][0][1].shape[0] if trans_rhs else groups[0][0][1].shape[1]
    tm, tn = min(tm, m), min(tn, n)
    in_specs, args, uses_pre = [], [], []
    for grp in groups:
        for lhs, rhs in grp:
            k = rhs.shape[1] if trans_rhs else rhs.shape[0]
            uses_pre.append(lhs is None)
            if lhs is not None:
                in_specs.append(pl.BlockSpec((tm, k), lambda j, i: (i, 0)))
                args.append(lhs)
            in_specs.append(pl.BlockSpec((tn, k), lambda j, i: (j, 0)) if trans_rhs
                            else pl.BlockSpec((k, tn), lambda j, i: (0, j)))
            args.append(rhs)
    n_mm = len(args)
    for p in pre_inputs:
        in_specs.append(pl.BlockSpec((tm, p.shape[1]), lambda j, i: (i, 0)))
        args.append(p)
    for e in extras:
        in_specs.append(pl.BlockSpec((tm, tn), lambda j, i: (i, j)) if e.shape[1] == n
                        else pl.BlockSpec((tm, e.shape[1]), lambda j, i: (i, 0)))
        args.append(e)
    for v in vecs:
        in_specs.append(pl.BlockSpec((1, tn), lambda j, i: (0, j)) if v.shape[1] == n
                        else pl.BlockSpec((1, v.shape[1]), lambda j, i: (0, 0)))
        args.append(v)
    sizes = [len(g) for g in groups]
    n_pre, n_ex, n_vec = len(pre_inputs), len(extras), len(vecs)
    dims = (((1,), (1,)), ((), ())) if trans_rhs else (((1,), (0,)), ((), ()))
    out_specs, out_shape = [], []
    if pre is not None:
        k_pre = args[n_mm - 1].shape[1] if trans_rhs else args[n_mm - 1].shape[0]
        out_specs.append(pl.BlockSpec((tm, k_pre), lambda j, i: (i, 0)))
        out_shape.append(jax.ShapeDtypeStruct((m, k_pre), BF16))
    for o in outs:
        dt, width = o if isinstance(o, tuple) else (o, n)
        out_specs.append(pl.BlockSpec((tm, tn), lambda j, i: (i, j)) if width == n
                         else pl.BlockSpec((tm, width), lambda j, i: (i, 0)))
        out_shape.append(jax.ShapeDtypeStruct((m, width), dt))
    n_tiles_out = len(out_specs)
    out_specs += [pl.BlockSpec((1, tn), lambda j, i: (0, j))] * n_sums
    out_shape += [jax.ShapeDtypeStruct((1, n), F32)] * n_sums

    def body(*refs):
        ins = refs[:n_mm + n_pre + n_ex + n_vec]
        out_refs = refs[n_mm + n_pre + n_ex + n_vec:]
        vc = [r[...] for r in ins[n_mm + n_pre + n_ex:]]
        vals = []
        made = None
        if pre is not None:
            made = pre([r[...] for r in ins[n_mm:n_mm + n_pre]], vc).astype(BF16)
            vals.append(made)
        accs, pos, pair = [], 0, 0
        for size in sizes:
            acc = None
            for _ in range(size):
                if uses_pre[pair]:
                    lhs_tile = made
                else:
                    lhs_tile = ins[pos][...].astype(BF16)
                    pos += 1
                part = lax.dot_general(lhs_tile, ins[pos][...].astype(BF16), dims, preferred_element_type=F32)
                acc = part if acc is None else acc + part
                pos += 1
                pair += 1
            accs.append(acc)
        ex = [r[...] for r in ins[n_mm + n_pre:n_mm + n_pre + n_ex]]
        vals += epi(accs, ex, vc)
        for ref, val in zip(out_refs[:n_tiles_out], vals):
            ref[...] = val.astype(ref.dtype)
        if n_sums:
            @pl.when(pl.program_id(1) == 0)
            def _():
                for ref in out_refs[n_tiles_out:]:
                    ref[...] = jnp.zeros_like(ref)
            for ref, val in zip(out_refs[n_tiles_out:], vals[n_tiles_out:]):
                ref[...] += val

    return _call(body, grid=(n // tn, m // tm), in_specs=in_specs, out_specs=out_specs, out_shape=out_shape,
                 args=args, name=name, comm=comm)


def _mm_tn(lhs, rhs, name, comm=None):
    t, a = lhs.shape
    b = rhs.shape[1]
    ta = a if a <= 1536 else _tile(a, 1536, LANES)
    tk = _tile(t, 2048, 8)

    def body(l_ref, r_ref, o_ref):
        @pl.when(pl.program_id(1) == 0)
        def _():
            o_ref[...] = jnp.zeros_like(o_ref)
        o_ref[...] += lax.dot_general(l_ref[...].astype(BF16), r_ref[...].astype(BF16), (((0,), (0,)), ((), ())),
                                      preferred_element_type=F32)

    res = _call(body, grid=(a // ta, t // tk),
                in_specs=[pl.BlockSpec((tk, ta), lambda i, k: (k, i)), pl.BlockSpec((tk, b), lambda i, k: (k, 0))],
                out_specs=[pl.BlockSpec((ta, b), lambda i, k: (i, 0))], out_shape=[jax.ShapeDtypeStruct((a, b), F32)],
                args=(lhs, rhs), name=name, comm=comm)
    return res[0] if comm is None else (res[0][0], res[1])


def _ffn_tn(f):
    return _tile(f, 1536, LANES)


def _swiglu_parts(a, b):
    sig = _sigmoid(a)
    silu = a * sig
    return [silu, b * (sig + silu * (1.0 - sig)), silu * b]


def _ffn_up(n, wg_t, wu_t, name, comm=None):
    def epi(accs, ex, vc):
        return _swiglu_parts(accs[0], accs[1])
    return _mm([[(n, wg_t)], [(n, wu_t)]], epi, [], [], [BF16, BF16, BF16], trans_rhs=True, tm=512,
               tn=_ffn_tn(wg_t.shape[0]), name=name, comm=comm)


def _norm_ffn_up(h, gain, scale, shift, wg_t, wu_t, name, comm=None):
    def pre(tiles, vc):
        _, xn = _rms_stats(tiles[0])
        return (xn * vc[0]) * (1.0 + vc[1]) + vc[2]

    def epi(accs, ex, vc):
        return _swiglu_parts(accs[0], accs[1])
    return _mm([[(None, wg_t)], [(None, wu_t)]], epi, [], [gain, scale, shift], [BF16, BF16, BF16], trans_rhs=True,
               tm=256, tn=wg_t.shape[0], name=name, pre=pre, pre_inputs=[h], comm=comm)


def _norm_gate(h, gain, scale, shift, wg_t, name, comm=None):
    def pre(tiles, vc):
        _, xn = _rms_stats(tiles[0])
        return (xn * vc[0]) * (1.0 + vc[1]) + vc[2]

    def epi(accs, ex, vc):
        return [accs[0]]
    return _mm([[(None, wg_t)]], epi, [], [gain, scale, shift], [BF16], trans_rhs=True, tm=512,
               tn=wg_t.shape[0], name=name, pre=pre, pre_inputs=[h], comm=comm)


def _mix_out(attn, u1, post, w, res, gate, norm, name):
    def pre(tiles, vc):
        return _mix_post(tiles[0], tiles[1], *vc[4:8])

    def epi(accs, ex, vc):
        h = ex[0] + vc[0] * accs[0]
        _, xn = _rms_stats(h)
        return [h, accs[0], (xn * vc[1]) * (1.0 + vc[2]) + vc[3]]
    return _mm([[(None, w)]], epi, [res], [gate] + list(norm) + list(post), [F32, BF16, BF16], trans_rhs=False,
               tm=512, tn=w.shape[1], name=name, pre=pre, pre_inputs=[attn, u1])


def _mix_dy_post_bwd(dmix, w, attn, u1, post, name):
    width = attn.shape[1]

    def epi(accs, ex, vc):
        dattn, du1, sums = _mix_post_back(accs[0], ex[0], ex[1], *vc)
        return [dattn, du1, jnp.concatenate(sums[0:2], axis=1), jnp.concatenate(sums[2:4], axis=1)]
    return _mm([[(dmix, w)]], epi, [attn, u1], list(post), [(F32, width), (F32, width)], trans_rhs=True, tm=256,
               tn=w.shape[0], name=name, n_sums=2)


def _ffn_up_given_gate(n, wu_t, a, name, comm=None):
    def epi(accs, ex, vc):
        return _swiglu_parts(ex[0].astype(F32), accs[0])
    return _mm([[(n, wu_t)]], epi, [a], [], [BF16, BF16, BF16], trans_rhs=True, tm=512,
               tn=_ffn_tn(wu_t.shape[0]), name=name, comm=comm)


def _residual_mm(lhs, w, res, gate, coef, name, norm=None, comm=None):
    def epi(accs, ex, vc):
        h = ex[0] + (coef * vc[0]) * accs[0]
        if norm is None:
            return [h, accs[0]]
        _, xn = _rms_stats(h)
        return [h, accs[0], (xn * vc[1]) * (1.0 + vc[2]) + vc[3]]
    vecs = [gate] + (list(norm) if norm is not None else [])
    outs = [F32, BF16] + ([BF16] if norm is not None else [])
    return _mm([[(lhs, w)]], epi, [res], vecs, outs, trans_rhs=False, tm=512, tn=w.shape[1], name=name, comm=comm)


def _ffn_bwd_hidden(df, wd, dhid_db, dhid_da, name, comm=None):
    def epi(accs, ex, vc):
        return [accs[0] * ex[1].astype(F32), accs[0] * ex[0].astype(F32)]
    return _mm([[(df, wd)]], epi, [dhid_db, dhid_da], [], [BF16, BF16], trans_rhs=True, tm=512,
               tn=_ffn_tn(wd.shape[0]), name=name, comm=comm)


def _plain_mm(pairs, out_dtype, trans_rhs, tn, name, tm=512, comm=None):
    def epi(accs, ex, vc):
        return [accs[0]]
    res = _mm([pairs], epi, [], [], [out_dtype], trans_rhs=trans_rhs, tm=tm, tn=tn, name=name, comm=comm)
    return res[0] if comm is None else (res[0][0], res[1])


HEADS_PER_TILE = LANES // HEAD_DIM


def _stack_heads(x):
    lane = lax.broadcasted_iota(jnp.int32, (1, LANES), 1)
    return jnp.concatenate([x * (lane // HEAD_DIM == h).astype(F32) for h in range(HEADS_PER_TILE)], axis=0)


def _unstack_heads(y):
    r = y.shape[0] // HEADS_PER_TILE
    lane = lax.broadcasted_iota(jnp.int32, (r, y.shape[1]), 1)
    out = y[0:r]
    for h in range(1, HEADS_PER_TILE):
        out = jnp.where(lane // HEAD_DIM == h, y[h * r:(h + 1) * r], out)
    return out


def _stacked_lse(lb):
    return jnp.concatenate([_lane_pick(lb, h) for h in range(HEADS_PER_TILE)], axis=0)


def _band_masks(n_row_blocks, n_col_blocks):
    shape = (n_row_blocks * BLOCK, n_col_blocks * BLOCK)
    qi = lax.broadcasted_iota(jnp.int32, shape, 0) % BLOCK
    kj = lax.broadcasted_iota(jnp.int32, shape, 1) % BLOCK
    return kj <= qi, kj >= qi


def _query_masks():
    first_valid, _ = _band_masks(HEADS_PER_TILE, 1)
    same_ok, before_ok = _band_masks(HEADS_PER_TILE, 2)
    is_cur = lax.broadcasted_iota(jnp.int32, same_ok.shape, 1) >= BLOCK
    return first_valid, jnp.logical_and(is_cur, same_ok), jnp.logical_and(jnp.logical_not(is_cur), before_ok)


def _dot_nt(a, b):
    return lax.dot_general(a.astype(BF16), b.astype(BF16), (((1,), (1,)), ((), ())), preferred_element_type=F32)


def _dot_nn(a, b):
    return lax.dot_general(a.astype(BF16), b.astype(BF16), (((1,), (0,)), ((), ())), preferred_element_type=F32)


def _dot_tn(a, b):
    return lax.dot_general(a.astype(BF16), b.astype(BF16), (((0,), (0,)), ((), ())), preferred_element_type=F32)


def _lane_pick(x, h):
    lane = lax.broadcasted_iota(jnp.int32, x.shape, 1)
    return jnp.sum(jnp.where(lane == h * HEAD_DIM, x, 0.0), axis=1, keepdims=True)


def _block_rows(idx, d):
    span = BLOCK * d
    q0 = (idx // d) * span + idx % d
    return pl.ds(q0, BLOCK, stride=d), pl.ds(q0 - span, BLOCK, stride=d)


def _branch_loops(n_blocks, d, visit, unroll, masks):
    first_valid, cur_part, prev_part = masks
    if d % unroll == 0 and (n_blocks - d) % unroll == 0:
        full_valid = jnp.logical_or(cur_part, prev_part)

        def first(idx, carry):
            rows = pl.ds(idx, BLOCK, stride=d)
            visit(rows, [rows], first_valid)
            return carry

        def rest(idx, carry):
            rows, prev = _block_rows(idx, d)
            visit(rows, [prev, rows], full_valid)
            return carry

        lax.fori_loop(0, d, first, 0, unroll=unroll)
        lax.fori_loop(d, n_blocks, rest, 0, unroll=unroll)
        return

    def every(idx, carry):
        span = BLOCK * d
        q0 = (idx // d) * span + idx % d
        has_prev = idx >= d
        rows = pl.ds(q0, BLOCK, stride=d)
        prev = pl.ds(jnp.where(has_prev, q0 - span, q0), BLOCK, stride=d)
        visit(rows, [prev, rows], jnp.logical_or(cur_part, jnp.logical_and(prev_part, has_prev)))
        return carry

    lax.fori_loop(0, n_blocks, every, 0, unroll=unroll)


def _qkv_specs(s, tiles):
    q, k, v = [pl.BlockSpec((s, LANES), functools.partial(lambda hb, off: (0, off + hb), off=i * tiles))
               for i in range(3)]
    return q, k, v, pl.BlockSpec((s, LANES), lambda hb: (0, hb))


def _attn_seq_fwd(proj, width, name, comm=None):
    s = proj.shape[0]
    q_spec, k_spec, v_spec, cur = _qkv_specs(s, width // LANES)

    def body(q_ref, k_ref, v_ref, o_ref, l_ref, o_s, l_s):
        masks = _query_masks()
        for bi, d in enumerate(DILATIONS):
            def visit(rows, key_rows, valid, bi=bi):
                q2 = _stack_heads(q_ref[rows, :])
                keys = jnp.concatenate([k_ref[r, :] for r in key_rows], axis=0)
                vals = jnp.concatenate([v_ref[r, :] for r in key_rows], axis=0)
                sc = jnp.where(valid, _dot_nt(q2, keys), NEG)
                mx = jnp.max(sc, axis=1, keepdims=True)
                p = jnp.exp(sc - mx)
                den = jnp.sum(p, axis=1, keepdims=True)
                o_s[bi, rows, :] = _unstack_heads(_dot_nn(p, vals) / den)
                l_s[bi, rows, :] = _unstack_heads(jnp.broadcast_to(mx + jnp.log(den), (q2.shape[0], LANES)))

            _branch_loops(s // BLOCK, d, visit, 8, masks)
        for c in range(s // MERGE_CHUNK):
            rows = slice(c * MERGE_CHUNK, (c + 1) * MERGE_CHUNK)
            ls = [l_s[bi, rows, :] for bi in range(len(DILATIONS))]
            top = functools.reduce(jnp.maximum, ls)
            ws = [jnp.exp(l - top) for l in ls]
            den = functools.reduce(lambda a, b: a + b, ws)
            num = functools.reduce(lambda a, b: a + b, [w * o_s[bi, rows, :] for bi, w in enumerate(ws)])
            o_ref[rows, :] = num / den
            l_ref[rows, :] = top + jnp.log(den)

    return _call(
        body, grid=(width // LANES,), in_specs=[q_spec, k_spec, v_spec], out_specs=[cur, cur],
        out_shape=[jax.ShapeDtypeStruct((s, width), F32)] * 2,
        scratch_shapes=[pltpu.VMEM((len(DILATIONS), s, LANES), F32)] * 2,
        args=(proj, proj, proj), name=name, comm=comm)


def _attn_seq_bwd(proj, do, o, lse, cos, sin_signed, name, comm=None):
    s, width = do.shape
    q_spec, k_spec, v_spec, cur = _qkv_specs(s, width // LANES)
    table = pl.BlockSpec((s, LANES), lambda hb: (0, 0))
    qscale = HEAD_DIM ** -0.5

    def body(q_ref, k_ref, v_ref, do_ref, o_ref, l_ref, cos_ref, sin_ref, dq_out, dk_out, dv_out,
             dq_ref, dk_ref, dv_ref):
        dq_ref[...] = jnp.zeros_like(dq_ref)
        dk_ref[...] = jnp.zeros_like(dk_ref)
        dv_ref[...] = jnp.zeros_like(dv_ref)
        masks = _query_masks()
        for d in DILATIONS:
            def visit(rows, key_rows, valid):
                dob = do_ref[rows, :]
                q2 = _stack_heads(q_ref[rows, :])
                do2 = _stack_heads(dob)
                delta = jnp.sum(_stack_heads(dob * o_ref[rows, :]), axis=1, keepdims=True)
                lse2 = _stacked_lse(l_ref[rows, :])
                keys = jnp.concatenate([k_ref[r, :] for r in key_rows], axis=0)
                vals = jnp.concatenate([v_ref[r, :] for r in key_rows], axis=0)
                p = jnp.where(valid, jnp.exp(_dot_nt(q2, keys) - lse2), 0.0)
                ds = p * (_dot_nt(do2, vals) - delta)
                dq_ref[rows, :] += _unstack_heads(_dot_nn(ds, keys))
                dkk = _dot_tn(ds, q2)
                dvv = _dot_tn(p, do2)
                for i, r in enumerate(key_rows):
                    dk_ref[r, :] += dkk[i * BLOCK:(i + 1) * BLOCK]
                    dv_ref[r, :] += dvv[i * BLOCK:(i + 1) * BLOCK]

            _branch_loops(s // BLOCK, d, visit, 8, masks)
        for c in range(s // MERGE_CHUNK):
            rows = slice(c * MERGE_CHUNK, (c + 1) * MERGE_CHUNK)
            cos, sin = cos_ref[rows, :], sin_ref[rows, :]
            dq, dk = dq_ref[rows, :], dk_ref[rows, :]
            dq_out[rows, :] = ((dq * cos - _partner(dq) * sin) * qscale).astype(BF16)
            dk_out[rows, :] = (dk * cos - _partner(dk) * sin).astype(BF16)
            dv_out[rows, :] = dv_ref[rows, :].astype(BF16)

    return _call(
        body, grid=(width // LANES,), in_specs=[q_spec, k_spec, v_spec, cur, cur, cur, table, table],
        out_specs=[cur, cur, cur], out_shape=[jax.ShapeDtypeStruct((s, width), BF16)] * 3,
        scratch_shapes=[pltpu.VMEM((s, LANES), F32)] * 3,
        args=(proj, proj, proj, do, o, lse, cos, sin_signed), name=name, comm=comm)


def _conv_specs(s, a_block, b_block):
    per = CONV_CHUNK // CONV_HALO
    a_cur = pl.BlockSpec((CONV_CHUNK, LANES), lambda cb, i: (i, a_block + cb))
    b_cur = pl.BlockSpec((CONV_CHUNK, LANES), lambda cb, i: (i, b_block + cb))
    a_halo = pl.BlockSpec((CONV_HALO, LANES), lambda cb, i: (jnp.maximum(i * per - 1, 0), a_block + cb))
    b_halo = pl.BlockSpec((CONV_HALO, LANES), lambda cb, i: (jnp.maximum(i * per - 1, 0), b_block + cb))
    w_spec = pl.BlockSpec((CONV_KERNEL, LANES), lambda cb, i: (0, cb))
    vec = pl.BlockSpec((1, LANES), lambda cb, i: (0, cb))
    out = pl.BlockSpec((CONV_CHUNK, LANES), lambda cb, i: (i, cb))
    return a_cur, b_cur, a_halo, b_halo, w_spec, vec, out


def _fill_glu_window(win, a_ref, b_ref, ah_ref, bh_ref, first):
    halo = ah_ref[...] * _sigmoid(bh_ref[...])
    win[0:CONV_HALO, :] = jnp.where(first, 0.0, halo)
    win[CONV_HALO:, :] = a_ref[...] * _sigmoid(b_ref[...])


def _conv_fwd(proj, a_block, b_block, w, bias, name, comm=None):
    s = proj.shape[0]
    cw = w.shape[1]
    a_cur, b_cur, a_halo, b_halo, w_spec, vec, out = _conv_specs(s, a_block, b_block)
    lead = CONV_HALO - (CONV_KERNEL - 1)

    def body(a_ref, b_ref, ah_ref, bh_ref, w_ref, bias_ref, o_ref, win):
        _fill_glu_window(win, a_ref, b_ref, ah_ref, bh_ref, pl.program_id(1) == 0)
        for sub in range(CONV_CHUNK // CONV_SUB):
            base = sub * CONV_SUB
            acc = jnp.zeros((CONV_SUB, LANES), F32) + bias_ref[...]
            for j in range(CONV_KERNEL):
                acc = acc + w_ref[j:j + 1, :] * win[base + lead + j:base + lead + j + CONV_SUB, :]
            o_ref[base:base + CONV_SUB, :] = acc

    return _call(
        body, grid=(cw // LANES, s // CONV_CHUNK), in_specs=[a_cur, b_cur, a_halo, b_halo, w_spec, vec],
        out_specs=[out], out_shape=[jax.ShapeDtypeStruct((s, cw), F32)],
        scratch_shapes=[pltpu.VMEM((CONV_CHUNK + CONV_HALO, LANES), F32)],
        args=(proj, proj, proj, proj, w, bias), name=name, comm=comm)


def _conv_bwd(proj, a_block, b_block, w, du1, name):
    s = proj.shape[0]
    cw = w.shape[1]
    a_cur, b_cur, a_halo, b_halo, w_spec, vec, out = _conv_specs(s, a_block, b_block)
    per = CONV_CHUNK // CONV_HALO
    n_chunks = s // CONV_CHUNK
    d_next = pl.BlockSpec((CONV_HALO, LANES), lambda cb, i: (jnp.minimum((i + 1) * per, s // CONV_HALO - 1), cb))
    lead = CONV_HALO - (CONV_KERNEL - 1)

    def body(a_ref, b_ref, ah_ref, bh_ref, w_ref, d_ref, dn_ref, da_ref, db_ref, dw_ref, dbias_ref, win, dwin):
        i = pl.program_id(1)
        _fill_glu_window(win, a_ref, b_ref, ah_ref, bh_ref, i == 0)
        dwin[0:CONV_CHUNK, :] = d_ref[...]
        dwin[CONV_CHUNK:, :] = jnp.where(i == n_chunks - 1, 0.0, dn_ref[...])

        @pl.when(i == 0)
        def _():
            dw_ref[...] = jnp.zeros_like(dw_ref)
            dbias_ref[...] = jnp.zeros_like(dbias_ref)

        dbias_ref[...] += _colsum(d_ref[...])
        for sub in range(CONV_CHUNK // CONV_SUB):
            base = sub * CONV_SUB
            dcur = dwin[base:base + CONV_SUB, :]
            du0 = jnp.zeros((CONV_SUB, LANES), F32)
            for j in range(CONV_KERNEL):
                back = CONV_KERNEL - 1 - j
                du0 = du0 + w_ref[j:j + 1, :] * dwin[base + back:base + back + CONV_SUB, :]
                dw_ref[j:j + 1, :] += _colsum(dcur * win[base + lead + j:base + lead + j + CONV_SUB, :])
            av = a_ref[base:base + CONV_SUB, :]
            sig = _sigmoid(b_ref[base:base + CONV_SUB, :])
            da_ref[base:base + CONV_SUB, :] = (du0 * sig).astype(BF16)
            db_ref[base:base + CONV_SUB, :] = (du0 * av * sig * (1.0 - sig)).astype(BF16)

    return pl.pallas_call(
        body, grid=(cw // LANES, n_chunks), in_specs=[a_cur, b_cur, a_halo, b_halo, w_spec, out, d_next],
        out_specs=[out, out, w_spec, vec],
        out_shape=[jax.ShapeDtypeStruct((s, cw), BF16), jax.ShapeDtypeStruct((s, cw), BF16),
                   jax.ShapeDtypeStruct((CONV_KERNEL, cw), F32), jax.ShapeDtypeStruct((1, cw), F32)],
        scratch_shapes=[pltpu.VMEM((CONV_CHUNK + CONV_HALO, LANES), F32)] * 2,
        compiler_params=_params(2), name=name)(proj, proj, proj, proj, w, du1, du1)


def _adamw_math(w, g, m, v):
    m = ADAM_B1 * m + (1.0 - ADAM_B1) * g
    v = ADAM_B2 * v + (1.0 - ADAM_B2) * (g * g)
    m_hat = m / (1.0 - ADAM_B1 ** ADAM_STEP)
    v_hat = v / (1.0 - ADAM_B2 ** ADAM_STEP)
    delta = -ADAM_LR * (m_hat / (jnp.sqrt(v_hat) + ADAM_EPS) + ADAM_WD * w)
    return delta, m, v


def _adamw_big(w, g, m, v, name):
    rows, cols = w.shape
    tile = _tile(rows, 256, 8)
    spec = pl.BlockSpec((tile, cols), lambda i: (i, 0))

    def body(w_ref, g_ref, m_ref, v_ref, d_out, m_out, v_out):
        d_out[...], m_out[...], v_out[...] = _adamw_math(w_ref[...], g_ref[...], m_ref[...], v_ref[...])

    return pl.pallas_call(body, grid=(rows // tile,), in_specs=[spec] * 4, out_specs=[spec] * 3,
                          out_shape=[jax.ShapeDtypeStruct(w.shape, F32)] * 3, compiler_params=_params(1),
                          name=name)(w, g, m, v)


def _adamw_reduced(w, land, m, v, name):
    rows, cols = w.shape
    tile = _tile(rows, 256, 16)
    spec = pl.BlockSpec((tile, cols), lambda i: (i, 0))

    def body(w_ref, l_ref, m_ref, v_ref, g_out, d_out, m_out, v_out):
        g = l_ref[0].astype(F32)
        for q in range(1, N_CHIP):
            g = g + l_ref[q].astype(F32)
        g_out[...] = g
        d_out[...], m_out[...], v_out[...] = _adamw_math(w_ref[...], g, m_ref[...], v_ref[...])

    return pl.pallas_call(body, grid=(rows // tile,),
                          in_specs=[spec, pl.BlockSpec((N_CHIP, tile, cols), lambda i: (0, i, 0)), spec, spec],
                          out_specs=[spec] * 4, out_shape=[jax.ShapeDtypeStruct(w.shape, F32)] * 4,
                          compiler_params=_params(1), name=name)(w, land, m, v)


def _adamw_small(ws, gs, ms, vs, name):
    n = len(ws)

    def body(*refs):
        ins, outs = refs[:4 * n], refs[4 * n:]
        for t in range(n):
            res = _adamw_math(ins[t][...], ins[n + t][...], ins[2 * n + t][...], ins[3 * n + t][...])
            for j in range(3):
                outs[j * n + t][...] = res[j]

    shapes = [jax.ShapeDtypeStruct(w.shape, F32) for w in ws]
    res = pl.pallas_call(body, out_shape=shapes * 3, compiler_params=pltpu.CompilerParams(vmem_limit_bytes=VMEM_LIMIT),
                         name=name)(*ws, *gs, *ms, *vs)
    return res[:n], res[n:2 * n], res[2 * n:]


def _sum_blocks(x, n_blocks, name):
    r = x.shape[0] // n_blocks

    def body(x_ref, o_ref):
        acc = x_ref[0:r, :]
        for b in range(1, n_blocks):
            acc = acc + x_ref[b * r:(b + 1) * r, :]
        o_ref[...] = acc

    return pl.pallas_call(body, out_shape=jax.ShapeDtypeStruct((r, x.shape[1]), F32),
                          compiler_params=pltpu.CompilerParams(vmem_limit_bytes=VMEM_LIMIT), name=name)(x)


def _coords():
    return lax.axis_index("x"), lax.axis_index("y"), lax.axis_index("c")


def _flip(v, bit):
    return 1 - v if bit else v


def _ag_small(x, name):
    r, c = x.shape

    def body(x_ref, o_ref, send, recv, local_sem):
        mx, my, mc = _coords()

        def rows(px, py, pc):
            return o_ref.at[pl.ds(pl.multiple_of((4 * px + 2 * py + pc) * r, 8), r), :]

        local = pltpu.make_async_copy(x_ref, rows(mx, my, mc), local_sem)
        local.start()
        peers = [(_flip(mx, k >> 2 & 1), _flip(my, k >> 1 & 1), _flip(mc, k & 1)) for k in range(1, N_DEV)]
        sends = [pltpu.make_async_remote_copy(x_ref, rows(mx, my, mc), send.at[k], recv.at[k], device_id=p,
                                              device_id_type=MESH) for k, p in enumerate(peers)]
        for cp in sends:
            cp.start()
        for k, p in enumerate(peers):
            pltpu.make_async_remote_copy(x_ref, rows(*p), send.at[k], recv.at[k], device_id=p,
                                         device_id_type=MESH).wait_recv()
        for cp in sends:
            cp.wait_send()
        local.wait()

    vm = pl.BlockSpec(memory_space=pltpu.VMEM)
    return pl.pallas_call(
        body, in_specs=[vm], out_specs=vm, out_shape=jax.ShapeDtypeStruct((N_DEV * r, c), x.dtype),
        scratch_shapes=[pltpu.SemaphoreType.DMA((N_DEV - 1,)), pltpu.SemaphoreType.DMA((N_DEV - 1,)),
                        pltpu.SemaphoreType.DMA(())],
        name=name)(x)


class _GatherSmall:
    mid = None

    def __init__(self, x):
        self.inputs = [x]
        self.out_shapes = [jax.ShapeDtypeStruct((N_DEV * x.shape[0], x.shape[1]), x.dtype)]
        self.scratch = [pltpu.SemaphoreType.DMA((N_DEV - 1,)), pltpu.SemaphoreType.DMA((N_DEV - 1,)),
                        pltpu.SemaphoreType.DMA(())]

    def _plan(self, x_refs, o_refs, sems):
        send, recv, local_sem = sems
        x_ref, o_ref = x_refs[0], o_refs[0]
        r = x_ref.shape[0]
        mx, my, mc = _coords()

        def rows(px, py, pc):
            return o_ref.at[pl.ds(pl.multiple_of((4 * px + 2 * py + pc) * r, 8), r), :]

        peers = [(_flip(mx, k >> 2 & 1), _flip(my, k >> 1 & 1), _flip(mc, k & 1)) for k in range(1, N_DEV)]
        out = [pltpu.make_async_remote_copy(x_ref, rows(mx, my, mc), send.at[k], recv.at[k], device_id=p,
                                            device_id_type=MESH) for k, p in enumerate(peers)]
        arrivals = [pltpu.make_async_remote_copy(x_ref, rows(*p), send.at[k], recv.at[k], device_id=p,
                                                 device_id_type=MESH) for k, p in enumerate(peers)]
        return out, arrivals, pltpu.make_async_copy(x_ref, rows(mx, my, mc), local_sem)

    def start(self, x_refs, o_refs, sems):
        out, _, local = self._plan(x_refs, o_refs, sems)
        local.start()
        for cp in out:
            cp.start()

    def finish(self, x_refs, o_refs, sems):
        out, arrivals, local = self._plan(x_refs, o_refs, sems)
        for cp in arrivals:
            cp.wait_recv()
        for cp in out:
            cp.wait_send()
        local.wait()


class _GatherWeights:
    def __init__(self, shards):
        n_t = len(shards)
        self.inputs = list(shards)
        self.out_shapes = [jax.ShapeDtypeStruct((N_DEV * x.shape[0], x.shape[1]), x.dtype) for x in shards]
        self.scratch = [pltpu.SemaphoreType.DMA((n_t, 8)), pltpu.SemaphoreType.DMA((n_t, 8)),
                        pltpu.SemaphoreType.DMA((n_t,))]

    def _plan(self, x_refs, o_refs, sems):
        send, recv, local_sem = sems
        mx, my, mc = _coords()
        me, sibling = (mx, my, mc), (mx, my, 1 - mc)
        xn, yn, diag = (1 - mx, my), (mx, 1 - my), (1 - mx, 1 - my)

        def rows(t, chip, core, half=None):
            r = x_refs[t].shape[0]
            base = (4 * chip[0] + 2 * chip[1] + core) * r
            if half is None:
                return o_refs[t].at[pl.ds(pl.multiple_of(base, 8), r), :]
            return o_refs[t].at[pl.ds(pl.multiple_of(base + half * (r // 2), 8), r // 2), :]

        def copy(t, k, block, to, src=None):
            return pltpu.make_async_remote_copy(
                src_ref=block if src is None else src, dst_ref=block,
                send_sem=send.at[t, k], recv_sem=recv.at[t, k], device_id=to, device_id_type=MESH)

        def local(t):
            return pltpu.make_async_copy(x_refs[t], rows(t, (mx, my), mc), local_sem.at[t])

        return (mx, my), mc, me, sibling, xn, yn, diag, rows, copy, local

    def start(self, x_refs, o_refs, sems):
        chip, mc, me, sibling, xn, yn, diag, rows, copy, local = self._plan(x_refs, o_refs, sems)
        for t in range(len(x_refs)):
            mine = rows(t, chip, mc)
            local(t).start()
            copy(t, 0, mine, sibling, src=x_refs[t]).start()
            copy(t, 1, mine, (*xn, mc), src=x_refs[t]).start()
            copy(t, 2, mine, (*yn, mc), src=x_refs[t]).start()

    def mid(self, x_refs, o_refs, sems):
        chip, mc, me, sibling, xn, yn, diag, rows, copy, local = self._plan(x_refs, o_refs, sems)
        for t in range(len(x_refs)):
            copy(t, 1, rows(t, xn, mc), me).wait_recv()
            copy(t, 3, rows(t, xn, mc, 0), (*yn, mc)).start()
            copy(t, 5, rows(t, xn, mc), sibling).start()
        for t in range(len(x_refs)):
            copy(t, 2, rows(t, yn, mc), me).wait_recv()
            copy(t, 4, rows(t, yn, mc, 1), (*xn, mc)).start()
            copy(t, 6, rows(t, yn, mc), sibling).start()

    def finish(self, x_refs, o_refs, sems):
        chip, mc, me, sibling, xn, yn, diag, rows, copy, local = self._plan(x_refs, o_refs, sems)
        for t in range(len(x_refs)):
            copy(t, 3, rows(t, diag, mc, 0), me).wait_recv()
            copy(t, 4, rows(t, diag, mc, 1), me).wait_recv()
            copy(t, 7, rows(t, diag, mc), sibling).start()
        for t in range(len(x_refs)):
            copy(t, 0, rows(t, chip, 1 - mc), me).wait_recv()
            copy(t, 5, rows(t, xn, 1 - mc), me).wait_recv()
            copy(t, 6, rows(t, yn, 1 - mc), me).wait_recv()
            copy(t, 7, rows(t, diag, 1 - mc), me).wait_recv()
            mine = rows(t, chip, mc)
            copy(t, 0, mine, sibling, src=x_refs[t]).wait_send()
            copy(t, 1, mine, (*xn, mc), src=x_refs[t]).wait_send()
            copy(t, 2, mine, (*yn, mc), src=x_refs[t]).wait_send()
            copy(t, 3, rows(t, xn, mc, 0), (*yn, mc)).wait_send()
            copy(t, 4, rows(t, yn, mc, 1), (*xn, mc)).wait_send()
            copy(t, 5, rows(t, xn, mc), sibling).wait_send()
            copy(t, 6, rows(t, yn, mc), sibling).wait_send()
            copy(t, 7, rows(t, diag, mc), sibling).wait_send()
            local(t).wait()


class _SiblingExchange:
    mid = None

    def __init__(self, grads):
        n_t = len(grads)
        self.inputs = list(grads)
        self.out_shapes = [jax.ShapeDtypeStruct((N_CHIP,) + g.shape[2:], F32) for g in grads]
        self.scratch = [pltpu.SemaphoreType.DMA((n_t,)), pltpu.SemaphoreType.DMA((n_t,))]

    def _copies(self, g_refs, land, sems):
        send, recv = sems
        mx, my, mc = _coords()
        return [pltpu.make_async_remote_copy(g_refs[t].at[:, 1 - mc], land[t], send.at[t], recv.at[t],
                                             device_id=(mx, my, 1 - mc), device_id_type=MESH)
                for t in range(len(g_refs))]

    def start(self, g_refs, land, sems):
        for cp in self._copies(g_refs, land, sems):
            cp.start()

    def finish(self, g_refs, land, sems):
        for cp in self._copies(g_refs, land, sems):
            cp.wait()


class _Together:
    def __init__(self, *comms):
        self.comms = comms
        self.inputs = [x for c in comms for x in c.inputs]
        self.out_shapes = [x for c in comms for x in c.out_shapes]
        self.scratch = [x for c in comms for x in c.scratch]
        self.mid = self._mid if any(c.mid is not None for c in comms) else None

    def _each(self, phase, cin, cout, sems):
        i = o = s = 0
        for c in self.comms:
            fn = getattr(c, phase)
            ni, no, ns = len(c.inputs), len(c.out_shapes), len(c.scratch)
            if fn is not None:
                fn(cin[i:i + ni], cout[o:o + no], sems[s:s + ns])
            i, o, s = i + ni, o + no, s + ns

    def start(self, cin, cout, sems):
        self._each("start", cin, cout, sems)

    def _mid(self, cin, cout, sems):
        self._each("mid", cin, cout, sems)

    def finish(self, cin, cout, sems):
        self._each("finish", cin, cout, sems)


def _standalone(comm, name):
    def body():
        pass
    return _call(body, grid=(1,), in_specs=[], out_specs=[], out_shape=[], args=(), name=name, comm=comm)[1]


def _chip_partials(g4s, lands, name):
    n_t = len(g4s)
    in_specs, out_specs, out_shape = [], [], []
    for g4 in g4s:
        _, _, r, c = g4.shape
        in_specs.append(pl.BlockSpec((None, None, r, c), lambda q: (q, lax.axis_index("c"), 0, 0)))
        out_specs.append(pl.BlockSpec((None, r, c), lambda q: (q, 0, 0)))
        out_shape.append(jax.ShapeDtypeStruct((N_CHIP, r, c), BF16))
    in_specs += [pl.BlockSpec((None,) + g4.shape[2:], lambda q: (q, 0, 0)) for g4 in g4s]

    def body(*refs):
        for t in range(n_t):
            refs[2 * n_t + t][...] = (refs[t][...] + refs[n_t + t][...]).astype(BF16)

    return pl.pallas_call(body, grid=(N_CHIP,), in_specs=in_specs, out_specs=out_specs, out_shape=out_shape,
                          compiler_params=_params(1), name=name)(*g4s, *lands)


class _ChipExchange:
    mid = None

    def __init__(self, parts):
        n_t = len(parts)
        self.inputs = list(parts)
        self.out_shapes = [jax.ShapeDtypeStruct(p.shape, p.dtype) for p in parts]
        self.scratch = [pltpu.SemaphoreType.DMA((n_t, 3)), pltpu.SemaphoreType.DMA((n_t, 3)),
                        pltpu.SemaphoreType.DMA((n_t,))]

    def _plan(self, p_refs, land, sems):
        send, recv, local_sem = sems
        mx, my, mc = _coords()
        my_chip = 2 * mx + my
        peers = [(_flip(mx, fx), _flip(my, fy)) for fx, fy in ((1, 0), (0, 1), (1, 1))]

        def out(t, k):
            px, py = peers[k]
            return pltpu.make_async_remote_copy(p_refs[t].at[2 * px + py], land[t].at[my_chip], send.at[t, k],
                                                recv.at[t, k], device_id=(px, py, mc), device_id_type=MESH)

        def arrival(t, k):
            px, py = peers[k]
            return pltpu.make_async_remote_copy(p_refs[t].at[my_chip], land[t].at[2 * px + py], send.at[t, k],
                                                recv.at[t, k], device_id=(px, py, mc), device_id_type=MESH)

        def local(t):
            return pltpu.make_async_copy(p_refs[t].at[my_chip], land[t].at[my_chip], local_sem.at[t])

        return out, arrival, local

    def start(self, p_refs, land, sems):
        out, arrival, local = self._plan(p_refs, land, sems)
        for t in range(len(p_refs)):
            local(t).start()
            for k in range(3):
                out(t, k).start()

    def finish(self, p_refs, land, sems):
        out, arrival, local = self._plan(p_refs, land, sems)
        for t in range(len(p_refs)):
            for k in range(3):
                arrival(t, k).wait_recv()
                out(t, k).wait_send()
            local(t).wait()


def _rope_tables(s, width):
    heads = width // HEAD_DIM
    inv_freq = ROPE_THETA ** (-jnp.arange(0, HEAD_DIM, 2, dtype=F32) / HEAD_DIM)
    inv_full = jnp.tile(inv_freq, 2 * heads)
    sign = jnp.tile(jnp.concatenate([-jnp.ones((HALF_HEAD,), F32), jnp.ones((HALF_HEAD,), F32)]), heads)
    ang = jnp.arange(s, dtype=F32)[:, None] * inv_full[None, :]
    return jnp.cos(ang), jnp.sin(ang) * sign[None, :]


def _pad_rows(v, rows):
    return jnp.concatenate([v, jnp.zeros((rows - 1, v.shape[1]), v.dtype)], axis=0)


def kernel(x, c, w_ada, b_ada, ffn1_norm_g, ffn1_w_gate, ffn1_w_up, ffn1_w_down, mix_norm_g, w_in, conv_dw_w, conv_dw_b, conv_ln_g, conv_ln_b, attn_out_g, conv_out_g, w_out, ffn2_norm_g, ffn2_w_gate, ffn2_w_up, ffn2_w_down, final_norm_g, loss_target, m_w_ada, m_b_ada, m_ffn1_norm_g, m_ffn1_w_gate, m_ffn1_w_up, m_ffn1_w_down, m_mix_norm_g, m_w_in, m_conv_dw_w, m_conv_dw_b, m_conv_ln_g, m_conv_ln_b, m_attn_out_g, m_conv_out_g, m_w_out, m_ffn2_norm_g, m_ffn2_w_gate, m_ffn2_w_up, m_ffn2_w_down, m_final_norm_g, v_w_ada, v_b_ada, v_ffn1_norm_g, v_ffn1_w_gate, v_ffn1_w_up, v_ffn1_w_down, v_mix_norm_g, v_w_in, v_conv_dw_w, v_conv_dw_b, v_conv_ln_g, v_conv_ln_b, v_attn_out_g, v_conv_out_g, v_w_out, v_ffn2_norm_g, v_ffn2_w_gate, v_ffn2_w_up, v_ffn2_w_down, v_final_norm_g):
    mx, my, mc = _coords()
    me = 4 * mx + 2 * my + mc
    s, d = x.shape[1], x.shape[2]
    aw = d // 2
    x2, target = x[0], loss_target[0]
    n_mod = w_ada.shape[2] * N_DEV // d
    mod_cols = w_ada.shape[2]

    def shard(w, transpose):
        return (w[0].T if transpose else w[0]).astype(BF16)

    cw_shard = conv_dw_w.shape[3]
    n_taps = CONV_KERNEL * cw_shard
    first_len = -(-(d + n_taps) // LANES) * LANES
    first = jnp.concatenate([c, conv_dw_w[0, :, 0, :].reshape(1, n_taps), jnp.zeros((1, first_len - d - n_taps), F32)], axis=1)
    first_all, wg1, wu1 = _standalone(
        _Together(_GatherSmall(_pad_rows(first, 8)),
                  _GatherWeights([shard(ffn1_w_gate, True), shard(ffn1_w_up, True)])), "ag_first")
    first_all = first_all[0::8]
    c_all = first_all[:, :d]
    conv_w = first_all[:, d:d + n_taps].reshape(N_DEV, CONV_KERNEL, cw_shard).transpose(1, 0, 2).reshape(CONV_KERNEL, aw)

    silu_c = _silu_rows(c_all, "silu_c")
    mod_part = _plain_mm([(silu_c, w_ada[0])], F32, False, mod_cols, "mod_mm")
    mod_all = _ag_small(mod_part, "ag_mod").reshape(N_DEV, N_DEV, mod_cols)
    mod = lax.dynamic_index_in_dim(mod_all, me, axis=1, keepdims=False).reshape(1, n_mod * d) + b_ada
    sh1, sc1, g1, sh2, sc2, g2, sh3, sc3, g3 = [mod[:, i * d:(i + 1) * d] for i in range(n_mod)]

    def split(g):
        return g.reshape(N_CHIP, 2, g.shape[0] // N_DEV, g.shape[1])

    def partials(g4s, lands, tag):
        return _chip_partials(g4s, lands, "chip_partials_" + tag)

    (n1, silu1, gs1, hid1), (wd1,) = _norm_ffn_up(x2, ffn1_norm_g, sc1, sh1, wg1, wu1, "ffn1_up",
                                                  comm=_GatherWeights([shard(ffn1_w_down, False)]))
    (h1, f1, n2), (win_t,) = _residual_mm(hid1, wd1, x2, g1, 0.5, "ffn1_down", norm=(mix_norm_g, sc2, sh2),
                                          comm=_GatherWeights([shard(w_in, True)]))
    cos, sin_signed = _rope_tables(s, LANES)
    proj, = _proj_rope(n2, win_t, cos, sin_signed, aw, "proj")
    lanes_per = aw // LANES
    (attn, lse), (wg2, wu2, wd2) = _attn_seq_fwd(
        proj, aw, "attn_fwd",
        comm=_GatherWeights([shard(ffn2_w_gate, True), shard(ffn2_w_up, True), shard(ffn2_w_down, False)]))
    (u1,), (wout,) = _conv_fwd(proj, 3 * lanes_per, 4 * lanes_per, conv_w, conv_dw_b, "conv_fwd",
                               comm=_GatherWeights([shard(w_out, False)]))
    post = (attn_out_g, conv_ln_g, conv_ln_b, conv_out_g)
    y, h2, mix, n3 = _mix_out(attn, u1, post, wout, h1, g2, (ffn2_norm_g, sc3, sh3), "mix_out")
    silu3, gs3, hid3 = _ffn_up(n3, wg2, wu2, "ffn2_up")

    dh3, df3, err2, d_final_g, dg3 = _last_mm_loss(hid3, wd2, h2, g3, 0.5, target, final_norm_g.reshape(1, d),
                                                   "ffn2_down_loss")
    loss_part = jnp.zeros((1, LANES), F32).at[0, 0].set(0.5 * jnp.sum(err2) / d)

    da3, db3 = _ffn_bwd_hidden(df3, wd2, silu3, gs3, "ffn2_hidden_bwd")
    g4_a = [split(_mm_tn(da3, n3, "ffn2_dwg")), split(_mm_tn(db3, n3, "ffn2_dwu")), split(_mm_tn(hid3, df3, "ffn2_dwd"))]
    (dh2, dmix, dsh3, dsc3, dgn3, dg2), land_a = _mm_norm_mod_bwd(
        [(da3, wg2), (db3, wu2)], h2, dh3, ffn2_norm_g, sc3, (mix, g2, 1.0), "ffn2_dn_norm3_bwd", tm=256,
        comm=_SiblingExchange(g4_a))
    parts_a = partials(g4_a, land_a, "a")
    g_wout = _mm_tn(y, dmix, "mix_dwout")
    dattn, du1, d_gains, d_ln = _mix_dy_post_bwd(dmix, wout, attn, u1, post, "mix_dy_post_bwd")
    d_attn_g, d_conv_g, d_ln_g, d_ln_b = d_gains[:, :aw], d_gains[:, aw:], d_ln[:, :aw], d_ln[:, aw:]
    dga, dgb, d_taps, d_conv_b = _conv_bwd(proj, 3 * lanes_per, 4 * lanes_per, conv_w, du1, "conv_bwd")
    (dq, dk, dv), sums_a = _attn_seq_bwd(proj, dattn, attn, lse, cos, sin_signed, "attn_bwd",
                                         comm=_ChipExchange(parts_a))
    dproj = jnp.concatenate([dq, dk, dv, dga, dgb], axis=1)
    g4_b = [split(g_wout), split(_mm_tn(dproj, n2, "mix_dwin"))]
    (dh1, df1, dsh2, dsc2, dgn2, dg1), land_b = _mm_norm_mod_bwd(
        [(dproj, win_t)], h1, dh2, mix_norm_g, sc2, (f1, g1, 0.5), "mix_dn_norm2_bwd", tm=512,
        comm=_SiblingExchange(g4_b))
    parts_b = partials(g4_b, land_b, "b")
    g4_c = [split(_mm_tn(hid1, df1, "ffn1_dwd"))]
    (da1, db1), both = _ffn_bwd_hidden(df1, wd1, silu1, gs1, "ffn1_hidden_bwd",
                                       comm=_Together(_ChipExchange(parts_b), _SiblingExchange(g4_c)))
    sums_b, land_c = both[:2], both[2:]
    parts_c = partials(g4_c, land_c, "c")
    g_wu1, sums_c = _mm_tn(db1, n1, "ffn1_dwu", comm=_ChipExchange(parts_c))
    g4_d = [split(g_wu1)]
    g_wg1, land_d = _mm_tn(da1, n1, "ffn1_dwg", comm=_SiblingExchange(g4_d))
    parts_d = partials(g4_d, land_d, "d")
    g4_e = [split(g_wg1)]
    dn1, both = _plain_mm([(da1, wg1), (db1, wu1)], BF16, False, d, "ffn1_dn",
                          comm=_Together(_ChipExchange(parts_d), _SiblingExchange(g4_e)))
    sums_d, land_e = both[:1], both[1:]
    parts_e = partials(g4_e, land_e, "e")
    (dx, dsh1, dsc1, dgn1), sums_e = _norm_mod_bwd(dn1, x2, dh1, ffn1_norm_g, sc1, "norm1_bwd",
                                                   comm=_ChipExchange(parts_e))

    dmod = jnp.concatenate([dsh1, dsc1, dg1, dsh2, dsc2, dg2, dsh3, dsc3, dg3], axis=1)
    small = [dmod, dgn1, dgn2, dgn3, d_final_g, d_conv_b, d_ln_g, d_ln_b, d_attn_g, d_conv_g,
             d_taps.reshape(1, CONV_KERNEL * aw), loss_part]
    sizes = [v.shape[1] for v in small]
    total = sum(sizes)
    padded = -(-total // (8 * LANES)) * (8 * LANES)
    packed = jnp.concatenate(small + [jnp.zeros((1, padded - total), F32)], axis=1).reshape(8, padded // 8)
    gathered = _ag_small(packed, "ag_small_grads")
    summed = _sum_blocks(gathered, N_DEV, "sum_small_grads").reshape(1, padded)
    offs = [sum(sizes[:i]) for i in range(len(sizes))]
    (g_b_ada, g_gn1, g_gn2, g_gn3, g_final, g_conv_b, g_ln_g, g_ln_b, g_attn_g, g_conv_g, g_taps, loss_row) = [
        summed[:, o:o + n] for o, n in zip(offs, sizes)]
    loss = loss_row[0, 0]
    g_taps_shard = lax.dynamic_slice_in_dim(g_taps.reshape(CONV_KERNEL, aw), me * cw_shard, cw_shard, axis=1)
    dmod_all = gathered.reshape(N_DEV, padded)[:, :n_mod * d]
    dmod_cols = lax.dynamic_slice_in_dim(dmod_all, me * mod_cols, mod_cols, axis=1)
    g_w_ada = _mm_tn(silu_c, dmod_cols, "ada_dw")

    arrived = dict(zip(["ffn2_w_gate", "ffn2_w_up", "ffn2_w_down", "w_out", "w_in", "ffn1_w_down", "ffn1_w_up",
                        "ffn1_w_gate"], list(sums_a) + list(sums_b) + list(sums_c) + list(sums_d) + list(sums_e)))
    transposed = ("ffn1_w_gate", "ffn1_w_up", "w_in", "ffn2_w_gate", "ffn2_w_up")
    grads = {
        "w_ada": g_w_ada, "b_ada": g_b_ada, "ffn1_norm_g": g_gn1, "mix_norm_g": g_gn2, "conv_dw_w": g_taps_shard,
        "conv_dw_b": g_conv_b, "conv_ln_g": g_ln_g, "conv_ln_b": g_ln_b, "attn_out_g": g_attn_g,
        "conv_out_g": g_conv_g, "ffn2_norm_g": g_gn3, "final_norm_g": g_final,
    }
    weights = dict(w_ada=w_ada, b_ada=b_ada, ffn1_norm_g=ffn1_norm_g, ffn1_w_gate=ffn1_w_gate, ffn1_w_up=ffn1_w_up, ffn1_w_down=ffn1_w_down, mix_norm_g=mix_norm_g, w_in=w_in, conv_dw_w=conv_dw_w, conv_dw_b=conv_dw_b, conv_ln_g=conv_ln_g, conv_ln_b=conv_ln_b, attn_out_g=attn_out_g, conv_out_g=conv_out_g, w_out=w_out, ffn2_norm_g=ffn2_norm_g, ffn2_w_gate=ffn2_w_gate, ffn2_w_up=ffn2_w_up, ffn2_w_down=ffn2_w_down, final_norm_g=final_norm_g)
    moms = dict(w_ada=m_w_ada, b_ada=m_b_ada, ffn1_norm_g=m_ffn1_norm_g, ffn1_w_gate=m_ffn1_w_gate, ffn1_w_up=m_ffn1_w_up, ffn1_w_down=m_ffn1_w_down, mix_norm_g=m_mix_norm_g, w_in=m_w_in, conv_dw_w=m_conv_dw_w, conv_dw_b=m_conv_dw_b, conv_ln_g=m_conv_ln_g, conv_ln_b=m_conv_ln_b, attn_out_g=m_attn_out_g, conv_out_g=m_conv_out_g, w_out=m_w_out, ffn2_norm_g=m_ffn2_norm_g, ffn2_w_gate=m_ffn2_w_gate, ffn2_w_up=m_ffn2_w_up, ffn2_w_down=m_ffn2_w_down, final_norm_g=m_final_norm_g)
    vars_ = dict(w_ada=v_w_ada, b_ada=v_b_ada, ffn1_norm_g=v_ffn1_norm_g, ffn1_w_gate=v_ffn1_w_gate, ffn1_w_up=v_ffn1_w_up, ffn1_w_down=v_ffn1_w_down, mix_norm_g=v_mix_norm_g, w_in=v_w_in, conv_dw_w=v_conv_dw_w, conv_dw_b=v_conv_dw_b, conv_ln_g=v_conv_ln_g, conv_ln_b=v_conv_ln_b, attn_out_g=v_attn_out_g, conv_out_g=v_conv_out_g, w_out=v_w_out, ffn2_norm_g=v_ffn2_norm_g, ffn2_w_gate=v_ffn2_w_gate, ffn2_w_up=v_ffn2_w_up, ffn2_w_down=v_ffn2_w_down, final_norm_g=v_final_norm_g)
    names = list(weights)
    big = ["w_ada", "ffn1_w_gate", "ffn1_w_up", "ffn1_w_down", "w_in", "w_out", "ffn2_w_gate", "ffn2_w_up",
           "ffn2_w_down"]
    shape2 = {n: (weights[n].shape[-2] if weights[n].ndim > 1 else 1, weights[n].shape[-1]) for n in names}
    shape2["conv_dw_w"] = (CONV_KERNEL, cw_shard)
    g_out, d_out, m_out, v_out = {}, {}, {}, {}
    for n in big:
        if n in arrived:
            def view(t, n=n):
                return t[0].T if n in transposed else t[0]
            res = _adamw_reduced(view(weights[n]), arrived[n], view(moms[n]), view(vars_[n]), "adamw_" + n)
            g_out[n], d_out[n], m_out[n], v_out[n] = [r.T if n in transposed else r for r in res]
        else:
            g2d = grads[n].reshape(shape2[n])
            res = _adamw_big(weights[n].reshape(shape2[n]), g2d, moms[n].reshape(shape2[n]),
                             vars_[n].reshape(shape2[n]), "adamw_" + n)
            g_out[n], (d_out[n], m_out[n], v_out[n]) = g2d, res
    rest = [n for n in names if n not in big]
    res = _adamw_small([weights[n].reshape(shape2[n]) for n in rest], [grads[n].reshape(shape2[n]) for n in rest],
                       [moms[n].reshape(shape2[n]) for n in rest], [vars_[n].reshape(shape2[n]) for n in rest],
                       "adamw_small")
    for i, n in enumerate(rest):
        g_out[n], d_out[n], m_out[n], v_out[n] = grads[n], res[0][i], res[1][i], res[2][i]

    def shaped(table):
        return [table[n].reshape(weights[n].shape) for n in names]

    return (loss, dx.reshape(x.shape), *shaped(g_out), *shaped(d_out), *shaped(m_out), *shaped(v_out))
```

```python
import functools

import jax
import jax.numpy as jnp
from jax import lax
from jax.experimental import pallas as pl
from jax.experimental.pallas import tpu as pltpu

F32 = jnp.float32
BF16 = jnp.bfloat16
MESH = pl.DeviceIdType.MESH
ANY = pl.BlockSpec(memory_space=pl.ANY)

N_DEV = 8
N_CHIP = 4
HEAD_DIM = 64
HALF_HEAD = HEAD_DIM // 2
LANES = 128
BLOCK = 128
DILATIONS = (1, 4, 16)
MERGE_CHUNK = 512
ROPE_THETA = 10000.0
CONV_KERNEL = 31
CONV_HALO = 32
CONV_CHUNK = 512
CONV_SUB = 128
RMS_EPS = 1e-6
LN_EPS = 1e-5
ADAM_LR = 0.001
ADAM_B1 = 0.9
ADAM_B2 = 0.999
ADAM_EPS = 1e-08
ADAM_WD = 0.01
ADAM_STEP = 10
VMEM_LIMIT = 56 * 1024 * 1024
NEG = -1e30


def _params(n_axes):
    return pltpu.CompilerParams(dimension_semantics=("arbitrary",) * n_axes, vmem_limit_bytes=VMEM_LIMIT)


def _tile(n, target, unit):
    best = None
    for t in range(unit, min(n, target) + 1, unit):
        if n % t == 0:
            best = t
    return best if best is not None else n


def _sigmoid(x):
    return 0.5 * (jnp.tanh(0.5 * x) + 1.0)


def _call(body, *, grid, in_specs, out_specs, out_shape, args, name, scratch_shapes=(), comm=None):
    params = _params(len(grid))
    if comm is None:
        return pl.pallas_call(body, grid=grid, in_specs=list(in_specs), out_specs=list(out_specs),
                              out_shape=list(out_shape), scratch_shapes=list(scratch_shapes),
                              compiler_params=params, name=name)(*args)
    n_in, n_out, n_scr = len(args), len(out_shape), len(scratch_shapes)
    c_in, c_out = len(comm.inputs), len(comm.out_shapes)
    steps = 1
    for g in grid:
        steps *= g

    def hosted(*refs):
        pos = 0
        parts = []
        for size in (n_in, c_in, n_out, c_out, n_scr, len(comm.scratch)):
            parts.append(refs[pos:pos + size])
            pos += size
        ins, cin, outs, cout, scr, cscr = parts
        step = 0
        for axis, g in enumerate(grid):
            step = step * g + pl.program_id(axis)

        @pl.when(step == 0)
        def _():
            comm.start(cin, cout, cscr)

        body(*ins, *outs, *scr)
        if comm.mid is not None and steps >= 4:
            @pl.when(step == steps // 2)
            def _():
                comm.mid(cin, cout, cscr)

        @pl.when(step == steps - 1)
        def _():
            if comm.mid is not None and steps < 4:
                comm.mid(cin, cout, cscr)
            comm.finish(cin, cout, cscr)

    res = pl.pallas_call(
        hosted, grid=grid, in_specs=list(in_specs) + [ANY] * c_in, out_specs=list(out_specs) + [ANY] * c_out,
        out_shape=list(out_shape) + list(comm.out_shapes), scratch_shapes=list(scratch_shapes) + list(comm.scratch),
        compiler_params=params, name=name)(*args, *comm.inputs)
    return res[:n_out], res[n_out:]


def _rows(fn, rows_in, vecs_in, rows_out, vecs_out, *, tile, name, comm=None):
    norm = [r if isinstance(r, tuple) else (r, r.shape[1], 0) for r in rows_in]
    n_rows = norm[0][0].shape[0]
    n_tiles = n_rows // tile
    in_specs, args = [], []
    for arr, width, cb in norm:
        in_specs.append(pl.BlockSpec((tile, width), functools.partial(lambda i, cb: (i, cb), cb=cb)))
        args.append(arr)
    for v in vecs_in:
        in_specs.append(pl.BlockSpec((1, v.shape[1]), lambda i: (0, 0)))
        args.append(v)
    out_shape = [jax.ShapeDtypeStruct((n_rows, w), dt) for w, dt in rows_out]
    out_shape += [jax.ShapeDtypeStruct((1, w), F32) for w in vecs_out]
    out_specs = [pl.BlockSpec((tile, w), lambda i: (i, 0)) for w, _ in rows_out]
    out_specs += [pl.BlockSpec((1, w), lambda i: (0, 0)) for w in vecs_out]
    n_in, n_ro = len(args), len(rows_out)

    def body(*refs):
        vals = [r[...] for r in refs[:n_in]]
        outs = refs[n_in:]
        row_vals, vec_vals = fn(*vals)
        for ref, val in zip(outs[:n_ro], row_vals):
            if isinstance(val, tuple):
                w = val[0].shape[1]
                for j, piece in enumerate(val):
                    ref[:, j * w:(j + 1) * w] = piece.astype(ref.dtype)
            else:
                ref[...] = val.astype(ref.dtype)
        if vecs_out:
            @pl.when(pl.program_id(0) == 0)
            def _():
                for ref in outs[n_ro:]:
                    ref[...] = jnp.zeros_like(ref)
            for ref, val in zip(outs[n_ro:], vec_vals):
                ref[...] += val

    return _call(body, grid=(n_tiles,), in_specs=in_specs, out_specs=out_specs, out_shape=out_shape, args=args,
                 name=name, comm=comm)


def _colsum(x):
    return jnp.sum(x, axis=0, keepdims=True)


def _rms_stats(h):
    r = lax.rsqrt(jnp.mean(h * h, axis=-1, keepdims=True) + RMS_EPS)
    return r, h * r


def _rms_back(r, xn, dxn):
    return r * (dxn - xn * jnp.mean(dxn * xn, axis=-1, keepdims=True))


def _branch_back(dh, f, gate, coef):
    return (coef * gate) * dh, coef * _colsum(f.astype(F32) * dh)


def _norm_mod_back(dn, h, dh_in, gain, scale):
    dn = dn.astype(F32)
    r, xn = _rms_stats(h)
    y = xn * gain
    dy = dn * (1.0 + scale)
    dh = dh_in + _rms_back(r, xn, dy * gain)
    return dh, [_colsum(dn), _colsum(dn * y), _colsum(dy * xn)]


def _norm_mod_bwd(dn, h, dh_in, gain, scale, name, comm=None):
    d = h.shape[1]

    def fn(dn, h, dh_in, gain, scale):
        dh, vecs = _norm_mod_back(dn, h, dh_in, gain, scale)
        return [dh], vecs
    return _rows(fn, [dn, h, dh_in], [gain, scale], [(d, F32)], [d, d, d], tile=256, name=name, comm=comm)


def _mm_norm_mod_bwd(pairs, h, dh_in, gain, scale, branch, name, tm, comm=None):
    f, gate, coef = branch

    def epi(accs, ex, vc):
        dh, vecs = _norm_mod_back(accs[0], ex[0], ex[1], vc[0], vc[1])
        df, dgate = _branch_back(dh, ex[2], vc[2], coef)
        return [dh, df] + vecs + [dgate]
    return _mm([pairs], epi, [h, dh_in, f], [gain, scale, gate], [F32, BF16], trans_rhs=False, tm=tm,
               tn=h.shape[1], name=name, n_sums=4, comm=comm)


def _last_mm_loss(lhs, w, res, gate, coef, target, gain, name):
    d = w.shape[1]

    def epi(accs, ex, vc):
        f = accs[0]
        h = ex[0] + (coef * vc[0]) * f
        r, xn = _rms_stats(h)
        err = xn * vc[1] - ex[1]
        dout = err * (1.0 / d)
        dh = _rms_back(r, xn, dout * vc[1])
        df, dgate = _branch_back(dh, f, vc[0], coef)
        return [dh, df, _colsum(err * err), _colsum(dout * xn), dgate]
    return _mm([[(lhs, w)]], epi, [res, target], [gate, gain], [F32, BF16], trans_rhs=False, tm=256, tn=d,
               name=name, n_sums=3)


def _partner(x):
    if x.shape[1] > LANES:
        return jnp.concatenate([_partner(x[:, c:c + LANES]) for c in range(0, x.shape[1], LANES)], axis=1)
    lane = lax.broadcasted_iota(jnp.int32, x.shape, 1) % HEAD_DIM
    return jnp.where(lane < HALF_HEAD, pltpu.roll(x, LANES - HALF_HEAD, 1), pltpu.roll(x, HALF_HEAD, 1))


def _proj_rope(n, w_t, cos, sin_signed, width, name, comm=None):
    s, kdim = n.shape
    n_cols = w_t.shape[0]
    tm = _tile(s, 1024, 8)
    qscale = HEAD_DIM ** -0.5

    chunk = _tile(tm, 256, 8)

    def body(n_ref, w_ref, cos_ref, sin_ref, o_ref):
        j = pl.program_id(0)

        def products(rows):
            return lax.dot_general(n_ref[rows, :].astype(BF16), w_ref[...].astype(BF16), (((1,), (1,)), ((), ())),
                                   preferred_element_type=F32)

        @pl.when(j >= 2)
        def _():
            for c in range(tm // chunk):
                rows = slice(c * chunk, (c + 1) * chunk)
                o_ref[rows, :] = products(rows)

        @pl.when(j < 2)
        def _():
            scale = jnp.where(j == 0, qscale, 1.0)
            for c in range(tm // chunk):
                rows = slice(c * chunk, (c + 1) * chunk)
                acc = products(rows)
                cos = jnp.tile(cos_ref[rows, :], (1, width // LANES))
                sin = jnp.tile(sin_ref[rows, :], (1, width // LANES))
                o_ref[rows, :] = scale * (acc * cos + _partner(acc) * sin)

    table = pl.BlockSpec((tm, LANES), lambda j, i: (jnp.where(j < 2, i, 0), 0))
    return _call(
        body, grid=(n_cols // width, s // tm),
        in_specs=[pl.BlockSpec((tm, kdim), lambda j, i: (i, 0)), pl.BlockSpec((width, kdim), lambda j, i: (j, 0)),
                  table, table],
        out_specs=[pl.BlockSpec((tm, width), lambda j, i: (i, j))],
        out_shape=[jax.ShapeDtypeStruct((s, n_cols), F32)], args=(n, w_t, cos, sin_signed), name=name, comm=comm)


def _mix_post(attn, u1, attn_g, ln_g, ln_b, conv_g):
    _, xa = _rms_stats(attn)
    mu = jnp.mean(u1, axis=-1, keepdims=True)
    xc = u1 - mu
    rstd = lax.rsqrt(jnp.mean(xc * xc, axis=-1, keepdims=True) + LN_EPS)
    u2 = (xc * rstd) * ln_g + ln_b
    u3 = u2 * _sigmoid(u2)
    _, x3 = _rms_stats(u3)
    return jnp.concatenate([xa * attn_g, x3 * conv_g], axis=1)


def _mix_post_back(dy, attn, u1, attn_g, ln_g, ln_b, conv_g):
    w = attn.shape[1]
    dya, dyc = dy[:, :w], dy[:, w:]
    ra, xa = _rms_stats(attn)
    dattn = _rms_back(ra, xa, dya * attn_g)
    mu = jnp.mean(u1, axis=-1, keepdims=True)
    xc = u1 - mu
    rstd = lax.rsqrt(jnp.mean(xc * xc, axis=-1, keepdims=True) + LN_EPS)
    xh = xc * rstd
    u2 = xh * ln_g + ln_b
    sig = _sigmoid(u2)
    u3 = u2 * sig
    r3, x3 = _rms_stats(u3)
    du3 = _rms_back(r3, x3, dyc * conv_g)
    du2 = du3 * (sig + u3 * (1.0 - sig))
    dxh = du2 * ln_g
    du1 = rstd * (dxh - jnp.mean(dxh, axis=-1, keepdims=True) - xh * jnp.mean(dxh * xh, axis=-1, keepdims=True))
    return dattn, du1, [_colsum(dya * xa), _colsum(dyc * x3), _colsum(du2 * xh), _colsum(du2)]


def _mm(groups, epi, extras, vecs, outs, *, trans_rhs, tm, tn, name, n_sums=0, pre=None, pre_inputs=(),
        comm=None):
    m = (pre_inputs[0] if pre is not None else groups[0][0][0]).shape[0]
    n = groups[0][0][1].shape[0] if trans_rhs else groups[0][0][1].shape[1]
    tm, tn = min(tm, m), min(tn, n)
    in_specs, args, uses_pre = [], [], []
    for grp in groups:
        for lhs, rhs in grp:
            k = rhs.shape[1] if trans_rhs else rhs.shape[0]
            uses_pre.append(lhs is None)
            if lhs is not None:
                in_specs.append(pl.BlockSpec((tm, k), lambda j, i: (i, 0)))
                args.append(lhs)
            in_specs.append(pl.BlockSpec((tn, k), lambda j, i: (j, 0)) if trans_rhs
                            else pl.BlockSpec((k, tn), lambda j, i: (0, j)))
            args.append(rhs)
    n_mm = len(args)
    for p in pre_inputs:
        in_specs.append(pl.BlockSpec((tm, p.shape[1]), lambda j, i: (i, 0)))
        args.append(p)
    for e in extras:
        in_specs.append(pl.BlockSpec((tm, tn), lambda j, i: (i, j)) if e.shape[1] == n
                        else pl.BlockSpec((tm, e.shape[1]), lambda j, i: (i, 0)))
        args.append(e)
    for v in vecs:
        in_specs.append(pl.BlockSpec((1, tn), lambda j, i: (0, j)) if v.shape[1] == n
                        else pl.BlockSpec((1, v.shape[1]), lambda j, i: (0, 0)))
        args.append(v)
    sizes = [len(g) for g in groups]
    n_pre, n_ex, n_vec = len(pre_inputs), len(extras), len(vecs)
    dims = (((1,), (1,)), ((), ())) if trans_rhs else (((1,), (0,)), ((), ()))
    out_specs, out_shape = [], []
    if pre is not None:
        k_pre = args[n_mm - 1].shape[1] if trans_rhs else args[n_mm - 1].shape[0]
        out_specs.append(pl.BlockSpec((tm, k_pre), lambda j, i: (i, 0)))
        out_shape.append(jax.ShapeDtypeStruct((m, k_pre), BF16))
    for o in outs:
        dt, width = o if isinstance(o, tuple) else (o, n)
        out_specs.append(pl.BlockSpec((tm, tn), lambda j, i: (i, j)) if width == n
                         else pl.BlockSpec((tm, width), lambda j, i: (i, 0)))
        out_shape.append(jax.ShapeDtypeStruct((m, width), dt))
    n_tiles_out = len(out_specs)
    out_specs += [pl.BlockSpec((1, tn), lambda j, i: (0, j))] * n_sums
    out_shape += [jax.ShapeDtypeStruct((1, n), F32)] * n_sums

    def body(*refs):
        ins = refs[:n_mm + n_pre + n_ex + n_vec]
        out_refs = refs[n_mm + n_pre + n_ex + n_vec:]
        vc = [r[...] for r in ins[n_mm + n_pre + n_ex:]]
        vals = []
        made = None
        if pre is not None:
            made = pre([r[...] for r in ins[n_mm:n_mm + n_pre]], vc).astype(BF16)
            vals.append(made)
        accs, pos, pair = [], 0, 0
        for size in sizes:
            acc = None
            for _ in range(size):
                if uses_pre[pair]:
                    lhs_tile = made
                else:
                    lhs_tile = ins[pos][...].astype(BF16)
                    pos += 1
                part = lax.dot_general(lhs_tile, ins[pos][...].astype(BF16), dims, preferred_element_type=F32)
                acc = part if acc is None else acc + part
                pos += 1
                pair += 1
            accs.append(acc)
        ex = [r[...] for r in ins[n_mm + n_pre:n_mm + n_pre + n_ex]]
        vals += epi(accs, ex, vc)
        for ref, val in zip(out_refs[:n_tiles_out], vals):
            ref[...] = val.astype(ref.dtype)
        if n_sums:
            @pl.when(pl.program_id(1) == 0)
            def _():
                for ref in out_refs[n_tiles_out:]:
                    ref[...] = jnp.zeros_like(ref)
            for ref, val in zip(out_refs[n_tiles_out:], vals[n_tiles_out:]):
                ref[...] += val

    return _call(body, grid=(n // tn, m // tm), in_specs=in_specs, out_specs=out_specs, out_shape=out_shape,
                 args=args, name=name, comm=comm)


def _mm_tn(lhs, rhs, name, comm=None):
    t, a = lhs.shape
    b = rhs.shape[1]
    ta = a if a <= 1536 else _tile(a, 1536, LANES)
    tk = _tile(t, 2048, 8)

    def body(l_ref, r_ref, o_ref):
        @pl.when(pl.program_id(1) == 0)
        def _():
            o_ref[...] = jnp.zeros_like(o_ref)
        o_ref[...] += lax.dot_general(l_ref[...].astype(BF16), r_ref[...].astype(BF16), (((0,), (0,)), ((), ())),
                                      preferred_element_type=F32)

    res = _call(body, grid=(a // ta, t // tk),
                in_specs=[pl.BlockSpec((tk, ta), lambda i, k: (k, i)), pl.BlockSpec((tk, b), lambda i, k: (k, 0))],
                out_specs=[pl.BlockSpec((ta, b), lambda i, k: (i, 0))], out_shape=[jax.ShapeDtypeStruct((a, b), F32)],
                args=(lhs, rhs), name=name, comm=comm)
    return res[0] if comm is None else (res[0][0], res[1])


def _ffn_tn(f):
    return _tile(f, 1536, LANES)


def _swiglu_parts(a, b):
    sig = _sigmoid(a)
    silu = a * sig
    return [silu, b * (sig + silu * (1.0 - sig)), silu * b]


def _ffn_up(n, wg_t, wu_t, name, comm=None):
    def epi(accs, ex, vc):
        return _swiglu_parts(accs[0], accs[1])
    return _mm([[(n, wg_t)], [(n, wu_t)]], epi, [], [], [BF16, BF16, BF16], trans_rhs=True, tm=512,
               tn=_ffn_tn(wg_t.shape[0]), name=name, comm=comm)


def _norm_ffn_up(h, gain, scale, shift, wg_t, wu_t, name, comm=None):
    def pre(tiles, vc):
        _, xn = _rms_stats(tiles[0])
        return (xn * vc[0]) * (1.0 + vc[1]) + vc[2]

    def epi(accs, ex, vc):
        return _swiglu_parts(accs[0], accs[1])
    return _mm([[(None, wg_t)], [(None, wu_t)]], epi, [], [gain, scale, shift], [BF16, BF16, BF16], trans_rhs=True,
               tm=256, tn=wg_t.shape[0], name=name, pre=pre, pre_inputs=[h], comm=comm)


def _mix_out(attn, u1, post, w, res, gate, norm, name):
    def pre(tiles, vc):
        return _mix_post(tiles[0], tiles[1], *vc[4:8])

    def epi(accs, ex, vc):
        h = ex[0] + vc[0] * accs[0]
        _, xn = _rms_stats(h)
        return [h, accs[0], (xn * vc[1]) * (1.0 + vc[2]) + vc[3]]
    return _mm([[(None, w)]], epi, [res], [gate] + list(norm) + list(post), [F32, BF16, BF16], trans_rhs=False,
               tm=512, tn=w.shape[1], name=name, pre=pre, pre_inputs=[attn, u1])


def _mix_dy_post_bwd(dmix, w, attn, u1, post, name):
    width = attn.shape[1]

    def epi(accs, ex, vc):
        dattn, du1, sums = _mix_post_back(accs[0], ex[0], ex[1], *vc)
        return [dattn, du1, jnp.concatenate(sums[0:2], axis=1), jnp.concatenate(sums[2:4], axis=1)]
    return _mm([[(dmix, w)]], epi, [attn, u1], list(post), [(F32, width), (F32, width)], trans_rhs=True, tm=256,
               tn=w.shape[0], name=name, n_sums=2)


def _residual_mm(lhs, w, res, gate, coef, name, norm=None, comm=None):
    def epi(accs, ex, vc):
        h = ex[0] + (coef * vc[0]) * accs[0]
        if norm is None:
            return [h, accs[0]]
        _, xn = _rms_stats(h)
        return [h, accs[0], (xn * vc[1]) * (1.0 + vc[2]) + vc[3]]
    vecs = [gate] + (list(norm) if norm is not None else [])
    outs = [F32, BF16] + ([BF16] if norm is not None else [])
    return _mm([[(lhs, w)]], epi, [res], vecs, outs, trans_rhs=False, tm=512, tn=w.shape[1], name=name, comm=comm)


def _ffn_bwd_hidden(df, wd, dhid_db, dhid_da, name, comm=None):
    def epi(accs, ex, vc):
        return [accs[0] * ex[1].astype(F32), accs[0] * ex[0].astype(F32)]
    return _mm([[(df, wd)]], epi, [dhid_db, dhid_da], [], [BF16, BF16], trans_rhs=True, tm=512,
               tn=_ffn_tn(wd.shape[0]), name=name, comm=comm)


def _plain_mm(pairs, out_dtype, trans_rhs, tn, name, tm=512, comm=None):
    def epi(accs, ex, vc):
        return [accs[0]]
    res = _mm([pairs], epi, [], [], [out_dtype], trans_rhs=trans_rhs, tm=tm, tn=tn, name=name, comm=comm)
    return res[0] if comm is None else (res[0][0], res[1])


HEADS_PER_TILE = LANES // HEAD_DIM


def _stack_heads(x):
    lane = lax.broadcasted_iota(jnp.int32, (1, LANES), 1)
    return jnp.concatenate([x * (lane // HEAD_DIM == h).astype(F32) for h in range(HEADS_PER_TILE)], axis=0)


def _unstack_heads(y):
    r = y.shape[0] // HEADS_PER_TILE
    lane = lax.broadcasted_iota(jnp.int32, (r, y.shape[1]), 1)
    out = y[0:r]
    for h in range(1, HEADS_PER_TILE):
        out = jnp.where(lane // HEAD_DIM == h, y[h * r:(h + 1) * r], out)
    return out


def _stacked_lse(lb):
    return jnp.concatenate([_lane_pick(lb, h) for h in range(HEADS_PER_TILE)], axis=0)


def _band_masks(n_row_blocks, n_col_blocks):
    shape = (n_row_blocks * BLOCK, n_col_blocks * BLOCK)
    qi = lax.broadcasted_iota(jnp.int32, shape, 0) % BLOCK
    kj = lax.broadcasted_iota(jnp.int32, shape, 1) % BLOCK
    return kj <= qi, kj >= qi


def _query_masks():
    first_valid, _ = _band_masks(HEADS_PER_TILE, 1)
    same_ok, before_ok = _band_masks(HEADS_PER_TILE, 2)
    is_cur = lax.broadcasted_iota(jnp.int32, same_ok.shape, 1) >= BLOCK
    return first_valid, jnp.logical_and(is_cur, same_ok), jnp.logical_and(jnp.logical_not(is_cur), before_ok)


def _dot_nt(a, b):
    return lax.dot_general(a.astype(BF16), b.astype(BF16), (((1,), (1,)), ((), ())), preferred_element_type=F32)


def _dot_nn(a, b):
    return lax.dot_general(a.astype(BF16), b.astype(BF16), (((1,), (0,)), ((), ())), preferred_element_type=F32)


def _dot_tn(a, b):
    return lax.dot_general(a.astype(BF16), b.astype(BF16), (((0,), (0,)), ((), ())), preferred_element_type=F32)


def _lane_pick(x, h):
    lane = lax.broadcasted_iota(jnp.int32, x.shape, 1)
    return jnp.sum(jnp.where(lane == h * HEAD_DIM, x, 0.0), axis=1, keepdims=True)


def _block_rows(idx, d):
    span = BLOCK * d
    q0 = (idx // d) * span + idx % d
    return pl.ds(q0, BLOCK, stride=d), pl.ds(q0 - span, BLOCK, stride=d)


def _branch_loops(n_blocks, d, visit, unroll, masks):
    first_valid, cur_part, prev_part = masks
    if d % unroll == 0 and (n_blocks - d) % unroll == 0:
        full_valid = jnp.logical_or(cur_part, prev_part)

        def first(idx, carry):
            rows = pl.ds(idx, BLOCK, stride=d)
            visit(rows, [rows], first_valid)
            return carry

        def rest(idx, carry):
            rows, prev = _block_rows(idx, d)
            visit(rows, [prev, rows], full_valid)
            return carry

        lax.fori_loop(0, d, first, 0, unroll=unroll)
        lax.fori_loop(d, n_blocks, rest, 0, unroll=unroll)
        return

    def every(idx, carry):
        span = BLOCK * d
        q0 = (idx // d) * span + idx % d
        has_prev = idx >= d
        rows = pl.ds(q0, BLOCK, stride=d)
        prev = pl.ds(jnp.where(has_prev, q0 - span, q0), BLOCK, stride=d)
        visit(rows, [prev, rows], jnp.logical_or(cur_part, jnp.logical_and(prev_part, has_prev)))
        return carry

    lax.fori_loop(0, n_blocks, every, 0, unroll=unroll)


def _qkv_specs(s, tiles):
    q, k, v = [pl.BlockSpec((s, LANES), functools.partial(lambda hb, off: (0, off + hb), off=i * tiles))
               for i in range(3)]
    return q, k, v, pl.BlockSpec((s, LANES), lambda hb: (0, hb))


def _attn_seq_fwd(proj, width, name, comm=None):
    s = proj.shape[0]
    q_spec, k_spec, v_spec, cur = _qkv_specs(s, width // LANES)

    def body(q_ref, k_ref, v_ref, o_ref, l_ref, o_s, l_s):
        masks = _query_masks()
        for bi, d in enumerate(DILATIONS):
            def visit(rows, key_rows, valid, bi=bi):
                q2 = _stack_heads(q_ref[rows, :])
                keys = jnp.concatenate([k_ref[r, :] for r in key_rows], axis=0)
                vals = jnp.concatenate([v_ref[r, :] for r in key_rows], axis=0)
                sc = jnp.where(valid, _dot_nt(q2, keys), NEG)
                mx = jnp.max(sc, axis=1, keepdims=True)
                p = jnp.exp(sc - mx)
                den = jnp.sum(p, axis=1, keepdims=True)
                o_s[bi, rows, :] = _unstack_heads(_dot_nn(p, vals) / den)
                l_s[bi, rows, :] = _unstack_heads(jnp.broadcast_to(mx + jnp.log(den), (q2.shape[0], LANES)))

            _branch_loops(s // BLOCK, d, visit, 8, masks)
        for c in range(s // MERGE_CHUNK):
            rows = slice(c * MERGE_CHUNK, (c + 1) * MERGE_CHUNK)
            ls = [l_s[bi, rows, :] for bi in range(len(DILATIONS))]
            top = functools.reduce(jnp.maximum, ls)
            ws = [jnp.exp(l - top) for l in ls]
            den = functools.reduce(lambda a, b: a + b, ws)
            num = functools.reduce(lambda a, b: a + b, [w * o_s[bi, rows, :] for bi, w in enumerate(ws)])
            o_ref[rows, :] = num / den
            l_ref[rows, :] = top + jnp.log(den)

    return _call(
        body, grid=(width // LANES,), in_specs=[q_spec, k_spec, v_spec], out_specs=[cur, cur],
        out_shape=[jax.ShapeDtypeStruct((s, width), F32)] * 2,
        scratch_shapes=[pltpu.VMEM((len(DILATIONS), s, LANES), F32)] * 2,
        args=(proj, proj, proj), name=name, comm=comm)


def _attn_seq_bwd(proj, do, o, lse, cos, sin_signed, name, comm=None):
    s, width = do.shape
    q_spec, k_spec, v_spec, cur = _qkv_specs(s, width // LANES)
    table = pl.BlockSpec((s, LANES), lambda hb: (0, 0))
    qscale = HEAD_DIM ** -0.5

    def body(q_ref, k_ref, v_ref, do_ref, o_ref, l_ref, cos_ref, sin_ref, dq_out, dk_out, dv_out,
             dq_ref, dk_ref, dv_ref):
        dq_ref[...] = jnp.zeros_like(dq_ref)
        dk_ref[...] = jnp.zeros_like(dk_ref)
        dv_ref[...] = jnp.zeros_like(dv_ref)
        masks = _query_masks()
        for d in DILATIONS:
            def visit(rows, key_rows, valid):
                dob = do_ref[rows, :]
                q2 = _stack_heads(q_ref[rows, :])
                do2 = _stack_heads(dob)
                delta = jnp.sum(_stack_heads(dob * o_ref[rows, :]), axis=1, keepdims=True)
                lse2 = _stacked_lse(l_ref[rows, :])
                keys = jnp.concatenate([k_ref[r, :] for r in key_rows], axis=0)
                vals = jnp.concatenate([v_ref[r, :] for r in key_rows], axis=0)
                p = jnp.where(valid, jnp.exp(_dot_nt(q2, keys) - lse2), 0.0)
                ds = p * (_dot_nt(do2, vals) - delta)
                dq_ref[rows, :] += _unstack_heads(_dot_nn(ds, keys))
                dkk = _dot_tn(ds, q2)
                dvv = _dot_tn(p, do2)
                for i, r in enumerate(key_rows):
                    dk_ref[r, :] += dkk[i * BLOCK:(i + 1) * BLOCK]
                    dv_ref[r, :] += dvv[i * BLOCK:(i + 1) * BLOCK]

            _branch_loops(s // BLOCK, d, visit, 8, masks)
        for c in range(s // MERGE_CHUNK):
            rows = slice(c * MERGE_CHUNK, (c + 1) * MERGE_CHUNK)
            cos, sin = cos_ref[rows, :], sin_ref[rows, :]
            dq, dk = dq_ref[rows, :], dk_ref[rows, :]
            dq_out[rows, :] = ((dq * cos - _partner(dq) * sin) * qscale).astype(BF16)
            dk_out[rows, :] = (dk * cos - _partner(dk) * sin).astype(BF16)
            dv_out[rows, :] = dv_ref[rows, :].astype(BF16)

    return _call(
        body, grid=(width // LANES,), in_specs=[q_spec, k_spec, v_spec, cur, cur, cur, table, table],
        out_specs=[cur, cur, cur], out_shape=[jax.ShapeDtypeStruct((s, width), BF16)] * 3,
        scratch_shapes=[pltpu.VMEM((s, LANES), F32)] * 3,
        args=(proj, proj, proj, do, o, lse, cos, sin_signed), name=name, comm=comm)


def _conv_specs(s, a_block, b_block):
    per = CONV_CHUNK // CONV_HALO
    a_cur = pl.BlockSpec((CONV_CHUNK, LANES), lambda cb, i: (i, a_block + cb))
    b_cur = pl.BlockSpec((CONV_CHUNK, LANES), lambda cb, i: (i, b_block + cb))
    a_halo = pl.BlockSpec((CONV_HALO, LANES), lambda cb, i: (jnp.maximum(i * per - 1, 0), a_block + cb))
    b_halo = pl.BlockSpec((CONV_HALO, LANES), lambda cb, i: (jnp.maximum(i * per - 1, 0), b_block + cb))
    w_spec = pl.BlockSpec((CONV_KERNEL, LANES), lambda cb, i: (0, cb))
    vec = pl.BlockSpec((1, LANES), lambda cb, i: (0, cb))
    out = pl.BlockSpec((CONV_CHUNK, LANES), lambda cb, i: (i, cb))
    return a_cur, b_cur, a_halo, b_halo, w_spec, vec, out


def _fill_glu_window(win, a_ref, b_ref, ah_ref, bh_ref, first):
    halo = ah_ref[...] * _sigmoid(bh_ref[...])
    win[0:CONV_HALO, :] = jnp.where(first, 0.0, halo)
    win[CONV_HALO:, :] = a_ref[...] * _sigmoid(b_ref[...])


def _conv_fwd(proj, a_block, b_block, w, bias, name, comm=None):
    s = proj.shape[0]
    cw = w.shape[1]
    a_cur, b_cur, a_halo, b_halo, w_spec, vec, out = _conv_specs(s, a_block, b_block)
    lead = CONV_HALO - (CONV_KERNEL - 1)

    def body(a_ref, b_ref, ah_ref, bh_ref, w_ref, bias_ref, o_ref, win):
        _fill_glu_window(win, a_ref, b_ref, ah_ref, bh_ref, pl.program_id(1) == 0)
        for sub in range(CONV_CHUNK // CONV_SUB):
            base = sub * CONV_SUB
            acc = jnp.zeros((CONV_SUB, LANES), F32) + bias_ref[...]
            for j in range(CONV_KERNEL):
                acc = acc + w_ref[j:j + 1, :] * win[base + lead + j:base + lead + j + CONV_SUB, :]
            o_ref[base:base + CONV_SUB, :] = acc

    return _call(
        body, grid=(cw // LANES, s // CONV_CHUNK), in_specs=[a_cur, b_cur, a_halo, b_halo, w_spec, vec],
        out_specs=[out], out_shape=[jax.ShapeDtypeStruct((s, cw), F32)],
        scratch_shapes=[pltpu.VMEM((CONV_CHUNK + CONV_HALO, LANES), F32)],
        args=(proj, proj, proj, proj, w, bias), name=name, comm=comm)


def _conv_bwd(proj, a_block, b_block, w, du1, name):
    s = proj.shape[0]
    cw = w.shape[1]
    a_cur, b_cur, a_halo, b_halo, w_spec, vec, out = _conv_specs(s, a_block, b_block)
    per = CONV_CHUNK // CONV_HALO
    n_chunks = s // CONV_CHUNK
    d_next = pl.BlockSpec((CONV_HALO, LANES), lambda cb, i: (jnp.minimum((i + 1) * per, s // CONV_HALO - 1), cb))
    lead = CONV_HALO - (CONV_KERNEL - 1)

    def body(a_ref, b_ref, ah_ref, bh_ref, w_ref, d_ref, dn_ref, da_ref, db_ref, dw_ref, dbias_ref, win, dwin):
        i = pl.program_id(1)
        _fill_glu_window(win, a_ref, b_ref, ah_ref, bh_ref, i == 0)
        dwin[0:CONV_CHUNK, :] = d_ref[...]
        dwin[CONV_CHUNK:, :] = jnp.where(i == n_chunks - 1, 0.0, dn_ref[...])

        @pl.when(i == 0)
        def _():
            dw_ref[...] = jnp.zeros_like(dw_ref)
            dbias_ref[...] = jnp.zeros_like(dbias_ref)

        dbias_ref[...] += _colsum(d_ref[...])
        for sub in range(CONV_CHUNK // CONV_SUB):
            base = sub * CONV_SUB
            dcur = dwin[base:base + CONV_SUB, :]
            du0 = jnp.zeros((CONV_SUB, LANES), F32)
            for j in range(CONV_KERNEL):
                back = CONV_KERNEL - 1 - j
                du0 = du0 + w_ref[j:j + 1, :] * dwin[base + back:base + back + CONV_SUB, :]
                dw_ref[j:j + 1, :] += _colsum(dcur * win[base + lead + j:base + lead + j + CONV_SUB, :])
            av = a_ref[base:base + CONV_SUB, :]
            sig = _sigmoid(b_ref[base:base + CONV_SUB, :])
            da_ref[base:base + CONV_SUB, :] = (du0 * sig).astype(BF16)
            db_ref[base:base + CONV_SUB, :] = (du0 * av * sig * (1.0 - sig)).astype(BF16)

    return pl.pallas_call(
        body, grid=(cw // LANES, n_chunks), in_specs=[a_cur, b_cur, a_halo, b_halo, w_spec, out, d_next],
        out_specs=[out, out, w_spec, vec],
        out_shape=[jax.ShapeDtypeStruct((s, cw), BF16), jax.ShapeDtypeStruct((s, cw), BF16),
                   jax.ShapeDtypeStruct((CONV_KERNEL, cw), F32), jax.ShapeDtypeStruct((1, cw), F32)],
        scratch_shapes=[pltpu.VMEM((CONV_CHUNK + CONV_HALO, LANES), F32)] * 2,
        compiler_params=_params(2), name=name)(proj, proj, proj, proj, w, du1, du1)


def _adamw_math(w, g, m, v):
    m = ADAM_B1 * m + (1.0 - ADAM_B1) * g
    v = ADAM_B2 * v + (1.0 - ADAM_B2) * (g * g)
    m_hat = m / (1.0 - ADAM_B1 ** ADAM_STEP)
    v_hat = v / (1.0 - ADAM_B2 ** ADAM_STEP)
    delta = -ADAM_LR * (m_hat / (jnp.sqrt(v_hat) + ADAM_EPS) + ADAM_WD * w)
    return delta, m, v


def _adamw_big(w, g, m, v, name):
    rows, cols = w.shape
    tile = _tile(rows, 256, 8)
    spec = pl.BlockSpec((tile, cols), lambda i: (i, 0))

    def body(w_ref, g_ref, m_ref, v_ref, d_out, m_out, v_out):
        d_out[...], m_out[...], v_out[...] = _adamw_math(w_ref[...], g_ref[...], m_ref[...], v_ref[...])

    return pl.pallas_call(body, grid=(rows // tile,), in_specs=[spec] * 4, out_specs=[spec] * 3,
                          out_shape=[jax.ShapeDtypeStruct(w.shape, F32)] * 3, compiler_params=_params(1),
                          name=name)(w, g, m, v)


def _adamw_reduced(w, land, m, v, name):
    rows, cols = w.shape
    tile = _tile(rows, 256, 16)
    spec = pl.BlockSpec((tile, cols), lambda i: (i, 0))

    def body(w_ref, l_ref, m_ref, v_ref, g_out, d_out, m_out, v_out):
        g = l_ref[0].astype(F32)
        for q in range(1, N_CHIP):
            g = g + l_ref[q].astype(F32)
        g_out[...] = g
        d_out[...], m_out[...], v_out[...] = _adamw_math(w_ref[...], g, m_ref[...], v_ref[...])

    return pl.pallas_call(body, grid=(rows // tile,),
                          in_specs=[spec, pl.BlockSpec((N_CHIP, tile, cols), lambda i: (0, i, 0)), spec, spec],
                          out_specs=[spec] * 4, out_shape=[jax.ShapeDtypeStruct(w.shape, F32)] * 4,
                          compiler_params=_params(1), name=name)(w, land, m, v)


def _adamw_small(ws, gs, ms, vs, name):
    n = len(ws)

    def body(*refs):
        ins, outs = refs[:4 * n], refs[4 * n:]
        for t in range(n):
            res = _adamw_math(ins[t][...], ins[n + t][...], ins[2 * n + t][...], ins[3 * n + t][...])
            for j in range(3):
                outs[j * n + t][...] = res[j]

    shapes = [jax.ShapeDtypeStruct(w.shape, F32) for w in ws]
    res = pl.pallas_call(body, out_shape=shapes * 3, compiler_params=pltpu.CompilerParams(vmem_limit_bytes=VMEM_LIMIT),
                         name=name)(*ws, *gs, *ms, *vs)
    return res[:n], res[n:2 * n], res[2 * n:]


def _sum_blocks(x, n_blocks, name):
    r = x.shape[0] // n_blocks

    def body(x_ref, o_ref):
        acc = x_ref[0:r, :]
        for b in range(1, n_blocks):
            acc = acc + x_ref[b * r:(b + 1) * r, :]
        o_ref[...] = acc

    return pl.pallas_call(body, out_shape=jax.ShapeDtypeStruct((r, x.shape[1]), F32),
                          compiler_params=pltpu.CompilerParams(vmem_limit_bytes=VMEM_LIMIT), name=name)(x)


def _coords():
    return lax.axis_index("x"), lax.axis_index("y"), lax.axis_index("c")


def _flip(v, bit):
    return 1 - v if bit else v


def _ag_small(x, name):
    r, c = x.shape

    def body(x_ref, o_ref, send, recv, local_sem):
        mx, my, mc = _coords()

        def rows(px, py, pc):
            return o_ref.at[pl.ds(pl.multiple_of((4 * px + 2 * py + pc) * r, 8), r), :]

        local = pltpu.make_async_copy(x_ref, rows(mx, my, mc), local_sem)
        local.start()
        peers = [(_flip(mx, k >> 2 & 1), _flip(my, k >> 1 & 1), _flip(mc, k & 1)) for k in range(1, N_DEV)]
        sends = [pltpu.make_async_remote_copy(x_ref, rows(mx, my, mc), send.at[k], recv.at[k], device_id=p,
                                              device_id_type=MESH) for k, p in enumerate(peers)]
        for cp in sends:
            cp.start()
        for k, p in enumerate(peers):
            pltpu.make_async_remote_copy(x_ref, rows(*p), send.at[k], recv.at[k], device_id=p,
                                         device_id_type=MESH).wait_recv()
        for cp in sends:
            cp.wait_send()
        local.wait()

    vm = pl.BlockSpec(memory_space=pltpu.VMEM)
    return pl.pallas_call(
        body, in_specs=[vm], out_specs=vm, out_shape=jax.ShapeDtypeStruct((N_DEV * r, c), x.dtype),
        scratch_shapes=[pltpu.SemaphoreType.DMA((N_DEV - 1,)), pltpu.SemaphoreType.DMA((N_DEV - 1,)),
                        pltpu.SemaphoreType.DMA(())],
        name=name)(x)


class _GatherSmall:
    mid = None

    def __init__(self, x):
        self.inputs = [x]
        self.out_shapes = [jax.ShapeDtypeStruct((N_DEV * x.shape[0], x.shape[1]), x.dtype)]
        self.scratch = [pltpu.SemaphoreType.DMA((N_DEV - 1,)), pltpu.SemaphoreType.DMA((N_DEV - 1,)),
                        pltpu.SemaphoreType.DMA(())]

    def _plan(self, x_refs, o_refs, sems):
        send, recv, local_sem = sems
        x_ref, o_ref = x_refs[0], o_refs[0]
        r = x_ref.shape[0]
        mx, my, mc = _coords()

        def rows(px, py, pc):
            return o_ref.at[pl.ds(pl.multiple_of((4 * px + 2 * py + pc) * r, 8), r), :]

        peers = [(_flip(mx, k >> 2 & 1), _flip(my, k >> 1 & 1), _flip(mc, k & 1)) for k in range(1, N_DEV)]
        out = [pltpu.make_async_remote_copy(x_ref, rows(mx, my, mc), send.at[k], recv.at[k], device_id=p,
                                            device_id_type=MESH) for k, p in enumerate(peers)]
        arrivals = [pltpu.make_async_remote_copy(x_ref, rows(*p), send.at[k], recv.at[k], device_id=p,
                                                 device_id_type=MESH) for k, p in enumerate(peers)]
        return out, arrivals, pltpu.make_async_copy(x_ref, rows(mx, my, mc), local_sem)

    def start(self, x_refs, o_refs, sems):
        out, _, local = self._plan(x_refs, o_refs, sems)
        local.start()
        for cp in out:
            cp.start()

    def finish(self, x_refs, o_refs, sems):
        out, arrivals, local = self._plan(x_refs, o_refs, sems)
        for cp in arrivals:
            cp.wait_recv()
        for cp in out:
            cp.wait_send()
        local.wait()


class _ModExchange:
    def __init__(self, first, w_ada):
        self.d, cols = w_ada.shape
        part = jax.ShapeDtypeStruct((N_DEV, cols), F32)
        self.g1, self.g2 = _GatherSmall(first), _GatherSmall(part)
        self.inputs = [first, w_ada]
        self.out_shapes = [self.g1.out_shapes[0], jax.ShapeDtypeStruct((N_DEV, self.d), F32), part,
                           self.g2.out_shapes[0]]
        self.scratch = self.g1.scratch + self.g2.scratch + [
            pltpu.VMEM(self.g1.out_shapes[0].shape, F32), pltpu.VMEM(w_ada.shape, F32),
            pltpu.VMEM((N_DEV, self.d), F32), pltpu.VMEM((N_DEV, cols), F32), pltpu.SemaphoreType.DMA(())]

    def start(self, cin, cout, scr):
        self.g1.start(cin[0:1], cout[0:1], scr[0:3])
        pltpu.make_async_copy(cin[1], scr[7], scr[10]).start()

    def mid(self, cin, cout, scr):
        gathered, w_v, silu_v, part_v = scr[6:10]
        self.g1.finish(cin[0:1], cout[0:1], scr[0:3])
        pltpu.sync_copy(cout[0], gathered)
        rows_per = cin[0].shape[0]
        for j in range(N_DEV):
            silu_v[j:j + 1, :] = gathered[j * rows_per:j * rows_per + 1, 0:self.d]
        c_all = silu_v[...]
        silu_v[...] = c_all * _sigmoid(c_all)
        pltpu.sync_copy(silu_v, cout[1])
        pltpu.make_async_copy(cin[1], w_v, scr[10]).wait()
        part_v[...] = _dot_nn(silu_v[...], w_v[...])
        pltpu.sync_copy(part_v, cout[2])
        self.g2.start(cout[2:3], cout[3:4], scr[3:6])

    def finish(self, cin, cout, scr):
        self.g2.finish(cout[2:3], cout[3:4], scr[3:6])


class _GatherWeights:
    def __init__(self, shards):
        n_t = len(shards)
        self.inputs = list(shards)
        self.out_shapes = [jax.ShapeDtypeStruct((N_DEV * x.shape[0], x.shape[1]), x.dtype) for x in shards]
        self.scratch = [pltpu.SemaphoreType.DMA((n_t, 8)), pltpu.SemaphoreType.DMA((n_t, 8)),
                        pltpu.SemaphoreType.DMA((n_t,))]

    def _plan(self, x_refs, o_refs, sems):
        send, recv, local_sem = sems
        mx, my, mc = _coords()
        me, sibling = (mx, my, mc), (mx, my, 1 - mc)
        xn, yn, diag = (1 - mx, my), (mx, 1 - my), (1 - mx, 1 - my)

        def rows(t, chip, core, half=None):
            r = x_refs[t].shape[0]
            base = (4 * chip[0] + 2 * chip[1] + core) * r
            if half is None:
                return o_refs[t].at[pl.ds(pl.multiple_of(base, 8), r), :]
            return o_refs[t].at[pl.ds(pl.multiple_of(base + half * (r // 2), 8), r // 2), :]

        def copy(t, k, block, to, src=None):
            return pltpu.make_async_remote_copy(
                src_ref=block if src is None else src, dst_ref=block,
                send_sem=send.at[t, k], recv_sem=recv.at[t, k], device_id=to, device_id_type=MESH)

        def local(t):
            return pltpu.make_async_copy(x_refs[t], rows(t, (mx, my), mc), local_sem.at[t])

        return (mx, my), mc, me, sibling, xn, yn, diag, rows, copy, local

    def start(self, x_refs, o_refs, sems):
        chip, mc, me, sibling, xn, yn, diag, rows, copy, local = self._plan(x_refs, o_refs, sems)
        for t in range(len(x_refs)):
            mine = rows(t, chip, mc)
            local(t).start()
            copy(t, 0, mine, sibling, src=x_refs[t]).start()
            copy(t, 1, mine, (*xn, mc), src=x_refs[t]).start()
            copy(t, 2, mine, (*yn, mc), src=x_refs[t]).start()

    def mid(self, x_refs, o_refs, sems):
        chip, mc, me, sibling, xn, yn, diag, rows, copy, local = self._plan(x_refs, o_refs, sems)
        for t in range(len(x_refs)):
            copy(t, 1, rows(t, xn, mc), me).wait_recv()
            copy(t, 3, rows(t, xn, mc, 0), (*yn, mc)).start()
            copy(t, 5, rows(t, xn, mc), sibling).start()
        for t in range(len(x_refs)):
            copy(t, 2, rows(t, yn, mc), me).wait_recv()
            copy(t, 4, rows(t, yn, mc, 1), (*xn, mc)).start()
            copy(t, 6, rows(t, yn, mc), sibling).start()

    def finish(self, x_refs, o_refs, sems):
        chip, mc, me, sibling, xn, yn, diag, rows, copy, local = self._plan(x_refs, o_refs, sems)
        for t in range(len(x_refs)):
            copy(t, 3, rows(t, diag, mc, 0), me).wait_recv()
            copy(t, 4, rows(t, diag, mc, 1), me).wait_recv()
            copy(t, 7, rows(t, diag, mc), sibling).start()
        for t in range(len(x_refs)):
            copy(t, 0, rows(t, chip, 1 - mc), me).wait_recv()
            copy(t, 5, rows(t, xn, 1 - mc), me).wait_recv()
            copy(t, 6, rows(t, yn, 1 - mc), me).wait_recv()
            copy(t, 7, rows(t, diag, 1 - mc), me).wait_recv()
            mine = rows(t, chip, mc)
            copy(t, 0, mine, sibling, src=x_refs[t]).wait_send()
            copy(t, 1, mine, (*xn, mc), src=x_refs[t]).wait_send()
            copy(t, 2, mine, (*yn, mc), src=x_refs[t]).wait_send()
            copy(t, 3, rows(t, xn, mc, 0), (*yn, mc)).wait_send()
            copy(t, 4, rows(t, yn, mc, 1), (*xn, mc)).wait_send()
            copy(t, 5, rows(t, xn, mc), sibling).wait_send()
            copy(t, 6, rows(t, yn, mc), sibling).wait_send()
            copy(t, 7, rows(t, diag, mc), sibling).wait_send()
            local(t).wait()


class _SiblingExchange:
    mid = None

    def __init__(self, grads):
        n_t = len(grads)
        self.inputs = list(grads)
        self.out_shapes = [jax.ShapeDtypeStruct((N_CHIP,) + g.shape[2:], F32) for g in grads]
        self.scratch = [pltpu.SemaphoreType.DMA((n_t,)), pltpu.SemaphoreType.DMA((n_t,))]

    def _copies(self, g_refs, land, sems):
        send, recv = sems
        mx, my, mc = _coords()
        return [pltpu.make_async_remote_copy(g_refs[t].at[:, 1 - mc], land[t], send.at[t], recv.at[t],
                                             device_id=(mx, my, 1 - mc), device_id_type=MESH)
                for t in range(len(g_refs))]

    def start(self, g_refs, land, sems):
        for cp in self._copies(g_refs, land, sems):
            cp.start()

    def finish(self, g_refs, land, sems):
        for cp in self._copies(g_refs, land, sems):
            cp.wait()


class _Together:
    def __init__(self, *comms):
        self.comms = comms
        self.inputs = [x for c in comms for x in c.inputs]
        self.out_shapes = [x for c in comms for x in c.out_shapes]
        self.scratch = [x for c in comms for x in c.scratch]
        self.mid = self._mid if any(c.mid is not None for c in comms) else None

    def _each(self, phase, cin, cout, sems):
        i = o = s = 0
        for c in self.comms:
            fn = getattr(c, phase)
            ni, no, ns = len(c.inputs), len(c.out_shapes), len(c.scratch)
            if fn is not None:
                fn(cin[i:i + ni], cout[o:o + no], sems[s:s + ns])
            i, o, s = i + ni, o + no, s + ns

    def start(self, cin, cout, sems):
        self._each("start", cin, cout, sems)

    def _mid(self, cin, cout, sems):
        self._each("mid", cin, cout, sems)

    def finish(self, cin, cout, sems):
        self._each("finish", cin, cout, sems)


def _standalone(comm, name):
    def body():
        pass
    return _call(body, grid=(1,), in_specs=[], out_specs=[], out_shape=[], args=(), name=name, comm=comm)[1]


def _chip_partials(g4s, lands, name):
    n_t = len(g4s)
    in_specs, out_specs, out_shape = [], [], []
    for g4 in g4s:
        _, _, r, c = g4.shape
        in_specs.append(pl.BlockSpec((None, None, r, c), lambda q: (q, lax.axis_index("c"), 0, 0)))
        out_specs.append(pl.BlockSpec((None, r, c), lambda q: (q, 0, 0)))
        out_shape.append(jax.ShapeDtypeStruct((N_CHIP, r, c), BF16))
    in_specs += [pl.BlockSpec((None,) + g4.shape[2:], lambda q: (q, 0, 0)) for g4 in g4s]

    def body(*refs):
        for t in range(n_t):
            refs[2 * n_t + t][...] = (refs[t][...] + refs[n_t + t][...]).astype(BF16)

    return pl.pallas_call(body, grid=(N_CHIP,), in_specs=in_specs, out_specs=out_specs, out_shape=out_shape,
                          compiler_params=_params(1), name=name)(*g4s, *lands)


class _ChipExchange:
    mid = None

    def __init__(self, parts):
        n_t = len(parts)
        self.inputs = list(parts)
        self.out_shapes = [jax.ShapeDtypeStruct(p.shape, p.dtype) for p in parts]
        self.scratch = [pltpu.SemaphoreType.DMA((n_t, 3)), pltpu.SemaphoreType.DMA((n_t, 3)),
                        pltpu.SemaphoreType.DMA((n_t,))]

    def _plan(self, p_refs, land, sems):
        send, recv, local_sem = sems
        mx, my, mc = _coords()
        my_chip = 2 * mx + my
        peers = [(_flip(mx, fx), _flip(my, fy)) for fx, fy in ((1, 0), (0, 1), (1, 1))]

        def out(t, k):
            px, py = peers[k]
            return pltpu.make_async_remote_copy(p_refs[t].at[2 * px + py], land[t].at[my_chip], send.at[t, k],
                                                recv.at[t, k], device_id=(px, py, mc), device_id_type=MESH)

        def arrival(t, k):
            px, py = peers[k]
            return pltpu.make_async_remote_copy(p_refs[t].at[my_chip], land[t].at[2 * px + py], send.at[t, k],
                                                recv.at[t, k], device_id=(px, py, mc), device_id_type=MESH)

        def local(t):
            return pltpu.make_async_copy(p_refs[t].at[my_chip], land[t].at[my_chip], local_sem.at[t])

        return out, arrival, local

    def start(self, p_refs, land, sems):
        out, arrival, local = self._plan(p_refs, land, sems)
        for t in range(len(p_refs)):
            local(t).start()
            for k in range(3):
                out(t, k).start()

    def finish(self, p_refs, land, sems):
        out, arrival, local = self._plan(p_refs, land, sems)
        for t in range(len(p_refs)):
            for k in range(3):
                arrival(t, k).wait_recv()
                out(t, k).wait_send()
            local(t).wait()


def _rope_tables(s, width):
    heads = width // HEAD_DIM
    inv_freq = ROPE_THETA ** (-jnp.arange(0, HEAD_DIM, 2, dtype=F32) / HEAD_DIM)
    inv_full = jnp.tile(inv_freq, 2 * heads)
    sign = jnp.tile(jnp.concatenate([-jnp.ones((HALF_HEAD,), F32), jnp.ones((HALF_HEAD,), F32)]), heads)
    ang = jnp.arange(s, dtype=F32)[:, None] * inv_full[None, :]
    return jnp.cos(ang), jnp.sin(ang) * sign[None, :]


def _pad_rows(v, rows):
    return jnp.concatenate([v, jnp.zeros((rows - 1, v.shape[1]), v.dtype)], axis=0)


def kernel(x, c, w_ada, b_ada, ffn1_norm_g, ffn1_w_gate, ffn1_w_up, ffn1_w_down, mix_norm_g, w_in, conv_dw_w, conv_dw_b, conv_ln_g, conv_ln_b, attn_out_g, conv_out_g, w_out, ffn2_norm_g, ffn2_w_gate, ffn2_w_up, ffn2_w_down, final_norm_g, loss_target, m_w_ada, m_b_ada, m_ffn1_norm_g, m_ffn1_w_gate, m_ffn1_w_up, m_ffn1_w_down, m_mix_norm_g, m_w_in, m_conv_dw_w, m_conv_dw_b, m_conv_ln_g, m_conv_ln_b, m_attn_out_g, m_conv_out_g, m_w_out, m_ffn2_norm_g, m_ffn2_w_gate, m_ffn2_w_up, m_ffn2_w_down, m_final_norm_g, v_w_ada, v_b_ada, v_ffn1_norm_g, v_ffn1_w_gate, v_ffn1_w_up, v_ffn1_w_down, v_mix_norm_g, v_w_in, v_conv_dw_w, v_conv_dw_b, v_conv_ln_g, v_conv_ln_b, v_attn_out_g, v_conv_out_g, v_w_out, v_ffn2_norm_g, v_ffn2_w_gate, v_ffn2_w_up, v_ffn2_w_down, v_final_norm_g):
    mx, my, mc = _coords()
    me = 4 * mx + 2 * my + mc
    s, d = x.shape[1], x.shape[2]
    aw = d // 2
    x2, target = x[0], loss_target[0]
    n_mod = w_ada.shape[2] * N_DEV // d
    mod_cols = w_ada.shape[2]

    def shard(w, transpose):
        return (w[0].T if transpose else w[0]).astype(BF16)

    cw_shard = conv_dw_w.shape[3]
    n_taps = CONV_KERNEL * cw_shard
    first_len = -(-(d + n_taps) // LANES) * LANES
    first = jnp.concatenate([c, conv_dw_w[0, :, 0, :].reshape(1, n_taps), jnp.zeros((1, first_len - d - n_taps), F32)], axis=1)
    first_all, silu_c, _, mod_all, wg1, wu1 = _standalone(
        _Together(_ModExchange(_pad_rows(first, 8), w_ada[0]),
                  _GatherWeights([shard(ffn1_w_gate, True), shard(ffn1_w_up, True)])), "ag_first")
    first_all = first_all[0::8]
    conv_w = first_all[:, d:d + n_taps].reshape(N_DEV, CONV_KERNEL, cw_shard).transpose(1, 0, 2).reshape(CONV_KERNEL, aw)

    mod_all = mod_all.reshape(N_DEV, N_DEV, mod_cols)
    mod = lax.dynamic_index_in_dim(mod_all, me, axis=1, keepdims=False).reshape(1, n_mod * d) + b_ada
    sh1, sc1, g1, sh2, sc2, g2, sh3, sc3, g3 = [mod[:, i * d:(i + 1) * d] for i in range(n_mod)]

    def split(g):
        return g.reshape(N_CHIP, 2, g.shape[0] // N_DEV, g.shape[1])

    def partials(g4s, lands, tag):
        return _chip_partials(g4s, lands, "chip_partials_" + tag)

    (n1, silu1, gs1, hid1), (wd1,) = _norm_ffn_up(x2, ffn1_norm_g, sc1, sh1, wg1, wu1, "ffn1_up",
                                                  comm=_GatherWeights([shard(ffn1_w_down, False)]))
    (h1, f1, n2), (win_t,) = _residual_mm(hid1, wd1, x2, g1, 0.5, "ffn1_down", norm=(mix_norm_g, sc2, sh2),
                                          comm=_GatherWeights([shard(w_in, True)]))
    cos, sin_signed = _rope_tables(s, LANES)
    proj, = _proj_rope(n2, win_t, cos, sin_signed, aw, "proj")
    lanes_per = aw // LANES
    (attn, lse), (wg2, wu2, wd2) = _attn_seq_fwd(
        proj, aw, "attn_fwd",
        comm=_GatherWeights([shard(ffn2_w_gate, True), shard(ffn2_w_up, True), shard(ffn2_w_down, False)]))
    (u1,), (wout,) = _conv_fwd(proj, 3 * lanes_per, 4 * lanes_per, conv_w, conv_dw_b, "conv_fwd",
                               comm=_GatherWeights([shard(w_out, False)]))
    post = (attn_out_g, conv_ln_g, conv_ln_b, conv_out_g)
    y, h2, mix, n3 = _mix_out(attn, u1, post, wout, h1, g2, (ffn2_norm_g, sc3, sh3), "mix_out")
    silu3, gs3, hid3 = _ffn_up(n3, wg2, wu2, "ffn2_up")

    dh3, df3, err2, d_final_g, dg3 = _last_mm_loss(hid3, wd2, h2, g3, 0.5, target, final_norm_g.reshape(1, d),
                                                   "ffn2_down_loss")
    loss_part = jnp.zeros((1, LANES), F32).at[0, 0].set(0.5 * jnp.sum(err2) / d)

    da3, db3 = _ffn_bwd_hidden(df3, wd2, silu3, gs3, "ffn2_hidden_bwd")
    g4_a = [split(_mm_tn(da3, n3, "ffn2_dwg")), split(_mm_tn(db3, n3, "ffn2_dwu")), split(_mm_tn(hid3, df3, "ffn2_dwd"))]
    (dh2, dmix, dsh3, dsc3, dgn3, dg2), land_a = _mm_norm_mod_bwd(
        [(da3, wg2), (db3, wu2)], h2, dh3, ffn2_norm_g, sc3, (mix, g2, 1.0), "ffn2_dn_norm3_bwd", tm=256,
        comm=_SiblingExchange(g4_a))
    parts_a = partials(g4_a, land_a, "a")
    g_wout = _mm_tn(y, dmix, "mix_dwout")
    dattn, du1, d_gains, d_ln = _mix_dy_post_bwd(dmix, wout, attn, u1, post, "mix_dy_post_bwd")
    d_attn_g, d_conv_g, d_ln_g, d_ln_b = d_gains[:, :aw], d_gains[:, aw:], d_ln[:, :aw], d_ln[:, aw:]
    dga, dgb, d_taps, d_conv_b = _conv_bwd(proj, 3 * lanes_per, 4 * lanes_per, conv_w, du1, "conv_bwd")
    (dq, dk, dv), sums_a = _attn_seq_bwd(proj, dattn, attn, lse, cos, sin_signed, "attn_bwd",
                                         comm=_ChipExchange(parts_a))
    dproj = jnp.concatenate([dq, dk, dv, dga, dgb], axis=1)
    g4_b = [split(g_wout), split(_mm_tn(dproj, n2, "mix_dwin"))]
    (dh1, df1, dsh2, dsc2, dgn2, dg1), land_b = _mm_norm_mod_bwd(
        [(dproj, win_t)], h1, dh2, mix_norm_g, sc2, (f1, g1, 0.5), "mix_dn_norm2_bwd", tm=512,
        comm=_SiblingExchange(g4_b))
    parts_b = partials(g4_b, land_b, "b")
    g4_c = [split(_mm_tn(hid1, df1, "ffn1_dwd"))]
    (da1, db1), both = _ffn_bwd_hidden(df1, wd1, silu1, gs1, "ffn1_hidden_bwd",
                                       comm=_Together(_ChipExchange(parts_b), _SiblingExchange(g4_c)))
    sums_b, land_c = both[:2], both[2:]
    parts_c = partials(g4_c, land_c, "c")
    g_wu1, sums_c = _mm_tn(db1, n1, "ffn1_dwu", comm=_ChipExchange(parts_c))
    g4_d = [split(g_wu1)]
    g_wg1, land_d = _mm_tn(da1, n1, "ffn1_dwg", comm=_SiblingExchange(g4_d))
    parts_d = partials(g4_d, land_d, "d")
    g4_e = [split(g_wg1)]
    dn1, both = _plain_mm([(da1, wg1), (db1, wu1)], BF16, False, d, "ffn1_dn",
                          comm=_Together(_ChipExchange(parts_d), _SiblingExchange(g4_e)))
    sums_d, land_e = both[:1], both[1:]
    parts_e = partials(g4_e, land_e, "e")
    (dx, dsh1, dsc1, dgn1), sums_e = _norm_mod_bwd(dn1, x2, dh1, ffn1_norm_g, sc1, "norm1_bwd",
                                                   comm=_ChipExchange(parts_e))

    dmod = jnp.concatenate([dsh1, dsc1, dg1, dsh2, dsc2, dg2, dsh3, dsc3, dg3], axis=1)
    small = [dmod, dgn1, dgn2, dgn3, d_final_g, d_conv_b, d_ln_g, d_ln_b, d_attn_g, d_conv_g,
             d_taps.reshape(1, CONV_KERNEL * aw), loss_part]
    sizes = [v.shape[1] for v in small]
    total = sum(sizes)
    padded = -(-total // (8 * LANES)) * (8 * LANES)
    packed = jnp.concatenate(small + [jnp.zeros((1, padded - total), F32)], axis=1).reshape(8, padded // 8)
    gathered = _ag_small(packed, "ag_small_grads")
    summed = _sum_blocks(gathered, N_DEV, "sum_small_grads").reshape(1, padded)
    offs = [sum(sizes[:i]) for i in range(len(sizes))]
    (g_b_ada, g_gn1, g_gn2, g_gn3, g_final, g_conv_b, g_ln_g, g_ln_b, g_attn_g, g_conv_g, g_taps, loss_row) = [
        summed[:, o:o + n] for o, n in zip(offs, sizes)]
    loss = loss_row[0, 0]
    g_taps_shard = lax.dynamic_slice_in_dim(g_taps.reshape(CONV_KERNEL, aw), me * cw_shard, cw_shard, axis=1)
    dmod_all = gathered.reshape(N_DEV, padded)[:, :n_mod * d]
    dmod_cols = lax.dynamic_slice_in_dim(dmod_all, me * mod_cols, mod_cols, axis=1)
    g_w_ada = _mm_tn(silu_c, dmod_cols, "ada_dw")

    arrived = dict(zip(["ffn2_w_gate", "ffn2_w_up", "ffn2_w_down", "w_out", "w_in", "ffn1_w_down", "ffn1_w_up",
                        "ffn1_w_gate"], list(sums_a) + list(sums_b) + list(sums_c) + list(sums_d) + list(sums_e)))
    transposed = ("ffn1_w_gate", "ffn1_w_up", "w_in", "ffn2_w_gate", "ffn2_w_up")
    grads = {
        "w_ada": g_w_ada, "b_ada": g_b_ada, "ffn1_norm_g": g_gn1, "mix_norm_g": g_gn2, "conv_dw_w": g_taps_shard,
        "conv_dw_b": g_conv_b, "conv_ln_g": g_ln_g, "conv_ln_b": g_ln_b, "attn_out_g": g_attn_g,
        "conv_out_g": g_conv_g, "ffn2_norm_g": g_gn3, "final_norm_g": g_final,
    }
    weights = dict(w_ada=w_ada, b_ada=b_ada, ffn1_norm_g=ffn1_norm_g, ffn1_w_gate=ffn1_w_gate, ffn1_w_up=ffn1_w_up, ffn1_w_down=ffn1_w_down, mix_norm_g=mix_norm_g, w_in=w_in, conv_dw_w=conv_dw_w, conv_dw_b=conv_dw_b, conv_ln_g=conv_ln_g, conv_ln_b=conv_ln_b, attn_out_g=attn_out_g, conv_out_g=conv_out_g, w_out=w_out, ffn2_norm_g=ffn2_norm_g, ffn2_w_gate=ffn2_w_gate, ffn2_w_up=ffn2_w_up, ffn2_w_down=ffn2_w_down, final_norm_g=final_norm_g)
    moms = dict(w_ada=m_w_ada, b_ada=m_b_ada, ffn1_norm_g=m_ffn1_norm_g, ffn1_w_gate=m_ffn1_w_gate, ffn1_w_up=m_ffn1_w_up, ffn1_w_down=m_ffn1_w_down, mix_norm_g=m_mix_norm_g, w_in=m_w_in, conv_dw_w=m_conv_dw_w, conv_dw_b=m_conv_dw_b, conv_ln_g=m_conv_ln_g, conv_ln_b=m_conv_ln_b, attn_out_g=m_attn_out_g, conv_out_g=m_conv_out_g, w_out=m_w_out, ffn2_norm_g=m_ffn2_norm_g, ffn2_w_gate=m_ffn2_w_gate, ffn2_w_up=m_ffn2_w_up, ffn2_w_down=m_ffn2_w_down, final_norm_g=m_final_norm_g)
    vars_ = dict(w_ada=v_w_ada, b_ada=v_b_ada, ffn1_norm_g=v_ffn1_norm_g, ffn1_w_gate=v_ffn1_w_gate, ffn1_w_up=v_ffn1_w_up, ffn1_w_down=v_ffn1_w_down, mix_norm_g=v_mix_norm_g, w_in=v_w_in, conv_dw_w=v_conv_dw_w, conv_dw_b=v_conv_dw_b, conv_ln_g=v_conv_ln_g, conv_ln_b=v_conv_ln_b, attn_out_g=v_attn_out_g, conv_out_g=v_conv_out_g, w_out=v_w_out, ffn2_norm_g=v_ffn2_norm_g, ffn2_w_gate=v_ffn2_w_gate, ffn2_w_up=v_ffn2_w_up, ffn2_w_down=v_ffn2_w_down, final_norm_g=v_final_norm_g)
    names = list(weights)
    big = ["w_ada", "ffn1_w_gate", "ffn1_w_up", "ffn1_w_down", "w_in", "w_out", "ffn2_w_gate", "ffn2_w_up",
           "ffn2_w_down"]
    shape2 = {n: (weights[n].shape[-2] if weights[n].ndim > 1 else 1, weights[n].shape[-1]) for n in names}
    shape2["conv_dw_w"] = (CONV_KERNEL, cw_shard)
    g_out, d_out, m_out, v_out = {}, {}, {}, {}
    for n in big:
        if n in arrived:
            def view(t, n=n):
                return t[0].T if n in transposed else t[0]
            res = _adamw_reduced(view(weights[n]), arrived[n], view(moms[n]), view(vars_[n]), "adamw_" + n)
            g_out[n], d_out[n], m_out[n], v_out[n] = [r.T if n in transposed else r for r in res]
        else:
            g2d = grads[n].reshape(shape2[n])
            res = _adamw_big(weights[n].reshape(shape2[n]), g2d, moms[n].reshape(shape2[n]),
                             vars_[n].reshape(shape2[n]), "adamw_" + n)
            g_out[n], (d_out[n], m_out[n], v_out[n]) = g2d, res
    rest = [n for n in names if n not in big]
    res = _adamw_small([weights[n].reshape(shape2[n]) for n in rest], [grads[n].reshape(shape2[n]) for n in rest],
                       [moms[n].reshape(shape2[n]) for n in rest], [vars_[n].reshape(shape2[n]) for n in rest],
                       "adamw_small")
    for i, n in enumerate(rest):
        g_out[n], d_out[n], m_out[n], v_out[n] = grads[n], res[0][i], res[1][i], res[2][i]

    def shaped(table):
        return [table[n].reshape(weights[n].shape) for n in names]

    return (loss, dx.reshape(x.shape), *shaped(g_out), *shaped(d_out), *shaped(m_out), *shaped(v_out))
```

```python
import functools

import jax
import jax.numpy as jnp
from jax import lax
from jax.experimental import pallas as pl
from jax.experimental.pallas import tpu as pltpu

F32 = jnp.float32
BF16 = jnp.bfloat16
MESH = pl.DeviceIdType.MESH
ANY = pl.BlockSpec(memory_space=pl.ANY)

N_DEV = 8
N_CHIP = 4
HEAD_DIM = 64
HALF_HEAD = HEAD_DIM // 2
LANES = 128
BLOCK = 128
DILATIONS = (1, 4, 16)
MERGE_CHUNK = 512
ROPE_THETA = 10000.0
CONV_KERNEL = 31
CONV_HALO = 32
CONV_CHUNK = 512
CONV_SUB = 128
RMS_EPS = 1e-6
LN_EPS = 1e-5
ADAM_LR = 0.001
ADAM_B1 = 0.9
ADAM_B2 = 0.999
ADAM_EPS = 1e-08
ADAM_WD = 0.01
ADAM_STEP = 10
VMEM_LIMIT = 56 * 1024 * 1024
NEG = -1e30


def _params(n_axes):
    return pltpu.CompilerParams(dimension_semantics=("arbitrary",) * n_axes, vmem_limit_bytes=VMEM_LIMIT)


def _tile(n, target, unit):
    best = None
    for t in range(unit, min(n, target) + 1, unit):
        if n % t == 0:
            best = t
    return best if best is not None else n


def _sigmoid(x):
    return 0.5 * (jnp.tanh(0.5 * x) + 1.0)


def _call(body, *, grid, in_specs, out_specs, out_shape, args, name, scratch_shapes=(), comm=None):
    params = _params(len(grid))
    if comm is None:
        return pl.pallas_call(body, grid=grid, in_specs=list(in_specs), out_specs=list(out_specs),
                              out_shape=list(out_shape), scratch_shapes=list(scratch_shapes),
                              compiler_params=params, name=name)(*args)
    n_in, n_out, n_scr = len(args), len(out_shape), len(scratch_shapes)
    c_in, c_out = len(comm.inputs), len(comm.out_shapes)
    steps = 1
    for g in grid:
        steps *= g

    def hosted(*refs):
        pos = 0
        parts = []
        for size in (n_in, c_in, n_out, c_out, n_scr, len(comm.scratch)):
            parts.append(refs[pos:pos + size])
            pos += size
        ins, cin, outs, cout, scr, cscr = parts
        step = 0
        for axis, g in enumerate(grid):
            step = step * g + pl.program_id(axis)

        @pl.when(step == 0)
        def _():
            comm.start(cin, cout, cscr)

        body(*ins, *outs, *scr)
        if comm.mid is not None and steps >= 4:
            @pl.when(step == steps // 2)
            def _():
                comm.mid(cin, cout, cscr)

        @pl.when(step == steps - 1)
        def _():
            if comm.mid is not None and steps < 4:
                comm.mid(cin, cout, cscr)
            comm.finish(cin, cout, cscr)

    res = pl.pallas_call(
        hosted, grid=grid, in_specs=list(in_specs) + [ANY] * c_in, out_specs=list(out_specs) + [ANY] * c_out,
        out_shape=list(out_shape) + list(comm.out_shapes), scratch_shapes=list(scratch_shapes) + list(comm.scratch),
        compiler_params=params, name=name)(*args, *comm.inputs)
    return res[:n_out], res[n_out:]


def _rows(fn, rows_in, vecs_in, rows_out, vecs_out, *, tile, name, comm=None):
    norm = [r if isinstance(r, tuple) else (r, r.shape[1], 0) for r in rows_in]
    n_rows = norm[0][0].shape[0]
    n_tiles = n_rows // tile
    in_specs, args = [], []
    for arr, width, cb in norm:
        in_specs.append(pl.BlockSpec((tile, width), functools.partial(lambda i, cb: (i, cb), cb=cb)))
        args.append(arr)
    for v in vecs_in:
        in_specs.append(pl.BlockSpec((1, v.shape[1]), lambda i: (0, 0)))
        args.append(v)
    out_shape = [jax.ShapeDtypeStruct((n_rows, w), dt) for w, dt in rows_out]
    out_shape += [jax.ShapeDtypeStruct((1, w), F32) for w in vecs_out]
    out_specs = [pl.BlockSpec((tile, w), lambda i: (i, 0)) for w, _ in rows_out]
    out_specs += [pl.BlockSpec((1, w), lambda i: (0, 0)) for w in vecs_out]
    n_in, n_ro = len(args), len(rows_out)

    def body(*refs):
        vals = [r[...] for r in refs[:n_in]]
        outs = refs[n_in:]
        row_vals, vec_vals = fn(*vals)
        for ref, val in zip(outs[:n_ro], row_vals):
            if isinstance(val, tuple):
                w = val[0].shape[1]
                for j, piece in enumerate(val):
                    ref[:, j * w:(j + 1) * w] = piece.astype(ref.dtype)
            else:
                ref[...] = val.astype(ref.dtype)
        if vecs_out:
            @pl.when(pl.program_id(0) == 0)
            def _():
                for ref in outs[n_ro:]:
                    ref[...] = jnp.zeros_like(ref)
            for ref, val in zip(outs[n_ro:], vec_vals):
                ref[...] += val

    return _call(body, grid=(n_tiles,), in_specs=in_specs, out_specs=out_specs, out_shape=out_shape, args=args,
                 name=name, comm=comm)


def _colsum(x):
    return jnp.sum(x, axis=0, keepdims=True)


def _rms_stats(h):
    r = lax.rsqrt(jnp.mean(h * h, axis=-1, keepdims=True) + RMS_EPS)
    return r, h * r


def _rms_back(r, xn, dxn):
    return r * (dxn - xn * jnp.mean(dxn * xn, axis=-1, keepdims=True))


def _branch_back(dh, f, gate, coef):
    return (coef * gate) * dh, coef * _colsum(f.astype(F32) * dh)


def _norm_mod_back(dn, h, dh_in, gain, scale):
    dn = dn.astype(F32)
    r, xn = _rms_stats(h)
    y = xn * gain
    dy = dn * (1.0 + scale)
    dh = dh_in + _rms_back(r, xn, dy * gain)
    return dh, [_colsum(dn), _colsum(dn * y), _colsum(dy * xn)]


def _norm_mod_bwd(dn, h, dh_in, gain, scale, name, comm=None):
    d = h.shape[1]

    def fn(dn, h, dh_in, gain, scale):
        dh, vecs = _norm_mod_back(dn, h, dh_in, gain, scale)
        return [dh], vecs
    return _rows(fn, [dn, h, dh_in], [gain, scale], [(d, F32)], [d, d, d], tile=256, name=name, comm=comm)


def _mm_norm_mod_bwd(pairs, h, dh_in, gain, scale, branch, name, tm, comm=None):
    f, gate, coef = branch

    def epi(accs, ex, vc):
        dh, vecs = _norm_mod_back(accs[0], ex[0], ex[1], vc[0], vc[1])
        df, dgate = _branch_back(dh, ex[2], vc[2], coef)
        return [dh, df] + vecs + [dgate]
    return _mm([pairs], epi, [h, dh_in, f], [gain, scale, gate], [F32, BF16], trans_rhs=False, tm=tm,
               tn=h.shape[1], name=name, n_sums=4, comm=comm)


def _last_mm_loss(lhs, w, res, gate, coef, target, gain, name):
    d = w.shape[1]

    def epi(accs, ex, vc):
        f = accs[0]
        h = ex[0] + (coef * vc[0]) * f
        r, xn = _rms_stats(h)
        err = xn * vc[1] - ex[1]
        dout = err * (1.0 / d)
        dh = _rms_back(r, xn, dout * vc[1])
        df, dgate = _branch_back(dh, f, vc[0], coef)
        return [dh, df, _colsum(err * err), _colsum(dout * xn), dgate]
    return _mm([[(lhs, w)]], epi, [res, target], [gate, gain], [F32, BF16], trans_rhs=False, tm=256, tn=d,
               name=name, n_sums=3)


def _partner(x):
    if x.shape[1] > LANES:
        return jnp.concatenate([_partner(x[:, c:c + LANES]) for c in range(0, x.shape[1], LANES)], axis=1)
    lane = lax.broadcasted_iota(jnp.int32, x.shape, 1) % HEAD_DIM
    return jnp.where(lane < HALF_HEAD, pltpu.roll(x, LANES - HALF_HEAD, 1), pltpu.roll(x, HALF_HEAD, 1))


def _proj_rope(n, w_t, cos, sin_signed, width, name, comm=None):
    s, kdim = n.shape
    n_cols = w_t.shape[0]
    tm = _tile(s, 1024, 8)
    qscale = HEAD_DIM ** -0.5

    chunk = _tile(tm, 256, 8)

    def body(n_ref, w_ref, cos_ref, sin_ref, o_ref):
        j = pl.program_id(0)

        def products(rows):
            return lax.dot_general(n_ref[rows, :].astype(BF16), w_ref[...].astype(BF16), (((1,), (1,)), ((), ())),
                                   preferred_element_type=F32)

        @pl.when(j >= 2)
        def _():
            for c in range(tm // chunk):
                rows = slice(c * chunk, (c + 1) * chunk)
                o_ref[rows, :] = products(rows)

        @pl.when(j < 2)
        def _():
            scale = jnp.where(j == 0, qscale, 1.0)
            for c in range(tm // chunk):
                rows = slice(c * chunk, (c + 1) * chunk)
                acc = products(rows)
                cos = jnp.tile(cos_ref[rows, :], (1, width // LANES))
                sin = jnp.tile(sin_ref[rows, :], (1, width // LANES))
                o_ref[rows, :] = scale * (acc * cos + _partner(acc) * sin)

    table = pl.BlockSpec((tm, LANES), lambda j, i: (jnp.where(j < 2, i, 0), 0))
    return _call(
        body, grid=(n_cols // width, s // tm),
        in_specs=[pl.BlockSpec((tm, kdim), lambda j, i: (i, 0)), pl.BlockSpec((width, kdim), lambda j, i: (j, 0)),
                  table, table],
        out_specs=[pl.BlockSpec((tm, width), lambda j, i: (i, j))],
        out_shape=[jax.ShapeDtypeStruct((s, n_cols), F32)], args=(n, w_t, cos, sin_signed), name=name, comm=comm)


def _mix_post(attn, u1, attn_g, ln_g, ln_b, conv_g):
    _, xa = _rms_stats(attn)
    mu = jnp.mean(u1, axis=-1, keepdims=True)
    xc = u1 - mu
    rstd = lax.rsqrt(jnp.mean(xc * xc, axis=-1, keepdims=True) + LN_EPS)
    u2 = (xc * rstd) * ln_g + ln_b
    u3 = u2 * _sigmoid(u2)
    _, x3 = _rms_stats(u3)
    return jnp.concatenate([xa * attn_g, x3 * conv_g], axis=1)


def _mix_post_back(dy, attn, u1, attn_g, ln_g, ln_b, conv_g):
    w = attn.shape[1]
    dya, dyc = dy[:, :w], dy[:, w:]
    ra, xa = _rms_stats(attn)
    dattn = _rms_back(ra, xa, dya * attn_g)
    mu = jnp.mean(u1, axis=-1, keepdims=True)
    xc = u1 - mu
    rstd = lax.rsqrt(jnp.mean(xc * xc, axis=-1, keepdims=True) + LN_EPS)
    xh = xc * rstd
    u2 = xh * ln_g + ln_b
    sig = _sigmoid(u2)
    u3 = u2 * sig
    r3, x3 = _rms_stats(u3)
    du3 = _rms_back(r3, x3, dyc * conv_g)
    du2 = du3 * (sig + u3 * (1.0 - sig))
    dxh = du2 * ln_g
    du1 = rstd * (dxh - jnp.mean(dxh, axis=-1, keepdims=True) - xh * jnp.mean(dxh * xh, axis=-1, keepdims=True))
    return dattn, du1, [_colsum(dya * xa), _colsum(dyc * x3), _colsum(du2 * xh), _colsum(du2)]


def _mm(groups, epi, extras, vecs, outs, *, trans_rhs, tm, tn, name, n_sums=0, pre=None, pre_inputs=(),
        comm=None):
    m = (pre_inputs[0] if pre is not None else groups[0][0][0]).shape[0]
    n = groups[0][0][1].shape[0] if trans_rhs else groups[0][0][1].shape[1]
    tm, tn = min(tm, m), min(tn, n)
    in_specs, args, uses_pre = [], [], []
    for grp in groups:
        for lhs, rhs in grp:
            k = rhs.shape[1] if trans_rhs else rhs.shape[0]
            uses_pre.append(lhs is None)
            if lhs is not None:
                in_specs.append(pl.BlockSpec((tm, k), lambda j, i: (i, 0)))
                args.append(lhs)
            in_specs.append(pl.BlockSpec((tn, k), lambda j, i: (j, 0)) if trans_rhs
                            else pl.BlockSpec((k, tn), lambda j, i: (0, j)))
            args.append(rhs)
    n_mm = len(args)
    for p in pre_inputs:
        in_specs.append(pl.BlockSpec((tm, p.shape[1]), lambda j, i: (i, 0)))
        args.append(p)
    for e in extras:
        in_specs.append(pl.BlockSpec((tm, tn), lambda j, i: (i, j)) if e.shape[1] == n
                        else pl.BlockSpec((tm, e.shape[1]), lambda j, i: (i, 0)))
        args.append(e)
    for v in vecs:
        in_specs.append(pl.BlockSpec((1, tn), lambda j, i: (0, j)) if v.shape[1] == n
                        else pl.BlockSpec((1, v.shape[1]), lambda j, i: (0, 0)))
        args.append(v)
    sizes = [len(g) for g in groups]
    n_pre, n_ex, n_vec = len(pre_inputs), len(extras), len(vecs)
    dims = (((1,), (1,)), ((), ())) if trans_rhs else (((1,), (0,)), ((), ()))
    out_specs, out_shape = [], []
    if pre is not None:
        k_pre = args[n_mm - 1].shape[1] if trans_rhs else args[n_mm - 1].shape[0]
        out_specs.append(pl.BlockSpec((tm, k_pre), lambda j, i: (i, 0)))
        out_shape.append(jax.ShapeDtypeStruct((m, k_pre), BF16))
    for o in outs:
        dt, width = o if isinstance(o, tuple) else (o, n)
        out_specs.append(pl.BlockSpec((tm, tn), lambda j, i: (i, j)) if width == n
                         else pl.BlockSpec((tm, width), lambda j, i: (i, 0)))
        out_shape.append(jax.ShapeDtypeStruct((m, width), dt))
    n_tiles_out = len(out_specs)
    out_specs += [pl.BlockSpec((1, tn), lambda j, i: (0, j))] * n_sums
    out_shape += [jax.ShapeDtypeStruct((1, n), F32)] * n_sums

    def body(*refs):
        ins = refs[:n_mm + n_pre + n_ex + n_vec]
        out_refs = refs[n_mm + n_pre + n_ex + n_vec:]
        vc = [r[...] for r in ins[n_mm + n_pre + n_ex:]]
        vals = []
        made = None
        if pre is not None:
            made = pre([r[...] for r in ins[n_mm:n_mm + n_pre]], vc).astype(BF16)
            vals.append(made)
        accs, pos, pair = [], 0, 0
        for size in sizes:
            acc = None
            for _ in range(size):
                if uses_pre[pair]:
                    lhs_tile = made
                else:
                    lhs_tile = ins[pos][...].astype(BF16)
                    pos += 1
                part = lax.dot_general(lhs_tile, ins[pos][...].astype(BF16), dims, preferred_element_type=F32)
                acc = part if acc is None else acc + part
                pos += 1
                pair += 1
            accs.append(acc)
        ex = [r[...] for r in ins[n_mm + n_pre:n_mm + n_pre + n_ex]]
        vals += epi(accs, ex, vc)
        for ref, val in zip(out_refs[:n_tiles_out], vals):
            ref[...] = val.astype(ref.dtype)
        if n_sums:
            @pl.when(pl.program_id(1) == 0)
            def _():
                for ref in out_refs[n_tiles_out:]:
                    ref[...] = jnp.zeros_like(ref)
            for ref, val in zip(out_refs[n_tiles_out:], vals[n_tiles_out:]):
                ref[...] += val

    return _call(body, grid=(n // tn, m // tm), in_specs=in_specs, out_specs=out_specs, out_shape=out_shape,
                 args=args, name=name, comm=comm)


def _mm_tn(lhs, rhs, name, comm=None):
    t, a = lhs.shape
    b = rhs.shape[1]
    ta = a if a <= 1536 else _tile(a, 1536, LANES)
    tk = _tile(t, 2048, 8)

    def body(l_ref, r_ref, o_ref):
        @pl.when(pl.program_id(1) == 0)
        def _():
            o_ref[...] = jnp.zeros_like(o_ref)
        o_ref[...] += lax.dot_general(l_ref[...].astype(BF16), r_ref[...].astype(BF16), (((0,), (0,)), ((), ())),
                                      preferred_element_type=F32)

    res = _call(body, grid=(a // ta, t // tk),
                in_specs=[pl.BlockSpec((tk, ta), lambda i, k: (k, i)), pl.BlockSpec((tk, b), lambda i, k: (k, 0))],
                out_specs=[pl.BlockSpec((ta, b), lambda i, k: (i, 0))], out_shape=[jax.ShapeDtypeStruct((a, b), F32)],
                args=(lhs, rhs), name=name, comm=comm)
    return res[0] if comm is None else (res[0][0], res[1])


def _ffn_tn(f):
    return _tile(f, 1536, LANES)


def _swiglu_parts(a, b):
    sig = _sigmoid(a)
    silu = a * sig
    return [silu, b * (sig + silu * (1.0 - sig)), silu * b]


def _ffn_up(n, wg_t, wu_t, name, comm=None):
    def epi(accs, ex, vc):
        return _swiglu_parts(accs[0], accs[1])
    return _mm([[(n, wg_t)], [(n, wu_t)]], epi, [], [], [BF16, BF16, BF16], trans_rhs=True, tm=512,
               tn=_ffn_tn(wg_t.shape[0]), name=name, comm=comm)


def _norm_ffn_up(h, gain, scale, shift, wg_t, wu_t, name, comm=None):
    def pre(tiles, vc):
        _, xn = _rms_stats(tiles[0])
        return (xn * vc[0]) * (1.0 + vc[1]) + vc[2]

    def epi(accs, ex, vc):
        return _swiglu_parts(accs[0], accs[1])
    return _mm([[(None, wg_t)], [(None, wu_t)]], epi, [], [gain, scale, shift], [BF16, BF16, BF16], trans_rhs=True,
               tm=256, tn=wg_t.shape[0], name=name, pre=pre, pre_inputs=[h], comm=comm)


def _mix_out(attn, u1, post, w, res, gate, norm, name):
    def pre(tiles, vc):
        return _mix_post(tiles[0], tiles[1], *vc[4:8])

    def epi(accs, ex, vc):
        h = ex[0] + vc[0] * accs[0]
        _, xn = _rms_stats(h)
        return [h, accs[0], (xn * vc[1]) * (1.0 + vc[2]) + vc[3]]
    return _mm([[(None, w)]], epi, [res], [gate] + list(norm) + list(post), [F32, BF16, BF16], trans_rhs=False,
               tm=512, tn=w.shape[1], name=name, pre=pre, pre_inputs=[attn, u1])


def _mix_dy_post_bwd(dmix, w, attn, u1, post, name):
    width = attn.shape[1]

    def epi(accs, ex, vc):
        dattn, du1, sums = _mix_post_back(accs[0], ex[0], ex[1], *vc)
        return [dattn, du1, jnp.concatenate(sums[0:2], axis=1), jnp.concatenate(sums[2:4], axis=1)]
    return _mm([[(dmix, w)]], epi, [attn, u1], list(post), [(F32, width), (F32, width)], trans_rhs=True, tm=256,
               tn=w.shape[0], name=name, n_sums=2)


def _residual_mm(lhs, w, res, gate, coef, name, norm=None, comm=None):
    def epi(accs, ex, vc):
        h = ex[0] + (coef * vc[0]) * accs[0]
        if norm is None:
            return [h, accs[0]]
        _, xn = _rms_stats(h)
        return [h, accs[0], (xn * vc[1]) * (1.0 + vc[2]) + vc[3]]
    vecs = [gate] + (list(norm) if norm is not None else [])
    outs = [F32, BF16] + ([BF16] if norm is not None else [])
    return _mm([[(lhs, w)]], epi, [res], vecs, outs, trans_rhs=False, tm=512, tn=w.shape[1], name=name, comm=comm)


def _ffn_bwd_hidden(df, wd, dhid_db, dhid_da, name, comm=None):
    def epi(accs, ex, vc):
        return [accs[0] * ex[1].astype(F32), accs[0] * ex[0].astype(F32)]
    return _mm([[(df, wd)]], epi, [dhid_db, dhid_da], [], [BF16, BF16], trans_rhs=True, tm=512,
               tn=_ffn_tn(wd.shape[0]), name=name, comm=comm)


def _plain_mm(pairs, out_dtype, trans_rhs, tn, name, tm=512, comm=None):
    def epi(accs, ex, vc):
        return [accs[0]]
    res = _mm([pairs], epi, [], [], [out_dtype], trans_rhs=trans_rhs, tm=tm, tn=tn, name=name, comm=comm)
    return res[0] if comm is None else (res[0][0], res[1])


HEADS_PER_TILE = LANES // HEAD_DIM


def _stack_heads(x):
    lane = lax.broadcasted_iota(jnp.int32, (1, LANES), 1)
    return jnp.concatenate([x * (lane // HEAD_DIM == h).astype(F32) for h in range(HEADS_PER_TILE)], axis=0)


def _unstack_heads(y):
    r = y.shape[0] // HEADS_PER_TILE
    lane = lax.broadcasted_iota(jnp.int32, (r, y.shape[1]), 1)
    out = y[0:r]
    for h in range(1, HEADS_PER_TILE):
        out = jnp.where(lane // HEAD_DIM == h, y[h * r:(h + 1) * r], out)
    return out


def _stacked_lse(lb):
    return jnp.concatenate([_lane_pick(lb, h) for h in range(HEADS_PER_TILE)], axis=0)


def _band_masks(n_row_blocks, n_col_blocks):
    shape = (n_row_blocks * BLOCK, n_col_blocks * BLOCK)
    qi = lax.broadcasted_iota(jnp.int32, shape, 0) % BLOCK
    kj = lax.broadcasted_iota(jnp.int32, shape, 1) % BLOCK
    return kj <= qi, kj >= qi


def _query_masks():
    first_valid, _ = _band_masks(HEADS_PER_TILE, 1)
    same_ok, before_ok = _band_masks(HEADS_PER_TILE, 2)
    is_cur = lax.broadcasted_iota(jnp.int32, same_ok.shape, 1) >= BLOCK
    return first_valid, jnp.logical_and(is_cur, same_ok), jnp.logical_and(jnp.logical_not(is_cur), before_ok)


def _dot_nt(a, b):
    return lax.dot_general(a.astype(BF16), b.astype(BF16), (((1,), (1,)), ((), ())), preferred_element_type=F32)


def _dot_nn(a, b):
    return lax.dot_general(a.astype(BF16), b.astype(BF16), (((1,), (0,)), ((), ())), preferred_element_type=F32)


def _dot_tn(a, b):
    return lax.dot_general(a.astype(BF16), b.astype(BF16), (((0,), (0,)), ((), ())), preferred_element_type=F32)


def _lane_pick(x, h):
    lane = lax.broadcasted_iota(jnp.int32, x.shape, 1)
    return jnp.sum(jnp.where(lane == h * HEAD_DIM, x, 0.0), axis=1, keepdims=True)


def _block_rows(idx, d):
    span = BLOCK * d
    q0 = (idx // d) * span + idx % d
    return pl.ds(q0, BLOCK, stride=d), pl.ds(q0 - span, BLOCK, stride=d)


def _branch_loops(n_blocks, d, visit, unroll, masks):
    first_valid, cur_part, prev_part = masks
    if d % unroll == 0 and (n_blocks - d) % unroll == 0:
        full_valid = jnp.logical_or(cur_part, prev_part)

        def first(idx, carry):
            rows = pl.ds(idx, BLOCK, stride=d)
            visit(rows, [rows], first_valid)
            return carry

        def rest(idx, carry):
            rows, prev = _block_rows(idx, d)
            visit(rows, [prev, rows], full_valid)
            return carry

        lax.fori_loop(0, d, first, 0, unroll=unroll)
        lax.fori_loop(d, n_blocks, rest, 0, unroll=unroll)
        return

    def every(idx, carry):
        span = BLOCK * d
        q0 = (idx // d) * span + idx % d
        has_prev = idx >= d
        rows = pl.ds(q0, BLOCK, stride=d)
        prev = pl.ds(jnp.where(has_prev, q0 - span, q0), BLOCK, stride=d)
        visit(rows, [prev, rows], jnp.logical_or(cur_part, jnp.logical_and(prev_part, has_prev)))
        return carry

    lax.fori_loop(0, n_blocks, every, 0, unroll=unroll)


def _qkv_specs(s, tiles):
    q, k, v = [pl.BlockSpec((s, LANES), functools.partial(lambda hb, off: (0, off + hb), off=i * tiles))
               for i in range(3)]
    return q, k, v, pl.BlockSpec((s, LANES), lambda hb: (0, hb))


def _attn_seq_fwd(proj, width, name, comm=None):
    s = proj.shape[0]
    q_spec, k_spec, v_spec, cur = _qkv_specs(s, width // LANES)

    def body(q_ref, k_ref, v_ref, o_ref, l_ref, o_s, l_s):
        masks = _query_masks()
        for bi, d in enumerate(DILATIONS):
            def visit(rows, key_rows, valid, bi=bi):
                q2 = _stack_heads(q_ref[rows, :])
                keys = jnp.concatenate([k_ref[r, :] for r in key_rows], axis=0)
                vals = jnp.concatenate([v_ref[r, :] for r in key_rows], axis=0)
                sc = jnp.where(valid, _dot_nt(q2, keys), NEG)
                mx = jnp.max(sc, axis=1, keepdims=True)
                p = jnp.exp(sc - mx)
                den = jnp.sum(p, axis=1, keepdims=True)
                o_s[bi, rows, :] = _unstack_heads(_dot_nn(p, vals) / den)
                l_s[bi, rows, :] = _unstack_heads(jnp.broadcast_to(mx + jnp.log(den), (q2.shape[0], LANES)))

            _branch_loops(s // BLOCK, d, visit, 8, masks)
        for c in range(s // MERGE_CHUNK):
            rows = slice(c * MERGE_CHUNK, (c + 1) * MERGE_CHUNK)
            ls = [l_s[bi, rows, :] for bi in range(len(DILATIONS))]
            top = functools.reduce(jnp.maximum, ls)
            ws = [jnp.exp(l - top) for l in ls]
            den = functools.reduce(lambda a, b: a + b, ws)
            num = functools.reduce(lambda a, b: a + b, [w * o_s[bi, rows, :] for bi, w in enumerate(ws)])
            o_ref[rows, :] = num / den
            l_ref[rows, :] = top + jnp.log(den)

    return _call(
        body, grid=(width // LANES,), in_specs=[q_spec, k_spec, v_spec], out_specs=[cur, cur],
        out_shape=[jax.ShapeDtypeStruct((s, width), F32)] * 2,
        scratch_shapes=[pltpu.VMEM((len(DILATIONS), s, LANES), F32)] * 2,
        args=(proj, proj, proj), name=name, comm=comm)


def _attn_seq_bwd(proj, do, o, lse, cos, sin_signed, name, comm=None):
    s, width = do.shape
    q_spec, k_spec, v_spec, cur = _qkv_specs(s, width // LANES)
    table = pl.BlockSpec((s, LANES), lambda hb: (0, 0))
    qscale = HEAD_DIM ** -0.5

    def body(q_ref, k_ref, v_ref, do_ref, o_ref, l_ref, cos_ref, sin_ref, dq_out, dk_out, dv_out,
             dq_ref, dk_ref, dv_ref):
        dq_ref[...] = jnp.zeros_like(dq_ref)
        dk_ref[...] = jnp.zeros_like(dk_ref)
        dv_ref[...] = jnp.zeros_like(dv_ref)
        masks = _query_masks()
        for d in DILATIONS:
            def visit(rows, key_rows, valid):
                dob = do_ref[rows, :]
                q2 = _stack_heads(q_ref[rows, :])
                do2 = _stack_heads(dob)
                delta = jnp.sum(_stack_heads(dob * o_ref[rows, :]), axis=1, keepdims=True)
                lse2 = _stacked_lse(l_ref[rows, :])
                keys = jnp.concatenate([k_ref[r, :] for r in key_rows], axis=0)
                vals = jnp.concatenate([v_ref[r, :] for r in key_rows], axis=0)
                p = jnp.where(valid, jnp.exp(_dot_nt(q2, keys) - lse2), 0.0)
                ds = p * (_dot_nt(do2, vals) - delta)
                dq_ref[rows, :] += _unstack_heads(_dot_nn(ds, keys))
                dkk = _dot_tn(ds, q2)
                dvv = _dot_tn(p, do2)
                for i, r in enumerate(key_rows):
                    dk_ref[r, :] += dkk[i * BLOCK:(i + 1) * BLOCK]
                    dv_ref[r, :] += dvv[i * BLOCK:(i + 1) * BLOCK]

            _branch_loops(s // BLOCK, d, visit, 8, masks)
        for c in range(s // MERGE_CHUNK):
            rows = slice(c * MERGE_CHUNK, (c + 1) * MERGE_CHUNK)
            cos, sin = cos_ref[rows, :], sin_ref[rows, :]
            dq, dk = dq_ref[rows, :], dk_ref[rows, :]
            dq_out[rows, :] = ((dq * cos - _partner(dq) * sin) * qscale).astype(BF16)
            dk_out[rows, :] = (dk * cos - _partner(dk) * sin).astype(BF16)
            dv_out[rows, :] = dv_ref[rows, :].astype(BF16)

    return _call(
        body, grid=(width // LANES,), in_specs=[q_spec, k_spec, v_spec, cur, cur, cur, table, table],
        out_specs=[cur, cur, cur], out_shape=[jax.ShapeDtypeStruct((s, width), BF16)] * 3,
        scratch_shapes=[pltpu.VMEM((s, LANES), F32)] * 3,
        args=(proj, proj, proj, do, o, lse, cos, sin_signed), name=name, comm=comm)


def _conv_specs(s, a_block, b_block):
    per = CONV_CHUNK // CONV_HALO
    a_cur = pl.BlockSpec((CONV_CHUNK, LANES), lambda cb, i: (i, a_block + cb))
    b_cur = pl.BlockSpec((CONV_CHUNK, LANES), lambda cb, i: (i, b_block + cb))
    a_halo = pl.BlockSpec((CONV_HALO, LANES), lambda cb, i: (jnp.maximum(i * per - 1, 0), a_block + cb))
    b_halo = pl.BlockSpec((CONV_HALO, LANES), lambda cb, i: (jnp.maximum(i * per - 1, 0), b_block + cb))
    w_spec = pl.BlockSpec((CONV_KERNEL, LANES), lambda cb, i: (0, cb))
    vec = pl.BlockSpec((1, LANES), lambda cb, i: (0, cb))
    out = pl.BlockSpec((CONV_CHUNK, LANES), lambda cb, i: (i, cb))
    return a_cur, b_cur, a_halo, b_halo, w_spec, vec, out


def _fill_glu_window(win, a_ref, b_ref, ah_ref, bh_ref, first):
    halo = ah_ref[...] * _sigmoid(bh_ref[...])
    win[0:CONV_HALO, :] = jnp.where(first, 0.0, halo)
    win[CONV_HALO:, :] = a_ref[...] * _sigmoid(b_ref[...])


def _conv_fwd(proj, a_block, b_block, w, bias, name, comm=None):
    s = proj.shape[0]
    cw = w.shape[1]
    a_cur, b_cur, a_halo, b_halo, w_spec, vec, out = _conv_specs(s, a_block, b_block)
    lead = CONV_HALO - (CONV_KERNEL - 1)

    def body(a_ref, b_ref, ah_ref, bh_ref, w_ref, bias_ref, o_ref, win):
        _fill_glu_window(win, a_ref, b_ref, ah_ref, bh_ref, pl.program_id(1) == 0)
        for sub in range(CONV_CHUNK // CONV_SUB):
            base = sub * CONV_SUB
            acc = jnp.zeros((CONV_SUB, LANES), F32) + bias_ref[...]
            for j in range(CONV_KERNEL):
                acc = acc + w_ref[j:j + 1, :] * win[base + lead + j:base + lead + j + CONV_SUB, :]
            o_ref[base:base + CONV_SUB, :] = acc

    return _call(
        body, grid=(cw // LANES, s // CONV_CHUNK), in_specs=[a_cur, b_cur, a_halo, b_halo, w_spec, vec],
        out_specs=[out], out_shape=[jax.ShapeDtypeStruct((s, cw), F32)],
        scratch_shapes=[pltpu.VMEM((CONV_CHUNK + CONV_HALO, LANES), F32)],
        args=(proj, proj, proj, proj, w, bias), name=name, comm=comm)


def _conv_bwd(proj, a_block, b_block, w, du1, name):
    s = proj.shape[0]
    cw = w.shape[1]
    a_cur, b_cur, a_halo, b_halo, w_spec, vec, out = _conv_specs(s, a_block, b_block)
    per = CONV_CHUNK // CONV_HALO
    n_chunks = s // CONV_CHUNK
    d_next = pl.BlockSpec((CONV_HALO, LANES), lambda cb, i: (jnp.minimum((i + 1) * per, s // CONV_HALO - 1), cb))
    lead = CONV_HALO - (CONV_KERNEL - 1)

    def body(a_ref, b_ref, ah_ref, bh_ref, w_ref, d_ref, dn_ref, da_ref, db_ref, dw_ref, dbias_ref, win, dwin):
        i = pl.program_id(1)
        _fill_glu_window(win, a_ref, b_ref, ah_ref, bh_ref, i == 0)
        dwin[0:CONV_CHUNK, :] = d_ref[...]
        dwin[CONV_CHUNK:, :] = jnp.where(i == n_chunks - 1, 0.0, dn_ref[...])

        @pl.when(i == 0)
        def _():
            dw_ref[...] = jnp.zeros_like(dw_ref)
            dbias_ref[...] = jnp.zeros_like(dbias_ref)

        dbias_ref[...] += _colsum(d_ref[...])
        for sub in range(CONV_CHUNK // CONV_SUB):
            base = sub * CONV_SUB
            dcur = dwin[base:base + CONV_SUB, :]
            du0 = jnp.zeros((CONV_SUB, LANES), F32)
            for j in range(CONV_KERNEL):
                back = CONV_KERNEL - 1 - j
                du0 = du0 + w_ref[j:j + 1, :] * dwin[base + back:base + back + CONV_SUB, :]
                dw_ref[j:j + 1, :] += _colsum(dcur * win[base + lead + j:base + lead + j + CONV_SUB, :])
            av = a_ref[base:base + CONV_SUB, :]
            sig = _sigmoid(b_ref[base:base + CONV_SUB, :])
            da_ref[base:base + CONV_SUB, :] = (du0 * sig).astype(BF16)
            db_ref[base:base + CONV_SUB, :] = (du0 * av * sig * (1.0 - sig)).astype(BF16)

    return pl.pallas_call(
        body, grid=(cw // LANES, n_chunks), in_specs=[a_cur, b_cur, a_halo, b_halo, w_spec, out, d_next],
        out_specs=[out, out, w_spec, vec],
        out_shape=[jax.ShapeDtypeStruct((s, cw), BF16), jax.ShapeDtypeStruct((s, cw), BF16),
                   jax.ShapeDtypeStruct((CONV_KERNEL, cw), F32), jax.ShapeDtypeStruct((1, cw), F32)],
        scratch_shapes=[pltpu.VMEM((CONV_CHUNK + CONV_HALO, LANES), F32)] * 2,
        compiler_params=_params(2), name=name)(proj, proj, proj, proj, w, du1, du1)


def _adamw_math(w, g, m, v):
    m = ADAM_B1 * m + (1.0 - ADAM_B1) * g
    v = ADAM_B2 * v + (1.0 - ADAM_B2) * (g * g)
    m_hat = m / (1.0 - ADAM_B1 ** ADAM_STEP)
    v_hat = v / (1.0 - ADAM_B2 ** ADAM_STEP)
    delta = -ADAM_LR * (m_hat / (jnp.sqrt(v_hat) + ADAM_EPS) + ADAM_WD * w)
    return delta, m, v


def _adamw_big(w, g, m, v, name):
    rows, cols = w.shape
    tile = _tile(rows, 256, 8)
    spec = pl.BlockSpec((tile, cols), lambda i: (i, 0))

    def body(w_ref, g_ref, m_ref, v_ref, d_out, m_out, v_out):
        d_out[...], m_out[...], v_out[...] = _adamw_math(w_ref[...], g_ref[...], m_ref[...], v_ref[...])

    return pl.pallas_call(body, grid=(rows // tile,), in_specs=[spec] * 4, out_specs=[spec] * 3,
                          out_shape=[jax.ShapeDtypeStruct(w.shape, F32)] * 3, compiler_params=_params(1),
                          name=name)(w, g, m, v)


def _adamw_reduced(w, land, m, v, name):
    rows, cols = w.shape
    tile = _tile(rows, 256, 16)
    spec = pl.BlockSpec((tile, cols), lambda i: (i, 0))

    def body(w_ref, l_ref, m_ref, v_ref, g_out, d_out, m_out, v_out):
        g = l_ref[0].astype(F32)
        for q in range(1, N_CHIP):
            g = g + l_ref[q].astype(F32)
        g_out[...] = g
        d_out[...], m_out[...], v_out[...] = _adamw_math(w_ref[...], g, m_ref[...], v_ref[...])

    return pl.pallas_call(body, grid=(rows // tile,),
                          in_specs=[spec, pl.BlockSpec((N_CHIP, tile, cols), lambda i: (0, i, 0)), spec, spec],
                          out_specs=[spec] * 4, out_shape=[jax.ShapeDtypeStruct(w.shape, F32)] * 4,
                          compiler_params=_params(1), name=name)(w, land, m, v)


def _adamw_small(ws, gs, ms, vs, name):
    n = len(ws)

    def body(*refs):
        ins, outs = refs[:4 * n], refs[4 * n:]
        for t in range(n):
            res = _adamw_math(ins[t][...], ins[n + t][...], ins[2 * n + t][...], ins[3 * n + t][...])
            for j in range(3):
                outs[j * n + t][...] = res[j]

    shapes = [jax.ShapeDtypeStruct(w.shape, F32) for w in ws]
    res = pl.pallas_call(body, out_shape=shapes * 3, compiler_params=pltpu.CompilerParams(vmem_limit_bytes=VMEM_LIMIT),
                         name=name)(*ws, *gs, *ms, *vs)
    return res[:n], res[n:2 * n], res[2 * n:]


def _sum_blocks(x, n_blocks, name):
    r = x.shape[0] // n_blocks

    def body(x_ref, o_ref):
        acc = x_ref[0:r, :]
        for b in range(1, n_blocks):
            acc = acc + x_ref[b * r:(b + 1) * r, :]
        o_ref[...] = acc

    return pl.pallas_call(body, out_shape=jax.ShapeDtypeStruct((r, x.shape[1]), F32),
                          compiler_params=pltpu.CompilerParams(vmem_limit_bytes=VMEM_LIMIT), name=name)(x)


def _coords():
    return lax.axis_index("x"), lax.axis_index("y"), lax.axis_index("c")


def _flip(v, bit):
    return 1 - v if bit else v


def _ag_small(x, name):
    r, c = x.shape

    def body(x_ref, o_ref, send, recv, local_sem):
        mx, my, mc = _coords()

        def rows(px, py, pc):
            return o_ref.at[pl.ds(pl.multiple_of((4 * px + 2 * py + pc) * r, 8), r), :]

        local = pltpu.make_async_copy(x_ref, rows(mx, my, mc), local_sem)
        local.start()
        peers = [(_flip(mx, k >> 2 & 1), _flip(my, k >> 1 & 1), _flip(mc, k & 1)) for k in range(1, N_DEV)]
        sends = [pltpu.make_async_remote_copy(x_ref, rows(mx, my, mc), send.at[k], recv.at[k], device_id=p,
                                              device_id_type=MESH) for k, p in enumerate(peers)]
        for cp in sends:
            cp.start()
        for k, p in enumerate(peers):
            pltpu.make_async_remote_copy(x_ref, rows(*p), send.at[k], recv.at[k], device_id=p,
                                         device_id_type=MESH).wait_recv()
        for cp in sends:
            cp.wait_send()
        local.wait()

    vm = pl.BlockSpec(memory_space=pltpu.VMEM)
    return pl.pallas_call(
        body, in_specs=[vm], out_specs=vm, out_shape=jax.ShapeDtypeStruct((N_DEV * r, c), x.dtype),
        scratch_shapes=[pltpu.SemaphoreType.DMA((N_DEV - 1,)), pltpu.SemaphoreType.DMA((N_DEV - 1,)),
                        pltpu.SemaphoreType.DMA(())],
        name=name)(x)


class _GatherSmall:
    mid = None

    def __init__(self, x):
        self.inputs = [x]
        self.out_shapes = [jax.ShapeDtypeStruct((N_DEV * x.shape[0], x.shape[1]), x.dtype)]
        self.scratch = [pltpu.SemaphoreType.DMA((N_DEV - 1,)), pltpu.SemaphoreType.DMA((N_DEV - 1,)),
                        pltpu.SemaphoreType.DMA(())]

    def _plan(self, x_refs, o_refs, sems):
        send, recv, local_sem = sems
        x_ref, o_ref = x_refs[0], o_refs[0]
        r = x_ref.shape[0]
        mx, my, mc = _coords()

        def rows(px, py, pc):
            return o_ref.at[pl.ds(pl.multiple_of((4 * px + 2 * py + pc) * r, 8), r), :]

        peers = [(_flip(mx, k >> 2 & 1), _flip(my, k >> 1 & 1), _flip(mc, k & 1)) for k in range(1, N_DEV)]
        out = [pltpu.make_async_remote_copy(x_ref, rows(mx, my, mc), send.at[k], recv.at[k], device_id=p,
                                            device_id_type=MESH) for k, p in enumerate(peers)]
        arrivals = [pltpu.make_async_remote_copy(x_ref, rows(*p), send.at[k], recv.at[k], device_id=p,
                                                 device_id_type=MESH) for k, p in enumerate(peers)]
        return out, arrivals, pltpu.make_async_copy(x_ref, rows(mx, my, mc), local_sem)

    def start(self, x_refs, o_refs, sems):
        out, _, local = self._plan(x_refs, o_refs, sems)
        local.start()
        for cp in out:
            cp.start()

    def finish(self, x_refs, o_refs, sems):
        out, arrivals, local = self._plan(x_refs, o_refs, sems)
        for cp in arrivals:
            cp.wait_recv()
        for cp in out:
            cp.wait_send()
        local.wait()


class _ModExchange:
    def __init__(self, first, w_ada):
        self.d, cols = w_ada.shape
        part = jax.ShapeDtypeStruct((N_DEV, cols), F32)
        self.g1, self.g2 = _GatherSmall(first), _GatherSmall(part)
        self.inputs = [first, w_ada]
        self.out_shapes = [self.g1.out_shapes[0], jax.ShapeDtypeStruct((N_DEV, self.d), F32), part,
                           self.g2.out_shapes[0]]
        self.scratch = self.g1.scratch + self.g2.scratch + [
            pltpu.VMEM(self.g1.out_shapes[0].shape, F32), pltpu.VMEM(w_ada.shape, F32),
            pltpu.VMEM((N_DEV, self.d), F32), pltpu.VMEM((N_DEV, cols), F32), pltpu.SemaphoreType.DMA(())]

    def start(self, cin, cout, scr):
        self.g1.start(cin[0:1], cout[0:1], scr[0:3])
        pltpu.make_async_copy(cin[1], scr[7], scr[10]).start()

    def mid(self, cin, cout, scr):
        gathered, w_v, silu_v, part_v = scr[6:10]
        self.g1.finish(cin[0:1], cout[0:1], scr[0:3])
        pltpu.sync_copy(cout[0], gathered)
        rows_per = cin[0].shape[0]
        for j in range(N_DEV):
            silu_v[j:j + 1, :] = gathered[j * rows_per:j * rows_per + 1, 0:self.d]
        c_all = silu_v[...]
        silu_v[...] = c_all * _sigmoid(c_all)
        pltpu.sync_copy(silu_v, cout[1])
        pltpu.make_async_copy(cin[1], w_v, scr[10]).wait()
        part_v[...] = _dot_nn(silu_v[...], w_v[...])
        pltpu.sync_copy(part_v, cout[2])
        self.g2.start(cout[2:3], cout[3:4], scr[3:6])

    def finish(self, cin, cout, scr):
        self.g2.finish(cout[2:3], cout[3:4], scr[3:6])


class _GatherWeights:
    def __init__(self, shards):
        n_t = len(shards)
        self.inputs = list(shards)
        self.out_shapes = [jax.ShapeDtypeStruct((N_DEV * x.shape[0], x.shape[1]), x.dtype) for x in shards]
        self.scratch = [pltpu.SemaphoreType.DMA((n_t, 8)), pltpu.SemaphoreType.DMA((n_t, 8)),
                        pltpu.SemaphoreType.DMA((n_t,))]

    def _plan(self, x_refs, o_refs, sems):
        send, recv, local_sem = sems
        mx, my, mc = _coords()
        me, sibling = (mx, my, mc), (mx, my, 1 - mc)
        xn, yn, diag = (1 - mx, my), (mx, 1 - my), (1 - mx, 1 - my)

        def rows(t, chip, core, half=None):
            r = x_refs[t].shape[0]
            base = (4 * chip[0] + 2 * chip[1] + core) * r
            if half is None:
                return o_refs[t].at[pl.ds(pl.multiple_of(base, 8), r), :]
            return o_refs[t].at[pl.ds(pl.multiple_of(base + half * (r // 2), 8), r // 2), :]

        def copy(t, k, block, to, src=None):
            return pltpu.make_async_remote_copy(
                src_ref=block if src is None else src, dst_ref=block,
                send_sem=send.at[t, k], recv_sem=recv.at[t, k], device_id=to, device_id_type=MESH)

        def local(t):
            return pltpu.make_async_copy(x_refs[t], rows(t, (mx, my), mc), local_sem.at[t])

        return (mx, my), mc, me, sibling, xn, yn, diag, rows, copy, local

    def start(self, x_refs, o_refs, sems):
        chip, mc, me, sibling, xn, yn, diag, rows, copy, local = self._plan(x_refs, o_refs, sems)
        for t in range(len(x_refs)):
            mine = rows(t, chip, mc)
            local(t).start()
            copy(t, 0, mine, sibling, src=x_refs[t]).start()
            copy(t, 1, mine, (*xn, mc), src=x_refs[t]).start()
            copy(t, 2, mine, (*yn, mc), src=x_refs[t]).start()

    def mid(self, x_refs, o_refs, sems):
        chip, mc, me, sibling, xn, yn, diag, rows, copy, local = self._plan(x_refs, o_refs, sems)
        for t in range(len(x_refs)):
            copy(t, 1, rows(t, xn, mc), me).wait_recv()
            copy(t, 3, rows(t, xn, mc, 0), (*yn, mc)).start()
            copy(t, 5, rows(t, xn, mc), sibling).start()
        for t in range(len(x_refs)):
            copy(t, 2, rows(t, yn, mc), me).wait_recv()
            copy(t, 4, rows(t, yn, mc, 1), (*xn, mc)).start()
            copy(t, 6, rows(t, yn, mc), sibling).start()

    def finish(self, x_refs, o_refs, sems):
        chip, mc, me, sibling, xn, yn, diag, rows, copy, local = self._plan(x_refs, o_refs, sems)
        for t in range(len(x_refs)):
            copy(t, 3, rows(t, diag, mc, 0), me).wait_recv()
            copy(t, 4, rows(t, diag, mc, 1), me).wait_recv()
            copy(t, 7, rows(t, diag, mc), sibling).start()
        for t in range(len(x_refs)):
            copy(t, 0, rows(t, chip, 1 - mc), me).wait_recv()
            copy(t, 5, rows(t, xn, 1 - mc), me).wait_recv()
            copy(t, 6, rows(t, yn, 1 - mc), me).wait_recv()
            copy(t, 7, rows(t, diag, 1 - mc), me).wait_recv()
            mine = rows(t, chip, mc)
            copy(t, 0, mine, sibling, src=x_refs[t]).wait_send()
            copy(t, 1, mine, (*xn, mc), src=x_refs[t]).wait_send()
            copy(t, 2, mine, (*yn, mc), src=x_refs[t]).wait_send()
            copy(t, 3, rows(t, xn, mc, 0), (*yn, mc)).wait_send()
            copy(t, 4, rows(t, yn, mc, 1), (*xn, mc)).wait_send()
            copy(t, 5, rows(t, xn, mc), sibling).wait_send()
            copy(t, 6, rows(t, yn, mc), sibling).wait_send()
            copy(t, 7, rows(t, diag, mc), sibling).wait_send()
            local(t).wait()


class _SiblingExchange:
    mid = None

    def __init__(self, grads):
        n_t = len(grads)
        self.inputs = list(grads)
        self.out_shapes = [jax.ShapeDtypeStruct((N_CHIP,) + g.shape[2:], F32) for g in grads]
        self.scratch = [pltpu.SemaphoreType.DMA((n_t,)), pltpu.SemaphoreType.DMA((n_t,))]

    def _copies(self, g_refs, land, sems):
        send, recv = sems
        mx, my, mc = _coords()
        return [pltpu.make_async_remote_copy(g_refs[t].at[:, 1 - mc], land[t], send.at[t], recv.at[t],
                                             device_id=(mx, my, 1 - mc), device_id_type=MESH)
                for t in range(len(g_refs))]

    def start(self, g_refs, land, sems):
        for cp in self._copies(g_refs, land, sems):
            cp.start()

    def finish(self, g_refs, land, sems):
        for cp in self._copies(g_refs, land, sems):
            cp.wait()


class _Together:
    def __init__(self, *comms):
        self.comms = comms
        self.inputs = [x for c in comms for x in c.inputs]
        self.out_shapes = [x for c in comms for x in c.out_shapes]
        self.scratch = [x for c in comms for x in c.scratch]
        self.mid = self._mid if any(c.mid is not None for c in comms) else None

    def _each(self, phase, cin, cout, sems):
        i = o = s = 0
        for c in self.comms:
            fn = getattr(c, phase)
            ni, no, ns = len(c.inputs), len(c.out_shapes), len(c.scratch)
            if fn is not None:
                fn(cin[i:i + ni], cout[o:o + no], sems[s:s + ns])
            i, o, s = i + ni, o + no, s + ns

    def start(self, cin, cout, sems):
        self._each("start", cin, cout, sems)

    def _mid(self, cin, cout, sems):
        self._each("mid", cin, cout, sems)

    def finish(self, cin, cout, sems):
        self._each("finish", cin, cout, sems)


def _standalone(comm, name):
    def body():
        pass
    return _call(body, grid=(1,), in_specs=[], out_specs=[], out_shape=[], args=(), name=name, comm=comm)[1]


def _chip_partials(g4s, lands, name):
    n_t = len(g4s)
    in_specs, out_specs, out_shape = [], [], []
    for g4 in g4s:
        _, _, r, c = g4.shape
        in_specs.append(pl.BlockSpec((None, None, r, c), lambda q: (q, lax.axis_index("c"), 0, 0)))
        out_specs.append(pl.BlockSpec((None, r, c), lambda q: (q, 0, 0)))
        out_shape.append(jax.ShapeDtypeStruct((N_CHIP, r, c), BF16))
    in_specs += [pl.BlockSpec((None,) + g4.shape[2:], lambda q: (q, 0, 0)) for g4 in g4s]

    def body(*refs):
        for t in range(n_t):
            refs[2 * n_t + t][...] = (refs[t][...] + refs[n_t + t][...]).astype(BF16)

    return pl.pallas_call(body, grid=(N_CHIP,), in_specs=in_specs, out_specs=out_specs, out_shape=out_shape,
                          compiler_params=_params(1), name=name)(*g4s, *lands)


class _ChipExchange:
    mid = None

    def __init__(self, parts):
        n_t = len(parts)
        self.inputs = list(parts)
        self.out_shapes = [jax.ShapeDtypeStruct(p.shape, p.dtype) for p in parts]
        self.scratch = [pltpu.SemaphoreType.DMA((n_t, 3)), pltpu.SemaphoreType.DMA((n_t, 3)),
                        pltpu.SemaphoreType.DMA((n_t,))]

    def _plan(self, p_refs, land, sems):
        send, recv, local_sem = sems
        mx, my, mc = _coords()
        my_chip = 2 * mx + my
        peers = [(_flip(mx, fx), _flip(my, fy)) for fx, fy in ((1, 0), (0, 1), (1, 1))]

        def out(t, k):
            px, py = peers[k]
            return pltpu.make_async_remote_copy(p_refs[t].at[2 * px + py], land[t].at[my_chip], send.at[t, k],
                                                recv.at[t, k], device_id=(px, py, mc), device_id_type=MESH)

        def arrival(t, k):
            px, py = peers[k]
            return pltpu.make_async_remote_copy(p_refs[t].at[my_chip], land[t].at[2 * px + py], send.at[t, k],
                                                recv.at[t, k], device_id=(px, py, mc), device_id_type=MESH)

        def local(t):
            return pltpu.make_async_copy(p_refs[t].at[my_chip], land[t].at[my_chip], local_sem.at[t])

        return out, arrival, local

    def start(self, p_refs, land, sems):
        out, arrival, local = self._plan(p_refs, land, sems)
        for t in range(len(p_refs)):
            local(t).start()
            for k in range(3):
                out(t, k).start()

    def finish(self, p_refs, land, sems):
        out, arrival, local = self._plan(p_refs, land, sems)
        for t in range(len(p_refs)):
            for k in range(3):
                arrival(t, k).wait_recv()
                out(t, k).wait_send()
            local(t).wait()


def _rope_tables(s, width):
    heads = width // HEAD_DIM
    inv_freq = ROPE_THETA ** (-jnp.arange(0, HEAD_DIM, 2, dtype=F32) / HEAD_DIM)
    inv_full = jnp.tile(inv_freq, 2 * heads)
    sign = jnp.tile(jnp.concatenate([-jnp.ones((HALF_HEAD,), F32), jnp.ones((HALF_HEAD,), F32)]), heads)
    ang = jnp.arange(s, dtype=F32)[:, None] * inv_full[None, :]
    return jnp.cos(ang), jnp.sin(ang) * sign[None, :]


def _pad_rows(v, rows):
    return jnp.concatenate([v, jnp.zeros((rows - 1, v.shape[1]), v.dtype)], axis=0)


def kernel(x, c, w_ada, b_ada, ffn1_norm_g, ffn1_w_gate, ffn1_w_up, ffn1_w_down, mix_norm_g, w_in, conv_dw_w, conv_dw_b, conv_ln_g, conv_ln_b, attn_out_g, conv_out_g, w_out, ffn2_norm_g, ffn2_w_gate, ffn2_w_up, ffn2_w_down, final_norm_g, loss_target, m_w_ada, m_b_ada, m_ffn1_norm_g, m_ffn1_w_gate, m_ffn1_w_up, m_ffn1_w_down, m_mix_norm_g, m_w_in, m_conv_dw_w, m_conv_dw_b, m_conv_ln_g, m_conv_ln_b, m_attn_out_g, m_conv_out_g, m_w_out, m_ffn2_norm_g, m_ffn2_w_gate, m_ffn2_w_up, m_ffn2_w_down, m_final_norm_g, v_w_ada, v_b_ada, v_ffn1_norm_g, v_ffn1_w_gate, v_ffn1_w_up, v_ffn1_w_down, v_mix_norm_g, v_w_in, v_conv_dw_w, v_conv_dw_b, v_conv_ln_g, v_conv_ln_b, v_attn_out_g, v_conv_out_g, v_w_out, v_ffn2_norm_g, v_ffn2_w_gate, v_ffn2_w_up, v_ffn2_w_down, v_final_norm_g):
    mx, my, mc = _coords()
    me = 4 * mx + 2 * my + mc
    s, d = x.shape[1], x.shape[2]
    aw = d // 2
    x2, target = x[0], loss_target[0]
    n_mod = w_ada.shape[2] * N_DEV // d
    mod_cols = w_ada.shape[2]

    def shard(w, transpose):
        return (w[0].T if transpose else w[0]).astype(BF16)

    cw_shard = conv_dw_w.shape[3]
    n_taps = CONV_KERNEL * cw_shard
    first_len = -(-(d + n_taps) // LANES) * LANES
    first = jnp.concatenate([c, conv_dw_w[0, :, 0, :].reshape(1, n_taps), jnp.zeros((1, first_len - d - n_taps), F32)], axis=1)
    first_all, silu_c, _, mod_all, wg1, wu1 = _standalone(
        _Together(_ModExchange(_pad_rows(first, 8), w_ada[0]),
                  _GatherWeights([shard(ffn1_w_gate, True), shard(ffn1_w_up, True)])), "ag_first")
    first_all = first_all[0::8]
    conv_w = first_all[:, d:d + n_taps].reshape(N_DEV, CONV_KERNEL, cw_shard).transpose(1, 0, 2).reshape(CONV_KERNEL, aw)

    mod_all = mod_all.reshape(N_DEV, N_DEV, mod_cols)
    mod = lax.dynamic_index_in_dim(mod_all, me, axis=1, keepdims=False).reshape(1, n_mod * d) + b_ada
    sh1, sc1, g1, sh2, sc2, g2, sh3, sc3, g3 = [mod[:, i * d:(i + 1) * d] for i in range(n_mod)]

    def split(g):
        return g.reshape(N_CHIP, 2, g.shape[0] // N_DEV, g.shape[1])

    def partials(g4s, lands, tag):
        return _chip_partials(g4s, lands, "chip_partials_" + tag)

    (n1, silu1, gs1, hid1), (wd1, win_t) = _norm_ffn_up(
        x2, ffn1_norm_g, sc1, sh1, wg1, wu1, "ffn1_up",
        comm=_GatherWeights([shard(ffn1_w_down, False), shard(w_in, True)]))
    h1, f1, n2 = _residual_mm(hid1, wd1, x2, g1, 0.5, "ffn1_down", norm=(mix_norm_g, sc2, sh2))
    cos, sin_signed = _rope_tables(s, LANES)
    (proj,), (wout,) = _proj_rope(n2, win_t, cos, sin_signed, aw, "proj", comm=_GatherWeights([shard(w_out, False)]))
    lanes_per = aw // LANES
    (attn, lse), (wg2, wu2, wd2) = _attn_seq_fwd(
        proj, aw, "attn_fwd",
        comm=_GatherWeights([shard(ffn2_w_gate, True), shard(ffn2_w_up, True), shard(ffn2_w_down, False)]))
    u1, = _conv_fwd(proj, 3 * lanes_per, 4 * lanes_per, conv_w, conv_dw_b, "conv_fwd")
    post = (attn_out_g, conv_ln_g, conv_ln_b, conv_out_g)
    y, h2, mix, n3 = _mix_out(attn, u1, post, wout, h1, g2, (ffn2_norm_g, sc3, sh3), "mix_out")
    silu3, gs3, hid3 = _ffn_up(n3, wg2, wu2, "ffn2_up")

    dh3, df3, err2, d_final_g, dg3 = _last_mm_loss(hid3, wd2, h2, g3, 0.5, target, final_norm_g.reshape(1, d),
                                                   "ffn2_down_loss")
    loss_part = jnp.zeros((1, LANES), F32).at[0, 0].set(0.5 * jnp.sum(err2) / d)

    da3, db3 = _ffn_bwd_hidden(df3, wd2, silu3, gs3, "ffn2_hidden_bwd")
    g4_a = [split(_mm_tn(da3, n3, "ffn2_dwg")), split(_mm_tn(db3, n3, "ffn2_dwu")), split(_mm_tn(hid3, df3, "ffn2_dwd"))]
    (dh2, dmix, dsh3, dsc3, dgn3, dg2), land_a = _mm_norm_mod_bwd(
        [(da3, wg2), (db3, wu2)], h2, dh3, ffn2_norm_g, sc3, (mix, g2, 1.0), "ffn2_dn_norm3_bwd", tm=256,
        comm=_SiblingExchange(g4_a))
    parts_a = partials(g4_a, land_a, "a")
    g_wout = _mm_tn(y, dmix, "mix_dwout")
    dattn, du1, d_gains, d_ln = _mix_dy_post_bwd(dmix, wout, attn, u1, post, "mix_dy_post_bwd")
    d_attn_g, d_conv_g, d_ln_g, d_ln_b = d_gains[:, :aw], d_gains[:, aw:], d_ln[:, :aw], d_ln[:, aw:]
    dga, dgb, d_taps, d_conv_b = _conv_bwd(proj, 3 * lanes_per, 4 * lanes_per, conv_w, du1, "conv_bwd")
    (dq, dk, dv), sums_a = _attn_seq_bwd(proj, dattn, attn, lse, cos, sin_signed, "attn_bwd",
                                         comm=_ChipExchange(parts_a))
    dproj = jnp.concatenate([dq, dk, dv, dga, dgb], axis=1)
    g4_b = [split(g_wout), split(_mm_tn(dproj, n2, "mix_dwin"))]
    (dh1, df1, dsh2, dsc2, dgn2, dg1), land_b = _mm_norm_mod_bwd(
        [(dproj, win_t)], h1, dh2, mix_norm_g, sc2, (f1, g1, 0.5), "mix_dn_norm2_bwd", tm=512,
        comm=_SiblingExchange(g4_b))
    parts_b = partials(g4_b, land_b, "b")
    g4_c = [split(_mm_tn(hid1, df1, "ffn1_dwd"))]
    (da1, db1), both = _ffn_bwd_hidden(df1, wd1, silu1, gs1, "ffn1_hidden_bwd",
                                       comm=_Together(_ChipExchange(parts_b), _SiblingExchange(g4_c)))
    sums_b, land_c = both[:2], both[2:]
    parts_c = partials(g4_c, land_c, "c")
    g_wu1, sums_c = _mm_tn(db1, n1, "ffn1_dwu", comm=_ChipExchange(parts_c))
    g4_d = [split(g_wu1)]
    g_wg1, land_d = _mm_tn(da1, n1, "ffn1_dwg", comm=_SiblingExchange(g4_d))
    parts_d = partials(g4_d, land_d, "d")
    g4_e = [split(g_wg1)]
    dn1, both = _plain_mm([(da1, wg1), (db1, wu1)], BF16, False, d, "ffn1_dn",
                          comm=_Together(_ChipExchange(parts_d), _SiblingExchange(g4_e)))
    sums_d, land_e = both[:1], both[1:]
    parts_e = partials(g4_e, land_e, "e")
    (dx, dsh1, dsc1, dgn1), sums_e = _norm_mod_bwd(dn1, x2, dh1, ffn1_norm_g, sc1, "norm1_bwd",
                                                   comm=_ChipExchange(parts_e))

    dmod = jnp.concatenate([dsh1, dsc1, dg1, dsh2, dsc2, dg2, dsh3, dsc3, dg3], axis=1)
    small = [dmod, dgn1, dgn2, dgn3, d_final_g, d_conv_b, d_ln_g, d_ln_b, d_attn_g, d_conv_g,
             d_taps.reshape(1, CONV_KERNEL * aw), loss_part]
    sizes = [v.shape[1] for v in small]
    total = sum(sizes)
    padded = -(-total // (8 * LANES)) * (8 * LANES)
    packed = jnp.concatenate(small + [jnp.zeros((1, padded - total), F32)], axis=1).reshape(8, padded // 8)
    gathered = _ag_small(packed, "ag_small_grads")
    summed = _sum_blocks(gathered, N_DEV, "sum_small_grads").reshape(1, padded)
    offs = [sum(sizes[:i]) for i in range(len(sizes))]
    (g_b_ada, g_gn1, g_gn2, g_gn3, g_final, g_conv_b, g_ln_g, g_ln_b, g_attn_g, g_conv_g, g_taps, loss_row) = [
        summed[:, o:o + n] for o, n in zip(offs, sizes)]
    loss = loss_row[0, 0]
    g_taps_shard = lax.dynamic_slice_in_dim(g_taps.reshape(CONV_KERNEL, aw), me * cw_shard, cw_shard, axis=1)
    dmod_all = gathered.reshape(N_DEV, padded)[:, :n_mod * d]
    dmod_cols = lax.dynamic_slice_in_dim(dmod_all, me * mod_cols, mod_cols, axis=1)
    g_w_ada = _mm_tn(silu_c, dmod_cols, "ada_dw")

    arrived = dict(zip(["ffn2_w_gate", "ffn2_w_up", "ffn2_w_down", "w_out", "w_in", "ffn1_w_down", "ffn1_w_up",
                        "ffn1_w_gate"], list(sums_a) + list(sums_b) + list(sums_c) + list(sums_d) + list(sums_e)))
    transposed = ("ffn1_w_gate", "ffn1_w_up", "w_in", "ffn2_w_gate", "ffn2_w_up")
    grads = {
        "w_ada": g_w_ada, "b_ada": g_b_ada, "ffn1_norm_g": g_gn1, "mix_norm_g": g_gn2, "conv_dw_w": g_taps_shard,
        "conv_dw_b": g_conv_b, "conv_ln_g": g_ln_g, "conv_ln_b": g_ln_b, "attn_out_g": g_attn_g,
        "conv_out_g": g_conv_g, "ffn2_norm_g": g_gn3, "final_norm_g": g_final,
    }
    weights = dict(w_ada=w_ada, b_ada=b_ada, ffn1_norm_g=ffn1_norm_g, ffn1_w_gate=ffn1_w_gate, ffn1_w_up=ffn1_w_up, ffn1_w_down=ffn1_w_down, mix_norm_g=mix_norm_g, w_in=w_in, conv_dw_w=conv_dw_w, conv_dw_b=conv_dw_b, conv_ln_g=conv_ln_g, conv_ln_b=conv_ln_b, attn_out_g=attn_out_g, conv_out_g=conv_out_g, w_out=w_out, ffn2_norm_g=ffn2_norm_g, ffn2_w_gate=ffn2_w_gate, ffn2_w_up=ffn2_w_up, ffn2_w_down=ffn2_w_down, final_norm_g=final_norm_g)
    moms = dict(w_ada=m_w_ada, b_ada=m_b_ada, ffn1_norm_g=m_ffn1_norm_g, ffn1_w_gate=m_ffn1_w_gate, ffn1_w_up=m_ffn1_w_up, ffn1_w_down=m_ffn1_w_down, mix_norm_g=m_mix_norm_g, w_in=m_w_in, conv_dw_w=m_conv_dw_w, conv_dw_b=m_conv_dw_b, conv_ln_g=m_conv_ln_g, conv_ln_b=m_conv_ln_b, attn_out_g=m_attn_out_g, conv_out_g=m_conv_out_g, w_out=m_w_out, ffn2_norm_g=m_ffn2_norm_g, ffn2_w_gate=m_ffn2_w_gate, ffn2_w_up=m_ffn2_w_up, ffn2_w_down=m_ffn2_w_down, final_norm_g=m_final_norm_g)
    vars_ = dict(w_ada=v_w_ada, b_ada=v_b_ada, ffn1_norm_g=v_ffn1_norm_g, ffn1_w_gate=v_ffn1_w_gate, ffn1_w_up=v_ffn1_w_up, ffn1_w_down=v_ffn1_w_down, mix_norm_g=v_mix_norm_g, w_in=v_w_in, conv_dw_w=v_conv_dw_w, conv_dw_b=v_conv_dw_b, conv_ln_g=v_conv_ln_g, conv_ln_b=v_conv_ln_b, attn_out_g=v_attn_out_g, conv_out_g=v_conv_out_g, w_out=v_w_out, ffn2_norm_g=v_ffn2_norm_g, ffn2_w_gate=v_ffn2_w_gate, ffn2_w_up=v_ffn2_w_up, ffn2_w_down=v_ffn2_w_down, final_norm_g=v_final_norm_g)
    names = list(weights)
    big = ["w_ada", "ffn1_w_gate", "ffn1_w_up", "ffn1_w_down", "w_in", "w_out", "ffn2_w_gate", "ffn2_w_up",
           "ffn2_w_down"]
    shape2 = {n: (weights[n].shape[-2] if weights[n].ndim > 1 else 1, weights[n].shape[-1]) for n in names}
    shape2["conv_dw_w"] = (CONV_KERNEL, cw_shard)
    g_out, d_out, m_out, v_out = {}, {}, {}, {}
    for n in big:
        if n in arrived:
            def view(t, n=n):
                return t[0].T if n in transposed else t[0]
            res = _adamw_reduced(view(weights[n]), arrived[n], view(moms[n]), view(vars_[n]), "adamw_" + n)
            g_out[n], d_out[n], m_out[n], v_out[n] = [r.T if n in transposed else r for r in res]
        else:
            g2d = grads[n].reshape(shape2[n])
            res = _adamw_big(weights[n].reshape(shape2[n]), g2d, moms[n].reshape(shape2[n]),
                             vars_[n].reshape(shape2[n]), "adamw_" + n)
            g_out[n], (d_out[n], m_out[n], v_out[n]) = g2d, res
    rest = [n for n in names if n not in big]
    res = _adamw_small([weights[n].reshape(shape2[n]) for n in rest], [grads[n].reshape(shape2[n]) for n in rest],
                       [moms[n].reshape(shape2[n]) for n in rest], [vars_[n].reshape(shape2[n]) for n in rest],
                       "adamw_small")
    for i, n in enumerate(rest):
        g_out[n], d_out[n], m_out[n], v_out[n] = grads[n], res[0][i], res[1][i], res[2][i]

    def shaped(table):
        return [table[n].reshape(weights[n].shape) for n in names]

    return (loss, dx.reshape(x.shape), *shaped(g_out), *shaped(d_out), *shaped(m_out), *shaped(v_out))
```

```python
import functools

import jax
import jax.numpy as jnp
from jax import lax
from jax.experimental import pallas as pl
from jax.experimental.pallas import tpu as pltpu

F32 = jnp.float32
BF16 = jnp.bfloat16
MESH = pl.DeviceIdType.MESH
ANY = pl.BlockSpec(memory_space=pl.ANY)

N_DEV = 8
N_CHIP = 4
HEAD_DIM = 64
HALF_HEAD = HEAD_DIM // 2
LANES = 128
BLOCK = 128
DILATIONS = (1, 4, 16)
MERGE_CHUNK = 512
ROPE_THETA = 10000.0
CONV_KERNEL = 31
CONV_HALO = 32
CONV_CHUNK = 512
CONV_SUB = 128
RMS_EPS = 1e-6
LN_EPS = 1e-5
ADAM_LR = 0.001
ADAM_B1 = 0.9
ADAM_B2 = 0.999
ADAM_EPS = 1e-08
ADAM_WD = 0.01
ADAM_STEP = 10
VMEM_LIMIT = 56 * 1024 * 1024
NEG = -1e30


def _params(n_axes):
    return pltpu.CompilerParams(dimension_semantics=("arbitrary",) * n_axes, vmem_limit_bytes=VMEM_LIMIT)


def _tile(n, target, unit):
    best = None
    for t in range(unit, min(n, target) + 1, unit):
        if n % t == 0:
            best = t
    return best if best is not None else n


def _sigmoid(x):
    return 0.5 * (jnp.tanh(0.5 * x) + 1.0)


def _call(body, *, grid, in_specs, out_specs, out_shape, args, name, scratch_shapes=(), comm=None):
    params = _params(len(grid))
    if comm is None:
        return pl.pallas_call(body, grid=grid, in_specs=list(in_specs), out_specs=list(out_specs),
                              out_shape=list(out_shape), scratch_shapes=list(scratch_shapes),
                              compiler_params=params, name=name)(*args)
    n_in, n_out, n_scr = len(args), len(out_shape), len(scratch_shapes)
    c_in, c_out = len(comm.inputs), len(comm.out_shapes)
    steps = 1
    for g in grid:
        steps *= g

    def hosted(*refs):
        pos = 0
        parts = []
        for size in (n_in, c_in, n_out, c_out, n_scr, len(comm.scratch)):
            parts.append(refs[pos:pos + size])
            pos += size
        ins, cin, outs, cout, scr, cscr = parts
        step = 0
        for axis, g in enumerate(grid):
            step = step * g + pl.program_id(axis)

        @pl.when(step == 0)
        def _():
            comm.start(cin, cout, cscr)

        body(*ins, *outs, *scr)
        if comm.mid is not None and steps >= 4:
            @pl.when(step == steps // 2)
            def _():
                comm.mid(cin, cout, cscr)

        @pl.when(step == steps - 1)
        def _():
            if comm.mid is not None and steps < 4:
                comm.mid(cin, cout, cscr)
            comm.finish(cin, cout, cscr)

    res = pl.pallas_call(
        hosted, grid=grid, in_specs=list(in_specs) + [ANY] * c_in, out_specs=list(out_specs) + [ANY] * c_out,
        out_shape=list(out_shape) + list(comm.out_shapes), scratch_shapes=list(scratch_shapes) + list(comm.scratch),
        compiler_params=params, name=name)(*args, *comm.inputs)
    return res[:n_out], res[n_out:]


def _rows(fn, rows_in, vecs_in, rows_out, vecs_out, *, tile, name, comm=None):
    norm = [r if isinstance(r, tuple) else (r, r.shape[1], 0) for r in rows_in]
    n_rows = norm[0][0].shape[0]
    n_tiles = n_rows // tile
    in_specs, args = [], []
    for arr, width, cb in norm:
        in_specs.append(pl.BlockSpec((tile, width), functools.partial(lambda i, cb: (i, cb), cb=cb)))
        args.append(arr)
    for v in vecs_in:
        in_specs.append(pl.BlockSpec((1, v.shape[1]), lambda i: (0, 0)))
        args.append(v)
    out_shape = [jax.ShapeDtypeStruct((n_rows, w), dt) for w, dt in rows_out]
    out_shape += [jax.ShapeDtypeStruct((1, w), F32) for w in vecs_out]
    out_specs = [pl.BlockSpec((tile, w), lambda i: (i, 0)) for w, _ in rows_out]
    out_specs += [pl.BlockSpec((1, w), lambda i: (0, 0)) for w in vecs_out]
    n_in, n_ro = len(args), len(rows_out)

    def body(*refs):
        vals = [r[...] for r in refs[:n_in]]
        outs = refs[n_in:]
        row_vals, vec_vals = fn(*vals)
        for ref, val in zip(outs[:n_ro], row_vals):
            if isinstance(val, tuple):
                w = val[0].shape[1]
                for j, piece in enumerate(val):
                    ref[:, j * w:(j + 1) * w] = piece.astype(ref.dtype)
            else:
                ref[...] = val.astype(ref.dtype)
        if vecs_out:
            @pl.when(pl.program_id(0) == 0)
            def _():
                for ref in outs[n_ro:]:
                    ref[...] = jnp.zeros_like(ref)
            for ref, val in zip(outs[n_ro:], vec_vals):
                ref[...] += val

    return _call(body, grid=(n_tiles,), in_specs=in_specs, out_specs=out_specs, out_shape=out_shape, args=args,
                 name=name, comm=comm)


def _colsum(x):
    return jnp.sum(x, axis=0, keepdims=True)


def _rms_stats(h):
    r = lax.rsqrt(jnp.mean(h * h, axis=-1, keepdims=True) + RMS_EPS)
    return r, h * r


def _rms_back(r, xn, dxn):
    return r * (dxn - xn * jnp.mean(dxn * xn, axis=-1, keepdims=True))


def _branch_back(dh, f, gate, coef):
    return (coef * gate) * dh, coef * _colsum(f.astype(F32) * dh)


def _norm_mod_back(dn, h, dh_in, gain, scale):
    dn = dn.astype(F32)
    r, xn = _rms_stats(h)
    y = xn * gain
    dy = dn * (1.0 + scale)
    dh = dh_in + _rms_back(r, xn, dy * gain)
    return dh, [_colsum(dn), _colsum(dn * y), _colsum(dy * xn)]


def _norm_mod_bwd(dn, h, dh_in, gain, scale, name, comm=None):
    d = h.shape[1]

    def fn(dn, h, dh_in, gain, scale):
        dh, vecs = _norm_mod_back(dn, h, dh_in, gain, scale)
        return [dh], vecs
    return _rows(fn, [dn, h, dh_in], [gain, scale], [(d, F32)], [d, d, d], tile=256, name=name, comm=comm)


def _mm_norm_mod_bwd(pairs, h, dh_in, gain, scale, branch, name, tm, comm=None):
    f, gate, coef = branch

    def epi(accs, ex, vc):
        dh, vecs = _norm_mod_back(accs[0], ex[0], ex[1], vc[0], vc[1])
        df, dgate = _branch_back(dh, ex[2], vc[2], coef)
        return [dh, df] + vecs + [dgate]
    return _mm([pairs], epi, [h, dh_in, f], [gain, scale, gate], [F32, BF16], trans_rhs=False, tm=tm,
               tn=h.shape[1], name=name, n_sums=4, comm=comm)


def _last_mm_loss(lhs, w, res, gate, coef, target, gain, name):
    d = w.shape[1]

    def epi(accs, ex, vc):
        f = accs[0]
        h = ex[0] + (coef * vc[0]) * f
        r, xn = _rms_stats(h)
        err = xn * vc[1] - ex[1]
        dout = err * (1.0 / d)
        dh = _rms_back(r, xn, dout * vc[1])
        df, dgate = _branch_back(dh, f, vc[0], coef)
        return [dh, df, _colsum(err * err), _colsum(dout * xn), dgate]
    return _mm([[(lhs, w)]], epi, [res, target], [gate, gain], [F32, BF16], trans_rhs=False, tm=256, tn=d,
               name=name, n_sums=3)


def _partner(x):
    if x.shape[1] > LANES:
        return jnp.concatenate([_partner(x[:, c:c + LANES]) for c in range(0, x.shape[1], LANES)], axis=1)
    lane = lax.broadcasted_iota(jnp.int32, x.shape, 1) % HEAD_DIM
    return jnp.where(lane < HALF_HEAD, pltpu.roll(x, LANES - HALF_HEAD, 1), pltpu.roll(x, HALF_HEAD, 1))


def _proj_rope(n, w_t, cos, sin_signed, width, name, comm=None):
    s, kdim = n.shape
    n_cols = w_t.shape[0]
    tm = _tile(s, 1024, 8)
    qscale = HEAD_DIM ** -0.5

    chunk = _tile(tm, 256, 8)

    def body(n_ref, w_ref, cos_ref, sin_ref, o_ref):
        j = pl.program_id(0)

        def products(rows):
            return lax.dot_general(n_ref[rows, :].astype(BF16), w_ref[...].astype(BF16), (((1,), (1,)), ((), ())),
                                   preferred_element_type=F32)

        @pl.when(j >= 2)
        def _():
            for c in range(tm // chunk):
                rows = slice(c * chunk, (c + 1) * chunk)
                o_ref[rows, :] = products(rows)

        @pl.when(j < 2)
        def _():
            scale = jnp.where(j == 0, qscale, 1.0)
            for c in range(tm // chunk):
                rows = slice(c * chunk, (c + 1) * chunk)
                acc = products(rows)
                cos = jnp.tile(cos_ref[rows, :], (1, width // LANES))
                sin = jnp.tile(sin_ref[rows, :], (1, width // LANES))
                o_ref[rows, :] = scale * (acc * cos + _partner(acc) * sin)

    table = pl.BlockSpec((tm, LANES), lambda j, i: (jnp.where(j < 2, i, 0), 0))
    return _call(
        body, grid=(n_cols // width, s // tm),
        in_specs=[pl.BlockSpec((tm, kdim), lambda j, i: (i, 0)), pl.BlockSpec((width, kdim), lambda j, i: (j, 0)),
                  table, table],
        out_specs=[pl.BlockSpec((tm, width), lambda j, i: (i, j))],
        out_shape=[jax.ShapeDtypeStruct((s, n_cols), F32)], args=(n, w_t, cos, sin_signed), name=name, comm=comm)


def _mix_post(attn, u1, attn_g, ln_g, ln_b, conv_g):
    _, xa = _rms_stats(attn)
    mu = jnp.mean(u1, axis=-1, keepdims=True)
    xc = u1 - mu
    rstd = lax.rsqrt(jnp.mean(xc * xc, axis=-1, keepdims=True) + LN_EPS)
    u2 = (xc * rstd) * ln_g + ln_b
    u3 = u2 * _sigmoid(u2)
    _, x3 = _rms_stats(u3)
    return jnp.concatenate([xa * attn_g, x3 * conv_g], axis=1)


def _mix_post_back(dy, attn, u1, attn_g, ln_g, ln_b, conv_g):
    w = attn.shape[1]
    dya, dyc = dy[:, :w], dy[:, w:]
    ra, xa = _rms_stats(attn)
    dattn = _rms_back(ra, xa, dya * attn_g)
    mu = jnp.mean(u1, axis=-1, keepdims=True)
    xc = u1 - mu
    rstd = lax.rsqrt(jnp.mean(xc * xc, axis=-1, keepdims=True) + LN_EPS)
    xh = xc * rstd
    u2 = xh * ln_g + ln_b
    sig = _sigmoid(u2)
    u3 = u2 * sig
    r3, x3 = _rms_stats(u3)
    du3 = _rms_back(r3, x3, dyc * conv_g)
    du2 = du3 * (sig + u3 * (1.0 - sig))
    dxh = du2 * ln_g
    du1 = rstd * (dxh - jnp.mean(dxh, axis=-1, keepdims=True) - xh * jnp.mean(dxh * xh, axis=-1, keepdims=True))
    return dattn, du1, [_colsum(dya * xa), _colsum(dyc * x3), _colsum(du2 * xh), _colsum(du2)]


def _mm(groups, epi, extras, vecs, outs, *, trans_rhs, tm, tn, name, n_sums=0, pre=None, pre_inputs=(),
        comm=None):
    m = (pre_inputs[0] if pre is not None else groups[0][0][0]).shape[0]
    n = groups[0][0][1].shape[0] if trans_rhs else groups[0][0][1].shape[1]
    tm, tn = min(tm, m), min(tn, n)
    in_specs, args, uses_pre = [], [], []
    for grp in groups:
        for lhs, rhs in grp:
            k = rhs.shape[1] if trans_rhs else rhs.shape[0]
            uses_pre.append(lhs is None)
            if lhs is not None:
                in_specs.append(pl.BlockSpec((tm, k), lambda j, i: (i, 0)))
                args.append(lhs)
            in_specs.append(pl.BlockSpec((tn, k), lambda j, i: (j, 0)) if trans_rhs
                            else pl.BlockSpec((k, tn), lambda j, i: (0, j)))
            args.append(rhs)
    n_mm = len(args)
    for p in pre_inputs:
        in_specs.append(pl.BlockSpec((tm, p.shape[1]), lambda j, i: (i, 0)))
        args.append(p)
    for e in extras:
        in_specs.append(pl.BlockSpec((tm, tn), lambda j, i: (i, j)) if e.shape[1] == n
                        else pl.BlockSpec((tm, e.shape[1]), lambda j, i: (i, 0)))
        args.append(e)
    for v in vecs:
        in_specs.append(pl.BlockSpec((1, tn), lambda j, i: (0, j)) if v.shape[1] == n
                        else pl.BlockSpec((1, v.shape[1]), lambda j, i: (0, 0)))
        args.append(v)
    sizes = [len(g) for g in groups]
    n_pre, n_ex, n_vec = len(pre_inputs), len(extras), len(vecs)
    dims = (((1,), (1,)), ((), ())) if trans_rhs else (((1,), (0,)), ((), ()))
    out_specs, out_shape = [], []
    if pre is not None:
        k_pre = args[n_mm - 1].shape[1] if trans_rhs else args[n_mm - 1].shape[0]
        out_specs.append(pl.BlockSpec((tm, k_pre), lambda j, i: (i, 0)))
        out_shape.append(jax.ShapeDtypeStruct((m, k_pre), BF16))
    for o in outs:
        dt, width = o if isinstance(o, tuple) else (o, n)
        out_specs.append(pl.BlockSpec((tm, tn), lambda j, i: (i, j)) if width == n
                         else pl.BlockSpec((tm, width), lambda j, i: (i, 0)))
        out_shape.append(jax.ShapeDtypeStruct((m, width), dt))
    n_tiles_out = len(out_specs)
    out_specs += [pl.BlockSpec((1, tn), lambda j, i: (0, j))] * n_sums
    out_shape += [jax.ShapeDtypeStruct((1, n), F32)] * n_sums

    def body(*refs):
        ins = refs[:n_mm + n_pre + n_ex + n_vec]
        out_refs = refs[n_mm + n_pre + n_ex + n_vec:]
        vc = [r[...] for r in ins[n_mm + n_pre + n_ex:]]
        vals = []
        made = None
        if pre is not None:
            made = pre([r[...] for r in ins[n_mm:n_mm + n_pre]], vc).astype(BF16)
            vals.append(made)
        accs, pos, pair = [], 0, 0
        for size in sizes:
            acc = None
            for _ in range(size):
                if uses_pre[pair]:
                    lhs_tile = made
                else:
                    lhs_tile = ins[pos][...].astype(BF16)
                    pos += 1
                part = lax.dot_general(lhs_tile, ins[pos][...].astype(BF16), dims, preferred_element_type=F32)
                acc = part if acc is None else acc + part
                pos += 1
                pair += 1
            accs.append(acc)
        ex = [r[...] for r in ins[n_mm + n_pre:n_mm + n_pre + n_ex]]
        vals += epi(accs, ex, vc)
        for ref, val in zip(out_refs[:n_tiles_out], vals):
            ref[...] = val.astype(ref.dtype)
        if n_sums:
            @pl.when(pl.program_id(1) == 0)
            def _():
                for ref in out_refs[n_tiles_out:]:
                    ref[...] = jnp.zeros_like(ref)
            for ref, val in zip(out_refs[n_tiles_out:], vals[n_tiles_out:]):
                ref[...] += val

    return _call(body, grid=(n // tn, m // tm), in_specs=in_specs, out_specs=out_specs, out_shape=out_shape,
                 args=args, name=name, comm=comm)


def _mm_tn(lhs, rhs, name, comm=None):
    t, a = lhs.shape
    b = rhs.shape[1]
    ta = a if a <= 1536 else _tile(a, 1536, LANES)
    tk = _tile(t, 2048, 8)

    def body(l_ref, r_ref, o_ref):
        @pl.when(pl.program_id(1) == 0)
        def _():
            o_ref[...] = jnp.zeros_like(o_ref)
        o_ref[...] += lax.dot_general(l_ref[...].astype(BF16), r_ref[...].astype(BF16), (((0,), (0,)), ((), ())),
                                      preferred_element_type=F32)

    res = _call(body, grid=(a // ta, t // tk),
                in_specs=[pl.BlockSpec((tk, ta), lambda i, k: (k, i)), pl.BlockSpec((tk, b), lambda i, k: (k, 0))],
                out_specs=[pl.BlockSpec((ta, b), lambda i, k: (i, 0))], out_shape=[jax.ShapeDtypeStruct((a, b), F32)],
                args=(lhs, rhs), name=name, comm=comm)
    return res[0] if comm is None else (res[0][0], res[1])


def _ffn_tn(f):
    return _tile(f, 1536, LANES)


def _swiglu_parts(a, b):
    sig = _sigmoid(a)
    silu = a * sig
    return [silu, b * (sig + silu * (1.0 - sig)), silu * b]


def _ffn_up(n, wg_t, wu_t, name, comm=None):
    def epi(accs, ex, vc):
        return _swiglu_parts(accs[0], accs[1])
    return _mm([[(n, wg_t)], [(n, wu_t)]], epi, [], [], [BF16, BF16, BF16], trans_rhs=True, tm=512,
               tn=_ffn_tn(wg_t.shape[0]), name=name, comm=comm)


def _norm_ffn_up(h, gain, scale, shift, wg_t, wu_t, name, comm=None):
    def pre(tiles, vc):
        _, xn = _rms_stats(tiles[0])
        return (xn * vc[0]) * (1.0 + vc[1]) + vc[2]

    def epi(accs, ex, vc):
        return _swiglu_parts(accs[0], accs[1])
    return _mm([[(None, wg_t)], [(None, wu_t)]], epi, [], [gain, scale, shift], [BF16, BF16, BF16], trans_rhs=True,
               tm=256, tn=wg_t.shape[0], name=name, pre=pre, pre_inputs=[h], comm=comm)


def _mix_out(attn, u1, post, w, res, gate, norm, name):
    def pre(tiles, vc):
        return _mix_post(tiles[0], tiles[1], *vc[4:8])

    def epi(accs, ex, vc):
        h = ex[0] + vc[0] * accs[0]
        _, xn = _rms_stats(h)
        return [h, accs[0], (xn * vc[1]) * (1.0 + vc[2]) + vc[3]]
    return _mm([[(None, w)]], epi, [res], [gate] + list(norm) + list(post), [F32, BF16, BF16], trans_rhs=False,
               tm=512, tn=w.shape[1], name=name, pre=pre, pre_inputs=[attn, u1])


def _mix_dy_post_bwd(dmix, w, attn, u1, post, name):
    width = attn.shape[1]

    def epi(accs, ex, vc):
        dattn, du1, sums = _mix_post_back(accs[0], ex[0], ex[1], *vc)
        return [dattn, du1, jnp.concatenate(sums[0:2], axis=1), jnp.concatenate(sums[2:4], axis=1)]
    return _mm([[(dmix, w)]], epi, [attn, u1], list(post), [(F32, width), (F32, width)], trans_rhs=True, tm=256,
               tn=w.shape[0], name=name, n_sums=2)


def _residual_mm(lhs, w, res, gate, coef, name, norm=None, comm=None):
    def epi(accs, ex, vc):
        h = ex[0] + (coef * vc[0]) * accs[0]
        if norm is None:
            return [h, accs[0]]
        _, xn = _rms_stats(h)
        return [h, accs[0], (xn * vc[1]) * (1.0 + vc[2]) + vc[3]]
    vecs = [gate] + (list(norm) if norm is not None else [])
    outs = [F32, BF16] + ([BF16] if norm is not None else [])
    return _mm([[(lhs, w)]], epi, [res], vecs, outs, trans_rhs=False, tm=512, tn=w.shape[1], name=name, comm=comm)


def _ffn_bwd_hidden(df, wd, dhid_db, dhid_da, name, comm=None):
    def epi(accs, ex, vc):
        return [accs[0] * ex[1].astype(F32), accs[0] * ex[0].astype(F32)]
    return _mm([[(df, wd)]], epi, [dhid_db, dhid_da], [], [BF16, BF16], trans_rhs=True, tm=512,
               tn=_ffn_tn(wd.shape[0]), name=name, comm=comm)


def _plain_mm(pairs, out_dtype, trans_rhs, tn, name, tm=512, comm=None):
    def epi(accs, ex, vc):
        return [accs[0]]
    res = _mm([pairs], epi, [], [], [out_dtype], trans_rhs=trans_rhs, tm=tm, tn=tn, name=name, comm=comm)
    return res[0] if comm is None else (res[0][0], res[1])


HEADS_PER_TILE = LANES // HEAD_DIM


def _stack_heads(x):
    lane = lax.broadcasted_iota(jnp.int32, (1, LANES), 1)
    return jnp.concatenate([x * (lane // HEAD_DIM == h).astype(F32) for h in range(HEADS_PER_TILE)], axis=0)


def _unstack_heads(y):
    r = y.shape[0] // HEADS_PER_TILE
    lane = lax.broadcasted_iota(jnp.int32, (r, y.shape[1]), 1)
    out = y[0:r]
    for h in range(1, HEADS_PER_TILE):
        out = jnp.where(lane // HEAD_DIM == h, y[h * r:(h + 1) * r], out)
    return out


def _stacked_lse(lb):
    return jnp.concatenate([_lane_pick(lb, h) for h in range(HEADS_PER_TILE)], axis=0)


def _band_masks(n_row_blocks, n_col_blocks):
    shape = (n_row_blocks * BLOCK, n_col_blocks * BLOCK)
    qi = lax.broadcasted_iota(jnp.int32, shape, 0) % BLOCK
    kj = lax.broadcasted_iota(jnp.int32, shape, 1) % BLOCK
    return kj <= qi, kj >= qi


def _query_masks():
    first_valid, _ = _band_masks(HEADS_PER_TILE, 1)
    same_ok, before_ok = _band_masks(HEADS_PER_TILE, 2)
    is_cur = lax.broadcasted_iota(jnp.int32, same_ok.shape, 1) >= BLOCK
    return first_valid, jnp.logical_and(is_cur, same_ok), jnp.logical_and(jnp.logical_not(is_cur), before_ok)


def _dot_nt(a, b):
    return lax.dot_general(a.astype(BF16), b.astype(BF16), (((1,), (1,)), ((), ())), preferred_element_type=F32)


def _dot_nn(a, b):
    return lax.dot_general(a.astype(BF16), b.astype(BF16), (((1,), (0,)), ((), ())), preferred_element_type=F32)


def _dot_tn(a, b):
    return lax.dot_general(a.astype(BF16), b.astype(BF16), (((0,), (0,)), ((), ())), preferred_element_type=F32)


def _lane_pick(x, h):
    lane = lax.broadcasted_iota(jnp.int32, x.shape, 1)
    return jnp.sum(jnp.where(lane == h * HEAD_DIM, x, 0.0), axis=1, keepdims=True)


def _block_rows(idx, d):
    span = BLOCK * d
    q0 = (idx // d) * span + idx % d
    return pl.ds(q0, BLOCK, stride=d), pl.ds(q0 - span, BLOCK, stride=d)


def _branch_loops(n_blocks, d, visit, unroll, masks):
    first_valid, cur_part, prev_part = masks
    if d % unroll == 0 and (n_blocks - d) % unroll == 0:
        full_valid = jnp.logical_or(cur_part, prev_part)

        def first(idx, carry):
            rows = pl.ds(idx, BLOCK, stride=d)
            visit(rows, [rows], first_valid)
            return carry

        def rest(idx, carry):
            rows, prev = _block_rows(idx, d)
            visit(rows, [prev, rows], full_valid)
            return carry

        lax.fori_loop(0, d, first, 0, unroll=unroll)
        lax.fori_loop(d, n_blocks, rest, 0, unroll=unroll)
        return

    def every(idx, carry):
        span = BLOCK * d
        q0 = (idx // d) * span + idx % d
        has_prev = idx >= d
        rows = pl.ds(q0, BLOCK, stride=d)
        prev = pl.ds(jnp.where(has_prev, q0 - span, q0), BLOCK, stride=d)
        visit(rows, [prev, rows], jnp.logical_or(cur_part, jnp.logical_and(prev_part, has_prev)))
        return carry

    lax.fori_loop(0, n_blocks, every, 0, unroll=unroll)


def _qkv_specs(s, tiles):
    q, k, v = [pl.BlockSpec((s, LANES), functools.partial(lambda hb, off: (0, off + hb), off=i * tiles))
               for i in range(3)]
    return q, k, v, pl.BlockSpec((s, LANES), lambda hb: (0, hb))


def _attn_seq_fwd(proj, width, name, comm=None):
    s = proj.shape[0]
    q_spec, k_spec, v_spec, cur = _qkv_specs(s, width // LANES)

    def body(q_ref, k_ref, v_ref, o_ref, l_ref, o_s, l_s):
        masks = _query_masks()
        for bi, d in enumerate(DILATIONS):
            def visit(rows, key_rows, valid, bi=bi):
                q2 = _stack_heads(q_ref[rows, :])
                keys = jnp.concatenate([k_ref[r, :] for r in key_rows], axis=0)
                vals = jnp.concatenate([v_ref[r, :] for r in key_rows], axis=0)
                sc = jnp.where(valid, _dot_nt(q2, keys), NEG)
                mx = jnp.max(sc, axis=1, keepdims=True)
                p = jnp.exp(sc - mx)
                den = jnp.sum(p, axis=1, keepdims=True)
                o_s[bi, rows, :] = _unstack_heads(_dot_nn(p, vals) / den)
                l_s[bi, rows, :] = _unstack_heads(jnp.broadcast_to(mx + jnp.log(den), (q2.shape[0], LANES)))

            _branch_loops(s // BLOCK, d, visit, 8, masks)
        for c in range(s // MERGE_CHUNK):
            rows = slice(c * MERGE_CHUNK, (c + 1) * MERGE_CHUNK)
            ls = [l_s[bi, rows, :] for bi in range(len(DILATIONS))]
            top = functools.reduce(jnp.maximum, ls)
            ws = [jnp.exp(l - top) for l in ls]
            den = functools.reduce(lambda a, b: a + b, ws)
            num = functools.reduce(lambda a, b: a + b, [w * o_s[bi, rows, :] for bi, w in enumerate(ws)])
            o_ref[rows, :] = num / den
            l_ref[rows, :] = top + jnp.log(den)

    return _call(
        body, grid=(width // LANES,), in_specs=[q_spec, k_spec, v_spec], out_specs=[cur, cur],
        out_shape=[jax.ShapeDtypeStruct((s, width), F32)] * 2,
        scratch_shapes=[pltpu.VMEM((len(DILATIONS), s, LANES), F32)] * 2,
        args=(proj, proj, proj), name=name, comm=comm)


def _attn_seq_bwd(proj, do, o, lse, cos, sin_signed, name, comm=None):
    s, width = do.shape
    q_spec, k_spec, v_spec, cur = _qkv_specs(s, width // LANES)
    table = pl.BlockSpec((s, LANES), lambda hb: (0, 0))
    qscale = HEAD_DIM ** -0.5

    def body(q_ref, k_ref, v_ref, do_ref, o_ref, l_ref, cos_ref, sin_ref, dq_out, dk_out, dv_out,
             dq_ref, dk_ref, dv_ref):
        dq_ref[...] = jnp.zeros_like(dq_ref)
        dk_ref[...] = jnp.zeros_like(dk_ref)
        dv_ref[...] = jnp.zeros_like(dv_ref)
        masks = _query_masks()
        for d in DILATIONS:
            def visit(rows, key_rows, valid):
                dob = do_ref[rows, :]
                q2 = _stack_heads(q_ref[rows, :])
                do2 = _stack_heads(dob)
                delta = jnp.sum(_stack_heads(dob * o_ref[rows, :]), axis=1, keepdims=True)
                lse2 = _stacked_lse(l_ref[rows, :])
                keys = jnp.concatenate([k_ref[r, :] for r in key_rows], axis=0)
                vals = jnp.concatenate([v_ref[r, :] for r in key_rows], axis=0)
                p = jnp.where(valid, jnp.exp(_dot_nt(q2, keys) - lse2), 0.0)
                ds = p * (_dot_nt(do2, vals) - delta)
                dq_ref[rows, :] += _unstack_heads(_dot_nn(ds, keys))
                dkk = _dot_tn(ds, q2)
                dvv = _dot_tn(p, do2)
                for i, r in enumerate(key_rows):
                    dk_ref[r, :] += dkk[i * BLOCK:(i + 1) * BLOCK]
                    dv_ref[r, :] += dvv[i * BLOCK:(i + 1) * BLOCK]

            _branch_loops(s // BLOCK, d, visit, 8, masks)
        for c in range(s // MERGE_CHUNK):
            rows = slice(c * MERGE_CHUNK, (c + 1) * MERGE_CHUNK)
            cos, sin = cos_ref[rows, :], sin_ref[rows, :]
            dq, dk = dq_ref[rows, :], dk_ref[rows, :]
            dq_out[rows, :] = ((dq * cos - _partner(dq) * sin) * qscale).astype(BF16)
            dk_out[rows, :] = (dk * cos - _partner(dk) * sin).astype(BF16)
            dv_out[rows, :] = dv_ref[rows, :].astype(BF16)

    return _call(
        body, grid=(width // LANES,), in_specs=[q_spec, k_spec, v_spec, cur, cur, cur, table, table],
        out_specs=[cur, cur, cur], out_shape=[jax.ShapeDtypeStruct((s, width), BF16)] * 3,
        scratch_shapes=[pltpu.VMEM((s, LANES), F32)] * 3,
        args=(proj, proj, proj, do, o, lse, cos, sin_signed), name=name, comm=comm)


def _conv_specs(s, a_block, b_block):
    per = CONV_CHUNK // CONV_HALO
    a_cur = pl.BlockSpec((CONV_CHUNK, LANES), lambda cb, i: (i, a_block + cb))
    b_cur = pl.BlockSpec((CONV_CHUNK, LANES), lambda cb, i: (i, b_block + cb))
    a_halo = pl.BlockSpec((CONV_HALO, LANES), lambda cb, i: (jnp.maximum(i * per - 1, 0), a_block + cb))
    b_halo = pl.BlockSpec((CONV_HALO, LANES), lambda cb, i: (jnp.maximum(i * per - 1, 0), b_block + cb))
    w_spec = pl.BlockSpec((CONV_KERNEL, LANES), lambda cb, i: (0, cb))
    vec = pl.BlockSpec((1, LANES), lambda cb, i: (0, cb))
    out = pl.BlockSpec((CONV_CHUNK, LANES), lambda cb, i: (i, cb))
    return a_cur, b_cur, a_halo, b_halo, w_spec, vec, out


def _fill_glu_window(win, a_ref, b_ref, ah_ref, bh_ref, first):
    halo = ah_ref[...] * _sigmoid(bh_ref[...])
    win[0:CONV_HALO, :] = jnp.where(first, 0.0, halo)
    win[CONV_HALO:, :] = a_ref[...] * _sigmoid(b_ref[...])


def _conv_fwd(proj, a_block, b_block, w, bias, name, comm=None):
    s = proj.shape[0]
    cw = w.shape[1]
    a_cur, b_cur, a_halo, b_halo, w_spec, vec, out = _conv_specs(s, a_block, b_block)
    lead = CONV_HALO - (CONV_KERNEL - 1)

    def body(a_ref, b_ref, ah_ref, bh_ref, w_ref, bias_ref, o_ref, win):
        _fill_glu_window(win, a_ref, b_ref, ah_ref, bh_ref, pl.program_id(1) == 0)
        for sub in range(CONV_CHUNK // CONV_SUB):
            base = sub * CONV_SUB
            acc = jnp.zeros((CONV_SUB, LANES), F32) + bias_ref[...]
            for j in range(CONV_KERNEL):
                acc = acc + w_ref[j:j + 1, :] * win[base + lead + j:base + lead + j + CONV_SUB, :]
            o_ref[base:base + CONV_SUB, :] = acc

    return _call(
        body, grid=(cw // LANES, s // CONV_CHUNK), in_specs=[a_cur, b_cur, a_halo, b_halo, w_spec, vec],
        out_specs=[out], out_shape=[jax.ShapeDtypeStruct((s, cw), F32)],
        scratch_shapes=[pltpu.VMEM((CONV_CHUNK + CONV_HALO, LANES), F32)],
        args=(proj, proj, proj, proj, w, bias), name=name, comm=comm)


def _conv_bwd(proj, a_block, b_block, w, du1, name):
    s = proj.shape[0]
    cw = w.shape[1]
    a_cur, b_cur, a_halo, b_halo, w_spec, vec, out = _conv_specs(s, a_block, b_block)
    per = CONV_CHUNK // CONV_HALO
    n_chunks = s // CONV_CHUNK
    d_next = pl.BlockSpec((CONV_HALO, LANES), lambda cb, i: (jnp.minimum((i + 1) * per, s // CONV_HALO - 1), cb))
    lead = CONV_HALO - (CONV_KERNEL - 1)

    def body(a_ref, b_ref, ah_ref, bh_ref, w_ref, d_ref, dn_ref, da_ref, db_ref, dw_ref, dbias_ref, win, dwin):
        i = pl.program_id(1)
        _fill_glu_window(win, a_ref, b_ref, ah_ref, bh_ref, i == 0)
        dwin[0:CONV_CHUNK, :] = d_ref[...]
        dwin[CONV_CHUNK:, :] = jnp.where(i == n_chunks - 1, 0.0, dn_ref[...])

        @pl.when(i == 0)
        def _():
            dw_ref[...] = jnp.zeros_like(dw_ref)
            dbias_ref[...] = jnp.zeros_like(dbias_ref)

        dbias_ref[...] += _colsum(d_ref[...])
        for sub in range(CONV_CHUNK // CONV_SUB):
            base = sub * CONV_SUB
            dcur = dwin[base:base + CONV_SUB, :]
            du0 = jnp.zeros((CONV_SUB, LANES), F32)
            for j in range(CONV_KERNEL):
                back = CONV_KERNEL - 1 - j
                du0 = du0 + w_ref[j:j + 1, :] * dwin[base + back:base + back + CONV_SUB, :]
                dw_ref[j:j + 1, :] += _colsum(dcur * win[base + lead + j:base + lead + j + CONV_SUB, :])
            av = a_ref[base:base + CONV_SUB, :]
            sig = _sigmoid(b_ref[base:base + CONV_SUB, :])
            da_ref[base:base + CONV_SUB, :] = (du0 * sig).astype(BF16)
            db_ref[base:base + CONV_SUB, :] = (du0 * av * sig * (1.0 - sig)).astype(BF16)

    return pl.pallas_call(
        body, grid=(cw // LANES, n_chunks), in_specs=[a_cur, b_cur, a_halo, b_halo, w_spec, out, d_next],
        out_specs=[out, out, w_spec, vec],
        out_shape=[jax.ShapeDtypeStruct((s, cw), BF16), jax.ShapeDtypeStruct((s, cw), BF16),
                   jax.ShapeDtypeStruct((CONV_KERNEL, cw), F32), jax.ShapeDtypeStruct((1, cw), F32)],
        scratch_shapes=[pltpu.VMEM((CONV_CHUNK + CONV_HALO, LANES), F32)] * 2,
        compiler_params=_params(2), name=name)(proj, proj, proj, proj, w, du1, du1)


def _adamw_math(w, g, m, v):
    m = ADAM_B1 * m + (1.0 - ADAM_B1) * g
    v = ADAM_B2 * v + (1.0 - ADAM_B2) * (g * g)
    m_hat = m / (1.0 - ADAM_B1 ** ADAM_STEP)
    v_hat = v / (1.0 - ADAM_B2 ** ADAM_STEP)
    delta = -ADAM_LR * (m_hat / (jnp.sqrt(v_hat) + ADAM_EPS) + ADAM_WD * w)
    return delta, m, v


def _adamw_big(w, g, m, v, name):
    rows, cols = w.shape
    tile = _tile(rows, 256, 8)
    spec = pl.BlockSpec((tile, cols), lambda i: (i, 0))

    def body(w_ref, g_ref, m_ref, v_ref, d_out, m_out, v_out):
        d_out[...], m_out[...], v_out[...] = _adamw_math(w_ref[...], g_ref[...], m_ref[...], v_ref[...])

    return pl.pallas_call(body, grid=(rows // tile,), in_specs=[spec] * 4, out_specs=[spec] * 3,
                          out_shape=[jax.ShapeDtypeStruct(w.shape, F32)] * 3, compiler_params=_params(1),
                          name=name)(w, g, m, v)


def _adamw_reduced(w, land, m, v, name):
    rows, cols = w.shape
    tile = _tile(rows, 256, 16)
    spec = pl.BlockSpec((tile, cols), lambda i: (i, 0))

    def body(w_ref, l_ref, m_ref, v_ref, g_out, d_out, m_out, v_out):
        g = l_ref[0].astype(F32)
        for q in range(1, N_CHIP):
            g = g + l_ref[q].astype(F32)
        g_out[...] = g
        d_out[...], m_out[...], v_out[...] = _adamw_math(w_ref[...], g, m_ref[...], v_ref[...])

    return pl.pallas_call(body, grid=(rows // tile,),
                          in_specs=[spec, pl.BlockSpec((N_CHIP, tile, cols), lambda i: (0, i, 0)), spec, spec],
                          out_specs=[spec] * 4, out_shape=[jax.ShapeDtypeStruct(w.shape, F32)] * 4,
                          compiler_params=_params(1), name=name)(w, land, m, v)


def _adamw_small(ws, gs, ms, vs, name):
    n = len(ws)

    def body(*refs):
        ins, outs = refs[:4 * n], refs[4 * n:]
        for t in range(n):
            res = _adamw_math(ins[t][...], ins[n + t][...], ins[2 * n + t][...], ins[3 * n + t][...])
            for j in range(3):
                outs[j * n + t][...] = res[j]

    shapes = [jax.ShapeDtypeStruct(w.shape, F32) for w in ws]
    res = pl.pallas_call(body, out_shape=shapes * 3, compiler_params=pltpu.CompilerParams(vmem_limit_bytes=VMEM_LIMIT),
                         name=name)(*ws, *gs, *ms, *vs)
    return res[:n], res[n:2 * n], res[2 * n:]


def _sum_blocks(x, n_blocks, name):
    r = x.shape[0] // n_blocks

    def body(x_ref, o_ref):
        acc = x_ref[0:r, :]
        for b in range(1, n_blocks):
            acc = acc + x_ref[b * r:(b + 1) * r, :]
        o_ref[...] = acc

    return pl.pallas_call(body, out_shape=jax.ShapeDtypeStruct((r, x.shape[1]), F32),
                          compiler_params=pltpu.CompilerParams(vmem_limit_bytes=VMEM_LIMIT), name=name)(x)


def _coords():
    return lax.axis_index("x"), lax.axis_index("y"), lax.axis_index("c")


def _flip(v, bit):
    return 1 - v if bit else v


def _ag_small(x, name):
    r, c = x.shape

    def body(x_ref, o_ref, send, recv, local_sem):
        mx, my, mc = _coords()

        def rows(px, py, pc):
            return o_ref.at[pl.ds(pl.multiple_of((4 * px + 2 * py + pc) * r, 8), r), :]

        local = pltpu.make_async_copy(x_ref, rows(mx, my, mc), local_sem)
        local.start()
        peers = [(_flip(mx, k >> 2 & 1), _flip(my, k >> 1 & 1), _flip(mc, k & 1)) for k in range(1, N_DEV)]
        sends = [pltpu.make_async_remote_copy(x_ref, rows(mx, my, mc), send.at[k], recv.at[k], device_id=p,
                                              device_id_type=MESH) for k, p in enumerate(peers)]
        for cp in sends:
            cp.start()
        for k, p in enumerate(peers):
            pltpu.make_async_remote_copy(x_ref, rows(*p), send.at[k], recv.at[k], device_id=p,
                                         device_id_type=MESH).wait_recv()
        for cp in sends:
            cp.wait_send()
        local.wait()

    vm = pl.BlockSpec(memory_space=pltpu.VMEM)
    return pl.pallas_call(
        body, in_specs=[vm], out_specs=vm, out_shape=jax.ShapeDtypeStruct((N_DEV * r, c), x.dtype),
        scratch_shapes=[pltpu.SemaphoreType.DMA((N_DEV - 1,)), pltpu.SemaphoreType.DMA((N_DEV - 1,)),
                        pltpu.SemaphoreType.DMA(())],
        name=name)(x)


class _GatherSmall:
    mid = None

    def __init__(self, x):
        self.inputs = [x]
        self.out_shapes = [jax.ShapeDtypeStruct((N_DEV * x.shape[0], x.shape[1]), x.dtype)]
        self.scratch = [pltpu.SemaphoreType.DMA((N_DEV - 1,)), pltpu.SemaphoreType.DMA((N_DEV - 1,)),
                        pltpu.SemaphoreType.DMA(())]

    def _plan(self, x_refs, o_refs, sems):
        send, recv, local_sem = sems
        x_ref, o_ref = x_refs[0], o_refs[0]
        r = x_ref.shape[0]
        mx, my, mc = _coords()

        def rows(px, py, pc):
            return o_ref.at[pl.ds(pl.multiple_of((4 * px + 2 * py + pc) * r, 8), r), :]

        peers = [(_flip(mx, k >> 2 & 1), _flip(my, k >> 1 & 1), _flip(mc, k & 1)) for k in range(1, N_DEV)]
        out = [pltpu.make_async_remote_copy(x_ref, rows(mx, my, mc), send.at[k], recv.at[k], device_id=p,
                                            device_id_type=MESH) for k, p in enumerate(peers)]
        arrivals = [pltpu.make_async_remote_copy(x_ref, rows(*p), send.at[k], recv.at[k], device_id=p,
                                                 device_id_type=MESH) for k, p in enumerate(peers)]
        return out, arrivals, pltpu.make_async_copy(x_ref, rows(mx, my, mc), local_sem)

    def start(self, x_refs, o_refs, sems):
        out, _, local = self._plan(x_refs, o_refs, sems)
        local.start()
        for cp in out:
            cp.start()

    def finish(self, x_refs, o_refs, sems):
        out, arrivals, local = self._plan(x_refs, o_refs, sems)
        for cp in arrivals:
            cp.wait_recv()
        for cp in out:
            cp.wait_send()
        local.wait()


class _ModExchange:
    def __init__(self, first, w_ada):
        self.d, cols = w_ada.shape
        part = jax.ShapeDtypeStruct((N_DEV, cols), F32)
        self.g1, self.g2 = _GatherSmall(first), _GatherSmall(part)
        self.inputs = [first, w_ada]
        self.out_shapes = [self.g1.out_shapes[0], jax.ShapeDtypeStruct((N_DEV, self.d), F32), part,
                           self.g2.out_shapes[0]]
        self.scratch = self.g1.scratch + self.g2.scratch + [
            pltpu.VMEM(self.g1.out_shapes[0].shape, F32), pltpu.VMEM(w_ada.shape, F32),
            pltpu.VMEM((N_DEV, self.d), F32), pltpu.VMEM((N_DEV, cols), F32), pltpu.SemaphoreType.DMA(())]

    def start(self, cin, cout, scr):
        self.g1.start(cin[0:1], cout[0:1], scr[0:3])
        pltpu.make_async_copy(cin[1], scr[7], scr[10]).start()

    def mid(self, cin, cout, scr):
        gathered, w_v, silu_v, part_v = scr[6:10]
        self.g1.finish(cin[0:1], cout[0:1], scr[0:3])
        pltpu.sync_copy(cout[0], gathered)
        rows_per = cin[0].shape[0]
        for j in range(N_DEV):
            silu_v[j:j + 1, :] = gathered[j * rows_per:j * rows_per + 1, 0:self.d]
        c_all = silu_v[...]
        silu_v[...] = c_all * _sigmoid(c_all)
        pltpu.sync_copy(silu_v, cout[1])
        pltpu.make_async_copy(cin[1], w_v, scr[10]).wait()
        part_v[...] = _dot_nn(silu_v[...], w_v[...])
        pltpu.sync_copy(part_v, cout[2])
        self.g2.start(cout[2:3], cout[3:4], scr[3:6])

    def finish(self, cin, cout, scr):
        self.g2.finish(cout[2:3], cout[3:4], scr[3:6])


class _GatherWeights:
    def __init__(self, shards):
        n_t = len(shards)
        self.inputs = list(shards)
        self.out_shapes = [jax.ShapeDtypeStruct((N_DEV * x.shape[0], x.shape[1]), x.dtype) for x in shards]
        self.scratch = [pltpu.SemaphoreType.DMA((n_t, 8)), pltpu.SemaphoreType.DMA((n_t, 8)),
                        pltpu.SemaphoreType.DMA((n_t,))]

    def _plan(self, x_refs, o_refs, sems):
        send, recv, local_sem = sems
        mx, my, mc = _coords()
        me, sibling = (mx, my, mc), (mx, my, 1 - mc)
        xn, yn, diag = (1 - mx, my), (mx, 1 - my), (1 - mx, 1 - my)

        def rows(t, chip, core, half=None):
            r = x_refs[t].shape[0]
            base = (4 * chip[0] + 2 * chip[1] + core) * r
            if half is None:
                return o_refs[t].at[pl.ds(pl.multiple_of(base, 8), r), :]
            return o_refs[t].at[pl.ds(pl.multiple_of(base + half * (r // 2), 8), r // 2), :]

        def copy(t, k, block, to, src=None):
            return pltpu.make_async_remote_copy(
                src_ref=block if src is None else src, dst_ref=block,
                send_sem=send.at[t, k], recv_sem=recv.at[t, k], device_id=to, device_id_type=MESH)

        def local(t):
            return pltpu.make_async_copy(x_refs[t], rows(t, (mx, my), mc), local_sem.at[t])

        return (mx, my), mc, me, sibling, xn, yn, diag, rows, copy, local

    def start(self, x_refs, o_refs, sems):
        chip, mc, me, sibling, xn, yn, diag, rows, copy, local = self._plan(x_refs, o_refs, sems)
        for t in range(len(x_refs)):
            mine = rows(t, chip, mc)
            local(t).start()
            copy(t, 0, mine, sibling, src=x_refs[t]).start()
            copy(t, 1, mine, (*xn, mc), src=x_refs[t]).start()
            copy(t, 2, mine, (*yn, mc), src=x_refs[t]).start()

    def mid(self, x_refs, o_refs, sems):
        chip, mc, me, sibling, xn, yn, diag, rows, copy, local = self._plan(x_refs, o_refs, sems)
        for t in range(len(x_refs)):
            copy(t, 1, rows(t, xn, mc), me).wait_recv()
            copy(t, 3, rows(t, xn, mc, 0), (*yn, mc)).start()
            copy(t, 5, rows(t, xn, mc), sibling).start()
        for t in range(len(x_refs)):
            copy(t, 2, rows(t, yn, mc), me).wait_recv()
            copy(t, 4, rows(t, yn, mc, 1), (*xn, mc)).start()
            copy(t, 6, rows(t, yn, mc), sibling).start()

    def finish(self, x_refs, o_refs, sems):
        chip, mc, me, sibling, xn, yn, diag, rows, copy, local = self._plan(x_refs, o_refs, sems)
        for t in range(len(x_refs)):
            copy(t, 3, rows(t, diag, mc, 0), me).wait_recv()
            copy(t, 4, rows(t, diag, mc, 1), me).wait_recv()
            copy(t, 7, rows(t, diag, mc), sibling).start()
        for t in range(len(x_refs)):
            copy(t, 0, rows(t, chip, 1 - mc), me).wait_recv()
            copy(t, 5, rows(t, xn, 1 - mc), me).wait_recv()
            copy(t, 6, rows(t, yn, 1 - mc), me).wait_recv()
            copy(t, 7, rows(t, diag, 1 - mc), me).wait_recv()
            mine = rows(t, chip, mc)
            copy(t, 0, mine, sibling, src=x_refs[t]).wait_send()
            copy(t, 1, mine, (*xn, mc), src=x_refs[t]).wait_send()
            copy(t, 2, mine, (*yn, mc), src=x_refs[t]).wait_send()
            copy(t, 3, rows(t, xn, mc, 0), (*yn, mc)).wait_send()
            copy(t, 4, rows(t, yn, mc, 1), (*xn, mc)).wait_send()
            copy(t, 5, rows(t, xn, mc), sibling).wait_send()
            copy(t, 6, rows(t, yn, mc), sibling).wait_send()
            copy(t, 7, rows(t, diag, mc), sibling).wait_send()
            local(t).wait()


class _SiblingExchange:
    mid = None

    def __init__(self, grads):
        n_t = len(grads)
        self.inputs = list(grads)
        self.out_shapes = [jax.ShapeDtypeStruct((N_CHIP,) + g.shape[2:], F32) for g in grads]
        self.scratch = [pltpu.SemaphoreType.DMA((n_t,)), pltpu.SemaphoreType.DMA((n_t,))]

    def _copies(self, g_refs, land, sems):
        send, recv = sems
        mx, my, mc = _coords()
        return [pltpu.make_async_remote_copy(g_refs[t].at[:, 1 - mc], land[t], send.at[t], recv.at[t],
                                             device_id=(mx, my, 1 - mc), device_id_type=MESH)
                for t in range(len(g_refs))]

    def start(self, g_refs, land, sems):
        for cp in self._copies(g_refs, land, sems):
            cp.start()

    def finish(self, g_refs, land, sems):
        for cp in self._copies(g_refs, land, sems):
            cp.wait()


class _Together:
    def __init__(self, *comms):
        self.comms = comms
        self.inputs = [x for c in comms for x in c.inputs]
        self.out_shapes = [x for c in comms for x in c.out_shapes]
        self.scratch = [x for c in comms for x in c.scratch]
        self.mid = self._mid if any(c.mid is not None for c in comms) else None

    def _each(self, phase, cin, cout, sems):
        i = o = s = 0
        for c in self.comms:
            fn = getattr(c, phase)
            ni, no, ns = len(c.inputs), len(c.out_shapes), len(c.scratch)
            if fn is not None:
                fn(cin[i:i + ni], cout[o:o + no], sems[s:s + ns])
            i, o, s = i + ni, o + no, s + ns

    def start(self, cin, cout, sems):
        self._each("start", cin, cout, sems)

    def _mid(self, cin, cout, sems):
        self._each("mid", cin, cout, sems)

    def finish(self, cin, cout, sems):
        self._each("finish", cin, cout, sems)


def _standalone(comm, name):
    def body():
        pass
    return _call(body, grid=(1,), in_specs=[], out_specs=[], out_shape=[], args=(), name=name, comm=comm)[1]


def _chip_partials(g4s, lands, name):
    n_t = len(g4s)
    in_specs, out_specs, out_shape = [], [], []
    for g4 in g4s:
        _, _, r, c = g4.shape
        in_specs.append(pl.BlockSpec((None, None, r, c), lambda q: (q, lax.axis_index("c"), 0, 0)))
        out_specs.append(pl.BlockSpec((None, r, c), lambda q: (q, 0, 0)))
        out_shape.append(jax.ShapeDtypeStruct((N_CHIP, r, c), BF16))
    in_specs += [pl.BlockSpec((None,) + g4.shape[2:], lambda q: (q, 0, 0)) for g4 in g4s]

    def body(*refs):
        for t in range(n_t):
            refs[2 * n_t + t][...] = (refs[t][...] + refs[n_t + t][...]).astype(BF16)

    return pl.pallas_call(body, grid=(N_CHIP,), in_specs=in_specs, out_specs=out_specs, out_shape=out_shape,
                          compiler_params=_params(1), name=name)(*g4s, *lands)


class _ChipExchange:
    mid = None

    def __init__(self, parts):
        n_t = len(parts)
        self.inputs = list(parts)
        self.out_shapes = [jax.ShapeDtypeStruct(p.shape, p.dtype) for p in parts]
        self.scratch = [pltpu.SemaphoreType.DMA((n_t, 3)), pltpu.SemaphoreType.DMA((n_t, 3)),
                        pltpu.SemaphoreType.DMA((n_t,))]

    def _plan(self, p_refs, land, sems):
        send, recv, local_sem = sems
        mx, my, mc = _coords()
        my_chip = 2 * mx + my
        peers = [(_flip(mx, fx), _flip(my, fy)) for fx, fy in ((1, 0), (0, 1), (1, 1))]

        def out(t, k):
            px, py = peers[k]
            return pltpu.make_async_remote_copy(p_refs[t].at[2 * px + py], land[t].at[my_chip], send.at[t, k],
                                                recv.at[t, k], device_id=(px, py, mc), device_id_type=MESH)

        def arrival(t, k):
            px, py = peers[k]
            return pltpu.make_async_remote_copy(p_refs[t].at[my_chip], land[t].at[2 * px + py], send.at[t, k],
                                                recv.at[t, k], device_id=(px, py, mc), device_id_type=MESH)

        def local(t):
            return pltpu.make_async_copy(p_refs[t].at[my_chip], land[t].at[my_chip], local_sem.at[t])

        return out, arrival, local

    def start(self, p_refs, land, sems):
        out, arrival, local = self._plan(p_refs, land, sems)
        for t in range(len(p_refs)):
            local(t).start()
            for k in range(3):
                out(t, k).start()

    def finish(self, p_refs, land, sems):
        out, arrival, local = self._plan(p_refs, land, sems)
        for t in range(len(p_refs)):
            for k in range(3):
                arrival(t, k).wait_recv()
                out(t, k).wait_send()
            local(t).wait()


def _rope_tables(s, width):
    heads = width // HEAD_DIM
    inv_freq = ROPE_THETA ** (-jnp.arange(0, HEAD_DIM, 2, dtype=F32) / HEAD_DIM)
    inv_full = jnp.tile(inv_freq, 2 * heads)
    sign = jnp.tile(jnp.concatenate([-jnp.ones((HALF_HEAD,), F32), jnp.ones((HALF_HEAD,), F32)]), heads)
    ang = jnp.arange(s, dtype=F32)[:, None] * inv_full[None, :]
    return jnp.cos(ang), jnp.sin(ang) * sign[None, :]


def _pad_rows(v, rows):
    return jnp.concatenate([v, jnp.zeros((rows - 1, v.shape[1]), v.dtype)], axis=0)


def kernel(x, c, w_ada, b_ada, ffn1_norm_g, ffn1_w_gate, ffn1_w_up, ffn1_w_down, mix_norm_g, w_in, conv_dw_w, conv_dw_b, conv_ln_g, conv_ln_b, attn_out_g, conv_out_g, w_out, ffn2_norm_g, ffn2_w_gate, ffn2_w_up, ffn2_w_down, final_norm_g, loss_target, m_w_ada, m_b_ada, m_ffn1_norm_g, m_ffn1_w_gate, m_ffn1_w_up, m_ffn1_w_down, m_mix_norm_g, m_w_in, m_conv_dw_w, m_conv_dw_b, m_conv_ln_g, m_conv_ln_b, m_attn_out_g, m_conv_out_g, m_w_out, m_ffn2_norm_g, m_ffn2_w_gate, m_ffn2_w_up, m_ffn2_w_down, m_final_norm_g, v_w_ada, v_b_ada, v_ffn1_norm_g, v_ffn1_w_gate, v_ffn1_w_up, v_ffn1_w_down, v_mix_norm_g, v_w_in, v_conv_dw_w, v_conv_dw_b, v_conv_ln_g, v_conv_ln_b, v_attn_out_g, v_conv_out_g, v_w_out, v_ffn2_norm_g, v_ffn2_w_gate, v_ffn2_w_up, v_ffn2_w_down, v_final_norm_g):
    mx, my, mc = _coords()
    me = 4 * mx + 2 * my + mc
    s, d = x.shape[1], x.shape[2]
    aw = d // 2
    x2, target = x[0], loss_target[0]
    n_mod = w_ada.shape[2] * N_DEV // d
    mod_cols = w_ada.shape[2]

    def shard(w, transpose):
        return (w[0].T if transpose else w[0]).astype(BF16)

    cw_shard = conv_dw_w.shape[3]
    n_taps = CONV_KERNEL * cw_shard
    first_len = -(-(d + n_taps) // LANES) * LANES
    first = jnp.concatenate([c, conv_dw_w[0, :, 0, :].reshape(1, n_taps), jnp.zeros((1, first_len - d - n_taps), F32)], axis=1)
    first_all, silu_c, _, mod_all, wg1, wu1 = _standalone(
        _Together(_ModExchange(_pad_rows(first, 8), w_ada[0]),
                  _GatherWeights([shard(ffn1_w_gate, True), shard(ffn1_w_up, True)])), "ag_first")
    first_all = first_all[0::8]
    conv_w = first_all[:, d:d + n_taps].reshape(N_DEV, CONV_KERNEL, cw_shard).transpose(1, 0, 2).reshape(CONV_KERNEL, aw)

    mod_all = mod_all.reshape(N_DEV, N_DEV, mod_cols)
    mod = lax.dynamic_index_in_dim(mod_all, me, axis=1, keepdims=False).reshape(1, n_mod * d) + b_ada
    sh1, sc1, g1, sh2, sc2, g2, sh3, sc3, g3 = [mod[:, i * d:(i + 1) * d] for i in range(n_mod)]

    def split(g):
        return g.reshape(N_CHIP, 2, g.shape[0] // N_DEV, g.shape[1])

    def partials(g4s, lands, tag):
        return _chip_partials(g4s, lands, "chip_partials_" + tag)

    (n1, silu1, gs1, hid1), (wd1, win_t, wout) = _norm_ffn_up(
        x2, ffn1_norm_g, sc1, sh1, wg1, wu1, "ffn1_up",
        comm=_GatherWeights([shard(ffn1_w_down, False), shard(w_in, True), shard(w_out, False)]))
    h1, f1, n2 = _residual_mm(hid1, wd1, x2, g1, 0.5, "ffn1_down", norm=(mix_norm_g, sc2, sh2))
    cos, sin_signed = _rope_tables(s, LANES)
    proj, = _proj_rope(n2, win_t, cos, sin_signed, aw, "proj")
    lanes_per = aw // LANES
    (attn, lse), (wg2, wu2, wd2) = _attn_seq_fwd(
        proj, aw, "attn_fwd",
        comm=_GatherWeights([shard(ffn2_w_gate, True), shard(ffn2_w_up, True), shard(ffn2_w_down, False)]))
    u1, = _conv_fwd(proj, 3 * lanes_per, 4 * lanes_per, conv_w, conv_dw_b, "conv_fwd")
    post = (attn_out_g, conv_ln_g, conv_ln_b, conv_out_g)
    y, h2, mix, n3 = _mix_out(attn, u1, post, wout, h1, g2, (ffn2_norm_g, sc3, sh3), "mix_out")
    silu3, gs3, hid3 = _ffn_up(n3, wg2, wu2, "ffn2_up")

    dh3, df3, err2, d_final_g, dg3 = _last_mm_loss(hid3, wd2, h2, g3, 0.5, target, final_norm_g.reshape(1, d),
                                                   "ffn2_down_loss")
    loss_part = jnp.zeros((1, LANES), F32).at[0, 0].set(0.5 * jnp.sum(err2) / d)

    da3, db3 = _ffn_bwd_hidden(df3, wd2, silu3, gs3, "ffn2_hidden_bwd")
    g4_a = [split(_mm_tn(da3, n3, "ffn2_dwg")), split(_mm_tn(db3, n3, "ffn2_dwu")), split(_mm_tn(hid3, df3, "ffn2_dwd"))]
    (dh2, dmix, dsh3, dsc3, dgn3, dg2), land_a = _mm_norm_mod_bwd(
        [(da3, wg2), (db3, wu2)], h2, dh3, ffn2_norm_g, sc3, (mix, g2, 1.0), "ffn2_dn_norm3_bwd", tm=256,
        comm=_SiblingExchange(g4_a))
    parts_a = partials(g4_a, land_a, "a")
    g_wout = _mm_tn(y, dmix, "mix_dwout")
    dattn, du1, d_gains, d_ln = _mix_dy_post_bwd(dmix, wout, attn, u1, post, "mix_dy_post_bwd")
    d_attn_g, d_conv_g, d_ln_g, d_ln_b = d_gains[:, :aw], d_gains[:, aw:], d_ln[:, :aw], d_ln[:, aw:]
    dga, dgb, d_taps, d_conv_b = _conv_bwd(proj, 3 * lanes_per, 4 * lanes_per, conv_w, du1, "conv_bwd")
    (dq, dk, dv), sums_a = _attn_seq_bwd(proj, dattn, attn, lse, cos, sin_signed, "attn_bwd",
                                         comm=_ChipExchange(parts_a))
    dproj = jnp.concatenate([dq, dk, dv, dga, dgb], axis=1)
    g4_b = [split(g_wout), split(_mm_tn(dproj, n2, "mix_dwin"))]
    (dh1, df1, dsh2, dsc2, dgn2, dg1), land_b = _mm_norm_mod_bwd(
        [(dproj, win_t)], h1, dh2, mix_norm_g, sc2, (f1, g1, 0.5), "mix_dn_norm2_bwd", tm=512,
        comm=_SiblingExchange(g4_b))
    parts_b = partials(g4_b, land_b, "b")
    g4_c = [split(_mm_tn(hid1, df1, "ffn1_dwd"))]
    (da1, db1), both = _ffn_bwd_hidden(df1, wd1, silu1, gs1, "ffn1_hidden_bwd",
                                       comm=_Together(_ChipExchange(parts_b), _SiblingExchange(g4_c)))
    sums_b, land_c = both[:2], both[2:]
    parts_c = partials(g4_c, land_c, "c")
    g_wu1, sums_c = _mm_tn(db1, n1, "ffn1_dwu", comm=_ChipExchange(parts_c))
    g4_d = [split(g_wu1)]
    g_wg1, land_d = _mm_tn(da1, n1, "ffn1_dwg", comm=_SiblingExchange(g4_d))
    parts_d = partials(g4_d, land_d, "d")
    g4_e = [split(g_wg1)]
    dn1, both = _plain_mm([(da1, wg1), (db1, wu1)], BF16, False, d, "ffn1_dn",
                          comm=_Together(_ChipExchange(parts_d), _SiblingExchange(g4_e)))
    sums_d, land_e = both[:1], both[1:]
    parts_e = partials(g4_e, land_e, "e")
    (dx, dsh1, dsc1, dgn1), sums_e = _norm_mod_bwd(dn1, x2, dh1, ffn1_norm_g, sc1, "norm1_bwd",
                                                   comm=_ChipExchange(parts_e))

    dmod = jnp.concatenate([dsh1, dsc1, dg1, dsh2, dsc2, dg2, dsh3, dsc3, dg3], axis=1)
    small = [dmod, dgn1, dgn2, dgn3, d_final_g, d_conv_b, d_ln_g, d_ln_b, d_attn_g, d_conv_g,
             d_taps.reshape(1, CONV_KERNEL * aw), loss_part]
    sizes = [v.shape[1] for v in small]
    total = sum(sizes)
    padded = -(-total // (8 * LANES)) * (8 * LANES)
    packed = jnp.concatenate(small + [jnp.zeros((1, padded - total), F32)], axis=1).reshape(8, padded // 8)
    gathered = _ag_small(packed, "ag_small_grads")
    summed = _sum_blocks(gathered, N_DEV, "sum_small_grads").reshape(1, padded)
    offs = [sum(sizes[:i]) for i in range(len(sizes))]
    (g_b_ada, g_gn1, g_gn2, g_gn3, g_final, g_conv_b, g_ln_g, g_ln_b, g_attn_g, g_conv_g, g_taps, loss_row) = [
        summed[:, o:o + n] for o, n in zip(offs, sizes)]
    loss = loss_row[0, 0]
    g_taps_shard = lax.dynamic_slice_in_dim(g_taps.reshape(CONV_KERNEL, aw), me * cw_shard, cw_shard, axis=1)
    dmod_all = gathered.reshape(N_DEV, padded)[:, :n_mod * d]
    dmod_cols = lax.dynamic_slice_in_dim(dmod_all, me * mod_cols, mod_cols, axis=1)
    g_w_ada = _mm_tn(silu_c, dmod_cols, "ada_dw")

    arrived = dict(zip(["ffn2_w_gate", "ffn2_w_up", "ffn2_w_down", "w_out", "w_in", "ffn1_w_down", "ffn1_w_up",
                        "ffn1_w_gate"], list(sums_a) + list(sums_b) + list(sums_c) + list(sums_d) + list(sums_e)))
    transposed = ("ffn1_w_gate", "ffn1_w_up", "w_in", "ffn2_w_gate", "ffn2_w_up")
    grads = {
        "w_ada": g_w_ada, "b_ada": g_b_ada, "ffn1_norm_g": g_gn1, "mix_norm_g": g_gn2, "conv_dw_w": g_taps_shard,
        "conv_dw_b": g_conv_b, "conv_ln_g": g_ln_g, "conv_ln_b": g_ln_b, "attn_out_g": g_attn_g,
        "conv_out_g": g_conv_g, "ffn2_norm_g": g_gn3, "final_norm_g": g_final,
    }
    weights = dict(w_ada=w_ada, b_ada=b_ada, ffn1_norm_g=ffn1_norm_g, ffn1_w_gate=ffn1_w_gate, ffn1_w_up=ffn1_w_up, ffn1_w_down=ffn1_w_down, mix_norm_g=mix_norm_g, w_in=w_in, conv_dw_w=conv_dw_w, conv_dw_b=conv_dw_b, conv_ln_g=conv_ln_g, conv_ln_b=conv_ln_b, attn_out_g=attn_out_g, conv_out_g=conv_out_g, w_out=w_out, ffn2_norm_g=ffn2_norm_g, ffn2_w_gate=ffn2_w_gate, ffn2_w_up=ffn2_w_up, ffn2_w_down=ffn2_w_down, final_norm_g=final_norm_g)
    moms = dict(w_ada=m_w_ada, b_ada=m_b_ada, ffn1_norm_g=m_ffn1_norm_g, ffn1_w_gate=m_ffn1_w_gate, ffn1_w_up=m_ffn1_w_up, ffn1_w_down=m_ffn1_w_down, mix_norm_g=m_mix_norm_g, w_in=m_w_in, conv_dw_w=m_conv_dw_w, conv_dw_b=m_conv_dw_b, conv_ln_g=m_conv_ln_g, conv_ln_b=m_conv_ln_b, attn_out_g=m_attn_out_g, conv_out_g=m_conv_out_g, w_out=m_w_out, ffn2_norm_g=m_ffn2_norm_g, ffn2_w_gate=m_ffn2_w_gate, ffn2_w_up=m_ffn2_w_up, ffn2_w_down=m_ffn2_w_down, final_norm_g=m_final_norm_g)
    vars_ = dict(w_ada=v_w_ada, b_ada=v_b_ada, ffn1_norm_g=v_ffn1_norm_g, ffn1_w_gate=v_ffn1_w_gate, ffn1_w_up=v_ffn1_w_up, ffn1_w_down=v_ffn1_w_down, mix_norm_g=v_mix_norm_g, w_in=v_w_in, conv_dw_w=v_conv_dw_w, conv_dw_b=v_conv_dw_b, conv_ln_g=v_conv_ln_g, conv_ln_b=v_conv_ln_b, attn_out_g=v_attn_out_g, conv_out_g=v_conv_out_g, w_out=v_w_out, ffn2_norm_g=v_ffn2_norm_g, ffn2_w_gate=v_ffn2_w_gate, ffn2_w_up=v_ffn2_w_up, ffn2_w_down=v_ffn2_w_down, final_norm_g=v_final_norm_g)
    names = list(weights)
    big = ["w_ada", "ffn1_w_gate", "ffn1_w_up", "ffn1_w_down", "w_in", "w_out", "ffn2_w_gate", "ffn2_w_up",
           "ffn2_w_down"]
    shape2 = {n: (weights[n].shape[-2] if weights[n].ndim > 1 else 1, weights[n].shape[-1]) for n in names}
    shape2["conv_dw_w"] = (CONV_KERNEL, cw_shard)
    g_out, d_out, m_out, v_out = {}, {}, {}, {}
    for n in big:
        if n in arrived:
            def view(t, n=n):
                return t[0].T if n in transposed else t[0]
            res = _adamw_reduced(view(weights[n]), arrived[n], view(moms[n]), view(vars_[n]), "adamw_" + n)
            g_out[n], d_out[n], m_out[n], v_out[n] = [r.T if n in transposed else r for r in res]
        else:
            g2d = grads[n].reshape(shape2[n])
            res = _adamw_big(weights[n].reshape(shape2[n]), g2d, moms[n].reshape(shape2[n]),
                             vars_[n].reshape(shape2[n]), "adamw_" + n)
            g_out[n], (d_out[n], m_out[n], v_out[n]) = g2d, res
    rest = [n for n in names if n not in big]
    res = _adamw_small([weights[n].reshape(shape2[n]) for n in rest], [grads[n].reshape(shape2[n]) for n in rest],
                       [moms[n].reshape(shape2[n]) for n in rest], [vars_[n].reshape(shape2[n]) for n in rest],
                       "adamw_small")
    for i, n in enumerate(rest):
        g_out[n], d_out[n], m_out[n], v_out[n] = grads[n], res[0][i], res[1][i], res[2][i]

    def shaped(table):
        return [table[n].reshape(weights[n].shape) for n in names]

    return (loss, dx.reshape(x.shape), *shaped(g_out), *shaped(d_out), *shaped(m_out), *shaped(v_out))
```

```python
import functools

import jax
import jax.numpy as jnp
from jax import lax
from jax.experimental import pallas as pl
from jax.experimental.pallas import tpu as pltpu

F32 = jnp.float32
BF16 = jnp.bfloat16
MESH = pl.DeviceIdType.MESH
ANY = pl.BlockSpec(memory_space=pl.ANY)

N_DEV = 8
N_CHIP = 4
HEAD_DIM = 64
HALF_HEAD = HEAD_DIM // 2
LANES = 128
BLOCK = 128
DILATIONS = (1, 4, 16)
MERGE_CHUNK = 512
ROPE_THETA = 10000.0
CONV_KERNEL = 31
CONV_HALO = 32
CONV_CHUNK = 512
CONV_SUB = 128
RMS_EPS = 1e-6
LN_EPS = 1e-5
ADAM_LR = 0.001
ADAM_B1 = 0.9
ADAM_B2 = 0.999
ADAM_EPS = 1e-08
ADAM_WD = 0.01
ADAM_STEP = 10
VMEM_LIMIT = 56 * 1024 * 1024
NEG = -1e30


def _params(n_axes):
    return pltpu.CompilerParams(dimension_semantics=("arbitrary",) * n_axes, vmem_limit_bytes=VMEM_LIMIT)


def _tile(n, target, unit):
    best = None
    for t in range(unit, min(n, target) + 1, unit):
        if n % t == 0:
            best = t
    return best if best is not None else n


def _sigmoid(x):
    return 0.5 * (jnp.tanh(0.5 * x) + 1.0)


def _call(body, *, grid, in_specs, out_specs, out_shape, args, name, scratch_shapes=(), comm=None):
    params = _params(len(grid))
    if comm is None:
        return pl.pallas_call(body, grid=grid, in_specs=list(in_specs), out_specs=list(out_specs),
                              out_shape=list(out_shape), scratch_shapes=list(scratch_shapes),
                              compiler_params=params, name=name)(*args)
    n_in, n_out, n_scr = len(args), len(out_shape), len(scratch_shapes)
    c_in, c_out = len(comm.inputs), len(comm.out_shapes)
    steps = 1
    for g in grid:
        steps *= g

    def hosted(*refs):
        pos = 0
        parts = []
        for size in (n_in, c_in, n_out, c_out, n_scr, len(comm.scratch)):
            parts.append(refs[pos:pos + size])
            pos += size
        ins, cin, outs, cout, scr, cscr = parts
        step = 0
        for axis, g in enumerate(grid):
            step = step * g + pl.program_id(axis)

        @pl.when(step == 0)
        def _():
            comm.start(cin, cout, cscr)

        body(*ins, *outs, *scr)
        if comm.mid is not None and steps >= 4:
            @pl.when(step == steps // 2)
            def _():
                comm.mid(cin, cout, cscr)

        @pl.when(step == steps - 1)
        def _():
            if comm.mid is not None and steps < 4:
                comm.mid(cin, cout, cscr)
            comm.finish(cin, cout, cscr)

    res = pl.pallas_call(
        hosted, grid=grid, in_specs=list(in_specs) + [ANY] * c_in, out_specs=list(out_specs) + [ANY] * c_out,
        out_shape=list(out_shape) + list(comm.out_shapes), scratch_shapes=list(scratch_shapes) + list(comm.scratch),
        compiler_params=params, name=name)(*args, *comm.inputs)
    return res[:n_out], res[n_out:]


def _rows(fn, rows_in, vecs_in, rows_out, vecs_out, *, tile, name, comm=None):
    norm = [r if isinstance(r, tuple) else (r, r.shape[1], 0) for r in rows_in]
    n_rows = norm[0][0].shape[0]
    n_tiles = n_rows // tile
    in_specs, args = [], []
    for arr, width, cb in norm:
        in_specs.append(pl.BlockSpec((tile, width), functools.partial(lambda i, cb: (i, cb), cb=cb)))
        args.append(arr)
    for v in vecs_in:
        in_specs.append(pl.BlockSpec((1, v.shape[1]), lambda i: (0, 0)))
        args.append(v)
    out_shape = [jax.ShapeDtypeStruct((n_rows, w), dt) for w, dt in rows_out]
    out_shape += [jax.ShapeDtypeStruct((1, w), F32) for w in vecs_out]
    out_specs = [pl.BlockSpec((tile, w), lambda i: (i, 0)) for w, _ in rows_out]
    out_specs += [pl.BlockSpec((1, w), lambda i: (0, 0)) for w in vecs_out]
    n_in, n_ro = len(args), len(rows_out)

    def body(*refs):
        vals = [r[...] for r in refs[:n_in]]
        outs = refs[n_in:]
        row_vals, vec_vals = fn(*vals)
        for ref, val in zip(outs[:n_ro], row_vals):
            if isinstance(val, tuple):
                w = val[0].shape[1]
                for j, piece in enumerate(val):
                    ref[:, j * w:(j + 1) * w] = piece.astype(ref.dtype)
            else:
                ref[...] = val.astype(ref.dtype)
        if vecs_out:
            @pl.when(pl.program_id(0) == 0)
            def _():
                for ref in outs[n_ro:]:
                    ref[...] = jnp.zeros_like(ref)
            for ref, val in zip(outs[n_ro:], vec_vals):
                ref[...] += val

    return _call(body, grid=(n_tiles,), in_specs=in_specs, out_specs=out_specs, out_shape=out_shape, args=args,
                 name=name, comm=comm)


def _colsum(x):
    return jnp.sum(x, axis=0, keepdims=True)


def _rms_stats(h):
    r = lax.rsqrt(jnp.mean(h * h, axis=-1, keepdims=True) + RMS_EPS)
    return r, h * r


def _rms_back(r, xn, dxn):
    return r * (dxn - xn * jnp.mean(dxn * xn, axis=-1, keepdims=True))


def _branch_back(dh, f, gate, coef):
    return (coef * gate) * dh, coef * _colsum(f.astype(F32) * dh)


def _norm_mod_back(dn, h, dh_in, gain, scale):
    dn = dn.astype(F32)
    r, xn = _rms_stats(h)
    y = xn * gain
    dy = dn * (1.0 + scale)
    dh = dh_in + _rms_back(r, xn, dy * gain)
    return dh, [_colsum(dn), _colsum(dn * y), _colsum(dy * xn)]


def _norm_mod_bwd(dn, h, dh_in, gain, scale, name, comm=None):
    d = h.shape[1]

    def fn(dn, h, dh_in, gain, scale):
        dh, vecs = _norm_mod_back(dn, h, dh_in, gain, scale)
        return [dh], vecs
    return _rows(fn, [dn, h, dh_in], [gain, scale], [(d, F32)], [d, d, d], tile=256, name=name, comm=comm)


def _mm_norm_mod_bwd(pairs, h, dh_in, gain, scale, branch, name, tm, comm=None):
    f, gate, coef = branch

    def epi(accs, ex, vc):
        dh, vecs = _norm_mod_back(accs[0], ex[0], ex[1], vc[0], vc[1])
        df, dgate = _branch_back(dh, ex[2], vc[2], coef)
        return [dh, df] + vecs + [dgate]
    return _mm([pairs], epi, [h, dh_in, f], [gain, scale, gate], [F32, BF16], trans_rhs=False, tm=tm,
               tn=h.shape[1], name=name, n_sums=4, comm=comm)


def _last_mm_loss(lhs, w, res, gate, coef, target, gain, name):
    d = w.shape[1]

    def epi(accs, ex, vc):
        f = accs[0]
        h = ex[0] + (coef * vc[0]) * f
        r, xn = _rms_stats(h)
        err = xn * vc[1] - ex[1]
        dout = err * (1.0 / d)
        dh = _rms_back(r, xn, dout * vc[1])
        df, dgate = _branch_back(dh, f, vc[0], coef)
        return [dh, df, _colsum(err * err), _colsum(dout * xn), dgate]
    return _mm([[(lhs, w)]], epi, [res, target], [gate, gain], [F32, BF16], trans_rhs=False, tm=256, tn=d,
               name=name, n_sums=3)


def _partner(x):
    if x.shape[1] > LANES:
        return jnp.concatenate([_partner(x[:, c:c + LANES]) for c in range(0, x.shape[1], LANES)], axis=1)
    lane = lax.broadcasted_iota(jnp.int32, x.shape, 1) % HEAD_DIM
    return jnp.where(lane < HALF_HEAD, pltpu.roll(x, LANES - HALF_HEAD, 1), pltpu.roll(x, HALF_HEAD, 1))


def _proj_rope(n, w_t, cos, sin_signed, width, name, comm=None):
    s, kdim = n.shape
    n_cols = w_t.shape[0]
    tm = _tile(s, 1024, 8)
    qscale = HEAD_DIM ** -0.5

    chunk = _tile(tm, 256, 8)

    def body(n_ref, w_ref, cos_ref, sin_ref, o_ref):
        j = pl.program_id(0)

        def products(rows):
            return lax.dot_general(n_ref[rows, :].astype(BF16), w_ref[...].astype(BF16), (((1,), (1,)), ((), ())),
                                   preferred_element_type=F32)

        @pl.when(j >= 2)
        def _():
            for c in range(tm // chunk):
                rows = slice(c * chunk, (c + 1) * chunk)
                o_ref[rows, :] = products(rows)

        @pl.when(j < 2)
        def _():
            scale = jnp.where(j == 0, qscale, 1.0)
            for c in range(tm // chunk):
                rows = slice(c * chunk, (c + 1) * chunk)
                acc = products(rows)
                cos = jnp.tile(cos_ref[rows, :], (1, width // LANES))
                sin = jnp.tile(sin_ref[rows, :], (1, width // LANES))
                o_ref[rows, :] = scale * (acc * cos + _partner(acc) * sin)

    table = pl.BlockSpec((tm, LANES), lambda j, i: (jnp.where(j < 2, i, 0), 0))
    return _call(
        body, grid=(n_cols // width, s // tm),
        in_specs=[pl.BlockSpec((tm, kdim), lambda j, i: (i, 0)), pl.BlockSpec((width, kdim), lambda j, i: (j, 0)),
                  table, table],
        out_specs=[pl.BlockSpec((tm, width), lambda j, i: (i, j))],
        out_shape=[jax.ShapeDtypeStruct((s, n_cols), F32)], args=(n, w_t, cos, sin_signed), name=name, comm=comm)


def _mix_post(attn, u1, attn_g, ln_g, ln_b, conv_g):
    _, xa = _rms_stats(attn)
    mu = jnp.mean(u1, axis=-1, keepdims=True)
    xc = u1 - mu
    rstd = lax.rsqrt(jnp.mean(xc * xc, axis=-1, keepdims=True) + LN_EPS)
    u2 = (xc * rstd) * ln_g + ln_b
    u3 = u2 * _sigmoid(u2)
    _, x3 = _rms_stats(u3)
    return jnp.concatenate([xa * attn_g, x3 * conv_g], axis=1)


def _mix_post_back(dy, attn, u1, attn_g, ln_g, ln_b, conv_g):
    w = attn.shape[1]
    dya, dyc = dy[:, :w], dy[:, w:]
    ra, xa = _rms_stats(attn)
    dattn = _rms_back(ra, xa, dya * attn_g)
    mu = jnp.mean(u1, axis=-1, keepdims=True)
    xc = u1 - mu
    rstd = lax.rsqrt(jnp.mean(xc * xc, axis=-1, keepdims=True) + LN_EPS)
    xh = xc * rstd
    u2 = xh * ln_g + ln_b
    sig = _sigmoid(u2)
    u3 = u2 * sig
    r3, x3 = _rms_stats(u3)
    du3 = _rms_back(r3, x3, dyc * conv_g)
    du2 = du3 * (sig + u3 * (1.0 - sig))
    dxh = du2 * ln_g
    du1 = rstd * (dxh - jnp.mean(dxh, axis=-1, keepdims=True) - xh * jnp.mean(dxh * xh, axis=-1, keepdims=True))
    return dattn, du1, [_colsum(dya * xa), _colsum(dyc * x3), _colsum(du2 * xh), _colsum(du2)]


def _mm(groups, epi, extras, vecs, outs, *, trans_rhs, tm, tn, name, n_sums=0, pre=None, pre_inputs=(),
        comm=None):
    m = (pre_inputs[0] if pre is not None else groups[0][0][0]).shape[0]
    n = groups[0][0][1].shape[0] if trans_rhs else groups[0][0][1].shape[1]
    tm, tn = min(tm, m), min(tn, n)
    in_specs, args, uses_pre = [], [], []
    for grp in groups:
        for lhs, rhs in grp:
            k = rhs.shape[1] if trans_rhs else rhs.shape[0]
            uses_pre.append(lhs is None)
            if lhs is not None:
                in_specs.append(pl.BlockSpec((tm, k), lambda j, i: (i, 0)))
                args.append(lhs)
            in_specs.append(pl.BlockSpec((tn, k), lambda j, i: (j, 0)) if trans_rhs
                            else pl.BlockSpec((k, tn), lambda j, i: (0, j)))
            args.append(rhs)
    n_mm = len(args)
    for p in pre_inputs:
        in_specs.append(pl.BlockSpec((tm, p.shape[1]), lambda j, i: (i, 0)))
        args.append(p)
    for e in extras:
        in_specs.append(pl.BlockSpec((tm, tn), lambda j, i: (i, j)) if e.shape[1] == n
                        else pl.BlockSpec((tm, e.shape[1]), lambda j, i: (i, 0)))
        args.append(e)
    for v in vecs:
        in_specs.append(pl.BlockSpec((1, tn), lambda j, i: (0, j)) if v.shape[1] == n
                        else pl.BlockSpec((1, v.shape[1]), lambda j, i: (0, 0)))
        args.append(v)
    sizes = [len(g) for g in groups]
    n_pre, n_ex, n_vec = len(pre_inputs), len(extras), len(vecs)
    dims = (((1,), (1,)), ((), ())) if trans_rhs else (((1,), (0,)), ((), ()))
    out_specs, out_shape = [], []
    if pre is not None:
        k_pre = args[n_mm - 1].shape[1] if trans_rhs else args[n_mm - 1].shape[0]
        out_specs.append(pl.BlockSpec((tm, k_pre), lambda j, i: (i, 0)))
        out_shape.append(jax.ShapeDtypeStruct((m, k_pre), BF16))
    for o in outs:
        dt, width = o if isinstance(o, tuple) else (o, n)
        out_specs.append(pl.BlockSpec((tm, tn), lambda j, i: (i, j)) if width == n
                         else pl.BlockSpec((tm, width), lambda j, i: (i, 0)))
        out_shape.append(jax.ShapeDtypeStruct((m, width), dt))
    n_tiles_out = len(out_specs)
    out_specs += [pl.BlockSpec((1, tn), lambda j, i: (0, j))] * n_sums
    out_shape += [jax.ShapeDtypeStruct((1, n), F32)] * n_sums

    def body(*refs):
        ins = refs[:n_mm + n_pre + n_ex + n_vec]
        out_refs = refs[n_mm + n_pre + n_ex + n_vec:]
        vc = [r[...] for r in ins[n_mm + n_pre + n_ex:]]
        vals = []
        made = None
        if pre is not None:
            made = pre([r[...] for r in ins[n_mm:n_mm + n_pre]], vc).astype(BF16)
            vals.append(made)
        accs, pos, pair = [], 0, 0
        for size in sizes:
            acc = None
            for _ in range(size):
                if uses_pre[pair]:
                    lhs_tile = made
                else:
                    lhs_tile = ins[pos][...].astype(BF16)
                    pos += 1
                part = lax.dot_general(lhs_tile, ins[pos][...].astype(BF16), dims, preferred_element_type=F32)
                acc = part if acc is None else acc + part
                pos += 1
                pair += 1
            accs.append(acc)
        ex = [r[...] for r in ins[n_mm + n_pre:n_mm + n_pre + n_ex]]
        vals += epi(accs, ex, vc)
        for ref, val in zip(out_refs[:n_tiles_out], vals):
            ref[...] = val.astype(ref.dtype)
        if n_sums:
            @pl.when(pl.program_id(1) == 0)
            def _():
                for ref in out_refs[n_tiles_out:]:
                    ref[...] = jnp.zeros_like(ref)
            for ref, val in zip(out_refs[n_tiles_out:], vals[n_tiles_out:]):
                ref[...] += val

    return _call(body, grid=(n // tn, m // tm), in_specs=in_specs, out_specs=out_specs, out_shape=out_shape,
                 args=args, name=name, comm=comm)


def _mm_tn(lhs, rhs, name, comm=None):
    t, a = lhs.shape
    b = rhs.shape[1]
    ta = a if a <= 1536 else _tile(a, 1536, LANES)
    tk = _tile(t, 2048, 8)

    def body(l_ref, r_ref, o_ref):
        @pl.when(pl.program_id(1) == 0)
        def _():
            o_ref[...] = jnp.zeros_like(o_ref)
        o_ref[...] += lax.dot_general(l_ref[...].astype(BF16), r_ref[...].astype(BF16), (((0,), (0,)), ((), ())),
                                      preferred_element_type=F32)

    res = _call(body, grid=(a // ta, t // tk),
                in_specs=[pl.BlockSpec((tk, ta), lambda i, k: (k, i)), pl.BlockSpec((tk, b), lambda i, k: (k, 0))],
                out_specs=[pl.BlockSpec((ta, b), lambda i, k: (i, 0))], out_shape=[jax.ShapeDtypeStruct((a, b), F32)],
                args=(lhs, rhs), name=name, comm=comm)
    return res[0] if comm is None else (res[0][0], res[1])


def _ffn_tn(f):
    return _tile(f, 1536, LANES)


def _swiglu_parts(a, b):
    sig = _sigmoid(a)
    silu = a * sig
    return [silu, b * (sig + silu * (1.0 - sig)), silu * b]


def _ffn_up(n, wg_t, wu_t, name, comm=None):
    def epi(accs, ex, vc):
        return _swiglu_parts(accs[0], accs[1])
    return _mm([[(n, wg_t)], [(n, wu_t)]], epi, [], [], [BF16, BF16, BF16], trans_rhs=True, tm=512,
               tn=_ffn_tn(wg_t.shape[0]), name=name, comm=comm)


def _norm_ffn_up(h, gain, scale, shift, wg_t, wu_t, name, comm=None):
    def pre(tiles, vc):
        _, xn = _rms_stats(tiles[0])
        return (xn * vc[0]) * (1.0 + vc[1]) + vc[2]

    def epi(accs, ex, vc):
        return _swiglu_parts(accs[0], accs[1])
    return _mm([[(None, wg_t)], [(None, wu_t)]], epi, [], [gain, scale, shift], [BF16, BF16, BF16], trans_rhs=True,
               tm=256, tn=wg_t.shape[0], name=name, pre=pre, pre_inputs=[h], comm=comm)


def _mix_out(attn, u1, post, w, res, gate, norm, name):
    def pre(tiles, vc):
        return _mix_post(tiles[0], tiles[1], *vc[4:8])

    def epi(accs, ex, vc):
        h = ex[0] + vc[0] * accs[0]
        _, xn = _rms_stats(h)
        return [h, accs[0], (xn * vc[1]) * (1.0 + vc[2]) + vc[3]]
    return _mm([[(None, w)]], epi, [res], [gate] + list(norm) + list(post), [F32, BF16, BF16], trans_rhs=False,
               tm=512, tn=w.shape[1], name=name, pre=pre, pre_inputs=[attn, u1])


def _mix_dy_post_bwd(dmix, w, attn, u1, post, name):
    width = attn.shape[1]

    def epi(accs, ex, vc):
        dattn, du1, sums = _mix_post_back(accs[0], ex[0], ex[1], *vc)
        return [dattn, du1, jnp.concatenate(sums[0:2], axis=1), jnp.concatenate(sums[2:4], axis=1)]
    return _mm([[(dmix, w)]], epi, [attn, u1], list(post), [(F32, width), (F32, width)], trans_rhs=True, tm=256,
               tn=w.shape[0], name=name, n_sums=2)


def _residual_mm(lhs, w, res, gate, coef, name, norm=None, comm=None):
    def epi(accs, ex, vc):
        h = ex[0] + (coef * vc[0]) * accs[0]
        if norm is None:
            return [h, accs[0]]
        _, xn = _rms_stats(h)
        return [h, accs[0], (xn * vc[1]) * (1.0 + vc[2]) + vc[3]]
    vecs = [gate] + (list(norm) if norm is not None else [])
    outs = [F32, BF16] + ([BF16] if norm is not None else [])
    return _mm([[(lhs, w)]], epi, [res], vecs, outs, trans_rhs=False, tm=512, tn=w.shape[1], name=name, comm=comm)


def _ffn_bwd_hidden(df, wd, dhid_db, dhid_da, name, comm=None):
    def epi(accs, ex, vc):
        return [accs[0] * ex[1].astype(F32), accs[0] * ex[0].astype(F32)]
    return _mm([[(df, wd)]], epi, [dhid_db, dhid_da], [], [BF16, BF16], trans_rhs=True, tm=512,
               tn=_ffn_tn(wd.shape[0]), name=name, comm=comm)


def _plain_mm(pairs, out_dtype, trans_rhs, tn, name, tm=512, comm=None):
    def epi(accs, ex, vc):
        return [accs[0]]
    res = _mm([pairs], epi, [], [], [out_dtype], trans_rhs=trans_rhs, tm=tm, tn=tn, name=name, comm=comm)
    return res[0] if comm is None else (res[0][0], res[1])


HEADS_PER_TILE = LANES // HEAD_DIM


def _stack_heads(x):
    lane = lax.broadcasted_iota(jnp.int32, (1, LANES), 1)
    return jnp.concatenate([x * (lane // HEAD_DIM == h).astype(F32) for h in range(HEADS_PER_TILE)], axis=0)


def _unstack_heads(y):
    r = y.shape[0] // HEADS_PER_TILE
    lane = lax.broadcasted_iota(jnp.int32, (r, y.shape[1]), 1)
    out = y[0:r]
    for h in range(1, HEADS_PER_TILE):
        out = jnp.where(lane // HEAD_DIM == h, y[h * r:(h + 1) * r], out)
    return out


def _stacked_lse(lb):
    return jnp.concatenate([_lane_pick(lb, h) for h in range(HEADS_PER_TILE)], axis=0)


def _band_masks(n_row_blocks, n_col_blocks):
    shape = (n_row_blocks * BLOCK, n_col_blocks * BLOCK)
    qi = lax.broadcasted_iota(jnp.int32, shape, 0) % BLOCK
    kj = lax.broadcasted_iota(jnp.int32, shape, 1) % BLOCK
    return kj <= qi, kj >= qi


def _query_masks():
    first_valid, _ = _band_masks(HEADS_PER_TILE, 1)
    same_ok, before_ok = _band_masks(HEADS_PER_TILE, 2)
    is_cur = lax.broadcasted_iota(jnp.int32, same_ok.shape, 1) >= BLOCK
    return first_valid, jnp.logical_and(is_cur, same_ok), jnp.logical_and(jnp.logical_not(is_cur), before_ok)


def _dot_nt(a, b):
    return lax.dot_general(a.astype(BF16), b.astype(BF16), (((1,), (1,)), ((), ())), preferred_element_type=F32)


def _dot_nn(a, b):
    return lax.dot_general(a.astype(BF16), b.astype(BF16), (((1,), (0,)), ((), ())), preferred_element_type=F32)


def _dot_tn(a, b):
    return lax.dot_general(a.astype(BF16), b.astype(BF16), (((0,), (0,)), ((), ())), preferred_element_type=F32)


def _lane_pick(x, h):
    lane = lax.broadcasted_iota(jnp.int32, x.shape, 1)
    return jnp.sum(jnp.where(lane == h * HEAD_DIM, x, 0.0), axis=1, keepdims=True)


def _block_rows(idx, d):
    span = BLOCK * d
    q0 = (idx // d) * span + idx % d
    return pl.ds(q0, BLOCK, stride=d), pl.ds(q0 - span, BLOCK, stride=d)


def _branch_loops(n_blocks, d, visit, unroll, masks):
    first_valid, cur_part, prev_part = masks
    if d % unroll == 0 and (n_blocks - d) % unroll == 0:
        full_valid = jnp.logical_or(cur_part, prev_part)

        def first(idx, carry):
            rows = pl.ds(idx, BLOCK, stride=d)
            visit(rows, [rows], first_valid)
            return carry

        def rest(idx, carry):
            rows, prev = _block_rows(idx, d)
            visit(rows, [prev, rows], full_valid)
            return carry

        lax.fori_loop(0, d, first, 0, unroll=unroll)
        lax.fori_loop(d, n_blocks, rest, 0, unroll=unroll)
        return

    def every(idx, carry):
        span = BLOCK * d
        q0 = (idx // d) * span + idx % d
        has_prev = idx >= d
        rows = pl.ds(q0, BLOCK, stride=d)
        prev = pl.ds(jnp.where(has_prev, q0 - span, q0), BLOCK, stride=d)
        visit(rows, [prev, rows], jnp.logical_or(cur_part, jnp.logical_and(prev_part, has_prev)))
        return carry

    lax.fori_loop(0, n_blocks, every, 0, unroll=unroll)


def _qkv_specs(s, tiles):
    q, k, v = [pl.BlockSpec((s, LANES), functools.partial(lambda hb, off: (0, off + hb), off=i * tiles))
               for i in range(3)]
    return q, k, v, pl.BlockSpec((s, LANES), lambda hb: (0, hb))


def _attn_seq_fwd(proj, width, name, comm=None):
    s = proj.shape[0]
    q_spec, k_spec, v_spec, cur = _qkv_specs(s, width // LANES)

    def body(q_ref, k_ref, v_ref, o_ref, l_ref, o_s, l_s):
        masks = _query_masks()
        for bi, d in enumerate(DILATIONS):
            def visit(rows, key_rows, valid, bi=bi):
                q2 = _stack_heads(q_ref[rows, :])
                keys = jnp.concatenate([k_ref[r, :] for r in key_rows], axis=0)
                vals = jnp.concatenate([v_ref[r, :] for r in key_rows], axis=0)
                sc = jnp.where(valid, _dot_nt(q2, keys), NEG)
                mx = jnp.max(sc, axis=1, keepdims=True)
                p = jnp.exp(sc - mx)
                den = jnp.sum(p, axis=1, keepdims=True)
                o_s[bi, rows, :] = _unstack_heads(_dot_nn(p, vals) / den)
                l_s[bi, rows, :] = _unstack_heads(jnp.broadcast_to(mx + jnp.log(den), (q2.shape[0], LANES)))

            _branch_loops(s // BLOCK, d, visit, 8, masks)
        for c in range(s // MERGE_CHUNK):
            rows = slice(c * MERGE_CHUNK, (c + 1) * MERGE_CHUNK)
            ls = [l_s[bi, rows, :] for bi in range(len(DILATIONS))]
            top = functools.reduce(jnp.maximum, ls)
            ws = [jnp.exp(l - top) for l in ls]
            den = functools.reduce(lambda a, b: a + b, ws)
            num = functools.reduce(lambda a, b: a + b, [w * o_s[bi, rows, :] for bi, w in enumerate(ws)])
            o_ref[rows, :] = num / den
            l_ref[rows, :] = top + jnp.log(den)

    return _call(
        body, grid=(width // LANES,), in_specs=[q_spec, k_spec, v_spec], out_specs=[cur, cur],
        out_shape=[jax.ShapeDtypeStruct((s, width), F32)] * 2,
        scratch_shapes=[pltpu.VMEM((len(DILATIONS), s, LANES), F32)] * 2,
        args=(proj, proj, proj), name=name, comm=comm)


def _attn_seq_bwd(proj, do, o, lse, cos, sin_signed, name, comm=None):
    s, width = do.shape
    q_spec, k_spec, v_spec, cur = _qkv_specs(s, width // LANES)
    table = pl.BlockSpec((s, LANES), lambda hb: (0, 0))
    qscale = HEAD_DIM ** -0.5

    def body(q_ref, k_ref, v_ref, do_ref, o_ref, l_ref, cos_ref, sin_ref, dq_out, dk_out, dv_out,
             dq_ref, dk_ref, dv_ref):
        dq_ref[...] = jnp.zeros_like(dq_ref)
        dk_ref[...] = jnp.zeros_like(dk_ref)
        dv_ref[...] = jnp.zeros_like(dv_ref)
        masks = _query_masks()
        for d in DILATIONS:
            def visit(rows, key_rows, valid):
                dob = do_ref[rows, :]
                q2 = _stack_heads(q_ref[rows, :])
                do2 = _stack_heads(dob)
                delta = jnp.sum(_stack_heads(dob * o_ref[rows, :]), axis=1, keepdims=True)
                lse2 = _stacked_lse(l_ref[rows, :])
                keys = jnp.concatenate([k_ref[r, :] for r in key_rows], axis=0)
                vals = jnp.concatenate([v_ref[r, :] for r in key_rows], axis=0)
                p = jnp.where(valid, jnp.exp(_dot_nt(q2, keys) - lse2), 0.0)
                ds = p * (_dot_nt(do2, vals) - delta)
                dq_ref[rows, :] += _unstack_heads(_dot_nn(ds, keys))
                dkk = _dot_tn(ds, q2)
                dvv = _dot_tn(p, do2)
                for i, r in enumerate(key_rows):
                    dk_ref[r, :] += dkk[i * BLOCK:(i + 1) * BLOCK]
                    dv_ref[r, :] += dvv[i * BLOCK:(i + 1) * BLOCK]

            _branch_loops(s // BLOCK, d, visit, 8, masks)
        for c in range(s // MERGE_CHUNK):
            rows = slice(c * MERGE_CHUNK, (c + 1) * MERGE_CHUNK)
            cos, sin = cos_ref[rows, :], sin_ref[rows, :]
            dq, dk = dq_ref[rows, :], dk_ref[rows, :]
            dq_out[rows, :] = ((dq * cos - _partner(dq) * sin) * qscale).astype(BF16)
            dk_out[rows, :] = (dk * cos - _partner(dk) * sin).astype(BF16)
            dv_out[rows, :] = dv_ref[rows, :].astype(BF16)

    return _call(
        body, grid=(width // LANES,), in_specs=[q_spec, k_spec, v_spec, cur, cur, cur, table, table],
        out_specs=[cur, cur, cur], out_shape=[jax.ShapeDtypeStruct((s, width), BF16)] * 3,
        scratch_shapes=[pltpu.VMEM((s, LANES), F32)] * 3,
        args=(proj, proj, proj, do, o, lse, cos, sin_signed), name=name, comm=comm)


def _conv_specs(s, a_block, b_block):
    per = CONV_CHUNK // CONV_HALO
    a_cur = pl.BlockSpec((CONV_CHUNK, LANES), lambda cb, i: (i, a_block + cb))
    b_cur = pl.BlockSpec((CONV_CHUNK, LANES), lambda cb, i: (i, b_block + cb))
    a_halo = pl.BlockSpec((CONV_HALO, LANES), lambda cb, i: (jnp.maximum(i * per - 1, 0), a_block + cb))
    b_halo = pl.BlockSpec((CONV_HALO, LANES), lambda cb, i: (jnp.maximum(i * per - 1, 0), b_block + cb))
    w_spec = pl.BlockSpec((CONV_KERNEL, LANES), lambda cb, i: (0, cb))
    vec = pl.BlockSpec((1, LANES), lambda cb, i: (0, cb))
    out = pl.BlockSpec((CONV_CHUNK, LANES), lambda cb, i: (i, cb))
    return a_cur, b_cur, a_halo, b_halo, w_spec, vec, out


def _fill_glu_window(win, a_ref, b_ref, ah_ref, bh_ref, first):
    halo = ah_ref[...] * _sigmoid(bh_ref[...])
    win[0:CONV_HALO, :] = jnp.where(first, 0.0, halo)
    win[CONV_HALO:, :] = a_ref[...] * _sigmoid(b_ref[...])


def _conv_fwd(proj, a_block, b_block, w, bias, name, comm=None):
    s = proj.shape[0]
    cw = w.shape[1]
    a_cur, b_cur, a_halo, b_halo, w_spec, vec, out = _conv_specs(s, a_block, b_block)
    lead = CONV_HALO - (CONV_KERNEL - 1)

    def body(a_ref, b_ref, ah_ref, bh_ref, w_ref, bias_ref, o_ref, win):
        _fill_glu_window(win, a_ref, b_ref, ah_ref, bh_ref, pl.program_id(1) == 0)
        for sub in range(CONV_CHUNK // CONV_SUB):
            base = sub * CONV_SUB
            acc = jnp.zeros((CONV_SUB, LANES), F32) + bias_ref[...]
            for j in range(CONV_KERNEL):
                acc = acc + w_ref[j:j + 1, :] * win[base + lead + j:base + lead + j + CONV_SUB, :]
            o_ref[base:base + CONV_SUB, :] = acc

    return _call(
        body, grid=(cw // LANES, s // CONV_CHUNK), in_specs=[a_cur, b_cur, a_halo, b_halo, w_spec, vec],
        out_specs=[out], out_shape=[jax.ShapeDtypeStruct((s, cw), F32)],
        scratch_shapes=[pltpu.VMEM((CONV_CHUNK + CONV_HALO, LANES), F32)],
        args=(proj, proj, proj, proj, w, bias), name=name, comm=comm)


def _conv_bwd(proj, a_block, b_block, w, du1, name):
    s = proj.shape[0]
    cw = w.shape[1]
    a_cur, b_cur, a_halo, b_halo, w_spec, vec, out = _conv_specs(s, a_block, b_block)
    per = CONV_CHUNK // CONV_HALO
    n_chunks = s // CONV_CHUNK
    d_next = pl.BlockSpec((CONV_HALO, LANES), lambda cb, i: (jnp.minimum((i + 1) * per, s // CONV_HALO - 1), cb))
    lead = CONV_HALO - (CONV_KERNEL - 1)

    def body(a_ref, b_ref, ah_ref, bh_ref, w_ref, d_ref, dn_ref, da_ref, db_ref, dw_ref, dbias_ref, win, dwin):
        i = pl.program_id(1)
        _fill_glu_window(win, a_ref, b_ref, ah_ref, bh_ref, i == 0)
        dwin[0:CONV_CHUNK, :] = d_ref[...]
        dwin[CONV_CHUNK:, :] = jnp.where(i == n_chunks - 1, 0.0, dn_ref[...])

        @pl.when(i == 0)
        def _():
            dw_ref[...] = jnp.zeros_like(dw_ref)
            dbias_ref[...] = jnp.zeros_like(dbias_ref)

        dbias_ref[...] += _colsum(d_ref[...])
        for sub in range(CONV_CHUNK // CONV_SUB):
            base = sub * CONV_SUB
            dcur = dwin[base:base + CONV_SUB, :]
            du0 = jnp.zeros((CONV_SUB, LANES), F32)
            for j in range(CONV_KERNEL):
                back = CONV_KERNEL - 1 - j
                du0 = du0 + w_ref[j:j + 1, :] * dwin[base + back:base + back + CONV_SUB, :]
                dw_ref[j:j + 1, :] += _colsum(dcur * win[base + lead + j:base + lead + j + CONV_SUB, :])
            av = a_ref[base:base + CONV_SUB, :]
            sig = _sigmoid(b_ref[base:base + CONV_SUB, :])
            da_ref[base:base + CONV_SUB, :] = (du0 * sig).astype(BF16)
            db_ref[base:base + CONV_SUB, :] = (du0 * av * sig * (1.0 - sig)).astype(BF16)

    return pl.pallas_call(
        body, grid=(cw // LANES, n_chunks), in_specs=[a_cur, b_cur, a_halo, b_halo, w_spec, out, d_next],
        out_specs=[out, out, w_spec, vec],
        out_shape=[jax.ShapeDtypeStruct((s, cw), BF16), jax.ShapeDtypeStruct((s, cw), BF16),
                   jax.ShapeDtypeStruct((CONV_KERNEL, cw), F32), jax.ShapeDtypeStruct((1, cw), F32)],
        scratch_shapes=[pltpu.VMEM((CONV_CHUNK + CONV_HALO, LANES), F32)] * 2,
        compiler_params=_params(2), name=name)(proj, proj, proj, proj, w, du1, du1)


def _adamw_math(w, g, m, v):
    m = ADAM_B1 * m + (1.0 - ADAM_B1) * g
    v = ADAM_B2 * v + (1.0 - ADAM_B2) * (g * g)
    m_hat = m / (1.0 - ADAM_B1 ** ADAM_STEP)
    v_hat = v / (1.0 - ADAM_B2 ** ADAM_STEP)
    delta = -ADAM_LR * (m_hat / (jnp.sqrt(v_hat) + ADAM_EPS) + ADAM_WD * w)
    return delta, m, v


def _adamw_big(w, g, m, v, name):
    rows, cols = w.shape
    tile = _tile(rows, 256, 8)
    spec = pl.BlockSpec((tile, cols), lambda i: (i, 0))

    def body(w_ref, g_ref, m_ref, v_ref, d_out, m_out, v_out):
        d_out[...], m_out[...], v_out[...] = _adamw_math(w_ref[...], g_ref[...], m_ref[...], v_ref[...])

    return pl.pallas_call(body, grid=(rows // tile,), in_specs=[spec] * 4, out_specs=[spec] * 3,
                          out_shape=[jax.ShapeDtypeStruct(w.shape, F32)] * 3, compiler_params=_params(1),
                          name=name)(w, g, m, v)


def _adamw_reduced(w, land, m, v, name):
    rows, cols = w.shape
    tile = _tile(rows, 256, 16)
    spec = pl.BlockSpec((tile, cols), lambda i: (i, 0))

    def body(w_ref, l_ref, m_ref, v_ref, g_out, d_out, m_out, v_out):
        g = l_ref[0].astype(F32)
        for q in range(1, N_CHIP):
            g = g + l_ref[q].astype(F32)
        g_out[...] = g
        d_out[...], m_out[...], v_out[...] = _adamw_math(w_ref[...], g, m_ref[...], v_ref[...])

    return pl.pallas_call(body, grid=(rows // tile,),
                          in_specs=[spec, pl.BlockSpec((N_CHIP, tile, cols), lambda i: (0, i, 0)), spec, spec],
                          out_specs=[spec] * 4, out_shape=[jax.ShapeDtypeStruct(w.shape, F32)] * 4,
                          compiler_params=_params(1), name=name)(w, land, m, v)


def _adamw_small(ws, gs, ms, vs, name):
    n = len(ws)

    def body(*refs):
        ins, outs = refs[:4 * n], refs[4 * n:]
        for t in range(n):
            res = _adamw_math(ins[t][...], ins[n + t][...], ins[2 * n + t][...], ins[3 * n + t][...])
            for j in range(3):
                outs[j * n + t][...] = res[j]

    shapes = [jax.ShapeDtypeStruct(w.shape, F32) for w in ws]
    res = pl.pallas_call(body, out_shape=shapes * 3, compiler_params=pltpu.CompilerParams(vmem_limit_bytes=VMEM_LIMIT),
                         name=name)(*ws, *gs, *ms, *vs)
    return res[:n], res[n:2 * n], res[2 * n:]


def _sum_blocks(x, n_blocks, name):
    r = x.shape[0] // n_blocks

    def body(x_ref, o_ref):
        acc = x_ref[0:r, :]
        for b in range(1, n_blocks):
            acc = acc + x_ref[b * r:(b + 1) * r, :]
        o_ref[...] = acc

    return pl.pallas_call(body, out_shape=jax.ShapeDtypeStruct((r, x.shape[1]), F32),
                          compiler_params=pltpu.CompilerParams(vmem_limit_bytes=VMEM_LIMIT), name=name)(x)


def _coords():
    return lax.axis_index("x"), lax.axis_index("y"), lax.axis_index("c")


def _flip(v, bit):
    return 1 - v if bit else v


def _ag_small(x, name):
    r, c = x.shape

    def body(x_ref, o_ref, send, recv, local_sem):
        mx, my, mc = _coords()

        def rows(px, py, pc):
            return o_ref.at[pl.ds(pl.multiple_of((4 * px + 2 * py + pc) * r, 8), r), :]

        local = pltpu.make_async_copy(x_ref, rows(mx, my, mc), local_sem)
        local.start()
        peers = [(_flip(mx, k >> 2 & 1), _flip(my, k >> 1 & 1), _flip(mc, k & 1)) for k in range(1, N_DEV)]
        sends = [pltpu.make_async_remote_copy(x_ref, rows(mx, my, mc), send.at[k], recv.at[k], device_id=p,
                                              device_id_type=MESH) for k, p in enumerate(peers)]
        for cp in sends:
            cp.start()
        for k, p in enumerate(peers):
            pltpu.make_async_remote_copy(x_ref, rows(*p), send.at[k], recv.at[k], device_id=p,
                                         device_id_type=MESH).wait_recv()
        for cp in sends:
            cp.wait_send()
        local.wait()

    vm = pl.BlockSpec(memory_space=pltpu.VMEM)
    return pl.pallas_call(
        body, in_specs=[vm], out_specs=vm, out_shape=jax.ShapeDtypeStruct((N_DEV * r, c), x.dtype),
        scratch_shapes=[pltpu.SemaphoreType.DMA((N_DEV - 1,)), pltpu.SemaphoreType.DMA((N_DEV - 1,)),
                        pltpu.SemaphoreType.DMA(())],
        name=name)(x)


class _GatherSmall:
    mid = None

    def __init__(self, x):
        self.inputs = [x]
        self.out_shapes = [jax.ShapeDtypeStruct((N_DEV * x.shape[0], x.shape[1]), x.dtype)]
        self.scratch = [pltpu.SemaphoreType.DMA((N_DEV - 1,)), pltpu.SemaphoreType.DMA((N_DEV - 1,)),
                        pltpu.SemaphoreType.DMA(())]

    def _plan(self, x_refs, o_refs, sems):
        send, recv, local_sem = sems
        x_ref, o_ref = x_refs[0], o_refs[0]
        r = x_ref.shape[0]
        mx, my, mc = _coords()

        def rows(px, py, pc):
            return o_ref.at[pl.ds(pl.multiple_of((4 * px + 2 * py + pc) * r, 8), r), :]

        peers = [(_flip(mx, k >> 2 & 1), _flip(my, k >> 1 & 1), _flip(mc, k & 1)) for k in range(1, N_DEV)]
        out = [pltpu.make_async_remote_copy(x_ref, rows(mx, my, mc), send.at[k], recv.at[k], device_id=p,
                                            device_id_type=MESH) for k, p in enumerate(peers)]
        arrivals = [pltpu.make_async_remote_copy(x_ref, rows(*p), send.at[k], recv.at[k], device_id=p,
                                                 device_id_type=MESH) for k, p in enumerate(peers)]
        return out, arrivals, pltpu.make_async_copy(x_ref, rows(mx, my, mc), local_sem)

    def start(self, x_refs, o_refs, sems):
        out, _, local = self._plan(x_refs, o_refs, sems)
        local.start()
        for cp in out:
            cp.start()

    def finish(self, x_refs, o_refs, sems):
        out, arrivals, local = self._plan(x_refs, o_refs, sems)
        for cp in arrivals:
            cp.wait_recv()
        for cp in out:
            cp.wait_send()
        local.wait()


class _ModExchange:
    def __init__(self, first, w_ada):
        self.d, cols = w_ada.shape
        part = jax.ShapeDtypeStruct((N_DEV, cols), F32)
        self.g1, self.g2 = _GatherSmall(first), _GatherSmall(part)
        self.inputs = [first, w_ada]
        self.out_shapes = [self.g1.out_shapes[0], jax.ShapeDtypeStruct((N_DEV, self.d), F32), part,
                           self.g2.out_shapes[0]]
        self.scratch = self.g1.scratch + self.g2.scratch + [
            pltpu.VMEM(self.g1.out_shapes[0].shape, F32), pltpu.VMEM(w_ada.shape, F32),
            pltpu.VMEM((N_DEV, self.d), F32), pltpu.VMEM((N_DEV, cols), F32), pltpu.SemaphoreType.DMA(())]

    def start(self, cin, cout, scr):
        self.g1.start(cin[0:1], cout[0:1], scr[0:3])
        pltpu.make_async_copy(cin[1], scr[7], scr[10]).start()

    def mid(self, cin, cout, scr):
        gathered, w_v, silu_v, part_v = scr[6:10]
        self.g1.finish(cin[0:1], cout[0:1], scr[0:3])
        pltpu.sync_copy(cout[0], gathered)
        rows_per = cin[0].shape[0]
        for j in range(N_DEV):
            silu_v[j:j + 1, :] = gathered[j * rows_per:j * rows_per + 1, 0:self.d]
        c_all = silu_v[...]
        silu_v[...] = c_all * _sigmoid(c_all)
        pltpu.sync_copy(silu_v, cout[1])
        pltpu.make_async_copy(cin[1], w_v, scr[10]).wait()
        part_v[...] = _dot_nn(silu_v[...], w_v[...])
        pltpu.sync_copy(part_v, cout[2])
        self.g2.start(cout[2:3], cout[3:4], scr[3:6])

    def finish(self, cin, cout, scr):
        self.g2.finish(cout[2:3], cout[3:4], scr[3:6])


class _GatherWeights:
    def __init__(self, shards):
        n_t = len(shards)
        self.inputs = list(shards)
        self.out_shapes = [jax.ShapeDtypeStruct((N_DEV * x.shape[0], x.shape[1]), x.dtype) for x in shards]
        self.scratch = [pltpu.SemaphoreType.DMA((n_t, 8)), pltpu.SemaphoreType.DMA((n_t, 8)),
                        pltpu.SemaphoreType.DMA((n_t,))]

    def _plan(self, x_refs, o_refs, sems):
        send, recv, local_sem = sems
        mx, my, mc = _coords()
        me, sibling = (mx, my, mc), (mx, my, 1 - mc)
        xn, yn, diag = (1 - mx, my), (mx, 1 - my), (1 - mx, 1 - my)

        def rows(t, chip, core, half=None):
            r = x_refs[t].shape[0]
            base = (4 * chip[0] + 2 * chip[1] + core) * r
            if half is None:
                return o_refs[t].at[pl.ds(pl.multiple_of(base, 8), r), :]
            return o_refs[t].at[pl.ds(pl.multiple_of(base + half * (r // 2), 8), r // 2), :]

        def copy(t, k, block, to, src=None):
            return pltpu.make_async_remote_copy(
                src_ref=block if src is None else src, dst_ref=block,
                send_sem=send.at[t, k], recv_sem=recv.at[t, k], device_id=to, device_id_type=MESH)

        def local(t):
            return pltpu.make_async_copy(x_refs[t], rows(t, (mx, my), mc), local_sem.at[t])

        return (mx, my), mc, me, sibling, xn, yn, diag, rows, copy, local

    def start(self, x_refs, o_refs, sems):
        chip, mc, me, sibling, xn, yn, diag, rows, copy, local = self._plan(x_refs, o_refs, sems)
        for t in range(len(x_refs)):
            mine = rows(t, chip, mc)
            local(t).start()
            copy(t, 0, mine, sibling, src=x_refs[t]).start()
            copy(t, 1, mine, (*xn, mc), src=x_refs[t]).start()
            copy(t, 2, mine, (*yn, mc), src=x_refs[t]).start()

    def mid(self, x_refs, o_refs, sems):
        chip, mc, me, sibling, xn, yn, diag, rows, copy, local = self._plan(x_refs, o_refs, sems)
        for t in range(len(x_refs)):
            copy(t, 1, rows(t, xn, mc), me).wait_recv()
            copy(t, 3, rows(t, xn, mc, 0), (*yn, mc)).start()
            copy(t, 5, rows(t, xn, mc), sibling).start()
        for t in range(len(x_refs)):
            copy(t, 2, rows(t, yn, mc), me).wait_recv()
            copy(t, 4, rows(t, yn, mc, 1), (*xn, mc)).start()
            copy(t, 6, rows(t, yn, mc), sibling).start()

    def finish(self, x_refs, o_refs, sems):
        chip, mc, me, sibling, xn, yn, diag, rows, copy, local = self._plan(x_refs, o_refs, sems)
        for t in range(len(x_refs)):
            copy(t, 3, rows(t, diag, mc, 0), me).wait_recv()
            copy(t, 4, rows(t, diag, mc, 1), me).wait_recv()
            copy(t, 7, rows(t, diag, mc), sibling).start()
        for t in range(len(x_refs)):
            copy(t, 0, rows(t, chip, 1 - mc), me).wait_recv()
            copy(t, 5, rows(t, xn, 1 - mc), me).wait_recv()
            copy(t, 6, rows(t, yn, 1 - mc), me).wait_recv()
            copy(t, 7, rows(t, diag, 1 - mc), me).wait_recv()
            mine = rows(t, chip, mc)
            copy(t, 0, mine, sibling, src=x_refs[t]).wait_send()
            copy(t, 1, mine, (*xn, mc), src=x_refs[t]).wait_send()
            copy(t, 2, mine, (*yn, mc), src=x_refs[t]).wait_send()
            copy(t, 3, rows(t, xn, mc, 0), (*yn, mc)).wait_send()
            copy(t, 4, rows(t, yn, mc, 1), (*xn, mc)).wait_send()
            copy(t, 5, rows(t, xn, mc), sibling).wait_send()
            copy(t, 6, rows(t, yn, mc), sibling).wait_send()
            copy(t, 7, rows(t, diag, mc), sibling).wait_send()
            local(t).wait()


class _SiblingExchange:
    mid = None

    def __init__(self, grads):
        n_t = len(grads)
        self.inputs = list(grads)
        self.out_shapes = [jax.ShapeDtypeStruct((N_CHIP,) + g.shape[2:], F32) for g in grads]
        self.scratch = [pltpu.SemaphoreType.DMA((n_t,)), pltpu.SemaphoreType.DMA((n_t,))]

    def _copies(self, g_refs, land, sems):
        send, recv = sems
        mx, my, mc = _coords()
        return [pltpu.make_async_remote_copy(g_refs[t].at[:, 1 - mc], land[t], send.at[t], recv.at[t],
                                             device_id=(mx, my, 1 - mc), device_id_type=MESH)
                for t in range(len(g_refs))]

    def start(self, g_refs, land, sems):
        for cp in self._copies(g_refs, land, sems):
            cp.start()

    def finish(self, g_refs, land, sems):
        for cp in self._copies(g_refs, land, sems):
            cp.wait()


class _Together:
    def __init__(self, *comms):
        self.comms = comms
        self.inputs = [x for c in comms for x in c.inputs]
        self.out_shapes = [x for c in comms for x in c.out_shapes]
        self.scratch = [x for c in comms for x in c.scratch]
        self.mid = self._mid if any(c.mid is not None for c in comms) else None

    def _each(self, phase, cin, cout, sems):
        i = o = s = 0
        for c in self.comms:
            fn = getattr(c, phase)
            ni, no, ns = len(c.inputs), len(c.out_shapes), len(c.scratch)
            if fn is not None:
                fn(cin[i:i + ni], cout[o:o + no], sems[s:s + ns])
            i, o, s = i + ni, o + no, s + ns

    def start(self, cin, cout, sems):
        self._each("start", cin, cout, sems)

    def _mid(self, cin, cout, sems):
        self._each("mid", cin, cout, sems)

    def finish(self, cin, cout, sems):
        self._each("finish", cin, cout, sems)


def _standalone(comm, name):
    def body():
        pass
    return _call(body, grid=(1,), in_specs=[], out_specs=[], out_shape=[], args=(), name=name, comm=comm)[1]


def _chip_partials(g4s, lands, name):
    n_t = len(g4s)
    in_specs, out_specs, out_shape = [], [], []
    for g4 in g4s:
        _, _, r, c = g4.shape
        in_specs.append(pl.BlockSpec((None, None, r, c), lambda q: (q, lax.axis_index("c"), 0, 0)))
        out_specs.append(pl.BlockSpec((None, r, c), lambda q: (q, 0, 0)))
        out_shape.append(jax.ShapeDtypeStruct((N_CHIP, r, c), BF16))
    in_specs += [pl.BlockSpec((None,) + g4.shape[2:], lambda q: (q, 0, 0)) for g4 in g4s]

    def body(*refs):
        for t in range(n_t):
            refs[2 * n_t + t][...] = (refs[t][...] + refs[n_t + t][...]).astype(BF16)

    return pl.pallas_call(body, grid=(N_CHIP,), in_specs=in_specs, out_specs=out_specs, out_shape=out_shape,
                          compiler_params=_params(1), name=name)(*g4s, *lands)


class _ChipExchange:
    mid = None

    def __init__(self, parts):
        n_t = len(parts)
        self.inputs = list(parts)
        self.out_shapes = [jax.ShapeDtypeStruct(p.shape, p.dtype) for p in parts]
        self.scratch = [pltpu.SemaphoreType.DMA((n_t, 3)), pltpu.SemaphoreType.DMA((n_t, 3)),
                        pltpu.SemaphoreType.DMA((n_t,))]

    def _plan(self, p_refs, land, sems):
        send, recv, local_sem = sems
        mx, my, mc = _coords()
        my_chip = 2 * mx + my
        peers = [(_flip(mx, fx), _flip(my, fy)) for fx, fy in ((1, 0), (0, 1), (1, 1))]

        def out(t, k):
            px, py = peers[k]
            return pltpu.make_async_remote_copy(p_refs[t].at[2 * px + py], land[t].at[my_chip], send.at[t, k],
                                                recv.at[t, k], device_id=(px, py, mc), device_id_type=MESH)

        def arrival(t, k):
            px, py = peers[k]
            return pltpu.make_async_remote_copy(p_refs[t].at[my_chip], land[t].at[2 * px + py], send.at[t, k],
                                                recv.at[t, k], device_id=(px, py, mc), device_id_type=MESH)

        def local(t):
            return pltpu.make_async_copy(p_refs[t].at[my_chip], land[t].at[my_chip], local_sem.at[t])

        return out, arrival, local

    def start(self, p_refs, land, sems):
        out, arrival, local = self._plan(p_refs, land, sems)
        for t in range(len(p_refs)):
            local(t).start()
            for k in range(3):
                out(t, k).start()

    def finish(self, p_refs, land, sems):
        out, arrival, local = self._plan(p_refs, land, sems)
        for t in range(len(p_refs)):
            for k in range(3):
                arrival(t, k).wait_recv()
                out(t, k).wait_send()
            local(t).wait()


def _rope_tables(s, width):
    heads = width // HEAD_DIM
    inv_freq = ROPE_THETA ** (-jnp.arange(0, HEAD_DIM, 2, dtype=F32) / HEAD_DIM)
    inv_full = jnp.tile(inv_freq, 2 * heads)
    sign = jnp.tile(jnp.concatenate([-jnp.ones((HALF_HEAD,), F32), jnp.ones((HALF_HEAD,), F32)]), heads)
    ang = jnp.arange(s, dtype=F32)[:, None] * inv_full[None, :]
    return jnp.cos(ang), jnp.sin(ang) * sign[None, :]


def _pad_rows(v, rows):
    return jnp.concatenate([v, jnp.zeros((rows - 1, v.shape[1]), v.dtype)], axis=0)


def kernel(x, c, w_ada, b_ada, ffn1_norm_g, ffn1_w_gate, ffn1_w_up, ffn1_w_down, mix_norm_g, w_in, conv_dw_w, conv_dw_b, conv_ln_g, conv_ln_b, attn_out_g, conv_out_g, w_out, ffn2_norm_g, ffn2_w_gate, ffn2_w_up, ffn2_w_down, final_norm_g, loss_target, m_w_ada, m_b_ada, m_ffn1_norm_g, m_ffn1_w_gate, m_ffn1_w_up, m_ffn1_w_down, m_mix_norm_g, m_w_in, m_conv_dw_w, m_conv_dw_b, m_conv_ln_g, m_conv_ln_b, m_attn_out_g, m_conv_out_g, m_w_out, m_ffn2_norm_g, m_ffn2_w_gate, m_ffn2_w_up, m_ffn2_w_down, m_final_norm_g, v_w_ada, v_b_ada, v_ffn1_norm_g, v_ffn1_w_gate, v_ffn1_w_up, v_ffn1_w_down, v_mix_norm_g, v_w_in, v_conv_dw_w, v_conv_dw_b, v_conv_ln_g, v_conv_ln_b, v_attn_out_g, v_conv_out_g, v_w_out, v_ffn2_norm_g, v_ffn2_w_gate, v_ffn2_w_up, v_ffn2_w_down, v_final_norm_g):
    mx, my, mc = _coords()
    me = 4 * mx + 2 * my + mc
    s, d = x.shape[1], x.shape[2]
    aw = d // 2
    x2, target = x[0], loss_target[0]
    n_mod = w_ada.shape[2] * N_DEV // d
    mod_cols = w_ada.shape[2]

    def shard(w, transpose):
        return (w[0].T if transpose else w[0]).astype(BF16)

    cw_shard = conv_dw_w.shape[3]
    n_taps = CONV_KERNEL * cw_shard
    first_len = -(-(d + n_taps) // LANES) * LANES
    first = jnp.concatenate([c, conv_dw_w[0, :, 0, :].reshape(1, n_taps), jnp.zeros((1, first_len - d - n_taps), F32)], axis=1)
    first_all, silu_c, _, mod_all, wg1, wu1 = _standalone(
        _Together(_ModExchange(_pad_rows(first, 8), w_ada[0]),
                  _GatherWeights([shard(ffn1_w_gate, True), shard(ffn1_w_up, True)])), "ag_first")
    first_all = first_all[0::8]
    conv_w = first_all[:, d:d + n_taps].reshape(N_DEV, CONV_KERNEL, cw_shard).transpose(1, 0, 2).reshape(CONV_KERNEL, aw)

    mod_all = mod_all.reshape(N_DEV, N_DEV, mod_cols)
    mod = lax.dynamic_index_in_dim(mod_all, me, axis=1, keepdims=False).reshape(1, n_mod * d) + b_ada
    sh1, sc1, g1, sh2, sc2, g2, sh3, sc3, g3 = [mod[:, i * d:(i + 1) * d] for i in range(n_mod)]

    def split(g):
        return g.reshape(N_CHIP, 2, g.shape[0] // N_DEV, g.shape[1])

    def partials(g4s, lands, tag):
        return _chip_partials(g4s, lands, "chip_partials_" + tag)

    (n1, silu1, gs1, hid1), (wd1, win_t, wout) = _norm_ffn_up(
        x2, ffn1_norm_g, sc1, sh1, wg1, wu1, "ffn1_up",
        comm=_GatherWeights([shard(ffn1_w_down, False), shard(w_in, True), shard(w_out, False)]))
    h1, f1, n2 = _residual_mm(hid1, wd1, x2, g1, 0.5, "ffn1_down", norm=(mix_norm_g, sc2, sh2))
    cos, sin_signed = _rope_tables(s, LANES)
    proj, = _proj_rope(n2, win_t, cos, sin_signed, aw, "proj")
    lanes_per = aw // LANES
    (attn, lse), (wg2, wu2, wd2) = _attn_seq_fwd(
        proj, aw, "attn_fwd",
        comm=_GatherWeights([shard(ffn2_w_gate, True), shard(ffn2_w_up, True), shard(ffn2_w_down, False)]))
    u1, = _conv_fwd(proj, 3 * lanes_per, 4 * lanes_per, conv_w, conv_dw_b, "conv_fwd")
    post = (attn_out_g, conv_ln_g, conv_ln_b, conv_out_g)
    y, h2, mix, n3 = _mix_out(attn, u1, post, wout, h1, g2, (ffn2_norm_g, sc3, sh3), "mix_out")
    silu3, gs3, hid3 = _ffn_up(n3, wg2, wu2, "ffn2_up")

    dh3, df3, err2, d_final_g, dg3 = _last_mm_loss(hid3, wd2, h2, g3, 0.5, target, final_norm_g.reshape(1, d),
                                                   "ffn2_down_loss")
    loss_part = jnp.zeros((1, LANES), F32).at[0, 0].set(0.5 * jnp.sum(err2) / d)

    da3, db3 = _ffn_bwd_hidden(df3, wd2, silu3, gs3, "ffn2_hidden_bwd")
    g4_a = [split(_mm_tn(da3, n3, "ffn2_dwg")), split(_mm_tn(db3, n3, "ffn2_dwu")), split(_mm_tn(hid3, df3, "ffn2_dwd"))]
    (dh2, dmix, dsh3, dsc3, dgn3, dg2), land_a = _mm_norm_mod_bwd(
        [(da3, wg2), (db3, wu2)], h2, dh3, ffn2_norm_g, sc3, (mix, g2, 1.0), "ffn2_dn_norm3_bwd", tm=256,
        comm=_SiblingExchange(g4_a))
    parts_a = partials(g4_a, land_a, "a")
    g_wout = _mm_tn(y, dmix, "mix_dwout")
    dattn, du1, d_gains, d_ln = _mix_dy_post_bwd(dmix, wout, attn, u1, post, "mix_dy_post_bwd")
    d_attn_g, d_conv_g, d_ln_g, d_ln_b = d_gains[:, :aw], d_gains[:, aw:], d_ln[:, :aw], d_ln[:, aw:]
    dga, dgb, d_taps, d_conv_b = _conv_bwd(proj, 3 * lanes_per, 4 * lanes_per, conv_w, du1, "conv_bwd")
    (dq, dk, dv), sums_a = _attn_seq_bwd(proj, dattn, attn, lse, cos, sin_signed, "attn_bwd",
                                         comm=_ChipExchange(parts_a))
    dproj = jnp.concatenate([dq, dk, dv, dga, dgb], axis=1)
    g4_b = [split(g_wout), split(_mm_tn(dproj, n2, "mix_dwin"))]
    (dh1, df1, dsh2, dsc2, dgn2, dg1), land_b = _mm_norm_mod_bwd(
        [(dproj, win_t)], h1, dh2, mix_norm_g, sc2, (f1, g1, 0.5), "mix_dn_norm2_bwd", tm=512,
        comm=_SiblingExchange(g4_b))
    parts_b = partials(g4_b, land_b, "b")
    g4_c = [split(_mm_tn(hid1, df1, "ffn1_dwd"))]
    (da1, db1), both = _ffn_bwd_hidden(df1, wd1, silu1, gs1, "ffn1_hidden_bwd",
                                       comm=_Together(_ChipExchange(parts_b), _SiblingExchange(g4_c)))
    sums_b, land_c = both[:2], both[2:]
    parts_c = partials(g4_c, land_c, "c")
    g4_d = [split(_mm_tn(db1, n1, "ffn1_dwu"))]
    g_wg1, both = _mm_tn(da1, n1, "ffn1_dwg", comm=_Together(_ChipExchange(parts_c), _SiblingExchange(g4_d)))
    sums_c, land_d = both[:1], both[1:]
    parts_d = partials(g4_d, land_d, "d")
    g4_e = [split(g_wg1)]
    dn1, both = _plain_mm([(da1, wg1), (db1, wu1)], BF16, False, d, "ffn1_dn",
                          comm=_Together(_ChipExchange(parts_d), _SiblingExchange(g4_e)))
    sums_d, land_e = both[:1], both[1:]
    parts_e = partials(g4_e, land_e, "e")
    (dx, dsh1, dsc1, dgn1), sums_e = _norm_mod_bwd(dn1, x2, dh1, ffn1_norm_g, sc1, "norm1_bwd",
                                                   comm=_ChipExchange(parts_e))

    dmod = jnp.concatenate([dsh1, dsc1, dg1, dsh2, dsc2, dg2, dsh3, dsc3, dg3], axis=1)
    small = [dmod, dgn1, dgn2, dgn3, d_final_g, d_conv_b, d_ln_g, d_ln_b, d_attn_g, d_conv_g,
             d_taps.reshape(1, CONV_KERNEL * aw), loss_part]
    sizes = [v.shape[1] for v in small]
    total = sum(sizes)
    padded = -(-total // (8 * LANES)) * (8 * LANES)
    packed = jnp.concatenate(small + [jnp.zeros((1, padded - total), F32)], axis=1).reshape(8, padded // 8)
    gathered = _ag_small(packed, "ag_small_grads")
    summed = _sum_blocks(gathered, N_DEV, "sum_small_grads").reshape(1, padded)
    offs = [sum(sizes[:i]) for i in range(len(sizes))]
    (g_b_ada, g_gn1, g_gn2, g_gn3, g_final, g_conv_b, g_ln_g, g_ln_b, g_attn_g, g_conv_g, g_taps, loss_row) = [
        summed[:, o:o + n] for o, n in zip(offs, sizes)]
    loss = loss_row[0, 0]
    g_taps_shard = lax.dynamic_slice_in_dim(g_taps.reshape(CONV_KERNEL, aw), me * cw_shard, cw_shard, axis=1)
    dmod_all = gathered.reshape(N_DEV, padded)[:, :n_mod * d]
    dmod_cols = lax.dynamic_slice_in_dim(dmod_all, me * mod_cols, mod_cols, axis=1)
    g_w_ada = _mm_tn(silu_c, dmod_cols, "ada_dw")

    arrived = dict(zip(["ffn2_w_gate", "ffn2_w_up", "ffn2_w_down", "w_out", "w_in", "ffn1_w_down", "ffn1_w_up",
                        "ffn1_w_gate"], list(sums_a) + list(sums_b) + list(sums_c) + list(sums_d) + list(sums_e)))
    transposed = ("ffn1_w_gate", "ffn1_w_up", "w_in", "ffn2_w_gate", "ffn2_w_up")
    grads = {
        "w_ada": g_w_ada, "b_ada": g_b_ada, "ffn1_norm_g": g_gn1, "mix_norm_g": g_gn2, "conv_dw_w": g_taps_shard,
        "conv_dw_b": g_conv_b, "conv_ln_g": g_ln_g, "conv_ln_b": g_ln_b, "attn_out_g": g_attn_g,
        "conv_out_g": g_conv_g, "ffn2_norm_g": g_gn3, "final_norm_g": g_final,
    }
    weights = dict(w_ada=w_ada, b_ada=b_ada, ffn1_norm_g=ffn1_norm_g, ffn1_w_gate=ffn1_w_gate, ffn1_w_up=ffn1_w_up, ffn1_w_down=ffn1_w_down, mix_norm_g=mix_norm_g, w_in=w_in, conv_dw_w=conv_dw_w, conv_dw_b=conv_dw_b, conv_ln_g=conv_ln_g, conv_ln_b=conv_ln_b, attn_out_g=attn_out_g, conv_out_g=conv_out_g, w_out=w_out, ffn2_norm_g=ffn2_norm_g, ffn2_w_gate=ffn2_w_gate, ffn2_w_up=ffn2_w_up, ffn2_w_down=ffn2_w_down, final_norm_g=final_norm_g)
    moms = dict(w_ada=m_w_ada, b_ada=m_b_ada, ffn1_norm_g=m_ffn1_norm_g, ffn1_w_gate=m_ffn1_w_gate, ffn1_w_up=m_ffn1_w_up, ffn1_w_down=m_ffn1_w_down, mix_norm_g=m_mix_norm_g, w_in=m_w_in, conv_dw_w=m_conv_dw_w, conv_dw_b=m_conv_dw_b, conv_ln_g=m_conv_ln_g, conv_ln_b=m_conv_ln_b, attn_out_g=m_attn_out_g, conv_out_g=m_conv_out_g, w_out=m_w_out, ffn2_norm_g=m_ffn2_norm_g, ffn2_w_gate=m_ffn2_w_gate, ffn2_w_up=m_ffn2_w_up, ffn2_w_down=m_ffn2_w_down, final_norm_g=m_final_norm_g)
    vars_ = dict(w_ada=v_w_ada, b_ada=v_b_ada, ffn1_norm_g=v_ffn1_norm_g, ffn1_w_gate=v_ffn1_w_gate, ffn1_w_up=v_ffn1_w_up, ffn1_w_down=v_ffn1_w_down, mix_norm_g=v_mix_norm_g, w_in=v_w_in, conv_dw_w=v_conv_dw_w, conv_dw_b=v_conv_dw_b, conv_ln_g=v_conv_ln_g, conv_ln_b=v_conv_ln_b, attn_out_g=v_attn_out_g, conv_out_g=v_conv_out_g, w_out=v_w_out, ffn2_norm_g=v_ffn2_norm_g, ffn2_w_gate=v_ffn2_w_gate, ffn2_w_up=v_ffn2_w_up, ffn2_w_down=v_ffn2_w_down, final_norm_g=v_final_norm_g)
    names = list(weights)
    big = ["w_ada", "ffn1_w_gate", "ffn1_w_up", "ffn1_w_down", "w_in", "w_out", "ffn2_w_gate", "ffn2_w_up",
           "ffn2_w_down"]
    shape2 = {n: (weights[n].shape[-2] if weights[n].ndim > 1 else 1, weights[n].shape[-1]) for n in names}
    shape2["conv_dw_w"] = (CONV_KERNEL, cw_shard)
    g_out, d_out, m_out, v_out = {}, {}, {}, {}
    for n in big:
        if n in arrived:
            def view(t, n=n):
                return t[0].T if n in transposed else t[0]
            res = _adamw_reduced(view(weights[n]), arrived[n], view(moms[n]), view(vars_[n]), "adamw_" + n)
            g_out[n], d_out[n], m_out[n], v_out[n] = [r.T if n in transposed else r for r in res]
        else:
            g2d = grads[n].reshape(shape2[n])
            res = _adamw_big(weights[n].reshape(shape2[n]), g2d, moms[n].reshape(shape2[n]),
                             vars_[n].reshape(shape2[n]), "adamw_" + n)
            g_out[n], (d_out[n], m_out[n], v_out[n]) = g2d, res
    rest = [n for n in names if n not in big]
    res = _adamw_small([weights[n].reshape(shape2[n]) for n in rest], [grads[n].reshape(shape2[n]) for n in rest],
                       [moms[n].reshape(shape2[n]) for n in rest], [vars_[n].reshape(shape2[n]) for n in rest],
                       "adamw_small")
    for i, n in enumerate(rest):
        g_out[n], d_out[n], m_out[n], v_out[n] = grads[n], res[0][i], res[1][i], res[2][i]

    def shaped(table):
        return [table[n].reshape(weights[n].shape) for n in names]

    return (loss, dx.reshape(x.shape), *shaped(g_out), *shaped(d_out), *shaped(m_out), *shaped(v_out))
```

```python
import functools

import jax
import jax.numpy as jnp
from jax import lax
from jax.experimental import pallas as pl
from jax.experimental.pallas import tpu as pltpu

F32 = jnp.float32
BF16 = jnp.bfloat16
MESH = pl.DeviceIdType.MESH
ANY = pl.BlockSpec(memory_space=pl.ANY)

N_DEV = 8
N_CHIP = 4
HEAD_DIM = 64
HALF_HEAD = HEAD_DIM // 2
LANES = 128
BLOCK = 128
DILATIONS = (1, 4, 16)
MERGE_CHUNK = 512
ROPE_THETA = 10000.0
CONV_KERNEL = 31
CONV_HALO = 32
CONV_CHUNK = 512
CONV_SUB = 128
RMS_EPS = 1e-6
LN_EPS = 1e-5
ADAM_LR = 0.001
ADAM_B1 = 0.9
ADAM_B2 = 0.999
ADAM_EPS = 1e-08
ADAM_WD = 0.01
ADAM_STEP = 10
VMEM_LIMIT = 56 * 1024 * 1024
NEG = -1e30


def _params(n_axes):
    return pltpu.CompilerParams(dimension_semantics=("arbitrary",) * n_axes, vmem_limit_bytes=VMEM_LIMIT)


def _tile(n, target, unit):
    best = None
    for t in range(unit, min(n, target) + 1, unit):
        if n % t == 0:
            best = t
    return best if best is not None else n


def _sigmoid(x):
    return 0.5 * (jnp.tanh(0.5 * x) + 1.0)


def _call(body, *, grid, in_specs, out_specs, out_shape, args, name, scratch_shapes=(), comm=None):
    params = _params(len(grid))
    if comm is None:
        return pl.pallas_call(body, grid=grid, in_specs=list(in_specs), out_specs=list(out_specs),
                              out_shape=list(out_shape), scratch_shapes=list(scratch_shapes),
                              compiler_params=params, name=name)(*args)
    n_in, n_out, n_scr = len(args), len(out_shape), len(scratch_shapes)
    c_in, c_out = len(comm.inputs), len(comm.out_shapes)
    steps = 1
    for g in grid:
        steps *= g

    def hosted(*refs):
        pos = 0
        parts = []
        for size in (n_in, c_in, n_out, c_out, n_scr, len(comm.scratch)):
            parts.append(refs[pos:pos + size])
            pos += size
        ins, cin, outs, cout, scr, cscr = parts
        step = 0
        for axis, g in enumerate(grid):
            step = step * g + pl.program_id(axis)

        @pl.when(step == 0)
        def _():
            comm.start(cin, cout, cscr)

        body(*ins, *outs, *scr)
        if comm.mid is not None and steps >= 4:
            @pl.when(step == steps // 2)
            def _():
                comm.mid(cin, cout, cscr)

        @pl.when(step == steps - 1)
        def _():
            if comm.mid is not None and steps < 4:
                comm.mid(cin, cout, cscr)
            comm.finish(cin, cout, cscr)

    res = pl.pallas_call(
        hosted, grid=grid, in_specs=list(in_specs) + [ANY] * c_in, out_specs=list(out_specs) + [ANY] * c_out,
        out_shape=list(out_shape) + list(comm.out_shapes), scratch_shapes=list(scratch_shapes) + list(comm.scratch),
        compiler_params=params, name=name)(*args, *comm.inputs)
    return res[:n_out], res[n_out:]


def _rows(fn, rows_in, vecs_in, rows_out, vecs_out, *, tile, name, comm=None):
    norm = [r if isinstance(r, tuple) else (r, r.shape[1], 0) for r in rows_in]
    n_rows = norm[0][0].shape[0]
    n_tiles = n_rows // tile
    in_specs, args = [], []
    for arr, width, cb in norm:
        in_specs.append(pl.BlockSpec((tile, width), functools.partial(lambda i, cb: (i, cb), cb=cb)))
        args.append(arr)
    for v in vecs_in:
        in_specs.append(pl.BlockSpec((1, v.shape[1]), lambda i: (0, 0)))
        args.append(v)
    out_shape = [jax.ShapeDtypeStruct((n_rows, w), dt) for w, dt in rows_out]
    out_shape += [jax.ShapeDtypeStruct((1, w), F32) for w in vecs_out]
    out_specs = [pl.BlockSpec((tile, w), lambda i: (i, 0)) for w, _ in rows_out]
    out_specs += [pl.BlockSpec((1, w), lambda i: (0, 0)) for w in vecs_out]
    n_in, n_ro = len(args), len(rows_out)

    def body(*refs):
        vals = [r[...] for r in refs[:n_in]]
        outs = refs[n_in:]
        row_vals, vec_vals = fn(*vals)
        for ref, val in zip(outs[:n_ro], row_vals):
            if isinstance(val, tuple):
                w = val[0].shape[1]
                for j, piece in enumerate(val):
                    ref[:, j * w:(j + 1) * w] = piece.astype(ref.dtype)
            else:
                ref[...] = val.astype(ref.dtype)
        if vecs_out:
            @pl.when(pl.program_id(0) == 0)
            def _():
                for ref in outs[n_ro:]:
                    ref[...] = jnp.zeros_like(ref)
            for ref, val in zip(outs[n_ro:], vec_vals):
                ref[...] += val

    return _call(body, grid=(n_tiles,), in_specs=in_specs, out_specs=out_specs, out_shape=out_shape, args=args,
                 name=name, comm=comm)


def _colsum(x):
    return jnp.sum(x, axis=0, keepdims=True)


def _rms_stats(h):
    r = lax.rsqrt(jnp.mean(h * h, axis=-1, keepdims=True) + RMS_EPS)
    return r, h * r


def _rms_back(r, xn, dxn):
    return r * (dxn - xn * jnp.mean(dxn * xn, axis=-1, keepdims=True))


def _branch_back(dh, f, gate, coef):
    return (coef * gate) * dh, coef * _colsum(f.astype(F32) * dh)


def _norm_mod_back(dn, h, dh_in, gain, scale):
    dn = dn.astype(F32)
    r, xn = _rms_stats(h)
    y = xn * gain
    dy = dn * (1.0 + scale)
    dh = dh_in + _rms_back(r, xn, dy * gain)
    return dh, [_colsum(dn), _colsum(dn * y), _colsum(dy * xn)]


def _norm_mod_bwd(dn, h, dh_in, gain, scale, name, comm=None):
    d = h.shape[1]

    def fn(dn, h, dh_in, gain, scale):
        dh, vecs = _norm_mod_back(dn, h, dh_in, gain, scale)
        return [dh], vecs
    return _rows(fn, [dn, h, dh_in], [gain, scale], [(d, F32)], [d, d, d], tile=256, name=name, comm=comm)


def _mm_norm_mod_bwd(pairs, h, dh_in, gain, scale, branch, name, tm, comm=None):
    f, gate, coef = branch

    def epi(accs, ex, vc):
        dh, vecs = _norm_mod_back(accs[0], ex[0], ex[1], vc[0], vc[1])
        df, dgate = _branch_back(dh, ex[2], vc[2], coef)
        return [dh, df] + vecs + [dgate]
    return _mm([pairs], epi, [h, dh_in, f], [gain, scale, gate], [F32, BF16], trans_rhs=False, tm=tm,
               tn=h.shape[1], name=name, n_sums=4, comm=comm)


def _last_mm_loss(lhs, w, res, gate, coef, target, gain, name):
    d = w.shape[1]

    def epi(accs, ex, vc):
        f = accs[0]
        h = ex[0] + (coef * vc[0]) * f
        r, xn = _rms_stats(h)
        err = xn * vc[1] - ex[1]
        dout = err * (1.0 / d)
        dh = _rms_back(r, xn, dout * vc[1])
        df, dgate = _branch_back(dh, f, vc[0], coef)
        return [dh, df, _colsum(err * err), _colsum(dout * xn), dgate]
    return _mm([[(lhs, w)]], epi, [res, target], [gate, gain], [F32, BF16], trans_rhs=False, tm=256, tn=d,
               name=name, n_sums=3)


def _partner(x):
    if x.shape[1] > LANES:
        return jnp.concatenate([_partner(x[:, c:c + LANES]) for c in range(0, x.shape[1], LANES)], axis=1)
    lane = lax.broadcasted_iota(jnp.int32, x.shape, 1) % HEAD_DIM
    return jnp.where(lane < HALF_HEAD, pltpu.roll(x, LANES - HALF_HEAD, 1), pltpu.roll(x, HALF_HEAD, 1))


def _proj_rope(n, w_t, cos, sin_signed, width, name, comm=None):
    s, kdim = n.shape
    n_cols = w_t.shape[0]
    tm = _tile(s, 1024, 8)
    qscale = HEAD_DIM ** -0.5

    chunk = _tile(tm, 256, 8)

    def body(n_ref, w_ref, cos_ref, sin_ref, o_ref):
        j = pl.program_id(0)

        def products(rows):
            return lax.dot_general(n_ref[rows, :].astype(BF16), w_ref[...].astype(BF16), (((1,), (1,)), ((), ())),
                                   preferred_element_type=F32)

        @pl.when(j >= 2)
        def _():
            for c in range(tm // chunk):
                rows = slice(c * chunk, (c + 1) * chunk)
                o_ref[rows, :] = products(rows)

        @pl.when(j < 2)
        def _():
            scale = jnp.where(j == 0, qscale, 1.0)
            for c in range(tm // chunk):
                rows = slice(c * chunk, (c + 1) * chunk)
                acc = products(rows)
                cos = jnp.tile(cos_ref[rows, :], (1, width // LANES))
                sin = jnp.tile(sin_ref[rows, :], (1, width // LANES))
                o_ref[rows, :] = scale * (acc * cos + _partner(acc) * sin)

    table = pl.BlockSpec((tm, LANES), lambda j, i: (jnp.where(j < 2, i, 0), 0))
    return _call(
        body, grid=(n_cols // width, s // tm),
        in_specs=[pl.BlockSpec((tm, kdim), lambda j, i: (i, 0)), pl.BlockSpec((width, kdim), lambda j, i: (j, 0)),
                  table, table],
        out_specs=[pl.BlockSpec((tm, width), lambda j, i: (i, j))],
        out_shape=[jax.ShapeDtypeStruct((s, n_cols), F32)], args=(n, w_t, cos, sin_signed), name=name, comm=comm)


def _mix_post(attn, u1, attn_g, ln_g, ln_b, conv_g):
    _, xa = _rms_stats(attn)
    mu = jnp.mean(u1, axis=-1, keepdims=True)
    xc = u1 - mu
    rstd = lax.rsqrt(jnp.mean(xc * xc, axis=-1, keepdims=True) + LN_EPS)
    u2 = (xc * rstd) * ln_g + ln_b
    u3 = u2 * _sigmoid(u2)
    _, x3 = _rms_stats(u3)
    return jnp.concatenate([xa * attn_g, x3 * conv_g], axis=1)


def _mix_post_back(dy, attn, u1, attn_g, ln_g, ln_b, conv_g):
    w = attn.shape[1]
    dya, dyc = dy[:, :w], dy[:, w:]
    ra, xa = _rms_stats(attn)
    dattn = _rms_back(ra, xa, dya * attn_g)
    mu = jnp.mean(u1, axis=-1, keepdims=True)
    xc = u1 - mu
    rstd = lax.rsqrt(jnp.mean(xc * xc, axis=-1, keepdims=True) + LN_EPS)
    xh = xc * rstd
    u2 = xh * ln_g + ln_b
    sig = _sigmoid(u2)
    u3 = u2 * sig
    r3, x3 = _rms_stats(u3)
    du3 = _rms_back(r3, x3, dyc * conv_g)
    du2 = du3 * (sig + u3 * (1.0 - sig))
    dxh = du2 * ln_g
    du1 = rstd * (dxh - jnp.mean(dxh, axis=-1, keepdims=True) - xh * jnp.mean(dxh * xh, axis=-1, keepdims=True))
    return dattn, du1, [_colsum(dya * xa), _colsum(dyc * x3), _colsum(du2 * xh), _colsum(du2)]


def _mm(groups, epi, extras, vecs, outs, *, trans_rhs, tm, tn, name, n_sums=0, pre=None, pre_inputs=(),
        comm=None):
    m = (pre_inputs[0] if pre is not None else groups[0][0][0]).shape[0]
    n = groups[0][0][1].shape[0] if trans_rhs else groups[0][0][1].shape[1]
    tm, tn = min(tm, m), min(tn, n)
    in_specs, args, uses_pre = [], [], []
    for grp in groups:
        for lhs, rhs in grp:
            k = rhs.shape[1] if trans_rhs else rhs.shape[0]
            uses_pre.append(lhs is None)
            if lhs is not None:
                in_specs.append(pl.BlockSpec((tm, k), lambda j, i: (i, 0)))
                args.append(lhs)
            in_specs.append(pl.BlockSpec((tn, k), lambda j, i: (j, 0)) if trans_rhs
                            else pl.BlockSpec((k, tn), lambda j, i: (0, j)))
            args.append(rhs)
    n_mm = len(args)
    for p in pre_inputs:
        in_specs.append(pl.BlockSpec((tm, p.shape[1]), lambda j, i: (i, 0)))
        args.append(p)
    for e in extras:
        in_specs.append(pl.BlockSpec((tm, tn), lambda j, i: (i, j)) if e.shape[1] == n
                        else pl.BlockSpec((tm, e.shape[1]), lambda j, i: (i, 0)))
        args.append(e)
    for v in vecs:
        in_specs.append(pl.BlockSpec((1, tn), lambda j, i: (0, j)) if v.shape[1] == n
                        else pl.BlockSpec((1, v.shape[1]), lambda j, i: (0, 0)))
        args.append(v)
    sizes = [len(g) for g in groups]
    n_pre, n_ex, n_vec = len(pre_inputs), len(extras), len(vecs)
    dims = (((1,), (1,)), ((), ())) if trans_rhs else (((1,), (0,)), ((), ()))
    out_specs, out_shape = [], []
    if pre is not None:
        k_pre = args[n_mm - 1].shape[1] if trans_rhs else args[n_mm - 1].shape[0]
        out_specs.append(pl.BlockSpec((tm, k_pre), lambda j, i: (i, 0)))
        out_shape.append(jax.ShapeDtypeStruct((m, k_pre), BF16))
    for o in outs:
        dt, width = o if isinstance(o, tuple) else (o, n)
        out_specs.append(pl.BlockSpec((tm, tn), lambda j, i: (i, j)) if width == n
                         else pl.BlockSpec((tm, width), lambda j, i: (i, 0)))
        out_shape.append(jax.ShapeDtypeStruct((m, width), dt))
    n_tiles_out = len(out_specs)
    out_specs += [pl.BlockSpec((1, tn), lambda j, i: (0, j))] * n_sums
    out_shape += [jax.ShapeDtypeStruct((1, n), F32)] * n_sums

    def body(*refs):
        ins = refs[:n_mm + n_pre + n_ex + n_vec]
        out_refs = refs[n_mm + n_pre + n_ex + n_vec:]
        vc = [r[...] for r in ins[n_mm + n_pre + n_ex:]]
        vals = []
        made = None
        if pre is not None:
            made = pre([r[...] for r in ins[n_mm:n_mm + n_pre]], vc).astype(BF16)
            vals.append(made)
        accs, pos, pair = [], 0, 0
        for size in sizes:
            acc = None
            for _ in range(size):
                if uses_pre[pair]:
                    lhs_tile = made
                else:
                    lhs_tile = ins[pos][...].astype(BF16)
                    pos += 1
                part = lax.dot_general(lhs_tile, ins[pos][...].astype(BF16), dims, preferred_element_type=F32)
                acc = part if acc is None else acc + part
                pos += 1
                pair += 1
            accs.append(acc)
        ex = [r[...] for r in ins[n_mm + n_pre:n_mm + n_pre + n_ex]]
        vals += epi(accs, ex, vc)
        for ref, val in zip(out_refs[:n_tiles_out], vals):
            ref[...] = val.astype(ref.dtype)
        if n_sums:
            @pl.when(pl.program_id(1) == 0)
            def _():
                for ref in out_refs[n_tiles_out:]:
                    ref[...] = jnp.zeros_like(ref)
            for ref, val in zip(out_refs[n_tiles_out:], vals[n_tiles_out:]):
                ref[...] += val

    return _call(body, grid=(n // tn, m // tm), in_specs=in_specs, out_specs=out_specs, out_shape=out_shape,
                 args=args, name=name, comm=comm)


def _mm_tn(lhs, rhs, name, comm=None):
    t, a = lhs.shape
    b = rhs.shape[1]
    ta = a if a <= 1536 else _tile(a, 1536, LANES)
    tk = _tile(t, 2048, 8)

    def body(l_ref, r_ref, o_ref):
        @pl.when(pl.program_id(1) == 0)
        def _():
            o_ref[...] = jnp.zeros_like(o_ref)
        o_ref[...] += lax.dot_general(l_ref[...].astype(BF16), r_ref[...].astype(BF16), (((0,), (0,)), ((), ())),
                                      preferred_element_type=F32)

    res = _call(body, grid=(a // ta, t // tk),
                in_specs=[pl.BlockSpec((tk, ta), lambda i, k: (k, i)), pl.BlockSpec((tk, b), lambda i, k: (k, 0))],
                out_specs=[pl.BlockSpec((ta, b), lambda i, k: (i, 0))], out_shape=[jax.ShapeDtypeStruct((a, b), F32)],
                args=(lhs, rhs), name=name, comm=comm)
    return res[0] if comm is None else (res[0][0], res[1])


def _ffn_tn(f):
    return _tile(f, 1536, LANES)


def _swiglu_parts(a, b):
    sig = _sigmoid(a)
    silu = a * sig
    return [silu, b * (sig + silu * (1.0 - sig)), silu * b]


def _ffn_up(n, wg_t, wu_t, name, comm=None):
    def epi(accs, ex, vc):
        return _swiglu_parts(accs[0], accs[1])
    return _mm([[(n, wg_t)], [(n, wu_t)]], epi, [], [], [BF16, BF16, BF16], trans_rhs=True, tm=512,
               tn=_ffn_tn(wg_t.shape[0]), name=name, comm=comm)


def _norm_ffn_up(h, gain, scale, shift, wg_t, wu_t, name, comm=None):
    def pre(tiles, vc):
        _, xn = _rms_stats(tiles[0])
        return (xn * vc[0]) * (1.0 + vc[1]) + vc[2]

    def epi(accs, ex, vc):
        return _swiglu_parts(accs[0], accs[1])
    return _mm([[(None, wg_t)], [(None, wu_t)]], epi, [], [gain, scale, shift], [BF16, BF16, BF16], trans_rhs=True,
               tm=256, tn=wg_t.shape[0], name=name, pre=pre, pre_inputs=[h], comm=comm)


def _mix_out(attn, u1, post, w, res, gate, norm, name):
    def pre(tiles, vc):
        return _mix_post(tiles[0], tiles[1], *vc[4:8])

    def epi(accs, ex, vc):
        h = ex[0] + vc[0] * accs[0]
        _, xn = _rms_stats(h)
        return [h, accs[0], (xn * vc[1]) * (1.0 + vc[2]) + vc[3]]
    return _mm([[(None, w)]], epi, [res], [gate] + list(norm) + list(post), [F32, BF16, BF16], trans_rhs=False,
               tm=512, tn=w.shape[1], name=name, pre=pre, pre_inputs=[attn, u1])


def _mix_dy_post_bwd(dmix, w, attn, u1, post, name):
    width = attn.shape[1]

    def epi(accs, ex, vc):
        dattn, du1, sums = _mix_post_back(accs[0], ex[0], ex[1], *vc)
        return [dattn, du1, jnp.concatenate(sums[0:2], axis=1), jnp.concatenate(sums[2:4], axis=1)]
    return _mm([[(dmix, w)]], epi, [attn, u1], list(post), [(F32, width), (F32, width)], trans_rhs=True, tm=256,
               tn=w.shape[0], name=name, n_sums=2)


def _residual_mm(lhs, w, res, gate, coef, name, norm=None, comm=None):
    def epi(accs, ex, vc):
        h = ex[0] + (coef * vc[0]) * accs[0]
        if norm is None:
            return [h, accs[0]]
        _, xn = _rms_stats(h)
        return [h, accs[0], (xn * vc[1]) * (1.0 + vc[2]) + vc[3]]
    vecs = [gate] + (list(norm) if norm is not None else [])
    outs = [F32, BF16] + ([BF16] if norm is not None else [])
    return _mm([[(lhs, w)]], epi, [res], vecs, outs, trans_rhs=False, tm=512, tn=w.shape[1], name=name, comm=comm)


def _ffn_bwd_hidden(df, wd, dhid_db, dhid_da, name, comm=None):
    def epi(accs, ex, vc):
        return [accs[0] * ex[1].astype(F32), accs[0] * ex[0].astype(F32)]
    return _mm([[(df, wd)]], epi, [dhid_db, dhid_da], [], [BF16, BF16], trans_rhs=True, tm=512,
               tn=_ffn_tn(wd.shape[0]), name=name, comm=comm)


def _plain_mm(pairs, out_dtype, trans_rhs, tn, name, tm=512, comm=None):
    def epi(accs, ex, vc):
        return [accs[0]]
    res = _mm([pairs], epi, [], [], [out_dtype], trans_rhs=trans_rhs, tm=tm, tn=tn, name=name, comm=comm)
    return res[0] if comm is None else (res[0][0], res[1])


HEADS_PER_TILE = LANES // HEAD_DIM


def _stack_heads(x):
    lane = lax.broadcasted_iota(jnp.int32, (1, LANES), 1)
    return jnp.concatenate([x * (lane // HEAD_DIM == h).astype(F32) for h in range(HEADS_PER_TILE)], axis=0)


def _unstack_heads(y):
    r = y.shape[0] // HEADS_PER_TILE
    lane = lax.broadcasted_iota(jnp.int32, (r, y.shape[1]), 1)
    out = y[0:r]
    for h in range(1, HEADS_PER_TILE):
        out = jnp.where(lane // HEAD_DIM == h, y[h * r:(h + 1) * r], out)
    return out


def _stacked_lse(lb):
    return jnp.concatenate([_lane_pick(lb, h) for h in range(HEADS_PER_TILE)], axis=0)


def _band_masks(n_row_blocks, n_col_blocks):
    shape = (n_row_blocks * BLOCK, n_col_blocks * BLOCK)
    qi = lax.broadcasted_iota(jnp.int32, shape, 0) % BLOCK
    kj = lax.broadcasted_iota(jnp.int32, shape, 1) % BLOCK
    return kj <= qi, kj >= qi


def _query_masks():
    first_valid, _ = _band_masks(HEADS_PER_TILE, 1)
    same_ok, before_ok = _band_masks(HEADS_PER_TILE, 2)
    is_cur = lax.broadcasted_iota(jnp.int32, same_ok.shape, 1) >= BLOCK
    return first_valid, jnp.logical_and(is_cur, same_ok), jnp.logical_and(jnp.logical_not(is_cur), before_ok)


def _dot_nt(a, b):
    return lax.dot_general(a.astype(BF16), b.astype(BF16), (((1,), (1,)), ((), ())), preferred_element_type=F32)


def _dot_nn(a, b):
    return lax.dot_general(a.astype(BF16), b.astype(BF16), (((1,), (0,)), ((), ())), preferred_element_type=F32)


def _dot_tn(a, b):
    return lax.dot_general(a.astype(BF16), b.astype(BF16), (((0,), (0,)), ((), ())), preferred_element_type=F32)


def _lane_pick(x, h):
    lane = lax.broadcasted_iota(jnp.int32, x.shape, 1)
    return jnp.sum(jnp.where(lane == h * HEAD_DIM, x, 0.0), axis=1, keepdims=True)


def _block_rows(idx, d):
    span = BLOCK * d
    q0 = (idx // d) * span + idx % d
    return pl.ds(q0, BLOCK, stride=d), pl.ds(q0 - span, BLOCK, stride=d)


def _two_loops(n_blocks, d, unroll):
    return d % unroll == 0 and (n_blocks - d) % unroll == 0 and n_blocks > d


def _branch_loops(n_blocks, d, visit, unroll, masks):
    first_valid, cur_part, prev_part = masks
    if _two_loops(n_blocks, d, unroll):
        full_valid = jnp.logical_or(cur_part, prev_part)

        def first(idx, carry):
            rows = pl.ds(idx, BLOCK, stride=d)
            visit(rows, [rows], first_valid)
            return carry

        def rest(idx, carry):
            rows, prev = _block_rows(idx, d)
            visit(rows, [prev, rows], full_valid)
            return carry

        lax.fori_loop(0, d, first, 0, unroll=unroll)
        lax.fori_loop(d, n_blocks, rest, 0, unroll=unroll)
        return

    def every(idx, carry):
        span = BLOCK * d
        q0 = (idx // d) * span + idx % d
        has_prev = idx >= d
        rows = pl.ds(q0, BLOCK, stride=d)
        prev = pl.ds(jnp.where(has_prev, q0 - span, q0), BLOCK, stride=d)
        visit(rows, [prev, rows], jnp.logical_or(cur_part, jnp.logical_and(prev_part, has_prev)))
        return carry

    lax.fori_loop(0, n_blocks, every, 0, unroll=unroll)


def _qkv_specs(s, tiles):
    q, k, v = [pl.BlockSpec((s, LANES), functools.partial(lambda hb, off: (0, off + hb), off=i * tiles))
               for i in range(3)]
    return q, k, v, pl.BlockSpec((s, LANES), lambda hb: (0, hb))


def _attn_seq_fwd(proj, width, name, comm=None):
    s = proj.shape[0]
    q_spec, k_spec, v_spec, cur = _qkv_specs(s, width // LANES)

    def body(q_ref, k_ref, v_ref, o_ref, l_ref, o_s, l_s):
        masks = _query_masks()
        for bi, d in enumerate(DILATIONS):
            def visit(rows, key_rows, valid, bi=bi):
                q2 = _stack_heads(q_ref[rows, :])
                keys = jnp.concatenate([k_ref[r, :] for r in key_rows], axis=0)
                vals = jnp.concatenate([v_ref[r, :] for r in key_rows], axis=0)
                sc = jnp.where(valid, _dot_nt(q2, keys), NEG)
                mx = jnp.max(sc, axis=1, keepdims=True)
                p = jnp.exp(sc - mx)
                den = jnp.sum(p, axis=1, keepdims=True)
                o_s[bi, rows, :] = _unstack_heads(_dot_nn(p, vals) / den)
                l_s[bi, rows, :] = _unstack_heads(jnp.broadcast_to(mx + jnp.log(den), (q2.shape[0], LANES)))

            _branch_loops(s // BLOCK, d, visit, 8, masks)
        for c in range(s // MERGE_CHUNK):
            rows = slice(c * MERGE_CHUNK, (c + 1) * MERGE_CHUNK)
            ls = [l_s[bi, rows, :] for bi in range(len(DILATIONS))]
            top = functools.reduce(jnp.maximum, ls)
            ws = [jnp.exp(l - top) for l in ls]
            den = functools.reduce(lambda a, b: a + b, ws)
            num = functools.reduce(lambda a, b: a + b, [w * o_s[bi, rows, :] for bi, w in enumerate(ws)])
            o_ref[rows, :] = num / den
            l_ref[rows, :] = top + jnp.log(den)

    return _call(
        body, grid=(width // LANES,), in_specs=[q_spec, k_spec, v_spec], out_specs=[cur, cur],
        out_shape=[jax.ShapeDtypeStruct((s, width), F32)] * 2,
        scratch_shapes=[pltpu.VMEM((len(DILATIONS), s, LANES), F32)] * 2,
        args=(proj, proj, proj), name=name, comm=comm)


def _attn_seq_bwd(proj, do, o, lse, cos, sin_signed, name, comm=None):
    s, width = do.shape
    q_spec, k_spec, v_spec, cur = _qkv_specs(s, width // LANES)
    table = pl.BlockSpec((s, LANES), lambda hb: (0, 0))
    qscale = HEAD_DIM ** -0.5

    def body(q_ref, k_ref, v_ref, do_ref, o_ref, l_ref, cos_ref, sin_ref, dq_out, dk_out, dv_out,
             dq_ref, dk_ref, dv_ref):
        unroll = 8
        masks = _query_masks()
        order = sorted(DILATIONS, reverse=True)
        fresh_first = _two_loops(s // BLOCK, order[0], unroll)
        if not fresh_first:
            dq_ref[...] = jnp.zeros_like(dq_ref)
            dk_ref[...] = jnp.zeros_like(dk_ref)
            dv_ref[...] = jnp.zeros_like(dv_ref)
        for d in order:
            def visit(rows, key_rows, valid, fresh=fresh_first and d == order[0]):
                dob = do_ref[rows, :]
                q2 = _stack_heads(q_ref[rows, :])
                do2 = _stack_heads(dob)
                delta = jnp.sum(_stack_heads(dob * o_ref[rows, :]), axis=1, keepdims=True)
                lse2 = _stacked_lse(l_ref[rows, :])
                keys = jnp.concatenate([k_ref[r, :] for r in key_rows], axis=0)
                vals = jnp.concatenate([v_ref[r, :] for r in key_rows], axis=0)
                p = jnp.where(valid, jnp.exp(_dot_nt(q2, keys) - lse2), 0.0)
                ds = p * (_dot_nt(do2, vals) - delta)
                dq = _unstack_heads(_dot_nn(ds, keys))
                dkk = _dot_tn(ds, q2)
                dvv = _dot_tn(p, do2)
                if fresh:
                    dq_ref[rows, :] = dq
                else:
                    dq_ref[rows, :] += dq
                for i, r in enumerate(key_rows):
                    own = i == len(key_rows) - 1
                    if fresh and own:
                        dk_ref[r, :] = dkk[i * BLOCK:(i + 1) * BLOCK]
                        dv_ref[r, :] = dvv[i * BLOCK:(i + 1) * BLOCK]
                    else:
                        dk_ref[r, :] += dkk[i * BLOCK:(i + 1) * BLOCK]
                        dv_ref[r, :] += dvv[i * BLOCK:(i + 1) * BLOCK]

            _branch_loops(s // BLOCK, d, visit, unroll, masks)
        for c in range(s // MERGE_CHUNK):
            rows = slice(c * MERGE_CHUNK, (c + 1) * MERGE_CHUNK)
            cos, sin = cos_ref[rows, :], sin_ref[rows, :]
            dq, dk = dq_ref[rows, :], dk_ref[rows, :]
            dq_out[rows, :] = ((dq * cos - _partner(dq) * sin) * qscale).astype(BF16)
            dk_out[rows, :] = (dk * cos - _partner(dk) * sin).astype(BF16)
            dv_out[rows, :] = dv_ref[rows, :].astype(BF16)

    return _call(
        body, grid=(width // LANES,), in_specs=[q_spec, k_spec, v_spec, cur, cur, cur, table, table],
        out_specs=[cur, cur, cur], out_shape=[jax.ShapeDtypeStruct((s, width), BF16)] * 3,
        scratch_shapes=[pltpu.VMEM((s, LANES), F32)] * 3,
        args=(proj, proj, proj, do, o, lse, cos, sin_signed), name=name, comm=comm)


def _conv_specs(s, a_block, b_block):
    per = CONV_CHUNK // CONV_HALO
    a_cur = pl.BlockSpec((CONV_CHUNK, LANES), lambda cb, i: (i, a_block + cb))
    b_cur = pl.BlockSpec((CONV_CHUNK, LANES), lambda cb, i: (i, b_block + cb))
    a_halo = pl.BlockSpec((CONV_HALO, LANES), lambda cb, i: (jnp.maximum(i * per - 1, 0), a_block + cb))
    b_halo = pl.BlockSpec((CONV_HALO, LANES), lambda cb, i: (jnp.maximum(i * per - 1, 0), b_block + cb))
    w_spec = pl.BlockSpec((CONV_KERNEL, LANES), lambda cb, i: (0, cb))
    vec = pl.BlockSpec((1, LANES), lambda cb, i: (0, cb))
    out = pl.BlockSpec((CONV_CHUNK, LANES), lambda cb, i: (i, cb))
    return a_cur, b_cur, a_halo, b_halo, w_spec, vec, out


def _fill_glu_window(win, a_ref, b_ref, ah_ref, bh_ref, first):
    halo = ah_ref[...] * _sigmoid(bh_ref[...])
    win[0:CONV_HALO, :] = jnp.where(first, 0.0, halo)
    win[CONV_HALO:, :] = a_ref[...] * _sigmoid(b_ref[...])


def _conv_fwd(proj, a_block, b_block, w, bias, name, comm=None):
    s = proj.shape[0]
    cw = w.shape[1]
    a_cur, b_cur, a_halo, b_halo, w_spec, vec, out = _conv_specs(s, a_block, b_block)
    lead = CONV_HALO - (CONV_KERNEL - 1)

    def body(a_ref, b_ref, ah_ref, bh_ref, w_ref, bias_ref, o_ref, win):
        _fill_glu_window(win, a_ref, b_ref, ah_ref, bh_ref, pl.program_id(1) == 0)
        for sub in range(CONV_CHUNK // CONV_SUB):
            base = sub * CONV_SUB
            acc = jnp.zeros((CONV_SUB, LANES), F32) + bias_ref[...]
            for j in range(CONV_KERNEL):
                acc = acc + w_ref[j:j + 1, :] * win[base + lead + j:base + lead + j + CONV_SUB, :]
            o_ref[base:base + CONV_SUB, :] = acc

    return _call(
        body, grid=(cw // LANES, s // CONV_CHUNK), in_specs=[a_cur, b_cur, a_halo, b_halo, w_spec, vec],
        out_specs=[out], out_shape=[jax.ShapeDtypeStruct((s, cw), F32)],
        scratch_shapes=[pltpu.VMEM((CONV_CHUNK + CONV_HALO, LANES), F32)],
        args=(proj, proj, proj, proj, w, bias), name=name, comm=comm)


def _conv_bwd(proj, a_block, b_block, w, du1, name):
    s = proj.shape[0]
    cw = w.shape[1]
    a_cur, b_cur, a_halo, b_halo, w_spec, vec, out = _conv_specs(s, a_block, b_block)
    per = CONV_CHUNK // CONV_HALO
    n_chunks = s // CONV_CHUNK
    d_next = pl.BlockSpec((CONV_HALO, LANES), lambda cb, i: (jnp.minimum((i + 1) * per, s // CONV_HALO - 1), cb))
    lead = CONV_HALO - (CONV_KERNEL - 1)

    def body(a_ref, b_ref, ah_ref, bh_ref, w_ref, d_ref, dn_ref, da_ref, db_ref, dw_ref, dbias_ref, win, dwin):
        i = pl.program_id(1)
        _fill_glu_window(win, a_ref, b_ref, ah_ref, bh_ref, i == 0)
        dwin[0:CONV_CHUNK, :] = d_ref[...]
        dwin[CONV_CHUNK:, :] = jnp.where(i == n_chunks - 1, 0.0, dn_ref[...])

        @pl.when(i == 0)
        def _():
            dw_ref[...] = jnp.zeros_like(dw_ref)
            dbias_ref[...] = jnp.zeros_like(dbias_ref)

        dbias_ref[...] += _colsum(d_ref[...])
        for sub in range(CONV_CHUNK // CONV_SUB):
            base = sub * CONV_SUB
            dcur = dwin[base:base + CONV_SUB, :]
            du0 = jnp.zeros((CONV_SUB, LANES), F32)
            for j in range(CONV_KERNEL):
                back = CONV_KERNEL - 1 - j
                du0 = du0 + w_ref[j:j + 1, :] * dwin[base + back:base + back + CONV_SUB, :]
                dw_ref[j:j + 1, :] += _colsum(dcur * win[base + lead + j:base + lead + j + CONV_SUB, :])
            av = a_ref[base:base + CONV_SUB, :]
            sig = _sigmoid(b_ref[base:base + CONV_SUB, :])
            da_ref[base:base + CONV_SUB, :] = (du0 * sig).astype(BF16)
            db_ref[base:base + CONV_SUB, :] = (du0 * av * sig * (1.0 - sig)).astype(BF16)

    return pl.pallas_call(
        body, grid=(cw // LANES, n_chunks), in_specs=[a_cur, b_cur, a_halo, b_halo, w_spec, out, d_next],
        out_specs=[out, out, w_spec, vec],
        out_shape=[jax.ShapeDtypeStruct((s, cw), BF16), jax.ShapeDtypeStruct((s, cw), BF16),
                   jax.ShapeDtypeStruct((CONV_KERNEL, cw), F32), jax.ShapeDtypeStruct((1, cw), F32)],
        scratch_shapes=[pltpu.VMEM((CONV_CHUNK + CONV_HALO, LANES), F32)] * 2,
        compiler_params=_params(2), name=name)(proj, proj, proj, proj, w, du1, du1)


def _adamw_math(w, g, m, v):
    m = ADAM_B1 * m + (1.0 - ADAM_B1) * g
    v = ADAM_B2 * v + (1.0 - ADAM_B2) * (g * g)
    m_hat = m / (1.0 - ADAM_B1 ** ADAM_STEP)
    v_hat = v / (1.0 - ADAM_B2 ** ADAM_STEP)
    delta = -ADAM_LR * (m_hat / (jnp.sqrt(v_hat) + ADAM_EPS) + ADAM_WD * w)
    return delta, m, v


def _adamw_big(w, g, m, v, name):
    rows, cols = w.shape
    tile = _tile(rows, 256, 8)
    spec = pl.BlockSpec((tile, cols), lambda i: (i, 0))

    def body(w_ref, g_ref, m_ref, v_ref, d_out, m_out, v_out):
        d_out[...], m_out[...], v_out[...] = _adamw_math(w_ref[...], g_ref[...], m_ref[...], v_ref[...])

    return pl.pallas_call(body, grid=(rows // tile,), in_specs=[spec] * 4, out_specs=[spec] * 3,
                          out_shape=[jax.ShapeDtypeStruct(w.shape, F32)] * 3, compiler_params=_params(1),
                          name=name)(w, g, m, v)


def _adamw_reduced(w, land, m, v, name):
    rows, cols = w.shape
    tile = _tile(rows, 256, 16)
    spec = pl.BlockSpec((tile, cols), lambda i: (i, 0))

    def body(w_ref, l_ref, m_ref, v_ref, g_out, d_out, m_out, v_out):
        g = l_ref[0].astype(F32)
        for q in range(1, N_CHIP):
            g = g + l_ref[q].astype(F32)
        g_out[...] = g
        d_out[...], m_out[...], v_out[...] = _adamw_math(w_ref[...], g, m_ref[...], v_ref[...])

    return pl.pallas_call(body, grid=(rows // tile,),
                          in_specs=[spec, pl.BlockSpec((N_CHIP, tile, cols), lambda i: (0, i, 0)), spec, spec],
                          out_specs=[spec] * 4, out_shape=[jax.ShapeDtypeStruct(w.shape, F32)] * 4,
                          compiler_params=_params(1), name=name)(w, land, m, v)


def _adamw_small(ws, gs, ms, vs, name):
    n = len(ws)

    def body(*refs):
        ins, outs = refs[:4 * n], refs[4 * n:]
        for t in range(n):
            res = _adamw_math(ins[t][...], ins[n + t][...], ins[2 * n + t][...], ins[3 * n + t][...])
            for j in range(3):
                outs[j * n + t][...] = res[j]

    shapes = [jax.ShapeDtypeStruct(w.shape, F32) for w in ws]
    res = pl.pallas_call(body, out_shape=shapes * 3, compiler_params=pltpu.CompilerParams(vmem_limit_bytes=VMEM_LIMIT),
                         name=name)(*ws, *gs, *ms, *vs)
    return res[:n], res[n:2 * n], res[2 * n:]


def _sum_blocks(x, n_blocks, name):
    r = x.shape[0] // n_blocks

    def body(x_ref, o_ref):
        acc = x_ref[0:r, :]
        for b in range(1, n_blocks):
            acc = acc + x_ref[b * r:(b + 1) * r, :]
        o_ref[...] = acc

    return pl.pallas_call(body, out_shape=jax.ShapeDtypeStruct((r, x.shape[1]), F32),
                          compiler_params=pltpu.CompilerParams(vmem_limit_bytes=VMEM_LIMIT), name=name)(x)


def _coords():
    return lax.axis_index("x"), lax.axis_index("y"), lax.axis_index("c")


def _flip(v, bit):
    return 1 - v if bit else v


def _ag_small(x, name):
    r, c = x.shape

    def body(x_ref, o_ref, send, recv, local_sem):
        mx, my, mc = _coords()

        def rows(px, py, pc):
            return o_ref.at[pl.ds(pl.multiple_of((4 * px + 2 * py + pc) * r, 8), r), :]

        local = pltpu.make_async_copy(x_ref, rows(mx, my, mc), local_sem)
        local.start()
        peers = [(_flip(mx, k >> 2 & 1), _flip(my, k >> 1 & 1), _flip(mc, k & 1)) for k in range(1, N_DEV)]
        sends = [pltpu.make_async_remote_copy(x_ref, rows(mx, my, mc), send.at[k], recv.at[k], device_id=p,
                                              device_id_type=MESH) for k, p in enumerate(peers)]
        for cp in sends:
            cp.start()
        for k, p in enumerate(peers):
            pltpu.make_async_remote_copy(x_ref, rows(*p), send.at[k], recv.at[k], device_id=p,
                                         device_id_type=MESH).wait_recv()
        for cp in sends:
            cp.wait_send()
        local.wait()

    vm = pl.BlockSpec(memory_space=pltpu.VMEM)
    return pl.pallas_call(
        body, in_specs=[vm], out_specs=vm, out_shape=jax.ShapeDtypeStruct((N_DEV * r, c), x.dtype),
        scratch_shapes=[pltpu.SemaphoreType.DMA((N_DEV - 1,)), pltpu.SemaphoreType.DMA((N_DEV - 1,)),
                        pltpu.SemaphoreType.DMA(())],
        name=name)(x)


class _GatherSmall:
    mid = None

    def __init__(self, x):
        self.inputs = [x]
        self.out_shapes = [jax.ShapeDtypeStruct((N_DEV * x.shape[0], x.shape[1]), x.dtype)]
        self.scratch = [pltpu.SemaphoreType.DMA((N_DEV - 1,)), pltpu.SemaphoreType.DMA((N_DEV - 1,)),
                        pltpu.SemaphoreType.DMA(())]

    def _plan(self, x_refs, o_refs, sems):
        send, recv, local_sem = sems
        x_ref, o_ref = x_refs[0], o_refs[0]
        r = x_ref.shape[0]
        mx, my, mc = _coords()

        def rows(px, py, pc):
            return o_ref.at[pl.ds(pl.multiple_of((4 * px + 2 * py + pc) * r, 8), r), :]

        peers = [(_flip(mx, k >> 2 & 1), _flip(my, k >> 1 & 1), _flip(mc, k & 1)) for k in range(1, N_DEV)]
        out = [pltpu.make_async_remote_copy(x_ref, rows(mx, my, mc), send.at[k], recv.at[k], device_id=p,
                                            device_id_type=MESH) for k, p in enumerate(peers)]
        arrivals = [pltpu.make_async_remote_copy(x_ref, rows(*p), send.at[k], recv.at[k], device_id=p,
                                                 device_id_type=MESH) for k, p in enumerate(peers)]
        return out, arrivals, pltpu.make_async_copy(x_ref, rows(mx, my, mc), local_sem)

    def start(self, x_refs, o_refs, sems):
        out, _, local = self._plan(x_refs, o_refs, sems)
        local.start()
        for cp in out:
            cp.start()

    def finish(self, x_refs, o_refs, sems):
        out, arrivals, local = self._plan(x_refs, o_refs, sems)
        for cp in arrivals:
            cp.wait_recv()
        for cp in out:
            cp.wait_send()
        local.wait()


class _ModExchange:
    def __init__(self, first, w_ada):
        self.d, cols = w_ada.shape
        part = jax.ShapeDtypeStruct((N_DEV, cols), F32)
        self.g1, self.g2 = _GatherSmall(first), _GatherSmall(part)
        self.inputs = [first, w_ada]
        self.out_shapes = [self.g1.out_shapes[0], jax.ShapeDtypeStruct((N_DEV, self.d), F32), part,
                           self.g2.out_shapes[0]]
        self.scratch = self.g1.scratch + self.g2.scratch + [
            pltpu.VMEM(self.g1.out_shapes[0].shape, F32), pltpu.VMEM(w_ada.shape, F32),
            pltpu.VMEM((N_DEV, self.d), F32), pltpu.VMEM((N_DEV, cols), F32), pltpu.SemaphoreType.DMA(())]

    def start(self, cin, cout, scr):
        self.g1.start(cin[0:1], cout[0:1], scr[0:3])
        pltpu.make_async_copy(cin[1], scr[7], scr[10]).start()

    def mid(self, cin, cout, scr):
        gathered, w_v, silu_v, part_v = scr[6:10]
        self.g1.finish(cin[0:1], cout[0:1], scr[0:3])
        pltpu.sync_copy(cout[0], gathered)
        rows_per = cin[0].shape[0]
        for j in range(N_DEV):
            silu_v[j:j + 1, :] = gathered[j * rows_per:j * rows_per + 1, 0:self.d]
        c_all = silu_v[...]
        silu_v[...] = c_all * _sigmoid(c_all)
        pltpu.sync_copy(silu_v, cout[1])
        pltpu.make_async_copy(cin[1], w_v, scr[10]).wait()
        part_v[...] = _dot_nn(silu_v[...], w_v[...])
        pltpu.sync_copy(part_v, cout[2])
        self.g2.start(cout[2:3], cout[3:4], scr[3:6])

    def finish(self, cin, cout, scr):
        self.g2.finish(cout[2:3], cout[3:4], scr[3:6])


class _GatherWeights:
    def __init__(self, shards):
        n_t = len(shards)
        self.inputs = list(shards)
        self.out_shapes = [jax.ShapeDtypeStruct((N_DEV * x.shape[0], x.shape[1]), x.dtype) for x in shards]
        self.scratch = [pltpu.SemaphoreType.DMA((n_t, 8)), pltpu.SemaphoreType.DMA((n_t, 8)),
                        pltpu.SemaphoreType.DMA((n_t,))]

    def _plan(self, x_refs, o_refs, sems):
        send, recv, local_sem = sems
        mx, my, mc = _coords()
        me, sibling = (mx, my, mc), (mx, my, 1 - mc)
        xn, yn, diag = (1 - mx, my), (mx, 1 - my), (1 - mx, 1 - my)

        def rows(t, chip, core, half=None):
            r = x_refs[t].shape[0]
            base = (4 * chip[0] + 2 * chip[1] + core) * r
            if half is None:
                return o_refs[t].at[pl.ds(pl.multiple_of(base, 8), r), :]
            return o_refs[t].at[pl.ds(pl.multiple_of(base + half * (r // 2), 8), r // 2), :]

        def copy(t, k, block, to, src=None):
            return pltpu.make_async_remote_copy(
                src_ref=block if src is None else src, dst_ref=block,
                send_sem=send.at[t, k], recv_sem=recv.at[t, k], device_id=to, device_id_type=MESH)

        def local(t):
            return pltpu.make_async_copy(x_refs[t], rows(t, (mx, my), mc), local_sem.at[t])

        return (mx, my), mc, me, sibling, xn, yn, diag, rows, copy, local

    def start(self, x_refs, o_refs, sems):
        chip, mc, me, sibling, xn, yn, diag, rows, copy, local = self._plan(x_refs, o_refs, sems)
        for t in range(len(x_refs)):
            mine = rows(t, chip, mc)
            local(t).start()
            copy(t, 0, mine, sibling, src=x_refs[t]).start()
            copy(t, 1, mine, (*xn, mc), src=x_refs[t]).start()
            copy(t, 2, mine, (*yn, mc), src=x_refs[t]).start()

    def mid(self, x_refs, o_refs, sems):
        chip, mc, me, sibling, xn, yn, diag, rows, copy, local = self._plan(x_refs, o_refs, sems)
        for t in range(len(x_refs)):
            copy(t, 1, rows(t, xn, mc), me).wait_recv()
            copy(t, 3, rows(t, xn, mc, 0), (*yn, mc)).start()
            copy(t, 5, rows(t, xn, mc), sibling).start()
        for t in range(len(x_refs)):
            copy(t, 2, rows(t, yn, mc), me).wait_recv()
            copy(t, 4, rows(t, yn, mc, 1), (*xn, mc)).start()
            copy(t, 6, rows(t, yn, mc), sibling).start()

    def finish(self, x_refs, o_refs, sems):
        chip, mc, me, sibling, xn, yn, diag, rows, copy, local = self._plan(x_refs, o_refs, sems)
        for t in range(len(x_refs)):
            copy(t, 3, rows(t, diag, mc, 0), me).wait_recv()
            copy(t, 4, rows(t, diag, mc, 1), me).wait_recv()
            copy(t, 7, rows(t, diag, mc), sibling).start()
        for t in range(len(x_refs)):
            copy(t, 0, rows(t, chip, 1 - mc), me).wait_recv()
            copy(t, 5, rows(t, xn, 1 - mc), me).wait_recv()
            copy(t, 6, rows(t, yn, 1 - mc), me).wait_recv()
            copy(t, 7, rows(t, diag, 1 - mc), me).wait_recv()
            mine = rows(t, chip, mc)
            copy(t, 0, mine, sibling, src=x_refs[t]).wait_send()
            copy(t, 1, mine, (*xn, mc), src=x_refs[t]).wait_send()
            copy(t, 2, mine, (*yn, mc), src=x_refs[t]).wait_send()
            copy(t, 3, rows(t, xn, mc, 0), (*yn, mc)).wait_send()
            copy(t, 4, rows(t, yn, mc, 1), (*xn, mc)).wait_send()
            copy(t, 5, rows(t, xn, mc), sibling).wait_send()
            copy(t, 6, rows(t, yn, mc), sibling).wait_send()
            copy(t, 7, rows(t, diag, mc), sibling).wait_send()
            local(t).wait()


class _SiblingExchange:
    mid = None

    def __init__(self, grads):
        n_t = len(grads)
        self.inputs = list(grads)
        self.out_shapes = [jax.ShapeDtypeStruct((N_CHIP,) + g.shape[2:], F32) for g in grads]
        self.scratch = [pltpu.SemaphoreType.DMA((n_t,)), pltpu.SemaphoreType.DMA((n_t,))]

    def _copies(self, g_refs, land, sems):
        send, recv = sems
        mx, my, mc = _coords()
        return [pltpu.make_async_remote_copy(g_refs[t].at[:, 1 - mc], land[t], send.at[t], recv.at[t],
                                             device_id=(mx, my, 1 - mc), device_id_type=MESH)
                for t in range(len(g_refs))]

    def start(self, g_refs, land, sems):
        for cp in self._copies(g_refs, land, sems):
            cp.start()

    def finish(self, g_refs, land, sems):
        for cp in self._copies(g_refs, land, sems):
            cp.wait()


class _Together:
    def __init__(self, *comms):
        self.comms = comms
        self.inputs = [x for c in comms for x in c.inputs]
        self.out_shapes = [x for c in comms for x in c.out_shapes]
        self.scratch = [x for c in comms for x in c.scratch]
        self.mid = self._mid if any(c.mid is not None for c in comms) else None

    def _each(self, phase, cin, cout, sems):
        i = o = s = 0
        for c in self.comms:
            fn = getattr(c, phase)
            ni, no, ns = len(c.inputs), len(c.out_shapes), len(c.scratch)
            if fn is not None:
                fn(cin[i:i + ni], cout[o:o + no], sems[s:s + ns])
            i, o, s = i + ni, o + no, s + ns

    def start(self, cin, cout, sems):
        self._each("start", cin, cout, sems)

    def _mid(self, cin, cout, sems):
        self._each("mid", cin, cout, sems)

    def finish(self, cin, cout, sems):
        self._each("finish", cin, cout, sems)


def _standalone(comm, name):
    def body():
        pass
    return _call(body, grid=(1,), in_specs=[], out_specs=[], out_shape=[], args=(), name=name, comm=comm)[1]


def _chip_partials(g4s, lands, name):
    n_t = len(g4s)
    in_specs, out_specs, out_shape = [], [], []
    for g4 in g4s:
        _, _, r, c = g4.shape
        in_specs.append(pl.BlockSpec((None, None, r, c), lambda q: (q, lax.axis_index("c"), 0, 0)))
        out_specs.append(pl.BlockSpec((None, r, c), lambda q: (q, 0, 0)))
        out_shape.append(jax.ShapeDtypeStruct((N_CHIP, r, c), BF16))
    in_specs += [pl.BlockSpec((None,) + g4.shape[2:], lambda q: (q, 0, 0)) for g4 in g4s]

    def body(*refs):
        for t in range(n_t):
            refs[2 * n_t + t][...] = (refs[t][...] + refs[n_t + t][...]).astype(BF16)

    return pl.pallas_call(body, grid=(N_CHIP,), in_specs=in_specs, out_specs=out_specs, out_shape=out_shape,
                          compiler_params=_params(1), name=name)(*g4s, *lands)


class _ChipExchange:
    mid = None

    def __init__(self, parts):
        n_t = len(parts)
        self.inputs = list(parts)
        self.out_shapes = [jax.ShapeDtypeStruct(p.shape, p.dtype) for p in parts]
        self.scratch = [pltpu.SemaphoreType.DMA((n_t, 3)), pltpu.SemaphoreType.DMA((n_t, 3)),
                        pltpu.SemaphoreType.DMA((n_t,))]

    def _plan(self, p_refs, land, sems):
        send, recv, local_sem = sems
        mx, my, mc = _coords()
        my_chip = 2 * mx + my
        peers = [(_flip(mx, fx), _flip(my, fy)) for fx, fy in ((1, 0), (0, 1), (1, 1))]

        def out(t, k):
            px, py = peers[k]
            return pltpu.make_async_remote_copy(p_refs[t].at[2 * px + py], land[t].at[my_chip], send.at[t, k],
                                                recv.at[t, k], device_id=(px, py, mc), device_id_type=MESH)

        def arrival(t, k):
            px, py = peers[k]
            return pltpu.make_async_remote_copy(p_refs[t].at[my_chip], land[t].at[2 * px + py], send.at[t, k],
                                                recv.at[t, k], device_id=(px, py, mc), device_id_type=MESH)

        def local(t):
            return pltpu.make_async_copy(p_refs[t].at[my_chip], land[t].at[my_chip], local_sem.at[t])

        return out, arrival, local

    def start(self, p_refs, land, sems):
        out, arrival, local = self._plan(p_refs, land, sems)
        for t in range(len(p_refs)):
            local(t).start()
            for k in range(3):
                out(t, k).start()

    def finish(self, p_refs, land, sems):
        out, arrival, local = self._plan(p_refs, land, sems)
        for t in range(len(p_refs)):
            for k in range(3):
                arrival(t, k).wait_recv()
                out(t, k).wait_send()
            local(t).wait()


def _rope_tables(s, width):
    heads = width // HEAD_DIM
    inv_freq = ROPE_THETA ** (-jnp.arange(0, HEAD_DIM, 2, dtype=F32) / HEAD_DIM)
    inv_full = jnp.tile(inv_freq, 2 * heads)
    sign = jnp.tile(jnp.concatenate([-jnp.ones((HALF_HEAD,), F32), jnp.ones((HALF_HEAD,), F32)]), heads)
    ang = jnp.arange(s, dtype=F32)[:, None] * inv_full[None, :]
    return jnp.cos(ang), jnp.sin(ang) * sign[None, :]


def _pad_rows(v, rows):
    return jnp.concatenate([v, jnp.zeros((rows - 1, v.shape[1]), v.dtype)], axis=0)


def kernel(x, c, w_ada, b_ada, ffn1_norm_g, ffn1_w_gate, ffn1_w_up, ffn1_w_down, mix_norm_g, w_in, conv_dw_w, conv_dw_b, conv_ln_g, conv_ln_b, attn_out_g, conv_out_g, w_out, ffn2_norm_g, ffn2_w_gate, ffn2_w_up, ffn2_w_down, final_norm_g, loss_target, m_w_ada, m_b_ada, m_ffn1_norm_g, m_ffn1_w_gate, m_ffn1_w_up, m_ffn1_w_down, m_mix_norm_g, m_w_in, m_conv_dw_w, m_conv_dw_b, m_conv_ln_g, m_conv_ln_b, m_attn_out_g, m_conv_out_g, m_w_out, m_ffn2_norm_g, m_ffn2_w_gate, m_ffn2_w_up, m_ffn2_w_down, m_final_norm_g, v_w_ada, v_b_ada, v_ffn1_norm_g, v_ffn1_w_gate, v_ffn1_w_up, v_ffn1_w_down, v_mix_norm_g, v_w_in, v_conv_dw_w, v_conv_dw_b, v_conv_ln_g, v_conv_ln_b, v_attn_out_g, v_conv_out_g, v_w_out, v_ffn2_norm_g, v_ffn2_w_gate, v_ffn2_w_up, v_ffn2_w_down, v_final_norm_g):
    mx, my, mc = _coords()
    me = 4 * mx + 2 * my + mc
    s, d = x.shape[1], x.shape[2]
    aw = d // 2
    x2, target = x[0], loss_target[0]
    n_mod = w_ada.shape[2] * N_DEV // d
    mod_cols = w_ada.shape[2]

    def shard(w, transpose):
        return (w[0].T if transpose else w[0]).astype(BF16)

    cw_shard = conv_dw_w.shape[3]
    n_taps = CONV_KERNEL * cw_shard
    first_len = -(-(d + n_taps) // LANES) * LANES
    first = jnp.concatenate([c, conv_dw_w[0, :, 0, :].reshape(1, n_taps), jnp.zeros((1, first_len - d - n_taps), F32)], axis=1)
    first_all, silu_c, _, mod_all, wg1, wu1 = _standalone(
        _Together(_ModExchange(_pad_rows(first, 8), w_ada[0]),
                  _GatherWeights([shard(ffn1_w_gate, True), shard(ffn1_w_up, True)])), "ag_first")
    first_all = first_all[0::8]
    conv_w = first_all[:, d:d + n_taps].reshape(N_DEV, CONV_KERNEL, cw_shard).transpose(1, 0, 2).reshape(CONV_KERNEL, aw)

    mod_all = mod_all.reshape(N_DEV, N_DEV, mod_cols)
    mod = lax.dynamic_index_in_dim(mod_all, me, axis=1, keepdims=False).reshape(1, n_mod * d) + b_ada
    sh1, sc1, g1, sh2, sc2, g2, sh3, sc3, g3 = [mod[:, i * d:(i + 1) * d] for i in range(n_mod)]

    def split(g):
        return g.reshape(N_CHIP, 2, g.shape[0] // N_DEV, g.shape[1])

    def partials(g4s, lands, tag):
        return _chip_partials(g4s, lands, "chip_partials_" + tag)

    (n1, silu1, gs1, hid1), (wd1, win_t, wout) = _norm_ffn_up(
        x2, ffn1_norm_g, sc1, sh1, wg1, wu1, "ffn1_up",
        comm=_GatherWeights([shard(ffn1_w_down, False), shard(w_in, True), shard(w_out, False)]))
    h1, f1, n2 = _residual_mm(hid1, wd1, x2, g1, 0.5, "ffn1_down", norm=(mix_norm_g, sc2, sh2))
    cos, sin_signed = _rope_tables(s, LANES)
    proj, = _proj_rope(n2, win_t, cos, sin_signed, aw, "proj")
    lanes_per = aw // LANES
    (attn, lse), (wg2, wu2, wd2) = _attn_seq_fwd(
        proj, aw, "attn_fwd",
        comm=_GatherWeights([shard(ffn2_w_gate, True), shard(ffn2_w_up, True), shard(ffn2_w_down, False)]))
    u1, = _conv_fwd(proj, 3 * lanes_per, 4 * lanes_per, conv_w, conv_dw_b, "conv_fwd")
    post = (attn_out_g, conv_ln_g, conv_ln_b, conv_out_g)
    y, h2, mix, n3 = _mix_out(attn, u1, post, wout, h1, g2, (ffn2_norm_g, sc3, sh3), "mix_out")
    silu3, gs3, hid3 = _ffn_up(n3, wg2, wu2, "ffn2_up")

    dh3, df3, err2, d_final_g, dg3 = _last_mm_loss(hid3, wd2, h2, g3, 0.5, target, final_norm_g.reshape(1, d),
                                                   "ffn2_down_loss")
    loss_part = jnp.zeros((1, LANES), F32).at[0, 0].set(0.5 * jnp.sum(err2) / d)

    da3, db3 = _ffn_bwd_hidden(df3, wd2, silu3, gs3, "ffn2_hidden_bwd")
    g4_a = [split(_mm_tn(da3, n3, "ffn2_dwg")), split(_mm_tn(db3, n3, "ffn2_dwu")), split(_mm_tn(hid3, df3, "ffn2_dwd"))]
    (dh2, dmix, dsh3, dsc3, dgn3, dg2), land_a = _mm_norm_mod_bwd(
        [(da3, wg2), (db3, wu2)], h2, dh3, ffn2_norm_g, sc3, (mix, g2, 1.0), "ffn2_dn_norm3_bwd", tm=256,
        comm=_SiblingExchange(g4_a))
    parts_a = partials(g4_a, land_a, "a")
    g_wout = _mm_tn(y, dmix, "mix_dwout")
    dattn, du1, d_gains, d_ln = _mix_dy_post_bwd(dmix, wout, attn, u1, post, "mix_dy_post_bwd")
    d_attn_g, d_conv_g, d_ln_g, d_ln_b = d_gains[:, :aw], d_gains[:, aw:], d_ln[:, :aw], d_ln[:, aw:]
    dga, dgb, d_taps, d_conv_b = _conv_bwd(proj, 3 * lanes_per, 4 * lanes_per, conv_w, du1, "conv_bwd")
    (dq, dk, dv), sums_a = _attn_seq_bwd(proj, dattn, attn, lse, cos, sin_signed, "attn_bwd",
                                         comm=_ChipExchange(parts_a))
    dproj = jnp.concatenate([dq, dk, dv, dga, dgb], axis=1)
    g4_b = [split(g_wout), split(_mm_tn(dproj, n2, "mix_dwin"))]
    (dh1, df1, dsh2, dsc2, dgn2, dg1), land_b = _mm_norm_mod_bwd(
        [(dproj, win_t)], h1, dh2, mix_norm_g, sc2, (f1, g1, 0.5), "mix_dn_norm2_bwd", tm=512,
        comm=_SiblingExchange(g4_b))
    parts_b = partials(g4_b, land_b, "b")
    g4_c = [split(_mm_tn(hid1, df1, "ffn1_dwd"))]
    (da1, db1), both = _ffn_bwd_hidden(df1, wd1, silu1, gs1, "ffn1_hidden_bwd",
                                       comm=_Together(_ChipExchange(parts_b), _SiblingExchange(g4_c)))
    sums_b, land_c = both[:2], both[2:]
    parts_c = partials(g4_c, land_c, "c")
    g4_d = [split(_mm_tn(db1, n1, "ffn1_dwu"))]
    g_wg1, both = _mm_tn(da1, n1, "ffn1_dwg", comm=_Together(_ChipExchange(parts_c), _SiblingExchange(g4_d)))
    sums_c, land_d = both[:1], both[1:]
    parts_d = partials(g4_d, land_d, "d")
    g4_e = [split(g_wg1)]
    dn1, both = _plain_mm([(da1, wg1), (db1, wu1)], BF16, False, d, "ffn1_dn",
                          comm=_Together(_ChipExchange(parts_d), _SiblingExchange(g4_e)))
    sums_d, land_e = both[:1], both[1:]
    parts_e = partials(g4_e, land_e, "e")
    (dx, dsh1, dsc1, dgn1), sums_e = _norm_mod_bwd(dn1, x2, dh1, ffn1_norm_g, sc1, "norm1_bwd",
                                                   comm=_ChipExchange(parts_e))

    dmod = jnp.concatenate([dsh1, dsc1, dg1, dsh2, dsc2, dg2, dsh3, dsc3, dg3], axis=1)
    small = [dmod, dgn1, dgn2, dgn3, d_final_g, d_conv_b, d_ln_g, d_ln_b, d_attn_g, d_conv_g,
             d_taps.reshape(1, CONV_KERNEL * aw), loss_part]
    sizes = [v.shape[1] for v in small]
    total = sum(sizes)
    padded = -(-total // (8 * LANES)) * (8 * LANES)
    packed = jnp.concatenate(small + [jnp.zeros((1, padded - total), F32)], axis=1).reshape(8, padded // 8)
    gathered = _ag_small(packed, "ag_small_grads")
    summed = _sum_blocks(gathered, N_DEV, "sum_small_grads").reshape(1, padded)
    offs = [sum(sizes[:i]) for i in range(len(sizes))]
    (g_b_ada, g_gn1, g_gn2, g_gn3, g_final, g_conv_b, g_ln_g, g_ln_b, g_attn_g, g_conv_g, g_taps, loss_row) = [
        summed[:, o:o + n] for o, n in zip(offs, sizes)]
    loss = loss_row[0, 0]
    g_taps_shard = lax.dynamic_slice_in_dim(g_taps.reshape(CONV_KERNEL, aw), me * cw_shard, cw_shard, axis=1)
    dmod_all = gathered.reshape(N_DEV, padded)[:, :n_mod * d]
    dmod_cols = lax.dynamic_slice_in_dim(dmod_all, me * mod_cols, mod_cols, axis=1)
    g_w_ada = _mm_tn(silu_c, dmod_cols, "ada_dw")

    arrived = dict(zip(["ffn2_w_gate", "ffn2_w_up", "ffn2_w_down", "w_out", "w_in", "ffn1_w_down", "ffn1_w_up",
                        "ffn1_w_gate"], list(sums_a) + list(sums_b) + list(sums_c) + list(sums_d) + list(sums_e)))
    transposed = ("ffn1_w_gate", "ffn1_w_up", "w_in", "ffn2_w_gate", "ffn2_w_up")
    grads = {
        "w_ada": g_w_ada, "b_ada": g_b_ada, "ffn1_norm_g": g_gn1, "mix_norm_g": g_gn2, "conv_dw_w": g_taps_shard,
        "conv_dw_b": g_conv_b, "conv_ln_g": g_ln_g, "conv_ln_b": g_ln_b, "attn_out_g": g_attn_g,
        "conv_out_g": g_conv_g, "ffn2_norm_g": g_gn3, "final_norm_g": g_final,
    }
    weights = dict(w_ada=w_ada, b_ada=b_ada, ffn1_norm_g=ffn1_norm_g, ffn1_w_gate=ffn1_w_gate, ffn1_w_up=ffn1_w_up, ffn1_w_down=ffn1_w_down, mix_norm_g=mix_norm_g, w_in=w_in, conv_dw_w=conv_dw_w, conv_dw_b=conv_dw_b, conv_ln_g=conv_ln_g, conv_ln_b=conv_ln_b, attn_out_g=attn_out_g, conv_out_g=conv_out_g, w_out=w_out, ffn2_norm_g=ffn2_norm_g, ffn2_w_gate=ffn2_w_gate, ffn2_w_up=ffn2_w_up, ffn2_w_down=ffn2_w_down, final_norm_g=final_norm_g)
    moms = dict(w_ada=m_w_ada, b_ada=m_b_ada, ffn1_norm_g=m_ffn1_norm_g, ffn1_w_gate=m_ffn1_w_gate, ffn1_w_up=m_ffn1_w_up, ffn1_w_down=m_ffn1_w_down, mix_norm_g=m_mix_norm_g, w_in=m_w_in, conv_dw_w=m_conv_dw_w, conv_dw_b=m_conv_dw_b, conv_ln_g=m_conv_ln_g, conv_ln_b=m_conv_ln_b, attn_out_g=m_attn_out_g, conv_out_g=m_conv_out_g, w_out=m_w_out, ffn2_norm_g=m_ffn2_norm_g, ffn2_w_gate=m_ffn2_w_gate, ffn2_w_up=m_ffn2_w_up, ffn2_w_down=m_ffn2_w_down, final_norm_g=m_final_norm_g)
    vars_ = dict(w_ada=v_w_ada, b_ada=v_b_ada, ffn1_norm_g=v_ffn1_norm_g, ffn1_w_gate=v_ffn1_w_gate, ffn1_w_up=v_ffn1_w_up, ffn1_w_down=v_ffn1_w_down, mix_norm_g=v_mix_norm_g, w_in=v_w_in, conv_dw_w=v_conv_dw_w, conv_dw_b=v_conv_dw_b, conv_ln_g=v_conv_ln_g, conv_ln_b=v_conv_ln_b, attn_out_g=v_attn_out_g, conv_out_g=v_conv_out_g, w_out=v_w_out, ffn2_norm_g=v_ffn2_norm_g, ffn2_w_gate=v_ffn2_w_gate, ffn2_w_up=v_ffn2_w_up, ffn2_w_down=v_ffn2_w_down, final_norm_g=v_final_norm_g)
    names = list(weights)
    big = ["w_ada", "ffn1_w_gate", "ffn1_w_up", "ffn1_w_down", "w_in", "w_out", "ffn2_w_gate", "ffn2_w_up",
           "ffn2_w_down"]
    shape2 = {n: (weights[n].shape[-2] if weights[n].ndim > 1 else 1, weights[n].shape[-1]) for n in names}
    shape2["conv_dw_w"] = (CONV_KERNEL, cw_shard)
    g_out, d_out, m_out, v_out = {}, {}, {}, {}
    for n in big:
        if n in arrived:
            def view(t, n=n):
                return t[0].T if n in transposed else t[0]
            res = _adamw_reduced(view(weights[n]), arrived[n], view(moms[n]), view(vars_[n]), "adamw_" + n)
            g_out[n], d_out[n], m_out[n], v_out[n] = [r.T if n in transposed else r for r in res]
        else:
            g2d = grads[n].reshape(shape2[n])
            res = _adamw_big(weights[n].reshape(shape2[n]), g2d, moms[n].reshape(shape2[n]),
                             vars_[n].reshape(shape2[n]), "adamw_" + n)
            g_out[n], (d_out[n], m_out[n], v_out[n]) = g2d, res
    rest = [n for n in names if n not in big]
    res = _adamw_small([weights[n].reshape(shape2[n]) for n in rest], [grads[n].reshape(shape2[n]) for n in rest],
                       [moms[n].reshape(shape2[n]) for n in rest], [vars_[n].reshape(shape2[n]) for n in rest],
                       "adamw_small")
    for i, n in enumerate(rest):
        g_out[n], d_out[n], m_out[n], v_out[n] = grads[n], res[0][i], res[1][i], res[2][i]

    def shaped(table):
        return [table[n].reshape(weights[n].shape) for n in names]

    return (loss, dx.reshape(x.shape), *shaped(g_out), *shaped(d_out), *shaped(m_out), *shaped(v_out))
```

```python
import functools

import jax
import jax.numpy as jnp
from jax import lax
from jax.experimental import pallas as pl
from jax.experimental.pallas import tpu as pltpu

F32 = jnp.float32
BF16 = jnp.bfloat16
MESH = pl.DeviceIdType.MESH
ANY = pl.BlockSpec(memory_space=pl.ANY)

N_DEV = 8
N_CHIP = 4
HEAD_DIM = 64
HALF_HEAD = HEAD_DIM // 2
LANES = 128
BLOCK = 128
DILATIONS = (1, 4, 16)
MERGE_CHUNK = 512
ROPE_THETA = 10000.0
CONV_KERNEL = 31
CONV_HALO = 32
CONV_CHUNK = 512
CONV_SUB = 128
RMS_EPS = 1e-6
LN_EPS = 1e-5
ADAM_LR = 0.001
ADAM_B1 = 0.9
ADAM_B2 = 0.999
ADAM_EPS = 1e-08
ADAM_WD = 0.01
ADAM_STEP = 10
VMEM_LIMIT = 56 * 1024 * 1024
NEG = -1e30


def _params(n_axes):
    return pltpu.CompilerParams(dimension_semantics=("arbitrary",) * n_axes, vmem_limit_bytes=VMEM_LIMIT)


def _tile(n, target, unit):
    best = None
    for t in range(unit, min(n, target) + 1, unit):
        if n % t == 0:
            best = t
    return best if best is not None else n


def _sigmoid(x):
    return 0.5 * (jnp.tanh(0.5 * x) + 1.0)


def _call(body, *, grid, in_specs, out_specs, out_shape, args, name, scratch_shapes=(), comm=None):
    params = _params(len(grid))
    if comm is None:
        return pl.pallas_call(body, grid=grid, in_specs=list(in_specs), out_specs=list(out_specs),
                              out_shape=list(out_shape), scratch_shapes=list(scratch_shapes),
                              compiler_params=params, name=name)(*args)
    n_in, n_out, n_scr = len(args), len(out_shape), len(scratch_shapes)
    c_in, c_out = len(comm.inputs), len(comm.out_shapes)
    steps = 1
    for g in grid:
        steps *= g

    def hosted(*refs):
        pos = 0
        parts = []
        for size in (n_in, c_in, n_out, c_out, n_scr, len(comm.scratch)):
            parts.append(refs[pos:pos + size])
            pos += size
        ins, cin, outs, cout, scr, cscr = parts
        step = 0
        for axis, g in enumerate(grid):
            step = step * g + pl.program_id(axis)

        @pl.when(step == 0)
        def _():
            comm.start(cin, cout, cscr)

        body(*ins, *outs, *scr)
        if comm.mid is not None and steps >= 4:
            @pl.when(step == steps // 2)
            def _():
                comm.mid(cin, cout, cscr)

        @pl.when(step == steps - 1)
        def _():
            if comm.mid is not None and steps < 4:
                comm.mid(cin, cout, cscr)
            comm.finish(cin, cout, cscr)

    res = pl.pallas_call(
        hosted, grid=grid, in_specs=list(in_specs) + [ANY] * c_in, out_specs=list(out_specs) + [ANY] * c_out,
        out_shape=list(out_shape) + list(comm.out_shapes), scratch_shapes=list(scratch_shapes) + list(comm.scratch),
        compiler_params=params, name=name)(*args, *comm.inputs)
    return res[:n_out], res[n_out:]


def _rows(fn, rows_in, vecs_in, rows_out, vecs_out, *, tile, name, comm=None):
    norm = [r if isinstance(r, tuple) else (r, r.shape[1], 0) for r in rows_in]
    n_rows = norm[0][0].shape[0]
    n_tiles = n_rows // tile
    in_specs, args = [], []
    for arr, width, cb in norm:
        in_specs.append(pl.BlockSpec((tile, width), functools.partial(lambda i, cb: (i, cb), cb=cb)))
        args.append(arr)
    for v in vecs_in:
        in_specs.append(pl.BlockSpec((1, v.shape[1]), lambda i: (0, 0)))
        args.append(v)
    out_shape = [jax.ShapeDtypeStruct((n_rows, w), dt) for w, dt in rows_out]
    out_shape += [jax.ShapeDtypeStruct((1, w), F32) for w in vecs_out]
    out_specs = [pl.BlockSpec((tile, w), lambda i: (i, 0)) for w, _ in rows_out]
    out_specs += [pl.BlockSpec((1, w), lambda i: (0, 0)) for w in vecs_out]
    n_in, n_ro = len(args), len(rows_out)

    def body(*refs):
        vals = [r[...] for r in refs[:n_in]]
        outs = refs[n_in:]
        row_vals, vec_vals = fn(*vals)
        for ref, val in zip(outs[:n_ro], row_vals):
            if isinstance(val, tuple):
                w = val[0].shape[1]
                for j, piece in enumerate(val):
                    ref[:, j * w:(j + 1) * w] = piece.astype(ref.dtype)
            else:
                ref[...] = val.astype(ref.dtype)
        if vecs_out:
            @pl.when(pl.program_id(0) == 0)
            def _():
                for ref in outs[n_ro:]:
                    ref[...] = jnp.zeros_like(ref)
            for ref, val in zip(outs[n_ro:], vec_vals):
                ref[...] += val

    return _call(body, grid=(n_tiles,), in_specs=in_specs, out_specs=out_specs, out_shape=out_shape, args=args,
                 name=name, comm=comm)


def _colsum(x):
    return jnp.sum(x, axis=0, keepdims=True)


def _rms_stats(h):
    r = lax.rsqrt(jnp.mean(h * h, axis=-1, keepdims=True) + RMS_EPS)
    return r, h * r


def _rms_back(r, xn, dxn):
    return r * (dxn - xn * jnp.mean(dxn * xn, axis=-1, keepdims=True))


def _branch_back(dh, f, gate, coef):
    return (coef * gate) * dh, coef * _colsum(f.astype(F32) * dh)


def _norm_mod_back(dn, h, dh_in, gain, scale):
    dn = dn.astype(F32)
    r, xn = _rms_stats(h)
    y = xn * gain
    dy = dn * (1.0 + scale)
    dh = dh_in + _rms_back(r, xn, dy * gain)
    return dh, [_colsum(dn), _colsum(dn * y), _colsum(dy * xn)]


def _norm_mod_bwd(dn, h, dh_in, gain, scale, name, comm=None):
    d = h.shape[1]

    def fn(dn, h, dh_in, gain, scale):
        dh, vecs = _norm_mod_back(dn, h, dh_in, gain, scale)
        return [dh], vecs
    return _rows(fn, [dn, h, dh_in], [gain, scale], [(d, F32)], [d, d, d], tile=256, name=name, comm=comm)


def _mm_norm_mod_bwd(pairs, h, dh_in, gain, scale, branch, name, tm, comm=None):
    f, gate, coef = branch

    def epi(accs, ex, vc):
        dh, vecs = _norm_mod_back(accs[0], ex[0], ex[1], vc[0], vc[1])
        df, dgate = _branch_back(dh, ex[2], vc[2], coef)
        return [dh, df] + vecs + [dgate]
    return _mm([pairs], epi, [h, dh_in, f], [gain, scale, gate], [F32, BF16], trans_rhs=False, tm=tm,
               tn=h.shape[1], name=name, n_sums=4, comm=comm)


def _last_mm_loss(lhs, w, res, gate, coef, target, gain, name):
    d = w.shape[1]

    def epi(accs, ex, vc):
        f = accs[0]
        h = ex[0] + (coef * vc[0]) * f
        r, xn = _rms_stats(h)
        err = xn * vc[1] - ex[1]
        dout = err * (1.0 / d)
        dh = _rms_back(r, xn, dout * vc[1])
        df, dgate = _branch_back(dh, f, vc[0], coef)
        return [dh, df, _colsum(err * err), _colsum(dout * xn), dgate]
    return _mm([[(lhs, w)]], epi, [res, target], [gate, gain], [F32, BF16], trans_rhs=False, tm=256, tn=d,
               name=name, n_sums=3)


def _partner(x):
    if x.shape[1] > LANES:
        return jnp.concatenate([_partner(x[:, c:c + LANES]) for c in range(0, x.shape[1], LANES)], axis=1)
    lane = lax.broadcasted_iota(jnp.int32, x.shape, 1) % HEAD_DIM
    return jnp.where(lane < HALF_HEAD, pltpu.roll(x, LANES - HALF_HEAD, 1), pltpu.roll(x, HALF_HEAD, 1))


def _proj_rope(n, w_t, cos, sin_signed, width, name, comm=None):
    s, kdim = n.shape
    n_cols = w_t.shape[0]
    tm = _tile(s, 1024, 8)
    qscale = HEAD_DIM ** -0.5

    chunk = _tile(tm, 256, 8)

    def body(n_ref, w_ref, cos_ref, sin_ref, o_ref):
        j = pl.program_id(0)

        def products(rows):
            return lax.dot_general(n_ref[rows, :].astype(BF16), w_ref[...].astype(BF16), (((1,), (1,)), ((), ())),
                                   preferred_element_type=F32)

        @pl.when(j >= 2)
        def _():
            for c in range(tm // chunk):
                rows = slice(c * chunk, (c + 1) * chunk)
                o_ref[rows, :] = products(rows)

        @pl.when(j < 2)
        def _():
            scale = jnp.where(j == 0, qscale, 1.0)
            for c in range(tm // chunk):
                rows = slice(c * chunk, (c + 1) * chunk)
                acc = products(rows)
                cos = jnp.tile(cos_ref[rows, :], (1, width // LANES))
                sin = jnp.tile(sin_ref[rows, :], (1, width // LANES))
                o_ref[rows, :] = scale * (acc * cos + _partner(acc) * sin)

    table = pl.BlockSpec((tm, LANES), lambda j, i: (jnp.where(j < 2, i, 0), 0))
    return _call(
        body, grid=(n_cols // width, s // tm),
        in_specs=[pl.BlockSpec((tm, kdim), lambda j, i: (i, 0)), pl.BlockSpec((width, kdim), lambda j, i: (j, 0)),
                  table, table],
        out_specs=[pl.BlockSpec((tm, width), lambda j, i: (i, j))],
        out_shape=[jax.ShapeDtypeStruct((s, n_cols), F32)], args=(n, w_t, cos, sin_signed), name=name, comm=comm)


def _mix_post(attn, u1, attn_g, ln_g, ln_b, conv_g):
    _, xa = _rms_stats(attn)
    mu = jnp.mean(u1, axis=-1, keepdims=True)
    xc = u1 - mu
    rstd = lax.rsqrt(jnp.mean(xc * xc, axis=-1, keepdims=True) + LN_EPS)
    u2 = (xc * rstd) * ln_g + ln_b
    u3 = u2 * _sigmoid(u2)
    _, x3 = _rms_stats(u3)
    return jnp.concatenate([xa * attn_g, x3 * conv_g], axis=1)


def _mix_post_back(dy, attn, u1, attn_g, ln_g, ln_b, conv_g):
    w = attn.shape[1]
    dya, dyc = dy[:, :w], dy[:, w:]
    ra, xa = _rms_stats(attn)
    dattn = _rms_back(ra, xa, dya * attn_g)
    mu = jnp.mean(u1, axis=-1, keepdims=True)
    xc = u1 - mu
    rstd = lax.rsqrt(jnp.mean(xc * xc, axis=-1, keepdims=True) + LN_EPS)
    xh = xc * rstd
    u2 = xh * ln_g + ln_b
    sig = _sigmoid(u2)
    u3 = u2 * sig
    r3, x3 = _rms_stats(u3)
    du3 = _rms_back(r3, x3, dyc * conv_g)
    du2 = du3 * (sig + u3 * (1.0 - sig))
    dxh = du2 * ln_g
    du1 = rstd * (dxh - jnp.mean(dxh, axis=-1, keepdims=True) - xh * jnp.mean(dxh * xh, axis=-1, keepdims=True))
    return dattn, du1, [_colsum(dya * xa), _colsum(dyc * x3), _colsum(du2 * xh), _colsum(du2)]


def _mm(groups, epi, extras, vecs, outs, *, trans_rhs, tm, tn, name, n_sums=0, pre=None, pre_inputs=(),
        comm=None):
    m = (pre_inputs[0] if pre is not None else groups[0][0][0]).shape[0]
    n = groups[0][0][1].shape[0] if trans_rhs else groups[0][0][1].shape[1]
    tm, tn = min(tm, m), min(tn, n)
    in_specs, args, uses_pre = [], [], []
    for grp in groups:
        for lhs, rhs in grp:
            k = rhs.shape[1] if trans_rhs else rhs.shape[0]
            uses_pre.append(lhs is None)
            if lhs is not None:
                in_specs.append(pl.BlockSpec((tm, k), lambda j, i: (i, 0)))
                args.append(lhs)
            in_specs.append(pl.BlockSpec((tn, k), lambda j, i: (j, 0)) if trans_rhs
                            else pl.BlockSpec((k, tn), lambda j, i: (0, j)))
            args.append(rhs)
    n_mm = len(args)
    for p in pre_inputs:
        in_specs.append(pl.BlockSpec((tm, p.shape[1]), lambda j, i: (i, 0)))
        args.append(p)
    for e in extras:
        in_specs.append(pl.BlockSpec((tm, tn), lambda j, i: (i, j)) if e.shape[1] == n
                        else pl.BlockSpec((tm, e.shape[1]), lambda j, i: (i, 0)))
        args.append(e)
    for v in vecs:
        in_specs.append(pl.BlockSpec((1, tn), lambda j, i: (0, j)) if v.shape[1] == n
                        else pl.BlockSpec((1, v.shape[1]), lambda j, i: (0, 0)))
        args.append(v)
    sizes = [len(g) for g in groups]
    n_pre, n_ex, n_vec = len(pre_inputs), len(extras), len(vecs)
    dims = (((1,), (1,)), ((), ())) if trans_rhs else (((1,), (0,)), ((), ()))
    out_specs, out_shape = [], []
    if pre is not None:
        k_pre = args[n_mm - 1].shape[1] if trans_rhs else args[n_mm - 1].shape[0]
        out_specs.append(pl.BlockSpec((tm, k_pre), lambda j, i: (i, 0)))
        out_shape.append(jax.ShapeDtypeStruct((m, k_pre), BF16))
    for o in outs:
        dt, width = o if isinstance(o, tuple) else (o, n)
        out_specs.append(pl.BlockSpec((tm, tn), lambda j, i: (i, j)) if width == n
                         else pl.BlockSpec((tm, width), lambda j, i: (i, 0)))
        out_shape.append(jax.ShapeDtypeStruct((m, width), dt))
    n_tiles_out = len(out_specs)
    out_specs += [pl.BlockSpec((1, tn), lambda j, i: (0, j))] * n_sums
    out_shape += [jax.ShapeDtypeStruct((1, n), F32)] * n_sums

    def body(*refs):
        ins = refs[:n_mm + n_pre + n_ex + n_vec]
        out_refs = refs[n_mm + n_pre + n_ex + n_vec:]
        vc = [r[...] for r in ins[n_mm + n_pre + n_ex:]]
        vals = []
        made = None
        if pre is not None:
            made = pre([r[...] for r in ins[n_mm:n_mm + n_pre]], vc).astype(BF16)
            vals.append(made)
        accs, pos, pair = [], 0, 0
        for size in sizes:
            acc = None
            for _ in range(size):
                if uses_pre[pair]:
                    lhs_tile = made
                else:
                    lhs_tile = ins[pos][...].astype(BF16)
                    pos += 1
                part = lax.dot_general(lhs_tile, ins[pos][...].astype(BF16), dims, preferred_element_type=F32)
                acc = part if acc is None else acc + part
                pos += 1
                pair += 1
            accs.append(acc)
        ex = [r[...] for r in ins[n_mm + n_pre:n_mm + n_pre + n_ex]]
        vals += epi(accs, ex, vc)
        for ref, val in zip(out_refs[:n_tiles_out], vals):
            ref[...] = val.astype(ref.dtype)
        if n_sums:
            @pl.when(pl.program_id(1) == 0)
            def _():
                for ref in out_refs[n_tiles_out:]:
                    ref[...] = jnp.zeros_like(ref)
            for ref, val in zip(out_refs[n_tiles_out:], vals[n_tiles_out:]):
                ref[...] += val

    return _call(body, grid=(n // tn, m // tm), in_specs=in_specs, out_specs=out_specs, out_shape=out_shape,
                 args=args, name=name, comm=comm)


def _mm_tn(lhs, rhs, name, comm=None):
    t, a = lhs.shape
    b = rhs.shape[1]
    ta = a if a <= 1536 else _tile(a, 1536, LANES)
    tk = _tile(t, 2048, 8)

    def body(l_ref, r_ref, o_ref):
        @pl.when(pl.program_id(1) == 0)
        def _():
            o_ref[...] = jnp.zeros_like(o_ref)
        o_ref[...] += lax.dot_general(l_ref[...].astype(BF16), r_ref[...].astype(BF16), (((0,), (0,)), ((), ())),
                                      preferred_element_type=F32)

    res = _call(body, grid=(a // ta, t // tk),
                in_specs=[pl.BlockSpec((tk, ta), lambda i, k: (k, i)), pl.BlockSpec((tk, b), lambda i, k: (k, 0))],
                out_specs=[pl.BlockSpec((ta, b), lambda i, k: (i, 0))], out_shape=[jax.ShapeDtypeStruct((a, b), F32)],
                args=(lhs, rhs), name=name, comm=comm)
    return res[0] if comm is None else (res[0][0], res[1])


def _ffn_tn(f):
    return _tile(f, 1536, LANES)


def _swiglu_parts(a, b):
    sig = _sigmoid(a)
    silu = a * sig
    return [silu, b * (sig + silu * (1.0 - sig)), silu * b]


def _ffn_up(n, wg_t, wu_t, name, comm=None):
    def epi(accs, ex, vc):
        return _swiglu_parts(accs[0], accs[1])
    return _mm([[(n, wg_t)], [(n, wu_t)]], epi, [], [], [BF16, BF16, BF16], trans_rhs=True, tm=512,
               tn=_ffn_tn(wg_t.shape[0]), name=name, comm=comm)


def _norm_ffn_up(h, gain, scale, shift, wg_t, wu_t, name, comm=None):
    def pre(tiles, vc):
        _, xn = _rms_stats(tiles[0])
        return (xn * vc[0]) * (1.0 + vc[1]) + vc[2]

    def epi(accs, ex, vc):
        return _swiglu_parts(accs[0], accs[1])
    return _mm([[(None, wg_t)], [(None, wu_t)]], epi, [], [gain, scale, shift], [BF16, BF16, BF16], trans_rhs=True,
               tm=256, tn=wg_t.shape[0], name=name, pre=pre, pre_inputs=[h], comm=comm)


def _mix_out(attn, u1, post, w, res, gate, norm, name):
    def pre(tiles, vc):
        return _mix_post(tiles[0], tiles[1], *vc[4:8])

    def epi(accs, ex, vc):
        h = ex[0] + vc[0] * accs[0]
        _, xn = _rms_stats(h)
        return [h, accs[0], (xn * vc[1]) * (1.0 + vc[2]) + vc[3]]
    return _mm([[(None, w)]], epi, [res], [gate] + list(norm) + list(post), [F32, BF16, BF16], trans_rhs=False,
               tm=512, tn=w.shape[1], name=name, pre=pre, pre_inputs=[attn, u1])


def _mix_dy_post_bwd(dmix, w, attn, u1, post, name):
    width = attn.shape[1]

    def epi(accs, ex, vc):
        dattn, du1, sums = _mix_post_back(accs[0], ex[0], ex[1], *vc)
        return [dattn, du1, jnp.concatenate(sums[0:2], axis=1), jnp.concatenate(sums[2:4], axis=1)]
    return _mm([[(dmix, w)]], epi, [attn, u1], list(post), [(F32, width), (F32, width)], trans_rhs=True, tm=256,
               tn=w.shape[0], name=name, n_sums=2)


def _residual_mm(lhs, w, res, gate, coef, name, norm=None, comm=None):
    def epi(accs, ex, vc):
        h = ex[0] + (coef * vc[0]) * accs[0]
        if norm is None:
            return [h, accs[0]]
        _, xn = _rms_stats(h)
        return [h, accs[0], (xn * vc[1]) * (1.0 + vc[2]) + vc[3]]
    vecs = [gate] + (list(norm) if norm is not None else [])
    outs = [F32, BF16] + ([BF16] if norm is not None else [])
    return _mm([[(lhs, w)]], epi, [res], vecs, outs, trans_rhs=False, tm=512, tn=w.shape[1], name=name, comm=comm)


def _ffn_bwd_hidden(df, wd, dhid_db, dhid_da, name, comm=None):
    def epi(accs, ex, vc):
        return [accs[0] * ex[1].astype(F32), accs[0] * ex[0].astype(F32)]
    return _mm([[(df, wd)]], epi, [dhid_db, dhid_da], [], [BF16, BF16], trans_rhs=True, tm=512,
               tn=_ffn_tn(wd.shape[0]), name=name, comm=comm)


def _plain_mm(pairs, out_dtype, trans_rhs, tn, name, tm=512, comm=None):
    def epi(accs, ex, vc):
        return [accs[0]]
    res = _mm([pairs], epi, [], [], [out_dtype], trans_rhs=trans_rhs, tm=tm, tn=tn, name=name, comm=comm)
    return res[0] if comm is None else (res[0][0], res[1])


HEADS_PER_TILE = LANES // HEAD_DIM


def _stack_heads(x):
    lane = lax.broadcasted_iota(jnp.int32, (1, LANES), 1)
    return jnp.concatenate([x * (lane // HEAD_DIM == h).astype(F32) for h in range(HEADS_PER_TILE)], axis=0)


def _unstack_heads(y):
    r = y.shape[0] // HEADS_PER_TILE
    lane = lax.broadcasted_iota(jnp.int32, (r, y.shape[1]), 1)
    out = y[0:r]
    for h in range(1, HEADS_PER_TILE):
        out = jnp.where(lane // HEAD_DIM == h, y[h * r:(h + 1) * r], out)
    return out


def _stacked_lse(lb):
    return jnp.concatenate([_lane_pick(lb, h) for h in range(HEADS_PER_TILE)], axis=0)


def _band_masks(n_row_blocks, n_col_blocks):
    shape = (n_row_blocks * BLOCK, n_col_blocks * BLOCK)
    qi = lax.broadcasted_iota(jnp.int32, shape, 0) % BLOCK
    kj = lax.broadcasted_iota(jnp.int32, shape, 1) % BLOCK
    return kj <= qi, kj >= qi


def _query_masks():
    first_valid, _ = _band_masks(HEADS_PER_TILE, 1)
    same_ok, before_ok = _band_masks(HEADS_PER_TILE, 2)
    is_cur = lax.broadcasted_iota(jnp.int32, same_ok.shape, 1) >= BLOCK
    return first_valid, jnp.logical_and(is_cur, same_ok), jnp.logical_and(jnp.logical_not(is_cur), before_ok)


def _dot_nt(a, b):
    return lax.dot_general(a.astype(BF16), b.astype(BF16), (((1,), (1,)), ((), ())), preferred_element_type=F32)


def _dot_nn(a, b):
    return lax.dot_general(a.astype(BF16), b.astype(BF16), (((1,), (0,)), ((), ())), preferred_element_type=F32)


def _dot_tn(a, b):
    return lax.dot_general(a.astype(BF16), b.astype(BF16), (((0,), (0,)), ((), ())), preferred_element_type=F32)


def _lane_pick(x, h):
    lane = lax.broadcasted_iota(jnp.int32, x.shape, 1)
    return jnp.sum(jnp.where(lane == h * HEAD_DIM, x, 0.0), axis=1, keepdims=True)


def _block_rows(idx, d):
    span = BLOCK * d
    q0 = (idx // d) * span + idx % d
    return pl.ds(q0, BLOCK, stride=d), pl.ds(q0 - span, BLOCK, stride=d)


def _two_loops(n_blocks, d, unroll):
    return d % unroll == 0 and (n_blocks - d) % unroll == 0 and n_blocks > d


def _branch_loops(n_blocks, d, visit, unroll, masks):
    first_valid, cur_part, prev_part = masks
    if _two_loops(n_blocks, d, unroll):
        full_valid = jnp.logical_or(cur_part, prev_part)

        def first(idx, carry):
            rows = pl.ds(idx, BLOCK, stride=d)
            visit(rows, [rows], first_valid)
            return carry

        def rest(idx, carry):
            rows, prev = _block_rows(idx, d)
            visit(rows, [prev, rows], full_valid)
            return carry

        lax.fori_loop(0, d, first, 0, unroll=unroll)
        lax.fori_loop(d, n_blocks, rest, 0, unroll=unroll)
        return

    def every(idx, carry):
        span = BLOCK * d
        q0 = (idx // d) * span + idx % d
        has_prev = idx >= d
        rows = pl.ds(q0, BLOCK, stride=d)
        prev = pl.ds(jnp.where(has_prev, q0 - span, q0), BLOCK, stride=d)
        visit(rows, [prev, rows], jnp.logical_or(cur_part, jnp.logical_and(prev_part, has_prev)))
        return carry

    lax.fori_loop(0, n_blocks, every, 0, unroll=unroll)


def _qkv_specs(s, tiles):
    q, k, v = [pl.BlockSpec((s, LANES), functools.partial(lambda hb, off: (0, off + hb), off=i * tiles))
               for i in range(3)]
    return q, k, v, pl.BlockSpec((s, LANES), lambda hb: (0, hb))


def _attn_seq_fwd(proj, width, name, comm=None):
    s = proj.shape[0]
    q_spec, k_spec, v_spec, cur = _qkv_specs(s, width // LANES)

    def body(q_ref, k_ref, v_ref, o_ref, l_ref, o_s, l_s):
        masks = _query_masks()
        for bi, d in enumerate(DILATIONS):
            def visit(rows, key_rows, valid, bi=bi):
                q2 = _stack_heads(q_ref[rows, :])
                keys = jnp.concatenate([k_ref[r, :] for r in key_rows], axis=0)
                vals = jnp.concatenate([v_ref[r, :] for r in key_rows], axis=0)
                sc = jnp.where(valid, _dot_nt(q2, keys), NEG)
                mx = jnp.max(sc, axis=1, keepdims=True)
                p = jnp.exp(sc - mx)
                den = jnp.sum(p, axis=1, keepdims=True)
                o_s[bi, rows, :] = _unstack_heads(_dot_nn(p, vals) / den)
                l_s[bi, rows, :] = _unstack_heads(jnp.broadcast_to(mx + jnp.log(den), (q2.shape[0], LANES)))

            _branch_loops(s // BLOCK, d, visit, 16, masks)
        for c in range(s // MERGE_CHUNK):
            rows = slice(c * MERGE_CHUNK, (c + 1) * MERGE_CHUNK)
            ls = [l_s[bi, rows, :] for bi in range(len(DILATIONS))]
            top = functools.reduce(jnp.maximum, ls)
            ws = [jnp.exp(l - top) for l in ls]
            den = functools.reduce(lambda a, b: a + b, ws)
            num = functools.reduce(lambda a, b: a + b, [w * o_s[bi, rows, :] for bi, w in enumerate(ws)])
            o_ref[rows, :] = num / den
            l_ref[rows, :] = top + jnp.log(den)

    return _call(
        body, grid=(width // LANES,), in_specs=[q_spec, k_spec, v_spec], out_specs=[cur, cur],
        out_shape=[jax.ShapeDtypeStruct((s, width), F32)] * 2,
        scratch_shapes=[pltpu.VMEM((len(DILATIONS), s, LANES), F32)] * 2,
        args=(proj, proj, proj), name=name, comm=comm)


def _attn_seq_bwd(proj, do, o, lse, cos, sin_signed, name, comm=None):
    s, width = do.shape
    q_spec, k_spec, v_spec, cur = _qkv_specs(s, width // LANES)
    table = pl.BlockSpec((s, LANES), lambda hb: (0, 0))
    qscale = HEAD_DIM ** -0.5

    def body(q_ref, k_ref, v_ref, do_ref, o_ref, l_ref, cos_ref, sin_ref, dq_out, dk_out, dv_out,
             dq_ref, dk_ref, dv_ref):
        unroll = 8
        masks = _query_masks()
        order = sorted(DILATIONS, reverse=True)
        fresh_first = _two_loops(s // BLOCK, order[0], unroll)
        if not fresh_first:
            dq_ref[...] = jnp.zeros_like(dq_ref)
            dk_ref[...] = jnp.zeros_like(dk_ref)
            dv_ref[...] = jnp.zeros_like(dv_ref)
        for d in order:
            def visit(rows, key_rows, valid, fresh=fresh_first and d == order[0]):
                dob = do_ref[rows, :]
                q2 = _stack_heads(q_ref[rows, :])
                do2 = _stack_heads(dob)
                delta = jnp.sum(_stack_heads(dob * o_ref[rows, :]), axis=1, keepdims=True)
                lse2 = _stacked_lse(l_ref[rows, :])
                keys = jnp.concatenate([k_ref[r, :] for r in key_rows], axis=0)
                vals = jnp.concatenate([v_ref[r, :] for r in key_rows], axis=0)
                p = jnp.where(valid, jnp.exp(_dot_nt(q2, keys) - lse2), 0.0)
                ds = p * (_dot_nt(do2, vals) - delta)
                dq = _unstack_heads(_dot_nn(ds, keys))
                dkk = _dot_tn(ds, q2)
                dvv = _dot_tn(p, do2)
                if fresh:
                    dq_ref[rows, :] = dq
                else:
                    dq_ref[rows, :] += dq
                for i, r in enumerate(key_rows):
                    own = i == len(key_rows) - 1
                    if fresh and own:
                        dk_ref[r, :] = dkk[i * BLOCK:(i + 1) * BLOCK]
                        dv_ref[r, :] = dvv[i * BLOCK:(i + 1) * BLOCK]
                    else:
                        dk_ref[r, :] += dkk[i * BLOCK:(i + 1) * BLOCK]
                        dv_ref[r, :] += dvv[i * BLOCK:(i + 1) * BLOCK]

            _branch_loops(s // BLOCK, d, visit, unroll, masks)
        for c in range(s // MERGE_CHUNK):
            rows = slice(c * MERGE_CHUNK, (c + 1) * MERGE_CHUNK)
            cos, sin = cos_ref[rows, :], sin_ref[rows, :]
            dq, dk = dq_ref[rows, :], dk_ref[rows, :]
            dq_out[rows, :] = ((dq * cos - _partner(dq) * sin) * qscale).astype(BF16)
            dk_out[rows, :] = (dk * cos - _partner(dk) * sin).astype(BF16)
            dv_out[rows, :] = dv_ref[rows, :].astype(BF16)

    return _call(
        body, grid=(width // LANES,), in_specs=[q_spec, k_spec, v_spec, cur, cur, cur, table, table],
        out_specs=[cur, cur, cur], out_shape=[jax.ShapeDtypeStruct((s, width), BF16)] * 3,
        scratch_shapes=[pltpu.VMEM((s, LANES), F32)] * 3,
        args=(proj, proj, proj, do, o, lse, cos, sin_signed), name=name, comm=comm)


def _conv_specs(s, a_block, b_block):
    per = CONV_CHUNK // CONV_HALO
    a_cur = pl.BlockSpec((CONV_CHUNK, LANES), lambda cb, i: (i, a_block + cb))
    b_cur = pl.BlockSpec((CONV_CHUNK, LANES), lambda cb, i: (i, b_block + cb))
    a_halo = pl.BlockSpec((CONV_HALO, LANES), lambda cb, i: (jnp.maximum(i * per - 1, 0), a_block + cb))
    b_halo = pl.BlockSpec((CONV_HALO, LANES), lambda cb, i: (jnp.maximum(i * per - 1, 0), b_block + cb))
    w_spec = pl.BlockSpec((CONV_KERNEL, LANES), lambda cb, i: (0, cb))
    vec = pl.BlockSpec((1, LANES), lambda cb, i: (0, cb))
    out = pl.BlockSpec((CONV_CHUNK, LANES), lambda cb, i: (i, cb))
    return a_cur, b_cur, a_halo, b_halo, w_spec, vec, out


def _fill_glu_window(win, a_ref, b_ref, ah_ref, bh_ref, first):
    halo = ah_ref[...] * _sigmoid(bh_ref[...])
    win[0:CONV_HALO, :] = jnp.where(first, 0.0, halo)
    win[CONV_HALO:, :] = a_ref[...] * _sigmoid(b_ref[...])


def _conv_fwd(proj, a_block, b_block, w, bias, name, comm=None):
    s = proj.shape[0]
    cw = w.shape[1]
    a_cur, b_cur, a_halo, b_halo, w_spec, vec, out = _conv_specs(s, a_block, b_block)
    lead = CONV_HALO - (CONV_KERNEL - 1)

    def body(a_ref, b_ref, ah_ref, bh_ref, w_ref, bias_ref, o_ref, win):
        _fill_glu_window(win, a_ref, b_ref, ah_ref, bh_ref, pl.program_id(1) == 0)
        for sub in range(CONV_CHUNK // CONV_SUB):
            base = sub * CONV_SUB
            acc = jnp.zeros((CONV_SUB, LANES), F32) + bias_ref[...]
            for j in range(CONV_KERNEL):
                acc = acc + w_ref[j:j + 1, :] * win[base + lead + j:base + lead + j + CONV_SUB, :]
            o_ref[base:base + CONV_SUB, :] = acc

    return _call(
        body, grid=(cw // LANES, s // CONV_CHUNK), in_specs=[a_cur, b_cur, a_halo, b_halo, w_spec, vec],
        out_specs=[out], out_shape=[jax.ShapeDtypeStruct((s, cw), F32)],
        scratch_shapes=[pltpu.VMEM((CONV_CHUNK + CONV_HALO, LANES), F32)],
        args=(proj, proj, proj, proj, w, bias), name=name, comm=comm)


def _conv_bwd(proj, a_block, b_block, w, du1, name):
    s = proj.shape[0]
    cw = w.shape[1]
    a_cur, b_cur, a_halo, b_halo, w_spec, vec, out = _conv_specs(s, a_block, b_block)
    per = CONV_CHUNK // CONV_HALO
    n_chunks = s // CONV_CHUNK
    d_next = pl.BlockSpec((CONV_HALO, LANES), lambda cb, i: (jnp.minimum((i + 1) * per, s // CONV_HALO - 1), cb))
    lead = CONV_HALO - (CONV_KERNEL - 1)

    def body(a_ref, b_ref, ah_ref, bh_ref, w_ref, d_ref, dn_ref, da_ref, db_ref, dw_ref, dbias_ref, win, dwin):
        i = pl.program_id(1)
        _fill_glu_window(win, a_ref, b_ref, ah_ref, bh_ref, i == 0)
        dwin[0:CONV_CHUNK, :] = d_ref[...]
        dwin[CONV_CHUNK:, :] = jnp.where(i == n_chunks - 1, 0.0, dn_ref[...])

        @pl.when(i == 0)
        def _():
            dw_ref[...] = jnp.zeros_like(dw_ref)
            dbias_ref[...] = jnp.zeros_like(dbias_ref)

        dbias_ref[...] += _colsum(d_ref[...])
        for sub in range(CONV_CHUNK // CONV_SUB):
            base = sub * CONV_SUB
            dcur = dwin[base:base + CONV_SUB, :]
            du0 = jnp.zeros((CONV_SUB, LANES), F32)
            for j in range(CONV_KERNEL):
                back = CONV_KERNEL - 1 - j
                du0 = du0 + w_ref[j:j + 1, :] * dwin[base + back:base + back + CONV_SUB, :]
                dw_ref[j:j + 1, :] += _colsum(dcur * win[base + lead + j:base + lead + j + CONV_SUB, :])
            av = a_ref[base:base + CONV_SUB, :]
            sig = _sigmoid(b_ref[base:base + CONV_SUB, :])
            da_ref[base:base + CONV_SUB, :] = (du0 * sig).astype(BF16)
            db_ref[base:base + CONV_SUB, :] = (du0 * av * sig * (1.0 - sig)).astype(BF16)

    return pl.pallas_call(
        body, grid=(cw // LANES, n_chunks), in_specs=[a_cur, b_cur, a_halo, b_halo, w_spec, out, d_next],
        out_specs=[out, out, w_spec, vec],
        out_shape=[jax.ShapeDtypeStruct((s, cw), BF16), jax.ShapeDtypeStruct((s, cw), BF16),
                   jax.ShapeDtypeStruct((CONV_KERNEL, cw), F32), jax.ShapeDtypeStruct((1, cw), F32)],
        scratch_shapes=[pltpu.VMEM((CONV_CHUNK + CONV_HALO, LANES), F32)] * 2,
        compiler_params=_params(2), name=name)(proj, proj, proj, proj, w, du1, du1)


def _adamw_math(w, g, m, v):
    m = ADAM_B1 * m + (1.0 - ADAM_B1) * g
    v = ADAM_B2 * v + (1.0 - ADAM_B2) * (g * g)
    m_hat = m / (1.0 - ADAM_B1 ** ADAM_STEP)
    v_hat = v / (1.0 - ADAM_B2 ** ADAM_STEP)
    delta = -ADAM_LR * (m_hat / (jnp.sqrt(v_hat) + ADAM_EPS) + ADAM_WD * w)
    return delta, m, v


def _adamw_big(w, g, m, v, name):
    rows, cols = w.shape
    tile = _tile(rows, 256, 8)
    spec = pl.BlockSpec((tile, cols), lambda i: (i, 0))

    def body(w_ref, g_ref, m_ref, v_ref, d_out, m_out, v_out):
        d_out[...], m_out[...], v_out[...] = _adamw_math(w_ref[...], g_ref[...], m_ref[...], v_ref[...])

    return pl.pallas_call(body, grid=(rows // tile,), in_specs=[spec] * 4, out_specs=[spec] * 3,
                          out_shape=[jax.ShapeDtypeStruct(w.shape, F32)] * 3, compiler_params=_params(1),
                          name=name)(w, g, m, v)


def _adamw_reduced(w, land, m, v, name):
    rows, cols = w.shape
    tile = _tile(rows, 256, 16)
    spec = pl.BlockSpec((tile, cols), lambda i: (i, 0))

    def body(w_ref, l_ref, m_ref, v_ref, g_out, d_out, m_out, v_out):
        g = l_ref[0].astype(F32)
        for q in range(1, N_CHIP):
            g = g + l_ref[q].astype(F32)
        g_out[...] = g
        d_out[...], m_out[...], v_out[...] = _adamw_math(w_ref[...], g, m_ref[...], v_ref[...])

    return pl.pallas_call(body, grid=(rows // tile,),
                          in_specs=[spec, pl.BlockSpec((N_CHIP, tile, cols), lambda i: (0, i, 0)), spec, spec],
                          out_specs=[spec] * 4, out_shape=[jax.ShapeDtypeStruct(w.shape, F32)] * 4,
                          compiler_params=_params(1), name=name)(w, land, m, v)


def _adamw_small(ws, gs, ms, vs, name):
    n = len(ws)

    def body(*refs):
        ins, outs = refs[:4 * n], refs[4 * n:]
        for t in range(n):
            res = _adamw_math(ins[t][...], ins[n + t][...], ins[2 * n + t][...], ins[3 * n + t][...])
            for j in range(3):
                outs[j * n + t][...] = res[j]

    shapes = [jax.ShapeDtypeStruct(w.shape, F32) for w in ws]
    res = pl.pallas_call(body, out_shape=shapes * 3, compiler_params=pltpu.CompilerParams(vmem_limit_bytes=VMEM_LIMIT),
                         name=name)(*ws, *gs, *ms, *vs)
    return res[:n], res[n:2 * n], res[2 * n:]


def _sum_blocks(x, n_blocks, name):
    r = x.shape[0] // n_blocks

    def body(x_ref, o_ref):
        acc = x_ref[0:r, :]
        for b in range(1, n_blocks):
            acc = acc + x_ref[b * r:(b + 1) * r, :]
        o_ref[...] = acc

    return pl.pallas_call(body, out_shape=jax.ShapeDtypeStruct((r, x.shape[1]), F32),
                          compiler_params=pltpu.CompilerParams(vmem_limit_bytes=VMEM_LIMIT), name=name)(x)


def _coords():
    return lax.axis_index("x"), lax.axis_index("y"), lax.axis_index("c")


def _flip(v, bit):
    return 1 - v if bit else v


def _ag_small(x, name):
    r, c = x.shape

    def body(x_ref, o_ref, send, recv, local_sem):
        mx, my, mc = _coords()

        def rows(px, py, pc):
            return o_ref.at[pl.ds(pl.multiple_of((4 * px + 2 * py + pc) * r, 8), r), :]

        local = pltpu.make_async_copy(x_ref, rows(mx, my, mc), local_sem)
        local.start()
        peers = [(_flip(mx, k >> 2 & 1), _flip(my, k >> 1 & 1), _flip(mc, k & 1)) for k in range(1, N_DEV)]
        sends = [pltpu.make_async_remote_copy(x_ref, rows(mx, my, mc), send.at[k], recv.at[k], device_id=p,
                                              device_id_type=MESH) for k, p in enumerate(peers)]
        for cp in sends:
            cp.start()
        for k, p in enumerate(peers):
            pltpu.make_async_remote_copy(x_ref, rows(*p), send.at[k], recv.at[k], device_id=p,
                                         device_id_type=MESH).wait_recv()
        for cp in sends:
            cp.wait_send()
        local.wait()

    vm = pl.BlockSpec(memory_space=pltpu.VMEM)
    return pl.pallas_call(
        body, in_specs=[vm], out_specs=vm, out_shape=jax.ShapeDtypeStruct((N_DEV * r, c), x.dtype),
        scratch_shapes=[pltpu.SemaphoreType.DMA((N_DEV - 1,)), pltpu.SemaphoreType.DMA((N_DEV - 1,)),
                        pltpu.SemaphoreType.DMA(())],
        name=name)(x)


class _GatherSmall:
    mid = None

    def __init__(self, x):
        self.inputs = [x]
        self.out_shapes = [jax.ShapeDtypeStruct((N_DEV * x.shape[0], x.shape[1]), x.dtype)]
        self.scratch = [pltpu.SemaphoreType.DMA((N_DEV - 1,)), pltpu.SemaphoreType.DMA((N_DEV - 1,)),
                        pltpu.SemaphoreType.DMA(())]

    def _plan(self, x_refs, o_refs, sems):
        send, recv, local_sem = sems
        x_ref, o_ref = x_refs[0], o_refs[0]
        r = x_ref.shape[0]
        mx, my, mc = _coords()

        def rows(px, py, pc):
            return o_ref.at[pl.ds(pl.multiple_of((4 * px + 2 * py + pc) * r, 8), r), :]

        peers = [(_flip(mx, k >> 2 & 1), _flip(my, k >> 1 & 1), _flip(mc, k & 1)) for k in range(1, N_DEV)]
        out = [pltpu.make_async_remote_copy(x_ref, rows(mx, my, mc), send.at[k], recv.at[k], device_id=p,
                                            device_id_type=MESH) for k, p in enumerate(peers)]
        arrivals = [pltpu.make_async_remote_copy(x_ref, rows(*p), send.at[k], recv.at[k], device_id=p,
                                                 device_id_type=MESH) for k, p in enumerate(peers)]
        return out, arrivals, pltpu.make_async_copy(x_ref, rows(mx, my, mc), local_sem)

    def start(self, x_refs, o_refs, sems):
        out, _, local = self._plan(x_refs, o_refs, sems)
        local.start()
        for cp in out:
            cp.start()

    def finish(self, x_refs, o_refs, sems):
        out, arrivals, local = self._plan(x_refs, o_refs, sems)
        for cp in arrivals:
            cp.wait_recv()
        for cp in out:
            cp.wait_send()
        local.wait()


class _ModExchange:
    def __init__(self, first, w_ada):
        self.d, cols = w_ada.shape
        part = jax.ShapeDtypeStruct((N_DEV, cols), F32)
        self.g1, self.g2 = _GatherSmall(first), _GatherSmall(part)
        self.inputs = [first, w_ada]
        self.out_shapes = [self.g1.out_shapes[0], jax.ShapeDtypeStruct((N_DEV, self.d), F32), part,
                           self.g2.out_shapes[0]]
        self.scratch = self.g1.scratch + self.g2.scratch + [
            pltpu.VMEM(self.g1.out_shapes[0].shape, F32), pltpu.VMEM(w_ada.shape, F32),
            pltpu.VMEM((N_DEV, self.d), F32), pltpu.VMEM((N_DEV, cols), F32), pltpu.SemaphoreType.DMA(())]

    def start(self, cin, cout, scr):
        self.g1.start(cin[0:1], cout[0:1], scr[0:3])
        pltpu.make_async_copy(cin[1], scr[7], scr[10]).start()

    def mid(self, cin, cout, scr):
        gathered, w_v, silu_v, part_v = scr[6:10]
        self.g1.finish(cin[0:1], cout[0:1], scr[0:3])
        pltpu.sync_copy(cout[0], gathered)
        rows_per = cin[0].shape[0]
        for j in range(N_DEV):
            silu_v[j:j + 1, :] = gathered[j * rows_per:j * rows_per + 1, 0:self.d]
        c_all = silu_v[...]
        silu_v[...] = c_all * _sigmoid(c_all)
        pltpu.sync_copy(silu_v, cout[1])
        pltpu.make_async_copy(cin[1], w_v, scr[10]).wait()
        part_v[...] = _dot_nn(silu_v[...], w_v[...])
        pltpu.sync_copy(part_v, cout[2])
        self.g2.start(cout[2:3], cout[3:4], scr[3:6])

    def finish(self, cin, cout, scr):
        self.g2.finish(cout[2:3], cout[3:4], scr[3:6])


class _GatherWeights:
    def __init__(self, shards):
        n_t = len(shards)
        self.inputs = list(shards)
        self.out_shapes = [jax.ShapeDtypeStruct((N_DEV * x.shape[0], x.shape[1]), x.dtype) for x in shards]
        self.scratch = [pltpu.SemaphoreType.DMA((n_t, 8)), pltpu.SemaphoreType.DMA((n_t, 8)),
                        pltpu.SemaphoreType.DMA((n_t,))]

    def _plan(self, x_refs, o_refs, sems):
        send, recv, local_sem = sems
        mx, my, mc = _coords()
        me, sibling = (mx, my, mc), (mx, my, 1 - mc)
        xn, yn, diag = (1 - mx, my), (mx, 1 - my), (1 - mx, 1 - my)

        def rows(t, chip, core, half=None):
            r = x_refs[t].shape[0]
            base = (4 * chip[0] + 2 * chip[1] + core) * r
            if half is None:
                return o_refs[t].at[pl.ds(pl.multiple_of(base, 8), r), :]
            return o_refs[t].at[pl.ds(pl.multiple_of(base + half * (r // 2), 8), r // 2), :]

        def copy(t, k, block, to, src=None):
            return pltpu.make_async_remote_copy(
                src_ref=block if src is None else src, dst_ref=block,
                send_sem=send.at[t, k], recv_sem=recv.at[t, k], device_id=to, device_id_type=MESH)

        def local(t):
            return pltpu.make_async_copy(x_refs[t], rows(t, (mx, my), mc), local_sem.at[t])

        return (mx, my), mc, me, sibling, xn, yn, diag, rows, copy, local

    def start(self, x_refs, o_refs, sems):
        chip, mc, me, sibling, xn, yn, diag, rows, copy, local = self._plan(x_refs, o_refs, sems)
        for t in range(len(x_refs)):
            mine = rows(t, chip, mc)
            local(t).start()
            copy(t, 0, mine, sibling, src=x_refs[t]).start()
            copy(t, 1, mine, (*xn, mc), src=x_refs[t]).start()
            copy(t, 2, mine, (*yn, mc), src=x_refs[t]).start()

    def mid(self, x_refs, o_refs, sems):
        chip, mc, me, sibling, xn, yn, diag, rows, copy, local = self._plan(x_refs, o_refs, sems)
        for t in range(len(x_refs)):
            copy(t, 1, rows(t, xn, mc), me).wait_recv()
            copy(t, 3, rows(t, xn, mc, 0), (*yn, mc)).start()
            copy(t, 5, rows(t, xn, mc), sibling).start()
        for t in range(len(x_refs)):
            copy(t, 2, rows(t, yn, mc), me).wait_recv()
            copy(t, 4, rows(t, yn, mc, 1), (*xn, mc)).start()
            copy(t, 6, rows(t, yn, mc), sibling).start()

    def finish(self, x_refs, o_refs, sems):
        chip, mc, me, sibling, xn, yn, diag, rows, copy, local = self._plan(x_refs, o_refs, sems)
        for t in range(len(x_refs)):
            copy(t, 3, rows(t, diag, mc, 0), me).wait_recv()
            copy(t, 4, rows(t, diag, mc, 1), me).wait_recv()
            copy(t, 7, rows(t, diag, mc), sibling).start()
        for t in range(len(x_refs)):
            copy(t, 0, rows(t, chip, 1 - mc), me).wait_recv()
            copy(t, 5, rows(t, xn, 1 - mc), me).wait_recv()
            copy(t, 6, rows(t, yn, 1 - mc), me).wait_recv()
            copy(t, 7, rows(t, diag, 1 - mc), me).wait_recv()
            mine = rows(t, chip, mc)
            copy(t, 0, mine, sibling, src=x_refs[t]).wait_send()
            copy(t, 1, mine, (*xn, mc), src=x_refs[t]).wait_send()
            copy(t, 2, mine, (*yn, mc), src=x_refs[t]).wait_send()
            copy(t, 3, rows(t, xn, mc, 0), (*yn, mc)).wait_send()
            copy(t, 4, rows(t, yn, mc, 1), (*xn, mc)).wait_send()
            copy(t, 5, rows(t, xn, mc), sibling).wait_send()
            copy(t, 6, rows(t, yn, mc), sibling).wait_send()
            copy(t, 7, rows(t, diag, mc), sibling).wait_send()
            local(t).wait()


class _SiblingExchange:
    mid = None

    def __init__(self, grads):
        n_t = len(grads)
        self.inputs = list(grads)
        self.out_shapes = [jax.ShapeDtypeStruct((N_CHIP,) + g.shape[2:], F32) for g in grads]
        self.scratch = [pltpu.SemaphoreType.DMA((n_t,)), pltpu.SemaphoreType.DMA((n_t,))]

    def _copies(self, g_refs, land, sems):
        send, recv = sems
        mx, my, mc = _coords()
        return [pltpu.make_async_remote_copy(g_refs[t].at[:, 1 - mc], land[t], send.at[t], recv.at[t],
                                             device_id=(mx, my, 1 - mc), device_id_type=MESH)
                for t in range(len(g_refs))]

    def start(self, g_refs, land, sems):
        for cp in self._copies(g_refs, land, sems):
            cp.start()

    def finish(self, g_refs, land, sems):
        for cp in self._copies(g_refs, land, sems):
            cp.wait()


class _Together:
    def __init__(self, *comms):
        self.comms = comms
        self.inputs = [x for c in comms for x in c.inputs]
        self.out_shapes = [x for c in comms for x in c.out_shapes]
        self.scratch = [x for c in comms for x in c.scratch]
        self.mid = self._mid if any(c.mid is not None for c in comms) else None

    def _each(self, phase, cin, cout, sems):
        i = o = s = 0
        for c in self.comms:
            fn = getattr(c, phase)
            ni, no, ns = len(c.inputs), len(c.out_shapes), len(c.scratch)
            if fn is not None:
                fn(cin[i:i + ni], cout[o:o + no], sems[s:s + ns])
            i, o, s = i + ni, o + no, s + ns

    def start(self, cin, cout, sems):
        self._each("start", cin, cout, sems)

    def _mid(self, cin, cout, sems):
        self._each("mid", cin, cout, sems)

    def finish(self, cin, cout, sems):
        self._each("finish", cin, cout, sems)


def _standalone(comm, name):
    def body():
        pass
    return _call(body, grid=(1,), in_specs=[], out_specs=[], out_shape=[], args=(), name=name, comm=comm)[1]


def _chip_partials(g4s, lands, name):
    n_t = len(g4s)
    in_specs, out_specs, out_shape = [], [], []
    for g4 in g4s:
        _, _, r, c = g4.shape
        in_specs.append(pl.BlockSpec((None, None, r, c), lambda q: (q, lax.axis_index("c"), 0, 0)))
        out_specs.append(pl.BlockSpec((None, r, c), lambda q: (q, 0, 0)))
        out_shape.append(jax.ShapeDtypeStruct((N_CHIP, r, c), BF16))
    in_specs += [pl.BlockSpec((None,) + g4.shape[2:], lambda q: (q, 0, 0)) for g4 in g4s]

    def body(*refs):
        for t in range(n_t):
            refs[2 * n_t + t][...] = (refs[t][...] + refs[n_t + t][...]).astype(BF16)

    return pl.pallas_call(body, grid=(N_CHIP,), in_specs=in_specs, out_specs=out_specs, out_shape=out_shape,
                          compiler_params=_params(1), name=name)(*g4s, *lands)


class _ChipExchange:
    mid = None

    def __init__(self, parts):
        n_t = len(parts)
        self.inputs = list(parts)
        self.out_shapes = [jax.ShapeDtypeStruct(p.shape, p.dtype) for p in parts]
        self.scratch = [pltpu.SemaphoreType.DMA((n_t, 3)), pltpu.SemaphoreType.DMA((n_t, 3)),
                        pltpu.SemaphoreType.DMA((n_t,))]

    def _plan(self, p_refs, land, sems):
        send, recv, local_sem = sems
        mx, my, mc = _coords()
        my_chip = 2 * mx + my
        peers = [(_flip(mx, fx), _flip(my, fy)) for fx, fy in ((1, 0), (0, 1), (1, 1))]

        def out(t, k):
            px, py = peers[k]
            return pltpu.make_async_remote_copy(p_refs[t].at[2 * px + py], land[t].at[my_chip], send.at[t, k],
                                                recv.at[t, k], device_id=(px, py, mc), device_id_type=MESH)

        def arrival(t, k):
            px, py = peers[k]
            return pltpu.make_async_remote_copy(p_refs[t].at[my_chip], land[t].at[2 * px + py], send.at[t, k],
                                                recv.at[t, k], device_id=(px, py, mc), device_id_type=MESH)

        def local(t):
            return pltpu.make_async_copy(p_refs[t].at[my_chip], land[t].at[my_chip], local_sem.at[t])

        return out, arrival, local

    def start(self, p_refs, land, sems):
        out, arrival, local = self._plan(p_refs, land, sems)
        for t in range(len(p_refs)):
            local(t).start()
            for k in range(3):
                out(t, k).start()

    def finish(self, p_refs, land, sems):
        out, arrival, local = self._plan(p_refs, land, sems)
        for t in range(len(p_refs)):
            for k in range(3):
                arrival(t, k).wait_recv()
                out(t, k).wait_send()
            local(t).wait()


def _rope_tables(s, width):
    heads = width // HEAD_DIM
    inv_freq = ROPE_THETA ** (-jnp.arange(0, HEAD_DIM, 2, dtype=F32) / HEAD_DIM)
    inv_full = jnp.tile(inv_freq, 2 * heads)
    sign = jnp.tile(jnp.concatenate([-jnp.ones((HALF_HEAD,), F32), jnp.ones((HALF_HEAD,), F32)]), heads)
    ang = jnp.arange(s, dtype=F32)[:, None] * inv_full[None, :]
    return jnp.cos(ang), jnp.sin(ang) * sign[None, :]


def _pad_rows(v, rows):
    return jnp.concatenate([v, jnp.zeros((rows - 1, v.shape[1]), v.dtype)], axis=0)


def kernel(x, c, w_ada, b_ada, ffn1_norm_g, ffn1_w_gate, ffn1_w_up, ffn1_w_down, mix_norm_g, w_in, conv_dw_w, conv_dw_b, conv_ln_g, conv_ln_b, attn_out_g, conv_out_g, w_out, ffn2_norm_g, ffn2_w_gate, ffn2_w_up, ffn2_w_down, final_norm_g, loss_target, m_w_ada, m_b_ada, m_ffn1_norm_g, m_ffn1_w_gate, m_ffn1_w_up, m_ffn1_w_down, m_mix_norm_g, m_w_in, m_conv_dw_w, m_conv_dw_b, m_conv_ln_g, m_conv_ln_b, m_attn_out_g, m_conv_out_g, m_w_out, m_ffn2_norm_g, m_ffn2_w_gate, m_ffn2_w_up, m_ffn2_w_down, m_final_norm_g, v_w_ada, v_b_ada, v_ffn1_norm_g, v_ffn1_w_gate, v_ffn1_w_up, v_ffn1_w_down, v_mix_norm_g, v_w_in, v_conv_dw_w, v_conv_dw_b, v_conv_ln_g, v_conv_ln_b, v_attn_out_g, v_conv_out_g, v_w_out, v_ffn2_norm_g, v_ffn2_w_gate, v_ffn2_w_up, v_ffn2_w_down, v_final_norm_g):
    mx, my, mc = _coords()
    me = 4 * mx + 2 * my + mc
    s, d = x.shape[1], x.shape[2]
    aw = d // 2
    x2, target = x[0], loss_target[0]
    n_mod = w_ada.shape[2] * N_DEV // d
    mod_cols = w_ada.shape[2]

    def shard(w, transpose):
        return (w[0].T if transpose else w[0]).astype(BF16)

    cw_shard = conv_dw_w.shape[3]
    n_taps = CONV_KERNEL * cw_shard
    first_len = -(-(d + n_taps) // LANES) * LANES
    first = jnp.concatenate([c, conv_dw_w[0, :, 0, :].reshape(1, n_taps), jnp.zeros((1, first_len - d - n_taps), F32)], axis=1)
    first_all, silu_c, _, mod_all, wg1, wu1 = _standalone(
        _Together(_ModExchange(_pad_rows(first, 8), w_ada[0]),
                  _GatherWeights([shard(ffn1_w_gate, True), shard(ffn1_w_up, True)])), "ag_first")
    first_all = first_all[0::8]
    conv_w = first_all[:, d:d + n_taps].reshape(N_DEV, CONV_KERNEL, cw_shard).transpose(1, 0, 2).reshape(CONV_KERNEL, aw)

    mod_all = mod_all.reshape(N_DEV, N_DEV, mod_cols)
    mod = lax.dynamic_index_in_dim(mod_all, me, axis=1, keepdims=False).reshape(1, n_mod * d) + b_ada
    sh1, sc1, g1, sh2, sc2, g2, sh3, sc3, g3 = [mod[:, i * d:(i + 1) * d] for i in range(n_mod)]

    def split(g):
        return g.reshape(N_CHIP, 2, g.shape[0] // N_DEV, g.shape[1])

    def partials(g4s, lands, tag):
        return _chip_partials(g4s, lands, "chip_partials_" + tag)

    (n1, silu1, gs1, hid1), (wd1, win_t, wout) = _norm_ffn_up(
        x2, ffn1_norm_g, sc1, sh1, wg1, wu1, "ffn1_up",
        comm=_GatherWeights([shard(ffn1_w_down, False), shard(w_in, True), shard(w_out, False)]))
    h1, f1, n2 = _residual_mm(hid1, wd1, x2, g1, 0.5, "ffn1_down", norm=(mix_norm_g, sc2, sh2))
    cos, sin_signed = _rope_tables(s, LANES)
    proj, = _proj_rope(n2, win_t, cos, sin_signed, aw, "proj")
    lanes_per = aw // LANES
    (attn, lse), (wg2, wu2, wd2) = _attn_seq_fwd(
        proj, aw, "attn_fwd",
        comm=_GatherWeights([shard(ffn2_w_gate, True), shard(ffn2_w_up, True), shard(ffn2_w_down, False)]))
    u1, = _conv_fwd(proj, 3 * lanes_per, 4 * lanes_per, conv_w, conv_dw_b, "conv_fwd")
    post = (attn_out_g, conv_ln_g, conv_ln_b, conv_out_g)
    y, h2, mix, n3 = _mix_out(attn, u1, post, wout, h1, g2, (ffn2_norm_g, sc3, sh3), "mix_out")
    silu3, gs3, hid3 = _ffn_up(n3, wg2, wu2, "ffn2_up")

    dh3, df3, err2, d_final_g, dg3 = _last_mm_loss(hid3, wd2, h2, g3, 0.5, target, final_norm_g.reshape(1, d),
                                                   "ffn2_down_loss")
    loss_part = jnp.zeros((1, LANES), F32).at[0, 0].set(0.5 * jnp.sum(err2) / d)

    da3, db3 = _ffn_bwd_hidden(df3, wd2, silu3, gs3, "ffn2_hidden_bwd")
    g4_a = [split(_mm_tn(da3, n3, "ffn2_dwg")), split(_mm_tn(db3, n3, "ffn2_dwu")), split(_mm_tn(hid3, df3, "ffn2_dwd"))]
    (dh2, dmix, dsh3, dsc3, dgn3, dg2), land_a = _mm_norm_mod_bwd(
        [(da3, wg2), (db3, wu2)], h2, dh3, ffn2_norm_g, sc3, (mix, g2, 1.0), "ffn2_dn_norm3_bwd", tm=256,
        comm=_SiblingExchange(g4_a))
    parts_a = partials(g4_a, land_a, "a")
    g_wout = _mm_tn(y, dmix, "mix_dwout")
    dattn, du1, d_gains, d_ln = _mix_dy_post_bwd(dmix, wout, attn, u1, post, "mix_dy_post_bwd")
    d_attn_g, d_conv_g, d_ln_g, d_ln_b = d_gains[:, :aw], d_gains[:, aw:], d_ln[:, :aw], d_ln[:, aw:]
    dga, dgb, d_taps, d_conv_b = _conv_bwd(proj, 3 * lanes_per, 4 * lanes_per, conv_w, du1, "conv_bwd")
    (dq, dk, dv), sums_a = _attn_seq_bwd(proj, dattn, attn, lse, cos, sin_signed, "attn_bwd",
                                         comm=_ChipExchange(parts_a))
    dproj = jnp.concatenate([dq, dk, dv, dga, dgb], axis=1)
    g4_b = [split(g_wout), split(_mm_tn(dproj, n2, "mix_dwin"))]
    (dh1, df1, dsh2, dsc2, dgn2, dg1), land_b = _mm_norm_mod_bwd(
        [(dproj, win_t)], h1, dh2, mix_norm_g, sc2, (f1, g1, 0.5), "mix_dn_norm2_bwd", tm=512,
        comm=_SiblingExchange(g4_b))
    parts_b = partials(g4_b, land_b, "b")
    g4_c = [split(_mm_tn(hid1, df1, "ffn1_dwd"))]
    (da1, db1), both = _ffn_bwd_hidden(df1, wd1, silu1, gs1, "ffn1_hidden_bwd",
                                       comm=_Together(_ChipExchange(parts_b), _SiblingExchange(g4_c)))
    sums_b, land_c = both[:2], both[2:]
    parts_c = partials(g4_c, land_c, "c")
    g4_d = [split(_mm_tn(db1, n1, "ffn1_dwu"))]
    g_wg1, both = _mm_tn(da1, n1, "ffn1_dwg", comm=_Together(_ChipExchange(parts_c), _SiblingExchange(g4_d)))
    sums_c, land_d = both[:1], both[1:]
    parts_d = partials(g4_d, land_d, "d")
    g4_e = [split(g_wg1)]
    dn1, both = _plain_mm([(da1, wg1), (db1, wu1)], BF16, False, d, "ffn1_dn",
                          comm=_Together(_ChipExchange(parts_d), _SiblingExchange(g4_e)))
    sums_d, land_e = both[:1], both[1:]
    parts_e = partials(g4_e, land_e, "e")
    (dx, dsh1, dsc1, dgn1), sums_e = _norm_mod_bwd(dn1, x2, dh1, ffn1_norm_g, sc1, "norm1_bwd",
                                                   comm=_ChipExchange(parts_e))

    dmod = jnp.concatenate([dsh1, dsc1, dg1, dsh2, dsc2, dg2, dsh3, dsc3, dg3], axis=1)
    small = [dmod, dgn1, dgn2, dgn3, d_final_g, d_conv_b, d_ln_g, d_ln_b, d_attn_g, d_conv_g,
             d_taps.reshape(1, CONV_KERNEL * aw), loss_part]
    sizes = [v.shape[1] for v in small]
    total = sum(sizes)
    padded = -(-total // (8 * LANES)) * (8 * LANES)
    packed = jnp.concatenate(small + [jnp.zeros((1, padded - total), F32)], axis=1).reshape(8, padded // 8)
    gathered = _ag_small(packed, "ag_small_grads")
    summed = _sum_blocks(gathered, N_DEV, "sum_small_grads").reshape(1, padded)
    offs = [sum(sizes[:i]) for i in range(len(sizes))]
    (g_b_ada, g_gn1, g_gn2, g_gn3, g_final, g_conv_b, g_ln_g, g_ln_b, g_attn_g, g_conv_g, g_taps, loss_row) = [
        summed[:, o:o + n] for o, n in zip(offs, sizes)]
    loss = loss_row[0, 0]
    g_taps_shard = lax.dynamic_slice_in_dim(g_taps.reshape(CONV_KERNEL, aw), me * cw_shard, cw_shard, axis=1)
    dmod_all = gathered.reshape(N_DEV, padded)[:, :n_mod * d]
    dmod_cols = lax.dynamic_slice_in_dim(dmod_all, me * mod_cols, mod_cols, axis=1)
    g_w_ada = _mm_tn(silu_c, dmod_cols, "ada_dw")

    arrived = dict(zip(["ffn2_w_gate", "ffn2_w_up", "ffn2_w_down", "w_out", "w_in", "ffn1_w_down", "ffn1_w_up",
                        "ffn1_w_gate"], list(sums_a) + list(sums_b) + list(sums_c) + list(sums_d) + list(sums_e)))
    transposed = ("ffn1_w_gate", "ffn1_w_up", "w_in", "ffn2_w_gate", "ffn2_w_up")
    grads = {
        "w_ada": g_w_ada, "b_ada": g_b_ada, "ffn1_norm_g": g_gn1, "mix_norm_g": g_gn2, "conv_dw_w": g_taps_shard,
        "conv_dw_b": g_conv_b, "conv_ln_g": g_ln_g, "conv_ln_b": g_ln_b, "attn_out_g": g_attn_g,
        "conv_out_g": g_conv_g, "ffn2_norm_g": g_gn3, "final_norm_g": g_final,
    }
    weights = dict(w_ada=w_ada, b_ada=b_ada, ffn1_norm_g=ffn1_norm_g, ffn1_w_gate=ffn1_w_gate, ffn1_w_up=ffn1_w_up, ffn1_w_down=ffn1_w_down, mix_norm_g=mix_norm_g, w_in=w_in, conv_dw_w=conv_dw_w, conv_dw_b=conv_dw_b, conv_ln_g=conv_ln_g, conv_ln_b=conv_ln_b, attn_out_g=attn_out_g, conv_out_g=conv_out_g, w_out=w_out, ffn2_norm_g=ffn2_norm_g, ffn2_w_gate=ffn2_w_gate, ffn2_w_up=ffn2_w_up, ffn2_w_down=ffn2_w_down, final_norm_g=final_norm_g)
    moms = dict(w_ada=m_w_ada, b_ada=m_b_ada, ffn1_norm_g=m_ffn1_norm_g, ffn1_w_gate=m_ffn1_w_gate, ffn1_w_up=m_ffn1_w_up, ffn1_w_down=m_ffn1_w_down, mix_norm_g=m_mix_norm_g, w_in=m_w_in, conv_dw_w=m_conv_dw_w, conv_dw_b=m_conv_dw_b, conv_ln_g=m_conv_ln_g, conv_ln_b=m_conv_ln_b, attn_out_g=m_attn_out_g, conv_out_g=m_conv_out_g, w_out=m_w_out, ffn2_norm_g=m_ffn2_norm_g, ffn2_w_gate=m_ffn2_w_gate, ffn2_w_up=m_ffn2_w_up, ffn2_w_down=m_ffn2_w_down, final_norm_g=m_final_norm_g)
    vars_ = dict(w_ada=v_w_ada, b_ada=v_b_ada, ffn1_norm_g=v_ffn1_norm_g, ffn1_w_gate=v_ffn1_w_gate, ffn1_w_up=v_ffn1_w_up, ffn1_w_down=v_ffn1_w_down, mix_norm_g=v_mix_norm_g, w_in=v_w_in, conv_dw_w=v_conv_dw_w, conv_dw_b=v_conv_dw_b, conv_ln_g=v_conv_ln_g, conv_ln_b=v_conv_ln_b, attn_out_g=v_attn_out_g, conv_out_g=v_conv_out_g, w_out=v_w_out, ffn2_norm_g=v_ffn2_norm_g, ffn2_w_gate=v_ffn2_w_gate, ffn2_w_up=v_ffn2_w_up, ffn2_w_down=v_ffn2_w_down, final_norm_g=v_final_norm_g)
    names = list(weights)
    big = ["w_ada", "ffn1_w_gate", "ffn1_w_up", "ffn1_w_down", "w_in", "w_out", "ffn2_w_gate", "ffn2_w_up",
           "ffn2_w_down"]
    shape2 = {n: (weights[n].shape[-2] if weights[n].ndim > 1 else 1, weights[n].shape[-1]) for n in names}
    shape2["conv_dw_w"] = (CONV_KERNEL, cw_shard)
    g_out, d_out, m_out, v_out = {}, {}, {}, {}
    for n in big:
        if n in arrived:
            def view(t, n=n):
                return t[0].T if n in transposed else t[0]
            res = _adamw_reduced(view(weights[n]), arrived[n], view(moms[n]), view(vars_[n]), "adamw_" + n)
            g_out[n], d_out[n], m_out[n], v_out[n] = [r.T if n in transposed else r for r in res]
        else:
            g2d = grads[n].reshape(shape2[n])
            res = _adamw_big(weights[n].reshape(shape2[n]), g2d, moms[n].reshape(shape2[n]),
                             vars_[n].reshape(shape2[n]), "adamw_" + n)
            g_out[n], (d_out[n], m_out[n], v_out[n]) = g2d, res
    rest = [n for n in names if n not in big]
    res = _adamw_small([weights[n].reshape(shape2[n]) for n in rest], [grads[n].reshape(shape2[n]) for n in rest],
                       [moms[n].reshape(shape2[n]) for n in rest], [vars_[n].reshape(shape2[n]) for n in rest],
                       "adamw_small")
    for i, n in enumerate(rest):
        g_out[n], d_out[n], m_out[n], v_out[n] = grads[n], res[0][i], res[1][i], res[2][i]

    def shaped(table):
        return [table[n].reshape(weights[n].shape) for n in names]

    return (loss, dx.reshape(x.shape), *shaped(g_out), *shaped(d_out), *shaped(m_out), *shaped(v_out))
```

```python
import functools

import jax
import jax.numpy as jnp
from jax import lax
from jax.experimental import pallas as pl
from jax.experimental.pallas import tpu as pltpu

F32 = jnp.float32
BF16 = jnp.bfloat16
MESH = pl.DeviceIdType.MESH
ANY = pl.BlockSpec(memory_space=pl.ANY)

N_DEV = 8
N_CHIP = 4
HEAD_DIM = 64
HALF_HEAD = HEAD_DIM // 2
LANES = 128
BLOCK = 128
DILATIONS = (1, 4, 16)
MERGE_CHUNK = 512
ROPE_THETA = 10000.0
CONV_KERNEL = 31
CONV_HALO = 32
CONV_CHUNK = 512
CONV_SUB = 128
RMS_EPS = 1e-6
LN_EPS = 1e-5
ADAM_LR = 0.001
ADAM_B1 = 0.9
ADAM_B2 = 0.999
ADAM_EPS = 1e-08
ADAM_WD = 0.01
ADAM_STEP = 10
VMEM_LIMIT = 56 * 1024 * 1024
NEG = -1e30


def _params(n_axes):
    return pltpu.CompilerParams(dimension_semantics=("arbitrary",) * n_axes, vmem_limit_bytes=VMEM_LIMIT)


def _tile(n, target, unit):
    best = None
    for t in range(unit, min(n, target) + 1, unit):
        if n % t == 0:
            best = t
    return best if best is not None else n


def _sigmoid(x):
    return 0.5 * (jnp.tanh(0.5 * x) + 1.0)


def _call(body, *, grid, in_specs, out_specs, out_shape, args, name, scratch_shapes=(), comm=None):
    params = _params(len(grid))
    if comm is None:
        return pl.pallas_call(body, grid=grid, in_specs=list(in_specs), out_specs=list(out_specs),
                              out_shape=list(out_shape), scratch_shapes=list(scratch_shapes),
                              compiler_params=params, name=name)(*args)
    n_in, n_out, n_scr = len(args), len(out_shape), len(scratch_shapes)
    c_in, c_out = len(comm.inputs), len(comm.out_shapes)
    steps = 1
    for g in grid:
        steps *= g

    def hosted(*refs):
        pos = 0
        parts = []
        for size in (n_in, c_in, n_out, c_out, n_scr, len(comm.scratch)):
            parts.append(refs[pos:pos + size])
            pos += size
        ins, cin, outs, cout, scr, cscr = parts
        step = 0
        for axis, g in enumerate(grid):
            step = step * g + pl.program_id(axis)

        @pl.when(step == 0)
        def _():
            comm.start(cin, cout, cscr)

        body(*ins, *outs, *scr)
        if comm.mid is not None and steps >= 4:
            @pl.when(step == steps // 2)
            def _():
                comm.mid(cin, cout, cscr)

        @pl.when(step == steps - 1)
        def _():
            if comm.mid is not None and steps < 4:
                comm.mid(cin, cout, cscr)
            comm.finish(cin, cout, cscr)

    res = pl.pallas_call(
        hosted, grid=grid, in_specs=list(in_specs) + [ANY] * c_in, out_specs=list(out_specs) + [ANY] * c_out,
        out_shape=list(out_shape) + list(comm.out_shapes), scratch_shapes=list(scratch_shapes) + list(comm.scratch),
        compiler_params=params, name=name)(*args, *comm.inputs)
    return res[:n_out], res[n_out:]


def _rows(fn, rows_in, vecs_in, rows_out, vecs_out, *, tile, name, comm=None):
    norm = [r if isinstance(r, tuple) else (r, r.shape[1], 0) for r in rows_in]
    n_rows = norm[0][0].shape[0]
    n_tiles = n_rows // tile
    in_specs, args = [], []
    for arr, width, cb in norm:
        in_specs.append(pl.BlockSpec((tile, width), functools.partial(lambda i, cb: (i, cb), cb=cb)))
        args.append(arr)
    for v in vecs_in:
        in_specs.append(pl.BlockSpec((1, v.shape[1]), lambda i: (0, 0)))
        args.append(v)
    out_shape = [jax.ShapeDtypeStruct((n_rows, w), dt) for w, dt in rows_out]
    out_shape += [jax.ShapeDtypeStruct((1, w), F32) for w in vecs_out]
    out_specs = [pl.BlockSpec((tile, w), lambda i: (i, 0)) for w, _ in rows_out]
    out_specs += [pl.BlockSpec((1, w), lambda i: (0, 0)) for w in vecs_out]
    n_in, n_ro = len(args), len(rows_out)

    def body(*refs):
        vals = [r[...] for r in refs[:n_in]]
        outs = refs[n_in:]
        row_vals, vec_vals = fn(*vals)
        for ref, val in zip(outs[:n_ro], row_vals):
            if isinstance(val, tuple):
                w = val[0].shape[1]
                for j, piece in enumerate(val):
                    ref[:, j * w:(j + 1) * w] = piece.astype(ref.dtype)
            else:
                ref[...] = val.astype(ref.dtype)
        if vecs_out:
            @pl.when(pl.program_id(0) == 0)
            def _():
                for ref in outs[n_ro:]:
                    ref[...] = jnp.zeros_like(ref)
            for ref, val in zip(outs[n_ro:], vec_vals):
                ref[...] += val

    return _call(body, grid=(n_tiles,), in_specs=in_specs, out_specs=out_specs, out_shape=out_shape, args=args,
                 name=name, comm=comm)


def _colsum(x):
    return jnp.sum(x, axis=0, keepdims=True)


def _rms_stats(h):
    r = lax.rsqrt(jnp.mean(h * h, axis=-1, keepdims=True) + RMS_EPS)
    return r, h * r


def _rms_back(r, xn, dxn):
    return r * (dxn - xn * jnp.mean(dxn * xn, axis=-1, keepdims=True))


def _branch_back(dh, f, gate, coef):
    return (coef * gate) * dh, coef * _colsum(f.astype(F32) * dh)


def _norm_mod_back(dn, h, dh_in, gain, scale):
    dn = dn.astype(F32)
    r, xn = _rms_stats(h)
    y = xn * gain
    dy = dn * (1.0 + scale)
    dh = dh_in + _rms_back(r, xn, dy * gain)
    return dh, [_colsum(dn), _colsum(dn * y), _colsum(dy * xn)]


def _norm_mod_bwd(dn, h, dh_in, gain, scale, name, comm=None):
    d = h.shape[1]

    def fn(dn, h, dh_in, gain, scale):
        dh, vecs = _norm_mod_back(dn, h, dh_in, gain, scale)
        return [dh], vecs
    return _rows(fn, [dn, h, dh_in], [gain, scale], [(d, F32)], [d, d, d], tile=256, name=name, comm=comm)


def _mm_norm_mod_bwd(pairs, h, dh_in, gain, scale, branch, name, tm, comm=None):
    f, gate, coef = branch

    def epi(accs, ex, vc):
        dh, vecs = _norm_mod_back(accs[0], ex[0], ex[1], vc[0], vc[1])
        df, dgate = _branch_back(dh, ex[2], vc[2], coef)
        return [dh, df] + vecs + [dgate]
    return _mm([pairs], epi, [h, dh_in, f], [gain, scale, gate], [F32, BF16], trans_rhs=False, tm=tm,
               tn=h.shape[1], name=name, n_sums=4, comm=comm)


def _last_mm_loss(lhs, w, res, gate, coef, target, gain, name):
    d = w.shape[1]

    def epi(accs, ex, vc):
        f = accs[0]
        h = ex[0] + (coef * vc[0]) * f
        r, xn = _rms_stats(h)
        err = xn * vc[1] - ex[1]
        dout = err * (1.0 / d)
        dh = _rms_back(r, xn, dout * vc[1])
        df, dgate = _branch_back(dh, f, vc[0], coef)
        return [dh, df, _colsum(err * err), _colsum(dout * xn), dgate]
    return _mm([[(lhs, w)]], epi, [res, target], [gate, gain], [F32, BF16], trans_rhs=False, tm=256, tn=d,
               name=name, n_sums=3)


def _partner(x):
    if x.shape[1] > LANES:
        return jnp.concatenate([_partner(x[:, c:c + LANES]) for c in range(0, x.shape[1], LANES)], axis=1)
    lane = lax.broadcasted_iota(jnp.int32, x.shape, 1) % HEAD_DIM
    return jnp.where(lane < HALF_HEAD, pltpu.roll(x, LANES - HALF_HEAD, 1), pltpu.roll(x, HALF_HEAD, 1))


def _proj_rope(n, w_t, cos, sin_signed, width, name, comm=None):
    s, kdim = n.shape
    n_cols = w_t.shape[0]
    tm = _tile(s, 1024, 8)
    qscale = HEAD_DIM ** -0.5

    chunk = _tile(tm, 256, 8)

    def body(n_ref, w_ref, cos_ref, sin_ref, o_ref):
        j = pl.program_id(0)

        def products(rows):
            return lax.dot_general(n_ref[rows, :].astype(BF16), w_ref[...].astype(BF16), (((1,), (1,)), ((), ())),
                                   preferred_element_type=F32)

        @pl.when(j >= 2)
        def _():
            for c in range(tm // chunk):
                rows = slice(c * chunk, (c + 1) * chunk)
                o_ref[rows, :] = products(rows)

        @pl.when(j < 2)
        def _():
            scale = jnp.where(j == 0, qscale, 1.0)
            for c in range(tm // chunk):
                rows = slice(c * chunk, (c + 1) * chunk)
                acc = products(rows)
                cos = jnp.tile(cos_ref[rows, :], (1, width // LANES))
                sin = jnp.tile(sin_ref[rows, :], (1, width // LANES))
                o_ref[rows, :] = scale * (acc * cos + _partner(acc) * sin)

    table = pl.BlockSpec((tm, LANES), lambda j, i: (jnp.where(j < 2, i, 0), 0))
    return _call(
        body, grid=(n_cols // width, s // tm),
        in_specs=[pl.BlockSpec((tm, kdim), lambda j, i: (i, 0)), pl.BlockSpec((width, kdim), lambda j, i: (j, 0)),
                  table, table],
        out_specs=[pl.BlockSpec((tm, width), lambda j, i: (i, j))],
        out_shape=[jax.ShapeDtypeStruct((s, n_cols), F32)], args=(n, w_t, cos, sin_signed), name=name, comm=comm)


def _mix_post(attn, u1, attn_g, ln_g, ln_b, conv_g):
    _, xa = _rms_stats(attn)
    mu = jnp.mean(u1, axis=-1, keepdims=True)
    xc = u1 - mu
    rstd = lax.rsqrt(jnp.mean(xc * xc, axis=-1, keepdims=True) + LN_EPS)
    u2 = (xc * rstd) * ln_g + ln_b
    u3 = u2 * _sigmoid(u2)
    _, x3 = _rms_stats(u3)
    return jnp.concatenate([xa * attn_g, x3 * conv_g], axis=1)


def _mix_post_back(dy, attn, u1, attn_g, ln_g, ln_b, conv_g):
    w = attn.shape[1]
    dya, dyc = dy[:, :w], dy[:, w:]
    ra, xa = _rms_stats(attn)
    dattn = _rms_back(ra, xa, dya * attn_g)
    mu = jnp.mean(u1, axis=-1, keepdims=True)
    xc = u1 - mu
    rstd = lax.rsqrt(jnp.mean(xc * xc, axis=-1, keepdims=True) + LN_EPS)
    xh = xc * rstd
    u2 = xh * ln_g + ln_b
    sig = _sigmoid(u2)
    u3 = u2 * sig
    r3, x3 = _rms_stats(u3)
    du3 = _rms_back(r3, x3, dyc * conv_g)
    du2 = du3 * (sig + u3 * (1.0 - sig))
    dxh = du2 * ln_g
    du1 = rstd * (dxh - jnp.mean(dxh, axis=-1, keepdims=True) - xh * jnp.mean(dxh * xh, axis=-1, keepdims=True))
    return dattn, du1, [_colsum(dya * xa), _colsum(dyc * x3), _colsum(du2 * xh), _colsum(du2)]


def _mm(groups, epi, extras, vecs, outs, *, trans_rhs, tm, tn, name, n_sums=0, pre=None, pre_inputs=(),
        comm=None):
    m = (pre_inputs[0] if pre is not None else groups[0][0][0]).shape[0]
    n = groups[0][0][1].shape[0] if trans_rhs else groups[0][0][1].shape[1]
    tm, tn = min(tm, m), min(tn, n)
    in_specs, args, uses_pre = [], [], []
    for grp in groups:
        for lhs, rhs in grp:
            k = rhs.shape[1] if trans_rhs else rhs.shape[0]
            uses_pre.append(lhs is None)
            if lhs is not None:
                in_specs.append(pl.BlockSpec((tm, k), lambda j, i: (i, 0)))
                args.append(lhs)
            in_specs.append(pl.BlockSpec((tn, k), lambda j, i: (j, 0)) if trans_rhs
                            else pl.BlockSpec((k, tn), lambda j, i: (0, j)))
            args.append(rhs)
    n_mm = len(args)
    for p in pre_inputs:
        in_specs.append(pl.BlockSpec((tm, p.shape[1]), lambda j, i: (i, 0)))
        args.append(p)
    for e in extras:
        in_specs.append(pl.BlockSpec((tm, tn), lambda j, i: (i, j)) if e.shape[1] == n
                        else pl.BlockSpec((tm, e.shape[1]), lambda j, i: (i, 0)))
        args.append(e)
    for v in vecs:
        in_specs.append(pl.BlockSpec((1, tn), lambda j, i: (0, j)) if v.shape[1] == n
                        else pl.BlockSpec((1, v.shape[1]), lambda j, i: (0, 0)))
        args.append(v)
    sizes = [len(g) for g in groups]
    n_pre, n_ex, n_vec = len(pre_inputs), len(extras), len(vecs)
    dims = (((1,), (1,)), ((), ())) if trans_rhs else (((1,), (0,)), ((), ()))
    out_specs, out_shape = [], []
    if pre is not None:
        k_pre = args[n_mm - 1].shape[1] if trans_rhs else args[n_mm - 1].shape[0]
        out_specs.append(pl.BlockSpec((tm, k_pre), lambda j, i: (i, 0)))
        out_shape.append(jax.ShapeDtypeStruct((m, k_pre), BF16))
    for o in outs:
        dt, width = o if isinstance(o, tuple) else (o, n)
        out_specs.append(pl.BlockSpec((tm, tn), lambda j, i: (i, j)) if width == n
                         else pl.BlockSpec((tm, width), lambda j, i: (i, 0)))
        out_shape.append(jax.ShapeDtypeStruct((m, width), dt))
    n_tiles_out = len(out_specs)
    out_specs += [pl.BlockSpec((1, tn), lambda j, i: (0, j))] * n_sums
    out_shape += [jax.ShapeDtypeStruct((1, n), F32)] * n_sums

    def body(*refs):
        ins = refs[:n_mm + n_pre + n_ex + n_vec]
        out_refs = refs[n_mm + n_pre + n_ex + n_vec:]
        vc = [r[...] for r in ins[n_mm + n_pre + n_ex:]]
        vals = []
        made = None
        if pre is not None:
            made = pre([r[...] for r in ins[n_mm:n_mm + n_pre]], vc).astype(BF16)
            vals.append(made)
        accs, pos, pair = [], 0, 0
        for size in sizes:
            acc = None
            for _ in range(size):
                if uses_pre[pair]:
                    lhs_tile = made
                else:
                    lhs_tile = ins[pos][...].astype(BF16)
                    pos += 1
                part = lax.dot_general(lhs_tile, ins[pos][...].astype(BF16), dims, preferred_element_type=F32)
                acc = part if acc is None else acc + part
                pos += 1
                pair += 1
            accs.append(acc)
        ex = [r[...] for r in ins[n_mm + n_pre:n_mm + n_pre + n_ex]]
        vals += epi(accs, ex, vc)
        for ref, val in zip(out_refs[:n_tiles_out], vals):
            ref[...] = val.astype(ref.dtype)
        if n_sums:
            @pl.when(pl.program_id(1) == 0)
            def _():
                for ref in out_refs[n_tiles_out:]:
                    ref[...] = jnp.zeros_like(ref)
            for ref, val in zip(out_refs[n_tiles_out:], vals[n_tiles_out:]):
                ref[...] += val

    return _call(body, grid=(n // tn, m // tm), in_specs=in_specs, out_specs=out_specs, out_shape=out_shape,
                 args=args, name=name, comm=comm)


def _mm_tn(lhs, rhs, name, comm=None):
    t, a = lhs.shape
    b = rhs.shape[1]
    ta = a if a <= 1536 else _tile(a, 1536, LANES)
    tk = _tile(t, 2048, 8)

    def body(l_ref, r_ref, o_ref):
        @pl.when(pl.program_id(1) == 0)
        def _():
            o_ref[...] = jnp.zeros_like(o_ref)
        o_ref[...] += lax.dot_general(l_ref[...].astype(BF16), r_ref[...].astype(BF16), (((0,), (0,)), ((), ())),
                                      preferred_element_type=F32)

    res = _call(body, grid=(a // ta, t // tk),
                in_specs=[pl.BlockSpec((tk, ta), lambda i, k: (k, i)), pl.BlockSpec((tk, b), lambda i, k: (k, 0))],
                out_specs=[pl.BlockSpec((ta, b), lambda i, k: (i, 0))], out_shape=[jax.ShapeDtypeStruct((a, b), F32)],
                args=(lhs, rhs), name=name, comm=comm)
    return res[0] if comm is None else (res[0][0], res[1])


def _ffn_tn(f):
    return _tile(f, 1536, LANES)


def _swiglu_parts(a, b):
    sig = _sigmoid(a)
    silu = a * sig
    return [silu, b * (sig + silu * (1.0 - sig)), silu * b]


def _ffn_up(n, wg_t, wu_t, name, comm=None):
    def epi(accs, ex, vc):
        return _swiglu_parts(accs[0], accs[1])
    return _mm([[(n, wg_t)], [(n, wu_t)]], epi, [], [], [BF16, BF16, BF16], trans_rhs=True, tm=512,
               tn=_ffn_tn(wg_t.shape[0]), name=name, comm=comm)


def _norm_ffn_up(h, gain, scale, shift, wg_t, wu_t, name, comm=None):
    def pre(tiles, vc):
        _, xn = _rms_stats(tiles[0])
        return (xn * vc[0]) * (1.0 + vc[1]) + vc[2]

    def epi(accs, ex, vc):
        return _swiglu_parts(accs[0], accs[1])
    return _mm([[(None, wg_t)], [(None, wu_t)]], epi, [], [gain, scale, shift], [BF16, BF16, BF16], trans_rhs=True,
               tm=256, tn=wg_t.shape[0], name=name, pre=pre, pre_inputs=[h], comm=comm)


def _mix_out(attn, u1, post, w, res, gate, norm, name):
    def pre(tiles, vc):
        return _mix_post(tiles[0], tiles[1], *vc[4:8])

    def epi(accs, ex, vc):
        h = ex[0] + vc[0] * accs[0]
        _, xn = _rms_stats(h)
        return [h, accs[0], (xn * vc[1]) * (1.0 + vc[2]) + vc[3]]
    return _mm([[(None, w)]], epi, [res], [gate] + list(norm) + list(post), [F32, BF16, BF16], trans_rhs=False,
               tm=512, tn=w.shape[1], name=name, pre=pre, pre_inputs=[attn, u1])


def _mix_dy_post_bwd(dmix, w, attn, u1, post, name):
    width = attn.shape[1]

    def epi(accs, ex, vc):
        dattn, du1, sums = _mix_post_back(accs[0], ex[0], ex[1], *vc)
        return [dattn, du1, jnp.concatenate(sums[0:2], axis=1), jnp.concatenate(sums[2:4], axis=1)]
    return _mm([[(dmix, w)]], epi, [attn, u1], list(post), [(F32, width), (F32, width)], trans_rhs=True, tm=256,
               tn=w.shape[0], name=name, n_sums=2)


def _residual_mm(lhs, w, res, gate, coef, name, norm=None, comm=None):
    def epi(accs, ex, vc):
        h = ex[0] + (coef * vc[0]) * accs[0]
        if norm is None:
            return [h, accs[0]]
        _, xn = _rms_stats(h)
        return [h, accs[0], (xn * vc[1]) * (1.0 + vc[2]) + vc[3]]
    vecs = [gate] + (list(norm) if norm is not None else [])
    outs = [F32, BF16] + ([BF16] if norm is not None else [])
    return _mm([[(lhs, w)]], epi, [res], vecs, outs, trans_rhs=False, tm=512, tn=w.shape[1], name=name, comm=comm)


def _ffn_bwd_hidden(df, wd, dhid_db, dhid_da, name, comm=None):
    def epi(accs, ex, vc):
        return [accs[0] * ex[1].astype(F32), accs[0] * ex[0].astype(F32)]
    return _mm([[(df, wd)]], epi, [dhid_db, dhid_da], [], [BF16, BF16], trans_rhs=True, tm=512,
               tn=_ffn_tn(wd.shape[0]), name=name, comm=comm)


def _plain_mm(pairs, out_dtype, trans_rhs, tn, name, tm=512, comm=None):
    def epi(accs, ex, vc):
        return [accs[0]]
    res = _mm([pairs], epi, [], [], [out_dtype], trans_rhs=trans_rhs, tm=tm, tn=tn, name=name, comm=comm)
    return res[0] if comm is None else (res[0][0], res[1])


HEADS_PER_TILE = LANES // HEAD_DIM


def _stack_heads(x):
    lane = lax.broadcasted_iota(jnp.int32, (1, LANES), 1)
    return jnp.concatenate([x * (lane // HEAD_DIM == h).astype(F32) for h in range(HEADS_PER_TILE)], axis=0)


def _unstack_heads(y):
    r = y.shape[0] // HEADS_PER_TILE
    lane = lax.broadcasted_iota(jnp.int32, (r, y.shape[1]), 1)
    out = y[0:r]
    for h in range(1, HEADS_PER_TILE):
        out = jnp.where(lane // HEAD_DIM == h, y[h * r:(h + 1) * r], out)
    return out


def _stacked_lse(lb):
    return jnp.concatenate([_lane_pick(lb, h) for h in range(HEADS_PER_TILE)], axis=0)


def _band_masks(n_row_blocks, n_col_blocks):
    shape = (n_row_blocks * BLOCK, n_col_blocks * BLOCK)
    qi = lax.broadcasted_iota(jnp.int32, shape, 0) % BLOCK
    kj = lax.broadcasted_iota(jnp.int32, shape, 1) % BLOCK
    return kj <= qi, kj >= qi


def _query_masks():
    first_valid, _ = _band_masks(HEADS_PER_TILE, 1)
    same_ok, before_ok = _band_masks(HEADS_PER_TILE, 2)
    is_cur = lax.broadcasted_iota(jnp.int32, same_ok.shape, 1) >= BLOCK
    return first_valid, jnp.logical_and(is_cur, same_ok), jnp.logical_and(jnp.logical_not(is_cur), before_ok)


def _dot_nt(a, b):
    return lax.dot_general(a.astype(BF16), b.astype(BF16), (((1,), (1,)), ((), ())), preferred_element_type=F32)


def _dot_nn(a, b):
    return lax.dot_general(a.astype(BF16), b.astype(BF16), (((1,), (0,)), ((), ())), preferred_element_type=F32)


def _dot_tn(a, b):
    return lax.dot_general(a.astype(BF16), b.astype(BF16), (((0,), (0,)), ((), ())), preferred_element_type=F32)


def _lane_pick(x, h):
    lane = lax.broadcasted_iota(jnp.int32, x.shape, 1)
    return jnp.sum(jnp.where(lane == h * HEAD_DIM, x, 0.0), axis=1, keepdims=True)


def _block_rows(idx, d):
    span = BLOCK * d
    q0 = (idx // d) * span + idx % d
    return pl.ds(q0, BLOCK, stride=d), pl.ds(q0 - span, BLOCK, stride=d)


def _two_loops(n_blocks, d, unroll):
    return d % unroll == 0 and (n_blocks - d) % unroll == 0 and n_blocks > d


def _branch_loops(n_blocks, d, visit, unroll, masks):
    first_valid, cur_part, prev_part = masks
    if _two_loops(n_blocks, d, unroll):
        full_valid = jnp.logical_or(cur_part, prev_part)

        def first(idx, carry):
            rows = pl.ds(idx, BLOCK, stride=d)
            visit(rows, [rows], first_valid)
            return carry

        def rest(idx, carry):
            rows, prev = _block_rows(idx, d)
            visit(rows, [prev, rows], full_valid)
            return carry

        lax.fori_loop(0, d, first, 0, unroll=unroll)
        lax.fori_loop(d, n_blocks, rest, 0, unroll=unroll)
        return

    def every(idx, carry):
        span = BLOCK * d
        q0 = (idx // d) * span + idx % d
        has_prev = idx >= d
        rows = pl.ds(q0, BLOCK, stride=d)
        prev = pl.ds(jnp.where(has_prev, q0 - span, q0), BLOCK, stride=d)
        visit(rows, [prev, rows], jnp.logical_or(cur_part, jnp.logical_and(prev_part, has_prev)))
        return carry

    lax.fori_loop(0, n_blocks, every, 0, unroll=unroll)


def _qkv_specs(s, tiles):
    q, k, v = [pl.BlockSpec((s, LANES), functools.partial(lambda hb, off: (0, off + hb), off=i * tiles))
               for i in range(3)]
    return q, k, v, pl.BlockSpec((s, LANES), lambda hb: (0, hb))


def _attn_seq_fwd(proj, width, name, comm=None):
    s = proj.shape[0]
    q_spec, k_spec, v_spec, cur = _qkv_specs(s, width // LANES)

    def body(q_ref, k_ref, v_ref, o_ref, l_ref, o_s, l_s):
        masks = _query_masks()
        for bi, d in enumerate(DILATIONS):
            def visit(rows, key_rows, valid, bi=bi):
                q2 = _stack_heads(q_ref[rows, :])
                keys = jnp.concatenate([k_ref[r, :] for r in key_rows], axis=0)
                vals = jnp.concatenate([v_ref[r, :] for r in key_rows], axis=0)
                sc = jnp.where(valid, _dot_nt(q2, keys), NEG)
                mx = jnp.max(sc, axis=1, keepdims=True)
                p = jnp.exp(sc - mx)
                den = jnp.sum(p, axis=1, keepdims=True)
                o_s[bi, rows, :] = _unstack_heads(_dot_nn(p, vals) / den)
                l_s[bi, rows, :] = _unstack_heads(jnp.broadcast_to(mx + jnp.log(den), (q2.shape[0], LANES)))

            _branch_loops(s // BLOCK, d, visit, 16, masks)
        for c in range(s // MERGE_CHUNK):
            rows = slice(c * MERGE_CHUNK, (c + 1) * MERGE_CHUNK)
            ls = [l_s[bi, rows, :] for bi in range(len(DILATIONS))]
            top = functools.reduce(jnp.maximum, ls)
            ws = [jnp.exp(l - top) for l in ls]
            den = functools.reduce(lambda a, b: a + b, ws)
            num = functools.reduce(lambda a, b: a + b, [w * o_s[bi, rows, :] for bi, w in enumerate(ws)])
            o_ref[rows, :] = num / den
            l_ref[rows, :] = top + jnp.log(den)

    return _call(
        body, grid=(width // LANES,), in_specs=[q_spec, k_spec, v_spec], out_specs=[cur, cur],
        out_shape=[jax.ShapeDtypeStruct((s, width), F32)] * 2,
        scratch_shapes=[pltpu.VMEM((len(DILATIONS), s, LANES), F32)] * 2,
        args=(proj, proj, proj), name=name, comm=comm)


def _attn_seq_bwd(proj, do, o, lse, cos, sin_signed, name, comm=None):
    s, width = do.shape
    q_spec, k_spec, v_spec, cur = _qkv_specs(s, width // LANES)
    table = pl.BlockSpec((s, LANES), lambda hb: (0, 0))
    qscale = HEAD_DIM ** -0.5

    def body(q_ref, k_ref, v_ref, do_ref, o_ref, l_ref, cos_ref, sin_ref, dq_out, dk_out, dv_out,
             dq_ref, dk_ref, dv_ref):
        unroll = 16
        masks = _query_masks()
        order = sorted(DILATIONS, reverse=True)
        fresh_first = _two_loops(s // BLOCK, order[0], unroll)
        if not fresh_first:
            dq_ref[...] = jnp.zeros_like(dq_ref)
            dk_ref[...] = jnp.zeros_like(dk_ref)
            dv_ref[...] = jnp.zeros_like(dv_ref)
        for d in order:
            def visit(rows, key_rows, valid, fresh=fresh_first and d == order[0]):
                dob = do_ref[rows, :]
                q2 = _stack_heads(q_ref[rows, :])
                do2 = _stack_heads(dob)
                delta = jnp.sum(_stack_heads(dob * o_ref[rows, :]), axis=1, keepdims=True)
                lse2 = _stacked_lse(l_ref[rows, :])
                keys = jnp.concatenate([k_ref[r, :] for r in key_rows], axis=0)
                vals = jnp.concatenate([v_ref[r, :] for r in key_rows], axis=0)
                p = jnp.where(valid, jnp.exp(_dot_nt(q2, keys) - lse2), 0.0)
                ds = p * (_dot_nt(do2, vals) - delta)
                dq = _unstack_heads(_dot_nn(ds, keys))
                dkk = _dot_tn(ds, q2)
                dvv = _dot_tn(p, do2)
                if fresh:
                    dq_ref[rows, :] = dq
                else:
                    dq_ref[rows, :] += dq
                for i, r in enumerate(key_rows):
                    own = i == len(key_rows) - 1
                    if fresh and own:
                        dk_ref[r, :] = dkk[i * BLOCK:(i + 1) * BLOCK]
                        dv_ref[r, :] = dvv[i * BLOCK:(i + 1) * BLOCK]
                    else:
                        dk_ref[r, :] += dkk[i * BLOCK:(i + 1) * BLOCK]
                        dv_ref[r, :] += dvv[i * BLOCK:(i + 1) * BLOCK]

            _branch_loops(s // BLOCK, d, visit, unroll, masks)
        for c in range(s // MERGE_CHUNK):
            rows = slice(c * MERGE_CHUNK, (c + 1) * MERGE_CHUNK)
            cos, sin = cos_ref[rows, :], sin_ref[rows, :]
            dq, dk = dq_ref[rows, :], dk_ref[rows, :]
            dq_out[rows, :] = ((dq * cos - _partner(dq) * sin) * qscale).astype(BF16)
            dk_out[rows, :] = (dk * cos - _partner(dk) * sin).astype(BF16)
            dv_out[rows, :] = dv_ref[rows, :].astype(BF16)

    return _call(
        body, grid=(width // LANES,), in_specs=[q_spec, k_spec, v_spec, cur, cur, cur, table, table],
        out_specs=[cur, cur, cur], out_shape=[jax.ShapeDtypeStruct((s, width), BF16)] * 3,
        scratch_shapes=[pltpu.VMEM((s, LANES), F32)] * 3,
        args=(proj, proj, proj, do, o, lse, cos, sin_signed), name=name, comm=comm)


def _conv_specs(s, a_block, b_block):
    per = CONV_CHUNK // CONV_HALO
    a_cur = pl.BlockSpec((CONV_CHUNK, LANES), lambda cb, i: (i, a_block + cb))
    b_cur = pl.BlockSpec((CONV_CHUNK, LANES), lambda cb, i: (i, b_block + cb))
    a_halo = pl.BlockSpec((CONV_HALO, LANES), lambda cb, i: (jnp.maximum(i * per - 1, 0), a_block + cb))
    b_halo = pl.BlockSpec((CONV_HALO, LANES), lambda cb, i: (jnp.maximum(i * per - 1, 0), b_block + cb))
    w_spec = pl.BlockSpec((CONV_KERNEL, LANES), lambda cb, i: (0, cb))
    vec = pl.BlockSpec((1, LANES), lambda cb, i: (0, cb))
    out = pl.BlockSpec((CONV_CHUNK, LANES), lambda cb, i: (i, cb))
    return a_cur, b_cur, a_halo, b_halo, w_spec, vec, out


def _fill_glu_window(win, a_ref, b_ref, ah_ref, bh_ref, first):
    halo = ah_ref[...] * _sigmoid(bh_ref[...])
    win[0:CONV_HALO, :] = jnp.where(first, 0.0, halo)
    win[CONV_HALO:, :] = a_ref[...] * _sigmoid(b_ref[...])


def _conv_fwd(proj, a_block, b_block, w, bias, name, comm=None):
    s = proj.shape[0]
    cw = w.shape[1]
    a_cur, b_cur, a_halo, b_halo, w_spec, vec, out = _conv_specs(s, a_block, b_block)
    lead = CONV_HALO - (CONV_KERNEL - 1)

    def body(a_ref, b_ref, ah_ref, bh_ref, w_ref, bias_ref, o_ref, win):
        _fill_glu_window(win, a_ref, b_ref, ah_ref, bh_ref, pl.program_id(1) == 0)
        for sub in range(CONV_CHUNK // CONV_SUB):
            base = sub * CONV_SUB
            acc = jnp.zeros((CONV_SUB, LANES), F32) + bias_ref[...]
            for j in range(CONV_KERNEL):
                acc = acc + w_ref[j:j + 1, :] * win[base + lead + j:base + lead + j + CONV_SUB, :]
            o_ref[base:base + CONV_SUB, :] = acc

    return _call(
        body, grid=(cw // LANES, s // CONV_CHUNK), in_specs=[a_cur, b_cur, a_halo, b_halo, w_spec, vec],
        out_specs=[out], out_shape=[jax.ShapeDtypeStruct((s, cw), F32)],
        scratch_shapes=[pltpu.VMEM((CONV_CHUNK + CONV_HALO, LANES), F32)],
        args=(proj, proj, proj, proj, w, bias), name=name, comm=comm)


def _conv_bwd(proj, a_block, b_block, w, du1, name):
    s = proj.shape[0]
    cw = w.shape[1]
    a_cur, b_cur, a_halo, b_halo, w_spec, vec, out = _conv_specs(s, a_block, b_block)
    per = CONV_CHUNK // CONV_HALO
    n_chunks = s // CONV_CHUNK
    d_next = pl.BlockSpec((CONV_HALO, LANES), lambda cb, i: (jnp.minimum((i + 1) * per, s // CONV_HALO - 1), cb))
    lead = CONV_HALO - (CONV_KERNEL - 1)

    def body(a_ref, b_ref, ah_ref, bh_ref, w_ref, d_ref, dn_ref, da_ref, db_ref, dw_ref, dbias_ref, win, dwin):
        i = pl.program_id(1)
        _fill_glu_window(win, a_ref, b_ref, ah_ref, bh_ref, i == 0)
        dwin[0:CONV_CHUNK, :] = d_ref[...]
        dwin[CONV_CHUNK:, :] = jnp.where(i == n_chunks - 1, 0.0, dn_ref[...])

        @pl.when(i == 0)
        def _():
            dw_ref[...] = jnp.zeros_like(dw_ref)
            dbias_ref[...] = jnp.zeros_like(dbias_ref)

        dbias_ref[...] += _colsum(d_ref[...])
        for sub in range(CONV_CHUNK // CONV_SUB):
            base = sub * CONV_SUB
            dcur = dwin[base:base + CONV_SUB, :]
            du0 = jnp.zeros((CONV_SUB, LANES), F32)
            for j in range(CONV_KERNEL):
                back = CONV_KERNEL - 1 - j
                du0 = du0 + w_ref[j:j + 1, :] * dwin[base + back:base + back + CONV_SUB, :]
                dw_ref[j:j + 1, :] += _colsum(dcur * win[base + lead + j:base + lead + j + CONV_SUB, :])
            av = a_ref[base:base + CONV_SUB, :]
            sig = _sigmoid(b_ref[base:base + CONV_SUB, :])
            da_ref[base:base + CONV_SUB, :] = (du0 * sig).astype(BF16)
            db_ref[base:base + CONV_SUB, :] = (du0 * av * sig * (1.0 - sig)).astype(BF16)

    return pl.pallas_call(
        body, grid=(cw // LANES, n_chunks), in_specs=[a_cur, b_cur, a_halo, b_halo, w_spec, out, d_next],
        out_specs=[out, out, w_spec, vec],
        out_shape=[jax.ShapeDtypeStruct((s, cw), BF16), jax.ShapeDtypeStruct((s, cw), BF16),
                   jax.ShapeDtypeStruct((CONV_KERNEL, cw), F32), jax.ShapeDtypeStruct((1, cw), F32)],
        scratch_shapes=[pltpu.VMEM((CONV_CHUNK + CONV_HALO, LANES), F32)] * 2,
        compiler_params=_params(2), name=name)(proj, proj, proj, proj, w, du1, du1)


def _adamw_math(w, g, m, v):
    m = ADAM_B1 * m + (1.0 - ADAM_B1) * g
    v = ADAM_B2 * v + (1.0 - ADAM_B2) * (g * g)
    m_hat = m / (1.0 - ADAM_B1 ** ADAM_STEP)
    v_hat = v / (1.0 - ADAM_B2 ** ADAM_STEP)
    delta = -ADAM_LR * (m_hat / (jnp.sqrt(v_hat) + ADAM_EPS) + ADAM_WD * w)
    return delta, m, v


def _adamw_big(w, g, m, v, name):
    rows, cols = w.shape
    tile = _tile(rows, 256, 8)
    spec = pl.BlockSpec((tile, cols), lambda i: (i, 0))

    def body(w_ref, g_ref, m_ref, v_ref, d_out, m_out, v_out):
        d_out[...], m_out[...], v_out[...] = _adamw_math(w_ref[...], g_ref[...], m_ref[...], v_ref[...])

    return pl.pallas_call(body, grid=(rows // tile,), in_specs=[spec] * 4, out_specs=[spec] * 3,
                          out_shape=[jax.ShapeDtypeStruct(w.shape, F32)] * 3, compiler_params=_params(1),
                          name=name)(w, g, m, v)


def _adamw_reduced(w, land, m, v, name):
    rows, cols = w.shape
    tile = _tile(rows, 256, 16)
    spec = pl.BlockSpec((tile, cols), lambda i: (i, 0))

    def body(w_ref, l_ref, m_ref, v_ref, g_out, d_out, m_out, v_out):
        g = l_ref[0].astype(F32)
        for q in range(1, N_CHIP):
            g = g + l_ref[q].astype(F32)
        g_out[...] = g
        d_out[...], m_out[...], v_out[...] = _adamw_math(w_ref[...], g, m_ref[...], v_ref[...])

    return pl.pallas_call(body, grid=(rows // tile,),
                          in_specs=[spec, pl.BlockSpec((N_CHIP, tile, cols), lambda i: (0, i, 0)), spec, spec],
                          out_specs=[spec] * 4, out_shape=[jax.ShapeDtypeStruct(w.shape, F32)] * 4,
                          compiler_params=_params(1), name=name)(w, land, m, v)


def _adamw_small(ws, gs, ms, vs, name):
    n = len(ws)

    def body(*refs):
        ins, outs = refs[:4 * n], refs[4 * n:]
        for t in range(n):
            res = _adamw_math(ins[t][...], ins[n + t][...], ins[2 * n + t][...], ins[3 * n + t][...])
            for j in range(3):
                outs[j * n + t][...] = res[j]

    shapes = [jax.ShapeDtypeStruct(w.shape, F32) for w in ws]
    res = pl.pallas_call(body, out_shape=shapes * 3, compiler_params=pltpu.CompilerParams(vmem_limit_bytes=VMEM_LIMIT),
                         name=name)(*ws, *gs, *ms, *vs)
    return res[:n], res[n:2 * n], res[2 * n:]


def _sum_blocks(x, n_blocks, name):
    r = x.shape[0] // n_blocks

    def body(x_ref, o_ref):
        acc = x_ref[0:r, :]
        for b in range(1, n_blocks):
            acc = acc + x_ref[b * r:(b + 1) * r, :]
        o_ref[...] = acc

    return pl.pallas_call(body, out_shape=jax.ShapeDtypeStruct((r, x.shape[1]), F32),
                          compiler_params=pltpu.CompilerParams(vmem_limit_bytes=VMEM_LIMIT), name=name)(x)


def _coords():
    return lax.axis_index("x"), lax.axis_index("y"), lax.axis_index("c")


def _flip(v, bit):
    return 1 - v if bit else v


def _ag_small(x, name):
    r, c = x.shape

    def body(x_ref, o_ref, send, recv, local_sem):
        mx, my, mc = _coords()

        def rows(px, py, pc):
            return o_ref.at[pl.ds(pl.multiple_of((4 * px + 2 * py + pc) * r, 8), r), :]

        local = pltpu.make_async_copy(x_ref, rows(mx, my, mc), local_sem)
        local.start()
        peers = [(_flip(mx, k >> 2 & 1), _flip(my, k >> 1 & 1), _flip(mc, k & 1)) for k in range(1, N_DEV)]
        sends = [pltpu.make_async_remote_copy(x_ref, rows(mx, my, mc), send.at[k], recv.at[k], device_id=p,
                                              device_id_type=MESH) for k, p in enumerate(peers)]
        for cp in sends:
            cp.start()
        for k, p in enumerate(peers):
            pltpu.make_async_remote_copy(x_ref, rows(*p), send.at[k], recv.at[k], device_id=p,
                                         device_id_type=MESH).wait_recv()
        for cp in sends:
            cp.wait_send()
        local.wait()

    vm = pl.BlockSpec(memory_space=pltpu.VMEM)
    return pl.pallas_call(
        body, in_specs=[vm], out_specs=vm, out_shape=jax.ShapeDtypeStruct((N_DEV * r, c), x.dtype),
        scratch_shapes=[pltpu.SemaphoreType.DMA((N_DEV - 1,)), pltpu.SemaphoreType.DMA((N_DEV - 1,)),
                        pltpu.SemaphoreType.DMA(())],
        name=name)(x)


class _GatherSmall:
    mid = None

    def __init__(self, x):
        self.inputs = [x]
        self.out_shapes = [jax.ShapeDtypeStruct((N_DEV * x.shape[0], x.shape[1]), x.dtype)]
        self.scratch = [pltpu.SemaphoreType.DMA((N_DEV - 1,)), pltpu.SemaphoreType.DMA((N_DEV - 1,)),
                        pltpu.SemaphoreType.DMA(())]

    def _plan(self, x_refs, o_refs, sems):
        send, recv, local_sem = sems
        x_ref, o_ref = x_refs[0], o_refs[0]
        r = x_ref.shape[0]
        mx, my, mc = _coords()

        def rows(px, py, pc):
            return o_ref.at[pl.ds(pl.multiple_of((4 * px + 2 * py + pc) * r, 8), r), :]

        peers = [(_flip(mx, k >> 2 & 1), _flip(my, k >> 1 & 1), _flip(mc, k & 1)) for k in range(1, N_DEV)]
        out = [pltpu.make_async_remote_copy(x_ref, rows(mx, my, mc), send.at[k], recv.at[k], device_id=p,
                                            device_id_type=MESH) for k, p in enumerate(peers)]
        arrivals = [pltpu.make_async_remote_copy(x_ref, rows(*p), send.at[k], recv.at[k], device_id=p,
                                                 device_id_type=MESH) for k, p in enumerate(peers)]
        return out, arrivals, pltpu.make_async_copy(x_ref, rows(mx, my, mc), local_sem)

    def start(self, x_refs, o_refs, sems):
        out, _, local = self._plan(x_refs, o_refs, sems)
        local.start()
        for cp in out:
            cp.start()

    def finish(self, x_refs, o_refs, sems):
        out, arrivals, local = self._plan(x_refs, o_refs, sems)
        for cp in arrivals:
            cp.wait_recv()
        for cp in out:
            cp.wait_send()
        local.wait()


class _ModExchange:
    def __init__(self, first, w_ada):
        self.d, cols = w_ada.shape
        part = jax.ShapeDtypeStruct((N_DEV, cols), F32)
        self.g1, self.g2 = _GatherSmall(first), _GatherSmall(part)
        self.inputs = [first, w_ada]
        self.out_shapes = [self.g1.out_shapes[0], jax.ShapeDtypeStruct((N_DEV, self.d), F32), part,
                           self.g2.out_shapes[0]]
        self.scratch = self.g1.scratch + self.g2.scratch + [
            pltpu.VMEM(self.g1.out_shapes[0].shape, F32), pltpu.VMEM(w_ada.shape, F32),
            pltpu.VMEM((N_DEV, self.d), F32), pltpu.VMEM((N_DEV, cols), F32), pltpu.SemaphoreType.DMA(())]

    def start(self, cin, cout, scr):
        self.g1.start(cin[0:1], cout[0:1], scr[0:3])
        pltpu.make_async_copy(cin[1], scr[7], scr[10]).start()

    def mid(self, cin, cout, scr):
        gathered, w_v, silu_v, part_v = scr[6:10]
        self.g1.finish(cin[0:1], cout[0:1], scr[0:3])
        pltpu.sync_copy(cout[0], gathered)
        rows_per = cin[0].shape[0]
        for j in range(N_DEV):
            silu_v[j:j + 1, :] = gathered[j * rows_per:j * rows_per + 1, 0:self.d]
        c_all = silu_v[...]
        silu_v[...] = c_all * _sigmoid(c_all)
        pltpu.sync_copy(silu_v, cout[1])
        pltpu.make_async_copy(cin[1], w_v, scr[10]).wait()
        part_v[...] = _dot_nn(silu_v[...], w_v[...])
        pltpu.sync_copy(part_v, cout[2])
        self.g2.start(cout[2:3], cout[3:4], scr[3:6])

    def finish(self, cin, cout, scr):
        self.g2.finish(cout[2:3], cout[3:4], scr[3:6])


class _GatherWeights:
    def __init__(self, shards):
        n_t = len(shards)
        self.inputs = list(shards)
        self.out_shapes = [jax.ShapeDtypeStruct((N_DEV * x.shape[0], x.shape[1]), x.dtype) for x in shards]
        self.scratch = [pltpu.SemaphoreType.DMA((n_t, 8)), pltpu.SemaphoreType.DMA((n_t, 8)),
                        pltpu.SemaphoreType.DMA((n_t,))]

    def _plan(self, x_refs, o_refs, sems):
        send, recv, local_sem = sems
        mx, my, mc = _coords()
        me, sibling = (mx, my, mc), (mx, my, 1 - mc)
        xn, yn, diag = (1 - mx, my), (mx, 1 - my), (1 - mx, 1 - my)

        def rows(t, chip, core, half=None):
            r = x_refs[t].shape[0]
            base = (4 * chip[0] + 2 * chip[1] + core) * r
            if half is None:
                return o_refs[t].at[pl.ds(pl.multiple_of(base, 8), r), :]
            return o_refs[t].at[pl.ds(pl.multiple_of(base + half * (r // 2), 8), r // 2), :]

        def copy(t, k, block, to, src=None):
            return pltpu.make_async_remote_copy(
                src_ref=block if src is None else src, dst_ref=block,
                send_sem=send.at[t, k], recv_sem=recv.at[t, k], device_id=to, device_id_type=MESH)

        def local(t):
            return pltpu.make_async_copy(x_refs[t], rows(t, (mx, my), mc), local_sem.at[t])

        return (mx, my), mc, me, sibling, xn, yn, diag, rows, copy, local

    def start(self, x_refs, o_refs, sems):
        chip, mc, me, sibling, xn, yn, diag, rows, copy, local = self._plan(x_refs, o_refs, sems)
        for t in range(len(x_refs)):
            mine = rows(t, chip, mc)
            local(t).start()
            copy(t, 0, mine, sibling, src=x_refs[t]).start()
            copy(t, 1, mine, (*xn, mc), src=x_refs[t]).start()
            copy(t, 2, mine, (*yn, mc), src=x_refs[t]).start()

    def mid(self, x_refs, o_refs, sems):
        chip, mc, me, sibling, xn, yn, diag, rows, copy, local = self._plan(x_refs, o_refs, sems)
        for t in range(len(x_refs)):
            copy(t, 1, rows(t, xn, mc), me).wait_recv()
            copy(t, 3, rows(t, xn, mc, 0), (*yn, mc)).start()
            copy(t, 5, rows(t, xn, mc), sibling).start()
        for t in range(len(x_refs)):
            copy(t, 2, rows(t, yn, mc), me).wait_recv()
            copy(t, 4, rows(t, yn, mc, 1), (*xn, mc)).start()
            copy(t, 6, rows(t, yn, mc), sibling).start()

    def finish(self, x_refs, o_refs, sems):
        chip, mc, me, sibling, xn, yn, diag, rows, copy, local = self._plan(x_refs, o_refs, sems)
        for t in range(len(x_refs)):
            copy(t, 3, rows(t, diag, mc, 0), me).wait_recv()
            copy(t, 4, rows(t, diag, mc, 1), me).wait_recv()
            copy(t, 7, rows(t, diag, mc), sibling).start()
        for t in range(len(x_refs)):
            copy(t, 0, rows(t, chip, 1 - mc), me).wait_recv()
            copy(t, 5, rows(t, xn, 1 - mc), me).wait_recv()
            copy(t, 6, rows(t, yn, 1 - mc), me).wait_recv()
            copy(t, 7, rows(t, diag, 1 - mc), me).wait_recv()
            mine = rows(t, chip, mc)
            copy(t, 0, mine, sibling, src=x_refs[t]).wait_send()
            copy(t, 1, mine, (*xn, mc), src=x_refs[t]).wait_send()
            copy(t, 2, mine, (*yn, mc), src=x_refs[t]).wait_send()
            copy(t, 3, rows(t, xn, mc, 0), (*yn, mc)).wait_send()
            copy(t, 4, rows(t, yn, mc, 1), (*xn, mc)).wait_send()
            copy(t, 5, rows(t, xn, mc), sibling).wait_send()
            copy(t, 6, rows(t, yn, mc), sibling).wait_send()
            copy(t, 7, rows(t, diag, mc), sibling).wait_send()
            local(t).wait()


class _SiblingExchange:
    mid = None

    def __init__(self, grads):
        n_t = len(grads)
        self.inputs = list(grads)
        self.out_shapes = [jax.ShapeDtypeStruct((N_CHIP,) + g.shape[2:], F32) for g in grads]
        self.scratch = [pltpu.SemaphoreType.DMA((n_t,)), pltpu.SemaphoreType.DMA((n_t,))]

    def _copies(self, g_refs, land, sems):
        send, recv = sems
        mx, my, mc = _coords()
        return [pltpu.make_async_remote_copy(g_refs[t].at[:, 1 - mc], land[t], send.at[t], recv.at[t],
                                             device_id=(mx, my, 1 - mc), device_id_type=MESH)
                for t in range(len(g_refs))]

    def start(self, g_refs, land, sems):
        for cp in self._copies(g_refs, land, sems):
            cp.start()

    def finish(self, g_refs, land, sems):
        for cp in self._copies(g_refs, land, sems):
            cp.wait()


class _Together:
    def __init__(self, *comms):
        self.comms = comms
        self.inputs = [x for c in comms for x in c.inputs]
        self.out_shapes = [x for c in comms for x in c.out_shapes]
        self.scratch = [x for c in comms for x in c.scratch]
        self.mid = self._mid if any(c.mid is not None for c in comms) else None

    def _each(self, phase, cin, cout, sems):
        i = o = s = 0
        for c in self.comms:
            fn = getattr(c, phase)
            ni, no, ns = len(c.inputs), len(c.out_shapes), len(c.scratch)
            if fn is not None:
                fn(cin[i:i + ni], cout[o:o + no], sems[s:s + ns])
            i, o, s = i + ni, o + no, s + ns

    def start(self, cin, cout, sems):
        self._each("start", cin, cout, sems)

    def _mid(self, cin, cout, sems):
        self._each("mid", cin, cout, sems)

    def finish(self, cin, cout, sems):
        self._each("finish", cin, cout, sems)


def _standalone(comm, name):
    def body():
        pass
    return _call(body, grid=(1,), in_specs=[], out_specs=[], out_shape=[], args=(), name=name, comm=comm)[1]


def _chip_partials(g4s, lands, name):
    n_t = len(g4s)
    in_specs, out_specs, out_shape = [], [], []
    for g4 in g4s:
        _, _, r, c = g4.shape
        in_specs.append(pl.BlockSpec((None, None, r, c), lambda q: (q, lax.axis_index("c"), 0, 0)))
        out_specs.append(pl.BlockSpec((None, r, c), lambda q: (q, 0, 0)))
        out_shape.append(jax.ShapeDtypeStruct((N_CHIP, r, c), BF16))
    in_specs += [pl.BlockSpec((None,) + g4.shape[2:], lambda q: (q, 0, 0)) for g4 in g4s]

    def body(*refs):
        for t in range(n_t):
            refs[2 * n_t + t][...] = (refs[t][...] + refs[n_t + t][...]).astype(BF16)

    return pl.pallas_call(body, grid=(N_CHIP,), in_specs=in_specs, out_specs=out_specs, out_shape=out_shape,
                          compiler_params=_params(1), name=name)(*g4s, *lands)


class _ChipExchange:
    mid = None

    def __init__(self, parts):
        n_t = len(parts)
        self.inputs = list(parts)
        self.out_shapes = [jax.ShapeDtypeStruct(p.shape, p.dtype) for p in parts]
        self.scratch = [pltpu.SemaphoreType.DMA((n_t, 3)), pltpu.SemaphoreType.DMA((n_t, 3)),
                        pltpu.SemaphoreType.DMA((n_t,))]

    def _plan(self, p_refs, land, sems):
        send, recv, local_sem = sems
        mx, my, mc = _coords()
        my_chip = 2 * mx + my
        peers = [(_flip(mx, fx), _flip(my, fy)) for fx, fy in ((1, 0), (0, 1), (1, 1))]

        def out(t, k):
            px, py = peers[k]
            return pltpu.make_async_remote_copy(p_refs[t].at[2 * px + py], land[t].at[my_chip], send.at[t, k],
                                                recv.at[t, k], device_id=(px, py, mc), device_id_type=MESH)

        def arrival(t, k):
            px, py = peers[k]
            return pltpu.make_async_remote_copy(p_refs[t].at[my_chip], land[t].at[2 * px + py], send.at[t, k],
                                                recv.at[t, k], device_id=(px, py, mc), device_id_type=MESH)

        def local(t):
            return pltpu.make_async_copy(p_refs[t].at[my_chip], land[t].at[my_chip], local_sem.at[t])

        return out, arrival, local

    def start(self, p_refs, land, sems):
        out, arrival, local = self._plan(p_refs, land, sems)
        for t in range(len(p_refs)):
            local(t).start()
            for k in range(3):
                out(t, k).start()

    def finish(self, p_refs, land, sems):
        out, arrival, local = self._plan(p_refs, land, sems)
        for t in range(len(p_refs)):
            for k in range(3):
                arrival(t, k).wait_recv()
                out(t, k).wait_send()
            local(t).wait()


def _rope_tables(s, width):
    heads = width // HEAD_DIM
    inv_freq = ROPE_THETA ** (-jnp.arange(0, HEAD_DIM, 2, dtype=F32) / HEAD_DIM)
    inv_full = jnp.tile(inv_freq, 2 * heads)
    sign = jnp.tile(jnp.concatenate([-jnp.ones((HALF_HEAD,), F32), jnp.ones((HALF_HEAD,), F32)]), heads)
    ang = jnp.arange(s, dtype=F32)[:, None] * inv_full[None, :]
    return jnp.cos(ang), jnp.sin(ang) * sign[None, :]


def _pad_rows(v, rows):
    return jnp.concatenate([v, jnp.zeros((rows - 1, v.shape[1]), v.dtype)], axis=0)


def kernel(x, c, w_ada, b_ada, ffn1_norm_g, ffn1_w_gate, ffn1_w_up, ffn1_w_down, mix_norm_g, w_in, conv_dw_w, conv_dw_b, conv_ln_g, conv_ln_b, attn_out_g, conv_out_g, w_out, ffn2_norm_g, ffn2_w_gate, ffn2_w_up, ffn2_w_down, final_norm_g, loss_target, m_w_ada, m_b_ada, m_ffn1_norm_g, m_ffn1_w_gate, m_ffn1_w_up, m_ffn1_w_down, m_mix_norm_g, m_w_in, m_conv_dw_w, m_conv_dw_b, m_conv_ln_g, m_conv_ln_b, m_attn_out_g, m_conv_out_g, m_w_out, m_ffn2_norm_g, m_ffn2_w_gate, m_ffn2_w_up, m_ffn2_w_down, m_final_norm_g, v_w_ada, v_b_ada, v_ffn1_norm_g, v_ffn1_w_gate, v_ffn1_w_up, v_ffn1_w_down, v_mix_norm_g, v_w_in, v_conv_dw_w, v_conv_dw_b, v_conv_ln_g, v_conv_ln_b, v_attn_out_g, v_conv_out_g, v_w_out, v_ffn2_norm_g, v_ffn2_w_gate, v_ffn2_w_up, v_ffn2_w_down, v_final_norm_g):
    mx, my, mc = _coords()
    me = 4 * mx + 2 * my + mc
    s, d = x.shape[1], x.shape[2]
    aw = d // 2
    x2, target = x[0], loss_target[0]
    n_mod = w_ada.shape[2] * N_DEV // d
    mod_cols = w_ada.shape[2]

    def shard(w, transpose):
        return (w[0].T if transpose else w[0]).astype(BF16)

    cw_shard = conv_dw_w.shape[3]
    n_taps = CONV_KERNEL * cw_shard
    first_len = -(-(d + n_taps) // LANES) * LANES
    first = jnp.concatenate([c, conv_dw_w[0, :, 0, :].reshape(1, n_taps), jnp.zeros((1, first_len - d - n_taps), F32)], axis=1)
    first_all, silu_c, _, mod_all, wg1, wu1 = _standalone(
        _Together(_ModExchange(_pad_rows(first, 8), w_ada[0]),
                  _GatherWeights([shard(ffn1_w_gate, True), shard(ffn1_w_up, True)])), "ag_first")
    first_all = first_all[0::8]
    conv_w = first_all[:, d:d + n_taps].reshape(N_DEV, CONV_KERNEL, cw_shard).transpose(1, 0, 2).reshape(CONV_KERNEL, aw)

    mod_all = mod_all.reshape(N_DEV, N_DEV, mod_cols)
    mod = lax.dynamic_index_in_dim(mod_all, me, axis=1, keepdims=False).reshape(1, n_mod * d) + b_ada
    sh1, sc1, g1, sh2, sc2, g2, sh3, sc3, g3 = [mod[:, i * d:(i + 1) * d] for i in range(n_mod)]

    def split(g):
        return g.reshape(N_CHIP, 2, g.shape[0] // N_DEV, g.shape[1])

    def partials(g4s, lands, tag):
        return _chip_partials(g4s, lands, "chip_partials_" + tag)

    (n1, silu1, gs1, hid1), (wd1, win_t, wout) = _norm_ffn_up(
        x2, ffn1_norm_g, sc1, sh1, wg1, wu1, "ffn1_up",
        comm=_GatherWeights([shard(ffn1_w_down, False), shard(w_in, True), shard(w_out, False)]))
    h1, f1, n2 = _residual_mm(hid1, wd1, x2, g1, 0.5, "ffn1_down", norm=(mix_norm_g, sc2, sh2))
    cos, sin_signed = _rope_tables(s, LANES)
    proj, = _proj_rope(n2, win_t, cos, sin_signed, aw, "proj")
    lanes_per = aw // LANES
    (attn, lse), (wg2, wu2, wd2) = _attn_seq_fwd(
        proj, aw, "attn_fwd",
        comm=_GatherWeights([shard(ffn2_w_gate, True), shard(ffn2_w_up, True), shard(ffn2_w_down, False)]))
    u1, = _conv_fwd(proj, 3 * lanes_per, 4 * lanes_per, conv_w, conv_dw_b, "conv_fwd")
    post = (attn_out_g, conv_ln_g, conv_ln_b, conv_out_g)
    y, h2, mix, n3 = _mix_out(attn, u1, post, wout, h1, g2, (ffn2_norm_g, sc3, sh3), "mix_out")
    silu3, gs3, hid3 = _ffn_up(n3, wg2, wu2, "ffn2_up")

    dh3, df3, err2, d_final_g, dg3 = _last_mm_loss(hid3, wd2, h2, g3, 0.5, target, final_norm_g.reshape(1, d),
                                                   "ffn2_down_loss")
    loss_part = jnp.zeros((1, LANES), F32).at[0, 0].set(0.5 * jnp.sum(err2) / d)

    da3, db3 = _ffn_bwd_hidden(df3, wd2, silu3, gs3, "ffn2_hidden_bwd")
    g4_a = [split(_mm_tn(da3, n3, "ffn2_dwg")), split(_mm_tn(db3, n3, "ffn2_dwu")), split(_mm_tn(hid3, df3, "ffn2_dwd"))]
    (dh2, dmix, dsh3, dsc3, dgn3, dg2), land_a = _mm_norm_mod_bwd(
        [(da3, wg2), (db3, wu2)], h2, dh3, ffn2_norm_g, sc3, (mix, g2, 1.0), "ffn2_dn_norm3_bwd", tm=256,
        comm=_SiblingExchange(g4_a))
    parts_a = partials(g4_a, land_a, "a")
    g_wout = _mm_tn(y, dmix, "mix_dwout")
    dattn, du1, d_gains, d_ln = _mix_dy_post_bwd(dmix, wout, attn, u1, post, "mix_dy_post_bwd")
    d_attn_g, d_conv_g, d_ln_g, d_ln_b = d_gains[:, :aw], d_gains[:, aw:], d_ln[:, :aw], d_ln[:, aw:]
    dga, dgb, d_taps, d_conv_b = _conv_bwd(proj, 3 * lanes_per, 4 * lanes_per, conv_w, du1, "conv_bwd")
    (dq, dk, dv), sums_a = _attn_seq_bwd(proj, dattn, attn, lse, cos, sin_signed, "attn_bwd",
                                         comm=_ChipExchange(parts_a))
    dproj = jnp.concatenate([dq, dk, dv, dga, dgb], axis=1)
    g4_b = [split(g_wout), split(_mm_tn(dproj, n2, "mix_dwin"))]
    (dh1, df1, dsh2, dsc2, dgn2, dg1), land_b = _mm_norm_mod_bwd(
        [(dproj, win_t)], h1, dh2, mix_norm_g, sc2, (f1, g1, 0.5), "mix_dn_norm2_bwd", tm=512,
        comm=_SiblingExchange(g4_b))
    parts_b = partials(g4_b, land_b, "b")
    g4_c = [split(_mm_tn(hid1, df1, "ffn1_dwd"))]
    (da1, db1), both = _ffn_bwd_hidden(df1, wd1, silu1, gs1, "ffn1_hidden_bwd",
                                       comm=_Together(_ChipExchange(parts_b), _SiblingExchange(g4_c)))
    sums_b, land_c = both[:2], both[2:]
    parts_c = partials(g4_c, land_c, "c")
    g4_d = [split(_mm_tn(db1, n1, "ffn1_dwu"))]
    g_wg1, both = _mm_tn(da1, n1, "ffn1_dwg", comm=_Together(_ChipExchange(parts_c), _SiblingExchange(g4_d)))
    sums_c, land_d = both[:1], both[1:]
    parts_d = partials(g4_d, land_d, "d")
    g4_e = [split(g_wg1)]
    dn1, both = _plain_mm([(da1, wg1), (db1, wu1)], BF16, False, d, "ffn1_dn",
                          comm=_Together(_ChipExchange(parts_d), _SiblingExchange(g4_e)))
    sums_d, land_e = both[:1], both[1:]
    parts_e = partials(g4_e, land_e, "e")
    (dx, dsh1, dsc1, dgn1), sums_e = _norm_mod_bwd(dn1, x2, dh1, ffn1_norm_g, sc1, "norm1_bwd",
                                                   comm=_ChipExchange(parts_e))

    dmod = jnp.concatenate([dsh1, dsc1, dg1, dsh2, dsc2, dg2, dsh3, dsc3, dg3], axis=1)
    small = [dmod, dgn1, dgn2, dgn3, d_final_g, d_conv_b, d_ln_g, d_ln_b, d_attn_g, d_conv_g,
             d_taps.reshape(1, CONV_KERNEL * aw), loss_part]
    sizes = [v.shape[1] for v in small]
    total = sum(sizes)
    padded = -(-total // (8 * LANES)) * (8 * LANES)
    packed = jnp.concatenate(small + [jnp.zeros((1, padded - total), F32)], axis=1).reshape(8, padded // 8)
    gathered = _ag_small(packed, "ag_small_grads")
    summed = _sum_blocks(gathered, N_DEV, "sum_small_grads").reshape(1, padded)
    offs = [sum(sizes[:i]) for i in range(len(sizes))]
    (g_b_ada, g_gn1, g_gn2, g_gn3, g_final, g_conv_b, g_ln_g, g_ln_b, g_attn_g, g_conv_g, g_taps, loss_row) = [
        summed[:, o:o + n] for o, n in zip(offs, sizes)]
    loss = loss_row[0, 0]
    g_taps_shard = lax.dynamic_slice_in_dim(g_taps.reshape(CONV_KERNEL, aw), me * cw_shard, cw_shard, axis=1)
    dmod_all = gathered.reshape(N_DEV, padded)[:, :n_mod * d]
    dmod_cols = lax.dynamic_slice_in_dim(dmod_all, me * mod_cols, mod_cols, axis=1)
    g_w_ada = _mm_tn(silu_c, dmod_cols, "ada_dw")

    arrived = dict(zip(["ffn2_w_gate", "ffn2_w_up", "ffn2_w_down", "w_out", "w_in", "ffn1_w_down", "ffn1_w_up",
                        "ffn1_w_gate"], list(sums_a) + list(sums_b) + list(sums_c) + list(sums_d) + list(sums_e)))
    transposed = ("ffn1_w_gate", "ffn1_w_up", "w_in", "ffn2_w_gate", "ffn2_w_up")
    grads = {
        "w_ada": g_w_ada, "b_ada": g_b_ada, "ffn1_norm_g": g_gn1, "mix_norm_g": g_gn2, "conv_dw_w": g_taps_shard,
        "conv_dw_b": g_conv_b, "conv_ln_g": g_ln_g, "conv_ln_b": g_ln_b, "attn_out_g": g_attn_g,
        "conv_out_g": g_conv_g, "ffn2_norm_g": g_gn3, "final_norm_g": g_final,
    }
    weights = dict(w_ada=w_ada, b_ada=b_ada, ffn1_norm_g=ffn1_norm_g, ffn1_w_gate=ffn1_w_gate, ffn1_w_up=ffn1_w_up, ffn1_w_down=ffn1_w_down, mix_norm_g=mix_norm_g, w_in=w_in, conv_dw_w=conv_dw_w, conv_dw_b=conv_dw_b, conv_ln_g=conv_ln_g, conv_ln_b=conv_ln_b, attn_out_g=attn_out_g, conv_out_g=conv_out_g, w_out=w_out, ffn2_norm_g=ffn2_norm_g, ffn2_w_gate=ffn2_w_gate, ffn2_w_up=ffn2_w_up, ffn2_w_down=ffn2_w_down, final_norm_g=final_norm_g)
    moms = dict(w_ada=m_w_ada, b_ada=m_b_ada, ffn1_norm_g=m_ffn1_norm_g, ffn1_w_gate=m_ffn1_w_gate, ffn1_w_up=m_ffn1_w_up, ffn1_w_down=m_ffn1_w_down, mix_norm_g=m_mix_norm_g, w_in=m_w_in, conv_dw_w=m_conv_dw_w, conv_dw_b=m_conv_dw_b, conv_ln_g=m_conv_ln_g, conv_ln_b=m_conv_ln_b, attn_out_g=m_attn_out_g, conv_out_g=m_conv_out_g, w_out=m_w_out, ffn2_norm_g=m_ffn2_norm_g, ffn2_w_gate=m_ffn2_w_gate, ffn2_w_up=m_ffn2_w_up, ffn2_w_down=m_ffn2_w_down, final_norm_g=m_final_norm_g)
    vars_ = dict(w_ada=v_w_ada, b_ada=v_b_ada, ffn1_norm_g=v_ffn1_norm_g, ffn1_w_gate=v_ffn1_w_gate, ffn1_w_up=v_ffn1_w_up, ffn1_w_down=v_ffn1_w_down, mix_norm_g=v_mix_norm_g, w_in=v_w_in, conv_dw_w=v_conv_dw_w, conv_dw_b=v_conv_dw_b, conv_ln_g=v_conv_ln_g, conv_ln_b=v_conv_ln_b, attn_out_g=v_attn_out_g, conv_out_g=v_conv_out_g, w_out=v_w_out, ffn2_norm_g=v_ffn2_norm_g, ffn2_w_gate=v_ffn2_w_gate, ffn2_w_up=v_ffn2_w_up, ffn2_w_down=v_ffn2_w_down, final_norm_g=v_final_norm_g)
    names = list(weights)
    big = ["w_ada", "ffn1_w_gate", "ffn1_w_up", "ffn1_w_down", "w_in", "w_out", "ffn2_w_gate", "ffn2_w_up",
           "ffn2_w_down"]
    shape2 = {n: (weights[n].shape[-2] if weights[n].ndim > 1 else 1, weights[n].shape[-1]) for n in names}
    shape2["conv_dw_w"] = (CONV_KERNEL, cw_shard)
    g_out, d_out, m_out, v_out = {}, {}, {}, {}
    for n in big:
        if n in arrived:
            def view(t, n=n):
                return t[0].T if n in transposed else t[0]
            res = _adamw_reduced(view(weights[n]), arrived[n], view(moms[n]), view(vars_[n]), "adamw_" + n)
            g_out[n], d_out[n], m_out[n], v_out[n] = [r.T if n in transposed else r for r in res]
        else:
            g2d = grads[n].reshape(shape2[n])
            res = _adamw_big(weights[n].reshape(shape2[n]), g2d, moms[n].reshape(shape2[n]),
                             vars_[n].reshape(shape2[n]), "adamw_" + n)
            g_out[n], (d_out[n], m_out[n], v_out[n]) = g2d, res
    rest = [n for n in names if n not in big]
    res = _adamw_small([weights[n].reshape(shape2[n]) for n in rest], [grads[n].reshape(shape2[n]) for n in rest],
                       [moms[n].reshape(shape2[n]) for n in rest], [vars_[n].reshape(shape2[n]) for n in rest],
                       "adamw_small")
    for i, n in enumerate(rest):
        g_out[n], d_out[n], m_out[n], v_out[n] = grads[n], res[0][i], res[1][i], res[2][i]

    def shaped(table):
        return [table[n].reshape(weights[n].shape) for n in names]

    return (loss, dx.reshape(x.shape), *shaped(g_out), *shaped(d_out), *shaped(m_out), *shaped(v_out))
```
